```python
import math
import jax, jax.numpy as jnp
from jax import lax
import numpy as np

D_MODEL = 1024
BATCH = 8
SEQ = 2048
DEPTH = 2

CHUNK = 64
N_A = DEPTH // 2
N_B = DEPTH - N_A
GMLP_BLOCK = 128
GATE_DIM = 2 * D_MODEL
A_GROUPS = 8
A_GROUP_DIM = GATE_DIM // A_GROUPS
B_HEADS = 8
QK_NOPE = 128
QK_ROPE = 64
V_HEAD = 128
Q_LORA = 384
KV_LORA = 256
ROPE_THETA = 10000.0
Q_BLOCK = 128
D_FF = 4 * D_MODEL
EPS = 1e-6

kernel_name = "yoco_gmlp_mla_sqrelu_trunk"


def rmsnorm(x, g):
    xf = x.astype(jnp.float32)
    y = xf * lax.rsqrt(jnp.mean(xf * xf, axis=-1, keepdims=True) + EPS)
    return (y * g.astype(jnp.float32)).astype(x.dtype)


def layernorm(x, g, b):
    xf = x.astype(jnp.float32)
    mu = jnp.mean(xf, axis=-1, keepdims=True)
    var = jnp.mean(jnp.square(xf - mu), axis=-1, keepdims=True)
    y = (xf - mu) * lax.rsqrt(var + EPS)
    return (y * g.astype(jnp.float32) + b.astype(jnp.float32)).astype(x.dtype)


def rope_angles(positions):
    inv_freq = ROPE_THETA ** (-jnp.arange(0, QK_ROPE, 2, dtype=jnp.float32) / QK_ROPE)
    return positions.astype(jnp.float32)[..., None] * inv_freq


def apply_rope(x, ang):
    xf = x.astype(jnp.float32)
    x1, x2 = jnp.split(xf, 2, axis=-1)
    c, s = jnp.cos(ang), jnp.sin(ang)
    return jnp.concatenate([x1 * c - x2 * s, x2 * c + x1 * s], axis=-1).astype(x.dtype)


def chunk_mask(q_idx, k_idx):
    return (k_idx[None, :] // CHUNK) <= (q_idx[:, None] // CHUNK)


def gmlp_mixer(hn, w_in, ln_g, ln_b, w_s, b_s, w_out):
    B, S, _ = hn.shape
    z = jax.nn.gelu(hn @ w_in, approximate=False)
    u, v = jnp.split(z, 2, axis=-1)
    v = layernorm(v, ln_g, ln_b)
    nb = S // GMLP_BLOCK
    vb = v.reshape(B, nb, GMLP_BLOCK, A_GROUPS, A_GROUP_DIM)
    idx = jnp.arange(GMLP_BLOCK)
    ws = jnp.where(chunk_mask(idx, idx)[None], w_s, jnp.zeros_like(w_s))
    sv = jnp.einsum('gij,bnjgc->bnigc', ws, vb) + b_s.T[None, None, :, :, None]
    return (u * sv.reshape(B, S, GATE_DIM)) @ w_out


def shared_latent_kv(h, positions, src_g, w_kv_a, kv_a_g, w_kv_b):
    B, S, _ = h.shape
    hn = rmsnorm(h, src_g)
    ckv = hn @ w_kv_a
    c_kv, k_pe = ckv[..., :KV_LORA], ckv[..., KV_LORA:]
    c_kv = rmsnorm(c_kv, kv_a_g)
    kv = (c_kv @ w_kv_b).reshape(B, S, B_HEADS, QK_NOPE + V_HEAD)
    k_nope, v = kv[..., :QK_NOPE], kv[..., QK_NOPE:]
    k_pe = apply_rope(k_pe, rope_angles(positions))
    return k_nope, k_pe, v


def chunk_causal_mla_attention(q_nope, q_pe, k_nope, k_pe, v):
    B, S, H, _ = q_nope.shape
    nb = S // Q_BLOCK
    qn = q_nope.reshape(B, nb, Q_BLOCK, H, QK_NOPE).transpose(1, 0, 2, 3, 4)
    qp = q_pe.reshape(B, nb, Q_BLOCK, H, QK_ROPE).transpose(1, 0, 2, 3, 4)
    k_idx = jnp.arange(S)
    scale = (QK_NOPE + QK_ROPE) ** -0.5

    def one_block(args):
        qn_b, qp_b, i = args
        s = (jnp.einsum('bqhd,bkhd->bhqk', qn_b, k_nope).astype(jnp.float32)
             + jnp.einsum('bqhd,bkd->bhqk', qp_b, k_pe).astype(jnp.float32)) * scale
        q_idx = i * Q_BLOCK + jnp.arange(Q_BLOCK)
        s = jnp.where(chunk_mask(q_idx, k_idx)[None, None], s, jnp.finfo(jnp.float32).min)
        p = jax.nn.softmax(s, axis=-1).astype(v.dtype)
        return jnp.einsum('bhqk,bkhd->bqhd', p, v)

    out = lax.map(one_block, (qn, qp, jnp.arange(nb)))
    return out.transpose(1, 0, 2, 3, 4).reshape(B, S, H, V_HEAD)


def mla_mixer(hn, kv, positions, w_q_a, q_g, w_q_b, w_o):
    B, S, _ = hn.shape
    k_nope, k_pe, v = kv
    cq = rmsnorm(hn @ w_q_a, q_g)
    q = (cq @ w_q_b).reshape(B, S, B_HEADS, QK_NOPE + QK_ROPE)
    q_nope = q[..., :QK_NOPE]
    q_pe = apply_rope(q[..., QK_NOPE:], rope_angles(positions)[:, :, None, :])
    o = chunk_causal_mla_attention(q_nope, q_pe, k_nope, k_pe, v)
    return o.reshape(B, S, B_HEADS * V_HEAD) @ w_o


def sq_relu_mlp(hn, w1, w2):
    return jnp.square(jax.nn.relu(hn @ w1)) @ w2


def _fwd_setup_inputs(seed: int = 0) -> dict:
    key = jax.random.key(seed)
    ks = jax.random.split(key, 24)

    def nrm(k, shape, fan_in, mult=1.0):
        return jax.random.normal(k, shape, jnp.float32) * (mult * fan_in ** -0.5)

    def gain(k, shape):
        return 1.0 + 0.05 * jax.random.normal(k, shape, jnp.float32)

    x = jax.random.normal(ks[0], (BATCH, SEQ, D_MODEL), jnp.float32)
    offset = jax.random.randint(ks[1], (BATCH, 1), 0, 4096, dtype=jnp.int32)
    positions = offset + jnp.arange(SEQ, dtype=jnp.int32)[None, :]
    return {
        "x": x,
        "positions": positions,
        "norm_mix_g": gain(ks[2], (DEPTH, D_MODEL)),
        "norm_mlp_g": gain(ks[3], (DEPTH, D_MODEL)),
        "a_w_in": nrm(ks[4], (N_A, D_MODEL, 2 * GATE_DIM), D_MODEL),
        "a_ln_v_g": gain(ks[5], (N_A, GATE_DIM)),
        "a_ln_v_b": 0.02 * jax.random.normal(ks[6], (N_A, GATE_DIM), jnp.float32),
        "a_w_s": nrm(ks[7], (N_A, A_GROUPS, GMLP_BLOCK, GMLP_BLOCK), GMLP_BLOCK, 0.5),
        "a_b_s": 1.0 + 0.1 * jax.random.normal(ks[8], (N_A, A_GROUPS, GMLP_BLOCK), jnp.float32),
        "a_w_out": nrm(ks[9], (N_A, GATE_DIM, D_MODEL), GATE_DIM),
        "b_w_q_a": nrm(ks[10], (N_B, D_MODEL, Q_LORA), D_MODEL),
        "b_q_norm_g": gain(ks[11], (N_B, Q_LORA)),
        "b_w_q_b": nrm(ks[12], (N_B, Q_LORA, B_HEADS * (QK_NOPE + QK_ROPE)), Q_LORA),
        "b_w_o": nrm(ks[13], (N_B, B_HEADS * V_HEAD, D_MODEL), B_HEADS * V_HEAD),
        "kv_src_norm_g": gain(ks[14], (D_MODEL,)),
        "kv_w_a": nrm(ks[15], (D_MODEL, KV_LORA + QK_ROPE), D_MODEL),
        "kv_a_norm_g": gain(ks[16], (KV_LORA,)),
        "kv_w_b": nrm(ks[17], (KV_LORA, B_HEADS * (QK_NOPE + V_HEAD)), KV_LORA),
        "mlp_w1": nrm(ks[18], (DEPTH, D_MODEL, D_FF), D_MODEL),
        "mlp_w2": nrm(ks[19], (DEPTH, D_FF, D_MODEL), D_FF, 0.5),
        "final_norm_g": gain(ks[20], (D_MODEL,)),
    }


def _fwd_reference(x, positions, norm_mix_g, norm_mlp_g, a_w_in, a_ln_v_g, a_ln_v_b, a_w_s,
              a_b_s, a_w_out, b_w_q_a, b_q_norm_g, b_w_q_b, b_w_o, kv_src_norm_g, kv_w_a,
              kv_a_norm_g, kv_w_b, mlp_w1, mlp_w2, final_norm_g):
    h = x
    kv = None
    for l in range(DEPTH):
        hn = rmsnorm(h, norm_mix_g[l])
        if l < N_A:
            h = h + gmlp_mixer(hn, a_w_in[l], a_ln_v_g[l], a_ln_v_b[l], a_w_s[l],
                               a_b_s[l], a_w_out[l])
        else:
            j = l - N_A
            h = h + mla_mixer(hn, kv, positions, b_w_q_a[j], b_q_norm_g[j], b_w_q_b[j], b_w_o[j])
        h = h + sq_relu_mlp(rmsnorm(h, norm_mlp_g[l]), mlp_w1[l], mlp_w2[l])
        if l == N_A - 1:
            kv = shared_latent_kv(h, positions, kv_src_norm_g, kv_w_a, kv_a_norm_g, kv_w_b)
    return rmsnorm(h, final_norm_g)


import jax as _jax
import jax.numpy as _jnp

TWIN_FORMAT = 'train_step'
FWD_PARAMS = ['x', 'positions', 'norm_mix_g', 'norm_mlp_g', 'a_w_in', 'a_ln_v_g', 'a_ln_v_b', 'a_w_s', 'a_b_s', 'a_w_out', 'b_w_q_a', 'b_q_norm_g', 'b_w_q_b', 'b_w_o', 'kv_src_norm_g', 'kv_w_a', 'kv_a_norm_g', 'kv_w_b', 'mlp_w1', 'mlp_w2', 'final_norm_g']
TWIN_WEIGHTS = ['norm_mix_g', 'norm_mlp_g', 'a_w_in', 'a_ln_v_g', 'a_ln_v_b', 'a_w_s', 'a_b_s', 'a_w_out', 'b_w_q_a', 'b_q_norm_g', 'b_w_q_b', 'b_w_o', 'kv_src_norm_g', 'kv_w_a', 'kv_a_norm_g', 'kv_w_b', 'mlp_w1', 'mlp_w2', 'final_norm_g']
TWIN_DIFF_INPUT = 'x'
TWIN_INPUTS = ['x', 'positions', 'norm_mix_g', 'norm_mlp_g', 'a_w_in', 'a_ln_v_g', 'a_ln_v_b', 'a_w_s', 'a_b_s', 'a_w_out', 'b_w_q_a', 'b_q_norm_g', 'b_w_q_b', 'b_w_o', 'kv_src_norm_g', 'kv_w_a', 'kv_a_norm_g', 'kv_w_b', 'mlp_w1', 'mlp_w2', 'final_norm_g', 'loss_target', 'm_norm_mix_g', 'm_norm_mlp_g', 'm_a_w_in', 'm_a_ln_v_g', 'm_a_ln_v_b', 'm_a_w_s', 'm_a_b_s', 'm_a_w_out', 'm_b_w_q_a', 'm_b_q_norm_g', 'm_b_w_q_b', 'm_b_w_o', 'm_kv_src_norm_g', 'm_kv_w_a', 'm_kv_a_norm_g', 'm_kv_w_b', 'm_mlp_w1', 'm_mlp_w2', 'm_final_norm_g', 'v_norm_mix_g', 'v_norm_mlp_g', 'v_a_w_in', 'v_a_ln_v_g', 'v_a_ln_v_b', 'v_a_w_s', 'v_a_b_s', 'v_a_w_out', 'v_b_w_q_a', 'v_b_q_norm_g', 'v_b_w_q_b', 'v_b_w_o', 'v_kv_src_norm_g', 'v_kv_w_a', 'v_kv_a_norm_g', 'v_kv_w_b', 'v_mlp_w1', 'v_mlp_w2', 'v_final_norm_g']
TWIN_OUTPUTS = ['loss', 'grad_x', 'grad_norm_mix_g', 'grad_norm_mlp_g', 'grad_a_w_in', 'grad_a_ln_v_g', 'grad_a_ln_v_b', 'grad_a_w_s', 'grad_a_b_s', 'grad_a_w_out', 'grad_b_w_q_a', 'grad_b_q_norm_g', 'grad_b_w_q_b', 'grad_b_w_o', 'grad_kv_src_norm_g', 'grad_kv_w_a', 'grad_kv_a_norm_g', 'grad_kv_w_b', 'grad_mlp_w1', 'grad_mlp_w2', 'grad_final_norm_g', 'delta_norm_mix_g', 'delta_norm_mlp_g', 'delta_a_w_in', 'delta_a_ln_v_g', 'delta_a_ln_v_b', 'delta_a_w_s', 'delta_a_b_s', 'delta_a_w_out', 'delta_b_w_q_a', 'delta_b_q_norm_g', 'delta_b_w_q_b', 'delta_b_w_o', 'delta_kv_src_norm_g', 'delta_kv_w_a', 'delta_kv_a_norm_g', 'delta_kv_w_b', 'delta_mlp_w1', 'delta_mlp_w2', 'delta_final_norm_g', 'new_m_norm_mix_g', 'new_m_norm_mlp_g', 'new_m_a_w_in', 'new_m_a_ln_v_g', 'new_m_a_ln_v_b', 'new_m_a_w_s', 'new_m_a_b_s', 'new_m_a_w_out', 'new_m_b_w_q_a', 'new_m_b_q_norm_g', 'new_m_b_w_q_b', 'new_m_b_w_o', 'new_m_kv_src_norm_g', 'new_m_kv_w_a', 'new_m_kv_a_norm_g', 'new_m_kv_w_b', 'new_m_mlp_w1', 'new_m_mlp_w2', 'new_m_final_norm_g', 'new_v_norm_mix_g', 'new_v_norm_mlp_g', 'new_v_a_w_in', 'new_v_a_ln_v_g', 'new_v_a_ln_v_b', 'new_v_a_w_s', 'new_v_a_b_s', 'new_v_a_w_out', 'new_v_b_w_q_a', 'new_v_b_q_norm_g', 'new_v_b_w_q_b', 'new_v_b_w_o', 'new_v_kv_src_norm_g', 'new_v_kv_w_a', 'new_v_kv_a_norm_g', 'new_v_kv_w_b', 'new_v_mlp_w1', 'new_v_mlp_w2', 'new_v_final_norm_g']
TWIN_LEAF_KINDS = {'loss': 'loss', 'grad_x': 'grad_x', 'grad_norm_mix_g': 'grad_w', 'grad_norm_mlp_g': 'grad_w', 'grad_a_w_in': 'grad_w', 'grad_a_ln_v_g': 'grad_w', 'grad_a_ln_v_b': 'grad_w', 'grad_a_w_s': 'grad_w', 'grad_a_b_s': 'grad_w', 'grad_a_w_out': 'grad_w', 'grad_b_w_q_a': 'grad_w', 'grad_b_q_norm_g': 'grad_w', 'grad_b_w_q_b': 'grad_w', 'grad_b_w_o': 'grad_w', 'grad_kv_src_norm_g': 'grad_w', 'grad_kv_w_a': 'grad_w', 'grad_kv_a_norm_g': 'grad_w', 'grad_kv_w_b': 'grad_w', 'grad_mlp_w1': 'grad_w', 'grad_mlp_w2': 'grad_w', 'grad_final_norm_g': 'grad_w', 'delta_norm_mix_g': 'delta_w', 'delta_norm_mlp_g': 'delta_w', 'delta_a_w_in': 'delta_w', 'delta_a_ln_v_g': 'delta_w', 'delta_a_ln_v_b': 'delta_w', 'delta_a_w_s': 'delta_w', 'delta_a_b_s': 'delta_w', 'delta_a_w_out': 'delta_w', 'delta_b_w_q_a': 'delta_w', 'delta_b_q_norm_g': 'delta_w', 'delta_b_w_q_b': 'delta_w', 'delta_b_w_o': 'delta_w', 'delta_kv_src_norm_g': 'delta_w', 'delta_kv_w_a': 'delta_w', 'delta_kv_a_norm_g': 'delta_w', 'delta_kv_w_b': 'delta_w', 'delta_mlp_w1': 'delta_w', 'delta_mlp_w2': 'delta_w', 'delta_final_norm_g': 'delta_w', 'new_m_norm_mix_g': 'new_m', 'new_m_norm_mlp_g': 'new_m', 'new_m_a_w_in': 'new_m', 'new_m_a_ln_v_g': 'new_m', 'new_m_a_ln_v_b': 'new_m', 'new_m_a_w_s': 'new_m', 'new_m_a_b_s': 'new_m', 'new_m_a_w_out': 'new_m', 'new_m_b_w_q_a': 'new_m', 'new_m_b_q_norm_g': 'new_m', 'new_m_b_w_q_b': 'new_m', 'new_m_b_w_o': 'new_m', 'new_m_kv_src_norm_g': 'new_m', 'new_m_kv_w_a': 'new_m', 'new_m_kv_a_norm_g': 'new_m', 'new_m_kv_w_b': 'new_m', 'new_m_mlp_w1': 'new_m', 'new_m_mlp_w2': 'new_m', 'new_m_final_norm_g': 'new_m', 'new_v_norm_mix_g': 'new_v', 'new_v_norm_mlp_g': 'new_v', 'new_v_a_w_in': 'new_v', 'new_v_a_ln_v_g': 'new_v', 'new_v_a_ln_v_b': 'new_v', 'new_v_a_w_s': 'new_v', 'new_v_a_b_s': 'new_v', 'new_v_a_w_out': 'new_v', 'new_v_b_w_q_a': 'new_v', 'new_v_b_q_norm_g': 'new_v', 'new_v_b_w_q_b': 'new_v', 'new_v_b_w_o': 'new_v', 'new_v_kv_src_norm_g': 'new_v', 'new_v_kv_w_a': 'new_v', 'new_v_kv_a_norm_g': 'new_v', 'new_v_kv_w_b': 'new_v', 'new_v_mlp_w1': 'new_v', 'new_v_mlp_w2': 'new_v', 'new_v_final_norm_g': 'new_v'}


def _forward(args):
    return _fwd_reference(*[args[k] for k in FWD_PARAMS])


def _output_shape():
    out = _jax.eval_shape(lambda: _forward(_fwd_setup_inputs(0)))
    return out.shape, out.dtype

N_MICROBATCH = 1
ADAM_LR = 0.001
ADAM_B1 = 0.9
ADAM_B2 = 0.999
ADAM_EPS = 1e-08
ADAM_WD = 0.01
ADAM_STEP = 10
PER_EXAMPLE_BATCH_AXIS = {'x': 0, 'positions': 0, 'loss_target': 0}
SHARED_INPUTS = []
_WEIGHT_DTYPES = {'norm_mix_g': _jnp.float32, 'norm_mlp_g': _jnp.float32, 'a_w_in': _jnp.float32, 'a_ln_v_g': _jnp.float32, 'a_ln_v_b': _jnp.float32, 'a_w_s': _jnp.float32, 'a_b_s': _jnp.float32, 'a_w_out': _jnp.float32, 'b_w_q_a': _jnp.float32, 'b_q_norm_g': _jnp.float32, 'b_w_q_b': _jnp.float32, 'b_w_o': _jnp.float32, 'kv_src_norm_g': _jnp.float32, 'kv_w_a': _jnp.float32, 'kv_a_norm_g': _jnp.float32, 'kv_w_b': _jnp.float32, 'mlp_w1': _jnp.float32, 'mlp_w2': _jnp.float32, 'final_norm_g': _jnp.float32}
MOMENT_SCALE = {'norm_mix_g': 6.279338e-02, 'norm_mlp_g': 7.265340e-02, 'a_w_in': 4.389359e-02, 'a_ln_v_g': 2.169404e-02, 'a_ln_v_b': 2.312987e-02, 'a_w_s': 6.082342e-02, 'a_b_s': 6.937973e-02, 'a_w_out': 1.293129e-01, 'b_w_q_a': 2.101836e-02, 'b_q_norm_g': 2.122805e-02, 'b_w_q_b': 1.045582e-02, 'b_w_o': 8.392735e-02, 'kv_src_norm_g': 8.398316e-02, 'kv_w_a': 1.660591e-01, 'kv_a_norm_g': 1.779772e-01, 'kv_w_b': 6.116759e-02, 'mlp_w1': 3.556185e-02, 'mlp_w2': 2.036463e-01, 'final_norm_g': 1.629274e+01}


def _to_microbatches(a, axis):
    t = _jnp.moveaxis(a, axis, 0)
    t = t.reshape((N_MICROBATCH, t.shape[0] // N_MICROBATCH) + t.shape[1:])
    return _jnp.moveaxis(t, 1, axis + 1)


def setup_inputs(seed: int = 0) -> dict:
    inp = _fwd_setup_inputs(seed)
    key = _jax.random.fold_in(_jax.random.key(seed), 7919)
    shape, _ = _output_shape()
    out = dict(inp)
    out["loss_target"] = _jax.random.normal(_jax.random.fold_in(key, 0), shape, _jnp.float32)
    for i, name in enumerate(TWIN_WEIGHTS):
        w = inp[name].astype(_jnp.float32)
        if MOMENT_SCALE is None:
            s = _jnp.sqrt(_jnp.mean(_jnp.square(w)) + 1e-30)
        else:
            s = MOMENT_SCALE[name]
        km, kv = _jax.random.split(_jax.random.fold_in(key, i + 1))
        out[name] = w
        out["m_" + name] = s * _jax.random.normal(km, w.shape, _jnp.float32)
        out["v_" + name] = (s * s) * _jax.random.uniform(kv, w.shape, _jnp.float32, 0.5, 1.5)
    if N_MICROBATCH > 1:
        for name, axis in PER_EXAMPLE_BATCH_AXIS.items():
            out[name] = _to_microbatches(out[name], axis)
    return {'x': out['x'], 'positions': out['positions'], 'norm_mix_g': out['norm_mix_g'], 'norm_mlp_g': out['norm_mlp_g'], 'a_w_in': out['a_w_in'], 'a_ln_v_g': out['a_ln_v_g'], 'a_ln_v_b': out['a_ln_v_b'], 'a_w_s': out['a_w_s'], 'a_b_s': out['a_b_s'], 'a_w_out': out['a_w_out'], 'b_w_q_a': out['b_w_q_a'], 'b_q_norm_g': out['b_q_norm_g'], 'b_w_q_b': out['b_w_q_b'], 'b_w_o': out['b_w_o'], 'kv_src_norm_g': out['kv_src_norm_g'], 'kv_w_a': out['kv_w_a'], 'kv_a_norm_g': out['kv_a_norm_g'], 'kv_w_b': out['kv_w_b'], 'mlp_w1': out['mlp_w1'], 'mlp_w2': out['mlp_w2'], 'final_norm_g': out['final_norm_g'], 'loss_target': out['loss_target'], 'm_norm_mix_g': out['m_norm_mix_g'], 'm_norm_mlp_g': out['m_norm_mlp_g'], 'm_a_w_in': out['m_a_w_in'], 'm_a_ln_v_g': out['m_a_ln_v_g'], 'm_a_ln_v_b': out['m_a_ln_v_b'], 'm_a_w_s': out['m_a_w_s'], 'm_a_b_s': out['m_a_b_s'], 'm_a_w_out': out['m_a_w_out'], 'm_b_w_q_a': out['m_b_w_q_a'], 'm_b_q_norm_g': out['m_b_q_norm_g'], 'm_b_w_q_b': out['m_b_w_q_b'], 'm_b_w_o': out['m_b_w_o'], 'm_kv_src_norm_g': out['m_kv_src_norm_g'], 'm_kv_w_a': out['m_kv_w_a'], 'm_kv_a_norm_g': out['m_kv_a_norm_g'], 'm_kv_w_b': out['m_kv_w_b'], 'm_mlp_w1': out['m_mlp_w1'], 'm_mlp_w2': out['m_mlp_w2'], 'm_final_norm_g': out['m_final_norm_g'], 'v_norm_mix_g': out['v_norm_mix_g'], 'v_norm_mlp_g': out['v_norm_mlp_g'], 'v_a_w_in': out['v_a_w_in'], 'v_a_ln_v_g': out['v_a_ln_v_g'], 'v_a_ln_v_b': out['v_a_ln_v_b'], 'v_a_w_s': out['v_a_w_s'], 'v_a_b_s': out['v_a_b_s'], 'v_a_w_out': out['v_a_w_out'], 'v_b_w_q_a': out['v_b_w_q_a'], 'v_b_q_norm_g': out['v_b_q_norm_g'], 'v_b_w_q_b': out['v_b_w_q_b'], 'v_b_w_o': out['v_b_w_o'], 'v_kv_src_norm_g': out['v_kv_src_norm_g'], 'v_kv_w_a': out['v_kv_w_a'], 'v_kv_a_norm_g': out['v_kv_a_norm_g'], 'v_kv_w_b': out['v_kv_w_b'], 'v_mlp_w1': out['v_mlp_w1'], 'v_mlp_w2': out['v_mlp_w2'], 'v_final_norm_g': out['v_final_norm_g']}


def _loss(weights, diff, rest, loss_target):
    with _jax.named_scope("forward"):
        args = {**rest, TWIN_DIFF_INPUT: diff, **{k: w.astype(_WEIGHT_DTYPES[k]) for k, w in weights.items()}}
        y = _forward(args)
    with _jax.named_scope("loss_head"):
        err = _jnp.square(y.astype(_jnp.float32) - loss_target)
        return 0.5 * _jnp.sum(_jnp.mean(err, axis=-1)) if err.ndim else 0.5 * err


def _adamw(w, g, m, v):
    m = ADAM_B1 * m + (1.0 - ADAM_B1) * g
    v = ADAM_B2 * v + (1.0 - ADAM_B2) * _jnp.square(g)
    m_hat = m / (1.0 - ADAM_B1 ** ADAM_STEP)
    v_hat = v / (1.0 - ADAM_B2 ** ADAM_STEP)
    delta = -ADAM_LR * (m_hat / (_jnp.sqrt(v_hat) + ADAM_EPS) + ADAM_WD * w)
    return delta, m, v


def reference(x, positions, norm_mix_g, norm_mlp_g, a_w_in, a_ln_v_g, a_ln_v_b, a_w_s, a_b_s, a_w_out, b_w_q_a, b_q_norm_g, b_w_q_b, b_w_o, kv_src_norm_g, kv_w_a, kv_a_norm_g, kv_w_b, mlp_w1, mlp_w2, final_norm_g, loss_target, m_norm_mix_g, m_norm_mlp_g, m_a_w_in, m_a_ln_v_g, m_a_ln_v_b, m_a_w_s, m_a_b_s, m_a_w_out, m_b_w_q_a, m_b_q_norm_g, m_b_w_q_b, m_b_w_o, m_kv_src_norm_g, m_kv_w_a, m_kv_a_norm_g, m_kv_w_b, m_mlp_w1, m_mlp_w2, m_final_norm_g, v_norm_mix_g, v_norm_mlp_g, v_a_w_in, v_a_ln_v_g, v_a_ln_v_b, v_a_w_s, v_a_b_s, v_a_w_out, v_b_w_q_a, v_b_q_norm_g, v_b_w_q_b, v_b_w_o, v_kv_src_norm_g, v_kv_w_a, v_kv_a_norm_g, v_kv_w_b, v_mlp_w1, v_mlp_w2, v_final_norm_g):
    given = dict(x=x, positions=positions, norm_mix_g=norm_mix_g, norm_mlp_g=norm_mlp_g, a_w_in=a_w_in, a_ln_v_g=a_ln_v_g, a_ln_v_b=a_ln_v_b, a_w_s=a_w_s, a_b_s=a_b_s, a_w_out=a_w_out, b_w_q_a=b_w_q_a, b_q_norm_g=b_q_norm_g, b_w_q_b=b_w_q_b, b_w_o=b_w_o, kv_src_norm_g=kv_src_norm_g, kv_w_a=kv_w_a, kv_a_norm_g=kv_a_norm_g, kv_w_b=kv_w_b, mlp_w1=mlp_w1, mlp_w2=mlp_w2, final_norm_g=final_norm_g, loss_target=loss_target, m_norm_mix_g=m_norm_mix_g, m_norm_mlp_g=m_norm_mlp_g, m_a_w_in=m_a_w_in, m_a_ln_v_g=m_a_ln_v_g, m_a_ln_v_b=m_a_ln_v_b, m_a_w_s=m_a_w_s, m_a_b_s=m_a_b_s, m_a_w_out=m_a_w_out, m_b_w_q_a=m_b_w_q_a, m_b_q_norm_g=m_b_q_norm_g, m_b_w_q_b=m_b_w_q_b, m_b_w_o=m_b_w_o, m_kv_src_norm_g=m_kv_src_norm_g, m_kv_w_a=m_kv_w_a, m_kv_a_norm_g=m_kv_a_norm_g, m_kv_w_b=m_kv_w_b, m_mlp_w1=m_mlp_w1, m_mlp_w2=m_mlp_w2, m_final_norm_g=m_final_norm_g, v_norm_mix_g=v_norm_mix_g, v_norm_mlp_g=v_norm_mlp_g, v_a_w_in=v_a_w_in, v_a_ln_v_g=v_a_ln_v_g, v_a_ln_v_b=v_a_ln_v_b, v_a_w_s=v_a_w_s, v_a_b_s=v_a_b_s, v_a_w_out=v_a_w_out, v_b_w_q_a=v_b_w_q_a, v_b_q_norm_g=v_b_q_norm_g, v_b_w_q_b=v_b_w_q_b, v_b_w_o=v_b_w_o, v_kv_src_norm_g=v_kv_src_norm_g, v_kv_w_a=v_kv_w_a, v_kv_a_norm_g=v_kv_a_norm_g, v_kv_w_b=v_kv_w_b, v_mlp_w1=v_mlp_w1, v_mlp_w2=v_mlp_w2, v_final_norm_g=v_final_norm_g)
    weights = {n: given[n] for n in TWIN_WEIGHTS}
    shared = {n: given[n] for n in SHARED_INPUTS}
    per_example = {n: given[n] for n in ['x', 'positions']}
    grad_fn = _jax.value_and_grad(_loss, argnums=(0, 1))

    def one_microbatch(ex, loss_target):
        ex = dict(ex)
        diff = ex.pop(TWIN_DIFF_INPUT)
        return grad_fn(weights, diff, {**shared, **ex}, loss_target)

    if N_MICROBATCH == 1:
        loss, (grad_w, grad_x) = one_microbatch(per_example, given["loss_target"])
    else:
        def body(carry, xs):
            loss_sum, grad_sum = carry
            l_k, (gw_k, gx_k) = one_microbatch(xs[0], xs[1])
            with _jax.named_scope("update"):
                return (loss_sum + l_k, _jax.tree.map(_jnp.add, grad_sum, gw_k)), gx_k

        init = (_jnp.zeros((), _jnp.float32), _jax.tree.map(_jnp.zeros_like, weights))
        (loss, grad_w), grad_x = _jax.lax.scan(body, init, (per_example, given["loss_target"]))
    with _jax.named_scope("update"):
        delta_w, new_m, new_v = {}, {}, {}
        for n in TWIN_WEIGHTS:
            delta_w[n], new_m[n], new_v[n] = _adamw(weights[n], grad_w[n], given["m_" + n], given["v_" + n])
    return (loss, grad_x, *[grad_w[n] for n in TWIN_WEIGHTS], *[delta_w[n] for n in TWIN_WEIGHTS],
            *[new_m[n] for n in TWIN_WEIGHTS], *[new_v[n] for n in TWIN_WEIGHTS])
```

```python
import functools
import math

import jax
import jax.numpy as jnp
from jax import lax
from jax.experimental import pallas as pl
from jax.experimental.pallas import tpu as pltpu

F32, BF16 = jnp.float32, jnp.bfloat16
MESH = pl.DeviceIdType.MESH

N_DEV = 8
D_MODEL = 1024
CHUNK = 64
GMLP_BLOCK = 128
GATE_DIM = 2048
A_GROUPS = 8
A_GROUP_DIM = GATE_DIM // A_GROUPS
B_HEADS = 8
QK_NOPE, QK_ROPE, V_HEAD = 128, 64, 128
Q_LORA, KV_LORA = 384, 256
ROPE_THETA = 10000.0
D_FF = 4096
FF_SLOT = D_FF // N_DEV
EPS = 1e-6
ATT_SCALE = (QK_NOPE + QK_ROPE) ** -0.5

ADAM_LR, ADAM_B1, ADAM_B2, ADAM_EPS, ADAM_WD, ADAM_STEP = 0.001, 0.9, 0.999, 1e-08, 0.01, 10

TM = 256
TM_GATE = 128
VMEM_LIMIT = 56 * 1024 * 1024
INV_SQRT2 = 1.0 / math.sqrt(2.0)
INV_SQRT_2PI = 1.0 / math.sqrt(2.0 * math.pi)


def _dot(a, b):
    return jnp.dot(a, b, preferred_element_type=F32)


def _dot_nt(a, b):
    return lax.dot_general(a, b, (((1,), (1,)), ((), ())), preferred_element_type=F32)


def _dot_tn(a, b):
    return lax.dot_general(a, b, (((0,), (0,)), ((), ())), preferred_element_type=F32)


def _rms_fwd(x, g):
    rstd = lax.rsqrt(jnp.mean(x * x, axis=-1, keepdims=True) + EPS)
    xhat = x * rstd
    return xhat * g, xhat, rstd


def _rms_bwd(dy, xhat, rstd, g):
    dxhat = dy * g
    dx = rstd * (dxhat - xhat * jnp.mean(dxhat * xhat, axis=-1, keepdims=True))
    return dx, jnp.sum(dy * xhat, axis=0, keepdims=True)


def _ln_fwd(v, g, b):
    mu = jnp.mean(v, axis=-1, keepdims=True)
    vc = v - mu
    rstd = lax.rsqrt(jnp.mean(vc * vc, axis=-1, keepdims=True) + EPS)
    vhat = vc * rstd
    return vhat * g + b, vhat, rstd


def _gelu(x):
    return 0.5 * x * (1.0 + lax.erf(x * INV_SQRT2))


def _gelu_grad(x):
    return 0.5 * (1.0 + lax.erf(x * INV_SQRT2)) + x * jnp.exp(-0.5 * x * x) * INV_SQRT_2PI


def _rope(x, cos, sin):
    x1, x2 = x[:, :QK_ROPE // 2], x[:, QK_ROPE // 2:]
    return jnp.concatenate([x1 * cos - x2 * sin, x2 * cos + x1 * sin], axis=-1)


def _gate_mask():
    row = lax.broadcasted_iota(jnp.int32, (GMLP_BLOCK, GMLP_BLOCK), 0)
    col = lax.broadcasted_iota(jnp.int32, (GMLP_BLOCK, GMLP_BLOCK), 1)
    return (col < CHUNK) | (row >= CHUNK)


def _att_mask(q0, tq, t):
    q = q0 + lax.broadcasted_iota(jnp.int32, (tq, t), 0)
    k = lax.broadcasted_iota(jnp.int32, (tq, t), 1)
    return jnp.right_shift(k, 6) <= jnp.right_shift(q, 6)


def _cp(*sem):
    return pltpu.CompilerParams(dimension_semantics=sem, vmem_limit_bytes=VMEM_LIMIT)


def _res(shape, imap=None):
    zeros = (0,) * len(shape)
    return pl.BlockSpec(shape, imap or (lambda i: zeros), pipeline_mode=pl.Buffered(1))


def _const(shape):
    zeros = (0,) * len(shape)
    return pl.BlockSpec(shape, lambda i: zeros)


def _row(d, tm=TM):
    return pl.BlockSpec((tm, d), lambda i: (i, 0))


def _acc(ref, val):
    @pl.when(pl.program_id(0) == 0)
    def _():
        ref[...] = jnp.zeros_like(ref)
    ref[...] += val


def _a_mix_fwd(x, g, w_in, ln_g, ln_b, w_s, b_st, w_out):
    t = x.shape[0]
    nblk = TM // GMLP_BLOCK

    def body(x_ref, g_ref, win_ref, lng_ref, lnb_ref, ws_ref, bst_ref, wout_ref, h_ref, z_ref, gated_scr):
        xv = x_ref[...]
        hb = _rms_fwd(xv, g_ref[...])[0].astype(BF16)
        for d in range(N_DEV):
            z_ref[:, d * FF_SLOT:(d + 1) * FF_SLOT] = _dot(hb, win_ref[d])
        u = _gelu(z_ref[:, :GATE_DIM])
        vb = _ln_fwd(_gelu(z_ref[:, GATE_DIM:]), lng_ref[...], lnb_ref[...])[0].astype(BF16)
        mask = _gate_mask()
        for gi in range(A_GROUPS):
            wm = jnp.where(mask, ws_ref[gi], 0.0).astype(BF16)
            bias = bst_ref[:, gi:gi + 1]
            cs = slice(gi * A_GROUP_DIM, (gi + 1) * A_GROUP_DIM)
            for n in range(nblk):
                rs = slice(n * GMLP_BLOCK, (n + 1) * GMLP_BLOCK)
                sv = _dot(wm, vb[rs, cs]) + bias
                gated_scr[rs, cs] = (u[rs, cs] * sv).astype(BF16)
        h_ref[...] = xv + _dot(gated_scr[...], wout_ref[...])

    return pl.pallas_call(
        body, name="a_mix_fwd", grid=(t // TM,),
        in_specs=[_row(D_MODEL), _res((1, D_MODEL)), _res((N_DEV, D_MODEL, FF_SLOT)), _res((1, GATE_DIM)),
                  _res((1, GATE_DIM)), _res((A_GROUPS, GMLP_BLOCK, GMLP_BLOCK)), _res((GMLP_BLOCK, A_GROUPS)),
                  _res((GATE_DIM, D_MODEL))],
        out_specs=[_row(D_MODEL), _row(2 * GATE_DIM)],
        out_shape=[jax.ShapeDtypeStruct((t, D_MODEL), F32), jax.ShapeDtypeStruct((t, 2 * GATE_DIM), F32)],
        scratch_shapes=[pltpu.VMEM((TM, GATE_DIM), BF16)],
        compiler_params=_cp("parallel"),
    )(x, g, w_in, ln_g, ln_b, w_s, b_st, w_out)


def _mlp_specs(layer):
    w1 = _res((N_DEV, None, D_MODEL, FF_SLOT), lambda i: (0, layer, 0, 0))
    w2 = _res((N_DEV, None, FF_SLOT, D_MODEL), lambda i: (0, layer, 0, 0))
    return w1, w2


def _mlp_fwd(h, g, w1, w2, layer):
    t = h.shape[0]

    def body(h_ref, g_ref, w1_ref, w2_ref, o_ref, a_ref):
        hv = h_ref[...]
        hb = _rms_fwd(hv, g_ref[...])[0].astype(BF16)
        o_ref[...] = hv
        for d in range(N_DEV):
            a = _dot(hb, w1_ref[d])
            a_ref[:, d * FF_SLOT:(d + 1) * FF_SLOT] = a
            r = jnp.maximum(a, 0.0)
            o_ref[...] += _dot((r * r).astype(BF16), w2_ref[d])

    w1s, w2s = _mlp_specs(layer)
    return pl.pallas_call(
        body, name=f"mlp_fwd_{layer}", grid=(t // TM,),
        in_specs=[_row(D_MODEL), _res((1, D_MODEL)), w1s, w2s],
        out_specs=[_row(D_MODEL), _row(D_FF)],
        out_shape=[jax.ShapeDtypeStruct((t, D_MODEL), F32), jax.ShapeDtypeStruct((t, D_FF), F32)],
        compiler_params=_cp("parallel"),
    )(h, g, w1, w2)


def _kvq_fwd(h, pos, inv_freq, src_g, w_kv_a, kv_a_g, w_kv_b, mix_g, w_q_a, q_g, w_q_b):
    t = h.shape[0]
    half = QK_ROPE // 2

    def body(h_ref, pos_ref, invf_ref, srcg_ref, wkva_ref, kvag_ref, wkvb_ref, mixg_ref, wqa_ref, qg_ref, wqb_ref,
             ckv_ref, kn_ref, v_ref, kpe_ref, cqpre_ref, q_ref, cos_ref, sin_ref):
        hv = h_ref[...]
        xhat = hv * lax.rsqrt(jnp.mean(hv * hv, axis=-1, keepdims=True) + EPS)
        ang = pos_ref[...].astype(F32) * invf_ref[...]
        cos, sin = jnp.cos(ang), jnp.sin(ang)
        cos_ref[...] = cos
        sin_ref[...] = sin
        ckv = _dot((xhat * srcg_ref[...]).astype(BF16), wkva_ref[...])
        ckv_ref[...] = ckv
        cb = _rms_fwd(ckv[:, :KV_LORA], kvag_ref[...])[0].astype(BF16)
        kpe_ref[...] = _rope(ckv[:, KV_LORA:], cos, sin).astype(BF16)
        for hd in range(B_HEADS):
            kv = _dot(cb, wkvb_ref[hd])
            kn_ref[hd] = kv[:, :QK_NOPE].astype(BF16)
            v_ref[hd] = kv[:, QK_NOPE:].astype(BF16)
        cqpre = _dot((xhat * mixg_ref[...]).astype(BF16), wqa_ref[...])
        cqpre_ref[...] = cqpre
        cqb = _rms_fwd(cqpre, qg_ref[...])[0].astype(BF16)
        for hd in range(B_HEADS):
            q = _dot(cqb, wqb_ref[hd])
            q_ref[hd, :, 0:QK_NOPE] = q[:, :QK_NOPE].astype(BF16)
            q_ref[hd, :, QK_NOPE:] = _rope(q[:, QK_NOPE:], cos, sin).astype(BF16)

    def heads(d):
        return pl.BlockSpec((B_HEADS, TM, d), lambda i: (0, i, 0))

    return pl.pallas_call(
        body, name="kvq_fwd", grid=(t // TM,),
        in_specs=[_row(D_MODEL), _row(1), _res((1, half)), _res((1, D_MODEL)), _res((D_MODEL, KV_LORA + QK_ROPE)),
                  _res((1, KV_LORA)), _res((B_HEADS, KV_LORA, QK_NOPE + V_HEAD)), _res((1, D_MODEL)),
                  _res((D_MODEL, Q_LORA)), _res((1, Q_LORA)), _res((B_HEADS, Q_LORA, QK_NOPE + QK_ROPE))],
        out_specs=[_row(KV_LORA + QK_ROPE), heads(QK_NOPE), heads(V_HEAD), _row(QK_ROPE), _row(Q_LORA),
                   heads(QK_NOPE + QK_ROPE), _row(half), _row(half)],
        out_shape=[jax.ShapeDtypeStruct((t, KV_LORA + QK_ROPE), F32),
                   jax.ShapeDtypeStruct((B_HEADS, t, QK_NOPE), BF16),
                   jax.ShapeDtypeStruct((B_HEADS, t, V_HEAD), BF16),
                   jax.ShapeDtypeStruct((t, QK_ROPE), BF16),
                   jax.ShapeDtypeStruct((t, Q_LORA), F32),
                   jax.ShapeDtypeStruct((B_HEADS, t, QK_NOPE + QK_ROPE), BF16),
                   jax.ShapeDtypeStruct((t, half), F32), jax.ShapeDtypeStruct((t, half), F32)],
        compiler_params=_cp("parallel"),
    )(h, pos, inv_freq, src_g, w_kv_a, kv_a_g, w_kv_b, mix_g, w_q_a, q_g, w_q_b)


def _softmax_rows(qn, qp, kn, kpe, mask):
    s = (_dot_nt(qn, kn) + _dot_nt(qp, kpe)) * ATT_SCALE
    s = jnp.where(mask, s, jnp.finfo(F32).min)
    e = jnp.exp(s - jnp.max(s, axis=-1, keepdims=True))
    return e * (1.0 / jnp.sum(e, axis=-1, keepdims=True))


def _att_specs(t):
    def heads(d):
        return pl.BlockSpec((B_HEADS, TM, d), lambda i: (0, i, 0))
    resident = [_res((B_HEADS, t, QK_NOPE)), _res((t, QK_ROPE)), _res((B_HEADS, t, V_HEAD)),
                _res((B_HEADS, V_HEAD, D_MODEL))]
    return heads, resident


def _attn_fwd(h, q, kn, kpe, v, w_o):
    t = h.shape[0]

    def body(h_ref, q_ref, kn_ref, kpe_ref, v_ref, wo_ref, o_ref, att_ref):
        mask = _att_mask(pl.program_id(0) * TM, TM, t)
        o_ref[...] = h_ref[...]
        for hd in range(B_HEADS):
            p = _softmax_rows(q_ref[hd, :, 0:QK_NOPE], q_ref[hd, :, QK_NOPE:], kn_ref[hd], kpe_ref[...], mask)
            ob = _dot(p.astype(BF16), v_ref[hd]).astype(BF16)
            att_ref[hd] = ob
            o_ref[...] += _dot(ob, wo_ref[hd])

    heads, resident = _att_specs(t)
    return pl.pallas_call(
        body, name="attn_fwd", grid=(t // TM,),
        in_specs=[_row(D_MODEL), heads(QK_NOPE + QK_ROPE)] + resident,
        out_specs=[_row(D_MODEL), heads(V_HEAD)],
        out_shape=[jax.ShapeDtypeStruct((t, D_MODEL), F32), jax.ShapeDtypeStruct((B_HEADS, t, V_HEAD), BF16)],
        compiler_params=_cp("parallel"),
    )(h, q, kn, kpe, v, w_o)


def _loss_head(h, g, target):
    t = h.shape[0]

    def body(h_ref, g_ref, t_ref, loss_ref, dh_ref, dg_ref):
        y, xhat, rstd = _rms_fwd(h_ref[...], g_ref[...])
        err = y - t_ref[...]
        part = 0.5 * jnp.sum(jnp.mean(err * err, axis=-1, keepdims=True), axis=0, keepdims=True)
        dx, dg = _rms_bwd(err * (1.0 / D_MODEL), xhat, rstd, g_ref[...])
        dh_ref[...] = dx
        _acc(dg_ref, dg)
        _acc(loss_ref, part)

    return pl.pallas_call(
        body, name="loss_head", grid=(t // TM,),
        in_specs=[_row(D_MODEL), _res((1, D_MODEL)), _row(D_MODEL)],
        out_specs=[_const((1, 1)), _row(D_MODEL), _const((1, D_MODEL))],
        out_shape=[jax.ShapeDtypeStruct((1, 1), F32), jax.ShapeDtypeStruct((t, D_MODEL), F32),
                   jax.ShapeDtypeStruct((1, D_MODEL), F32)],
        compiler_params=_cp("arbitrary"),
    )(h, g, target)


def _mlp_bwd(h, a, dho, g, w1, w2, layer):
    t = h.shape[0]

    def body(h_ref, a_ref, dho_ref, g_ref, w1_ref, w2_ref, dhi_ref, dg_ref, hn_ref, f_ref, da_ref):
        gv = g_ref[...]
        y, xhat, rstd = _rms_fwd(h_ref[...], gv)
        hn_ref[...] = y.astype(BF16)
        dho_v = dho_ref[...]
        dhob = dho_v.astype(BF16)
        dhn = jnp.zeros((TM, D_MODEL), F32)
        for d in range(N_DEV):
            cs = slice(d * FF_SLOT, (d + 1) * FF_SLOT)
            r = jnp.maximum(a_ref[:, cs], 0.0)
            f_ref[:, cs] = (r * r).astype(BF16)
            da = (_dot_nt(dhob, w2_ref[d]) * (2.0 * r)).astype(BF16)
            da_ref[:, cs] = da
            dhn = dhn + _dot_nt(da, w1_ref[d])
        dx, dg = _rms_bwd(dhn, xhat, rstd, gv)
        dhi_ref[...] = dho_v + dx
        _acc(dg_ref, dg)

    w1s, w2s = _mlp_specs(layer)
    return pl.pallas_call(
        body, name=f"mlp_bwd_{layer}", grid=(t // TM,),
        in_specs=[_row(D_MODEL), _row(D_FF), _row(D_MODEL), _res((1, D_MODEL)), w1s, w2s],
        out_specs=[_row(D_MODEL), _const((1, D_MODEL)), _row(D_MODEL), _row(D_FF), _row(D_FF)],
        out_shape=[jax.ShapeDtypeStruct((t, D_MODEL), F32), jax.ShapeDtypeStruct((1, D_MODEL), F32),
                   jax.ShapeDtypeStruct((t, D_MODEL), BF16), jax.ShapeDtypeStruct((t, D_FF), BF16),
                   jax.ShapeDtypeStruct((t, D_FF), BF16)],
        compiler_params=_cp("arbitrary"),
    )(h, a, dho, g, w1, w2)


def _attn_bwd(dh, q, kn, kpe, v, w_o, cos, sin):
    t = dh.shape[0]
    half = QK_ROPE // 2

    def body(dh_ref, q_ref, kn_ref, kpe_ref, v_ref, wo_ref, cos_ref, sin_ref, dq_ref, dkn_ref, dv_ref, dkpe_ref):
        hd, i = pl.program_id(0), pl.program_id(1)

        @pl.when(i == 0)
        def _():
            dkn_ref[...] = jnp.zeros_like(dkn_ref)
            dv_ref[...] = jnp.zeros_like(dv_ref)

        @pl.when((i == 0) & (hd == 0))
        def _():
            dkpe_ref[...] = jnp.zeros_like(dkpe_ref)

        mask = _att_mask(i * TM, TM, t)
        qn, qp = q_ref[:, 0:QK_NOPE], q_ref[:, QK_NOPE:]
        do = _dot_nt(dh_ref[...].astype(BF16), wo_ref[...]).astype(BF16)
        p = _softmax_rows(qn, qp, kn_ref[...], kpe_ref[...], mask)
        dp = _dot_nt(do, v_ref[...])
        ds = (p * (dp - jnp.sum(p * dp, axis=-1, keepdims=True)) * ATT_SCALE).astype(BF16)
        dq_ref[:, 0:QK_NOPE] = _dot(ds, kn_ref[...]).astype(BF16)
        dq_ref[:, QK_NOPE:] = _rope(_dot(ds, kpe_ref[...]), cos_ref[...], -sin_ref[...]).astype(BF16)
        dkn_ref[...] += _dot_tn(ds, qn)
        dv_ref[...] += _dot_tn(p.astype(BF16), do)
        dkpe_ref[...] += _dot_tn(ds, qp)

    def per_head(rows, d, tiled):
        return pl.BlockSpec((None, rows, d), (lambda hd, i: (hd, i, 0)) if tiled else (lambda hd, i: (hd, 0, 0)))

    def tile(d):
        return pl.BlockSpec((TM, d), lambda hd, i: (i, 0))

    return pl.pallas_call(
        body, name="attn_bwd", grid=(B_HEADS, t // TM),
        in_specs=[tile(D_MODEL), per_head(TM, QK_NOPE + QK_ROPE, True), per_head(t, QK_NOPE, False),
                  pl.BlockSpec((t, QK_ROPE), lambda hd, i: (0, 0)), per_head(t, V_HEAD, False),
                  per_head(V_HEAD, D_MODEL, False), tile(half), tile(half)],
        out_specs=[per_head(TM, QK_NOPE + QK_ROPE, True), per_head(t, QK_NOPE, False), per_head(t, V_HEAD, False),
                   pl.BlockSpec((t, QK_ROPE), lambda hd, i: (0, 0))],
        out_shape=[jax.ShapeDtypeStruct((B_HEADS, t, QK_NOPE + QK_ROPE), BF16),
                   jax.ShapeDtypeStruct((B_HEADS, t, QK_NOPE), F32),
                   jax.ShapeDtypeStruct((B_HEADS, t, V_HEAD), F32),
                   jax.ShapeDtypeStruct((t, QK_ROPE), F32)],
        compiler_params=_cp("arbitrary", "arbitrary"),
    )(dh, q, kn, kpe, v, w_o, cos, sin)


def _kvq_bwd(h, dh, ckv, cqpre, dq, dkn, dv, dkpe, cos, sin, src_g, w_kv_a, kv_a_g, w_kv_b, mix_g, w_q_a, q_g, w_q_b):
    t = h.shape[0]
    half = QK_ROPE // 2

    def body(h_ref, dh_ref, ckv_ref, cqpre_ref, dq_ref, dkn_ref, dv_ref, dkpe_ref, cos_ref, sin_ref,
             srcg_ref, wkva_ref, kvag_ref, wkvb_ref, mixg_ref, wqa_ref, qg_ref, wqb_ref,
             dhi_ref, hq_ref, hk_ref, cq_ref, dcqpre_ref, c_ref, dkv_ref, dckv_ref,
             dmixg_ref, dsrcg_ref, dqg_ref, dkvag_ref):
        hv = h_ref[...]
        rstd = lax.rsqrt(jnp.mean(hv * hv, axis=-1, keepdims=True) + EPS)
        xhat = hv * rstd
        mixg, srcg, qg, kvag = mixg_ref[...], srcg_ref[...], qg_ref[...], kvag_ref[...]
        hq_ref[...] = (xhat * mixg).astype(BF16)
        hk_ref[...] = (xhat * srcg).astype(BF16)
        cq, cqhat, crstd = _rms_fwd(cqpre_ref[...], qg)
        cq_ref[...] = cq.astype(BF16)
        dcq = jnp.zeros((TM, Q_LORA), F32)
        for hd in range(B_HEADS):
            dcq = dcq + _dot_nt(dq_ref[hd], wqb_ref[hd])
        dcqpre, dqg = _rms_bwd(dcq, cqhat, crstd, qg)
        dcqpre_b = dcqpre.astype(BF16)
        dcqpre_ref[...] = dcqpre_b
        dxq, dmixg = _rms_bwd(_dot_nt(dcqpre_b, wqa_ref[...]), xhat, rstd, mixg)
        ckv = ckv_ref[...]
        c, chat, krstd = _rms_fwd(ckv[:, :KV_LORA], kvag)
        c_ref[...] = c.astype(BF16)
        dc = jnp.zeros((TM, KV_LORA), F32)
        for hd in range(B_HEADS):
            dkv = jnp.concatenate([dkn_ref[hd], dv_ref[hd]], axis=-1).astype(BF16)
            dkv_ref[hd] = dkv
            dc = dc + _dot_nt(dkv, wkvb_ref[hd])
        dlat, dkvag = _rms_bwd(dc, chat, krstd, kvag)
        dpe = _rope(dkpe_ref[...], cos_ref[...], -sin_ref[...])
        dckv_b = jnp.concatenate([dlat, dpe], axis=-1).astype(BF16)
        dckv_ref[...] = dckv_b
        dxk, dsrcg = _rms_bwd(_dot_nt(dckv_b, wkva_ref[...]), xhat, rstd, srcg)
        dhi_ref[...] = dh_ref[...] + dxq + dxk
        _acc(dmixg_ref, dmixg)
        _acc(dsrcg_ref, dsrcg)
        _acc(dqg_ref, dqg)
        _acc(dkvag_ref, dkvag)

    def heads(d):
        return pl.BlockSpec((B_HEADS, TM, d), lambda i: (0, i, 0))

    def sds(shape, dt):
        return jax.ShapeDtypeStruct(shape, dt)

    return pl.pallas_call(
        body, name="kvq_bwd", grid=(t // TM,),
        in_specs=[_row(D_MODEL), _row(D_MODEL), _row(KV_LORA + QK_ROPE), _row(Q_LORA), heads(QK_NOPE + QK_ROPE),
                  heads(QK_NOPE), heads(V_HEAD), _row(QK_ROPE), _row(half), _row(half),
                  _res((1, D_MODEL)), _res((D_MODEL, KV_LORA + QK_ROPE)), _res((1, KV_LORA)),
                  _res((B_HEADS, KV_LORA, QK_NOPE + V_HEAD)), _res((1, D_MODEL)), _res((D_MODEL, Q_LORA)),
                  _res((1, Q_LORA)), _res((B_HEADS, Q_LORA, QK_NOPE + QK_ROPE))],
        out_specs=[_row(D_MODEL), _row(D_MODEL), _row(D_MODEL), _row(Q_LORA), _row(Q_LORA), _row(KV_LORA),
                   heads(QK_NOPE + V_HEAD), _row(KV_LORA + QK_ROPE),
                   _const((1, D_MODEL)), _const((1, D_MODEL)), _const((1, Q_LORA)), _const((1, KV_LORA))],
        out_shape=[sds((t, D_MODEL), F32), sds((t, D_MODEL), BF16), sds((t, D_MODEL), BF16), sds((t, Q_LORA), BF16),
                   sds((t, Q_LORA), BF16), sds((t, KV_LORA), BF16), sds((B_HEADS, t, QK_NOPE + V_HEAD), BF16),
                   sds((t, KV_LORA + QK_ROPE), BF16),
                   sds((1, D_MODEL), F32), sds((1, D_MODEL), F32), sds((1, Q_LORA), F32), sds((1, KV_LORA), F32)],
        compiler_params=_cp("arbitrary"),
    )(h, dh, ckv, cqpre, dq, dkn, dv, dkpe, cos, sin, src_g, w_kv_a, kv_a_g, w_kv_b, mix_g, w_q_a, q_g, w_q_b)


def _a_mix_bwd(x, z, dh, g, w_in, ln_g, ln_b, w_s, b_st, w_out):
    t = x.shape[0]
    tm = TM_GATE
    nblk = tm // GMLP_BLOCK

    def body(x_ref, z_ref, dh_ref, g_ref, win_ref, lng_ref, lnb_ref, ws_ref, bst_ref, wout_ref,
             dx_ref, hn_ref, gated_ref, dz_ref, dg_ref, dlng_ref, dlnb_ref, dws_ref, dbs_ref, du_scr, dvn_scr):
        @pl.when(pl.program_id(0) == 0)
        def _():
            dws_ref[...] = jnp.zeros_like(dws_ref)
            dbs_ref[...] = jnp.zeros_like(dbs_ref)

        gv, lng = g_ref[...], lng_ref[...]
        y, xhat, rstd = _rms_fwd(x_ref[...], gv)
        hn_ref[...] = y.astype(BF16)
        dhv = dh_ref[...]
        dgated = _dot_nt(dhv.astype(BF16), wout_ref[...])
        u = _gelu(z_ref[:, :GATE_DIM])
        vn, vhat, lrstd = _ln_fwd(_gelu(z_ref[:, GATE_DIM:]), lng, lnb_ref[...])
        vb = vn.astype(BF16)
        mask = _gate_mask()
        for gi in range(A_GROUPS):
            wm = jnp.where(mask, ws_ref[gi], 0.0).astype(BF16)
            bias = bst_ref[:, gi:gi + 1]
            cs = slice(gi * A_GROUP_DIM, (gi + 1) * A_GROUP_DIM)
            dws = jnp.zeros((GMLP_BLOCK, GMLP_BLOCK), F32)
            dbs = jnp.zeros((GMLP_BLOCK, 1), F32)
            for n in range(nblk):
                rs = slice(n * GMLP_BLOCK, (n + 1) * GMLP_BLOCK)
                sv = _dot(wm, vb[rs, cs]) + bias
                gated_ref[rs, cs] = (u[rs, cs] * sv).astype(BF16)
                du_scr[rs, cs] = dgated[rs, cs] * sv
                dsv = dgated[rs, cs] * u[rs, cs]
                dsvb = dsv.astype(BF16)
                dws = dws + _dot_nt(dsvb, vb[rs, cs])
                dbs = dbs + jnp.sum(dsv, axis=-1, keepdims=True)
                dvn_scr[rs, cs] = _dot_tn(wm, dsvb)
            dws_ref[gi] += jnp.where(mask, dws, 0.0)
            dbs_ref[gi] += dbs
        dvn = dvn_scr[...]
        dvhat = dvn * lng
        dv = lrstd * (dvhat - jnp.mean(dvhat, axis=-1, keepdims=True)
                      - vhat * jnp.mean(dvhat * vhat, axis=-1, keepdims=True))
        dz_ref[:, :GATE_DIM] = (du_scr[...] * _gelu_grad(z_ref[:, :GATE_DIM])).astype(BF16)
        dz_ref[:, GATE_DIM:] = (dv * _gelu_grad(z_ref[:, GATE_DIM:])).astype(BF16)
        dhn = jnp.zeros((tm, D_MODEL), F32)
        for d in range(N_DEV):
            dhn = dhn + _dot_nt(dz_ref[:, d * FF_SLOT:(d + 1) * FF_SLOT], win_ref[d])
        dx, dg = _rms_bwd(dhn, xhat, rstd, gv)
        dx_ref[...] = dhv + dx
        _acc(dg_ref, dg)
        _acc(dlng_ref, jnp.sum(dvn * vhat, axis=0, keepdims=True))
        _acc(dlnb_ref, jnp.sum(dvn, axis=0, keepdims=True))

    def sds(shape, dt):
        return jax.ShapeDtypeStruct(shape, dt)

    return pl.pallas_call(
        body, name="a_mix_bwd", grid=(t // tm,),
        in_specs=[_row(D_MODEL, tm), _row(2 * GATE_DIM, tm), _row(D_MODEL, tm), _res((1, D_MODEL)),
                  _res((N_DEV, D_MODEL, FF_SLOT)), _res((1, GATE_DIM)), _res((1, GATE_DIM)),
                  _res((A_GROUPS, GMLP_BLOCK, GMLP_BLOCK)), _res((GMLP_BLOCK, A_GROUPS)), _res((GATE_DIM, D_MODEL))],
        out_specs=[_row(D_MODEL, tm), _row(D_MODEL, tm), _row(GATE_DIM, tm), _row(2 * GATE_DIM, tm),
                   _const((1, D_MODEL)), _const((1, GATE_DIM)), _const((1, GATE_DIM)),
                   _const((A_GROUPS, GMLP_BLOCK, GMLP_BLOCK)), _const((A_GROUPS, GMLP_BLOCK, 1))],
        out_shape=[sds((t, D_MODEL), F32), sds((t, D_MODEL), BF16), sds((t, GATE_DIM), BF16),
                   sds((t, 2 * GATE_DIM), BF16), sds((1, D_MODEL), F32), sds((1, GATE_DIM), F32),
                   sds((1, GATE_DIM), F32), sds((A_GROUPS, GMLP_BLOCK, GMLP_BLOCK), F32),
                   sds((A_GROUPS, GMLP_BLOCK, 1), F32)],
        scratch_shapes=[pltpu.VMEM((tm, GATE_DIM), F32), pltpu.VMEM((tm, GATE_DIM), F32)],
        compiler_params=_cp("arbitrary"),
    )(x, z, dh, g, w_in, ln_g, ln_b, w_s, b_st, w_out)


def _wgrad(name, a, b, a_spec, b_spec, m, n, layer=None, into=None):
    def body(*refs):
        a_ref, b_ref, o_ref = refs[0], refs[1], refs[-1]
        o_ref[0] = _dot_tn(a_ref[...].astype(BF16), b_ref[...].astype(BF16)).astype(BF16)

    if layer is None:
        shape, o_spec = (N_DEV, m, n), pl.BlockSpec((1, m, n), lambda d: (d, 0, 0))
    else:
        shape, o_spec = (N_DEV, 2, m, n), pl.BlockSpec((1, None, m, n), lambda d: (d, layer, 0, 0))
    args, in_specs, aliases = [a, b], [a_spec, b_spec], {}
    if into is not None:
        args.append(into)
        in_specs.append(pl.BlockSpec(memory_space=pl.ANY))
        aliases = {2: 0}
    return pl.pallas_call(
        body, name=name, grid=(N_DEV,), in_specs=in_specs, out_specs=o_spec,
        out_shape=jax.ShapeDtypeStruct(shape, BF16), input_output_aliases=aliases,
        compiler_params=_cp("parallel"),
    )(*args)


def _full(t, d):
    return pl.BlockSpec((t, d), lambda i: (0, 0), pipeline_mode=pl.Buffered(1))


def _cols(t, d):
    return pl.BlockSpec((t, d), lambda i: (0, i))


def _head(t, d):
    return pl.BlockSpec((None, t, d), lambda i: (i, 0, 0))


def _local_step(x, pos, target, inv_freq, wg, sm):
    t = x.shape[0]
    mix_g = [sm["norm_mix_g"][l:l + 1] for l in range(2)]
    mlp_g = [sm["norm_mlp_g"][l:l + 1] for l in range(2)]
    a_args = (wg["a_w_in"], wg["a_ln_v_g"], wg["a_ln_v_b"], sm["a_w_s"], sm["a_b_st"], wg["a_w_out"])
    kvq_w = (sm["kv_src_norm_g"], wg["kv_w_a"], sm["kv_a_norm_g"], wg["kv_w_b"], mix_g[1], wg["b_w_q_a"],
             sm["b_q_norm_g"], wg["b_w_q_b"])

    h1, z = _a_mix_fwd(x, mix_g[0], *a_args)
    h2, a0 = _mlp_fwd(h1, mlp_g[0], wg["mlp_w1"], wg["mlp_w2"], 0)
    ckv, kn, v, kpe, cqpre, q, cos, sin = _kvq_fwd(h2, pos, inv_freq, *kvq_w)
    h3, att = _attn_fwd(h2, q, kn, kpe, v, wg["b_w_o"])
    h4, a1 = _mlp_fwd(h3, mlp_g[1], wg["mlp_w1"], wg["mlp_w2"], 1)
    loss, dh4, d_final_g = _loss_head(h4, sm["final_norm_g"], target)

    g = {}
    dh3, d_mlp_g1, hn, f, da = _mlp_bwd(h3, a1, dh4, mlp_g[1], wg["mlp_w1"], wg["mlp_w2"], 1)
    w1_1 = _wgrad("wgrad_w1_1", hn, da, _full(t, D_MODEL), _cols(t, FF_SLOT), D_MODEL, FF_SLOT, layer=1)
    w2_1 = _wgrad("wgrad_w2_1", f, dh4, _cols(t, FF_SLOT), _full(t, D_MODEL), FF_SLOT, D_MODEL, layer=1)
    g["b_w_o"] = _wgrad("wgrad_w_o", att, dh3, _head(t, V_HEAD), _full(t, D_MODEL), V_HEAD, D_MODEL)
    dq, dkn, dv, dkpe = _attn_bwd(dh3, q, kn, kpe, v, wg["b_w_o"], cos, sin)
    (dh2, hq, hk, cq, dcqpre, c, dkv, dckv, d_mix_g1, d_src_g, d_q_g, d_kv_a_g) = _kvq_bwd(
        h2, dh3, ckv, cqpre, dq, dkn, dv, dkpe, cos, sin, *kvq_w)
    g["b_w_q_a"] = _wgrad("wgrad_w_q_a", hq, dcqpre, _cols(t, D_MODEL // N_DEV), _full(t, Q_LORA),
                          D_MODEL // N_DEV, Q_LORA)
    g["b_w_q_b"] = _wgrad("wgrad_w_q_b", cq, dq, _full(t, Q_LORA), _head(t, QK_NOPE + QK_ROPE),
                          Q_LORA, QK_NOPE + QK_ROPE)
    g["kv_w_a"] = _wgrad("wgrad_kv_w_a", hk, dckv, _cols(t, D_MODEL // N_DEV), _full(t, KV_LORA + QK_ROPE),
                         D_MODEL // N_DEV, KV_LORA + QK_ROPE)
    g["kv_w_b"] = _wgrad("wgrad_kv_w_b", c, dkv, _full(t, KV_LORA), _head(t, QK_NOPE + V_HEAD),
                         KV_LORA, QK_NOPE + V_HEAD)
    dh1, d_mlp_g0, hn, f, da = _mlp_bwd(h1, a0, dh2, mlp_g[0], wg["mlp_w1"], wg["mlp_w2"], 0)
    g["mlp_w1"] = _wgrad("wgrad_w1_0", hn, da, _full(t, D_MODEL), _cols(t, FF_SLOT), D_MODEL, FF_SLOT,
                         layer=0, into=w1_1)
    g["mlp_w2"] = _wgrad("wgrad_w2_0", f, dh2, _cols(t, FF_SLOT), _full(t, D_MODEL), FF_SLOT, D_MODEL,
                         layer=0, into=w2_1)
    dx, hn, gated, dz, d_mix_g0, d_ln_g, d_ln_b, d_ws, d_bs = _a_mix_bwd(x, z, dh1, mix_g[0], *a_args)
    g["a_w_in"] = _wgrad("wgrad_a_w_in", hn, dz, _full(t, D_MODEL), _cols(t, FF_SLOT), D_MODEL, FF_SLOT)
    g["a_w_out"] = _wgrad("wgrad_a_w_out", gated, dh1, _cols(t, GATE_DIM // N_DEV), _full(t, D_MODEL),
                          GATE_DIM // N_DEV, D_MODEL)
    small = {
        "norm_mix_g": jnp.concatenate([d_mix_g0, d_mix_g1], axis=0),
        "norm_mlp_g": jnp.concatenate([d_mlp_g0, d_mlp_g1], axis=0),
        "a_ln_v_g": d_ln_g.reshape(N_DEV, GATE_DIM // N_DEV),
        "a_ln_v_b": d_ln_b.reshape(N_DEV, GATE_DIM // N_DEV),
        "a_w_s": d_ws,
        "a_b_s": d_bs.reshape(A_GROUPS, GMLP_BLOCK),
        "b_q_norm_g": d_q_g,
        "kv_src_norm_g": d_src_g,
        "kv_a_norm_g": d_kv_a_g,
        "final_norm_g": d_final_g,
    }
    return loss, dx, g, small


def _my_place():
    x, y, c = lax.axis_index("x"), lax.axis_index("y"), lax.axis_index("c")
    return x, y, c, 4 * x + 2 * y + c


def _peer(x, y, c, k):
    px = 1 - x if k & 4 else x
    py = 1 - y if k & 2 else y
    pc = 1 - c if k & 1 else c
    return (px, py, pc), 4 * px + 2 * py + pc


def _all_gather(shards, dtypes):
    n = len(shards)
    chips = (2, 4, 6)

    def body(*refs):
        ins, outs, stage = refs[:n], refs[n:2 * n], refs[2 * n:3 * n]
        send_sems, recv_sems, local_sems = refs[3 * n:]
        x, y, c, me = _my_place()
        sib, sib_i = _peer(x, y, c, 1)

        def copy(a, k, block, to, src=None):
            rows = outs[a].at[pl.ds(block, 1)]
            return pltpu.make_async_remote_copy(
                src_ref=rows if src is None else src, dst_ref=rows, send_sem=send_sems.at[a, k],
                recv_sem=recv_sems.at[a, k], device_id=to, device_id_type=MESH)

        sends = []
        for a in range(n):
            stage[a][0] = ins[a][...].astype(stage[a].dtype)
            mine = pltpu.make_async_copy(stage[a], outs[a].at[pl.ds(me, 1)], local_sems.at[a])
            mine.start()
            sends.append(mine)
        first, passed = [], []
        for a in range(n):
            first.append(copy(a, 0, me, sib, src=stage[a]))
            for j, k in enumerate(chips):
                first.append(copy(a, 1 + j, me, _peer(x, y, c, k)[0], src=stage[a]))
        for cp in first:
            cp.start()
        for j, k in enumerate(chips):
            for a in range(n):
                copy(a, 1 + j, _peer(x, y, c, k)[1], sib).wait_recv()
                cp = copy(a, 4 + j, _peer(x, y, c, k)[1], sib)
                cp.start()
                passed.append(cp)
        for a in range(n):
            copy(a, 0, sib_i, sib).wait_recv()
            for j, k in enumerate(chips):
                copy(a, 4 + j, _peer(x, y, c, k ^ 1)[1], sib).wait_recv()
        for cp in first + passed:
            cp.wait_send()
        for cp in sends:
            cp.wait()

    return pl.pallas_call(
        body, name="all_gather_weights",
        in_specs=[pl.BlockSpec(memory_space=pltpu.VMEM)] * n,
        out_specs=[pl.BlockSpec(memory_space=pl.ANY)] * n,
        out_shape=[jax.ShapeDtypeStruct((N_DEV,) + s.shape, dt) for s, dt in zip(shards, dtypes)],
        scratch_shapes=[pltpu.VMEM((1,) + s.shape, dt) for s, dt in zip(shards, dtypes)]
        + [pltpu.SemaphoreType.DMA((n, 7)), pltpu.SemaphoreType.DMA((n, 7)), pltpu.SemaphoreType.DMA((n,))],
        compiler_params=pltpu.CompilerParams(vmem_limit_bytes=VMEM_LIMIT),
    )(*shards)


def _scatter_grads(grads):
    n = len(grads)

    def body(*refs):
        ins, outs = refs[:n], refs[n:2 * n]
        send_sems, recv_sems, local_sems = refs[2 * n:]
        x, y, c, me = _my_place()
        copies = []
        for a in range(n):
            mine = pltpu.make_async_copy(ins[a].at[pl.ds(me, 1)], outs[a].at[pl.ds(me, 1)], local_sems.at[a])
            mine.start()
            copies.append(mine)
            for k in range(1, N_DEV):
                to, to_i = _peer(x, y, c, k)
                cp = pltpu.make_async_remote_copy(
                    src_ref=ins[a].at[pl.ds(to_i, 1)], dst_ref=outs[a].at[pl.ds(me, 1)],
                    send_sem=send_sems.at[a, k - 1], recv_sem=recv_sems.at[a, k - 1], device_id=to,
                    device_id_type=MESH)
                cp.start()
                copies.append(cp)
        for cp in copies:
            cp.wait()

    return pl.pallas_call(
        body, name="scatter_grads",
        in_specs=[pl.BlockSpec(memory_space=pl.ANY)] * n,
        out_specs=[pl.BlockSpec(memory_space=pl.ANY)] * n,
        out_shape=[jax.ShapeDtypeStruct(g.shape, g.dtype) for g in grads],
        scratch_shapes=[pltpu.SemaphoreType.DMA((n, 7)), pltpu.SemaphoreType.DMA((n, 7)),
                        pltpu.SemaphoreType.DMA((n,))],
    )(*grads)


def _adamw(w, g, m, v):
    m = ADAM_B1 * m + (1.0 - ADAM_B1) * g
    v = ADAM_B2 * v + (1.0 - ADAM_B2) * (g * g)
    m_hat = m / (1.0 - ADAM_B1 ** ADAM_STEP)
    v_hat = v / (1.0 - ADAM_B2 ** ADAM_STEP)
    return -ADAM_LR * (m_hat / (jnp.sqrt(v_hat) + ADAM_EPS) + ADAM_WD * w), m, v


def _adamw_sharded(name, recv, w, m, v):
    layers, r, c = w.shape
    tr = math.gcd(r, 256)

    def body(r_ref, w_ref, m_ref, v_ref, g_ref, d_ref, nm_ref, nv_ref):
        g = r_ref[0].astype(F32)
        for j in range(1, N_DEV):
            g = g + r_ref[j].astype(F32)
        g_ref[...] = g
        d_ref[...], nm_ref[...], nv_ref[...] = _adamw(w_ref[...], g, m_ref[...], v_ref[...])

    blk = pl.BlockSpec((None, tr, c), lambda l, i: (l, i, 0))
    return pl.pallas_call(
        body, name=name, grid=(layers, r // tr),
        in_specs=[pl.BlockSpec((N_DEV, None, tr, c), lambda l, i: (0, l, i, 0)), blk, blk, blk],
        out_specs=[blk] * 4, out_shape=[jax.ShapeDtypeStruct(w.shape, F32)] * 4,
        compiler_params=_cp("parallel", "parallel"),
    )(recv, w, m, v)


def _small_allreduce_adamw(parts, ws, ms, vs, own_row):
    n = len(parts)

    def body(*refs):
        p_refs, w_refs, m_refs, v_refs = (refs[i * n:(i + 1) * n] for i in range(4))
        outs = refs[4 * n:8 * n]
        recv = refs[8 * n:9 * n]
        send_sems, recv_sems = refs[9 * n:]
        x, y, c, me = _my_place()
        copies = []
        for a in range(n):
            recv[a][me] = p_refs[a][...]
            for k in range(1, N_DEV):
                cp = pltpu.make_async_remote_copy(
                    src_ref=p_refs[a], dst_ref=recv[a].at[me], send_sem=send_sems.at[a, k - 1],
                    recv_sem=recv_sems.at[a, k - 1], device_id=_peer(x, y, c, k)[0], device_id_type=MESH)
                cp.start()
                copies.append(cp)
        for cp in copies:
            cp.wait()
        for a in range(n):
            g = recv[a][0]
            for j in range(1, N_DEV):
                g = g + recv[a][j]
            if own_row[a]:
                recv[a][0] = g
                g = recv[a][0, pl.ds(me, 1), :]
            g_ref, d_ref, nm_ref, nv_ref = outs[4 * a:4 * a + 4]
            g_ref[...] = g
            d_ref[...], nm_ref[...], nv_ref[...] = _adamw(w_refs[a][...], g, m_refs[a][...], v_refs[a][...])

    vmem = pl.BlockSpec(memory_space=pltpu.VMEM)
    out_shape = []
    for w in ws:
        out_shape += [jax.ShapeDtypeStruct(w.shape, F32)] * 4
    return pl.pallas_call(
        body, name="small_allreduce_adamw",
        in_specs=[vmem] * (4 * n), out_specs=[vmem] * (4 * n), out_shape=out_shape,
        scratch_shapes=[pltpu.VMEM((N_DEV,) + p.shape, F32) for p in parts]
        + [pltpu.SemaphoreType.DMA((n, 7)), pltpu.SemaphoreType.DMA((n, 7))],
    )(*parts, *ws, *ms, *vs)


BIG = ("a_w_in", "a_w_out", "b_w_q_a", "b_w_q_b", "b_w_o", "kv_w_a", "kv_w_b", "mlp_w1", "mlp_w2")
SMALL = ("norm_mix_g", "norm_mlp_g", "a_ln_v_g", "a_ln_v_b", "a_w_s", "a_b_s", "b_q_norm_g", "kv_src_norm_g",
         "kv_a_norm_g", "final_norm_g")
WEIGHTS = ("norm_mix_g", "norm_mlp_g", "a_w_in", "a_ln_v_g", "a_ln_v_b", "a_w_s", "a_b_s", "a_w_out", "b_w_q_a",
           "b_q_norm_g", "b_w_q_b", "b_w_o", "kv_src_norm_g", "kv_w_a", "kv_a_norm_g", "kv_w_b", "mlp_w1", "mlp_w2",
           "final_norm_g")


def _two_d(name, a):
    if name in ("a_w_s", "a_b_s"):
        return a.reshape(a.shape[1:])
    return a.reshape(1, -1) if a.ndim == 1 else a


def _three_d(name, a):
    return a if a.ndim == 3 else a.reshape((1,) + a.shape)


def kernel(x, positions, norm_mix_g, norm_mlp_g, a_w_in, a_ln_v_g, a_ln_v_b, a_w_s, a_b_s, a_w_out, b_w_q_a, b_q_norm_g, b_w_q_b, b_w_o, kv_src_norm_g, kv_w_a, kv_a_norm_g, kv_w_b, mlp_w1, mlp_w2, final_norm_g, loss_target, m_norm_mix_g, m_norm_mlp_g, m_a_w_in, m_a_ln_v_g, m_a_ln_v_b, m_a_w_s, m_a_b_s, m_a_w_out, m_b_w_q_a, m_b_q_norm_g, m_b_w_q_b, m_b_w_o, m_kv_src_norm_g, m_kv_w_a, m_kv_a_norm_g, m_kv_w_b, m_mlp_w1, m_mlp_w2, m_final_norm_g, v_norm_mix_g, v_norm_mlp_g, v_a_w_in, v_a_ln_v_g, v_a_ln_v_b, v_a_w_s, v_a_b_s, v_a_w_out, v_b_w_q_a, v_b_q_norm_g, v_b_w_q_b, v_b_w_o, v_kv_src_norm_g, v_kv_w_a, v_kv_a_norm_g, v_kv_w_b, v_mlp_w1, v_mlp_w2, v_final_norm_g):
    w = dict(norm_mix_g=norm_mix_g, norm_mlp_g=norm_mlp_g, a_w_in=a_w_in, a_ln_v_g=a_ln_v_g, a_ln_v_b=a_ln_v_b,
             a_w_s=a_w_s, a_b_s=a_b_s, a_w_out=a_w_out, b_w_q_a=b_w_q_a, b_q_norm_g=b_q_norm_g, b_w_q_b=b_w_q_b,
             b_w_o=b_w_o, kv_src_norm_g=kv_src_norm_g, kv_w_a=kv_w_a, kv_a_norm_g=kv_a_norm_g, kv_w_b=kv_w_b,
             mlp_w1=mlp_w1, mlp_w2=mlp_w2, final_norm_g=final_norm_g)
    m = dict(norm_mix_g=m_norm_mix_g, norm_mlp_g=m_norm_mlp_g, a_w_in=m_a_w_in, a_ln_v_g=m_a_ln_v_g,
             a_ln_v_b=m_a_ln_v_b, a_w_s=m_a_w_s, a_b_s=m_a_b_s, a_w_out=m_a_w_out, b_w_q_a=m_b_w_q_a,
             b_q_norm_g=m_b_q_norm_g, b_w_q_b=m_b_w_q_b, b_w_o=m_b_w_o, kv_src_norm_g=m_kv_src_norm_g,
             kv_w_a=m_kv_w_a, kv_a_norm_g=m_kv_a_norm_g, kv_w_b=m_kv_w_b, mlp_w1=m_mlp_w1, mlp_w2=m_mlp_w2,
             final_norm_g=m_final_norm_g)
    v = dict(norm_mix_g=v_norm_mix_g, norm_mlp_g=v_norm_mlp_g, a_w_in=v_a_w_in, a_ln_v_g=v_a_ln_v_g,
             a_ln_v_b=v_a_ln_v_b, a_w_s=v_a_w_s, a_b_s=v_a_b_s, a_w_out=v_a_w_out, b_w_q_a=v_b_w_q_a,
             b_q_norm_g=v_b_q_norm_g, b_w_q_b=v_b_w_q_b, b_w_o=v_b_w_o, kv_src_norm_g=v_kv_src_norm_g,
             kv_w_a=v_kv_w_a, kv_a_norm_g=v_kv_a_norm_g, kv_w_b=v_kv_w_b, mlp_w1=v_mlp_w1, mlp_w2=v_mlp_w2,
             final_norm_g=v_final_norm_g)
    t = x.shape[1]

    gather_names = BIG + ("a_ln_v_g", "a_ln_v_b")
    shards = [w[k][0] if k in ("a_w_in", "a_w_out", "b_w_q_a", "b_w_q_b", "b_w_o") else w[k] for k in gather_names]
    dtypes = [BF16] * len(BIG) + [F32, F32]
    wg = dict(zip(gather_names, _all_gather(shards, dtypes)))
    wg["a_w_out"] = wg["a_w_out"].reshape(GATE_DIM, D_MODEL)
    wg["b_w_q_a"] = wg["b_w_q_a"].reshape(D_MODEL, Q_LORA)
    wg["kv_w_a"] = wg["kv_w_a"].reshape(D_MODEL, KV_LORA + QK_ROPE)
    wg["a_ln_v_g"] = wg["a_ln_v_g"].reshape(1, GATE_DIM)
    wg["a_ln_v_b"] = wg["a_ln_v_b"].reshape(1, GATE_DIM)

    sm = {k: _two_d(k, w[k]) for k in SMALL if k not in ("a_ln_v_g", "a_ln_v_b")}
    sm["a_b_st"] = sm["a_b_s"].T
    inv_freq = (ROPE_THETA ** (-jnp.arange(0, QK_ROPE, 2, dtype=F32) / QK_ROPE)).reshape(1, QK_ROPE // 2)

    loss, dx, g, small = _local_step(x[0], positions.reshape(t, 1), loss_target[0], inv_freq, wg, sm)
    loss = lax.psum(loss[0, 0], ("x", "y", "c"))

    recv = dict(zip(BIG, _scatter_grads([g[k] for k in BIG])))

    out = {}
    for k in BIG:
        shard3 = _three_d(k, w[k])
        r = recv[k] if recv[k].ndim == 4 else recv[k].reshape((N_DEV, 1) + recv[k].shape[1:])
        res = _adamw_sharded("adamw_" + k, r, shard3, _three_d(k, m[k]), _three_d(k, v[k]))
        out[k] = [o.reshape(w[k].shape) for o in res]
    own_row = [k in ("a_ln_v_g", "a_ln_v_b") for k in SMALL]
    res = _small_allreduce_adamw([small[k] for k in SMALL], [_two_d(k, w[k]) for k in SMALL],
                                 [_two_d(k, m[k]) for k in SMALL], [_two_d(k, v[k]) for k in SMALL], own_row)
    for i, k in enumerate(SMALL):
        out[k] = [o.reshape(w[k].shape) for o in res[4 * i:4 * i + 4]]

    return (loss, dx.reshape(x.shape), *[out[k][0] for k in WEIGHTS], *[out[k][1] for k in WEIGHTS],
            *[out[k][2] for k in WEIGHTS], *[out[k][3] for k in WEIGHTS])
```

```python
import math

import jax
import jax.numpy as jnp
from jax import lax
from jax.experimental import pallas as pl
from jax.experimental.pallas import tpu as pltpu

F32, BF16 = jnp.float32, jnp.bfloat16
MESH = pl.DeviceIdType.MESH
ANY = pl.BlockSpec(memory_space=pl.ANY)
VMEM = pl.BlockSpec(memory_space=pltpu.VMEM)

N_DEV = 8
D_MODEL = 1024
CHUNK = 64
GMLP_BLOCK = 128
GATE_DIM = 2048
A_GROUPS = 8
A_GROUP_DIM = GATE_DIM // A_GROUPS
B_HEADS = 8
QK_NOPE, QK_ROPE, V_HEAD = 128, 64, 128
Q_LORA, KV_LORA = 384, 256
ROPE_THETA = 10000.0
D_FF = 4096
FF_SLOT = D_FF // N_DEV
EPS = 1e-6
ATT_SCALE = (QK_NOPE + QK_ROPE) ** -0.5

ADAM_LR, ADAM_B1, ADAM_B2, ADAM_EPS, ADAM_WD, ADAM_STEP = 0.001, 0.9, 0.999, 1e-08, 0.01, 10

TM = 256
TM_GATE = 128
VMEM_LIMIT = 56 * 1024 * 1024
INV_SQRT2 = 1.0 / math.sqrt(2.0)
INV_SQRT_2PI = 1.0 / math.sqrt(2.0 * math.pi)


def _dot(a, b):
    return jnp.dot(a, b, preferred_element_type=F32)


def _dot_nt(a, b):
    return lax.dot_general(a, b, (((1,), (1,)), ((), ())), preferred_element_type=F32)


def _dot_tn(a, b):
    return lax.dot_general(a, b, (((0,), (0,)), ((), ())), preferred_element_type=F32)


def _rms_fwd(x, g):
    rstd = lax.rsqrt(jnp.mean(x * x, axis=-1, keepdims=True) + EPS)
    xhat = x * rstd
    return xhat * g, xhat, rstd


def _rms_bwd(dy, xhat, rstd, g):
    dxhat = dy * g
    dx = rstd * (dxhat - xhat * jnp.mean(dxhat * xhat, axis=-1, keepdims=True))
    return dx, jnp.sum(dy * xhat, axis=0, keepdims=True)


def _ln_fwd(v, g, b):
    mu = jnp.mean(v, axis=-1, keepdims=True)
    vc = v - mu
    rstd = lax.rsqrt(jnp.mean(vc * vc, axis=-1, keepdims=True) + EPS)
    vhat = vc * rstd
    return vhat * g + b, vhat, rstd


def _gelu(x):
    return 0.5 * x * (1.0 + lax.erf(x * INV_SQRT2))


def _gelu_grad(x):
    return 0.5 * (1.0 + lax.erf(x * INV_SQRT2)) + x * jnp.exp(-0.5 * x * x) * INV_SQRT_2PI


def _rope(x, cos, sin):
    x1, x2 = x[:, :QK_ROPE // 2], x[:, QK_ROPE // 2:]
    return jnp.concatenate([x1 * cos - x2 * sin, x2 * cos + x1 * sin], axis=-1)


def _gate_mask():
    row = lax.broadcasted_iota(jnp.int32, (GMLP_BLOCK, GMLP_BLOCK), 0)
    col = lax.broadcasted_iota(jnp.int32, (GMLP_BLOCK, GMLP_BLOCK), 1)
    return (col < CHUNK) | (row >= CHUNK)


def _att_mask(q0, tq, t):
    q = q0 + lax.broadcasted_iota(jnp.int32, (tq, t), 0)
    k = lax.broadcasted_iota(jnp.int32, (tq, t), 1)
    return jnp.right_shift(k, 6) <= jnp.right_shift(q, 6)


def _res(shape, imap=None):
    zeros = (0,) * len(shape)
    return pl.BlockSpec(shape, imap or (lambda i: zeros), pipeline_mode=pl.Buffered(1))


def _const(shape):
    zeros = (0,) * len(shape)
    return pl.BlockSpec(shape, lambda i: zeros)


def _row(d, tm=TM):
    return pl.BlockSpec((tm, d), lambda i: (i, 0))


def _heads(d):
    return pl.BlockSpec((B_HEADS, TM, d), lambda i: (0, i, 0))


def _sds(shape, dt):
    return jax.ShapeDtypeStruct(shape, dt)


def _acc(ref, val):
    @pl.when(pl.program_id(0) == 0)
    def _():
        ref[...] = jnp.zeros_like(ref)
    ref[...] += val


def _my_place():
    x, y, c = lax.axis_index("x"), lax.axis_index("y"), lax.axis_index("c")
    return x, y, c, 4 * x + 2 * y + c


def _peer(x, y, c, k):
    px = 1 - x if k & 4 else x
    py = 1 - y if k & 2 else y
    pc = 1 - c if k & 1 else c
    return (px, py, pc), 4 * px + 2 * py + pc


CHIPS = (2, 4, 6)


def _gather_copy(outs, send_sems, recv_sems, a, k, block, to, src=None):
    rows = outs[a].at[pl.ds(block, 1)]
    return pltpu.make_async_remote_copy(
        src_ref=rows if src is None else src, dst_ref=rows, send_sem=send_sems.at[a, k], recv_sem=recv_sems.at[a, k],
        device_id=to, device_id_type=MESH)


def _gather_start(srcs, outs, sems):
    send_sems, recv_sems, local_sems = sems
    x, y, c, me = _my_place()
    for a in range(len(srcs)):
        pltpu.make_async_copy(srcs[a], outs[a].at[pl.ds(me, 1)], local_sems.at[a]).start()
        _gather_copy(outs, send_sems, recv_sems, a, 0, me, _peer(x, y, c, 1)[0], src=srcs[a]).start()
        for j, k in enumerate(CHIPS):
            _gather_copy(outs, send_sems, recv_sems, a, 1 + j, me, _peer(x, y, c, k)[0], src=srcs[a]).start()


def _gather_finish(srcs, outs, sems):
    send_sems, recv_sems, local_sems = sems
    x, y, c, me = _my_place()
    sib, sib_i = _peer(x, y, c, 1)
    n = len(srcs)
    for j, k in enumerate(CHIPS):
        for a in range(n):
            block = _peer(x, y, c, k)[1]
            _gather_copy(outs, send_sems, recv_sems, a, 1 + j, block, sib).wait_recv()
            _gather_copy(outs, send_sems, recv_sems, a, 4 + j, block, sib).start()
    for a in range(n):
        _gather_copy(outs, send_sems, recv_sems, a, 0, sib_i, sib).wait_recv()
        for j, k in enumerate(CHIPS):
            _gather_copy(outs, send_sems, recv_sems, a, 4 + j, _peer(x, y, c, k ^ 1)[1], sib).wait_recv()
    for a in range(n):
        for k in range(7):
            _gather_copy(outs, send_sems, recv_sems, a, k, me, sib, src=srcs[a] if k < 4 else None).wait_send()
        pltpu.make_async_copy(srcs[a], outs[a].at[pl.ds(me, 1)], local_sems.at[a]).wait()


def _gather_sems(n):
    return [pltpu.SemaphoreType.DMA((n, 7)), pltpu.SemaphoreType.DMA((n, 7)), pltpu.SemaphoreType.DMA((n,))]


class _Comm:
    def __init__(self, args, out_shape, scratch, start, finish):
        self.args, self.out_shape, self.scratch, self.start, self.finish = args, out_shape, scratch, start, finish


def _gather_comm(shards):
    return _Comm(list(shards), [_sds((N_DEV,) + s.shape[1:], s.dtype) for s in shards], _gather_sems(len(shards)),
                 _gather_start, _gather_finish)


def _direct_copies(ins, outs, sems, wait, from_block):
    send_sems, recv_sems, local_sems = sems
    x, y, c, me = _my_place()
    for a in range(len(ins)):
        src = ins[a].at[pl.ds(me, 1)] if from_block[a] else ins[a]
        local = pltpu.make_async_copy(src, outs[a].at[pl.ds(me, 1)], local_sems.at[a])
        local.wait() if wait else local.start()
        for k in range(1, N_DEV):
            to, to_i = _peer(x, y, c, k)
            cp = pltpu.make_async_remote_copy(
                src_ref=ins[a].at[pl.ds(to_i, 1)] if from_block[a] else ins[a], dst_ref=outs[a].at[pl.ds(me, 1)],
                send_sem=send_sems.at[a, k - 1], recv_sem=recv_sems.at[a, k - 1], device_id=to, device_id_type=MESH)
            cp.wait() if wait else cp.start()


def _exchange_comm(grads=(), parts=()):
    ins = list(grads) + list(parts)
    from_block = [True] * len(grads) + [False] * len(parts)
    out_shape = [_sds(g.shape, g.dtype) for g in grads] + [_sds((N_DEV,) + p.shape[1:], p.dtype) for p in parts]

    def start(ins_, outs_, sems_):
        _direct_copies(ins_, outs_, sems_, False, from_block)

    def finish(ins_, outs_, sems_):
        _direct_copies(ins_, outs_, sems_, True, from_block)

    return _Comm(ins, out_shape, _gather_sems(len(ins)), start, finish)


def _call(name, body, grid, in_specs, out_specs, out_shape, args, scratch=(), comm=None):
    params = pltpu.CompilerParams(dimension_semantics=("arbitrary",) * len(grid), vmem_limit_bytes=VMEM_LIMIT)
    if comm is None:
        outs = pl.pallas_call(body, name=name, grid=grid, in_specs=list(in_specs), out_specs=list(out_specs),
                              out_shape=list(out_shape), scratch_shapes=list(scratch), compiler_params=params)(*args)
        return list(outs), []
    ni, nci, no, nco, ns = len(in_specs), len(comm.args), len(out_specs), len(comm.out_shape), len(scratch)

    def carrying(*refs):
        ins, refs = refs[:ni], refs[ni:]
        cin, refs = refs[:nci], refs[nci:]
        outs, refs = refs[:no], refs[no:]
        cout, refs = refs[:nco], refs[nco:]
        scr, csems = refs[:ns], refs[ns:]
        ids = [pl.program_id(ax) for ax in range(len(grid))]
        first, last = ids[0] == 0, ids[0] == grid[0] - 1
        for ax in range(1, len(grid)):
            first, last = first & (ids[ax] == 0), last & (ids[ax] == grid[ax] - 1)

        @pl.when(first)
        def _():
            comm.start(cin, cout, csems)

        body(*ins, *outs, *scr)

        @pl.when(last)
        def _():
            comm.finish(cin, cout, csems)

    outs = pl.pallas_call(
        carrying, name=name, grid=grid, in_specs=list(in_specs) + [ANY] * nci, out_specs=list(out_specs) + [ANY] * nco,
        out_shape=list(out_shape) + list(comm.out_shape), scratch_shapes=list(scratch) + list(comm.scratch),
        compiler_params=params)(*args, *comm.args)
    return list(outs[:no]), list(outs[no:])


def _comm_only(name, comm):
    def body(*refs):
        nci, nco = len(comm.args), len(comm.out_shape)
        cin, cout, csems = refs[:nci], refs[nci:nci + nco], refs[nci + nco:]
        comm.start(cin, cout, csems)
        comm.finish(cin, cout, csems)

    return pl.pallas_call(body, name=name, in_specs=[ANY] * len(comm.args), out_specs=[ANY] * len(comm.out_shape),
                          out_shape=list(comm.out_shape), scratch_shapes=list(comm.scratch))(*comm.args)


def _gather_first(first, later):
    nf, nl = len(first), len(later)
    dts = [BF16] * (nf - 2) + [F32, F32]

    def body(*refs):
        ins, refs = refs[:nf + nl], refs[nf + nl:]
        outs, refs = refs[:nf], refs[nf:]
        casts, refs = refs[:nl], refs[nl:]
        stage, sems = refs[:nf], refs[nf:]
        for a in range(nf):
            stage[a][...] = ins[a][...].astype(dts[a])
        _gather_start(stage, outs, sems)
        for a in range(nl):
            casts[a][...] = ins[nf + a][...].astype(BF16)
        _gather_finish(stage, outs, sems)

    res = pl.pallas_call(
        body, name="gather_first",
        in_specs=[VMEM] * (nf + nl), out_specs=[ANY] * nf + [VMEM] * nl,
        out_shape=[_sds((N_DEV,) + s.shape[1:], dt) for s, dt in zip(first, dts)]
        + [_sds(s.shape, BF16) for s in later],
        scratch_shapes=[pltpu.VMEM(s.shape, dt) for s, dt in zip(first, dts)] + _gather_sems(nf),
        compiler_params=pltpu.CompilerParams(vmem_limit_bytes=VMEM_LIMIT),
    )(*first, *later)
    return list(res[:nf]), list(res[nf:])


def _a_mix_fwd(x, g, w_in, ln_g, ln_b, w_s, b_st, w_out, comm=None):
    t = x.shape[0]
    nblk = TM // GMLP_BLOCK

    def body(x_ref, g_ref, win_ref, lng_ref, lnb_ref, ws_ref, bst_ref, wout_ref, h_ref, z_ref, gated_scr):
        xv = x_ref[...]
        hb = _rms_fwd(xv, g_ref[...])[0].astype(BF16)
        for d in range(N_DEV):
            z_ref[:, d * FF_SLOT:(d + 1) * FF_SLOT] = _dot(hb, win_ref[d])
        u = _gelu(z_ref[:, :GATE_DIM])
        vb = _ln_fwd(_gelu(z_ref[:, GATE_DIM:]), lng_ref[...], lnb_ref[...])[0].astype(BF16)
        mask = _gate_mask()
        for gi in range(A_GROUPS):
            wm = jnp.where(mask, ws_ref[gi], 0.0).astype(BF16)
            bias = bst_ref[:, gi:gi + 1]
            cs = slice(gi * A_GROUP_DIM, (gi + 1) * A_GROUP_DIM)
            for n in range(nblk):
                rs = slice(n * GMLP_BLOCK, (n + 1) * GMLP_BLOCK)
                sv = _dot(wm, vb[rs, cs]) + bias
                gated_scr[rs, cs] = (u[rs, cs] * sv).astype(BF16)
        h_ref[...] = xv + _dot(gated_scr[...], wout_ref[...])

    return _call(
        "a_mix_fwd", body, (t // TM,),
        [_row(D_MODEL), _res((1, D_MODEL)), _res((N_DEV, D_MODEL, FF_SLOT)), _res((1, GATE_DIM)),
         _res((1, GATE_DIM)), _res((A_GROUPS, GMLP_BLOCK, GMLP_BLOCK)), _res((GMLP_BLOCK, A_GROUPS)),
         _res((GATE_DIM, D_MODEL))],
        [_row(D_MODEL), _row(2 * GATE_DIM)],
        [_sds((t, D_MODEL), F32), _sds((t, 2 * GATE_DIM), F32)],
        (x, g, w_in, ln_g, ln_b, w_s, b_st, w_out), scratch=[pltpu.VMEM((TM, GATE_DIM), BF16)], comm=comm)


MLP_W_SPECS = (_res((N_DEV, D_MODEL, FF_SLOT)), _res((N_DEV, FF_SLOT, D_MODEL)))


def _mlp_fwd(h, g, w1, w2, layer, comm=None):
    t = h.shape[0]

    def body(h_ref, g_ref, w1_ref, w2_ref, o_ref, a_ref):
        hv = h_ref[...]
        hb = _rms_fwd(hv, g_ref[...])[0].astype(BF16)
        o_ref[...] = hv
        for d in range(N_DEV):
            a = _dot(hb, w1_ref[d])
            a_ref[:, d * FF_SLOT:(d + 1) * FF_SLOT] = a
            r = jnp.maximum(a, 0.0)
            o_ref[...] += _dot((r * r).astype(BF16), w2_ref[d])

    return _call(
        f"mlp_fwd_{layer}", body, (t // TM,), [_row(D_MODEL), _res((1, D_MODEL)), *MLP_W_SPECS],
        [_row(D_MODEL), _row(D_FF)], [_sds((t, D_MODEL), F32), _sds((t, D_FF), F32)], (h, g, w1, w2), comm=comm)


KVQ_W_SPECS = (_res((1, D_MODEL)), _res((D_MODEL, KV_LORA + QK_ROPE)), _res((1, KV_LORA)),
               _res((B_HEADS, KV_LORA, QK_NOPE + V_HEAD)), _res((1, D_MODEL)), _res((D_MODEL, Q_LORA)),
               _res((1, Q_LORA)), _res((B_HEADS, Q_LORA, QK_NOPE + QK_ROPE)))


def _kvq_fwd(h, pos, inv_freq, kvq_w):
    t = h.shape[0]
    half = QK_ROPE // 2

    def body(h_ref, pos_ref, invf_ref, srcg_ref, wkva_ref, kvag_ref, wkvb_ref, mixg_ref, wqa_ref, qg_ref, wqb_ref,
             ckv_ref, kn_ref, v_ref, kpe_ref, cqpre_ref, q_ref, cos_ref, sin_ref):
        hv = h_ref[...]
        xhat = hv * lax.rsqrt(jnp.mean(hv * hv, axis=-1, keepdims=True) + EPS)
        ang = pos_ref[...].astype(F32) * invf_ref[...]
        cos, sin = jnp.cos(ang), jnp.sin(ang)
        cos_ref[...] = cos
        sin_ref[...] = sin
        ckv = _dot((xhat * srcg_ref[...]).astype(BF16), wkva_ref[...])
        ckv_ref[...] = ckv
        cb = _rms_fwd(ckv[:, :KV_LORA], kvag_ref[...])[0].astype(BF16)
        kpe_ref[...] = _rope(ckv[:, KV_LORA:], cos, sin).astype(BF16)
        for hd in range(B_HEADS):
            kv = _dot(cb, wkvb_ref[hd])
            kn_ref[hd] = kv[:, :QK_NOPE].astype(BF16)
            v_ref[hd] = kv[:, QK_NOPE:].astype(BF16)
        cqpre = _dot((xhat * mixg_ref[...]).astype(BF16), wqa_ref[...])
        cqpre_ref[...] = cqpre
        cqb = _rms_fwd(cqpre, qg_ref[...])[0].astype(BF16)
        for hd in range(B_HEADS):
            q = _dot(cqb, wqb_ref[hd])
            q_ref[hd, :, 0:QK_NOPE] = q[:, :QK_NOPE].astype(BF16)
            q_ref[hd, :, QK_NOPE:] = _rope(q[:, QK_NOPE:], cos, sin).astype(BF16)

    return _call(
        "kvq_fwd", body, (t // TM,), [_row(D_MODEL), _row(1), _res((1, half)), *KVQ_W_SPECS],
        [_row(KV_LORA + QK_ROPE), _heads(QK_NOPE), _heads(V_HEAD), _row(QK_ROPE), _row(Q_LORA),
         _heads(QK_NOPE + QK_ROPE), _row(half), _row(half)],
        [_sds((t, KV_LORA + QK_ROPE), F32), _sds((B_HEADS, t, QK_NOPE), BF16), _sds((B_HEADS, t, V_HEAD), BF16),
         _sds((t, QK_ROPE), BF16), _sds((t, Q_LORA), F32), _sds((B_HEADS, t, QK_NOPE + QK_ROPE), BF16),
         _sds((t, half), F32), _sds((t, half), F32)],
        (h, pos, inv_freq, *kvq_w))[0]


def _softmax_rows(qn, qp, kn, kpe, mask):
    s = (_dot_nt(qn, kn) + _dot_nt(qp, kpe)) * ATT_SCALE
    s = jnp.where(mask, s, jnp.finfo(F32).min)
    e = jnp.exp(s - jnp.max(s, axis=-1, keepdims=True))
    return e * (1.0 / jnp.sum(e, axis=-1, keepdims=True))


def _attn_fwd(h, q, kn, kpe, v, w_o, comm=None):
    t = h.shape[0]

    def body(h_ref, q_ref, kn_ref, kpe_ref, v_ref, wo_ref, o_ref, att_ref):
        mask = _att_mask(pl.program_id(0) * TM, TM, t)
        o_ref[...] = h_ref[...]
        for hd in range(B_HEADS):
            p = _softmax_rows(q_ref[hd, :, 0:QK_NOPE], q_ref[hd, :, QK_NOPE:], kn_ref[hd], kpe_ref[...], mask)
            ob = _dot(p.astype(BF16), v_ref[hd]).astype(BF16)
            att_ref[hd] = ob
            o_ref[...] += _dot(ob, wo_ref[hd])

    return _call(
        "attn_fwd", body, (t // TM,),
        [_row(D_MODEL), _heads(QK_NOPE + QK_ROPE), _res((B_HEADS, t, QK_NOPE)), _res((t, QK_ROPE)),
         _res((B_HEADS, t, V_HEAD)), _res((B_HEADS, V_HEAD, D_MODEL))],
        [_row(D_MODEL), _heads(V_HEAD)], [_sds((t, D_MODEL), F32), _sds((B_HEADS, t, V_HEAD), BF16)],
        (h, q, kn, kpe, v, w_o), comm=comm)


def _loss_head(h, g, target):
    t = h.shape[0]

    def body(h_ref, g_ref, t_ref, loss_ref, dh_ref, dg_ref):
        y, xhat, rstd = _rms_fwd(h_ref[...], g_ref[...])
        err = y - t_ref[...]
        part = 0.5 * jnp.sum(jnp.mean(err * err, axis=-1, keepdims=True), axis=0, keepdims=True)
        dx, dg = _rms_bwd(err * (1.0 / D_MODEL), xhat, rstd, g_ref[...])
        dh_ref[...] = dx
        _acc(dg_ref, dg)
        _acc(loss_ref, part)

    return _call(
        "loss_head", body, (t // TM,), [_row(D_MODEL), _res((1, D_MODEL)), _row(D_MODEL)],
        [_const((1, 1)), _row(D_MODEL), _const((1, D_MODEL))],
        [_sds((1, 1), F32), _sds((t, D_MODEL), F32), _sds((1, D_MODEL), F32)], (h, g, target))[0]


def _mlp_bwd(h, a, dho, g, w1, w2, layer, comm=None):
    t = h.shape[0]

    def body(h_ref, a_ref, dho_ref, g_ref, w1_ref, w2_ref, dhi_ref, dg_ref, hn_ref, f_ref, da_ref):
        gv = g_ref[...]
        y, xhat, rstd = _rms_fwd(h_ref[...], gv)
        hn_ref[...] = y.astype(BF16)
        dho_v = dho_ref[...]
        dhob = dho_v.astype(BF16)
        dhn = jnp.zeros((TM, D_MODEL), F32)
        for d in range(N_DEV):
            cs = slice(d * FF_SLOT, (d + 1) * FF_SLOT)
            r = jnp.maximum(a_ref[:, cs], 0.0)
            f_ref[:, cs] = (r * r).astype(BF16)
            da = (_dot_nt(dhob, w2_ref[d]) * (2.0 * r)).astype(BF16)
            da_ref[:, cs] = da
            dhn = dhn + _dot_nt(da, w1_ref[d])
        dx, dg = _rms_bwd(dhn, xhat, rstd, gv)
        dhi_ref[...] = dho_v + dx
        _acc(dg_ref, dg)

    return _call(
        f"mlp_bwd_{layer}", body, (t // TM,),
        [_row(D_MODEL), _row(D_FF), _row(D_MODEL), _res((1, D_MODEL)), *MLP_W_SPECS],
        [_row(D_MODEL), _const((1, D_MODEL)), _row(D_MODEL), _row(D_FF), _row(D_FF)],
        [_sds((t, D_MODEL), F32), _sds((1, D_MODEL), F32), _sds((t, D_MODEL), BF16), _sds((t, D_FF), BF16),
         _sds((t, D_FF), BF16)],
        (h, a, dho, g, w1, w2), comm=comm)


def _attn_bwd(dh, q, kn, kpe, v, w_o, cos, sin, comm=None):
    t = dh.shape[0]
    half = QK_ROPE // 2

    def body(dh_ref, q_ref, kn_ref, kpe_ref, v_ref, wo_ref, cos_ref, sin_ref, dq_ref, dkn_ref, dv_ref, dkpe_ref):
        hd, i = pl.program_id(0), pl.program_id(1)

        @pl.when(i == 0)
        def _():
            dkn_ref[...] = jnp.zeros_like(dkn_ref)
            dv_ref[...] = jnp.zeros_like(dv_ref)

        @pl.when((i == 0) & (hd == 0))
        def _():
            dkpe_ref[...] = jnp.zeros_like(dkpe_ref)

        mask = _att_mask(i * TM, TM, t)
        qn, qp = q_ref[:, 0:QK_NOPE], q_ref[:, QK_NOPE:]
        do = _dot_nt(dh_ref[...].astype(BF16), wo_ref[...]).astype(BF16)
        p = _softmax_rows(qn, qp, kn_ref[...], kpe_ref[...], mask)
        dp = _dot_nt(do, v_ref[...])
        ds = (p * (dp - jnp.sum(p * dp, axis=-1, keepdims=True)) * ATT_SCALE).astype(BF16)
        dq_ref[:, 0:QK_NOPE] = _dot(ds, kn_ref[...]).astype(BF16)
        dq_ref[:, QK_NOPE:] = _rope(_dot(ds, kpe_ref[...]), cos_ref[...], -sin_ref[...]).astype(BF16)
        dkn_ref[...] += _dot_tn(ds, qn)
        dv_ref[...] += _dot_tn(p.astype(BF16), do)
        dkpe_ref[...] += _dot_tn(ds, qp)

    def per_head(rows, d, tiled):
        return pl.BlockSpec((None, rows, d), (lambda hd, i: (hd, i, 0)) if tiled else (lambda hd, i: (hd, 0, 0)))

    def tile(d):
        return pl.BlockSpec((TM, d), lambda hd, i: (i, 0))

    return _call(
        "attn_bwd", body, (B_HEADS, t // TM),
        [tile(D_MODEL), per_head(TM, QK_NOPE + QK_ROPE, True), per_head(t, QK_NOPE, False),
         pl.BlockSpec((t, QK_ROPE), lambda hd, i: (0, 0)), per_head(t, V_HEAD, False),
         per_head(V_HEAD, D_MODEL, False), tile(half), tile(half)],
        [per_head(TM, QK_NOPE + QK_ROPE, True), per_head(t, QK_NOPE, False), per_head(t, V_HEAD, False),
         pl.BlockSpec((t, QK_ROPE), lambda hd, i: (0, 0))],
        [_sds((B_HEADS, t, QK_NOPE + QK_ROPE), BF16), _sds((B_HEADS, t, QK_NOPE), F32),
         _sds((B_HEADS, t, V_HEAD), F32), _sds((t, QK_ROPE), F32)],
        (dh, q, kn, kpe, v, w_o, cos, sin), comm=comm)


def _kvq_bwd(h, dh, ckv, cqpre, dq, dkn, dv, dkpe, cos, sin, kvq_w):
    t = h.shape[0]
    half = QK_ROPE // 2

    def body(h_ref, dh_ref, ckv_ref, cqpre_ref, dq_ref, dkn_ref, dv_ref, dkpe_ref, cos_ref, sin_ref,
             srcg_ref, wkva_ref, kvag_ref, wkvb_ref, mixg_ref, wqa_ref, qg_ref, wqb_ref,
             dhi_ref, hq_ref, hk_ref, cq_ref, dcqpre_ref, c_ref, dkv_ref, dckv_ref,
             dmixg_ref, dsrcg_ref, dqg_ref, dkvag_ref):
        hv = h_ref[...]
        rstd = lax.rsqrt(jnp.mean(hv * hv, axis=-1, keepdims=True) + EPS)
        xhat = hv * rstd
        mixg, srcg, qg, kvag = mixg_ref[...], srcg_ref[...], qg_ref[...], kvag_ref[...]
        hq_ref[...] = (xhat * mixg).astype(BF16)
        hk_ref[...] = (xhat * srcg).astype(BF16)
        cq, cqhat, crstd = _rms_fwd(cqpre_ref[...], qg)
        cq_ref[...] = cq.astype(BF16)
        dcq = jnp.zeros((TM, Q_LORA), F32)
        for hd in range(B_HEADS):
            dcq = dcq + _dot_nt(dq_ref[hd], wqb_ref[hd])
        dcqpre, dqg = _rms_bwd(dcq, cqhat, crstd, qg)
        dcqpre_b = dcqpre.astype(BF16)
        dcqpre_ref[...] = dcqpre_b
        dxq, dmixg = _rms_bwd(_dot_nt(dcqpre_b, wqa_ref[...]), xhat, rstd, mixg)
        ckv = ckv_ref[...]
        c, chat, krstd = _rms_fwd(ckv[:, :KV_LORA], kvag)
        c_ref[...] = c.astype(BF16)
        dc = jnp.zeros((TM, KV_LORA), F32)
        for hd in range(B_HEADS):
            dkv = jnp.concatenate([dkn_ref[hd], dv_ref[hd]], axis=-1).astype(BF16)
            dkv_ref[hd] = dkv
            dc = dc + _dot_nt(dkv, wkvb_ref[hd])
        dlat, dkvag = _rms_bwd(dc, chat, krstd, kvag)
        dpe = _rope(dkpe_ref[...], cos_ref[...], -sin_ref[...])
        dckv_b = jnp.concatenate([dlat, dpe], axis=-1).astype(BF16)
        dckv_ref[...] = dckv_b
        dxk, dsrcg = _rms_bwd(_dot_nt(dckv_b, wkva_ref[...]), xhat, rstd, srcg)
        dhi_ref[...] = dh_ref[...] + dxq + dxk
        _acc(dmixg_ref, dmixg)
        _acc(dsrcg_ref, dsrcg)
        _acc(dqg_ref, dqg)
        _acc(dkvag_ref, dkvag)

    return _call(
        "kvq_bwd", body, (t // TM,),
        [_row(D_MODEL), _row(D_MODEL), _row(KV_LORA + QK_ROPE), _row(Q_LORA), _heads(QK_NOPE + QK_ROPE),
         _heads(QK_NOPE), _heads(V_HEAD), _row(QK_ROPE), _row(half), _row(half), *KVQ_W_SPECS],
        [_row(D_MODEL), _row(D_MODEL), _row(D_MODEL), _row(Q_LORA), _row(Q_LORA), _row(KV_LORA),
         _heads(QK_NOPE + V_HEAD), _row(KV_LORA + QK_ROPE),
         _const((1, D_MODEL)), _const((1, D_MODEL)), _const((1, Q_LORA)), _const((1, KV_LORA))],
        [_sds((t, D_MODEL), F32), _sds((t, D_MODEL), BF16), _sds((t, D_MODEL), BF16), _sds((t, Q_LORA), BF16),
         _sds((t, Q_LORA), BF16), _sds((t, KV_LORA), BF16), _sds((B_HEADS, t, QK_NOPE + V_HEAD), BF16),
         _sds((t, KV_LORA + QK_ROPE), BF16),
         _sds((1, D_MODEL), F32), _sds((1, D_MODEL), F32), _sds((1, Q_LORA), F32), _sds((1, KV_LORA), F32)],
        (h, dh, ckv, cqpre, dq, dkn, dv, dkpe, cos, sin, *kvq_w))[0]


def _a_mix_bwd(x, z, dh, g, w_in, ln_g, ln_b, w_s, b_st, w_out, comm=None):
    t = x.shape[0]
    tm = TM_GATE
    nblk = tm // GMLP_BLOCK

    def body(x_ref, z_ref, dh_ref, g_ref, win_ref, lng_ref, lnb_ref, ws_ref, bst_ref, wout_ref,
             dx_ref, hn_ref, gated_ref, dz_ref, dg_ref, dlng_ref, dlnb_ref, dws_ref, dbs_ref, du_scr, dvn_scr):
        @pl.when(pl.program_id(0) == 0)
        def _():
            dws_ref[...] = jnp.zeros_like(dws_ref)
            dbs_ref[...] = jnp.zeros_like(dbs_ref)

        gv, lng = g_ref[...], lng_ref[...]
        y, xhat, rstd = _rms_fwd(x_ref[...], gv)
        hn_ref[...] = y.astype(BF16)
        dhv = dh_ref[...]
        dgated = _dot_nt(dhv.astype(BF16), wout_ref[...])
        u = _gelu(z_ref[:, :GATE_DIM])
        vn, vhat, lrstd = _ln_fwd(_gelu(z_ref[:, GATE_DIM:]), lng, lnb_ref[...])
        vb = vn.astype(BF16)
        mask = _gate_mask()
        for gi in range(A_GROUPS):
            wm = jnp.where(mask, ws_ref[gi], 0.0).astype(BF16)
            bias = bst_ref[:, gi:gi + 1]
            cs = slice(gi * A_GROUP_DIM, (gi + 1) * A_GROUP_DIM)
            dws = jnp.zeros((GMLP_BLOCK, GMLP_BLOCK), F32)
            dbs = jnp.zeros((GMLP_BLOCK, 1), F32)
            for n in range(nblk):
                rs = slice(n * GMLP_BLOCK, (n + 1) * GMLP_BLOCK)
                sv = _dot(wm, vb[rs, cs]) + bias
                gated_ref[rs, cs] = (u[rs, cs] * sv).astype(BF16)
                du_scr[rs, cs] = dgated[rs, cs] * sv
                dsv = dgated[rs, cs] * u[rs, cs]
                dsvb = dsv.astype(BF16)
                dws = dws + _dot_nt(dsvb, vb[rs, cs])
                dbs = dbs + jnp.sum(dsv, axis=-1, keepdims=True)
                dvn_scr[rs, cs] = _dot_tn(wm, dsvb)
            dws_ref[gi] += jnp.where(mask, dws, 0.0)
            dbs_ref[gi] += dbs
        dvn = dvn_scr[...]
        dvhat = dvn * lng
        dv = lrstd * (dvhat - jnp.mean(dvhat, axis=-1, keepdims=True)
                      - vhat * jnp.mean(dvhat * vhat, axis=-1, keepdims=True))
        dz_ref[:, :GATE_DIM] = (du_scr[...] * _gelu_grad(z_ref[:, :GATE_DIM])).astype(BF16)
        dz_ref[:, GATE_DIM:] = (dv * _gelu_grad(z_ref[:, GATE_DIM:])).astype(BF16)
        dhn = jnp.zeros((tm, D_MODEL), F32)
        for d in range(N_DEV):
            dhn = dhn + _dot_nt(dz_ref[:, d * FF_SLOT:(d + 1) * FF_SLOT], win_ref[d])
        dx, dg = _rms_bwd(dhn, xhat, rstd, gv)
        dx_ref[...] = dhv + dx
        _acc(dg_ref, dg)
        _acc(dlng_ref, jnp.sum(dvn * vhat, axis=0, keepdims=True))
        _acc(dlnb_ref, jnp.sum(dvn, axis=0, keepdims=True))

    return _call(
        "a_mix_bwd", body, (t // tm,),
        [_row(D_MODEL, tm), _row(2 * GATE_DIM, tm), _row(D_MODEL, tm), _res((1, D_MODEL)),
         _res((N_DEV, D_MODEL, FF_SLOT)), _res((1, GATE_DIM)), _res((1, GATE_DIM)),
         _res((A_GROUPS, GMLP_BLOCK, GMLP_BLOCK)), _res((GMLP_BLOCK, A_GROUPS)), _res((GATE_DIM, D_MODEL))],
        [_row(D_MODEL, tm), _row(D_MODEL, tm), _row(GATE_DIM, tm), _row(2 * GATE_DIM, tm),
         _const((1, D_MODEL)), _const((1, GATE_DIM)), _const((1, GATE_DIM)),
         _const((A_GROUPS, GMLP_BLOCK, GMLP_BLOCK)), _const((A_GROUPS, GMLP_BLOCK, 1))],
        [_sds((t, D_MODEL), F32), _sds((t, D_MODEL), BF16), _sds((t, GATE_DIM), BF16),
         _sds((t, 2 * GATE_DIM), BF16), _sds((1, D_MODEL), F32), _sds((1, GATE_DIM), F32),
         _sds((1, GATE_DIM), F32), _sds((A_GROUPS, GMLP_BLOCK, GMLP_BLOCK), F32),
         _sds((A_GROUPS, GMLP_BLOCK, 1), F32)],
        (x, z, dh, g, w_in, ln_g, ln_b, w_s, b_st, w_out),
        scratch=[pltpu.VMEM((tm, GATE_DIM), F32), pltpu.VMEM((tm, GATE_DIM), F32)], comm=comm)


def _wgrad(name, a, b, a_spec, b_spec, m, n):
    def body(a_ref, b_ref, o_ref):
        o_ref[0] = _dot_tn(a_ref[...].astype(BF16), b_ref[...].astype(BF16)).astype(BF16)

    return _call(name, body, (N_DEV,), [a_spec, b_spec], [pl.BlockSpec((1, m, n), lambda d: (d, 0, 0))],
                 [_sds((N_DEV, m, n), BF16)], (a, b))[0][0]


def _full(t, d):
    return pl.BlockSpec((t, d), lambda i: (0, 0), pipeline_mode=pl.Buffered(1))


def _cols(t, d):
    return pl.BlockSpec((t, d), lambda i: (0, i))


def _head(t, d):
    return pl.BlockSpec((None, t, d), lambda i: (i, 0, 0))


def _local_step(x, pos, target, inv_freq, wg, sm, shards=None):
    t = x.shape[0]
    wg = dict(wg)
    dist = shards is not None
    mix_g = [sm["norm_mix_g"][l:l + 1] for l in range(2)]
    mlp_g = [sm["norm_mlp_g"][l:l + 1] for l in range(2)]

    def gather(names):
        return _gather_comm([shards[k] for k in names]) if dist else None

    def send(grads):
        return _exchange_comm(grads=grads) if dist else None

    def a_args():
        return (wg["a_w_in"], wg["a_ln_v_g"], wg["a_ln_v_b"], sm["a_w_s"], sm["a_b_st"], wg["a_w_out"])

    def kvq_w():
        return (sm["kv_src_norm_g"], wg["kv_w_a"], sm["kv_a_norm_g"], wg["kv_w_b"], mix_g[1], wg["b_w_q_a"],
                sm["b_q_norm_g"], wg["b_w_q_b"])

    names = ("mlp_w1_0", "mlp_w2_0")
    (h1, z), got = _a_mix_fwd(x, mix_g[0], *a_args(), comm=gather(names))
    wg.update(zip(names, got))
    names = ("kv_w_a", "kv_w_b", "b_w_q_a", "b_w_q_b", "b_w_o")
    (h2, a0), got = _mlp_fwd(h1, mlp_g[0], wg["mlp_w1_0"], wg["mlp_w2_0"], 0, comm=gather(names))
    wg.update(zip(names, got))
    if dist:
        wg["b_w_q_a"] = wg["b_w_q_a"].reshape(D_MODEL, Q_LORA)
        wg["kv_w_a"] = wg["kv_w_a"].reshape(D_MODEL, KV_LORA + QK_ROPE)
    ckv, kn, v, kpe, cqpre, q, cos, sin = _kvq_fwd(h2, pos, inv_freq, kvq_w())
    names = ("mlp_w1_1", "mlp_w2_1")
    (h3, att), got = _attn_fwd(h2, q, kn, kpe, v, wg["b_w_o"], comm=gather(names))
    wg.update(zip(names, got))
    (h4, a1), _ = _mlp_fwd(h3, mlp_g[1], wg["mlp_w1_1"], wg["mlp_w2_1"], 1)
    loss, dh4, d_final_g = _loss_head(h4, sm["final_norm_g"], target)

    g = {}
    (dh3, d_mlp_g1, hn, f, da), _ = _mlp_bwd(h3, a1, dh4, mlp_g[1], wg["mlp_w1_1"], wg["mlp_w2_1"], 1)
    g["mlp_w1_1"] = _wgrad("wgrad_w1_1", hn, da, _full(t, D_MODEL), _cols(t, FF_SLOT), D_MODEL, FF_SLOT)
    g["mlp_w2_1"] = _wgrad("wgrad_w2_1", f, dh4, _cols(t, FF_SLOT), _full(t, D_MODEL), FF_SLOT, D_MODEL)
    g["b_w_o"] = _wgrad("wgrad_w_o", att, dh3, _head(t, V_HEAD), _full(t, D_MODEL), V_HEAD, D_MODEL)
    names = ("mlp_w1_1", "mlp_w2_1", "b_w_o")
    (dq, dkn, dv, dkpe), got = _attn_bwd(dh3, q, kn, kpe, v, wg["b_w_o"], cos, sin, comm=send([g[k] for k in names]))
    g.update(zip(names, got))
    (dh2, hq, hk, cq, dcqpre, c, dkv, dckv, d_mix_g1, d_src_g, d_q_g, d_kv_a_g) = _kvq_bwd(
        h2, dh3, ckv, cqpre, dq, dkn, dv, dkpe, cos, sin, kvq_w())
    g["b_w_q_a"] = _wgrad("wgrad_w_q_a", hq, dcqpre, _cols(t, D_MODEL // N_DEV), _full(t, Q_LORA),
                          D_MODEL // N_DEV, Q_LORA)
    g["b_w_q_b"] = _wgrad("wgrad_w_q_b", cq, dq, _full(t, Q_LORA), _head(t, QK_NOPE + QK_ROPE),
                          Q_LORA, QK_NOPE + QK_ROPE)
    g["kv_w_a"] = _wgrad("wgrad_kv_w_a", hk, dckv, _cols(t, D_MODEL // N_DEV), _full(t, KV_LORA + QK_ROPE),
                         D_MODEL // N_DEV, KV_LORA + QK_ROPE)
    g["kv_w_b"] = _wgrad("wgrad_kv_w_b", c, dkv, _full(t, KV_LORA), _head(t, QK_NOPE + V_HEAD),
                         KV_LORA, QK_NOPE + V_HEAD)
    names = ("b_w_q_a", "b_w_q_b", "kv_w_a", "kv_w_b")
    (dh1, d_mlp_g0, hn, f, da), got = _mlp_bwd(h1, a0, dh2, mlp_g[0], wg["mlp_w1_0"], wg["mlp_w2_0"], 0,
                                               comm=send([g[k] for k in names]))
    g.update(zip(names, got))
    g["mlp_w1_0"] = _wgrad("wgrad_w1_0", hn, da, _full(t, D_MODEL), _cols(t, FF_SLOT), D_MODEL, FF_SLOT)
    g["mlp_w2_0"] = _wgrad("wgrad_w2_0", f, dh2, _cols(t, FF_SLOT), _full(t, D_MODEL), FF_SLOT, D_MODEL)
    names = ("mlp_w1_0", "mlp_w2_0")
    (dx, hn, gated, dz, d_mix_g0, d_ln_g, d_ln_b, d_ws, d_bs), got = _a_mix_bwd(
        x, z, dh1, mix_g[0], *a_args(), comm=send([g[k] for k in names]))
    g.update(zip(names, got))
    g["a_w_in"] = _wgrad("wgrad_a_w_in", hn, dz, _full(t, D_MODEL), _cols(t, FF_SLOT), D_MODEL, FF_SLOT)
    g["a_w_out"] = _wgrad("wgrad_a_w_out", gated, dh1, _cols(t, GATE_DIM // N_DEV), _full(t, D_MODEL),
                          GATE_DIM // N_DEV, D_MODEL)
    small = {
        "norm_mix_g": jnp.concatenate([d_mix_g0, d_mix_g1], axis=0),
        "norm_mlp_g": jnp.concatenate([d_mlp_g0, d_mlp_g1], axis=0),
        "a_ln_v_g": d_ln_g.reshape(N_DEV, GATE_DIM // N_DEV),
        "a_ln_v_b": d_ln_b.reshape(N_DEV, GATE_DIM // N_DEV),
        "a_w_s": d_ws,
        "a_b_s": d_bs.reshape(A_GROUPS, GMLP_BLOCK),
        "b_q_norm_g": d_q_g,
        "kv_src_norm_g": d_src_g,
        "kv_a_norm_g": d_kv_a_g,
        "final_norm_g": d_final_g,
    }
    return loss, dx, g, small


def _adamw(w, g, m, v):
    m = ADAM_B1 * m + (1.0 - ADAM_B1) * g
    v = ADAM_B2 * v + (1.0 - ADAM_B2) * (g * g)
    m_hat = m / (1.0 - ADAM_B1 ** ADAM_STEP)
    v_hat = v / (1.0 - ADAM_B2 ** ADAM_STEP)
    return -ADAM_LR * (m_hat / (jnp.sqrt(v_hat) + ADAM_EPS) + ADAM_WD * w), m, v


def _sum_in_device_order(r_ref):
    g = r_ref[0].astype(F32)
    for j in range(1, N_DEV):
        g = g + r_ref[j].astype(F32)
    return g


def _adamw_sharded(name, recv, w, m, v, layer, into=None):
    layers, r, c = w.shape
    tr = math.gcd(r, 256)

    def body(r_ref, w_ref, m_ref, v_ref, *refs):
        g_ref, d_ref, nm_ref, nv_ref = refs[-4:]
        g = _sum_in_device_order(r_ref)
        g_ref[...] = g
        d_ref[...], nm_ref[...], nv_ref[...] = _adamw(w_ref[...], g, m_ref[...], v_ref[...])

    blk = pl.BlockSpec((None, tr, c), lambda i: (layer, i, 0))
    args, in_specs, aliases = [recv, w, m, v], [pl.BlockSpec((N_DEV, tr, c), lambda i: (0, i, 0)), blk, blk, blk], {}
    if into is not None:
        args += list(into)
        in_specs += [ANY] * 4
        aliases = {4 + i: i for i in range(4)}
    return pl.pallas_call(
        body, name=name, grid=(r // tr,), in_specs=in_specs, out_specs=[blk] * 4,
        out_shape=[_sds(w.shape, F32)] * 4, input_output_aliases=aliases,
        compiler_params=pltpu.CompilerParams(dimension_semantics=("arbitrary",), vmem_limit_bytes=VMEM_LIMIT),
    )(*args)


def _adamw_small(recvs, ws, ms, vs, own_row):
    n = len(recvs)

    def body(*refs):
        r_refs, w_refs, m_refs, v_refs = (refs[i * n:(i + 1) * n] for i in range(4))
        outs, scr = refs[4 * n:8 * n], refs[8 * n:]
        me = _my_place()[3]
        for a in range(n):
            g = _sum_in_device_order(r_refs[a])
            if own_row[a]:
                scr[0][...] = g
                g = scr[0][pl.ds(me, 1), :]
            g_ref, d_ref, nm_ref, nv_ref = outs[4 * a:4 * a + 4]
            g_ref[...] = g
            d_ref[...], nm_ref[...], nv_ref[...] = _adamw(w_refs[a][...], g, m_refs[a][...], v_refs[a][...])

    out_shape = []
    for w in ws:
        out_shape += [_sds(w.shape, F32)] * 4
    return pl.pallas_call(
        body, name="adamw_small", in_specs=[VMEM] * (4 * n), out_specs=[VMEM] * (4 * n), out_shape=out_shape,
        scratch_shapes=[pltpu.VMEM((N_DEV, GATE_DIM // N_DEV), F32)],
    )(*recvs, *ws, *ms, *vs)


BIG = ("a_w_in", "a_w_out", "b_w_q_a", "b_w_q_b", "b_w_o", "kv_w_a", "kv_w_b", "mlp_w1", "mlp_w2")
SMALL = ("norm_mix_g", "norm_mlp_g", "a_ln_v_g", "a_ln_v_b", "a_w_s", "a_b_s", "b_q_norm_g", "kv_src_norm_g",
         "kv_a_norm_g", "final_norm_g")
WEIGHTS = ("norm_mix_g", "norm_mlp_g", "a_w_in", "a_ln_v_g", "a_ln_v_b", "a_w_s", "a_b_s", "a_w_out", "b_w_q_a",
           "b_q_norm_g", "b_w_q_b", "b_w_o", "kv_src_norm_g", "kv_w_a", "kv_a_norm_g", "kv_w_b", "mlp_w1", "mlp_w2",
           "final_norm_g")


def _two_d(name, a):
    if name in ("a_w_s", "a_b_s"):
        return a.reshape(a.shape[1:])
    return a.reshape(1, -1) if a.ndim == 1 else a


def _three_d(a):
    return a if a.ndim == 3 else a.reshape((1,) + a.shape)


def kernel(x, positions, norm_mix_g, norm_mlp_g, a_w_in, a_ln_v_g, a_ln_v_b, a_w_s, a_b_s, a_w_out, b_w_q_a, b_q_norm_g, b_w_q_b, b_w_o, kv_src_norm_g, kv_w_a, kv_a_norm_g, kv_w_b, mlp_w1, mlp_w2, final_norm_g, loss_target, m_norm_mix_g, m_norm_mlp_g, m_a_w_in, m_a_ln_v_g, m_a_ln_v_b, m_a_w_s, m_a_b_s, m_a_w_out, m_b_w_q_a, m_b_q_norm_g, m_b_w_q_b, m_b_w_o, m_kv_src_norm_g, m_kv_w_a, m_kv_a_norm_g, m_kv_w_b, m_mlp_w1, m_mlp_w2, m_final_norm_g, v_norm_mix_g, v_norm_mlp_g, v_a_w_in, v_a_ln_v_g, v_a_ln_v_b, v_a_w_s, v_a_b_s, v_a_w_out, v_b_w_q_a, v_b_q_norm_g, v_b_w_q_b, v_b_w_o, v_kv_src_norm_g, v_kv_w_a, v_kv_a_norm_g, v_kv_w_b, v_mlp_w1, v_mlp_w2, v_final_norm_g):
    w = dict(norm_mix_g=norm_mix_g, norm_mlp_g=norm_mlp_g, a_w_in=a_w_in, a_ln_v_g=a_ln_v_g, a_ln_v_b=a_ln_v_b,
             a_w_s=a_w_s, a_b_s=a_b_s, a_w_out=a_w_out, b_w_q_a=b_w_q_a, b_q_norm_g=b_q_norm_g, b_w_q_b=b_w_q_b,
             b_w_o=b_w_o, kv_src_norm_g=kv_src_norm_g, kv_w_a=kv_w_a, kv_a_norm_g=kv_a_norm_g, kv_w_b=kv_w_b,
             mlp_w1=mlp_w1, mlp_w2=mlp_w2, final_norm_g=final_norm_g)
    m = dict(norm_mix_g=m_norm_mix_g, norm_mlp_g=m_norm_mlp_g, a_w_in=m_a_w_in, a_ln_v_g=m_a_ln_v_g,
             a_ln_v_b=m_a_ln_v_b, a_w_s=m_a_w_s, a_b_s=m_a_b_s, a_w_out=m_a_w_out, b_w_q_a=m_b_w_q_a,
             b_q_norm_g=m_b_q_norm_g, b_w_q_b=m_b_w_q_b, b_w_o=m_b_w_o, kv_src_norm_g=m_kv_src_norm_g,
             kv_w_a=m_kv_w_a, kv_a_norm_g=m_kv_a_norm_g, kv_w_b=m_kv_w_b, mlp_w1=m_mlp_w1, mlp_w2=m_mlp_w2,
             final_norm_g=m_final_norm_g)
    v = dict(norm_mix_g=v_norm_mix_g, norm_mlp_g=v_norm_mlp_g, a_w_in=v_a_w_in, a_ln_v_g=v_a_ln_v_g,
             a_ln_v_b=v_a_ln_v_b, a_w_s=v_a_w_s, a_b_s=v_a_b_s, a_w_out=v_a_w_out, b_w_q_a=v_b_w_q_a,
             b_q_norm_g=v_b_q_norm_g, b_w_q_b=v_b_w_q_b, b_w_o=v_b_w_o, kv_src_norm_g=v_kv_src_norm_g,
             kv_w_a=v_kv_w_a, kv_a_norm_g=v_kv_a_norm_g, kv_w_b=v_kv_w_b, mlp_w1=v_mlp_w1, mlp_w2=v_mlp_w2,
             final_norm_g=v_final_norm_g)
    t = x.shape[1]

    first = ("a_w_in", "a_w_out", "a_ln_v_g", "a_ln_v_b")
    later = ("mlp_w1_0", "mlp_w2_0", "mlp_w1_1", "mlp_w2_1", "kv_w_a", "kv_w_b", "b_w_q_a", "b_w_q_b", "b_w_o")
    blocks = {k: _three_d(w[k]) for k in BIG if not k.startswith("mlp")}
    for k in ("mlp_w1", "mlp_w2"):
        blocks[k + "_0"], blocks[k + "_1"] = w[k][0:1], w[k][1:2]
    got, casts = _gather_first([blocks[k] if k in blocks else w[k] for k in first], [blocks[k] for k in later])
    wg = dict(zip(first, got))
    wg["a_w_out"] = wg["a_w_out"].reshape(GATE_DIM, D_MODEL)
    wg["a_ln_v_g"] = wg["a_ln_v_g"].reshape(1, GATE_DIM)
    wg["a_ln_v_b"] = wg["a_ln_v_b"].reshape(1, GATE_DIM)
    shards = dict(zip(later, casts))

    sm = {k: _two_d(k, w[k]) for k in SMALL if k not in ("a_ln_v_g", "a_ln_v_b")}
    sm["a_b_st"] = sm["a_b_s"].T
    inv_freq = (ROPE_THETA ** (-jnp.arange(0, QK_ROPE, 2, dtype=F32) / QK_ROPE)).reshape(1, QK_ROPE // 2)

    loss, dx, g, small = _local_step(x[0], positions.reshape(t, 1), loss_target[0], inv_freq, wg, sm, shards)
    loss = lax.psum(loss[0, 0], ("x", "y", "c"))

    parts = [small[k].reshape((1,) + small[k].shape) for k in SMALL]
    got = _comm_only("exchange_last", _exchange_comm(grads=[g["a_w_in"], g["a_w_out"]], parts=parts))
    g["a_w_in"], g["a_w_out"] = got[:2]
    small_recv = got[2:]

    out = {}
    for k in BIG:
        if k.startswith("mlp"):
            args = (_three_d(w[k]), _three_d(m[k]), _three_d(v[k]))
            res = _adamw_sharded(f"adamw_{k}_1", g[k + "_1"], *args, 1)
            res = _adamw_sharded(f"adamw_{k}_0", g[k + "_0"], *args, 0, into=res)
        else:
            res = _adamw_sharded("adamw_" + k, g[k], _three_d(w[k]), _three_d(m[k]), _three_d(v[k]), 0)
        out[k] = [o.reshape(w[k].shape) for o in res]
    own_row = [k in ("a_ln_v_g", "a_ln_v_b") for k in SMALL]
    res = _adamw_small(small_recv, [_two_d(k, w[k]) for k in SMALL], [_two_d(k, m[k]) for k in SMALL],
                       [_two_d(k, v[k]) for k in SMALL], own_row)
    for i, k in enumerate(SMALL):
        out[k] = [o.reshape(w[k].shape) for o in res[4 * i:4 * i + 4]]

    return (loss, dx.reshape(x.shape), *[out[k][0] for k in WEIGHTS], *[out[k][1] for k in WEIGHTS],
            *[out[k][2] for k in WEIGHTS], *[out[k][3] for k in WEIGHTS])
```

```python
import math

import jax
import jax.numpy as jnp
from jax import lax
from jax.experimental import pallas as pl
from jax.experimental.pallas import tpu as pltpu

F32, BF16 = jnp.float32, jnp.bfloat16
MESH = pl.DeviceIdType.MESH
ANY = pl.BlockSpec(memory_space=pl.ANY)
VMEM = pl.BlockSpec(memory_space=pltpu.VMEM)

N_DEV = 8
D_MODEL = 1024
CHUNK = 64
GMLP_BLOCK = 128
GATE_DIM = 2048
A_GROUPS = 8
A_GROUP_DIM = GATE_DIM // A_GROUPS
B_HEADS = 8
QK_NOPE, QK_ROPE, V_HEAD = 128, 64, 128
Q_LORA, KV_LORA = 384, 256
ROPE_THETA = 10000.0
D_FF = 4096
FF_SLOT = D_FF // N_DEV
EPS = 1e-6
ATT_SCALE = (QK_NOPE + QK_ROPE) ** -0.5

ADAM_LR, ADAM_B1, ADAM_B2, ADAM_EPS, ADAM_WD, ADAM_STEP = 0.001, 0.9, 0.999, 1e-08, 0.01, 10

TM = 256
TM_GATE = 128
VMEM_LIMIT = 56 * 1024 * 1024
INV_SQRT2 = 1.0 / math.sqrt(2.0)
INV_SQRT_2PI = 1.0 / math.sqrt(2.0 * math.pi)


def _dot(a, b):
    return jnp.dot(a, b, preferred_element_type=F32)


def _dot_nt(a, b):
    return lax.dot_general(a, b, (((1,), (1,)), ((), ())), preferred_element_type=F32)


def _dot_tn(a, b):
    return lax.dot_general(a, b, (((0,), (0,)), ((), ())), preferred_element_type=F32)


def _rms_fwd(x, g):
    rstd = lax.rsqrt(jnp.mean(x * x, axis=-1, keepdims=True) + EPS)
    xhat = x * rstd
    return xhat * g, xhat, rstd


def _rms_bwd(dy, xhat, rstd, g):
    dxhat = dy * g
    dx = rstd * (dxhat - xhat * jnp.mean(dxhat * xhat, axis=-1, keepdims=True))
    return dx, jnp.sum(dy * xhat, axis=0, keepdims=True)


def _ln_fwd(v, g, b):
    mu = jnp.mean(v, axis=-1, keepdims=True)
    vc = v - mu
    rstd = lax.rsqrt(jnp.mean(vc * vc, axis=-1, keepdims=True) + EPS)
    vhat = vc * rstd
    return vhat * g + b, vhat, rstd


def _gelu(x):
    return 0.5 * x * (1.0 + lax.erf(x * INV_SQRT2))


def _gelu_grad(x):
    return 0.5 * (1.0 + lax.erf(x * INV_SQRT2)) + x * jnp.exp(-0.5 * x * x) * INV_SQRT_2PI


def _rope(x, cos, sin):
    x1, x2 = x[:, :QK_ROPE // 2], x[:, QK_ROPE // 2:]
    return jnp.concatenate([x1 * cos - x2 * sin, x2 * cos + x1 * sin], axis=-1)


def _gate_mask():
    row = lax.broadcasted_iota(jnp.int32, (GMLP_BLOCK, GMLP_BLOCK), 0)
    col = lax.broadcasted_iota(jnp.int32, (GMLP_BLOCK, GMLP_BLOCK), 1)
    return (col < CHUNK) | (row >= CHUNK)


def _att_mask(q0, tq, t):
    q = q0 + lax.broadcasted_iota(jnp.int32, (tq, t), 0)
    k = lax.broadcasted_iota(jnp.int32, (tq, t), 1)
    return jnp.right_shift(k, 6) <= jnp.right_shift(q, 6)


def _res(shape, imap=None):
    zeros = (0,) * len(shape)
    return pl.BlockSpec(shape, imap or (lambda i: zeros), pipeline_mode=pl.Buffered(1))


def _const(shape):
    zeros = (0,) * len(shape)
    return pl.BlockSpec(shape, lambda i: zeros)


def _row(d, tm=TM):
    return pl.BlockSpec((tm, d), lambda i: (i, 0))


def _heads(d):
    return pl.BlockSpec((B_HEADS, TM, d), lambda i: (0, i, 0))


def _sds(shape, dt):
    return jax.ShapeDtypeStruct(shape, dt)


def _acc(ref, val):
    @pl.when(pl.program_id(0) == 0)
    def _():
        ref[...] = jnp.zeros_like(ref)
    ref[...] += val


def _my_place():
    x, y, c = lax.axis_index("x"), lax.axis_index("y"), lax.axis_index("c")
    return x, y, c, 4 * x + 2 * y + c


def _peer(x, y, c, k):
    px = 1 - x if k & 4 else x
    py = 1 - y if k & 2 else y
    pc = 1 - c if k & 1 else c
    return (px, py, pc), 4 * px + 2 * py + pc


CHIPS = (2, 4, 6)


def _gather_copy(outs, send_sems, recv_sems, a, k, block, to, src=None):
    rows = outs[a].at[pl.ds(block, 1)]
    return pltpu.make_async_remote_copy(
        src_ref=rows if src is None else src, dst_ref=rows, send_sem=send_sems.at[a, k], recv_sem=recv_sems.at[a, k],
        device_id=to, device_id_type=MESH)


def _gather_start(srcs, outs, sems):
    send_sems, recv_sems, local_sems = sems
    x, y, c, me = _my_place()
    for a in range(len(srcs)):
        pltpu.make_async_copy(srcs[a], outs[a].at[pl.ds(me, 1)], local_sems.at[a]).start()
        _gather_copy(outs, send_sems, recv_sems, a, 0, me, _peer(x, y, c, 1)[0], src=srcs[a]).start()
        for j, k in enumerate(CHIPS):
            _gather_copy(outs, send_sems, recv_sems, a, 1 + j, me, _peer(x, y, c, k)[0], src=srcs[a]).start()


def _gather_finish(srcs, outs, sems):
    send_sems, recv_sems, local_sems = sems
    x, y, c, me = _my_place()
    sib, sib_i = _peer(x, y, c, 1)
    n = len(srcs)
    for j, k in enumerate(CHIPS):
        for a in range(n):
            block = _peer(x, y, c, k)[1]
            _gather_copy(outs, send_sems, recv_sems, a, 1 + j, block, sib).wait_recv()
            _gather_copy(outs, send_sems, recv_sems, a, 4 + j, block, sib).start()
    for a in range(n):
        _gather_copy(outs, send_sems, recv_sems, a, 0, sib_i, sib).wait_recv()
        for j, k in enumerate(CHIPS):
            _gather_copy(outs, send_sems, recv_sems, a, 4 + j, _peer(x, y, c, k ^ 1)[1], sib).wait_recv()
    for a in range(n):
        for k in range(7):
            _gather_copy(outs, send_sems, recv_sems, a, k, me, sib, src=srcs[a] if k < 4 else None).wait_send()
        pltpu.make_async_copy(srcs[a], outs[a].at[pl.ds(me, 1)], local_sems.at[a]).wait()


def _gather_sems(n):
    return [pltpu.SemaphoreType.DMA((n, 7)), pltpu.SemaphoreType.DMA((n, 7)), pltpu.SemaphoreType.DMA((n,))]


class _Comm:
    def __init__(self, args, out_shape, scratch, start, finish):
        self.args, self.out_shape, self.scratch, self.start, self.finish = args, out_shape, scratch, start, finish


def _gather_comm(shards):
    return _Comm(list(shards), [_sds((N_DEV,) + s.shape[1:], s.dtype) for s in shards], _gather_sems(len(shards)),
                 _gather_start, _gather_finish)


def _direct_copies(ins, outs, sems, wait, from_block):
    send_sems, recv_sems, local_sems = sems
    x, y, c, me = _my_place()
    for a in range(len(ins)):
        src = ins[a].at[pl.ds(me, 1)] if from_block[a] else ins[a]
        local = pltpu.make_async_copy(src, outs[a].at[pl.ds(me, 1)], local_sems.at[a])
        local.wait() if wait else local.start()
        for k in range(1, N_DEV):
            to, to_i = _peer(x, y, c, k)
            cp = pltpu.make_async_remote_copy(
                src_ref=ins[a].at[pl.ds(to_i, 1)] if from_block[a] else ins[a], dst_ref=outs[a].at[pl.ds(me, 1)],
                send_sem=send_sems.at[a, k - 1], recv_sem=recv_sems.at[a, k - 1], device_id=to, device_id_type=MESH)
            cp.wait() if wait else cp.start()


def _exchange_comm(grads=(), parts=()):
    ins = list(grads) + list(parts)
    from_block = [True] * len(grads) + [False] * len(parts)
    out_shape = [_sds(g.shape, g.dtype) for g in grads] + [_sds((N_DEV,) + p.shape[1:], p.dtype) for p in parts]

    def start(ins_, outs_, sems_):
        _direct_copies(ins_, outs_, sems_, False, from_block)

    def finish(ins_, outs_, sems_):
        _direct_copies(ins_, outs_, sems_, True, from_block)

    return _Comm(ins, out_shape, _gather_sems(len(ins)), start, finish)


def _call(name, body, grid, in_specs, out_specs, out_shape, args, scratch=(), comm=None):
    params = pltpu.CompilerParams(dimension_semantics=("arbitrary",) * len(grid), vmem_limit_bytes=VMEM_LIMIT)
    if comm is None:
        outs = pl.pallas_call(body, name=name, grid=grid, in_specs=list(in_specs), out_specs=list(out_specs),
                              out_shape=list(out_shape), scratch_shapes=list(scratch), compiler_params=params)(*args)
        return list(outs), []
    ni, nci, no, nco, ns = len(in_specs), len(comm.args), len(out_specs), len(comm.out_shape), len(scratch)

    def carrying(*refs):
        ins, refs = refs[:ni], refs[ni:]
        cin, refs = refs[:nci], refs[nci:]
        outs, refs = refs[:no], refs[no:]
        cout, refs = refs[:nco], refs[nco:]
        scr, csems = refs[:ns], refs[ns:]
        ids = [pl.program_id(ax) for ax in range(len(grid))]
        first, last = ids[0] == 0, ids[0] == grid[0] - 1
        for ax in range(1, len(grid)):
            first, last = first & (ids[ax] == 0), last & (ids[ax] == grid[ax] - 1)

        @pl.when(first)
        def _():
            comm.start(cin, cout, csems)

        body(*ins, *outs, *scr)

        @pl.when(last)
        def _():
            comm.finish(cin, cout, csems)

    outs = pl.pallas_call(
        carrying, name=name, grid=grid, in_specs=list(in_specs) + [ANY] * nci, out_specs=list(out_specs) + [ANY] * nco,
        out_shape=list(out_shape) + list(comm.out_shape), scratch_shapes=list(scratch) + list(comm.scratch),
        compiler_params=params)(*args, *comm.args)
    return list(outs[:no]), list(outs[no:])


def _comm_only(name, comm):
    def body(*refs):
        nci, nco = len(comm.args), len(comm.out_shape)
        cin, cout, csems = refs[:nci], refs[nci:nci + nco], refs[nci + nco:]
        comm.start(cin, cout, csems)
        comm.finish(cin, cout, csems)

    return pl.pallas_call(body, name=name, in_specs=[ANY] * len(comm.args), out_specs=[ANY] * len(comm.out_shape),
                          out_shape=list(comm.out_shape), scratch_shapes=list(comm.scratch))(*comm.args)


def _gather_first(first, later):
    nf, nl = len(first), len(later)
    dts = [BF16] * (nf - 2) + [F32, F32]

    def body(*refs):
        ins, refs = refs[:nf + nl], refs[nf + nl:]
        outs, refs = refs[:nf], refs[nf:]
        casts, refs = refs[:nl], refs[nl:]
        stage, sems = refs[:nf], refs[nf:]
        for a in range(nf):
            stage[a][...] = ins[a][...].astype(dts[a])
        _gather_start(stage, outs, sems)
        for a in range(nl):
            casts[a][...] = ins[nf + a][...].astype(BF16)
        _gather_finish(stage, outs, sems)

    res = pl.pallas_call(
        body, name="gather_first",
        in_specs=[VMEM] * (nf + nl), out_specs=[ANY] * nf + [VMEM] * nl,
        out_shape=[_sds((N_DEV,) + s.shape[1:], dt) for s, dt in zip(first, dts)]
        + [_sds(s.shape, BF16) for s in later],
        scratch_shapes=[pltpu.VMEM(s.shape, dt) for s, dt in zip(first, dts)] + _gather_sems(nf),
        compiler_params=pltpu.CompilerParams(vmem_limit_bytes=VMEM_LIMIT),
    )(*first, *later)
    return list(res[:nf]), list(res[nf:])


def _a_mix_fwd(x, g, w_in, ln_g, ln_b, w_s, b_st, w_out, comm=None):
    t = x.shape[0]
    nblk = TM // GMLP_BLOCK

    def body(x_ref, g_ref, win_ref, lng_ref, lnb_ref, ws_ref, bst_ref, wout_ref, h_ref, z_ref, gated_scr):
        xv = x_ref[...]
        hb = _rms_fwd(xv, g_ref[...])[0].astype(BF16)
        for d in range(N_DEV):
            z_ref[:, d * FF_SLOT:(d + 1) * FF_SLOT] = _dot(hb, win_ref[d])
        u = _gelu(z_ref[:, :GATE_DIM])
        vb = _ln_fwd(_gelu(z_ref[:, GATE_DIM:]), lng_ref[...], lnb_ref[...])[0].astype(BF16)
        mask = _gate_mask()
        for gi in range(A_GROUPS):
            wm = jnp.where(mask, ws_ref[gi], 0.0).astype(BF16)
            bias = bst_ref[:, gi:gi + 1]
            cs = slice(gi * A_GROUP_DIM, (gi + 1) * A_GROUP_DIM)
            for n in range(nblk):
                rs = slice(n * GMLP_BLOCK, (n + 1) * GMLP_BLOCK)
                sv = _dot(wm, vb[rs, cs]) + bias
                gated_scr[rs, cs] = (u[rs, cs] * sv).astype(BF16)
        h_ref[...] = xv + _dot(gated_scr[...], wout_ref[...])

    return _call(
        "a_mix_fwd", body, (t // TM,),
        [_row(D_MODEL), _res((1, D_MODEL)), _res((N_DEV, D_MODEL, FF_SLOT)), _res((1, GATE_DIM)),
         _res((1, GATE_DIM)), _res((A_GROUPS, GMLP_BLOCK, GMLP_BLOCK)), _res((GMLP_BLOCK, A_GROUPS)),
         _res((GATE_DIM, D_MODEL))],
        [_row(D_MODEL), _row(2 * GATE_DIM)],
        [_sds((t, D_MODEL), F32), _sds((t, 2 * GATE_DIM), F32)],
        (x, g, w_in, ln_g, ln_b, w_s, b_st, w_out), scratch=[pltpu.VMEM((TM, GATE_DIM), BF16)], comm=comm)


MLP_W_SPECS = (_res((N_DEV, D_MODEL, FF_SLOT)), _res((N_DEV, FF_SLOT, D_MODEL)))


def _mlp_fwd(h, g, w1, w2, layer, comm=None):
    t = h.shape[0]

    def body(h_ref, g_ref, w1_ref, w2_ref, o_ref, a_ref):
        hv = h_ref[...]
        hb = _rms_fwd(hv, g_ref[...])[0].astype(BF16)
        o_ref[...] = hv
        for d in range(N_DEV):
            a = _dot(hb, w1_ref[d])
            a_ref[:, d * FF_SLOT:(d + 1) * FF_SLOT] = a
            r = jnp.maximum(a, 0.0)
            o_ref[...] += _dot((r * r).astype(BF16), w2_ref[d])

    return _call(
        f"mlp_fwd_{layer}", body, (t // TM,), [_row(D_MODEL), _res((1, D_MODEL)), *MLP_W_SPECS],
        [_row(D_MODEL), _row(D_FF)], [_sds((t, D_MODEL), F32), _sds((t, D_FF), F32)], (h, g, w1, w2), comm=comm)


KVQ_W_SPECS = (_res((1, D_MODEL)), _res((D_MODEL, KV_LORA + QK_ROPE)), _res((1, KV_LORA)),
               _res((B_HEADS, KV_LORA, QK_NOPE + V_HEAD)), _res((1, D_MODEL)), _res((D_MODEL, Q_LORA)),
               _res((1, Q_LORA)), _res((B_HEADS, Q_LORA, QK_NOPE + QK_ROPE)))


def _kvq_fwd(h, pos, inv_freq, kvq_w):
    t = h.shape[0]
    half = QK_ROPE // 2

    def body(h_ref, pos_ref, invf_ref, srcg_ref, wkva_ref, kvag_ref, wkvb_ref, mixg_ref, wqa_ref, qg_ref, wqb_ref,
             ckv_ref, kn_ref, v_ref, kpe_ref, cqpre_ref, q_ref, cos_ref, sin_ref):
        hv = h_ref[...]
        xhat = hv * lax.rsqrt(jnp.mean(hv * hv, axis=-1, keepdims=True) + EPS)
        ang = pos_ref[...].astype(F32) * invf_ref[...]
        cos, sin = jnp.cos(ang), jnp.sin(ang)
        cos_ref[...] = cos
        sin_ref[...] = sin
        ckv = _dot((xhat * srcg_ref[...]).astype(BF16), wkva_ref[...])
        ckv_ref[...] = ckv
        cb = _rms_fwd(ckv[:, :KV_LORA], kvag_ref[...])[0].astype(BF16)
        kpe_ref[...] = _rope(ckv[:, KV_LORA:], cos, sin).astype(BF16)
        for hd in range(B_HEADS):
            kv = _dot(cb, wkvb_ref[hd])
            kn_ref[hd] = kv[:, :QK_NOPE].astype(BF16)
            v_ref[hd] = kv[:, QK_NOPE:].astype(BF16)
        cqpre = _dot((xhat * mixg_ref[...]).astype(BF16), wqa_ref[...])
        cqpre_ref[...] = cqpre
        cqb = _rms_fwd(cqpre, qg_ref[...])[0].astype(BF16)
        for hd in range(B_HEADS):
            q = _dot(cqb, wqb_ref[hd])
            q_ref[hd, :, 0:QK_NOPE] = q[:, :QK_NOPE].astype(BF16)
            q_ref[hd, :, QK_NOPE:] = _rope(q[:, QK_NOPE:], cos, sin).astype(BF16)

    return _call(
        "kvq_fwd", body, (t // TM,), [_row(D_MODEL), _row(1), _res((1, half)), *KVQ_W_SPECS],
        [_row(KV_LORA + QK_ROPE), _heads(QK_NOPE), _heads(V_HEAD), _row(QK_ROPE), _row(Q_LORA),
         _heads(QK_NOPE + QK_ROPE), _row(half), _row(half)],
        [_sds((t, KV_LORA + QK_ROPE), F32), _sds((B_HEADS, t, QK_NOPE), BF16), _sds((B_HEADS, t, V_HEAD), BF16),
         _sds((t, QK_ROPE), BF16), _sds((t, Q_LORA), F32), _sds((B_HEADS, t, QK_NOPE + QK_ROPE), BF16),
         _sds((t, half), F32), _sds((t, half), F32)],
        (h, pos, inv_freq, *kvq_w))[0]


def _softmax_rows(qn, qp, kn_ref, kpe_ref, k):
    past, upto = k * TM, (k + 1) * TM
    s = (_dot_nt(qn, kn_ref[0:upto, :]) + _dot_nt(qp, kpe_ref[0:upto, :])) * ATT_SCALE
    own = jnp.where(_att_mask(0, TM, TM), s[:, past:], jnp.finfo(F32).min)
    s = own if k == 0 else jnp.concatenate([s[:, :past], own], axis=1)
    e = jnp.exp(s - jnp.max(s, axis=-1, keepdims=True))
    return e * (1.0 / jnp.sum(e, axis=-1, keepdims=True))


def _for_my_tile(i, nq, fn):
    for k in range(nq):
        @pl.when(i == k)
        def _(k=k):
            fn(k)


def _attn_fwd(h, q, kn, kpe, v, w_o, comm=None):
    t = h.shape[0]
    nq = t // TM

    def body(h_ref, q_ref, kn_ref, kpe_ref, v_ref, wo_ref, o_ref, att_ref):
        i, hd = pl.program_id(0), pl.program_id(1)

        @pl.when(hd == 0)
        def _():
            o_ref[...] = h_ref[...]

        def tile(k):
            p = _softmax_rows(q_ref[:, 0:QK_NOPE], q_ref[:, QK_NOPE:], kn_ref, kpe_ref, k)
            ob = _dot(p.astype(BF16), v_ref[0:(k + 1) * TM, :]).astype(BF16)
            att_ref[...] = ob
            o_ref[...] += _dot(ob, wo_ref[...])

        _for_my_tile(i, nq, tile)

    def per_head(rows, d, tiled):
        return pl.BlockSpec((None, rows, d), (lambda i, hd: (hd, i, 0)) if tiled else (lambda i, hd: (hd, 0, 0)))

    tile_spec = pl.BlockSpec((TM, D_MODEL), lambda i, hd: (i, 0))
    return _call(
        "attn_fwd", body, (nq, B_HEADS),
        [tile_spec, per_head(TM, QK_NOPE + QK_ROPE, True), per_head(t, QK_NOPE, False),
         pl.BlockSpec((t, QK_ROPE), lambda i, hd: (0, 0)), per_head(t, V_HEAD, False),
         per_head(V_HEAD, D_MODEL, False)],
        [tile_spec, per_head(TM, V_HEAD, True)], [_sds((t, D_MODEL), F32), _sds((B_HEADS, t, V_HEAD), BF16)],
        (h, q, kn, kpe, v, w_o), comm=comm)


def _loss_head(h, g, target):
    t = h.shape[0]

    def body(h_ref, g_ref, t_ref, loss_ref, dh_ref, dg_ref):
        y, xhat, rstd = _rms_fwd(h_ref[...], g_ref[...])
        err = y - t_ref[...]
        part = 0.5 * jnp.sum(jnp.mean(err * err, axis=-1, keepdims=True), axis=0, keepdims=True)
        dx, dg = _rms_bwd(err * (1.0 / D_MODEL), xhat, rstd, g_ref[...])
        dh_ref[...] = dx
        _acc(dg_ref, dg)
        _acc(loss_ref, part)

    return _call(
        "loss_head", body, (t // TM,), [_row(D_MODEL), _res((1, D_MODEL)), _row(D_MODEL)],
        [_const((1, 1)), _row(D_MODEL), _const((1, D_MODEL))],
        [_sds((1, 1), F32), _sds((t, D_MODEL), F32), _sds((1, D_MODEL), F32)], (h, g, target))[0]


def _mlp_bwd(h, a, dho, g, w1, w2, layer, comm=None):
    t = h.shape[0]

    def body(h_ref, a_ref, dho_ref, g_ref, w1_ref, w2_ref, dhi_ref, dg_ref, hn_ref, f_ref, da_ref):
        gv = g_ref[...]
        y, xhat, rstd = _rms_fwd(h_ref[...], gv)
        hn_ref[...] = y.astype(BF16)
        dho_v = dho_ref[...]
        dhob = dho_v.astype(BF16)
        dhn = jnp.zeros((TM, D_MODEL), F32)
        for d in range(N_DEV):
            cs = slice(d * FF_SLOT, (d + 1) * FF_SLOT)
            r = jnp.maximum(a_ref[:, cs], 0.0)
            f_ref[:, cs] = (r * r).astype(BF16)
            da = (_dot_nt(dhob, w2_ref[d]) * (2.0 * r)).astype(BF16)
            da_ref[:, cs] = da
            dhn = dhn + _dot_nt(da, w1_ref[d])
        dx, dg = _rms_bwd(dhn, xhat, rstd, gv)
        dhi_ref[...] = dho_v + dx
        _acc(dg_ref, dg)

    return _call(
        f"mlp_bwd_{layer}", body, (t // TM,),
        [_row(D_MODEL), _row(D_FF), _row(D_MODEL), _res((1, D_MODEL)), *MLP_W_SPECS],
        [_row(D_MODEL), _const((1, D_MODEL)), _row(D_MODEL), _row(D_FF), _row(D_FF)],
        [_sds((t, D_MODEL), F32), _sds((1, D_MODEL), F32), _sds((t, D_MODEL), BF16), _sds((t, D_FF), BF16),
         _sds((t, D_FF), BF16)],
        (h, a, dho, g, w1, w2), comm=comm)


def _attn_bwd(dh, q, kn, kpe, v, w_o, cos, sin, comm=None):
    t = dh.shape[0]
    half = QK_ROPE // 2

    def body(dh_ref, q_ref, kn_ref, kpe_ref, v_ref, wo_ref, cos_ref, sin_ref, dq_ref, dkn_ref, dv_ref, dkpe_ref):
        hd, i = pl.program_id(0), pl.program_id(1)

        @pl.when(i == 0)
        def _():
            dkn_ref[...] = jnp.zeros_like(dkn_ref)
            dv_ref[...] = jnp.zeros_like(dv_ref)

        @pl.when((i == 0) & (hd == 0))
        def _():
            dkpe_ref[...] = jnp.zeros_like(dkpe_ref)

        def tile(k):
            keys = slice(0, (k + 1) * TM)
            qn, qp = q_ref[:, 0:QK_NOPE], q_ref[:, QK_NOPE:]
            do = _dot_nt(dh_ref[...].astype(BF16), wo_ref[...]).astype(BF16)
            p = _softmax_rows(qn, qp, kn_ref, kpe_ref, k)
            dp = _dot_nt(do, v_ref[keys, :])
            ds = (p * (dp - jnp.sum(p * dp, axis=-1, keepdims=True)) * ATT_SCALE).astype(BF16)
            dq_ref[:, 0:QK_NOPE] = _dot(ds, kn_ref[keys, :]).astype(BF16)
            dq_ref[:, QK_NOPE:] = _rope(_dot(ds, kpe_ref[keys, :]), cos_ref[...], -sin_ref[...]).astype(BF16)
            dkn_ref[keys, :] += _dot_tn(ds, qn)
            dv_ref[keys, :] += _dot_tn(p.astype(BF16), do)
            dkpe_ref[keys, :] += _dot_tn(ds, qp)

        _for_my_tile(i, t // TM, tile)

    def per_head(rows, d, tiled):
        return pl.BlockSpec((None, rows, d), (lambda hd, i: (hd, i, 0)) if tiled else (lambda hd, i: (hd, 0, 0)))

    def tile(d):
        return pl.BlockSpec((TM, d), lambda hd, i: (i, 0))

    return _call(
        "attn_bwd", body, (B_HEADS, t // TM),
        [tile(D_MODEL), per_head(TM, QK_NOPE + QK_ROPE, True), per_head(t, QK_NOPE, False),
         pl.BlockSpec((t, QK_ROPE), lambda hd, i: (0, 0)), per_head(t, V_HEAD, False),
         per_head(V_HEAD, D_MODEL, False), tile(half), tile(half)],
        [per_head(TM, QK_NOPE + QK_ROPE, True), per_head(t, QK_NOPE, False), per_head(t, V_HEAD, False),
         pl.BlockSpec((t, QK_ROPE), lambda hd, i: (0, 0))],
        [_sds((B_HEADS, t, QK_NOPE + QK_ROPE), BF16), _sds((B_HEADS, t, QK_NOPE), F32),
         _sds((B_HEADS, t, V_HEAD), F32), _sds((t, QK_ROPE), F32)],
        (dh, q, kn, kpe, v, w_o, cos, sin), comm=comm)


def _kvq_bwd(h, dh, ckv, cqpre, dq, dkn, dv, dkpe, cos, sin, kvq_w):
    t = h.shape[0]
    half = QK_ROPE // 2

    def body(h_ref, dh_ref, ckv_ref, cqpre_ref, dq_ref, dkn_ref, dv_ref, dkpe_ref, cos_ref, sin_ref,
             srcg_ref, wkva_ref, kvag_ref, wkvb_ref, mixg_ref, wqa_ref, qg_ref, wqb_ref,
             dhi_ref, hq_ref, hk_ref, cq_ref, dcqpre_ref, c_ref, dkv_ref, dckv_ref,
             dmixg_ref, dsrcg_ref, dqg_ref, dkvag_ref):
        hv = h_ref[...]
        rstd = lax.rsqrt(jnp.mean(hv * hv, axis=-1, keepdims=True) + EPS)
        xhat = hv * rstd
        mixg, srcg, qg, kvag = mixg_ref[...], srcg_ref[...], qg_ref[...], kvag_ref[...]
        hq_ref[...] = (xhat * mixg).astype(BF16)
        hk_ref[...] = (xhat * srcg).astype(BF16)
        cq, cqhat, crstd = _rms_fwd(cqpre_ref[...], qg)
        cq_ref[...] = cq.astype(BF16)
        dcq = jnp.zeros((TM, Q_LORA), F32)
        for hd in range(B_HEADS):
            dcq = dcq + _dot_nt(dq_ref[hd], wqb_ref[hd])
        dcqpre, dqg = _rms_bwd(dcq, cqhat, crstd, qg)
        dcqpre_b = dcqpre.astype(BF16)
        dcqpre_ref[...] = dcqpre_b
        dxq, dmixg = _rms_bwd(_dot_nt(dcqpre_b, wqa_ref[...]), xhat, rstd, mixg)
        ckv = ckv_ref[...]
        c, chat, krstd = _rms_fwd(ckv[:, :KV_LORA], kvag)
        c_ref[...] = c.astype(BF16)
        dc = jnp.zeros((TM, KV_LORA), F32)
        for hd in range(B_HEADS):
            dkv = jnp.concatenate([dkn_ref[hd], dv_ref[hd]], axis=-1).astype(BF16)
            dkv_ref[hd] = dkv
            dc = dc + _dot_nt(dkv, wkvb_ref[hd])
        dlat, dkvag = _rms_bwd(dc, chat, krstd, kvag)
        dpe = _rope(dkpe_ref[...], cos_ref[...], -sin_ref[...])
        dckv_b = jnp.concatenate([dlat, dpe], axis=-1).astype(BF16)
        dckv_ref[...] = dckv_b
        dxk, dsrcg = _rms_bwd(_dot_nt(dckv_b, wkva_ref[...]), xhat, rstd, srcg)
        dhi_ref[...] = dh_ref[...] + dxq + dxk
        _acc(dmixg_ref, dmixg)
        _acc(dsrcg_ref, dsrcg)
        _acc(dqg_ref, dqg)
        _acc(dkvag_ref, dkvag)

    return _call(
        "kvq_bwd", body, (t // TM,),
        [_row(D_MODEL), _row(D_MODEL), _row(KV_LORA + QK_ROPE), _row(Q_LORA), _heads(QK_NOPE + QK_ROPE),
         _heads(QK_NOPE), _heads(V_HEAD), _row(QK_ROPE), _row(half), _row(half), *KVQ_W_SPECS],
        [_row(D_MODEL), _row(D_MODEL), _row(D_MODEL), _row(Q_LORA), _row(Q_LORA), _row(KV_LORA),
         _heads(QK_NOPE + V_HEAD), _row(KV_LORA + QK_ROPE),
         _const((1, D_MODEL)), _const((1, D_MODEL)), _const((1, Q_LORA)), _const((1, KV_LORA))],
        [_sds((t, D_MODEL), F32), _sds((t, D_MODEL), BF16), _sds((t, D_MODEL), BF16), _sds((t, Q_LORA), BF16),
         _sds((t, Q_LORA), BF16), _sds((t, KV_LORA), BF16), _sds((B_HEADS, t, QK_NOPE + V_HEAD), BF16),
         _sds((t, KV_LORA + QK_ROPE), BF16),
         _sds((1, D_MODEL), F32), _sds((1, D_MODEL), F32), _sds((1, Q_LORA), F32), _sds((1, KV_LORA), F32)],
        (h, dh, ckv, cqpre, dq, dkn, dv, dkpe, cos, sin, *kvq_w))[0]


def _a_mix_bwd(x, z, dh, g, w_in, ln_g, ln_b, w_s, b_st, w_out, comm=None):
    t = x.shape[0]
    tm = TM_GATE
    nblk = tm // GMLP_BLOCK

    def body(x_ref, z_ref, dh_ref, g_ref, win_ref, lng_ref, lnb_ref, ws_ref, bst_ref, wout_ref,
             dx_ref, hn_ref, gated_ref, dz_ref, dg_ref, dlng_ref, dlnb_ref, dws_ref, dbs_ref, du_scr, dvn_scr):
        @pl.when(pl.program_id(0) == 0)
        def _():
            dws_ref[...] = jnp.zeros_like(dws_ref)
            dbs_ref[...] = jnp.zeros_like(dbs_ref)

        gv, lng = g_ref[...], lng_ref[...]
        y, xhat, rstd = _rms_fwd(x_ref[...], gv)
        hn_ref[...] = y.astype(BF16)
        dhv = dh_ref[...]
        dgated = _dot_nt(dhv.astype(BF16), wout_ref[...])
        u = _gelu(z_ref[:, :GATE_DIM])
        vn, vhat, lrstd = _ln_fwd(_gelu(z_ref[:, GATE_DIM:]), lng, lnb_ref[...])
        vb = vn.astype(BF16)
        mask = _gate_mask()
        for gi in range(A_GROUPS):
            wm = jnp.where(mask, ws_ref[gi], 0.0).astype(BF16)
            bias = bst_ref[:, gi:gi + 1]
            cs = slice(gi * A_GROUP_DIM, (gi + 1) * A_GROUP_DIM)
            dws = jnp.zeros((GMLP_BLOCK, GMLP_BLOCK), F32)
            dbs = jnp.zeros((GMLP_BLOCK, 1), F32)
            for n in range(nblk):
                rs = slice(n * GMLP_BLOCK, (n + 1) * GMLP_BLOCK)
                sv = _dot(wm, vb[rs, cs]) + bias
                gated_ref[rs, cs] = (u[rs, cs] * sv).astype(BF16)
                du_scr[rs, cs] = dgated[rs, cs] * sv
                dsv = dgated[rs, cs] * u[rs, cs]
                dsvb = dsv.astype(BF16)
                dws = dws + _dot_nt(dsvb, vb[rs, cs])
                dbs = dbs + jnp.sum(dsv, axis=-1, keepdims=True)
                dvn_scr[rs, cs] = _dot_tn(wm, dsvb)
            dws_ref[gi] += jnp.where(mask, dws, 0.0)
            dbs_ref[gi] += dbs
        dvn = dvn_scr[...]
        dvhat = dvn * lng
        dv = lrstd * (dvhat - jnp.mean(dvhat, axis=-1, keepdims=True)
                      - vhat * jnp.mean(dvhat * vhat, axis=-1, keepdims=True))
        dz_ref[:, :GATE_DIM] = (du_scr[...] * _gelu_grad(z_ref[:, :GATE_DIM])).astype(BF16)
        dz_ref[:, GATE_DIM:] = (dv * _gelu_grad(z_ref[:, GATE_DIM:])).astype(BF16)
        dhn = jnp.zeros((tm, D_MODEL), F32)
        for d in range(N_DEV):
            dhn = dhn + _dot_nt(dz_ref[:, d * FF_SLOT:(d + 1) * FF_SLOT], win_ref[d])
        dx, dg = _rms_bwd(dhn, xhat, rstd, gv)
        dx_ref[...] = dhv + dx
        _acc(dg_ref, dg)
        _acc(dlng_ref, jnp.sum(dvn * vhat, axis=0, keepdims=True))
        _acc(dlnb_ref, jnp.sum(dvn, axis=0, keepdims=True))

    return _call(
        "a_mix_bwd", body, (t // tm,),
        [_row(D_MODEL, tm), _row(2 * GATE_DIM, tm), _row(D_MODEL, tm), _res((1, D_MODEL)),
         _res((N_DEV, D_MODEL, FF_SLOT)), _res((1, GATE_DIM)), _res((1, GATE_DIM)),
         _res((A_GROUPS, GMLP_BLOCK, GMLP_BLOCK)), _res((GMLP_BLOCK, A_GROUPS)), _res((GATE_DIM, D_MODEL))],
        [_row(D_MODEL, tm), _row(D_MODEL, tm), _row(GATE_DIM, tm), _row(2 * GATE_DIM, tm),
         _const((1, D_MODEL)), _const((1, GATE_DIM)), _const((1, GATE_DIM)),
         _const((A_GROUPS, GMLP_BLOCK, GMLP_BLOCK)), _const((A_GROUPS, GMLP_BLOCK, 1))],
        [_sds((t, D_MODEL), F32), _sds((t, D_MODEL), BF16), _sds((t, GATE_DIM), BF16),
         _sds((t, 2 * GATE_DIM), BF16), _sds((1, D_MODEL), F32), _sds((1, GATE_DIM), F32),
         _sds((1, GATE_DIM), F32), _sds((A_GROUPS, GMLP_BLOCK, GMLP_BLOCK), F32),
         _sds((A_GROUPS, GMLP_BLOCK, 1), F32)],
        (x, z, dh, g, w_in, ln_g, ln_b, w_s, b_st, w_out),
        scratch=[pltpu.VMEM((tm, GATE_DIM), F32), pltpu.VMEM((tm, GATE_DIM), F32)], comm=comm)


def _wgrad(name, a, b, a_spec, b_spec, m, n):
    def body(a_ref, b_ref, o_ref):
        o_ref[0] = _dot_tn(a_ref[...].astype(BF16), b_ref[...].astype(BF16)).astype(BF16)

    return _call(name, body, (N_DEV,), [a_spec, b_spec], [pl.BlockSpec((1, m, n), lambda d: (d, 0, 0))],
                 [_sds((N_DEV, m, n), BF16)], (a, b))[0][0]


def _full(t, d):
    return pl.BlockSpec((t, d), lambda i: (0, 0), pipeline_mode=pl.Buffered(1))


def _cols(t, d):
    return pl.BlockSpec((t, d), lambda i: (0, i))


def _head(t, d):
    return pl.BlockSpec((None, t, d), lambda i: (i, 0, 0))


def _local_step(x, pos, target, inv_freq, wg, sm, shards=None):
    t = x.shape[0]
    wg = dict(wg)
    dist = shards is not None
    mix_g = [sm["norm_mix_g"][l:l + 1] for l in range(2)]
    mlp_g = [sm["norm_mlp_g"][l:l + 1] for l in range(2)]

    def gather(names):
        return _gather_comm([shards[k] for k in names]) if dist else None

    def send(grads):
        return _exchange_comm(grads=grads) if dist else None

    def a_args():
        return (wg["a_w_in"], wg["a_ln_v_g"], wg["a_ln_v_b"], sm["a_w_s"], sm["a_b_st"], wg["a_w_out"])

    def kvq_w():
        return (sm["kv_src_norm_g"], wg["kv_w_a"], sm["kv_a_norm_g"], wg["kv_w_b"], mix_g[1], wg["b_w_q_a"],
                sm["b_q_norm_g"], wg["b_w_q_b"])

    names = ("mlp_w1_0", "mlp_w2_0")
    (h1, z), got = _a_mix_fwd(x, mix_g[0], *a_args(), comm=gather(names))
    wg.update(zip(names, got))
    names = ("kv_w_a", "kv_w_b", "b_w_q_a", "b_w_q_b", "b_w_o")
    (h2, a0), got = _mlp_fwd(h1, mlp_g[0], wg["mlp_w1_0"], wg["mlp_w2_0"], 0, comm=gather(names))
    wg.update(zip(names, got))
    if dist:
        wg["b_w_q_a"] = wg["b_w_q_a"].reshape(D_MODEL, Q_LORA)
        wg["kv_w_a"] = wg["kv_w_a"].reshape(D_MODEL, KV_LORA + QK_ROPE)
    ckv, kn, v, kpe, cqpre, q, cos, sin = _kvq_fwd(h2, pos, inv_freq, kvq_w())
    names = ("mlp_w1_1", "mlp_w2_1")
    (h3, att), got = _attn_fwd(h2, q, kn, kpe, v, wg["b_w_o"], comm=gather(names))
    wg.update(zip(names, got))
    (h4, a1), _ = _mlp_fwd(h3, mlp_g[1], wg["mlp_w1_1"], wg["mlp_w2_1"], 1)
    loss, dh4, d_final_g = _loss_head(h4, sm["final_norm_g"], target)

    g = {}
    (dh3, d_mlp_g1, hn, f, da), _ = _mlp_bwd(h3, a1, dh4, mlp_g[1], wg["mlp_w1_1"], wg["mlp_w2_1"], 1)
    g["mlp_w1_1"] = _wgrad("wgrad_w1_1", hn, da, _full(t, D_MODEL), _cols(t, FF_SLOT), D_MODEL, FF_SLOT)
    g["mlp_w2_1"] = _wgrad("wgrad_w2_1", f, dh4, _cols(t, FF_SLOT), _full(t, D_MODEL), FF_SLOT, D_MODEL)
    g["b_w_o"] = _wgrad("wgrad_w_o", att, dh3, _head(t, V_HEAD), _full(t, D_MODEL), V_HEAD, D_MODEL)
    names = ("mlp_w1_1", "mlp_w2_1", "b_w_o")
    (dq, dkn, dv, dkpe), got = _attn_bwd(dh3, q, kn, kpe, v, wg["b_w_o"], cos, sin, comm=send([g[k] for k in names]))
    g.update(zip(names, got))
    (dh2, hq, hk, cq, dcqpre, c, dkv, dckv, d_mix_g1, d_src_g, d_q_g, d_kv_a_g) = _kvq_bwd(
        h2, dh3, ckv, cqpre, dq, dkn, dv, dkpe, cos, sin, kvq_w())
    g["b_w_q_a"] = _wgrad("wgrad_w_q_a", hq, dcqpre, _cols(t, D_MODEL // N_DEV), _full(t, Q_LORA),
                          D_MODEL // N_DEV, Q_LORA)
    g["b_w_q_b"] = _wgrad("wgrad_w_q_b", cq, dq, _full(t, Q_LORA), _head(t, QK_NOPE + QK_ROPE),
                          Q_LORA, QK_NOPE + QK_ROPE)
    g["kv_w_a"] = _wgrad("wgrad_kv_w_a", hk, dckv, _cols(t, D_MODEL // N_DEV), _full(t, KV_LORA + QK_ROPE),
                         D_MODEL // N_DEV, KV_LORA + QK_ROPE)
    g["kv_w_b"] = _wgrad("wgrad_kv_w_b", c, dkv, _full(t, KV_LORA), _head(t, QK_NOPE + V_HEAD),
                         KV_LORA, QK_NOPE + V_HEAD)
    names = ("b_w_q_a", "b_w_q_b", "kv_w_a", "kv_w_b")
    (dh1, d_mlp_g0, hn, f, da), got = _mlp_bwd(h1, a0, dh2, mlp_g[0], wg["mlp_w1_0"], wg["mlp_w2_0"], 0,
                                               comm=send([g[k] for k in names]))
    g.update(zip(names, got))
    g["mlp_w1_0"] = _wgrad("wgrad_w1_0", hn, da, _full(t, D_MODEL), _cols(t, FF_SLOT), D_MODEL, FF_SLOT)
    g["mlp_w2_0"] = _wgrad("wgrad_w2_0", f, dh2, _cols(t, FF_SLOT), _full(t, D_MODEL), FF_SLOT, D_MODEL)
    names = ("mlp_w1_0", "mlp_w2_0")
    (dx, hn, gated, dz, d_mix_g0, d_ln_g, d_ln_b, d_ws, d_bs), got = _a_mix_bwd(
        x, z, dh1, mix_g[0], *a_args(), comm=send([g[k] for k in names]))
    g.update(zip(names, got))
    g["a_w_in"] = _wgrad("wgrad_a_w_in", hn, dz, _full(t, D_MODEL), _cols(t, FF_SLOT), D_MODEL, FF_SLOT)
    g["a_w_out"] = _wgrad("wgrad_a_w_out", gated, dh1, _cols(t, GATE_DIM // N_DEV), _full(t, D_MODEL),
                          GATE_DIM // N_DEV, D_MODEL)
    small = {
        "norm_mix_g": jnp.concatenate([d_mix_g0, d_mix_g1], axis=0),
        "norm_mlp_g": jnp.concatenate([d_mlp_g0, d_mlp_g1], axis=0),
        "a_ln_v_g": d_ln_g.reshape(N_DEV, GATE_DIM // N_DEV),
        "a_ln_v_b": d_ln_b.reshape(N_DEV, GATE_DIM // N_DEV),
        "a_w_s": d_ws,
        "a_b_s": d_bs.reshape(A_GROUPS, GMLP_BLOCK),
        "b_q_norm_g": d_q_g,
        "kv_src_norm_g": d_src_g,
        "kv_a_norm_g": d_kv_a_g,
        "final_norm_g": d_final_g,
    }
    return loss, dx, g, small


def _adamw(w, g, m, v):
    m = ADAM_B1 * m + (1.0 - ADAM_B1) * g
    v = ADAM_B2 * v + (1.0 - ADAM_B2) * (g * g)
    m_hat = m / (1.0 - ADAM_B1 ** ADAM_STEP)
    v_hat = v / (1.0 - ADAM_B2 ** ADAM_STEP)
    return -ADAM_LR * (m_hat / (jnp.sqrt(v_hat) + ADAM_EPS) + ADAM_WD * w), m, v


def _sum_in_device_order(r_ref):
    g = r_ref[0].astype(F32)
    for j in range(1, N_DEV):
        g = g + r_ref[j].astype(F32)
    return g


def _adamw_sharded(name, recv, w, m, v, layer, into=None):
    layers, r, c = w.shape
    tr = math.gcd(r, 256)

    def body(r_ref, w_ref, m_ref, v_ref, *refs):
        g_ref, d_ref, nm_ref, nv_ref = refs[-4:]
        g = _sum_in_device_order(r_ref)
        g_ref[...] = g
        d_ref[...], nm_ref[...], nv_ref[...] = _adamw(w_ref[...], g, m_ref[...], v_ref[...])

    blk = pl.BlockSpec((None, tr, c), lambda i: (layer, i, 0))
    args, in_specs, aliases = [recv, w, m, v], [pl.BlockSpec((N_DEV, tr, c), lambda i: (0, i, 0)), blk, blk, blk], {}
    if into is not None:
        args += list(into)
        in_specs += [ANY] * 4
        aliases = {4 + i: i for i in range(4)}
    return pl.pallas_call(
        body, name=name, grid=(r // tr,), in_specs=in_specs, out_specs=[blk] * 4,
        out_shape=[_sds(w.shape, F32)] * 4, input_output_aliases=aliases,
        compiler_params=pltpu.CompilerParams(dimension_semantics=("arbitrary",), vmem_limit_bytes=VMEM_LIMIT),
    )(*args)


def _adamw_small(recvs, ws, ms, vs, own_row):
    n = len(recvs)

    def body(*refs):
        r_refs, w_refs, m_refs, v_refs = (refs[i * n:(i + 1) * n] for i in range(4))
        outs, scr = refs[4 * n:8 * n], refs[8 * n:]
        me = _my_place()[3]
        for a in range(n):
            g = _sum_in_device_order(r_refs[a])
            if own_row[a]:
                scr[0][...] = g
                g = scr[0][pl.ds(me, 1), :]
            g_ref, d_ref, nm_ref, nv_ref = outs[4 * a:4 * a + 4]
            g_ref[...] = g
            d_ref[...], nm_ref[...], nv_ref[...] = _adamw(w_refs[a][...], g, m_refs[a][...], v_refs[a][...])

    out_shape = []
    for w in ws:
        out_shape += [_sds(w.shape, F32)] * 4
    return pl.pallas_call(
        body, name="adamw_small", in_specs=[VMEM] * (4 * n), out_specs=[VMEM] * (4 * n), out_shape=out_shape,
        scratch_shapes=[pltpu.VMEM((N_DEV, GATE_DIM // N_DEV), F32)],
    )(*recvs, *ws, *ms, *vs)


BIG = ("a_w_in", "a_w_out", "b_w_q_a", "b_w_q_b", "b_w_o", "kv_w_a", "kv_w_b", "mlp_w1", "mlp_w2")
SMALL = ("norm_mix_g", "norm_mlp_g", "a_ln_v_g", "a_ln_v_b", "a_w_s", "a_b_s", "b_q_norm_g", "kv_src_norm_g",
         "kv_a_norm_g", "final_norm_g")
WEIGHTS = ("norm_mix_g", "norm_mlp_g", "a_w_in", "a_ln_v_g", "a_ln_v_b", "a_w_s", "a_b_s", "a_w_out", "b_w_q_a",
           "b_q_norm_g", "b_w_q_b", "b_w_o", "kv_src_norm_g", "kv_w_a", "kv_a_norm_g", "kv_w_b", "mlp_w1", "mlp_w2",
           "final_norm_g")


def _two_d(name, a):
    if name in ("a_w_s", "a_b_s"):
        return a.reshape(a.shape[1:])
    return a.reshape(1, -1) if a.ndim == 1 else a


def _three_d(a):
    return a if a.ndim == 3 else a.reshape((1,) + a.shape)


def kernel(x, positions, norm_mix_g, norm_mlp_g, a_w_in, a_ln_v_g, a_ln_v_b, a_w_s, a_b_s, a_w_out, b_w_q_a, b_q_norm_g, b_w_q_b, b_w_o, kv_src_norm_g, kv_w_a, kv_a_norm_g, kv_w_b, mlp_w1, mlp_w2, final_norm_g, loss_target, m_norm_mix_g, m_norm_mlp_g, m_a_w_in, m_a_ln_v_g, m_a_ln_v_b, m_a_w_s, m_a_b_s, m_a_w_out, m_b_w_q_a, m_b_q_norm_g, m_b_w_q_b, m_b_w_o, m_kv_src_norm_g, m_kv_w_a, m_kv_a_norm_g, m_kv_w_b, m_mlp_w1, m_mlp_w2, m_final_norm_g, v_norm_mix_g, v_norm_mlp_g, v_a_w_in, v_a_ln_v_g, v_a_ln_v_b, v_a_w_s, v_a_b_s, v_a_w_out, v_b_w_q_a, v_b_q_norm_g, v_b_w_q_b, v_b_w_o, v_kv_src_norm_g, v_kv_w_a, v_kv_a_norm_g, v_kv_w_b, v_mlp_w1, v_mlp_w2, v_final_norm_g):
    w = dict(norm_mix_g=norm_mix_g, norm_mlp_g=norm_mlp_g, a_w_in=a_w_in, a_ln_v_g=a_ln_v_g, a_ln_v_b=a_ln_v_b,
             a_w_s=a_w_s, a_b_s=a_b_s, a_w_out=a_w_out, b_w_q_a=b_w_q_a, b_q_norm_g=b_q_norm_g, b_w_q_b=b_w_q_b,
             b_w_o=b_w_o, kv_src_norm_g=kv_src_norm_g, kv_w_a=kv_w_a, kv_a_norm_g=kv_a_norm_g, kv_w_b=kv_w_b,
             mlp_w1=mlp_w1, mlp_w2=mlp_w2, final_norm_g=final_norm_g)
    m = dict(norm_mix_g=m_norm_mix_g, norm_mlp_g=m_norm_mlp_g, a_w_in=m_a_w_in, a_ln_v_g=m_a_ln_v_g,
             a_ln_v_b=m_a_ln_v_b, a_w_s=m_a_w_s, a_b_s=m_a_b_s, a_w_out=m_a_w_out, b_w_q_a=m_b_w_q_a,
             b_q_norm_g=m_b_q_norm_g, b_w_q_b=m_b_w_q_b, b_w_o=m_b_w_o, kv_src_norm_g=m_kv_src_norm_g,
             kv_w_a=m_kv_w_a, kv_a_norm_g=m_kv_a_norm_g, kv_w_b=m_kv_w_b, mlp_w1=m_mlp_w1, mlp_w2=m_mlp_w2,
             final_norm_g=m_final_norm_g)
    v = dict(norm_mix_g=v_norm_mix_g, norm_mlp_g=v_norm_mlp_g, a_w_in=v_a_w_in, a_ln_v_g=v_a_ln_v_g,
             a_ln_v_b=v_a_ln_v_b, a_w_s=v_a_w_s, a_b_s=v_a_b_s, a_w_out=v_a_w_out, b_w_q_a=v_b_w_q_a,
             b_q_norm_g=v_b_q_norm_g, b_w_q_b=v_b_w_q_b, b_w_o=v_b_w_o, kv_src_norm_g=v_kv_src_norm_g,
             kv_w_a=v_kv_w_a, kv_a_norm_g=v_kv_a_norm_g, kv_w_b=v_kv_w_b, mlp_w1=v_mlp_w1, mlp_w2=v_mlp_w2,
             final_norm_g=v_final_norm_g)
    t = x.shape[1]

    first = ("a_w_in", "a_w_out", "a_ln_v_g", "a_ln_v_b")
    later = ("mlp_w1_0", "mlp_w2_0", "mlp_w1_1", "mlp_w2_1", "kv_w_a", "kv_w_b", "b_w_q_a", "b_w_q_b", "b_w_o")
    blocks = {k: _three_d(w[k]) for k in BIG if not k.startswith("mlp")}
    for k in ("mlp_w1", "mlp_w2"):
        blocks[k + "_0"], blocks[k + "_1"] = w[k][0:1], w[k][1:2]
    got, casts = _gather_first([blocks[k] if k in blocks else w[k] for k in first], [blocks[k] for k in later])
    wg = dict(zip(first, got))
    wg["a_w_out"] = wg["a_w_out"].reshape(GATE_DIM, D_MODEL)
    wg["a_ln_v_g"] = wg["a_ln_v_g"].reshape(1, GATE_DIM)
    wg["a_ln_v_b"] = wg["a_ln_v_b"].reshape(1, GATE_DIM)
    shards = dict(zip(later, casts))

    sm = {k: _two_d(k, w[k]) for k in SMALL if k not in ("a_ln_v_g", "a_ln_v_b")}
    sm["a_b_st"] = sm["a_b_s"].T
    inv_freq = (ROPE_THETA ** (-jnp.arange(0, QK_ROPE, 2, dtype=F32) / QK_ROPE)).reshape(1, QK_ROPE // 2)

    loss, dx, g, small = _local_step(x[0], positions.reshape(t, 1), loss_target[0], inv_freq, wg, sm, shards)
    loss = lax.psum(loss[0, 0], ("x", "y", "c"))

    parts = [small[k].reshape((1,) + small[k].shape) for k in SMALL]
    got = _comm_only("exchange_last", _exchange_comm(grads=[g["a_w_in"], g["a_w_out"]], parts=parts))
    g["a_w_in"], g["a_w_out"] = got[:2]
    small_recv = got[2:]

    out = {}
    for k in BIG:
        if k.startswith("mlp"):
            args = (_three_d(w[k]), _three_d(m[k]), _three_d(v[k]))
            res = _adamw_sharded(f"adamw_{k}_1", g[k + "_1"], *args, 1)
            res = _adamw_sharded(f"adamw_{k}_0", g[k + "_0"], *args, 0, into=res)
        else:
            res = _adamw_sharded("adamw_" + k, g[k], _three_d(w[k]), _three_d(m[k]), _three_d(v[k]), 0)
        out[k] = [o.reshape(w[k].shape) for o in res]
    own_row = [k in ("a_ln_v_g", "a_ln_v_b") for k in SMALL]
    res = _adamw_small(small_recv, [_two_d(k, w[k]) for k in SMALL], [_two_d(k, m[k]) for k in SMALL],
                       [_two_d(k, v[k]) for k in SMALL], own_row)
    for i, k in enumerate(SMALL):
        out[k] = [o.reshape(w[k].shape) for o in res[4 * i:4 * i + 4]]

    return (loss, dx.reshape(x.shape), *[out[k][0] for k in WEIGHTS], *[out[k][1] for k in WEIGHTS],
            *[out[k][2] for k in WEIGHTS], *[out[k][3] for k in WEIGHTS])
```

```python
import math

import jax
import jax.numpy as jnp
from jax import lax
from jax.experimental import pallas as pl
from jax.experimental.pallas import tpu as pltpu

F32, BF16 = jnp.float32, jnp.bfloat16
MESH = pl.DeviceIdType.MESH
ANY = pl.BlockSpec(memory_space=pl.ANY)
VMEM = pl.BlockSpec(memory_space=pltpu.VMEM)

N_DEV = 8
D_MODEL = 1024
CHUNK = 64
GMLP_BLOCK = 128
GATE_DIM = 2048
A_GROUPS = 8
A_GROUP_DIM = GATE_DIM // A_GROUPS
B_HEADS = 8
QK_NOPE, QK_ROPE, V_HEAD = 128, 64, 128
Q_LORA, KV_LORA = 384, 256
ROPE_THETA = 10000.0
D_FF = 4096
FF_SLOT = D_FF // N_DEV
EPS = 1e-6
ATT_SCALE = (QK_NOPE + QK_ROPE) ** -0.5

ADAM_LR, ADAM_B1, ADAM_B2, ADAM_EPS, ADAM_WD, ADAM_STEP = 0.001, 0.9, 0.999, 1e-08, 0.01, 10

TM = 256
TM_GATE = 128
VMEM_LIMIT = 56 * 1024 * 1024
INV_SQRT2 = 1.0 / math.sqrt(2.0)
INV_SQRT_2PI = 1.0 / math.sqrt(2.0 * math.pi)


def _dot(a, b):
    return jnp.dot(a, b, preferred_element_type=F32)


def _dot_nt(a, b):
    return lax.dot_general(a, b, (((1,), (1,)), ((), ())), preferred_element_type=F32)


def _dot_tn(a, b):
    return lax.dot_general(a, b, (((0,), (0,)), ((), ())), preferred_element_type=F32)


def _rms_fwd(x, g):
    rstd = lax.rsqrt(jnp.mean(x * x, axis=-1, keepdims=True) + EPS)
    xhat = x * rstd
    return xhat * g, xhat, rstd


def _rms_bwd(dy, xhat, rstd, g):
    dxhat = dy * g
    dx = rstd * (dxhat - xhat * jnp.mean(dxhat * xhat, axis=-1, keepdims=True))
    return dx, jnp.sum(dy * xhat, axis=0, keepdims=True)


def _ln_fwd(v, g, b):
    mu = jnp.mean(v, axis=-1, keepdims=True)
    vc = v - mu
    rstd = lax.rsqrt(jnp.mean(vc * vc, axis=-1, keepdims=True) + EPS)
    vhat = vc * rstd
    return vhat * g + b, vhat, rstd


def _gelu(x):
    return 0.5 * x * (1.0 + lax.erf(x * INV_SQRT2))


def _gelu_grad(x):
    return 0.5 * (1.0 + lax.erf(x * INV_SQRT2)) + x * jnp.exp(-0.5 * x * x) * INV_SQRT_2PI


def _rope(x, cos, sin):
    x1, x2 = x[:, :QK_ROPE // 2], x[:, QK_ROPE // 2:]
    return jnp.concatenate([x1 * cos - x2 * sin, x2 * cos + x1 * sin], axis=-1)


def _gate_mask():
    row = lax.broadcasted_iota(jnp.int32, (GMLP_BLOCK, GMLP_BLOCK), 0)
    col = lax.broadcasted_iota(jnp.int32, (GMLP_BLOCK, GMLP_BLOCK), 1)
    return (col < CHUNK) | (row >= CHUNK)


def _att_mask(q0, tq, t):
    q = q0 + lax.broadcasted_iota(jnp.int32, (tq, t), 0)
    k = lax.broadcasted_iota(jnp.int32, (tq, t), 1)
    return jnp.right_shift(k, 6) <= jnp.right_shift(q, 6)


def _res(shape, imap=None):
    zeros = (0,) * len(shape)
    return pl.BlockSpec(shape, imap or (lambda i: zeros), pipeline_mode=pl.Buffered(1))


def _const(shape):
    zeros = (0,) * len(shape)
    return pl.BlockSpec(shape, lambda i: zeros)


def _row(d, tm=TM):
    return pl.BlockSpec((tm, d), lambda i: (i, 0))


def _heads(d):
    return pl.BlockSpec((B_HEADS, TM, d), lambda i: (0, i, 0))


def _sds(shape, dt):
    return jax.ShapeDtypeStruct(shape, dt)


def _acc(ref, val):
    @pl.when(pl.program_id(0) == 0)
    def _():
        ref[...] = jnp.zeros_like(ref)
    ref[...] += val


def _my_place():
    x, y, c = lax.axis_index("x"), lax.axis_index("y"), lax.axis_index("c")
    return x, y, c, 4 * x + 2 * y + c


def _peer(x, y, c, k):
    px = 1 - x if k & 4 else x
    py = 1 - y if k & 2 else y
    pc = 1 - c if k & 1 else c
    return (px, py, pc), 4 * px + 2 * py + pc


CHIPS = (2, 4, 6)


def _gather_copy(outs, send_sems, recv_sems, a, k, block, to, src=None):
    rows = outs[a].at[pl.ds(block, 1)]
    return pltpu.make_async_remote_copy(
        src_ref=rows if src is None else src, dst_ref=rows, send_sem=send_sems.at[a, k], recv_sem=recv_sems.at[a, k],
        device_id=to, device_id_type=MESH)


def _gather_start(srcs, outs, sems):
    send_sems, recv_sems, local_sems = sems
    x, y, c, me = _my_place()
    for a in range(len(srcs)):
        pltpu.make_async_copy(srcs[a], outs[a].at[pl.ds(me, 1)], local_sems.at[a]).start()
        _gather_copy(outs, send_sems, recv_sems, a, 0, me, _peer(x, y, c, 1)[0], src=srcs[a]).start()
        for j, k in enumerate(CHIPS):
            _gather_copy(outs, send_sems, recv_sems, a, 1 + j, me, _peer(x, y, c, k)[0], src=srcs[a]).start()


def _gather_finish(srcs, outs, sems):
    send_sems, recv_sems, local_sems = sems
    x, y, c, me = _my_place()
    sib, sib_i = _peer(x, y, c, 1)
    n = len(srcs)
    for j, k in enumerate(CHIPS):
        for a in range(n):
            block = _peer(x, y, c, k)[1]
            _gather_copy(outs, send_sems, recv_sems, a, 1 + j, block, sib).wait_recv()
            _gather_copy(outs, send_sems, recv_sems, a, 4 + j, block, sib).start()
    for a in range(n):
        _gather_copy(outs, send_sems, recv_sems, a, 0, sib_i, sib).wait_recv()
        for j, k in enumerate(CHIPS):
            _gather_copy(outs, send_sems, recv_sems, a, 4 + j, _peer(x, y, c, k ^ 1)[1], sib).wait_recv()
    for a in range(n):
        for k in range(7):
            _gather_copy(outs, send_sems, recv_sems, a, k, me, sib, src=srcs[a] if k < 4 else None).wait_send()
        pltpu.make_async_copy(srcs[a], outs[a].at[pl.ds(me, 1)], local_sems.at[a]).wait()


def _gather_sems(n):
    return [pltpu.SemaphoreType.DMA((n, 7)), pltpu.SemaphoreType.DMA((n, 7)), pltpu.SemaphoreType.DMA((n,))]


class _Comm:
    def __init__(self, args, out_shape, scratch, start, finish):
        self.args, self.out_shape, self.scratch, self.start, self.finish = args, out_shape, scratch, start, finish


def _gather_comm(shards):
    return _Comm(list(shards), [_sds((N_DEV,) + s.shape[1:], s.dtype) for s in shards], _gather_sems(len(shards)),
                 _gather_start, _gather_finish)


def _direct_copies(ins, outs, sems, wait, from_block):
    send_sems, recv_sems, local_sems = sems
    x, y, c, me = _my_place()
    for a in range(len(ins)):
        src = ins[a].at[pl.ds(me, 1)] if from_block[a] else ins[a]
        local = pltpu.make_async_copy(src, outs[a].at[pl.ds(me, 1)], local_sems.at[a])
        local.wait() if wait else local.start()
        for k in range(1, N_DEV):
            to, to_i = _peer(x, y, c, k)
            cp = pltpu.make_async_remote_copy(
                src_ref=ins[a].at[pl.ds(to_i, 1)] if from_block[a] else ins[a], dst_ref=outs[a].at[pl.ds(me, 1)],
                send_sem=send_sems.at[a, k - 1], recv_sem=recv_sems.at[a, k - 1], device_id=to, device_id_type=MESH)
            cp.wait() if wait else cp.start()


def _exchange_comm(grads=(), parts=()):
    ins = list(grads) + list(parts)
    from_block = [True] * len(grads) + [False] * len(parts)
    out_shape = [_sds(g.shape, g.dtype) for g in grads] + [_sds((N_DEV,) + p.shape[1:], p.dtype) for p in parts]

    def start(ins_, outs_, sems_):
        _direct_copies(ins_, outs_, sems_, False, from_block)

    def finish(ins_, outs_, sems_):
        _direct_copies(ins_, outs_, sems_, True, from_block)

    return _Comm(ins, out_shape, _gather_sems(len(ins)), start, finish)


def _chip_copies(ins, outs, sems, wait):
    send_sems, recv_sems, local_sems = sems
    x, y, c, _ = _my_place()
    my_chip = 2 * x + y
    for a in range(len(ins)):
        local = pltpu.make_async_copy(ins[a].at[pl.ds(my_chip, 1)], outs[a].at[pl.ds(my_chip, 1)], local_sems.at[a])
        local.wait() if wait else local.start()
        for j, k in enumerate(CHIPS):
            to = _peer(x, y, c, k)[0]
            cp = pltpu.make_async_remote_copy(
                src_ref=ins[a].at[pl.ds(2 * to[0] + to[1], 1)], dst_ref=outs[a].at[pl.ds(my_chip, 1)],
                send_sem=send_sems.at[a, j], recv_sem=recv_sems.at[a, j], device_id=to, device_id_type=MESH)
            cp.wait() if wait else cp.start()


def _chip_exchange_comm(sums):
    def start(ins_, outs_, sems_):
        _chip_copies(ins_, outs_, sems_, False)

    def finish(ins_, outs_, sems_):
        _chip_copies(ins_, outs_, sems_, True)

    n = len(sums)
    sems = [pltpu.SemaphoreType.DMA((n, 3)), pltpu.SemaphoreType.DMA((n, 3)), pltpu.SemaphoreType.DMA((n,))]
    return _Comm(list(sums), [_sds(s.shape, s.dtype) for s in sums], sems, start, finish)


def _join(c1, c2):
    ni, no, ns = len(c1.args), len(c1.out_shape), len(c1.scratch)

    def start(i, o, s):
        c1.start(i[:ni], o[:no], s[:ns])
        c2.start(i[ni:], o[no:], s[ns:])

    def finish(i, o, s):
        c1.finish(i[:ni], o[:no], s[:ns])
        c2.finish(i[ni:], o[no:], s[ns:])

    return _Comm(c1.args + c2.args, c1.out_shape + c2.out_shape, c1.scratch + c2.scratch, start, finish)


def _pair_reduce(name, grads):
    n = len(grads)
    n_chips = N_DEV // 2

    def body(*refs):
        g_refs, gh_refs, p_refs, land = refs[:n], refs[n:2 * n], refs[2 * n:3 * n], refs[3 * n:4 * n]
        buf = refs[4 * n:5 * n]
        send_sems, recv_sems, copy_sems = refs[5 * n:]
        x, y, c, _ = _my_place()
        sib = _peer(x, y, c, 1)[0]
        q = pl.program_id(0)

        def to_sibling(a, j):
            return pltpu.make_async_remote_copy(
                src_ref=gh_refs[a].at[j, pl.ds(1 - c, 1)], dst_ref=land[a].at[pl.ds(j, 1)],
                send_sem=send_sems.at[a, j], recv_sem=recv_sems.at[a, j], device_id=sib, device_id_type=MESH)

        @pl.when(q == 0)
        def _():
            for a in range(n):
                for j in range(n_chips):
                    to_sibling(a, j).start()

        for a in range(n):
            to_sibling(a, q).wait_recv()
            landed = pltpu.make_async_copy(land[a].at[pl.ds(q, 1)], buf[a], copy_sems.at[a])
            landed.start()
            landed.wait()
            p_refs[a][...] = (g_refs[a][0, pl.ds(c, 1)].astype(F32) + buf[a][...].astype(F32)).astype(BF16)

        @pl.when(q == n_chips - 1)
        def _():
            for a in range(n):
                for j in range(n_chips):
                    to_sibling(a, j).wait_send()

    views = [g.reshape((n_chips, 2) + g.shape[1:]) for g in grads]
    res = pl.pallas_call(
        body, name=name, grid=(n_chips,),
        in_specs=[pl.BlockSpec((1, 2) + g.shape[1:], lambda q: (q, 0, 0, 0)) for g in grads] + [ANY] * n,
        out_specs=[pl.BlockSpec((1,) + g.shape[1:], lambda q: (q, 0, 0)) for g in grads] + [ANY] * n,
        out_shape=[_sds((n_chips,) + g.shape[1:], BF16) for g in grads] * 2,
        scratch_shapes=[pltpu.VMEM((1,) + g.shape[1:], BF16) for g in grads]
        + [pltpu.SemaphoreType.DMA((n, n_chips)), pltpu.SemaphoreType.DMA((n, n_chips)), pltpu.SemaphoreType.DMA((n,))],
        compiler_params=pltpu.CompilerParams(dimension_semantics=("arbitrary",), vmem_limit_bytes=VMEM_LIMIT),
    )(*views, *views)
    return list(res[:n])


def _call(name, body, grid, in_specs, out_specs, out_shape, args, scratch=(), comm=None):
    params = pltpu.CompilerParams(dimension_semantics=("arbitrary",) * len(grid), vmem_limit_bytes=VMEM_LIMIT)
    if comm is None:
        outs = pl.pallas_call(body, name=name, grid=grid, in_specs=list(in_specs), out_specs=list(out_specs),
                              out_shape=list(out_shape), scratch_shapes=list(scratch), compiler_params=params)(*args)
        return list(outs), []
    ni, nci, no, nco, ns = len(in_specs), len(comm.args), len(out_specs), len(comm.out_shape), len(scratch)

    def carrying(*refs):
        ins, refs = refs[:ni], refs[ni:]
        cin, refs = refs[:nci], refs[nci:]
        outs, refs = refs[:no], refs[no:]
        cout, refs = refs[:nco], refs[nco:]
        scr, csems = refs[:ns], refs[ns:]
        ids = [pl.program_id(ax) for ax in range(len(grid))]
        first, last = ids[0] == 0, ids[0] == grid[0] - 1
        for ax in range(1, len(grid)):
            first, last = first & (ids[ax] == 0), last & (ids[ax] == grid[ax] - 1)

        @pl.when(first)
        def _():
            comm.start(cin, cout, csems)

        body(*ins, *outs, *scr)

        @pl.when(last)
        def _():
            comm.finish(cin, cout, csems)

    outs = pl.pallas_call(
        carrying, name=name, grid=grid, in_specs=list(in_specs) + [ANY] * nci, out_specs=list(out_specs) + [ANY] * nco,
        out_shape=list(out_shape) + list(comm.out_shape), scratch_shapes=list(scratch) + list(comm.scratch),
        compiler_params=params)(*args, *comm.args)
    return list(outs[:no]), list(outs[no:])


def _comm_only(name, comm):
    def body(*refs):
        nci, nco = len(comm.args), len(comm.out_shape)
        cin, cout, csems = refs[:nci], refs[nci:nci + nco], refs[nci + nco:]
        comm.start(cin, cout, csems)
        comm.finish(cin, cout, csems)

    return pl.pallas_call(body, name=name, in_specs=[ANY] * len(comm.args), out_specs=[ANY] * len(comm.out_shape),
                          out_shape=list(comm.out_shape), scratch_shapes=list(comm.scratch))(*comm.args)


def _gather_first(first, later):
    nf, nl = len(first), len(later)
    dts = [BF16] * (nf - 2) + [F32, F32]

    def body(*refs):
        ins, refs = refs[:nf + nl], refs[nf + nl:]
        outs, refs = refs[:nf], refs[nf:]
        casts, refs = refs[:nl], refs[nl:]
        stage, sems = refs[:nf], refs[nf:]
        for a in range(nf):
            stage[a][...] = ins[a][...].astype(dts[a])
        _gather_start(stage, outs, sems)
        for a in range(nl):
            casts[a][...] = ins[nf + a][...].astype(BF16)
        _gather_finish(stage, outs, sems)

    res = pl.pallas_call(
        body, name="gather_first",
        in_specs=[VMEM] * (nf + nl), out_specs=[ANY] * nf + [VMEM] * nl,
        out_shape=[_sds((N_DEV,) + s.shape[1:], dt) for s, dt in zip(first, dts)]
        + [_sds(s.shape, BF16) for s in later],
        scratch_shapes=[pltpu.VMEM(s.shape, dt) for s, dt in zip(first, dts)] + _gather_sems(nf),
        compiler_params=pltpu.CompilerParams(vmem_limit_bytes=VMEM_LIMIT),
    )(*first, *later)
    return list(res[:nf]), list(res[nf:])


def _a_mix_fwd(x, g, w_in, ln_g, ln_b, w_s, b_st, w_out, comm=None):
    t = x.shape[0]
    nblk = TM // GMLP_BLOCK

    def body(x_ref, g_ref, win_ref, lng_ref, lnb_ref, ws_ref, bst_ref, wout_ref, h_ref, z_ref, gated_scr):
        xv = x_ref[...]
        hb = _rms_fwd(xv, g_ref[...])[0].astype(BF16)
        for d in range(N_DEV):
            z_ref[:, d * FF_SLOT:(d + 1) * FF_SLOT] = _dot(hb, win_ref[d])
        u = _gelu(z_ref[:, :GATE_DIM])
        vb = _ln_fwd(_gelu(z_ref[:, GATE_DIM:]), lng_ref[...], lnb_ref[...])[0].astype(BF16)
        mask = _gate_mask()
        for gi in range(A_GROUPS):
            wm = jnp.where(mask, ws_ref[gi], 0.0).astype(BF16)
            bias = bst_ref[:, gi:gi + 1]
            cs = slice(gi * A_GROUP_DIM, (gi + 1) * A_GROUP_DIM)
            for n in range(nblk):
                rs = slice(n * GMLP_BLOCK, (n + 1) * GMLP_BLOCK)
                sv = _dot(wm, vb[rs, cs]) + bias
                gated_scr[rs, cs] = (u[rs, cs] * sv).astype(BF16)
        h_ref[...] = xv + _dot(gated_scr[...], wout_ref[...])

    return _call(
        "a_mix_fwd", body, (t // TM,),
        [_row(D_MODEL), _res((1, D_MODEL)), _res((N_DEV, D_MODEL, FF_SLOT)), _res((1, GATE_DIM)),
         _res((1, GATE_DIM)), _res((A_GROUPS, GMLP_BLOCK, GMLP_BLOCK)), _res((GMLP_BLOCK, A_GROUPS)),
         _res((GATE_DIM, D_MODEL))],
        [_row(D_MODEL), _row(2 * GATE_DIM)],
        [_sds((t, D_MODEL), F32), _sds((t, 2 * GATE_DIM), F32)],
        (x, g, w_in, ln_g, ln_b, w_s, b_st, w_out), scratch=[pltpu.VMEM((TM, GATE_DIM), BF16)], comm=comm)


MLP_W_SPECS = (_res((N_DEV, D_MODEL, FF_SLOT)), _res((N_DEV, FF_SLOT, D_MODEL)))


def _mlp_fwd(h, g, w1, w2, layer, comm=None):
    t = h.shape[0]

    def body(h_ref, g_ref, w1_ref, w2_ref, o_ref, a_ref):
        hv = h_ref[...]
        hb = _rms_fwd(hv, g_ref[...])[0].astype(BF16)
        o_ref[...] = hv
        for d in range(N_DEV):
            a = _dot(hb, w1_ref[d])
            a_ref[:, d * FF_SLOT:(d + 1) * FF_SLOT] = a
            r = jnp.maximum(a, 0.0)
            o_ref[...] += _dot((r * r).astype(BF16), w2_ref[d])

    return _call(
        f"mlp_fwd_{layer}", body, (t // TM,), [_row(D_MODEL), _res((1, D_MODEL)), *MLP_W_SPECS],
        [_row(D_MODEL), _row(D_FF)], [_sds((t, D_MODEL), F32), _sds((t, D_FF), F32)], (h, g, w1, w2), comm=comm)


KVQ_W_SPECS = (_res((1, D_MODEL)), _res((D_MODEL, KV_LORA + QK_ROPE)), _res((1, KV_LORA)),
               _res((B_HEADS, KV_LORA, QK_NOPE + V_HEAD)), _res((1, D_MODEL)), _res((D_MODEL, Q_LORA)),
               _res((1, Q_LORA)), _res((B_HEADS, Q_LORA, QK_NOPE + QK_ROPE)))


def _kvq_fwd(h, pos, inv_freq, kvq_w):
    t = h.shape[0]
    half = QK_ROPE // 2

    def body(h_ref, pos_ref, invf_ref, srcg_ref, wkva_ref, kvag_ref, wkvb_ref, mixg_ref, wqa_ref, qg_ref, wqb_ref,
             ckv_ref, kn_ref, v_ref, kpe_ref, cqpre_ref, q_ref, cos_ref, sin_ref):
        hv = h_ref[...]
        xhat = hv * lax.rsqrt(jnp.mean(hv * hv, axis=-1, keepdims=True) + EPS)
        ang = pos_ref[...].astype(F32) * invf_ref[...]
        cos, sin = jnp.cos(ang), jnp.sin(ang)
        cos_ref[...] = cos
        sin_ref[...] = sin
        ckv = _dot((xhat * srcg_ref[...]).astype(BF16), wkva_ref[...])
        ckv_ref[...] = ckv
        cb = _rms_fwd(ckv[:, :KV_LORA], kvag_ref[...])[0].astype(BF16)
        kpe_ref[...] = _rope(ckv[:, KV_LORA:], cos, sin).astype(BF16)
        for hd in range(B_HEADS):
            kv = _dot(cb, wkvb_ref[hd])
            kn_ref[hd] = kv[:, :QK_NOPE].astype(BF16)
            v_ref[hd] = kv[:, QK_NOPE:].astype(BF16)
        cqpre = _dot((xhat * mixg_ref[...]).astype(BF16), wqa_ref[...])
        cqpre_ref[...] = cqpre
        cqb = _rms_fwd(cqpre, qg_ref[...])[0].astype(BF16)
        for hd in range(B_HEADS):
            q = _dot(cqb, wqb_ref[hd])
            q_ref[hd, :, 0:QK_NOPE] = q[:, :QK_NOPE].astype(BF16)
            q_ref[hd, :, QK_NOPE:] = _rope(q[:, QK_NOPE:], cos, sin).astype(BF16)

    return _call(
        "kvq_fwd", body, (t // TM,), [_row(D_MODEL), _row(1), _res((1, half)), *KVQ_W_SPECS],
        [_row(KV_LORA + QK_ROPE), _heads(QK_NOPE), _heads(V_HEAD), _row(QK_ROPE), _row(Q_LORA),
         _heads(QK_NOPE + QK_ROPE), _row(half), _row(half)],
        [_sds((t, KV_LORA + QK_ROPE), F32), _sds((B_HEADS, t, QK_NOPE), BF16), _sds((B_HEADS, t, V_HEAD), BF16),
         _sds((t, QK_ROPE), BF16), _sds((t, Q_LORA), F32), _sds((B_HEADS, t, QK_NOPE + QK_ROPE), BF16),
         _sds((t, half), F32), _sds((t, half), F32)],
        (h, pos, inv_freq, *kvq_w))[0]


def _softmax_rows(qn, qp, kn_ref, kpe_ref, k):
    past, upto = k * TM, (k + 1) * TM
    s = (_dot_nt(qn, kn_ref[0:upto, :]) + _dot_nt(qp, kpe_ref[0:upto, :])) * ATT_SCALE
    own = jnp.where(_att_mask(0, TM, TM), s[:, past:], jnp.finfo(F32).min)
    s = own if k == 0 else jnp.concatenate([s[:, :past], own], axis=1)
    e = jnp.exp(s - jnp.max(s, axis=-1, keepdims=True))
    return e * (1.0 / jnp.sum(e, axis=-1, keepdims=True))


def _for_my_tile(i, nq, fn):
    for k in range(nq):
        @pl.when(i == k)
        def _(k=k):
            fn(k)


def _attn_fwd(h, q, kn, kpe, v, w_o, comm=None):
    t = h.shape[0]
    nq = t // TM

    def body(h_ref, q_ref, kn_ref, kpe_ref, v_ref, wo_ref, o_ref, att_ref):
        i, hd = pl.program_id(0), pl.program_id(1)

        @pl.when(hd == 0)
        def _():
            o_ref[...] = h_ref[...]

        def tile(k):
            p = _softmax_rows(q_ref[:, 0:QK_NOPE], q_ref[:, QK_NOPE:], kn_ref, kpe_ref, k)
            ob = _dot(p.astype(BF16), v_ref[0:(k + 1) * TM, :]).astype(BF16)
            att_ref[...] = ob
            o_ref[...] += _dot(ob, wo_ref[...])

        _for_my_tile(i, nq, tile)

    def per_head(rows, d, tiled):
        return pl.BlockSpec((None, rows, d), (lambda i, hd: (hd, i, 0)) if tiled else (lambda i, hd: (hd, 0, 0)))

    tile_spec = pl.BlockSpec((TM, D_MODEL), lambda i, hd: (i, 0))
    return _call(
        "attn_fwd", body, (nq, B_HEADS),
        [tile_spec, per_head(TM, QK_NOPE + QK_ROPE, True), per_head(t, QK_NOPE, False),
         pl.BlockSpec((t, QK_ROPE), lambda i, hd: (0, 0)), per_head(t, V_HEAD, False),
         per_head(V_HEAD, D_MODEL, False)],
        [tile_spec, per_head(TM, V_HEAD, True)], [_sds((t, D_MODEL), F32), _sds((B_HEADS, t, V_HEAD), BF16)],
        (h, q, kn, kpe, v, w_o), comm=comm)


def _loss_head(h, g, target):
    t = h.shape[0]

    def body(h_ref, g_ref, t_ref, loss_ref, dh_ref, dg_ref):
        y, xhat, rstd = _rms_fwd(h_ref[...], g_ref[...])
        err = y - t_ref[...]
        part = 0.5 * jnp.sum(jnp.mean(err * err, axis=-1, keepdims=True), axis=0, keepdims=True)
        dx, dg = _rms_bwd(err * (1.0 / D_MODEL), xhat, rstd, g_ref[...])
        dh_ref[...] = dx
        _acc(dg_ref, dg)
        _acc(loss_ref, part)

    return _call(
        "loss_head", body, (t // TM,), [_row(D_MODEL), _res((1, D_MODEL)), _row(D_MODEL)],
        [_const((1, 1)), _row(D_MODEL), _const((1, D_MODEL))],
        [_sds((1, 1), F32), _sds((t, D_MODEL), F32), _sds((1, D_MODEL), F32)], (h, g, target))[0]


def _mlp_bwd(h, a, dho, g, w1, w2, layer, comm=None):
    t = h.shape[0]

    def body(h_ref, a_ref, dho_ref, g_ref, w1_ref, w2_ref, dhi_ref, dg_ref, hn_ref, f_ref, da_ref):
        gv = g_ref[...]
        y, xhat, rstd = _rms_fwd(h_ref[...], gv)
        hn_ref[...] = y.astype(BF16)
        dho_v = dho_ref[...]
        dhob = dho_v.astype(BF16)
        dhn = jnp.zeros((TM, D_MODEL), F32)
        for d in range(N_DEV):
            cs = slice(d * FF_SLOT, (d + 1) * FF_SLOT)
            r = jnp.maximum(a_ref[:, cs], 0.0)
            f_ref[:, cs] = (r * r).astype(BF16)
            da = (_dot_nt(dhob, w2_ref[d]) * (2.0 * r)).astype(BF16)
            da_ref[:, cs] = da
            dhn = dhn + _dot_nt(da, w1_ref[d])
        dx, dg = _rms_bwd(dhn, xhat, rstd, gv)
        dhi_ref[...] = dho_v + dx
        _acc(dg_ref, dg)

    return _call(
        f"mlp_bwd_{layer}", body, (t // TM,),
        [_row(D_MODEL), _row(D_FF), _row(D_MODEL), _res((1, D_MODEL)), *MLP_W_SPECS],
        [_row(D_MODEL), _const((1, D_MODEL)), _row(D_MODEL), _row(D_FF), _row(D_FF)],
        [_sds((t, D_MODEL), F32), _sds((1, D_MODEL), F32), _sds((t, D_MODEL), BF16), _sds((t, D_FF), BF16),
         _sds((t, D_FF), BF16)],
        (h, a, dho, g, w1, w2), comm=comm)


def _attn_bwd(dh, q, kn, kpe, v, w_o, cos, sin, comm=None):
    t = dh.shape[0]
    half = QK_ROPE // 2

    def body(dh_ref, q_ref, kn_ref, kpe_ref, v_ref, wo_ref, cos_ref, sin_ref, dq_ref, dkn_ref, dv_ref, dkpe_ref):
        hd, i = pl.program_id(0), pl.program_id(1)

        @pl.when(i == 0)
        def _():
            dkn_ref[...] = jnp.zeros_like(dkn_ref)
            dv_ref[...] = jnp.zeros_like(dv_ref)

        @pl.when((i == 0) & (hd == 0))
        def _():
            dkpe_ref[...] = jnp.zeros_like(dkpe_ref)

        def tile(k):
            keys = slice(0, (k + 1) * TM)
            qn, qp = q_ref[:, 0:QK_NOPE], q_ref[:, QK_NOPE:]
            do = _dot_nt(dh_ref[...].astype(BF16), wo_ref[...]).astype(BF16)
            p = _softmax_rows(qn, qp, kn_ref, kpe_ref, k)
            dp = _dot_nt(do, v_ref[keys, :])
            ds = (p * (dp - jnp.sum(p * dp, axis=-1, keepdims=True)) * ATT_SCALE).astype(BF16)
            dq_ref[:, 0:QK_NOPE] = _dot(ds, kn_ref[keys, :]).astype(BF16)
            dq_ref[:, QK_NOPE:] = _rope(_dot(ds, kpe_ref[keys, :]), cos_ref[...], -sin_ref[...]).astype(BF16)
            dkn_ref[keys, :] += _dot_tn(ds, qn)
            dv_ref[keys, :] += _dot_tn(p.astype(BF16), do)
            dkpe_ref[keys, :] += _dot_tn(ds, qp)

        _for_my_tile(i, t // TM, tile)

    def per_head(rows, d, tiled):
        return pl.BlockSpec((None, rows, d), (lambda hd, i: (hd, i, 0)) if tiled else (lambda hd, i: (hd, 0, 0)))

    def tile(d):
        return pl.BlockSpec((TM, d), lambda hd, i: (i, 0))

    return _call(
        "attn_bwd", body, (B_HEADS, t // TM),
        [tile(D_MODEL), per_head(TM, QK_NOPE + QK_ROPE, True), per_head(t, QK_NOPE, False),
         pl.BlockSpec((t, QK_ROPE), lambda hd, i: (0, 0)), per_head(t, V_HEAD, False),
         per_head(V_HEAD, D_MODEL, False), tile(half), tile(half)],
        [per_head(TM, QK_NOPE + QK_ROPE, True), per_head(t, QK_NOPE, False), per_head(t, V_HEAD, False),
         pl.BlockSpec((t, QK_ROPE), lambda hd, i: (0, 0))],
        [_sds((B_HEADS, t, QK_NOPE + QK_ROPE), BF16), _sds((B_HEADS, t, QK_NOPE), F32),
         _sds((B_HEADS, t, V_HEAD), F32), _sds((t, QK_ROPE), F32)],
        (dh, q, kn, kpe, v, w_o, cos, sin), comm=comm)


def _kvq_bwd(h, dh, ckv, cqpre, dq, dkn, dv, dkpe, cos, sin, kvq_w):
    t = h.shape[0]
    half = QK_ROPE // 2

    def body(h_ref, dh_ref, ckv_ref, cqpre_ref, dq_ref, dkn_ref, dv_ref, dkpe_ref, cos_ref, sin_ref,
             srcg_ref, wkva_ref, kvag_ref, wkvb_ref, mixg_ref, wqa_ref, qg_ref, wqb_ref,
             dhi_ref, hq_ref, hk_ref, cq_ref, dcqpre_ref, c_ref, dkv_ref, dckv_ref,
             dmixg_ref, dsrcg_ref, dqg_ref, dkvag_ref):
        hv = h_ref[...]
        rstd = lax.rsqrt(jnp.mean(hv * hv, axis=-1, keepdims=True) + EPS)
        xhat = hv * rstd
        mixg, srcg, qg, kvag = mixg_ref[...], srcg_ref[...], qg_ref[...], kvag_ref[...]
        hq_ref[...] = (xhat * mixg).astype(BF16)
        hk_ref[...] = (xhat * srcg).astype(BF16)
        cq, cqhat, crstd = _rms_fwd(cqpre_ref[...], qg)
        cq_ref[...] = cq.astype(BF16)
        dcq = jnp.zeros((TM, Q_LORA), F32)
        for hd in range(B_HEADS):
            dcq = dcq + _dot_nt(dq_ref[hd], wqb_ref[hd])
        dcqpre, dqg = _rms_bwd(dcq, cqhat, crstd, qg)
        dcqpre_b = dcqpre.astype(BF16)
        dcqpre_ref[...] = dcqpre_b
        dxq, dmixg = _rms_bwd(_dot_nt(dcqpre_b, wqa_ref[...]), xhat, rstd, mixg)
        ckv = ckv_ref[...]
        c, chat, krstd = _rms_fwd(ckv[:, :KV_LORA], kvag)
        c_ref[...] = c.astype(BF16)
        dc = jnp.zeros((TM, KV_LORA), F32)
        for hd in range(B_HEADS):
            dkv = jnp.concatenate([dkn_ref[hd], dv_ref[hd]], axis=-1).astype(BF16)
            dkv_ref[hd] = dkv
            dc = dc + _dot_nt(dkv, wkvb_ref[hd])
        dlat, dkvag = _rms_bwd(dc, chat, krstd, kvag)
        dpe = _rope(dkpe_ref[...], cos_ref[...], -sin_ref[...])
        dckv_b = jnp.concatenate([dlat, dpe], axis=-1).astype(BF16)
        dckv_ref[...] = dckv_b
        dxk, dsrcg = _rms_bwd(_dot_nt(dckv_b, wkva_ref[...]), xhat, rstd, srcg)
        dhi_ref[...] = dh_ref[...] + dxq + dxk
        _acc(dmixg_ref, dmixg)
        _acc(dsrcg_ref, dsrcg)
        _acc(dqg_ref, dqg)
        _acc(dkvag_ref, dkvag)

    return _call(
        "kvq_bwd", body, (t // TM,),
        [_row(D_MODEL), _row(D_MODEL), _row(KV_LORA + QK_ROPE), _row(Q_LORA), _heads(QK_NOPE + QK_ROPE),
         _heads(QK_NOPE), _heads(V_HEAD), _row(QK_ROPE), _row(half), _row(half), *KVQ_W_SPECS],
        [_row(D_MODEL), _row(D_MODEL), _row(D_MODEL), _row(Q_LORA), _row(Q_LORA), _row(KV_LORA),
         _heads(QK_NOPE + V_HEAD), _row(KV_LORA + QK_ROPE),
         _const((1, D_MODEL)), _const((1, D_MODEL)), _const((1, Q_LORA)), _const((1, KV_LORA))],
        [_sds((t, D_MODEL), F32), _sds((t, D_MODEL), BF16), _sds((t, D_MODEL), BF16), _sds((t, Q_LORA), BF16),
         _sds((t, Q_LORA), BF16), _sds((t, KV_LORA), BF16), _sds((B_HEADS, t, QK_NOPE + V_HEAD), BF16),
         _sds((t, KV_LORA + QK_ROPE), BF16),
         _sds((1, D_MODEL), F32), _sds((1, D_MODEL), F32), _sds((1, Q_LORA), F32), _sds((1, KV_LORA), F32)],
        (h, dh, ckv, cqpre, dq, dkn, dv, dkpe, cos, sin, *kvq_w))[0]


def _a_mix_bwd(x, z, dh, g, w_in, ln_g, ln_b, w_s, b_st, w_out, comm=None):
    t = x.shape[0]
    tm = TM_GATE
    nblk = tm // GMLP_BLOCK

    def body(x_ref, z_ref, dh_ref, g_ref, win_ref, lng_ref, lnb_ref, ws_ref, bst_ref, wout_ref,
             dx_ref, hn_ref, gated_ref, dz_ref, dg_ref, dlng_ref, dlnb_ref, dws_ref, dbs_ref, du_scr, dvn_scr):
        @pl.when(pl.program_id(0) == 0)
        def _():
            dws_ref[...] = jnp.zeros_like(dws_ref)
            dbs_ref[...] = jnp.zeros_like(dbs_ref)

        gv, lng = g_ref[...], lng_ref[...]
        y, xhat, rstd = _rms_fwd(x_ref[...], gv)
        hn_ref[...] = y.astype(BF16)
        dhv = dh_ref[...]
        dgated = _dot_nt(dhv.astype(BF16), wout_ref[...])
        u = _gelu(z_ref[:, :GATE_DIM])
        vn, vhat, lrstd = _ln_fwd(_gelu(z_ref[:, GATE_DIM:]), lng, lnb_ref[...])
        vb = vn.astype(BF16)
        mask = _gate_mask()
        for gi in range(A_GROUPS):
            wm = jnp.where(mask, ws_ref[gi], 0.0).astype(BF16)
            bias = bst_ref[:, gi:gi + 1]
            cs = slice(gi * A_GROUP_DIM, (gi + 1) * A_GROUP_DIM)
            dws = jnp.zeros((GMLP_BLOCK, GMLP_BLOCK), F32)
            dbs = jnp.zeros((GMLP_BLOCK, 1), F32)
            for n in range(nblk):
                rs = slice(n * GMLP_BLOCK, (n + 1) * GMLP_BLOCK)
                sv = _dot(wm, vb[rs, cs]) + bias
                gated_ref[rs, cs] = (u[rs, cs] * sv).astype(BF16)
                du_scr[rs, cs] = dgated[rs, cs] * sv
                dsv = dgated[rs, cs] * u[rs, cs]
                dsvb = dsv.astype(BF16)
                dws = dws + _dot_nt(dsvb, vb[rs, cs])
                dbs = dbs + jnp.sum(dsv, axis=-1, keepdims=True)
                dvn_scr[rs, cs] = _dot_tn(wm, dsvb)
            dws_ref[gi] += jnp.where(mask, dws, 0.0)
            dbs_ref[gi] += dbs
        dvn = dvn_scr[...]
        dvhat = dvn * lng
        dv = lrstd * (dvhat - jnp.mean(dvhat, axis=-1, keepdims=True)
                      - vhat * jnp.mean(dvhat * vhat, axis=-1, keepdims=True))
        dz_ref[:, :GATE_DIM] = (du_scr[...] * _gelu_grad(z_ref[:, :GATE_DIM])).astype(BF16)
        dz_ref[:, GATE_DIM:] = (dv * _gelu_grad(z_ref[:, GATE_DIM:])).astype(BF16)
        dhn = jnp.zeros((tm, D_MODEL), F32)
        for d in range(N_DEV):
            dhn = dhn + _dot_nt(dz_ref[:, d * FF_SLOT:(d + 1) * FF_SLOT], win_ref[d])
        dx, dg = _rms_bwd(dhn, xhat, rstd, gv)
        dx_ref[...] = dhv + dx
        _acc(dg_ref, dg)
        _acc(dlng_ref, jnp.sum(dvn * vhat, axis=0, keepdims=True))
        _acc(dlnb_ref, jnp.sum(dvn, axis=0, keepdims=True))

    return _call(
        "a_mix_bwd", body, (t // tm,),
        [_row(D_MODEL, tm), _row(2 * GATE_DIM, tm), _row(D_MODEL, tm), _res((1, D_MODEL)),
         _res((N_DEV, D_MODEL, FF_SLOT)), _res((1, GATE_DIM)), _res((1, GATE_DIM)),
         _res((A_GROUPS, GMLP_BLOCK, GMLP_BLOCK)), _res((GMLP_BLOCK, A_GROUPS)), _res((GATE_DIM, D_MODEL))],
        [_row(D_MODEL, tm), _row(D_MODEL, tm), _row(GATE_DIM, tm), _row(2 * GATE_DIM, tm),
         _const((1, D_MODEL)), _const((1, GATE_DIM)), _const((1, GATE_DIM)),
         _const((A_GROUPS, GMLP_BLOCK, GMLP_BLOCK)), _const((A_GROUPS, GMLP_BLOCK, 1))],
        [_sds((t, D_MODEL), F32), _sds((t, D_MODEL), BF16), _sds((t, GATE_DIM), BF16),
         _sds((t, 2 * GATE_DIM), BF16), _sds((1, D_MODEL), F32), _sds((1, GATE_DIM), F32),
         _sds((1, GATE_DIM), F32), _sds((A_GROUPS, GMLP_BLOCK, GMLP_BLOCK), F32),
         _sds((A_GROUPS, GMLP_BLOCK, 1), F32)],
        (x, z, dh, g, w_in, ln_g, ln_b, w_s, b_st, w_out),
        scratch=[pltpu.VMEM((tm, GATE_DIM), F32), pltpu.VMEM((tm, GATE_DIM), F32)], comm=comm)


def _wgrad(name, a, b, a_spec, b_spec, m, n):
    def body(a_ref, b_ref, o_ref):
        o_ref[0] = _dot_tn(a_ref[...].astype(BF16), b_ref[...].astype(BF16)).astype(BF16)

    return _call(name, body, (N_DEV,), [a_spec, b_spec], [pl.BlockSpec((1, m, n), lambda d: (d, 0, 0))],
                 [_sds((N_DEV, m, n), BF16)], (a, b))[0][0]


def _full(t, d):
    return pl.BlockSpec((t, d), lambda i: (0, 0), pipeline_mode=pl.Buffered(1))


def _cols(t, d):
    return pl.BlockSpec((t, d), lambda i: (0, i))


def _head(t, d):
    return pl.BlockSpec((None, t, d), lambda i: (i, 0, 0))


def _local_step(x, pos, target, inv_freq, wg, sm, shards=None):
    t = x.shape[0]
    wg = dict(wg)
    dist = shards is not None
    mix_g = [sm["norm_mix_g"][l:l + 1] for l in range(2)]
    mlp_g = [sm["norm_mlp_g"][l:l + 1] for l in range(2)]

    def gather(names):
        return _gather_comm([shards[k] for k in names]) if dist else None

    def send(grads):
        return _exchange_comm(grads=grads) if dist else None

    def send_sums(name, grads):
        return _chip_exchange_comm(_pair_reduce(name, grads)) if dist else None

    def a_args():
        return (wg["a_w_in"], wg["a_ln_v_g"], wg["a_ln_v_b"], sm["a_w_s"], sm["a_b_st"], wg["a_w_out"])

    def kvq_w():
        return (sm["kv_src_norm_g"], wg["kv_w_a"], sm["kv_a_norm_g"], wg["kv_w_b"], mix_g[1], wg["b_w_q_a"],
                sm["b_q_norm_g"], wg["b_w_q_b"])

    names = ("mlp_w1_0", "mlp_w2_0")
    (h1, z), got = _a_mix_fwd(x, mix_g[0], *a_args(), comm=gather(names))
    wg.update(zip(names, got))
    names = ("kv_w_a", "kv_w_b", "b_w_q_a", "b_w_q_b", "b_w_o")
    (h2, a0), got = _mlp_fwd(h1, mlp_g[0], wg["mlp_w1_0"], wg["mlp_w2_0"], 0, comm=gather(names))
    wg.update(zip(names, got))
    if dist:
        wg["b_w_q_a"] = wg["b_w_q_a"].reshape(D_MODEL, Q_LORA)
        wg["kv_w_a"] = wg["kv_w_a"].reshape(D_MODEL, KV_LORA + QK_ROPE)
    ckv, kn, v, kpe, cqpre, q, cos, sin = _kvq_fwd(h2, pos, inv_freq, kvq_w())
    names = ("mlp_w1_1", "mlp_w2_1")
    (h3, att), got = _attn_fwd(h2, q, kn, kpe, v, wg["b_w_o"], comm=gather(names))
    wg.update(zip(names, got))
    (h4, a1), _ = _mlp_fwd(h3, mlp_g[1], wg["mlp_w1_1"], wg["mlp_w2_1"], 1)
    loss, dh4, d_final_g = _loss_head(h4, sm["final_norm_g"], target)

    g = {}
    (dh3, d_mlp_g1, hn, f, da), _ = _mlp_bwd(h3, a1, dh4, mlp_g[1], wg["mlp_w1_1"], wg["mlp_w2_1"], 1)
    g["mlp_w1_1"] = _wgrad("wgrad_w1_1", hn, da, _full(t, D_MODEL), _cols(t, FF_SLOT), D_MODEL, FF_SLOT)
    g["mlp_w2_1"] = _wgrad("wgrad_w2_1", f, dh4, _cols(t, FF_SLOT), _full(t, D_MODEL), FF_SLOT, D_MODEL)
    g["b_w_o"] = _wgrad("wgrad_w_o", att, dh3, _head(t, V_HEAD), _full(t, D_MODEL), V_HEAD, D_MODEL)
    names = ("mlp_w1_1", "mlp_w2_1", "b_w_o")
    (dq, dkn, dv, dkpe), got = _attn_bwd(dh3, q, kn, kpe, v, wg["b_w_o"], cos, sin,
                                         comm=send_sums("pair_reduce_1", [g[k] for k in names]))
    g.update(zip(names, got))
    (dh2, hq, hk, cq, dcqpre, c, dkv, dckv, d_mix_g1, d_src_g, d_q_g, d_kv_a_g) = _kvq_bwd(
        h2, dh3, ckv, cqpre, dq, dkn, dv, dkpe, cos, sin, kvq_w())
    g["b_w_q_a"] = _wgrad("wgrad_w_q_a", hq, dcqpre, _cols(t, D_MODEL // N_DEV), _full(t, Q_LORA),
                          D_MODEL // N_DEV, Q_LORA)
    g["b_w_q_b"] = _wgrad("wgrad_w_q_b", cq, dq, _full(t, Q_LORA), _head(t, QK_NOPE + QK_ROPE),
                          Q_LORA, QK_NOPE + QK_ROPE)
    g["kv_w_a"] = _wgrad("wgrad_kv_w_a", hk, dckv, _cols(t, D_MODEL // N_DEV), _full(t, KV_LORA + QK_ROPE),
                         D_MODEL // N_DEV, KV_LORA + QK_ROPE)
    g["kv_w_b"] = _wgrad("wgrad_kv_w_b", c, dkv, _full(t, KV_LORA), _head(t, QK_NOPE + V_HEAD),
                         KV_LORA, QK_NOPE + V_HEAD)
    names = ("b_w_q_a", "b_w_q_b", "kv_w_a", "kv_w_b")
    (dh1, d_mlp_g0, hn, f, da), got = _mlp_bwd(h1, a0, dh2, mlp_g[0], wg["mlp_w1_0"], wg["mlp_w2_0"], 0,
                                               comm=send([g[k] for k in names]))
    g.update(zip(names, got))
    g["mlp_w1_0"] = _wgrad("wgrad_w1_0", hn, da, _full(t, D_MODEL), _cols(t, FF_SLOT), D_MODEL, FF_SLOT)
    g["mlp_w2_0"] = _wgrad("wgrad_w2_0", f, dh2, _cols(t, FF_SLOT), _full(t, D_MODEL), FF_SLOT, D_MODEL)
    names = ("mlp_w1_0", "mlp_w2_0")
    (dx, hn, gated, dz, d_mix_g0, d_ln_g, d_ln_b, d_ws, d_bs), got = _a_mix_bwd(
        x, z, dh1, mix_g[0], *a_args(), comm=send_sums("pair_reduce_0", [g[k] for k in names]))
    g.update(zip(names, got))
    g["a_w_in"] = _wgrad("wgrad_a_w_in", hn, dz, _full(t, D_MODEL), _cols(t, FF_SLOT), D_MODEL, FF_SLOT)
    g["a_w_out"] = _wgrad("wgrad_a_w_out", gated, dh1, _cols(t, GATE_DIM // N_DEV), _full(t, D_MODEL),
                          GATE_DIM // N_DEV, D_MODEL)
    small = {
        "norm_mix_g": jnp.concatenate([d_mix_g0, d_mix_g1], axis=0),
        "norm_mlp_g": jnp.concatenate([d_mlp_g0, d_mlp_g1], axis=0),
        "a_ln_v_g": d_ln_g.reshape(N_DEV, GATE_DIM // N_DEV),
        "a_ln_v_b": d_ln_b.reshape(N_DEV, GATE_DIM // N_DEV),
        "a_w_s": d_ws,
        "a_b_s": d_bs.reshape(A_GROUPS, GMLP_BLOCK),
        "b_q_norm_g": d_q_g,
        "kv_src_norm_g": d_src_g,
        "kv_a_norm_g": d_kv_a_g,
        "final_norm_g": d_final_g,
    }
    return loss, dx, g, small


def _adamw(w, g, m, v):
    m = ADAM_B1 * m + (1.0 - ADAM_B1) * g
    v = ADAM_B2 * v + (1.0 - ADAM_B2) * (g * g)
    m_hat = m / (1.0 - ADAM_B1 ** ADAM_STEP)
    v_hat = v / (1.0 - ADAM_B2 ** ADAM_STEP)
    return -ADAM_LR * (m_hat / (jnp.sqrt(v_hat) + ADAM_EPS) + ADAM_WD * w), m, v


def _sum_in_device_order(r_ref):
    g = r_ref[0].astype(F32)
    for j in range(1, r_ref.shape[0]):
        g = g + r_ref[j].astype(F32)
    return g


def _adamw_sharded(name, recv, w, m, v, layer, into=None):
    layers, r, c = w.shape
    tr = math.gcd(r, 256)

    def body(r_ref, w_ref, m_ref, v_ref, *refs):
        g_ref, d_ref, nm_ref, nv_ref = refs[-4:]
        g = _sum_in_device_order(r_ref)
        g_ref[...] = g
        d_ref[...], nm_ref[...], nv_ref[...] = _adamw(w_ref[...], g, m_ref[...], v_ref[...])

    blk = pl.BlockSpec((None, tr, c), lambda i: (layer, i, 0))
    args, in_specs, aliases = [recv, w, m, v], [pl.BlockSpec((recv.shape[0], tr, c), lambda i: (0, i, 0)), blk, blk, blk], {}
    if into is not None:
        args += list(into)
        in_specs += [ANY] * 4
        aliases = {4 + i: i for i in range(4)}
    return pl.pallas_call(
        body, name=name, grid=(r // tr,), in_specs=in_specs, out_specs=[blk] * 4,
        out_shape=[_sds(w.shape, F32)] * 4, input_output_aliases=aliases,
        compiler_params=pltpu.CompilerParams(dimension_semantics=("arbitrary",), vmem_limit_bytes=VMEM_LIMIT),
    )(*args)


def _adamw_small(recvs, ws, ms, vs, own_row):
    n = len(recvs)

    def body(*refs):
        r_refs, w_refs, m_refs, v_refs = (refs[i * n:(i + 1) * n] for i in range(4))
        outs, scr = refs[4 * n:8 * n], refs[8 * n:]
        me = _my_place()[3]
        for a in range(n):
            g = _sum_in_device_order(r_refs[a])
            if own_row[a]:
                scr[0][...] = g
                g = scr[0][pl.ds(me, 1), :]
            g_ref, d_ref, nm_ref, nv_ref = outs[4 * a:4 * a + 4]
            g_ref[...] = g
            d_ref[...], nm_ref[...], nv_ref[...] = _adamw(w_refs[a][...], g, m_refs[a][...], v_refs[a][...])

    out_shape = []
    for w in ws:
        out_shape += [_sds(w.shape, F32)] * 4
    return pl.pallas_call(
        body, name="adamw_small", in_specs=[VMEM] * (4 * n), out_specs=[VMEM] * (4 * n), out_shape=out_shape,
        scratch_shapes=[pltpu.VMEM((N_DEV, GATE_DIM // N_DEV), F32)],
    )(*recvs, *ws, *ms, *vs)


BIG = ("a_w_in", "a_w_out", "b_w_q_a", "b_w_q_b", "b_w_o", "kv_w_a", "kv_w_b", "mlp_w1", "mlp_w2")
SMALL = ("norm_mix_g", "norm_mlp_g", "a_ln_v_g", "a_ln_v_b", "a_w_s", "a_b_s", "b_q_norm_g", "kv_src_norm_g",
         "kv_a_norm_g", "final_norm_g")
WEIGHTS = ("norm_mix_g", "norm_mlp_g", "a_w_in", "a_ln_v_g", "a_ln_v_b", "a_w_s", "a_b_s", "a_w_out", "b_w_q_a",
           "b_q_norm_g", "b_w_q_b", "b_w_o", "kv_src_norm_g", "kv_w_a", "kv_a_norm_g", "kv_w_b", "mlp_w1", "mlp_w2",
           "final_norm_g")


def _two_d(name, a):
    if name in ("a_w_s", "a_b_s"):
        return a.reshape(a.shape[1:])
    return a.reshape(1, -1) if a.ndim == 1 else a


def _three_d(a):
    return a if a.ndim == 3 else a.reshape((1,) + a.shape)


def kernel(x, positions, norm_mix_g, norm_mlp_g, a_w_in, a_ln_v_g, a_ln_v_b, a_w_s, a_b_s, a_w_out, b_w_q_a, b_q_norm_g, b_w_q_b, b_w_o, kv_src_norm_g, kv_w_a, kv_a_norm_g, kv_w_b, mlp_w1, mlp_w2, final_norm_g, loss_target, m_norm_mix_g, m_norm_mlp_g, m_a_w_in, m_a_ln_v_g, m_a_ln_v_b, m_a_w_s, m_a_b_s, m_a_w_out, m_b_w_q_a, m_b_q_norm_g, m_b_w_q_b, m_b_w_o, m_kv_src_norm_g, m_kv_w_a, m_kv_a_norm_g, m_kv_w_b, m_mlp_w1, m_mlp_w2, m_final_norm_g, v_norm_mix_g, v_norm_mlp_g, v_a_w_in, v_a_ln_v_g, v_a_ln_v_b, v_a_w_s, v_a_b_s, v_a_w_out, v_b_w_q_a, v_b_q_norm_g, v_b_w_q_b, v_b_w_o, v_kv_src_norm_g, v_kv_w_a, v_kv_a_norm_g, v_kv_w_b, v_mlp_w1, v_mlp_w2, v_final_norm_g):
    w = dict(norm_mix_g=norm_mix_g, norm_mlp_g=norm_mlp_g, a_w_in=a_w_in, a_ln_v_g=a_ln_v_g, a_ln_v_b=a_ln_v_b,
             a_w_s=a_w_s, a_b_s=a_b_s, a_w_out=a_w_out, b_w_q_a=b_w_q_a, b_q_norm_g=b_q_norm_g, b_w_q_b=b_w_q_b,
             b_w_o=b_w_o, kv_src_norm_g=kv_src_norm_g, kv_w_a=kv_w_a, kv_a_norm_g=kv_a_norm_g, kv_w_b=kv_w_b,
             mlp_w1=mlp_w1, mlp_w2=mlp_w2, final_norm_g=final_norm_g)
    m = dict(norm_mix_g=m_norm_mix_g, norm_mlp_g=m_norm_mlp_g, a_w_in=m_a_w_in, a_ln_v_g=m_a_ln_v_g,
             a_ln_v_b=m_a_ln_v_b, a_w_s=m_a_w_s, a_b_s=m_a_b_s, a_w_out=m_a_w_out, b_w_q_a=m_b_w_q_a,
             b_q_norm_g=m_b_q_norm_g, b_w_q_b=m_b_w_q_b, b_w_o=m_b_w_o, kv_src_norm_g=m_kv_src_norm_g,
             kv_w_a=m_kv_w_a, kv_a_norm_g=m_kv_a_norm_g, kv_w_b=m_kv_w_b, mlp_w1=m_mlp_w1, mlp_w2=m_mlp_w2,
             final_norm_g=m_final_norm_g)
    v = dict(norm_mix_g=v_norm_mix_g, norm_mlp_g=v_norm_mlp_g, a_w_in=v_a_w_in, a_ln_v_g=v_a_ln_v_g,
             a_ln_v_b=v_a_ln_v_b, a_w_s=v_a_w_s, a_b_s=v_a_b_s, a_w_out=v_a_w_out, b_w_q_a=v_b_w_q_a,
             b_q_norm_g=v_b_q_norm_g, b_w_q_b=v_b_w_q_b, b_w_o=v_b_w_o, kv_src_norm_g=v_kv_src_norm_g,
             kv_w_a=v_kv_w_a, kv_a_norm_g=v_kv_a_norm_g, kv_w_b=v_kv_w_b, mlp_w1=v_mlp_w1, mlp_w2=v_mlp_w2,
             final_norm_g=v_final_norm_g)
    t = x.shape[1]

    first = ("a_w_in", "a_w_out", "a_ln_v_g", "a_ln_v_b")
    later = ("mlp_w1_0", "mlp_w2_0", "mlp_w1_1", "mlp_w2_1", "kv_w_a", "kv_w_b", "b_w_q_a", "b_w_q_b", "b_w_o")
    blocks = {k: _three_d(w[k]) for k in BIG if not k.startswith("mlp")}
    for k in ("mlp_w1", "mlp_w2"):
        blocks[k + "_0"], blocks[k + "_1"] = w[k][0:1], w[k][1:2]
    got, casts = _gather_first([blocks[k] if k in blocks else w[k] for k in first], [blocks[k] for k in later])
    wg = dict(zip(first, got))
    wg["a_w_out"] = wg["a_w_out"].reshape(GATE_DIM, D_MODEL)
    wg["a_ln_v_g"] = wg["a_ln_v_g"].reshape(1, GATE_DIM)
    wg["a_ln_v_b"] = wg["a_ln_v_b"].reshape(1, GATE_DIM)
    shards = dict(zip(later, casts))

    sm = {k: _two_d(k, w[k]) for k in SMALL if k not in ("a_ln_v_g", "a_ln_v_b")}
    sm["a_b_st"] = sm["a_b_s"].T
    inv_freq = (ROPE_THETA ** (-jnp.arange(0, QK_ROPE, 2, dtype=F32) / QK_ROPE)).reshape(1, QK_ROPE // 2)

    loss, dx, g, small = _local_step(x[0], positions.reshape(t, 1), loss_target[0], inv_freq, wg, sm, shards)
    loss = lax.psum(loss[0, 0], ("x", "y", "c"))

    parts = [small[k].reshape((1,) + small[k].shape) for k in SMALL]
    sums = _pair_reduce("pair_reduce_a", [g["a_w_in"], g["a_w_out"]])
    got = _comm_only("exchange_last", _join(_chip_exchange_comm(sums), _exchange_comm(parts=parts)))
    g["a_w_in"], g["a_w_out"] = got[:2]
    small_recv = got[2:]

    out = {}
    for k in BIG:
        if k.startswith("mlp"):
            args = (_three_d(w[k]), _three_d(m[k]), _three_d(v[k]))
            res = _adamw_sharded(f"adamw_{k}_1", g[k + "_1"], *args, 1)
            res = _adamw_sharded(f"adamw_{k}_0", g[k + "_0"], *args, 0, into=res)
        else:
            res = _adamw_sharded("adamw_" + k, g[k], _three_d(w[k]), _three_d(m[k]), _three_d(v[k]), 0)
        out[k] = [o.reshape(w[k].shape) for o in res]
    own_row = [k in ("a_ln_v_g", "a_ln_v_b") for k in SMALL]
    res = _adamw_small(small_recv, [_two_d(k, w[k]) for k in SMALL], [_two_d(k, m[k]) for k in SMALL],
                       [_two_d(k, v[k]) for k in SMALL], own_row)
    for i, k in enumerate(SMALL):
        out[k] = [o.reshape(w[k].shape) for o in res[4 * i:4 * i + 4]]

    return (loss, dx.reshape(x.shape), *[out[k][0] for k in WEIGHTS], *[out[k][1] for k in WEIGHTS],
            *[out[k][2] for k in WEIGHTS], *[out[k][3] for k in WEIGHTS])
```

```python
import math

import jax
import jax.numpy as jnp
from jax import lax
from jax.experimental import pallas as pl
from jax.experimental.pallas import tpu as pltpu

F32, BF16 = jnp.float32, jnp.bfloat16
MESH = pl.DeviceIdType.MESH
ANY = pl.BlockSpec(memory_space=pl.ANY)
VMEM = pl.BlockSpec(memory_space=pltpu.VMEM)

N_DEV = 8
D_MODEL = 1024
CHUNK = 64
GMLP_BLOCK = 128
GATE_DIM = 2048
A_GROUPS = 8
A_GROUP_DIM = GATE_DIM // A_GROUPS
B_HEADS = 8
QK_NOPE, QK_ROPE, V_HEAD = 128, 64, 128
Q_LORA, KV_LORA = 384, 256
ROPE_THETA = 10000.0
D_FF = 4096
FF_SLOT = D_FF // N_DEV
EPS = 1e-6
ATT_SCALE = (QK_NOPE + QK_ROPE) ** -0.5

ADAM_LR, ADAM_B1, ADAM_B2, ADAM_EPS, ADAM_WD, ADAM_STEP = 0.001, 0.9, 0.999, 1e-08, 0.01, 10

TM = 256
TM_GATE = 128
VMEM_LIMIT = 56 * 1024 * 1024
INV_SQRT2 = 1.0 / math.sqrt(2.0)
INV_SQRT_2PI = 1.0 / math.sqrt(2.0 * math.pi)


def _dot(a, b):
    return jnp.dot(a, b, preferred_element_type=F32)


def _dot_nt(a, b):
    return lax.dot_general(a, b, (((1,), (1,)), ((), ())), preferred_element_type=F32)


def _dot_tn(a, b):
    return lax.dot_general(a, b, (((0,), (0,)), ((), ())), preferred_element_type=F32)


def _rms_fwd(x, g):
    rstd = lax.rsqrt(jnp.mean(x * x, axis=-1, keepdims=True) + EPS)
    xhat = x * rstd
    return xhat * g, xhat, rstd


def _rms_bwd(dy, xhat, rstd, g):
    dxhat = dy * g
    dx = rstd * (dxhat - xhat * jnp.mean(dxhat * xhat, axis=-1, keepdims=True))
    return dx, jnp.sum(dy * xhat, axis=0, keepdims=True)


def _ln_fwd(v, g, b):
    mu = jnp.mean(v, axis=-1, keepdims=True)
    vc = v - mu
    rstd = lax.rsqrt(jnp.mean(vc * vc, axis=-1, keepdims=True) + EPS)
    vhat = vc * rstd
    return vhat * g + b, vhat, rstd


def _gelu(x):
    return 0.5 * x * (1.0 + lax.erf(x * INV_SQRT2))


def _gelu_grad(x):
    return 0.5 * (1.0 + lax.erf(x * INV_SQRT2)) + x * jnp.exp(-0.5 * x * x) * INV_SQRT_2PI


def _rope(x, cos, sin):
    x1, x2 = x[:, :QK_ROPE // 2], x[:, QK_ROPE // 2:]
    return jnp.concatenate([x1 * cos - x2 * sin, x2 * cos + x1 * sin], axis=-1)


def _gate_mask():
    row = lax.broadcasted_iota(jnp.int32, (GMLP_BLOCK, GMLP_BLOCK), 0)
    col = lax.broadcasted_iota(jnp.int32, (GMLP_BLOCK, GMLP_BLOCK), 1)
    return (col < CHUNK) | (row >= CHUNK)


def _att_mask(q0, tq, t):
    q = q0 + lax.broadcasted_iota(jnp.int32, (tq, t), 0)
    k = lax.broadcasted_iota(jnp.int32, (tq, t), 1)
    return jnp.right_shift(k, 6) <= jnp.right_shift(q, 6)


def _res(shape, imap=None):
    zeros = (0,) * len(shape)
    return pl.BlockSpec(shape, imap or (lambda i: zeros), pipeline_mode=pl.Buffered(1))


def _const(shape):
    zeros = (0,) * len(shape)
    return pl.BlockSpec(shape, lambda i: zeros)


def _row(d, tm=TM):
    return pl.BlockSpec((tm, d), lambda i: (i, 0))


def _heads(d):
    return pl.BlockSpec((B_HEADS, TM, d), lambda i: (0, i, 0))


def _sds(shape, dt):
    return jax.ShapeDtypeStruct(shape, dt)


def _acc(ref, val):
    @pl.when(pl.program_id(0) == 0)
    def _():
        ref[...] = jnp.zeros_like(ref)
    ref[...] += val


def _my_place():
    x, y, c = lax.axis_index("x"), lax.axis_index("y"), lax.axis_index("c")
    return x, y, c, 4 * x + 2 * y + c


def _peer(x, y, c, k):
    px = 1 - x if k & 4 else x
    py = 1 - y if k & 2 else y
    pc = 1 - c if k & 1 else c
    return (px, py, pc), 4 * px + 2 * py + pc


CHIPS = (2, 4, 6)


def _gather_copy(outs, send_sems, recv_sems, a, k, block, to, src=None):
    rows = outs[a].at[pl.ds(block, 1)]
    return pltpu.make_async_remote_copy(
        src_ref=rows if src is None else src, dst_ref=rows, send_sem=send_sems.at[a, k], recv_sem=recv_sems.at[a, k],
        device_id=to, device_id_type=MESH)


def _gather_start(srcs, outs, sems):
    send_sems, recv_sems, local_sems = sems
    x, y, c, me = _my_place()
    for a in range(len(srcs)):
        pltpu.make_async_copy(srcs[a], outs[a].at[pl.ds(me, 1)], local_sems.at[a]).start()
        _gather_copy(outs, send_sems, recv_sems, a, 0, me, _peer(x, y, c, 1)[0], src=srcs[a]).start()
        for j, k in enumerate(CHIPS):
            _gather_copy(outs, send_sems, recv_sems, a, 1 + j, me, _peer(x, y, c, k)[0], src=srcs[a]).start()


def _gather_finish(srcs, outs, sems):
    send_sems, recv_sems, local_sems = sems
    x, y, c, me = _my_place()
    sib, sib_i = _peer(x, y, c, 1)
    n = len(srcs)
    for a in range(n):
        for j, k in enumerate(CHIPS):
            block = _peer(x, y, c, k)[1]
            _gather_copy(outs, send_sems, recv_sems, a, 1 + j, block, sib).wait_recv()
            _gather_copy(outs, send_sems, recv_sems, a, 4 + j, block, sib).start()
    for a in range(n):
        _gather_copy(outs, send_sems, recv_sems, a, 0, sib_i, sib).wait_recv()
        for j, k in enumerate(CHIPS):
            _gather_copy(outs, send_sems, recv_sems, a, 4 + j, _peer(x, y, c, k ^ 1)[1], sib).wait_recv()
    for a in range(n):
        for k in range(7):
            _gather_copy(outs, send_sems, recv_sems, a, k, me, sib, src=srcs[a] if k < 4 else None).wait_send()
        pltpu.make_async_copy(srcs[a], outs[a].at[pl.ds(me, 1)], local_sems.at[a]).wait()


def _gather_sems(n):
    return [pltpu.SemaphoreType.DMA((n, 7)), pltpu.SemaphoreType.DMA((n, 7)), pltpu.SemaphoreType.DMA((n,))]


class _Comm:
    def __init__(self, args, out_shape, scratch, start, finish):
        self.args, self.out_shape, self.scratch, self.start, self.finish = args, out_shape, scratch, start, finish


def _gather_comm(shards):
    return _Comm(list(shards), [_sds((N_DEV,) + s.shape[1:], s.dtype) for s in shards], _gather_sems(len(shards)),
                 _gather_start, _gather_finish)


def _direct_copies(ins, outs, sems, wait, from_block):
    send_sems, recv_sems, local_sems = sems
    x, y, c, me = _my_place()
    for a in range(len(ins)):
        src = ins[a].at[pl.ds(me, 1)] if from_block[a] else ins[a]
        local = pltpu.make_async_copy(src, outs[a].at[pl.ds(me, 1)], local_sems.at[a])
        local.wait() if wait else local.start()
        for k in range(1, N_DEV):
            to, to_i = _peer(x, y, c, k)
            cp = pltpu.make_async_remote_copy(
                src_ref=ins[a].at[pl.ds(to_i, 1)] if from_block[a] else ins[a], dst_ref=outs[a].at[pl.ds(me, 1)],
                send_sem=send_sems.at[a, k - 1], recv_sem=recv_sems.at[a, k - 1], device_id=to, device_id_type=MESH)
            cp.wait() if wait else cp.start()


def _exchange_comm(grads=(), parts=()):
    ins = list(grads) + list(parts)
    from_block = [True] * len(grads) + [False] * len(parts)
    out_shape = [_sds(g.shape, g.dtype) for g in grads] + [_sds((N_DEV,) + p.shape[1:], p.dtype) for p in parts]

    def start(ins_, outs_, sems_):
        _direct_copies(ins_, outs_, sems_, False, from_block)

    def finish(ins_, outs_, sems_):
        _direct_copies(ins_, outs_, sems_, True, from_block)

    return _Comm(ins, out_shape, _gather_sems(len(ins)), start, finish)


def _chip_copies(ins, outs, sems, wait):
    send_sems, recv_sems, local_sems = sems
    x, y, c, _ = _my_place()
    my_chip = 2 * x + y
    for a in range(len(ins)):
        local = pltpu.make_async_copy(ins[a].at[pl.ds(my_chip, 1)], outs[a].at[pl.ds(my_chip, 1)], local_sems.at[a])
        local.wait() if wait else local.start()
        for j, k in enumerate(CHIPS):
            to = _peer(x, y, c, k)[0]
            cp = pltpu.make_async_remote_copy(
                src_ref=ins[a].at[pl.ds(2 * to[0] + to[1], 1)], dst_ref=outs[a].at[pl.ds(my_chip, 1)],
                send_sem=send_sems.at[a, j], recv_sem=recv_sems.at[a, j], device_id=to, device_id_type=MESH)
            cp.wait() if wait else cp.start()


def _chip_exchange_comm(sums):
    def start(ins_, outs_, sems_):
        _chip_copies(ins_, outs_, sems_, False)

    def finish(ins_, outs_, sems_):
        _chip_copies(ins_, outs_, sems_, True)

    n = len(sums)
    sems = [pltpu.SemaphoreType.DMA((n, 3)), pltpu.SemaphoreType.DMA((n, 3)), pltpu.SemaphoreType.DMA((n,))]
    return _Comm(list(sums), [_sds(s.shape, s.dtype) for s in sums], sems, start, finish)


def _join(c1, c2):
    ni, no, ns = len(c1.args), len(c1.out_shape), len(c1.scratch)

    def start(i, o, s):
        c1.start(i[:ni], o[:no], s[:ns])
        c2.start(i[ni:], o[no:], s[ns:])

    def finish(i, o, s):
        c1.finish(i[:ni], o[:no], s[:ns])
        c2.finish(i[ni:], o[no:], s[ns:])

    return _Comm(c1.args + c2.args, c1.out_shape + c2.out_shape, c1.scratch + c2.scratch, start, finish)


def _pair_reduce(name, grads):
    n = len(grads)
    n_chips = N_DEV // 2

    def body(*refs):
        g_refs, gh_refs, p_refs, land = refs[:n], refs[n:2 * n], refs[2 * n:3 * n], refs[3 * n:4 * n]
        buf = refs[4 * n:5 * n]
        send_sems, recv_sems, copy_sems = refs[5 * n:]
        x, y, c, _ = _my_place()
        sib = _peer(x, y, c, 1)[0]
        q = pl.program_id(0)

        def to_sibling(a, j):
            return pltpu.make_async_remote_copy(
                src_ref=gh_refs[a].at[j, pl.ds(1 - c, 1)], dst_ref=land[a].at[pl.ds(j, 1)],
                send_sem=send_sems.at[a, j], recv_sem=recv_sems.at[a, j], device_id=sib, device_id_type=MESH)

        @pl.when(q == 0)
        def _():
            for j in range(n_chips):
                for a in range(n):
                    to_sibling(a, j).start()

        for a in range(n):
            to_sibling(a, q).wait_recv()
            landed = pltpu.make_async_copy(land[a].at[pl.ds(q, 1)], buf[a], copy_sems.at[a])
            landed.start()
            landed.wait()
            p_refs[a][...] = (g_refs[a][0, pl.ds(c, 1)].astype(F32) + buf[a][...].astype(F32)).astype(BF16)

        @pl.when(q == n_chips - 1)
        def _():
            for a in range(n):
                for j in range(n_chips):
                    to_sibling(a, j).wait_send()

    views = [g.reshape((n_chips, 2) + g.shape[1:]) for g in grads]
    res = pl.pallas_call(
        body, name=name, grid=(n_chips,),
        in_specs=[pl.BlockSpec((1, 2) + g.shape[1:], lambda q: (q, 0, 0, 0)) for g in grads] + [ANY] * n,
        out_specs=[pl.BlockSpec((1,) + g.shape[1:], lambda q: (q, 0, 0)) for g in grads] + [ANY] * n,
        out_shape=[_sds((n_chips,) + g.shape[1:], BF16) for g in grads] * 2,
        scratch_shapes=[pltpu.VMEM((1,) + g.shape[1:], BF16) for g in grads]
        + [pltpu.SemaphoreType.DMA((n, n_chips)), pltpu.SemaphoreType.DMA((n, n_chips)), pltpu.SemaphoreType.DMA((n,))],
        compiler_params=pltpu.CompilerParams(dimension_semantics=("arbitrary",), vmem_limit_bytes=VMEM_LIMIT),
    )(*views, *views)
    return list(res[:n])


def _call(name, body, grid, in_specs, out_specs, out_shape, args, scratch=(), comm=None):
    params = pltpu.CompilerParams(dimension_semantics=("arbitrary",) * len(grid), vmem_limit_bytes=VMEM_LIMIT)
    if comm is None:
        outs = pl.pallas_call(body, name=name, grid=grid, in_specs=list(in_specs), out_specs=list(out_specs),
                              out_shape=list(out_shape), scratch_shapes=list(scratch), compiler_params=params)(*args)
        return list(outs), []
    ni, nci, no, nco, ns = len(in_specs), len(comm.args), len(out_specs), len(comm.out_shape), len(scratch)

    def carrying(*refs):
        ins, refs = refs[:ni], refs[ni:]
        cin, refs = refs[:nci], refs[nci:]
        outs, refs = refs[:no], refs[no:]
        cout, refs = refs[:nco], refs[nco:]
        scr, csems = refs[:ns], refs[ns:]
        ids = [pl.program_id(ax) for ax in range(len(grid))]
        first, last = ids[0] == 0, ids[0] == grid[0] - 1
        for ax in range(1, len(grid)):
            first, last = first & (ids[ax] == 0), last & (ids[ax] == grid[ax] - 1)

        @pl.when(first)
        def _():
            comm.start(cin, cout, csems)

        body(*ins, *outs, *scr)

        @pl.when(last)
        def _():
            comm.finish(cin, cout, csems)

    outs = pl.pallas_call(
        carrying, name=name, grid=grid, in_specs=list(in_specs) + [ANY] * nci, out_specs=list(out_specs) + [ANY] * nco,
        out_shape=list(out_shape) + list(comm.out_shape), scratch_shapes=list(scratch) + list(comm.scratch),
        compiler_params=params)(*args, *comm.args)
    return list(outs[:no]), list(outs[no:])


def _comm_only(name, comm):
    def body(*refs):
        nci, nco = len(comm.args), len(comm.out_shape)
        cin, cout, csems = refs[:nci], refs[nci:nci + nco], refs[nci + nco:]
        comm.start(cin, cout, csems)
        comm.finish(cin, cout, csems)

    return pl.pallas_call(body, name=name, in_specs=[ANY] * len(comm.args), out_specs=[ANY] * len(comm.out_shape),
                          out_shape=list(comm.out_shape), scratch_shapes=list(comm.scratch))(*comm.args)


def _gather_first(first, later):
    nf, nl = len(first), len(later)
    dts = [BF16] * (nf - 2) + [F32, F32]

    def body(*refs):
        ins, refs = refs[:nf + nl], refs[nf + nl:]
        outs, refs = refs[:nf], refs[nf:]
        casts, refs = refs[:nl], refs[nl:]
        stage, sems = refs[:nf], refs[nf:]
        for a in range(nf):
            stage[a][...] = ins[a][...].astype(dts[a])
        _gather_start(stage, outs, sems)
        for a in range(nl):
            casts[a][...] = ins[nf + a][...].astype(BF16)
        _gather_finish(stage, outs, sems)

    res = pl.pallas_call(
        body, name="gather_first",
        in_specs=[VMEM] * (nf + nl), out_specs=[ANY] * nf + [VMEM] * nl,
        out_shape=[_sds((N_DEV,) + s.shape[1:], dt) for s, dt in zip(first, dts)]
        + [_sds(s.shape, BF16) for s in later],
        scratch_shapes=[pltpu.VMEM(s.shape, dt) for s, dt in zip(first, dts)] + _gather_sems(nf),
        compiler_params=pltpu.CompilerParams(vmem_limit_bytes=VMEM_LIMIT),
    )(*first, *later)
    return list(res[:nf]), list(res[nf:])


def _a_mix_fwd(x, g, w_in, ln_g, ln_b, w_s, b_st, w_out, comm=None):
    t = x.shape[0]
    nblk = TM // GMLP_BLOCK

    def body(x_ref, g_ref, win_ref, lng_ref, lnb_ref, ws_ref, bst_ref, wout_ref, h_ref, z_ref, gated_scr):
        xv = x_ref[...]
        hb = _rms_fwd(xv, g_ref[...])[0].astype(BF16)
        for d in range(N_DEV):
            z_ref[:, d * FF_SLOT:(d + 1) * FF_SLOT] = _dot(hb, win_ref[d])
        u = _gelu(z_ref[:, :GATE_DIM])
        vb = _ln_fwd(_gelu(z_ref[:, GATE_DIM:]), lng_ref[...], lnb_ref[...])[0].astype(BF16)
        mask = _gate_mask()
        for gi in range(A_GROUPS):
            wm = jnp.where(mask, ws_ref[gi], 0.0).astype(BF16)
            bias = bst_ref[:, gi:gi + 1]
            cs = slice(gi * A_GROUP_DIM, (gi + 1) * A_GROUP_DIM)
            for n in range(nblk):
                rs = slice(n * GMLP_BLOCK, (n + 1) * GMLP_BLOCK)
                sv = _dot(wm, vb[rs, cs]) + bias
                gated_scr[rs, cs] = (u[rs, cs] * sv).astype(BF16)
        h_ref[...] = xv + _dot(gated_scr[...], wout_ref[...])

    return _call(
        "a_mix_fwd", body, (t // TM,),
        [_row(D_MODEL), _res((1, D_MODEL)), _res((N_DEV, D_MODEL, FF_SLOT)), _res((1, GATE_DIM)),
         _res((1, GATE_DIM)), _res((A_GROUPS, GMLP_BLOCK, GMLP_BLOCK)), _res((GMLP_BLOCK, A_GROUPS)),
         _res((GATE_DIM, D_MODEL))],
        [_row(D_MODEL), _row(2 * GATE_DIM), _row(GATE_DIM)],
        [_sds((t, D_MODEL), F32), _sds((t, 2 * GATE_DIM), F32), _sds((t, GATE_DIM), BF16)],
        (x, g, w_in, ln_g, ln_b, w_s, b_st, w_out), comm=comm)


MLP_W_SPECS = (_res((N_DEV, D_MODEL, FF_SLOT)), _res((N_DEV, FF_SLOT, D_MODEL)))


def _mlp_fwd(h, g, w1, w2, layer, comm=None):
    t = h.shape[0]

    def body(h_ref, g_ref, w1_ref, w2_ref, o_ref, a_ref):
        hv = h_ref[...]
        hb = _rms_fwd(hv, g_ref[...])[0].astype(BF16)
        o_ref[...] = hv
        for d in range(N_DEV):
            a = _dot(hb, w1_ref[d])
            a_ref[:, d * FF_SLOT:(d + 1) * FF_SLOT] = a
            r = jnp.maximum(a, 0.0)
            o_ref[...] += _dot((r * r).astype(BF16), w2_ref[d])

    return _call(
        f"mlp_fwd_{layer}", body, (t // TM,), [_row(D_MODEL), _res((1, D_MODEL)), *MLP_W_SPECS],
        [_row(D_MODEL), _row(D_FF)], [_sds((t, D_MODEL), F32), _sds((t, D_FF), F32)], (h, g, w1, w2), comm=comm)


KVQ_W_SPECS = (_res((1, D_MODEL)), _res((D_MODEL, KV_LORA + QK_ROPE)), _res((1, KV_LORA)),
               _res((B_HEADS, KV_LORA, QK_NOPE + V_HEAD)), _res((1, D_MODEL)), _res((D_MODEL, Q_LORA)),
               _res((1, Q_LORA)), _res((B_HEADS, Q_LORA, QK_NOPE + QK_ROPE)))


def _kvq_fwd(h, pos, inv_freq, kvq_w):
    t = h.shape[0]
    half = QK_ROPE // 2

    def body(h_ref, pos_ref, invf_ref, srcg_ref, wkva_ref, kvag_ref, wkvb_ref, mixg_ref, wqa_ref, qg_ref, wqb_ref,
             ckv_ref, kn_ref, v_ref, kpe_ref, cqpre_ref, q_ref, cos_ref, sin_ref):
        hv = h_ref[...]
        xhat = hv * lax.rsqrt(jnp.mean(hv * hv, axis=-1, keepdims=True) + EPS)
        ang = pos_ref[...].astype(F32) * invf_ref[...]
        cos, sin = jnp.cos(ang), jnp.sin(ang)
        cos_ref[...] = cos
        sin_ref[...] = sin
        ckv = _dot((xhat * srcg_ref[...]).astype(BF16), wkva_ref[...])
        ckv_ref[...] = ckv
        cb = _rms_fwd(ckv[:, :KV_LORA], kvag_ref[...])[0].astype(BF16)
        kpe_ref[...] = _rope(ckv[:, KV_LORA:], cos, sin).astype(BF16)
        for hd in range(B_HEADS):
            kv = _dot(cb, wkvb_ref[hd])
            kn_ref[hd] = kv[:, :QK_NOPE].astype(BF16)
            v_ref[hd] = kv[:, QK_NOPE:].astype(BF16)
        cqpre = _dot((xhat * mixg_ref[...]).astype(BF16), wqa_ref[...])
        cqpre_ref[...] = cqpre
        cqb = _rms_fwd(cqpre, qg_ref[...])[0].astype(BF16)
        for hd in range(B_HEADS):
            q = _dot(cqb, wqb_ref[hd])
            q_ref[hd, :, 0:QK_NOPE] = q[:, :QK_NOPE].astype(BF16)
            q_ref[hd, :, QK_NOPE:] = _rope(q[:, QK_NOPE:], cos, sin).astype(BF16)

    return _call(
        "kvq_fwd", body, (t // TM,), [_row(D_MODEL), _row(1), _res((1, half)), *KVQ_W_SPECS],
        [_row(KV_LORA + QK_ROPE), _heads(QK_NOPE), _heads(V_HEAD), _row(QK_ROPE), _row(Q_LORA),
         _heads(QK_NOPE + QK_ROPE), _row(half), _row(half)],
        [_sds((t, KV_LORA + QK_ROPE), F32), _sds((B_HEADS, t, QK_NOPE), BF16), _sds((B_HEADS, t, V_HEAD), BF16),
         _sds((t, QK_ROPE), BF16), _sds((t, Q_LORA), F32), _sds((B_HEADS, t, QK_NOPE + QK_ROPE), BF16),
         _sds((t, half), F32), _sds((t, half), F32)],
        (h, pos, inv_freq, *kvq_w))[0]


def _softmax_rows(qn, qp, kn_ref, kpe_ref, k):
    past, upto = k * TM, (k + 1) * TM
    s = (_dot_nt(qn, kn_ref[0:upto, :]) + _dot_nt(qp, kpe_ref[0:upto, :])) * ATT_SCALE
    own = jnp.where(_att_mask(0, TM, TM), s[:, past:], jnp.finfo(F32).min)
    s = own if k == 0 else jnp.concatenate([s[:, :past], own], axis=1)
    e = jnp.exp(s - jnp.max(s, axis=-1, keepdims=True))
    return e * (1.0 / jnp.sum(e, axis=-1, keepdims=True))


def _for_my_tile(i, nq, fn):
    for k in range(nq):
        @pl.when(i == k)
        def _(k=k):
            fn(k)


def _attn_fwd(h, q, kn, kpe, v, w_o, comm=None):
    t = h.shape[0]
    nq = t // TM

    def body(h_ref, q_ref, kn_ref, kpe_ref, v_ref, wo_ref, o_ref, att_ref):
        i, hd = pl.program_id(0), pl.program_id(1)

        @pl.when(hd == 0)
        def _():
            o_ref[...] = h_ref[...]

        def tile(k):
            p = _softmax_rows(q_ref[:, 0:QK_NOPE], q_ref[:, QK_NOPE:], kn_ref, kpe_ref, k)
            ob = _dot(p.astype(BF16), v_ref[0:(k + 1) * TM, :]).astype(BF16)
            att_ref[...] = ob
            o_ref[...] += _dot(ob, wo_ref[...])

        _for_my_tile(i, nq, tile)

    def per_head(rows, d, tiled):
        return pl.BlockSpec((None, rows, d), (lambda i, hd: (hd, i, 0)) if tiled else (lambda i, hd: (hd, 0, 0)))

    tile_spec = pl.BlockSpec((TM, D_MODEL), lambda i, hd: (i, 0))
    return _call(
        "attn_fwd", body, (nq, B_HEADS),
        [tile_spec, per_head(TM, QK_NOPE + QK_ROPE, True), per_head(t, QK_NOPE, False),
         pl.BlockSpec((t, QK_ROPE), lambda i, hd: (0, 0)), per_head(t, V_HEAD, False),
         per_head(V_HEAD, D_MODEL, False)],
        [tile_spec, per_head(TM, V_HEAD, True)], [_sds((t, D_MODEL), F32), _sds((B_HEADS, t, V_HEAD), BF16)],
        (h, q, kn, kpe, v, w_o), comm=comm)


def _loss_head(h, g, target):
    t = h.shape[0]

    def body(h_ref, g_ref, t_ref, loss_ref, dh_ref, dg_ref):
        y, xhat, rstd = _rms_fwd(h_ref[...], g_ref[...])
        err = y - t_ref[...]
        part = 0.5 * jnp.sum(jnp.mean(err * err, axis=-1, keepdims=True), axis=0, keepdims=True)
        dx, dg = _rms_bwd(err * (1.0 / D_MODEL), xhat, rstd, g_ref[...])
        dh_ref[...] = dx
        _acc(dg_ref, dg)
        _acc(loss_ref, part)

    return _call(
        "loss_head", body, (t // TM,), [_row(D_MODEL), _res((1, D_MODEL)), _row(D_MODEL)],
        [_const((1, 1)), _row(D_MODEL), _const((1, D_MODEL))],
        [_sds((1, 1), F32), _sds((t, D_MODEL), F32), _sds((1, D_MODEL), F32)], (h, g, target))[0]


def _mlp_bwd(h, a, dho, g, w1, w2, layer, comm=None):
    t = h.shape[0]

    def body(h_ref, a_ref, dho_ref, g_ref, w1_ref, w2_ref, dhi_ref, dg_ref, hn_ref, f_ref, da_ref):
        gv = g_ref[...]
        y, xhat, rstd = _rms_fwd(h_ref[...], gv)
        hn_ref[...] = y.astype(BF16)
        dho_v = dho_ref[...]
        dhob = dho_v.astype(BF16)
        dhn = jnp.zeros((TM, D_MODEL), F32)
        for d in range(N_DEV):
            cs = slice(d * FF_SLOT, (d + 1) * FF_SLOT)
            r = jnp.maximum(a_ref[:, cs], 0.0)
            f_ref[:, cs] = (r * r).astype(BF16)
            da = (_dot_nt(dhob, w2_ref[d]) * (2.0 * r)).astype(BF16)
            da_ref[:, cs] = da
            dhn = dhn + _dot_nt(da, w1_ref[d])
        dx, dg = _rms_bwd(dhn, xhat, rstd, gv)
        dhi_ref[...] = dho_v + dx
        _acc(dg_ref, dg)

    return _call(
        f"mlp_bwd_{layer}", body, (t // TM,),
        [_row(D_MODEL), _row(D_FF), _row(D_MODEL), _res((1, D_MODEL)), *MLP_W_SPECS],
        [_row(D_MODEL), _const((1, D_MODEL)), _row(D_MODEL), _row(D_FF), _row(D_FF)],
        [_sds((t, D_MODEL), F32), _sds((1, D_MODEL), F32), _sds((t, D_MODEL), BF16), _sds((t, D_FF), BF16),
         _sds((t, D_FF), BF16)],
        (h, a, dho, g, w1, w2), comm=comm)


def _attn_bwd(dh, q, kn, kpe, v, w_o, cos, sin, comm=None):
    t = dh.shape[0]
    half = QK_ROPE // 2

    def body(dh_ref, q_ref, kn_ref, kpe_ref, v_ref, wo_ref, cos_ref, sin_ref, dq_ref, dkn_ref, dv_ref, dkpe_ref):
        hd, i = pl.program_id(0), pl.program_id(1)

        @pl.when(i == 0)
        def _():
            dkn_ref[...] = jnp.zeros_like(dkn_ref)
            dv_ref[...] = jnp.zeros_like(dv_ref)

        @pl.when((i == 0) & (hd == 0))
        def _():
            dkpe_ref[...] = jnp.zeros_like(dkpe_ref)

        def tile(k):
            keys = slice(0, (k + 1) * TM)
            qn, qp = q_ref[:, 0:QK_NOPE], q_ref[:, QK_NOPE:]
            do = _dot_nt(dh_ref[...].astype(BF16), wo_ref[...]).astype(BF16)
            p = _softmax_rows(qn, qp, kn_ref, kpe_ref, k)
            dp = _dot_nt(do, v_ref[keys, :])
            ds = (p * (dp - jnp.sum(p * dp, axis=-1, keepdims=True)) * ATT_SCALE).astype(BF16)
            dq_ref[:, 0:QK_NOPE] = _dot(ds, kn_ref[keys, :]).astype(BF16)
            dq_ref[:, QK_NOPE:] = _rope(_dot(ds, kpe_ref[keys, :]), cos_ref[...], -sin_ref[...]).astype(BF16)
            dkn_ref[keys, :] += _dot_tn(ds, qn)
            dv_ref[keys, :] += _dot_tn(p.astype(BF16), do)
            dkpe_ref[keys, :] += _dot_tn(ds, qp)

        _for_my_tile(i, t // TM, tile)

    def per_head(rows, d, tiled):
        return pl.BlockSpec((None, rows, d), (lambda hd, i: (hd, i, 0)) if tiled else (lambda hd, i: (hd, 0, 0)))

    def tile(d):
        return pl.BlockSpec((TM, d), lambda hd, i: (i, 0))

    return _call(
        "attn_bwd", body, (B_HEADS, t // TM),
        [tile(D_MODEL), per_head(TM, QK_NOPE + QK_ROPE, True), per_head(t, QK_NOPE, False),
         pl.BlockSpec((t, QK_ROPE), lambda hd, i: (0, 0)), per_head(t, V_HEAD, False),
         per_head(V_HEAD, D_MODEL, False), tile(half), tile(half)],
        [per_head(TM, QK_NOPE + QK_ROPE, True), per_head(t, QK_NOPE, False), per_head(t, V_HEAD, False),
         pl.BlockSpec((t, QK_ROPE), lambda hd, i: (0, 0))],
        [_sds((B_HEADS, t, QK_NOPE + QK_ROPE), BF16), _sds((B_HEADS, t, QK_NOPE), F32),
         _sds((B_HEADS, t, V_HEAD), F32), _sds((t, QK_ROPE), F32)],
        (dh, q, kn, kpe, v, w_o, cos, sin), comm=comm)


def _kvq_bwd(h, dh, ckv, cqpre, dq, dkn, dv, dkpe, cos, sin, kvq_w):
    t = h.shape[0]
    half = QK_ROPE // 2

    def body(h_ref, dh_ref, ckv_ref, cqpre_ref, dq_ref, dkn_ref, dv_ref, dkpe_ref, cos_ref, sin_ref,
             srcg_ref, wkva_ref, kvag_ref, wkvb_ref, mixg_ref, wqa_ref, qg_ref, wqb_ref,
             dhi_ref, hq_ref, hk_ref, cq_ref, dcqpre_ref, c_ref, dkv_ref, dckv_ref,
             dmixg_ref, dsrcg_ref, dqg_ref, dkvag_ref):
        hv = h_ref[...]
        rstd = lax.rsqrt(jnp.mean(hv * hv, axis=-1, keepdims=True) + EPS)
        xhat = hv * rstd
        mixg, srcg, qg, kvag = mixg_ref[...], srcg_ref[...], qg_ref[...], kvag_ref[...]
        hq_ref[...] = (xhat * mixg).astype(BF16)
        hk_ref[...] = (xhat * srcg).astype(BF16)
        cq, cqhat, crstd = _rms_fwd(cqpre_ref[...], qg)
        cq_ref[...] = cq.astype(BF16)
        dcq = jnp.zeros((TM, Q_LORA), F32)
        for hd in range(B_HEADS):
            dcq = dcq + _dot_nt(dq_ref[hd], wqb_ref[hd])
        dcqpre, dqg = _rms_bwd(dcq, cqhat, crstd, qg)
        dcqpre_b = dcqpre.astype(BF16)
        dcqpre_ref[...] = dcqpre_b
        dxq, dmixg = _rms_bwd(_dot_nt(dcqpre_b, wqa_ref[...]), xhat, rstd, mixg)
        ckv = ckv_ref[...]
        c, chat, krstd = _rms_fwd(ckv[:, :KV_LORA], kvag)
        c_ref[...] = c.astype(BF16)
        dc = jnp.zeros((TM, KV_LORA), F32)
        for hd in range(B_HEADS):
            dkv = jnp.concatenate([dkn_ref[hd], dv_ref[hd]], axis=-1).astype(BF16)
            dkv_ref[hd] = dkv
            dc = dc + _dot_nt(dkv, wkvb_ref[hd])
        dlat, dkvag = _rms_bwd(dc, chat, krstd, kvag)
        dpe = _rope(dkpe_ref[...], cos_ref[...], -sin_ref[...])
        dckv_b = jnp.concatenate([dlat, dpe], axis=-1).astype(BF16)
        dckv_ref[...] = dckv_b
        dxk, dsrcg = _rms_bwd(_dot_nt(dckv_b, wkva_ref[...]), xhat, rstd, srcg)
        dhi_ref[...] = dh_ref[...] + dxq + dxk
        _acc(dmixg_ref, dmixg)
        _acc(dsrcg_ref, dsrcg)
        _acc(dqg_ref, dqg)
        _acc(dkvag_ref, dkvag)

    return _call(
        "kvq_bwd", body, (t // TM,),
        [_row(D_MODEL), _row(D_MODEL), _row(KV_LORA + QK_ROPE), _row(Q_LORA), _heads(QK_NOPE + QK_ROPE),
         _heads(QK_NOPE), _heads(V_HEAD), _row(QK_ROPE), _row(half), _row(half), *KVQ_W_SPECS],
        [_row(D_MODEL), _row(D_MODEL), _row(D_MODEL), _row(Q_LORA), _row(Q_LORA), _row(KV_LORA),
         _heads(QK_NOPE + V_HEAD), _row(KV_LORA + QK_ROPE),
         _const((1, D_MODEL)), _const((1, D_MODEL)), _const((1, Q_LORA)), _const((1, KV_LORA))],
        [_sds((t, D_MODEL), F32), _sds((t, D_MODEL), BF16), _sds((t, D_MODEL), BF16), _sds((t, Q_LORA), BF16),
         _sds((t, Q_LORA), BF16), _sds((t, KV_LORA), BF16), _sds((B_HEADS, t, QK_NOPE + V_HEAD), BF16),
         _sds((t, KV_LORA + QK_ROPE), BF16),
         _sds((1, D_MODEL), F32), _sds((1, D_MODEL), F32), _sds((1, Q_LORA), F32), _sds((1, KV_LORA), F32)],
        (h, dh, ckv, cqpre, dq, dkn, dv, dkpe, cos, sin, *kvq_w))[0]


def _a_mix_bwd(x, z, dh, g, w_in, ln_g, ln_b, w_s, b_st, w_out, comm=None):
    t = x.shape[0]
    tm = TM_GATE
    nblk = tm // GMLP_BLOCK

    def body(x_ref, z_ref, dh_ref, g_ref, win_ref, lng_ref, lnb_ref, ws_ref, bst_ref, wout_ref,
             dx_ref, hn_ref, dz_ref, dg_ref, dlng_ref, dlnb_ref, dws_ref, dbs_ref, du_scr, dvn_scr):
        @pl.when(pl.program_id(0) == 0)
        def _():
            dws_ref[...] = jnp.zeros_like(dws_ref)
            dbs_ref[...] = jnp.zeros_like(dbs_ref)

        gv, lng = g_ref[...], lng_ref[...]
        y, xhat, rstd = _rms_fwd(x_ref[...], gv)
        hn_ref[...] = y.astype(BF16)
        dhv = dh_ref[...]
        dgated = _dot_nt(dhv.astype(BF16), wout_ref[...])
        u = _gelu(z_ref[:, :GATE_DIM])
        vn, vhat, lrstd = _ln_fwd(_gelu(z_ref[:, GATE_DIM:]), lng, lnb_ref[...])
        vb = vn.astype(BF16)
        mask = _gate_mask()
        for gi in range(A_GROUPS):
            wm = jnp.where(mask, ws_ref[gi], 0.0).astype(BF16)
            bias = bst_ref[:, gi:gi + 1]
            cs = slice(gi * A_GROUP_DIM, (gi + 1) * A_GROUP_DIM)
            dws = jnp.zeros((GMLP_BLOCK, GMLP_BLOCK), F32)
            dbs = jnp.zeros((GMLP_BLOCK, 1), F32)
            for n in range(nblk):
                rs = slice(n * GMLP_BLOCK, (n + 1) * GMLP_BLOCK)
                sv = _dot(wm, vb[rs, cs]) + bias
                du_scr[rs, cs] = dgated[rs, cs] * sv
                dsv = dgated[rs, cs] * u[rs, cs]
                dsvb = dsv.astype(BF16)
                dws = dws + _dot_nt(dsvb, vb[rs, cs])
                dbs = dbs + jnp.sum(dsv, axis=-1, keepdims=True)
                dvn_scr[rs, cs] = _dot_tn(wm, dsvb)
            dws_ref[gi] += jnp.where(mask, dws, 0.0)
            dbs_ref[gi] += dbs
        dvn = dvn_scr[...]
        dvhat = dvn * lng
        dv = lrstd * (dvhat - jnp.mean(dvhat, axis=-1, keepdims=True)
                      - vhat * jnp.mean(dvhat * vhat, axis=-1, keepdims=True))
        dz_ref[:, :GATE_DIM] = (du_scr[...] * _gelu_grad(z_ref[:, :GATE_DIM])).astype(BF16)
        dz_ref[:, GATE_DIM:] = (dv * _gelu_grad(z_ref[:, GATE_DIM:])).astype(BF16)
        dhn = jnp.zeros((tm, D_MODEL), F32)
        for d in range(N_DEV):
            dhn = dhn + _dot_nt(dz_ref[:, d * FF_SLOT:(d + 1) * FF_SLOT], win_ref[d])
        dx, dg = _rms_bwd(dhn, xhat, rstd, gv)
        dx_ref[...] = dhv + dx
        _acc(dg_ref, dg)
        _acc(dlng_ref, jnp.sum(dvn * vhat, axis=0, keepdims=True))
        _acc(dlnb_ref, jnp.sum(dvn, axis=0, keepdims=True))

    return _call(
        "a_mix_bwd", body, (t // tm,),
        [_row(D_MODEL, tm), _row(2 * GATE_DIM, tm), _row(D_MODEL, tm), _res((1, D_MODEL)),
         _res((N_DEV, D_MODEL, FF_SLOT)), _res((1, GATE_DIM)), _res((1, GATE_DIM)),
         _res((A_GROUPS, GMLP_BLOCK, GMLP_BLOCK)), _res((GMLP_BLOCK, A_GROUPS)), _res((GATE_DIM, D_MODEL))],
        [_row(D_MODEL, tm), _row(D_MODEL, tm), _row(2 * GATE_DIM, tm),
         _const((1, D_MODEL)), _const((1, GATE_DIM)), _const((1, GATE_DIM)),
         _const((A_GROUPS, GMLP_BLOCK, GMLP_BLOCK)), _const((A_GROUPS, GMLP_BLOCK, 1))],
        [_sds((t, D_MODEL), F32), _sds((t, D_MODEL), BF16),
         _sds((t, 2 * GATE_DIM), BF16), _sds((1, D_MODEL), F32), _sds((1, GATE_DIM), F32),
         _sds((1, GATE_DIM), F32), _sds((A_GROUPS, GMLP_BLOCK, GMLP_BLOCK), F32),
         _sds((A_GROUPS, GMLP_BLOCK, 1), F32)],
        (x, z, dh, g, w_in, ln_g, ln_b, w_s, b_st, w_out),
        scratch=[pltpu.VMEM((tm, GATE_DIM), F32), pltpu.VMEM((tm, GATE_DIM), F32)], comm=comm)


def _wgrad(name, a, b, a_spec, b_spec, m, n, comm=None):
    def body(a_ref, b_ref, o_ref):
        o_ref[0] = _dot_tn(a_ref[...].astype(BF16), b_ref[...].astype(BF16)).astype(BF16)

    outs, got = _call(name, body, (N_DEV,), [a_spec, b_spec], [pl.BlockSpec((1, m, n), lambda d: (d, 0, 0))],
                      [_sds((N_DEV, m, n), BF16)], (a, b), comm=comm)
    return outs[0] if comm is None else (outs[0], got)


def _full(t, d):
    return pl.BlockSpec((t, d), lambda i: (0, 0), pipeline_mode=pl.Buffered(1))


def _cols(t, d):
    return pl.BlockSpec((t, d), lambda i: (0, i))


def _head(t, d):
    return pl.BlockSpec((None, t, d), lambda i: (i, 0, 0))


def _local_step(x, pos, target, inv_freq, wg, sm, shards=None):
    t = x.shape[0]
    wg = dict(wg)
    dist = shards is not None
    mix_g = [sm["norm_mix_g"][l:l + 1] for l in range(2)]
    mlp_g = [sm["norm_mlp_g"][l:l + 1] for l in range(2)]

    def gather(names):
        return _gather_comm([shards[k] for k in names]) if dist else None

    def send(grads):
        return _exchange_comm(grads=grads) if dist else None

    def send_sums(name, grads):
        return _chip_exchange_comm(_pair_reduce(name, grads)) if dist else None

    def a_args():
        return (wg["a_w_in"], wg["a_ln_v_g"], wg["a_ln_v_b"], sm["a_w_s"], sm["a_b_st"], wg["a_w_out"])

    def kvq_w():
        return (sm["kv_src_norm_g"], wg["kv_w_a"], sm["kv_a_norm_g"], wg["kv_w_b"], mix_g[1], wg["b_w_q_a"],
                sm["b_q_norm_g"], wg["b_w_q_b"])

    names = ("mlp_w1_0", "mlp_w2_0")
    (h1, z, gated), got = _a_mix_fwd(x, mix_g[0], *a_args(), comm=gather(names))
    wg.update(zip(names, got))
    names = ("kv_w_a", "kv_w_b", "b_w_q_a", "b_w_q_b", "b_w_o")
    (h2, a0), got = _mlp_fwd(h1, mlp_g[0], wg["mlp_w1_0"], wg["mlp_w2_0"], 0, comm=gather(names))
    wg.update(zip(names, got))
    if dist:
        wg["b_w_q_a"] = wg["b_w_q_a"].reshape(D_MODEL, Q_LORA)
        wg["kv_w_a"] = wg["kv_w_a"].reshape(D_MODEL, KV_LORA + QK_ROPE)
    ckv, kn, v, kpe, cqpre, q, cos, sin = _kvq_fwd(h2, pos, inv_freq, kvq_w())
    names = ("mlp_w1_1", "mlp_w2_1")
    (h3, att), got = _attn_fwd(h2, q, kn, kpe, v, wg["b_w_o"], comm=gather(names))
    wg.update(zip(names, got))
    (h4, a1), _ = _mlp_fwd(h3, mlp_g[1], wg["mlp_w1_1"], wg["mlp_w2_1"], 1)
    loss, dh4, d_final_g = _loss_head(h4, sm["final_norm_g"], target)

    g = {}
    (dh3, d_mlp_g1, hn, f, da), _ = _mlp_bwd(h3, a1, dh4, mlp_g[1], wg["mlp_w1_1"], wg["mlp_w2_1"], 1)
    g["mlp_w1_1"] = _wgrad("wgrad_w1_1", hn, da, _full(t, D_MODEL), _cols(t, FF_SLOT), D_MODEL, FF_SLOT)
    g["mlp_w2_1"] = _wgrad("wgrad_w2_1", f, dh4, _cols(t, FF_SLOT), _full(t, D_MODEL), FF_SLOT, D_MODEL)
    g["b_w_o"] = _wgrad("wgrad_w_o", att, dh3, _head(t, V_HEAD), _full(t, D_MODEL), V_HEAD, D_MODEL)
    names = ("mlp_w1_1", "mlp_w2_1", "b_w_o")
    (dq, dkn, dv, dkpe), got = _attn_bwd(dh3, q, kn, kpe, v, wg["b_w_o"], cos, sin,
                                         comm=send_sums("pair_reduce_1", [g[k] for k in names]))
    g.update(zip(names, got))
    (dh2, hq, hk, cq, dcqpre, c, dkv, dckv, d_mix_g1, d_src_g, d_q_g, d_kv_a_g) = _kvq_bwd(
        h2, dh3, ckv, cqpre, dq, dkn, dv, dkpe, cos, sin, kvq_w())
    g["b_w_q_a"] = _wgrad("wgrad_w_q_a", hq, dcqpre, _cols(t, D_MODEL // N_DEV), _full(t, Q_LORA),
                          D_MODEL // N_DEV, Q_LORA)
    g["b_w_q_b"] = _wgrad("wgrad_w_q_b", cq, dq, _full(t, Q_LORA), _head(t, QK_NOPE + QK_ROPE),
                          Q_LORA, QK_NOPE + QK_ROPE)
    g["kv_w_a"] = _wgrad("wgrad_kv_w_a", hk, dckv, _cols(t, D_MODEL // N_DEV), _full(t, KV_LORA + QK_ROPE),
                         D_MODEL // N_DEV, KV_LORA + QK_ROPE)
    g["kv_w_b"] = _wgrad("wgrad_kv_w_b", c, dkv, _full(t, KV_LORA), _head(t, QK_NOPE + V_HEAD),
                         KV_LORA, QK_NOPE + V_HEAD)
    names = ("b_w_q_a", "b_w_q_b", "kv_w_a", "kv_w_b")
    (dh1, d_mlp_g0, hn, f, da), got = _mlp_bwd(h1, a0, dh2, mlp_g[0], wg["mlp_w1_0"], wg["mlp_w2_0"], 0,
                                               comm=send([g[k] for k in names]))
    g.update(zip(names, got))
    g["mlp_w1_0"] = _wgrad("wgrad_w1_0", hn, da, _full(t, D_MODEL), _cols(t, FF_SLOT), D_MODEL, FF_SLOT)
    g["mlp_w2_0"] = _wgrad("wgrad_w2_0", f, dh2, _cols(t, FF_SLOT), _full(t, D_MODEL), FF_SLOT, D_MODEL)
    g["a_w_out"] = _wgrad("wgrad_a_w_out", gated, dh1, _cols(t, GATE_DIM // N_DEV), _full(t, D_MODEL),
                          GATE_DIM // N_DEV, D_MODEL)
    names = ("mlp_w1_0", "mlp_w2_0", "a_w_out")
    (dx, hn, dz, d_mix_g0, d_ln_g, d_ln_b, d_ws, d_bs), got = _a_mix_bwd(
        x, z, dh1, mix_g[0], *a_args(), comm=send_sums("pair_reduce_0", [g[k] for k in names]))
    g.update(zip(names, got))
    small = {
        "norm_mix_g": jnp.concatenate([d_mix_g0, d_mix_g1], axis=0),
        "norm_mlp_g": jnp.concatenate([d_mlp_g0, d_mlp_g1], axis=0),
        "a_ln_v_g": d_ln_g.reshape(N_DEV, GATE_DIM // N_DEV),
        "a_ln_v_b": d_ln_b.reshape(N_DEV, GATE_DIM // N_DEV),
        "a_w_s": d_ws.astype(BF16) if dist else d_ws,
        "a_b_s": d_bs.reshape(A_GROUPS, GMLP_BLOCK),
        "b_q_norm_g": d_q_g,
        "kv_src_norm_g": d_src_g,
        "kv_a_norm_g": d_kv_a_g,
        "final_norm_g": d_final_g,
    }
    wgrad_in = ("wgrad_a_w_in", hn, dz, _full(t, D_MODEL), _cols(t, FF_SLOT), D_MODEL, FF_SLOT)
    if dist:
        parts = [small[k].reshape((1,) + small[k].shape) for k in SMALL]
        g["a_w_in"], got = _wgrad(*wgrad_in, comm=_exchange_comm(parts=parts))
        small = dict(zip(SMALL, got))
    else:
        g["a_w_in"] = _wgrad(*wgrad_in)
    return loss, dx, g, small


def _adamw(w, g, m, v):
    m = ADAM_B1 * m + (1.0 - ADAM_B1) * g
    v = ADAM_B2 * v + (1.0 - ADAM_B2) * (g * g)
    m_hat = m / (1.0 - ADAM_B1 ** ADAM_STEP)
    v_hat = v / (1.0 - ADAM_B2 ** ADAM_STEP)
    return -ADAM_LR * (m_hat / (jnp.sqrt(v_hat) + ADAM_EPS) + ADAM_WD * w), m, v


def _sum_in_device_order(r_ref):
    g = r_ref[0].astype(F32)
    for j in range(1, r_ref.shape[0]):
        g = g + r_ref[j].astype(F32)
    return g


def _adamw_sharded(name, recv, w, m, v, layer, into=None):
    layers, r, c = w.shape
    tr = math.gcd(r, 256)

    def body(r_ref, w_ref, m_ref, v_ref, *refs):
        g_ref, d_ref, nm_ref, nv_ref = refs[-4:]
        g = _sum_in_device_order(r_ref)
        g_ref[...] = g
        d_ref[...], nm_ref[...], nv_ref[...] = _adamw(w_ref[...], g, m_ref[...], v_ref[...])

    blk = pl.BlockSpec((None, tr, c), lambda i: (layer, i, 0))
    args, in_specs, aliases = [recv, w, m, v], [pl.BlockSpec((recv.shape[0], tr, c), lambda i: (0, i, 0)), blk, blk, blk], {}
    if into is not None:
        args += list(into)
        in_specs += [ANY] * 4
        aliases = {4 + i: i for i in range(4)}
    return pl.pallas_call(
        body, name=name, grid=(r // tr,), in_specs=in_specs, out_specs=[blk] * 4,
        out_shape=[_sds(w.shape, F32)] * 4, input_output_aliases=aliases,
        compiler_params=pltpu.CompilerParams(dimension_semantics=("arbitrary",), vmem_limit_bytes=VMEM_LIMIT),
    )(*args)


def _adamw_small(recvs, ws, ms, vs, own_row):
    n = len(recvs)

    def body(*refs):
        r_refs, w_refs, m_refs, v_refs = (refs[i * n:(i + 1) * n] for i in range(4))
        outs, scr = refs[4 * n:8 * n], refs[8 * n:]
        me = _my_place()[3]
        for a in range(n):
            g = _sum_in_device_order(r_refs[a])
            if own_row[a]:
                scr[0][...] = g
                g = scr[0][pl.ds(me, 1), :]
            g_ref, d_ref, nm_ref, nv_ref = outs[4 * a:4 * a + 4]
            g_ref[...] = g
            d_ref[...], nm_ref[...], nv_ref[...] = _adamw(w_refs[a][...], g, m_refs[a][...], v_refs[a][...])

    out_shape = []
    for w in ws:
        out_shape += [_sds(w.shape, F32)] * 4
    return pl.pallas_call(
        body, name="adamw_small", in_specs=[VMEM] * (4 * n), out_specs=[VMEM] * (4 * n), out_shape=out_shape,
        scratch_shapes=[pltpu.VMEM((N_DEV, GATE_DIM // N_DEV), F32)],
    )(*recvs, *ws, *ms, *vs)


BIG = ("a_w_in", "a_w_out", "b_w_q_a", "b_w_q_b", "b_w_o", "kv_w_a", "kv_w_b", "mlp_w1", "mlp_w2")
SMALL = ("norm_mix_g", "norm_mlp_g", "a_ln_v_g", "a_ln_v_b", "a_w_s", "a_b_s", "b_q_norm_g", "kv_src_norm_g",
         "kv_a_norm_g", "final_norm_g")
WEIGHTS = ("norm_mix_g", "norm_mlp_g", "a_w_in", "a_ln_v_g", "a_ln_v_b", "a_w_s", "a_b_s", "a_w_out", "b_w_q_a",
           "b_q_norm_g", "b_w_q_b", "b_w_o", "kv_src_norm_g", "kv_w_a", "kv_a_norm_g", "kv_w_b", "mlp_w1", "mlp_w2",
           "final_norm_g")


def _two_d(name, a):
    if name in ("a_w_s", "a_b_s"):
        return a.reshape(a.shape[1:])
    return a.reshape(1, -1) if a.ndim == 1 else a


def _three_d(a):
    return a if a.ndim == 3 else a.reshape((1,) + a.shape)


def kernel(x, positions, norm_mix_g, norm_mlp_g, a_w_in, a_ln_v_g, a_ln_v_b, a_w_s, a_b_s, a_w_out, b_w_q_a, b_q_norm_g, b_w_q_b, b_w_o, kv_src_norm_g, kv_w_a, kv_a_norm_g, kv_w_b, mlp_w1, mlp_w2, final_norm_g, loss_target, m_norm_mix_g, m_norm_mlp_g, m_a_w_in, m_a_ln_v_g, m_a_ln_v_b, m_a_w_s, m_a_b_s, m_a_w_out, m_b_w_q_a, m_b_q_norm_g, m_b_w_q_b, m_b_w_o, m_kv_src_norm_g, m_kv_w_a, m_kv_a_norm_g, m_kv_w_b, m_mlp_w1, m_mlp_w2, m_final_norm_g, v_norm_mix_g, v_norm_mlp_g, v_a_w_in, v_a_ln_v_g, v_a_ln_v_b, v_a_w_s, v_a_b_s, v_a_w_out, v_b_w_q_a, v_b_q_norm_g, v_b_w_q_b, v_b_w_o, v_kv_src_norm_g, v_kv_w_a, v_kv_a_norm_g, v_kv_w_b, v_mlp_w1, v_mlp_w2, v_final_norm_g):
    w = dict(norm_mix_g=norm_mix_g, norm_mlp_g=norm_mlp_g, a_w_in=a_w_in, a_ln_v_g=a_ln_v_g, a_ln_v_b=a_ln_v_b,
             a_w_s=a_w_s, a_b_s=a_b_s, a_w_out=a_w_out, b_w_q_a=b_w_q_a, b_q_norm_g=b_q_norm_g, b_w_q_b=b_w_q_b,
             b_w_o=b_w_o, kv_src_norm_g=kv_src_norm_g, kv_w_a=kv_w_a, kv_a_norm_g=kv_a_norm_g, kv_w_b=kv_w_b,
             mlp_w1=mlp_w1, mlp_w2=mlp_w2, final_norm_g=final_norm_g)
    m = dict(norm_mix_g=m_norm_mix_g, norm_mlp_g=m_norm_mlp_g, a_w_in=m_a_w_in, a_ln_v_g=m_a_ln_v_g,
             a_ln_v_b=m_a_ln_v_b, a_w_s=m_a_w_s, a_b_s=m_a_b_s, a_w_out=m_a_w_out, b_w_q_a=m_b_w_q_a,
             b_q_norm_g=m_b_q_norm_g, b_w_q_b=m_b_w_q_b, b_w_o=m_b_w_o, kv_src_norm_g=m_kv_src_norm_g,
             kv_w_a=m_kv_w_a, kv_a_norm_g=m_kv_a_norm_g, kv_w_b=m_kv_w_b, mlp_w1=m_mlp_w1, mlp_w2=m_mlp_w2,
             final_norm_g=m_final_norm_g)
    v = dict(norm_mix_g=v_norm_mix_g, norm_mlp_g=v_norm_mlp_g, a_w_in=v_a_w_in, a_ln_v_g=v_a_ln_v_g,
             a_ln_v_b=v_a_ln_v_b, a_w_s=v_a_w_s, a_b_s=v_a_b_s, a_w_out=v_a_w_out, b_w_q_a=v_b_w_q_a,
             b_q_norm_g=v_b_q_norm_g, b_w_q_b=v_b_w_q_b, b_w_o=v_b_w_o, kv_src_norm_g=v_kv_src_norm_g,
             kv_w_a=v_kv_w_a, kv_a_norm_g=v_kv_a_norm_g, kv_w_b=v_kv_w_b, mlp_w1=v_mlp_w1, mlp_w2=v_mlp_w2,
             final_norm_g=v_final_norm_g)
    t = x.shape[1]

    first = ("a_w_in", "a_w_out", "a_ln_v_g", "a_ln_v_b")
    later = ("mlp_w1_0", "mlp_w2_0", "mlp_w1_1", "mlp_w2_1", "kv_w_a", "kv_w_b", "b_w_q_a", "b_w_q_b", "b_w_o")
    blocks = {k: _three_d(w[k]) for k in BIG if not k.startswith("mlp")}
    for k in ("mlp_w1", "mlp_w2"):
        blocks[k + "_0"], blocks[k + "_1"] = w[k][0:1], w[k][1:2]
    got, casts = _gather_first([blocks[k] if k in blocks else w[k] for k in first], [blocks[k] for k in later])
    wg = dict(zip(first, got))
    wg["a_w_out"] = wg["a_w_out"].reshape(GATE_DIM, D_MODEL)
    wg["a_ln_v_g"] = wg["a_ln_v_g"].reshape(1, GATE_DIM)
    wg["a_ln_v_b"] = wg["a_ln_v_b"].reshape(1, GATE_DIM)
    shards = dict(zip(later, casts))

    sm = {k: _two_d(k, w[k]) for k in SMALL if k not in ("a_ln_v_g", "a_ln_v_b")}
    sm["a_b_st"] = sm["a_b_s"].T
    inv_freq = (ROPE_THETA ** (-jnp.arange(0, QK_ROPE, 2, dtype=F32) / QK_ROPE)).reshape(1, QK_ROPE // 2)

    loss, dx, g, small = _local_step(x[0], positions.reshape(t, 1), loss_target[0], inv_freq, wg, sm, shards)
    loss = lax.psum(loss[0, 0], ("x", "y", "c"))

    g["a_w_in"], = _comm_only("exchange_last", _chip_exchange_comm(_pair_reduce("pair_reduce_a", [g["a_w_in"]])))
    small_recv = [small[k] for k in SMALL]

    out = {}
    for k in BIG:
        if k.startswith("mlp"):
            args = (_three_d(w[k]), _three_d(m[k]), _three_d(v[k]))
            res = _adamw_sharded(f"adamw_{k}_1", g[k + "_1"], *args, 1)
            res = _adamw_sharded(f"adamw_{k}_0", g[k + "_0"], *args, 0, into=res)
        else:
            res = _adamw_sharded("adamw_" + k, g[k], _three_d(w[k]), _three_d(m[k]), _three_d(v[k]), 0)
        out[k] = [o.reshape(w[k].shape) for o in res]
    own_row = [k in ("a_ln_v_g", "a_ln_v_b") for k in SMALL]
    res = _adamw_small(small_recv, [_two_d(k, w[k]) for k in SMALL], [_two_d(k, m[k]) for k in SMALL],
                       [_two_d(k, v[k]) for k in SMALL], own_row)
    for i, k in enumerate(SMALL):
        out[k] = [o.reshape(w[k].shape) for o in res[4 * i:4 * i + 4]]

    return (loss, dx.reshape(x.shape), *[out[k][0] for k in WEIGHTS], *[out[k][1] for k in WEIGHTS],
            *[out[k][2] for k in WEIGHTS], *[out[k][3] for k in WEIGHTS])
```

```python
import math

import jax
import jax.numpy as jnp
from jax import lax
from jax.experimental import pallas as pl
from jax.experimental.pallas import tpu as pltpu

F32, BF16 = jnp.float32, jnp.bfloat16
MESH = pl.DeviceIdType.MESH
ANY = pl.BlockSpec(memory_space=pl.ANY)
VMEM = pl.BlockSpec(memory_space=pltpu.VMEM)

N_DEV = 8
D_MODEL = 1024
CHUNK = 64
GMLP_BLOCK = 128
GATE_DIM = 2048
A_GROUPS = 8
A_GROUP_DIM = GATE_DIM // A_GROUPS
B_HEADS = 8
QK_NOPE, QK_ROPE, V_HEAD = 128, 64, 128
Q_LORA, KV_LORA = 384, 256
ROPE_THETA = 10000.0
D_FF = 4096
FF_SLOT = D_FF // N_DEV
EPS = 1e-6
ATT_SCALE = (QK_NOPE + QK_ROPE) ** -0.5

ADAM_LR, ADAM_B1, ADAM_B2, ADAM_EPS, ADAM_WD, ADAM_STEP = 0.001, 0.9, 0.999, 1e-08, 0.01, 10

TM = 256
TM_GATE = 128
VMEM_LIMIT = 56 * 1024 * 1024
INV_SQRT2 = 1.0 / math.sqrt(2.0)
INV_SQRT_2PI = 1.0 / math.sqrt(2.0 * math.pi)


def _dot(a, b):
    return jnp.dot(a, b, preferred_element_type=F32)


def _dot_nt(a, b):
    return lax.dot_general(a, b, (((1,), (1,)), ((), ())), preferred_element_type=F32)


def _dot_tn(a, b):
    return lax.dot_general(a, b, (((0,), (0,)), ((), ())), preferred_element_type=F32)


def _rms_fwd(x, g):
    rstd = lax.rsqrt(jnp.mean(x * x, axis=-1, keepdims=True) + EPS)
    xhat = x * rstd
    return xhat * g, xhat, rstd


def _rms_bwd(dy, xhat, rstd, g):
    dxhat = dy * g
    dx = rstd * (dxhat - xhat * jnp.mean(dxhat * xhat, axis=-1, keepdims=True))
    return dx, jnp.sum(dy * xhat, axis=0, keepdims=True)


def _ln_fwd(v, g, b):
    mu = jnp.mean(v, axis=-1, keepdims=True)
    vc = v - mu
    rstd = lax.rsqrt(jnp.mean(vc * vc, axis=-1, keepdims=True) + EPS)
    vhat = vc * rstd
    return vhat * g + b, vhat, rstd


def _gelu(x):
    return 0.5 * x * (1.0 + lax.erf(x * INV_SQRT2))


def _gelu_grad(x):
    return 0.5 * (1.0 + lax.erf(x * INV_SQRT2)) + x * jnp.exp(-0.5 * x * x) * INV_SQRT_2PI


def _rope(x, cos, sin):
    x1, x2 = x[:, :QK_ROPE // 2], x[:, QK_ROPE // 2:]
    return jnp.concatenate([x1 * cos - x2 * sin, x2 * cos + x1 * sin], axis=-1)


def _gate_mask():
    row = lax.broadcasted_iota(jnp.int32, (GMLP_BLOCK, GMLP_BLOCK), 0)
    col = lax.broadcasted_iota(jnp.int32, (GMLP_BLOCK, GMLP_BLOCK), 1)
    return (col < CHUNK) | (row >= CHUNK)


def _att_mask(q0, tq, t):
    q = q0 + lax.broadcasted_iota(jnp.int32, (tq, t), 0)
    k = lax.broadcasted_iota(jnp.int32, (tq, t), 1)
    return jnp.right_shift(k, 6) <= jnp.right_shift(q, 6)


def _res(shape, imap=None):
    zeros = (0,) * len(shape)
    return pl.BlockSpec(shape, imap or (lambda i: zeros), pipeline_mode=pl.Buffered(1))


def _const(shape):
    zeros = (0,) * len(shape)
    return pl.BlockSpec(shape, lambda i: zeros)


def _row(d, tm=TM):
    return pl.BlockSpec((tm, d), lambda i: (i, 0))


def _heads(d):
    return pl.BlockSpec((B_HEADS, TM, d), lambda i: (0, i, 0))


def _sds(shape, dt):
    return jax.ShapeDtypeStruct(shape, dt)


def _acc(ref, val):
    @pl.when(pl.program_id(0) == 0)
    def _():
        ref[...] = jnp.zeros_like(ref)
    ref[...] += val


def _my_place():
    x, y, c = lax.axis_index("x"), lax.axis_index("y"), lax.axis_index("c")
    return x, y, c, 4 * x + 2 * y + c


def _peer(x, y, c, k):
    px = 1 - x if k & 4 else x
    py = 1 - y if k & 2 else y
    pc = 1 - c if k & 1 else c
    return (px, py, pc), 4 * px + 2 * py + pc


CHIPS = (2, 4, 6)


def _gather_copy(outs, send_sems, recv_sems, a, k, block, to, src=None):
    rows = outs[a].at[pl.ds(block, 1)]
    return pltpu.make_async_remote_copy(
        src_ref=rows if src is None else src, dst_ref=rows, send_sem=send_sems.at[a, k], recv_sem=recv_sems.at[a, k],
        device_id=to, device_id_type=MESH)


def _gather_start(srcs, outs, sems, only=None):
    send_sems, recv_sems, local_sems = sems
    x, y, c, me = _my_place()
    for a in range(len(srcs)) if only is None else (only,):
        pltpu.make_async_copy(srcs[a], outs[a].at[pl.ds(me, 1)], local_sems.at[a]).start()
        _gather_copy(outs, send_sems, recv_sems, a, 0, me, _peer(x, y, c, 1)[0], src=srcs[a]).start()
        for j, k in enumerate(CHIPS):
            _gather_copy(outs, send_sems, recv_sems, a, 1 + j, me, _peer(x, y, c, k)[0], src=srcs[a]).start()


def _gather_finish(srcs, outs, sems):
    send_sems, recv_sems, local_sems = sems
    x, y, c, me = _my_place()
    sib, sib_i = _peer(x, y, c, 1)
    n = len(srcs)
    for a in range(n):
        for j, k in enumerate(CHIPS):
            block = _peer(x, y, c, k)[1]
            _gather_copy(outs, send_sems, recv_sems, a, 1 + j, block, sib).wait_recv()
            _gather_copy(outs, send_sems, recv_sems, a, 4 + j, block, sib).start()
    for a in range(n):
        _gather_copy(outs, send_sems, recv_sems, a, 0, sib_i, sib).wait_recv()
        for j, k in enumerate(CHIPS):
            _gather_copy(outs, send_sems, recv_sems, a, 4 + j, _peer(x, y, c, k ^ 1)[1], sib).wait_recv()
    for a in range(n):
        for k in range(7):
            _gather_copy(outs, send_sems, recv_sems, a, k, me, sib, src=srcs[a] if k < 4 else None).wait_send()
        pltpu.make_async_copy(srcs[a], outs[a].at[pl.ds(me, 1)], local_sems.at[a]).wait()


def _gather_sems(n):
    return [pltpu.SemaphoreType.DMA((n, 7)), pltpu.SemaphoreType.DMA((n, 7)), pltpu.SemaphoreType.DMA((n,))]


class _Comm:
    def __init__(self, args, out_shape, scratch, start, finish):
        self.args, self.out_shape, self.scratch, self.start, self.finish = args, out_shape, scratch, start, finish


def _gather_comm(shards):
    return _Comm(list(shards), [_sds((N_DEV,) + s.shape[1:], s.dtype) for s in shards], _gather_sems(len(shards)),
                 _gather_start, _gather_finish)


def _direct_copies(ins, outs, sems, wait, from_block):
    send_sems, recv_sems, local_sems = sems
    x, y, c, me = _my_place()
    for a in range(len(ins)):
        src = ins[a].at[pl.ds(me, 1)] if from_block[a] else ins[a]
        local = pltpu.make_async_copy(src, outs[a].at[pl.ds(me, 1)], local_sems.at[a])
        local.wait() if wait else local.start()
        for k in range(1, N_DEV):
            to, to_i = _peer(x, y, c, k)
            cp = pltpu.make_async_remote_copy(
                src_ref=ins[a].at[pl.ds(to_i, 1)] if from_block[a] else ins[a], dst_ref=outs[a].at[pl.ds(me, 1)],
                send_sem=send_sems.at[a, k - 1], recv_sem=recv_sems.at[a, k - 1], device_id=to, device_id_type=MESH)
            cp.wait() if wait else cp.start()


def _exchange_comm(grads=(), parts=()):
    ins = list(grads) + list(parts)
    from_block = [True] * len(grads) + [False] * len(parts)
    out_shape = [_sds(g.shape, g.dtype) for g in grads] + [_sds((N_DEV,) + p.shape[1:], p.dtype) for p in parts]

    def start(ins_, outs_, sems_):
        _direct_copies(ins_, outs_, sems_, False, from_block)

    def finish(ins_, outs_, sems_):
        _direct_copies(ins_, outs_, sems_, True, from_block)

    return _Comm(ins, out_shape, _gather_sems(len(ins)), start, finish)


def _chip_copies(ins, outs, sems, wait):
    send_sems, recv_sems, local_sems = sems
    x, y, c, _ = _my_place()
    my_chip = 2 * x + y
    for a in range(len(ins)):
        local = pltpu.make_async_copy(ins[a].at[pl.ds(my_chip, 1)], outs[a].at[pl.ds(my_chip, 1)], local_sems.at[a])
        local.wait() if wait else local.start()
        for j, k in enumerate(CHIPS):
            to = _peer(x, y, c, k)[0]
            cp = pltpu.make_async_remote_copy(
                src_ref=ins[a].at[pl.ds(2 * to[0] + to[1], 1)], dst_ref=outs[a].at[pl.ds(my_chip, 1)],
                send_sem=send_sems.at[a, j], recv_sem=recv_sems.at[a, j], device_id=to, device_id_type=MESH)
            cp.wait() if wait else cp.start()


def _chip_exchange_comm(sums):
    def start(ins_, outs_, sems_):
        _chip_copies(ins_, outs_, sems_, False)

    def finish(ins_, outs_, sems_):
        _chip_copies(ins_, outs_, sems_, True)

    n = len(sums)
    sems = [pltpu.SemaphoreType.DMA((n, 3)), pltpu.SemaphoreType.DMA((n, 3)), pltpu.SemaphoreType.DMA((n,))]
    return _Comm(list(sums), [_sds(s.shape, s.dtype) for s in sums], sems, start, finish)


def _join(c1, c2):
    ni, no, ns = len(c1.args), len(c1.out_shape), len(c1.scratch)

    def start(i, o, s):
        c1.start(i[:ni], o[:no], s[:ns])
        c2.start(i[ni:], o[no:], s[ns:])

    def finish(i, o, s):
        c1.finish(i[:ni], o[:no], s[:ns])
        c2.finish(i[ni:], o[no:], s[ns:])

    return _Comm(c1.args + c2.args, c1.out_shape + c2.out_shape, c1.scratch + c2.scratch, start, finish)


def _pair_reduce(name, grads):
    n = len(grads)
    n_chips = N_DEV // 2

    def body(*refs):
        g_refs, gh_refs, p_refs, land = refs[:n], refs[n:2 * n], refs[2 * n:3 * n], refs[3 * n:4 * n]
        send_sems, recv_sems = refs[4 * n:]
        x, y, c, _ = _my_place()
        sib = _peer(x, y, c, 1)[0]
        q = pl.program_id(0)

        def to_sibling(a, j):
            return pltpu.make_async_remote_copy(
                src_ref=gh_refs[a].at[j, pl.ds(1 - c, 1)], dst_ref=land[a].at[pl.ds(j, 1)],
                send_sem=send_sems.at[a, j], recv_sem=recv_sems.at[a, j], device_id=sib, device_id_type=MESH)

        @pl.when(q == 0)
        def _():
            for j in range(n_chips):
                for a in range(n):
                    to_sibling(a, j).start()

        for a in range(n):
            to_sibling(a, q).wait_recv()
            p_refs[a][...] = (g_refs[a][0, pl.ds(c, 1)].astype(F32) + land[a][pl.ds(q, 1)].astype(F32)).astype(BF16)

        @pl.when(q == n_chips - 1)
        def _():
            for a in range(n):
                for j in range(n_chips):
                    to_sibling(a, j).wait_send()

    views = [g.reshape((n_chips, 2) + g.shape[1:]) for g in grads]
    res = pl.pallas_call(
        body, name=name, grid=(n_chips,),
        in_specs=[pl.BlockSpec((1, 2) + g.shape[1:], lambda q: (q, 0, 0, 0)) for g in grads] + [ANY] * n,
        out_specs=[pl.BlockSpec((1,) + g.shape[1:], lambda q: (q, 0, 0)) for g in grads],
        out_shape=[_sds((n_chips,) + g.shape[1:], BF16) for g in grads],
        scratch_shapes=[pltpu.VMEM((n_chips,) + g.shape[1:], BF16) for g in grads]
        + [pltpu.SemaphoreType.DMA((n, n_chips)), pltpu.SemaphoreType.DMA((n, n_chips))],
        compiler_params=pltpu.CompilerParams(dimension_semantics=("arbitrary",), vmem_limit_bytes=VMEM_LIMIT),
    )(*views, *views)
    return list(res)


def _call(name, body, grid, in_specs, out_specs, out_shape, args, scratch=(), comm=None):
    params = pltpu.CompilerParams(dimension_semantics=("arbitrary",) * len(grid), vmem_limit_bytes=VMEM_LIMIT)
    if comm is None:
        outs = pl.pallas_call(body, name=name, grid=grid, in_specs=list(in_specs), out_specs=list(out_specs),
                              out_shape=list(out_shape), scratch_shapes=list(scratch), compiler_params=params)(*args)
        return list(outs), []
    ni, nci, no, nco, ns = len(in_specs), len(comm.args), len(out_specs), len(comm.out_shape), len(scratch)

    def carrying(*refs):
        ins, refs = refs[:ni], refs[ni:]
        cin, refs = refs[:nci], refs[nci:]
        outs, refs = refs[:no], refs[no:]
        cout, refs = refs[:nco], refs[nco:]
        scr, csems = refs[:ns], refs[ns:]
        ids = [pl.program_id(ax) for ax in range(len(grid))]
        first, last = ids[0] == 0, ids[0] == grid[0] - 1
        for ax in range(1, len(grid)):
            first, last = first & (ids[ax] == 0), last & (ids[ax] == grid[ax] - 1)

        @pl.when(first)
        def _():
            comm.start(cin, cout, csems)

        body(*ins, *outs, *scr)

        @pl.when(last)
        def _():
            comm.finish(cin, cout, csems)

    outs = pl.pallas_call(
        carrying, name=name, grid=grid, in_specs=list(in_specs) + [ANY] * nci, out_specs=list(out_specs) + [ANY] * nco,
        out_shape=list(out_shape) + list(comm.out_shape), scratch_shapes=list(scratch) + list(comm.scratch),
        compiler_params=params)(*args, *comm.args)
    return list(outs[:no]), list(outs[no:])


def _comm_only(name, comm):
    def body(*refs):
        nci, nco = len(comm.args), len(comm.out_shape)
        cin, cout, csems = refs[:nci], refs[nci:nci + nco], refs[nci + nco:]
        comm.start(cin, cout, csems)
        comm.finish(cin, cout, csems)

    return pl.pallas_call(body, name=name, in_specs=[ANY] * len(comm.args), out_specs=[ANY] * len(comm.out_shape),
                          out_shape=list(comm.out_shape), scratch_shapes=list(comm.scratch))(*comm.args)


def _gather_first(first, later):
    nf, nl = len(first), len(later)
    dts = [BF16] * (nf - 2) + [F32, F32]

    def body(*refs):
        ins, refs = refs[:nf + nl], refs[nf + nl:]
        outs, refs = refs[:nf], refs[nf:]
        casts, refs = refs[:nl], refs[nl:]
        stage, sems = refs[:nf], refs[nf:]
        for a in range(nf):
            stage[a][...] = ins[a][...].astype(dts[a])
            _gather_start(stage, outs, sems, only=a)
        for a in range(nl):
            casts[a][...] = ins[nf + a][...].astype(BF16)
        _gather_finish(stage, outs, sems)

    res = pl.pallas_call(
        body, name="gather_first",
        in_specs=[VMEM] * (nf + nl), out_specs=[ANY] * nf + [VMEM] * nl,
        out_shape=[_sds((N_DEV,) + s.shape[1:], dt) for s, dt in zip(first, dts)]
        + [_sds(s.shape, BF16) for s in later],
        scratch_shapes=[pltpu.VMEM(s.shape, dt) for s, dt in zip(first, dts)] + _gather_sems(nf),
        compiler_params=pltpu.CompilerParams(vmem_limit_bytes=VMEM_LIMIT),
    )(*first, *later)
    return list(res[:nf]), list(res[nf:])


def _a_mix_fwd(x, g, w_in, ln_g, ln_b, w_s, b_st, w_out, comm=None):
    t = x.shape[0]
    nblk = TM // GMLP_BLOCK

    def body(x_ref, g_ref, win_ref, lng_ref, lnb_ref, ws_ref, bst_ref, wout_ref, h_ref, z_ref, gated_scr):
        xv = x_ref[...]
        hb = _rms_fwd(xv, g_ref[...])[0].astype(BF16)
        for d in range(N_DEV):
            z_ref[:, d * FF_SLOT:(d + 1) * FF_SLOT] = _dot(hb, win_ref[d])
        u = _gelu(z_ref[:, :GATE_DIM])
        vb = _ln_fwd(_gelu(z_ref[:, GATE_DIM:]), lng_ref[...], lnb_ref[...])[0].astype(BF16)
        mask = _gate_mask()
        for gi in range(A_GROUPS):
            wm = jnp.where(mask, ws_ref[gi], 0.0).astype(BF16)
            bias = bst_ref[:, gi:gi + 1]
            cs = slice(gi * A_GROUP_DIM, (gi + 1) * A_GROUP_DIM)
            for n in range(nblk):
                rs = slice(n * GMLP_BLOCK, (n + 1) * GMLP_BLOCK)
                sv = _dot(wm, vb[rs, cs]) + bias
                gated_scr[rs, cs] = (u[rs, cs] * sv).astype(BF16)
        h_ref[...] = xv + _dot(gated_scr[...], wout_ref[...])

    return _call(
        "a_mix_fwd", body, (t // TM,),
        [_row(D_MODEL), _res((1, D_MODEL)), _res((N_DEV, D_MODEL, FF_SLOT)), _res((1, GATE_DIM)),
         _res((1, GATE_DIM)), _res((A_GROUPS, GMLP_BLOCK, GMLP_BLOCK)), _res((GMLP_BLOCK, A_GROUPS)),
         _res((GATE_DIM, D_MODEL))],
        [_row(D_MODEL), _row(2 * GATE_DIM), _row(GATE_DIM)],
        [_sds((t, D_MODEL), F32), _sds((t, 2 * GATE_DIM), F32), _sds((t, GATE_DIM), BF16)],
        (x, g, w_in, ln_g, ln_b, w_s, b_st, w_out), comm=comm)


MLP_W_SPECS = (_res((N_DEV, D_MODEL, FF_SLOT)), _res((N_DEV, FF_SLOT, D_MODEL)))


def _mlp_fwd(h, g, w1, w2, layer, comm=None):
    t = h.shape[0]

    def body(h_ref, g_ref, w1_ref, w2_ref, o_ref, a_ref):
        hv = h_ref[...]
        hb = _rms_fwd(hv, g_ref[...])[0].astype(BF16)
        o_ref[...] = hv
        for d in range(N_DEV):
            a = _dot(hb, w1_ref[d])
            a_ref[:, d * FF_SLOT:(d + 1) * FF_SLOT] = a
            r = jnp.maximum(a, 0.0)
            o_ref[...] += _dot((r * r).astype(BF16), w2_ref[d])

    return _call(
        f"mlp_fwd_{layer}", body, (t // TM,), [_row(D_MODEL), _res((1, D_MODEL)), *MLP_W_SPECS],
        [_row(D_MODEL), _row(D_FF)], [_sds((t, D_MODEL), F32), _sds((t, D_FF), F32)], (h, g, w1, w2), comm=comm)


KVQ_W_SPECS = (_res((1, D_MODEL)), _res((D_MODEL, KV_LORA + QK_ROPE)), _res((1, KV_LORA)),
               _res((B_HEADS, KV_LORA, QK_NOPE + V_HEAD)), _res((1, D_MODEL)), _res((D_MODEL, Q_LORA)),
               _res((1, Q_LORA)), _res((B_HEADS, Q_LORA, QK_NOPE + QK_ROPE)))


def _kvq_fwd(h, pos, inv_freq, kvq_w):
    t = h.shape[0]
    half = QK_ROPE // 2

    def body(h_ref, pos_ref, invf_ref, srcg_ref, wkva_ref, kvag_ref, wkvb_ref, mixg_ref, wqa_ref, qg_ref, wqb_ref,
             ckv_ref, kn_ref, v_ref, kpe_ref, cqpre_ref, q_ref, cos_ref, sin_ref):
        hv = h_ref[...]
        xhat = hv * lax.rsqrt(jnp.mean(hv * hv, axis=-1, keepdims=True) + EPS)
        ang = pos_ref[...].astype(F32) * invf_ref[...]
        cos, sin = jnp.cos(ang), jnp.sin(ang)
        cos_ref[...] = cos
        sin_ref[...] = sin
        ckv = _dot((xhat * srcg_ref[...]).astype(BF16), wkva_ref[...])
        ckv_ref[...] = ckv
        cb = _rms_fwd(ckv[:, :KV_LORA], kvag_ref[...])[0].astype(BF16)
        kpe_ref[...] = _rope(ckv[:, KV_LORA:], cos, sin).astype(BF16)
        for hd in range(B_HEADS):
            kv = _dot(cb, wkvb_ref[hd])
            kn_ref[hd] = kv[:, :QK_NOPE].astype(BF16)
            v_ref[hd] = kv[:, QK_NOPE:].astype(BF16)
        cqpre = _dot((xhat * mixg_ref[...]).astype(BF16), wqa_ref[...])
        cqpre_ref[...] = cqpre
        cqb = _rms_fwd(cqpre, qg_ref[...])[0].astype(BF16)
        for hd in range(B_HEADS):
            q = _dot(cqb, wqb_ref[hd])
            q_ref[hd, :, 0:QK_NOPE] = q[:, :QK_NOPE].astype(BF16)
            q_ref[hd, :, QK_NOPE:] = _rope(q[:, QK_NOPE:], cos, sin).astype(BF16)

    return _call(
        "kvq_fwd", body, (t // TM,), [_row(D_MODEL), _row(1), _res((1, half)), *KVQ_W_SPECS],
        [_row(KV_LORA + QK_ROPE), _heads(QK_NOPE), _heads(V_HEAD), _row(QK_ROPE), _row(Q_LORA),
         _heads(QK_NOPE + QK_ROPE), _row(half), _row(half)],
        [_sds((t, KV_LORA + QK_ROPE), F32), _sds((B_HEADS, t, QK_NOPE), BF16), _sds((B_HEADS, t, V_HEAD), BF16),
         _sds((t, QK_ROPE), BF16), _sds((t, Q_LORA), F32), _sds((B_HEADS, t, QK_NOPE + QK_ROPE), BF16),
         _sds((t, half), F32), _sds((t, half), F32)],
        (h, pos, inv_freq, *kvq_w))[0]


def _softmax_rows(qn, qp, kn_ref, kpe_ref, k):
    past, upto = k * TM, (k + 1) * TM
    s = (_dot_nt(qn, kn_ref[0:upto, :]) + _dot_nt(qp, kpe_ref[0:upto, :])) * ATT_SCALE
    own = jnp.where(_att_mask(0, TM, TM), s[:, past:], jnp.finfo(F32).min)
    s = own if k == 0 else jnp.concatenate([s[:, :past], own], axis=1)
    e = jnp.exp(s - jnp.max(s, axis=-1, keepdims=True))
    return e * (1.0 / jnp.sum(e, axis=-1, keepdims=True))


def _for_my_tile(i, nq, fn):
    for k in range(nq):
        @pl.when(i == k)
        def _(k=k):
            fn(k)


def _attn_fwd(h, q, kn, kpe, v, w_o, comm=None):
    t = h.shape[0]
    nq = t // TM

    def body(h_ref, q_ref, kn_ref, kpe_ref, v_ref, wo_ref, o_ref, att_ref):
        i, hd = pl.program_id(0), pl.program_id(1)

        @pl.when(hd == 0)
        def _():
            o_ref[...] = h_ref[...]

        def tile(k):
            p = _softmax_rows(q_ref[:, 0:QK_NOPE], q_ref[:, QK_NOPE:], kn_ref.at[hd], kpe_ref, k)
            ob = _dot(p.astype(BF16), v_ref[hd, 0:(k + 1) * TM, :]).astype(BF16)
            att_ref[...] = ob
            o_ref[...] += _dot(ob, wo_ref[hd])

        _for_my_tile(i, nq, tile)

    def per_head(d):
        return pl.BlockSpec((None, TM, d), lambda i, hd: (hd, i, 0))

    def resident(shape):
        zeros = (0,) * len(shape)
        return pl.BlockSpec(shape, lambda i, hd: zeros, pipeline_mode=pl.Buffered(1))

    tile_spec = pl.BlockSpec((TM, D_MODEL), lambda i, hd: (i, 0))
    return _call(
        "attn_fwd", body, (nq, B_HEADS),
        [tile_spec, per_head(QK_NOPE + QK_ROPE), resident((B_HEADS, t, QK_NOPE)), resident((t, QK_ROPE)),
         resident((B_HEADS, t, V_HEAD)), resident((B_HEADS, V_HEAD, D_MODEL))],
        [tile_spec, per_head(V_HEAD)], [_sds((t, D_MODEL), F32), _sds((B_HEADS, t, V_HEAD), BF16)],
        (h, q, kn, kpe, v, w_o), comm=comm)


def _loss_head(h, g, target):
    t = h.shape[0]

    def body(h_ref, g_ref, t_ref, loss_ref, dh_ref, dg_ref):
        y, xhat, rstd = _rms_fwd(h_ref[...], g_ref[...])
        err = y - t_ref[...]
        part = 0.5 * jnp.sum(jnp.mean(err * err, axis=-1, keepdims=True), axis=0, keepdims=True)
        dx, dg = _rms_bwd(err * (1.0 / D_MODEL), xhat, rstd, g_ref[...])
        dh_ref[...] = dx
        _acc(dg_ref, dg)
        _acc(loss_ref, part)

    return _call(
        "loss_head", body, (t // TM,), [_row(D_MODEL), _res((1, D_MODEL)), _row(D_MODEL)],
        [_const((1, 1)), _row(D_MODEL), _const((1, D_MODEL))],
        [_sds((1, 1), F32), _sds((t, D_MODEL), F32), _sds((1, D_MODEL), F32)], (h, g, target))[0]


def _mlp_bwd(h, a, dho, g, w1, w2, layer, comm=None):
    t = h.shape[0]

    def body(h_ref, a_ref, dho_ref, g_ref, w1_ref, w2_ref, dhi_ref, dg_ref, hn_ref, f_ref, da_ref, dhib_ref):
        gv = g_ref[...]
        y, xhat, rstd = _rms_fwd(h_ref[...], gv)
        hn_ref[...] = y.astype(BF16)
        dho_v = dho_ref[...]
        dhob = dho_v.astype(BF16)
        dhn = jnp.zeros((TM, D_MODEL), F32)
        for d in range(N_DEV):
            cs = slice(d * FF_SLOT, (d + 1) * FF_SLOT)
            r = jnp.maximum(a_ref[:, cs], 0.0)
            f_ref[:, cs] = (r * r).astype(BF16)
            da = (_dot_nt(dhob, w2_ref[d]) * (2.0 * r)).astype(BF16)
            da_ref[:, cs] = da
            dhn = dhn + _dot_nt(da, w1_ref[d])
        dx, dg = _rms_bwd(dhn, xhat, rstd, gv)
        dhi = dho_v + dx
        dhi_ref[...] = dhi
        dhib_ref[...] = dhi.astype(BF16)
        _acc(dg_ref, dg)

    return _call(
        f"mlp_bwd_{layer}", body, (t // TM,),
        [_row(D_MODEL), _row(D_FF), _row(D_MODEL), _res((1, D_MODEL)), *MLP_W_SPECS],
        [_row(D_MODEL), _const((1, D_MODEL)), _row(D_MODEL), _row(D_FF), _row(D_FF), _row(D_MODEL)],
        [_sds((t, D_MODEL), F32), _sds((1, D_MODEL), F32), _sds((t, D_MODEL), BF16), _sds((t, D_FF), BF16),
         _sds((t, D_FF), BF16), _sds((t, D_MODEL), BF16)],
        (h, a, dho, g, w1, w2), comm=comm)


def _attn_bwd(dh, q, kn, kpe, v, w_o, cos, sin, comm=None):
    t = dh.shape[0]
    half = QK_ROPE // 2

    def body(dh_ref, q_ref, kn_ref, kpe_ref, v_ref, wo_ref, cos_ref, sin_ref, dq_ref, dkn_ref, dv_ref, dkpe_ref):
        hd, i = pl.program_id(0), pl.program_id(1)

        @pl.when(i == 0)
        def _():
            dkn_ref[...] = jnp.zeros_like(dkn_ref)
            dv_ref[...] = jnp.zeros_like(dv_ref)

        @pl.when((i == 0) & (hd == 0))
        def _():
            dkpe_ref[...] = jnp.zeros_like(dkpe_ref)

        def tile(k):
            keys = slice(0, (k + 1) * TM)
            qn, qp = q_ref[:, 0:QK_NOPE], q_ref[:, QK_NOPE:]
            do = _dot_nt(dh_ref[k * TM:(k + 1) * TM, :], wo_ref[...]).astype(BF16)
            p = _softmax_rows(qn, qp, kn_ref, kpe_ref, k)
            dp = _dot_nt(do, v_ref[keys, :])
            ds = (p * (dp - jnp.sum(p * dp, axis=-1, keepdims=True)) * ATT_SCALE).astype(BF16)
            dq_ref[:, 0:QK_NOPE] = _dot(ds, kn_ref[keys, :]).astype(BF16)
            dq_ref[:, QK_NOPE:] = _rope(_dot(ds, kpe_ref[keys, :]), cos_ref[...], -sin_ref[...]).astype(BF16)
            dkn_ref[keys, :] += _dot_tn(ds, qn)
            dv_ref[keys, :] += _dot_tn(p.astype(BF16), do)
            dkpe_ref[keys, :] += _dot_tn(ds, qp)

        _for_my_tile(i, t // TM, tile)

    def per_head(rows, d, tiled):
        return pl.BlockSpec((None, rows, d), (lambda hd, i: (hd, i, 0)) if tiled else (lambda hd, i: (hd, 0, 0)))

    def tile(d):
        return pl.BlockSpec((TM, d), lambda hd, i: (i, 0))

    return _call(
        "attn_bwd", body, (B_HEADS, t // TM),
        [pl.BlockSpec((t, D_MODEL), lambda hd, i: (0, 0), pipeline_mode=pl.Buffered(1)),
         per_head(TM, QK_NOPE + QK_ROPE, True), per_head(t, QK_NOPE, False),
         pl.BlockSpec((t, QK_ROPE), lambda hd, i: (0, 0)), per_head(t, V_HEAD, False),
         per_head(V_HEAD, D_MODEL, False), tile(half), tile(half)],
        [per_head(TM, QK_NOPE + QK_ROPE, True), per_head(t, QK_NOPE, False), per_head(t, V_HEAD, False),
         pl.BlockSpec((t, QK_ROPE), lambda hd, i: (0, 0))],
        [_sds((B_HEADS, t, QK_NOPE + QK_ROPE), BF16), _sds((B_HEADS, t, QK_NOPE), F32),
         _sds((B_HEADS, t, V_HEAD), F32), _sds((t, QK_ROPE), F32)],
        (dh, q, kn, kpe, v, w_o, cos, sin), comm=comm)


def _kvq_bwd(h, dh, ckv, cqpre, dq, dkn, dv, dkpe, cos, sin, kvq_w):
    t = h.shape[0]
    half = QK_ROPE // 2

    def body(h_ref, dh_ref, ckv_ref, cqpre_ref, dq_ref, dkn_ref, dv_ref, dkpe_ref, cos_ref, sin_ref,
             srcg_ref, wkva_ref, kvag_ref, wkvb_ref, mixg_ref, wqa_ref, qg_ref, wqb_ref,
             dhi_ref, hq_ref, hk_ref, cq_ref, dcqpre_ref, c_ref, dkv_ref, dckv_ref,
             dmixg_ref, dsrcg_ref, dqg_ref, dkvag_ref):
        hv = h_ref[...]
        rstd = lax.rsqrt(jnp.mean(hv * hv, axis=-1, keepdims=True) + EPS)
        xhat = hv * rstd
        mixg, srcg, qg, kvag = mixg_ref[...], srcg_ref[...], qg_ref[...], kvag_ref[...]
        hq_ref[...] = (xhat * mixg).astype(BF16)
        hk_ref[...] = (xhat * srcg).astype(BF16)
        cq, cqhat, crstd = _rms_fwd(cqpre_ref[...], qg)
        cq_ref[...] = cq.astype(BF16)
        dcq = jnp.zeros((TM, Q_LORA), F32)
        for hd in range(B_HEADS):
            dcq = dcq + _dot_nt(dq_ref[hd], wqb_ref[hd])
        dcqpre, dqg = _rms_bwd(dcq, cqhat, crstd, qg)
        dcqpre_b = dcqpre.astype(BF16)
        dcqpre_ref[...] = dcqpre_b
        dxq, dmixg = _rms_bwd(_dot_nt(dcqpre_b, wqa_ref[...]), xhat, rstd, mixg)
        ckv = ckv_ref[...]
        c, chat, krstd = _rms_fwd(ckv[:, :KV_LORA], kvag)
        c_ref[...] = c.astype(BF16)
        dc = jnp.zeros((TM, KV_LORA), F32)
        for hd in range(B_HEADS):
            dkv = jnp.concatenate([dkn_ref[hd], dv_ref[hd]], axis=-1).astype(BF16)
            dkv_ref[hd] = dkv
            dc = dc + _dot_nt(dkv, wkvb_ref[hd])
        dlat, dkvag = _rms_bwd(dc, chat, krstd, kvag)
        dpe = _rope(dkpe_ref[...], cos_ref[...], -sin_ref[...])
        dckv_b = jnp.concatenate([dlat, dpe], axis=-1).astype(BF16)
        dckv_ref[...] = dckv_b
        dxk, dsrcg = _rms_bwd(_dot_nt(dckv_b, wkva_ref[...]), xhat, rstd, srcg)
        dhi_ref[...] = dh_ref[...] + dxq + dxk
        _acc(dmixg_ref, dmixg)
        _acc(dsrcg_ref, dsrcg)
        _acc(dqg_ref, dqg)
        _acc(dkvag_ref, dkvag)

    return _call(
        "kvq_bwd", body, (t // TM,),
        [_row(D_MODEL), _row(D_MODEL), _row(KV_LORA + QK_ROPE), _row(Q_LORA), _heads(QK_NOPE + QK_ROPE),
         _heads(QK_NOPE), _heads(V_HEAD), _row(QK_ROPE), _row(half), _row(half), *KVQ_W_SPECS],
        [_row(D_MODEL), _row(D_MODEL), _row(D_MODEL), _row(Q_LORA), _row(Q_LORA), _row(KV_LORA),
         _heads(QK_NOPE + V_HEAD), _row(KV_LORA + QK_ROPE),
         _const((1, D_MODEL)), _const((1, D_MODEL)), _const((1, Q_LORA)), _const((1, KV_LORA))],
        [_sds((t, D_MODEL), F32), _sds((t, D_MODEL), BF16), _sds((t, D_MODEL), BF16), _sds((t, Q_LORA), BF16),
         _sds((t, Q_LORA), BF16), _sds((t, KV_LORA), BF16), _sds((B_HEADS, t, QK_NOPE + V_HEAD), BF16),
         _sds((t, KV_LORA + QK_ROPE), BF16),
         _sds((1, D_MODEL), F32), _sds((1, D_MODEL), F32), _sds((1, Q_LORA), F32), _sds((1, KV_LORA), F32)],
        (h, dh, ckv, cqpre, dq, dkn, dv, dkpe, cos, sin, *kvq_w))[0]


def _a_mix_bwd(x, z, dh, g, w_in, ln_g, ln_b, w_s, b_st, w_out, comm=None):
    t = x.shape[0]
    tm = TM_GATE
    nblk = tm // GMLP_BLOCK

    def body(x_ref, z_ref, dh_ref, g_ref, win_ref, lng_ref, lnb_ref, ws_ref, bst_ref, wout_ref,
             dx_ref, hn_ref, dz_ref, dg_ref, dlng_ref, dlnb_ref, dws_ref, dbs_ref, du_scr, dvn_scr):
        @pl.when(pl.program_id(0) == 0)
        def _():
            dws_ref[...] = jnp.zeros_like(dws_ref)
            dbs_ref[...] = jnp.zeros_like(dbs_ref)

        gv, lng = g_ref[...], lng_ref[...]
        y, xhat, rstd = _rms_fwd(x_ref[...], gv)
        hn_ref[...] = y.astype(BF16)
        dhv = dh_ref[...]
        dgated = _dot_nt(dhv.astype(BF16), wout_ref[...])
        u = _gelu(z_ref[:, :GATE_DIM])
        vn, vhat, lrstd = _ln_fwd(_gelu(z_ref[:, GATE_DIM:]), lng, lnb_ref[...])
        vb = vn.astype(BF16)
        mask = _gate_mask()
        for gi in range(A_GROUPS):
            wm = jnp.where(mask, ws_ref[gi], 0.0).astype(BF16)
            bias = bst_ref[:, gi:gi + 1]
            cs = slice(gi * A_GROUP_DIM, (gi + 1) * A_GROUP_DIM)
            dws = jnp.zeros((GMLP_BLOCK, GMLP_BLOCK), F32)
            dbs = jnp.zeros((GMLP_BLOCK, 1), F32)
            for n in range(nblk):
                rs = slice(n * GMLP_BLOCK, (n + 1) * GMLP_BLOCK)
                sv = _dot(wm, vb[rs, cs]) + bias
                du_scr[rs, cs] = dgated[rs, cs] * sv
                dsv = dgated[rs, cs] * u[rs, cs]
                dsvb = dsv.astype(BF16)
                dws = dws + _dot_nt(dsvb, vb[rs, cs])
                dbs = dbs + jnp.sum(dsv, axis=-1, keepdims=True)
                dvn_scr[rs, cs] = _dot_tn(wm, dsvb)
            dws_ref[gi] += jnp.where(mask, dws, 0.0)
            dbs_ref[gi] += dbs
        dvn = dvn_scr[...]
        dvhat = dvn * lng
        dv = lrstd * (dvhat - jnp.mean(dvhat, axis=-1, keepdims=True)
                      - vhat * jnp.mean(dvhat * vhat, axis=-1, keepdims=True))
        dz_ref[:, :GATE_DIM] = (du_scr[...] * _gelu_grad(z_ref[:, :GATE_DIM])).astype(BF16)
        dz_ref[:, GATE_DIM:] = (dv * _gelu_grad(z_ref[:, GATE_DIM:])).astype(BF16)
        dhn = jnp.zeros((tm, D_MODEL), F32)
        for d in range(N_DEV):
            dhn = dhn + _dot_nt(dz_ref[:, d * FF_SLOT:(d + 1) * FF_SLOT], win_ref[d])
        dx, dg = _rms_bwd(dhn, xhat, rstd, gv)
        dx_ref[...] = dhv + dx
        _acc(dg_ref, dg)
        _acc(dlng_ref, jnp.sum(dvn * vhat, axis=0, keepdims=True))
        _acc(dlnb_ref, jnp.sum(dvn, axis=0, keepdims=True))

    return _call(
        "a_mix_bwd", body, (t // tm,),
        [_row(D_MODEL, tm), _row(2 * GATE_DIM, tm), _row(D_MODEL, tm), _res((1, D_MODEL)),
         _res((N_DEV, D_MODEL, FF_SLOT)), _res((1, GATE_DIM)), _res((1, GATE_DIM)),
         _res((A_GROUPS, GMLP_BLOCK, GMLP_BLOCK)), _res((GMLP_BLOCK, A_GROUPS)), _res((GATE_DIM, D_MODEL))],
        [_row(D_MODEL, tm), _row(D_MODEL, tm), _row(2 * GATE_DIM, tm),
         _const((1, D_MODEL)), _const((1, GATE_DIM)), _const((1, GATE_DIM)),
         _const((A_GROUPS, GMLP_BLOCK, GMLP_BLOCK)), _const((A_GROUPS, GMLP_BLOCK, 1))],
        [_sds((t, D_MODEL), F32), _sds((t, D_MODEL), BF16),
         _sds((t, 2 * GATE_DIM), BF16), _sds((1, D_MODEL), F32), _sds((1, GATE_DIM), F32),
         _sds((1, GATE_DIM), F32), _sds((A_GROUPS, GMLP_BLOCK, GMLP_BLOCK), F32),
         _sds((A_GROUPS, GMLP_BLOCK, 1), F32)],
        (x, z, dh, g, w_in, ln_g, ln_b, w_s, b_st, w_out),
        scratch=[pltpu.VMEM((tm, GATE_DIM), F32), pltpu.VMEM((tm, GATE_DIM), F32)], comm=comm)


def _wgrad(name, a, b, a_spec, b_spec, m, n, comm=None):
    def body(a_ref, b_ref, o_ref):
        o_ref[0] = _dot_tn(a_ref[...].astype(BF16), b_ref[...].astype(BF16)).astype(BF16)

    outs, got = _call(name, body, (N_DEV,), [a_spec, b_spec], [pl.BlockSpec((1, m, n), lambda d: (d, 0, 0))],
                      [_sds((N_DEV, m, n), BF16)], (a, b), comm=comm)
    return outs[0] if comm is None else (outs[0], got)


def _full(t, d):
    return pl.BlockSpec((t, d), lambda i: (0, 0), pipeline_mode=pl.Buffered(1))


def _cols(t, d):
    return pl.BlockSpec((t, d), lambda i: (0, i))


def _head(t, d):
    return pl.BlockSpec((None, t, d), lambda i: (i, 0, 0))


def _local_step(x, pos, target, inv_freq, wg, sm, shards=None):
    t = x.shape[0]
    wg = dict(wg)
    dist = shards is not None
    mix_g = [sm["norm_mix_g"][l:l + 1] for l in range(2)]
    mlp_g = [sm["norm_mlp_g"][l:l + 1] for l in range(2)]

    def gather(names):
        return _gather_comm([shards[k] for k in names]) if dist else None

    def send(grads):
        return _exchange_comm(grads=grads) if dist else None

    def send_sums(name, grads):
        return _chip_exchange_comm(_pair_reduce(name, grads)) if dist else None

    def a_args():
        return (wg["a_w_in"], wg["a_ln_v_g"], wg["a_ln_v_b"], sm["a_w_s"], sm["a_b_st"], wg["a_w_out"])

    def kvq_w():
        return (sm["kv_src_norm_g"], wg["kv_w_a"], sm["kv_a_norm_g"], wg["kv_w_b"], mix_g[1], wg["b_w_q_a"],
                sm["b_q_norm_g"], wg["b_w_q_b"])

    names = ("mlp_w1_0", "mlp_w2_0")
    (h1, z, gated), got = _a_mix_fwd(x, mix_g[0], *a_args(), comm=gather(names))
    wg.update(zip(names, got))
    names = ("kv_w_a", "kv_w_b", "b_w_q_a", "b_w_q_b", "b_w_o")
    (h2, a0), got = _mlp_fwd(h1, mlp_g[0], wg["mlp_w1_0"], wg["mlp_w2_0"], 0, comm=gather(names))
    wg.update(zip(names, got))
    if dist:
        wg["b_w_q_a"] = wg["b_w_q_a"].reshape(D_MODEL, Q_LORA)
        wg["kv_w_a"] = wg["kv_w_a"].reshape(D_MODEL, KV_LORA + QK_ROPE)
    ckv, kn, v, kpe, cqpre, q, cos, sin = _kvq_fwd(h2, pos, inv_freq, kvq_w())
    names = ("mlp_w1_1", "mlp_w2_1")
    (h3, att), got = _attn_fwd(h2, q, kn, kpe, v, wg["b_w_o"], comm=gather(names))
    wg.update(zip(names, got))
    (h4, a1), _ = _mlp_fwd(h3, mlp_g[1], wg["mlp_w1_1"], wg["mlp_w2_1"], 1)
    loss, dh4, d_final_g = _loss_head(h4, sm["final_norm_g"], target)

    g = {}
    (dh3, d_mlp_g1, hn, f, da, dh3_b), _ = _mlp_bwd(h3, a1, dh4, mlp_g[1], wg["mlp_w1_1"], wg["mlp_w2_1"], 1)
    g["mlp_w1_1"] = _wgrad("wgrad_w1_1", hn, da, _full(t, D_MODEL), _cols(t, FF_SLOT), D_MODEL, FF_SLOT)
    g["mlp_w2_1"] = _wgrad("wgrad_w2_1", f, dh4, _cols(t, FF_SLOT), _full(t, D_MODEL), FF_SLOT, D_MODEL)
    g["b_w_o"] = _wgrad("wgrad_w_o", att, dh3_b, _head(t, V_HEAD), _full(t, D_MODEL), V_HEAD, D_MODEL)
    names = ("mlp_w1_1", "mlp_w2_1", "b_w_o")
    (dq, dkn, dv, dkpe), got = _attn_bwd(dh3_b, q, kn, kpe, v, wg["b_w_o"], cos, sin,
                                         comm=send_sums("pair_reduce_1", [g[k] for k in names]))
    g.update(zip(names, got))
    (dh2, hq, hk, cq, dcqpre, c, dkv, dckv, d_mix_g1, d_src_g, d_q_g, d_kv_a_g) = _kvq_bwd(
        h2, dh3, ckv, cqpre, dq, dkn, dv, dkpe, cos, sin, kvq_w())
    g["b_w_q_a"] = _wgrad("wgrad_w_q_a", hq, dcqpre, _cols(t, D_MODEL // N_DEV), _full(t, Q_LORA),
                          D_MODEL // N_DEV, Q_LORA)
    g["b_w_q_b"] = _wgrad("wgrad_w_q_b", cq, dq, _full(t, Q_LORA), _head(t, QK_NOPE + QK_ROPE),
                          Q_LORA, QK_NOPE + QK_ROPE)
    g["kv_w_a"] = _wgrad("wgrad_kv_w_a", hk, dckv, _cols(t, D_MODEL // N_DEV), _full(t, KV_LORA + QK_ROPE),
                         D_MODEL // N_DEV, KV_LORA + QK_ROPE)
    g["kv_w_b"] = _wgrad("wgrad_kv_w_b", c, dkv, _full(t, KV_LORA), _head(t, QK_NOPE + V_HEAD),
                         KV_LORA, QK_NOPE + V_HEAD)
    names = ("b_w_q_a", "b_w_q_b", "kv_w_a", "kv_w_b")
    (dh1, d_mlp_g0, hn, f, da, dh1_b), got = _mlp_bwd(h1, a0, dh2, mlp_g[0], wg["mlp_w1_0"], wg["mlp_w2_0"], 0,
                                                      comm=send([g[k] for k in names]))
    g.update(zip(names, got))
    g["mlp_w1_0"] = _wgrad("wgrad_w1_0", hn, da, _full(t, D_MODEL), _cols(t, FF_SLOT), D_MODEL, FF_SLOT)
    g["mlp_w2_0"] = _wgrad("wgrad_w2_0", f, dh2, _cols(t, FF_SLOT), _full(t, D_MODEL), FF_SLOT, D_MODEL)
    g["a_w_out"] = _wgrad("wgrad_a_w_out", gated, dh1_b, _cols(t, GATE_DIM // N_DEV), _full(t, D_MODEL),
                          GATE_DIM // N_DEV, D_MODEL)
    names = ("mlp_w1_0", "mlp_w2_0", "a_w_out")
    (dx, hn, dz, d_mix_g0, d_ln_g, d_ln_b, d_ws, d_bs), got = _a_mix_bwd(
        x, z, dh1, mix_g[0], *a_args(), comm=send_sums("pair_reduce_0", [g[k] for k in names]))
    g.update(zip(names, got))
    small = {
        "norm_mix_g": jnp.concatenate([d_mix_g0, d_mix_g1], axis=0),
        "norm_mlp_g": jnp.concatenate([d_mlp_g0, d_mlp_g1], axis=0),
        "a_ln_v_g": d_ln_g.reshape(N_DEV, GATE_DIM // N_DEV),
        "a_ln_v_b": d_ln_b.reshape(N_DEV, GATE_DIM // N_DEV),
        "a_w_s": d_ws.astype(BF16) if dist else d_ws,
        "a_b_s": d_bs.reshape(A_GROUPS, GMLP_BLOCK),
        "b_q_norm_g": d_q_g,
        "kv_src_norm_g": d_src_g,
        "kv_a_norm_g": d_kv_a_g,
        "final_norm_g": d_final_g,
    }
    wgrad_in = ("wgrad_a_w_in", hn, dz, _full(t, D_MODEL), _cols(t, FF_SLOT), D_MODEL, FF_SLOT)
    if dist:
        parts = [small[k].reshape((1,) + small[k].shape) for k in SMALL]
        g["a_w_in"], got = _wgrad(*wgrad_in, comm=_exchange_comm(parts=parts))
        small = dict(zip(SMALL, got))
    else:
        g["a_w_in"] = _wgrad(*wgrad_in)
    return loss, dx, g, small


def _adamw(w, g, m, v):
    m = ADAM_B1 * m + (1.0 - ADAM_B1) * g
    v = ADAM_B2 * v + (1.0 - ADAM_B2) * (g * g)
    m_hat = m / (1.0 - ADAM_B1 ** ADAM_STEP)
    v_hat = v / (1.0 - ADAM_B2 ** ADAM_STEP)
    return -ADAM_LR * (m_hat / (jnp.sqrt(v_hat) + ADAM_EPS) + ADAM_WD * w), m, v


def _sum_in_device_order(r_ref):
    g = r_ref[0].astype(F32)
    for j in range(1, r_ref.shape[0]):
        g = g + r_ref[j].astype(F32)
    return g


def _adamw_sharded(name, recv, w, m, v, layer, into=None):
    layers, r, c = w.shape
    tr = math.gcd(r, 256)

    def body(r_ref, w_ref, m_ref, v_ref, *refs):
        g_ref, d_ref, nm_ref, nv_ref = refs[-4:]
        g = _sum_in_device_order(r_ref)
        g_ref[...] = g
        d_ref[...], nm_ref[...], nv_ref[...] = _adamw(w_ref[...], g, m_ref[...], v_ref[...])

    blk = pl.BlockSpec((None, tr, c), lambda i: (layer, i, 0))
    args, in_specs, aliases = [recv, w, m, v], [pl.BlockSpec((recv.shape[0], tr, c), lambda i: (0, i, 0)), blk, blk, blk], {}
    if into is not None:
        args += list(into)
        in_specs += [ANY] * 4
        aliases = {4 + i: i for i in range(4)}
    return pl.pallas_call(
        body, name=name, grid=(r // tr,), in_specs=in_specs, out_specs=[blk] * 4,
        out_shape=[_sds(w.shape, F32)] * 4, input_output_aliases=aliases,
        compiler_params=pltpu.CompilerParams(dimension_semantics=("arbitrary",), vmem_limit_bytes=VMEM_LIMIT),
    )(*args)


def _adamw_small(recvs, ws, ms, vs, own_row):
    n = len(recvs)

    def body(*refs):
        r_refs, w_refs, m_refs, v_refs = (refs[i * n:(i + 1) * n] for i in range(4))
        outs, scr = refs[4 * n:8 * n], refs[8 * n:]
        me = _my_place()[3]
        for a in range(n):
            g = _sum_in_device_order(r_refs[a])
            if own_row[a]:
                scr[0][...] = g
                g = scr[0][pl.ds(me, 1), :]
            g_ref, d_ref, nm_ref, nv_ref = outs[4 * a:4 * a + 4]
            g_ref[...] = g
            d_ref[...], nm_ref[...], nv_ref[...] = _adamw(w_refs[a][...], g, m_refs[a][...], v_refs[a][...])

    out_shape = []
    for w in ws:
        out_shape += [_sds(w.shape, F32)] * 4
    return pl.pallas_call(
        body, name="adamw_small", in_specs=[VMEM] * (4 * n), out_specs=[VMEM] * (4 * n), out_shape=out_shape,
        scratch_shapes=[pltpu.VMEM((N_DEV, GATE_DIM // N_DEV), F32)],
    )(*recvs, *ws, *ms, *vs)


BIG = ("a_w_in", "a_w_out", "b_w_q_a", "b_w_q_b", "b_w_o", "kv_w_a", "kv_w_b", "mlp_w1", "mlp_w2")
SMALL = ("norm_mix_g", "norm_mlp_g", "a_ln_v_g", "a_ln_v_b", "a_w_s", "a_b_s", "b_q_norm_g", "kv_src_norm_g",
         "kv_a_norm_g", "final_norm_g")
WEIGHTS = ("norm_mix_g", "norm_mlp_g", "a_w_in", "a_ln_v_g", "a_ln_v_b", "a_w_s", "a_b_s", "a_w_out", "b_w_q_a",
           "b_q_norm_g", "b_w_q_b", "b_w_o", "kv_src_norm_g", "kv_w_a", "kv_a_norm_g", "kv_w_b", "mlp_w1", "mlp_w2",
           "final_norm_g")


def _two_d(name, a):
    if name in ("a_w_s", "a_b_s"):
        return a.reshape(a.shape[1:])
    return a.reshape(1, -1) if a.ndim == 1 else a


def _three_d(a):
    return a if a.ndim == 3 else a.reshape((1,) + a.shape)


def kernel(x, positions, norm_mix_g, norm_mlp_g, a_w_in, a_ln_v_g, a_ln_v_b, a_w_s, a_b_s, a_w_out, b_w_q_a, b_q_norm_g, b_w_q_b, b_w_o, kv_src_norm_g, kv_w_a, kv_a_norm_g, kv_w_b, mlp_w1, mlp_w2, final_norm_g, loss_target, m_norm_mix_g, m_norm_mlp_g, m_a_w_in, m_a_ln_v_g, m_a_ln_v_b, m_a_w_s, m_a_b_s, m_a_w_out, m_b_w_q_a, m_b_q_norm_g, m_b_w_q_b, m_b_w_o, m_kv_src_norm_g, m_kv_w_a, m_kv_a_norm_g, m_kv_w_b, m_mlp_w1, m_mlp_w2, m_final_norm_g, v_norm_mix_g, v_norm_mlp_g, v_a_w_in, v_a_ln_v_g, v_a_ln_v_b, v_a_w_s, v_a_b_s, v_a_w_out, v_b_w_q_a, v_b_q_norm_g, v_b_w_q_b, v_b_w_o, v_kv_src_norm_g, v_kv_w_a, v_kv_a_norm_g, v_kv_w_b, v_mlp_w1, v_mlp_w2, v_final_norm_g):
    w = dict(norm_mix_g=norm_mix_g, norm_mlp_g=norm_mlp_g, a_w_in=a_w_in, a_ln_v_g=a_ln_v_g, a_ln_v_b=a_ln_v_b,
             a_w_s=a_w_s, a_b_s=a_b_s, a_w_out=a_w_out, b_w_q_a=b_w_q_a, b_q_norm_g=b_q_norm_g, b_w_q_b=b_w_q_b,
             b_w_o=b_w_o, kv_src_norm_g=kv_src_norm_g, kv_w_a=kv_w_a, kv_a_norm_g=kv_a_norm_g, kv_w_b=kv_w_b,
             mlp_w1=mlp_w1, mlp_w2=mlp_w2, final_norm_g=final_norm_g)
    m = dict(norm_mix_g=m_norm_mix_g, norm_mlp_g=m_norm_mlp_g, a_w_in=m_a_w_in, a_ln_v_g=m_a_ln_v_g,
             a_ln_v_b=m_a_ln_v_b, a_w_s=m_a_w_s, a_b_s=m_a_b_s, a_w_out=m_a_w_out, b_w_q_a=m_b_w_q_a,
             b_q_norm_g=m_b_q_norm_g, b_w_q_b=m_b_w_q_b, b_w_o=m_b_w_o, kv_src_norm_g=m_kv_src_norm_g,
             kv_w_a=m_kv_w_a, kv_a_norm_g=m_kv_a_norm_g, kv_w_b=m_kv_w_b, mlp_w1=m_mlp_w1, mlp_w2=m_mlp_w2,
             final_norm_g=m_final_norm_g)
    v = dict(norm_mix_g=v_norm_mix_g, norm_mlp_g=v_norm_mlp_g, a_w_in=v_a_w_in, a_ln_v_g=v_a_ln_v_g,
             a_ln_v_b=v_a_ln_v_b, a_w_s=v_a_w_s, a_b_s=v_a_b_s, a_w_out=v_a_w_out, b_w_q_a=v_b_w_q_a,
             b_q_norm_g=v_b_q_norm_g, b_w_q_b=v_b_w_q_b, b_w_o=v_b_w_o, kv_src_norm_g=v_kv_src_norm_g,
             kv_w_a=v_kv_w_a, kv_a_norm_g=v_kv_a_norm_g, kv_w_b=v_kv_w_b, mlp_w1=v_mlp_w1, mlp_w2=v_mlp_w2,
             final_norm_g=v_final_norm_g)
    t = x.shape[1]

    first = ("a_w_in", "a_w_out", "a_ln_v_g", "a_ln_v_b")
    later = ("mlp_w1_0", "mlp_w2_0", "mlp_w1_1", "mlp_w2_1", "kv_w_a", "kv_w_b", "b_w_q_a", "b_w_q_b", "b_w_o")
    blocks = {k: _three_d(w[k]) for k in BIG if not k.startswith("mlp")}
    for k in ("mlp_w1", "mlp_w2"):
        blocks[k + "_0"], blocks[k + "_1"] = w[k][0:1], w[k][1:2]
    got, casts = _gather_first([blocks[k] if k in blocks else w[k] for k in first], [blocks[k] for k in later])
    wg = dict(zip(first, got))
    wg["a_w_out"] = wg["a_w_out"].reshape(GATE_DIM, D_MODEL)
    wg["a_ln_v_g"] = wg["a_ln_v_g"].reshape(1, GATE_DIM)
    wg["a_ln_v_b"] = wg["a_ln_v_b"].reshape(1, GATE_DIM)
    shards = dict(zip(later, casts))

    sm = {k: _two_d(k, w[k]) for k in SMALL if k not in ("a_ln_v_g", "a_ln_v_b")}
    sm["a_b_st"] = sm["a_b_s"].T
    inv_freq = (ROPE_THETA ** (-jnp.arange(0, QK_ROPE, 2, dtype=F32) / QK_ROPE)).reshape(1, QK_ROPE // 2)

    loss, dx, g, small = _local_step(x[0], positions.reshape(t, 1), loss_target[0], inv_freq, wg, sm, shards)
    loss = lax.psum(loss[0, 0], ("x", "y", "c"))

    g["a_w_in"], = _comm_only("exchange_last", _chip_exchange_comm(_pair_reduce("pair_reduce_a", [g["a_w_in"]])))
    small_recv = [small[k] for k in SMALL]

    out = {}
    for k in BIG:
        if k.startswith("mlp"):
            args = (_three_d(w[k]), _three_d(m[k]), _three_d(v[k]))
            res = _adamw_sharded(f"adamw_{k}_1", g[k + "_1"], *args, 1)
            res = _adamw_sharded(f"adamw_{k}_0", g[k + "_0"], *args, 0, into=res)
        else:
            res = _adamw_sharded("adamw_" + k, g[k], _three_d(w[k]), _three_d(m[k]), _three_d(v[k]), 0)
        out[k] = [o.reshape(w[k].shape) for o in res]
    own_row = [k in ("a_ln_v_g", "a_ln_v_b") for k in SMALL]
    res = _adamw_small(small_recv, [_two_d(k, w[k]) for k in SMALL], [_two_d(k, m[k]) for k in SMALL],
                       [_two_d(k, v[k]) for k in SMALL], own_row)
    for i, k in enumerate(SMALL):
        out[k] = [o.reshape(w[k].shape) for o in res[4 * i:4 * i + 4]]

    return (loss, dx.reshape(x.shape), *[out[k][0] for k in WEIGHTS], *[out[k][1] for k in WEIGHTS],
            *[out[k][2] for k in WEIGHTS], *[out[k][3] for k in WEIGHTS])
```

```python
import math

import jax
import jax.numpy as jnp
from jax import lax
from jax.experimental import pallas as pl
from jax.experimental.pallas import tpu as pltpu

F32, BF16 = jnp.float32, jnp.bfloat16
MESH = pl.DeviceIdType.MESH
ANY = pl.BlockSpec(memory_space=pl.ANY)
VMEM = pl.BlockSpec(memory_space=pltpu.VMEM)

N_DEV = 8
D_MODEL = 1024
CHUNK = 64
GMLP_BLOCK = 128
GATE_DIM = 2048
A_GROUPS = 8
A_GROUP_DIM = GATE_DIM // A_GROUPS
B_HEADS = 8
QK_NOPE, QK_ROPE, V_HEAD = 128, 64, 128
Q_LORA, KV_LORA = 384, 256
ROPE_THETA = 10000.0
D_FF = 4096
FF_SLOT = D_FF // N_DEV
EPS = 1e-6
ATT_SCALE = (QK_NOPE + QK_ROPE) ** -0.5

ADAM_LR, ADAM_B1, ADAM_B2, ADAM_EPS, ADAM_WD, ADAM_STEP = 0.001, 0.9, 0.999, 1e-08, 0.01, 10

TM = 256
TM_GATE = 128
VMEM_LIMIT = 56 * 1024 * 1024
INV_SQRT2 = 1.0 / math.sqrt(2.0)
INV_SQRT_2PI = 1.0 / math.sqrt(2.0 * math.pi)


def _dot(a, b):
    return jnp.dot(a, b, preferred_element_type=F32)


def _dot_nt(a, b):
    return lax.dot_general(a, b, (((1,), (1,)), ((), ())), preferred_element_type=F32)


def _dot_tn(a, b):
    return lax.dot_general(a, b, (((0,), (0,)), ((), ())), preferred_element_type=F32)


def _rms_fwd(x, g):
    rstd = lax.rsqrt(jnp.mean(x * x, axis=-1, keepdims=True) + EPS)
    xhat = x * rstd
    return xhat * g, xhat, rstd


def _rms_bwd(dy, xhat, rstd, g):
    dxhat = dy * g
    dx = rstd * (dxhat - xhat * jnp.mean(dxhat * xhat, axis=-1, keepdims=True))
    return dx, jnp.sum(dy * xhat, axis=0, keepdims=True)


def _ln_fwd(v, g, b):
    mu = jnp.mean(v, axis=-1, keepdims=True)
    vc = v - mu
    rstd = lax.rsqrt(jnp.mean(vc * vc, axis=-1, keepdims=True) + EPS)
    vhat = vc * rstd
    return vhat * g + b, vhat, rstd


def _gelu(x):
    return 0.5 * x * (1.0 + lax.erf(x * INV_SQRT2))


def _gelu_and_grad(x):
    cdf = 0.5 * (1.0 + lax.erf(x * INV_SQRT2))
    return x * cdf, cdf + x * jnp.exp(-0.5 * x * x) * INV_SQRT_2PI


def _rope(x, cos, sin):
    x1, x2 = x[:, :QK_ROPE // 2], x[:, QK_ROPE // 2:]
    return jnp.concatenate([x1 * cos - x2 * sin, x2 * cos + x1 * sin], axis=-1)


def _gate_mask():
    row = lax.broadcasted_iota(jnp.int32, (GMLP_BLOCK, GMLP_BLOCK), 0)
    col = lax.broadcasted_iota(jnp.int32, (GMLP_BLOCK, GMLP_BLOCK), 1)
    return (col < CHUNK) | (row >= CHUNK)


def _att_mask(q0, tq, t):
    q = q0 + lax.broadcasted_iota(jnp.int32, (tq, t), 0)
    k = lax.broadcasted_iota(jnp.int32, (tq, t), 1)
    return jnp.right_shift(k, 6) <= jnp.right_shift(q, 6)


def _res(shape, imap=None):
    zeros = (0,) * len(shape)
    return pl.BlockSpec(shape, imap or (lambda i: zeros), pipeline_mode=pl.Buffered(1))


def _const(shape):
    zeros = (0,) * len(shape)
    return pl.BlockSpec(shape, lambda i: zeros)


def _row(d, tm=TM):
    return pl.BlockSpec((tm, d), lambda i: (i, 0))


def _heads(d):
    return pl.BlockSpec((B_HEADS, TM, d), lambda i: (0, i, 0))


def _sds(shape, dt):
    return jax.ShapeDtypeStruct(shape, dt)


def _acc(ref, val):
    @pl.when(pl.program_id(0) == 0)
    def _():
        ref[...] = jnp.zeros_like(ref)
    ref[...] += val


def _my_place():
    x, y, c = lax.axis_index("x"), lax.axis_index("y"), lax.axis_index("c")
    return x, y, c, 4 * x + 2 * y + c


def _peer(x, y, c, k):
    px = 1 - x if k & 4 else x
    py = 1 - y if k & 2 else y
    pc = 1 - c if k & 1 else c
    return (px, py, pc), 4 * px + 2 * py + pc


CHIPS = (2, 4, 6)


def _gather_copy(outs, send_sems, recv_sems, a, k, block, to, src=None):
    rows = outs[a].at[pl.ds(block, 1)]
    return pltpu.make_async_remote_copy(
        src_ref=rows if src is None else src, dst_ref=rows, send_sem=send_sems.at[a, k], recv_sem=recv_sems.at[a, k],
        device_id=to, device_id_type=MESH)


def _gather_start(srcs, outs, sems, only=None):
    send_sems, recv_sems, local_sems = sems
    x, y, c, me = _my_place()
    for a in range(len(srcs)) if only is None else (only,):
        pltpu.make_async_copy(srcs[a], outs[a].at[pl.ds(me, 1)], local_sems.at[a]).start()
        _gather_copy(outs, send_sems, recv_sems, a, 0, me, _peer(x, y, c, 1)[0], src=srcs[a]).start()
        for j, k in enumerate(CHIPS):
            _gather_copy(outs, send_sems, recv_sems, a, 1 + j, me, _peer(x, y, c, k)[0], src=srcs[a]).start()


def _gather_finish(srcs, outs, sems):
    send_sems, recv_sems, local_sems = sems
    x, y, c, me = _my_place()
    sib, sib_i = _peer(x, y, c, 1)
    n = len(srcs)
    for a in range(n):
        for j, k in enumerate(CHIPS):
            block = _peer(x, y, c, k)[1]
            _gather_copy(outs, send_sems, recv_sems, a, 1 + j, block, sib).wait_recv()
            _gather_copy(outs, send_sems, recv_sems, a, 4 + j, block, sib).start()
    for a in range(n):
        _gather_copy(outs, send_sems, recv_sems, a, 0, sib_i, sib).wait_recv()
        for j, k in enumerate(CHIPS):
            _gather_copy(outs, send_sems, recv_sems, a, 4 + j, _peer(x, y, c, k ^ 1)[1], sib).wait_recv()
    for a in range(n):
        for k in range(7):
            _gather_copy(outs, send_sems, recv_sems, a, k, me, sib, src=srcs[a] if k < 4 else None).wait_send()
        pltpu.make_async_copy(srcs[a], outs[a].at[pl.ds(me, 1)], local_sems.at[a]).wait()


def _gather_sems(n):
    return [pltpu.SemaphoreType.DMA((n, 7)), pltpu.SemaphoreType.DMA((n, 7)), pltpu.SemaphoreType.DMA((n,))]


class _Comm:
    def __init__(self, args, out_shape, scratch, start, finish):
        self.args, self.out_shape, self.scratch, self.start, self.finish = args, out_shape, scratch, start, finish


def _gather_comm(shards):
    return _Comm(list(shards), [_sds((N_DEV,) + s.shape[1:], s.dtype) for s in shards], _gather_sems(len(shards)),
                 _gather_start, _gather_finish)


def _direct_copies(ins, outs, sems, wait, from_block):
    send_sems, recv_sems, local_sems = sems
    x, y, c, me = _my_place()
    for a in range(len(ins)):
        src = ins[a].at[pl.ds(me, 1)] if from_block[a] else ins[a]
        local = pltpu.make_async_copy(src, outs[a].at[pl.ds(me, 1)], local_sems.at[a])
        local.wait() if wait else local.start()
        for k in range(1, N_DEV):
            to, to_i = _peer(x, y, c, k)
            cp = pltpu.make_async_remote_copy(
                src_ref=ins[a].at[pl.ds(to_i, 1)] if from_block[a] else ins[a], dst_ref=outs[a].at[pl.ds(me, 1)],
                send_sem=send_sems.at[a, k - 1], recv_sem=recv_sems.at[a, k - 1], device_id=to, device_id_type=MESH)
            cp.wait() if wait else cp.start()


def _exchange_comm(grads=(), parts=()):
    ins = list(grads) + list(parts)
    from_block = [True] * len(grads) + [False] * len(parts)
    out_shape = [_sds(g.shape, g.dtype) for g in grads] + [_sds((N_DEV,) + p.shape[1:], p.dtype) for p in parts]

    def start(ins_, outs_, sems_):
        _direct_copies(ins_, outs_, sems_, False, from_block)

    def finish(ins_, outs_, sems_):
        _direct_copies(ins_, outs_, sems_, True, from_block)

    return _Comm(ins, out_shape, _gather_sems(len(ins)), start, finish)


def _chip_copies(ins, outs, sems, wait, rels, own):
    send_sems, recv_sems, local_sems = sems
    x, y, c, _ = _my_place()
    for a in range(len(ins)):
        if own:
            local = pltpu.make_async_copy(ins[a].at[pl.ds(2 * x + y, 1)], outs[a].at[pl.ds(len(rels), 1)],
                                          local_sems.at[a])
            local.wait() if wait else local.start()
        for i, j in enumerate(rels):
            to = _peer(x, y, c, CHIPS[j])[0]
            cp = pltpu.make_async_remote_copy(
                src_ref=ins[a].at[pl.ds(2 * to[0] + to[1], 1)], dst_ref=outs[a].at[pl.ds(i, 1)],
                send_sem=send_sems.at[a, i], recv_sem=recv_sems.at[a, i], device_id=to, device_id_type=MESH)
            cp.wait() if wait else cp.start()


def _chip_exchange_comm(sums, rels=(0, 1, 2), own=True):
    def start(ins_, outs_, sems_):
        _chip_copies(ins_, outs_, sems_, False, rels, own)

    def finish(ins_, outs_, sems_):
        _chip_copies(ins_, outs_, sems_, True, rels, own)

    n = len(sums)
    sems = [pltpu.SemaphoreType.DMA((n, len(rels))), pltpu.SemaphoreType.DMA((n, len(rels))),
            pltpu.SemaphoreType.DMA((n,))]
    return _Comm(list(sums), [_sds((len(rels) + own,) + s.shape[1:], s.dtype) for s in sums], sems, start, finish)


def _join(c1, c2):
    ni, no, ns = len(c1.args), len(c1.out_shape), len(c1.scratch)

    def start(i, o, s):
        c1.start(i[:ni], o[:no], s[:ns])
        c2.start(i[ni:], o[no:], s[ns:])

    def finish(i, o, s):
        c1.finish(i[:ni], o[:no], s[:ns])
        c2.finish(i[ni:], o[no:], s[ns:])

    return _Comm(c1.args + c2.args, c1.out_shape + c2.out_shape, c1.scratch + c2.scratch, start, finish)


def _pair_reduce(name, grads):
    n = len(grads)
    n_chips = N_DEV // 2

    def body(*refs):
        g_refs, gh_refs, p_refs, land = refs[:n], refs[n:2 * n], refs[2 * n:3 * n], refs[3 * n:4 * n]
        send_sems, recv_sems = refs[4 * n:]
        x, y, c, _ = _my_place()
        sib = _peer(x, y, c, 1)[0]
        q = pl.program_id(0)

        def to_sibling(a, j):
            return pltpu.make_async_remote_copy(
                src_ref=gh_refs[a].at[j, pl.ds(1 - c, 1)], dst_ref=land[a].at[pl.ds(j, 1)],
                send_sem=send_sems.at[a, j], recv_sem=recv_sems.at[a, j], device_id=sib, device_id_type=MESH)

        @pl.when(q == 0)
        def _():
            for j in range(n_chips):
                for a in range(n):
                    to_sibling(a, j).start()

        for a in range(n):
            to_sibling(a, q).wait_recv()
            p_refs[a][...] = (g_refs[a][0, pl.ds(c, 1)].astype(F32) + land[a][pl.ds(q, 1)].astype(F32)).astype(BF16)

        @pl.when(q == n_chips - 1)
        def _():
            for a in range(n):
                for j in range(n_chips):
                    to_sibling(a, j).wait_send()

    views = [g.reshape((n_chips, 2) + g.shape[1:]) for g in grads]
    res = pl.pallas_call(
        body, name=name, grid=(n_chips,),
        in_specs=[pl.BlockSpec((1, 2) + g.shape[1:], lambda q: (q, 0, 0, 0)) for g in grads] + [ANY] * n,
        out_specs=[pl.BlockSpec((1,) + g.shape[1:], lambda q: (q, 0, 0)) for g in grads],
        out_shape=[_sds((n_chips,) + g.shape[1:], BF16) for g in grads],
        scratch_shapes=[pltpu.VMEM((n_chips,) + g.shape[1:], BF16) for g in grads]
        + [pltpu.SemaphoreType.DMA((n, n_chips)), pltpu.SemaphoreType.DMA((n, n_chips))],
        compiler_params=pltpu.CompilerParams(dimension_semantics=("arbitrary",), vmem_limit_bytes=VMEM_LIMIT),
    )(*views, *views)
    return list(res)


def _call(name, body, grid, in_specs, out_specs, out_shape, args, scratch=(), comm=None):
    params = pltpu.CompilerParams(dimension_semantics=("arbitrary",) * len(grid), vmem_limit_bytes=VMEM_LIMIT)
    if comm is None:
        outs = pl.pallas_call(body, name=name, grid=grid, in_specs=list(in_specs), out_specs=list(out_specs),
                              out_shape=list(out_shape), scratch_shapes=list(scratch), compiler_params=params)(*args)
        return list(outs), []
    ni, nci, no, nco, ns = len(in_specs), len(comm.args), len(out_specs), len(comm.out_shape), len(scratch)

    def carrying(*refs):
        ins, refs = refs[:ni], refs[ni:]
        cin, refs = refs[:nci], refs[nci:]
        outs, refs = refs[:no], refs[no:]
        cout, refs = refs[:nco], refs[nco:]
        scr, csems = refs[:ns], refs[ns:]
        ids = [pl.program_id(ax) for ax in range(len(grid))]
        first, last = ids[0] == 0, ids[0] == grid[0] - 1
        for ax in range(1, len(grid)):
            first, last = first & (ids[ax] == 0), last & (ids[ax] == grid[ax] - 1)

        @pl.when(first)
        def _():
            comm.start(cin, cout, csems)

        body(*ins, *outs, *scr)

        @pl.when(last)
        def _():
            comm.finish(cin, cout, csems)

    outs = pl.pallas_call(
        carrying, name=name, grid=grid, in_specs=list(in_specs) + [ANY] * nci, out_specs=list(out_specs) + [ANY] * nco,
        out_shape=list(out_shape) + list(comm.out_shape), scratch_shapes=list(scratch) + list(comm.scratch),
        compiler_params=params)(*args, *comm.args)
    return list(outs[:no]), list(outs[no:])


def _comm_only(name, comm):
    def body(*refs):
        nci, nco = len(comm.args), len(comm.out_shape)
        cin, cout, csems = refs[:nci], refs[nci:nci + nco], refs[nci + nco:]
        comm.start(cin, cout, csems)
        comm.finish(cin, cout, csems)

    return pl.pallas_call(body, name=name, in_specs=[ANY] * len(comm.args), out_specs=[ANY] * len(comm.out_shape),
                          out_shape=list(comm.out_shape), scratch_shapes=list(comm.scratch))(*comm.args)


def _gather_first(first, later):
    nf, nl = len(first), len(later)
    dts = [BF16] * (nf - 2) + [F32, F32]

    def body(*refs):
        ins, refs = refs[:nf + nl], refs[nf + nl:]
        outs, refs = refs[:nf], refs[nf:]
        casts, refs = refs[:nl], refs[nl:]
        stage, sems = refs[:nf], refs[nf:]
        for a in range(nf):
            stage[a][...] = ins[a][...].astype(dts[a])
            _gather_start(stage, outs, sems, only=a)
        for a in range(nl):
            casts[a][...] = ins[nf + a][...].astype(BF16)
        _gather_finish(stage, outs, sems)

    res = pl.pallas_call(
        body, name="gather_first",
        in_specs=[VMEM] * (nf + nl), out_specs=[ANY] * nf + [VMEM] * nl,
        out_shape=[_sds((N_DEV,) + s.shape[1:], dt) for s, dt in zip(first, dts)]
        + [_sds(s.shape, BF16) for s in later],
        scratch_shapes=[pltpu.VMEM(s.shape, dt) for s, dt in zip(first, dts)] + _gather_sems(nf),
        compiler_params=pltpu.CompilerParams(vmem_limit_bytes=VMEM_LIMIT),
    )(*first, *later)
    return list(res[:nf]), list(res[nf:])


def _a_mix_fwd(x, g, w_in, ln_g, ln_b, w_s, b_st, w_out, comm=None):
    t = x.shape[0]
    nblk = TM // GMLP_BLOCK

    def body(x_ref, g_ref, win_ref, lng_ref, lnb_ref, ws_ref, bst_ref, wout_ref, h_ref, z_ref, gated_scr):
        xv = x_ref[...]
        hb = _rms_fwd(xv, g_ref[...])[0].astype(BF16)
        for d in range(N_DEV):
            z_ref[:, d * FF_SLOT:(d + 1) * FF_SLOT] = _dot(hb, win_ref[d])
        u = _gelu(z_ref[:, :GATE_DIM])
        vb = _ln_fwd(_gelu(z_ref[:, GATE_DIM:]), lng_ref[...], lnb_ref[...])[0].astype(BF16)
        mask = _gate_mask()
        for gi in range(A_GROUPS):
            wm = jnp.where(mask, ws_ref[gi], 0.0).astype(BF16)
            bias = bst_ref[:, gi:gi + 1]
            cs = slice(gi * A_GROUP_DIM, (gi + 1) * A_GROUP_DIM)
            for n in range(nblk):
                rs = slice(n * GMLP_BLOCK, (n + 1) * GMLP_BLOCK)
                sv = _dot(wm, vb[rs, cs]) + bias
                gated_scr[rs, cs] = (u[rs, cs] * sv).astype(BF16)
        h_ref[...] = xv + _dot(gated_scr[...], wout_ref[...])

    return _call(
        "a_mix_fwd", body, (t // TM,),
        [_row(D_MODEL), _res((1, D_MODEL)), _res((N_DEV, D_MODEL, FF_SLOT)), _res((1, GATE_DIM)),
         _res((1, GATE_DIM)), _res((A_GROUPS, GMLP_BLOCK, GMLP_BLOCK)), _res((GMLP_BLOCK, A_GROUPS)),
         _res((GATE_DIM, D_MODEL))],
        [_row(D_MODEL), _row(2 * GATE_DIM), _row(GATE_DIM)],
        [_sds((t, D_MODEL), F32), _sds((t, 2 * GATE_DIM), F32), _sds((t, GATE_DIM), BF16)],
        (x, g, w_in, ln_g, ln_b, w_s, b_st, w_out), comm=comm)


MLP_W_SPECS = (_res((N_DEV, D_MODEL, FF_SLOT)), _res((N_DEV, FF_SLOT, D_MODEL)))


def _mlp_fwd(h, g, w1, w2, layer, comm=None):
    t = h.shape[0]

    def body(h_ref, g_ref, w1_ref, w2_ref, o_ref, a_ref):
        hv = h_ref[...]
        hb = _rms_fwd(hv, g_ref[...])[0].astype(BF16)
        o_ref[...] = hv
        for d in range(N_DEV):
            a = _dot(hb, w1_ref[d])
            a_ref[:, d * FF_SLOT:(d + 1) * FF_SLOT] = a
            r = jnp.maximum(a, 0.0)
            o_ref[...] += _dot((r * r).astype(BF16), w2_ref[d])

    return _call(
        f"mlp_fwd_{layer}", body, (t // TM,), [_row(D_MODEL), _res((1, D_MODEL)), *MLP_W_SPECS],
        [_row(D_MODEL), _row(D_FF)], [_sds((t, D_MODEL), F32), _sds((t, D_FF), F32)], (h, g, w1, w2), comm=comm)


KVQ_W_SPECS = (_res((1, D_MODEL)), _res((D_MODEL, KV_LORA + QK_ROPE)), _res((1, KV_LORA)),
               _res((B_HEADS, KV_LORA, QK_NOPE + V_HEAD)), _res((1, D_MODEL)), _res((D_MODEL, Q_LORA)),
               _res((1, Q_LORA)), _res((B_HEADS, Q_LORA, QK_NOPE + QK_ROPE)))


def _kvq_fwd(h, pos, inv_freq, kvq_w):
    t = h.shape[0]
    half = QK_ROPE // 2

    def body(h_ref, pos_ref, invf_ref, srcg_ref, wkva_ref, kvag_ref, wkvb_ref, mixg_ref, wqa_ref, qg_ref, wqb_ref,
             ckv_ref, kn_ref, v_ref, kpe_ref, cqpre_ref, q_ref, cos_ref, sin_ref):
        hv = h_ref[...]
        xhat = hv * lax.rsqrt(jnp.mean(hv * hv, axis=-1, keepdims=True) + EPS)
        ang = pos_ref[...].astype(F32) * invf_ref[...]
        cos, sin = jnp.cos(ang), jnp.sin(ang)
        cos_ref[...] = cos
        sin_ref[...] = sin
        ckv = _dot((xhat * srcg_ref[...]).astype(BF16), wkva_ref[...])
        ckv_ref[...] = ckv
        cb = _rms_fwd(ckv[:, :KV_LORA], kvag_ref[...])[0].astype(BF16)
        kpe_ref[...] = _rope(ckv[:, KV_LORA:], cos, sin).astype(BF16)
        for hd in range(B_HEADS):
            kv = _dot(cb, wkvb_ref[hd])
            kn_ref[hd] = kv[:, :QK_NOPE].astype(BF16)
            v_ref[hd] = kv[:, QK_NOPE:].astype(BF16)
        cqpre = _dot((xhat * mixg_ref[...]).astype(BF16), wqa_ref[...])
        cqpre_ref[...] = cqpre
        cqb = _rms_fwd(cqpre, qg_ref[...])[0].astype(BF16)
        for hd in range(B_HEADS):
            q = _dot(cqb, wqb_ref[hd])
            q_ref[hd, :, 0:QK_NOPE] = q[:, :QK_NOPE].astype(BF16)
            q_ref[hd, :, QK_NOPE:] = _rope(q[:, QK_NOPE:], cos, sin).astype(BF16)

    return _call(
        "kvq_fwd", body, (t // TM,), [_row(D_MODEL), _row(1), _res((1, half)), *KVQ_W_SPECS],
        [_row(KV_LORA + QK_ROPE), _heads(QK_NOPE), _heads(V_HEAD), _row(QK_ROPE), _row(Q_LORA),
         _heads(QK_NOPE + QK_ROPE), _row(half), _row(half)],
        [_sds((t, KV_LORA + QK_ROPE), F32), _sds((B_HEADS, t, QK_NOPE), BF16), _sds((B_HEADS, t, V_HEAD), BF16),
         _sds((t, QK_ROPE), BF16), _sds((t, Q_LORA), F32), _sds((B_HEADS, t, QK_NOPE + QK_ROPE), BF16),
         _sds((t, half), F32), _sds((t, half), F32)],
        (h, pos, inv_freq, *kvq_w))[0]


def _softmax_rows(qn, qp, kn_ref, kpe_ref, k):
    past, upto = k * TM, (k + 1) * TM
    s = (_dot_nt(qn, kn_ref[0:upto, :]) + _dot_nt(qp, kpe_ref[0:upto, :])) * ATT_SCALE
    own = jnp.where(_att_mask(0, TM, TM), s[:, past:], jnp.finfo(F32).min)
    s = own if k == 0 else jnp.concatenate([s[:, :past], own], axis=1)
    e = jnp.exp(s - jnp.max(s, axis=-1, keepdims=True))
    return e * (1.0 / jnp.sum(e, axis=-1, keepdims=True))


def _for_my_tile(i, nq, fn):
    for k in range(nq):
        @pl.when(i == k)
        def _(k=k):
            fn(k)


def _attn_fwd(h, q, kn, kpe, v, w_o, comm=None):
    t = h.shape[0]
    nq = t // TM

    def body(h_ref, q_ref, kn_ref, kpe_ref, v_ref, wo_ref, o_ref, att_ref):
        i, hd = pl.program_id(0), pl.program_id(1)

        @pl.when(hd == 0)
        def _():
            o_ref[...] = h_ref[...]

        def tile(k):
            p = _softmax_rows(q_ref[:, 0:QK_NOPE], q_ref[:, QK_NOPE:], kn_ref.at[hd], kpe_ref, k)
            ob = _dot(p.astype(BF16), v_ref[hd, 0:(k + 1) * TM, :]).astype(BF16)
            att_ref[...] = ob
            o_ref[...] += _dot(ob, wo_ref[hd])

        _for_my_tile(i, nq, tile)

    def per_head(d):
        return pl.BlockSpec((None, TM, d), lambda i, hd: (hd, i, 0))

    def resident(shape):
        zeros = (0,) * len(shape)
        return pl.BlockSpec(shape, lambda i, hd: zeros, pipeline_mode=pl.Buffered(1))

    tile_spec = pl.BlockSpec((TM, D_MODEL), lambda i, hd: (i, 0))
    return _call(
        "attn_fwd", body, (nq, B_HEADS),
        [tile_spec, per_head(QK_NOPE + QK_ROPE), resident((B_HEADS, t, QK_NOPE)), resident((t, QK_ROPE)),
         resident((B_HEADS, t, V_HEAD)), resident((B_HEADS, V_HEAD, D_MODEL))],
        [tile_spec, per_head(V_HEAD)], [_sds((t, D_MODEL), F32), _sds((B_HEADS, t, V_HEAD), BF16)],
        (h, q, kn, kpe, v, w_o), comm=comm)


def _loss_head(h, g, target):
    t = h.shape[0]

    def body(h_ref, g_ref, t_ref, loss_ref, dh_ref, dg_ref):
        y, xhat, rstd = _rms_fwd(h_ref[...], g_ref[...])
        err = y - t_ref[...]
        part = 0.5 * jnp.sum(jnp.mean(err * err, axis=-1, keepdims=True), axis=0, keepdims=True)
        dx, dg = _rms_bwd(err * (1.0 / D_MODEL), xhat, rstd, g_ref[...])
        dh_ref[...] = dx
        _acc(dg_ref, dg)
        _acc(loss_ref, part)

    return _call(
        "loss_head", body, (t // TM,), [_row(D_MODEL), _res((1, D_MODEL)), _row(D_MODEL)],
        [_const((1, 1)), _row(D_MODEL), _const((1, D_MODEL))],
        [_sds((1, 1), F32), _sds((t, D_MODEL), F32), _sds((1, D_MODEL), F32)], (h, g, target))[0]


def _mlp_bwd(h, a, dho, g, w1, w2, layer, comm=None):
    t = h.shape[0]

    def body(h_ref, a_ref, dho_ref, g_ref, w1_ref, w2_ref, dhi_ref, dg_ref, hn_ref, f_ref, da_ref, dhib_ref):
        gv = g_ref[...]
        y, xhat, rstd = _rms_fwd(h_ref[...], gv)
        hn_ref[...] = y.astype(BF16)
        dho_v = dho_ref[...]
        dhob = dho_v.astype(BF16)
        dhn = jnp.zeros((TM, D_MODEL), F32)
        for d in range(N_DEV):
            cs = slice(d * FF_SLOT, (d + 1) * FF_SLOT)
            r = jnp.maximum(a_ref[:, cs], 0.0)
            f_ref[:, cs] = (r * r).astype(BF16)
            da = (_dot_nt(dhob, w2_ref[d]) * (2.0 * r)).astype(BF16)
            da_ref[:, cs] = da
            dhn = dhn + _dot_nt(da, w1_ref[d])
        dx, dg = _rms_bwd(dhn, xhat, rstd, gv)
        dhi = dho_v + dx
        dhi_ref[...] = dhi
        dhib_ref[...] = dhi.astype(BF16)
        _acc(dg_ref, dg)

    return _call(
        f"mlp_bwd_{layer}", body, (t // TM,),
        [_row(D_MODEL), _row(D_FF), _row(D_MODEL), _res((1, D_MODEL)), *MLP_W_SPECS],
        [_row(D_MODEL), _const((1, D_MODEL)), _row(D_MODEL), _row(D_FF), _row(D_FF), _row(D_MODEL)],
        [_sds((t, D_MODEL), F32), _sds((1, D_MODEL), F32), _sds((t, D_MODEL), BF16), _sds((t, D_FF), BF16),
         _sds((t, D_FF), BF16), _sds((t, D_MODEL), BF16)],
        (h, a, dho, g, w1, w2), comm=comm)


def _attn_bwd(dh, q, kn, kpe, v, w_o, cos, sin, comm=None):
    t = dh.shape[0]
    half = QK_ROPE // 2

    def body(dh_ref, q_ref, kn_ref, kpe_ref, v_ref, wo_ref, cos_ref, sin_ref, dq_ref, dkn_ref, dv_ref, dkpe_ref):
        hd, i = pl.program_id(0), pl.program_id(1)

        @pl.when(i == 0)
        def _():
            dkn_ref[...] = jnp.zeros_like(dkn_ref)
            dv_ref[...] = jnp.zeros_like(dv_ref)

        @pl.when((i == 0) & (hd == 0))
        def _():
            dkpe_ref[...] = jnp.zeros_like(dkpe_ref)

        def tile(k):
            keys = slice(0, (k + 1) * TM)
            qn, qp = q_ref[:, 0:QK_NOPE], q_ref[:, QK_NOPE:]
            do = _dot_nt(dh_ref[k * TM:(k + 1) * TM, :], wo_ref[...]).astype(BF16)
            p = _softmax_rows(qn, qp, kn_ref, kpe_ref, k)
            dp = _dot_nt(do, v_ref[keys, :])
            ds = (p * (dp - jnp.sum(p * dp, axis=-1, keepdims=True)) * ATT_SCALE).astype(BF16)
            dq_ref[:, 0:QK_NOPE] = _dot(ds, kn_ref[keys, :]).astype(BF16)
            dq_ref[:, QK_NOPE:] = _rope(_dot(ds, kpe_ref[keys, :]), cos_ref[...], -sin_ref[...]).astype(BF16)
            dkn_ref[keys, :] += _dot_tn(ds, qn)
            dv_ref[keys, :] += _dot_tn(p.astype(BF16), do)
            dkpe_ref[keys, :] += _dot_tn(ds, qp)

        _for_my_tile(i, t // TM, tile)

    def per_head(rows, d, tiled):
        return pl.BlockSpec((None, rows, d), (lambda hd, i: (hd, i, 0)) if tiled else (lambda hd, i: (hd, 0, 0)))

    def tile(d):
        return pl.BlockSpec((TM, d), lambda hd, i: (i, 0))

    return _call(
        "attn_bwd", body, (B_HEADS, t // TM),
        [pl.BlockSpec((t, D_MODEL), lambda hd, i: (0, 0), pipeline_mode=pl.Buffered(1)),
         per_head(TM, QK_NOPE + QK_ROPE, True), per_head(t, QK_NOPE, False),
         pl.BlockSpec((t, QK_ROPE), lambda hd, i: (0, 0)), per_head(t, V_HEAD, False),
         per_head(V_HEAD, D_MODEL, False), tile(half), tile(half)],
        [per_head(TM, QK_NOPE + QK_ROPE, True), per_head(t, QK_NOPE, False), per_head(t, V_HEAD, False),
         pl.BlockSpec((t, QK_ROPE), lambda hd, i: (0, 0))],
        [_sds((B_HEADS, t, QK_NOPE + QK_ROPE), BF16), _sds((B_HEADS, t, QK_NOPE), F32),
         _sds((B_HEADS, t, V_HEAD), F32), _sds((t, QK_ROPE), F32)],
        (dh, q, kn, kpe, v, w_o, cos, sin), comm=comm)


def _kvq_bwd(h, dh, ckv, cqpre, dq, dkn, dv, dkpe, cos, sin, kvq_w):
    t = h.shape[0]
    half = QK_ROPE // 2

    def body(h_ref, dh_ref, ckv_ref, cqpre_ref, dq_ref, dkn_ref, dv_ref, dkpe_ref, cos_ref, sin_ref,
             srcg_ref, wkva_ref, kvag_ref, wkvb_ref, mixg_ref, wqa_ref, qg_ref, wqb_ref,
             dhi_ref, hq_ref, hk_ref, cq_ref, dcqpre_ref, c_ref, dkv_ref, dckv_ref,
             dmixg_ref, dsrcg_ref, dqg_ref, dkvag_ref):
        hv = h_ref[...]
        rstd = lax.rsqrt(jnp.mean(hv * hv, axis=-1, keepdims=True) + EPS)
        xhat = hv * rstd
        mixg, srcg, qg, kvag = mixg_ref[...], srcg_ref[...], qg_ref[...], kvag_ref[...]
        hq_ref[...] = (xhat * mixg).astype(BF16)
        hk_ref[...] = (xhat * srcg).astype(BF16)
        cq, cqhat, crstd = _rms_fwd(cqpre_ref[...], qg)
        cq_ref[...] = cq.astype(BF16)
        dcq = jnp.zeros((TM, Q_LORA), F32)
        for hd in range(B_HEADS):
            dcq = dcq + _dot_nt(dq_ref[hd], wqb_ref[hd])
        dcqpre, dqg = _rms_bwd(dcq, cqhat, crstd, qg)
        dcqpre_b = dcqpre.astype(BF16)
        dcqpre_ref[...] = dcqpre_b
        dxq, dmixg = _rms_bwd(_dot_nt(dcqpre_b, wqa_ref[...]), xhat, rstd, mixg)
        ckv = ckv_ref[...]
        c, chat, krstd = _rms_fwd(ckv[:, :KV_LORA], kvag)
        c_ref[...] = c.astype(BF16)
        dc = jnp.zeros((TM, KV_LORA), F32)
        for hd in range(B_HEADS):
            dkv = jnp.concatenate([dkn_ref[hd], dv_ref[hd]], axis=-1).astype(BF16)
            dkv_ref[hd] = dkv
            dc = dc + _dot_nt(dkv, wkvb_ref[hd])
        dlat, dkvag = _rms_bwd(dc, chat, krstd, kvag)
        dpe = _rope(dkpe_ref[...], cos_ref[...], -sin_ref[...])
        dckv_b = jnp.concatenate([dlat, dpe], axis=-1).astype(BF16)
        dckv_ref[...] = dckv_b
        dxk, dsrcg = _rms_bwd(_dot_nt(dckv_b, wkva_ref[...]), xhat, rstd, srcg)
        dhi_ref[...] = dh_ref[...] + dxq + dxk
        _acc(dmixg_ref, dmixg)
        _acc(dsrcg_ref, dsrcg)
        _acc(dqg_ref, dqg)
        _acc(dkvag_ref, dkvag)

    return _call(
        "kvq_bwd", body, (t // TM,),
        [_row(D_MODEL), _row(D_MODEL), _row(KV_LORA + QK_ROPE), _row(Q_LORA), _heads(QK_NOPE + QK_ROPE),
         _heads(QK_NOPE), _heads(V_HEAD), _row(QK_ROPE), _row(half), _row(half), *KVQ_W_SPECS],
        [_row(D_MODEL), _row(D_MODEL), _row(D_MODEL), _row(Q_LORA), _row(Q_LORA), _row(KV_LORA),
         _heads(QK_NOPE + V_HEAD), _row(KV_LORA + QK_ROPE),
         _const((1, D_MODEL)), _const((1, D_MODEL)), _const((1, Q_LORA)), _const((1, KV_LORA))],
        [_sds((t, D_MODEL), F32), _sds((t, D_MODEL), BF16), _sds((t, D_MODEL), BF16), _sds((t, Q_LORA), BF16),
         _sds((t, Q_LORA), BF16), _sds((t, KV_LORA), BF16), _sds((B_HEADS, t, QK_NOPE + V_HEAD), BF16),
         _sds((t, KV_LORA + QK_ROPE), BF16),
         _sds((1, D_MODEL), F32), _sds((1, D_MODEL), F32), _sds((1, Q_LORA), F32), _sds((1, KV_LORA), F32)],
        (h, dh, ckv, cqpre, dq, dkn, dv, dkpe, cos, sin, *kvq_w))[0]


def _a_mix_bwd(x, z, dh, g, w_in, ln_g, ln_b, w_s, b_st, w_out, comm=None):
    t = x.shape[0]
    tm = TM_GATE
    nblk = tm // GMLP_BLOCK

    def body(x_ref, z_ref, dh_ref, g_ref, win_ref, lng_ref, lnb_ref, ws_ref, bst_ref, wout_ref,
             dx_ref, hn_ref, dz_ref, dg_ref, dlng_ref, dlnb_ref, dws_ref, dbs_ref, dvn_scr, gelu_grad_v):
        @pl.when(pl.program_id(0) == 0)
        def _():
            dws_ref[...] = jnp.zeros_like(dws_ref)
            dbs_ref[...] = jnp.zeros_like(dbs_ref)

        gv, lng = g_ref[...], lng_ref[...]
        y, xhat, rstd = _rms_fwd(x_ref[...], gv)
        hn_ref[...] = y.astype(BF16)
        dhv = dh_ref[...]
        dgated = _dot_nt(dhv.astype(BF16), wout_ref[...])
        u, gelu_grad_u = _gelu_and_grad(z_ref[:, :GATE_DIM])
        v, gelu_grad_v[...] = _gelu_and_grad(z_ref[:, GATE_DIM:])
        vn, vhat, lrstd = _ln_fwd(v, lng, lnb_ref[...])
        vb = vn.astype(BF16)
        mask = _gate_mask()
        for gi in range(A_GROUPS):
            wm = jnp.where(mask, ws_ref[gi], 0.0).astype(BF16)
            bias = bst_ref[:, gi:gi + 1]
            cs = slice(gi * A_GROUP_DIM, (gi + 1) * A_GROUP_DIM)
            dws = jnp.zeros((GMLP_BLOCK, GMLP_BLOCK), F32)
            dbs = jnp.zeros((GMLP_BLOCK, 1), F32)
            for n in range(nblk):
                rs = slice(n * GMLP_BLOCK, (n + 1) * GMLP_BLOCK)
                sv = _dot(wm, vb[rs, cs]) + bias
                dz_ref[rs, cs] = (dgated[rs, cs] * sv * gelu_grad_u[rs, cs]).astype(BF16)
                dsv = dgated[rs, cs] * u[rs, cs]
                dsvb = dsv.astype(BF16)
                dws = dws + _dot_nt(dsvb, vb[rs, cs])
                dbs = dbs + jnp.sum(dsv, axis=-1, keepdims=True)
                dvn_scr[rs, cs] = _dot_tn(wm, dsvb)
            dws_ref[gi] += jnp.where(mask, dws, 0.0)
            dbs_ref[gi] += dbs
        dvn = dvn_scr[...]
        dvhat = dvn * lng
        dv = lrstd * (dvhat - jnp.mean(dvhat, axis=-1, keepdims=True)
                      - vhat * jnp.mean(dvhat * vhat, axis=-1, keepdims=True))
        dz_ref[:, GATE_DIM:] = (dv * gelu_grad_v[...]).astype(BF16)
        dhn = jnp.zeros((tm, D_MODEL), F32)
        for d in range(N_DEV):
            dhn = dhn + _dot_nt(dz_ref[:, d * FF_SLOT:(d + 1) * FF_SLOT], win_ref[d])
        dx, dg = _rms_bwd(dhn, xhat, rstd, gv)
        dx_ref[...] = dhv + dx
        _acc(dg_ref, dg)
        _acc(dlng_ref, jnp.sum(dvn * vhat, axis=0, keepdims=True))
        _acc(dlnb_ref, jnp.sum(dvn, axis=0, keepdims=True))

    return _call(
        "a_mix_bwd", body, (t // tm,),
        [_row(D_MODEL, tm), _row(2 * GATE_DIM, tm), _row(D_MODEL, tm), _res((1, D_MODEL)),
         _res((N_DEV, D_MODEL, FF_SLOT)), _res((1, GATE_DIM)), _res((1, GATE_DIM)),
         _res((A_GROUPS, GMLP_BLOCK, GMLP_BLOCK)), _res((GMLP_BLOCK, A_GROUPS)), _res((GATE_DIM, D_MODEL))],
        [_row(D_MODEL, tm), _row(D_MODEL, tm), _row(2 * GATE_DIM, tm),
         _const((1, D_MODEL)), _const((1, GATE_DIM)), _const((1, GATE_DIM)),
         _const((A_GROUPS, GMLP_BLOCK, GMLP_BLOCK)), _const((A_GROUPS, GMLP_BLOCK, 1))],
        [_sds((t, D_MODEL), F32), _sds((t, D_MODEL), BF16),
         _sds((t, 2 * GATE_DIM), BF16), _sds((1, D_MODEL), F32), _sds((1, GATE_DIM), F32),
         _sds((1, GATE_DIM), F32), _sds((A_GROUPS, GMLP_BLOCK, GMLP_BLOCK), F32),
         _sds((A_GROUPS, GMLP_BLOCK, 1), F32)],
        (x, z, dh, g, w_in, ln_g, ln_b, w_s, b_st, w_out),
        scratch=[pltpu.VMEM((tm, GATE_DIM), F32), pltpu.VMEM((tm, GATE_DIM), F32)], comm=comm)


def _wgrad(name, a, b, a_spec, b_spec, m, n, comm=None):
    def body(a_ref, b_ref, o_ref):
        o_ref[0] = _dot_tn(a_ref[...].astype(BF16), b_ref[...].astype(BF16)).astype(BF16)

    outs, got = _call(name, body, (N_DEV,), [a_spec, b_spec], [pl.BlockSpec((1, m, n), lambda d: (d, 0, 0))],
                      [_sds((N_DEV, m, n), BF16)], (a, b), comm=comm)
    return outs[0] if comm is None else (outs[0], got)


def _full(t, d):
    return pl.BlockSpec((t, d), lambda i: (0, 0), pipeline_mode=pl.Buffered(1))


def _cols(t, d):
    return pl.BlockSpec((t, d), lambda i: (0, i))


def _head(t, d):
    return pl.BlockSpec((None, t, d), lambda i: (i, 0, 0))


def _local_step(x, pos, target, inv_freq, wg, sm, shards=None):
    t = x.shape[0]
    wg = dict(wg)
    dist = shards is not None
    mix_g = [sm["norm_mix_g"][l:l + 1] for l in range(2)]
    mlp_g = [sm["norm_mlp_g"][l:l + 1] for l in range(2)]

    def gather(names):
        return _gather_comm([shards[k] for k in names]) if dist else None

    def send(grads):
        return _exchange_comm(grads=grads) if dist else None

    def send_sums(name, grads):
        return _chip_exchange_comm(_pair_reduce(name, grads)) if dist else None

    def a_args():
        return (wg["a_w_in"], wg["a_ln_v_g"], wg["a_ln_v_b"], sm["a_w_s"], sm["a_b_st"], wg["a_w_out"])

    def kvq_w():
        return (sm["kv_src_norm_g"], wg["kv_w_a"], sm["kv_a_norm_g"], wg["kv_w_b"], mix_g[1], wg["b_w_q_a"],
                sm["b_q_norm_g"], wg["b_w_q_b"])

    names = ("mlp_w1_0", "mlp_w2_0")
    (h1, z, gated), got = _a_mix_fwd(x, mix_g[0], *a_args(), comm=gather(names))
    wg.update(zip(names, got))
    names = ("kv_w_a", "kv_w_b", "b_w_q_a", "b_w_q_b", "b_w_o")
    (h2, a0), got = _mlp_fwd(h1, mlp_g[0], wg["mlp_w1_0"], wg["mlp_w2_0"], 0, comm=gather(names))
    wg.update(zip(names, got))
    if dist:
        wg["b_w_q_a"] = wg["b_w_q_a"].reshape(D_MODEL, Q_LORA)
        wg["kv_w_a"] = wg["kv_w_a"].reshape(D_MODEL, KV_LORA + QK_ROPE)
    ckv, kn, v, kpe, cqpre, q, cos, sin = _kvq_fwd(h2, pos, inv_freq, kvq_w())
    names = ("mlp_w1_1", "mlp_w2_1")
    (h3, att), got = _attn_fwd(h2, q, kn, kpe, v, wg["b_w_o"], comm=gather(names))
    wg.update(zip(names, got))
    (h4, a1), _ = _mlp_fwd(h3, mlp_g[1], wg["mlp_w1_1"], wg["mlp_w2_1"], 1)
    loss, dh4, d_final_g = _loss_head(h4, sm["final_norm_g"], target)

    g = {}
    (dh3, d_mlp_g1, hn, f, da, dh3_b), _ = _mlp_bwd(h3, a1, dh4, mlp_g[1], wg["mlp_w1_1"], wg["mlp_w2_1"], 1)
    g["mlp_w1_1"] = _wgrad("wgrad_w1_1", hn, da, _full(t, D_MODEL), _cols(t, FF_SLOT), D_MODEL, FF_SLOT)
    g["mlp_w2_1"] = _wgrad("wgrad_w2_1", f, dh4, _cols(t, FF_SLOT), _full(t, D_MODEL), FF_SLOT, D_MODEL)
    g["b_w_o"] = _wgrad("wgrad_w_o", att, dh3_b, _head(t, V_HEAD), _full(t, D_MODEL), V_HEAD, D_MODEL)
    names = ("mlp_w1_1", "mlp_w2_1", "b_w_o")
    (dq, dkn, dv, dkpe), got = _attn_bwd(dh3_b, q, kn, kpe, v, wg["b_w_o"], cos, sin,
                                         comm=send_sums("pair_reduce_1", [g[k] for k in names]))
    g.update(zip(names, got))
    (dh2, hq, hk, cq, dcqpre, c, dkv, dckv, d_mix_g1, d_src_g, d_q_g, d_kv_a_g) = _kvq_bwd(
        h2, dh3, ckv, cqpre, dq, dkn, dv, dkpe, cos, sin, kvq_w())
    g["b_w_q_a"] = _wgrad("wgrad_w_q_a", hq, dcqpre, _cols(t, D_MODEL // N_DEV), _full(t, Q_LORA),
                          D_MODEL // N_DEV, Q_LORA)
    g["b_w_q_b"] = _wgrad("wgrad_w_q_b", cq, dq, _full(t, Q_LORA), _head(t, QK_NOPE + QK_ROPE),
                          Q_LORA, QK_NOPE + QK_ROPE)
    g["kv_w_a"] = _wgrad("wgrad_kv_w_a", hk, dckv, _cols(t, D_MODEL // N_DEV), _full(t, KV_LORA + QK_ROPE),
                         D_MODEL // N_DEV, KV_LORA + QK_ROPE)
    g["kv_w_b"] = _wgrad("wgrad_kv_w_b", c, dkv, _full(t, KV_LORA), _head(t, QK_NOPE + V_HEAD),
                         KV_LORA, QK_NOPE + V_HEAD)
    names = ("b_w_q_a", "b_w_q_b", "kv_w_a", "kv_w_b")
    (dh1, d_mlp_g0, hn, f, da, dh1_b), got = _mlp_bwd(h1, a0, dh2, mlp_g[0], wg["mlp_w1_0"], wg["mlp_w2_0"], 0,
                                                      comm=send([g[k] for k in names]))
    g.update(zip(names, got))
    g["mlp_w1_0"] = _wgrad("wgrad_w1_0", hn, da, _full(t, D_MODEL), _cols(t, FF_SLOT), D_MODEL, FF_SLOT)
    g["mlp_w2_0"] = _wgrad("wgrad_w2_0", f, dh2, _cols(t, FF_SLOT), _full(t, D_MODEL), FF_SLOT, D_MODEL)
    g["a_w_out"] = _wgrad("wgrad_a_w_out", gated, dh1_b, _cols(t, GATE_DIM // N_DEV), _full(t, D_MODEL),
                          GATE_DIM // N_DEV, D_MODEL)
    names = ("mlp_w1_0", "mlp_w2_0", "a_w_out")
    sums = _pair_reduce("pair_reduce_0", [g[k] for k in names]) if dist else None
    (dx, hn, dz, d_mix_g0, d_ln_g, d_ln_b, d_ws, d_bs), got = _a_mix_bwd(
        x, z, dh1, mix_g[0], *a_args(), comm=_chip_exchange_comm(sums[:2]) if dist else None)
    g.update(zip(names[:2], got))
    small = {
        "norm_mix_g": jnp.concatenate([d_mix_g0, d_mix_g1], axis=0),
        "norm_mlp_g": jnp.concatenate([d_mlp_g0, d_mlp_g1], axis=0),
        "a_ln_v_g": d_ln_g.reshape(N_DEV, GATE_DIM // N_DEV),
        "a_ln_v_b": d_ln_b.reshape(N_DEV, GATE_DIM // N_DEV),
        "a_w_s": d_ws.astype(BF16) if dist else d_ws,
        "a_b_s": d_bs.reshape(A_GROUPS, GMLP_BLOCK),
        "b_q_norm_g": d_q_g,
        "kv_src_norm_g": d_src_g,
        "kv_a_norm_g": d_kv_a_g,
        "final_norm_g": d_final_g,
    }
    wgrad_in = ("wgrad_a_w_in", hn, dz, _full(t, D_MODEL), _cols(t, FF_SLOT), D_MODEL, FF_SLOT)
    if dist:
        parts = [small[k].reshape((1,) + small[k].shape) for k in SMALL]
        g["a_w_in"], got = _wgrad(*wgrad_in, comm=_join(_chip_exchange_comm(sums[2:]), _exchange_comm(parts=parts)))
        g["a_w_out"] = got[0]
        small = dict(zip(SMALL, got[1:]))
    else:
        g["a_w_in"] = _wgrad(*wgrad_in)
    return loss, dx, g, small


def _adamw(w, g, m, v):
    m = ADAM_B1 * m + (1.0 - ADAM_B1) * g
    v = ADAM_B2 * v + (1.0 - ADAM_B2) * (g * g)
    m_hat = m / (1.0 - ADAM_B1 ** ADAM_STEP)
    v_hat = v / (1.0 - ADAM_B2 ** ADAM_STEP)
    return -ADAM_LR * (m_hat / (jnp.sqrt(v_hat) + ADAM_EPS) + ADAM_WD * w), m, v


def _sum_in_device_order(r_ref):
    g = r_ref[0].astype(F32)
    for j in range(1, r_ref.shape[0]):
        g = g + r_ref[j].astype(F32)
    return g


def _adamw_sharded(name, recvs, w, m, v, comm=None):
    layers, r, c = w.shape
    tr = math.gcd(r, 256)
    flat = [a for per_layer in recvs for a in per_layer]

    def body(*refs):
        r_refs, (w_ref, m_ref, v_ref) = refs[:len(flat)], refs[len(flat):len(flat) + 3]
        g_ref, d_ref, nm_ref, nv_ref = refs[-4:]
        layer = pl.program_id(0)
        g, pos = None, 0
        for li, per_layer in enumerate(recvs):
            total = None
            for ref in r_refs[pos:pos + len(per_layer)]:
                part = _sum_in_device_order(ref)
                total = part if total is None else total + part
            pos += len(per_layer)
            g = total if g is None else jnp.where(layer == li, total, g)
        g_ref[...] = g
        d_ref[...], nm_ref[...], nv_ref[...] = _adamw(w_ref[...], g, m_ref[...], v_ref[...])

    blk = pl.BlockSpec((None, tr, c), lambda l, i: (l, i, 0))
    return _call(name, body, (layers, r // tr),
                 [pl.BlockSpec((a.shape[0], tr, c), lambda l, i: (0, i, 0)) for a in flat] + [blk] * 3,
                 [blk] * 4, [_sds(w.shape, F32)] * 4, (*flat, w, m, v), comm=comm)


def _adamw_small(recvs, ws, ms, vs, own_row):
    n = len(recvs)

    def body(*refs):
        r_refs, w_refs, m_refs, v_refs = (refs[i * n:(i + 1) * n] for i in range(4))
        outs, scr = refs[4 * n:8 * n], refs[8 * n:]
        me = _my_place()[3]
        for a in range(n):
            g = _sum_in_device_order(r_refs[a])
            if own_row[a]:
                scr[0][...] = g
                g = scr[0][pl.ds(me, 1), :]
            g_ref, d_ref, nm_ref, nv_ref = outs[4 * a:4 * a + 4]
            g_ref[...] = g
            d_ref[...], nm_ref[...], nv_ref[...] = _adamw(w_refs[a][...], g, m_refs[a][...], v_refs[a][...])

    out_shape = []
    for w in ws:
        out_shape += [_sds(w.shape, F32)] * 4
    return pl.pallas_call(
        body, name="adamw_small", in_specs=[VMEM] * (4 * n), out_specs=[VMEM] * (4 * n), out_shape=out_shape,
        scratch_shapes=[pltpu.VMEM((N_DEV, GATE_DIM // N_DEV), F32)],
    )(*recvs, *ws, *ms, *vs)


BIG = ("a_w_in", "a_w_out", "b_w_q_a", "b_w_q_b", "b_w_o", "kv_w_a", "kv_w_b", "mlp_w1", "mlp_w2")
SMALL = ("norm_mix_g", "norm_mlp_g", "a_ln_v_g", "a_ln_v_b", "a_w_s", "a_b_s", "b_q_norm_g", "kv_src_norm_g",
         "kv_a_norm_g", "final_norm_g")
WEIGHTS = ("norm_mix_g", "norm_mlp_g", "a_w_in", "a_ln_v_g", "a_ln_v_b", "a_w_s", "a_b_s", "a_w_out", "b_w_q_a",
           "b_q_norm_g", "b_w_q_b", "b_w_o", "kv_src_norm_g", "kv_w_a", "kv_a_norm_g", "kv_w_b", "mlp_w1", "mlp_w2",
           "final_norm_g")


def _two_d(name, a):
    if name in ("a_w_s", "a_b_s"):
        return a.reshape(a.shape[1:])
    return a.reshape(1, -1) if a.ndim == 1 else a


def _three_d(a):
    return a if a.ndim == 3 else a.reshape((1,) + a.shape)


def kernel(x, positions, norm_mix_g, norm_mlp_g, a_w_in, a_ln_v_g, a_ln_v_b, a_w_s, a_b_s, a_w_out, b_w_q_a, b_q_norm_g, b_w_q_b, b_w_o, kv_src_norm_g, kv_w_a, kv_a_norm_g, kv_w_b, mlp_w1, mlp_w2, final_norm_g, loss_target, m_norm_mix_g, m_norm_mlp_g, m_a_w_in, m_a_ln_v_g, m_a_ln_v_b, m_a_w_s, m_a_b_s, m_a_w_out, m_b_w_q_a, m_b_q_norm_g, m_b_w_q_b, m_b_w_o, m_kv_src_norm_g, m_kv_w_a, m_kv_a_norm_g, m_kv_w_b, m_mlp_w1, m_mlp_w2, m_final_norm_g, v_norm_mix_g, v_norm_mlp_g, v_a_w_in, v_a_ln_v_g, v_a_ln_v_b, v_a_w_s, v_a_b_s, v_a_w_out, v_b_w_q_a, v_b_q_norm_g, v_b_w_q_b, v_b_w_o, v_kv_src_norm_g, v_kv_w_a, v_kv_a_norm_g, v_kv_w_b, v_mlp_w1, v_mlp_w2, v_final_norm_g):
    w = dict(norm_mix_g=norm_mix_g, norm_mlp_g=norm_mlp_g, a_w_in=a_w_in, a_ln_v_g=a_ln_v_g, a_ln_v_b=a_ln_v_b,
             a_w_s=a_w_s, a_b_s=a_b_s, a_w_out=a_w_out, b_w_q_a=b_w_q_a, b_q_norm_g=b_q_norm_g, b_w_q_b=b_w_q_b,
             b_w_o=b_w_o, kv_src_norm_g=kv_src_norm_g, kv_w_a=kv_w_a, kv_a_norm_g=kv_a_norm_g, kv_w_b=kv_w_b,
             mlp_w1=mlp_w1, mlp_w2=mlp_w2, final_norm_g=final_norm_g)
    m = dict(norm_mix_g=m_norm_mix_g, norm_mlp_g=m_norm_mlp_g, a_w_in=m_a_w_in, a_ln_v_g=m_a_ln_v_g,
             a_ln_v_b=m_a_ln_v_b, a_w_s=m_a_w_s, a_b_s=m_a_b_s, a_w_out=m_a_w_out, b_w_q_a=m_b_w_q_a,
             b_q_norm_g=m_b_q_norm_g, b_w_q_b=m_b_w_q_b, b_w_o=m_b_w_o, kv_src_norm_g=m_kv_src_norm_g,
             kv_w_a=m_kv_w_a, kv_a_norm_g=m_kv_a_norm_g, kv_w_b=m_kv_w_b, mlp_w1=m_mlp_w1, mlp_w2=m_mlp_w2,
             final_norm_g=m_final_norm_g)
    v = dict(norm_mix_g=v_norm_mix_g, norm_mlp_g=v_norm_mlp_g, a_w_in=v_a_w_in, a_ln_v_g=v_a_ln_v_g,
             a_ln_v_b=v_a_ln_v_b, a_w_s=v_a_w_s, a_b_s=v_a_b_s, a_w_out=v_a_w_out, b_w_q_a=v_b_w_q_a,
             b_q_norm_g=v_b_q_norm_g, b_w_q_b=v_b_w_q_b, b_w_o=v_b_w_o, kv_src_norm_g=v_kv_src_norm_g,
             kv_w_a=v_kv_w_a, kv_a_norm_g=v_kv_a_norm_g, kv_w_b=v_kv_w_b, mlp_w1=v_mlp_w1, mlp_w2=v_mlp_w2,
             final_norm_g=v_final_norm_g)
    t = x.shape[1]

    first = ("a_w_in", "a_w_out", "a_ln_v_g", "a_ln_v_b")
    later = ("mlp_w1_0", "mlp_w2_0", "mlp_w1_1", "mlp_w2_1", "kv_w_a", "kv_w_b", "b_w_q_a", "b_w_q_b", "b_w_o")
    blocks = {k: _three_d(w[k]) for k in BIG if not k.startswith("mlp")}
    for k in ("mlp_w1", "mlp_w2"):
        blocks[k + "_0"], blocks[k + "_1"] = w[k][0:1], w[k][1:2]
    got, casts = _gather_first([blocks[k] if k in blocks else w[k] for k in first], [blocks[k] for k in later])
    wg = dict(zip(first, got))
    wg["a_w_out"] = wg["a_w_out"].reshape(GATE_DIM, D_MODEL)
    wg["a_ln_v_g"] = wg["a_ln_v_g"].reshape(1, GATE_DIM)
    wg["a_ln_v_b"] = wg["a_ln_v_b"].reshape(1, GATE_DIM)
    shards = dict(zip(later, casts))

    sm = {k: _two_d(k, w[k]) for k in SMALL if k not in ("a_ln_v_g", "a_ln_v_b")}
    sm["a_b_st"] = sm["a_b_s"].T
    inv_freq = (ROPE_THETA ** (-jnp.arange(0, QK_ROPE, 2, dtype=F32) / QK_ROPE)).reshape(1, QK_ROPE // 2)

    loss, dx, g, small = _local_step(x[0], positions.reshape(t, 1), loss_target[0], inv_freq, wg, sm, shards)
    loss = lax.psum(loss[0, 0], ("x", "y", "c"))

    sums = _pair_reduce("pair_reduce_a", [g["a_w_in"]])
    carried = {"mlp_w1": _chip_exchange_comm(sums, rels=(2,), own=True),
               "mlp_w2": _chip_exchange_comm(sums, rels=(0, 1), own=False)}
    out, a_w_in_blocks = {}, []
    for k in BIG[1:]:
        recvs = [[g[k + "_0"]], [g[k + "_1"]]] if k in carried else [[g[k]]]
        res, got = _adamw_sharded("adamw_" + k, recvs, _three_d(w[k]), _three_d(m[k]), _three_d(v[k]),
                                  comm=carried.get(k))
        a_w_in_blocks += got
        out[k] = [o.reshape(w[k].shape) for o in res]
    k = BIG[0]
    res, _ = _adamw_sharded("adamw_" + k, [a_w_in_blocks], _three_d(w[k]), _three_d(m[k]), _three_d(v[k]))
    out[k] = [o.reshape(w[k].shape) for o in res]
    own_row = [k in ("a_ln_v_g", "a_ln_v_b") for k in SMALL]
    res = _adamw_small([small[k] for k in SMALL], [_two_d(k, w[k]) for k in SMALL], [_two_d(k, m[k]) for k in SMALL],
                       [_two_d(k, v[k]) for k in SMALL], own_row)
    for i, k in enumerate(SMALL):
        out[k] = [o.reshape(w[k].shape) for o in res[4 * i:4 * i + 4]]

    return (loss, dx.reshape(x.shape), *[out[k][0] for k in WEIGHTS], *[out[k][1] for k in WEIGHTS],
            *[out[k][2] for k in WEIGHTS], *[out[k][3] for k in WEIGHTS])
```

```python
import math

import jax
import jax.numpy as jnp
from jax import lax
from jax.experimental import pallas as pl
from jax.experimental.pallas import tpu as pltpu

F32, BF16 = jnp.float32, jnp.bfloat16
MESH = pl.DeviceIdType.MESH
ANY = pl.BlockSpec(memory_space=pl.ANY)
VMEM = pl.BlockSpec(memory_space=pltpu.VMEM)

N_DEV = 8
D_MODEL = 1024
CHUNK = 64
GMLP_BLOCK = 128
GATE_DIM = 2048
A_GROUPS = 8
A_GROUP_DIM = GATE_DIM // A_GROUPS
B_HEADS = 8
QK_NOPE, QK_ROPE, V_HEAD = 128, 64, 128
Q_LORA, KV_LORA = 384, 256
ROPE_THETA = 10000.0
D_FF = 4096
FF_SLOT = D_FF // N_DEV
EPS = 1e-6
ATT_SCALE = (QK_NOPE + QK_ROPE) ** -0.5

ADAM_LR, ADAM_B1, ADAM_B2, ADAM_EPS, ADAM_WD, ADAM_STEP = 0.001, 0.9, 0.999, 1e-08, 0.01, 10

TM = 256
TM_GATE = 128
VMEM_LIMIT = 56 * 1024 * 1024
INV_SQRT2 = 1.0 / math.sqrt(2.0)
INV_SQRT_2PI = 1.0 / math.sqrt(2.0 * math.pi)


def _dot(a, b):
    return jnp.dot(a, b, preferred_element_type=F32)


def _dot_nt(a, b):
    return lax.dot_general(a, b, (((1,), (1,)), ((), ())), preferred_element_type=F32)


def _dot_tn(a, b):
    return lax.dot_general(a, b, (((0,), (0,)), ((), ())), preferred_element_type=F32)


def _rms_fwd(x, g):
    rstd = lax.rsqrt(jnp.mean(x * x, axis=-1, keepdims=True) + EPS)
    xhat = x * rstd
    return xhat * g, xhat, rstd


def _rms_bwd(dy, xhat, rstd, g):
    dxhat = dy * g
    dx = rstd * (dxhat - xhat * jnp.mean(dxhat * xhat, axis=-1, keepdims=True))
    return dx, jnp.sum(dy * xhat, axis=0, keepdims=True)


def _ln_fwd(v, g, b):
    mu = jnp.mean(v, axis=-1, keepdims=True)
    vc = v - mu
    rstd = lax.rsqrt(jnp.mean(vc * vc, axis=-1, keepdims=True) + EPS)
    vhat = vc * rstd
    return vhat * g + b, vhat, rstd


def _gelu(x):
    return 0.5 * x * (1.0 + lax.erf(x * INV_SQRT2))


def _gelu_and_grad(x):
    cdf = 0.5 * (1.0 + lax.erf(x * INV_SQRT2))
    return x * cdf, cdf + x * jnp.exp(-0.5 * x * x) * INV_SQRT_2PI


def _rope(x, cos, sin):
    x1, x2 = x[:, :QK_ROPE // 2], x[:, QK_ROPE // 2:]
    return jnp.concatenate([x1 * cos - x2 * sin, x2 * cos + x1 * sin], axis=-1)


def _gate_mask():
    row = lax.broadcasted_iota(jnp.int32, (GMLP_BLOCK, GMLP_BLOCK), 0)
    col = lax.broadcasted_iota(jnp.int32, (GMLP_BLOCK, GMLP_BLOCK), 1)
    return (col < CHUNK) | (row >= CHUNK)


def _att_mask(q0, tq, t):
    q = q0 + lax.broadcasted_iota(jnp.int32, (tq, t), 0)
    k = lax.broadcasted_iota(jnp.int32, (tq, t), 1)
    return jnp.right_shift(k, 6) <= jnp.right_shift(q, 6)


def _res(shape, imap=None):
    zeros = (0,) * len(shape)
    return pl.BlockSpec(shape, imap or (lambda i: zeros), pipeline_mode=pl.Buffered(1))


def _const(shape):
    zeros = (0,) * len(shape)
    return pl.BlockSpec(shape, lambda i: zeros)


def _row(d, tm=TM):
    return pl.BlockSpec((tm, d), lambda i: (i, 0))


def _heads(d):
    return pl.BlockSpec((B_HEADS, TM, d), lambda i: (0, i, 0))


def _sds(shape, dt):
    return jax.ShapeDtypeStruct(shape, dt)


def _acc(ref, val):
    @pl.when(pl.program_id(0) == 0)
    def _():
        ref[...] = jnp.zeros_like(ref)
    ref[...] += val


def _my_place():
    x, y, c = lax.axis_index("x"), lax.axis_index("y"), lax.axis_index("c")
    return x, y, c, 4 * x + 2 * y + c


def _peer(x, y, c, k):
    px = 1 - x if k & 4 else x
    py = 1 - y if k & 2 else y
    pc = 1 - c if k & 1 else c
    return (px, py, pc), 4 * px + 2 * py + pc


CHIPS = (2, 4, 6)


def _gather_copy(outs, send_sems, recv_sems, a, k, block, to, src=None):
    rows = outs[a].at[pl.ds(block, 1)]
    return pltpu.make_async_remote_copy(
        src_ref=rows if src is None else src, dst_ref=rows, send_sem=send_sems.at[a, k], recv_sem=recv_sems.at[a, k],
        device_id=to, device_id_type=MESH)


def _gather_start(srcs, outs, sems, only=None):
    send_sems, recv_sems, local_sems = sems
    x, y, c, me = _my_place()
    for a in range(len(srcs)) if only is None else (only,):
        pltpu.make_async_copy(srcs[a], outs[a].at[pl.ds(me, 1)], local_sems.at[a]).start()
        _gather_copy(outs, send_sems, recv_sems, a, 0, me, _peer(x, y, c, 1)[0], src=srcs[a]).start()
        for j, k in enumerate(CHIPS):
            _gather_copy(outs, send_sems, recv_sems, a, 1 + j, me, _peer(x, y, c, k)[0], src=srcs[a]).start()


def _gather_finish(srcs, outs, sems):
    send_sems, recv_sems, local_sems = sems
    x, y, c, me = _my_place()
    sib, sib_i = _peer(x, y, c, 1)
    n = len(srcs)
    for a in range(n):
        for j, k in enumerate(CHIPS):
            block = _peer(x, y, c, k)[1]
            _gather_copy(outs, send_sems, recv_sems, a, 1 + j, block, sib).wait_recv()
            _gather_copy(outs, send_sems, recv_sems, a, 4 + j, block, sib).start()
    for a in range(n):
        _gather_copy(outs, send_sems, recv_sems, a, 0, sib_i, sib).wait_recv()
        for j, k in enumerate(CHIPS):
            _gather_copy(outs, send_sems, recv_sems, a, 4 + j, _peer(x, y, c, k ^ 1)[1], sib).wait_recv()
    for a in range(n):
        for k in range(7):
            _gather_copy(outs, send_sems, recv_sems, a, k, me, sib, src=srcs[a] if k < 4 else None).wait_send()
        pltpu.make_async_copy(srcs[a], outs[a].at[pl.ds(me, 1)], local_sems.at[a]).wait()


def _gather_sems(n):
    return [pltpu.SemaphoreType.DMA((n, 7)), pltpu.SemaphoreType.DMA((n, 7)), pltpu.SemaphoreType.DMA((n,))]


class _Comm:
    def __init__(self, args, out_shape, scratch, start, finish):
        self.args, self.out_shape, self.scratch, self.start, self.finish = args, out_shape, scratch, start, finish


def _gather_comm(shards):
    return _Comm(list(shards), [_sds((N_DEV,) + s.shape[1:], s.dtype) for s in shards], _gather_sems(len(shards)),
                 _gather_start, _gather_finish)


def _direct_copies(ins, outs, sems, wait, from_block):
    send_sems, recv_sems, local_sems = sems
    x, y, c, me = _my_place()
    for a in range(len(ins)):
        src = ins[a].at[pl.ds(me, 1)] if from_block[a] else ins[a]
        local = pltpu.make_async_copy(src, outs[a].at[pl.ds(me, 1)], local_sems.at[a])
        local.wait() if wait else local.start()
        for k in range(1, N_DEV):
            to, to_i = _peer(x, y, c, k)
            cp = pltpu.make_async_remote_copy(
                src_ref=ins[a].at[pl.ds(to_i, 1)] if from_block[a] else ins[a], dst_ref=outs[a].at[pl.ds(me, 1)],
                send_sem=send_sems.at[a, k - 1], recv_sem=recv_sems.at[a, k - 1], device_id=to, device_id_type=MESH)
            cp.wait() if wait else cp.start()


def _exchange_comm(grads=(), parts=()):
    ins = list(grads) + list(parts)
    from_block = [True] * len(grads) + [False] * len(parts)
    out_shape = [_sds(g.shape, g.dtype) for g in grads] + [_sds((N_DEV,) + p.shape[1:], p.dtype) for p in parts]

    def start(ins_, outs_, sems_):
        _direct_copies(ins_, outs_, sems_, False, from_block)

    def finish(ins_, outs_, sems_):
        _direct_copies(ins_, outs_, sems_, True, from_block)

    return _Comm(ins, out_shape, _gather_sems(len(ins)), start, finish)


def _chip_copies(ins, outs, sems, wait, rels, own):
    send_sems, recv_sems, local_sems = sems
    x, y, c, _ = _my_place()
    for a in range(len(ins)):
        if own:
            local = pltpu.make_async_copy(ins[a].at[pl.ds(2 * x + y, 1)], outs[a].at[pl.ds(len(rels), 1)],
                                          local_sems.at[a])
            local.wait() if wait else local.start()
        for i, j in enumerate(rels):
            to = _peer(x, y, c, CHIPS[j])[0]
            cp = pltpu.make_async_remote_copy(
                src_ref=ins[a].at[pl.ds(2 * to[0] + to[1], 1)], dst_ref=outs[a].at[pl.ds(i, 1)],
                send_sem=send_sems.at[a, i], recv_sem=recv_sems.at[a, i], device_id=to, device_id_type=MESH)
            cp.wait() if wait else cp.start()


def _chip_exchange_comm(sums, rels=(0, 1, 2), own=True):
    def start(ins_, outs_, sems_):
        _chip_copies(ins_, outs_, sems_, False, rels, own)

    def finish(ins_, outs_, sems_):
        _chip_copies(ins_, outs_, sems_, True, rels, own)

    n = len(sums)
    sems = [pltpu.SemaphoreType.DMA((n, len(rels))), pltpu.SemaphoreType.DMA((n, len(rels))),
            pltpu.SemaphoreType.DMA((n,))]
    return _Comm(list(sums), [_sds((len(rels) + own,) + s.shape[1:], s.dtype) for s in sums], sems, start, finish)


def _pair_reduce(name, grads):
    n = len(grads)
    n_chips = N_DEV // 2

    def body(*refs):
        g_refs, gh_refs, p_refs, land = refs[:n], refs[n:2 * n], refs[2 * n:3 * n], refs[3 * n:4 * n]
        send_sems, recv_sems = refs[4 * n:]
        x, y, c, _ = _my_place()
        sib = _peer(x, y, c, 1)[0]
        q = pl.program_id(0)

        def to_sibling(a, j):
            return pltpu.make_async_remote_copy(
                src_ref=gh_refs[a].at[j, pl.ds(1 - c, 1)], dst_ref=land[a].at[pl.ds(j, 1)],
                send_sem=send_sems.at[a, j], recv_sem=recv_sems.at[a, j], device_id=sib, device_id_type=MESH)

        @pl.when(q == 0)
        def _():
            for j in range(n_chips):
                for a in range(n):
                    to_sibling(a, j).start()

        for a in range(n):
            to_sibling(a, q).wait_recv()
            p_refs[a][...] = (g_refs[a][0, pl.ds(c, 1)].astype(F32) + land[a][pl.ds(q, 1)].astype(F32)).astype(BF16)

        @pl.when(q == n_chips - 1)
        def _():
            for a in range(n):
                for j in range(n_chips):
                    to_sibling(a, j).wait_send()

    views = [g.reshape((n_chips, 2) + g.shape[1:]) for g in grads]
    res = pl.pallas_call(
        body, name=name, grid=(n_chips,),
        in_specs=[pl.BlockSpec((1, 2) + g.shape[1:], lambda q: (q, 0, 0, 0)) for g in grads] + [ANY] * n,
        out_specs=[pl.BlockSpec((1,) + g.shape[1:], lambda q: (q, 0, 0)) for g in grads],
        out_shape=[_sds((n_chips,) + g.shape[1:], BF16) for g in grads],
        scratch_shapes=[pltpu.VMEM((n_chips,) + g.shape[1:], BF16) for g in grads]
        + [pltpu.SemaphoreType.DMA((n, n_chips)), pltpu.SemaphoreType.DMA((n, n_chips))],
        compiler_params=pltpu.CompilerParams(dimension_semantics=("arbitrary",), vmem_limit_bytes=VMEM_LIMIT),
    )(*views, *views)
    return list(res)


def _call(name, body, grid, in_specs, out_specs, out_shape, args, scratch=(), comm=None):
    params = pltpu.CompilerParams(dimension_semantics=("arbitrary",) * len(grid), vmem_limit_bytes=VMEM_LIMIT)
    if comm is None:
        outs = pl.pallas_call(body, name=name, grid=grid, in_specs=list(in_specs), out_specs=list(out_specs),
                              out_shape=list(out_shape), scratch_shapes=list(scratch), compiler_params=params)(*args)
        return list(outs), []
    ni, nci, no, nco, ns = len(in_specs), len(comm.args), len(out_specs), len(comm.out_shape), len(scratch)

    def carrying(*refs):
        ins, refs = refs[:ni], refs[ni:]
        cin, refs = refs[:nci], refs[nci:]
        outs, refs = refs[:no], refs[no:]
        cout, refs = refs[:nco], refs[nco:]
        scr, csems = refs[:ns], refs[ns:]
        ids = [pl.program_id(ax) for ax in range(len(grid))]
        first, last = ids[0] == 0, ids[0] == grid[0] - 1
        for ax in range(1, len(grid)):
            first, last = first & (ids[ax] == 0), last & (ids[ax] == grid[ax] - 1)

        @pl.when(first)
        def _():
            comm.start(cin, cout, csems)

        body(*ins, *outs, *scr)

        @pl.when(last)
        def _():
            comm.finish(cin, cout, csems)

    outs = pl.pallas_call(
        carrying, name=name, grid=grid, in_specs=list(in_specs) + [ANY] * nci, out_specs=list(out_specs) + [ANY] * nco,
        out_shape=list(out_shape) + list(comm.out_shape), scratch_shapes=list(scratch) + list(comm.scratch),
        compiler_params=params)(*args, *comm.args)
    return list(outs[:no]), list(outs[no:])


def _comm_only(name, comm):
    def body(*refs):
        nci, nco = len(comm.args), len(comm.out_shape)
        cin, cout, csems = refs[:nci], refs[nci:nci + nco], refs[nci + nco:]
        comm.start(cin, cout, csems)
        comm.finish(cin, cout, csems)

    return pl.pallas_call(body, name=name, in_specs=[ANY] * len(comm.args), out_specs=[ANY] * len(comm.out_shape),
                          out_shape=list(comm.out_shape), scratch_shapes=list(comm.scratch))(*comm.args)


def _gather_first(first, later):
    nf, nl = len(first), len(later)
    dts = [BF16] * (nf - 2) + [F32, F32]

    def body(*refs):
        ins, refs = refs[:nf + nl], refs[nf + nl:]
        outs, refs = refs[:nf], refs[nf:]
        casts, refs = refs[:nl], refs[nl:]
        stage, sems = refs[:nf], refs[nf:]
        for a in range(nf):
            stage[a][...] = ins[a][...].astype(dts[a])
            _gather_start(stage, outs, sems, only=a)
        for a in range(nl):
            casts[a][...] = ins[nf + a][...].astype(BF16)
        _gather_finish(stage, outs, sems)

    res = pl.pallas_call(
        body, name="gather_first",
        in_specs=[VMEM] * (nf + nl), out_specs=[ANY] * nf + [VMEM] * nl,
        out_shape=[_sds((N_DEV,) + s.shape[1:], dt) for s, dt in zip(first, dts)]
        + [_sds(s.shape, BF16) for s in later],
        scratch_shapes=[pltpu.VMEM(s.shape, dt) for s, dt in zip(first, dts)] + _gather_sems(nf),
        compiler_params=pltpu.CompilerParams(vmem_limit_bytes=VMEM_LIMIT),
    )(*first, *later)
    return list(res[:nf]), list(res[nf:])


def _a_mix_fwd(x, g, w_in, ln_g, ln_b, w_s, b_st, w_out, comm=None):
    t = x.shape[0]
    nblk = TM // GMLP_BLOCK

    def body(x_ref, g_ref, win_ref, lng_ref, lnb_ref, ws_ref, bst_ref, wout_ref, h_ref, z_ref, gated_scr):
        xv = x_ref[...]
        hb = _rms_fwd(xv, g_ref[...])[0].astype(BF16)
        for d in range(N_DEV):
            z_ref[:, d * FF_SLOT:(d + 1) * FF_SLOT] = _dot(hb, win_ref[d])
        u = _gelu(z_ref[:, :GATE_DIM])
        vb = _ln_fwd(_gelu(z_ref[:, GATE_DIM:]), lng_ref[...], lnb_ref[...])[0].astype(BF16)
        mask = _gate_mask()
        for gi in range(A_GROUPS):
            wm = jnp.where(mask, ws_ref[gi], 0.0).astype(BF16)
            bias = bst_ref[:, gi:gi + 1]
            cs = slice(gi * A_GROUP_DIM, (gi + 1) * A_GROUP_DIM)
            for n in range(nblk):
                rs = slice(n * GMLP_BLOCK, (n + 1) * GMLP_BLOCK)
                sv = _dot(wm, vb[rs, cs]) + bias
                gated_scr[rs, cs] = (u[rs, cs] * sv).astype(BF16)
        h_ref[...] = xv + _dot(gated_scr[...], wout_ref[...])

    return _call(
        "a_mix_fwd", body, (t // TM,),
        [_row(D_MODEL), _res((1, D_MODEL)), _res((N_DEV, D_MODEL, FF_SLOT)), _res((1, GATE_DIM)),
         _res((1, GATE_DIM)), _res((A_GROUPS, GMLP_BLOCK, GMLP_BLOCK)), _res((GMLP_BLOCK, A_GROUPS)),
         _res((GATE_DIM, D_MODEL))],
        [_row(D_MODEL), _row(2 * GATE_DIM), _row(GATE_DIM)],
        [_sds((t, D_MODEL), F32), _sds((t, 2 * GATE_DIM), F32), _sds((t, GATE_DIM), BF16)],
        (x, g, w_in, ln_g, ln_b, w_s, b_st, w_out), comm=comm)


MLP_W_SPECS = (_res((N_DEV, D_MODEL, FF_SLOT)), _res((N_DEV, FF_SLOT, D_MODEL)))


def _mlp_fwd(h, g, w1, w2, layer, comm=None):
    t = h.shape[0]

    def body(h_ref, g_ref, w1_ref, w2_ref, o_ref, a_ref):
        hv = h_ref[...]
        hb = _rms_fwd(hv, g_ref[...])[0].astype(BF16)
        o_ref[...] = hv
        for d in range(N_DEV):
            a = _dot(hb, w1_ref[d])
            a_ref[:, d * FF_SLOT:(d + 1) * FF_SLOT] = a
            r = jnp.maximum(a, 0.0)
            o_ref[...] += _dot((r * r).astype(BF16), w2_ref[d])

    return _call(
        f"mlp_fwd_{layer}", body, (t // TM,), [_row(D_MODEL), _res((1, D_MODEL)), *MLP_W_SPECS],
        [_row(D_MODEL), _row(D_FF)], [_sds((t, D_MODEL), F32), _sds((t, D_FF), F32)], (h, g, w1, w2), comm=comm)


KVQ_W_SPECS = (_res((1, D_MODEL)), _res((D_MODEL, KV_LORA + QK_ROPE)), _res((1, KV_LORA)),
               _res((B_HEADS, KV_LORA, QK_NOPE + V_HEAD)), _res((1, D_MODEL)), _res((D_MODEL, Q_LORA)),
               _res((1, Q_LORA)), _res((B_HEADS, Q_LORA, QK_NOPE + QK_ROPE)))


def _kvq_fwd(h, pos, inv_freq, kvq_w):
    t = h.shape[0]
    half = QK_ROPE // 2

    def body(h_ref, pos_ref, invf_ref, srcg_ref, wkva_ref, kvag_ref, wkvb_ref, mixg_ref, wqa_ref, qg_ref, wqb_ref,
             ckv_ref, kn_ref, v_ref, kpe_ref, cqpre_ref, q_ref, cos_ref, sin_ref):
        hv = h_ref[...]
        xhat = hv * lax.rsqrt(jnp.mean(hv * hv, axis=-1, keepdims=True) + EPS)
        ang = pos_ref[...].astype(F32) * invf_ref[...]
        cos, sin = jnp.cos(ang), jnp.sin(ang)
        cos_ref[...] = cos
        sin_ref[...] = sin
        ckv = _dot((xhat * srcg_ref[...]).astype(BF16), wkva_ref[...])
        ckv_ref[...] = ckv
        cb = _rms_fwd(ckv[:, :KV_LORA], kvag_ref[...])[0].astype(BF16)
        kpe_ref[...] = _rope(ckv[:, KV_LORA:], cos, sin).astype(BF16)
        for hd in range(B_HEADS):
            kv = _dot(cb, wkvb_ref[hd])
            kn_ref[hd] = kv[:, :QK_NOPE].astype(BF16)
            v_ref[hd] = kv[:, QK_NOPE:].astype(BF16)
        cqpre = _dot((xhat * mixg_ref[...]).astype(BF16), wqa_ref[...])
        cqpre_ref[...] = cqpre
        cqb = _rms_fwd(cqpre, qg_ref[...])[0].astype(BF16)
        for hd in range(B_HEADS):
            q = _dot(cqb, wqb_ref[hd])
            q_ref[hd, :, 0:QK_NOPE] = q[:, :QK_NOPE].astype(BF16)
            q_ref[hd, :, QK_NOPE:] = _rope(q[:, QK_NOPE:], cos, sin).astype(BF16)

    return _call(
        "kvq_fwd", body, (t // TM,), [_row(D_MODEL), _row(1), _res((1, half)), *KVQ_W_SPECS],
        [_row(KV_LORA + QK_ROPE), _heads(QK_NOPE), _heads(V_HEAD), _row(QK_ROPE), _row(Q_LORA),
         _heads(QK_NOPE + QK_ROPE), _row(half), _row(half)],
        [_sds((t, KV_LORA + QK_ROPE), F32), _sds((B_HEADS, t, QK_NOPE), BF16), _sds((B_HEADS, t, V_HEAD), BF16),
         _sds((t, QK_ROPE), BF16), _sds((t, Q_LORA), F32), _sds((B_HEADS, t, QK_NOPE + QK_ROPE), BF16),
         _sds((t, half), F32), _sds((t, half), F32)],
        (h, pos, inv_freq, *kvq_w))[0]


def _softmax_rows(qn, qp, kn_ref, kpe_ref, k):
    past, upto = k * TM, (k + 1) * TM
    s = (_dot_nt(qn, kn_ref[0:upto, :]) + _dot_nt(qp, kpe_ref[0:upto, :])) * ATT_SCALE
    own = jnp.where(_att_mask(0, TM, TM), s[:, past:], jnp.finfo(F32).min)
    s = own if k == 0 else jnp.concatenate([s[:, :past], own], axis=1)
    e = jnp.exp(s - jnp.max(s, axis=-1, keepdims=True))
    return e * (1.0 / jnp.sum(e, axis=-1, keepdims=True))


def _for_my_tile(i, nq, fn):
    for k in range(nq):
        @pl.when(i == k)
        def _(k=k):
            fn(k)


def _attn_fwd(h, q, kn, kpe, v, w_o, comm=None):
    t = h.shape[0]
    nq = t // TM

    def body(h_ref, q_ref, kn_ref, kpe_ref, v_ref, wo_ref, o_ref, att_ref):
        i, hd = pl.program_id(0), pl.program_id(1)

        @pl.when(hd == 0)
        def _():
            o_ref[...] = h_ref[...]

        def tile(k):
            p = _softmax_rows(q_ref[:, 0:QK_NOPE], q_ref[:, QK_NOPE:], kn_ref.at[hd], kpe_ref, k)
            ob = _dot(p.astype(BF16), v_ref[hd, 0:(k + 1) * TM, :]).astype(BF16)
            att_ref[...] = ob
            o_ref[...] += _dot(ob, wo_ref[hd])

        _for_my_tile(i, nq, tile)

    def per_head(d):
        return pl.BlockSpec((None, TM, d), lambda i, hd: (hd, i, 0))

    def resident(shape):
        zeros = (0,) * len(shape)
        return pl.BlockSpec(shape, lambda i, hd: zeros, pipeline_mode=pl.Buffered(1))

    tile_spec = pl.BlockSpec((TM, D_MODEL), lambda i, hd: (i, 0))
    return _call(
        "attn_fwd", body, (nq, B_HEADS),
        [tile_spec, per_head(QK_NOPE + QK_ROPE), resident((B_HEADS, t, QK_NOPE)), resident((t, QK_ROPE)),
         resident((B_HEADS, t, V_HEAD)), resident((B_HEADS, V_HEAD, D_MODEL))],
        [tile_spec, per_head(V_HEAD)], [_sds((t, D_MODEL), F32), _sds((B_HEADS, t, V_HEAD), BF16)],
        (h, q, kn, kpe, v, w_o), comm=comm)


def _loss_head(h, g, target):
    t = h.shape[0]

    def body(h_ref, g_ref, t_ref, loss_ref, dh_ref, dg_ref):
        y, xhat, rstd = _rms_fwd(h_ref[...], g_ref[...])
        err = y - t_ref[...]
        part = 0.5 * jnp.sum(jnp.mean(err * err, axis=-1, keepdims=True), axis=0, keepdims=True)
        dx, dg = _rms_bwd(err * (1.0 / D_MODEL), xhat, rstd, g_ref[...])
        dh_ref[...] = dx
        _acc(dg_ref, dg)
        _acc(loss_ref, part)

    return _call(
        "loss_head", body, (t // TM,), [_row(D_MODEL), _res((1, D_MODEL)), _row(D_MODEL)],
        [_const((1, 1)), _row(D_MODEL), _const((1, D_MODEL))],
        [_sds((1, 1), F32), _sds((t, D_MODEL), F32), _sds((1, D_MODEL), F32)], (h, g, target))[0]


def _mlp_bwd(h, a, dho, g, w1, w2, layer, comm=None):
    t = h.shape[0]

    def body(h_ref, a_ref, dho_ref, g_ref, w1_ref, w2_ref, dhi_ref, dg_ref, hn_ref, f_ref, da_ref, dhib_ref):
        gv = g_ref[...]
        y, xhat, rstd = _rms_fwd(h_ref[...], gv)
        hn_ref[...] = y.astype(BF16)
        dho_v = dho_ref[...]
        dhob = dho_v.astype(BF16)
        dhn = jnp.zeros((TM, D_MODEL), F32)
        for d in range(N_DEV):
            cs = slice(d * FF_SLOT, (d + 1) * FF_SLOT)
            r = jnp.maximum(a_ref[:, cs], 0.0)
            f_ref[:, cs] = (r * r).astype(BF16)
            da = (_dot_nt(dhob, w2_ref[d]) * (2.0 * r)).astype(BF16)
            da_ref[:, cs] = da
            dhn = dhn + _dot_nt(da, w1_ref[d])
        dx, dg = _rms_bwd(dhn, xhat, rstd, gv)
        dhi = dho_v + dx
        dhi_ref[...] = dhi
        dhib_ref[...] = dhi.astype(BF16)
        _acc(dg_ref, dg)

    return _call(
        f"mlp_bwd_{layer}", body, (t // TM,),
        [_row(D_MODEL), _row(D_FF), _row(D_MODEL), _res((1, D_MODEL)), *MLP_W_SPECS],
        [_row(D_MODEL), _const((1, D_MODEL)), _row(D_MODEL), _row(D_FF), _row(D_FF), _row(D_MODEL)],
        [_sds((t, D_MODEL), F32), _sds((1, D_MODEL), F32), _sds((t, D_MODEL), BF16), _sds((t, D_FF), BF16),
         _sds((t, D_FF), BF16), _sds((t, D_MODEL), BF16)],
        (h, a, dho, g, w1, w2), comm=comm)


def _attn_bwd(dh, q, kn, kpe, v, w_o, cos, sin, comm=None):
    t = dh.shape[0]
    half = QK_ROPE // 2

    def body(dh_ref, q_ref, kn_ref, kpe_ref, v_ref, wo_ref, cos_ref, sin_ref, dq_ref, dkn_ref, dv_ref, dkpe_ref):
        hd, i = pl.program_id(0), pl.program_id(1)

        @pl.when(i == 0)
        def _():
            dkn_ref[...] = jnp.zeros_like(dkn_ref)
            dv_ref[...] = jnp.zeros_like(dv_ref)

        @pl.when((i == 0) & (hd == 0))
        def _():
            dkpe_ref[...] = jnp.zeros_like(dkpe_ref)

        def tile(k):
            keys = slice(0, (k + 1) * TM)
            qn, qp = q_ref[:, 0:QK_NOPE], q_ref[:, QK_NOPE:]
            do = _dot_nt(dh_ref[k * TM:(k + 1) * TM, :], wo_ref[...]).astype(BF16)
            p = _softmax_rows(qn, qp, kn_ref, kpe_ref, k)
            dp = _dot_nt(do, v_ref[keys, :])
            ds = (p * (dp - jnp.sum(p * dp, axis=-1, keepdims=True)) * ATT_SCALE).astype(BF16)
            dq_ref[:, 0:QK_NOPE] = _dot(ds, kn_ref[keys, :]).astype(BF16)
            dq_ref[:, QK_NOPE:] = _rope(_dot(ds, kpe_ref[keys, :]), cos_ref[...], -sin_ref[...]).astype(BF16)
            dkn_ref[keys, :] += _dot_tn(ds, qn)
            dv_ref[keys, :] += _dot_tn(p.astype(BF16), do)
            dkpe_ref[keys, :] += _dot_tn(ds, qp)

        _for_my_tile(i, t // TM, tile)

    def per_head(rows, d, tiled):
        return pl.BlockSpec((None, rows, d), (lambda hd, i: (hd, i, 0)) if tiled else (lambda hd, i: (hd, 0, 0)))

    def tile(d):
        return pl.BlockSpec((TM, d), lambda hd, i: (i, 0))

    return _call(
        "attn_bwd", body, (B_HEADS, t // TM),
        [pl.BlockSpec((t, D_MODEL), lambda hd, i: (0, 0), pipeline_mode=pl.Buffered(1)),
         per_head(TM, QK_NOPE + QK_ROPE, True), per_head(t, QK_NOPE, False),
         pl.BlockSpec((t, QK_ROPE), lambda hd, i: (0, 0)), per_head(t, V_HEAD, False),
         per_head(V_HEAD, D_MODEL, False), tile(half), tile(half)],
        [per_head(TM, QK_NOPE + QK_ROPE, True), per_head(t, QK_NOPE, False), per_head(t, V_HEAD, False),
         pl.BlockSpec((t, QK_ROPE), lambda hd, i: (0, 0))],
        [_sds((B_HEADS, t, QK_NOPE + QK_ROPE), BF16), _sds((B_HEADS, t, QK_NOPE), F32),
         _sds((B_HEADS, t, V_HEAD), F32), _sds((t, QK_ROPE), F32)],
        (dh, q, kn, kpe, v, w_o, cos, sin), comm=comm)


def _kvq_bwd(h, dh, ckv, cqpre, dq, dkn, dv, dkpe, cos, sin, kvq_w):
    t = h.shape[0]
    half = QK_ROPE // 2

    def body(h_ref, dh_ref, ckv_ref, cqpre_ref, dq_ref, dkn_ref, dv_ref, dkpe_ref, cos_ref, sin_ref,
             srcg_ref, wkva_ref, kvag_ref, wkvb_ref, mixg_ref, wqa_ref, qg_ref, wqb_ref,
             dhi_ref, hq_ref, hk_ref, cq_ref, dcqpre_ref, c_ref, dkv_ref, dckv_ref,
             dmixg_ref, dsrcg_ref, dqg_ref, dkvag_ref):
        hv = h_ref[...]
        rstd = lax.rsqrt(jnp.mean(hv * hv, axis=-1, keepdims=True) + EPS)
        xhat = hv * rstd
        mixg, srcg, qg, kvag = mixg_ref[...], srcg_ref[...], qg_ref[...], kvag_ref[...]
        hq_ref[...] = (xhat * mixg).astype(BF16)
        hk_ref[...] = (xhat * srcg).astype(BF16)
        cq, cqhat, crstd = _rms_fwd(cqpre_ref[...], qg)
        cq_ref[...] = cq.astype(BF16)
        dcq = jnp.zeros((TM, Q_LORA), F32)
        for hd in range(B_HEADS):
            dcq = dcq + _dot_nt(dq_ref[hd], wqb_ref[hd])
        dcqpre, dqg = _rms_bwd(dcq, cqhat, crstd, qg)
        dcqpre_b = dcqpre.astype(BF16)
        dcqpre_ref[...] = dcqpre_b
        dxq, dmixg = _rms_bwd(_dot_nt(dcqpre_b, wqa_ref[...]), xhat, rstd, mixg)
        ckv = ckv_ref[...]
        c, chat, krstd = _rms_fwd(ckv[:, :KV_LORA], kvag)
        c_ref[...] = c.astype(BF16)
        dc = jnp.zeros((TM, KV_LORA), F32)
        for hd in range(B_HEADS):
            dkv = jnp.concatenate([dkn_ref[hd], dv_ref[hd]], axis=-1).astype(BF16)
            dkv_ref[hd] = dkv
            dc = dc + _dot_nt(dkv, wkvb_ref[hd])
        dlat, dkvag = _rms_bwd(dc, chat, krstd, kvag)
        dpe = _rope(dkpe_ref[...], cos_ref[...], -sin_ref[...])
        dckv_b = jnp.concatenate([dlat, dpe], axis=-1).astype(BF16)
        dckv_ref[...] = dckv_b
        dxk, dsrcg = _rms_bwd(_dot_nt(dckv_b, wkva_ref[...]), xhat, rstd, srcg)
        dhi_ref[...] = dh_ref[...] + dxq + dxk
        _acc(dmixg_ref, dmixg)
        _acc(dsrcg_ref, dsrcg)
        _acc(dqg_ref, dqg)
        _acc(dkvag_ref, dkvag)

    return _call(
        "kvq_bwd", body, (t // TM,),
        [_row(D_MODEL), _row(D_MODEL), _row(KV_LORA + QK_ROPE), _row(Q_LORA), _heads(QK_NOPE + QK_ROPE),
         _heads(QK_NOPE), _heads(V_HEAD), _row(QK_ROPE), _row(half), _row(half), *KVQ_W_SPECS],
        [_row(D_MODEL), _row(D_MODEL), _row(D_MODEL), _row(Q_LORA), _row(Q_LORA), _row(KV_LORA),
         _heads(QK_NOPE + V_HEAD), _row(KV_LORA + QK_ROPE),
         _const((1, D_MODEL)), _const((1, D_MODEL)), _const((1, Q_LORA)), _const((1, KV_LORA))],
        [_sds((t, D_MODEL), F32), _sds((t, D_MODEL), BF16), _sds((t, D_MODEL), BF16), _sds((t, Q_LORA), BF16),
         _sds((t, Q_LORA), BF16), _sds((t, KV_LORA), BF16), _sds((B_HEADS, t, QK_NOPE + V_HEAD), BF16),
         _sds((t, KV_LORA + QK_ROPE), BF16),
         _sds((1, D_MODEL), F32), _sds((1, D_MODEL), F32), _sds((1, Q_LORA), F32), _sds((1, KV_LORA), F32)],
        (h, dh, ckv, cqpre, dq, dkn, dv, dkpe, cos, sin, *kvq_w))[0]


def _a_mix_bwd(x, z, dh, g, w_in, ln_g, ln_b, w_s, b_st, w_out, comm=None):
    t = x.shape[0]
    tm = TM_GATE
    nblk = tm // GMLP_BLOCK

    def body(x_ref, z_ref, dh_ref, g_ref, win_ref, lng_ref, lnb_ref, ws_ref, bst_ref, wout_ref,
             dx_ref, hn_ref, dz_ref, dg_ref, dlng_ref, dlnb_ref, dws_ref, dbs_ref, dvn_scr, gelu_grad_v):
        @pl.when(pl.program_id(0) == 0)
        def _():
            dws_ref[...] = jnp.zeros_like(dws_ref)
            dbs_ref[...] = jnp.zeros_like(dbs_ref)

        gv, lng = g_ref[...], lng_ref[...]
        y, xhat, rstd = _rms_fwd(x_ref[...], gv)
        hn_ref[...] = y.astype(BF16)
        dhv = dh_ref[...]
        dgated = _dot_nt(dhv.astype(BF16), wout_ref[...])
        u, gelu_grad_u = _gelu_and_grad(z_ref[:, :GATE_DIM])
        v, gelu_grad_v[...] = _gelu_and_grad(z_ref[:, GATE_DIM:])
        vn, vhat, lrstd = _ln_fwd(v, lng, lnb_ref[...])
        vb = vn.astype(BF16)
        mask = _gate_mask()
        for gi in range(A_GROUPS):
            wm = jnp.where(mask, ws_ref[gi], 0.0).astype(BF16)
            bias = bst_ref[:, gi:gi + 1]
            cs = slice(gi * A_GROUP_DIM, (gi + 1) * A_GROUP_DIM)
            dws = jnp.zeros((GMLP_BLOCK, GMLP_BLOCK), F32)
            dbs = jnp.zeros((GMLP_BLOCK, 1), F32)
            for n in range(nblk):
                rs = slice(n * GMLP_BLOCK, (n + 1) * GMLP_BLOCK)
                sv = _dot(wm, vb[rs, cs]) + bias
                dz_ref[rs, cs] = (dgated[rs, cs] * sv * gelu_grad_u[rs, cs]).astype(BF16)
                dsv = dgated[rs, cs] * u[rs, cs]
                dsvb = dsv.astype(BF16)
                dws = dws + _dot_nt(dsvb, vb[rs, cs])
                dbs = dbs + jnp.sum(dsv, axis=-1, keepdims=True)
                dvn_scr[rs, cs] = _dot_tn(wm, dsvb)
            dws_ref[gi] += jnp.where(mask, dws, 0.0)
            dbs_ref[gi] += dbs
        dvn = dvn_scr[...]
        dvhat = dvn * lng
        dv = lrstd * (dvhat - jnp.mean(dvhat, axis=-1, keepdims=True)
                      - vhat * jnp.mean(dvhat * vhat, axis=-1, keepdims=True))
        dz_ref[:, GATE_DIM:] = (dv * gelu_grad_v[...]).astype(BF16)
        dhn = jnp.zeros((tm, D_MODEL), F32)
        for d in range(N_DEV):
            dhn = dhn + _dot_nt(dz_ref[:, d * FF_SLOT:(d + 1) * FF_SLOT], win_ref[d])
        dx, dg = _rms_bwd(dhn, xhat, rstd, gv)
        dx_ref[...] = dhv + dx
        _acc(dg_ref, dg)
        _acc(dlng_ref, jnp.sum(dvn * vhat, axis=0, keepdims=True))
        _acc(dlnb_ref, jnp.sum(dvn, axis=0, keepdims=True))

    return _call(
        "a_mix_bwd", body, (t // tm,),
        [_row(D_MODEL, tm), _row(2 * GATE_DIM, tm), _row(D_MODEL, tm), _res((1, D_MODEL)),
         _res((N_DEV, D_MODEL, FF_SLOT)), _res((1, GATE_DIM)), _res((1, GATE_DIM)),
         _res((A_GROUPS, GMLP_BLOCK, GMLP_BLOCK)), _res((GMLP_BLOCK, A_GROUPS)), _res((GATE_DIM, D_MODEL))],
        [_row(D_MODEL, tm), _row(D_MODEL, tm), _row(2 * GATE_DIM, tm),
         _const((1, D_MODEL)), _const((1, GATE_DIM)), _const((1, GATE_DIM)),
         _const((A_GROUPS, GMLP_BLOCK, GMLP_BLOCK)), _const((A_GROUPS, GMLP_BLOCK, 1))],
        [_sds((t, D_MODEL), F32), _sds((t, D_MODEL), BF16),
         _sds((t, 2 * GATE_DIM), BF16), _sds((1, D_MODEL), F32), _sds((1, GATE_DIM), F32),
         _sds((1, GATE_DIM), F32), _sds((A_GROUPS, GMLP_BLOCK, GMLP_BLOCK), F32),
         _sds((A_GROUPS, GMLP_BLOCK, 1), F32)],
        (x, z, dh, g, w_in, ln_g, ln_b, w_s, b_st, w_out),
        scratch=[pltpu.VMEM((tm, GATE_DIM), F32), pltpu.VMEM((tm, GATE_DIM), F32)], comm=comm)


def _wgrad(name, a, b, a_spec, b_spec, m, n, comm=None):
    def body(a_ref, b_ref, o_ref):
        o_ref[0] = _dot_tn(a_ref[...].astype(BF16), b_ref[...].astype(BF16)).astype(BF16)

    outs, got = _call(name, body, (N_DEV,), [a_spec, b_spec], [pl.BlockSpec((1, m, n), lambda d: (d, 0, 0))],
                      [_sds((N_DEV, m, n), BF16)], (a, b), comm=comm)
    return outs[0] if comm is None else (outs[0], got)


def _full(t, d):
    return pl.BlockSpec((t, d), lambda i: (0, 0), pipeline_mode=pl.Buffered(1))


def _cols(t, d):
    return pl.BlockSpec((t, d), lambda i: (0, i))


def _head(t, d):
    return pl.BlockSpec((None, t, d), lambda i: (i, 0, 0))


def _local_step(x, pos, target, inv_freq, wg, sm, shards=None):
    t = x.shape[0]
    wg = dict(wg)
    dist = shards is not None
    mix_g = [sm["norm_mix_g"][l:l + 1] for l in range(2)]
    mlp_g = [sm["norm_mlp_g"][l:l + 1] for l in range(2)]

    def gather(names):
        return _gather_comm([shards[k] for k in names]) if dist else None

    def send(grads):
        return _exchange_comm(grads=grads) if dist else None

    def send_sums(name, grads):
        return _chip_exchange_comm(_pair_reduce(name, grads)) if dist else None

    def a_args():
        return (wg["a_w_in"], wg["a_ln_v_g"], wg["a_ln_v_b"], sm["a_w_s"], sm["a_b_st"], wg["a_w_out"])

    def kvq_w():
        return (sm["kv_src_norm_g"], wg["kv_w_a"], sm["kv_a_norm_g"], wg["kv_w_b"], mix_g[1], wg["b_w_q_a"],
                sm["b_q_norm_g"], wg["b_w_q_b"])

    names = ("mlp_w1_0", "mlp_w2_0")
    (h1, z, gated), got = _a_mix_fwd(x, mix_g[0], *a_args(), comm=gather(names))
    wg.update(zip(names, got))
    names = ("kv_w_a", "kv_w_b", "b_w_q_a", "b_w_q_b", "b_w_o")
    (h2, a0), got = _mlp_fwd(h1, mlp_g[0], wg["mlp_w1_0"], wg["mlp_w2_0"], 0, comm=gather(names))
    wg.update(zip(names, got))
    if dist:
        wg["b_w_q_a"] = wg["b_w_q_a"].reshape(D_MODEL, Q_LORA)
        wg["kv_w_a"] = wg["kv_w_a"].reshape(D_MODEL, KV_LORA + QK_ROPE)
    ckv, kn, v, kpe, cqpre, q, cos, sin = _kvq_fwd(h2, pos, inv_freq, kvq_w())
    names = ("mlp_w1_1", "mlp_w2_1")
    (h3, att), got = _attn_fwd(h2, q, kn, kpe, v, wg["b_w_o"], comm=gather(names))
    wg.update(zip(names, got))
    (h4, a1), _ = _mlp_fwd(h3, mlp_g[1], wg["mlp_w1_1"], wg["mlp_w2_1"], 1)
    loss, dh4, d_final_g = _loss_head(h4, sm["final_norm_g"], target)

    g = {}
    (dh3, d_mlp_g1, hn, f, da, dh3_b), _ = _mlp_bwd(h3, a1, dh4, mlp_g[1], wg["mlp_w1_1"], wg["mlp_w2_1"], 1)
    g["mlp_w1_1"] = _wgrad("wgrad_w1_1", hn, da, _full(t, D_MODEL), _cols(t, FF_SLOT), D_MODEL, FF_SLOT)
    g["mlp_w2_1"] = _wgrad("wgrad_w2_1", f, dh4, _cols(t, FF_SLOT), _full(t, D_MODEL), FF_SLOT, D_MODEL)
    g["b_w_o"] = _wgrad("wgrad_w_o", att, dh3_b, _head(t, V_HEAD), _full(t, D_MODEL), V_HEAD, D_MODEL)
    names = ("mlp_w1_1", "mlp_w2_1", "b_w_o")
    (dq, dkn, dv, dkpe), got = _attn_bwd(dh3_b, q, kn, kpe, v, wg["b_w_o"], cos, sin,
                                         comm=send_sums("pair_reduce_1", [g[k] for k in names]))
    g.update(zip(names, got))
    (dh2, hq, hk, cq, dcqpre, c, dkv, dckv, d_mix_g1, d_src_g, d_q_g, d_kv_a_g) = _kvq_bwd(
        h2, dh3, ckv, cqpre, dq, dkn, dv, dkpe, cos, sin, kvq_w())
    g["b_w_q_a"] = _wgrad("wgrad_w_q_a", hq, dcqpre, _cols(t, D_MODEL // N_DEV), _full(t, Q_LORA),
                          D_MODEL // N_DEV, Q_LORA)
    g["b_w_q_b"] = _wgrad("wgrad_w_q_b", cq, dq, _full(t, Q_LORA), _head(t, QK_NOPE + QK_ROPE),
                          Q_LORA, QK_NOPE + QK_ROPE)
    g["kv_w_a"] = _wgrad("wgrad_kv_w_a", hk, dckv, _cols(t, D_MODEL // N_DEV), _full(t, KV_LORA + QK_ROPE),
                         D_MODEL // N_DEV, KV_LORA + QK_ROPE)
    g["kv_w_b"] = _wgrad("wgrad_kv_w_b", c, dkv, _full(t, KV_LORA), _head(t, QK_NOPE + V_HEAD),
                         KV_LORA, QK_NOPE + V_HEAD)
    names = ("b_w_q_a", "b_w_q_b", "kv_w_a", "kv_w_b")
    (dh1, d_mlp_g0, hn, f, da, dh1_b), got = _mlp_bwd(h1, a0, dh2, mlp_g[0], wg["mlp_w1_0"], wg["mlp_w2_0"], 0,
                                                      comm=send([g[k] for k in names]))
    g.update(zip(names, got))
    g["mlp_w1_0"] = _wgrad("wgrad_w1_0", hn, da, _full(t, D_MODEL), _cols(t, FF_SLOT), D_MODEL, FF_SLOT)
    g["mlp_w2_0"] = _wgrad("wgrad_w2_0", f, dh2, _cols(t, FF_SLOT), _full(t, D_MODEL), FF_SLOT, D_MODEL)
    g["a_w_out"] = _wgrad("wgrad_a_w_out", gated, dh1_b, _cols(t, GATE_DIM // N_DEV), _full(t, D_MODEL),
                          GATE_DIM // N_DEV, D_MODEL)
    names = ("mlp_w1_0", "mlp_w2_0", "a_w_out")
    (dx, hn, dz, d_mix_g0, d_ln_g, d_ln_b, d_ws, d_bs), got = _a_mix_bwd(
        x, z, dh1, mix_g[0], *a_args(), comm=send_sums("pair_reduce_0", [g[k] for k in names]))
    g.update(zip(names, got))
    small = {
        "norm_mix_g": jnp.concatenate([d_mix_g0, d_mix_g1], axis=0),
        "norm_mlp_g": jnp.concatenate([d_mlp_g0, d_mlp_g1], axis=0),
        "a_ln_v_g": d_ln_g.reshape(N_DEV, GATE_DIM // N_DEV),
        "a_ln_v_b": d_ln_b.reshape(N_DEV, GATE_DIM // N_DEV),
        "a_w_s": d_ws.astype(BF16) if dist else d_ws,
        "a_b_s": d_bs.reshape(A_GROUPS, GMLP_BLOCK),
        "b_q_norm_g": d_q_g,
        "kv_src_norm_g": d_src_g,
        "kv_a_norm_g": d_kv_a_g,
        "final_norm_g": d_final_g,
    }
    wgrad_in = ("wgrad_a_w_in", hn, dz, _full(t, D_MODEL), _cols(t, FF_SLOT), D_MODEL, FF_SLOT)
    if dist:
        parts = [small[k].reshape((1,) + small[k].shape) for k in SMALL] + [loss.reshape(1, 1, 1)]
        g["a_w_in"], got = _wgrad(*wgrad_in, comm=_exchange_comm(parts=parts))
        small, loss = dict(zip(SMALL, got)), got[-1]
    else:
        g["a_w_in"] = _wgrad(*wgrad_in)
    return loss, dx, g, small


def _adamw(w, g, m, v):
    m = ADAM_B1 * m + (1.0 - ADAM_B1) * g
    v = ADAM_B2 * v + (1.0 - ADAM_B2) * (g * g)
    m_hat = m / (1.0 - ADAM_B1 ** ADAM_STEP)
    v_hat = v / (1.0 - ADAM_B2 ** ADAM_STEP)
    return -ADAM_LR * (m_hat / (jnp.sqrt(v_hat) + ADAM_EPS) + ADAM_WD * w), m, v


def _sum_in_device_order(r_ref):
    g = r_ref[0].astype(F32)
    for j in range(1, r_ref.shape[0]):
        g = g + r_ref[j].astype(F32)
    return g


def _adamw_sharded(name, recvs, w, m, v, comm=None):
    layers, r, c = w.shape
    tr = math.gcd(r, 256)
    flat = [a for per_layer in recvs for a in per_layer]

    def body(*refs):
        r_refs, (w_ref, m_ref, v_ref) = refs[:len(flat)], refs[len(flat):len(flat) + 3]
        g_ref, d_ref, nm_ref, nv_ref = refs[-4:]
        layer = pl.program_id(0)
        g, pos = None, 0
        for li, per_layer in enumerate(recvs):
            total = None
            for ref in r_refs[pos:pos + len(per_layer)]:
                part = _sum_in_device_order(ref)
                total = part if total is None else total + part
            pos += len(per_layer)
            g = total if g is None else jnp.where(layer == li, total, g)
        g_ref[...] = g
        d_ref[...], nm_ref[...], nv_ref[...] = _adamw(w_ref[...], g, m_ref[...], v_ref[...])

    blk = pl.BlockSpec((None, tr, c), lambda l, i: (l, i, 0))
    return _call(name, body, (layers, r // tr),
                 [pl.BlockSpec((a.shape[0], tr, c), lambda l, i: (0, i, 0)) for a in flat] + [blk] * 3,
                 [blk] * 4, [_sds(w.shape, F32)] * 4, (*flat, w, m, v), comm=comm)


def _adamw_small(recvs, ws, ms, vs, own_row, losses):
    n = len(recvs)

    def body(*refs):
        r_refs, w_refs, m_refs, v_refs = (refs[i * n:(i + 1) * n] for i in range(4))
        outs, scr = refs[4 * n + 1:8 * n + 2], refs[8 * n + 2:]
        outs[-1][...] = _sum_in_device_order(refs[4 * n])
        me = _my_place()[3]
        for a in range(n):
            g = _sum_in_device_order(r_refs[a])
            if own_row[a]:
                scr[0][...] = g
                g = scr[0][pl.ds(me, 1), :]
            g_ref, d_ref, nm_ref, nv_ref = outs[4 * a:4 * a + 4]
            g_ref[...] = g
            d_ref[...], nm_ref[...], nv_ref[...] = _adamw(w_refs[a][...], g, m_refs[a][...], v_refs[a][...])

    out_shape = []
    for w in ws:
        out_shape += [_sds(w.shape, F32)] * 4
    return pl.pallas_call(
        body, name="adamw_small", in_specs=[VMEM] * (4 * n + 1), out_specs=[VMEM] * (4 * n + 1),
        out_shape=out_shape + [_sds((1, 1), F32)], scratch_shapes=[pltpu.VMEM((N_DEV, GATE_DIM // N_DEV), F32)],
    )(*recvs, *ws, *ms, *vs, losses)


BIG = ("a_w_in", "a_w_out", "b_w_q_a", "b_w_q_b", "b_w_o", "kv_w_a", "kv_w_b", "mlp_w1", "mlp_w2")
SMALL = ("norm_mix_g", "norm_mlp_g", "a_ln_v_g", "a_ln_v_b", "a_w_s", "a_b_s", "b_q_norm_g", "kv_src_norm_g",
         "kv_a_norm_g", "final_norm_g")
WEIGHTS = ("norm_mix_g", "norm_mlp_g", "a_w_in", "a_ln_v_g", "a_ln_v_b", "a_w_s", "a_b_s", "a_w_out", "b_w_q_a",
           "b_q_norm_g", "b_w_q_b", "b_w_o", "kv_src_norm_g", "kv_w_a", "kv_a_norm_g", "kv_w_b", "mlp_w1", "mlp_w2",
           "final_norm_g")


def _two_d(name, a):
    if name in ("a_w_s", "a_b_s"):
        return a.reshape(a.shape[1:])
    return a.reshape(1, -1) if a.ndim == 1 else a


def _three_d(a):
    return a if a.ndim == 3 else a.reshape((1,) + a.shape)


def kernel(x, positions, norm_mix_g, norm_mlp_g, a_w_in, a_ln_v_g, a_ln_v_b, a_w_s, a_b_s, a_w_out, b_w_q_a, b_q_norm_g, b_w_q_b, b_w_o, kv_src_norm_g, kv_w_a, kv_a_norm_g, kv_w_b, mlp_w1, mlp_w2, final_norm_g, loss_target, m_norm_mix_g, m_norm_mlp_g, m_a_w_in, m_a_ln_v_g, m_a_ln_v_b, m_a_w_s, m_a_b_s, m_a_w_out, m_b_w_q_a, m_b_q_norm_g, m_b_w_q_b, m_b_w_o, m_kv_src_norm_g, m_kv_w_a, m_kv_a_norm_g, m_kv_w_b, m_mlp_w1, m_mlp_w2, m_final_norm_g, v_norm_mix_g, v_norm_mlp_g, v_a_w_in, v_a_ln_v_g, v_a_ln_v_b, v_a_w_s, v_a_b_s, v_a_w_out, v_b_w_q_a, v_b_q_norm_g, v_b_w_q_b, v_b_w_o, v_kv_src_norm_g, v_kv_w_a, v_kv_a_norm_g, v_kv_w_b, v_mlp_w1, v_mlp_w2, v_final_norm_g):
    w = dict(norm_mix_g=norm_mix_g, norm_mlp_g=norm_mlp_g, a_w_in=a_w_in, a_ln_v_g=a_ln_v_g, a_ln_v_b=a_ln_v_b,
             a_w_s=a_w_s, a_b_s=a_b_s, a_w_out=a_w_out, b_w_q_a=b_w_q_a, b_q_norm_g=b_q_norm_g, b_w_q_b=b_w_q_b,
             b_w_o=b_w_o, kv_src_norm_g=kv_src_norm_g, kv_w_a=kv_w_a, kv_a_norm_g=kv_a_norm_g, kv_w_b=kv_w_b,
             mlp_w1=mlp_w1, mlp_w2=mlp_w2, final_norm_g=final_norm_g)
    m = dict(norm_mix_g=m_norm_mix_g, norm_mlp_g=m_norm_mlp_g, a_w_in=m_a_w_in, a_ln_v_g=m_a_ln_v_g,
             a_ln_v_b=m_a_ln_v_b, a_w_s=m_a_w_s, a_b_s=m_a_b_s, a_w_out=m_a_w_out, b_w_q_a=m_b_w_q_a,
             b_q_norm_g=m_b_q_norm_g, b_w_q_b=m_b_w_q_b, b_w_o=m_b_w_o, kv_src_norm_g=m_kv_src_norm_g,
             kv_w_a=m_kv_w_a, kv_a_norm_g=m_kv_a_norm_g, kv_w_b=m_kv_w_b, mlp_w1=m_mlp_w1, mlp_w2=m_mlp_w2,
             final_norm_g=m_final_norm_g)
    v = dict(norm_mix_g=v_norm_mix_g, norm_mlp_g=v_norm_mlp_g, a_w_in=v_a_w_in, a_ln_v_g=v_a_ln_v_g,
             a_ln_v_b=v_a_ln_v_b, a_w_s=v_a_w_s, a_b_s=v_a_b_s, a_w_out=v_a_w_out, b_w_q_a=v_b_w_q_a,
             b_q_norm_g=v_b_q_norm_g, b_w_q_b=v_b_w_q_b, b_w_o=v_b_w_o, kv_src_norm_g=v_kv_src_norm_g,
             kv_w_a=v_kv_w_a, kv_a_norm_g=v_kv_a_norm_g, kv_w_b=v_kv_w_b, mlp_w1=v_mlp_w1, mlp_w2=v_mlp_w2,
             final_norm_g=v_final_norm_g)
    t = x.shape[1]

    first = ("a_w_in", "a_w_out", "a_ln_v_g", "a_ln_v_b")
    later = ("mlp_w1_0", "mlp_w2_0", "mlp_w1_1", "mlp_w2_1", "kv_w_a", "kv_w_b", "b_w_q_a", "b_w_q_b", "b_w_o")
    blocks = {k: _three_d(w[k]) for k in BIG if not k.startswith("mlp")}
    for k in ("mlp_w1", "mlp_w2"):
        blocks[k + "_0"], blocks[k + "_1"] = w[k][0:1], w[k][1:2]
    got, casts = _gather_first([blocks[k] if k in blocks else w[k] for k in first], [blocks[k] for k in later])
    wg = dict(zip(first, got))
    wg["a_w_out"] = wg["a_w_out"].reshape(GATE_DIM, D_MODEL)
    wg["a_ln_v_g"] = wg["a_ln_v_g"].reshape(1, GATE_DIM)
    wg["a_ln_v_b"] = wg["a_ln_v_b"].reshape(1, GATE_DIM)
    shards = dict(zip(later, casts))

    sm = {k: _two_d(k, w[k]) for k in SMALL if k not in ("a_ln_v_g", "a_ln_v_b")}
    sm["a_b_st"] = sm["a_b_s"].T
    inv_freq = (ROPE_THETA ** (-jnp.arange(0, QK_ROPE, 2, dtype=F32) / QK_ROPE)).reshape(1, QK_ROPE // 2)

    losses, dx, g, small = _local_step(x[0], positions.reshape(t, 1), loss_target[0], inv_freq, wg, sm, shards)

    g["a_w_in"], = _comm_only("exchange_last", _chip_exchange_comm(_pair_reduce("pair_reduce_a", [g["a_w_in"]])))

    out = {}
    for k in BIG:
        recvs = [[g[k + "_0"]], [g[k + "_1"]]] if k.startswith("mlp") else [[g[k]]]
        res, _ = _adamw_sharded("adamw_" + k, recvs, _three_d(w[k]), _three_d(m[k]), _three_d(v[k]))
        out[k] = [o.reshape(w[k].shape) for o in res]
    own_row = [k in ("a_ln_v_g", "a_ln_v_b") for k in SMALL]
    res = _adamw_small([small[k] for k in SMALL], [_two_d(k, w[k]) for k in SMALL], [_two_d(k, m[k]) for k in SMALL],
                       [_two_d(k, v[k]) for k in SMALL], own_row, losses)
    for i, k in enumerate(SMALL):
        out[k] = [o.reshape(w[k].shape) for o in res[4 * i:4 * i + 4]]

    return (res[-1].reshape(()), dx.reshape(x.shape), *[out[k][0] for k in WEIGHTS], *[out[k][1] for k in WEIGHTS],
            *[out[k][2] for k in WEIGHTS], *[out[k][3] for k in WEIGHTS])
```

```python
import math

import jax
import jax.numpy as jnp
from jax import lax
from jax.experimental import pallas as pl
from jax.experimental.pallas import tpu as pltpu

F32, BF16 = jnp.float32, jnp.bfloat16
MESH = pl.DeviceIdType.MESH
ANY = pl.BlockSpec(memory_space=pl.ANY)
VMEM = pl.BlockSpec(memory_space=pltpu.VMEM)

N_DEV = 8
D_MODEL = 1024
CHUNK = 64
GMLP_BLOCK = 128
GATE_DIM = 2048
A_GROUPS = 8
A_GROUP_DIM = GATE_DIM // A_GROUPS
B_HEADS = 8
QK_NOPE, QK_ROPE, V_HEAD = 128, 64, 128
Q_LORA, KV_LORA = 384, 256
ROPE_THETA = 10000.0
D_FF = 4096
FF_SLOT = D_FF // N_DEV
EPS = 1e-6
ATT_SCALE = (QK_NOPE + QK_ROPE) ** -0.5

ADAM_LR, ADAM_B1, ADAM_B2, ADAM_EPS, ADAM_WD, ADAM_STEP = 0.001, 0.9, 0.999, 1e-08, 0.01, 10

TM = 256
TM_GATE = 128
VMEM_LIMIT = 56 * 1024 * 1024
INV_SQRT2 = 1.0 / math.sqrt(2.0)
INV_SQRT_2PI = 1.0 / math.sqrt(2.0 * math.pi)


def _dot(a, b):
    return jnp.dot(a, b, preferred_element_type=F32)


def _dot_nt(a, b):
    return lax.dot_general(a, b, (((1,), (1,)), ((), ())), preferred_element_type=F32)


def _dot_tn(a, b):
    return lax.dot_general(a, b, (((0,), (0,)), ((), ())), preferred_element_type=F32)


def _rms_fwd(x, g):
    rstd = lax.rsqrt(jnp.mean(x * x, axis=-1, keepdims=True) + EPS)
    xhat = x * rstd
    return xhat * g, xhat, rstd


def _rms_bwd(dy, xhat, rstd, g):
    dxhat = dy * g
    dx = rstd * (dxhat - xhat * jnp.mean(dxhat * xhat, axis=-1, keepdims=True))
    return dx, jnp.sum(dy * xhat, axis=0, keepdims=True)


def _ln_fwd(v, g, b):
    mu = jnp.mean(v, axis=-1, keepdims=True)
    vc = v - mu
    rstd = lax.rsqrt(jnp.mean(vc * vc, axis=-1, keepdims=True) + EPS)
    vhat = vc * rstd
    return vhat * g + b, vhat, rstd


def _gelu(x):
    return 0.5 * x * (1.0 + lax.erf(x * INV_SQRT2))


def _gelu_and_grad(x):
    cdf = 0.5 * (1.0 + lax.erf(x * INV_SQRT2))
    return x * cdf, cdf + x * jnp.exp(-0.5 * x * x) * INV_SQRT_2PI


def _rope(x, cos, sin):
    x1, x2 = x[:, :QK_ROPE // 2], x[:, QK_ROPE // 2:]
    return jnp.concatenate([x1 * cos - x2 * sin, x2 * cos + x1 * sin], axis=-1)


def _gate_mask():
    row = lax.broadcasted_iota(jnp.int32, (GMLP_BLOCK, GMLP_BLOCK), 0)
    col = lax.broadcasted_iota(jnp.int32, (GMLP_BLOCK, GMLP_BLOCK), 1)
    return (col < CHUNK) | (row >= CHUNK)


def _att_mask(q0, tq, t):
    q = q0 + lax.broadcasted_iota(jnp.int32, (tq, t), 0)
    k = lax.broadcasted_iota(jnp.int32, (tq, t), 1)
    return jnp.right_shift(k, 6) <= jnp.right_shift(q, 6)


def _res(shape, imap=None):
    zeros = (0,) * len(shape)
    return pl.BlockSpec(shape, imap or (lambda i: zeros), pipeline_mode=pl.Buffered(1))


def _const(shape):
    zeros = (0,) * len(shape)
    return pl.BlockSpec(shape, lambda i: zeros)


def _row(d, tm=TM):
    return pl.BlockSpec((tm, d), lambda i: (i, 0))


def _heads(d):
    return pl.BlockSpec((B_HEADS, TM, d), lambda i: (0, i, 0))


def _sds(shape, dt):
    return jax.ShapeDtypeStruct(shape, dt)


def _acc(ref, val):
    @pl.when(pl.program_id(0) == 0)
    def _():
        ref[...] = jnp.zeros_like(ref)
    ref[...] += val


def _my_place():
    x, y, c = lax.axis_index("x"), lax.axis_index("y"), lax.axis_index("c")
    return x, y, c, 4 * x + 2 * y + c


def _peer(x, y, c, k):
    px = 1 - x if k & 4 else x
    py = 1 - y if k & 2 else y
    pc = 1 - c if k & 1 else c
    return (px, py, pc), 4 * px + 2 * py + pc


CHIPS = (2, 4, 6)


def _splits(ref):
    return len(ref.shape) >= 3 and ref.shape[1] % 32 == 0


def _piece(ref, block, half=None):
    if half is None or not _splits(ref):
        return ref.at[pl.ds(block, 1)]
    rows = ref.shape[1] // 2
    return ref.at[pl.ds(block, 1), pl.ds(half * rows, rows)]


def _gather_copy(sems, a, k, piece, to, src=None):
    return pltpu.make_async_remote_copy(
        src_ref=piece if src is None else src, dst_ref=piece, send_sem=sems[0].at[a, k], recv_sem=sems[1].at[a, k],
        device_id=to, device_id_type=MESH)


def _gather_start(srcs, outs, sems, only=None):
    x, y, c, me = _my_place()
    for a in range(len(srcs)) if only is None else (only,):
        mine = _piece(outs[a], me)
        pltpu.make_async_copy(srcs[a], mine, sems[2].at[a]).start()
        for k, rel in enumerate((1, 4, 2)):
            _gather_copy(sems, a, k, mine, _peer(x, y, c, rel)[0], src=srcs[a]).start()


def _gather_finish(srcs, outs, sems):
    x, y, c, me = _my_place()
    sib = _peer(x, y, c, 1)[0]
    (xn, xn_i), (yn, yn_i), (_, dg_i) = _peer(x, y, c, 4), _peer(x, y, c, 2), _peer(x, y, c, 6)
    n = len(srcs)
    for a in range(n):
        out = outs[a]
        _gather_copy(sems, a, 1, _piece(out, xn_i), xn).wait_recv()
        _gather_copy(sems, a, 3, _piece(out, xn_i, 0), yn).start()
        _gather_copy(sems, a, 5, _piece(out, xn_i), sib).start()
        _gather_copy(sems, a, 2, _piece(out, yn_i), yn).wait_recv()
        if _splits(out):
            _gather_copy(sems, a, 4, _piece(out, yn_i, 1), xn).start()
        _gather_copy(sems, a, 6, _piece(out, yn_i), sib).start()
    for a in range(n):
        out = outs[a]
        _gather_copy(sems, a, 3, _piece(out, dg_i, 0), yn).wait_recv()
        _gather_copy(sems, a, 7, _piece(out, dg_i, 0), sib).start()
        if _splits(out):
            _gather_copy(sems, a, 4, _piece(out, dg_i, 1), xn).wait_recv()
            _gather_copy(sems, a, 8, _piece(out, dg_i, 1), sib).start()
    for a in range(n):
        out = outs[a]
        whole, half = _piece(out, me), _piece(out, me, 0)
        for k in (0, 5, 6):
            _gather_copy(sems, a, k, whole, sib).wait_recv()
        for k in (7, 8) if _splits(out) else (7,):
            _gather_copy(sems, a, k, half, sib).wait_recv()
        for k in (0, 1, 2):
            _gather_copy(sems, a, k, whole, sib, src=srcs[a]).wait_send()
        for k in (5, 6):
            _gather_copy(sems, a, k, whole, sib).wait_send()
        for k in (3, 4, 7, 8) if _splits(out) else (3, 7):
            _gather_copy(sems, a, k, half, sib).wait_send()
        pltpu.make_async_copy(srcs[a], whole, sems[2].at[a]).wait()


def _relay_sems(n):
    return [pltpu.SemaphoreType.DMA((n, 9)), pltpu.SemaphoreType.DMA((n, 9)), pltpu.SemaphoreType.DMA((n,))]


def _gather_sems(n):
    return [pltpu.SemaphoreType.DMA((n, 7)), pltpu.SemaphoreType.DMA((n, 7)), pltpu.SemaphoreType.DMA((n,))]


class _Comm:
    def __init__(self, args, out_shape, scratch, start, finish):
        self.args, self.out_shape, self.scratch, self.start, self.finish = args, out_shape, scratch, start, finish


def _gather_comm(shards):
    return _Comm(list(shards), [_sds((N_DEV,) + s.shape[1:], s.dtype) for s in shards], _relay_sems(len(shards)),
                 _gather_start, _gather_finish)


def _direct_copies(ins, outs, sems, wait, from_block):
    send_sems, recv_sems, local_sems = sems
    x, y, c, me = _my_place()
    for a in range(len(ins)):
        src = ins[a].at[pl.ds(me, 1)] if from_block[a] else ins[a]
        local = pltpu.make_async_copy(src, outs[a].at[pl.ds(me, 1)], local_sems.at[a])
        local.wait() if wait else local.start()
        for k in range(1, N_DEV):
            to, to_i = _peer(x, y, c, k)
            cp = pltpu.make_async_remote_copy(
                src_ref=ins[a].at[pl.ds(to_i, 1)] if from_block[a] else ins[a], dst_ref=outs[a].at[pl.ds(me, 1)],
                send_sem=send_sems.at[a, k - 1], recv_sem=recv_sems.at[a, k - 1], device_id=to, device_id_type=MESH)
            cp.wait() if wait else cp.start()


def _exchange_comm(grads=(), parts=()):
    ins = list(grads) + list(parts)
    from_block = [True] * len(grads) + [False] * len(parts)
    out_shape = [_sds(g.shape, g.dtype) for g in grads] + [_sds((N_DEV,) + p.shape[1:], p.dtype) for p in parts]

    def start(ins_, outs_, sems_):
        _direct_copies(ins_, outs_, sems_, False, from_block)

    def finish(ins_, outs_, sems_):
        _direct_copies(ins_, outs_, sems_, True, from_block)

    return _Comm(ins, out_shape, _gather_sems(len(ins)), start, finish)


def _chip_copies(ins, outs, sems, wait, rels, own):
    send_sems, recv_sems, local_sems = sems
    x, y, c, _ = _my_place()
    for a in range(len(ins)):
        if own:
            local = pltpu.make_async_copy(ins[a].at[pl.ds(2 * x + y, 1)], outs[a].at[pl.ds(len(rels), 1)],
                                          local_sems.at[a])
            local.wait() if wait else local.start()
        for i, j in enumerate(rels):
            to = _peer(x, y, c, CHIPS[j])[0]
            cp = pltpu.make_async_remote_copy(
                src_ref=ins[a].at[pl.ds(2 * to[0] + to[1], 1)], dst_ref=outs[a].at[pl.ds(i, 1)],
                send_sem=send_sems.at[a, i], recv_sem=recv_sems.at[a, i], device_id=to, device_id_type=MESH)
            cp.wait() if wait else cp.start()


def _chip_exchange_comm(sums, rels=(0, 1, 2), own=True):
    def start(ins_, outs_, sems_):
        _chip_copies(ins_, outs_, sems_, False, rels, own)

    def finish(ins_, outs_, sems_):
        _chip_copies(ins_, outs_, sems_, True, rels, own)

    n = len(sums)
    sems = [pltpu.SemaphoreType.DMA((n, len(rels))), pltpu.SemaphoreType.DMA((n, len(rels))),
            pltpu.SemaphoreType.DMA((n,))]
    return _Comm(list(sums), [_sds((len(rels) + own,) + s.shape[1:], s.dtype) for s in sums], sems, start, finish)


def _pair_reduce(name, grads):
    n = len(grads)
    n_chips = N_DEV // 2

    def body(*refs):
        g_refs, gh_refs, p_refs, land = refs[:n], refs[n:2 * n], refs[2 * n:3 * n], refs[3 * n:4 * n]
        send_sems, recv_sems = refs[4 * n:]
        x, y, c, _ = _my_place()
        sib = _peer(x, y, c, 1)[0]
        q = pl.program_id(0)

        def to_sibling(a, j):
            return pltpu.make_async_remote_copy(
                src_ref=gh_refs[a].at[j, pl.ds(1 - c, 1)], dst_ref=land[a].at[pl.ds(j, 1)],
                send_sem=send_sems.at[a, j], recv_sem=recv_sems.at[a, j], device_id=sib, device_id_type=MESH)

        @pl.when(q == 0)
        def _():
            for j in range(n_chips):
                for a in range(n):
                    to_sibling(a, j).start()

        for a in range(n):
            to_sibling(a, q).wait_recv()
            p_refs[a][...] = (g_refs[a][0, pl.ds(c, 1)].astype(F32) + land[a][pl.ds(q, 1)].astype(F32)).astype(BF16)

        @pl.when(q == n_chips - 1)
        def _():
            for a in range(n):
                for j in range(n_chips):
                    to_sibling(a, j).wait_send()

    views = [g.reshape((n_chips, 2) + g.shape[1:]) for g in grads]
    res = pl.pallas_call(
        body, name=name, grid=(n_chips,),
        in_specs=[pl.BlockSpec((1, 2) + g.shape[1:], lambda q: (q, 0, 0, 0)) for g in grads] + [ANY] * n,
        out_specs=[pl.BlockSpec((1,) + g.shape[1:], lambda q: (q, 0, 0)) for g in grads],
        out_shape=[_sds((n_chips,) + g.shape[1:], BF16) for g in grads],
        scratch_shapes=[pltpu.VMEM((n_chips,) + g.shape[1:], BF16) for g in grads]
        + [pltpu.SemaphoreType.DMA((n, n_chips)), pltpu.SemaphoreType.DMA((n, n_chips))],
        compiler_params=pltpu.CompilerParams(dimension_semantics=("arbitrary",), vmem_limit_bytes=VMEM_LIMIT),
    )(*views, *views)
    return list(res)


def _call(name, body, grid, in_specs, out_specs, out_shape, args, scratch=(), comm=None):
    params = pltpu.CompilerParams(dimension_semantics=("arbitrary",) * len(grid), vmem_limit_bytes=VMEM_LIMIT)
    if comm is None:
        outs = pl.pallas_call(body, name=name, grid=grid, in_specs=list(in_specs), out_specs=list(out_specs),
                              out_shape=list(out_shape), scratch_shapes=list(scratch), compiler_params=params)(*args)
        return list(outs), []
    ni, nci, no, nco, ns = len(in_specs), len(comm.args), len(out_specs), len(comm.out_shape), len(scratch)

    def carrying(*refs):
        ins, refs = refs[:ni], refs[ni:]
        cin, refs = refs[:nci], refs[nci:]
        outs, refs = refs[:no], refs[no:]
        cout, refs = refs[:nco], refs[nco:]
        scr, csems = refs[:ns], refs[ns:]
        ids = [pl.program_id(ax) for ax in range(len(grid))]
        first, last = ids[0] == 0, ids[0] == grid[0] - 1
        for ax in range(1, len(grid)):
            first, last = first & (ids[ax] == 0), last & (ids[ax] == grid[ax] - 1)

        @pl.when(first)
        def _():
            comm.start(cin, cout, csems)

        body(*ins, *outs, *scr)

        @pl.when(last)
        def _():
            comm.finish(cin, cout, csems)

    outs = pl.pallas_call(
        carrying, name=name, grid=grid, in_specs=list(in_specs) + [ANY] * nci, out_specs=list(out_specs) + [ANY] * nco,
        out_shape=list(out_shape) + list(comm.out_shape), scratch_shapes=list(scratch) + list(comm.scratch),
        compiler_params=params)(*args, *comm.args)
    return list(outs[:no]), list(outs[no:])


def _comm_only(name, comm):
    def body(*refs):
        nci, nco = len(comm.args), len(comm.out_shape)
        cin, cout, csems = refs[:nci], refs[nci:nci + nco], refs[nci + nco:]
        comm.start(cin, cout, csems)
        comm.finish(cin, cout, csems)

    return pl.pallas_call(body, name=name, in_specs=[ANY] * len(comm.args), out_specs=[ANY] * len(comm.out_shape),
                          out_shape=list(comm.out_shape), scratch_shapes=list(comm.scratch))(*comm.args)


def _gather_first(first, later):
    nf, nl = len(first), len(later)
    dts = [BF16] * (nf - 2) + [F32, F32]

    def body(*refs):
        ins, refs = refs[:nf + nl], refs[nf + nl:]
        outs, refs = refs[:nf], refs[nf:]
        casts, refs = refs[:nl], refs[nl:]
        stage, sems = refs[:nf], refs[nf:]
        for a in range(nf):
            stage[a][...] = ins[a][...].astype(dts[a])
            _gather_start(stage, outs, sems, only=a)
        for a in range(nl):
            casts[a][...] = ins[nf + a][...].astype(BF16)
        _gather_finish(stage, outs, sems)

    res = pl.pallas_call(
        body, name="gather_first",
        in_specs=[VMEM] * (nf + nl), out_specs=[ANY] * nf + [VMEM] * nl,
        out_shape=[_sds((N_DEV,) + s.shape[1:], dt) for s, dt in zip(first, dts)]
        + [_sds(s.shape, BF16) for s in later],
        scratch_shapes=[pltpu.VMEM(s.shape, dt) for s, dt in zip(first, dts)] + _relay_sems(nf),
        compiler_params=pltpu.CompilerParams(vmem_limit_bytes=VMEM_LIMIT),
    )(*first, *later)
    return list(res[:nf]), list(res[nf:])


def _a_mix_fwd(x, g, w_in, ln_g, ln_b, w_s, b_st, w_out, comm=None):
    t = x.shape[0]
    nblk = TM // GMLP_BLOCK

    def body(x_ref, g_ref, win_ref, lng_ref, lnb_ref, ws_ref, bst_ref, wout_ref, h_ref, z_ref, gated_scr):
        xv = x_ref[...]
        hb = _rms_fwd(xv, g_ref[...])[0].astype(BF16)
        for d in range(N_DEV):
            z_ref[:, d * FF_SLOT:(d + 1) * FF_SLOT] = _dot(hb, win_ref[d])
        u = _gelu(z_ref[:, :GATE_DIM])
        vb = _ln_fwd(_gelu(z_ref[:, GATE_DIM:]), lng_ref[...], lnb_ref[...])[0].astype(BF16)
        mask = _gate_mask()
        for gi in range(A_GROUPS):
            wm = jnp.where(mask, ws_ref[gi], 0.0).astype(BF16)
            bias = bst_ref[:, gi:gi + 1]
            cs = slice(gi * A_GROUP_DIM, (gi + 1) * A_GROUP_DIM)
            for n in range(nblk):
                rs = slice(n * GMLP_BLOCK, (n + 1) * GMLP_BLOCK)
                sv = _dot(wm, vb[rs, cs]) + bias
                gated_scr[rs, cs] = (u[rs, cs] * sv).astype(BF16)
        h_ref[...] = xv + _dot(gated_scr[...], wout_ref[...])

    return _call(
        "a_mix_fwd", body, (t // TM,),
        [_row(D_MODEL), _res((1, D_MODEL)), _res((N_DEV, D_MODEL, FF_SLOT)), _res((1, GATE_DIM)),
         _res((1, GATE_DIM)), _res((A_GROUPS, GMLP_BLOCK, GMLP_BLOCK)), _res((GMLP_BLOCK, A_GROUPS)),
         _res((GATE_DIM, D_MODEL))],
        [_row(D_MODEL), _row(2 * GATE_DIM), _row(GATE_DIM)],
        [_sds((t, D_MODEL), F32), _sds((t, 2 * GATE_DIM), F32), _sds((t, GATE_DIM), BF16)],
        (x, g, w_in, ln_g, ln_b, w_s, b_st, w_out), comm=comm)


MLP_W_SPECS = (_res((N_DEV, D_MODEL, FF_SLOT)), _res((N_DEV, FF_SLOT, D_MODEL)))


def _mlp_fwd(h, g, w1, w2, layer, comm=None):
    t = h.shape[0]

    def body(h_ref, g_ref, w1_ref, w2_ref, o_ref, a_ref):
        hv = h_ref[...]
        hb = _rms_fwd(hv, g_ref[...])[0].astype(BF16)
        o_ref[...] = hv
        for d in range(N_DEV):
            a = _dot(hb, w1_ref[d])
            a_ref[:, d * FF_SLOT:(d + 1) * FF_SLOT] = a
            r = jnp.maximum(a, 0.0)
            o_ref[...] += _dot((r * r).astype(BF16), w2_ref[d])

    return _call(
        f"mlp_fwd_{layer}", body, (t // TM,), [_row(D_MODEL), _res((1, D_MODEL)), *MLP_W_SPECS],
        [_row(D_MODEL), _row(D_FF)], [_sds((t, D_MODEL), F32), _sds((t, D_FF), F32)], (h, g, w1, w2), comm=comm)


KVQ_W_SPECS = (_res((1, D_MODEL)), _res((D_MODEL, KV_LORA + QK_ROPE)), _res((1, KV_LORA)),
               _res((B_HEADS, KV_LORA, QK_NOPE + V_HEAD)), _res((1, D_MODEL)), _res((D_MODEL, Q_LORA)),
               _res((1, Q_LORA)), _res((B_HEADS, Q_LORA, QK_NOPE + QK_ROPE)))


def _kvq_fwd(h, pos, inv_freq, kvq_w):
    t = h.shape[0]
    half = QK_ROPE // 2

    def body(h_ref, pos_ref, invf_ref, srcg_ref, wkva_ref, kvag_ref, wkvb_ref, mixg_ref, wqa_ref, qg_ref, wqb_ref,
             ckv_ref, kn_ref, v_ref, kpe_ref, cqpre_ref, q_ref, cos_ref, sin_ref):
        hv = h_ref[...]
        xhat = hv * lax.rsqrt(jnp.mean(hv * hv, axis=-1, keepdims=True) + EPS)
        ang = pos_ref[...].astype(F32) * invf_ref[...]
        cos, sin = jnp.cos(ang), jnp.sin(ang)
        cos_ref[...] = cos
        sin_ref[...] = sin
        ckv = _dot((xhat * srcg_ref[...]).astype(BF16), wkva_ref[...])
        ckv_ref[...] = ckv
        cb = _rms_fwd(ckv[:, :KV_LORA], kvag_ref[...])[0].astype(BF16)
        kpe_ref[...] = _rope(ckv[:, KV_LORA:], cos, sin).astype(BF16)
        for hd in range(B_HEADS):
            kv = _dot(cb, wkvb_ref[hd])
            kn_ref[hd] = kv[:, :QK_NOPE].astype(BF16)
            v_ref[hd] = kv[:, QK_NOPE:].astype(BF16)
        cqpre = _dot((xhat * mixg_ref[...]).astype(BF16), wqa_ref[...])
        cqpre_ref[...] = cqpre
        cqb = _rms_fwd(cqpre, qg_ref[...])[0].astype(BF16)
        for hd in range(B_HEADS):
            q = _dot(cqb, wqb_ref[hd])
            q_ref[hd, :, 0:QK_NOPE] = q[:, :QK_NOPE].astype(BF16)
            q_ref[hd, :, QK_NOPE:] = _rope(q[:, QK_NOPE:], cos, sin).astype(BF16)

    return _call(
        "kvq_fwd", body, (t // TM,), [_row(D_MODEL), _row(1), _res((1, half)), *KVQ_W_SPECS],
        [_row(KV_LORA + QK_ROPE), _heads(QK_NOPE), _heads(V_HEAD), _row(QK_ROPE), _row(Q_LORA),
         _heads(QK_NOPE + QK_ROPE), _row(half), _row(half)],
        [_sds((t, KV_LORA + QK_ROPE), F32), _sds((B_HEADS, t, QK_NOPE), BF16), _sds((B_HEADS, t, V_HEAD), BF16),
         _sds((t, QK_ROPE), BF16), _sds((t, Q_LORA), F32), _sds((B_HEADS, t, QK_NOPE + QK_ROPE), BF16),
         _sds((t, half), F32), _sds((t, half), F32)],
        (h, pos, inv_freq, *kvq_w))[0]


def _softmax_rows(qn, qp, kn_ref, kpe_ref, k):
    past, upto = k * TM, (k + 1) * TM
    s = (_dot_nt(qn, kn_ref[0:upto, :]) + _dot_nt(qp, kpe_ref[0:upto, :])) * ATT_SCALE
    own = jnp.where(_att_mask(0, TM, TM), s[:, past:], jnp.finfo(F32).min)
    s = own if k == 0 else jnp.concatenate([s[:, :past], own], axis=1)
    e = jnp.exp(s - jnp.max(s, axis=-1, keepdims=True))
    return e * (1.0 / jnp.sum(e, axis=-1, keepdims=True))


def _for_my_tile(i, nq, fn):
    for k in range(nq):
        @pl.when(i == k)
        def _(k=k):
            fn(k)


def _attn_fwd(h, q, kn, kpe, v, w_o, comm=None):
    t = h.shape[0]
    nq = t // TM

    def body(h_ref, q_ref, kn_ref, kpe_ref, v_ref, wo_ref, o_ref, att_ref):
        i, hd = pl.program_id(0), pl.program_id(1)

        @pl.when(hd == 0)
        def _():
            o_ref[...] = h_ref[...]

        def tile(k):
            p = _softmax_rows(q_ref[:, 0:QK_NOPE], q_ref[:, QK_NOPE:], kn_ref.at[hd], kpe_ref, k)
            ob = _dot(p.astype(BF16), v_ref[hd, 0:(k + 1) * TM, :]).astype(BF16)
            att_ref[...] = ob
            o_ref[...] += _dot(ob, wo_ref[hd])

        _for_my_tile(i, nq, tile)

    def per_head(d):
        return pl.BlockSpec((None, TM, d), lambda i, hd: (hd, i, 0))

    def resident(shape):
        zeros = (0,) * len(shape)
        return pl.BlockSpec(shape, lambda i, hd: zeros, pipeline_mode=pl.Buffered(1))

    tile_spec = pl.BlockSpec((TM, D_MODEL), lambda i, hd: (i, 0))
    return _call(
        "attn_fwd", body, (nq, B_HEADS),
        [tile_spec, per_head(QK_NOPE + QK_ROPE), resident((B_HEADS, t, QK_NOPE)), resident((t, QK_ROPE)),
         resident((B_HEADS, t, V_HEAD)), resident((B_HEADS, V_HEAD, D_MODEL))],
        [tile_spec, per_head(V_HEAD)], [_sds((t, D_MODEL), F32), _sds((B_HEADS, t, V_HEAD), BF16)],
        (h, q, kn, kpe, v, w_o), comm=comm)


def _loss_head(h, g, target):
    t = h.shape[0]

    def body(h_ref, g_ref, t_ref, loss_ref, dh_ref, dg_ref):
        y, xhat, rstd = _rms_fwd(h_ref[...], g_ref[...])
        err = y - t_ref[...]
        part = 0.5 * jnp.sum(jnp.mean(err * err, axis=-1, keepdims=True), axis=0, keepdims=True)
        dx, dg = _rms_bwd(err * (1.0 / D_MODEL), xhat, rstd, g_ref[...])
        dh_ref[...] = dx
        _acc(dg_ref, dg)
        _acc(loss_ref, part)

    return _call(
        "loss_head", body, (t // TM,), [_row(D_MODEL), _res((1, D_MODEL)), _row(D_MODEL)],
        [_const((1, 1)), _row(D_MODEL), _const((1, D_MODEL))],
        [_sds((1, 1), F32), _sds((t, D_MODEL), F32), _sds((1, D_MODEL), F32)], (h, g, target))[0]


def _mlp_bwd(h, a, dho, g, w1, w2, layer, comm=None):
    t = h.shape[0]

    def body(h_ref, a_ref, dho_ref, g_ref, w1_ref, w2_ref, dhi_ref, dg_ref, hn_ref, f_ref, da_ref, dhib_ref):
        gv = g_ref[...]
        y, xhat, rstd = _rms_fwd(h_ref[...], gv)
        hn_ref[...] = y.astype(BF16)
        dho_v = dho_ref[...]
        dhob = dho_v.astype(BF16)
        dhn = jnp.zeros((TM, D_MODEL), F32)
        for d in range(N_DEV):
            cs = slice(d * FF_SLOT, (d + 1) * FF_SLOT)
            r = jnp.maximum(a_ref[:, cs], 0.0)
            f_ref[:, cs] = (r * r).astype(BF16)
            da = (_dot_nt(dhob, w2_ref[d]) * (2.0 * r)).astype(BF16)
            da_ref[:, cs] = da
            dhn = dhn + _dot_nt(da, w1_ref[d])
        dx, dg = _rms_bwd(dhn, xhat, rstd, gv)
        dhi = dho_v + dx
        dhi_ref[...] = dhi
        dhib_ref[...] = dhi.astype(BF16)
        _acc(dg_ref, dg)

    return _call(
        f"mlp_bwd_{layer}", body, (t // TM,),
        [_row(D_MODEL), _row(D_FF), _row(D_MODEL), _res((1, D_MODEL)), *MLP_W_SPECS],
        [_row(D_MODEL), _const((1, D_MODEL)), _row(D_MODEL), _row(D_FF), _row(D_FF), _row(D_MODEL)],
        [_sds((t, D_MODEL), F32), _sds((1, D_MODEL), F32), _sds((t, D_MODEL), BF16), _sds((t, D_FF), BF16),
         _sds((t, D_FF), BF16), _sds((t, D_MODEL), BF16)],
        (h, a, dho, g, w1, w2), comm=comm)


def _attn_bwd(dh, q, kn, kpe, v, w_o, cos, sin, comm=None):
    t = dh.shape[0]
    half = QK_ROPE // 2

    def body(dh_ref, q_ref, kn_ref, kpe_ref, v_ref, wo_ref, cos_ref, sin_ref, dq_ref, dkn_ref, dv_ref, dkpe_ref):
        hd, i = pl.program_id(0), pl.program_id(1)

        @pl.when(i == 0)
        def _():
            dkn_ref[...] = jnp.zeros_like(dkn_ref)
            dv_ref[...] = jnp.zeros_like(dv_ref)

        @pl.when((i == 0) & (hd == 0))
        def _():
            dkpe_ref[...] = jnp.zeros_like(dkpe_ref)

        def tile(k):
            keys = slice(0, (k + 1) * TM)
            qn, qp = q_ref[:, 0:QK_NOPE], q_ref[:, QK_NOPE:]
            do = _dot_nt(dh_ref[k * TM:(k + 1) * TM, :], wo_ref[...]).astype(BF16)
            p = _softmax_rows(qn, qp, kn_ref, kpe_ref, k)
            dp = _dot_nt(do, v_ref[keys, :])
            ds = (p * (dp - jnp.sum(p * dp, axis=-1, keepdims=True)) * ATT_SCALE).astype(BF16)
            dq_ref[:, 0:QK_NOPE] = _dot(ds, kn_ref[keys, :]).astype(BF16)
            dq_ref[:, QK_NOPE:] = _rope(_dot(ds, kpe_ref[keys, :]), cos_ref[...], -sin_ref[...]).astype(BF16)
            dkn_ref[keys, :] += _dot_tn(ds, qn)
            dv_ref[keys, :] += _dot_tn(p.astype(BF16), do)
            dkpe_ref[keys, :] += _dot_tn(ds, qp)

        _for_my_tile(i, t // TM, tile)

    def per_head(rows, d, tiled):
        return pl.BlockSpec((None, rows, d), (lambda hd, i: (hd, i, 0)) if tiled else (lambda hd, i: (hd, 0, 0)))

    def tile(d):
        return pl.BlockSpec((TM, d), lambda hd, i: (i, 0))

    return _call(
        "attn_bwd", body, (B_HEADS, t // TM),
        [pl.BlockSpec((t, D_MODEL), lambda hd, i: (0, 0), pipeline_mode=pl.Buffered(1)),
         per_head(TM, QK_NOPE + QK_ROPE, True), per_head(t, QK_NOPE, False),
         pl.BlockSpec((t, QK_ROPE), lambda hd, i: (0, 0)), per_head(t, V_HEAD, False),
         per_head(V_HEAD, D_MODEL, False), tile(half), tile(half)],
        [per_head(TM, QK_NOPE + QK_ROPE, True), per_head(t, QK_NOPE, False), per_head(t, V_HEAD, False),
         pl.BlockSpec((t, QK_ROPE), lambda hd, i: (0, 0))],
        [_sds((B_HEADS, t, QK_NOPE + QK_ROPE), BF16), _sds((B_HEADS, t, QK_NOPE), F32),
         _sds((B_HEADS, t, V_HEAD), F32), _sds((t, QK_ROPE), F32)],
        (dh, q, kn, kpe, v, w_o, cos, sin), comm=comm)


def _kvq_bwd(h, dh, ckv, cqpre, dq, dkn, dv, dkpe, cos, sin, kvq_w):
    t = h.shape[0]
    half = QK_ROPE // 2

    def body(h_ref, dh_ref, ckv_ref, cqpre_ref, dq_ref, dkn_ref, dv_ref, dkpe_ref, cos_ref, sin_ref,
             srcg_ref, wkva_ref, kvag_ref, wkvb_ref, mixg_ref, wqa_ref, qg_ref, wqb_ref,
             dhi_ref, hq_ref, hk_ref, cq_ref, dcqpre_ref, c_ref, dkv_ref, dckv_ref,
             dmixg_ref, dsrcg_ref, dqg_ref, dkvag_ref):
        hv = h_ref[...]
        rstd = lax.rsqrt(jnp.mean(hv * hv, axis=-1, keepdims=True) + EPS)
        xhat = hv * rstd
        mixg, srcg, qg, kvag = mixg_ref[...], srcg_ref[...], qg_ref[...], kvag_ref[...]
        hq_ref[...] = (xhat * mixg).astype(BF16)
        hk_ref[...] = (xhat * srcg).astype(BF16)
        cq, cqhat, crstd = _rms_fwd(cqpre_ref[...], qg)
        cq_ref[...] = cq.astype(BF16)
        dcq = jnp.zeros((TM, Q_LORA), F32)
        for hd in range(B_HEADS):
            dcq = dcq + _dot_nt(dq_ref[hd], wqb_ref[hd])
        dcqpre, dqg = _rms_bwd(dcq, cqhat, crstd, qg)
        dcqpre_b = dcqpre.astype(BF16)
        dcqpre_ref[...] = dcqpre_b
        dxq, dmixg = _rms_bwd(_dot_nt(dcqpre_b, wqa_ref[...]), xhat, rstd, mixg)
        ckv = ckv_ref[...]
        c, chat, krstd = _rms_fwd(ckv[:, :KV_LORA], kvag)
        c_ref[...] = c.astype(BF16)
        dc = jnp.zeros((TM, KV_LORA), F32)
        for hd in range(B_HEADS):
            dkv = jnp.concatenate([dkn_ref[hd], dv_ref[hd]], axis=-1).astype(BF16)
            dkv_ref[hd] = dkv
            dc = dc + _dot_nt(dkv, wkvb_ref[hd])
        dlat, dkvag = _rms_bwd(dc, chat, krstd, kvag)
        dpe = _rope(dkpe_ref[...], cos_ref[...], -sin_ref[...])
        dckv_b = jnp.concatenate([dlat, dpe], axis=-1).astype(BF16)
        dckv_ref[...] = dckv_b
        dxk, dsrcg = _rms_bwd(_dot_nt(dckv_b, wkva_ref[...]), xhat, rstd, srcg)
        dhi_ref[...] = dh_ref[...] + dxq + dxk
        _acc(dmixg_ref, dmixg)
        _acc(dsrcg_ref, dsrcg)
        _acc(dqg_ref, dqg)
        _acc(dkvag_ref, dkvag)

    return _call(
        "kvq_bwd", body, (t // TM,),
        [_row(D_MODEL), _row(D_MODEL), _row(KV_LORA + QK_ROPE), _row(Q_LORA), _heads(QK_NOPE + QK_ROPE),
         _heads(QK_NOPE), _heads(V_HEAD), _row(QK_ROPE), _row(half), _row(half), *KVQ_W_SPECS],
        [_row(D_MODEL), _row(D_MODEL), _row(D_MODEL), _row(Q_LORA), _row(Q_LORA), _row(KV_LORA),
         _heads(QK_NOPE + V_HEAD), _row(KV_LORA + QK_ROPE),
         _const((1, D_MODEL)), _const((1, D_MODEL)), _const((1, Q_LORA)), _const((1, KV_LORA))],
        [_sds((t, D_MODEL), F32), _sds((t, D_MODEL), BF16), _sds((t, D_MODEL), BF16), _sds((t, Q_LORA), BF16),
         _sds((t, Q_LORA), BF16), _sds((t, KV_LORA), BF16), _sds((B_HEADS, t, QK_NOPE + V_HEAD), BF16),
         _sds((t, KV_LORA + QK_ROPE), BF16),
         _sds((1, D_MODEL), F32), _sds((1, D_MODEL), F32), _sds((1, Q_LORA), F32), _sds((1, KV_LORA), F32)],
        (h, dh, ckv, cqpre, dq, dkn, dv, dkpe, cos, sin, *kvq_w))[0]


def _a_mix_bwd(x, z, dh, g, w_in, ln_g, ln_b, w_s, b_st, w_out, comm=None):
    t = x.shape[0]
    tm = TM_GATE
    nblk = tm // GMLP_BLOCK

    def body(x_ref, z_ref, dh_ref, g_ref, win_ref, lng_ref, lnb_ref, ws_ref, bst_ref, wout_ref,
             dx_ref, hn_ref, dz_ref, dg_ref, dlng_ref, dlnb_ref, dws_ref, dbs_ref, dvn_scr, gelu_grad_v):
        @pl.when(pl.program_id(0) == 0)
        def _():
            dws_ref[...] = jnp.zeros_like(dws_ref)
            dbs_ref[...] = jnp.zeros_like(dbs_ref)

        gv, lng = g_ref[...], lng_ref[...]
        y, xhat, rstd = _rms_fwd(x_ref[...], gv)
        hn_ref[...] = y.astype(BF16)
        dhv = dh_ref[...]
        dgated = _dot_nt(dhv.astype(BF16), wout_ref[...])
        u, gelu_grad_u = _gelu_and_grad(z_ref[:, :GATE_DIM])
        v, gelu_grad_v[...] = _gelu_and_grad(z_ref[:, GATE_DIM:])
        vn, vhat, lrstd = _ln_fwd(v, lng, lnb_ref[...])
        vb = vn.astype(BF16)
        mask = _gate_mask()
        for gi in range(A_GROUPS):
            wm = jnp.where(mask, ws_ref[gi], 0.0).astype(BF16)
            bias = bst_ref[:, gi:gi + 1]
            cs = slice(gi * A_GROUP_DIM, (gi + 1) * A_GROUP_DIM)
            dws = jnp.zeros((GMLP_BLOCK, GMLP_BLOCK), F32)
            dbs = jnp.zeros((GMLP_BLOCK, 1), F32)
            for n in range(nblk):
                rs = slice(n * GMLP_BLOCK, (n + 1) * GMLP_BLOCK)
                sv = _dot(wm, vb[rs, cs]) + bias
                dz_ref[rs, cs] = (dgated[rs, cs] * sv * gelu_grad_u[rs, cs]).astype(BF16)
                dsv = dgated[rs, cs] * u[rs, cs]
                dsvb = dsv.astype(BF16)
                dws = dws + _dot_nt(dsvb, vb[rs, cs])
                dbs = dbs + jnp.sum(dsv, axis=-1, keepdims=True)
                dvn_scr[rs, cs] = _dot_tn(wm, dsvb)
            dws_ref[gi] += jnp.where(mask, dws, 0.0)
            dbs_ref[gi] += dbs
        dvn = dvn_scr[...]
        dvhat = dvn * lng
        dv = lrstd * (dvhat - jnp.mean(dvhat, axis=-1, keepdims=True)
                      - vhat * jnp.mean(dvhat * vhat, axis=-1, keepdims=True))
        dz_ref[:, GATE_DIM:] = (dv * gelu_grad_v[...]).astype(BF16)
        dhn = jnp.zeros((tm, D_MODEL), F32)
        for d in range(N_DEV):
            dhn = dhn + _dot_nt(dz_ref[:, d * FF_SLOT:(d + 1) * FF_SLOT], win_ref[d])
        dx, dg = _rms_bwd(dhn, xhat, rstd, gv)
        dx_ref[...] = dhv + dx
        _acc(dg_ref, dg)
        _acc(dlng_ref, jnp.sum(dvn * vhat, axis=0, keepdims=True))
        _acc(dlnb_ref, jnp.sum(dvn, axis=0, keepdims=True))

    return _call(
        "a_mix_bwd", body, (t // tm,),
        [_row(D_MODEL, tm), _row(2 * GATE_DIM, tm), _row(D_MODEL, tm), _res((1, D_MODEL)),
         _res((N_DEV, D_MODEL, FF_SLOT)), _res((1, GATE_DIM)), _res((1, GATE_DIM)),
         _res((A_GROUPS, GMLP_BLOCK, GMLP_BLOCK)), _res((GMLP_BLOCK, A_GROUPS)), _res((GATE_DIM, D_MODEL))],
        [_row(D_MODEL, tm), _row(D_MODEL, tm), _row(2 * GATE_DIM, tm),
         _const((1, D_MODEL)), _const((1, GATE_DIM)), _const((1, GATE_DIM)),
         _const((A_GROUPS, GMLP_BLOCK, GMLP_BLOCK)), _const((A_GROUPS, GMLP_BLOCK, 1))],
        [_sds((t, D_MODEL), F32), _sds((t, D_MODEL), BF16),
         _sds((t, 2 * GATE_DIM), BF16), _sds((1, D_MODEL), F32), _sds((1, GATE_DIM), F32),
         _sds((1, GATE_DIM), F32), _sds((A_GROUPS, GMLP_BLOCK, GMLP_BLOCK), F32),
         _sds((A_GROUPS, GMLP_BLOCK, 1), F32)],
        (x, z, dh, g, w_in, ln_g, ln_b, w_s, b_st, w_out),
        scratch=[pltpu.VMEM((tm, GATE_DIM), F32), pltpu.VMEM((tm, GATE_DIM), F32)], comm=comm)


def _wgrad(name, a, b, a_spec, b_spec, m, n, comm=None):
    def body(a_ref, b_ref, o_ref):
        o_ref[0] = _dot_tn(a_ref[...].astype(BF16), b_ref[...].astype(BF16)).astype(BF16)

    outs, got = _call(name, body, (N_DEV,), [a_spec, b_spec], [pl.BlockSpec((1, m, n), lambda d: (d, 0, 0))],
                      [_sds((N_DEV, m, n), BF16)], (a, b), comm=comm)
    return outs[0] if comm is None else (outs[0], got)


def _full(t, d):
    return pl.BlockSpec((t, d), lambda i: (0, 0), pipeline_mode=pl.Buffered(1))


def _cols(t, d):
    return pl.BlockSpec((t, d), lambda i: (0, i))


def _head(t, d):
    return pl.BlockSpec((None, t, d), lambda i: (i, 0, 0))


def _local_step(x, pos, target, inv_freq, wg, sm, shards=None):
    t = x.shape[0]
    wg = dict(wg)
    dist = shards is not None
    mix_g = [sm["norm_mix_g"][l:l + 1] for l in range(2)]
    mlp_g = [sm["norm_mlp_g"][l:l + 1] for l in range(2)]

    def gather(names):
        return _gather_comm([shards[k] for k in names]) if dist else None

    def send(grads):
        return _exchange_comm(grads=grads) if dist else None

    def send_sums(name, grads):
        return _chip_exchange_comm(_pair_reduce(name, grads)) if dist else None

    def a_args():
        return (wg["a_w_in"], wg["a_ln_v_g"], wg["a_ln_v_b"], sm["a_w_s"], sm["a_b_st"], wg["a_w_out"])

    def kvq_w():
        return (sm["kv_src_norm_g"], wg["kv_w_a"], sm["kv_a_norm_g"], wg["kv_w_b"], mix_g[1], wg["b_w_q_a"],
                sm["b_q_norm_g"], wg["b_w_q_b"])

    names = ("mlp_w1_0", "mlp_w2_0")
    (h1, z, gated), got = _a_mix_fwd(x, mix_g[0], *a_args(), comm=gather(names))
    wg.update(zip(names, got))
    names = ("kv_w_a", "kv_w_b", "b_w_q_a", "b_w_q_b", "b_w_o")
    (h2, a0), got = _mlp_fwd(h1, mlp_g[0], wg["mlp_w1_0"], wg["mlp_w2_0"], 0, comm=gather(names))
    wg.update(zip(names, got))
    if dist:
        wg["b_w_q_a"] = wg["b_w_q_a"].reshape(D_MODEL, Q_LORA)
        wg["kv_w_a"] = wg["kv_w_a"].reshape(D_MODEL, KV_LORA + QK_ROPE)
    ckv, kn, v, kpe, cqpre, q, cos, sin = _kvq_fwd(h2, pos, inv_freq, kvq_w())
    names = ("mlp_w1_1", "mlp_w2_1")
    (h3, att), got = _attn_fwd(h2, q, kn, kpe, v, wg["b_w_o"], comm=gather(names))
    wg.update(zip(names, got))
    (h4, a1), _ = _mlp_fwd(h3, mlp_g[1], wg["mlp_w1_1"], wg["mlp_w2_1"], 1)
    loss, dh4, d_final_g = _loss_head(h4, sm["final_norm_g"], target)

    g = {}
    (dh3, d_mlp_g1, hn, f, da, dh3_b), _ = _mlp_bwd(h3, a1, dh4, mlp_g[1], wg["mlp_w1_1"], wg["mlp_w2_1"], 1)
    g["mlp_w1_1"] = _wgrad("wgrad_w1_1", hn, da, _full(t, D_MODEL), _cols(t, FF_SLOT), D_MODEL, FF_SLOT)
    g["mlp_w2_1"] = _wgrad("wgrad_w2_1", f, dh4, _cols(t, FF_SLOT), _full(t, D_MODEL), FF_SLOT, D_MODEL)
    g["b_w_o"] = _wgrad("wgrad_w_o", att, dh3_b, _head(t, V_HEAD), _full(t, D_MODEL), V_HEAD, D_MODEL)
    names = ("mlp_w1_1", "mlp_w2_1", "b_w_o")
    (dq, dkn, dv, dkpe), got = _attn_bwd(dh3_b, q, kn, kpe, v, wg["b_w_o"], cos, sin,
                                         comm=send_sums("pair_reduce_1", [g[k] for k in names]))
    g.update(zip(names, got))
    (dh2, hq, hk, cq, dcqpre, c, dkv, dckv, d_mix_g1, d_src_g, d_q_g, d_kv_a_g) = _kvq_bwd(
        h2, dh3, ckv, cqpre, dq, dkn, dv, dkpe, cos, sin, kvq_w())
    g["b_w_q_a"] = _wgrad("wgrad_w_q_a", hq, dcqpre, _cols(t, D_MODEL // N_DEV), _full(t, Q_LORA),
                          D_MODEL // N_DEV, Q_LORA)
    g["b_w_q_b"] = _wgrad("wgrad_w_q_b", cq, dq, _full(t, Q_LORA), _head(t, QK_NOPE + QK_ROPE),
                          Q_LORA, QK_NOPE + QK_ROPE)
    g["kv_w_a"] = _wgrad("wgrad_kv_w_a", hk, dckv, _cols(t, D_MODEL // N_DEV), _full(t, KV_LORA + QK_ROPE),
                         D_MODEL // N_DEV, KV_LORA + QK_ROPE)
    g["kv_w_b"] = _wgrad("wgrad_kv_w_b", c, dkv, _full(t, KV_LORA), _head(t, QK_NOPE + V_HEAD),
                         KV_LORA, QK_NOPE + V_HEAD)
    names = ("b_w_q_a", "b_w_q_b", "kv_w_a", "kv_w_b")
    (dh1, d_mlp_g0, hn, f, da, dh1_b), got = _mlp_bwd(h1, a0, dh2, mlp_g[0], wg["mlp_w1_0"], wg["mlp_w2_0"], 0,
                                                      comm=send([g[k] for k in names]))
    g.update(zip(names, got))
    g["mlp_w1_0"] = _wgrad("wgrad_w1_0", hn, da, _full(t, D_MODEL), _cols(t, FF_SLOT), D_MODEL, FF_SLOT)
    g["mlp_w2_0"] = _wgrad("wgrad_w2_0", f, dh2, _cols(t, FF_SLOT), _full(t, D_MODEL), FF_SLOT, D_MODEL)
    g["a_w_out"] = _wgrad("wgrad_a_w_out", gated, dh1_b, _cols(t, GATE_DIM // N_DEV), _full(t, D_MODEL),
                          GATE_DIM // N_DEV, D_MODEL)
    names = ("mlp_w1_0", "mlp_w2_0", "a_w_out")
    (dx, hn, dz, d_mix_g0, d_ln_g, d_ln_b, d_ws, d_bs), got = _a_mix_bwd(
        x, z, dh1, mix_g[0], *a_args(), comm=send_sums("pair_reduce_0", [g[k] for k in names]))
    g.update(zip(names, got))
    small = {
        "norm_mix_g": jnp.concatenate([d_mix_g0, d_mix_g1], axis=0),
        "norm_mlp_g": jnp.concatenate([d_mlp_g0, d_mlp_g1], axis=0),
        "a_ln_v_g": d_ln_g.reshape(N_DEV, GATE_DIM // N_DEV),
        "a_ln_v_b": d_ln_b.reshape(N_DEV, GATE_DIM // N_DEV),
        "a_w_s": d_ws.astype(BF16) if dist else d_ws,
        "a_b_s": d_bs.reshape(A_GROUPS, GMLP_BLOCK),
        "b_q_norm_g": d_q_g,
        "kv_src_norm_g": d_src_g,
        "kv_a_norm_g": d_kv_a_g,
        "final_norm_g": d_final_g,
    }
    wgrad_in = ("wgrad_a_w_in", hn, dz, _full(t, D_MODEL), _cols(t, FF_SLOT), D_MODEL, FF_SLOT)
    if dist:
        parts = [small[k].reshape((1,) + small[k].shape) for k in SMALL] + [loss.reshape(1, 1, 1)]
        g["a_w_in"], got = _wgrad(*wgrad_in, comm=_exchange_comm(parts=parts))
        small, loss = dict(zip(SMALL, got)), got[-1]
    else:
        g["a_w_in"] = _wgrad(*wgrad_in)
    return loss, dx, g, small


def _adamw(w, g, m, v):
    m = ADAM_B1 * m + (1.0 - ADAM_B1) * g
    v = ADAM_B2 * v + (1.0 - ADAM_B2) * (g * g)
    m_hat = m / (1.0 - ADAM_B1 ** ADAM_STEP)
    v_hat = v / (1.0 - ADAM_B2 ** ADAM_STEP)
    return -ADAM_LR * (m_hat / (jnp.sqrt(v_hat) + ADAM_EPS) + ADAM_WD * w), m, v


def _sum_in_device_order(r_ref):
    g = r_ref[0].astype(F32)
    for j in range(1, r_ref.shape[0]):
        g = g + r_ref[j].astype(F32)
    return g


def _adamw_sharded(name, recvs, w, m, v, comm=None):
    layers, r, c = w.shape
    tr = math.gcd(r, 256)
    flat = [a for per_layer in recvs for a in per_layer]

    def body(*refs):
        r_refs, (w_ref, m_ref, v_ref) = refs[:len(flat)], refs[len(flat):len(flat) + 3]
        g_ref, d_ref, nm_ref, nv_ref = refs[-4:]
        layer = pl.program_id(0)
        g, pos = None, 0
        for li, per_layer in enumerate(recvs):
            total = None
            for ref in r_refs[pos:pos + len(per_layer)]:
                part = _sum_in_device_order(ref)
                total = part if total is None else total + part
            pos += len(per_layer)
            g = total if g is None else jnp.where(layer == li, total, g)
        g_ref[...] = g
        d_ref[...], nm_ref[...], nv_ref[...] = _adamw(w_ref[...], g, m_ref[...], v_ref[...])

    blk = pl.BlockSpec((None, tr, c), lambda l, i: (l, i, 0))
    return _call(name, body, (layers, r // tr),
                 [pl.BlockSpec((a.shape[0], tr, c), lambda l, i: (0, i, 0)) for a in flat] + [blk] * 3,
                 [blk] * 4, [_sds(w.shape, F32)] * 4, (*flat, w, m, v), comm=comm)


def _adamw_small(recvs, ws, ms, vs, own_row, losses):
    n = len(recvs)

    def body(*refs):
        r_refs, w_refs, m_refs, v_refs = (refs[i * n:(i + 1) * n] for i in range(4))
        outs, scr = refs[4 * n + 1:8 * n + 2], refs[8 * n + 2:]
        outs[-1][...] = _sum_in_device_order(refs[4 * n])
        me = _my_place()[3]
        for a in range(n):
            g = _sum_in_device_order(r_refs[a])
            if own_row[a]:
                scr[0][...] = g
                g = scr[0][pl.ds(me, 1), :]
            g_ref, d_ref, nm_ref, nv_ref = outs[4 * a:4 * a + 4]
            g_ref[...] = g
            d_ref[...], nm_ref[...], nv_ref[...] = _adamw(w_refs[a][...], g, m_refs[a][...], v_refs[a][...])

    out_shape = []
    for w in ws:
        out_shape += [_sds(w.shape, F32)] * 4
    return pl.pallas_call(
        body, name="adamw_small", in_specs=[VMEM] * (4 * n + 1), out_specs=[VMEM] * (4 * n + 1),
        out_shape=out_shape + [_sds((1, 1), F32)], scratch_shapes=[pltpu.VMEM((N_DEV, GATE_DIM // N_DEV), F32)],
    )(*recvs, *ws, *ms, *vs, losses)


BIG = ("a_w_in", "a_w_out", "b_w_q_a", "b_w_q_b", "b_w_o", "kv_w_a", "kv_w_b", "mlp_w1", "mlp_w2")
SMALL = ("norm_mix_g", "norm_mlp_g", "a_ln_v_g", "a_ln_v_b", "a_w_s", "a_b_s", "b_q_norm_g", "kv_src_norm_g",
         "kv_a_norm_g", "final_norm_g")
WEIGHTS = ("norm_mix_g", "norm_mlp_g", "a_w_in", "a_ln_v_g", "a_ln_v_b", "a_w_s", "a_b_s", "a_w_out", "b_w_q_a",
           "b_q_norm_g", "b_w_q_b", "b_w_o", "kv_src_norm_g", "kv_w_a", "kv_a_norm_g", "kv_w_b", "mlp_w1", "mlp_w2",
           "final_norm_g")


def _two_d(name, a):
    if name in ("a_w_s", "a_b_s"):
        return a.reshape(a.shape[1:])
    return a.reshape(1, -1) if a.ndim == 1 else a


def _three_d(a):
    return a if a.ndim == 3 else a.reshape((1,) + a.shape)


def kernel(x, positions, norm_mix_g, norm_mlp_g, a_w_in, a_ln_v_g, a_ln_v_b, a_w_s, a_b_s, a_w_out, b_w_q_a, b_q_norm_g, b_w_q_b, b_w_o, kv_src_norm_g, kv_w_a, kv_a_norm_g, kv_w_b, mlp_w1, mlp_w2, final_norm_g, loss_target, m_norm_mix_g, m_norm_mlp_g, m_a_w_in, m_a_ln_v_g, m_a_ln_v_b, m_a_w_s, m_a_b_s, m_a_w_out, m_b_w_q_a, m_b_q_norm_g, m_b_w_q_b, m_b_w_o, m_kv_src_norm_g, m_kv_w_a, m_kv_a_norm_g, m_kv_w_b, m_mlp_w1, m_mlp_w2, m_final_norm_g, v_norm_mix_g, v_norm_mlp_g, v_a_w_in, v_a_ln_v_g, v_a_ln_v_b, v_a_w_s, v_a_b_s, v_a_w_out, v_b_w_q_a, v_b_q_norm_g, v_b_w_q_b, v_b_w_o, v_kv_src_norm_g, v_kv_w_a, v_kv_a_norm_g, v_kv_w_b, v_mlp_w1, v_mlp_w2, v_final_norm_g):
    w = dict(norm_mix_g=norm_mix_g, norm_mlp_g=norm_mlp_g, a_w_in=a_w_in, a_ln_v_g=a_ln_v_g, a_ln_v_b=a_ln_v_b,
             a_w_s=a_w_s, a_b_s=a_b_s, a_w_out=a_w_out, b_w_q_a=b_w_q_a, b_q_norm_g=b_q_norm_g, b_w_q_b=b_w_q_b,
             b_w_o=b_w_o, kv_src_norm_g=kv_src_norm_g, kv_w_a=kv_w_a, kv_a_norm_g=kv_a_norm_g, kv_w_b=kv_w_b,
             mlp_w1=mlp_w1, mlp_w2=mlp_w2, final_norm_g=final_norm_g)
    m = dict(norm_mix_g=m_norm_mix_g, norm_mlp_g=m_norm_mlp_g, a_w_in=m_a_w_in, a_ln_v_g=m_a_ln_v_g,
             a_ln_v_b=m_a_ln_v_b, a_w_s=m_a_w_s, a_b_s=m_a_b_s, a_w_out=m_a_w_out, b_w_q_a=m_b_w_q_a,
             b_q_norm_g=m_b_q_norm_g, b_w_q_b=m_b_w_q_b, b_w_o=m_b_w_o, kv_src_norm_g=m_kv_src_norm_g,
             kv_w_a=m_kv_w_a, kv_a_norm_g=m_kv_a_norm_g, kv_w_b=m_kv_w_b, mlp_w1=m_mlp_w1, mlp_w2=m_mlp_w2,
             final_norm_g=m_final_norm_g)
    v = dict(norm_mix_g=v_norm_mix_g, norm_mlp_g=v_norm_mlp_g, a_w_in=v_a_w_in, a_ln_v_g=v_a_ln_v_g,
             a_ln_v_b=v_a_ln_v_b, a_w_s=v_a_w_s, a_b_s=v_a_b_s, a_w_out=v_a_w_out, b_w_q_a=v_b_w_q_a,
             b_q_norm_g=v_b_q_norm_g, b_w_q_b=v_b_w_q_b, b_w_o=v_b_w_o, kv_src_norm_g=v_kv_src_norm_g,
             kv_w_a=v_kv_w_a, kv_a_norm_g=v_kv_a_norm_g, kv_w_b=v_kv_w_b, mlp_w1=v_mlp_w1, mlp_w2=v_mlp_w2,
             final_norm_g=v_final_norm_g)
    t = x.shape[1]

    first = ("a_w_in", "a_w_out", "a_ln_v_g", "a_ln_v_b")
    later = ("mlp_w1_0", "mlp_w2_0", "mlp_w1_1", "mlp_w2_1", "kv_w_a", "kv_w_b", "b_w_q_a", "b_w_q_b", "b_w_o")
    blocks = {k: _three_d(w[k]) for k in BIG if not k.startswith("mlp")}
    for k in ("mlp_w1", "mlp_w2"):
        blocks[k + "_0"], blocks[k + "_1"] = w[k][0:1], w[k][1:2]
    got, casts = _gather_first([blocks[k] if k in blocks else w[k] for k in first], [blocks[k] for k in later])
    wg = dict(zip(first, got))
    wg["a_w_out"] = wg["a_w_out"].reshape(GATE_DIM, D_MODEL)
    wg["a_ln_v_g"] = wg["a_ln_v_g"].reshape(1, GATE_DIM)
    wg["a_ln_v_b"] = wg["a_ln_v_b"].reshape(1, GATE_DIM)
    shards = dict(zip(later, casts))

    sm = {k: _two_d(k, w[k]) for k in SMALL if k not in ("a_ln_v_g", "a_ln_v_b")}
    sm["a_b_st"] = sm["a_b_s"].T
    inv_freq = (ROPE_THETA ** (-jnp.arange(0, QK_ROPE, 2, dtype=F32) / QK_ROPE)).reshape(1, QK_ROPE // 2)

    losses, dx, g, small = _local_step(x[0], positions.reshape(t, 1), loss_target[0], inv_freq, wg, sm, shards)

    g["a_w_in"], = _comm_only("exchange_last", _chip_exchange_comm(_pair_reduce("pair_reduce_a", [g["a_w_in"]])))

    out = {}
    for k in BIG:
        recvs = [[g[k + "_0"]], [g[k + "_1"]]] if k.startswith("mlp") else [[g[k]]]
        res, _ = _adamw_sharded("adamw_" + k, recvs, _three_d(w[k]), _three_d(m[k]), _three_d(v[k]))
        out[k] = [o.reshape(w[k].shape) for o in res]
    own_row = [k in ("a_ln_v_g", "a_ln_v_b") for k in SMALL]
    res = _adamw_small([small[k] for k in SMALL], [_two_d(k, w[k]) for k in SMALL], [_two_d(k, m[k]) for k in SMALL],
                       [_two_d(k, v[k]) for k in SMALL], own_row, losses)
    for i, k in enumerate(SMALL):
        out[k] = [o.reshape(w[k].shape) for o in res[4 * i:4 * i + 4]]

    return (res[-1].reshape(()), dx.reshape(x.shape), *[out[k][0] for k in WEIGHTS], *[out[k][1] for k in WEIGHTS],
            *[out[k][2] for k in WEIGHTS], *[out[k][3] for k in WEIGHTS])
```

```python
import math

import jax
import jax.numpy as jnp
from jax import lax
from jax.experimental import pallas as pl
from jax.experimental.pallas import tpu as pltpu

F32, BF16 = jnp.float32, jnp.bfloat16
MESH = pl.DeviceIdType.MESH
ANY = pl.BlockSpec(memory_space=pl.ANY)
VMEM = pl.BlockSpec(memory_space=pltpu.VMEM)

N_DEV = 8
D_MODEL = 1024
CHUNK = 64
GMLP_BLOCK = 128
GATE_DIM = 2048
A_GROUPS = 8
A_GROUP_DIM = GATE_DIM // A_GROUPS
B_HEADS = 8
QK_NOPE, QK_ROPE, V_HEAD = 128, 64, 128
Q_LORA, KV_LORA = 384, 256
ROPE_THETA = 10000.0
D_FF = 4096
FF_SLOT = D_FF // N_DEV
EPS = 1e-6
ATT_SCALE = (QK_NOPE + QK_ROPE) ** -0.5

ADAM_LR, ADAM_B1, ADAM_B2, ADAM_EPS, ADAM_WD, ADAM_STEP = 0.001, 0.9, 0.999, 1e-08, 0.01, 10

TM = 256
TM_GATE = 128
VMEM_LIMIT = 56 * 1024 * 1024
INV_SQRT2 = 1.0 / math.sqrt(2.0)
INV_SQRT_2PI = 1.0 / math.sqrt(2.0 * math.pi)


def _dot(a, b):
    return jnp.dot(a, b, preferred_element_type=F32)


def _dot_nt(a, b):
    return lax.dot_general(a, b, (((1,), (1,)), ((), ())), preferred_element_type=F32)


def _dot_tn(a, b):
    return lax.dot_general(a, b, (((0,), (0,)), ((), ())), preferred_element_type=F32)


def _rms_fwd(x, g):
    rstd = lax.rsqrt(jnp.mean(x * x, axis=-1, keepdims=True) + EPS)
    xhat = x * rstd
    return xhat * g, xhat, rstd


def _rms_bwd(dy, xhat, rstd, g):
    dxhat = dy * g
    dx = rstd * (dxhat - xhat * jnp.mean(dxhat * xhat, axis=-1, keepdims=True))
    return dx, jnp.sum(dy * xhat, axis=0, keepdims=True)


def _ln_fwd(v, g, b):
    mu = jnp.mean(v, axis=-1, keepdims=True)
    vc = v - mu
    rstd = lax.rsqrt(jnp.mean(vc * vc, axis=-1, keepdims=True) + EPS)
    vhat = vc * rstd
    return vhat * g + b, vhat, rstd


def _gelu(x):
    return 0.5 * x * (1.0 + lax.erf(x * INV_SQRT2))


def _gelu_and_grad(x):
    cdf = 0.5 * (1.0 + lax.erf(x * INV_SQRT2))
    return x * cdf, cdf + x * jnp.exp(-0.5 * x * x) * INV_SQRT_2PI


def _rope(x, cos, sin):
    x1, x2 = x[:, :QK_ROPE // 2], x[:, QK_ROPE // 2:]
    return jnp.concatenate([x1 * cos - x2 * sin, x2 * cos + x1 * sin], axis=-1)


def _gate_mask():
    row = lax.broadcasted_iota(jnp.int32, (GMLP_BLOCK, GMLP_BLOCK), 0)
    col = lax.broadcasted_iota(jnp.int32, (GMLP_BLOCK, GMLP_BLOCK), 1)
    return (col < CHUNK) | (row >= CHUNK)


def _att_mask(q0, tq, t):
    q = q0 + lax.broadcasted_iota(jnp.int32, (tq, t), 0)
    k = lax.broadcasted_iota(jnp.int32, (tq, t), 1)
    return jnp.right_shift(k, 6) <= jnp.right_shift(q, 6)


def _res(shape, imap=None):
    zeros = (0,) * len(shape)
    return pl.BlockSpec(shape, imap or (lambda i: zeros), pipeline_mode=pl.Buffered(1))


def _const(shape):
    zeros = (0,) * len(shape)
    return pl.BlockSpec(shape, lambda i: zeros)


def _row(d, tm=TM):
    return pl.BlockSpec((tm, d), lambda i: (i, 0))


def _heads(d):
    return pl.BlockSpec((B_HEADS, TM, d), lambda i: (0, i, 0))


def _sds(shape, dt):
    return jax.ShapeDtypeStruct(shape, dt)


def _acc(ref, val):
    @pl.when(pl.program_id(0) == 0)
    def _():
        ref[...] = jnp.zeros_like(ref)
    ref[...] += val


def _my_place():
    x, y, c = lax.axis_index("x"), lax.axis_index("y"), lax.axis_index("c")
    return x, y, c, 4 * x + 2 * y + c


def _peer(x, y, c, k):
    px = 1 - x if k & 4 else x
    py = 1 - y if k & 2 else y
    pc = 1 - c if k & 1 else c
    return (px, py, pc), 4 * px + 2 * py + pc


CHIPS = (2, 4, 6)


def _splits(ref):
    return len(ref.shape) >= 3 and ref.shape[1] % 32 == 0


def _piece(ref, block, half=None):
    if half is None or not _splits(ref):
        return ref.at[pl.ds(block, 1)]
    rows = ref.shape[1] // 2
    return ref.at[pl.ds(block, 1), pl.ds(half * rows, rows)]


def _gather_copy(sems, a, k, piece, to, src=None):
    return pltpu.make_async_remote_copy(
        src_ref=piece if src is None else src, dst_ref=piece, send_sem=sems[0].at[a, k], recv_sem=sems[1].at[a, k],
        device_id=to, device_id_type=MESH)


def _gather_start(srcs, outs, sems, only=None):
    x, y, c, me = _my_place()
    for a in range(len(srcs)) if only is None else (only,):
        mine = _piece(outs[a], me)
        pltpu.make_async_copy(srcs[a], mine, sems[2].at[a]).start()
        for k, rel in enumerate((1, 4, 2)):
            _gather_copy(sems, a, k, mine, _peer(x, y, c, rel)[0], src=srcs[a]).start()


def _gather_relay(srcs, outs, sems):
    x, y, c, _ = _my_place()
    sib = _peer(x, y, c, 1)[0]
    (xn, xn_i), (yn, yn_i) = _peer(x, y, c, 4), _peer(x, y, c, 2)
    for a in range(len(srcs)):
        out = outs[a]
        _gather_copy(sems, a, 1, _piece(out, xn_i), xn).wait_recv()
        _gather_copy(sems, a, 3, _piece(out, xn_i, 0), yn).start()
        _gather_copy(sems, a, 5, _piece(out, xn_i), sib).start()
        _gather_copy(sems, a, 2, _piece(out, yn_i), yn).wait_recv()
        if _splits(out):
            _gather_copy(sems, a, 4, _piece(out, yn_i, 1), xn).start()
        _gather_copy(sems, a, 6, _piece(out, yn_i), sib).start()


def _gather_finish(srcs, outs, sems):
    x, y, c, me = _my_place()
    sib = _peer(x, y, c, 1)[0]
    xn, yn, dg_i = _peer(x, y, c, 4)[0], _peer(x, y, c, 2)[0], _peer(x, y, c, 6)[1]
    n = len(srcs)
    for a in range(n):
        out = outs[a]
        _gather_copy(sems, a, 3, _piece(out, dg_i, 0), yn).wait_recv()
        _gather_copy(sems, a, 7, _piece(out, dg_i, 0), sib).start()
        if _splits(out):
            _gather_copy(sems, a, 4, _piece(out, dg_i, 1), xn).wait_recv()
            _gather_copy(sems, a, 8, _piece(out, dg_i, 1), sib).start()
    for a in range(n):
        out = outs[a]
        whole, half = _piece(out, me), _piece(out, me, 0)
        for k in (0, 5, 6):
            _gather_copy(sems, a, k, whole, sib).wait_recv()
        for k in (7, 8) if _splits(out) else (7,):
            _gather_copy(sems, a, k, half, sib).wait_recv()
        for k in (0, 1, 2):
            _gather_copy(sems, a, k, whole, sib, src=srcs[a]).wait_send()
        for k in (5, 6):
            _gather_copy(sems, a, k, whole, sib).wait_send()
        for k in (3, 4, 7, 8) if _splits(out) else (3, 7):
            _gather_copy(sems, a, k, half, sib).wait_send()
        pltpu.make_async_copy(srcs[a], whole, sems[2].at[a]).wait()


def _relay_sems(n):
    return [pltpu.SemaphoreType.DMA((n, 9)), pltpu.SemaphoreType.DMA((n, 9)), pltpu.SemaphoreType.DMA((n,))]


def _gather_sems(n):
    return [pltpu.SemaphoreType.DMA((n, 7)), pltpu.SemaphoreType.DMA((n, 7)), pltpu.SemaphoreType.DMA((n,))]


class _Comm:
    def __init__(self, args, out_shape, scratch, start, finish, relay=None):
        self.args, self.out_shape, self.scratch, self.start, self.finish = args, out_shape, scratch, start, finish
        self.relay = relay


def _gather_comm(shards):
    return _Comm(list(shards), [_sds((N_DEV,) + s.shape[1:], s.dtype) for s in shards], _relay_sems(len(shards)),
                 _gather_start, _gather_finish, relay=_gather_relay)


def _direct_copies(ins, outs, sems, wait, from_block):
    send_sems, recv_sems, local_sems = sems
    x, y, c, me = _my_place()
    for a in range(len(ins)):
        src = ins[a].at[pl.ds(me, 1)] if from_block[a] else ins[a]
        local = pltpu.make_async_copy(src, outs[a].at[pl.ds(me, 1)], local_sems.at[a])
        local.wait() if wait else local.start()
        for k in range(1, N_DEV):
            to, to_i = _peer(x, y, c, k)
            cp = pltpu.make_async_remote_copy(
                src_ref=ins[a].at[pl.ds(to_i, 1)] if from_block[a] else ins[a], dst_ref=outs[a].at[pl.ds(me, 1)],
                send_sem=send_sems.at[a, k - 1], recv_sem=recv_sems.at[a, k - 1], device_id=to, device_id_type=MESH)
            cp.wait() if wait else cp.start()


def _exchange_comm(grads=(), parts=()):
    ins = list(grads) + list(parts)
    from_block = [True] * len(grads) + [False] * len(parts)
    out_shape = [_sds(g.shape, g.dtype) for g in grads] + [_sds((N_DEV,) + p.shape[1:], p.dtype) for p in parts]

    def start(ins_, outs_, sems_):
        _direct_copies(ins_, outs_, sems_, False, from_block)

    def finish(ins_, outs_, sems_):
        _direct_copies(ins_, outs_, sems_, True, from_block)

    return _Comm(ins, out_shape, _gather_sems(len(ins)), start, finish)


def _chip_copies(ins, outs, sems, wait, rels, own):
    send_sems, recv_sems, local_sems = sems
    x, y, c, _ = _my_place()
    for a in range(len(ins)):
        if own:
            local = pltpu.make_async_copy(ins[a].at[pl.ds(2 * x + y, 1)], outs[a].at[pl.ds(len(rels), 1)],
                                          local_sems.at[a])
            local.wait() if wait else local.start()
        for i, j in enumerate(rels):
            to = _peer(x, y, c, CHIPS[j])[0]
            cp = pltpu.make_async_remote_copy(
                src_ref=ins[a].at[pl.ds(2 * to[0] + to[1], 1)], dst_ref=outs[a].at[pl.ds(i, 1)],
                send_sem=send_sems.at[a, i], recv_sem=recv_sems.at[a, i], device_id=to, device_id_type=MESH)
            cp.wait() if wait else cp.start()


def _chip_exchange_comm(sums, rels=(0, 1, 2), own=True):
    def start(ins_, outs_, sems_):
        _chip_copies(ins_, outs_, sems_, False, rels, own)

    def finish(ins_, outs_, sems_):
        _chip_copies(ins_, outs_, sems_, True, rels, own)

    n = len(sums)
    sems = [pltpu.SemaphoreType.DMA((n, len(rels))), pltpu.SemaphoreType.DMA((n, len(rels))),
            pltpu.SemaphoreType.DMA((n,))]
    return _Comm(list(sums), [_sds((len(rels) + own,) + s.shape[1:], s.dtype) for s in sums], sems, start, finish)


def _pair_reduce(name, grads):
    n = len(grads)
    n_chips = N_DEV // 2

    def body(*refs):
        g_refs, gh_refs, p_refs, land = refs[:n], refs[n:2 * n], refs[2 * n:3 * n], refs[3 * n:4 * n]
        send_sems, recv_sems = refs[4 * n:]
        x, y, c, _ = _my_place()
        sib = _peer(x, y, c, 1)[0]
        q = pl.program_id(0)

        def to_sibling(a, j):
            return pltpu.make_async_remote_copy(
                src_ref=gh_refs[a].at[j, pl.ds(1 - c, 1)], dst_ref=land[a].at[pl.ds(j, 1)],
                send_sem=send_sems.at[a, j], recv_sem=recv_sems.at[a, j], device_id=sib, device_id_type=MESH)

        @pl.when(q == 0)
        def _():
            for j in range(n_chips):
                for a in range(n):
                    to_sibling(a, j).start()

        for a in range(n):
            to_sibling(a, q).wait_recv()
            p_refs[a][...] = (g_refs[a][0, pl.ds(c, 1)].astype(F32) + land[a][pl.ds(q, 1)].astype(F32)).astype(BF16)

        @pl.when(q == n_chips - 1)
        def _():
            for a in range(n):
                for j in range(n_chips):
                    to_sibling(a, j).wait_send()

    views = [g.reshape((n_chips, 2) + g.shape[1:]) for g in grads]
    res = pl.pallas_call(
        body, name=name, grid=(n_chips,),
        in_specs=[pl.BlockSpec((1, 2) + g.shape[1:], lambda q: (q, 0, 0, 0)) for g in grads] + [ANY] * n,
        out_specs=[pl.BlockSpec((1,) + g.shape[1:], lambda q: (q, 0, 0)) for g in grads],
        out_shape=[_sds((n_chips,) + g.shape[1:], BF16) for g in grads],
        scratch_shapes=[pltpu.VMEM((n_chips,) + g.shape[1:], BF16) for g in grads]
        + [pltpu.SemaphoreType.DMA((n, n_chips)), pltpu.SemaphoreType.DMA((n, n_chips))],
        compiler_params=pltpu.CompilerParams(dimension_semantics=("arbitrary",), vmem_limit_bytes=VMEM_LIMIT),
    )(*views, *views)
    return list(res)


def _call(name, body, grid, in_specs, out_specs, out_shape, args, scratch=(), comm=None):
    params = pltpu.CompilerParams(dimension_semantics=("arbitrary",) * len(grid), vmem_limit_bytes=VMEM_LIMIT)
    if comm is None:
        outs = pl.pallas_call(body, name=name, grid=grid, in_specs=list(in_specs), out_specs=list(out_specs),
                              out_shape=list(out_shape), scratch_shapes=list(scratch), compiler_params=params)(*args)
        return list(outs), []
    ni, nci, no, nco, ns = len(in_specs), len(comm.args), len(out_specs), len(comm.out_shape), len(scratch)

    def carrying(*refs):
        ins, refs = refs[:ni], refs[ni:]
        cin, refs = refs[:nci], refs[nci:]
        outs, refs = refs[:no], refs[no:]
        cout, refs = refs[:nco], refs[nco:]
        scr, csems = refs[:ns], refs[ns:]
        step = pl.program_id(0)
        for ax in range(1, len(grid)):
            step = step * grid[ax] + pl.program_id(ax)
        steps = math.prod(grid)

        @pl.when(step == 0)
        def _():
            comm.start(cin, cout, csems)

        if comm.relay is not None:
            @pl.when(step == (2 * steps) // 3)
            def _():
                comm.relay(cin, cout, csems)

        body(*ins, *outs, *scr)

        @pl.when(step == steps - 1)
        def _():
            comm.finish(cin, cout, csems)

    outs = pl.pallas_call(
        carrying, name=name, grid=grid, in_specs=list(in_specs) + [ANY] * nci, out_specs=list(out_specs) + [ANY] * nco,
        out_shape=list(out_shape) + list(comm.out_shape), scratch_shapes=list(scratch) + list(comm.scratch),
        compiler_params=params)(*args, *comm.args)
    return list(outs[:no]), list(outs[no:])


def _comm_only(name, comm):
    def body(*refs):
        nci, nco = len(comm.args), len(comm.out_shape)
        cin, cout, csems = refs[:nci], refs[nci:nci + nco], refs[nci + nco:]
        comm.start(cin, cout, csems)
        if comm.relay is not None:
            comm.relay(cin, cout, csems)
        comm.finish(cin, cout, csems)

    return pl.pallas_call(body, name=name, in_specs=[ANY] * len(comm.args), out_specs=[ANY] * len(comm.out_shape),
                          out_shape=list(comm.out_shape), scratch_shapes=list(comm.scratch))(*comm.args)


def _gather_first(first, later):
    nf, nl = len(first), len(later)
    dts = [BF16] * (nf - 2) + [F32, F32]

    def body(*refs):
        ins, refs = refs[:nf + nl], refs[nf + nl:]
        outs, refs = refs[:nf], refs[nf:]
        casts, refs = refs[:nl], refs[nl:]
        stage, sems = refs[:nf], refs[nf:]
        for a in range(nf):
            stage[a][...] = ins[a][...].astype(dts[a])
            _gather_start(stage, outs, sems, only=a)
        for a in range(nl):
            casts[a][...] = ins[nf + a][...].astype(BF16)
        _gather_relay(stage, outs, sems)
        _gather_finish(stage, outs, sems)

    res = pl.pallas_call(
        body, name="gather_first",
        in_specs=[VMEM] * (nf + nl), out_specs=[ANY] * nf + [VMEM] * nl,
        out_shape=[_sds((N_DEV,) + s.shape[1:], dt) for s, dt in zip(first, dts)]
        + [_sds(s.shape, BF16) for s in later],
        scratch_shapes=[pltpu.VMEM(s.shape, dt) for s, dt in zip(first, dts)] + _relay_sems(nf),
        compiler_params=pltpu.CompilerParams(vmem_limit_bytes=VMEM_LIMIT),
    )(*first, *later)
    return list(res[:nf]), list(res[nf:])


def _a_mix_fwd(x, g, w_in, ln_g, ln_b, w_s, b_st, w_out, comm=None):
    t = x.shape[0]
    nblk = TM // GMLP_BLOCK

    def body(x_ref, g_ref, win_ref, lng_ref, lnb_ref, ws_ref, bst_ref, wout_ref, h_ref, z_ref, gated_scr):
        xv = x_ref[...]
        hb = _rms_fwd(xv, g_ref[...])[0].astype(BF16)
        for d in range(N_DEV):
            z_ref[:, d * FF_SLOT:(d + 1) * FF_SLOT] = _dot(hb, win_ref[d])
        u = _gelu(z_ref[:, :GATE_DIM])
        vb = _ln_fwd(_gelu(z_ref[:, GATE_DIM:]), lng_ref[...], lnb_ref[...])[0].astype(BF16)
        mask = _gate_mask()
        for gi in range(A_GROUPS):
            wm = jnp.where(mask, ws_ref[gi], 0.0).astype(BF16)
            bias = bst_ref[:, gi:gi + 1]
            cs = slice(gi * A_GROUP_DIM, (gi + 1) * A_GROUP_DIM)
            for n in range(nblk):
                rs = slice(n * GMLP_BLOCK, (n + 1) * GMLP_BLOCK)
                sv = _dot(wm, vb[rs, cs]) + bias
                gated_scr[rs, cs] = (u[rs, cs] * sv).astype(BF16)
        h_ref[...] = xv + _dot(gated_scr[...], wout_ref[...])

    return _call(
        "a_mix_fwd", body, (t // TM,),
        [_row(D_MODEL), _res((1, D_MODEL)), _res((N_DEV, D_MODEL, FF_SLOT)), _res((1, GATE_DIM)),
         _res((1, GATE_DIM)), _res((A_GROUPS, GMLP_BLOCK, GMLP_BLOCK)), _res((GMLP_BLOCK, A_GROUPS)),
         _res((GATE_DIM, D_MODEL))],
        [_row(D_MODEL), _row(2 * GATE_DIM), _row(GATE_DIM)],
        [_sds((t, D_MODEL), F32), _sds((t, 2 * GATE_DIM), F32), _sds((t, GATE_DIM), BF16)],
        (x, g, w_in, ln_g, ln_b, w_s, b_st, w_out), comm=comm)


MLP_W_SPECS = (_res((N_DEV, D_MODEL, FF_SLOT)), _res((N_DEV, FF_SLOT, D_MODEL)))


def _mlp_fwd(h, g, w1, w2, layer, comm=None):
    t = h.shape[0]

    def body(h_ref, g_ref, w1_ref, w2_ref, o_ref, a_ref):
        hv = h_ref[...]
        hb = _rms_fwd(hv, g_ref[...])[0].astype(BF16)
        o_ref[...] = hv
        for d in range(N_DEV):
            a = _dot(hb, w1_ref[d])
            a_ref[:, d * FF_SLOT:(d + 1) * FF_SLOT] = a
            r = jnp.maximum(a, 0.0)
            o_ref[...] += _dot((r * r).astype(BF16), w2_ref[d])

    return _call(
        f"mlp_fwd_{layer}", body, (t // TM,), [_row(D_MODEL), _res((1, D_MODEL)), *MLP_W_SPECS],
        [_row(D_MODEL), _row(D_FF)], [_sds((t, D_MODEL), F32), _sds((t, D_FF), F32)], (h, g, w1, w2), comm=comm)


KVQ_W_SPECS = (_res((1, D_MODEL)), _res((D_MODEL, KV_LORA + QK_ROPE)), _res((1, KV_LORA)),
               _res((B_HEADS, KV_LORA, QK_NOPE + V_HEAD)), _res((1, D_MODEL)), _res((D_MODEL, Q_LORA)),
               _res((1, Q_LORA)), _res((B_HEADS, Q_LORA, QK_NOPE + QK_ROPE)))


def _kvq_fwd(h, pos, inv_freq, kvq_w):
    t = h.shape[0]
    half = QK_ROPE // 2

    def body(h_ref, pos_ref, invf_ref, srcg_ref, wkva_ref, kvag_ref, wkvb_ref, mixg_ref, wqa_ref, qg_ref, wqb_ref,
             ckv_ref, kn_ref, v_ref, kpe_ref, cqpre_ref, q_ref, cos_ref, sin_ref):
        hv = h_ref[...]
        xhat = hv * lax.rsqrt(jnp.mean(hv * hv, axis=-1, keepdims=True) + EPS)
        ang = pos_ref[...].astype(F32) * invf_ref[...]
        cos, sin = jnp.cos(ang), jnp.sin(ang)
        cos_ref[...] = cos
        sin_ref[...] = sin
        ckv = _dot((xhat * srcg_ref[...]).astype(BF16), wkva_ref[...])
        ckv_ref[...] = ckv
        cb = _rms_fwd(ckv[:, :KV_LORA], kvag_ref[...])[0].astype(BF16)
        kpe_ref[...] = _rope(ckv[:, KV_LORA:], cos, sin).astype(BF16)
        for hd in range(B_HEADS):
            kv = _dot(cb, wkvb_ref[hd])
            kn_ref[hd] = kv[:, :QK_NOPE].astype(BF16)
            v_ref[hd] = kv[:, QK_NOPE:].astype(BF16)
        cqpre = _dot((xhat * mixg_ref[...]).astype(BF16), wqa_ref[...])
        cqpre_ref[...] = cqpre
        cqb = _rms_fwd(cqpre, qg_ref[...])[0].astype(BF16)
        for hd in range(B_HEADS):
            q = _dot(cqb, wqb_ref[hd])
            q_ref[hd, :, 0:QK_NOPE] = q[:, :QK_NOPE].astype(BF16)
            q_ref[hd, :, QK_NOPE:] = _rope(q[:, QK_NOPE:], cos, sin).astype(BF16)

    return _call(
        "kvq_fwd", body, (t // TM,), [_row(D_MODEL), _row(1), _res((1, half)), *KVQ_W_SPECS],
        [_row(KV_LORA + QK_ROPE), _heads(QK_NOPE), _heads(V_HEAD), _row(QK_ROPE), _row(Q_LORA),
         _heads(QK_NOPE + QK_ROPE), _row(half), _row(half)],
        [_sds((t, KV_LORA + QK_ROPE), F32), _sds((B_HEADS, t, QK_NOPE), BF16), _sds((B_HEADS, t, V_HEAD), BF16),
         _sds((t, QK_ROPE), BF16), _sds((t, Q_LORA), F32), _sds((B_HEADS, t, QK_NOPE + QK_ROPE), BF16),
         _sds((t, half), F32), _sds((t, half), F32)],
        (h, pos, inv_freq, *kvq_w))[0]


def _softmax_rows(qn, qp, kn_ref, kpe_ref, k):
    past, upto = k * TM, (k + 1) * TM
    s = (_dot_nt(qn, kn_ref[0:upto, :]) + _dot_nt(qp, kpe_ref[0:upto, :])) * ATT_SCALE
    own = jnp.where(_att_mask(0, TM, TM), s[:, past:], jnp.finfo(F32).min)
    s = own if k == 0 else jnp.concatenate([s[:, :past], own], axis=1)
    e = jnp.exp(s - jnp.max(s, axis=-1, keepdims=True))
    return e * (1.0 / jnp.sum(e, axis=-1, keepdims=True))


def _for_my_tile(i, nq, fn):
    for k in range(nq):
        @pl.when(i == k)
        def _(k=k):
            fn(k)


def _attn_fwd(h, q, kn, kpe, v, w_o, comm=None):
    t = h.shape[0]
    nq = t // TM

    def body(h_ref, q_ref, kn_ref, kpe_ref, v_ref, wo_ref, o_ref, att_ref):
        i, hd = pl.program_id(0), pl.program_id(1)

        @pl.when(hd == 0)
        def _():
            o_ref[...] = h_ref[...]

        def tile(k):
            p = _softmax_rows(q_ref[:, 0:QK_NOPE], q_ref[:, QK_NOPE:], kn_ref.at[hd], kpe_ref, k)
            ob = _dot(p.astype(BF16), v_ref[hd, 0:(k + 1) * TM, :]).astype(BF16)
            att_ref[...] = ob
            o_ref[...] += _dot(ob, wo_ref[hd])

        _for_my_tile(i, nq, tile)

    def per_head(d):
        return pl.BlockSpec((None, TM, d), lambda i, hd: (hd, i, 0))

    def resident(shape):
        zeros = (0,) * len(shape)
        return pl.BlockSpec(shape, lambda i, hd: zeros, pipeline_mode=pl.Buffered(1))

    tile_spec = pl.BlockSpec((TM, D_MODEL), lambda i, hd: (i, 0))
    return _call(
        "attn_fwd", body, (nq, B_HEADS),
        [tile_spec, per_head(QK_NOPE + QK_ROPE), resident((B_HEADS, t, QK_NOPE)), resident((t, QK_ROPE)),
         resident((B_HEADS, t, V_HEAD)), resident((B_HEADS, V_HEAD, D_MODEL))],
        [tile_spec, per_head(V_HEAD)], [_sds((t, D_MODEL), F32), _sds((B_HEADS, t, V_HEAD), BF16)],
        (h, q, kn, kpe, v, w_o), comm=comm)


def _loss_head(h, g, target):
    t = h.shape[0]

    def body(h_ref, g_ref, t_ref, loss_ref, dh_ref, dg_ref):
        y, xhat, rstd = _rms_fwd(h_ref[...], g_ref[...])
        err = y - t_ref[...]
        part = 0.5 * jnp.sum(jnp.mean(err * err, axis=-1, keepdims=True), axis=0, keepdims=True)
        dx, dg = _rms_bwd(err * (1.0 / D_MODEL), xhat, rstd, g_ref[...])
        dh_ref[...] = dx
        _acc(dg_ref, dg)
        _acc(loss_ref, part)

    return _call(
        "loss_head", body, (t // TM,), [_row(D_MODEL), _res((1, D_MODEL)), _row(D_MODEL)],
        [_const((1, 1)), _row(D_MODEL), _const((1, D_MODEL))],
        [_sds((1, 1), F32), _sds((t, D_MODEL), F32), _sds((1, D_MODEL), F32)], (h, g, target))[0]


def _mlp_bwd(h, a, dho, g, w1, w2, layer, comm=None):
    t = h.shape[0]

    def body(h_ref, a_ref, dho_ref, g_ref, w1_ref, w2_ref, dhi_ref, dg_ref, hn_ref, f_ref, da_ref, dhib_ref):
        gv = g_ref[...]
        y, xhat, rstd = _rms_fwd(h_ref[...], gv)
        hn_ref[...] = y.astype(BF16)
        dho_v = dho_ref[...]
        dhob = dho_v.astype(BF16)
        dhn = jnp.zeros((TM, D_MODEL), F32)
        for d in range(N_DEV):
            cs = slice(d * FF_SLOT, (d + 1) * FF_SLOT)
            r = jnp.maximum(a_ref[:, cs], 0.0)
            f_ref[:, cs] = (r * r).astype(BF16)
            da = (_dot_nt(dhob, w2_ref[d]) * (2.0 * r)).astype(BF16)
            da_ref[:, cs] = da
            dhn = dhn + _dot_nt(da, w1_ref[d])
        dx, dg = _rms_bwd(dhn, xhat, rstd, gv)
        dhi = dho_v + dx
        dhi_ref[...] = dhi
        dhib_ref[...] = dhi.astype(BF16)
        _acc(dg_ref, dg)

    return _call(
        f"mlp_bwd_{layer}", body, (t // TM,),
        [_row(D_MODEL), _row(D_FF), _row(D_MODEL), _res((1, D_MODEL)), *MLP_W_SPECS],
        [_row(D_MODEL), _const((1, D_MODEL)), _row(D_MODEL), _row(D_FF), _row(D_FF), _row(D_MODEL)],
        [_sds((t, D_MODEL), F32), _sds((1, D_MODEL), F32), _sds((t, D_MODEL), BF16), _sds((t, D_FF), BF16),
         _sds((t, D_FF), BF16), _sds((t, D_MODEL), BF16)],
        (h, a, dho, g, w1, w2), comm=comm)


def _attn_bwd(dh, q, kn, kpe, v, w_o, cos, sin, comm=None):
    t = dh.shape[0]
    half = QK_ROPE // 2

    def body(dh_ref, q_ref, kn_ref, kpe_ref, v_ref, wo_ref, cos_ref, sin_ref, dq_ref, dkn_ref, dv_ref, dkpe_ref):
        hd, i = pl.program_id(0), pl.program_id(1)

        @pl.when(i == 0)
        def _():
            dkn_ref[...] = jnp.zeros_like(dkn_ref)
            dv_ref[...] = jnp.zeros_like(dv_ref)

        @pl.when((i == 0) & (hd == 0))
        def _():
            dkpe_ref[...] = jnp.zeros_like(dkpe_ref)

        def tile(k):
            keys = slice(0, (k + 1) * TM)
            qn, qp = q_ref[:, 0:QK_NOPE], q_ref[:, QK_NOPE:]
            do = _dot_nt(dh_ref[k * TM:(k + 1) * TM, :], wo_ref[...]).astype(BF16)
            p = _softmax_rows(qn, qp, kn_ref, kpe_ref, k)
            dp = _dot_nt(do, v_ref[keys, :])
            ds = (p * (dp - jnp.sum(p * dp, axis=-1, keepdims=True)) * ATT_SCALE).astype(BF16)
            dq_ref[:, 0:QK_NOPE] = _dot(ds, kn_ref[keys, :]).astype(BF16)
            dq_ref[:, QK_NOPE:] = _rope(_dot(ds, kpe_ref[keys, :]), cos_ref[...], -sin_ref[...]).astype(BF16)
            dkn_ref[keys, :] += _dot_tn(ds, qn)
            dv_ref[keys, :] += _dot_tn(p.astype(BF16), do)
            dkpe_ref[keys, :] += _dot_tn(ds, qp)

        _for_my_tile(i, t // TM, tile)

    def per_head(rows, d, tiled):
        return pl.BlockSpec((None, rows, d), (lambda hd, i: (hd, i, 0)) if tiled else (lambda hd, i: (hd, 0, 0)))

    def tile(d):
        return pl.BlockSpec((TM, d), lambda hd, i: (i, 0))

    return _call(
        "attn_bwd", body, (B_HEADS, t // TM),
        [pl.BlockSpec((t, D_MODEL), lambda hd, i: (0, 0), pipeline_mode=pl.Buffered(1)),
         per_head(TM, QK_NOPE + QK_ROPE, True), per_head(t, QK_NOPE, False),
         pl.BlockSpec((t, QK_ROPE), lambda hd, i: (0, 0)), per_head(t, V_HEAD, False),
         per_head(V_HEAD, D_MODEL, False), tile(half), tile(half)],
        [per_head(TM, QK_NOPE + QK_ROPE, True), per_head(t, QK_NOPE, False), per_head(t, V_HEAD, False),
         pl.BlockSpec((t, QK_ROPE), lambda hd, i: (0, 0))],
        [_sds((B_HEADS, t, QK_NOPE + QK_ROPE), BF16), _sds((B_HEADS, t, QK_NOPE), F32),
         _sds((B_HEADS, t, V_HEAD), F32), _sds((t, QK_ROPE), F32)],
        (dh, q, kn, kpe, v, w_o, cos, sin), comm=comm)


def _kvq_bwd(h, dh, ckv, cqpre, dq, dkn, dv, dkpe, cos, sin, kvq_w):
    t = h.shape[0]
    half = QK_ROPE // 2

    def body(h_ref, dh_ref, ckv_ref, cqpre_ref, dq_ref, dkn_ref, dv_ref, dkpe_ref, cos_ref, sin_ref,
             srcg_ref, wkva_ref, kvag_ref, wkvb_ref, mixg_ref, wqa_ref, qg_ref, wqb_ref,
             dhi_ref, hq_ref, hk_ref, cq_ref, dcqpre_ref, c_ref, dkv_ref, dckv_ref,
             dmixg_ref, dsrcg_ref, dqg_ref, dkvag_ref):
        hv = h_ref[...]
        rstd = lax.rsqrt(jnp.mean(hv * hv, axis=-1, keepdims=True) + EPS)
        xhat = hv * rstd
        mixg, srcg, qg, kvag = mixg_ref[...], srcg_ref[...], qg_ref[...], kvag_ref[...]
        hq_ref[...] = (xhat * mixg).astype(BF16)
        hk_ref[...] = (xhat * srcg).astype(BF16)
        cq, cqhat, crstd = _rms_fwd(cqpre_ref[...], qg)
        cq_ref[...] = cq.astype(BF16)
        dcq = jnp.zeros((TM, Q_LORA), F32)
        for hd in range(B_HEADS):
            dcq = dcq + _dot_nt(dq_ref[hd], wqb_ref[hd])
        dcqpre, dqg = _rms_bwd(dcq, cqhat, crstd, qg)
        dcqpre_b = dcqpre.astype(BF16)
        dcqpre_ref[...] = dcqpre_b
        dxq, dmixg = _rms_bwd(_dot_nt(dcqpre_b, wqa_ref[...]), xhat, rstd, mixg)
        ckv = ckv_ref[...]
        c, chat, krstd = _rms_fwd(ckv[:, :KV_LORA], kvag)
        c_ref[...] = c.astype(BF16)
        dc = jnp.zeros((TM, KV_LORA), F32)
        for hd in range(B_HEADS):
            dkv = jnp.concatenate([dkn_ref[hd], dv_ref[hd]], axis=-1).astype(BF16)
            dkv_ref[hd] = dkv
            dc = dc + _dot_nt(dkv, wkvb_ref[hd])
        dlat, dkvag = _rms_bwd(dc, chat, krstd, kvag)
        dpe = _rope(dkpe_ref[...], cos_ref[...], -sin_ref[...])
        dckv_b = jnp.concatenate([dlat, dpe], axis=-1).astype(BF16)
        dckv_ref[...] = dckv_b
        dxk, dsrcg = _rms_bwd(_dot_nt(dckv_b, wkva_ref[...]), xhat, rstd, srcg)
        dhi_ref[...] = dh_ref[...] + dxq + dxk
        _acc(dmixg_ref, dmixg)
        _acc(dsrcg_ref, dsrcg)
        _acc(dqg_ref, dqg)
        _acc(dkvag_ref, dkvag)

    return _call(
        "kvq_bwd", body, (t // TM,),
        [_row(D_MODEL), _row(D_MODEL), _row(KV_LORA + QK_ROPE), _row(Q_LORA), _heads(QK_NOPE + QK_ROPE),
         _heads(QK_NOPE), _heads(V_HEAD), _row(QK_ROPE), _row(half), _row(half), *KVQ_W_SPECS],
        [_row(D_MODEL), _row(D_MODEL), _row(D_MODEL), _row(Q_LORA), _row(Q_LORA), _row(KV_LORA),
         _heads(QK_NOPE + V_HEAD), _row(KV_LORA + QK_ROPE),
         _const((1, D_MODEL)), _const((1, D_MODEL)), _const((1, Q_LORA)), _const((1, KV_LORA))],
        [_sds((t, D_MODEL), F32), _sds((t, D_MODEL), BF16), _sds((t, D_MODEL), BF16), _sds((t, Q_LORA), BF16),
         _sds((t, Q_LORA), BF16), _sds((t, KV_LORA), BF16), _sds((B_HEADS, t, QK_NOPE + V_HEAD), BF16),
         _sds((t, KV_LORA + QK_ROPE), BF16),
         _sds((1, D_MODEL), F32), _sds((1, D_MODEL), F32), _sds((1, Q_LORA), F32), _sds((1, KV_LORA), F32)],
        (h, dh, ckv, cqpre, dq, dkn, dv, dkpe, cos, sin, *kvq_w))[0]


def _a_mix_bwd(x, z, dh, g, w_in, ln_g, ln_b, w_s, b_st, w_out, comm=None):
    t = x.shape[0]
    tm = TM_GATE
    nblk = tm // GMLP_BLOCK

    def body(x_ref, z_ref, dh_ref, g_ref, win_ref, lng_ref, lnb_ref, ws_ref, bst_ref, wout_ref,
             dx_ref, hn_ref, dz_ref, dg_ref, dlng_ref, dlnb_ref, dws_ref, dbs_ref, dvn_scr, gelu_grad_v):
        @pl.when(pl.program_id(0) == 0)
        def _():
            dws_ref[...] = jnp.zeros_like(dws_ref)
            dbs_ref[...] = jnp.zeros_like(dbs_ref)

        gv, lng = g_ref[...], lng_ref[...]
        y, xhat, rstd = _rms_fwd(x_ref[...], gv)
        hn_ref[...] = y.astype(BF16)
        dhv = dh_ref[...]
        dgated = _dot_nt(dhv.astype(BF16), wout_ref[...])
        u, gelu_grad_u = _gelu_and_grad(z_ref[:, :GATE_DIM])
        v, gelu_grad_v[...] = _gelu_and_grad(z_ref[:, GATE_DIM:])
        vn, vhat, lrstd = _ln_fwd(v, lng, lnb_ref[...])
        vb = vn.astype(BF16)
        mask = _gate_mask()
        for gi in range(A_GROUPS):
            wm = jnp.where(mask, ws_ref[gi], 0.0).astype(BF16)
            bias = bst_ref[:, gi:gi + 1]
            cs = slice(gi * A_GROUP_DIM, (gi + 1) * A_GROUP_DIM)
            dws = jnp.zeros((GMLP_BLOCK, GMLP_BLOCK), F32)
            dbs = jnp.zeros((GMLP_BLOCK, 1), F32)
            for n in range(nblk):
                rs = slice(n * GMLP_BLOCK, (n + 1) * GMLP_BLOCK)
                sv = _dot(wm, vb[rs, cs]) + bias
                dz_ref[rs, cs] = (dgated[rs, cs] * sv * gelu_grad_u[rs, cs]).astype(BF16)
                dsv = dgated[rs, cs] * u[rs, cs]
                dsvb = dsv.astype(BF16)
                dws = dws + _dot_nt(dsvb, vb[rs, cs])
                dbs = dbs + jnp.sum(dsv, axis=-1, keepdims=True)
                dvn_scr[rs, cs] = _dot_tn(wm, dsvb)
            dws_ref[gi] += jnp.where(mask, dws, 0.0)
            dbs_ref[gi] += dbs
        dvn = dvn_scr[...]
        dvhat = dvn * lng
        dv = lrstd * (dvhat - jnp.mean(dvhat, axis=-1, keepdims=True)
                      - vhat * jnp.mean(dvhat * vhat, axis=-1, keepdims=True))
        dz_ref[:, GATE_DIM:] = (dv * gelu_grad_v[...]).astype(BF16)
        dhn = jnp.zeros((tm, D_MODEL), F32)
        for d in range(N_DEV):
            dhn = dhn + _dot_nt(dz_ref[:, d * FF_SLOT:(d + 1) * FF_SLOT], win_ref[d])
        dx, dg = _rms_bwd(dhn, xhat, rstd, gv)
        dx_ref[...] = dhv + dx
        _acc(dg_ref, dg)
        _acc(dlng_ref, jnp.sum(dvn * vhat, axis=0, keepdims=True))
        _acc(dlnb_ref, jnp.sum(dvn, axis=0, keepdims=True))

    return _call(
        "a_mix_bwd", body, (t // tm,),
        [_row(D_MODEL, tm), _row(2 * GATE_DIM, tm), _row(D_MODEL, tm), _res((1, D_MODEL)),
         _res((N_DEV, D_MODEL, FF_SLOT)), _res((1, GATE_DIM)), _res((1, GATE_DIM)),
         _res((A_GROUPS, GMLP_BLOCK, GMLP_BLOCK)), _res((GMLP_BLOCK, A_GROUPS)), _res((GATE_DIM, D_MODEL))],
        [_row(D_MODEL, tm), _row(D_MODEL, tm), _row(2 * GATE_DIM, tm),
         _const((1, D_MODEL)), _const((1, GATE_DIM)), _const((1, GATE_DIM)),
         _const((A_GROUPS, GMLP_BLOCK, GMLP_BLOCK)), _const((A_GROUPS, GMLP_BLOCK, 1))],
        [_sds((t, D_MODEL), F32), _sds((t, D_MODEL), BF16),
         _sds((t, 2 * GATE_DIM), BF16), _sds((1, D_MODEL), F32), _sds((1, GATE_DIM), F32),
         _sds((1, GATE_DIM), F32), _sds((A_GROUPS, GMLP_BLOCK, GMLP_BLOCK), F32),
         _sds((A_GROUPS, GMLP_BLOCK, 1), F32)],
        (x, z, dh, g, w_in, ln_g, ln_b, w_s, b_st, w_out),
        scratch=[pltpu.VMEM((tm, GATE_DIM), F32), pltpu.VMEM((tm, GATE_DIM), F32)], comm=comm)


def _wgrad(name, a, b, a_spec, b_spec, m, n, comm=None):
    def body(a_ref, b_ref, o_ref):
        o_ref[0] = _dot_tn(a_ref[...].astype(BF16), b_ref[...].astype(BF16)).astype(BF16)

    outs, got = _call(name, body, (N_DEV,), [a_spec, b_spec], [pl.BlockSpec((1, m, n), lambda d: (d, 0, 0))],
                      [_sds((N_DEV, m, n), BF16)], (a, b), comm=comm)
    return outs[0] if comm is None else (outs[0], got)


def _full(t, d):
    return pl.BlockSpec((t, d), lambda i: (0, 0), pipeline_mode=pl.Buffered(1))


def _cols(t, d):
    return pl.BlockSpec((t, d), lambda i: (0, i))


def _head(t, d):
    return pl.BlockSpec((None, t, d), lambda i: (i, 0, 0))


def _local_step(x, pos, target, inv_freq, wg, sm, shards=None):
    t = x.shape[0]
    wg = dict(wg)
    dist = shards is not None
    mix_g = [sm["norm_mix_g"][l:l + 1] for l in range(2)]
    mlp_g = [sm["norm_mlp_g"][l:l + 1] for l in range(2)]

    def gather(names):
        return _gather_comm([shards[k] for k in names]) if dist else None

    def send(grads):
        return _exchange_comm(grads=grads) if dist else None

    def send_sums(name, grads):
        return _chip_exchange_comm(_pair_reduce(name, grads)) if dist else None

    def a_args():
        return (wg["a_w_in"], wg["a_ln_v_g"], wg["a_ln_v_b"], sm["a_w_s"], sm["a_b_st"], wg["a_w_out"])

    def kvq_w():
        return (sm["kv_src_norm_g"], wg["kv_w_a"], sm["kv_a_norm_g"], wg["kv_w_b"], mix_g[1], wg["b_w_q_a"],
                sm["b_q_norm_g"], wg["b_w_q_b"])

    names = ("mlp_w1_0", "mlp_w2_0")
    (h1, z, gated), got = _a_mix_fwd(x, mix_g[0], *a_args(), comm=gather(names))
    wg.update(zip(names, got))
    names = ("kv_w_a", "kv_w_b", "b_w_q_a", "b_w_q_b", "b_w_o")
    (h2, a0), got = _mlp_fwd(h1, mlp_g[0], wg["mlp_w1_0"], wg["mlp_w2_0"], 0, comm=gather(names))
    wg.update(zip(names, got))
    if dist:
        wg["b_w_q_a"] = wg["b_w_q_a"].reshape(D_MODEL, Q_LORA)
        wg["kv_w_a"] = wg["kv_w_a"].reshape(D_MODEL, KV_LORA + QK_ROPE)
    ckv, kn, v, kpe, cqpre, q, cos, sin = _kvq_fwd(h2, pos, inv_freq, kvq_w())
    names = ("mlp_w1_1", "mlp_w2_1")
    (h3, att), got = _attn_fwd(h2, q, kn, kpe, v, wg["b_w_o"], comm=gather(names))
    wg.update(zip(names, got))
    (h4, a1), _ = _mlp_fwd(h3, mlp_g[1], wg["mlp_w1_1"], wg["mlp_w2_1"], 1)
    loss, dh4, d_final_g = _loss_head(h4, sm["final_norm_g"], target)

    g = {}
    (dh3, d_mlp_g1, hn, f, da, dh3_b), _ = _mlp_bwd(h3, a1, dh4, mlp_g[1], wg["mlp_w1_1"], wg["mlp_w2_1"], 1)
    g["mlp_w1_1"] = _wgrad("wgrad_w1_1", hn, da, _full(t, D_MODEL), _cols(t, FF_SLOT), D_MODEL, FF_SLOT)
    g["mlp_w2_1"] = _wgrad("wgrad_w2_1", f, dh4, _cols(t, FF_SLOT), _full(t, D_MODEL), FF_SLOT, D_MODEL)
    g["b_w_o"] = _wgrad("wgrad_w_o", att, dh3_b, _head(t, V_HEAD), _full(t, D_MODEL), V_HEAD, D_MODEL)
    names = ("mlp_w1_1", "mlp_w2_1", "b_w_o")
    (dq, dkn, dv, dkpe), got = _attn_bwd(dh3_b, q, kn, kpe, v, wg["b_w_o"], cos, sin,
                                         comm=send_sums("pair_reduce_1", [g[k] for k in names]))
    g.update(zip(names, got))
    (dh2, hq, hk, cq, dcqpre, c, dkv, dckv, d_mix_g1, d_src_g, d_q_g, d_kv_a_g) = _kvq_bwd(
        h2, dh3, ckv, cqpre, dq, dkn, dv, dkpe, cos, sin, kvq_w())
    g["b_w_q_a"] = _wgrad("wgrad_w_q_a", hq, dcqpre, _cols(t, D_MODEL // N_DEV), _full(t, Q_LORA),
                          D_MODEL // N_DEV, Q_LORA)
    g["b_w_q_b"] = _wgrad("wgrad_w_q_b", cq, dq, _full(t, Q_LORA), _head(t, QK_NOPE + QK_ROPE),
                          Q_LORA, QK_NOPE + QK_ROPE)
    g["kv_w_a"] = _wgrad("wgrad_kv_w_a", hk, dckv, _cols(t, D_MODEL // N_DEV), _full(t, KV_LORA + QK_ROPE),
                         D_MODEL // N_DEV, KV_LORA + QK_ROPE)
    g["kv_w_b"] = _wgrad("wgrad_kv_w_b", c, dkv, _full(t, KV_LORA), _head(t, QK_NOPE + V_HEAD),
                         KV_LORA, QK_NOPE + V_HEAD)
    names = ("b_w_q_a", "b_w_q_b", "kv_w_a", "kv_w_b")
    (dh1, d_mlp_g0, hn, f, da, dh1_b), got = _mlp_bwd(h1, a0, dh2, mlp_g[0], wg["mlp_w1_0"], wg["mlp_w2_0"], 0,
                                                      comm=send([g[k] for k in names]))
    g.update(zip(names, got))
    g["mlp_w1_0"] = _wgrad("wgrad_w1_0", hn, da, _full(t, D_MODEL), _cols(t, FF_SLOT), D_MODEL, FF_SLOT)
    g["mlp_w2_0"] = _wgrad("wgrad_w2_0", f, dh2, _cols(t, FF_SLOT), _full(t, D_MODEL), FF_SLOT, D_MODEL)
    g["a_w_out"] = _wgrad("wgrad_a_w_out", gated, dh1_b, _cols(t, GATE_DIM // N_DEV), _full(t, D_MODEL),
                          GATE_DIM // N_DEV, D_MODEL)
    names = ("mlp_w1_0", "mlp_w2_0", "a_w_out")
    (dx, hn, dz, d_mix_g0, d_ln_g, d_ln_b, d_ws, d_bs), got = _a_mix_bwd(
        x, z, dh1, mix_g[0], *a_args(), comm=send_sums("pair_reduce_0", [g[k] for k in names]))
    g.update(zip(names, got))
    small = {
        "norm_mix_g": jnp.concatenate([d_mix_g0, d_mix_g1], axis=0),
        "norm_mlp_g": jnp.concatenate([d_mlp_g0, d_mlp_g1], axis=0),
        "a_ln_v_g": d_ln_g.reshape(N_DEV, GATE_DIM // N_DEV),
        "a_ln_v_b": d_ln_b.reshape(N_DEV, GATE_DIM // N_DEV),
        "a_w_s": d_ws.astype(BF16) if dist else d_ws,
        "a_b_s": d_bs.reshape(A_GROUPS, GMLP_BLOCK),
        "b_q_norm_g": d_q_g,
        "kv_src_norm_g": d_src_g,
        "kv_a_norm_g": d_kv_a_g,
        "final_norm_g": d_final_g,
    }
    wgrad_in = ("wgrad_a_w_in", hn, dz, _full(t, D_MODEL), _cols(t, FF_SLOT), D_MODEL, FF_SLOT)
    if dist:
        parts = [small[k].reshape((1,) + small[k].shape) for k in SMALL] + [loss.reshape(1, 1, 1)]
        g["a_w_in"], got = _wgrad(*wgrad_in, comm=_exchange_comm(parts=parts))
        small, loss = dict(zip(SMALL, got)), got[-1]
    else:
        g["a_w_in"] = _wgrad(*wgrad_in)
    return loss, dx, g, small


def _adamw(w, g, m, v):
    m = ADAM_B1 * m + (1.0 - ADAM_B1) * g
    v = ADAM_B2 * v + (1.0 - ADAM_B2) * (g * g)
    m_hat = m / (1.0 - ADAM_B1 ** ADAM_STEP)
    v_hat = v / (1.0 - ADAM_B2 ** ADAM_STEP)
    return -ADAM_LR * (m_hat / (jnp.sqrt(v_hat) + ADAM_EPS) + ADAM_WD * w), m, v


def _sum_in_device_order(r_ref):
    g = r_ref[0].astype(F32)
    for j in range(1, r_ref.shape[0]):
        g = g + r_ref[j].astype(F32)
    return g


def _adamw_sharded(name, recvs, w, m, v, comm=None):
    layers, r, c = w.shape
    tr = math.gcd(r, 256)
    flat = [a for per_layer in recvs for a in per_layer]

    def body(*refs):
        r_refs, (w_ref, m_ref, v_ref) = refs[:len(flat)], refs[len(flat):len(flat) + 3]
        g_ref, d_ref, nm_ref, nv_ref = refs[-4:]
        layer = pl.program_id(0)
        g, pos = None, 0
        for li, per_layer in enumerate(recvs):
            total = None
            for ref in r_refs[pos:pos + len(per_layer)]:
                part = _sum_in_device_order(ref)
                total = part if total is None else total + part
            pos += len(per_layer)
            g = total if g is None else jnp.where(layer == li, total, g)
        g_ref[...] = g
        d_ref[...], nm_ref[...], nv_ref[...] = _adamw(w_ref[...], g, m_ref[...], v_ref[...])

    blk = pl.BlockSpec((None, tr, c), lambda l, i: (l, i, 0))
    return _call(name, body, (layers, r // tr),
                 [pl.BlockSpec((a.shape[0], tr, c), lambda l, i: (0, i, 0)) for a in flat] + [blk] * 3,
                 [blk] * 4, [_sds(w.shape, F32)] * 4, (*flat, w, m, v), comm=comm)


def _adamw_small(recvs, ws, ms, vs, own_row, losses):
    n = len(recvs)

    def body(*refs):
        r_refs, w_refs, m_refs, v_refs = (refs[i * n:(i + 1) * n] for i in range(4))
        outs, scr = refs[4 * n + 1:8 * n + 2], refs[8 * n + 2:]
        outs[-1][...] = _sum_in_device_order(refs[4 * n])
        me = _my_place()[3]
        for a in range(n):
            g = _sum_in_device_order(r_refs[a])
            if own_row[a]:
                scr[0][...] = g
                g = scr[0][pl.ds(me, 1), :]
            g_ref, d_ref, nm_ref, nv_ref = outs[4 * a:4 * a + 4]
            g_ref[...] = g
            d_ref[...], nm_ref[...], nv_ref[...] = _adamw(w_refs[a][...], g, m_refs[a][...], v_refs[a][...])

    out_shape = []
    for w in ws:
        out_shape += [_sds(w.shape, F32)] * 4
    return pl.pallas_call(
        body, name="adamw_small", in_specs=[VMEM] * (4 * n + 1), out_specs=[VMEM] * (4 * n + 1),
        out_shape=out_shape + [_sds((1, 1), F32)], scratch_shapes=[pltpu.VMEM((N_DEV, GATE_DIM // N_DEV), F32)],
    )(*recvs, *ws, *ms, *vs, losses)


BIG = ("a_w_in", "a_w_out", "b_w_q_a", "b_w_q_b", "b_w_o", "kv_w_a", "kv_w_b", "mlp_w1", "mlp_w2")
SMALL = ("norm_mix_g", "norm_mlp_g", "a_ln_v_g", "a_ln_v_b", "a_w_s", "a_b_s", "b_q_norm_g", "kv_src_norm_g",
         "kv_a_norm_g", "final_norm_g")
WEIGHTS = ("norm_mix_g", "norm_mlp_g", "a_w_in", "a_ln_v_g", "a_ln_v_b", "a_w_s", "a_b_s", "a_w_out", "b_w_q_a",
           "b_q_norm_g", "b_w_q_b", "b_w_o", "kv_src_norm_g", "kv_w_a", "kv_a_norm_g", "kv_w_b", "mlp_w1", "mlp_w2",
           "final_norm_g")


def _two_d(name, a):
    if name in ("a_w_s", "a_b_s"):
        return a.reshape(a.shape[1:])
    return a.reshape(1, -1) if a.ndim == 1 else a


def _three_d(a):
    return a if a.ndim == 3 else a.reshape((1,) + a.shape)


def kernel(x, positions, norm_mix_g, norm_mlp_g, a_w_in, a_ln_v_g, a_ln_v_b, a_w_s, a_b_s, a_w_out, b_w_q_a, b_q_norm_g, b_w_q_b, b_w_o, kv_src_norm_g, kv_w_a, kv_a_norm_g, kv_w_b, mlp_w1, mlp_w2, final_norm_g, loss_target, m_norm_mix_g, m_norm_mlp_g, m_a_w_in, m_a_ln_v_g, m_a_ln_v_b, m_a_w_s, m_a_b_s, m_a_w_out, m_b_w_q_a, m_b_q_norm_g, m_b_w_q_b, m_b_w_o, m_kv_src_norm_g, m_kv_w_a, m_kv_a_norm_g, m_kv_w_b, m_mlp_w1, m_mlp_w2, m_final_norm_g, v_norm_mix_g, v_norm_mlp_g, v_a_w_in, v_a_ln_v_g, v_a_ln_v_b, v_a_w_s, v_a_b_s, v_a_w_out, v_b_w_q_a, v_b_q_norm_g, v_b_w_q_b, v_b_w_o, v_kv_src_norm_g, v_kv_w_a, v_kv_a_norm_g, v_kv_w_b, v_mlp_w1, v_mlp_w2, v_final_norm_g):
    w = dict(norm_mix_g=norm_mix_g, norm_mlp_g=norm_mlp_g, a_w_in=a_w_in, a_ln_v_g=a_ln_v_g, a_ln_v_b=a_ln_v_b,
             a_w_s=a_w_s, a_b_s=a_b_s, a_w_out=a_w_out, b_w_q_a=b_w_q_a, b_q_norm_g=b_q_norm_g, b_w_q_b=b_w_q_b,
             b_w_o=b_w_o, kv_src_norm_g=kv_src_norm_g, kv_w_a=kv_w_a, kv_a_norm_g=kv_a_norm_g, kv_w_b=kv_w_b,
             mlp_w1=mlp_w1, mlp_w2=mlp_w2, final_norm_g=final_norm_g)
    m = dict(norm_mix_g=m_norm_mix_g, norm_mlp_g=m_norm_mlp_g, a_w_in=m_a_w_in, a_ln_v_g=m_a_ln_v_g,
             a_ln_v_b=m_a_ln_v_b, a_w_s=m_a_w_s, a_b_s=m_a_b_s, a_w_out=m_a_w_out, b_w_q_a=m_b_w_q_a,
             b_q_norm_g=m_b_q_norm_g, b_w_q_b=m_b_w_q_b, b_w_o=m_b_w_o, kv_src_norm_g=m_kv_src_norm_g,
             kv_w_a=m_kv_w_a, kv_a_norm_g=m_kv_a_norm_g, kv_w_b=m_kv_w_b, mlp_w1=m_mlp_w1, mlp_w2=m_mlp_w2,
             final_norm_g=m_final_norm_g)
    v = dict(norm_mix_g=v_norm_mix_g, norm_mlp_g=v_norm_mlp_g, a_w_in=v_a_w_in, a_ln_v_g=v_a_ln_v_g,
             a_ln_v_b=v_a_ln_v_b, a_w_s=v_a_w_s, a_b_s=v_a_b_s, a_w_out=v_a_w_out, b_w_q_a=v_b_w_q_a,
             b_q_norm_g=v_b_q_norm_g, b_w_q_b=v_b_w_q_b, b_w_o=v_b_w_o, kv_src_norm_g=v_kv_src_norm_g,
             kv_w_a=v_kv_w_a, kv_a_norm_g=v_kv_a_norm_g, kv_w_b=v_kv_w_b, mlp_w1=v_mlp_w1, mlp_w2=v_mlp_w2,
             final_norm_g=v_final_norm_g)
    t = x.shape[1]

    first = ("a_w_in", "a_w_out", "a_ln_v_g", "a_ln_v_b")
    later = ("mlp_w1_0", "mlp_w2_0", "mlp_w1_1", "mlp_w2_1", "kv_w_a", "kv_w_b", "b_w_q_a", "b_w_q_b", "b_w_o")
    blocks = {k: _three_d(w[k]) for k in BIG if not k.startswith("mlp")}
    for k in ("mlp_w1", "mlp_w2"):
        blocks[k + "_0"], blocks[k + "_1"] = w[k][0:1], w[k][1:2]
    got, casts = _gather_first([blocks[k] if k in blocks else w[k] for k in first], [blocks[k] for k in later])
    wg = dict(zip(first, got))
    wg["a_w_out"] = wg["a_w_out"].reshape(GATE_DIM, D_MODEL)
    wg["a_ln_v_g"] = wg["a_ln_v_g"].reshape(1, GATE_DIM)
    wg["a_ln_v_b"] = wg["a_ln_v_b"].reshape(1, GATE_DIM)
    shards = dict(zip(later, casts))

    sm = {k: _two_d(k, w[k]) for k in SMALL if k not in ("a_ln_v_g", "a_ln_v_b")}
    sm["a_b_st"] = sm["a_b_s"].T
    inv_freq = (ROPE_THETA ** (-jnp.arange(0, QK_ROPE, 2, dtype=F32) / QK_ROPE)).reshape(1, QK_ROPE // 2)

    losses, dx, g, small = _local_step(x[0], positions.reshape(t, 1), loss_target[0], inv_freq, wg, sm, shards)

    g["a_w_in"], = _comm_only("exchange_last", _chip_exchange_comm(_pair_reduce("pair_reduce_a", [g["a_w_in"]])))

    out = {}
    for k in BIG:
        recvs = [[g[k + "_0"]], [g[k + "_1"]]] if k.startswith("mlp") else [[g[k]]]
        res, _ = _adamw_sharded("adamw_" + k, recvs, _three_d(w[k]), _three_d(m[k]), _three_d(v[k]))
        out[k] = [o.reshape(w[k].shape) for o in res]
    own_row = [k in ("a_ln_v_g", "a_ln_v_b") for k in SMALL]
    res = _adamw_small([small[k] for k in SMALL], [_two_d(k, w[k]) for k in SMALL], [_two_d(k, m[k]) for k in SMALL],
                       [_two_d(k, v[k]) for k in SMALL], own_row, losses)
    for i, k in enumerate(SMALL):
        out[k] = [o.reshape(w[k].shape) for o in res[4 * i:4 * i + 4]]

    return (res[-1].reshape(()), dx.reshape(x.shape), *[out[k][0] for k in WEIGHTS], *[out[k][1] for k in WEIGHTS],
            *[out[k][2] for k in WEIGHTS], *[out[k][3] for k in WEIGHTS])
```

```python
import math

import jax
import jax.numpy as jnp
from jax import lax
from jax.experimental import pallas as pl
from jax.experimental.pallas import tpu as pltpu

F32, BF16 = jnp.float32, jnp.bfloat16
MESH = pl.DeviceIdType.MESH
ANY = pl.BlockSpec(memory_space=pl.ANY)
VMEM = pl.BlockSpec(memory_space=pltpu.VMEM)

N_DEV = 8
D_MODEL = 1024
CHUNK = 64
GMLP_BLOCK = 128
GATE_DIM = 2048
A_GROUPS = 8
A_GROUP_DIM = GATE_DIM // A_GROUPS
B_HEADS = 8
QK_NOPE, QK_ROPE, V_HEAD = 128, 64, 128
Q_LORA, KV_LORA = 384, 256
ROPE_THETA = 10000.0
D_FF = 4096
FF_SLOT = D_FF // N_DEV
EPS = 1e-6
ATT_SCALE = (QK_NOPE + QK_ROPE) ** -0.5

ADAM_LR, ADAM_B1, ADAM_B2, ADAM_EPS, ADAM_WD, ADAM_STEP = 0.001, 0.9, 0.999, 1e-08, 0.01, 10

TM = 256
TM_GATE = 128
VMEM_LIMIT = 56 * 1024 * 1024
INV_SQRT2 = 1.0 / math.sqrt(2.0)
INV_SQRT_2PI = 1.0 / math.sqrt(2.0 * math.pi)


def _dot(a, b):
    return jnp.dot(a, b, preferred_element_type=F32)


def _dot_nt(a, b):
    return lax.dot_general(a, b, (((1,), (1,)), ((), ())), preferred_element_type=F32)


def _dot_tn(a, b):
    return lax.dot_general(a, b, (((0,), (0,)), ((), ())), preferred_element_type=F32)


def _rms_fwd(x, g):
    rstd = lax.rsqrt(jnp.mean(x * x, axis=-1, keepdims=True) + EPS)
    xhat = x * rstd
    return xhat * g, xhat, rstd


def _rms_bwd(dy, xhat, rstd, g):
    dxhat = dy * g
    dx = rstd * (dxhat - xhat * jnp.mean(dxhat * xhat, axis=-1, keepdims=True))
    return dx, jnp.sum(dy * xhat, axis=0, keepdims=True)


def _ln_fwd(v, g, b):
    mu = jnp.mean(v, axis=-1, keepdims=True)
    vc = v - mu
    rstd = lax.rsqrt(jnp.mean(vc * vc, axis=-1, keepdims=True) + EPS)
    vhat = vc * rstd
    return vhat * g + b, vhat, rstd


def _gelu(x):
    return 0.5 * x * (1.0 + lax.erf(x * INV_SQRT2))


def _gelu_and_grad(x):
    cdf = 0.5 * (1.0 + lax.erf(x * INV_SQRT2))
    return x * cdf, cdf + x * jnp.exp(-0.5 * x * x) * INV_SQRT_2PI


def _rope(x, cos, sin):
    x1, x2 = x[:, :QK_ROPE // 2], x[:, QK_ROPE // 2:]
    return jnp.concatenate([x1 * cos - x2 * sin, x2 * cos + x1 * sin], axis=-1)


def _gate_mask():
    row = lax.broadcasted_iota(jnp.int32, (GMLP_BLOCK, GMLP_BLOCK), 0)
    col = lax.broadcasted_iota(jnp.int32, (GMLP_BLOCK, GMLP_BLOCK), 1)
    return (col < CHUNK) | (row >= CHUNK)


def _att_mask(q0, tq, t):
    q = q0 + lax.broadcasted_iota(jnp.int32, (tq, t), 0)
    k = lax.broadcasted_iota(jnp.int32, (tq, t), 1)
    return jnp.right_shift(k, 6) <= jnp.right_shift(q, 6)


def _res(shape, imap=None):
    zeros = (0,) * len(shape)
    return pl.BlockSpec(shape, imap or (lambda i: zeros), pipeline_mode=pl.Buffered(1))


def _const(shape):
    zeros = (0,) * len(shape)
    return pl.BlockSpec(shape, lambda i: zeros)


def _row(d, tm=TM):
    return pl.BlockSpec((tm, d), lambda i: (i, 0))


def _heads(d):
    return pl.BlockSpec((B_HEADS, TM, d), lambda i: (0, i, 0))


def _sds(shape, dt):
    return jax.ShapeDtypeStruct(shape, dt)


def _acc(ref, val):
    @pl.when(pl.program_id(0) == 0)
    def _():
        ref[...] = jnp.zeros_like(ref)
    ref[...] += val


def _my_place():
    x, y, c = lax.axis_index("x"), lax.axis_index("y"), lax.axis_index("c")
    return x, y, c, 4 * x + 2 * y + c


def _peer(x, y, c, k):
    px = 1 - x if k & 4 else x
    py = 1 - y if k & 2 else y
    pc = 1 - c if k & 1 else c
    return (px, py, pc), 4 * px + 2 * py + pc


CHIPS = (2, 4, 6)


def _splits(ref):
    return len(ref.shape) >= 3 and ref.shape[1] % 32 == 0


def _piece(ref, block, half=None):
    if half is None or not _splits(ref):
        return ref.at[pl.ds(block, 1)]
    rows = ref.shape[1] // 2
    return ref.at[pl.ds(block, 1), pl.ds(half * rows, rows)]


def _gather_copy(sems, a, k, piece, to, src=None):
    return pltpu.make_async_remote_copy(
        src_ref=piece if src is None else src, dst_ref=piece, send_sem=sems[0].at[a, k], recv_sem=sems[1].at[a, k],
        device_id=to, device_id_type=MESH)


def _gather_start(srcs, outs, sems, only=None):
    x, y, c, me = _my_place()
    for a in range(len(srcs)) if only is None else (only,):
        mine = _piece(outs[a], me)
        pltpu.make_async_copy(srcs[a], mine, sems[2].at[a]).start()
        for k, rel in enumerate((1, 4, 2)):
            _gather_copy(sems, a, k, mine, _peer(x, y, c, rel)[0], src=srcs[a]).start()


def _gather_relay(srcs, outs, sems):
    x, y, c, _ = _my_place()
    sib = _peer(x, y, c, 1)[0]
    (xn, xn_i), (yn, yn_i) = _peer(x, y, c, 4), _peer(x, y, c, 2)
    for a in range(len(srcs)):
        out = outs[a]
        _gather_copy(sems, a, 1, _piece(out, xn_i), xn).wait_recv()
        _gather_copy(sems, a, 3, _piece(out, xn_i, 0), yn).start()
        _gather_copy(sems, a, 5, _piece(out, xn_i), sib).start()
        _gather_copy(sems, a, 2, _piece(out, yn_i), yn).wait_recv()
        if _splits(out):
            _gather_copy(sems, a, 4, _piece(out, yn_i, 1), xn).start()
        _gather_copy(sems, a, 6, _piece(out, yn_i), sib).start()


def _gather_finish(srcs, outs, sems):
    x, y, c, me = _my_place()
    sib = _peer(x, y, c, 1)[0]
    xn, yn, dg_i = _peer(x, y, c, 4)[0], _peer(x, y, c, 2)[0], _peer(x, y, c, 6)[1]
    n = len(srcs)
    for a in range(n):
        out = outs[a]
        _gather_copy(sems, a, 3, _piece(out, dg_i, 0), yn).wait_recv()
        _gather_copy(sems, a, 7, _piece(out, dg_i, 0), sib).start()
        if _splits(out):
            _gather_copy(sems, a, 4, _piece(out, dg_i, 1), xn).wait_recv()
            _gather_copy(sems, a, 8, _piece(out, dg_i, 1), sib).start()
    for a in range(n):
        out = outs[a]
        whole, half = _piece(out, me), _piece(out, me, 0)
        for k in (0, 5, 6):
            _gather_copy(sems, a, k, whole, sib).wait_recv()
        for k in (7, 8) if _splits(out) else (7,):
            _gather_copy(sems, a, k, half, sib).wait_recv()
        for k in (0, 1, 2):
            _gather_copy(sems, a, k, whole, sib, src=srcs[a]).wait_send()
        for k in (5, 6):
            _gather_copy(sems, a, k, whole, sib).wait_send()
        for k in (3, 4, 7, 8) if _splits(out) else (3, 7):
            _gather_copy(sems, a, k, half, sib).wait_send()
        pltpu.make_async_copy(srcs[a], whole, sems[2].at[a]).wait()


def _relay_sems(n):
    return [pltpu.SemaphoreType.DMA((n, 9)), pltpu.SemaphoreType.DMA((n, 9)), pltpu.SemaphoreType.DMA((n,))]


def _gather_sems(n):
    return [pltpu.SemaphoreType.DMA((n, 7)), pltpu.SemaphoreType.DMA((n, 7)), pltpu.SemaphoreType.DMA((n,))]


class _Comm:
    def __init__(self, args, out_shape, scratch, start, finish, relay=None):
        self.args, self.out_shape, self.scratch, self.start, self.finish = args, out_shape, scratch, start, finish
        self.relay = relay


def _gather_comm(shards):
    return _Comm(list(shards), [_sds((N_DEV,) + s.shape[1:], s.dtype) for s in shards], _relay_sems(len(shards)),
                 _gather_start, _gather_finish, relay=_gather_relay)


def _direct_copies(ins, outs, sems, wait, from_block):
    send_sems, recv_sems, local_sems = sems
    x, y, c, me = _my_place()
    for a in range(len(ins)):
        src = ins[a].at[pl.ds(me, 1)] if from_block[a] else ins[a]
        local = pltpu.make_async_copy(src, outs[a].at[pl.ds(me, 1)], local_sems.at[a])
        local.wait() if wait else local.start()
        for k in range(1, N_DEV):
            to, to_i = _peer(x, y, c, k)
            cp = pltpu.make_async_remote_copy(
                src_ref=ins[a].at[pl.ds(to_i, 1)] if from_block[a] else ins[a], dst_ref=outs[a].at[pl.ds(me, 1)],
                send_sem=send_sems.at[a, k - 1], recv_sem=recv_sems.at[a, k - 1], device_id=to, device_id_type=MESH)
            cp.wait() if wait else cp.start()


def _exchange_comm(grads=(), parts=()):
    ins = list(grads) + list(parts)
    from_block = [True] * len(grads) + [False] * len(parts)
    out_shape = [_sds(g.shape, g.dtype) for g in grads] + [_sds((N_DEV,) + p.shape[1:], p.dtype) for p in parts]

    def start(ins_, outs_, sems_):
        _direct_copies(ins_, outs_, sems_, False, from_block)

    def finish(ins_, outs_, sems_):
        _direct_copies(ins_, outs_, sems_, True, from_block)

    return _Comm(ins, out_shape, _gather_sems(len(ins)), start, finish)


def _chip_copies(ins, outs, sems, wait, rels, own):
    send_sems, recv_sems, local_sems = sems
    x, y, c, _ = _my_place()
    for a in range(len(ins)):
        if own:
            local = pltpu.make_async_copy(ins[a].at[pl.ds(2 * x + y, 1)], outs[a].at[pl.ds(len(rels), 1)],
                                          local_sems.at[a])
            local.wait() if wait else local.start()
        for i, j in enumerate(rels):
            to = _peer(x, y, c, CHIPS[j])[0]
            cp = pltpu.make_async_remote_copy(
                src_ref=ins[a].at[pl.ds(2 * to[0] + to[1], 1)], dst_ref=outs[a].at[pl.ds(i, 1)],
                send_sem=send_sems.at[a, i], recv_sem=recv_sems.at[a, i], device_id=to, device_id_type=MESH)
            cp.wait() if wait else cp.start()


def _chip_exchange_comm(sums, rels=(0, 1, 2), own=True):
    def start(ins_, outs_, sems_):
        _chip_copies(ins_, outs_, sems_, False, rels, own)

    def finish(ins_, outs_, sems_):
        _chip_copies(ins_, outs_, sems_, True, rels, own)

    n = len(sums)
    sems = [pltpu.SemaphoreType.DMA((n, len(rels))), pltpu.SemaphoreType.DMA((n, len(rels))),
            pltpu.SemaphoreType.DMA((n,))]
    return _Comm(list(sums), [_sds((len(rels) + own,) + s.shape[1:], s.dtype) for s in sums], sems, start, finish)


def _pair_reduce(name, grads):
    n = len(grads)
    n_chips = N_DEV // 2

    def body(*refs):
        g_refs, gh_refs, p_refs, land = refs[:n], refs[n:2 * n], refs[2 * n:3 * n], refs[3 * n:4 * n]
        send_sems, recv_sems = refs[4 * n:]
        x, y, c, _ = _my_place()
        sib = _peer(x, y, c, 1)[0]
        q = pl.program_id(0)

        def to_sibling(a, j):
            return pltpu.make_async_remote_copy(
                src_ref=gh_refs[a].at[j, pl.ds(1 - c, 1)], dst_ref=land[a].at[pl.ds(j, 1)],
                send_sem=send_sems.at[a, j], recv_sem=recv_sems.at[a, j], device_id=sib, device_id_type=MESH)

        @pl.when(q == 0)
        def _():
            for j in range(n_chips):
                for a in range(n):
                    to_sibling(a, j).start()

        for a in range(n):
            to_sibling(a, q).wait_recv()
            p_refs[a][...] = (g_refs[a][0, pl.ds(c, 1)].astype(F32) + land[a][pl.ds(q, 1)].astype(F32)).astype(BF16)

        @pl.when(q == n_chips - 1)
        def _():
            for a in range(n):
                for j in range(n_chips):
                    to_sibling(a, j).wait_send()

    views = [g.reshape((n_chips, 2) + g.shape[1:]) for g in grads]
    res = pl.pallas_call(
        body, name=name, grid=(n_chips,),
        in_specs=[pl.BlockSpec((1, 2) + g.shape[1:], lambda q: (q, 0, 0, 0)) for g in grads] + [ANY] * n,
        out_specs=[pl.BlockSpec((1,) + g.shape[1:], lambda q: (q, 0, 0)) for g in grads],
        out_shape=[_sds((n_chips,) + g.shape[1:], BF16) for g in grads],
        scratch_shapes=[pltpu.VMEM((n_chips,) + g.shape[1:], BF16) for g in grads]
        + [pltpu.SemaphoreType.DMA((n, n_chips)), pltpu.SemaphoreType.DMA((n, n_chips))],
        compiler_params=pltpu.CompilerParams(dimension_semantics=("arbitrary",), vmem_limit_bytes=VMEM_LIMIT),
    )(*views, *views)
    return list(res)


def _call(name, body, grid, in_specs, out_specs, out_shape, args, scratch=(), comm=None):
    params = pltpu.CompilerParams(dimension_semantics=("arbitrary",) * len(grid), vmem_limit_bytes=VMEM_LIMIT)
    if comm is None:
        outs = pl.pallas_call(body, name=name, grid=grid, in_specs=list(in_specs), out_specs=list(out_specs),
                              out_shape=list(out_shape), scratch_shapes=list(scratch), compiler_params=params)(*args)
        return list(outs), []
    ni, nci, no, nco, ns = len(in_specs), len(comm.args), len(out_specs), len(comm.out_shape), len(scratch)

    def carrying(*refs):
        ins, refs = refs[:ni], refs[ni:]
        cin, refs = refs[:nci], refs[nci:]
        outs, refs = refs[:no], refs[no:]
        cout, refs = refs[:nco], refs[nco:]
        scr, csems = refs[:ns], refs[ns:]
        step = pl.program_id(0)
        for ax in range(1, len(grid)):
            step = step * grid[ax] + pl.program_id(ax)
        steps = math.prod(grid)

        @pl.when(step == 0)
        def _():
            comm.start(cin, cout, csems)

        if comm.relay is not None:
            @pl.when(step == (2 * steps) // 3)
            def _():
                comm.relay(cin, cout, csems)

        body(*ins, *outs, *scr)

        @pl.when(step == steps - 1)
        def _():
            comm.finish(cin, cout, csems)

    outs = pl.pallas_call(
        carrying, name=name, grid=grid, in_specs=list(in_specs) + [ANY] * nci, out_specs=list(out_specs) + [ANY] * nco,
        out_shape=list(out_shape) + list(comm.out_shape), scratch_shapes=list(scratch) + list(comm.scratch),
        compiler_params=params)(*args, *comm.args)
    return list(outs[:no]), list(outs[no:])


def _comm_only(name, comm):
    def body(*refs):
        nci, nco = len(comm.args), len(comm.out_shape)
        cin, cout, csems = refs[:nci], refs[nci:nci + nco], refs[nci + nco:]
        comm.start(cin, cout, csems)
        if comm.relay is not None:
            comm.relay(cin, cout, csems)
        comm.finish(cin, cout, csems)

    return pl.pallas_call(body, name=name, in_specs=[ANY] * len(comm.args), out_specs=[ANY] * len(comm.out_shape),
                          out_shape=list(comm.out_shape), scratch_shapes=list(comm.scratch))(*comm.args)


def _gather_first(first, later):
    nf, nl = len(first), len(later)
    dts = [BF16] * (nf - 2) + [F32, F32]

    def body(*refs):
        ins, refs = refs[:nf + nl], refs[nf + nl:]
        outs, refs = refs[:nf], refs[nf:]
        casts, refs = refs[:nl], refs[nl:]
        stage, sems = refs[:nf], refs[nf:]
        for a in range(nf):
            stage[a][...] = ins[a][...].astype(dts[a])
            _gather_start(stage, outs, sems, only=a)
        for a in range(nl):
            casts[a][...] = ins[nf + a][...].astype(BF16)
        _gather_relay(stage, outs, sems)
        _gather_finish(stage, outs, sems)

    res = pl.pallas_call(
        body, name="gather_first",
        in_specs=[VMEM] * (nf + nl), out_specs=[ANY] * nf + [VMEM] * nl,
        out_shape=[_sds((N_DEV,) + s.shape[1:], dt) for s, dt in zip(first, dts)]
        + [_sds(s.shape, BF16) for s in later],
        scratch_shapes=[pltpu.VMEM(s.shape, dt) for s, dt in zip(first, dts)] + _relay_sems(nf),
        compiler_params=pltpu.CompilerParams(vmem_limit_bytes=VMEM_LIMIT),
    )(*first, *later)
    return list(res[:nf]), list(res[nf:])


def _a_mix_fwd(x, g, w_in, ln_g, ln_b, w_s, b_st, w_out, comm=None):
    t = x.shape[0]
    nblk = TM // GMLP_BLOCK

    def body(x_ref, g_ref, win_ref, lng_ref, lnb_ref, ws_ref, bst_ref, wout_ref, h_ref, z_ref, gated_scr):
        xv = x_ref[...]
        hb = _rms_fwd(xv, g_ref[...])[0].astype(BF16)
        for d in range(N_DEV):
            z_ref[:, d * FF_SLOT:(d + 1) * FF_SLOT] = _dot(hb, win_ref[d])
        u = _gelu(z_ref[:, :GATE_DIM])
        vb = _ln_fwd(_gelu(z_ref[:, GATE_DIM:]), lng_ref[...], lnb_ref[...])[0].astype(BF16)
        mask = _gate_mask()
        for gi in range(A_GROUPS):
            wm = jnp.where(mask, ws_ref[gi], 0.0).astype(BF16)
            bias = bst_ref[:, gi:gi + 1]
            cs = slice(gi * A_GROUP_DIM, (gi + 1) * A_GROUP_DIM)
            for n in range(nblk):
                rs = slice(n * GMLP_BLOCK, (n + 1) * GMLP_BLOCK)
                sv = _dot(wm, vb[rs, cs]) + bias
                gated_scr[rs, cs] = (u[rs, cs] * sv).astype(BF16)
        h_ref[...] = xv + _dot(gated_scr[...], wout_ref[...])

    return _call(
        "a_mix_fwd", body, (t // TM,),
        [_row(D_MODEL), _res((1, D_MODEL)), _res((N_DEV, D_MODEL, FF_SLOT)), _res((1, GATE_DIM)),
         _res((1, GATE_DIM)), _res((A_GROUPS, GMLP_BLOCK, GMLP_BLOCK)), _res((GMLP_BLOCK, A_GROUPS)),
         _res((GATE_DIM, D_MODEL))],
        [_row(D_MODEL), _row(2 * GATE_DIM), _row(GATE_DIM)],
        [_sds((t, D_MODEL), F32), _sds((t, 2 * GATE_DIM), F32), _sds((t, GATE_DIM), BF16)],
        (x, g, w_in, ln_g, ln_b, w_s, b_st, w_out), comm=comm)


MLP_W_SPECS = (_res((N_DEV, D_MODEL, FF_SLOT)), _res((N_DEV, FF_SLOT, D_MODEL)))


def _mlp_fwd(h, g, w1, w2, layer, comm=None):
    t = h.shape[0]

    def body(h_ref, g_ref, w1_ref, w2_ref, o_ref, a_ref):
        hv = h_ref[...]
        hb = _rms_fwd(hv, g_ref[...])[0].astype(BF16)
        o_ref[...] = hv
        for d in range(N_DEV):
            a = _dot(hb, w1_ref[d])
            a_ref[:, d * FF_SLOT:(d + 1) * FF_SLOT] = a
            r = jnp.maximum(a, 0.0)
            o_ref[...] += _dot((r * r).astype(BF16), w2_ref[d])

    return _call(
        f"mlp_fwd_{layer}", body, (t // TM,), [_row(D_MODEL), _res((1, D_MODEL)), *MLP_W_SPECS],
        [_row(D_MODEL), _row(D_FF)], [_sds((t, D_MODEL), F32), _sds((t, D_FF), F32)], (h, g, w1, w2), comm=comm)


def _mlp_fwd_loss(h, g, w1, w2, final_g, target):
    t = h.shape[0]

    def body(h_ref, g_ref, w1_ref, w2_ref, fg_ref, t_ref, a_ref, loss_ref, dh_ref, dg_ref):
        hv = h_ref[...]
        hb = _rms_fwd(hv, g_ref[...])[0].astype(BF16)
        out = hv
        for d in range(N_DEV):
            a = _dot(hb, w1_ref[d])
            a_ref[:, d * FF_SLOT:(d + 1) * FF_SLOT] = a
            r = jnp.maximum(a, 0.0)
            out = out + _dot((r * r).astype(BF16), w2_ref[d])
        y, xhat, rstd = _rms_fwd(out, fg_ref[...])
        err = y - t_ref[...]
        part = 0.5 * jnp.sum(jnp.mean(err * err, axis=-1, keepdims=True), axis=0, keepdims=True)
        dx, dg = _rms_bwd(err * (1.0 / D_MODEL), xhat, rstd, fg_ref[...])
        dh_ref[...] = dx
        _acc(dg_ref, dg)
        _acc(loss_ref, part)

    return _call(
        "mlp_fwd_loss", body, (t // TM,),
        [_row(D_MODEL), _res((1, D_MODEL)), *MLP_W_SPECS, _res((1, D_MODEL)), _row(D_MODEL)],
        [_row(D_FF), _const((1, 1)), _row(D_MODEL), _const((1, D_MODEL))],
        [_sds((t, D_FF), F32), _sds((1, 1), F32), _sds((t, D_MODEL), F32), _sds((1, D_MODEL), F32)],
        (h, g, w1, w2, final_g, target))[0]


KVQ_W_SPECS = (_res((1, D_MODEL)), _res((D_MODEL, KV_LORA + QK_ROPE)), _res((1, KV_LORA)),
               _res((B_HEADS, KV_LORA, QK_NOPE + V_HEAD)), _res((1, D_MODEL)), _res((D_MODEL, Q_LORA)),
               _res((1, Q_LORA)), _res((B_HEADS, Q_LORA, QK_NOPE + QK_ROPE)))


def _kvq_fwd(h, pos, inv_freq, kvq_w):
    t = h.shape[0]
    half = QK_ROPE // 2

    def body(h_ref, pos_ref, invf_ref, srcg_ref, wkva_ref, kvag_ref, wkvb_ref, mixg_ref, wqa_ref, qg_ref, wqb_ref,
             ckv_ref, kn_ref, v_ref, kpe_ref, cqpre_ref, q_ref, cos_ref, sin_ref):
        hv = h_ref[...]
        xhat = hv * lax.rsqrt(jnp.mean(hv * hv, axis=-1, keepdims=True) + EPS)
        ang = pos_ref[...].astype(F32) * invf_ref[...]
        cos, sin = jnp.cos(ang), jnp.sin(ang)
        cos_ref[...] = cos
        sin_ref[...] = sin
        ckv = _dot((xhat * srcg_ref[...]).astype(BF16), wkva_ref[...])
        ckv_ref[...] = ckv
        cb = _rms_fwd(ckv[:, :KV_LORA], kvag_ref[...])[0].astype(BF16)
        kpe_ref[...] = _rope(ckv[:, KV_LORA:], cos, sin).astype(BF16)
        for hd in range(B_HEADS):
            kv = _dot(cb, wkvb_ref[hd])
            kn_ref[hd] = kv[:, :QK_NOPE].astype(BF16)
            v_ref[hd] = kv[:, QK_NOPE:].astype(BF16)
        cqpre = _dot((xhat * mixg_ref[...]).astype(BF16), wqa_ref[...])
        cqpre_ref[...] = cqpre
        cqb = _rms_fwd(cqpre, qg_ref[...])[0].astype(BF16)
        for hd in range(B_HEADS):
            q = _dot(cqb, wqb_ref[hd])
            q_ref[hd, :, 0:QK_NOPE] = q[:, :QK_NOPE].astype(BF16)
            q_ref[hd, :, QK_NOPE:] = _rope(q[:, QK_NOPE:], cos, sin).astype(BF16)

    return _call(
        "kvq_fwd", body, (t // TM,), [_row(D_MODEL), _row(1), _res((1, half)), *KVQ_W_SPECS],
        [_row(KV_LORA + QK_ROPE), _heads(QK_NOPE), _heads(V_HEAD), _row(QK_ROPE), _row(Q_LORA),
         _heads(QK_NOPE + QK_ROPE), _row(half), _row(half)],
        [_sds((t, KV_LORA + QK_ROPE), F32), _sds((B_HEADS, t, QK_NOPE), BF16), _sds((B_HEADS, t, V_HEAD), BF16),
         _sds((t, QK_ROPE), BF16), _sds((t, Q_LORA), F32), _sds((B_HEADS, t, QK_NOPE + QK_ROPE), BF16),
         _sds((t, half), F32), _sds((t, half), F32)],
        (h, pos, inv_freq, *kvq_w))[0]


def _softmax_rows(qn, qp, kn_ref, kpe_ref, k):
    past, upto = k * TM, (k + 1) * TM
    s = (_dot_nt(qn, kn_ref[0:upto, :]) + _dot_nt(qp, kpe_ref[0:upto, :])) * ATT_SCALE
    own = jnp.where(_att_mask(0, TM, TM), s[:, past:], jnp.finfo(F32).min)
    s = own if k == 0 else jnp.concatenate([s[:, :past], own], axis=1)
    e = jnp.exp(s - jnp.max(s, axis=-1, keepdims=True))
    return e * (1.0 / jnp.sum(e, axis=-1, keepdims=True))


def _for_my_tile(i, nq, fn):
    for k in range(nq):
        @pl.when(i == k)
        def _(k=k):
            fn(k)


def _attn_fwd(h, q, kn, kpe, v, w_o, comm=None):
    t = h.shape[0]
    nq = t // TM

    def body(h_ref, q_ref, kn_ref, kpe_ref, v_ref, wo_ref, o_ref, att_ref):
        i, hd = pl.program_id(0), pl.program_id(1)

        @pl.when(hd == 0)
        def _():
            o_ref[...] = h_ref[...]

        def tile(k):
            p = _softmax_rows(q_ref[:, 0:QK_NOPE], q_ref[:, QK_NOPE:], kn_ref.at[hd], kpe_ref, k)
            ob = _dot(p.astype(BF16), v_ref[hd, 0:(k + 1) * TM, :]).astype(BF16)
            att_ref[...] = ob
            o_ref[...] += _dot(ob, wo_ref[hd])

        _for_my_tile(i, nq, tile)

    def per_head(d):
        return pl.BlockSpec((None, TM, d), lambda i, hd: (hd, i, 0))

    def resident(shape):
        zeros = (0,) * len(shape)
        return pl.BlockSpec(shape, lambda i, hd: zeros, pipeline_mode=pl.Buffered(1))

    tile_spec = pl.BlockSpec((TM, D_MODEL), lambda i, hd: (i, 0))
    return _call(
        "attn_fwd", body, (nq, B_HEADS),
        [tile_spec, per_head(QK_NOPE + QK_ROPE), resident((B_HEADS, t, QK_NOPE)), resident((t, QK_ROPE)),
         resident((B_HEADS, t, V_HEAD)), resident((B_HEADS, V_HEAD, D_MODEL))],
        [tile_spec, per_head(V_HEAD)], [_sds((t, D_MODEL), F32), _sds((B_HEADS, t, V_HEAD), BF16)],
        (h, q, kn, kpe, v, w_o), comm=comm)


def _mlp_bwd(h, a, dho, g, w1, w2, layer, comm=None):
    t = h.shape[0]

    def body(h_ref, a_ref, dho_ref, g_ref, w1_ref, w2_ref, dhi_ref, dg_ref, hn_ref, f_ref, da_ref, dhib_ref):
        gv = g_ref[...]
        y, xhat, rstd = _rms_fwd(h_ref[...], gv)
        hn_ref[...] = y.astype(BF16)
        dho_v = dho_ref[...]
        dhob = dho_v.astype(BF16)
        dhn = jnp.zeros((TM, D_MODEL), F32)
        for d in range(N_DEV):
            cs = slice(d * FF_SLOT, (d + 1) * FF_SLOT)
            r = jnp.maximum(a_ref[:, cs], 0.0)
            f_ref[:, cs] = (r * r).astype(BF16)
            da = (_dot_nt(dhob, w2_ref[d]) * (2.0 * r)).astype(BF16)
            da_ref[:, cs] = da
            dhn = dhn + _dot_nt(da, w1_ref[d])
        dx, dg = _rms_bwd(dhn, xhat, rstd, gv)
        dhi = dho_v + dx
        dhi_ref[...] = dhi
        dhib_ref[...] = dhi.astype(BF16)
        _acc(dg_ref, dg)

    return _call(
        f"mlp_bwd_{layer}", body, (t // TM,),
        [_row(D_MODEL), _row(D_FF), _row(D_MODEL), _res((1, D_MODEL)), *MLP_W_SPECS],
        [_row(D_MODEL), _const((1, D_MODEL)), _row(D_MODEL), _row(D_FF), _row(D_FF), _row(D_MODEL)],
        [_sds((t, D_MODEL), F32), _sds((1, D_MODEL), F32), _sds((t, D_MODEL), BF16), _sds((t, D_FF), BF16),
         _sds((t, D_FF), BF16), _sds((t, D_MODEL), BF16)],
        (h, a, dho, g, w1, w2), comm=comm)


def _attn_bwd(dh, q, kn, kpe, v, w_o, cos, sin, comm=None):
    t = dh.shape[0]
    half = QK_ROPE // 2

    def body(dh_ref, q_ref, kn_ref, kpe_ref, v_ref, wo_ref, cos_ref, sin_ref, dq_ref, dkn_ref, dv_ref, dkpe_ref):
        hd, i = pl.program_id(0), pl.program_id(1)

        @pl.when(i == 0)
        def _():
            dkn_ref[...] = jnp.zeros_like(dkn_ref)
            dv_ref[...] = jnp.zeros_like(dv_ref)

        @pl.when((i == 0) & (hd == 0))
        def _():
            dkpe_ref[...] = jnp.zeros_like(dkpe_ref)

        def tile(k):
            keys = slice(0, (k + 1) * TM)
            qn, qp = q_ref[:, 0:QK_NOPE], q_ref[:, QK_NOPE:]
            do = _dot_nt(dh_ref[k * TM:(k + 1) * TM, :], wo_ref[...]).astype(BF16)
            p = _softmax_rows(qn, qp, kn_ref, kpe_ref, k)
            dp = _dot_nt(do, v_ref[keys, :])
            ds = (p * (dp - jnp.sum(p * dp, axis=-1, keepdims=True)) * ATT_SCALE).astype(BF16)
            dq_ref[:, 0:QK_NOPE] = _dot(ds, kn_ref[keys, :]).astype(BF16)
            dq_ref[:, QK_NOPE:] = _rope(_dot(ds, kpe_ref[keys, :]), cos_ref[...], -sin_ref[...]).astype(BF16)
            dkn_ref[keys, :] += _dot_tn(ds, qn)
            dv_ref[keys, :] += _dot_tn(p.astype(BF16), do)
            dkpe_ref[keys, :] += _dot_tn(ds, qp)

        _for_my_tile(i, t // TM, tile)

    def per_head(rows, d, tiled):
        return pl.BlockSpec((None, rows, d), (lambda hd, i: (hd, i, 0)) if tiled else (lambda hd, i: (hd, 0, 0)))

    def tile(d):
        return pl.BlockSpec((TM, d), lambda hd, i: (i, 0))

    return _call(
        "attn_bwd", body, (B_HEADS, t // TM),
        [pl.BlockSpec((t, D_MODEL), lambda hd, i: (0, 0), pipeline_mode=pl.Buffered(1)),
         per_head(TM, QK_NOPE + QK_ROPE, True), per_head(t, QK_NOPE, False),
         pl.BlockSpec((t, QK_ROPE), lambda hd, i: (0, 0)), per_head(t, V_HEAD, False),
         per_head(V_HEAD, D_MODEL, False), tile(half), tile(half)],
        [per_head(TM, QK_NOPE + QK_ROPE, True), per_head(t, QK_NOPE, False), per_head(t, V_HEAD, False),
         pl.BlockSpec((t, QK_ROPE), lambda hd, i: (0, 0))],
        [_sds((B_HEADS, t, QK_NOPE + QK_ROPE), BF16), _sds((B_HEADS, t, QK_NOPE), F32),
         _sds((B_HEADS, t, V_HEAD), F32), _sds((t, QK_ROPE), F32)],
        (dh, q, kn, kpe, v, w_o, cos, sin), comm=comm)


def _kvq_bwd(h, dh, ckv, cqpre, dq, dkn, dv, dkpe, cos, sin, kvq_w):
    t = h.shape[0]
    half = QK_ROPE // 2

    def body(h_ref, dh_ref, ckv_ref, cqpre_ref, dq_ref, dkn_ref, dv_ref, dkpe_ref, cos_ref, sin_ref,
             srcg_ref, wkva_ref, kvag_ref, wkvb_ref, mixg_ref, wqa_ref, qg_ref, wqb_ref,
             dhi_ref, hq_ref, hk_ref, cq_ref, dcqpre_ref, c_ref, dkv_ref, dckv_ref,
             dmixg_ref, dsrcg_ref, dqg_ref, dkvag_ref):
        hv = h_ref[...]
        rstd = lax.rsqrt(jnp.mean(hv * hv, axis=-1, keepdims=True) + EPS)
        xhat = hv * rstd
        mixg, srcg, qg, kvag = mixg_ref[...], srcg_ref[...], qg_ref[...], kvag_ref[...]
        hq_ref[...] = (xhat * mixg).astype(BF16)
        hk_ref[...] = (xhat * srcg).astype(BF16)
        cq, cqhat, crstd = _rms_fwd(cqpre_ref[...], qg)
        cq_ref[...] = cq.astype(BF16)
        dcq = jnp.zeros((TM, Q_LORA), F32)
        for hd in range(B_HEADS):
            dcq = dcq + _dot_nt(dq_ref[hd], wqb_ref[hd])
        dcqpre, dqg = _rms_bwd(dcq, cqhat, crstd, qg)
        dcqpre_b = dcqpre.astype(BF16)
        dcqpre_ref[...] = dcqpre_b
        dxq, dmixg = _rms_bwd(_dot_nt(dcqpre_b, wqa_ref[...]), xhat, rstd, mixg)
        ckv = ckv_ref[...]
        c, chat, krstd = _rms_fwd(ckv[:, :KV_LORA], kvag)
        c_ref[...] = c.astype(BF16)
        dc = jnp.zeros((TM, KV_LORA), F32)
        for hd in range(B_HEADS):
            dkv = jnp.concatenate([dkn_ref[hd], dv_ref[hd]], axis=-1).astype(BF16)
            dkv_ref[hd] = dkv
            dc = dc + _dot_nt(dkv, wkvb_ref[hd])
        dlat, dkvag = _rms_bwd(dc, chat, krstd, kvag)
        dpe = _rope(dkpe_ref[...], cos_ref[...], -sin_ref[...])
        dckv_b = jnp.concatenate([dlat, dpe], axis=-1).astype(BF16)
        dckv_ref[...] = dckv_b
        dxk, dsrcg = _rms_bwd(_dot_nt(dckv_b, wkva_ref[...]), xhat, rstd, srcg)
        dhi_ref[...] = dh_ref[...] + dxq + dxk
        _acc(dmixg_ref, dmixg)
        _acc(dsrcg_ref, dsrcg)
        _acc(dqg_ref, dqg)
        _acc(dkvag_ref, dkvag)

    return _call(
        "kvq_bwd", body, (t // TM,),
        [_row(D_MODEL), _row(D_MODEL), _row(KV_LORA + QK_ROPE), _row(Q_LORA), _heads(QK_NOPE + QK_ROPE),
         _heads(QK_NOPE), _heads(V_HEAD), _row(QK_ROPE), _row(half), _row(half), *KVQ_W_SPECS],
        [_row(D_MODEL), _row(D_MODEL), _row(D_MODEL), _row(Q_LORA), _row(Q_LORA), _row(KV_LORA),
         _heads(QK_NOPE + V_HEAD), _row(KV_LORA + QK_ROPE),
         _const((1, D_MODEL)), _const((1, D_MODEL)), _const((1, Q_LORA)), _const((1, KV_LORA))],
        [_sds((t, D_MODEL), F32), _sds((t, D_MODEL), BF16), _sds((t, D_MODEL), BF16), _sds((t, Q_LORA), BF16),
         _sds((t, Q_LORA), BF16), _sds((t, KV_LORA), BF16), _sds((B_HEADS, t, QK_NOPE + V_HEAD), BF16),
         _sds((t, KV_LORA + QK_ROPE), BF16),
         _sds((1, D_MODEL), F32), _sds((1, D_MODEL), F32), _sds((1, Q_LORA), F32), _sds((1, KV_LORA), F32)],
        (h, dh, ckv, cqpre, dq, dkn, dv, dkpe, cos, sin, *kvq_w))[0]


def _a_mix_bwd(x, z, dh, g, w_in, ln_g, ln_b, w_s, b_st, w_out, comm=None):
    t = x.shape[0]
    tm = TM_GATE
    nblk = tm // GMLP_BLOCK

    def body(x_ref, z_ref, dh_ref, g_ref, win_ref, lng_ref, lnb_ref, ws_ref, bst_ref, wout_ref,
             dx_ref, hn_ref, dz_ref, dg_ref, dlng_ref, dlnb_ref, dws_ref, dbs_ref, dvn_scr, gelu_grad_v):
        @pl.when(pl.program_id(0) == 0)
        def _():
            dws_ref[...] = jnp.zeros_like(dws_ref)
            dbs_ref[...] = jnp.zeros_like(dbs_ref)

        gv, lng = g_ref[...], lng_ref[...]
        y, xhat, rstd = _rms_fwd(x_ref[...], gv)
        hn_ref[...] = y.astype(BF16)
        dhv = dh_ref[...]
        dgated = _dot_nt(dhv.astype(BF16), wout_ref[...])
        u, gelu_grad_u = _gelu_and_grad(z_ref[:, :GATE_DIM])
        v, gelu_grad_v[...] = _gelu_and_grad(z_ref[:, GATE_DIM:])
        vn, vhat, lrstd = _ln_fwd(v, lng, lnb_ref[...])
        vb = vn.astype(BF16)
        mask = _gate_mask()
        for gi in range(A_GROUPS):
            wm = jnp.where(mask, ws_ref[gi], 0.0).astype(BF16)
            bias = bst_ref[:, gi:gi + 1]
            cs = slice(gi * A_GROUP_DIM, (gi + 1) * A_GROUP_DIM)
            dws = jnp.zeros((GMLP_BLOCK, GMLP_BLOCK), F32)
            dbs = jnp.zeros((GMLP_BLOCK, 1), F32)
            for n in range(nblk):
                rs = slice(n * GMLP_BLOCK, (n + 1) * GMLP_BLOCK)
                sv = _dot(wm, vb[rs, cs]) + bias
                dz_ref[rs, cs] = (dgated[rs, cs] * sv * gelu_grad_u[rs, cs]).astype(BF16)
                dsv = dgated[rs, cs] * u[rs, cs]
                dsvb = dsv.astype(BF16)
                dws = dws + _dot_nt(dsvb, vb[rs, cs])
                dbs = dbs + jnp.sum(dsv, axis=-1, keepdims=True)
                dvn_scr[rs, cs] = _dot_tn(wm, dsvb)
            dws_ref[gi] += jnp.where(mask, dws, 0.0)
            dbs_ref[gi] += dbs
        dvn = dvn_scr[...]
        dvhat = dvn * lng
        dv = lrstd * (dvhat - jnp.mean(dvhat, axis=-1, keepdims=True)
                      - vhat * jnp.mean(dvhat * vhat, axis=-1, keepdims=True))
        dz_ref[:, GATE_DIM:] = (dv * gelu_grad_v[...]).astype(BF16)
        dhn = jnp.zeros((tm, D_MODEL), F32)
        for d in range(N_DEV):
            dhn = dhn + _dot_nt(dz_ref[:, d * FF_SLOT:(d + 1) * FF_SLOT], win_ref[d])
        dx, dg = _rms_bwd(dhn, xhat, rstd, gv)
        dx_ref[...] = dhv + dx
        _acc(dg_ref, dg)
        _acc(dlng_ref, jnp.sum(dvn * vhat, axis=0, keepdims=True))
        _acc(dlnb_ref, jnp.sum(dvn, axis=0, keepdims=True))

    return _call(
        "a_mix_bwd", body, (t // tm,),
        [_row(D_MODEL, tm), _row(2 * GATE_DIM, tm), _row(D_MODEL, tm), _res((1, D_MODEL)),
         _res((N_DEV, D_MODEL, FF_SLOT)), _res((1, GATE_DIM)), _res((1, GATE_DIM)),
         _res((A_GROUPS, GMLP_BLOCK, GMLP_BLOCK)), _res((GMLP_BLOCK, A_GROUPS)), _res((GATE_DIM, D_MODEL))],
        [_row(D_MODEL, tm), _row(D_MODEL, tm), _row(2 * GATE_DIM, tm),
         _const((1, D_MODEL)), _const((1, GATE_DIM)), _const((1, GATE_DIM)),
         _const((A_GROUPS, GMLP_BLOCK, GMLP_BLOCK)), _const((A_GROUPS, GMLP_BLOCK, 1))],
        [_sds((t, D_MODEL), F32), _sds((t, D_MODEL), BF16),
         _sds((t, 2 * GATE_DIM), BF16), _sds((1, D_MODEL), F32), _sds((1, GATE_DIM), F32),
         _sds((1, GATE_DIM), F32), _sds((A_GROUPS, GMLP_BLOCK, GMLP_BLOCK), F32),
         _sds((A_GROUPS, GMLP_BLOCK, 1), F32)],
        (x, z, dh, g, w_in, ln_g, ln_b, w_s, b_st, w_out),
        scratch=[pltpu.VMEM((tm, GATE_DIM), F32), pltpu.VMEM((tm, GATE_DIM), F32)], comm=comm)


def _wgrad(name, a, b, a_spec, b_spec, m, n, comm=None):
    def body(a_ref, b_ref, o_ref):
        o_ref[0] = _dot_tn(a_ref[...].astype(BF16), b_ref[...].astype(BF16)).astype(BF16)

    outs, got = _call(name, body, (N_DEV,), [a_spec, b_spec], [pl.BlockSpec((1, m, n), lambda d: (d, 0, 0))],
                      [_sds((N_DEV, m, n), BF16)], (a, b), comm=comm)
    return outs[0] if comm is None else (outs[0], got)


def _full(t, d):
    return pl.BlockSpec((t, d), lambda i: (0, 0), pipeline_mode=pl.Buffered(1))


def _cols(t, d):
    return pl.BlockSpec((t, d), lambda i: (0, i))


def _head(t, d):
    return pl.BlockSpec((None, t, d), lambda i: (i, 0, 0))


def _local_step(x, pos, target, inv_freq, wg, sm, shards=None):
    t = x.shape[0]
    wg = dict(wg)
    dist = shards is not None
    mix_g = [sm["norm_mix_g"][l:l + 1] for l in range(2)]
    mlp_g = [sm["norm_mlp_g"][l:l + 1] for l in range(2)]

    def gather(names):
        return _gather_comm([shards[k] for k in names]) if dist else None

    def send(grads):
        return _exchange_comm(grads=grads) if dist else None

    def send_sums(name, grads):
        return _chip_exchange_comm(_pair_reduce(name, grads)) if dist else None

    def a_args():
        return (wg["a_w_in"], wg["a_ln_v_g"], wg["a_ln_v_b"], sm["a_w_s"], sm["a_b_st"], wg["a_w_out"])

    def kvq_w():
        return (sm["kv_src_norm_g"], wg["kv_w_a"], sm["kv_a_norm_g"], wg["kv_w_b"], mix_g[1], wg["b_w_q_a"],
                sm["b_q_norm_g"], wg["b_w_q_b"])

    names = ("mlp_w1_0", "mlp_w2_0")
    (h1, z, gated), got = _a_mix_fwd(x, mix_g[0], *a_args(), comm=gather(names))
    wg.update(zip(names, got))
    names = ("kv_w_a", "kv_w_b", "b_w_q_a", "b_w_q_b", "b_w_o")
    (h2, a0), got = _mlp_fwd(h1, mlp_g[0], wg["mlp_w1_0"], wg["mlp_w2_0"], 0, comm=gather(names))
    wg.update(zip(names, got))
    if dist:
        wg["b_w_q_a"] = wg["b_w_q_a"].reshape(D_MODEL, Q_LORA)
        wg["kv_w_a"] = wg["kv_w_a"].reshape(D_MODEL, KV_LORA + QK_ROPE)
    ckv, kn, v, kpe, cqpre, q, cos, sin = _kvq_fwd(h2, pos, inv_freq, kvq_w())
    names = ("mlp_w1_1", "mlp_w2_1")
    (h3, att), got = _attn_fwd(h2, q, kn, kpe, v, wg["b_w_o"], comm=gather(names))
    wg.update(zip(names, got))
    a1, loss, dh4, d_final_g = _mlp_fwd_loss(h3, mlp_g[1], wg["mlp_w1_1"], wg["mlp_w2_1"], sm["final_norm_g"], target)

    g = {}
    (dh3, d_mlp_g1, hn, f, da, dh3_b), _ = _mlp_bwd(h3, a1, dh4, mlp_g[1], wg["mlp_w1_1"], wg["mlp_w2_1"], 1)
    g["mlp_w1_1"] = _wgrad("wgrad_w1_1", hn, da, _full(t, D_MODEL), _cols(t, FF_SLOT), D_MODEL, FF_SLOT)
    g["mlp_w2_1"] = _wgrad("wgrad_w2_1", f, dh4, _cols(t, FF_SLOT), _full(t, D_MODEL), FF_SLOT, D_MODEL)
    g["b_w_o"] = _wgrad("wgrad_w_o", att, dh3_b, _head(t, V_HEAD), _full(t, D_MODEL), V_HEAD, D_MODEL)
    names = ("mlp_w1_1", "mlp_w2_1", "b_w_o")
    (dq, dkn, dv, dkpe), got = _attn_bwd(dh3_b, q, kn, kpe, v, wg["b_w_o"], cos, sin,
                                         comm=send_sums("pair_reduce_1", [g[k] for k in names]))
    g.update(zip(names, got))
    (dh2, hq, hk, cq, dcqpre, c, dkv, dckv, d_mix_g1, d_src_g, d_q_g, d_kv_a_g) = _kvq_bwd(
        h2, dh3, ckv, cqpre, dq, dkn, dv, dkpe, cos, sin, kvq_w())
    g["b_w_q_a"] = _wgrad("wgrad_w_q_a", hq, dcqpre, _cols(t, D_MODEL // N_DEV), _full(t, Q_LORA),
                          D_MODEL // N_DEV, Q_LORA)
    g["b_w_q_b"] = _wgrad("wgrad_w_q_b", cq, dq, _full(t, Q_LORA), _head(t, QK_NOPE + QK_ROPE),
                          Q_LORA, QK_NOPE + QK_ROPE)
    g["kv_w_a"] = _wgrad("wgrad_kv_w_a", hk, dckv, _cols(t, D_MODEL // N_DEV), _full(t, KV_LORA + QK_ROPE),
                         D_MODEL // N_DEV, KV_LORA + QK_ROPE)
    g["kv_w_b"] = _wgrad("wgrad_kv_w_b", c, dkv, _full(t, KV_LORA), _head(t, QK_NOPE + V_HEAD),
                         KV_LORA, QK_NOPE + V_HEAD)
    names = ("b_w_q_a", "b_w_q_b", "kv_w_a", "kv_w_b")
    (dh1, d_mlp_g0, hn, f, da, dh1_b), got = _mlp_bwd(h1, a0, dh2, mlp_g[0], wg["mlp_w1_0"], wg["mlp_w2_0"], 0,
                                                      comm=send([g[k] for k in names]))
    g.update(zip(names, got))
    g["mlp_w1_0"] = _wgrad("wgrad_w1_0", hn, da, _full(t, D_MODEL), _cols(t, FF_SLOT), D_MODEL, FF_SLOT)
    g["mlp_w2_0"] = _wgrad("wgrad_w2_0", f, dh2, _cols(t, FF_SLOT), _full(t, D_MODEL), FF_SLOT, D_MODEL)
    g["a_w_out"] = _wgrad("wgrad_a_w_out", gated, dh1_b, _cols(t, GATE_DIM // N_DEV), _full(t, D_MODEL),
                          GATE_DIM // N_DEV, D_MODEL)
    names = ("mlp_w1_0", "mlp_w2_0")
    (dx, hn, dz, d_mix_g0, d_ln_g, d_ln_b, d_ws, d_bs), got = _a_mix_bwd(
        x, z, dh1, mix_g[0], *a_args(), comm=send_sums("pair_reduce_0", [g[k] for k in names]))
    g.update(zip(names, got))
    small = {
        "norm_mix_g": jnp.concatenate([d_mix_g0, d_mix_g1], axis=0),
        "norm_mlp_g": jnp.concatenate([d_mlp_g0, d_mlp_g1], axis=0),
        "a_ln_v_g": d_ln_g.reshape(N_DEV, GATE_DIM // N_DEV),
        "a_ln_v_b": d_ln_b.reshape(N_DEV, GATE_DIM // N_DEV),
        "a_w_s": d_ws.astype(BF16) if dist else d_ws,
        "a_b_s": d_bs.reshape(A_GROUPS, GMLP_BLOCK),
        "b_q_norm_g": d_q_g,
        "kv_src_norm_g": d_src_g,
        "kv_a_norm_g": d_kv_a_g,
        "final_norm_g": d_final_g,
    }
    wgrad_in = ("wgrad_a_w_in", hn, dz, _full(t, D_MODEL), _cols(t, FF_SLOT), D_MODEL, FF_SLOT)
    if dist:
        parts = [small[k].reshape((1,) + small[k].shape) for k in SMALL] + [loss.reshape(1, 1, 1)]
        g["a_w_in"], got = _wgrad(*wgrad_in, comm=_exchange_comm(parts=parts))
        small, loss = dict(zip(SMALL, got)), got[-1]
    else:
        g["a_w_in"] = _wgrad(*wgrad_in)
    return loss, dx, g, small


def _adamw(w, g, m, v):
    m = ADAM_B1 * m + (1.0 - ADAM_B1) * g
    v = ADAM_B2 * v + (1.0 - ADAM_B2) * (g * g)
    m_hat = m / (1.0 - ADAM_B1 ** ADAM_STEP)
    v_hat = v / (1.0 - ADAM_B2 ** ADAM_STEP)
    return -ADAM_LR * (m_hat / (jnp.sqrt(v_hat) + ADAM_EPS) + ADAM_WD * w), m, v


def _sum_in_device_order(r_ref):
    g = r_ref[0].astype(F32)
    for j in range(1, r_ref.shape[0]):
        g = g + r_ref[j].astype(F32)
    return g


def _adamw_sharded(name, recvs, w, m, v, comm=None):
    layers, r, c = w.shape
    tr = math.gcd(r, 256)
    flat = [a for per_layer in recvs for a in per_layer]

    def body(*refs):
        r_refs, (w_ref, m_ref, v_ref) = refs[:len(flat)], refs[len(flat):len(flat) + 3]
        g_ref, d_ref, nm_ref, nv_ref = refs[-4:]
        layer = pl.program_id(0)
        g, pos = None, 0
        for li, per_layer in enumerate(recvs):
            total = None
            for ref in r_refs[pos:pos + len(per_layer)]:
                part = _sum_in_device_order(ref)
                total = part if total is None else total + part
            pos += len(per_layer)
            g = total if g is None else jnp.where(layer == li, total, g)
        g_ref[...] = g
        d_ref[...], nm_ref[...], nv_ref[...] = _adamw(w_ref[...], g, m_ref[...], v_ref[...])

    blk = pl.BlockSpec((None, tr, c), lambda l, i: (l, i, 0))
    return _call(name, body, (layers, r // tr),
                 [pl.BlockSpec((a.shape[0], tr, c), lambda l, i: (0, i, 0)) for a in flat] + [blk] * 3,
                 [blk] * 4, [_sds(w.shape, F32)] * 4, (*flat, w, m, v), comm=comm)


def _adamw_small(recvs, ws, ms, vs, own_row, losses):
    n = len(recvs)

    def body(*refs):
        r_refs, w_refs, m_refs, v_refs = (refs[i * n:(i + 1) * n] for i in range(4))
        outs, scr = refs[4 * n + 1:8 * n + 2], refs[8 * n + 2:]
        outs[-1][...] = _sum_in_device_order(refs[4 * n])
        me = _my_place()[3]
        for a in range(n):
            g = _sum_in_device_order(r_refs[a])
            if own_row[a]:
                scr[0][...] = g
                g = scr[0][pl.ds(me, 1), :]
            g_ref, d_ref, nm_ref, nv_ref = outs[4 * a:4 * a + 4]
            g_ref[...] = g
            d_ref[...], nm_ref[...], nv_ref[...] = _adamw(w_refs[a][...], g, m_refs[a][...], v_refs[a][...])

    out_shape = []
    for w in ws:
        out_shape += [_sds(w.shape, F32)] * 4
    return pl.pallas_call(
        body, name="adamw_small", in_specs=[VMEM] * (4 * n + 1), out_specs=[VMEM] * (4 * n + 1),
        out_shape=out_shape + [_sds((1, 1), F32)], scratch_shapes=[pltpu.VMEM((N_DEV, GATE_DIM // N_DEV), F32)],
    )(*recvs, *ws, *ms, *vs, losses)


BIG = ("a_w_in", "a_w_out", "b_w_q_a", "b_w_q_b", "b_w_o", "kv_w_a", "kv_w_b", "mlp_w1", "mlp_w2")
SMALL = ("norm_mix_g", "norm_mlp_g", "a_ln_v_g", "a_ln_v_b", "a_w_s", "a_b_s", "b_q_norm_g", "kv_src_norm_g",
         "kv_a_norm_g", "final_norm_g")
WEIGHTS = ("norm_mix_g", "norm_mlp_g", "a_w_in", "a_ln_v_g", "a_ln_v_b", "a_w_s", "a_b_s", "a_w_out", "b_w_q_a",
           "b_q_norm_g", "b_w_q_b", "b_w_o", "kv_src_norm_g", "kv_w_a", "kv_a_norm_g", "kv_w_b", "mlp_w1", "mlp_w2",
           "final_norm_g")


def _two_d(name, a):
    if name in ("a_w_s", "a_b_s"):
        return a.reshape(a.shape[1:])
    return a.reshape(1, -1) if a.ndim == 1 else a


def _three_d(a):
    return a if a.ndim == 3 else a.reshape((1,) + a.shape)


def kernel(x, positions, norm_mix_g, norm_mlp_g, a_w_in, a_ln_v_g, a_ln_v_b, a_w_s, a_b_s, a_w_out, b_w_q_a, b_q_norm_g, b_w_q_b, b_w_o, kv_src_norm_g, kv_w_a, kv_a_norm_g, kv_w_b, mlp_w1, mlp_w2, final_norm_g, loss_target, m_norm_mix_g, m_norm_mlp_g, m_a_w_in, m_a_ln_v_g, m_a_ln_v_b, m_a_w_s, m_a_b_s, m_a_w_out, m_b_w_q_a, m_b_q_norm_g, m_b_w_q_b, m_b_w_o, m_kv_src_norm_g, m_kv_w_a, m_kv_a_norm_g, m_kv_w_b, m_mlp_w1, m_mlp_w2, m_final_norm_g, v_norm_mix_g, v_norm_mlp_g, v_a_w_in, v_a_ln_v_g, v_a_ln_v_b, v_a_w_s, v_a_b_s, v_a_w_out, v_b_w_q_a, v_b_q_norm_g, v_b_w_q_b, v_b_w_o, v_kv_src_norm_g, v_kv_w_a, v_kv_a_norm_g, v_kv_w_b, v_mlp_w1, v_mlp_w2, v_final_norm_g):
    w = dict(norm_mix_g=norm_mix_g, norm_mlp_g=norm_mlp_g, a_w_in=a_w_in, a_ln_v_g=a_ln_v_g, a_ln_v_b=a_ln_v_b,
             a_w_s=a_w_s, a_b_s=a_b_s, a_w_out=a_w_out, b_w_q_a=b_w_q_a, b_q_norm_g=b_q_norm_g, b_w_q_b=b_w_q_b,
             b_w_o=b_w_o, kv_src_norm_g=kv_src_norm_g, kv_w_a=kv_w_a, kv_a_norm_g=kv_a_norm_g, kv_w_b=kv_w_b,
             mlp_w1=mlp_w1, mlp_w2=mlp_w2, final_norm_g=final_norm_g)
    m = dict(norm_mix_g=m_norm_mix_g, norm_mlp_g=m_norm_mlp_g, a_w_in=m_a_w_in, a_ln_v_g=m_a_ln_v_g,
             a_ln_v_b=m_a_ln_v_b, a_w_s=m_a_w_s, a_b_s=m_a_b_s, a_w_out=m_a_w_out, b_w_q_a=m_b_w_q_a,
             b_q_norm_g=m_b_q_norm_g, b_w_q_b=m_b_w_q_b, b_w_o=m_b_w_o, kv_src_norm_g=m_kv_src_norm_g,
             kv_w_a=m_kv_w_a, kv_a_norm_g=m_kv_a_norm_g, kv_w_b=m_kv_w_b, mlp_w1=m_mlp_w1, mlp_w2=m_mlp_w2,
             final_norm_g=m_final_norm_g)
    v = dict(norm_mix_g=v_norm_mix_g, norm_mlp_g=v_norm_mlp_g, a_w_in=v_a_w_in, a_ln_v_g=v_a_ln_v_g,
             a_ln_v_b=v_a_ln_v_b, a_w_s=v_a_w_s, a_b_s=v_a_b_s, a_w_out=v_a_w_out, b_w_q_a=v_b_w_q_a,
             b_q_norm_g=v_b_q_norm_g, b_w_q_b=v_b_w_q_b, b_w_o=v_b_w_o, kv_src_norm_g=v_kv_src_norm_g,
             kv_w_a=v_kv_w_a, kv_a_norm_g=v_kv_a_norm_g, kv_w_b=v_kv_w_b, mlp_w1=v_mlp_w1, mlp_w2=v_mlp_w2,
             final_norm_g=v_final_norm_g)
    t = x.shape[1]

    first = ("a_w_in", "a_w_out", "a_ln_v_g", "a_ln_v_b")
    later = ("mlp_w1_0", "mlp_w2_0", "mlp_w1_1", "mlp_w2_1", "kv_w_a", "kv_w_b", "b_w_q_a", "b_w_q_b", "b_w_o")
    blocks = {k: _three_d(w[k]) for k in BIG if not k.startswith("mlp")}
    for k in ("mlp_w1", "mlp_w2"):
        blocks[k + "_0"], blocks[k + "_1"] = w[k][0:1], w[k][1:2]
    got, casts = _gather_first([blocks[k] if k in blocks else w[k] for k in first], [blocks[k] for k in later])
    wg = dict(zip(first, got))
    wg["a_w_out"] = wg["a_w_out"].reshape(GATE_DIM, D_MODEL)
    wg["a_ln_v_g"] = wg["a_ln_v_g"].reshape(1, GATE_DIM)
    wg["a_ln_v_b"] = wg["a_ln_v_b"].reshape(1, GATE_DIM)
    shards = dict(zip(later, casts))

    sm = {k: _two_d(k, w[k]) for k in SMALL if k not in ("a_ln_v_g", "a_ln_v_b")}
    sm["a_b_st"] = sm["a_b_s"].T
    inv_freq = (ROPE_THETA ** (-jnp.arange(0, QK_ROPE, 2, dtype=F32) / QK_ROPE)).reshape(1, QK_ROPE // 2)

    losses, dx, g, small = _local_step(x[0], positions.reshape(t, 1), loss_target[0], inv_freq, wg, sm, shards)

    names = ("a_w_in", "a_w_out")
    sums = _pair_reduce("pair_reduce_a", [g[k] for k in names])
    g.update(zip(names, _comm_only("exchange_last", _chip_exchange_comm(sums))))

    out = {}
    for k in BIG:
        recvs = [[g[k + "_0"]], [g[k + "_1"]]] if k.startswith("mlp") else [[g[k]]]
        res, _ = _adamw_sharded("adamw_" + k, recvs, _three_d(w[k]), _three_d(m[k]), _three_d(v[k]))
        out[k] = [o.reshape(w[k].shape) for o in res]
    own_row = [k in ("a_ln_v_g", "a_ln_v_b") for k in SMALL]
    res = _adamw_small([small[k] for k in SMALL], [_two_d(k, w[k]) for k in SMALL], [_two_d(k, m[k]) for k in SMALL],
                       [_two_d(k, v[k]) for k in SMALL], own_row, losses)
    for i, k in enumerate(SMALL):
        out[k] = [o.reshape(w[k].shape) for o in res[4 * i:4 * i + 4]]

    return (res[-1].reshape(()), dx.reshape(x.shape), *[out[k][0] for k in WEIGHTS], *[out[k][1] for k in WEIGHTS],
            *[out[k][2] for k in WEIGHTS], *[out[k][3] for k in WEIGHTS])
```

```python
import math

import jax
import jax.numpy as jnp
from jax import lax
from jax.experimental import pallas as pl
from jax.experimental.pallas import tpu as pltpu

F32, BF16 = jnp.float32, jnp.bfloat16
MESH = pl.DeviceIdType.MESH
ANY = pl.BlockSpec(memory_space=pl.ANY)
VMEM = pl.BlockSpec(memory_space=pltpu.VMEM)

N_DEV = 8
D_MODEL = 1024
CHUNK = 64
GMLP_BLOCK = 128
GATE_DIM = 2048
A_GROUPS = 8
A_GROUP_DIM = GATE_DIM // A_GROUPS
B_HEADS = 8
QK_NOPE, QK_ROPE, V_HEAD = 128, 64, 128
Q_LORA, KV_LORA = 384, 256
ROPE_THETA = 10000.0
D_FF = 4096
FF_SLOT = D_FF // N_DEV
EPS = 1e-6
ATT_SCALE = (QK_NOPE + QK_ROPE) ** -0.5

ADAM_LR, ADAM_B1, ADAM_B2, ADAM_EPS, ADAM_WD, ADAM_STEP = 0.001, 0.9, 0.999, 1e-08, 0.01, 10

TM = 256
TM_GATE = 128
VMEM_LIMIT = 56 * 1024 * 1024
INV_SQRT2 = 1.0 / math.sqrt(2.0)
INV_SQRT_2PI = 1.0 / math.sqrt(2.0 * math.pi)
LOG2_E = 1.0 / math.log(2.0)
HEADS_PER_STEP = 2


def _dot(a, b):
    return jnp.dot(a, b, preferred_element_type=F32)


def _dot_nt(a, b):
    return lax.dot_general(a, b, (((1,), (1,)), ((), ())), preferred_element_type=F32)


def _dot_tn(a, b):
    return lax.dot_general(a, b, (((0,), (0,)), ((), ())), preferred_element_type=F32)


def _rms_fwd(x, g):
    rstd = lax.rsqrt(jnp.mean(x * x, axis=-1, keepdims=True) + EPS)
    xhat = x * rstd
    return xhat * g, xhat, rstd


def _rms_bwd(dy, xhat, rstd, g):
    dxhat = dy * g
    dx = rstd * (dxhat - xhat * jnp.mean(dxhat * xhat, axis=-1, keepdims=True))
    return dx, jnp.sum(dy * xhat, axis=0, keepdims=True)


def _ln_fwd(v, g, b):
    mu = jnp.mean(v, axis=-1, keepdims=True)
    vc = v - mu
    rstd = lax.rsqrt(jnp.mean(vc * vc, axis=-1, keepdims=True) + EPS)
    vhat = vc * rstd
    return vhat * g + b, vhat, rstd


def _gelu(x):
    return 0.5 * x * (1.0 + lax.erf(x * INV_SQRT2))


def _gelu_and_grad(x):
    cdf = 0.5 * (1.0 + lax.erf(x * INV_SQRT2))
    return x * cdf, cdf + x * jnp.exp(-0.5 * x * x) * INV_SQRT_2PI


def _rope(x, cos, sin):
    x1, x2 = x[:, :QK_ROPE // 2], x[:, QK_ROPE // 2:]
    return jnp.concatenate([x1 * cos - x2 * sin, x2 * cos + x1 * sin], axis=-1)


def _gate_mask():
    row = lax.broadcasted_iota(jnp.int32, (GMLP_BLOCK, GMLP_BLOCK), 0)
    col = lax.broadcasted_iota(jnp.int32, (GMLP_BLOCK, GMLP_BLOCK), 1)
    return (col < CHUNK) | (row >= CHUNK)


def _att_mask(q0, tq, t):
    q = q0 + lax.broadcasted_iota(jnp.int32, (tq, t), 0)
    k = lax.broadcasted_iota(jnp.int32, (tq, t), 1)
    return jnp.right_shift(k, 6) <= jnp.right_shift(q, 6)


def _res(shape, imap=None):
    zeros = (0,) * len(shape)
    return pl.BlockSpec(shape, imap or (lambda i: zeros), pipeline_mode=pl.Buffered(1))


def _const(shape):
    zeros = (0,) * len(shape)
    return pl.BlockSpec(shape, lambda i: zeros)


def _row(d, tm=TM):
    return pl.BlockSpec((tm, d), lambda i: (i, 0))


def _heads(d):
    return pl.BlockSpec((B_HEADS, TM, d), lambda i: (0, i, 0))


def _sds(shape, dt):
    return jax.ShapeDtypeStruct(shape, dt)


def _acc(ref, val):
    @pl.when(pl.program_id(0) == 0)
    def _():
        ref[...] = jnp.zeros_like(ref)
    ref[...] += val


def _my_place():
    x, y, c = lax.axis_index("x"), lax.axis_index("y"), lax.axis_index("c")
    return x, y, c, 4 * x + 2 * y + c


def _peer(x, y, c, k):
    px = 1 - x if k & 4 else x
    py = 1 - y if k & 2 else y
    pc = 1 - c if k & 1 else c
    return (px, py, pc), 4 * px + 2 * py + pc


CHIPS = (2, 4, 6)


def _splits(ref):
    return len(ref.shape) >= 3 and ref.shape[1] % 32 == 0


def _piece(ref, block, half=None):
    if half is None or not _splits(ref):
        return ref.at[pl.ds(block, 1)]
    rows = ref.shape[1] // 2
    return ref.at[pl.ds(block, 1), pl.ds(half * rows, rows)]


def _gather_copy(sems, a, k, piece, to, src=None):
    return pltpu.make_async_remote_copy(
        src_ref=piece if src is None else src, dst_ref=piece, send_sem=sems[0].at[a, k], recv_sem=sems[1].at[a, k],
        device_id=to, device_id_type=MESH)


def _gather_start(srcs, outs, sems, only=None):
    x, y, c, me = _my_place()
    for a in range(len(srcs)) if only is None else (only,):
        mine = _piece(outs[a], me)
        pltpu.make_async_copy(srcs[a], mine, sems[2].at[a]).start()
        for k, rel in enumerate((1, 4, 2)):
            _gather_copy(sems, a, k, mine, _peer(x, y, c, rel)[0], src=srcs[a]).start()


def _gather_relay(srcs, outs, sems):
    x, y, c, _ = _my_place()
    sib = _peer(x, y, c, 1)[0]
    (xn, xn_i), (yn, yn_i) = _peer(x, y, c, 4), _peer(x, y, c, 2)
    for a in range(len(srcs)):
        out = outs[a]
        _gather_copy(sems, a, 1, _piece(out, xn_i), xn).wait_recv()
        _gather_copy(sems, a, 3, _piece(out, xn_i, 0), yn).start()
        _gather_copy(sems, a, 5, _piece(out, xn_i), sib).start()
        _gather_copy(sems, a, 2, _piece(out, yn_i), yn).wait_recv()
        if _splits(out):
            _gather_copy(sems, a, 4, _piece(out, yn_i, 1), xn).start()
        _gather_copy(sems, a, 6, _piece(out, yn_i), sib).start()


def _gather_finish(srcs, outs, sems):
    x, y, c, me = _my_place()
    sib = _peer(x, y, c, 1)[0]
    xn, yn, dg_i = _peer(x, y, c, 4)[0], _peer(x, y, c, 2)[0], _peer(x, y, c, 6)[1]
    n = len(srcs)
    for a in range(n):
        out = outs[a]
        _gather_copy(sems, a, 3, _piece(out, dg_i, 0), yn).wait_recv()
        _gather_copy(sems, a, 7, _piece(out, dg_i, 0), sib).start()
        if _splits(out):
            _gather_copy(sems, a, 4, _piece(out, dg_i, 1), xn).wait_recv()
            _gather_copy(sems, a, 8, _piece(out, dg_i, 1), sib).start()
    for a in range(n):
        out = outs[a]
        whole, half = _piece(out, me), _piece(out, me, 0)
        for k in (0, 5, 6):
            _gather_copy(sems, a, k, whole, sib).wait_recv()
        for k in (7, 8) if _splits(out) else (7,):
            _gather_copy(sems, a, k, half, sib).wait_recv()
        for k in (0, 1, 2):
            _gather_copy(sems, a, k, whole, sib, src=srcs[a]).wait_send()
        for k in (5, 6):
            _gather_copy(sems, a, k, whole, sib).wait_send()
        for k in (3, 4, 7, 8) if _splits(out) else (3, 7):
            _gather_copy(sems, a, k, half, sib).wait_send()
        pltpu.make_async_copy(srcs[a], whole, sems[2].at[a]).wait()


def _relay_sems(n):
    return [pltpu.SemaphoreType.DMA((n, 9)), pltpu.SemaphoreType.DMA((n, 9)), pltpu.SemaphoreType.DMA((n,))]


def _gather_sems(n):
    return [pltpu.SemaphoreType.DMA((n, 7)), pltpu.SemaphoreType.DMA((n, 7)), pltpu.SemaphoreType.DMA((n,))]


class _Comm:
    def __init__(self, args, out_shape, scratch, start, finish, relay=None):
        self.args, self.out_shape, self.scratch, self.start, self.finish = args, out_shape, scratch, start, finish
        self.relay = relay


def _gather_comm(shards):
    return _Comm(list(shards), [_sds((N_DEV,) + s.shape[1:], s.dtype) for s in shards], _relay_sems(len(shards)),
                 _gather_start, _gather_finish, relay=_gather_relay)


def _direct_copies(ins, outs, sems, wait, from_block):
    send_sems, recv_sems, local_sems = sems
    x, y, c, me = _my_place()
    for a in range(len(ins)):
        src = ins[a].at[pl.ds(me, 1)] if from_block[a] else ins[a]
        local = pltpu.make_async_copy(src, outs[a].at[pl.ds(me, 1)], local_sems.at[a])
        local.wait() if wait else local.start()
        for k in range(1, N_DEV):
            to, to_i = _peer(x, y, c, k)
            cp = pltpu.make_async_remote_copy(
                src_ref=ins[a].at[pl.ds(to_i, 1)] if from_block[a] else ins[a], dst_ref=outs[a].at[pl.ds(me, 1)],
                send_sem=send_sems.at[a, k - 1], recv_sem=recv_sems.at[a, k - 1], device_id=to, device_id_type=MESH)
            cp.wait() if wait else cp.start()


def _exchange_comm(grads=(), parts=()):
    ins = list(grads) + list(parts)
    from_block = [True] * len(grads) + [False] * len(parts)
    out_shape = [_sds(g.shape, g.dtype) for g in grads] + [_sds((N_DEV,) + p.shape[1:], p.dtype) for p in parts]

    def start(ins_, outs_, sems_):
        _direct_copies(ins_, outs_, sems_, False, from_block)

    def finish(ins_, outs_, sems_):
        _direct_copies(ins_, outs_, sems_, True, from_block)

    return _Comm(ins, out_shape, _gather_sems(len(ins)), start, finish)


def _chip_copies(ins, outs, sems, wait, rels, own):
    send_sems, recv_sems, local_sems = sems
    x, y, c, _ = _my_place()
    for a in range(len(ins)):
        if own:
            local = pltpu.make_async_copy(ins[a].at[pl.ds(2 * x + y, 1)], outs[a].at[pl.ds(len(rels), 1)],
                                          local_sems.at[a])
            local.wait() if wait else local.start()
        for i, j in enumerate(rels):
            to = _peer(x, y, c, CHIPS[j])[0]
            cp = pltpu.make_async_remote_copy(
                src_ref=ins[a].at[pl.ds(2 * to[0] + to[1], 1)], dst_ref=outs[a].at[pl.ds(i, 1)],
                send_sem=send_sems.at[a, i], recv_sem=recv_sems.at[a, i], device_id=to, device_id_type=MESH)
            cp.wait() if wait else cp.start()


def _chip_exchange_comm(sums, rels=(0, 1, 2), own=True):
    def start(ins_, outs_, sems_):
        _chip_copies(ins_, outs_, sems_, False, rels, own)

    def finish(ins_, outs_, sems_):
        _chip_copies(ins_, outs_, sems_, True, rels, own)

    n = len(sums)
    sems = [pltpu.SemaphoreType.DMA((n, len(rels))), pltpu.SemaphoreType.DMA((n, len(rels))),
            pltpu.SemaphoreType.DMA((n,))]
    return _Comm(list(sums), [_sds((len(rels) + own,) + s.shape[1:], s.dtype) for s in sums], sems, start, finish)


def _pair_reduce(name, grads):
    n = len(grads)
    n_chips = N_DEV // 2

    def body(*refs):
        g_refs, gh_refs, p_refs, land = refs[:n], refs[n:2 * n], refs[2 * n:3 * n], refs[3 * n:4 * n]
        send_sems, recv_sems = refs[4 * n:]
        x, y, c, _ = _my_place()
        sib = _peer(x, y, c, 1)[0]
        q = pl.program_id(0)

        def to_sibling(a, j):
            return pltpu.make_async_remote_copy(
                src_ref=gh_refs[a].at[j, pl.ds(1 - c, 1)], dst_ref=land[a].at[pl.ds(j, 1)],
                send_sem=send_sems.at[a, j], recv_sem=recv_sems.at[a, j], device_id=sib, device_id_type=MESH)

        @pl.when(q == 0)
        def _():
            for j in range(n_chips):
                for a in range(n):
                    to_sibling(a, j).start()

        for a in range(n):
            to_sibling(a, q).wait_recv()
            p_refs[a][...] = (g_refs[a][0, pl.ds(c, 1)].astype(F32) + land[a][pl.ds(q, 1)].astype(F32)).astype(BF16)

        @pl.when(q == n_chips - 1)
        def _():
            for a in range(n):
                for j in range(n_chips):
                    to_sibling(a, j).wait_send()

    views = [g.reshape((n_chips, 2) + g.shape[1:]) for g in grads]
    res = pl.pallas_call(
        body, name=name, grid=(n_chips,),
        in_specs=[pl.BlockSpec((1, 2) + g.shape[1:], lambda q: (q, 0, 0, 0)) for g in grads] + [ANY] * n,
        out_specs=[pl.BlockSpec((1,) + g.shape[1:], lambda q: (q, 0, 0)) for g in grads],
        out_shape=[_sds((n_chips,) + g.shape[1:], BF16) for g in grads],
        scratch_shapes=[pltpu.VMEM((n_chips,) + g.shape[1:], BF16) for g in grads]
        + [pltpu.SemaphoreType.DMA((n, n_chips)), pltpu.SemaphoreType.DMA((n, n_chips))],
        compiler_params=pltpu.CompilerParams(dimension_semantics=("arbitrary",), vmem_limit_bytes=VMEM_LIMIT),
    )(*views, *views)
    return list(res)


def _call(name, body, grid, in_specs, out_specs, out_shape, args, scratch=(), comm=None):
    params = pltpu.CompilerParams(dimension_semantics=("arbitrary",) * len(grid), vmem_limit_bytes=VMEM_LIMIT)
    if comm is None:
        outs = pl.pallas_call(body, name=name, grid=grid, in_specs=list(in_specs), out_specs=list(out_specs),
                              out_shape=list(out_shape), scratch_shapes=list(scratch), compiler_params=params)(*args)
        return list(outs), []
    ni, nci, no, nco, ns = len(in_specs), len(comm.args), len(out_specs), len(comm.out_shape), len(scratch)

    def carrying(*refs):
        ins, refs = refs[:ni], refs[ni:]
        cin, refs = refs[:nci], refs[nci:]
        outs, refs = refs[:no], refs[no:]
        cout, refs = refs[:nco], refs[nco:]
        scr, csems = refs[:ns], refs[ns:]
        step = pl.program_id(0)
        for ax in range(1, len(grid)):
            step = step * grid[ax] + pl.program_id(ax)
        steps = math.prod(grid)

        @pl.when(step == 0)
        def _():
            comm.start(cin, cout, csems)

        if comm.relay is not None:
            @pl.when(step == (2 * steps) // 3)
            def _():
                comm.relay(cin, cout, csems)

        body(*ins, *outs, *scr)

        @pl.when(step == steps - 1)
        def _():
            comm.finish(cin, cout, csems)

    outs = pl.pallas_call(
        carrying, name=name, grid=grid, in_specs=list(in_specs) + [ANY] * nci, out_specs=list(out_specs) + [ANY] * nco,
        out_shape=list(out_shape) + list(comm.out_shape), scratch_shapes=list(scratch) + list(comm.scratch),
        compiler_params=params)(*args, *comm.args)
    return list(outs[:no]), list(outs[no:])


def _comm_only(name, comm):
    def body(*refs):
        nci, nco = len(comm.args), len(comm.out_shape)
        cin, cout, csems = refs[:nci], refs[nci:nci + nco], refs[nci + nco:]
        comm.start(cin, cout, csems)
        if comm.relay is not None:
            comm.relay(cin, cout, csems)
        comm.finish(cin, cout, csems)

    return pl.pallas_call(body, name=name, in_specs=[ANY] * len(comm.args), out_specs=[ANY] * len(comm.out_shape),
                          out_shape=list(comm.out_shape), scratch_shapes=list(comm.scratch))(*comm.args)


def _gather_first(first, later):
    nf, nl = len(first), len(later)
    dts = [BF16] * (nf - 2) + [F32, F32]

    def body(*refs):
        ins, refs = refs[:nf + nl], refs[nf + nl:]
        outs, refs = refs[:nf], refs[nf:]
        casts, refs = refs[:nl], refs[nl:]
        stage, sems = refs[:nf], refs[nf:]
        for a in range(nf):
            stage[a][...] = ins[a][...].astype(dts[a])
            _gather_start(stage, outs, sems, only=a)
        for a in range(nl):
            casts[a][...] = ins[nf + a][...].astype(BF16)
        _gather_relay(stage, outs, sems)
        _gather_finish(stage, outs, sems)

    res = pl.pallas_call(
        body, name="gather_first",
        in_specs=[VMEM] * (nf + nl), out_specs=[ANY] * nf + [VMEM] * nl,
        out_shape=[_sds((N_DEV,) + s.shape[1:], dt) for s, dt in zip(first, dts)]
        + [_sds(s.shape, BF16) for s in later],
        scratch_shapes=[pltpu.VMEM(s.shape, dt) for s, dt in zip(first, dts)] + _relay_sems(nf),
        compiler_params=pltpu.CompilerParams(vmem_limit_bytes=VMEM_LIMIT),
    )(*first, *later)
    return list(res[:nf]), list(res[nf:])


def _a_mix_fwd(x, g, w_in, ln_g, ln_b, w_s, b_st, w_out, comm=None):
    t = x.shape[0]
    nblk = TM // GMLP_BLOCK

    def body(x_ref, g_ref, win_ref, lng_ref, lnb_ref, ws_ref, bst_ref, wout_ref, h_ref, z_ref, gated_scr):
        xv = x_ref[...]
        hb = _rms_fwd(xv, g_ref[...])[0].astype(BF16)
        for d in range(N_DEV):
            z_ref[:, d * FF_SLOT:(d + 1) * FF_SLOT] = _dot(hb, win_ref[d])
        u = _gelu(z_ref[:, :GATE_DIM])
        vb = _ln_fwd(_gelu(z_ref[:, GATE_DIM:]), lng_ref[...], lnb_ref[...])[0].astype(BF16)
        mask = _gate_mask()
        for gi in range(A_GROUPS):
            wm = jnp.where(mask, ws_ref[gi], 0.0).astype(BF16)
            bias = bst_ref[:, gi:gi + 1]
            cs = slice(gi * A_GROUP_DIM, (gi + 1) * A_GROUP_DIM)
            for n in range(nblk):
                rs = slice(n * GMLP_BLOCK, (n + 1) * GMLP_BLOCK)
                sv = _dot(wm, vb[rs, cs]) + bias
                gated_scr[rs, cs] = (u[rs, cs] * sv).astype(BF16)
        h_ref[...] = xv + _dot(gated_scr[...], wout_ref[...])

    return _call(
        "a_mix_fwd", body, (t // TM,),
        [_row(D_MODEL), _res((1, D_MODEL)), _res((N_DEV, D_MODEL, FF_SLOT)), _res((1, GATE_DIM)),
         _res((1, GATE_DIM)), _res((A_GROUPS, GMLP_BLOCK, GMLP_BLOCK)), _res((GMLP_BLOCK, A_GROUPS)),
         _res((GATE_DIM, D_MODEL))],
        [_row(D_MODEL), _row(2 * GATE_DIM), _row(GATE_DIM)],
        [_sds((t, D_MODEL), F32), _sds((t, 2 * GATE_DIM), F32), _sds((t, GATE_DIM), BF16)],
        (x, g, w_in, ln_g, ln_b, w_s, b_st, w_out), comm=comm)


MLP_W_SPECS = (_res((N_DEV, D_MODEL, FF_SLOT)), _res((N_DEV, FF_SLOT, D_MODEL)))


def _mlp_fwd(h, g, w1, w2, layer, comm=None):
    t = h.shape[0]

    def body(h_ref, g_ref, w1_ref, w2_ref, o_ref, a_ref):
        hv = h_ref[...]
        hb = _rms_fwd(hv, g_ref[...])[0].astype(BF16)
        o_ref[...] = hv
        for d in range(N_DEV):
            a = _dot(hb, w1_ref[d])
            a_ref[:, d * FF_SLOT:(d + 1) * FF_SLOT] = a
            r = jnp.maximum(a, 0.0)
            o_ref[...] += _dot((r * r).astype(BF16), w2_ref[d])

    return _call(
        f"mlp_fwd_{layer}", body, (t // TM,), [_row(D_MODEL), _res((1, D_MODEL)), *MLP_W_SPECS],
        [_row(D_MODEL), _row(D_FF)], [_sds((t, D_MODEL), F32), _sds((t, D_FF), F32)], (h, g, w1, w2), comm=comm)


def _mlp_fwd_loss(h, g, w1, w2, final_g, target):
    t = h.shape[0]

    def body(h_ref, g_ref, w1_ref, w2_ref, fg_ref, t_ref, a_ref, loss_ref, dh_ref, dg_ref):
        hv = h_ref[...]
        hb = _rms_fwd(hv, g_ref[...])[0].astype(BF16)
        out = hv
        for d in range(N_DEV):
            a = _dot(hb, w1_ref[d])
            a_ref[:, d * FF_SLOT:(d + 1) * FF_SLOT] = a
            r = jnp.maximum(a, 0.0)
            out = out + _dot((r * r).astype(BF16), w2_ref[d])
        y, xhat, rstd = _rms_fwd(out, fg_ref[...])
        err = y - t_ref[...]
        part = 0.5 * jnp.sum(jnp.mean(err * err, axis=-1, keepdims=True), axis=0, keepdims=True)
        dx, dg = _rms_bwd(err * (1.0 / D_MODEL), xhat, rstd, fg_ref[...])
        dh_ref[...] = dx
        _acc(dg_ref, dg)
        _acc(loss_ref, part)

    return _call(
        "mlp_fwd_loss", body, (t // TM,),
        [_row(D_MODEL), _res((1, D_MODEL)), *MLP_W_SPECS, _res((1, D_MODEL)), _row(D_MODEL)],
        [_row(D_FF), _const((1, 1)), _row(D_MODEL), _const((1, D_MODEL))],
        [_sds((t, D_FF), F32), _sds((1, 1), F32), _sds((t, D_MODEL), F32), _sds((1, D_MODEL), F32)],
        (h, g, w1, w2, final_g, target))[0]


KVQ_W_SPECS = (_res((1, D_MODEL)), _res((D_MODEL, KV_LORA + QK_ROPE)), _res((1, KV_LORA)),
               _res((B_HEADS, KV_LORA, QK_NOPE + V_HEAD)), _res((1, D_MODEL)), _res((D_MODEL, Q_LORA)),
               _res((1, Q_LORA)), _res((B_HEADS, Q_LORA, QK_NOPE + QK_ROPE)))


def _kvq_fwd(h, pos, inv_freq, kvq_w):
    t = h.shape[0]
    half = QK_ROPE // 2

    def body(h_ref, pos_ref, invf_ref, srcg_ref, wkva_ref, kvag_ref, wkvb_ref, mixg_ref, wqa_ref, qg_ref, wqb_ref,
             ckv_ref, k_ref, v_ref, cqpre_ref, q_ref, cos_ref, sin_ref):
        hv = h_ref[...]
        xhat = hv * lax.rsqrt(jnp.mean(hv * hv, axis=-1, keepdims=True) + EPS)
        ang = pos_ref[...].astype(F32) * invf_ref[...]
        cos, sin = jnp.cos(ang), jnp.sin(ang)
        cos_ref[...] = cos
        sin_ref[...] = sin
        ckv = _dot((xhat * srcg_ref[...]).astype(BF16), wkva_ref[...])
        ckv_ref[...] = ckv
        cb = _rms_fwd(ckv[:, :KV_LORA], kvag_ref[...])[0].astype(BF16)
        kpe = _rope(ckv[:, KV_LORA:], cos, sin).astype(BF16)
        for hd in range(B_HEADS):
            kv = _dot(cb, wkvb_ref[hd])
            k_ref[hd, :, 0:QK_NOPE] = kv[:, :QK_NOPE].astype(BF16)
            k_ref[hd, :, QK_NOPE:] = kpe
            v_ref[hd] = kv[:, QK_NOPE:].astype(BF16)
        cqpre = _dot((xhat * mixg_ref[...]).astype(BF16), wqa_ref[...])
        cqpre_ref[...] = cqpre
        cqb = _rms_fwd(cqpre, qg_ref[...])[0].astype(BF16)
        for hd in range(B_HEADS):
            q = _dot(cqb, wqb_ref[hd])
            q_ref[hd, :, 0:QK_NOPE] = q[:, :QK_NOPE].astype(BF16)
            q_ref[hd, :, QK_NOPE:] = _rope(q[:, QK_NOPE:], cos, sin).astype(BF16)

    return _call(
        "kvq_fwd", body, (t // TM,), [_row(D_MODEL), _row(1), _res((1, half)), *KVQ_W_SPECS],
        [_row(KV_LORA + QK_ROPE), _heads(QK_NOPE + QK_ROPE), _heads(V_HEAD), _row(Q_LORA),
         _heads(QK_NOPE + QK_ROPE), _row(half), _row(half)],
        [_sds((t, KV_LORA + QK_ROPE), F32), _sds((B_HEADS, t, QK_NOPE + QK_ROPE), BF16),
         _sds((B_HEADS, t, V_HEAD), BF16), _sds((t, Q_LORA), F32), _sds((B_HEADS, t, QK_NOPE + QK_ROPE), BF16),
         _sds((t, half), F32), _sds((t, half), F32)],
        (h, pos, inv_freq, *kvq_w))[0]


def _softmax_rows(q, k_ref, k):
    past, upto = k * TM, (k + 1) * TM
    s = _dot_nt(q, k_ref[0:upto, :])
    own = jnp.where(_att_mask(0, TM, TM), s[:, past:], jnp.finfo(F32).min)
    s = own if k == 0 else jnp.concatenate([s[:, :past], own], axis=1)
    e = jnp.exp2((s - jnp.max(s, axis=-1, keepdims=True)) * (ATT_SCALE * LOG2_E))
    return e * (1.0 / jnp.sum(e, axis=-1, keepdims=True))


def _for_my_tile(i, nq, fn):
    for k in range(nq):
        @pl.when(i == k)
        def _(k=k):
            fn(k)


def _attn_fwd(h, q, k, v, w_o, comm=None):
    t = h.shape[0]
    nq, hps = t // TM, HEADS_PER_STEP

    def body(h_ref, q_ref, k_ref, v_ref, wo_ref, o_ref, att_ref):
        i, pair = pl.program_id(0), pl.program_id(1)

        @pl.when(pair == 0)
        def _():
            o_ref[...] = h_ref[...]

        def tile(kt):
            proj = None
            for j in range(hps):
                hd = pair * hps + j
                p = _softmax_rows(q_ref[j], k_ref.at[hd], kt)
                ob = _dot(p.astype(BF16), v_ref[hd, 0:(kt + 1) * TM, :]).astype(BF16)
                att_ref[j] = ob
                proj = _dot(ob, wo_ref[hd]) if proj is None else proj + _dot(ob, wo_ref[hd])
            o_ref[...] += proj

        _for_my_tile(i, nq, tile)

    def per_head(d):
        return pl.BlockSpec((hps, TM, d), lambda i, pair: (pair, i, 0))

    def resident(shape):
        zeros = (0,) * len(shape)
        return pl.BlockSpec(shape, lambda i, pair: zeros, pipeline_mode=pl.Buffered(1))

    tile_spec = pl.BlockSpec((TM, D_MODEL), lambda i, pair: (i, 0))
    return _call(
        "attn_fwd", body, (nq, B_HEADS // hps),
        [tile_spec, per_head(QK_NOPE + QK_ROPE), resident((B_HEADS, t, QK_NOPE + QK_ROPE)),
         resident((B_HEADS, t, V_HEAD)), resident((B_HEADS, V_HEAD, D_MODEL))],
        [tile_spec, per_head(V_HEAD)], [_sds((t, D_MODEL), F32), _sds((B_HEADS, t, V_HEAD), BF16)],
        (h, q, k, v, w_o), comm=comm)


def _mlp_bwd(h, a, dho, g, w1, w2, layer, comm=None):
    t = h.shape[0]

    def body(h_ref, a_ref, dho_ref, g_ref, w1_ref, w2_ref, dhi_ref, dg_ref, hn_ref, f_ref, da_ref, dhib_ref):
        gv = g_ref[...]
        y, xhat, rstd = _rms_fwd(h_ref[...], gv)
        hn_ref[...] = y.astype(BF16)
        dho_v = dho_ref[...]
        dhob = dho_v.astype(BF16)
        dhn = jnp.zeros((TM, D_MODEL), F32)
        for d in range(N_DEV):
            cs = slice(d * FF_SLOT, (d + 1) * FF_SLOT)
            r = jnp.maximum(a_ref[:, cs], 0.0)
            f_ref[:, cs] = (r * r).astype(BF16)
            da = (_dot_nt(dhob, w2_ref[d]) * (2.0 * r)).astype(BF16)
            da_ref[:, cs] = da
            dhn = dhn + _dot_nt(da, w1_ref[d])
        dx, dg = _rms_bwd(dhn, xhat, rstd, gv)
        dhi = dho_v + dx
        dhi_ref[...] = dhi
        dhib_ref[...] = dhi.astype(BF16)
        _acc(dg_ref, dg)

    return _call(
        f"mlp_bwd_{layer}", body, (t // TM,),
        [_row(D_MODEL), _row(D_FF), _row(D_MODEL), _res((1, D_MODEL)), *MLP_W_SPECS],
        [_row(D_MODEL), _const((1, D_MODEL)), _row(D_MODEL), _row(D_FF), _row(D_FF), _row(D_MODEL)],
        [_sds((t, D_MODEL), F32), _sds((1, D_MODEL), F32), _sds((t, D_MODEL), BF16), _sds((t, D_FF), BF16),
         _sds((t, D_FF), BF16), _sds((t, D_MODEL), BF16)],
        (h, a, dho, g, w1, w2), comm=comm)


def _attn_bwd(dh, q, k, v, w_o, cos, sin, comm=None):
    t = dh.shape[0]
    half, hps = QK_ROPE // 2, HEADS_PER_STEP

    def body(dh_ref, q_ref, k_ref, v_ref, wo_ref, cos_ref, sin_ref, dq_ref, dk_ref, dv_ref):
        i = pl.program_id(1)

        @pl.when(i == 0)
        def _():
            dk_ref[...] = jnp.zeros_like(dk_ref)
            dv_ref[...] = jnp.zeros_like(dv_ref)

        def tile(kt):
            keys = slice(0, (kt + 1) * TM)
            for j in range(hps):
                qj = q_ref[j]
                do = _dot_nt(dh_ref[kt * TM:(kt + 1) * TM, :], wo_ref[j]).astype(BF16)
                p = _softmax_rows(qj, k_ref.at[j], kt)
                dp = _dot_nt(do, v_ref[j, keys, :])
                ds = (p * (dp - jnp.sum(p * dp, axis=-1, keepdims=True)) * ATT_SCALE).astype(BF16)
                dq = _dot(ds, k_ref[j, keys, :])
                dq_ref[j, :, 0:QK_NOPE] = dq[:, :QK_NOPE].astype(BF16)
                dq_ref[j, :, QK_NOPE:] = _rope(dq[:, QK_NOPE:], cos_ref[...], -sin_ref[...]).astype(BF16)
                dk_ref[j, keys, :] += _dot_tn(ds, qj)
                dv_ref[j, keys, :] += _dot_tn(p.astype(BF16), do)

        _for_my_tile(i, t // TM, tile)

    def per_pair(rows, d, tiled):
        return pl.BlockSpec((hps, rows, d), (lambda pair, i: (pair, i, 0)) if tiled else (lambda pair, i: (pair, 0, 0)))

    def tile(d):
        return pl.BlockSpec((TM, d), lambda pair, i: (i, 0))

    return _call(
        "attn_bwd", body, (B_HEADS // hps, t // TM),
        [pl.BlockSpec((t, D_MODEL), lambda pair, i: (0, 0), pipeline_mode=pl.Buffered(1)),
         per_pair(TM, QK_NOPE + QK_ROPE, True), per_pair(t, QK_NOPE + QK_ROPE, False), per_pair(t, V_HEAD, False),
         per_pair(V_HEAD, D_MODEL, False), tile(half), tile(half)],
        [per_pair(TM, QK_NOPE + QK_ROPE, True), per_pair(t, QK_NOPE + QK_ROPE, False), per_pair(t, V_HEAD, False)],
        [_sds((B_HEADS, t, QK_NOPE + QK_ROPE), BF16), _sds((B_HEADS, t, QK_NOPE + QK_ROPE), F32),
         _sds((B_HEADS, t, V_HEAD), F32)],
        (dh, q, k, v, w_o, cos, sin), comm=comm)


def _kvq_bwd(h, dh, ckv, cqpre, dq, dk, dv, cos, sin, kvq_w):
    t = h.shape[0]
    half = QK_ROPE // 2

    def body(h_ref, dh_ref, ckv_ref, cqpre_ref, dq_ref, dk_ref, dv_ref, cos_ref, sin_ref,
             srcg_ref, wkva_ref, kvag_ref, wkvb_ref, mixg_ref, wqa_ref, qg_ref, wqb_ref,
             dhi_ref, hq_ref, hk_ref, cq_ref, dcqpre_ref, c_ref, dkv_ref, dckv_ref,
             dmixg_ref, dsrcg_ref, dqg_ref, dkvag_ref):
        hv = h_ref[...]
        rstd = lax.rsqrt(jnp.mean(hv * hv, axis=-1, keepdims=True) + EPS)
        xhat = hv * rstd
        mixg, srcg, qg, kvag = mixg_ref[...], srcg_ref[...], qg_ref[...], kvag_ref[...]
        hq_ref[...] = (xhat * mixg).astype(BF16)
        hk_ref[...] = (xhat * srcg).astype(BF16)
        cq, cqhat, crstd = _rms_fwd(cqpre_ref[...], qg)
        cq_ref[...] = cq.astype(BF16)
        dcq = jnp.zeros((TM, Q_LORA), F32)
        for hd in range(B_HEADS):
            dcq = dcq + _dot_nt(dq_ref[hd], wqb_ref[hd])
        dcqpre, dqg = _rms_bwd(dcq, cqhat, crstd, qg)
        dcqpre_b = dcqpre.astype(BF16)
        dcqpre_ref[...] = dcqpre_b
        dxq, dmixg = _rms_bwd(_dot_nt(dcqpre_b, wqa_ref[...]), xhat, rstd, mixg)
        ckv = ckv_ref[...]
        c, chat, krstd = _rms_fwd(ckv[:, :KV_LORA], kvag)
        c_ref[...] = c.astype(BF16)
        dc = jnp.zeros((TM, KV_LORA), F32)
        dkpe = jnp.zeros((TM, QK_ROPE), F32)
        for hd in range(B_HEADS):
            dkv = jnp.concatenate([dk_ref[hd, :, 0:QK_NOPE], dv_ref[hd]], axis=-1).astype(BF16)
            dkv_ref[hd] = dkv
            dc = dc + _dot_nt(dkv, wkvb_ref[hd])
            dkpe = dkpe + dk_ref[hd, :, QK_NOPE:]
        dlat, dkvag = _rms_bwd(dc, chat, krstd, kvag)
        dpe = _rope(dkpe, cos_ref[...], -sin_ref[...])
        dckv_b = jnp.concatenate([dlat, dpe], axis=-1).astype(BF16)
        dckv_ref[...] = dckv_b
        dxk, dsrcg = _rms_bwd(_dot_nt(dckv_b, wkva_ref[...]), xhat, rstd, srcg)
        dhi_ref[...] = dh_ref[...] + dxq + dxk
        _acc(dmixg_ref, dmixg)
        _acc(dsrcg_ref, dsrcg)
        _acc(dqg_ref, dqg)
        _acc(dkvag_ref, dkvag)

    return _call(
        "kvq_bwd", body, (t // TM,),
        [_row(D_MODEL), _row(D_MODEL), _row(KV_LORA + QK_ROPE), _row(Q_LORA), _heads(QK_NOPE + QK_ROPE),
         _heads(QK_NOPE + QK_ROPE), _heads(V_HEAD), _row(half), _row(half), *KVQ_W_SPECS],
        [_row(D_MODEL), _row(D_MODEL), _row(D_MODEL), _row(Q_LORA), _row(Q_LORA), _row(KV_LORA),
         _heads(QK_NOPE + V_HEAD), _row(KV_LORA + QK_ROPE),
         _const((1, D_MODEL)), _const((1, D_MODEL)), _const((1, Q_LORA)), _const((1, KV_LORA))],
        [_sds((t, D_MODEL), F32), _sds((t, D_MODEL), BF16), _sds((t, D_MODEL), BF16), _sds((t, Q_LORA), BF16),
         _sds((t, Q_LORA), BF16), _sds((t, KV_LORA), BF16), _sds((B_HEADS, t, QK_NOPE + V_HEAD), BF16),
         _sds((t, KV_LORA + QK_ROPE), BF16),
         _sds((1, D_MODEL), F32), _sds((1, D_MODEL), F32), _sds((1, Q_LORA), F32), _sds((1, KV_LORA), F32)],
        (h, dh, ckv, cqpre, dq, dk, dv, cos, sin, *kvq_w))[0]


def _a_mix_bwd(x, z, dh, g, w_in, ln_g, ln_b, w_s, b_st, w_out, comm=None):
    t = x.shape[0]
    tm = TM_GATE
    nblk = tm // GMLP_BLOCK

    def body(x_ref, z_ref, dh_ref, g_ref, win_ref, lng_ref, lnb_ref, ws_ref, bst_ref, wout_ref,
             dx_ref, hn_ref, dz_ref, dg_ref, dlng_ref, dlnb_ref, dws_ref, dbs_ref, dvn_scr, gelu_grad_v):
        @pl.when(pl.program_id(0) == 0)
        def _():
            dws_ref[...] = jnp.zeros_like(dws_ref)
            dbs_ref[...] = jnp.zeros_like(dbs_ref)

        gv, lng = g_ref[...], lng_ref[...]
        y, xhat, rstd = _rms_fwd(x_ref[...], gv)
        hn_ref[...] = y.astype(BF16)
        dhv = dh_ref[...]
        dgated = _dot_nt(dhv.astype(BF16), wout_ref[...])
        u, gelu_grad_u = _gelu_and_grad(z_ref[:, :GATE_DIM])
        v, gelu_grad_v[...] = _gelu_and_grad(z_ref[:, GATE_DIM:])
        vn, vhat, lrstd = _ln_fwd(v, lng, lnb_ref[...])
        vb = vn.astype(BF16)
        mask = _gate_mask()
        for gi in range(A_GROUPS):
            wm = jnp.where(mask, ws_ref[gi], 0.0).astype(BF16)
            bias = bst_ref[:, gi:gi + 1]
            cs = slice(gi * A_GROUP_DIM, (gi + 1) * A_GROUP_DIM)
            dws = jnp.zeros((GMLP_BLOCK, GMLP_BLOCK), F32)
            dbs = jnp.zeros((GMLP_BLOCK, 1), F32)
            for n in range(nblk):
                rs = slice(n * GMLP_BLOCK, (n + 1) * GMLP_BLOCK)
                sv = _dot(wm, vb[rs, cs]) + bias
                dz_ref[rs, cs] = (dgated[rs, cs] * sv * gelu_grad_u[rs, cs]).astype(BF16)
                dsv = dgated[rs, cs] * u[rs, cs]
                dsvb = dsv.astype(BF16)
                dws = dws + _dot_nt(dsvb, vb[rs, cs])
                dbs = dbs + jnp.sum(dsv, axis=-1, keepdims=True)
                dvn_scr[rs, cs] = _dot_tn(wm, dsvb)
            dws_ref[gi] += jnp.where(mask, dws, 0.0)
            dbs_ref[gi] += dbs
        dvn = dvn_scr[...]
        dvhat = dvn * lng
        dv = lrstd * (dvhat - jnp.mean(dvhat, axis=-1, keepdims=True)
                      - vhat * jnp.mean(dvhat * vhat, axis=-1, keepdims=True))
        dz_ref[:, GATE_DIM:] = (dv * gelu_grad_v[...]).astype(BF16)
        dhn = jnp.zeros((tm, D_MODEL), F32)
        for d in range(N_DEV):
            dhn = dhn + _dot_nt(dz_ref[:, d * FF_SLOT:(d + 1) * FF_SLOT], win_ref[d])
        dx, dg = _rms_bwd(dhn, xhat, rstd, gv)
        dx_ref[...] = dhv + dx
        _acc(dg_ref, dg)
        _acc(dlng_ref, jnp.sum(dvn * vhat, axis=0, keepdims=True))
        _acc(dlnb_ref, jnp.sum(dvn, axis=0, keepdims=True))

    return _call(
        "a_mix_bwd", body, (t // tm,),
        [_row(D_MODEL, tm), _row(2 * GATE_DIM, tm), _row(D_MODEL, tm), _res((1, D_MODEL)),
         _res((N_DEV, D_MODEL, FF_SLOT)), _res((1, GATE_DIM)), _res((1, GATE_DIM)),
         _res((A_GROUPS, GMLP_BLOCK, GMLP_BLOCK)), _res((GMLP_BLOCK, A_GROUPS)), _res((GATE_DIM, D_MODEL))],
        [_row(D_MODEL, tm), _row(D_MODEL, tm), _row(2 * GATE_DIM, tm),
         _const((1, D_MODEL)), _const((1, GATE_DIM)), _const((1, GATE_DIM)),
         _const((A_GROUPS, GMLP_BLOCK, GMLP_BLOCK)), _const((A_GROUPS, GMLP_BLOCK, 1))],
        [_sds((t, D_MODEL), F32), _sds((t, D_MODEL), BF16),
         _sds((t, 2 * GATE_DIM), BF16), _sds((1, D_MODEL), F32), _sds((1, GATE_DIM), F32),
         _sds((1, GATE_DIM), F32), _sds((A_GROUPS, GMLP_BLOCK, GMLP_BLOCK), F32),
         _sds((A_GROUPS, GMLP_BLOCK, 1), F32)],
        (x, z, dh, g, w_in, ln_g, ln_b, w_s, b_st, w_out),
        scratch=[pltpu.VMEM((tm, GATE_DIM), F32), pltpu.VMEM((tm, GATE_DIM), F32)], comm=comm)


def _wgrad(name, a, b, a_spec, b_spec, m, n, comm=None):
    def body(a_ref, b_ref, o_ref):
        o_ref[0] = _dot_tn(a_ref[...].astype(BF16), b_ref[...].astype(BF16)).astype(BF16)

    outs, got = _call(name, body, (N_DEV,), [a_spec, b_spec], [pl.BlockSpec((1, m, n), lambda d: (d, 0, 0))],
                      [_sds((N_DEV, m, n), BF16)], (a, b), comm=comm)
    return outs[0] if comm is None else (outs[0], got)


def _full(t, d):
    return pl.BlockSpec((t, d), lambda i: (0, 0), pipeline_mode=pl.Buffered(1))


def _cols(t, d):
    return pl.BlockSpec((t, d), lambda i: (0, i))


def _head(t, d):
    return pl.BlockSpec((None, t, d), lambda i: (i, 0, 0))


def _local_step(x, pos, target, inv_freq, wg, sm, shards=None):
    t = x.shape[0]
    wg = dict(wg)
    dist = shards is not None
    mix_g = [sm["norm_mix_g"][l:l + 1] for l in range(2)]
    mlp_g = [sm["norm_mlp_g"][l:l + 1] for l in range(2)]

    def gather(names):
        return _gather_comm([shards[k] for k in names]) if dist else None

    def send(grads):
        return _exchange_comm(grads=grads) if dist else None

    def send_sums(name, grads):
        return _chip_exchange_comm(_pair_reduce(name, grads)) if dist else None

    def a_args():
        return (wg["a_w_in"], wg["a_ln_v_g"], wg["a_ln_v_b"], sm["a_w_s"], sm["a_b_st"], wg["a_w_out"])

    def kvq_w():
        return (sm["kv_src_norm_g"], wg["kv_w_a"], sm["kv_a_norm_g"], wg["kv_w_b"], mix_g[1], wg["b_w_q_a"],
                sm["b_q_norm_g"], wg["b_w_q_b"])

    names = ("mlp_w1_0", "mlp_w2_0")
    (h1, z, gated), got = _a_mix_fwd(x, mix_g[0], *a_args(), comm=gather(names))
    wg.update(zip(names, got))
    names = ("kv_w_a", "kv_w_b", "b_w_q_a", "b_w_q_b", "b_w_o")
    (h2, a0), got = _mlp_fwd(h1, mlp_g[0], wg["mlp_w1_0"], wg["mlp_w2_0"], 0, comm=gather(names))
    wg.update(zip(names, got))
    if dist:
        wg["b_w_q_a"] = wg["b_w_q_a"].reshape(D_MODEL, Q_LORA)
        wg["kv_w_a"] = wg["kv_w_a"].reshape(D_MODEL, KV_LORA + QK_ROPE)
    ckv, k, v, cqpre, q, cos, sin = _kvq_fwd(h2, pos, inv_freq, kvq_w())
    names = ("mlp_w1_1", "mlp_w2_1")
    (h3, att), got = _attn_fwd(h2, q, k, v, wg["b_w_o"], comm=gather(names))
    wg.update(zip(names, got))
    a1, loss, dh4, d_final_g = _mlp_fwd_loss(h3, mlp_g[1], wg["mlp_w1_1"], wg["mlp_w2_1"], sm["final_norm_g"], target)

    g = {}
    (dh3, d_mlp_g1, hn, f, da, dh3_b), _ = _mlp_bwd(h3, a1, dh4, mlp_g[1], wg["mlp_w1_1"], wg["mlp_w2_1"], 1)
    g["mlp_w1_1"] = _wgrad("wgrad_w1_1", hn, da, _full(t, D_MODEL), _cols(t, FF_SLOT), D_MODEL, FF_SLOT)
    g["mlp_w2_1"] = _wgrad("wgrad_w2_1", f, dh4, _cols(t, FF_SLOT), _full(t, D_MODEL), FF_SLOT, D_MODEL)
    g["b_w_o"] = _wgrad("wgrad_w_o", att, dh3_b, _head(t, V_HEAD), _full(t, D_MODEL), V_HEAD, D_MODEL)
    names = ("mlp_w1_1", "mlp_w2_1", "b_w_o")
    (dq, dk, dv), got = _attn_bwd(dh3_b, q, k, v, wg["b_w_o"], cos, sin,
                                         comm=send_sums("pair_reduce_1", [g[k] for k in names]))
    g.update(zip(names, got))
    (dh2, hq, hk, cq, dcqpre, c, dkv, dckv, d_mix_g1, d_src_g, d_q_g, d_kv_a_g) = _kvq_bwd(
        h2, dh3, ckv, cqpre, dq, dk, dv, cos, sin, kvq_w())
    g["b_w_q_a"] = _wgrad("wgrad_w_q_a", hq, dcqpre, _cols(t, D_MODEL // N_DEV), _full(t, Q_LORA),
                          D_MODEL // N_DEV, Q_LORA)
    g["b_w_q_b"] = _wgrad("wgrad_w_q_b", cq, dq, _full(t, Q_LORA), _head(t, QK_NOPE + QK_ROPE),
                          Q_LORA, QK_NOPE + QK_ROPE)
    g["kv_w_a"] = _wgrad("wgrad_kv_w_a", hk, dckv, _cols(t, D_MODEL // N_DEV), _full(t, KV_LORA + QK_ROPE),
                         D_MODEL // N_DEV, KV_LORA + QK_ROPE)
    g["kv_w_b"] = _wgrad("wgrad_kv_w_b", c, dkv, _full(t, KV_LORA), _head(t, QK_NOPE + V_HEAD),
                         KV_LORA, QK_NOPE + V_HEAD)
    names = ("b_w_q_a", "b_w_q_b", "kv_w_a", "kv_w_b")
    (dh1, d_mlp_g0, hn, f, da, dh1_b), got = _mlp_bwd(h1, a0, dh2, mlp_g[0], wg["mlp_w1_0"], wg["mlp_w2_0"], 0,
                                                      comm=send([g[k] for k in names]))
    g.update(zip(names, got))
    g["mlp_w1_0"] = _wgrad("wgrad_w1_0", hn, da, _full(t, D_MODEL), _cols(t, FF_SLOT), D_MODEL, FF_SLOT)
    g["mlp_w2_0"] = _wgrad("wgrad_w2_0", f, dh2, _cols(t, FF_SLOT), _full(t, D_MODEL), FF_SLOT, D_MODEL)
    g["a_w_out"] = _wgrad("wgrad_a_w_out", gated, dh1_b, _cols(t, GATE_DIM // N_DEV), _full(t, D_MODEL),
                          GATE_DIM // N_DEV, D_MODEL)
    names = ("mlp_w1_0", "mlp_w2_0")
    (dx, hn, dz, d_mix_g0, d_ln_g, d_ln_b, d_ws, d_bs), got = _a_mix_bwd(
        x, z, dh1, mix_g[0], *a_args(), comm=send_sums("pair_reduce_0", [g[k] for k in names]))
    g.update(zip(names, got))
    small = {
        "norm_mix_g": jnp.concatenate([d_mix_g0, d_mix_g1], axis=0),
        "norm_mlp_g": jnp.concatenate([d_mlp_g0, d_mlp_g1], axis=0),
        "a_ln_v_g": d_ln_g.reshape(N_DEV, GATE_DIM // N_DEV),
        "a_ln_v_b": d_ln_b.reshape(N_DEV, GATE_DIM // N_DEV),
        "a_w_s": d_ws.astype(BF16) if dist else d_ws,
        "a_b_s": d_bs.reshape(A_GROUPS, GMLP_BLOCK),
        "b_q_norm_g": d_q_g,
        "kv_src_norm_g": d_src_g,
        "kv_a_norm_g": d_kv_a_g,
        "final_norm_g": d_final_g,
    }
    wgrad_in = ("wgrad_a_w_in", hn, dz, _full(t, D_MODEL), _cols(t, FF_SLOT), D_MODEL, FF_SLOT)
    if dist:
        parts = [small[k].reshape((1,) + small[k].shape) for k in SMALL] + [loss.reshape(1, 1, 1)]
        g["a_w_in"], got = _wgrad(*wgrad_in, comm=_exchange_comm(parts=parts))
        small, loss = dict(zip(SMALL, got)), got[-1]
    else:
        g["a_w_in"] = _wgrad(*wgrad_in)
    return loss, dx, g, small


def _adamw(w, g, m, v):
    m = ADAM_B1 * m + (1.0 - ADAM_B1) * g
    v = ADAM_B2 * v + (1.0 - ADAM_B2) * (g * g)
    m_hat = m / (1.0 - ADAM_B1 ** ADAM_STEP)
    v_hat = v / (1.0 - ADAM_B2 ** ADAM_STEP)
    return -ADAM_LR * (m_hat / (jnp.sqrt(v_hat) + ADAM_EPS) + ADAM_WD * w), m, v


def _sum_in_device_order(r_ref):
    g = r_ref[0].astype(F32)
    for j in range(1, r_ref.shape[0]):
        g = g + r_ref[j].astype(F32)
    return g


def _adamw_sharded(name, recvs, w, m, v, comm=None):
    layers, r, c = w.shape
    tr = math.gcd(r, 256)
    flat = [a for per_layer in recvs for a in per_layer]

    def body(*refs):
        r_refs, (w_ref, m_ref, v_ref) = refs[:len(flat)], refs[len(flat):len(flat) + 3]
        g_ref, d_ref, nm_ref, nv_ref = refs[-4:]
        layer = pl.program_id(0)
        g, pos = None, 0
        for li, per_layer in enumerate(recvs):
            total = None
            for ref in r_refs[pos:pos + len(per_layer)]:
                part = _sum_in_device_order(ref)
                total = part if total is None else total + part
            pos += len(per_layer)
            g = total if g is None else jnp.where(layer == li, total, g)
        g_ref[...] = g
        d_ref[...], nm_ref[...], nv_ref[...] = _adamw(w_ref[...], g, m_ref[...], v_ref[...])

    blk = pl.BlockSpec((None, tr, c), lambda l, i: (l, i, 0))
    return _call(name, body, (layers, r // tr),
                 [pl.BlockSpec((a.shape[0], tr, c), lambda l, i: (0, i, 0)) for a in flat] + [blk] * 3,
                 [blk] * 4, [_sds(w.shape, F32)] * 4, (*flat, w, m, v), comm=comm)


def _adamw_small(recvs, ws, ms, vs, own_row, losses):
    n = len(recvs)

    def body(*refs):
        r_refs, w_refs, m_refs, v_refs = (refs[i * n:(i + 1) * n] for i in range(4))
        outs, scr = refs[4 * n + 1:8 * n + 2], refs[8 * n + 2:]
        outs[-1][...] = _sum_in_device_order(refs[4 * n])
        me = _my_place()[3]
        for a in range(n):
            g = _sum_in_device_order(r_refs[a])
            if own_row[a]:
                scr[0][...] = g
                g = scr[0][pl.ds(me, 1), :]
            g_ref, d_ref, nm_ref, nv_ref = outs[4 * a:4 * a + 4]
            g_ref[...] = g
            d_ref[...], nm_ref[...], nv_ref[...] = _adamw(w_refs[a][...], g, m_refs[a][...], v_refs[a][...])

    out_shape = []
    for w in ws:
        out_shape += [_sds(w.shape, F32)] * 4
    return pl.pallas_call(
        body, name="adamw_small", in_specs=[VMEM] * (4 * n + 1), out_specs=[VMEM] * (4 * n + 1),
        out_shape=out_shape + [_sds((1, 1), F32)], scratch_shapes=[pltpu.VMEM((N_DEV, GATE_DIM // N_DEV), F32)],
    )(*recvs, *ws, *ms, *vs, losses)


BIG = ("a_w_in", "a_w_out", "b_w_q_a", "b_w_q_b", "b_w_o", "kv_w_a", "kv_w_b", "mlp_w1", "mlp_w2")
SMALL = ("norm_mix_g", "norm_mlp_g", "a_ln_v_g", "a_ln_v_b", "a_w_s", "a_b_s", "b_q_norm_g", "kv_src_norm_g",
         "kv_a_norm_g", "final_norm_g")
WEIGHTS = ("norm_mix_g", "norm_mlp_g", "a_w_in", "a_ln_v_g", "a_ln_v_b", "a_w_s", "a_b_s", "a_w_out", "b_w_q_a",
           "b_q_norm_g", "b_w_q_b", "b_w_o", "kv_src_norm_g", "kv_w_a", "kv_a_norm_g", "kv_w_b", "mlp_w1", "mlp_w2",
           "final_norm_g")


def _two_d(name, a):
    if name in ("a_w_s", "a_b_s"):
        return a.reshape(a.shape[1:])
    return a.reshape(1, -1) if a.ndim == 1 else a


def _three_d(a):
    return a if a.ndim == 3 else a.reshape((1,) + a.shape)


def kernel(x, positions, norm_mix_g, norm_mlp_g, a_w_in, a_ln_v_g, a_ln_v_b, a_w_s, a_b_s, a_w_out, b_w_q_a, b_q_norm_g, b_w_q_b, b_w_o, kv_src_norm_g, kv_w_a, kv_a_norm_g, kv_w_b, mlp_w1, mlp_w2, final_norm_g, loss_target, m_norm_mix_g, m_norm_mlp_g, m_a_w_in, m_a_ln_v_g, m_a_ln_v_b, m_a_w_s, m_a_b_s, m_a_w_out, m_b_w_q_a, m_b_q_norm_g, m_b_w_q_b, m_b_w_o, m_kv_src_norm_g, m_kv_w_a, m_kv_a_norm_g, m_kv_w_b, m_mlp_w1, m_mlp_w2, m_final_norm_g, v_norm_mix_g, v_norm_mlp_g, v_a_w_in, v_a_ln_v_g, v_a_ln_v_b, v_a_w_s, v_a_b_s, v_a_w_out, v_b_w_q_a, v_b_q_norm_g, v_b_w_q_b, v_b_w_o, v_kv_src_norm_g, v_kv_w_a, v_kv_a_norm_g, v_kv_w_b, v_mlp_w1, v_mlp_w2, v_final_norm_g):
    w = dict(norm_mix_g=norm_mix_g, norm_mlp_g=norm_mlp_g, a_w_in=a_w_in, a_ln_v_g=a_ln_v_g, a_ln_v_b=a_ln_v_b,
             a_w_s=a_w_s, a_b_s=a_b_s, a_w_out=a_w_out, b_w_q_a=b_w_q_a, b_q_norm_g=b_q_norm_g, b_w_q_b=b_w_q_b,
             b_w_o=b_w_o, kv_src_norm_g=kv_src_norm_g, kv_w_a=kv_w_a, kv_a_norm_g=kv_a_norm_g, kv_w_b=kv_w_b,
             mlp_w1=mlp_w1, mlp_w2=mlp_w2, final_norm_g=final_norm_g)
    m = dict(norm_mix_g=m_norm_mix_g, norm_mlp_g=m_norm_mlp_g, a_w_in=m_a_w_in, a_ln_v_g=m_a_ln_v_g,
             a_ln_v_b=m_a_ln_v_b, a_w_s=m_a_w_s, a_b_s=m_a_b_s, a_w_out=m_a_w_out, b_w_q_a=m_b_w_q_a,
             b_q_norm_g=m_b_q_norm_g, b_w_q_b=m_b_w_q_b, b_w_o=m_b_w_o, kv_src_norm_g=m_kv_src_norm_g,
             kv_w_a=m_kv_w_a, kv_a_norm_g=m_kv_a_norm_g, kv_w_b=m_kv_w_b, mlp_w1=m_mlp_w1, mlp_w2=m_mlp_w2,
             final_norm_g=m_final_norm_g)
    v = dict(norm_mix_g=v_norm_mix_g, norm_mlp_g=v_norm_mlp_g, a_w_in=v_a_w_in, a_ln_v_g=v_a_ln_v_g,
             a_ln_v_b=v_a_ln_v_b, a_w_s=v_a_w_s, a_b_s=v_a_b_s, a_w_out=v_a_w_out, b_w_q_a=v_b_w_q_a,
             b_q_norm_g=v_b_q_norm_g, b_w_q_b=v_b_w_q_b, b_w_o=v_b_w_o, kv_src_norm_g=v_kv_src_norm_g,
             kv_w_a=v_kv_w_a, kv_a_norm_g=v_kv_a_norm_g, kv_w_b=v_kv_w_b, mlp_w1=v_mlp_w1, mlp_w2=v_mlp_w2,
             final_norm_g=v_final_norm_g)
    t = x.shape[1]

    first = ("a_w_in", "a_w_out", "a_ln_v_g", "a_ln_v_b")
    later = ("mlp_w1_0", "mlp_w2_0", "mlp_w1_1", "mlp_w2_1", "kv_w_a", "kv_w_b", "b_w_q_a", "b_w_q_b", "b_w_o")
    blocks = {k: _three_d(w[k]) for k in BIG if not k.startswith("mlp")}
    for k in ("mlp_w1", "mlp_w2"):
        blocks[k + "_0"], blocks[k + "_1"] = w[k][0:1], w[k][1:2]
    got, casts = _gather_first([blocks[k] if k in blocks else w[k] for k in first], [blocks[k] for k in later])
    wg = dict(zip(first, got))
    wg["a_w_out"] = wg["a_w_out"].reshape(GATE_DIM, D_MODEL)
    wg["a_ln_v_g"] = wg["a_ln_v_g"].reshape(1, GATE_DIM)
    wg["a_ln_v_b"] = wg["a_ln_v_b"].reshape(1, GATE_DIM)
    shards = dict(zip(later, casts))

    sm = {k: _two_d(k, w[k]) for k in SMALL if k not in ("a_ln_v_g", "a_ln_v_b")}
    sm["a_b_st"] = sm["a_b_s"].T
    inv_freq = (ROPE_THETA ** (-jnp.arange(0, QK_ROPE, 2, dtype=F32) / QK_ROPE)).reshape(1, QK_ROPE // 2)

    losses, dx, g, small = _local_step(x[0], positions.reshape(t, 1), loss_target[0], inv_freq, wg, sm, shards)

    names = ("a_w_in", "a_w_out")
    sums = _pair_reduce("pair_reduce_a", [g[k] for k in names])
    g.update(zip(names, _comm_only("exchange_last", _chip_exchange_comm(sums))))

    out = {}
    for k in BIG:
        recvs = [[g[k + "_0"]], [g[k + "_1"]]] if k.startswith("mlp") else [[g[k]]]
        res, _ = _adamw_sharded("adamw_" + k, recvs, _three_d(w[k]), _three_d(m[k]), _three_d(v[k]))
        out[k] = [o.reshape(w[k].shape) for o in res]
    own_row = [k in ("a_ln_v_g", "a_ln_v_b") for k in SMALL]
    res = _adamw_small([small[k] for k in SMALL], [_two_d(k, w[k]) for k in SMALL], [_two_d(k, m[k]) for k in SMALL],
                       [_two_d(k, v[k]) for k in SMALL], own_row, losses)
    for i, k in enumerate(SMALL):
        out[k] = [o.reshape(w[k].shape) for o in res[4 * i:4 * i + 4]]

    return (res[-1].reshape(()), dx.reshape(x.shape), *[out[k][0] for k in WEIGHTS], *[out[k][1] for k in WEIGHTS],
            *[out[k][2] for k in WEIGHTS], *[out[k][3] for k in WEIGHTS])
```

```python
import math

import jax
import jax.numpy as jnp
from jax import lax
from jax.experimental import pallas as pl
from jax.experimental.pallas import tpu as pltpu

F32, BF16 = jnp.float32, jnp.bfloat16
MESH = pl.DeviceIdType.MESH
ANY = pl.BlockSpec(memory_space=pl.ANY)
VMEM = pl.BlockSpec(memory_space=pltpu.VMEM)

N_DEV = 8
D_MODEL = 1024
CHUNK = 64
GMLP_BLOCK = 128
GATE_DIM = 2048
A_GROUPS = 8
A_GROUP_DIM = GATE_DIM // A_GROUPS
B_HEADS = 8
QK_NOPE, QK_ROPE, V_HEAD = 128, 64, 128
Q_LORA, KV_LORA = 384, 256
ROPE_THETA = 10000.0
D_FF = 4096
FF_SLOT = D_FF // N_DEV
EPS = 1e-6
ATT_SCALE = (QK_NOPE + QK_ROPE) ** -0.5

ADAM_LR, ADAM_B1, ADAM_B2, ADAM_EPS, ADAM_WD, ADAM_STEP = 0.001, 0.9, 0.999, 1e-08, 0.01, 10

TM = 256
TM_GATE = 128
VMEM_LIMIT = 56 * 1024 * 1024
INV_SQRT2 = 1.0 / math.sqrt(2.0)
INV_SQRT_2PI = 1.0 / math.sqrt(2.0 * math.pi)
LOG2_E = 1.0 / math.log(2.0)
HEADS_PER_STEP = 4


def _dot(a, b):
    return jnp.dot(a, b, preferred_element_type=F32)


def _dot_nt(a, b):
    return lax.dot_general(a, b, (((1,), (1,)), ((), ())), preferred_element_type=F32)


def _dot_tn(a, b):
    return lax.dot_general(a, b, (((0,), (0,)), ((), ())), preferred_element_type=F32)


def _rms_fwd(x, g):
    rstd = lax.rsqrt(jnp.mean(x * x, axis=-1, keepdims=True) + EPS)
    xhat = x * rstd
    return xhat * g, xhat, rstd


def _rms_bwd(dy, xhat, rstd, g):
    dxhat = dy * g
    dx = rstd * (dxhat - xhat * jnp.mean(dxhat * xhat, axis=-1, keepdims=True))
    return dx, jnp.sum(dy * xhat, axis=0, keepdims=True)


def _ln_fwd(v, g, b):
    mu = jnp.mean(v, axis=-1, keepdims=True)
    vc = v - mu
    rstd = lax.rsqrt(jnp.mean(vc * vc, axis=-1, keepdims=True) + EPS)
    vhat = vc * rstd
    return vhat * g + b, vhat, rstd


def _gelu(x):
    return 0.5 * x * (1.0 + lax.erf(x * INV_SQRT2))


def _gelu_and_grad(x):
    cdf = 0.5 * (1.0 + lax.erf(x * INV_SQRT2))
    return x * cdf, cdf + x * jnp.exp(-0.5 * x * x) * INV_SQRT_2PI


def _rope(x, cos, sin):
    x1, x2 = x[:, :QK_ROPE // 2], x[:, QK_ROPE // 2:]
    return jnp.concatenate([x1 * cos - x2 * sin, x2 * cos + x1 * sin], axis=-1)


def _gate_mask():
    row = lax.broadcasted_iota(jnp.int32, (GMLP_BLOCK, GMLP_BLOCK), 0)
    col = lax.broadcasted_iota(jnp.int32, (GMLP_BLOCK, GMLP_BLOCK), 1)
    return (col < CHUNK) | (row >= CHUNK)


def _att_mask(q0, tq, t):
    q = q0 + lax.broadcasted_iota(jnp.int32, (tq, t), 0)
    k = lax.broadcasted_iota(jnp.int32, (tq, t), 1)
    return jnp.right_shift(k, 6) <= jnp.right_shift(q, 6)


def _res(shape, imap=None):
    zeros = (0,) * len(shape)
    return pl.BlockSpec(shape, imap or (lambda i: zeros), pipeline_mode=pl.Buffered(1))


def _const(shape):
    zeros = (0,) * len(shape)
    return pl.BlockSpec(shape, lambda i: zeros)


def _row(d, tm=TM):
    return pl.BlockSpec((tm, d), lambda i: (i, 0))


def _heads(d):
    return pl.BlockSpec((B_HEADS, TM, d), lambda i: (0, i, 0))


def _sds(shape, dt):
    return jax.ShapeDtypeStruct(shape, dt)


def _acc(ref, val):
    @pl.when(pl.program_id(0) == 0)
    def _():
        ref[...] = jnp.zeros_like(ref)
    ref[...] += val


def _my_place():
    x, y, c = lax.axis_index("x"), lax.axis_index("y"), lax.axis_index("c")
    return x, y, c, 4 * x + 2 * y + c


def _peer(x, y, c, k):
    px = 1 - x if k & 4 else x
    py = 1 - y if k & 2 else y
    pc = 1 - c if k & 1 else c
    return (px, py, pc), 4 * px + 2 * py + pc


CHIPS = (2, 4, 6)


def _splits(ref):
    return len(ref.shape) >= 3 and ref.shape[1] % 32 == 0


def _piece(ref, block, half=None):
    if half is None or not _splits(ref):
        return ref.at[pl.ds(block, 1)]
    rows = ref.shape[1] // 2
    return ref.at[pl.ds(block, 1), pl.ds(half * rows, rows)]


def _gather_copy(sems, a, k, piece, to, src=None):
    return pltpu.make_async_remote_copy(
        src_ref=piece if src is None else src, dst_ref=piece, send_sem=sems[0].at[a, k], recv_sem=sems[1].at[a, k],
        device_id=to, device_id_type=MESH)


def _gather_start(srcs, outs, sems, only=None):
    x, y, c, me = _my_place()
    for a in range(len(srcs)) if only is None else (only,):
        mine = _piece(outs[a], me)
        pltpu.make_async_copy(srcs[a], mine, sems[2].at[a]).start()
        for k, rel in enumerate((1, 4, 2)):
            _gather_copy(sems, a, k, mine, _peer(x, y, c, rel)[0], src=srcs[a]).start()


def _gather_relay(srcs, outs, sems):
    x, y, c, _ = _my_place()
    sib = _peer(x, y, c, 1)[0]
    (xn, xn_i), (yn, yn_i) = _peer(x, y, c, 4), _peer(x, y, c, 2)
    for a in range(len(srcs)):
        out = outs[a]
        _gather_copy(sems, a, 1, _piece(out, xn_i), xn).wait_recv()
        _gather_copy(sems, a, 3, _piece(out, xn_i, 0), yn).start()
        _gather_copy(sems, a, 5, _piece(out, xn_i), sib).start()
        _gather_copy(sems, a, 2, _piece(out, yn_i), yn).wait_recv()
        if _splits(out):
            _gather_copy(sems, a, 4, _piece(out, yn_i, 1), xn).start()
        _gather_copy(sems, a, 6, _piece(out, yn_i), sib).start()


def _gather_finish(srcs, outs, sems):
    x, y, c, me = _my_place()
    sib = _peer(x, y, c, 1)[0]
    xn, yn, dg_i = _peer(x, y, c, 4)[0], _peer(x, y, c, 2)[0], _peer(x, y, c, 6)[1]
    n = len(srcs)
    for a in range(n):
        out = outs[a]
        _gather_copy(sems, a, 3, _piece(out, dg_i, 0), yn).wait_recv()
        _gather_copy(sems, a, 7, _piece(out, dg_i, 0), sib).start()
        if _splits(out):
            _gather_copy(sems, a, 4, _piece(out, dg_i, 1), xn).wait_recv()
            _gather_copy(sems, a, 8, _piece(out, dg_i, 1), sib).start()
    for a in range(n):
        out = outs[a]
        whole, half = _piece(out, me), _piece(out, me, 0)
        for k in (0, 5, 6):
            _gather_copy(sems, a, k, whole, sib).wait_recv()
        for k in (7, 8) if _splits(out) else (7,):
            _gather_copy(sems, a, k, half, sib).wait_recv()
        for k in (0, 1, 2):
            _gather_copy(sems, a, k, whole, sib, src=srcs[a]).wait_send()
        for k in (5, 6):
            _gather_copy(sems, a, k, whole, sib).wait_send()
        for k in (3, 4, 7, 8) if _splits(out) else (3, 7):
            _gather_copy(sems, a, k, half, sib).wait_send()
        pltpu.make_async_copy(srcs[a], whole, sems[2].at[a]).wait()


def _relay_sems(n):
    return [pltpu.SemaphoreType.DMA((n, 9)), pltpu.SemaphoreType.DMA((n, 9)), pltpu.SemaphoreType.DMA((n,))]


def _gather_sems(n):
    return [pltpu.SemaphoreType.DMA((n, 7)), pltpu.SemaphoreType.DMA((n, 7)), pltpu.SemaphoreType.DMA((n,))]


class _Comm:
    def __init__(self, args, out_shape, scratch, start, finish, relay=None):
        self.args, self.out_shape, self.scratch, self.start, self.finish = args, out_shape, scratch, start, finish
        self.relay = relay


def _gather_comm(shards):
    return _Comm(list(shards), [_sds((N_DEV,) + s.shape[1:], s.dtype) for s in shards], _relay_sems(len(shards)),
                 _gather_start, _gather_finish, relay=_gather_relay)


def _direct_copies(ins, outs, sems, wait, from_block):
    send_sems, recv_sems, local_sems = sems
    x, y, c, me = _my_place()
    for a in range(len(ins)):
        src = ins[a].at[pl.ds(me, 1)] if from_block[a] else ins[a]
        local = pltpu.make_async_copy(src, outs[a].at[pl.ds(me, 1)], local_sems.at[a])
        local.wait() if wait else local.start()
        for k in range(1, N_DEV):
            to, to_i = _peer(x, y, c, k)
            cp = pltpu.make_async_remote_copy(
                src_ref=ins[a].at[pl.ds(to_i, 1)] if from_block[a] else ins[a], dst_ref=outs[a].at[pl.ds(me, 1)],
                send_sem=send_sems.at[a, k - 1], recv_sem=recv_sems.at[a, k - 1], device_id=to, device_id_type=MESH)
            cp.wait() if wait else cp.start()


def _exchange_comm(grads=(), parts=()):
    ins = list(grads) + list(parts)
    from_block = [True] * len(grads) + [False] * len(parts)
    out_shape = [_sds(g.shape, g.dtype) for g in grads] + [_sds((N_DEV,) + p.shape[1:], p.dtype) for p in parts]

    def start(ins_, outs_, sems_):
        _direct_copies(ins_, outs_, sems_, False, from_block)

    def finish(ins_, outs_, sems_):
        _direct_copies(ins_, outs_, sems_, True, from_block)

    return _Comm(ins, out_shape, _gather_sems(len(ins)), start, finish)


def _chip_copies(ins, outs, sems, wait, rels, own):
    send_sems, recv_sems, local_sems = sems
    x, y, c, _ = _my_place()
    for a in range(len(ins)):
        if own:
            local = pltpu.make_async_copy(ins[a].at[pl.ds(2 * x + y, 1)], outs[a].at[pl.ds(len(rels), 1)],
                                          local_sems.at[a])
            local.wait() if wait else local.start()
        for i, j in enumerate(rels):
            to = _peer(x, y, c, CHIPS[j])[0]
            cp = pltpu.make_async_remote_copy(
                src_ref=ins[a].at[pl.ds(2 * to[0] + to[1], 1)], dst_ref=outs[a].at[pl.ds(i, 1)],
                send_sem=send_sems.at[a, i], recv_sem=recv_sems.at[a, i], device_id=to, device_id_type=MESH)
            cp.wait() if wait else cp.start()


def _chip_exchange_comm(sums, rels=(0, 1, 2), own=True):
    def start(ins_, outs_, sems_):
        _chip_copies(ins_, outs_, sems_, False, rels, own)

    def finish(ins_, outs_, sems_):
        _chip_copies(ins_, outs_, sems_, True, rels, own)

    n = len(sums)
    sems = [pltpu.SemaphoreType.DMA((n, len(rels))), pltpu.SemaphoreType.DMA((n, len(rels))),
            pltpu.SemaphoreType.DMA((n,))]
    return _Comm(list(sums), [_sds((len(rels) + own,) + s.shape[1:], s.dtype) for s in sums], sems, start, finish)


def _pair_reduce(name, grads):
    n = len(grads)
    n_chips = N_DEV // 2

    def body(*refs):
        g_refs, gh_refs, p_refs, land = refs[:n], refs[n:2 * n], refs[2 * n:3 * n], refs[3 * n:4 * n]
        send_sems, recv_sems = refs[4 * n:]
        x, y, c, _ = _my_place()
        sib = _peer(x, y, c, 1)[0]
        q = pl.program_id(0)

        def to_sibling(a, j):
            return pltpu.make_async_remote_copy(
                src_ref=gh_refs[a].at[j, pl.ds(1 - c, 1)], dst_ref=land[a].at[pl.ds(j, 1)],
                send_sem=send_sems.at[a, j], recv_sem=recv_sems.at[a, j], device_id=sib, device_id_type=MESH)

        @pl.when(q == 0)
        def _():
            for j in range(n_chips):
                for a in range(n):
                    to_sibling(a, j).start()

        for a in range(n):
            to_sibling(a, q).wait_recv()
            p_refs[a][...] = (g_refs[a][0, pl.ds(c, 1)].astype(F32) + land[a][pl.ds(q, 1)].astype(F32)).astype(BF16)

        @pl.when(q == n_chips - 1)
        def _():
            for a in range(n):
                for j in range(n_chips):
                    to_sibling(a, j).wait_send()

    views = [g.reshape((n_chips, 2) + g.shape[1:]) for g in grads]
    res = pl.pallas_call(
        body, name=name, grid=(n_chips,),
        in_specs=[pl.BlockSpec((1, 2) + g.shape[1:], lambda q: (q, 0, 0, 0)) for g in grads] + [ANY] * n,
        out_specs=[pl.BlockSpec((1,) + g.shape[1:], lambda q: (q, 0, 0)) for g in grads],
        out_shape=[_sds((n_chips,) + g.shape[1:], BF16) for g in grads],
        scratch_shapes=[pltpu.VMEM((n_chips,) + g.shape[1:], BF16) for g in grads]
        + [pltpu.SemaphoreType.DMA((n, n_chips)), pltpu.SemaphoreType.DMA((n, n_chips))],
        compiler_params=pltpu.CompilerParams(dimension_semantics=("arbitrary",), vmem_limit_bytes=VMEM_LIMIT),
    )(*views, *views)
    return list(res)


def _call(name, body, grid, in_specs, out_specs, out_shape, args, scratch=(), comm=None):
    params = pltpu.CompilerParams(dimension_semantics=("arbitrary",) * len(grid), vmem_limit_bytes=VMEM_LIMIT)
    if comm is None:
        outs = pl.pallas_call(body, name=name, grid=grid, in_specs=list(in_specs), out_specs=list(out_specs),
                              out_shape=list(out_shape), scratch_shapes=list(scratch), compiler_params=params)(*args)
        return list(outs), []
    ni, nci, no, nco, ns = len(in_specs), len(comm.args), len(out_specs), len(comm.out_shape), len(scratch)

    def carrying(*refs):
        ins, refs = refs[:ni], refs[ni:]
        cin, refs = refs[:nci], refs[nci:]
        outs, refs = refs[:no], refs[no:]
        cout, refs = refs[:nco], refs[nco:]
        scr, csems = refs[:ns], refs[ns:]
        step = pl.program_id(0)
        for ax in range(1, len(grid)):
            step = step * grid[ax] + pl.program_id(ax)
        steps = math.prod(grid)

        @pl.when(step == 0)
        def _():
            comm.start(cin, cout, csems)

        if comm.relay is not None:
            @pl.when(step == (2 * steps) // 3)
            def _():
                comm.relay(cin, cout, csems)

        body(*ins, *outs, *scr)

        @pl.when(step == steps - 1)
        def _():
            comm.finish(cin, cout, csems)

    outs = pl.pallas_call(
        carrying, name=name, grid=grid, in_specs=list(in_specs) + [ANY] * nci, out_specs=list(out_specs) + [ANY] * nco,
        out_shape=list(out_shape) + list(comm.out_shape), scratch_shapes=list(scratch) + list(comm.scratch),
        compiler_params=params)(*args, *comm.args)
    return list(outs[:no]), list(outs[no:])


def _comm_only(name, comm):
    def body(*refs):
        nci, nco = len(comm.args), len(comm.out_shape)
        cin, cout, csems = refs[:nci], refs[nci:nci + nco], refs[nci + nco:]
        comm.start(cin, cout, csems)
        if comm.relay is not None:
            comm.relay(cin, cout, csems)
        comm.finish(cin, cout, csems)

    return pl.pallas_call(body, name=name, in_specs=[ANY] * len(comm.args), out_specs=[ANY] * len(comm.out_shape),
                          out_shape=list(comm.out_shape), scratch_shapes=list(comm.scratch))(*comm.args)


def _gather_first(first, later):
    nf, nl = len(first), len(later)
    dts = [BF16] * (nf - 2) + [F32, F32]

    def body(*refs):
        ins, refs = refs[:nf + nl], refs[nf + nl:]
        outs, refs = refs[:nf], refs[nf:]
        casts, refs = refs[:nl], refs[nl:]
        stage, sems = refs[:nf], refs[nf:]
        for a in range(nf):
            stage[a][...] = ins[a][...].astype(dts[a])
            _gather_start(stage, outs, sems, only=a)
        for a in range(nl):
            casts[a][...] = ins[nf + a][...].astype(BF16)
        _gather_relay(stage, outs, sems)
        _gather_finish(stage, outs, sems)

    res = pl.pallas_call(
        body, name="gather_first",
        in_specs=[VMEM] * (nf + nl), out_specs=[ANY] * nf + [VMEM] * nl,
        out_shape=[_sds((N_DEV,) + s.shape[1:], dt) for s, dt in zip(first, dts)]
        + [_sds(s.shape, BF16) for s in later],
        scratch_shapes=[pltpu.VMEM(s.shape, dt) for s, dt in zip(first, dts)] + _relay_sems(nf),
        compiler_params=pltpu.CompilerParams(vmem_limit_bytes=VMEM_LIMIT),
    )(*first, *later)
    return list(res[:nf]), list(res[nf:])


def _a_mix_fwd(x, g, w_in, ln_g, ln_b, w_s, b_st, w_out, comm=None):
    t = x.shape[0]
    nblk = TM // GMLP_BLOCK

    def body(x_ref, g_ref, win_ref, lng_ref, lnb_ref, ws_ref, bst_ref, wout_ref, h_ref, z_ref, gated_scr):
        xv = x_ref[...]
        hb = _rms_fwd(xv, g_ref[...])[0].astype(BF16)
        for d in range(N_DEV):
            z_ref[:, d * FF_SLOT:(d + 1) * FF_SLOT] = _dot(hb, win_ref[d])
        u = _gelu(z_ref[:, :GATE_DIM])
        vb = _ln_fwd(_gelu(z_ref[:, GATE_DIM:]), lng_ref[...], lnb_ref[...])[0].astype(BF16)
        mask = _gate_mask()
        for gi in range(A_GROUPS):
            wm = jnp.where(mask, ws_ref[gi], 0.0).astype(BF16)
            bias = bst_ref[:, gi:gi + 1]
            cs = slice(gi * A_GROUP_DIM, (gi + 1) * A_GROUP_DIM)
            for n in range(nblk):
                rs = slice(n * GMLP_BLOCK, (n + 1) * GMLP_BLOCK)
                sv = _dot(wm, vb[rs, cs]) + bias
                gated_scr[rs, cs] = (u[rs, cs] * sv).astype(BF16)
        h_ref[...] = xv + _dot(gated_scr[...], wout_ref[...])

    return _call(
        "a_mix_fwd", body, (t // TM,),
        [_row(D_MODEL), _res((1, D_MODEL)), _res((N_DEV, D_MODEL, FF_SLOT)), _res((1, GATE_DIM)),
         _res((1, GATE_DIM)), _res((A_GROUPS, GMLP_BLOCK, GMLP_BLOCK)), _res((GMLP_BLOCK, A_GROUPS)),
         _res((GATE_DIM, D_MODEL))],
        [_row(D_MODEL), _row(2 * GATE_DIM), _row(GATE_DIM)],
        [_sds((t, D_MODEL), F32), _sds((t, 2 * GATE_DIM), F32), _sds((t, GATE_DIM), BF16)],
        (x, g, w_in, ln_g, ln_b, w_s, b_st, w_out), comm=comm)


MLP_W_SPECS = (_res((N_DEV, D_MODEL, FF_SLOT)), _res((N_DEV, FF_SLOT, D_MODEL)))


def _mlp_fwd(h, g, w1, w2, layer, comm=None):
    t = h.shape[0]

    def body(h_ref, g_ref, w1_ref, w2_ref, o_ref, a_ref):
        hv = h_ref[...]
        hb = _rms_fwd(hv, g_ref[...])[0].astype(BF16)
        o_ref[...] = hv
        for d in range(N_DEV):
            a = _dot(hb, w1_ref[d])
            a_ref[:, d * FF_SLOT:(d + 1) * FF_SLOT] = a
            r = jnp.maximum(a, 0.0)
            o_ref[...] += _dot((r * r).astype(BF16), w2_ref[d])

    return _call(
        f"mlp_fwd_{layer}", body, (t // TM,), [_row(D_MODEL), _res((1, D_MODEL)), *MLP_W_SPECS],
        [_row(D_MODEL), _row(D_FF)], [_sds((t, D_MODEL), F32), _sds((t, D_FF), F32)], (h, g, w1, w2), comm=comm)


def _mlp_fwd_loss(h, g, w1, w2, final_g, target):
    t = h.shape[0]

    def body(h_ref, g_ref, w1_ref, w2_ref, fg_ref, t_ref, a_ref, loss_ref, dh_ref, dg_ref):
        hv = h_ref[...]
        hb = _rms_fwd(hv, g_ref[...])[0].astype(BF16)
        out = hv
        for d in range(N_DEV):
            a = _dot(hb, w1_ref[d])
            a_ref[:, d * FF_SLOT:(d + 1) * FF_SLOT] = a
            r = jnp.maximum(a, 0.0)
            out = out + _dot((r * r).astype(BF16), w2_ref[d])
        y, xhat, rstd = _rms_fwd(out, fg_ref[...])
        err = y - t_ref[...]
        part = 0.5 * jnp.sum(jnp.mean(err * err, axis=-1, keepdims=True), axis=0, keepdims=True)
        dx, dg = _rms_bwd(err * (1.0 / D_MODEL), xhat, rstd, fg_ref[...])
        dh_ref[...] = dx
        _acc(dg_ref, dg)
        _acc(loss_ref, part)

    return _call(
        "mlp_fwd_loss", body, (t // TM,),
        [_row(D_MODEL), _res((1, D_MODEL)), *MLP_W_SPECS, _res((1, D_MODEL)), _row(D_MODEL)],
        [_row(D_FF), _const((1, 1)), _row(D_MODEL), _const((1, D_MODEL))],
        [_sds((t, D_FF), F32), _sds((1, 1), F32), _sds((t, D_MODEL), F32), _sds((1, D_MODEL), F32)],
        (h, g, w1, w2, final_g, target))[0]


KVQ_W_SPECS = (_res((1, D_MODEL)), _res((D_MODEL, KV_LORA + QK_ROPE)), _res((1, KV_LORA)),
               _res((B_HEADS, KV_LORA, QK_NOPE + V_HEAD)), _res((1, D_MODEL)), _res((D_MODEL, Q_LORA)),
               _res((1, Q_LORA)), _res((B_HEADS, Q_LORA, QK_NOPE + QK_ROPE)))


def _kvq_fwd(h, pos, inv_freq, kvq_w):
    t = h.shape[0]
    half = QK_ROPE // 2

    def body(h_ref, pos_ref, invf_ref, srcg_ref, wkva_ref, kvag_ref, wkvb_ref, mixg_ref, wqa_ref, qg_ref, wqb_ref,
             ckv_ref, k_ref, v_ref, cqpre_ref, q_ref, cos_ref, sin_ref):
        hv = h_ref[...]
        xhat = hv * lax.rsqrt(jnp.mean(hv * hv, axis=-1, keepdims=True) + EPS)
        ang = pos_ref[...].astype(F32) * invf_ref[...]
        cos, sin = jnp.cos(ang), jnp.sin(ang)
        cos_ref[...] = cos
        sin_ref[...] = sin
        ckv = _dot((xhat * srcg_ref[...]).astype(BF16), wkva_ref[...])
        ckv_ref[...] = ckv
        cb = _rms_fwd(ckv[:, :KV_LORA], kvag_ref[...])[0].astype(BF16)
        kpe = _rope(ckv[:, KV_LORA:], cos, sin).astype(BF16)
        for hd in range(B_HEADS):
            kv = _dot(cb, wkvb_ref[hd])
            k_ref[hd, :, 0:QK_NOPE] = kv[:, :QK_NOPE].astype(BF16)
            k_ref[hd, :, QK_NOPE:] = kpe
            v_ref[hd] = kv[:, QK_NOPE:].astype(BF16)
        cqpre = _dot((xhat * mixg_ref[...]).astype(BF16), wqa_ref[...])
        cqpre_ref[...] = cqpre
        cqb = _rms_fwd(cqpre, qg_ref[...])[0].astype(BF16)
        for hd in range(B_HEADS):
            q = _dot(cqb, wqb_ref[hd])
            q_ref[hd, :, 0:QK_NOPE] = q[:, :QK_NOPE].astype(BF16)
            q_ref[hd, :, QK_NOPE:] = _rope(q[:, QK_NOPE:], cos, sin).astype(BF16)

    return _call(
        "kvq_fwd", body, (t // TM,), [_row(D_MODEL), _row(1), _res((1, half)), *KVQ_W_SPECS],
        [_row(KV_LORA + QK_ROPE), _heads(QK_NOPE + QK_ROPE), _heads(V_HEAD), _row(Q_LORA),
         _heads(QK_NOPE + QK_ROPE), _row(half), _row(half)],
        [_sds((t, KV_LORA + QK_ROPE), F32), _sds((B_HEADS, t, QK_NOPE + QK_ROPE), BF16),
         _sds((B_HEADS, t, V_HEAD), BF16), _sds((t, Q_LORA), F32), _sds((B_HEADS, t, QK_NOPE + QK_ROPE), BF16),
         _sds((t, half), F32), _sds((t, half), F32)],
        (h, pos, inv_freq, *kvq_w))[0]


def _softmax_rows(q, k_ref, k):
    past, upto = k * TM, (k + 1) * TM
    s = _dot_nt(q, k_ref[0:upto, :])
    own = jnp.where(_att_mask(0, TM, TM), s[:, past:], jnp.finfo(F32).min)
    s = own if k == 0 else jnp.concatenate([s[:, :past], own], axis=1)
    e = jnp.exp2((s - jnp.max(s, axis=-1, keepdims=True)) * (ATT_SCALE * LOG2_E))
    return e * (1.0 / jnp.sum(e, axis=-1, keepdims=True))


def _for_my_tile(i, nq, fn):
    for k in range(nq):
        @pl.when(i == k)
        def _(k=k):
            fn(k)


def _attn_fwd(h, q, k, v, w_o, comm=None):
    t = h.shape[0]
    nq, hps = t // TM, HEADS_PER_STEP

    def body(h_ref, q_ref, k_ref, v_ref, wo_ref, o_ref, att_ref):
        i, pair = pl.program_id(0), pl.program_id(1)

        @pl.when(pair == 0)
        def _():
            o_ref[...] = h_ref[...]

        def tile(kt):
            proj = None
            for j in range(hps):
                hd = pair * hps + j
                p = _softmax_rows(q_ref[j], k_ref.at[hd], kt)
                ob = _dot(p.astype(BF16), v_ref[hd, 0:(kt + 1) * TM, :]).astype(BF16)
                att_ref[j] = ob
                proj = _dot(ob, wo_ref[hd]) if proj is None else proj + _dot(ob, wo_ref[hd])
            o_ref[...] += proj

        _for_my_tile(i, nq, tile)

    def per_head(d):
        return pl.BlockSpec((hps, TM, d), lambda i, pair: (pair, i, 0))

    def resident(shape):
        zeros = (0,) * len(shape)
        return pl.BlockSpec(shape, lambda i, pair: zeros, pipeline_mode=pl.Buffered(1))

    tile_spec = pl.BlockSpec((TM, D_MODEL), lambda i, pair: (i, 0))
    return _call(
        "attn_fwd", body, (nq, B_HEADS // hps),
        [tile_spec, per_head(QK_NOPE + QK_ROPE), resident((B_HEADS, t, QK_NOPE + QK_ROPE)),
         resident((B_HEADS, t, V_HEAD)), resident((B_HEADS, V_HEAD, D_MODEL))],
        [tile_spec, per_head(V_HEAD)], [_sds((t, D_MODEL), F32), _sds((B_HEADS, t, V_HEAD), BF16)],
        (h, q, k, v, w_o), comm=comm)


def _mlp_bwd(h, a, dho, g, w1, w2, layer, comm=None):
    t = h.shape[0]

    def body(h_ref, a_ref, dho_ref, g_ref, w1_ref, w2_ref, dhi_ref, dg_ref, hn_ref, f_ref, da_ref, dhib_ref):
        gv = g_ref[...]
        y, xhat, rstd = _rms_fwd(h_ref[...], gv)
        hn_ref[...] = y.astype(BF16)
        dho_v = dho_ref[...]
        dhob = dho_v.astype(BF16)
        dhn = jnp.zeros((TM, D_MODEL), F32)
        for d in range(N_DEV):
            cs = slice(d * FF_SLOT, (d + 1) * FF_SLOT)
            r = jnp.maximum(a_ref[:, cs], 0.0)
            f_ref[:, cs] = (r * r).astype(BF16)
            da = (_dot_nt(dhob, w2_ref[d]) * (2.0 * r)).astype(BF16)
            da_ref[:, cs] = da
            dhn = dhn + _dot_nt(da, w1_ref[d])
        dx, dg = _rms_bwd(dhn, xhat, rstd, gv)
        dhi = dho_v + dx
        dhi_ref[...] = dhi
        dhib_ref[...] = dhi.astype(BF16)
        _acc(dg_ref, dg)

    return _call(
        f"mlp_bwd_{layer}", body, (t // TM,),
        [_row(D_MODEL), _row(D_FF), _row(D_MODEL), _res((1, D_MODEL)), *MLP_W_SPECS],
        [_row(D_MODEL), _const((1, D_MODEL)), _row(D_MODEL), _row(D_FF), _row(D_FF), _row(D_MODEL)],
        [_sds((t, D_MODEL), F32), _sds((1, D_MODEL), F32), _sds((t, D_MODEL), BF16), _sds((t, D_FF), BF16),
         _sds((t, D_FF), BF16), _sds((t, D_MODEL), BF16)],
        (h, a, dho, g, w1, w2), comm=comm)


def _attn_bwd(dh, q, k, v, w_o, cos, sin, comm=None):
    t = dh.shape[0]
    half, hps = QK_ROPE // 2, HEADS_PER_STEP

    def body(dh_ref, q_ref, k_ref, v_ref, wo_ref, cos_ref, sin_ref, dq_ref, dk_ref, dv_ref):
        i = pl.program_id(1)

        @pl.when(i == 0)
        def _():
            dk_ref[...] = jnp.zeros_like(dk_ref)
            dv_ref[...] = jnp.zeros_like(dv_ref)

        def tile(kt):
            keys = slice(0, (kt + 1) * TM)
            for j in range(hps):
                qj = q_ref[j]
                do = _dot_nt(dh_ref[kt * TM:(kt + 1) * TM, :], wo_ref[j]).astype(BF16)
                p = _softmax_rows(qj, k_ref.at[j], kt)
                dp = _dot_nt(do, v_ref[j, keys, :])
                ds = (p * (dp - jnp.sum(p * dp, axis=-1, keepdims=True)) * ATT_SCALE).astype(BF16)
                dq = _dot(ds, k_ref[j, keys, :])
                dq_ref[j, :, 0:QK_NOPE] = dq[:, :QK_NOPE].astype(BF16)
                dq_ref[j, :, QK_NOPE:] = _rope(dq[:, QK_NOPE:], cos_ref[...], -sin_ref[...]).astype(BF16)
                dk_ref[j, keys, :] += _dot_tn(ds, qj)
                dv_ref[j, keys, :] += _dot_tn(p.astype(BF16), do)

        _for_my_tile(i, t // TM, tile)

    def per_pair(rows, d, tiled):
        return pl.BlockSpec((hps, rows, d), (lambda pair, i: (pair, i, 0)) if tiled else (lambda pair, i: (pair, 0, 0)))

    def tile(d):
        return pl.BlockSpec((TM, d), lambda pair, i: (i, 0))

    return _call(
        "attn_bwd", body, (B_HEADS // hps, t // TM),
        [pl.BlockSpec((t, D_MODEL), lambda pair, i: (0, 0), pipeline_mode=pl.Buffered(1)),
         per_pair(TM, QK_NOPE + QK_ROPE, True), per_pair(t, QK_NOPE + QK_ROPE, False), per_pair(t, V_HEAD, False),
         per_pair(V_HEAD, D_MODEL, False), tile(half), tile(half)],
        [per_pair(TM, QK_NOPE + QK_ROPE, True), per_pair(t, QK_NOPE + QK_ROPE, False), per_pair(t, V_HEAD, False)],
        [_sds((B_HEADS, t, QK_NOPE + QK_ROPE), BF16), _sds((B_HEADS, t, QK_NOPE + QK_ROPE), F32),
         _sds((B_HEADS, t, V_HEAD), F32)],
        (dh, q, k, v, w_o, cos, sin), comm=comm)


def _kvq_bwd(h, dh, ckv, cqpre, dq, dk, dv, cos, sin, kvq_w):
    t = h.shape[0]
    half = QK_ROPE // 2

    def body(h_ref, dh_ref, ckv_ref, cqpre_ref, dq_ref, dk_ref, dv_ref, cos_ref, sin_ref,
             srcg_ref, wkva_ref, kvag_ref, wkvb_ref, mixg_ref, wqa_ref, qg_ref, wqb_ref,
             dhi_ref, hq_ref, hk_ref, cq_ref, dcqpre_ref, c_ref, dkv_ref, dckv_ref,
             dmixg_ref, dsrcg_ref, dqg_ref, dkvag_ref):
        hv = h_ref[...]
        rstd = lax.rsqrt(jnp.mean(hv * hv, axis=-1, keepdims=True) + EPS)
        xhat = hv * rstd
        mixg, srcg, qg, kvag = mixg_ref[...], srcg_ref[...], qg_ref[...], kvag_ref[...]
        hq_ref[...] = (xhat * mixg).astype(BF16)
        hk_ref[...] = (xhat * srcg).astype(BF16)
        cq, cqhat, crstd = _rms_fwd(cqpre_ref[...], qg)
        cq_ref[...] = cq.astype(BF16)
        dcq = jnp.zeros((TM, Q_LORA), F32)
        for hd in range(B_HEADS):
            dcq = dcq + _dot_nt(dq_ref[hd], wqb_ref[hd])
        dcqpre, dqg = _rms_bwd(dcq, cqhat, crstd, qg)
        dcqpre_b = dcqpre.astype(BF16)
        dcqpre_ref[...] = dcqpre_b
        dxq, dmixg = _rms_bwd(_dot_nt(dcqpre_b, wqa_ref[...]), xhat, rstd, mixg)
        ckv = ckv_ref[...]
        c, chat, krstd = _rms_fwd(ckv[:, :KV_LORA], kvag)
        c_ref[...] = c.astype(BF16)
        dc = jnp.zeros((TM, KV_LORA), F32)
        dkpe = jnp.zeros((TM, QK_ROPE), F32)
        for hd in range(B_HEADS):
            dkv = jnp.concatenate([dk_ref[hd, :, 0:QK_NOPE], dv_ref[hd]], axis=-1).astype(BF16)
            dkv_ref[hd] = dkv
            dc = dc + _dot_nt(dkv, wkvb_ref[hd])
            dkpe = dkpe + dk_ref[hd, :, QK_NOPE:]
        dlat, dkvag = _rms_bwd(dc, chat, krstd, kvag)
        dpe = _rope(dkpe, cos_ref[...], -sin_ref[...])
        dckv_b = jnp.concatenate([dlat, dpe], axis=-1).astype(BF16)
        dckv_ref[...] = dckv_b
        dxk, dsrcg = _rms_bwd(_dot_nt(dckv_b, wkva_ref[...]), xhat, rstd, srcg)
        dhi_ref[...] = dh_ref[...] + dxq + dxk
        _acc(dmixg_ref, dmixg)
        _acc(dsrcg_ref, dsrcg)
        _acc(dqg_ref, dqg)
        _acc(dkvag_ref, dkvag)

    return _call(
        "kvq_bwd", body, (t // TM,),
        [_row(D_MODEL), _row(D_MODEL), _row(KV_LORA + QK_ROPE), _row(Q_LORA), _heads(QK_NOPE + QK_ROPE),
         _heads(QK_NOPE + QK_ROPE), _heads(V_HEAD), _row(half), _row(half), *KVQ_W_SPECS],
        [_row(D_MODEL), _row(D_MODEL), _row(D_MODEL), _row(Q_LORA), _row(Q_LORA), _row(KV_LORA),
         _heads(QK_NOPE + V_HEAD), _row(KV_LORA + QK_ROPE),
         _const((1, D_MODEL)), _const((1, D_MODEL)), _const((1, Q_LORA)), _const((1, KV_LORA))],
        [_sds((t, D_MODEL), F32), _sds((t, D_MODEL), BF16), _sds((t, D_MODEL), BF16), _sds((t, Q_LORA), BF16),
         _sds((t, Q_LORA), BF16), _sds((t, KV_LORA), BF16), _sds((B_HEADS, t, QK_NOPE + V_HEAD), BF16),
         _sds((t, KV_LORA + QK_ROPE), BF16),
         _sds((1, D_MODEL), F32), _sds((1, D_MODEL), F32), _sds((1, Q_LORA), F32), _sds((1, KV_LORA), F32)],
        (h, dh, ckv, cqpre, dq, dk, dv, cos, sin, *kvq_w))[0]


def _a_mix_bwd(x, z, dh, g, w_in, ln_g, ln_b, w_s, b_st, w_out, comm=None):
    t = x.shape[0]
    tm = TM_GATE
    nblk = tm // GMLP_BLOCK

    def body(x_ref, z_ref, dh_ref, g_ref, win_ref, lng_ref, lnb_ref, ws_ref, bst_ref, wout_ref,
             dx_ref, hn_ref, dz_ref, dg_ref, dlng_ref, dlnb_ref, dws_ref, dbs_ref, dvn_scr, gelu_grad_v):
        @pl.when(pl.program_id(0) == 0)
        def _():
            dws_ref[...] = jnp.zeros_like(dws_ref)
            dbs_ref[...] = jnp.zeros_like(dbs_ref)

        gv, lng = g_ref[...], lng_ref[...]
        y, xhat, rstd = _rms_fwd(x_ref[...], gv)
        hn_ref[...] = y.astype(BF16)
        dhv = dh_ref[...]
        dgated = _dot_nt(dhv.astype(BF16), wout_ref[...])
        u, gelu_grad_u = _gelu_and_grad(z_ref[:, :GATE_DIM])
        v, gelu_grad_v[...] = _gelu_and_grad(z_ref[:, GATE_DIM:])
        vn, vhat, lrstd = _ln_fwd(v, lng, lnb_ref[...])
        vb = vn.astype(BF16)
        mask = _gate_mask()
        for gi in range(A_GROUPS):
            wm = jnp.where(mask, ws_ref[gi], 0.0).astype(BF16)
            bias = bst_ref[:, gi:gi + 1]
            cs = slice(gi * A_GROUP_DIM, (gi + 1) * A_GROUP_DIM)
            dws = jnp.zeros((GMLP_BLOCK, GMLP_BLOCK), F32)
            dbs = jnp.zeros((GMLP_BLOCK, 1), F32)
            for n in range(nblk):
                rs = slice(n * GMLP_BLOCK, (n + 1) * GMLP_BLOCK)
                sv = _dot(wm, vb[rs, cs]) + bias
                dz_ref[rs, cs] = (dgated[rs, cs] * sv * gelu_grad_u[rs, cs]).astype(BF16)
                dsv = dgated[rs, cs] * u[rs, cs]
                dsvb = dsv.astype(BF16)
                dws = dws + _dot_nt(dsvb, vb[rs, cs])
                dbs = dbs + jnp.sum(dsv, axis=-1, keepdims=True)
                dvn_scr[rs, cs] = _dot_tn(wm, dsvb)
            dws_ref[gi] += jnp.where(mask, dws, 0.0)
            dbs_ref[gi] += dbs
        dvn = dvn_scr[...]
        dvhat = dvn * lng
        dv = lrstd * (dvhat - jnp.mean(dvhat, axis=-1, keepdims=True)
                      - vhat * jnp.mean(dvhat * vhat, axis=-1, keepdims=True))
        dz_ref[:, GATE_DIM:] = (dv * gelu_grad_v[...]).astype(BF16)
        dhn = jnp.zeros((tm, D_MODEL), F32)
        for d in range(N_DEV):
            dhn = dhn + _dot_nt(dz_ref[:, d * FF_SLOT:(d + 1) * FF_SLOT], win_ref[d])
        dx, dg = _rms_bwd(dhn, xhat, rstd, gv)
        dx_ref[...] = dhv + dx
        _acc(dg_ref, dg)
        _acc(dlng_ref, jnp.sum(dvn * vhat, axis=0, keepdims=True))
        _acc(dlnb_ref, jnp.sum(dvn, axis=0, keepdims=True))

    return _call(
        "a_mix_bwd", body, (t // tm,),
        [_row(D_MODEL, tm), _row(2 * GATE_DIM, tm), _row(D_MODEL, tm), _res((1, D_MODEL)),
         _res((N_DEV, D_MODEL, FF_SLOT)), _res((1, GATE_DIM)), _res((1, GATE_DIM)),
         _res((A_GROUPS, GMLP_BLOCK, GMLP_BLOCK)), _res((GMLP_BLOCK, A_GROUPS)), _res((GATE_DIM, D_MODEL))],
        [_row(D_MODEL, tm), _row(D_MODEL, tm), _row(2 * GATE_DIM, tm),
         _const((1, D_MODEL)), _const((1, GATE_DIM)), _const((1, GATE_DIM)),
         _const((A_GROUPS, GMLP_BLOCK, GMLP_BLOCK)), _const((A_GROUPS, GMLP_BLOCK, 1))],
        [_sds((t, D_MODEL), F32), _sds((t, D_MODEL), BF16),
         _sds((t, 2 * GATE_DIM), BF16), _sds((1, D_MODEL), F32), _sds((1, GATE_DIM), F32),
         _sds((1, GATE_DIM), F32), _sds((A_GROUPS, GMLP_BLOCK, GMLP_BLOCK), F32),
         _sds((A_GROUPS, GMLP_BLOCK, 1), F32)],
        (x, z, dh, g, w_in, ln_g, ln_b, w_s, b_st, w_out),
        scratch=[pltpu.VMEM((tm, GATE_DIM), F32), pltpu.VMEM((tm, GATE_DIM), F32)], comm=comm)


def _wgrad(name, a, b, a_spec, b_spec, m, n, comm=None):
    def body(a_ref, b_ref, o_ref):
        o_ref[0] = _dot_tn(a_ref[...].astype(BF16), b_ref[...].astype(BF16)).astype(BF16)

    outs, got = _call(name, body, (N_DEV,), [a_spec, b_spec], [pl.BlockSpec((1, m, n), lambda d: (d, 0, 0))],
                      [_sds((N_DEV, m, n), BF16)], (a, b), comm=comm)
    return outs[0] if comm is None else (outs[0], got)


def _full(t, d):
    return pl.BlockSpec((t, d), lambda i: (0, 0), pipeline_mode=pl.Buffered(1))


def _cols(t, d):
    return pl.BlockSpec((t, d), lambda i: (0, i))


def _head(t, d):
    return pl.BlockSpec((None, t, d), lambda i: (i, 0, 0))


def _local_step(x, pos, target, inv_freq, wg, sm, shards=None):
    t = x.shape[0]
    wg = dict(wg)
    dist = shards is not None
    mix_g = [sm["norm_mix_g"][l:l + 1] for l in range(2)]
    mlp_g = [sm["norm_mlp_g"][l:l + 1] for l in range(2)]

    def gather(names):
        return _gather_comm([shards[k] for k in names]) if dist else None

    def send(grads):
        return _exchange_comm(grads=grads) if dist else None

    def send_sums(name, grads):
        return _chip_exchange_comm(_pair_reduce(name, grads)) if dist else None

    def a_args():
        return (wg["a_w_in"], wg["a_ln_v_g"], wg["a_ln_v_b"], sm["a_w_s"], sm["a_b_st"], wg["a_w_out"])

    def kvq_w():
        return (sm["kv_src_norm_g"], wg["kv_w_a"], sm["kv_a_norm_g"], wg["kv_w_b"], mix_g[1], wg["b_w_q_a"],
                sm["b_q_norm_g"], wg["b_w_q_b"])

    names = ("mlp_w1_0", "mlp_w2_0")
    (h1, z, gated), got = _a_mix_fwd(x, mix_g[0], *a_args(), comm=gather(names))
    wg.update(zip(names, got))
    names = ("kv_w_a", "kv_w_b", "b_w_q_a", "b_w_q_b", "b_w_o")
    (h2, a0), got = _mlp_fwd(h1, mlp_g[0], wg["mlp_w1_0"], wg["mlp_w2_0"], 0, comm=gather(names))
    wg.update(zip(names, got))
    if dist:
        wg["b_w_q_a"] = wg["b_w_q_a"].reshape(D_MODEL, Q_LORA)
        wg["kv_w_a"] = wg["kv_w_a"].reshape(D_MODEL, KV_LORA + QK_ROPE)
    ckv, k, v, cqpre, q, cos, sin = _kvq_fwd(h2, pos, inv_freq, kvq_w())
    names = ("mlp_w1_1", "mlp_w2_1")
    (h3, att), got = _attn_fwd(h2, q, k, v, wg["b_w_o"], comm=gather(names))
    wg.update(zip(names, got))
    a1, loss, dh4, d_final_g = _mlp_fwd_loss(h3, mlp_g[1], wg["mlp_w1_1"], wg["mlp_w2_1"], sm["final_norm_g"], target)

    g = {}
    (dh3, d_mlp_g1, hn, f, da, dh3_b), _ = _mlp_bwd(h3, a1, dh4, mlp_g[1], wg["mlp_w1_1"], wg["mlp_w2_1"], 1)
    g["mlp_w1_1"] = _wgrad("wgrad_w1_1", hn, da, _full(t, D_MODEL), _cols(t, FF_SLOT), D_MODEL, FF_SLOT)
    g["mlp_w2_1"] = _wgrad("wgrad_w2_1", f, dh4, _cols(t, FF_SLOT), _full(t, D_MODEL), FF_SLOT, D_MODEL)
    g["b_w_o"] = _wgrad("wgrad_w_o", att, dh3_b, _head(t, V_HEAD), _full(t, D_MODEL), V_HEAD, D_MODEL)
    names = ("mlp_w1_1", "mlp_w2_1", "b_w_o")
    (dq, dk, dv), got = _attn_bwd(dh3_b, q, k, v, wg["b_w_o"], cos, sin,
                                         comm=send_sums("pair_reduce_1", [g[k] for k in names]))
    g.update(zip(names, got))
    (dh2, hq, hk, cq, dcqpre, c, dkv, dckv, d_mix_g1, d_src_g, d_q_g, d_kv_a_g) = _kvq_bwd(
        h2, dh3, ckv, cqpre, dq, dk, dv, cos, sin, kvq_w())
    g["b_w_q_a"] = _wgrad("wgrad_w_q_a", hq, dcqpre, _cols(t, D_MODEL // N_DEV), _full(t, Q_LORA),
                          D_MODEL // N_DEV, Q_LORA)
    g["b_w_q_b"] = _wgrad("wgrad_w_q_b", cq, dq, _full(t, Q_LORA), _head(t, QK_NOPE + QK_ROPE),
                          Q_LORA, QK_NOPE + QK_ROPE)
    g["kv_w_a"] = _wgrad("wgrad_kv_w_a", hk, dckv, _cols(t, D_MODEL // N_DEV), _full(t, KV_LORA + QK_ROPE),
                         D_MODEL // N_DEV, KV_LORA + QK_ROPE)
    g["kv_w_b"] = _wgrad("wgrad_kv_w_b", c, dkv, _full(t, KV_LORA), _head(t, QK_NOPE + V_HEAD),
                         KV_LORA, QK_NOPE + V_HEAD)
    names = ("b_w_q_a", "b_w_q_b", "kv_w_a", "kv_w_b")
    (dh1, d_mlp_g0, hn, f, da, dh1_b), got = _mlp_bwd(h1, a0, dh2, mlp_g[0], wg["mlp_w1_0"], wg["mlp_w2_0"], 0,
                                                      comm=send([g[k] for k in names]))
    g.update(zip(names, got))
    g["mlp_w1_0"] = _wgrad("wgrad_w1_0", hn, da, _full(t, D_MODEL), _cols(t, FF_SLOT), D_MODEL, FF_SLOT)
    g["mlp_w2_0"] = _wgrad("wgrad_w2_0", f, dh2, _cols(t, FF_SLOT), _full(t, D_MODEL), FF_SLOT, D_MODEL)
    g["a_w_out"] = _wgrad("wgrad_a_w_out", gated, dh1_b, _cols(t, GATE_DIM // N_DEV), _full(t, D_MODEL),
                          GATE_DIM // N_DEV, D_MODEL)
    names = ("mlp_w1_0", "mlp_w2_0")
    (dx, hn, dz, d_mix_g0, d_ln_g, d_ln_b, d_ws, d_bs), got = _a_mix_bwd(
        x, z, dh1, mix_g[0], *a_args(), comm=send_sums("pair_reduce_0", [g[k] for k in names]))
    g.update(zip(names, got))
    small = {
        "norm_mix_g": jnp.concatenate([d_mix_g0, d_mix_g1], axis=0),
        "norm_mlp_g": jnp.concatenate([d_mlp_g0, d_mlp_g1], axis=0),
        "a_ln_v_g": d_ln_g.reshape(N_DEV, GATE_DIM // N_DEV),
        "a_ln_v_b": d_ln_b.reshape(N_DEV, GATE_DIM // N_DEV),
        "a_w_s": d_ws.astype(BF16) if dist else d_ws,
        "a_b_s": d_bs.reshape(A_GROUPS, GMLP_BLOCK),
        "b_q_norm_g": d_q_g,
        "kv_src_norm_g": d_src_g,
        "kv_a_norm_g": d_kv_a_g,
        "final_norm_g": d_final_g,
    }
    wgrad_in = ("wgrad_a_w_in", hn, dz, _full(t, D_MODEL), _cols(t, FF_SLOT), D_MODEL, FF_SLOT)
    if dist:
        parts = [small[k].reshape((1,) + small[k].shape) for k in SMALL] + [loss.reshape(1, 1, 1)]
        g["a_w_in"], got = _wgrad(*wgrad_in, comm=_exchange_comm(parts=parts))
        small, loss = dict(zip(SMALL, got)), got[-1]
    else:
        g["a_w_in"] = _wgrad(*wgrad_in)
    return loss, dx, g, small


def _adamw(w, g, m, v):
    m = ADAM_B1 * m + (1.0 - ADAM_B1) * g
    v = ADAM_B2 * v + (1.0 - ADAM_B2) * (g * g)
    m_hat = m / (1.0 - ADAM_B1 ** ADAM_STEP)
    v_hat = v / (1.0 - ADAM_B2 ** ADAM_STEP)
    return -ADAM_LR * (m_hat / (jnp.sqrt(v_hat) + ADAM_EPS) + ADAM_WD * w), m, v


def _sum_in_device_order(r_ref):
    g = r_ref[0].astype(F32)
    for j in range(1, r_ref.shape[0]):
        g = g + r_ref[j].astype(F32)
    return g


def _adamw_sharded(name, recvs, w, m, v, comm=None):
    layers, r, c = w.shape
    tr = math.gcd(r, 512)
    flat = [a for per_layer in recvs for a in per_layer]

    def body(*refs):
        r_refs, (w_ref, m_ref, v_ref) = refs[:len(flat)], refs[len(flat):len(flat) + 3]
        g_ref, d_ref, nm_ref, nv_ref = refs[-4:]
        layer = pl.program_id(0)
        g, pos = None, 0
        for li, per_layer in enumerate(recvs):
            total = None
            for ref in r_refs[pos:pos + len(per_layer)]:
                part = _sum_in_device_order(ref)
                total = part if total is None else total + part
            pos += len(per_layer)
            g = total if g is None else jnp.where(layer == li, total, g)
        g_ref[...] = g
        d_ref[...], nm_ref[...], nv_ref[...] = _adamw(w_ref[...], g, m_ref[...], v_ref[...])

    blk = pl.BlockSpec((None, tr, c), lambda l, i: (l, i, 0))
    return _call(name, body, (layers, r // tr),
                 [pl.BlockSpec((a.shape[0], tr, c), lambda l, i: (0, i, 0)) for a in flat] + [blk] * 3,
                 [blk] * 4, [_sds(w.shape, F32)] * 4, (*flat, w, m, v), comm=comm)


def _adamw_small(recvs, ws, ms, vs, own_row, losses):
    n = len(recvs)

    def body(*refs):
        r_refs, w_refs, m_refs, v_refs = (refs[i * n:(i + 1) * n] for i in range(4))
        outs, scr = refs[4 * n + 1:8 * n + 2], refs[8 * n + 2:]
        outs[-1][...] = _sum_in_device_order(refs[4 * n])
        me = _my_place()[3]
        for a in range(n):
            g = _sum_in_device_order(r_refs[a])
            if own_row[a]:
                scr[0][...] = g
                g = scr[0][pl.ds(me, 1), :]
            g_ref, d_ref, nm_ref, nv_ref = outs[4 * a:4 * a + 4]
            g_ref[...] = g
            d_ref[...], nm_ref[...], nv_ref[...] = _adamw(w_refs[a][...], g, m_refs[a][...], v_refs[a][...])

    out_shape = []
    for w in ws:
        out_shape += [_sds(w.shape, F32)] * 4
    return pl.pallas_call(
        body, name="adamw_small", in_specs=[VMEM] * (4 * n + 1), out_specs=[VMEM] * (4 * n + 1),
        out_shape=out_shape + [_sds((1, 1), F32)], scratch_shapes=[pltpu.VMEM((N_DEV, GATE_DIM // N_DEV), F32)],
    )(*recvs, *ws, *ms, *vs, losses)


BIG = ("a_w_in", "a_w_out", "b_w_q_a", "b_w_q_b", "b_w_o", "kv_w_a", "kv_w_b", "mlp_w1", "mlp_w2")
SMALL = ("norm_mix_g", "norm_mlp_g", "a_ln_v_g", "a_ln_v_b", "a_w_s", "a_b_s", "b_q_norm_g", "kv_src_norm_g",
         "kv_a_norm_g", "final_norm_g")
WEIGHTS = ("norm_mix_g", "norm_mlp_g", "a_w_in", "a_ln_v_g", "a_ln_v_b", "a_w_s", "a_b_s", "a_w_out", "b_w_q_a",
           "b_q_norm_g", "b_w_q_b", "b_w_o", "kv_src_norm_g", "kv_w_a", "kv_a_norm_g", "kv_w_b", "mlp_w1", "mlp_w2",
           "final_norm_g")


def _two_d(name, a):
    if name in ("a_w_s", "a_b_s"):
        return a.reshape(a.shape[1:])
    return a.reshape(1, -1) if a.ndim == 1 else a


def _three_d(a):
    return a if a.ndim == 3 else a.reshape((1,) + a.shape)


def kernel(x, positions, norm_mix_g, norm_mlp_g, a_w_in, a_ln_v_g, a_ln_v_b, a_w_s, a_b_s, a_w_out, b_w_q_a, b_q_norm_g, b_w_q_b, b_w_o, kv_src_norm_g, kv_w_a, kv_a_norm_g, kv_w_b, mlp_w1, mlp_w2, final_norm_g, loss_target, m_norm_mix_g, m_norm_mlp_g, m_a_w_in, m_a_ln_v_g, m_a_ln_v_b, m_a_w_s, m_a_b_s, m_a_w_out, m_b_w_q_a, m_b_q_norm_g, m_b_w_q_b, m_b_w_o, m_kv_src_norm_g, m_kv_w_a, m_kv_a_norm_g, m_kv_w_b, m_mlp_w1, m_mlp_w2, m_final_norm_g, v_norm_mix_g, v_norm_mlp_g, v_a_w_in, v_a_ln_v_g, v_a_ln_v_b, v_a_w_s, v_a_b_s, v_a_w_out, v_b_w_q_a, v_b_q_norm_g, v_b_w_q_b, v_b_w_o, v_kv_src_norm_g, v_kv_w_a, v_kv_a_norm_g, v_kv_w_b, v_mlp_w1, v_mlp_w2, v_final_norm_g):
    w = dict(norm_mix_g=norm_mix_g, norm_mlp_g=norm_mlp_g, a_w_in=a_w_in, a_ln_v_g=a_ln_v_g, a_ln_v_b=a_ln_v_b,
             a_w_s=a_w_s, a_b_s=a_b_s, a_w_out=a_w_out, b_w_q_a=b_w_q_a, b_q_norm_g=b_q_norm_g, b_w_q_b=b_w_q_b,
             b_w_o=b_w_o, kv_src_norm_g=kv_src_norm_g, kv_w_a=kv_w_a, kv_a_norm_g=kv_a_norm_g, kv_w_b=kv_w_b,
             mlp_w1=mlp_w1, mlp_w2=mlp_w2, final_norm_g=final_norm_g)
    m = dict(norm_mix_g=m_norm_mix_g, norm_mlp_g=m_norm_mlp_g, a_w_in=m_a_w_in, a_ln_v_g=m_a_ln_v_g,
             a_ln_v_b=m_a_ln_v_b, a_w_s=m_a_w_s, a_b_s=m_a_b_s, a_w_out=m_a_w_out, b_w_q_a=m_b_w_q_a,
             b_q_norm_g=m_b_q_norm_g, b_w_q_b=m_b_w_q_b, b_w_o=m_b_w_o, kv_src_norm_g=m_kv_src_norm_g,
             kv_w_a=m_kv_w_a, kv_a_norm_g=m_kv_a_norm_g, kv_w_b=m_kv_w_b, mlp_w1=m_mlp_w1, mlp_w2=m_mlp_w2,
             final_norm_g=m_final_norm_g)
    v = dict(norm_mix_g=v_norm_mix_g, norm_mlp_g=v_norm_mlp_g, a_w_in=v_a_w_in, a_ln_v_g=v_a_ln_v_g,
             a_ln_v_b=v_a_ln_v_b, a_w_s=v_a_w_s, a_b_s=v_a_b_s, a_w_out=v_a_w_out, b_w_q_a=v_b_w_q_a,
             b_q_norm_g=v_b_q_norm_g, b_w_q_b=v_b_w_q_b, b_w_o=v_b_w_o, kv_src_norm_g=v_kv_src_norm_g,
             kv_w_a=v_kv_w_a, kv_a_norm_g=v_kv_a_norm_g, kv_w_b=v_kv_w_b, mlp_w1=v_mlp_w1, mlp_w2=v_mlp_w2,
             final_norm_g=v_final_norm_g)
    t = x.shape[1]

    first = ("a_w_in", "a_w_out", "a_ln_v_g", "a_ln_v_b")
    later = ("mlp_w1_0", "mlp_w2_0", "mlp_w1_1", "mlp_w2_1", "kv_w_a", "kv_w_b", "b_w_q_a", "b_w_q_b", "b_w_o")
    blocks = {k: _three_d(w[k]) for k in BIG if not k.startswith("mlp")}
    for k in ("mlp_w1", "mlp_w2"):
        blocks[k + "_0"], blocks[k + "_1"] = w[k][0:1], w[k][1:2]
    got, casts = _gather_first([blocks[k] if k in blocks else w[k] for k in first], [blocks[k] for k in later])
    wg = dict(zip(first, got))
    wg["a_w_out"] = wg["a_w_out"].reshape(GATE_DIM, D_MODEL)
    wg["a_ln_v_g"] = wg["a_ln_v_g"].reshape(1, GATE_DIM)
    wg["a_ln_v_b"] = wg["a_ln_v_b"].reshape(1, GATE_DIM)
    shards = dict(zip(later, casts))

    sm = {k: _two_d(k, w[k]) for k in SMALL if k not in ("a_ln_v_g", "a_ln_v_b")}
    sm["a_b_st"] = sm["a_b_s"].T
    inv_freq = (ROPE_THETA ** (-jnp.arange(0, QK_ROPE, 2, dtype=F32) / QK_ROPE)).reshape(1, QK_ROPE // 2)

    losses, dx, g, small = _local_step(x[0], positions.reshape(t, 1), loss_target[0], inv_freq, wg, sm, shards)

    names = ("a_w_in", "a_w_out")
    sums = _pair_reduce("pair_reduce_a", [g[k] for k in names])
    g.update(zip(names, _comm_only("exchange_last", _chip_exchange_comm(sums))))

    out = {}
    for k in BIG:
        recvs = [[g[k + "_0"]], [g[k + "_1"]]] if k.startswith("mlp") else [[g[k]]]
        res, _ = _adamw_sharded("adamw_" + k, recvs, _three_d(w[k]), _three_d(m[k]), _three_d(v[k]))
        out[k] = [o.reshape(w[k].shape) for o in res]
    own_row = [k in ("a_ln_v_g", "a_ln_v_b") for k in SMALL]
    res = _adamw_small([small[k] for k in SMALL], [_two_d(k, w[k]) for k in SMALL], [_two_d(k, m[k]) for k in SMALL],
                       [_two_d(k, v[k]) for k in SMALL], own_row, losses)
    for i, k in enumerate(SMALL):
        out[k] = [o.reshape(w[k].shape) for o in res[4 * i:4 * i + 4]]

    return (res[-1].reshape(()), dx.reshape(x.shape), *[out[k][0] for k in WEIGHTS], *[out[k][1] for k in WEIGHTS],
            *[out[k][2] for k in WEIGHTS], *[out[k][3] for k in WEIGHTS])
```

```python
import math

import jax
import jax.numpy as jnp
from jax import lax
from jax.experimental import pallas as pl
from jax.experimental.pallas import tpu as pltpu
from jax.experimental.pallas import tpu_sc as plsc

F32, BF16 = jnp.float32, jnp.bfloat16
MESH = pl.DeviceIdType.MESH
ANY = pl.BlockSpec(memory_space=pl.ANY)
VMEM = pl.BlockSpec(memory_space=pltpu.VMEM)

N_DEV = 8
D_MODEL = 1024
CHUNK = 64
GMLP_BLOCK = 128
GATE_DIM = 2048
A_GROUPS = 8
A_GROUP_DIM = GATE_DIM // A_GROUPS
B_HEADS = 8
QK_NOPE, QK_ROPE, V_HEAD = 128, 64, 128
Q_LORA, KV_LORA = 384, 256
ROPE_THETA = 10000.0
D_FF = 4096
FF_SLOT = D_FF // N_DEV
EPS = 1e-6
ATT_SCALE = (QK_NOPE + QK_ROPE) ** -0.5

ADAM_LR, ADAM_B1, ADAM_B2, ADAM_EPS, ADAM_WD, ADAM_STEP = 0.001, 0.9, 0.999, 1e-08, 0.01, 10

TM = 256
TM_GATE = 128
VMEM_LIMIT = 56 * 1024 * 1024
INV_SQRT2 = 1.0 / math.sqrt(2.0)
INV_SQRT_2PI = 1.0 / math.sqrt(2.0 * math.pi)
LOG2_E = 1.0 / math.log(2.0)
HEADS_PER_STEP = 2


def _dot(a, b):
    return jnp.dot(a, b, preferred_element_type=F32)


def _dot_nt(a, b):
    return lax.dot_general(a, b, (((1,), (1,)), ((), ())), preferred_element_type=F32)


def _dot_tn(a, b):
    return lax.dot_general(a, b, (((0,), (0,)), ((), ())), preferred_element_type=F32)


def _rms_fwd(x, g):
    rstd = lax.rsqrt(jnp.mean(x * x, axis=-1, keepdims=True) + EPS)
    xhat = x * rstd
    return xhat * g, xhat, rstd


def _rms_bwd(dy, xhat, rstd, g):
    dxhat = dy * g
    dx = rstd * (dxhat - xhat * jnp.mean(dxhat * xhat, axis=-1, keepdims=True))
    return dx, jnp.sum(dy * xhat, axis=0, keepdims=True)


def _ln_fwd(v, g, b):
    mu = jnp.mean(v, axis=-1, keepdims=True)
    vc = v - mu
    rstd = lax.rsqrt(jnp.mean(vc * vc, axis=-1, keepdims=True) + EPS)
    vhat = vc * rstd
    return vhat * g + b, vhat, rstd


def _gelu(x):
    return 0.5 * x * (1.0 + lax.erf(x * INV_SQRT2))


def _gelu_and_grad(x):
    cdf = 0.5 * (1.0 + lax.erf(x * INV_SQRT2))
    return x * cdf, cdf + x * jnp.exp(-0.5 * x * x) * INV_SQRT_2PI


def _rope(x, cos, sin):
    x1, x2 = x[:, :QK_ROPE // 2], x[:, QK_ROPE // 2:]
    return jnp.concatenate([x1 * cos - x2 * sin, x2 * cos + x1 * sin], axis=-1)


def _gate_mask():
    row = lax.broadcasted_iota(jnp.int32, (GMLP_BLOCK, GMLP_BLOCK), 0)
    col = lax.broadcasted_iota(jnp.int32, (GMLP_BLOCK, GMLP_BLOCK), 1)
    return (col < CHUNK) | (row >= CHUNK)


def _att_mask(q0, tq, t):
    q = q0 + lax.broadcasted_iota(jnp.int32, (tq, t), 0)
    k = lax.broadcasted_iota(jnp.int32, (tq, t), 1)
    return jnp.right_shift(k, 6) <= jnp.right_shift(q, 6)


def _res(shape, imap=None):
    zeros = (0,) * len(shape)
    return pl.BlockSpec(shape, imap or (lambda i: zeros), pipeline_mode=pl.Buffered(1))


def _const(shape):
    zeros = (0,) * len(shape)
    return pl.BlockSpec(shape, lambda i: zeros)


def _row(d, tm=TM):
    return pl.BlockSpec((tm, d), lambda i: (i, 0))


def _heads(d):
    return pl.BlockSpec((B_HEADS, TM, d), lambda i: (0, i, 0))


def _sds(shape, dt):
    return jax.ShapeDtypeStruct(shape, dt)


def _acc(ref, val):
    @pl.when(pl.program_id(0) == 0)
    def _():
        ref[...] = jnp.zeros_like(ref)
    ref[...] += val


def _my_place():
    x, y, c = lax.axis_index("x"), lax.axis_index("y"), lax.axis_index("c")
    return x, y, c, 4 * x + 2 * y + c


def _peer(x, y, c, k):
    px = 1 - x if k & 4 else x
    py = 1 - y if k & 2 else y
    pc = 1 - c if k & 1 else c
    return (px, py, pc), 4 * px + 2 * py + pc


CHIPS = (2, 4, 6)


def _splits(ref):
    return len(ref.shape) >= 3 and ref.shape[1] % 32 == 0


def _piece(ref, block, half=None):
    if half is None or not _splits(ref):
        return ref.at[pl.ds(block, 1)]
    rows = ref.shape[1] // 2
    return ref.at[pl.ds(block, 1), pl.ds(half * rows, rows)]


def _gather_copy(sems, a, k, piece, to, src=None):
    return pltpu.make_async_remote_copy(
        src_ref=piece if src is None else src, dst_ref=piece, send_sem=sems[0].at[a, k], recv_sem=sems[1].at[a, k],
        device_id=to, device_id_type=MESH)


def _gather_start(srcs, outs, sems, only=None):
    x, y, c, me = _my_place()
    for a in range(len(srcs)) if only is None else (only,):
        mine = _piece(outs[a], me)
        pltpu.make_async_copy(srcs[a], mine, sems[2].at[a]).start()
        for k, rel in enumerate((1, 4, 2)):
            _gather_copy(sems, a, k, mine, _peer(x, y, c, rel)[0], src=srcs[a]).start()


def _gather_relay(srcs, outs, sems):
    x, y, c, _ = _my_place()
    sib = _peer(x, y, c, 1)[0]
    (xn, xn_i), (yn, yn_i) = _peer(x, y, c, 4), _peer(x, y, c, 2)
    for a in range(len(srcs)):
        out = outs[a]
        _gather_copy(sems, a, 1, _piece(out, xn_i), xn).wait_recv()
        _gather_copy(sems, a, 3, _piece(out, xn_i, 0), yn).start()
        _gather_copy(sems, a, 5, _piece(out, xn_i), sib).start()
        _gather_copy(sems, a, 2, _piece(out, yn_i), yn).wait_recv()
        if _splits(out):
            _gather_copy(sems, a, 4, _piece(out, yn_i, 1), xn).start()
        _gather_copy(sems, a, 6, _piece(out, yn_i), sib).start()


def _gather_finish(srcs, outs, sems):
    x, y, c, me = _my_place()
    sib = _peer(x, y, c, 1)[0]
    xn, yn, dg_i = _peer(x, y, c, 4)[0], _peer(x, y, c, 2)[0], _peer(x, y, c, 6)[1]
    n = len(srcs)
    for a in range(n):
        out = outs[a]
        _gather_copy(sems, a, 3, _piece(out, dg_i, 0), yn).wait_recv()
        _gather_copy(sems, a, 7, _piece(out, dg_i, 0), sib).start()
        if _splits(out):
            _gather_copy(sems, a, 4, _piece(out, dg_i, 1), xn).wait_recv()
            _gather_copy(sems, a, 8, _piece(out, dg_i, 1), sib).start()
    for a in range(n):
        out = outs[a]
        whole, half = _piece(out, me), _piece(out, me, 0)
        for k in (0, 5, 6):
            _gather_copy(sems, a, k, whole, sib).wait_recv()
        for k in (7, 8) if _splits(out) else (7,):
            _gather_copy(sems, a, k, half, sib).wait_recv()
        for k in (0, 1, 2):
            _gather_copy(sems, a, k, whole, sib, src=srcs[a]).wait_send()
        for k in (5, 6):
            _gather_copy(sems, a, k, whole, sib).wait_send()
        for k in (3, 4, 7, 8) if _splits(out) else (3, 7):
            _gather_copy(sems, a, k, half, sib).wait_send()
        pltpu.make_async_copy(srcs[a], whole, sems[2].at[a]).wait()


def _relay_sems(n):
    return [pltpu.SemaphoreType.DMA((n, 9)), pltpu.SemaphoreType.DMA((n, 9)), pltpu.SemaphoreType.DMA((n,))]


def _gather_sems(n):
    return [pltpu.SemaphoreType.DMA((n, 7)), pltpu.SemaphoreType.DMA((n, 7)), pltpu.SemaphoreType.DMA((n,))]


class _Comm:
    def __init__(self, args, out_shape, scratch, start, finish, relay=None):
        self.args, self.out_shape, self.scratch, self.start, self.finish = args, out_shape, scratch, start, finish
        self.relay = relay


def _gather_comm(shards):
    return _Comm(list(shards), [_sds((N_DEV,) + s.shape[1:], s.dtype) for s in shards], _relay_sems(len(shards)),
                 _gather_start, _gather_finish, relay=_gather_relay)


def _direct_copies(ins, outs, sems, wait, from_block):
    send_sems, recv_sems, local_sems = sems
    x, y, c, me = _my_place()
    for a in range(len(ins)):
        src = ins[a].at[pl.ds(me, 1)] if from_block[a] else ins[a]
        local = pltpu.make_async_copy(src, outs[a].at[pl.ds(me, 1)], local_sems.at[a])
        local.wait() if wait else local.start()
        for k in range(1, N_DEV):
            to, to_i = _peer(x, y, c, k)
            cp = pltpu.make_async_remote_copy(
                src_ref=ins[a].at[pl.ds(to_i, 1)] if from_block[a] else ins[a], dst_ref=outs[a].at[pl.ds(me, 1)],
                send_sem=send_sems.at[a, k - 1], recv_sem=recv_sems.at[a, k - 1], device_id=to, device_id_type=MESH)
            cp.wait() if wait else cp.start()


def _exchange_comm(grads=(), parts=()):
    ins = list(grads) + list(parts)
    from_block = [True] * len(grads) + [False] * len(parts)
    out_shape = [_sds(g.shape, g.dtype) for g in grads] + [_sds((N_DEV,) + p.shape[1:], p.dtype) for p in parts]

    def start(ins_, outs_, sems_):
        _direct_copies(ins_, outs_, sems_, False, from_block)

    def finish(ins_, outs_, sems_):
        _direct_copies(ins_, outs_, sems_, True, from_block)

    return _Comm(ins, out_shape, _gather_sems(len(ins)), start, finish)


def _chip_copies(ins, outs, sems, wait, rels, own):
    send_sems, recv_sems, local_sems = sems
    x, y, c, _ = _my_place()
    for a in range(len(ins)):
        if own:
            local = pltpu.make_async_copy(ins[a].at[pl.ds(2 * x + y, 1)], outs[a].at[pl.ds(len(rels), 1)],
                                          local_sems.at[a])
            local.wait() if wait else local.start()
        for i, j in enumerate(rels):
            to = _peer(x, y, c, CHIPS[j])[0]
            cp = pltpu.make_async_remote_copy(
                src_ref=ins[a].at[pl.ds(2 * to[0] + to[1], 1)], dst_ref=outs[a].at[pl.ds(i, 1)],
                send_sem=send_sems.at[a, i], recv_sem=recv_sems.at[a, i], device_id=to, device_id_type=MESH)
            cp.wait() if wait else cp.start()


def _chip_exchange_comm(sums, rels=(0, 1, 2), own=True):
    def start(ins_, outs_, sems_):
        _chip_copies(ins_, outs_, sems_, False, rels, own)

    def finish(ins_, outs_, sems_):
        _chip_copies(ins_, outs_, sems_, True, rels, own)

    n = len(sums)
    sems = [pltpu.SemaphoreType.DMA((n, len(rels))), pltpu.SemaphoreType.DMA((n, len(rels))),
            pltpu.SemaphoreType.DMA((n,))]
    return _Comm(list(sums), [_sds((len(rels) + own,) + s.shape[1:], s.dtype) for s in sums], sems, start, finish)


def _pair_reduce(name, grads):
    n = len(grads)
    n_chips = N_DEV // 2

    def body(*refs):
        g_refs, gh_refs, p_refs, land = refs[:n], refs[n:2 * n], refs[2 * n:3 * n], refs[3 * n:4 * n]
        send_sems, recv_sems = refs[4 * n:]
        x, y, c, _ = _my_place()
        sib = _peer(x, y, c, 1)[0]
        q = pl.program_id(0)

        def to_sibling(a, j):
            return pltpu.make_async_remote_copy(
                src_ref=gh_refs[a].at[j, pl.ds(1 - c, 1)], dst_ref=land[a].at[pl.ds(j, 1)],
                send_sem=send_sems.at[a, j], recv_sem=recv_sems.at[a, j], device_id=sib, device_id_type=MESH)

        @pl.when(q == 0)
        def _():
            for j in range(n_chips):
                for a in range(n):
                    to_sibling(a, j).start()

        for a in range(n):
            to_sibling(a, q).wait_recv()
            p_refs[a][...] = (g_refs[a][0, pl.ds(c, 1)].astype(F32) + land[a][pl.ds(q, 1)].astype(F32)).astype(BF16)

        @pl.when(q == n_chips - 1)
        def _():
            for a in range(n):
                for j in range(n_chips):
                    to_sibling(a, j).wait_send()

    views = [g.reshape((n_chips, 2) + g.shape[1:]) for g in grads]
    res = pl.pallas_call(
        body, name=name, grid=(n_chips,),
        in_specs=[pl.BlockSpec((1, 2) + g.shape[1:], lambda q: (q, 0, 0, 0)) for g in grads] + [ANY] * n,
        out_specs=[pl.BlockSpec((1,) + g.shape[1:], lambda q: (q, 0, 0)) for g in grads],
        out_shape=[_sds((n_chips,) + g.shape[1:], BF16) for g in grads],
        scratch_shapes=[pltpu.VMEM((n_chips,) + g.shape[1:], BF16) for g in grads]
        + [pltpu.SemaphoreType.DMA((n, n_chips)), pltpu.SemaphoreType.DMA((n, n_chips))],
        compiler_params=pltpu.CompilerParams(dimension_semantics=("arbitrary",), vmem_limit_bytes=VMEM_LIMIT),
    )(*views, *views)
    return list(res)


def _call(name, body, grid, in_specs, out_specs, out_shape, args, scratch=(), comm=None):
    params = pltpu.CompilerParams(dimension_semantics=("arbitrary",) * len(grid), vmem_limit_bytes=VMEM_LIMIT)
    if comm is None:
        outs = pl.pallas_call(body, name=name, grid=grid, in_specs=list(in_specs), out_specs=list(out_specs),
                              out_shape=list(out_shape), scratch_shapes=list(scratch), compiler_params=params)(*args)
        return list(outs), []
    ni, nci, no, nco, ns = len(in_specs), len(comm.args), len(out_specs), len(comm.out_shape), len(scratch)

    def carrying(*refs):
        ins, refs = refs[:ni], refs[ni:]
        cin, refs = refs[:nci], refs[nci:]
        outs, refs = refs[:no], refs[no:]
        cout, refs = refs[:nco], refs[nco:]
        scr, csems = refs[:ns], refs[ns:]
        step = pl.program_id(0)
        for ax in range(1, len(grid)):
            step = step * grid[ax] + pl.program_id(ax)
        steps = math.prod(grid)

        @pl.when(step == 0)
        def _():
            comm.start(cin, cout, csems)

        if comm.relay is not None:
            @pl.when(step == (2 * steps) // 3)
            def _():
                comm.relay(cin, cout, csems)

        body(*ins, *outs, *scr)

        @pl.when(step == steps - 1)
        def _():
            comm.finish(cin, cout, csems)

    outs = pl.pallas_call(
        carrying, name=name, grid=grid, in_specs=list(in_specs) + [ANY] * nci, out_specs=list(out_specs) + [ANY] * nco,
        out_shape=list(out_shape) + list(comm.out_shape), scratch_shapes=list(scratch) + list(comm.scratch),
        compiler_params=params)(*args, *comm.args)
    return list(outs[:no]), list(outs[no:])


def _comm_only(name, comm):
    def body(*refs):
        nci, nco = len(comm.args), len(comm.out_shape)
        cin, cout, csems = refs[:nci], refs[nci:nci + nco], refs[nci + nco:]
        comm.start(cin, cout, csems)
        if comm.relay is not None:
            comm.relay(cin, cout, csems)
        comm.finish(cin, cout, csems)

    return pl.pallas_call(body, name=name, in_specs=[ANY] * len(comm.args), out_specs=[ANY] * len(comm.out_shape),
                          out_shape=list(comm.out_shape), scratch_shapes=list(comm.scratch))(*comm.args)


def _chip_exchange_by_sequencer(name, sums, collective_id):
    n = len(sums)
    src = [jax.new_ref(s, memory_space=pltpu.MemorySpace.HBM) for s in sums]
    dst = [jax.empty_ref(_sds(s.shape, s.dtype), memory_space=pltpu.MemorySpace.HBM) for s in sums]

    @pl.kernel(mesh=plsc.ScalarSubcoreMesh(axis_name="sequencer", num_cores=1), name=name,
               scratch_types=(pltpu.SemaphoreType.DMA((n, 3)), pltpu.SemaphoreType.DMA((n, 3)),
                              pltpu.SemaphoreType.DMA((n,))),
               compiler_params=pltpu.CompilerParams(collective_id=collective_id))
    def launch(send_sems, recv_sems, local_sems):
        x, y, c, _ = _my_place()
        barrier = pltpu.get_barrier_semaphore()
        for k in CHIPS:
            pl.semaphore_signal(barrier, inc=1, device_id=_peer(x, y, c, k)[0], device_id_type=MESH)
        pl.semaphore_wait(barrier, len(CHIPS))
        sems = (send_sems, recv_sems, local_sems)
        _chip_copies(src, dst, sems, False, (0, 1, 2), True)
        _chip_copies(src, dst, sems, True, (0, 1, 2), True)

    launch()
    return [d[...] for d in dst]


def _gather_first(first, later):
    nf, nl = len(first), len(later)
    dts = [BF16] * (nf - 2) + [F32, F32]

    def body(*refs):
        ins, refs = refs[:nf + nl], refs[nf + nl:]
        outs, refs = refs[:nf], refs[nf:]
        casts, refs = refs[:nl], refs[nl:]
        stage, sems = refs[:nf], refs[nf:]
        for a in range(nf):
            stage[a][...] = ins[a][...].astype(dts[a])
            _gather_start(stage, outs, sems, only=a)
        for a in range(nl):
            casts[a][...] = ins[nf + a][...].astype(BF16)
        _gather_relay(stage, outs, sems)
        _gather_finish(stage, outs, sems)

    res = pl.pallas_call(
        body, name="gather_first",
        in_specs=[VMEM] * (nf + nl), out_specs=[ANY] * nf + [VMEM] * nl,
        out_shape=[_sds((N_DEV,) + s.shape[1:], dt) for s, dt in zip(first, dts)]
        + [_sds(s.shape, BF16) for s in later],
        scratch_shapes=[pltpu.VMEM(s.shape, dt) for s, dt in zip(first, dts)] + _relay_sems(nf),
        compiler_params=pltpu.CompilerParams(vmem_limit_bytes=VMEM_LIMIT),
    )(*first, *later)
    return list(res[:nf]), list(res[nf:])


def _a_mix_fwd(x, g, w_in, ln_g, ln_b, w_s, b_st, w_out, comm=None):
    t = x.shape[0]
    nblk = TM // GMLP_BLOCK

    def body(x_ref, g_ref, win_ref, lng_ref, lnb_ref, ws_ref, bst_ref, wout_ref, h_ref, z_ref, gated_scr):
        xv = x_ref[...]
        hb = _rms_fwd(xv, g_ref[...])[0].astype(BF16)
        for d in range(N_DEV):
            z_ref[:, d * FF_SLOT:(d + 1) * FF_SLOT] = _dot(hb, win_ref[d])
        u = _gelu(z_ref[:, :GATE_DIM])
        vb = _ln_fwd(_gelu(z_ref[:, GATE_DIM:]), lng_ref[...], lnb_ref[...])[0].astype(BF16)
        mask = _gate_mask()
        for gi in range(A_GROUPS):
            wm = jnp.where(mask, ws_ref[gi], 0.0).astype(BF16)
            bias = bst_ref[:, gi:gi + 1]
            cs = slice(gi * A_GROUP_DIM, (gi + 1) * A_GROUP_DIM)
            for n in range(nblk):
                rs = slice(n * GMLP_BLOCK, (n + 1) * GMLP_BLOCK)
                sv = _dot(wm, vb[rs, cs]) + bias
                gated_scr[rs, cs] = (u[rs, cs] * sv).astype(BF16)
        h_ref[...] = xv + _dot(gated_scr[...], wout_ref[...])

    return _call(
        "a_mix_fwd", body, (t // TM,),
        [_row(D_MODEL), _res((1, D_MODEL)), _res((N_DEV, D_MODEL, FF_SLOT)), _res((1, GATE_DIM)),
         _res((1, GATE_DIM)), _res((A_GROUPS, GMLP_BLOCK, GMLP_BLOCK)), _res((GMLP_BLOCK, A_GROUPS)),
         _res((GATE_DIM, D_MODEL))],
        [_row(D_MODEL), _row(2 * GATE_DIM), _row(GATE_DIM)],
        [_sds((t, D_MODEL), F32), _sds((t, 2 * GATE_DIM), F32), _sds((t, GATE_DIM), BF16)],
        (x, g, w_in, ln_g, ln_b, w_s, b_st, w_out), comm=comm)


MLP_W_SPECS = (_res((N_DEV, D_MODEL, FF_SLOT)), _res((N_DEV, FF_SLOT, D_MODEL)))


def _mlp_fwd(h, g, w1, w2, layer, comm=None):
    t = h.shape[0]

    def body(h_ref, g_ref, w1_ref, w2_ref, o_ref, a_ref):
        hv = h_ref[...]
        hb = _rms_fwd(hv, g_ref[...])[0].astype(BF16)
        o_ref[...] = hv
        for d in range(N_DEV):
            a = _dot(hb, w1_ref[d])
            a_ref[:, d * FF_SLOT:(d + 1) * FF_SLOT] = a
            r = jnp.maximum(a, 0.0)
            o_ref[...] += _dot((r * r).astype(BF16), w2_ref[d])

    return _call(
        f"mlp_fwd_{layer}", body, (t // TM,), [_row(D_MODEL), _res((1, D_MODEL)), *MLP_W_SPECS],
        [_row(D_MODEL), _row(D_FF)], [_sds((t, D_MODEL), F32), _sds((t, D_FF), F32)], (h, g, w1, w2), comm=comm)


def _mlp_fwd_loss(h, g, w1, w2, final_g, target):
    t = h.shape[0]

    def body(h_ref, g_ref, w1_ref, w2_ref, fg_ref, t_ref, a_ref, loss_ref, dh_ref, dg_ref):
        hv = h_ref[...]
        hb = _rms_fwd(hv, g_ref[...])[0].astype(BF16)
        out = hv
        for d in range(N_DEV):
            a = _dot(hb, w1_ref[d])
            a_ref[:, d * FF_SLOT:(d + 1) * FF_SLOT] = a
            r = jnp.maximum(a, 0.0)
            out = out + _dot((r * r).astype(BF16), w2_ref[d])
        y, xhat, rstd = _rms_fwd(out, fg_ref[...])
        err = y - t_ref[...]
        part = 0.5 * jnp.sum(jnp.mean(err * err, axis=-1, keepdims=True), axis=0, keepdims=True)
        dx, dg = _rms_bwd(err * (1.0 / D_MODEL), xhat, rstd, fg_ref[...])
        dh_ref[...] = dx
        _acc(dg_ref, dg)
        _acc(loss_ref, part)

    return _call(
        "mlp_fwd_loss", body, (t // TM,),
        [_row(D_MODEL), _res((1, D_MODEL)), *MLP_W_SPECS, _res((1, D_MODEL)), _row(D_MODEL)],
        [_row(D_FF), _const((1, 1)), _row(D_MODEL), _const((1, D_MODEL))],
        [_sds((t, D_FF), F32), _sds((1, 1), F32), _sds((t, D_MODEL), F32), _sds((1, D_MODEL), F32)],
        (h, g, w1, w2, final_g, target))[0]


KVQ_W_SPECS = (_res((1, D_MODEL)), _res((D_MODEL, KV_LORA + QK_ROPE)), _res((1, KV_LORA)),
               _res((B_HEADS, KV_LORA, QK_NOPE + V_HEAD)), _res((1, D_MODEL)), _res((D_MODEL, Q_LORA)),
               _res((1, Q_LORA)), _res((B_HEADS, Q_LORA, QK_NOPE + QK_ROPE)))


def _kvq_fwd(h, pos, inv_freq, kvq_w):
    t = h.shape[0]
    half = QK_ROPE // 2

    def body(h_ref, pos_ref, invf_ref, srcg_ref, wkva_ref, kvag_ref, wkvb_ref, mixg_ref, wqa_ref, qg_ref, wqb_ref,
             ckv_ref, k_ref, v_ref, cqpre_ref, q_ref, cos_ref, sin_ref):
        hv = h_ref[...]
        xhat = hv * lax.rsqrt(jnp.mean(hv * hv, axis=-1, keepdims=True) + EPS)
        ang = pos_ref[...].astype(F32) * invf_ref[...]
        cos, sin = jnp.cos(ang), jnp.sin(ang)
        cos_ref[...] = cos
        sin_ref[...] = sin
        ckv = _dot((xhat * srcg_ref[...]).astype(BF16), wkva_ref[...])
        ckv_ref[...] = ckv
        cb = _rms_fwd(ckv[:, :KV_LORA], kvag_ref[...])[0].astype(BF16)
        kpe = _rope(ckv[:, KV_LORA:], cos, sin).astype(BF16)
        for hd in range(B_HEADS):
            kv = _dot(cb, wkvb_ref[hd])
            k_ref[hd, :, 0:QK_NOPE] = kv[:, :QK_NOPE].astype(BF16)
            k_ref[hd, :, QK_NOPE:] = kpe
            v_ref[hd] = kv[:, QK_NOPE:].astype(BF16)
        cqpre = _dot((xhat * mixg_ref[...]).astype(BF16), wqa_ref[...])
        cqpre_ref[...] = cqpre
        cqb = _rms_fwd(cqpre, qg_ref[...])[0].astype(BF16)
        for hd in range(B_HEADS):
            q = _dot(cqb, wqb_ref[hd])
            q_ref[hd, :, 0:QK_NOPE] = q[:, :QK_NOPE].astype(BF16)
            q_ref[hd, :, QK_NOPE:] = _rope(q[:, QK_NOPE:], cos, sin).astype(BF16)

    return _call(
        "kvq_fwd", body, (t // TM,), [_row(D_MODEL), _row(1), _res((1, half)), *KVQ_W_SPECS],
        [_row(KV_LORA + QK_ROPE), _heads(QK_NOPE + QK_ROPE), _heads(V_HEAD), _row(Q_LORA),
         _heads(QK_NOPE + QK_ROPE), _row(half), _row(half)],
        [_sds((t, KV_LORA + QK_ROPE), F32), _sds((B_HEADS, t, QK_NOPE + QK_ROPE), BF16),
         _sds((B_HEADS, t, V_HEAD), BF16), _sds((t, Q_LORA), F32), _sds((B_HEADS, t, QK_NOPE + QK_ROPE), BF16),
         _sds((t, half), F32), _sds((t, half), F32)],
        (h, pos, inv_freq, *kvq_w))[0]


def _softmax_rows(q, k_ref, k):
    past, upto = k * TM, (k + 1) * TM
    s = _dot_nt(q, k_ref[0:upto, :])
    own = jnp.where(_att_mask(0, TM, TM), s[:, past:], jnp.finfo(F32).min)
    s = own if k == 0 else jnp.concatenate([s[:, :past], own], axis=1)
    e = jnp.exp2((s - jnp.max(s, axis=-1, keepdims=True)) * (ATT_SCALE * LOG2_E))
    return e * (1.0 / jnp.sum(e, axis=-1, keepdims=True))


def _for_my_tile(i, nq, fn):
    for k in range(nq):
        @pl.when(i == k)
        def _(k=k):
            fn(k)


def _attn_fwd(h, q, k, v, w_o, comm=None):
    t = h.shape[0]
    nq, hps = t // TM, HEADS_PER_STEP

    def body(h_ref, q_ref, k_ref, v_ref, wo_ref, o_ref, att_ref):
        i, pair = pl.program_id(0), pl.program_id(1)

        @pl.when(pair == 0)
        def _():
            o_ref[...] = h_ref[...]

        def tile(kt):
            proj = None
            for j in range(hps):
                hd = pair * hps + j
                p = _softmax_rows(q_ref[j], k_ref.at[hd], kt)
                ob = _dot(p.astype(BF16), v_ref[hd, 0:(kt + 1) * TM, :]).astype(BF16)
                att_ref[j] = ob
                proj = _dot(ob, wo_ref[hd]) if proj is None else proj + _dot(ob, wo_ref[hd])
            o_ref[...] += proj

        _for_my_tile(i, nq, tile)

    def per_head(d):
        return pl.BlockSpec((hps, TM, d), lambda i, pair: (pair, i, 0))

    def resident(shape):
        zeros = (0,) * len(shape)
        return pl.BlockSpec(shape, lambda i, pair: zeros, pipeline_mode=pl.Buffered(1))

    tile_spec = pl.BlockSpec((TM, D_MODEL), lambda i, pair: (i, 0))
    return _call(
        "attn_fwd", body, (nq, B_HEADS // hps),
        [tile_spec, per_head(QK_NOPE + QK_ROPE), resident((B_HEADS, t, QK_NOPE + QK_ROPE)),
         resident((B_HEADS, t, V_HEAD)), resident((B_HEADS, V_HEAD, D_MODEL))],
        [tile_spec, per_head(V_HEAD)], [_sds((t, D_MODEL), F32), _sds((B_HEADS, t, V_HEAD), BF16)],
        (h, q, k, v, w_o), comm=comm)


def _mlp_bwd(h, a, dho, g, w1, w2, layer, comm=None):
    t = h.shape[0]

    def body(h_ref, a_ref, dho_ref, g_ref, w1_ref, w2_ref, dhi_ref, dg_ref, hn_ref, f_ref, da_ref, dhib_ref):
        gv = g_ref[...]
        y, xhat, rstd = _rms_fwd(h_ref[...], gv)
        hn_ref[...] = y.astype(BF16)
        dho_v = dho_ref[...]
        dhob = dho_v.astype(BF16)
        dhn = jnp.zeros((TM, D_MODEL), F32)
        for d in range(N_DEV):
            cs = slice(d * FF_SLOT, (d + 1) * FF_SLOT)
            r = jnp.maximum(a_ref[:, cs], 0.0)
            f_ref[:, cs] = (r * r).astype(BF16)
            da = (_dot_nt(dhob, w2_ref[d]) * (2.0 * r)).astype(BF16)
            da_ref[:, cs] = da
            dhn = dhn + _dot_nt(da, w1_ref[d])
        dx, dg = _rms_bwd(dhn, xhat, rstd, gv)
        dhi = dho_v + dx
        dhi_ref[...] = dhi
        dhib_ref[...] = dhi.astype(BF16)
        _acc(dg_ref, dg)

    return _call(
        f"mlp_bwd_{layer}", body, (t // TM,),
        [_row(D_MODEL), _row(D_FF), _row(D_MODEL), _res((1, D_MODEL)), *MLP_W_SPECS],
        [_row(D_MODEL), _const((1, D_MODEL)), _row(D_MODEL), _row(D_FF), _row(D_FF), _row(D_MODEL)],
        [_sds((t, D_MODEL), F32), _sds((1, D_MODEL), F32), _sds((t, D_MODEL), BF16), _sds((t, D_FF), BF16),
         _sds((t, D_FF), BF16), _sds((t, D_MODEL), BF16)],
        (h, a, dho, g, w1, w2), comm=comm)


def _attn_bwd(dh, q, k, v, w_o, cos, sin, comm=None):
    t = dh.shape[0]
    half, hps = QK_ROPE // 2, HEADS_PER_STEP

    def body(dh_ref, q_ref, k_ref, v_ref, wo_ref, cos_ref, sin_ref, dq_ref, dk_ref, dv_ref):
        i = pl.program_id(1)

        @pl.when(i == 0)
        def _():
            dk_ref[...] = jnp.zeros_like(dk_ref)
            dv_ref[...] = jnp.zeros_like(dv_ref)

        def tile(kt):
            keys = slice(0, (kt + 1) * TM)
            for j in range(hps):
                qj = q_ref[j]
                do = _dot_nt(dh_ref[kt * TM:(kt + 1) * TM, :], wo_ref[j]).astype(BF16)
                p = _softmax_rows(qj, k_ref.at[j], kt)
                dp = _dot_nt(do, v_ref[j, keys, :])
                ds = (p * (dp - jnp.sum(p * dp, axis=-1, keepdims=True)) * ATT_SCALE).astype(BF16)
                dq = _dot(ds, k_ref[j, keys, :])
                dq_ref[j, :, 0:QK_NOPE] = dq[:, :QK_NOPE].astype(BF16)
                dq_ref[j, :, QK_NOPE:] = _rope(dq[:, QK_NOPE:], cos_ref[...], -sin_ref[...]).astype(BF16)
                dk_ref[j, keys, :] += _dot_tn(ds, qj)
                dv_ref[j, keys, :] += _dot_tn(p.astype(BF16), do)

        _for_my_tile(i, t // TM, tile)

    def per_pair(rows, d, tiled):
        return pl.BlockSpec((hps, rows, d), (lambda pair, i: (pair, i, 0)) if tiled else (lambda pair, i: (pair, 0, 0)))

    def tile(d):
        return pl.BlockSpec((TM, d), lambda pair, i: (i, 0))

    return _call(
        "attn_bwd", body, (B_HEADS // hps, t // TM),
        [pl.BlockSpec((t, D_MODEL), lambda pair, i: (0, 0), pipeline_mode=pl.Buffered(1)),
         per_pair(TM, QK_NOPE + QK_ROPE, True), per_pair(t, QK_NOPE + QK_ROPE, False), per_pair(t, V_HEAD, False),
         per_pair(V_HEAD, D_MODEL, False), tile(half), tile(half)],
        [per_pair(TM, QK_NOPE + QK_ROPE, True), per_pair(t, QK_NOPE + QK_ROPE, False), per_pair(t, V_HEAD, False)],
        [_sds((B_HEADS, t, QK_NOPE + QK_ROPE), BF16), _sds((B_HEADS, t, QK_NOPE + QK_ROPE), F32),
         _sds((B_HEADS, t, V_HEAD), F32)],
        (dh, q, k, v, w_o, cos, sin), comm=comm)


def _kvq_bwd(h, dh, ckv, cqpre, dq, dk, dv, cos, sin, kvq_w):
    t = h.shape[0]
    half = QK_ROPE // 2

    def body(h_ref, dh_ref, ckv_ref, cqpre_ref, dq_ref, dk_ref, dv_ref, cos_ref, sin_ref,
             srcg_ref, wkva_ref, kvag_ref, wkvb_ref, mixg_ref, wqa_ref, qg_ref, wqb_ref,
             dhi_ref, hq_ref, hk_ref, cq_ref, dcqpre_ref, c_ref, dkv_ref, dckv_ref,
             dmixg_ref, dsrcg_ref, dqg_ref, dkvag_ref):
        hv = h_ref[...]
        rstd = lax.rsqrt(jnp.mean(hv * hv, axis=-1, keepdims=True) + EPS)
        xhat = hv * rstd
        mixg, srcg, qg, kvag = mixg_ref[...], srcg_ref[...], qg_ref[...], kvag_ref[...]
        hq_ref[...] = (xhat * mixg).astype(BF16)
        hk_ref[...] = (xhat * srcg).astype(BF16)
        cq, cqhat, crstd = _rms_fwd(cqpre_ref[...], qg)
        cq_ref[...] = cq.astype(BF16)
        dcq = jnp.zeros((TM, Q_LORA), F32)
        for hd in range(B_HEADS):
            dcq = dcq + _dot_nt(dq_ref[hd], wqb_ref[hd])
        dcqpre, dqg = _rms_bwd(dcq, cqhat, crstd, qg)
        dcqpre_b = dcqpre.astype(BF16)
        dcqpre_ref[...] = dcqpre_b
        dxq, dmixg = _rms_bwd(_dot_nt(dcqpre_b, wqa_ref[...]), xhat, rstd, mixg)
        ckv = ckv_ref[...]
        c, chat, krstd = _rms_fwd(ckv[:, :KV_LORA], kvag)
        c_ref[...] = c.astype(BF16)
        dc = jnp.zeros((TM, KV_LORA), F32)
        dkpe = jnp.zeros((TM, QK_ROPE), F32)
        for hd in range(B_HEADS):
            dkv = jnp.concatenate([dk_ref[hd, :, 0:QK_NOPE], dv_ref[hd]], axis=-1).astype(BF16)
            dkv_ref[hd] = dkv
            dc = dc + _dot_nt(dkv, wkvb_ref[hd])
            dkpe = dkpe + dk_ref[hd, :, QK_NOPE:]
        dlat, dkvag = _rms_bwd(dc, chat, krstd, kvag)
        dpe = _rope(dkpe, cos_ref[...], -sin_ref[...])
        dckv_b = jnp.concatenate([dlat, dpe], axis=-1).astype(BF16)
        dckv_ref[...] = dckv_b
        dxk, dsrcg = _rms_bwd(_dot_nt(dckv_b, wkva_ref[...]), xhat, rstd, srcg)
        dhi_ref[...] = dh_ref[...] + dxq + dxk
        _acc(dmixg_ref, dmixg)
        _acc(dsrcg_ref, dsrcg)
        _acc(dqg_ref, dqg)
        _acc(dkvag_ref, dkvag)

    return _call(
        "kvq_bwd", body, (t // TM,),
        [_row(D_MODEL), _row(D_MODEL), _row(KV_LORA + QK_ROPE), _row(Q_LORA), _heads(QK_NOPE + QK_ROPE),
         _heads(QK_NOPE + QK_ROPE), _heads(V_HEAD), _row(half), _row(half), *KVQ_W_SPECS],
        [_row(D_MODEL), _row(D_MODEL), _row(D_MODEL), _row(Q_LORA), _row(Q_LORA), _row(KV_LORA),
         _heads(QK_NOPE + V_HEAD), _row(KV_LORA + QK_ROPE),
         _const((1, D_MODEL)), _const((1, D_MODEL)), _const((1, Q_LORA)), _const((1, KV_LORA))],
        [_sds((t, D_MODEL), F32), _sds((t, D_MODEL), BF16), _sds((t, D_MODEL), BF16), _sds((t, Q_LORA), BF16),
         _sds((t, Q_LORA), BF16), _sds((t, KV_LORA), BF16), _sds((B_HEADS, t, QK_NOPE + V_HEAD), BF16),
         _sds((t, KV_LORA + QK_ROPE), BF16),
         _sds((1, D_MODEL), F32), _sds((1, D_MODEL), F32), _sds((1, Q_LORA), F32), _sds((1, KV_LORA), F32)],
        (h, dh, ckv, cqpre, dq, dk, dv, cos, sin, *kvq_w))[0]


def _a_mix_bwd(x, z, dh, g, w_in, ln_g, ln_b, w_s, b_st, w_out, comm=None):
    t = x.shape[0]
    tm = TM_GATE
    nblk = tm // GMLP_BLOCK

    def body(x_ref, z_ref, dh_ref, g_ref, win_ref, lng_ref, lnb_ref, ws_ref, bst_ref, wout_ref,
             dx_ref, hn_ref, dz_ref, dg_ref, dlng_ref, dlnb_ref, dws_ref, dbs_ref, dvn_scr, gelu_grad_v):
        @pl.when(pl.program_id(0) == 0)
        def _():
            dws_ref[...] = jnp.zeros_like(dws_ref)
            dbs_ref[...] = jnp.zeros_like(dbs_ref)

        gv, lng = g_ref[...], lng_ref[...]
        y, xhat, rstd = _rms_fwd(x_ref[...], gv)
        hn_ref[...] = y.astype(BF16)
        dhv = dh_ref[...]
        dgated = _dot_nt(dhv.astype(BF16), wout_ref[...])
        u, gelu_grad_u = _gelu_and_grad(z_ref[:, :GATE_DIM])
        v, gelu_grad_v[...] = _gelu_and_grad(z_ref[:, GATE_DIM:])
        vn, vhat, lrstd = _ln_fwd(v, lng, lnb_ref[...])
        vb = vn.astype(BF16)
        mask = _gate_mask()
        for gi in range(A_GROUPS):
            wm = jnp.where(mask, ws_ref[gi], 0.0).astype(BF16)
            bias = bst_ref[:, gi:gi + 1]
            cs = slice(gi * A_GROUP_DIM, (gi + 1) * A_GROUP_DIM)
            dws = jnp.zeros((GMLP_BLOCK, GMLP_BLOCK), F32)
            dbs = jnp.zeros((GMLP_BLOCK, 1), F32)
            for n in range(nblk):
                rs = slice(n * GMLP_BLOCK, (n + 1) * GMLP_BLOCK)
                sv = _dot(wm, vb[rs, cs]) + bias
                dz_ref[rs, cs] = (dgated[rs, cs] * sv * gelu_grad_u[rs, cs]).astype(BF16)
                dsv = dgated[rs, cs] * u[rs, cs]
                dsvb = dsv.astype(BF16)
                dws = dws + _dot_nt(dsvb, vb[rs, cs])
                dbs = dbs + jnp.sum(dsv, axis=-1, keepdims=True)
                dvn_scr[rs, cs] = _dot_tn(wm, dsvb)
            dws_ref[gi] += jnp.where(mask, dws, 0.0)
            dbs_ref[gi] += dbs
        dvn = dvn_scr[...]
        dvhat = dvn * lng
        dv = lrstd * (dvhat - jnp.mean(dvhat, axis=-1, keepdims=True)
                      - vhat * jnp.mean(dvhat * vhat, axis=-1, keepdims=True))
        dz_ref[:, GATE_DIM:] = (dv * gelu_grad_v[...]).astype(BF16)
        dhn = jnp.zeros((tm, D_MODEL), F32)
        for d in range(N_DEV):
            dhn = dhn + _dot_nt(dz_ref[:, d * FF_SLOT:(d + 1) * FF_SLOT], win_ref[d])
        dx, dg = _rms_bwd(dhn, xhat, rstd, gv)
        dx_ref[...] = dhv + dx
        _acc(dg_ref, dg)
        _acc(dlng_ref, jnp.sum(dvn * vhat, axis=0, keepdims=True))
        _acc(dlnb_ref, jnp.sum(dvn, axis=0, keepdims=True))

    return _call(
        "a_mix_bwd", body, (t // tm,),
        [_row(D_MODEL, tm), _row(2 * GATE_DIM, tm), _row(D_MODEL, tm), _res((1, D_MODEL)),
         _res((N_DEV, D_MODEL, FF_SLOT)), _res((1, GATE_DIM)), _res((1, GATE_DIM)),
         _res((A_GROUPS, GMLP_BLOCK, GMLP_BLOCK)), _res((GMLP_BLOCK, A_GROUPS)), _res((GATE_DIM, D_MODEL))],
        [_row(D_MODEL, tm), _row(D_MODEL, tm), _row(2 * GATE_DIM, tm),
         _const((1, D_MODEL)), _const((1, GATE_DIM)), _const((1, GATE_DIM)),
         _const((A_GROUPS, GMLP_BLOCK, GMLP_BLOCK)), _const((A_GROUPS, GMLP_BLOCK, 1))],
        [_sds((t, D_MODEL), F32), _sds((t, D_MODEL), BF16),
         _sds((t, 2 * GATE_DIM), BF16), _sds((1, D_MODEL), F32), _sds((1, GATE_DIM), F32),
         _sds((1, GATE_DIM), F32), _sds((A_GROUPS, GMLP_BLOCK, GMLP_BLOCK), F32),
         _sds((A_GROUPS, GMLP_BLOCK, 1), F32)],
        (x, z, dh, g, w_in, ln_g, ln_b, w_s, b_st, w_out),
        scratch=[pltpu.VMEM((tm, GATE_DIM), F32), pltpu.VMEM((tm, GATE_DIM), F32)], comm=comm)


def _wgrad(name, a, b, a_spec, b_spec, m, n, comm=None):
    def body(a_ref, b_ref, o_ref):
        o_ref[0] = _dot_tn(a_ref[...].astype(BF16), b_ref[...].astype(BF16)).astype(BF16)

    outs, got = _call(name, body, (N_DEV,), [a_spec, b_spec], [pl.BlockSpec((1, m, n), lambda d: (d, 0, 0))],
                      [_sds((N_DEV, m, n), BF16)], (a, b), comm=comm)
    return outs[0] if comm is None else (outs[0], got)


def _full(t, d):
    return pl.BlockSpec((t, d), lambda i: (0, 0), pipeline_mode=pl.Buffered(1))


def _cols(t, d):
    return pl.BlockSpec((t, d), lambda i: (0, i))


def _head(t, d):
    return pl.BlockSpec((None, t, d), lambda i: (i, 0, 0))


def _local_step(x, pos, target, inv_freq, wg, sm, shards=None):
    t = x.shape[0]
    wg = dict(wg)
    dist = shards is not None
    mix_g = [sm["norm_mix_g"][l:l + 1] for l in range(2)]
    mlp_g = [sm["norm_mlp_g"][l:l + 1] for l in range(2)]

    def gather(names):
        return _gather_comm([shards[k] for k in names]) if dist else None

    def send(grads):
        return _exchange_comm(grads=grads) if dist else None

    def send_sums(name, grads):
        return _chip_exchange_comm(_pair_reduce(name, grads)) if dist else None

    def a_args():
        return (wg["a_w_in"], wg["a_ln_v_g"], wg["a_ln_v_b"], sm["a_w_s"], sm["a_b_st"], wg["a_w_out"])

    def kvq_w():
        return (sm["kv_src_norm_g"], wg["kv_w_a"], sm["kv_a_norm_g"], wg["kv_w_b"], mix_g[1], wg["b_w_q_a"],
                sm["b_q_norm_g"], wg["b_w_q_b"])

    names = ("mlp_w1_0", "mlp_w2_0")
    (h1, z, gated), got = _a_mix_fwd(x, mix_g[0], *a_args(), comm=gather(names))
    wg.update(zip(names, got))
    names = ("kv_w_a", "kv_w_b", "b_w_q_a", "b_w_q_b", "b_w_o")
    (h2, a0), got = _mlp_fwd(h1, mlp_g[0], wg["mlp_w1_0"], wg["mlp_w2_0"], 0, comm=gather(names))
    wg.update(zip(names, got))
    if dist:
        wg["b_w_q_a"] = wg["b_w_q_a"].reshape(D_MODEL, Q_LORA)
        wg["kv_w_a"] = wg["kv_w_a"].reshape(D_MODEL, KV_LORA + QK_ROPE)
    ckv, k, v, cqpre, q, cos, sin = _kvq_fwd(h2, pos, inv_freq, kvq_w())
    names = ("mlp_w1_1", "mlp_w2_1")
    (h3, att), got = _attn_fwd(h2, q, k, v, wg["b_w_o"], comm=gather(names))
    wg.update(zip(names, got))
    a1, loss, dh4, d_final_g = _mlp_fwd_loss(h3, mlp_g[1], wg["mlp_w1_1"], wg["mlp_w2_1"], sm["final_norm_g"], target)

    g = {}
    (dh3, d_mlp_g1, hn, f, da, dh3_b), _ = _mlp_bwd(h3, a1, dh4, mlp_g[1], wg["mlp_w1_1"], wg["mlp_w2_1"], 1)
    g["mlp_w1_1"] = _wgrad("wgrad_w1_1", hn, da, _full(t, D_MODEL), _cols(t, FF_SLOT), D_MODEL, FF_SLOT)
    g["mlp_w2_1"] = _wgrad("wgrad_w2_1", f, dh4, _cols(t, FF_SLOT), _full(t, D_MODEL), FF_SLOT, D_MODEL)
    g["b_w_o"] = _wgrad("wgrad_w_o", att, dh3_b, _head(t, V_HEAD), _full(t, D_MODEL), V_HEAD, D_MODEL)
    names = ("mlp_w1_1", "mlp_w2_1", "b_w_o")
    (dq, dk, dv), got = _attn_bwd(dh3_b, q, k, v, wg["b_w_o"], cos, sin,
                                         comm=send_sums("pair_reduce_1", [g[k] for k in names]))
    g.update(zip(names, got))
    (dh2, hq, hk, cq, dcqpre, c, dkv, dckv, d_mix_g1, d_src_g, d_q_g, d_kv_a_g) = _kvq_bwd(
        h2, dh3, ckv, cqpre, dq, dk, dv, cos, sin, kvq_w())
    g["b_w_q_a"] = _wgrad("wgrad_w_q_a", hq, dcqpre, _cols(t, D_MODEL // N_DEV), _full(t, Q_LORA),
                          D_MODEL // N_DEV, Q_LORA)
    g["b_w_q_b"] = _wgrad("wgrad_w_q_b", cq, dq, _full(t, Q_LORA), _head(t, QK_NOPE + QK_ROPE),
                          Q_LORA, QK_NOPE + QK_ROPE)
    g["kv_w_a"] = _wgrad("wgrad_kv_w_a", hk, dckv, _cols(t, D_MODEL // N_DEV), _full(t, KV_LORA + QK_ROPE),
                         D_MODEL // N_DEV, KV_LORA + QK_ROPE)
    g["kv_w_b"] = _wgrad("wgrad_kv_w_b", c, dkv, _full(t, KV_LORA), _head(t, QK_NOPE + V_HEAD),
                         KV_LORA, QK_NOPE + V_HEAD)
    names = ("b_w_q_a", "b_w_q_b", "kv_w_a", "kv_w_b")
    (dh1, d_mlp_g0, hn, f, da, dh1_b), got = _mlp_bwd(h1, a0, dh2, mlp_g[0], wg["mlp_w1_0"], wg["mlp_w2_0"], 0,
                                                      comm=send([g[k] for k in names]))
    g.update(zip(names, got))
    g["mlp_w1_0"] = _wgrad("wgrad_w1_0", hn, da, _full(t, D_MODEL), _cols(t, FF_SLOT), D_MODEL, FF_SLOT)
    g["mlp_w2_0"] = _wgrad("wgrad_w2_0", f, dh2, _cols(t, FF_SLOT), _full(t, D_MODEL), FF_SLOT, D_MODEL)
    g["a_w_out"] = _wgrad("wgrad_a_w_out", gated, dh1_b, _cols(t, GATE_DIM // N_DEV), _full(t, D_MODEL),
                          GATE_DIM // N_DEV, D_MODEL)
    names = ("mlp_w1_0", "mlp_w2_0")
    (dx, hn, dz, d_mix_g0, d_ln_g, d_ln_b, d_ws, d_bs), got = _a_mix_bwd(
        x, z, dh1, mix_g[0], *a_args(), comm=send_sums("pair_reduce_0", [g[k] for k in names]))
    g.update(zip(names, got))
    small = {
        "norm_mix_g": jnp.concatenate([d_mix_g0, d_mix_g1], axis=0),
        "norm_mlp_g": jnp.concatenate([d_mlp_g0, d_mlp_g1], axis=0),
        "a_ln_v_g": d_ln_g.reshape(N_DEV, GATE_DIM // N_DEV),
        "a_ln_v_b": d_ln_b.reshape(N_DEV, GATE_DIM // N_DEV),
        "a_w_s": d_ws.astype(BF16) if dist else d_ws,
        "a_b_s": d_bs.reshape(A_GROUPS, GMLP_BLOCK),
        "b_q_norm_g": d_q_g,
        "kv_src_norm_g": d_src_g,
        "kv_a_norm_g": d_kv_a_g,
        "final_norm_g": d_final_g,
    }
    wgrad_in = ("wgrad_a_w_in", hn, dz, _full(t, D_MODEL), _cols(t, FF_SLOT), D_MODEL, FF_SLOT)
    if dist:
        parts = [small[k].reshape((1,) + small[k].shape) for k in SMALL] + [loss.reshape(1, 1, 1)]
        g["a_w_in"], got = _wgrad(*wgrad_in, comm=_exchange_comm(parts=parts))
        small, loss = dict(zip(SMALL, got)), got[-1]
    else:
        g["a_w_in"] = _wgrad(*wgrad_in)
    return loss, dx, g, small


def _adamw(w, g, m, v):
    m = ADAM_B1 * m + (1.0 - ADAM_B1) * g
    v = ADAM_B2 * v + (1.0 - ADAM_B2) * (g * g)
    m_hat = m / (1.0 - ADAM_B1 ** ADAM_STEP)
    v_hat = v / (1.0 - ADAM_B2 ** ADAM_STEP)
    return -ADAM_LR * (m_hat / (jnp.sqrt(v_hat) + ADAM_EPS) + ADAM_WD * w), m, v


def _sum_in_device_order(r_ref):
    g = r_ref[0].astype(F32)
    for j in range(1, r_ref.shape[0]):
        g = g + r_ref[j].astype(F32)
    return g


def _adamw_sharded(name, recvs, w, m, v, comm=None):
    layers, r, c = w.shape
    tr = math.gcd(r, 512)
    flat = [a for per_layer in recvs for a in per_layer]

    def body(*refs):
        r_refs, (w_ref, m_ref, v_ref) = refs[:len(flat)], refs[len(flat):len(flat) + 3]
        g_ref, d_ref, nm_ref, nv_ref = refs[-4:]
        layer = pl.program_id(0)
        g, pos = None, 0
        for li, per_layer in enumerate(recvs):
            total = None
            for ref in r_refs[pos:pos + len(per_layer)]:
                part = _sum_in_device_order(ref)
                total = part if total is None else total + part
            pos += len(per_layer)
            g = total if g is None else jnp.where(layer == li, total, g)
        g_ref[...] = g
        d_ref[...], nm_ref[...], nv_ref[...] = _adamw(w_ref[...], g, m_ref[...], v_ref[...])

    blk = pl.BlockSpec((None, tr, c), lambda l, i: (l, i, 0))
    return _call(name, body, (layers, r // tr),
                 [pl.BlockSpec((a.shape[0], tr, c), lambda l, i: (0, i, 0)) for a in flat] + [blk] * 3,
                 [blk] * 4, [_sds(w.shape, F32)] * 4, (*flat, w, m, v), comm=comm)


def _adamw_small(recvs, ws, ms, vs, own_row, losses):
    n = len(recvs)

    def body(*refs):
        r_refs, w_refs, m_refs, v_refs = (refs[i * n:(i + 1) * n] for i in range(4))
        outs, scr = refs[4 * n + 1:8 * n + 2], refs[8 * n + 2:]
        outs[-1][...] = _sum_in_device_order(refs[4 * n])
        me = _my_place()[3]
        for a in range(n):
            g = _sum_in_device_order(r_refs[a])
            if own_row[a]:
                scr[0][...] = g
                g = scr[0][pl.ds(me, 1), :]
            g_ref, d_ref, nm_ref, nv_ref = outs[4 * a:4 * a + 4]
            g_ref[...] = g
            d_ref[...], nm_ref[...], nv_ref[...] = _adamw(w_refs[a][...], g, m_refs[a][...], v_refs[a][...])

    out_shape = []
    for w in ws:
        out_shape += [_sds(w.shape, F32)] * 4
    return pl.pallas_call(
        body, name="adamw_small", in_specs=[VMEM] * (4 * n + 1), out_specs=[VMEM] * (4 * n + 1),
        out_shape=out_shape + [_sds((1, 1), F32)], scratch_shapes=[pltpu.VMEM((N_DEV, GATE_DIM // N_DEV), F32)],
    )(*recvs, *ws, *ms, *vs, losses)


BIG = ("a_w_in", "a_w_out", "b_w_q_a", "b_w_q_b", "b_w_o", "kv_w_a", "kv_w_b", "mlp_w1", "mlp_w2")
SMALL = ("norm_mix_g", "norm_mlp_g", "a_ln_v_g", "a_ln_v_b", "a_w_s", "a_b_s", "b_q_norm_g", "kv_src_norm_g",
         "kv_a_norm_g", "final_norm_g")
WEIGHTS = ("norm_mix_g", "norm_mlp_g", "a_w_in", "a_ln_v_g", "a_ln_v_b", "a_w_s", "a_b_s", "a_w_out", "b_w_q_a",
           "b_q_norm_g", "b_w_q_b", "b_w_o", "kv_src_norm_g", "kv_w_a", "kv_a_norm_g", "kv_w_b", "mlp_w1", "mlp_w2",
           "final_norm_g")


def _two_d(name, a):
    if name in ("a_w_s", "a_b_s"):
        return a.reshape(a.shape[1:])
    return a.reshape(1, -1) if a.ndim == 1 else a


def _three_d(a):
    return a if a.ndim == 3 else a.reshape((1,) + a.shape)


def kernel(x, positions, norm_mix_g, norm_mlp_g, a_w_in, a_ln_v_g, a_ln_v_b, a_w_s, a_b_s, a_w_out, b_w_q_a, b_q_norm_g, b_w_q_b, b_w_o, kv_src_norm_g, kv_w_a, kv_a_norm_g, kv_w_b, mlp_w1, mlp_w2, final_norm_g, loss_target, m_norm_mix_g, m_norm_mlp_g, m_a_w_in, m_a_ln_v_g, m_a_ln_v_b, m_a_w_s, m_a_b_s, m_a_w_out, m_b_w_q_a, m_b_q_norm_g, m_b_w_q_b, m_b_w_o, m_kv_src_norm_g, m_kv_w_a, m_kv_a_norm_g, m_kv_w_b, m_mlp_w1, m_mlp_w2, m_final_norm_g, v_norm_mix_g, v_norm_mlp_g, v_a_w_in, v_a_ln_v_g, v_a_ln_v_b, v_a_w_s, v_a_b_s, v_a_w_out, v_b_w_q_a, v_b_q_norm_g, v_b_w_q_b, v_b_w_o, v_kv_src_norm_g, v_kv_w_a, v_kv_a_norm_g, v_kv_w_b, v_mlp_w1, v_mlp_w2, v_final_norm_g):
    w = dict(norm_mix_g=norm_mix_g, norm_mlp_g=norm_mlp_g, a_w_in=a_w_in, a_ln_v_g=a_ln_v_g, a_ln_v_b=a_ln_v_b,
             a_w_s=a_w_s, a_b_s=a_b_s, a_w_out=a_w_out, b_w_q_a=b_w_q_a, b_q_norm_g=b_q_norm_g, b_w_q_b=b_w_q_b,
             b_w_o=b_w_o, kv_src_norm_g=kv_src_norm_g, kv_w_a=kv_w_a, kv_a_norm_g=kv_a_norm_g, kv_w_b=kv_w_b,
             mlp_w1=mlp_w1, mlp_w2=mlp_w2, final_norm_g=final_norm_g)
    m = dict(norm_mix_g=m_norm_mix_g, norm_mlp_g=m_norm_mlp_g, a_w_in=m_a_w_in, a_ln_v_g=m_a_ln_v_g,
             a_ln_v_b=m_a_ln_v_b, a_w_s=m_a_w_s, a_b_s=m_a_b_s, a_w_out=m_a_w_out, b_w_q_a=m_b_w_q_a,
             b_q_norm_g=m_b_q_norm_g, b_w_q_b=m_b_w_q_b, b_w_o=m_b_w_o, kv_src_norm_g=m_kv_src_norm_g,
             kv_w_a=m_kv_w_a, kv_a_norm_g=m_kv_a_norm_g, kv_w_b=m_kv_w_b, mlp_w1=m_mlp_w1, mlp_w2=m_mlp_w2,
             final_norm_g=m_final_norm_g)
    v = dict(norm_mix_g=v_norm_mix_g, norm_mlp_g=v_norm_mlp_g, a_w_in=v_a_w_in, a_ln_v_g=v_a_ln_v_g,
             a_ln_v_b=v_a_ln_v_b, a_w_s=v_a_w_s, a_b_s=v_a_b_s, a_w_out=v_a_w_out, b_w_q_a=v_b_w_q_a,
             b_q_norm_g=v_b_q_norm_g, b_w_q_b=v_b_w_q_b, b_w_o=v_b_w_o, kv_src_norm_g=v_kv_src_norm_g,
             kv_w_a=v_kv_w_a, kv_a_norm_g=v_kv_a_norm_g, kv_w_b=v_kv_w_b, mlp_w1=v_mlp_w1, mlp_w2=v_mlp_w2,
             final_norm_g=v_final_norm_g)
    t = x.shape[1]

    first = ("a_w_in", "a_w_out", "a_ln_v_g", "a_ln_v_b")
    later = ("mlp_w1_0", "mlp_w2_0", "mlp_w1_1", "mlp_w2_1", "kv_w_a", "kv_w_b", "b_w_q_a", "b_w_q_b", "b_w_o")
    blocks = {k: _three_d(w[k]) for k in BIG if not k.startswith("mlp")}
    for k in ("mlp_w1", "mlp_w2"):
        blocks[k + "_0"], blocks[k + "_1"] = w[k][0:1], w[k][1:2]
    got, casts = _gather_first([blocks[k] if k in blocks else w[k] for k in first], [blocks[k] for k in later])
    wg = dict(zip(first, got))
    wg["a_w_out"] = wg["a_w_out"].reshape(GATE_DIM, D_MODEL)
    wg["a_ln_v_g"] = wg["a_ln_v_g"].reshape(1, GATE_DIM)
    wg["a_ln_v_b"] = wg["a_ln_v_b"].reshape(1, GATE_DIM)
    shards = dict(zip(later, casts))

    sm = {k: _two_d(k, w[k]) for k in SMALL if k not in ("a_ln_v_g", "a_ln_v_b")}
    sm["a_b_st"] = sm["a_b_s"].T
    inv_freq = (ROPE_THETA ** (-jnp.arange(0, QK_ROPE, 2, dtype=F32) / QK_ROPE)).reshape(1, QK_ROPE // 2)

    losses, dx, g, small = _local_step(x[0], positions.reshape(t, 1), loss_target[0], inv_freq, wg, sm, shards)

    names = ("a_w_in", "a_w_out")
    sums = _pair_reduce("pair_reduce_a", [g[k] for k in names])
    g.update(zip(names, _chip_exchange_by_sequencer("exchange_last", sums, collective_id=1)))

    out = {}
    for k in BIG:
        recvs = [[g[k + "_0"]], [g[k + "_1"]]] if k.startswith("mlp") else [[g[k]]]
        res, _ = _adamw_sharded("adamw_" + k, recvs, _three_d(w[k]), _three_d(m[k]), _three_d(v[k]))
        out[k] = [o.reshape(w[k].shape) for o in res]
    own_row = [k in ("a_ln_v_g", "a_ln_v_b") for k in SMALL]
    res = _adamw_small([small[k] for k in SMALL], [_two_d(k, w[k]) for k in SMALL], [_two_d(k, m[k]) for k in SMALL],
                       [_two_d(k, v[k]) for k in SMALL], own_row, losses)
    for i, k in enumerate(SMALL):
        out[k] = [o.reshape(w[k].shape) for o in res[4 * i:4 * i + 4]]

    return (res[-1].reshape(()), dx.reshape(x.shape), *[out[k][0] for k in WEIGHTS], *[out[k][1] for k in WEIGHTS],
            *[out[k][2] for k in WEIGHTS], *[out[k][3] for k in WEIGHTS])
```

```python
import math

import jax
import jax.numpy as jnp
from jax import lax
from jax.experimental import pallas as pl
from jax.experimental.pallas import tpu as pltpu
from jax.experimental.pallas import tpu_sc as plsc

F32, BF16 = jnp.float32, jnp.bfloat16
MESH = pl.DeviceIdType.MESH
ANY = pl.BlockSpec(memory_space=pl.ANY)
VMEM = pl.BlockSpec(memory_space=pltpu.VMEM)

N_DEV = 8
D_MODEL = 1024
CHUNK = 64
GMLP_BLOCK = 128
GATE_DIM = 2048
A_GROUPS = 8
A_GROUP_DIM = GATE_DIM // A_GROUPS
B_HEADS = 8
QK_NOPE, QK_ROPE, V_HEAD = 128, 64, 128
Q_LORA, KV_LORA = 384, 256
ROPE_THETA = 10000.0
D_FF = 4096
FF_SLOT = D_FF // N_DEV
EPS = 1e-6
ATT_SCALE = (QK_NOPE + QK_ROPE) ** -0.5

ADAM_LR, ADAM_B1, ADAM_B2, ADAM_EPS, ADAM_WD, ADAM_STEP = 0.001, 0.9, 0.999, 1e-08, 0.01, 10

TM = 256
TM_GATE = 128
VMEM_LIMIT = 56 * 1024 * 1024
INV_SQRT2 = 1.0 / math.sqrt(2.0)
INV_SQRT_2PI = 1.0 / math.sqrt(2.0 * math.pi)
LOG2_E = 1.0 / math.log(2.0)
HEADS_PER_STEP = 2


def _dot(a, b):
    return jnp.dot(a, b, preferred_element_type=F32)


def _dot_nt(a, b):
    return lax.dot_general(a, b, (((1,), (1,)), ((), ())), preferred_element_type=F32)


def _dot_tn(a, b):
    return lax.dot_general(a, b, (((0,), (0,)), ((), ())), preferred_element_type=F32)


def _rms_fwd(x, g):
    rstd = lax.rsqrt(jnp.mean(x * x, axis=-1, keepdims=True) + EPS)
    xhat = x * rstd
    return xhat * g, xhat, rstd


def _rms_bwd(dy, xhat, rstd, g):
    dxhat = dy * g
    dx = rstd * (dxhat - xhat * jnp.mean(dxhat * xhat, axis=-1, keepdims=True))
    return dx, jnp.sum(dy * xhat, axis=0, keepdims=True)


def _ln_fwd(v, g, b):
    mu = jnp.mean(v, axis=-1, keepdims=True)
    vc = v - mu
    rstd = lax.rsqrt(jnp.mean(vc * vc, axis=-1, keepdims=True) + EPS)
    vhat = vc * rstd
    return vhat * g + b, vhat, rstd


def _gelu(x):
    return 0.5 * x * (1.0 + lax.erf(x * INV_SQRT2))


def _gelu_and_grad(x):
    cdf = 0.5 * (1.0 + lax.erf(x * INV_SQRT2))
    return x * cdf, cdf + x * jnp.exp(-0.5 * x * x) * INV_SQRT_2PI


def _rope(x, cos, sin):
    x1, x2 = x[:, :QK_ROPE // 2], x[:, QK_ROPE // 2:]
    return jnp.concatenate([x1 * cos - x2 * sin, x2 * cos + x1 * sin], axis=-1)


def _gate_mask():
    row = lax.broadcasted_iota(jnp.int32, (GMLP_BLOCK, GMLP_BLOCK), 0)
    col = lax.broadcasted_iota(jnp.int32, (GMLP_BLOCK, GMLP_BLOCK), 1)
    return (col < CHUNK) | (row >= CHUNK)


def _att_mask(q0, tq, t):
    q = q0 + lax.broadcasted_iota(jnp.int32, (tq, t), 0)
    k = lax.broadcasted_iota(jnp.int32, (tq, t), 1)
    return jnp.right_shift(k, 6) <= jnp.right_shift(q, 6)


def _res(shape, imap=None):
    zeros = (0,) * len(shape)
    return pl.BlockSpec(shape, imap or (lambda i: zeros), pipeline_mode=pl.Buffered(1))


def _const(shape):
    zeros = (0,) * len(shape)
    return pl.BlockSpec(shape, lambda i: zeros)


def _row(d, tm=TM):
    return pl.BlockSpec((tm, d), lambda i: (i, 0))


def _heads(d):
    return pl.BlockSpec((B_HEADS, TM, d), lambda i: (0, i, 0))


def _sds(shape, dt):
    return jax.ShapeDtypeStruct(shape, dt)


def _acc(ref, val):
    @pl.when(pl.program_id(0) == 0)
    def _():
        ref[...] = jnp.zeros_like(ref)
    ref[...] += val


def _my_place():
    x, y, c = lax.axis_index("x"), lax.axis_index("y"), lax.axis_index("c")
    return x, y, c, 4 * x + 2 * y + c


def _peer(x, y, c, k):
    px = 1 - x if k & 4 else x
    py = 1 - y if k & 2 else y
    pc = 1 - c if k & 1 else c
    return (px, py, pc), 4 * px + 2 * py + pc


CHIPS = (2, 4, 6)


def _splits(ref):
    return len(ref.shape) >= 3 and ref.shape[1] % 32 == 0


def _piece(ref, block, half=None):
    if half is None or not _splits(ref):
        return ref.at[pl.ds(block, 1)]
    rows = ref.shape[1] // 2
    return ref.at[pl.ds(block, 1), pl.ds(half * rows, rows)]


def _gather_copy(sems, a, k, piece, to, src=None):
    return pltpu.make_async_remote_copy(
        src_ref=piece if src is None else src, dst_ref=piece, send_sem=sems[0].at[a, k], recv_sem=sems[1].at[a, k],
        device_id=to, device_id_type=MESH)


def _gather_start(srcs, outs, sems, only=None):
    x, y, c, me = _my_place()
    for a in range(len(srcs)) if only is None else (only,):
        mine = _piece(outs[a], me)
        pltpu.make_async_copy(srcs[a], mine, sems[2].at[a]).start()
        for k, rel in enumerate((1, 4, 2)):
            _gather_copy(sems, a, k, mine, _peer(x, y, c, rel)[0], src=srcs[a]).start()


def _gather_relay(srcs, outs, sems):
    x, y, c, _ = _my_place()
    sib = _peer(x, y, c, 1)[0]
    (xn, xn_i), (yn, yn_i) = _peer(x, y, c, 4), _peer(x, y, c, 2)
    for a in range(len(srcs)):
        out = outs[a]
        _gather_copy(sems, a, 1, _piece(out, xn_i), xn).wait_recv()
        _gather_copy(sems, a, 3, _piece(out, xn_i, 0), yn).start()
        _gather_copy(sems, a, 5, _piece(out, xn_i), sib).start()
        _gather_copy(sems, a, 2, _piece(out, yn_i), yn).wait_recv()
        if _splits(out):
            _gather_copy(sems, a, 4, _piece(out, yn_i, 1), xn).start()
        _gather_copy(sems, a, 6, _piece(out, yn_i), sib).start()


def _gather_finish(srcs, outs, sems):
    x, y, c, me = _my_place()
    sib = _peer(x, y, c, 1)[0]
    xn, yn, dg_i = _peer(x, y, c, 4)[0], _peer(x, y, c, 2)[0], _peer(x, y, c, 6)[1]
    n = len(srcs)
    for a in range(n):
        out = outs[a]
        _gather_copy(sems, a, 3, _piece(out, dg_i, 0), yn).wait_recv()
        _gather_copy(sems, a, 7, _piece(out, dg_i, 0), sib).start()
        if _splits(out):
            _gather_copy(sems, a, 4, _piece(out, dg_i, 1), xn).wait_recv()
            _gather_copy(sems, a, 8, _piece(out, dg_i, 1), sib).start()
    for a in range(n):
        out = outs[a]
        whole, half = _piece(out, me), _piece(out, me, 0)
        for k in (0, 5, 6):
            _gather_copy(sems, a, k, whole, sib).wait_recv()
        for k in (7, 8) if _splits(out) else (7,):
            _gather_copy(sems, a, k, half, sib).wait_recv()
        for k in (0, 1, 2):
            _gather_copy(sems, a, k, whole, sib, src=srcs[a]).wait_send()
        for k in (5, 6):
            _gather_copy(sems, a, k, whole, sib).wait_send()
        for k in (3, 4, 7, 8) if _splits(out) else (3, 7):
            _gather_copy(sems, a, k, half, sib).wait_send()
        pltpu.make_async_copy(srcs[a], whole, sems[2].at[a]).wait()


def _relay_sems(n):
    return [pltpu.SemaphoreType.DMA((n, 9)), pltpu.SemaphoreType.DMA((n, 9)), pltpu.SemaphoreType.DMA((n,))]


def _gather_sems(n):
    return [pltpu.SemaphoreType.DMA((n, 7)), pltpu.SemaphoreType.DMA((n, 7)), pltpu.SemaphoreType.DMA((n,))]


class _Comm:
    def __init__(self, args, out_shape, scratch, start, finish, relay=None):
        self.args, self.out_shape, self.scratch, self.start, self.finish = args, out_shape, scratch, start, finish
        self.relay = relay


def _gather_comm(shards):
    return _Comm(list(shards), [_sds((N_DEV,) + s.shape[1:], s.dtype) for s in shards], _relay_sems(len(shards)),
                 _gather_start, _gather_finish, relay=_gather_relay)


def _direct_copies(ins, outs, sems, wait, from_block):
    send_sems, recv_sems, local_sems = sems
    x, y, c, me = _my_place()
    for a in range(len(ins)):
        src = ins[a].at[pl.ds(me, 1)] if from_block[a] else ins[a]
        local = pltpu.make_async_copy(src, outs[a].at[pl.ds(me, 1)], local_sems.at[a])
        local.wait() if wait else local.start()
        for k in range(1, N_DEV):
            to, to_i = _peer(x, y, c, k)
            cp = pltpu.make_async_remote_copy(
                src_ref=ins[a].at[pl.ds(to_i, 1)] if from_block[a] else ins[a], dst_ref=outs[a].at[pl.ds(me, 1)],
                send_sem=send_sems.at[a, k - 1], recv_sem=recv_sems.at[a, k - 1], device_id=to, device_id_type=MESH)
            cp.wait() if wait else cp.start()


def _exchange_comm(grads=(), parts=()):
    ins = list(grads) + list(parts)
    from_block = [True] * len(grads) + [False] * len(parts)
    out_shape = [_sds(g.shape, g.dtype) for g in grads] + [_sds((N_DEV,) + p.shape[1:], p.dtype) for p in parts]

    def start(ins_, outs_, sems_):
        _direct_copies(ins_, outs_, sems_, False, from_block)

    def finish(ins_, outs_, sems_):
        _direct_copies(ins_, outs_, sems_, True, from_block)

    return _Comm(ins, out_shape, _gather_sems(len(ins)), start, finish)


def _chip_copies(ins, outs, sems, wait, rels, own):
    send_sems, recv_sems, local_sems = sems
    x, y, c, _ = _my_place()
    for a in range(len(ins)):
        if own:
            local = pltpu.make_async_copy(ins[a].at[pl.ds(2 * x + y, 1)], outs[a].at[pl.ds(len(rels), 1)],
                                          local_sems.at[a])
            local.wait() if wait else local.start()
        for i, j in enumerate(rels):
            to = _peer(x, y, c, CHIPS[j])[0]
            cp = pltpu.make_async_remote_copy(
                src_ref=ins[a].at[pl.ds(2 * to[0] + to[1], 1)], dst_ref=outs[a].at[pl.ds(i, 1)],
                send_sem=send_sems.at[a, i], recv_sem=recv_sems.at[a, i], device_id=to, device_id_type=MESH)
            cp.wait() if wait else cp.start()


def _chip_exchange_comm(sums, rels=(0, 1, 2), own=True):
    def start(ins_, outs_, sems_):
        _chip_copies(ins_, outs_, sems_, False, rels, own)

    def finish(ins_, outs_, sems_):
        _chip_copies(ins_, outs_, sems_, True, rels, own)

    n = len(sums)
    sems = [pltpu.SemaphoreType.DMA((n, len(rels))), pltpu.SemaphoreType.DMA((n, len(rels))),
            pltpu.SemaphoreType.DMA((n,))]
    return _Comm(list(sums), [_sds((len(rels) + own,) + s.shape[1:], s.dtype) for s in sums], sems, start, finish)


def _pair_reduce(name, grads):
    n = len(grads)
    n_chips = N_DEV // 2

    def body(*refs):
        g_refs, gh_refs, p_refs, land = refs[:n], refs[n:2 * n], refs[2 * n:3 * n], refs[3 * n:4 * n]
        send_sems, recv_sems = refs[4 * n:]
        x, y, c, _ = _my_place()
        sib = _peer(x, y, c, 1)[0]
        q = pl.program_id(0)

        def to_sibling(a, j):
            return pltpu.make_async_remote_copy(
                src_ref=gh_refs[a].at[j, pl.ds(1 - c, 1)], dst_ref=land[a].at[pl.ds(j, 1)],
                send_sem=send_sems.at[a, j], recv_sem=recv_sems.at[a, j], device_id=sib, device_id_type=MESH)

        @pl.when(q == 0)
        def _():
            for j in range(n_chips):
                for a in range(n):
                    to_sibling(a, j).start()

        for a in range(n):
            to_sibling(a, q).wait_recv()
            p_refs[a][...] = (g_refs[a][0, pl.ds(c, 1)].astype(F32) + land[a][pl.ds(q, 1)].astype(F32)).astype(BF16)

        @pl.when(q == n_chips - 1)
        def _():
            for a in range(n):
                for j in range(n_chips):
                    to_sibling(a, j).wait_send()

    views = [g.reshape((n_chips, 2) + g.shape[1:]) for g in grads]
    res = pl.pallas_call(
        body, name=name, grid=(n_chips,),
        in_specs=[pl.BlockSpec((1, 2) + g.shape[1:], lambda q: (q, 0, 0, 0)) for g in grads] + [ANY] * n,
        out_specs=[pl.BlockSpec((1,) + g.shape[1:], lambda q: (q, 0, 0)) for g in grads],
        out_shape=[_sds((n_chips,) + g.shape[1:], BF16) for g in grads],
        scratch_shapes=[pltpu.VMEM((n_chips,) + g.shape[1:], BF16) for g in grads]
        + [pltpu.SemaphoreType.DMA((n, n_chips)), pltpu.SemaphoreType.DMA((n, n_chips))],
        compiler_params=pltpu.CompilerParams(dimension_semantics=("arbitrary",), vmem_limit_bytes=VMEM_LIMIT),
    )(*views, *views)
    return list(res)


def _call(name, body, grid, in_specs, out_specs, out_shape, args, scratch=(), comm=None):
    params = pltpu.CompilerParams(dimension_semantics=("arbitrary",) * len(grid), vmem_limit_bytes=VMEM_LIMIT)
    if comm is None:
        outs = pl.pallas_call(body, name=name, grid=grid, in_specs=list(in_specs), out_specs=list(out_specs),
                              out_shape=list(out_shape), scratch_shapes=list(scratch), compiler_params=params)(*args)
        return list(outs), []
    ni, nci, no, nco, ns = len(in_specs), len(comm.args), len(out_specs), len(comm.out_shape), len(scratch)

    def carrying(*refs):
        ins, refs = refs[:ni], refs[ni:]
        cin, refs = refs[:nci], refs[nci:]
        outs, refs = refs[:no], refs[no:]
        cout, refs = refs[:nco], refs[nco:]
        scr, csems = refs[:ns], refs[ns:]
        step = pl.program_id(0)
        for ax in range(1, len(grid)):
            step = step * grid[ax] + pl.program_id(ax)
        steps = math.prod(grid)

        @pl.when(step == 0)
        def _():
            comm.start(cin, cout, csems)

        if comm.relay is not None:
            @pl.when(step == (2 * steps) // 3)
            def _():
                comm.relay(cin, cout, csems)

        body(*ins, *outs, *scr)

        @pl.when(step == steps - 1)
        def _():
            comm.finish(cin, cout, csems)

    outs = pl.pallas_call(
        carrying, name=name, grid=grid, in_specs=list(in_specs) + [ANY] * nci, out_specs=list(out_specs) + [ANY] * nco,
        out_shape=list(out_shape) + list(comm.out_shape), scratch_shapes=list(scratch) + list(comm.scratch),
        compiler_params=params)(*args, *comm.args)
    return list(outs[:no]), list(outs[no:])


def _comm_only(name, comm):
    def body(*refs):
        nci, nco = len(comm.args), len(comm.out_shape)
        cin, cout, csems = refs[:nci], refs[nci:nci + nco], refs[nci + nco:]
        comm.start(cin, cout, csems)
        if comm.relay is not None:
            comm.relay(cin, cout, csems)
        comm.finish(cin, cout, csems)

    return pl.pallas_call(body, name=name, in_specs=[ANY] * len(comm.args), out_specs=[ANY] * len(comm.out_shape),
                          out_shape=list(comm.out_shape), scratch_shapes=list(comm.scratch))(*comm.args)


SIBLING_AND_NEIGHBOURS, OTHER_CHIPS, EVERYONE = (1, 4, 2), CHIPS, tuple(range(1, N_DEV))


def _by_sequencer(name, comm, peers, collective_id):
    src = [jax.new_ref(a, memory_space=pltpu.MemorySpace.HBM) for a in comm.args]
    dst = [jax.empty_ref(s, memory_space=pltpu.MemorySpace.HBM) for s in comm.out_shape]

    @pl.kernel(mesh=plsc.ScalarSubcoreMesh(axis_name="sequencer", num_cores=1), name=name,
               scratch_types=tuple(comm.scratch), compiler_params=pltpu.CompilerParams(collective_id=collective_id))
    def launch(*sems):
        x, y, c, _ = _my_place()
        barrier = pltpu.get_barrier_semaphore()
        for k in peers:
            pl.semaphore_signal(barrier, inc=1, device_id=_peer(x, y, c, k)[0], device_id_type=MESH)
        pl.semaphore_wait(barrier, len(peers))
        comm.start(src, dst, sems)
        if comm.relay is not None:
            comm.relay(src, dst, sems)
        comm.finish(src, dst, sems)

    launch()
    return [d[...] for d in dst]


def _gather_first(first, later):
    nf, nl = len(first), len(later)
    dts = [BF16] * (nf - 2) + [F32, F32]

    def body(*refs):
        ins, refs = refs[:nf + nl], refs[nf + nl:]
        outs, refs = refs[:nf], refs[nf:]
        casts, refs = refs[:nl], refs[nl:]
        stage, sems = refs[:nf], refs[nf:]
        for a in range(nf):
            stage[a][...] = ins[a][...].astype(dts[a])
            _gather_start(stage, outs, sems, only=a)
        for a in range(nl):
            casts[a][...] = ins[nf + a][...].astype(BF16)
        _gather_relay(stage, outs, sems)
        _gather_finish(stage, outs, sems)

    res = pl.pallas_call(
        body, name="gather_first",
        in_specs=[VMEM] * (nf + nl), out_specs=[ANY] * nf + [VMEM] * nl,
        out_shape=[_sds((N_DEV,) + s.shape[1:], dt) for s, dt in zip(first, dts)]
        + [_sds(s.shape, BF16) for s in later],
        scratch_shapes=[pltpu.VMEM(s.shape, dt) for s, dt in zip(first, dts)] + _relay_sems(nf),
        compiler_params=pltpu.CompilerParams(vmem_limit_bytes=VMEM_LIMIT),
    )(*first, *later)
    return list(res[:nf]), list(res[nf:])


def _a_mix_fwd(x, g, w_in, ln_g, ln_b, w_s, b_st, w_out, comm=None):
    t = x.shape[0]
    nblk = TM // GMLP_BLOCK

    def body(x_ref, g_ref, win_ref, lng_ref, lnb_ref, ws_ref, bst_ref, wout_ref, h_ref, z_ref, gated_scr):
        xv = x_ref[...]
        hb = _rms_fwd(xv, g_ref[...])[0].astype(BF16)
        for d in range(N_DEV):
            z_ref[:, d * FF_SLOT:(d + 1) * FF_SLOT] = _dot(hb, win_ref[d])
        u = _gelu(z_ref[:, :GATE_DIM])
        vb = _ln_fwd(_gelu(z_ref[:, GATE_DIM:]), lng_ref[...], lnb_ref[...])[0].astype(BF16)
        mask = _gate_mask()
        for gi in range(A_GROUPS):
            wm = jnp.where(mask, ws_ref[gi], 0.0).astype(BF16)
            bias = bst_ref[:, gi:gi + 1]
            cs = slice(gi * A_GROUP_DIM, (gi + 1) * A_GROUP_DIM)
            for n in range(nblk):
                rs = slice(n * GMLP_BLOCK, (n + 1) * GMLP_BLOCK)
                sv = _dot(wm, vb[rs, cs]) + bias
                gated_scr[rs, cs] = (u[rs, cs] * sv).astype(BF16)
        h_ref[...] = xv + _dot(gated_scr[...], wout_ref[...])

    return _call(
        "a_mix_fwd", body, (t // TM,),
        [_row(D_MODEL), _res((1, D_MODEL)), _res((N_DEV, D_MODEL, FF_SLOT)), _res((1, GATE_DIM)),
         _res((1, GATE_DIM)), _res((A_GROUPS, GMLP_BLOCK, GMLP_BLOCK)), _res((GMLP_BLOCK, A_GROUPS)),
         _res((GATE_DIM, D_MODEL))],
        [_row(D_MODEL), _row(2 * GATE_DIM), _row(GATE_DIM)],
        [_sds((t, D_MODEL), F32), _sds((t, 2 * GATE_DIM), F32), _sds((t, GATE_DIM), BF16)],
        (x, g, w_in, ln_g, ln_b, w_s, b_st, w_out), comm=comm)


MLP_W_SPECS = (_res((N_DEV, D_MODEL, FF_SLOT)), _res((N_DEV, FF_SLOT, D_MODEL)))


def _mlp_fwd(h, g, w1, w2, layer, comm=None):
    t = h.shape[0]

    def body(h_ref, g_ref, w1_ref, w2_ref, o_ref, a_ref):
        hv = h_ref[...]
        hb = _rms_fwd(hv, g_ref[...])[0].astype(BF16)
        o_ref[...] = hv
        for d in range(N_DEV):
            a = _dot(hb, w1_ref[d])
            a_ref[:, d * FF_SLOT:(d + 1) * FF_SLOT] = a
            r = jnp.maximum(a, 0.0)
            o_ref[...] += _dot((r * r).astype(BF16), w2_ref[d])

    return _call(
        f"mlp_fwd_{layer}", body, (t // TM,), [_row(D_MODEL), _res((1, D_MODEL)), *MLP_W_SPECS],
        [_row(D_MODEL), _row(D_FF)], [_sds((t, D_MODEL), F32), _sds((t, D_FF), F32)], (h, g, w1, w2), comm=comm)


def _mlp_fwd_loss(h, g, w1, w2, final_g, target):
    t = h.shape[0]

    def body(h_ref, g_ref, w1_ref, w2_ref, fg_ref, t_ref, a_ref, loss_ref, dh_ref, dg_ref):
        hv = h_ref[...]
        hb = _rms_fwd(hv, g_ref[...])[0].astype(BF16)
        out = hv
        for d in range(N_DEV):
            a = _dot(hb, w1_ref[d])
            a_ref[:, d * FF_SLOT:(d + 1) * FF_SLOT] = a
            r = jnp.maximum(a, 0.0)
            out = out + _dot((r * r).astype(BF16), w2_ref[d])
        y, xhat, rstd = _rms_fwd(out, fg_ref[...])
        err = y - t_ref[...]
        part = 0.5 * jnp.sum(jnp.mean(err * err, axis=-1, keepdims=True), axis=0, keepdims=True)
        dx, dg = _rms_bwd(err * (1.0 / D_MODEL), xhat, rstd, fg_ref[...])
        dh_ref[...] = dx
        _acc(dg_ref, dg)
        _acc(loss_ref, part)

    return _call(
        "mlp_fwd_loss", body, (t // TM,),
        [_row(D_MODEL), _res((1, D_MODEL)), *MLP_W_SPECS, _res((1, D_MODEL)), _row(D_MODEL)],
        [_row(D_FF), _const((1, 1)), _row(D_MODEL), _const((1, D_MODEL))],
        [_sds((t, D_FF), F32), _sds((1, 1), F32), _sds((t, D_MODEL), F32), _sds((1, D_MODEL), F32)],
        (h, g, w1, w2, final_g, target))[0]


KVQ_W_SPECS = (_res((1, D_MODEL)), _res((D_MODEL, KV_LORA + QK_ROPE)), _res((1, KV_LORA)),
               _res((B_HEADS, KV_LORA, QK_NOPE + V_HEAD)), _res((1, D_MODEL)), _res((D_MODEL, Q_LORA)),
               _res((1, Q_LORA)), _res((B_HEADS, Q_LORA, QK_NOPE + QK_ROPE)))


def _kvq_fwd(h, pos, inv_freq, kvq_w):
    t = h.shape[0]
    half = QK_ROPE // 2

    def body(h_ref, pos_ref, invf_ref, srcg_ref, wkva_ref, kvag_ref, wkvb_ref, mixg_ref, wqa_ref, qg_ref, wqb_ref,
             ckv_ref, k_ref, v_ref, cqpre_ref, q_ref, cos_ref, sin_ref):
        hv = h_ref[...]
        xhat = hv * lax.rsqrt(jnp.mean(hv * hv, axis=-1, keepdims=True) + EPS)
        ang = pos_ref[...].astype(F32) * invf_ref[...]
        cos, sin = jnp.cos(ang), jnp.sin(ang)
        cos_ref[...] = cos
        sin_ref[...] = sin
        ckv = _dot((xhat * srcg_ref[...]).astype(BF16), wkva_ref[...])
        ckv_ref[...] = ckv
        cb = _rms_fwd(ckv[:, :KV_LORA], kvag_ref[...])[0].astype(BF16)
        kpe = _rope(ckv[:, KV_LORA:], cos, sin).astype(BF16)
        for hd in range(B_HEADS):
            kv = _dot(cb, wkvb_ref[hd])
            k_ref[hd, :, 0:QK_NOPE] = kv[:, :QK_NOPE].astype(BF16)
            k_ref[hd, :, QK_NOPE:] = kpe
            v_ref[hd] = kv[:, QK_NOPE:].astype(BF16)
        cqpre = _dot((xhat * mixg_ref[...]).astype(BF16), wqa_ref[...])
        cqpre_ref[...] = cqpre
        cqb = _rms_fwd(cqpre, qg_ref[...])[0].astype(BF16)
        for hd in range(B_HEADS):
            q = _dot(cqb, wqb_ref[hd])
            q_ref[hd, :, 0:QK_NOPE] = q[:, :QK_NOPE].astype(BF16)
            q_ref[hd, :, QK_NOPE:] = _rope(q[:, QK_NOPE:], cos, sin).astype(BF16)

    return _call(
        "kvq_fwd", body, (t // TM,), [_row(D_MODEL), _row(1), _res((1, half)), *KVQ_W_SPECS],
        [_row(KV_LORA + QK_ROPE), _heads(QK_NOPE + QK_ROPE), _heads(V_HEAD), _row(Q_LORA),
         _heads(QK_NOPE + QK_ROPE), _row(half), _row(half)],
        [_sds((t, KV_LORA + QK_ROPE), F32), _sds((B_HEADS, t, QK_NOPE + QK_ROPE), BF16),
         _sds((B_HEADS, t, V_HEAD), BF16), _sds((t, Q_LORA), F32), _sds((B_HEADS, t, QK_NOPE + QK_ROPE), BF16),
         _sds((t, half), F32), _sds((t, half), F32)],
        (h, pos, inv_freq, *kvq_w))[0]


def _softmax_rows(q, k_ref, k):
    past, upto = k * TM, (k + 1) * TM
    s = _dot_nt(q, k_ref[0:upto, :])
    own = jnp.where(_att_mask(0, TM, TM), s[:, past:], jnp.finfo(F32).min)
    s = own if k == 0 else jnp.concatenate([s[:, :past], own], axis=1)
    e = jnp.exp2((s - jnp.max(s, axis=-1, keepdims=True)) * (ATT_SCALE * LOG2_E))
    return e * (1.0 / jnp.sum(e, axis=-1, keepdims=True))


def _for_my_tile(i, nq, fn):
    for k in range(nq):
        @pl.when(i == k)
        def _(k=k):
            fn(k)


def _attn_fwd(h, q, k, v, w_o, comm=None):
    t = h.shape[0]
    nq, hps = t // TM, HEADS_PER_STEP

    def body(h_ref, q_ref, k_ref, v_ref, wo_ref, o_ref, att_ref):
        i, pair = pl.program_id(0), pl.program_id(1)

        @pl.when(pair == 0)
        def _():
            o_ref[...] = h_ref[...]

        def tile(kt):
            proj = None
            for j in range(hps):
                hd = pair * hps + j
                p = _softmax_rows(q_ref[j], k_ref.at[hd], kt)
                ob = _dot(p.astype(BF16), v_ref[hd, 0:(kt + 1) * TM, :]).astype(BF16)
                att_ref[j] = ob
                proj = _dot(ob, wo_ref[hd]) if proj is None else proj + _dot(ob, wo_ref[hd])
            o_ref[...] += proj

        _for_my_tile(i, nq, tile)

    def per_head(d):
        return pl.BlockSpec((hps, TM, d), lambda i, pair: (pair, i, 0))

    def resident(shape):
        zeros = (0,) * len(shape)
        return pl.BlockSpec(shape, lambda i, pair: zeros, pipeline_mode=pl.Buffered(1))

    tile_spec = pl.BlockSpec((TM, D_MODEL), lambda i, pair: (i, 0))
    return _call(
        "attn_fwd", body, (nq, B_HEADS // hps),
        [tile_spec, per_head(QK_NOPE + QK_ROPE), resident((B_HEADS, t, QK_NOPE + QK_ROPE)),
         resident((B_HEADS, t, V_HEAD)), resident((B_HEADS, V_HEAD, D_MODEL))],
        [tile_spec, per_head(V_HEAD)], [_sds((t, D_MODEL), F32), _sds((B_HEADS, t, V_HEAD), BF16)],
        (h, q, k, v, w_o), comm=comm)


def _mlp_bwd(h, a, dho, g, w1, w2, layer, comm=None):
    t = h.shape[0]

    def body(h_ref, a_ref, dho_ref, g_ref, w1_ref, w2_ref, dhi_ref, dg_ref, hn_ref, f_ref, da_ref, dhib_ref):
        gv = g_ref[...]
        y, xhat, rstd = _rms_fwd(h_ref[...], gv)
        hn_ref[...] = y.astype(BF16)
        dho_v = dho_ref[...]
        dhob = dho_v.astype(BF16)
        dhn = jnp.zeros((TM, D_MODEL), F32)
        for d in range(N_DEV):
            cs = slice(d * FF_SLOT, (d + 1) * FF_SLOT)
            r = jnp.maximum(a_ref[:, cs], 0.0)
            f_ref[:, cs] = (r * r).astype(BF16)
            da = (_dot_nt(dhob, w2_ref[d]) * (2.0 * r)).astype(BF16)
            da_ref[:, cs] = da
            dhn = dhn + _dot_nt(da, w1_ref[d])
        dx, dg = _rms_bwd(dhn, xhat, rstd, gv)
        dhi = dho_v + dx
        dhi_ref[...] = dhi
        dhib_ref[...] = dhi.astype(BF16)
        _acc(dg_ref, dg)

    return _call(
        f"mlp_bwd_{layer}", body, (t // TM,),
        [_row(D_MODEL), _row(D_FF), _row(D_MODEL), _res((1, D_MODEL)), *MLP_W_SPECS],
        [_row(D_MODEL), _const((1, D_MODEL)), _row(D_MODEL), _row(D_FF), _row(D_FF), _row(D_MODEL)],
        [_sds((t, D_MODEL), F32), _sds((1, D_MODEL), F32), _sds((t, D_MODEL), BF16), _sds((t, D_FF), BF16),
         _sds((t, D_FF), BF16), _sds((t, D_MODEL), BF16)],
        (h, a, dho, g, w1, w2), comm=comm)


def _attn_bwd(dh, q, k, v, w_o, cos, sin, comm=None):
    t = dh.shape[0]
    half, hps = QK_ROPE // 2, HEADS_PER_STEP

    def body(dh_ref, q_ref, k_ref, v_ref, wo_ref, cos_ref, sin_ref, dq_ref, dk_ref, dv_ref):
        i = pl.program_id(1)

        @pl.when(i == 0)
        def _():
            dk_ref[...] = jnp.zeros_like(dk_ref)
            dv_ref[...] = jnp.zeros_like(dv_ref)

        def tile(kt):
            keys = slice(0, (kt + 1) * TM)
            for j in range(hps):
                qj = q_ref[j]
                do = _dot_nt(dh_ref[kt * TM:(kt + 1) * TM, :], wo_ref[j]).astype(BF16)
                p = _softmax_rows(qj, k_ref.at[j], kt)
                dp = _dot_nt(do, v_ref[j, keys, :])
                ds = (p * (dp - jnp.sum(p * dp, axis=-1, keepdims=True)) * ATT_SCALE).astype(BF16)
                dq = _dot(ds, k_ref[j, keys, :])
                dq_ref[j, :, 0:QK_NOPE] = dq[:, :QK_NOPE].astype(BF16)
                dq_ref[j, :, QK_NOPE:] = _rope(dq[:, QK_NOPE:], cos_ref[...], -sin_ref[...]).astype(BF16)
                dk_ref[j, keys, :] += _dot_tn(ds, qj)
                dv_ref[j, keys, :] += _dot_tn(p.astype(BF16), do)

        _for_my_tile(i, t // TM, tile)

    def per_pair(rows, d, tiled):
        return pl.BlockSpec((hps, rows, d), (lambda pair, i: (pair, i, 0)) if tiled else (lambda pair, i: (pair, 0, 0)))

    def tile(d):
        return pl.BlockSpec((TM, d), lambda pair, i: (i, 0))

    return _call(
        "attn_bwd", body, (B_HEADS // hps, t // TM),
        [pl.BlockSpec((t, D_MODEL), lambda pair, i: (0, 0), pipeline_mode=pl.Buffered(1)),
         per_pair(TM, QK_NOPE + QK_ROPE, True), per_pair(t, QK_NOPE + QK_ROPE, False), per_pair(t, V_HEAD, False),
         per_pair(V_HEAD, D_MODEL, False), tile(half), tile(half)],
        [per_pair(TM, QK_NOPE + QK_ROPE, True), per_pair(t, QK_NOPE + QK_ROPE, False), per_pair(t, V_HEAD, False)],
        [_sds((B_HEADS, t, QK_NOPE + QK_ROPE), BF16), _sds((B_HEADS, t, QK_NOPE + QK_ROPE), F32),
         _sds((B_HEADS, t, V_HEAD), F32)],
        (dh, q, k, v, w_o, cos, sin), comm=comm)


def _kvq_bwd(h, dh, ckv, cqpre, dq, dk, dv, cos, sin, kvq_w):
    t = h.shape[0]
    half = QK_ROPE // 2

    def body(h_ref, dh_ref, ckv_ref, cqpre_ref, dq_ref, dk_ref, dv_ref, cos_ref, sin_ref,
             srcg_ref, wkva_ref, kvag_ref, wkvb_ref, mixg_ref, wqa_ref, qg_ref, wqb_ref,
             dhi_ref, hq_ref, hk_ref, cq_ref, dcqpre_ref, c_ref, dkv_ref, dckv_ref,
             dmixg_ref, dsrcg_ref, dqg_ref, dkvag_ref):
        hv = h_ref[...]
        rstd = lax.rsqrt(jnp.mean(hv * hv, axis=-1, keepdims=True) + EPS)
        xhat = hv * rstd
        mixg, srcg, qg, kvag = mixg_ref[...], srcg_ref[...], qg_ref[...], kvag_ref[...]
        hq_ref[...] = (xhat * mixg).astype(BF16)
        hk_ref[...] = (xhat * srcg).astype(BF16)
        cq, cqhat, crstd = _rms_fwd(cqpre_ref[...], qg)
        cq_ref[...] = cq.astype(BF16)
        dcq = jnp.zeros((TM, Q_LORA), F32)
        for hd in range(B_HEADS):
            dcq = dcq + _dot_nt(dq_ref[hd], wqb_ref[hd])
        dcqpre, dqg = _rms_bwd(dcq, cqhat, crstd, qg)
        dcqpre_b = dcqpre.astype(BF16)
        dcqpre_ref[...] = dcqpre_b
        dxq, dmixg = _rms_bwd(_dot_nt(dcqpre_b, wqa_ref[...]), xhat, rstd, mixg)
        ckv = ckv_ref[...]
        c, chat, krstd = _rms_fwd(ckv[:, :KV_LORA], kvag)
        c_ref[...] = c.astype(BF16)
        dc = jnp.zeros((TM, KV_LORA), F32)
        dkpe = jnp.zeros((TM, QK_ROPE), F32)
        for hd in range(B_HEADS):
            dkv = jnp.concatenate([dk_ref[hd, :, 0:QK_NOPE], dv_ref[hd]], axis=-1).astype(BF16)
            dkv_ref[hd] = dkv
            dc = dc + _dot_nt(dkv, wkvb_ref[hd])
            dkpe = dkpe + dk_ref[hd, :, QK_NOPE:]
        dlat, dkvag = _rms_bwd(dc, chat, krstd, kvag)
        dpe = _rope(dkpe, cos_ref[...], -sin_ref[...])
        dckv_b = jnp.concatenate([dlat, dpe], axis=-1).astype(BF16)
        dckv_ref[...] = dckv_b
        dxk, dsrcg = _rms_bwd(_dot_nt(dckv_b, wkva_ref[...]), xhat, rstd, srcg)
        dhi_ref[...] = dh_ref[...] + dxq + dxk
        _acc(dmixg_ref, dmixg)
        _acc(dsrcg_ref, dsrcg)
        _acc(dqg_ref, dqg)
        _acc(dkvag_ref, dkvag)

    return _call(
        "kvq_bwd", body, (t // TM,),
        [_row(D_MODEL), _row(D_MODEL), _row(KV_LORA + QK_ROPE), _row(Q_LORA), _heads(QK_NOPE + QK_ROPE),
         _heads(QK_NOPE + QK_ROPE), _heads(V_HEAD), _row(half), _row(half), *KVQ_W_SPECS],
        [_row(D_MODEL), _row(D_MODEL), _row(D_MODEL), _row(Q_LORA), _row(Q_LORA), _row(KV_LORA),
         _heads(QK_NOPE + V_HEAD), _row(KV_LORA + QK_ROPE),
         _const((1, D_MODEL)), _const((1, D_MODEL)), _const((1, Q_LORA)), _const((1, KV_LORA))],
        [_sds((t, D_MODEL), F32), _sds((t, D_MODEL), BF16), _sds((t, D_MODEL), BF16), _sds((t, Q_LORA), BF16),
         _sds((t, Q_LORA), BF16), _sds((t, KV_LORA), BF16), _sds((B_HEADS, t, QK_NOPE + V_HEAD), BF16),
         _sds((t, KV_LORA + QK_ROPE), BF16),
         _sds((1, D_MODEL), F32), _sds((1, D_MODEL), F32), _sds((1, Q_LORA), F32), _sds((1, KV_LORA), F32)],
        (h, dh, ckv, cqpre, dq, dk, dv, cos, sin, *kvq_w))[0]


def _a_mix_bwd(x, z, dh, g, w_in, ln_g, ln_b, w_s, b_st, w_out, comm=None):
    t = x.shape[0]
    tm = TM_GATE
    nblk = tm // GMLP_BLOCK

    def body(x_ref, z_ref, dh_ref, g_ref, win_ref, lng_ref, lnb_ref, ws_ref, bst_ref, wout_ref,
             dx_ref, hn_ref, dz_ref, dg_ref, dlng_ref, dlnb_ref, dws_ref, dbs_ref, dvn_scr, gelu_grad_v):
        @pl.when(pl.program_id(0) == 0)
        def _():
            dws_ref[...] = jnp.zeros_like(dws_ref)
            dbs_ref[...] = jnp.zeros_like(dbs_ref)

        gv, lng = g_ref[...], lng_ref[...]
        y, xhat, rstd = _rms_fwd(x_ref[...], gv)
        hn_ref[...] = y.astype(BF16)
        dhv = dh_ref[...]
        dgated = _dot_nt(dhv.astype(BF16), wout_ref[...])
        u, gelu_grad_u = _gelu_and_grad(z_ref[:, :GATE_DIM])
        v, gelu_grad_v[...] = _gelu_and_grad(z_ref[:, GATE_DIM:])
        vn, vhat, lrstd = _ln_fwd(v, lng, lnb_ref[...])
        vb = vn.astype(BF16)
        mask = _gate_mask()
        for gi in range(A_GROUPS):
            wm = jnp.where(mask, ws_ref[gi], 0.0).astype(BF16)
            bias = bst_ref[:, gi:gi + 1]
            cs = slice(gi * A_GROUP_DIM, (gi + 1) * A_GROUP_DIM)
            dws = jnp.zeros((GMLP_BLOCK, GMLP_BLOCK), F32)
            dbs = jnp.zeros((GMLP_BLOCK, 1), F32)
            for n in range(nblk):
                rs = slice(n * GMLP_BLOCK, (n + 1) * GMLP_BLOCK)
                sv = _dot(wm, vb[rs, cs]) + bias
                dz_ref[rs, cs] = (dgated[rs, cs] * sv * gelu_grad_u[rs, cs]).astype(BF16)
                dsv = dgated[rs, cs] * u[rs, cs]
                dsvb = dsv.astype(BF16)
                dws = dws + _dot_nt(dsvb, vb[rs, cs])
                dbs = dbs + jnp.sum(dsv, axis=-1, keepdims=True)
                dvn_scr[rs, cs] = _dot_tn(wm, dsvb)
            dws_ref[gi] += jnp.where(mask, dws, 0.0)
            dbs_ref[gi] += dbs
        dvn = dvn_scr[...]
        dvhat = dvn * lng
        dv = lrstd * (dvhat - jnp.mean(dvhat, axis=-1, keepdims=True)
                      - vhat * jnp.mean(dvhat * vhat, axis=-1, keepdims=True))
        dz_ref[:, GATE_DIM:] = (dv * gelu_grad_v[...]).astype(BF16)
        dhn = jnp.zeros((tm, D_MODEL), F32)
        for d in range(N_DEV):
            dhn = dhn + _dot_nt(dz_ref[:, d * FF_SLOT:(d + 1) * FF_SLOT], win_ref[d])
        dx, dg = _rms_bwd(dhn, xhat, rstd, gv)
        dx_ref[...] = dhv + dx
        _acc(dg_ref, dg)
        _acc(dlng_ref, jnp.sum(dvn * vhat, axis=0, keepdims=True))
        _acc(dlnb_ref, jnp.sum(dvn, axis=0, keepdims=True))

    return _call(
        "a_mix_bwd", body, (t // tm,),
        [_row(D_MODEL, tm), _row(2 * GATE_DIM, tm), _row(D_MODEL, tm), _res((1, D_MODEL)),
         _res((N_DEV, D_MODEL, FF_SLOT)), _res((1, GATE_DIM)), _res((1, GATE_DIM)),
         _res((A_GROUPS, GMLP_BLOCK, GMLP_BLOCK)), _res((GMLP_BLOCK, A_GROUPS)), _res((GATE_DIM, D_MODEL))],
        [_row(D_MODEL, tm), _row(D_MODEL, tm), _row(2 * GATE_DIM, tm),
         _const((1, D_MODEL)), _const((1, GATE_DIM)), _const((1, GATE_DIM)),
         _const((A_GROUPS, GMLP_BLOCK, GMLP_BLOCK)), _const((A_GROUPS, GMLP_BLOCK, 1))],
        [_sds((t, D_MODEL), F32), _sds((t, D_MODEL), BF16),
         _sds((t, 2 * GATE_DIM), BF16), _sds((1, D_MODEL), F32), _sds((1, GATE_DIM), F32),
         _sds((1, GATE_DIM), F32), _sds((A_GROUPS, GMLP_BLOCK, GMLP_BLOCK), F32),
         _sds((A_GROUPS, GMLP_BLOCK, 1), F32)],
        (x, z, dh, g, w_in, ln_g, ln_b, w_s, b_st, w_out),
        scratch=[pltpu.VMEM((tm, GATE_DIM), F32), pltpu.VMEM((tm, GATE_DIM), F32)], comm=comm)


def _wgrad(name, a, b, a_spec, b_spec, m, n, comm=None):
    def body(a_ref, b_ref, o_ref):
        o_ref[0] = _dot_tn(a_ref[...].astype(BF16), b_ref[...].astype(BF16)).astype(BF16)

    outs, got = _call(name, body, (N_DEV,), [a_spec, b_spec], [pl.BlockSpec((1, m, n), lambda d: (d, 0, 0))],
                      [_sds((N_DEV, m, n), BF16)], (a, b), comm=comm)
    return outs[0] if comm is None else (outs[0], got)


def _full(t, d):
    return pl.BlockSpec((t, d), lambda i: (0, 0), pipeline_mode=pl.Buffered(1))


def _cols(t, d):
    return pl.BlockSpec((t, d), lambda i: (0, i))


def _head(t, d):
    return pl.BlockSpec((None, t, d), lambda i: (i, 0, 0))


def _local_step(x, pos, target, inv_freq, wg, sm, shards=None):
    t = x.shape[0]
    wg = dict(wg)
    dist = shards is not None
    mix_g = [sm["norm_mix_g"][l:l + 1] for l in range(2)]
    mlp_g = [sm["norm_mlp_g"][l:l + 1] for l in range(2)]

    ids = iter(range(2, 2 + 7))

    def gather(names):
        if dist:
            got = _by_sequencer("gather_" + names[0], _gather_comm([shards[k] for k in names]),
                                SIBLING_AND_NEIGHBOURS, next(ids))
            wg.update(zip(names, got))

    def send(name, names, sums_first):
        if dist:
            grads = [g[k] for k in names]
            comm = _chip_exchange_comm(_pair_reduce("pair_reduce_" + name, grads)) if sums_first \
                else _exchange_comm(grads=grads)
            g.update(zip(names, _by_sequencer("exchange_" + name, comm, OTHER_CHIPS if sums_first else EVERYONE,
                                              next(ids))))

    def a_args():
        return (wg["a_w_in"], wg["a_ln_v_g"], wg["a_ln_v_b"], sm["a_w_s"], sm["a_b_st"], wg["a_w_out"])

    def kvq_w():
        return (sm["kv_src_norm_g"], wg["kv_w_a"], sm["kv_a_norm_g"], wg["kv_w_b"], mix_g[1], wg["b_w_q_a"],
                sm["b_q_norm_g"], wg["b_w_q_b"])

    gather(("mlp_w1_0", "mlp_w2_0"))
    (h1, z, gated), _ = _a_mix_fwd(x, mix_g[0], *a_args())
    gather(("kv_w_a", "kv_w_b", "b_w_q_a", "b_w_q_b", "b_w_o"))
    (h2, a0), _ = _mlp_fwd(h1, mlp_g[0], wg["mlp_w1_0"], wg["mlp_w2_0"], 0)
    if dist:
        wg["b_w_q_a"] = wg["b_w_q_a"].reshape(D_MODEL, Q_LORA)
        wg["kv_w_a"] = wg["kv_w_a"].reshape(D_MODEL, KV_LORA + QK_ROPE)
    gather(("mlp_w1_1", "mlp_w2_1"))
    ckv, k, v, cqpre, q, cos, sin = _kvq_fwd(h2, pos, inv_freq, kvq_w())
    (h3, att), _ = _attn_fwd(h2, q, k, v, wg["b_w_o"])
    a1, loss, dh4, d_final_g = _mlp_fwd_loss(h3, mlp_g[1], wg["mlp_w1_1"], wg["mlp_w2_1"], sm["final_norm_g"], target)

    g = {}
    (dh3, d_mlp_g1, hn, f, da, dh3_b), _ = _mlp_bwd(h3, a1, dh4, mlp_g[1], wg["mlp_w1_1"], wg["mlp_w2_1"], 1)
    g["mlp_w1_1"] = _wgrad("wgrad_w1_1", hn, da, _full(t, D_MODEL), _cols(t, FF_SLOT), D_MODEL, FF_SLOT)
    g["mlp_w2_1"] = _wgrad("wgrad_w2_1", f, dh4, _cols(t, FF_SLOT), _full(t, D_MODEL), FF_SLOT, D_MODEL)
    g["b_w_o"] = _wgrad("wgrad_w_o", att, dh3_b, _head(t, V_HEAD), _full(t, D_MODEL), V_HEAD, D_MODEL)
    send("mlp_1", ("mlp_w1_1", "mlp_w2_1", "b_w_o"), True)
    (dq, dk, dv), _ = _attn_bwd(dh3_b, q, k, v, wg["b_w_o"], cos, sin)
    (dh2, hq, hk, cq, dcqpre, c, dkv, dckv, d_mix_g1, d_src_g, d_q_g, d_kv_a_g) = _kvq_bwd(
        h2, dh3, ckv, cqpre, dq, dk, dv, cos, sin, kvq_w())
    g["b_w_q_a"] = _wgrad("wgrad_w_q_a", hq, dcqpre, _cols(t, D_MODEL // N_DEV), _full(t, Q_LORA),
                          D_MODEL // N_DEV, Q_LORA)
    g["b_w_q_b"] = _wgrad("wgrad_w_q_b", cq, dq, _full(t, Q_LORA), _head(t, QK_NOPE + QK_ROPE),
                          Q_LORA, QK_NOPE + QK_ROPE)
    g["kv_w_a"] = _wgrad("wgrad_kv_w_a", hk, dckv, _cols(t, D_MODEL // N_DEV), _full(t, KV_LORA + QK_ROPE),
                         D_MODEL // N_DEV, KV_LORA + QK_ROPE)
    g["kv_w_b"] = _wgrad("wgrad_kv_w_b", c, dkv, _full(t, KV_LORA), _head(t, QK_NOPE + V_HEAD),
                         KV_LORA, QK_NOPE + V_HEAD)
    send("qkv", ("b_w_q_a", "b_w_q_b", "kv_w_a", "kv_w_b"), False)
    (dh1, d_mlp_g0, hn, f, da, dh1_b), _ = _mlp_bwd(h1, a0, dh2, mlp_g[0], wg["mlp_w1_0"], wg["mlp_w2_0"], 0)
    g["mlp_w1_0"] = _wgrad("wgrad_w1_0", hn, da, _full(t, D_MODEL), _cols(t, FF_SLOT), D_MODEL, FF_SLOT)
    g["mlp_w2_0"] = _wgrad("wgrad_w2_0", f, dh2, _cols(t, FF_SLOT), _full(t, D_MODEL), FF_SLOT, D_MODEL)
    g["a_w_out"] = _wgrad("wgrad_a_w_out", gated, dh1_b, _cols(t, GATE_DIM // N_DEV), _full(t, D_MODEL),
                          GATE_DIM // N_DEV, D_MODEL)
    send("mlp_0", ("mlp_w1_0", "mlp_w2_0", "a_w_out"), True)
    (dx, hn, dz, d_mix_g0, d_ln_g, d_ln_b, d_ws, d_bs), _ = _a_mix_bwd(x, z, dh1, mix_g[0], *a_args())
    small = {
        "norm_mix_g": jnp.concatenate([d_mix_g0, d_mix_g1], axis=0),
        "norm_mlp_g": jnp.concatenate([d_mlp_g0, d_mlp_g1], axis=0),
        "a_ln_v_g": d_ln_g.reshape(N_DEV, GATE_DIM // N_DEV),
        "a_ln_v_b": d_ln_b.reshape(N_DEV, GATE_DIM // N_DEV),
        "a_w_s": d_ws.astype(BF16) if dist else d_ws,
        "a_b_s": d_bs.reshape(A_GROUPS, GMLP_BLOCK),
        "b_q_norm_g": d_q_g,
        "kv_src_norm_g": d_src_g,
        "kv_a_norm_g": d_kv_a_g,
        "final_norm_g": d_final_g,
    }
    if dist:
        parts = [small[k].reshape((1,) + small[k].shape) for k in SMALL] + [loss.reshape(1, 1, 1)]
        got = _by_sequencer("gather_small", _exchange_comm(parts=parts), EVERYONE, next(ids))
        small, loss = dict(zip(SMALL, got)), got[-1]
    g["a_w_in"] = _wgrad("wgrad_a_w_in", hn, dz, _full(t, D_MODEL), _cols(t, FF_SLOT), D_MODEL, FF_SLOT)
    return loss, dx, g, small


def _adamw(w, g, m, v):
    m = ADAM_B1 * m + (1.0 - ADAM_B1) * g
    v = ADAM_B2 * v + (1.0 - ADAM_B2) * (g * g)
    m_hat = m / (1.0 - ADAM_B1 ** ADAM_STEP)
    v_hat = v / (1.0 - ADAM_B2 ** ADAM_STEP)
    return -ADAM_LR * (m_hat / (jnp.sqrt(v_hat) + ADAM_EPS) + ADAM_WD * w), m, v


def _sum_in_device_order(r_ref):
    g = r_ref[0].astype(F32)
    for j in range(1, r_ref.shape[0]):
        g = g + r_ref[j].astype(F32)
    return g


def _adamw_sharded(name, recvs, w, m, v, comm=None):
    layers, r, c = w.shape
    tr = math.gcd(r, 512)
    flat = [a for per_layer in recvs for a in per_layer]

    def body(*refs):
        r_refs, (w_ref, m_ref, v_ref) = refs[:len(flat)], refs[len(flat):len(flat) + 3]
        g_ref, d_ref, nm_ref, nv_ref = refs[-4:]
        layer = pl.program_id(0)
        g, pos = None, 0
        for li, per_layer in enumerate(recvs):
            total = None
            for ref in r_refs[pos:pos + len(per_layer)]:
                part = _sum_in_device_order(ref)
                total = part if total is None else total + part
            pos += len(per_layer)
            g = total if g is None else jnp.where(layer == li, total, g)
        g_ref[...] = g
        d_ref[...], nm_ref[...], nv_ref[...] = _adamw(w_ref[...], g, m_ref[...], v_ref[...])

    blk = pl.BlockSpec((None, tr, c), lambda l, i: (l, i, 0))
    return _call(name, body, (layers, r // tr),
                 [pl.BlockSpec((a.shape[0], tr, c), lambda l, i: (0, i, 0)) for a in flat] + [blk] * 3,
                 [blk] * 4, [_sds(w.shape, F32)] * 4, (*flat, w, m, v), comm=comm)


def _adamw_small(recvs, ws, ms, vs, own_row, losses):
    n = len(recvs)

    def body(*refs):
        r_refs, w_refs, m_refs, v_refs = (refs[i * n:(i + 1) * n] for i in range(4))
        outs, scr = refs[4 * n + 1:8 * n + 2], refs[8 * n + 2:]
        outs[-1][...] = _sum_in_device_order(refs[4 * n])
        me = _my_place()[3]
        for a in range(n):
            g = _sum_in_device_order(r_refs[a])
            if own_row[a]:
                scr[0][...] = g
                g = scr[0][pl.ds(me, 1), :]
            g_ref, d_ref, nm_ref, nv_ref = outs[4 * a:4 * a + 4]
            g_ref[...] = g
            d_ref[...], nm_ref[...], nv_ref[...] = _adamw(w_refs[a][...], g, m_refs[a][...], v_refs[a][...])

    out_shape = []
    for w in ws:
        out_shape += [_sds(w.shape, F32)] * 4
    return pl.pallas_call(
        body, name="adamw_small", in_specs=[VMEM] * (4 * n + 1), out_specs=[VMEM] * (4 * n + 1),
        out_shape=out_shape + [_sds((1, 1), F32)], scratch_shapes=[pltpu.VMEM((N_DEV, GATE_DIM // N_DEV), F32)],
    )(*recvs, *ws, *ms, *vs, losses)


BIG = ("a_w_in", "a_w_out", "b_w_q_a", "b_w_q_b", "b_w_o", "kv_w_a", "kv_w_b", "mlp_w1", "mlp_w2")
SMALL = ("norm_mix_g", "norm_mlp_g", "a_ln_v_g", "a_ln_v_b", "a_w_s", "a_b_s", "b_q_norm_g", "kv_src_norm_g",
         "kv_a_norm_g", "final_norm_g")
WEIGHTS = ("norm_mix_g", "norm_mlp_g", "a_w_in", "a_ln_v_g", "a_ln_v_b", "a_w_s", "a_b_s", "a_w_out", "b_w_q_a",
           "b_q_norm_g", "b_w_q_b", "b_w_o", "kv_src_norm_g", "kv_w_a", "kv_a_norm_g", "kv_w_b", "mlp_w1", "mlp_w2",
           "final_norm_g")


def _two_d(name, a):
    if name in ("a_w_s", "a_b_s"):
        return a.reshape(a.shape[1:])
    return a.reshape(1, -1) if a.ndim == 1 else a


def _three_d(a):
    return a if a.ndim == 3 else a.reshape((1,) + a.shape)


def kernel(x, positions, norm_mix_g, norm_mlp_g, a_w_in, a_ln_v_g, a_ln_v_b, a_w_s, a_b_s, a_w_out, b_w_q_a, b_q_norm_g, b_w_q_b, b_w_o, kv_src_norm_g, kv_w_a, kv_a_norm_g, kv_w_b, mlp_w1, mlp_w2, final_norm_g, loss_target, m_norm_mix_g, m_norm_mlp_g, m_a_w_in, m_a_ln_v_g, m_a_ln_v_b, m_a_w_s, m_a_b_s, m_a_w_out, m_b_w_q_a, m_b_q_norm_g, m_b_w_q_b, m_b_w_o, m_kv_src_norm_g, m_kv_w_a, m_kv_a_norm_g, m_kv_w_b, m_mlp_w1, m_mlp_w2, m_final_norm_g, v_norm_mix_g, v_norm_mlp_g, v_a_w_in, v_a_ln_v_g, v_a_ln_v_b, v_a_w_s, v_a_b_s, v_a_w_out, v_b_w_q_a, v_b_q_norm_g, v_b_w_q_b, v_b_w_o, v_kv_src_norm_g, v_kv_w_a, v_kv_a_norm_g, v_kv_w_b, v_mlp_w1, v_mlp_w2, v_final_norm_g):
    w = dict(norm_mix_g=norm_mix_g, norm_mlp_g=norm_mlp_g, a_w_in=a_w_in, a_ln_v_g=a_ln_v_g, a_ln_v_b=a_ln_v_b,
             a_w_s=a_w_s, a_b_s=a_b_s, a_w_out=a_w_out, b_w_q_a=b_w_q_a, b_q_norm_g=b_q_norm_g, b_w_q_b=b_w_q_b,
             b_w_o=b_w_o, kv_src_norm_g=kv_src_norm_g, kv_w_a=kv_w_a, kv_a_norm_g=kv_a_norm_g, kv_w_b=kv_w_b,
             mlp_w1=mlp_w1, mlp_w2=mlp_w2, final_norm_g=final_norm_g)
    m = dict(norm_mix_g=m_norm_mix_g, norm_mlp_g=m_norm_mlp_g, a_w_in=m_a_w_in, a_ln_v_g=m_a_ln_v_g,
             a_ln_v_b=m_a_ln_v_b, a_w_s=m_a_w_s, a_b_s=m_a_b_s, a_w_out=m_a_w_out, b_w_q_a=m_b_w_q_a,
             b_q_norm_g=m_b_q_norm_g, b_w_q_b=m_b_w_q_b, b_w_o=m_b_w_o, kv_src_norm_g=m_kv_src_norm_g,
             kv_w_a=m_kv_w_a, kv_a_norm_g=m_kv_a_norm_g, kv_w_b=m_kv_w_b, mlp_w1=m_mlp_w1, mlp_w2=m_mlp_w2,
             final_norm_g=m_final_norm_g)
    v = dict(norm_mix_g=v_norm_mix_g, norm_mlp_g=v_norm_mlp_g, a_w_in=v_a_w_in, a_ln_v_g=v_a_ln_v_g,
             a_ln_v_b=v_a_ln_v_b, a_w_s=v_a_w_s, a_b_s=v_a_b_s, a_w_out=v_a_w_out, b_w_q_a=v_b_w_q_a,
             b_q_norm_g=v_b_q_norm_g, b_w_q_b=v_b_w_q_b, b_w_o=v_b_w_o, kv_src_norm_g=v_kv_src_norm_g,
             kv_w_a=v_kv_w_a, kv_a_norm_g=v_kv_a_norm_g, kv_w_b=v_kv_w_b, mlp_w1=v_mlp_w1, mlp_w2=v_mlp_w2,
             final_norm_g=v_final_norm_g)
    t = x.shape[1]

    first = ("a_w_in", "a_w_out", "a_ln_v_g", "a_ln_v_b")
    later = ("mlp_w1_0", "mlp_w2_0", "mlp_w1_1", "mlp_w2_1", "kv_w_a", "kv_w_b", "b_w_q_a", "b_w_q_b", "b_w_o")
    blocks = {k: _three_d(w[k]) for k in BIG if not k.startswith("mlp")}
    for k in ("mlp_w1", "mlp_w2"):
        blocks[k + "_0"], blocks[k + "_1"] = w[k][0:1], w[k][1:2]
    got, casts = _gather_first([blocks[k] if k in blocks else w[k] for k in first], [blocks[k] for k in later])
    wg = dict(zip(first, got))
    wg["a_w_out"] = wg["a_w_out"].reshape(GATE_DIM, D_MODEL)
    wg["a_ln_v_g"] = wg["a_ln_v_g"].reshape(1, GATE_DIM)
    wg["a_ln_v_b"] = wg["a_ln_v_b"].reshape(1, GATE_DIM)
    shards = dict(zip(later, casts))

    sm = {k: _two_d(k, w[k]) for k in SMALL if k not in ("a_ln_v_g", "a_ln_v_b")}
    sm["a_b_st"] = sm["a_b_s"].T
    inv_freq = (ROPE_THETA ** (-jnp.arange(0, QK_ROPE, 2, dtype=F32) / QK_ROPE)).reshape(1, QK_ROPE // 2)

    losses, dx, g, small = _local_step(x[0], positions.reshape(t, 1), loss_target[0], inv_freq, wg, sm, shards)

    sums = _pair_reduce("pair_reduce_a", [g["a_w_in"]])
    g["a_w_in"], = _by_sequencer("exchange_last", _chip_exchange_comm(sums), OTHER_CHIPS, collective_id=1)

    out = {}
    for k in BIG:
        recvs = [[g[k + "_0"]], [g[k + "_1"]]] if k.startswith("mlp") else [[g[k]]]
        res, _ = _adamw_sharded("adamw_" + k, recvs, _three_d(w[k]), _three_d(m[k]), _three_d(v[k]))
        out[k] = [o.reshape(w[k].shape) for o in res]
    own_row = [k in ("a_ln_v_g", "a_ln_v_b") for k in SMALL]
    res = _adamw_small([small[k] for k in SMALL], [_two_d(k, w[k]) for k in SMALL], [_two_d(k, m[k]) for k in SMALL],
                       [_two_d(k, v[k]) for k in SMALL], own_row, losses)
    for i, k in enumerate(SMALL):
        out[k] = [o.reshape(w[k].shape) for o in res[4 * i:4 * i + 4]]

    return (res[-1].reshape(()), dx.reshape(x.shape), *[out[k][0] for k in WEIGHTS], *[out[k][1] for k in WEIGHTS],
            *[out[k][2] for k in WEIGHTS], *[out[k][3] for k in WEIGHTS])
```

```python
import math

import jax
import jax.numpy as jnp
from jax import lax
from jax.experimental import pallas as pl
from jax.experimental.pallas import tpu as pltpu
from jax.experimental.pallas import tpu_sc as plsc

F32, BF16 = jnp.float32, jnp.bfloat16
MESH = pl.DeviceIdType.MESH
ANY = pl.BlockSpec(memory_space=pl.ANY)
VMEM = pl.BlockSpec(memory_space=pltpu.VMEM)

N_DEV = 8
D_MODEL = 1024
CHUNK = 64
GMLP_BLOCK = 128
GATE_DIM = 2048
A_GROUPS = 8
A_GROUP_DIM = GATE_DIM // A_GROUPS
B_HEADS = 8
QK_NOPE, QK_ROPE, V_HEAD = 128, 64, 128
Q_LORA, KV_LORA = 384, 256
ROPE_THETA = 10000.0
D_FF = 4096
FF_SLOT = D_FF // N_DEV
EPS = 1e-6
ATT_SCALE = (QK_NOPE + QK_ROPE) ** -0.5

ADAM_LR, ADAM_B1, ADAM_B2, ADAM_EPS, ADAM_WD, ADAM_STEP = 0.001, 0.9, 0.999, 1e-08, 0.01, 10

TM = 256
TM_GATE = 128
VMEM_LIMIT = 56 * 1024 * 1024
INV_SQRT2 = 1.0 / math.sqrt(2.0)
INV_SQRT_2PI = 1.0 / math.sqrt(2.0 * math.pi)
LOG2_E = 1.0 / math.log(2.0)
HEADS_PER_STEP = 2


def _dot(a, b):
    return jnp.dot(a, b, preferred_element_type=F32)


def _dot_nt(a, b):
    return lax.dot_general(a, b, (((1,), (1,)), ((), ())), preferred_element_type=F32)


def _dot_tn(a, b):
    return lax.dot_general(a, b, (((0,), (0,)), ((), ())), preferred_element_type=F32)


def _rms_fwd(x, g):
    rstd = lax.rsqrt(jnp.mean(x * x, axis=-1, keepdims=True) + EPS)
    xhat = x * rstd
    return xhat * g, xhat, rstd


def _rms_bwd(dy, xhat, rstd, g):
    dxhat = dy * g
    dx = rstd * (dxhat - xhat * jnp.mean(dxhat * xhat, axis=-1, keepdims=True))
    return dx, jnp.sum(dy * xhat, axis=0, keepdims=True)


def _ln_fwd(v, g, b):
    mu = jnp.mean(v, axis=-1, keepdims=True)
    vc = v - mu
    rstd = lax.rsqrt(jnp.mean(vc * vc, axis=-1, keepdims=True) + EPS)
    vhat = vc * rstd
    return vhat * g + b, vhat, rstd


def _gelu(x):
    return 0.5 * x * (1.0 + lax.erf(x * INV_SQRT2))


def _gelu_and_grad(x):
    cdf = 0.5 * (1.0 + lax.erf(x * INV_SQRT2))
    return x * cdf, cdf + x * jnp.exp(-0.5 * x * x) * INV_SQRT_2PI


def _rope(x, cos, sin):
    x1, x2 = x[:, :QK_ROPE // 2], x[:, QK_ROPE // 2:]
    return jnp.concatenate([x1 * cos - x2 * sin, x2 * cos + x1 * sin], axis=-1)


def _gate_mask():
    row = lax.broadcasted_iota(jnp.int32, (GMLP_BLOCK, GMLP_BLOCK), 0)
    col = lax.broadcasted_iota(jnp.int32, (GMLP_BLOCK, GMLP_BLOCK), 1)
    return (col < CHUNK) | (row >= CHUNK)


def _att_mask(q0, tq, t):
    q = q0 + lax.broadcasted_iota(jnp.int32, (tq, t), 0)
    k = lax.broadcasted_iota(jnp.int32, (tq, t), 1)
    return jnp.right_shift(k, 6) <= jnp.right_shift(q, 6)


def _res(shape, imap=None):
    zeros = (0,) * len(shape)
    return pl.BlockSpec(shape, imap or (lambda i: zeros), pipeline_mode=pl.Buffered(1))


def _const(shape):
    zeros = (0,) * len(shape)
    return pl.BlockSpec(shape, lambda i: zeros)


def _row(d, tm=TM):
    return pl.BlockSpec((tm, d), lambda i: (i, 0))


def _heads(d):
    return pl.BlockSpec((B_HEADS, TM, d), lambda i: (0, i, 0))


def _sds(shape, dt):
    return jax.ShapeDtypeStruct(shape, dt)


def _acc(ref, val):
    @pl.when(pl.program_id(0) == 0)
    def _():
        ref[...] = jnp.zeros_like(ref)
    ref[...] += val


def _my_place():
    x, y, c = lax.axis_index("x"), lax.axis_index("y"), lax.axis_index("c")
    return x, y, c, 4 * x + 2 * y + c


def _peer(x, y, c, k):
    px = 1 - x if k & 4 else x
    py = 1 - y if k & 2 else y
    pc = 1 - c if k & 1 else c
    return (px, py, pc), 4 * px + 2 * py + pc


CHIPS = (2, 4, 6)


def _splits(ref):
    return len(ref.shape) >= 3 and ref.shape[1] % 32 == 0


def _piece(ref, block, half=None):
    if half is None or not _splits(ref):
        return ref.at[pl.ds(block, 1)]
    rows = ref.shape[1] // 2
    return ref.at[pl.ds(block, 1), pl.ds(half * rows, rows)]


def _gather_copy(sems, a, k, piece, to, src=None):
    return pltpu.make_async_remote_copy(
        src_ref=piece if src is None else src, dst_ref=piece, send_sem=sems[0].at[a, k], recv_sem=sems[1].at[a, k],
        device_id=to, device_id_type=MESH)


def _gather_start(srcs, outs, sems, only=None):
    x, y, c, me = _my_place()
    for a in range(len(srcs)) if only is None else (only,):
        mine = _piece(outs[a], me)
        pltpu.make_async_copy(srcs[a], mine, sems[2].at[a]).start()
        for k, rel in enumerate((1, 4, 2)):
            _gather_copy(sems, a, k, mine, _peer(x, y, c, rel)[0], src=srcs[a]).start()


def _gather_relay(srcs, outs, sems):
    x, y, c, _ = _my_place()
    sib = _peer(x, y, c, 1)[0]
    (xn, xn_i), (yn, yn_i) = _peer(x, y, c, 4), _peer(x, y, c, 2)
    for a in range(len(srcs)):
        out = outs[a]
        _gather_copy(sems, a, 1, _piece(out, xn_i), xn).wait_recv()
        _gather_copy(sems, a, 3, _piece(out, xn_i, 0), yn).start()
        _gather_copy(sems, a, 5, _piece(out, xn_i), sib).start()
        _gather_copy(sems, a, 2, _piece(out, yn_i), yn).wait_recv()
        if _splits(out):
            _gather_copy(sems, a, 4, _piece(out, yn_i, 1), xn).start()
        _gather_copy(sems, a, 6, _piece(out, yn_i), sib).start()


def _gather_finish(srcs, outs, sems):
    x, y, c, me = _my_place()
    sib = _peer(x, y, c, 1)[0]
    xn, yn, dg_i = _peer(x, y, c, 4)[0], _peer(x, y, c, 2)[0], _peer(x, y, c, 6)[1]
    n = len(srcs)
    for a in range(n):
        out = outs[a]
        _gather_copy(sems, a, 3, _piece(out, dg_i, 0), yn).wait_recv()
        _gather_copy(sems, a, 7, _piece(out, dg_i, 0), sib).start()
        if _splits(out):
            _gather_copy(sems, a, 4, _piece(out, dg_i, 1), xn).wait_recv()
            _gather_copy(sems, a, 8, _piece(out, dg_i, 1), sib).start()
    for a in range(n):
        out = outs[a]
        whole, half = _piece(out, me), _piece(out, me, 0)
        for k in (0, 5, 6):
            _gather_copy(sems, a, k, whole, sib).wait_recv()
        for k in (7, 8) if _splits(out) else (7,):
            _gather_copy(sems, a, k, half, sib).wait_recv()
        for k in (0, 1, 2):
            _gather_copy(sems, a, k, whole, sib, src=srcs[a]).wait_send()
        for k in (5, 6):
            _gather_copy(sems, a, k, whole, sib).wait_send()
        for k in (3, 4, 7, 8) if _splits(out) else (3, 7):
            _gather_copy(sems, a, k, half, sib).wait_send()
        pltpu.make_async_copy(srcs[a], whole, sems[2].at[a]).wait()


def _relay_sems(n):
    return [pltpu.SemaphoreType.DMA((n, 9)), pltpu.SemaphoreType.DMA((n, 9)), pltpu.SemaphoreType.DMA((n,))]


def _gather_sems(n):
    return [pltpu.SemaphoreType.DMA((n, 7)), pltpu.SemaphoreType.DMA((n, 7)), pltpu.SemaphoreType.DMA((n,))]


class _Comm:
    def __init__(self, args, out_shape, scratch, start, finish, relay=None):
        self.args, self.out_shape, self.scratch, self.start, self.finish = args, out_shape, scratch, start, finish
        self.relay = relay


def _gather_comm(shards):
    return _Comm(list(shards), [_sds((N_DEV,) + s.shape[1:], s.dtype) for s in shards], _relay_sems(len(shards)),
                 _gather_start, _gather_finish, relay=_gather_relay)


def _direct_copies(ins, outs, sems, wait, from_block):
    send_sems, recv_sems, local_sems = sems
    x, y, c, me = _my_place()
    for a in range(len(ins)):
        src = ins[a].at[pl.ds(me, 1)] if from_block[a] else ins[a]
        local = pltpu.make_async_copy(src, outs[a].at[pl.ds(me, 1)], local_sems.at[a])
        local.wait() if wait else local.start()
        for k in range(1, N_DEV):
            to, to_i = _peer(x, y, c, k)
            cp = pltpu.make_async_remote_copy(
                src_ref=ins[a].at[pl.ds(to_i, 1)] if from_block[a] else ins[a], dst_ref=outs[a].at[pl.ds(me, 1)],
                send_sem=send_sems.at[a, k - 1], recv_sem=recv_sems.at[a, k - 1], device_id=to, device_id_type=MESH)
            cp.wait() if wait else cp.start()


def _exchange_comm(grads=(), parts=()):
    ins = list(grads) + list(parts)
    from_block = [True] * len(grads) + [False] * len(parts)
    out_shape = [_sds(g.shape, g.dtype) for g in grads] + [_sds((N_DEV,) + p.shape[1:], p.dtype) for p in parts]

    def start(ins_, outs_, sems_):
        _direct_copies(ins_, outs_, sems_, False, from_block)

    def finish(ins_, outs_, sems_):
        _direct_copies(ins_, outs_, sems_, True, from_block)

    return _Comm(ins, out_shape, _gather_sems(len(ins)), start, finish)


def _chip_copies(ins, outs, sems, wait, rels, own):
    send_sems, recv_sems, local_sems = sems
    x, y, c, _ = _my_place()
    for a in range(len(ins)):
        if own:
            local = pltpu.make_async_copy(ins[a].at[pl.ds(2 * x + y, 1)], outs[a].at[pl.ds(len(rels), 1)],
                                          local_sems.at[a])
            local.wait() if wait else local.start()
        for i, j in enumerate(rels):
            to = _peer(x, y, c, CHIPS[j])[0]
            cp = pltpu.make_async_remote_copy(
                src_ref=ins[a].at[pl.ds(2 * to[0] + to[1], 1)], dst_ref=outs[a].at[pl.ds(i, 1)],
                send_sem=send_sems.at[a, i], recv_sem=recv_sems.at[a, i], device_id=to, device_id_type=MESH)
            cp.wait() if wait else cp.start()


def _chip_exchange_comm(sums, rels=(0, 1, 2), own=True):
    def start(ins_, outs_, sems_):
        _chip_copies(ins_, outs_, sems_, False, rels, own)

    def finish(ins_, outs_, sems_):
        _chip_copies(ins_, outs_, sems_, True, rels, own)

    n = len(sums)
    sems = [pltpu.SemaphoreType.DMA((n, len(rels))), pltpu.SemaphoreType.DMA((n, len(rels))),
            pltpu.SemaphoreType.DMA((n,))]
    return _Comm(list(sums), [_sds((len(rels) + own,) + s.shape[1:], s.dtype) for s in sums], sems, start, finish)


def _pair_reduce(name, grads):
    n = len(grads)
    n_chips = N_DEV // 2

    def body(*refs):
        g_refs, gh_refs, p_refs, land = refs[:n], refs[n:2 * n], refs[2 * n:3 * n], refs[3 * n:4 * n]
        send_sems, recv_sems = refs[4 * n:]
        x, y, c, _ = _my_place()
        sib = _peer(x, y, c, 1)[0]
        q = pl.program_id(0)

        def to_sibling(a, j):
            return pltpu.make_async_remote_copy(
                src_ref=gh_refs[a].at[j, pl.ds(1 - c, 1)], dst_ref=land[a].at[pl.ds(j, 1)],
                send_sem=send_sems.at[a, j], recv_sem=recv_sems.at[a, j], device_id=sib, device_id_type=MESH)

        @pl.when(q == 0)
        def _():
            for j in range(n_chips):
                for a in range(n):
                    to_sibling(a, j).start()

        for a in range(n):
            to_sibling(a, q).wait_recv()
            p_refs[a][...] = (g_refs[a][0, pl.ds(c, 1)].astype(F32) + land[a][pl.ds(q, 1)].astype(F32)).astype(BF16)

        @pl.when(q == n_chips - 1)
        def _():
            for a in range(n):
                for j in range(n_chips):
                    to_sibling(a, j).wait_send()

    views = [g.reshape((n_chips, 2) + g.shape[1:]) for g in grads]
    res = pl.pallas_call(
        body, name=name, grid=(n_chips,),
        in_specs=[pl.BlockSpec((1, 2) + g.shape[1:], lambda q: (q, 0, 0, 0)) for g in grads] + [ANY] * n,
        out_specs=[pl.BlockSpec((1,) + g.shape[1:], lambda q: (q, 0, 0)) for g in grads],
        out_shape=[_sds((n_chips,) + g.shape[1:], BF16) for g in grads],
        scratch_shapes=[pltpu.VMEM((n_chips,) + g.shape[1:], BF16) for g in grads]
        + [pltpu.SemaphoreType.DMA((n, n_chips)), pltpu.SemaphoreType.DMA((n, n_chips))],
        compiler_params=pltpu.CompilerParams(dimension_semantics=("arbitrary",), vmem_limit_bytes=VMEM_LIMIT),
    )(*views, *views)
    return list(res)


def _pair_exchange_comm(grads):
    n, n_chips = len(grads), N_DEV // 2

    def copies(ins, outs, sems, wait):
        x, y, c, _ = _my_place()
        for j in range(n_chips):
            for a in range(n):
                cp = pltpu.make_async_remote_copy(
                    src_ref=ins[a].at[j, pl.ds(1 - c, 1)], dst_ref=outs[a].at[pl.ds(j, 1)], send_sem=sems[0].at[a, j],
                    recv_sem=sems[1].at[a, j], device_id=_peer(x, y, c, 1)[0], device_id_type=MESH)
                cp.wait() if wait else cp.start()

    views = [g.reshape((n_chips, 2) + g.shape[1:]) for g in grads]
    sems = [pltpu.SemaphoreType.DMA((n, n_chips)), pltpu.SemaphoreType.DMA((n, n_chips))]
    return _Comm(views, [_sds((n_chips,) + g.shape[1:], g.dtype) for g in grads], sems,
                 lambda i, o, s: copies(i, o, s, False), lambda i, o, s: copies(i, o, s, True))


def _pair_add(name, grads, landed):
    n, n_chips = len(grads), N_DEV // 2

    def body(*refs):
        g_refs, l_refs, p_refs = refs[:n], refs[n:2 * n], refs[2 * n:]
        c = lax.axis_index("c")
        for a in range(n):
            p_refs[a][...] = (g_refs[a][0, pl.ds(c, 1)].astype(F32) + l_refs[a][...].astype(F32)).astype(BF16)

    views = [g.reshape((n_chips, 2) + g.shape[1:]) for g in grads]
    blocks = [pl.BlockSpec((1,) + g.shape[1:], lambda q: (q, 0, 0)) for g in grads]
    return _call(name, body, (n_chips,),
                 [pl.BlockSpec((1, 2) + g.shape[1:], lambda q: (q, 0, 0, 0)) for g in grads] + blocks, blocks,
                 [_sds((n_chips,) + g.shape[1:], BF16) for g in grads], (*views, *landed))[0]


def _call(name, body, grid, in_specs, out_specs, out_shape, args, scratch=(), comm=None):
    params = pltpu.CompilerParams(dimension_semantics=("arbitrary",) * len(grid), vmem_limit_bytes=VMEM_LIMIT)
    if comm is None:
        outs = pl.pallas_call(body, name=name, grid=grid, in_specs=list(in_specs), out_specs=list(out_specs),
                              out_shape=list(out_shape), scratch_shapes=list(scratch), compiler_params=params)(*args)
        return list(outs), []
    ni, nci, no, nco, ns = len(in_specs), len(comm.args), len(out_specs), len(comm.out_shape), len(scratch)

    def carrying(*refs):
        ins, refs = refs[:ni], refs[ni:]
        cin, refs = refs[:nci], refs[nci:]
        outs, refs = refs[:no], refs[no:]
        cout, refs = refs[:nco], refs[nco:]
        scr, csems = refs[:ns], refs[ns:]
        step = pl.program_id(0)
        for ax in range(1, len(grid)):
            step = step * grid[ax] + pl.program_id(ax)
        steps = math.prod(grid)

        @pl.when(step == 0)
        def _():
            comm.start(cin, cout, csems)

        if comm.relay is not None:
            @pl.when(step == (2 * steps) // 3)
            def _():
                comm.relay(cin, cout, csems)

        body(*ins, *outs, *scr)

        @pl.when(step == steps - 1)
        def _():
            comm.finish(cin, cout, csems)

    outs = pl.pallas_call(
        carrying, name=name, grid=grid, in_specs=list(in_specs) + [ANY] * nci, out_specs=list(out_specs) + [ANY] * nco,
        out_shape=list(out_shape) + list(comm.out_shape), scratch_shapes=list(scratch) + list(comm.scratch),
        compiler_params=params)(*args, *comm.args)
    return list(outs[:no]), list(outs[no:])


def _comm_only(name, comm):
    def body(*refs):
        nci, nco = len(comm.args), len(comm.out_shape)
        cin, cout, csems = refs[:nci], refs[nci:nci + nco], refs[nci + nco:]
        comm.start(cin, cout, csems)
        if comm.relay is not None:
            comm.relay(cin, cout, csems)
        comm.finish(cin, cout, csems)

    return pl.pallas_call(body, name=name, in_specs=[ANY] * len(comm.args), out_specs=[ANY] * len(comm.out_shape),
                          out_shape=list(comm.out_shape), scratch_shapes=list(comm.scratch))(*comm.args)


SIBLING_AND_NEIGHBOURS, OTHER_CHIPS, EVERYONE = (1, 4, 2), CHIPS, tuple(range(1, N_DEV))


def _by_sequencer(name, comm, peers, collective_id):
    src = [jax.new_ref(a, memory_space=pltpu.MemorySpace.HBM) for a in comm.args]
    dst = [jax.empty_ref(s, memory_space=pltpu.MemorySpace.HBM) for s in comm.out_shape]

    @pl.kernel(mesh=plsc.ScalarSubcoreMesh(axis_name="sequencer", num_cores=1), name=name,
               scratch_types=tuple(comm.scratch), compiler_params=pltpu.CompilerParams(collective_id=collective_id))
    def launch(*sems):
        x, y, c, _ = _my_place()
        barrier = pltpu.get_barrier_semaphore()
        for k in peers:
            pl.semaphore_signal(barrier, inc=1, device_id=_peer(x, y, c, k)[0], device_id_type=MESH)
        pl.semaphore_wait(barrier, len(peers))
        comm.start(src, dst, sems)
        if comm.relay is not None:
            comm.relay(src, dst, sems)
        comm.finish(src, dst, sems)

    launch()
    return [d[...] for d in dst]


def _gather_first(first, later):
    nf, nl = len(first), len(later)
    dts = [BF16] * (nf - 2) + [F32, F32]

    def body(*refs):
        ins, refs = refs[:nf + nl], refs[nf + nl:]
        outs, refs = refs[:nf], refs[nf:]
        casts, refs = refs[:nl], refs[nl:]
        stage, sems = refs[:nf], refs[nf:]
        for a in range(nf):
            stage[a][...] = ins[a][...].astype(dts[a])
            _gather_start(stage, outs, sems, only=a)
        for a in range(nl):
            casts[a][...] = ins[nf + a][...].astype(BF16)
        _gather_relay(stage, outs, sems)
        _gather_finish(stage, outs, sems)

    res = pl.pallas_call(
        body, name="gather_first",
        in_specs=[VMEM] * (nf + nl), out_specs=[ANY] * nf + [VMEM] * nl,
        out_shape=[_sds((N_DEV,) + s.shape[1:], dt) for s, dt in zip(first, dts)]
        + [_sds(s.shape, BF16) for s in later],
        scratch_shapes=[pltpu.VMEM(s.shape, dt) for s, dt in zip(first, dts)] + _relay_sems(nf),
        compiler_params=pltpu.CompilerParams(vmem_limit_bytes=VMEM_LIMIT),
    )(*first, *later)
    return list(res[:nf]), list(res[nf:])


def _a_mix_fwd(x, g, w_in, ln_g, ln_b, w_s, b_st, w_out, comm=None):
    t = x.shape[0]
    nblk = TM // GMLP_BLOCK

    def body(x_ref, g_ref, win_ref, lng_ref, lnb_ref, ws_ref, bst_ref, wout_ref, h_ref, z_ref, gated_scr):
        xv = x_ref[...]
        hb = _rms_fwd(xv, g_ref[...])[0].astype(BF16)
        for d in range(N_DEV):
            z_ref[:, d * FF_SLOT:(d + 1) * FF_SLOT] = _dot(hb, win_ref[d])
        u = _gelu(z_ref[:, :GATE_DIM])
        vb = _ln_fwd(_gelu(z_ref[:, GATE_DIM:]), lng_ref[...], lnb_ref[...])[0].astype(BF16)
        mask = _gate_mask()
        for gi in range(A_GROUPS):
            wm = jnp.where(mask, ws_ref[gi], 0.0).astype(BF16)
            bias = bst_ref[:, gi:gi + 1]
            cs = slice(gi * A_GROUP_DIM, (gi + 1) * A_GROUP_DIM)
            for n in range(nblk):
                rs = slice(n * GMLP_BLOCK, (n + 1) * GMLP_BLOCK)
                sv = _dot(wm, vb[rs, cs]) + bias
                gated_scr[rs, cs] = (u[rs, cs] * sv).astype(BF16)
        h_ref[...] = xv + _dot(gated_scr[...], wout_ref[...])

    return _call(
        "a_mix_fwd", body, (t // TM,),
        [_row(D_MODEL), _res((1, D_MODEL)), _res((N_DEV, D_MODEL, FF_SLOT)), _res((1, GATE_DIM)),
         _res((1, GATE_DIM)), _res((A_GROUPS, GMLP_BLOCK, GMLP_BLOCK)), _res((GMLP_BLOCK, A_GROUPS)),
         _res((GATE_DIM, D_MODEL))],
        [_row(D_MODEL), _row(2 * GATE_DIM), _row(GATE_DIM)],
        [_sds((t, D_MODEL), F32), _sds((t, 2 * GATE_DIM), F32), _sds((t, GATE_DIM), BF16)],
        (x, g, w_in, ln_g, ln_b, w_s, b_st, w_out), comm=comm)


MLP_W_SPECS = (_res((N_DEV, D_MODEL, FF_SLOT)), _res((N_DEV, FF_SLOT, D_MODEL)))


def _mlp_fwd(h, g, w1, w2, layer, comm=None):
    t = h.shape[0]

    def body(h_ref, g_ref, w1_ref, w2_ref, o_ref, a_ref):
        hv = h_ref[...]
        hb = _rms_fwd(hv, g_ref[...])[0].astype(BF16)
        o_ref[...] = hv
        for d in range(N_DEV):
            a = _dot(hb, w1_ref[d])
            a_ref[:, d * FF_SLOT:(d + 1) * FF_SLOT] = a
            r = jnp.maximum(a, 0.0)
            o_ref[...] += _dot((r * r).astype(BF16), w2_ref[d])

    return _call(
        f"mlp_fwd_{layer}", body, (t // TM,), [_row(D_MODEL), _res((1, D_MODEL)), *MLP_W_SPECS],
        [_row(D_MODEL), _row(D_FF)], [_sds((t, D_MODEL), F32), _sds((t, D_FF), F32)], (h, g, w1, w2), comm=comm)


def _mlp_fwd_loss(h, g, w1, w2, final_g, target):
    t = h.shape[0]

    def body(h_ref, g_ref, w1_ref, w2_ref, fg_ref, t_ref, a_ref, loss_ref, dh_ref, dg_ref):
        hv = h_ref[...]
        hb = _rms_fwd(hv, g_ref[...])[0].astype(BF16)
        out = hv
        for d in range(N_DEV):
            a = _dot(hb, w1_ref[d])
            a_ref[:, d * FF_SLOT:(d + 1) * FF_SLOT] = a
            r = jnp.maximum(a, 0.0)
            out = out + _dot((r * r).astype(BF16), w2_ref[d])
        y, xhat, rstd = _rms_fwd(out, fg_ref[...])
        err = y - t_ref[...]
        part = 0.5 * jnp.sum(jnp.mean(err * err, axis=-1, keepdims=True), axis=0, keepdims=True)
        dx, dg = _rms_bwd(err * (1.0 / D_MODEL), xhat, rstd, fg_ref[...])
        dh_ref[...] = dx
        _acc(dg_ref, dg)
        _acc(loss_ref, part)

    return _call(
        "mlp_fwd_loss", body, (t // TM,),
        [_row(D_MODEL), _res((1, D_MODEL)), *MLP_W_SPECS, _res((1, D_MODEL)), _row(D_MODEL)],
        [_row(D_FF), _const((1, 1)), _row(D_MODEL), _const((1, D_MODEL))],
        [_sds((t, D_FF), F32), _sds((1, 1), F32), _sds((t, D_MODEL), F32), _sds((1, D_MODEL), F32)],
        (h, g, w1, w2, final_g, target))[0]


KVQ_W_SPECS = (_res((1, D_MODEL)), _res((D_MODEL, KV_LORA + QK_ROPE)), _res((1, KV_LORA)),
               _res((B_HEADS, KV_LORA, QK_NOPE + V_HEAD)), _res((1, D_MODEL)), _res((D_MODEL, Q_LORA)),
               _res((1, Q_LORA)), _res((B_HEADS, Q_LORA, QK_NOPE + QK_ROPE)))


def _kvq_fwd(h, pos, inv_freq, kvq_w):
    t = h.shape[0]
    half = QK_ROPE // 2

    def body(h_ref, pos_ref, invf_ref, srcg_ref, wkva_ref, kvag_ref, wkvb_ref, mixg_ref, wqa_ref, qg_ref, wqb_ref,
             ckv_ref, k_ref, v_ref, cqpre_ref, q_ref, cos_ref, sin_ref):
        hv = h_ref[...]
        xhat = hv * lax.rsqrt(jnp.mean(hv * hv, axis=-1, keepdims=True) + EPS)
        ang = pos_ref[...].astype(F32) * invf_ref[...]
        cos, sin = jnp.cos(ang), jnp.sin(ang)
        cos_ref[...] = cos
        sin_ref[...] = sin
        ckv = _dot((xhat * srcg_ref[...]).astype(BF16), wkva_ref[...])
        ckv_ref[...] = ckv
        cb = _rms_fwd(ckv[:, :KV_LORA], kvag_ref[...])[0].astype(BF16)
        kpe = _rope(ckv[:, KV_LORA:], cos, sin).astype(BF16)
        for hd in range(B_HEADS):
            kv = _dot(cb, wkvb_ref[hd])
            k_ref[hd, :, 0:QK_NOPE] = kv[:, :QK_NOPE].astype(BF16)
            k_ref[hd, :, QK_NOPE:] = kpe
            v_ref[hd] = kv[:, QK_NOPE:].astype(BF16)
        cqpre = _dot((xhat * mixg_ref[...]).astype(BF16), wqa_ref[...])
        cqpre_ref[...] = cqpre
        cqb = _rms_fwd(cqpre, qg_ref[...])[0].astype(BF16)
        for hd in range(B_HEADS):
            q = _dot(cqb, wqb_ref[hd])
            q_ref[hd, :, 0:QK_NOPE] = q[:, :QK_NOPE].astype(BF16)
            q_ref[hd, :, QK_NOPE:] = _rope(q[:, QK_NOPE:], cos, sin).astype(BF16)

    return _call(
        "kvq_fwd", body, (t // TM,), [_row(D_MODEL), _row(1), _res((1, half)), *KVQ_W_SPECS],
        [_row(KV_LORA + QK_ROPE), _heads(QK_NOPE + QK_ROPE), _heads(V_HEAD), _row(Q_LORA),
         _heads(QK_NOPE + QK_ROPE), _row(half), _row(half)],
        [_sds((t, KV_LORA + QK_ROPE), F32), _sds((B_HEADS, t, QK_NOPE + QK_ROPE), BF16),
         _sds((B_HEADS, t, V_HEAD), BF16), _sds((t, Q_LORA), F32), _sds((B_HEADS, t, QK_NOPE + QK_ROPE), BF16),
         _sds((t, half), F32), _sds((t, half), F32)],
        (h, pos, inv_freq, *kvq_w))[0]


def _softmax_rows(q, k_ref, k):
    past, upto = k * TM, (k + 1) * TM
    s = _dot_nt(q, k_ref[0:upto, :])
    own = jnp.where(_att_mask(0, TM, TM), s[:, past:], jnp.finfo(F32).min)
    s = own if k == 0 else jnp.concatenate([s[:, :past], own], axis=1)
    e = jnp.exp2((s - jnp.max(s, axis=-1, keepdims=True)) * (ATT_SCALE * LOG2_E))
    return e * (1.0 / jnp.sum(e, axis=-1, keepdims=True))


def _for_my_tile(i, nq, fn):
    for k in range(nq):
        @pl.when(i == k)
        def _(k=k):
            fn(k)


def _attn_fwd(h, q, k, v, w_o, comm=None):
    t = h.shape[0]
    nq, hps = t // TM, HEADS_PER_STEP

    def body(h_ref, q_ref, k_ref, v_ref, wo_ref, o_ref, att_ref):
        i, pair = pl.program_id(0), pl.program_id(1)

        @pl.when(pair == 0)
        def _():
            o_ref[...] = h_ref[...]

        def tile(kt):
            proj = None
            for j in range(hps):
                hd = pair * hps + j
                p = _softmax_rows(q_ref[j], k_ref.at[hd], kt)
                ob = _dot(p.astype(BF16), v_ref[hd, 0:(kt + 1) * TM, :]).astype(BF16)
                att_ref[j] = ob
                proj = _dot(ob, wo_ref[hd]) if proj is None else proj + _dot(ob, wo_ref[hd])
            o_ref[...] += proj

        _for_my_tile(i, nq, tile)

    def per_head(d):
        return pl.BlockSpec((hps, TM, d), lambda i, pair: (pair, i, 0))

    def resident(shape):
        zeros = (0,) * len(shape)
        return pl.BlockSpec(shape, lambda i, pair: zeros, pipeline_mode=pl.Buffered(1))

    tile_spec = pl.BlockSpec((TM, D_MODEL), lambda i, pair: (i, 0))
    return _call(
        "attn_fwd", body, (nq, B_HEADS // hps),
        [tile_spec, per_head(QK_NOPE + QK_ROPE), resident((B_HEADS, t, QK_NOPE + QK_ROPE)),
         resident((B_HEADS, t, V_HEAD)), resident((B_HEADS, V_HEAD, D_MODEL))],
        [tile_spec, per_head(V_HEAD)], [_sds((t, D_MODEL), F32), _sds((B_HEADS, t, V_HEAD), BF16)],
        (h, q, k, v, w_o), comm=comm)


def _mlp_bwd(h, a, dho, g, w1, w2, layer, comm=None):
    t = h.shape[0]

    def body(h_ref, a_ref, dho_ref, g_ref, w1_ref, w2_ref, dhi_ref, dg_ref, hn_ref, f_ref, da_ref, dhib_ref):
        gv = g_ref[...]
        y, xhat, rstd = _rms_fwd(h_ref[...], gv)
        hn_ref[...] = y.astype(BF16)
        dho_v = dho_ref[...]
        dhob = dho_v.astype(BF16)
        dhn = jnp.zeros((TM, D_MODEL), F32)
        for d in range(N_DEV):
            cs = slice(d * FF_SLOT, (d + 1) * FF_SLOT)
            r = jnp.maximum(a_ref[:, cs], 0.0)
            f_ref[:, cs] = (r * r).astype(BF16)
            da = (_dot_nt(dhob, w2_ref[d]) * (2.0 * r)).astype(BF16)
            da_ref[:, cs] = da
            dhn = dhn + _dot_nt(da, w1_ref[d])
        dx, dg = _rms_bwd(dhn, xhat, rstd, gv)
        dhi = dho_v + dx
        dhi_ref[...] = dhi
        dhib_ref[...] = dhi.astype(BF16)
        _acc(dg_ref, dg)

    return _call(
        f"mlp_bwd_{layer}", body, (t // TM,),
        [_row(D_MODEL), _row(D_FF), _row(D_MODEL), _res((1, D_MODEL)), *MLP_W_SPECS],
        [_row(D_MODEL), _const((1, D_MODEL)), _row(D_MODEL), _row(D_FF), _row(D_FF), _row(D_MODEL)],
        [_sds((t, D_MODEL), F32), _sds((1, D_MODEL), F32), _sds((t, D_MODEL), BF16), _sds((t, D_FF), BF16),
         _sds((t, D_FF), BF16), _sds((t, D_MODEL), BF16)],
        (h, a, dho, g, w1, w2), comm=comm)


def _attn_bwd(dh, q, k, v, w_o, cos, sin, comm=None):
    t = dh.shape[0]
    half, hps = QK_ROPE // 2, HEADS_PER_STEP

    def body(dh_ref, q_ref, k_ref, v_ref, wo_ref, cos_ref, sin_ref, dq_ref, dk_ref, dv_ref):
        i = pl.program_id(1)

        @pl.when(i == 0)
        def _():
            dk_ref[...] = jnp.zeros_like(dk_ref)
            dv_ref[...] = jnp.zeros_like(dv_ref)

        def tile(kt):
            keys = slice(0, (kt + 1) * TM)
            for j in range(hps):
                qj = q_ref[j]
                do = _dot_nt(dh_ref[kt * TM:(kt + 1) * TM, :], wo_ref[j]).astype(BF16)
                p = _softmax_rows(qj, k_ref.at[j], kt)
                dp = _dot_nt(do, v_ref[j, keys, :])
                ds = (p * (dp - jnp.sum(p * dp, axis=-1, keepdims=True)) * ATT_SCALE).astype(BF16)
                dq = _dot(ds, k_ref[j, keys, :])
                dq_ref[j, :, 0:QK_NOPE] = dq[:, :QK_NOPE].astype(BF16)
                dq_ref[j, :, QK_NOPE:] = _rope(dq[:, QK_NOPE:], cos_ref[...], -sin_ref[...]).astype(BF16)
                dk_ref[j, keys, :] += _dot_tn(ds, qj)
                dv_ref[j, keys, :] += _dot_tn(p.astype(BF16), do)

        _for_my_tile(i, t // TM, tile)

    def per_pair(rows, d, tiled):
        return pl.BlockSpec((hps, rows, d), (lambda pair, i: (pair, i, 0)) if tiled else (lambda pair, i: (pair, 0, 0)))

    def tile(d):
        return pl.BlockSpec((TM, d), lambda pair, i: (i, 0))

    return _call(
        "attn_bwd", body, (B_HEADS // hps, t // TM),
        [pl.BlockSpec((t, D_MODEL), lambda pair, i: (0, 0), pipeline_mode=pl.Buffered(1)),
         per_pair(TM, QK_NOPE + QK_ROPE, True), per_pair(t, QK_NOPE + QK_ROPE, False), per_pair(t, V_HEAD, False),
         per_pair(V_HEAD, D_MODEL, False), tile(half), tile(half)],
        [per_pair(TM, QK_NOPE + QK_ROPE, True), per_pair(t, QK_NOPE + QK_ROPE, False), per_pair(t, V_HEAD, False)],
        [_sds((B_HEADS, t, QK_NOPE + QK_ROPE), BF16), _sds((B_HEADS, t, QK_NOPE + QK_ROPE), F32),
         _sds((B_HEADS, t, V_HEAD), F32)],
        (dh, q, k, v, w_o, cos, sin), comm=comm)


def _kvq_bwd(h, dh, ckv, cqpre, dq, dk, dv, cos, sin, kvq_w):
    t = h.shape[0]
    half = QK_ROPE // 2

    def body(h_ref, dh_ref, ckv_ref, cqpre_ref, dq_ref, dk_ref, dv_ref, cos_ref, sin_ref,
             srcg_ref, wkva_ref, kvag_ref, wkvb_ref, mixg_ref, wqa_ref, qg_ref, wqb_ref,
             dhi_ref, hq_ref, hk_ref, cq_ref, dcqpre_ref, c_ref, dkv_ref, dckv_ref,
             dmixg_ref, dsrcg_ref, dqg_ref, dkvag_ref):
        hv = h_ref[...]
        rstd = lax.rsqrt(jnp.mean(hv * hv, axis=-1, keepdims=True) + EPS)
        xhat = hv * rstd
        mixg, srcg, qg, kvag = mixg_ref[...], srcg_ref[...], qg_ref[...], kvag_ref[...]
        hq_ref[...] = (xhat * mixg).astype(BF16)
        hk_ref[...] = (xhat * srcg).astype(BF16)
        cq, cqhat, crstd = _rms_fwd(cqpre_ref[...], qg)
        cq_ref[...] = cq.astype(BF16)
        dcq = jnp.zeros((TM, Q_LORA), F32)
        for hd in range(B_HEADS):
            dcq = dcq + _dot_nt(dq_ref[hd], wqb_ref[hd])
        dcqpre, dqg = _rms_bwd(dcq, cqhat, crstd, qg)
        dcqpre_b = dcqpre.astype(BF16)
        dcqpre_ref[...] = dcqpre_b
        dxq, dmixg = _rms_bwd(_dot_nt(dcqpre_b, wqa_ref[...]), xhat, rstd, mixg)
        ckv = ckv_ref[...]
        c, chat, krstd = _rms_fwd(ckv[:, :KV_LORA], kvag)
        c_ref[...] = c.astype(BF16)
        dc = jnp.zeros((TM, KV_LORA), F32)
        dkpe = jnp.zeros((TM, QK_ROPE), F32)
        for hd in range(B_HEADS):
            dkv = jnp.concatenate([dk_ref[hd, :, 0:QK_NOPE], dv_ref[hd]], axis=-1).astype(BF16)
            dkv_ref[hd] = dkv
            dc = dc + _dot_nt(dkv, wkvb_ref[hd])
            dkpe = dkpe + dk_ref[hd, :, QK_NOPE:]
        dlat, dkvag = _rms_bwd(dc, chat, krstd, kvag)
        dpe = _rope(dkpe, cos_ref[...], -sin_ref[...])
        dckv_b = jnp.concatenate([dlat, dpe], axis=-1).astype(BF16)
        dckv_ref[...] = dckv_b
        dxk, dsrcg = _rms_bwd(_dot_nt(dckv_b, wkva_ref[...]), xhat, rstd, srcg)
        dhi_ref[...] = dh_ref[...] + dxq + dxk
        _acc(dmixg_ref, dmixg)
        _acc(dsrcg_ref, dsrcg)
        _acc(dqg_ref, dqg)
        _acc(dkvag_ref, dkvag)

    return _call(
        "kvq_bwd", body, (t // TM,),
        [_row(D_MODEL), _row(D_MODEL), _row(KV_LORA + QK_ROPE), _row(Q_LORA), _heads(QK_NOPE + QK_ROPE),
         _heads(QK_NOPE + QK_ROPE), _heads(V_HEAD), _row(half), _row(half), *KVQ_W_SPECS],
        [_row(D_MODEL), _row(D_MODEL), _row(D_MODEL), _row(Q_LORA), _row(Q_LORA), _row(KV_LORA),
         _heads(QK_NOPE + V_HEAD), _row(KV_LORA + QK_ROPE),
         _const((1, D_MODEL)), _const((1, D_MODEL)), _const((1, Q_LORA)), _const((1, KV_LORA))],
        [_sds((t, D_MODEL), F32), _sds((t, D_MODEL), BF16), _sds((t, D_MODEL), BF16), _sds((t, Q_LORA), BF16),
         _sds((t, Q_LORA), BF16), _sds((t, KV_LORA), BF16), _sds((B_HEADS, t, QK_NOPE + V_HEAD), BF16),
         _sds((t, KV_LORA + QK_ROPE), BF16),
         _sds((1, D_MODEL), F32), _sds((1, D_MODEL), F32), _sds((1, Q_LORA), F32), _sds((1, KV_LORA), F32)],
        (h, dh, ckv, cqpre, dq, dk, dv, cos, sin, *kvq_w))[0]


def _a_mix_bwd(x, z, dh, g, w_in, ln_g, ln_b, w_s, b_st, w_out, comm=None):
    t = x.shape[0]
    tm = TM_GATE
    nblk = tm // GMLP_BLOCK

    def body(x_ref, z_ref, dh_ref, g_ref, win_ref, lng_ref, lnb_ref, ws_ref, bst_ref, wout_ref,
             dx_ref, hn_ref, dz_ref, dg_ref, dlng_ref, dlnb_ref, dws_ref, dbs_ref, dvn_scr, gelu_grad_v):
        @pl.when(pl.program_id(0) == 0)
        def _():
            dws_ref[...] = jnp.zeros_like(dws_ref)
            dbs_ref[...] = jnp.zeros_like(dbs_ref)

        gv, lng = g_ref[...], lng_ref[...]
        y, xhat, rstd = _rms_fwd(x_ref[...], gv)
        hn_ref[...] = y.astype(BF16)
        dhv = dh_ref[...]
        dgated = _dot_nt(dhv.astype(BF16), wout_ref[...])
        u, gelu_grad_u = _gelu_and_grad(z_ref[:, :GATE_DIM])
        v, gelu_grad_v[...] = _gelu_and_grad(z_ref[:, GATE_DIM:])
        vn, vhat, lrstd = _ln_fwd(v, lng, lnb_ref[...])
        vb = vn.astype(BF16)
        mask = _gate_mask()
        for gi in range(A_GROUPS):
            wm = jnp.where(mask, ws_ref[gi], 0.0).astype(BF16)
            bias = bst_ref[:, gi:gi + 1]
            cs = slice(gi * A_GROUP_DIM, (gi + 1) * A_GROUP_DIM)
            dws = jnp.zeros((GMLP_BLOCK, GMLP_BLOCK), F32)
            dbs = jnp.zeros((GMLP_BLOCK, 1), F32)
            for n in range(nblk):
                rs = slice(n * GMLP_BLOCK, (n + 1) * GMLP_BLOCK)
                sv = _dot(wm, vb[rs, cs]) + bias
                dz_ref[rs, cs] = (dgated[rs, cs] * sv * gelu_grad_u[rs, cs]).astype(BF16)
                dsv = dgated[rs, cs] * u[rs, cs]
                dsvb = dsv.astype(BF16)
                dws = dws + _dot_nt(dsvb, vb[rs, cs])
                dbs = dbs + jnp.sum(dsv, axis=-1, keepdims=True)
                dvn_scr[rs, cs] = _dot_tn(wm, dsvb)
            dws_ref[gi] += jnp.where(mask, dws, 0.0)
            dbs_ref[gi] += dbs
        dvn = dvn_scr[...]
        dvhat = dvn * lng
        dv = lrstd * (dvhat - jnp.mean(dvhat, axis=-1, keepdims=True)
                      - vhat * jnp.mean(dvhat * vhat, axis=-1, keepdims=True))
        dz_ref[:, GATE_DIM:] = (dv * gelu_grad_v[...]).astype(BF16)
        dhn = jnp.zeros((tm, D_MODEL), F32)
        for d in range(N_DEV):
            dhn = dhn + _dot_nt(dz_ref[:, d * FF_SLOT:(d + 1) * FF_SLOT], win_ref[d])
        dx, dg = _rms_bwd(dhn, xhat, rstd, gv)
        dx_ref[...] = dhv + dx
        _acc(dg_ref, dg)
        _acc(dlng_ref, jnp.sum(dvn * vhat, axis=0, keepdims=True))
        _acc(dlnb_ref, jnp.sum(dvn, axis=0, keepdims=True))

    return _call(
        "a_mix_bwd", body, (t // tm,),
        [_row(D_MODEL, tm), _row(2 * GATE_DIM, tm), _row(D_MODEL, tm), _res((1, D_MODEL)),
         _res((N_DEV, D_MODEL, FF_SLOT)), _res((1, GATE_DIM)), _res((1, GATE_DIM)),
         _res((A_GROUPS, GMLP_BLOCK, GMLP_BLOCK)), _res((GMLP_BLOCK, A_GROUPS)), _res((GATE_DIM, D_MODEL))],
        [_row(D_MODEL, tm), _row(D_MODEL, tm), _row(2 * GATE_DIM, tm),
         _const((1, D_MODEL)), _const((1, GATE_DIM)), _const((1, GATE_DIM)),
         _const((A_GROUPS, GMLP_BLOCK, GMLP_BLOCK)), _const((A_GROUPS, GMLP_BLOCK, 1))],
        [_sds((t, D_MODEL), F32), _sds((t, D_MODEL), BF16),
         _sds((t, 2 * GATE_DIM), BF16), _sds((1, D_MODEL), F32), _sds((1, GATE_DIM), F32),
         _sds((1, GATE_DIM), F32), _sds((A_GROUPS, GMLP_BLOCK, GMLP_BLOCK), F32),
         _sds((A_GROUPS, GMLP_BLOCK, 1), F32)],
        (x, z, dh, g, w_in, ln_g, ln_b, w_s, b_st, w_out),
        scratch=[pltpu.VMEM((tm, GATE_DIM), F32), pltpu.VMEM((tm, GATE_DIM), F32)], comm=comm)


def _wgrad(name, a, b, a_spec, b_spec, m, n, comm=None):
    def body(a_ref, b_ref, o_ref):
        o_ref[0] = _dot_tn(a_ref[...].astype(BF16), b_ref[...].astype(BF16)).astype(BF16)

    outs, got = _call(name, body, (N_DEV,), [a_spec, b_spec], [pl.BlockSpec((1, m, n), lambda d: (d, 0, 0))],
                      [_sds((N_DEV, m, n), BF16)], (a, b), comm=comm)
    return outs[0] if comm is None else (outs[0], got)


def _full(t, d):
    return pl.BlockSpec((t, d), lambda i: (0, 0), pipeline_mode=pl.Buffered(1))


def _cols(t, d):
    return pl.BlockSpec((t, d), lambda i: (0, i))


def _head(t, d):
    return pl.BlockSpec((None, t, d), lambda i: (i, 0, 0))


def _local_step(x, pos, target, inv_freq, wg, sm, shards=None):
    t = x.shape[0]
    wg = dict(wg)
    dist = shards is not None
    mix_g = [sm["norm_mix_g"][l:l + 1] for l in range(2)]
    mlp_g = [sm["norm_mlp_g"][l:l + 1] for l in range(2)]

    ids = iter(range(2, 2 + 9))

    def gather(names):
        if dist:
            got = _by_sequencer("gather_" + names[0], _gather_comm([shards[k] for k in names]),
                                SIBLING_AND_NEIGHBOURS, next(ids))
            wg.update(zip(names, got))

    def send(name, names):
        if dist:
            comm = _exchange_comm(grads=[g[k] for k in names])
            g.update(zip(names, _by_sequencer("exchange_" + name, comm, EVERYONE, next(ids))))

    def send_sums(name, names, meanwhile):
        if dist:
            grads = [g[k] for k in names]
            landed = _by_sequencer("pair_exchange_" + name, _pair_exchange_comm(grads), (1,), next(ids))
        meanwhile()
        if dist:
            sums = _pair_add("pair_add_" + name, grads, landed)
            g.update(zip(names, _by_sequencer("exchange_" + name, _chip_exchange_comm(sums), OTHER_CHIPS, next(ids))))

    def a_args():
        return (wg["a_w_in"], wg["a_ln_v_g"], wg["a_ln_v_b"], sm["a_w_s"], sm["a_b_st"], wg["a_w_out"])

    def kvq_w():
        return (sm["kv_src_norm_g"], wg["kv_w_a"], sm["kv_a_norm_g"], wg["kv_w_b"], mix_g[1], wg["b_w_q_a"],
                sm["b_q_norm_g"], wg["b_w_q_b"])

    gather(("mlp_w1_0", "mlp_w2_0"))
    (h1, z, gated), _ = _a_mix_fwd(x, mix_g[0], *a_args())
    gather(("kv_w_a", "kv_w_b", "b_w_q_a", "b_w_q_b", "b_w_o"))
    (h2, a0), _ = _mlp_fwd(h1, mlp_g[0], wg["mlp_w1_0"], wg["mlp_w2_0"], 0)
    if dist:
        wg["b_w_q_a"] = wg["b_w_q_a"].reshape(D_MODEL, Q_LORA)
        wg["kv_w_a"] = wg["kv_w_a"].reshape(D_MODEL, KV_LORA + QK_ROPE)
    gather(("mlp_w1_1", "mlp_w2_1"))
    ckv, k, v, cqpre, q, cos, sin = _kvq_fwd(h2, pos, inv_freq, kvq_w())
    (h3, att), _ = _attn_fwd(h2, q, k, v, wg["b_w_o"])
    a1, loss, dh4, d_final_g = _mlp_fwd_loss(h3, mlp_g[1], wg["mlp_w1_1"], wg["mlp_w2_1"], sm["final_norm_g"], target)

    g = {}
    (dh3, d_mlp_g1, hn, f, da, dh3_b), _ = _mlp_bwd(h3, a1, dh4, mlp_g[1], wg["mlp_w1_1"], wg["mlp_w2_1"], 1)
    g["mlp_w1_1"] = _wgrad("wgrad_w1_1", hn, da, _full(t, D_MODEL), _cols(t, FF_SLOT), D_MODEL, FF_SLOT)
    g["mlp_w2_1"] = _wgrad("wgrad_w2_1", f, dh4, _cols(t, FF_SLOT), _full(t, D_MODEL), FF_SLOT, D_MODEL)

    def wgrad_w_o():
        g["b_w_o"] = _wgrad("wgrad_w_o", att, dh3_b, _head(t, V_HEAD), _full(t, D_MODEL), V_HEAD, D_MODEL)

    send_sums("mlp_1", ("mlp_w1_1", "mlp_w2_1"), wgrad_w_o)
    (dq, dk, dv), _ = _attn_bwd(dh3_b, q, k, v, wg["b_w_o"], cos, sin)
    (dh2, hq, hk, cq, dcqpre, c, dkv, dckv, d_mix_g1, d_src_g, d_q_g, d_kv_a_g) = _kvq_bwd(
        h2, dh3, ckv, cqpre, dq, dk, dv, cos, sin, kvq_w())
    g["b_w_q_a"] = _wgrad("wgrad_w_q_a", hq, dcqpre, _cols(t, D_MODEL // N_DEV), _full(t, Q_LORA),
                          D_MODEL // N_DEV, Q_LORA)
    g["b_w_q_b"] = _wgrad("wgrad_w_q_b", cq, dq, _full(t, Q_LORA), _head(t, QK_NOPE + QK_ROPE),
                          Q_LORA, QK_NOPE + QK_ROPE)
    g["kv_w_a"] = _wgrad("wgrad_kv_w_a", hk, dckv, _cols(t, D_MODEL // N_DEV), _full(t, KV_LORA + QK_ROPE),
                         D_MODEL // N_DEV, KV_LORA + QK_ROPE)
    g["kv_w_b"] = _wgrad("wgrad_kv_w_b", c, dkv, _full(t, KV_LORA), _head(t, QK_NOPE + V_HEAD),
                         KV_LORA, QK_NOPE + V_HEAD)
    send("qkv", ("b_w_o", "b_w_q_a", "b_w_q_b", "kv_w_a", "kv_w_b"))
    (dh1, d_mlp_g0, hn, f, da, dh1_b), _ = _mlp_bwd(h1, a0, dh2, mlp_g[0], wg["mlp_w1_0"], wg["mlp_w2_0"], 0)
    g["mlp_w1_0"] = _wgrad("wgrad_w1_0", hn, da, _full(t, D_MODEL), _cols(t, FF_SLOT), D_MODEL, FF_SLOT)
    g["mlp_w2_0"] = _wgrad("wgrad_w2_0", f, dh2, _cols(t, FF_SLOT), _full(t, D_MODEL), FF_SLOT, D_MODEL)

    def wgrad_a_w_out():
        g["a_w_out"] = _wgrad("wgrad_a_w_out", gated, dh1_b, _cols(t, GATE_DIM // N_DEV), _full(t, D_MODEL),
                              GATE_DIM // N_DEV, D_MODEL)

    send_sums("mlp_0", ("mlp_w1_0", "mlp_w2_0"), wgrad_a_w_out)
    (dx, hn, dz, d_mix_g0, d_ln_g, d_ln_b, d_ws, d_bs), _ = _a_mix_bwd(x, z, dh1, mix_g[0], *a_args())
    small = {
        "norm_mix_g": jnp.concatenate([d_mix_g0, d_mix_g1], axis=0),
        "norm_mlp_g": jnp.concatenate([d_mlp_g0, d_mlp_g1], axis=0),
        "a_ln_v_g": d_ln_g.reshape(N_DEV, GATE_DIM // N_DEV),
        "a_ln_v_b": d_ln_b.reshape(N_DEV, GATE_DIM // N_DEV),
        "a_w_s": d_ws.astype(BF16) if dist else d_ws,
        "a_b_s": d_bs.reshape(A_GROUPS, GMLP_BLOCK),
        "b_q_norm_g": d_q_g,
        "kv_src_norm_g": d_src_g,
        "kv_a_norm_g": d_kv_a_g,
        "final_norm_g": d_final_g,
    }
    if dist:
        parts = [small[k].reshape((1,) + small[k].shape) for k in SMALL] + [loss.reshape(1, 1, 1)]
        got = _by_sequencer("gather_small", _exchange_comm(parts=parts), EVERYONE, next(ids))
        small, loss = dict(zip(SMALL, got)), got[-1]
    g["a_w_in"] = _wgrad("wgrad_a_w_in", hn, dz, _full(t, D_MODEL), _cols(t, FF_SLOT), D_MODEL, FF_SLOT)
    return loss, dx, g, small


def _adamw(w, g, m, v):
    m = ADAM_B1 * m + (1.0 - ADAM_B1) * g
    v = ADAM_B2 * v + (1.0 - ADAM_B2) * (g * g)
    m_hat = m / (1.0 - ADAM_B1 ** ADAM_STEP)
    v_hat = v / (1.0 - ADAM_B2 ** ADAM_STEP)
    return -ADAM_LR * (m_hat / (jnp.sqrt(v_hat) + ADAM_EPS) + ADAM_WD * w), m, v


def _sum_in_device_order(r_ref):
    g = r_ref[0].astype(F32)
    for j in range(1, r_ref.shape[0]):
        g = g + r_ref[j].astype(F32)
    return g


def _adamw_sharded(name, recvs, w, m, v, comm=None):
    layers, r, c = w.shape
    tr = math.gcd(r, 512)
    flat = [a for per_layer in recvs for a in per_layer]

    def body(*refs):
        r_refs, (w_ref, m_ref, v_ref) = refs[:len(flat)], refs[len(flat):len(flat) + 3]
        g_ref, d_ref, nm_ref, nv_ref = refs[-4:]
        layer = pl.program_id(0)
        g, pos = None, 0
        for li, per_layer in enumerate(recvs):
            total = None
            for ref in r_refs[pos:pos + len(per_layer)]:
                part = _sum_in_device_order(ref)
                total = part if total is None else total + part
            pos += len(per_layer)
            g = total if g is None else jnp.where(layer == li, total, g)
        g_ref[...] = g
        d_ref[...], nm_ref[...], nv_ref[...] = _adamw(w_ref[...], g, m_ref[...], v_ref[...])

    blk = pl.BlockSpec((None, tr, c), lambda l, i: (l, i, 0))
    return _call(name, body, (layers, r // tr),
                 [pl.BlockSpec((a.shape[0], tr, c), lambda l, i: (0, i, 0)) for a in flat] + [blk] * 3,
                 [blk] * 4, [_sds(w.shape, F32)] * 4, (*flat, w, m, v), comm=comm)


def _adamw_small(recvs, ws, ms, vs, own_row, losses):
    n = len(recvs)

    def body(*refs):
        r_refs, w_refs, m_refs, v_refs = (refs[i * n:(i + 1) * n] for i in range(4))
        outs, scr = refs[4 * n + 1:8 * n + 2], refs[8 * n + 2:]
        outs[-1][...] = _sum_in_device_order(refs[4 * n])
        me = _my_place()[3]
        for a in range(n):
            g = _sum_in_device_order(r_refs[a])
            if own_row[a]:
                scr[0][...] = g
                g = scr[0][pl.ds(me, 1), :]
            g_ref, d_ref, nm_ref, nv_ref = outs[4 * a:4 * a + 4]
            g_ref[...] = g
            d_ref[...], nm_ref[...], nv_ref[...] = _adamw(w_refs[a][...], g, m_refs[a][...], v_refs[a][...])

    out_shape = []
    for w in ws:
        out_shape += [_sds(w.shape, F32)] * 4
    return pl.pallas_call(
        body, name="adamw_small", in_specs=[VMEM] * (4 * n + 1), out_specs=[VMEM] * (4 * n + 1),
        out_shape=out_shape + [_sds((1, 1), F32)], scratch_shapes=[pltpu.VMEM((N_DEV, GATE_DIM // N_DEV), F32)],
    )(*recvs, *ws, *ms, *vs, losses)


BIG = ("a_w_in", "a_w_out", "b_w_q_a", "b_w_q_b", "b_w_o", "kv_w_a", "kv_w_b", "mlp_w1", "mlp_w2")
SMALL = ("norm_mix_g", "norm_mlp_g", "a_ln_v_g", "a_ln_v_b", "a_w_s", "a_b_s", "b_q_norm_g", "kv_src_norm_g",
         "kv_a_norm_g", "final_norm_g")
WEIGHTS = ("norm_mix_g", "norm_mlp_g", "a_w_in", "a_ln_v_g", "a_ln_v_b", "a_w_s", "a_b_s", "a_w_out", "b_w_q_a",
           "b_q_norm_g", "b_w_q_b", "b_w_o", "kv_src_norm_g", "kv_w_a", "kv_a_norm_g", "kv_w_b", "mlp_w1", "mlp_w2",
           "final_norm_g")


def _two_d(name, a):
    if name in ("a_w_s", "a_b_s"):
        return a.reshape(a.shape[1:])
    return a.reshape(1, -1) if a.ndim == 1 else a


def _three_d(a):
    return a if a.ndim == 3 else a.reshape((1,) + a.shape)


def kernel(x, positions, norm_mix_g, norm_mlp_g, a_w_in, a_ln_v_g, a_ln_v_b, a_w_s, a_b_s, a_w_out, b_w_q_a, b_q_norm_g, b_w_q_b, b_w_o, kv_src_norm_g, kv_w_a, kv_a_norm_g, kv_w_b, mlp_w1, mlp_w2, final_norm_g, loss_target, m_norm_mix_g, m_norm_mlp_g, m_a_w_in, m_a_ln_v_g, m_a_ln_v_b, m_a_w_s, m_a_b_s, m_a_w_out, m_b_w_q_a, m_b_q_norm_g, m_b_w_q_b, m_b_w_o, m_kv_src_norm_g, m_kv_w_a, m_kv_a_norm_g, m_kv_w_b, m_mlp_w1, m_mlp_w2, m_final_norm_g, v_norm_mix_g, v_norm_mlp_g, v_a_w_in, v_a_ln_v_g, v_a_ln_v_b, v_a_w_s, v_a_b_s, v_a_w_out, v_b_w_q_a, v_b_q_norm_g, v_b_w_q_b, v_b_w_o, v_kv_src_norm_g, v_kv_w_a, v_kv_a_norm_g, v_kv_w_b, v_mlp_w1, v_mlp_w2, v_final_norm_g):
    w = dict(norm_mix_g=norm_mix_g, norm_mlp_g=norm_mlp_g, a_w_in=a_w_in, a_ln_v_g=a_ln_v_g, a_ln_v_b=a_ln_v_b,
             a_w_s=a_w_s, a_b_s=a_b_s, a_w_out=a_w_out, b_w_q_a=b_w_q_a, b_q_norm_g=b_q_norm_g, b_w_q_b=b_w_q_b,
             b_w_o=b_w_o, kv_src_norm_g=kv_src_norm_g, kv_w_a=kv_w_a, kv_a_norm_g=kv_a_norm_g, kv_w_b=kv_w_b,
             mlp_w1=mlp_w1, mlp_w2=mlp_w2, final_norm_g=final_norm_g)
    m = dict(norm_mix_g=m_norm_mix_g, norm_mlp_g=m_norm_mlp_g, a_w_in=m_a_w_in, a_ln_v_g=m_a_ln_v_g,
             a_ln_v_b=m_a_ln_v_b, a_w_s=m_a_w_s, a_b_s=m_a_b_s, a_w_out=m_a_w_out, b_w_q_a=m_b_w_q_a,
             b_q_norm_g=m_b_q_norm_g, b_w_q_b=m_b_w_q_b, b_w_o=m_b_w_o, kv_src_norm_g=m_kv_src_norm_g,
             kv_w_a=m_kv_w_a, kv_a_norm_g=m_kv_a_norm_g, kv_w_b=m_kv_w_b, mlp_w1=m_mlp_w1, mlp_w2=m_mlp_w2,
             final_norm_g=m_final_norm_g)
    v = dict(norm_mix_g=v_norm_mix_g, norm_mlp_g=v_norm_mlp_g, a_w_in=v_a_w_in, a_ln_v_g=v_a_ln_v_g,
             a_ln_v_b=v_a_ln_v_b, a_w_s=v_a_w_s, a_b_s=v_a_b_s, a_w_out=v_a_w_out, b_w_q_a=v_b_w_q_a,
             b_q_norm_g=v_b_q_norm_g, b_w_q_b=v_b_w_q_b, b_w_o=v_b_w_o, kv_src_norm_g=v_kv_src_norm_g,
             kv_w_a=v_kv_w_a, kv_a_norm_g=v_kv_a_norm_g, kv_w_b=v_kv_w_b, mlp_w1=v_mlp_w1, mlp_w2=v_mlp_w2,
             final_norm_g=v_final_norm_g)
    t = x.shape[1]

    first = ("a_w_in", "a_w_out", "a_ln_v_g", "a_ln_v_b")
    later = ("mlp_w1_0", "mlp_w2_0", "mlp_w1_1", "mlp_w2_1", "kv_w_a", "kv_w_b", "b_w_q_a", "b_w_q_b", "b_w_o")
    blocks = {k: _three_d(w[k]) for k in BIG if not k.startswith("mlp")}
    for k in ("mlp_w1", "mlp_w2"):
        blocks[k + "_0"], blocks[k + "_1"] = w[k][0:1], w[k][1:2]
    got, casts = _gather_first([blocks[k] if k in blocks else w[k] for k in first], [blocks[k] for k in later])
    wg = dict(zip(first, got))
    wg["a_w_out"] = wg["a_w_out"].reshape(GATE_DIM, D_MODEL)
    wg["a_ln_v_g"] = wg["a_ln_v_g"].reshape(1, GATE_DIM)
    wg["a_ln_v_b"] = wg["a_ln_v_b"].reshape(1, GATE_DIM)
    shards = dict(zip(later, casts))

    sm = {k: _two_d(k, w[k]) for k in SMALL if k not in ("a_ln_v_g", "a_ln_v_b")}
    sm["a_b_st"] = sm["a_b_s"].T
    inv_freq = (ROPE_THETA ** (-jnp.arange(0, QK_ROPE, 2, dtype=F32) / QK_ROPE)).reshape(1, QK_ROPE // 2)

    losses, dx, g, small = _local_step(x[0], positions.reshape(t, 1), loss_target[0], inv_freq, wg, sm, shards)

    names = ("a_w_in", "a_w_out")
    sums = _pair_reduce("pair_reduce_a", [g[k] for k in names])
    g.update(zip(names, _by_sequencer("exchange_last", _chip_exchange_comm(sums), OTHER_CHIPS, collective_id=1)))

    out = {}
    for k in BIG:
        recvs = [[g[k + "_0"]], [g[k + "_1"]]] if k.startswith("mlp") else [[g[k]]]
        res, _ = _adamw_sharded("adamw_" + k, recvs, _three_d(w[k]), _three_d(m[k]), _three_d(v[k]))
        out[k] = [o.reshape(w[k].shape) for o in res]
    own_row = [k in ("a_ln_v_g", "a_ln_v_b") for k in SMALL]
    res = _adamw_small([small[k] for k in SMALL], [_two_d(k, w[k]) for k in SMALL], [_two_d(k, m[k]) for k in SMALL],
                       [_two_d(k, v[k]) for k in SMALL], own_row, losses)
    for i, k in enumerate(SMALL):
        out[k] = [o.reshape(w[k].shape) for o in res[4 * i:4 * i + 4]]

    return (res[-1].reshape(()), dx.reshape(x.shape), *[out[k][0] for k in WEIGHTS], *[out[k][1] for k in WEIGHTS],
            *[out[k][2] for k in WEIGHTS], *[out[k][3] for k in WEIGHTS])
```

```python
import math

import jax
import jax.numpy as jnp
from jax import lax
from jax.experimental import pallas as pl
from jax.experimental.pallas import tpu as pltpu
from jax.experimental.pallas import tpu_sc as plsc

F32, BF16 = jnp.float32, jnp.bfloat16
MESH = pl.DeviceIdType.MESH
ANY = pl.BlockSpec(memory_space=pl.ANY)
VMEM = pl.BlockSpec(memory_space=pltpu.VMEM)

N_DEV = 8
D_MODEL = 1024
CHUNK = 64
GMLP_BLOCK = 128
GATE_DIM = 2048
A_GROUPS = 8
A_GROUP_DIM = GATE_DIM // A_GROUPS
B_HEADS = 8
QK_NOPE, QK_ROPE, V_HEAD = 128, 64, 128
Q_LORA, KV_LORA = 384, 256
ROPE_THETA = 10000.0
D_FF = 4096
FF_SLOT = D_FF // N_DEV
EPS = 1e-6
ATT_SCALE = (QK_NOPE + QK_ROPE) ** -0.5

ADAM_LR, ADAM_B1, ADAM_B2, ADAM_EPS, ADAM_WD, ADAM_STEP = 0.001, 0.9, 0.999, 1e-08, 0.01, 10

TM = 256
TM_GATE = 128
VMEM_LIMIT = 56 * 1024 * 1024
INV_SQRT2 = 1.0 / math.sqrt(2.0)
INV_SQRT_2PI = 1.0 / math.sqrt(2.0 * math.pi)
LOG2_E = 1.0 / math.log(2.0)
HEADS_PER_STEP = 2


def _dot(a, b):
    return jnp.dot(a, b, preferred_element_type=F32)


def _dot_nt(a, b):
    return lax.dot_general(a, b, (((1,), (1,)), ((), ())), preferred_element_type=F32)


def _dot_tn(a, b):
    return lax.dot_general(a, b, (((0,), (0,)), ((), ())), preferred_element_type=F32)


def _rms_fwd(x, g):
    rstd = lax.rsqrt(jnp.mean(x * x, axis=-1, keepdims=True) + EPS)
    xhat = x * rstd
    return xhat * g, xhat, rstd


def _rms_bwd(dy, xhat, rstd, g):
    dxhat = dy * g
    dx = rstd * (dxhat - xhat * jnp.mean(dxhat * xhat, axis=-1, keepdims=True))
    return dx, jnp.sum(dy * xhat, axis=0, keepdims=True)


def _ln_fwd(v, g, b):
    mu = jnp.mean(v, axis=-1, keepdims=True)
    vc = v - mu
    rstd = lax.rsqrt(jnp.mean(vc * vc, axis=-1, keepdims=True) + EPS)
    vhat = vc * rstd
    return vhat * g + b, vhat, rstd


def _gelu(x):
    return 0.5 * x * (1.0 + lax.erf(x * INV_SQRT2))


def _gelu_and_grad(x):
    cdf = 0.5 * (1.0 + lax.erf(x * INV_SQRT2))
    return x * cdf, cdf + x * jnp.exp(-0.5 * x * x) * INV_SQRT_2PI


def _rope(x, cos, sin):
    x1, x2 = x[:, :QK_ROPE // 2], x[:, QK_ROPE // 2:]
    return jnp.concatenate([x1 * cos - x2 * sin, x2 * cos + x1 * sin], axis=-1)


def _gate_mask():
    row = lax.broadcasted_iota(jnp.int32, (GMLP_BLOCK, GMLP_BLOCK), 0)
    col = lax.broadcasted_iota(jnp.int32, (GMLP_BLOCK, GMLP_BLOCK), 1)
    return (col < CHUNK) | (row >= CHUNK)


def _att_mask(q0, tq, t):
    q = q0 + lax.broadcasted_iota(jnp.int32, (tq, t), 0)
    k = lax.broadcasted_iota(jnp.int32, (tq, t), 1)
    return jnp.right_shift(k, 6) <= jnp.right_shift(q, 6)


def _res(shape, imap=None):
    zeros = (0,) * len(shape)
    return pl.BlockSpec(shape, imap or (lambda i: zeros), pipeline_mode=pl.Buffered(1))


def _const(shape):
    zeros = (0,) * len(shape)
    return pl.BlockSpec(shape, lambda i: zeros)


def _row(d, tm=TM):
    return pl.BlockSpec((tm, d), lambda i: (i, 0))


def _heads(d):
    return pl.BlockSpec((B_HEADS, TM, d), lambda i: (0, i, 0))


def _sds(shape, dt):
    return jax.ShapeDtypeStruct(shape, dt)


def _acc(ref, val):
    @pl.when(pl.program_id(0) == 0)
    def _():
        ref[...] = jnp.zeros_like(ref)
    ref[...] += val


def _my_place():
    x, y, c = lax.axis_index("x"), lax.axis_index("y"), lax.axis_index("c")
    return x, y, c, 4 * x + 2 * y + c


def _peer(x, y, c, k):
    px = 1 - x if k & 4 else x
    py = 1 - y if k & 2 else y
    pc = 1 - c if k & 1 else c
    return (px, py, pc), 4 * px + 2 * py + pc


CHIPS = (2, 4, 6)


def _splits(ref):
    return len(ref.shape) >= 3 and ref.shape[1] % 32 == 0


def _piece(ref, block, half=None):
    if half is None or not _splits(ref):
        return ref.at[pl.ds(block, 1)]
    rows = ref.shape[1] // 2
    return ref.at[pl.ds(block, 1), pl.ds(half * rows, rows)]


def _gather_copy(sems, a, k, piece, to, src=None):
    return pltpu.make_async_remote_copy(
        src_ref=piece if src is None else src, dst_ref=piece, send_sem=sems[0].at[a, k], recv_sem=sems[1].at[a, k],
        device_id=to, device_id_type=MESH)


def _gather_start(srcs, outs, sems, only=None):
    x, y, c, me = _my_place()
    for a in range(len(srcs)) if only is None else (only,):
        mine = _piece(outs[a], me)
        pltpu.make_async_copy(srcs[a], mine, sems[2].at[a]).start()
        for k, rel in enumerate((1, 4, 2)):
            _gather_copy(sems, a, k, mine, _peer(x, y, c, rel)[0], src=srcs[a]).start()


def _gather_relay(srcs, outs, sems):
    x, y, c, _ = _my_place()
    sib = _peer(x, y, c, 1)[0]
    (xn, xn_i), (yn, yn_i) = _peer(x, y, c, 4), _peer(x, y, c, 2)
    for a in range(len(srcs)):
        out = outs[a]
        _gather_copy(sems, a, 1, _piece(out, xn_i), xn).wait_recv()
        _gather_copy(sems, a, 3, _piece(out, xn_i, 0), yn).start()
        _gather_copy(sems, a, 5, _piece(out, xn_i), sib).start()
        _gather_copy(sems, a, 2, _piece(out, yn_i), yn).wait_recv()
        if _splits(out):
            _gather_copy(sems, a, 4, _piece(out, yn_i, 1), xn).start()
        _gather_copy(sems, a, 6, _piece(out, yn_i), sib).start()


def _gather_finish(srcs, outs, sems):
    x, y, c, me = _my_place()
    sib = _peer(x, y, c, 1)[0]
    xn, yn, dg_i = _peer(x, y, c, 4)[0], _peer(x, y, c, 2)[0], _peer(x, y, c, 6)[1]
    n = len(srcs)
    for a in range(n):
        out = outs[a]
        _gather_copy(sems, a, 3, _piece(out, dg_i, 0), yn).wait_recv()
        _gather_copy(sems, a, 7, _piece(out, dg_i, 0), sib).start()
        if _splits(out):
            _gather_copy(sems, a, 4, _piece(out, dg_i, 1), xn).wait_recv()
            _gather_copy(sems, a, 8, _piece(out, dg_i, 1), sib).start()
    for a in range(n):
        out = outs[a]
        whole, half = _piece(out, me), _piece(out, me, 0)
        for k in (0, 5, 6):
            _gather_copy(sems, a, k, whole, sib).wait_recv()
        for k in (7, 8) if _splits(out) else (7,):
            _gather_copy(sems, a, k, half, sib).wait_recv()
        for k in (0, 1, 2):
            _gather_copy(sems, a, k, whole, sib, src=srcs[a]).wait_send()
        for k in (5, 6):
            _gather_copy(sems, a, k, whole, sib).wait_send()
        for k in (3, 4, 7, 8) if _splits(out) else (3, 7):
            _gather_copy(sems, a, k, half, sib).wait_send()
        pltpu.make_async_copy(srcs[a], whole, sems[2].at[a]).wait()


def _relay_sems(n):
    return [pltpu.SemaphoreType.DMA((n, 9)), pltpu.SemaphoreType.DMA((n, 9)), pltpu.SemaphoreType.DMA((n,))]


def _gather_sems(n):
    return [pltpu.SemaphoreType.DMA((n, 7)), pltpu.SemaphoreType.DMA((n, 7)), pltpu.SemaphoreType.DMA((n,))]


class _Comm:
    def __init__(self, args, out_shape, scratch, start, finish, relay=None):
        self.args, self.out_shape, self.scratch, self.start, self.finish = args, out_shape, scratch, start, finish
        self.relay = relay


def _gather_comm(shards):
    return _Comm(list(shards), [_sds((N_DEV,) + s.shape[1:], s.dtype) for s in shards], _relay_sems(len(shards)),
                 _gather_start, _gather_finish, relay=_gather_relay)


def _direct_copies(ins, outs, sems, wait, from_block):
    send_sems, recv_sems, local_sems = sems
    x, y, c, me = _my_place()
    for a in range(len(ins)):
        src = ins[a].at[pl.ds(me, 1)] if from_block[a] else ins[a]
        local = pltpu.make_async_copy(src, outs[a].at[pl.ds(me, 1)], local_sems.at[a])
        local.wait() if wait else local.start()
        for k in range(1, N_DEV):
            to, to_i = _peer(x, y, c, k)
            cp = pltpu.make_async_remote_copy(
                src_ref=ins[a].at[pl.ds(to_i, 1)] if from_block[a] else ins[a], dst_ref=outs[a].at[pl.ds(me, 1)],
                send_sem=send_sems.at[a, k - 1], recv_sem=recv_sems.at[a, k - 1], device_id=to, device_id_type=MESH)
            cp.wait() if wait else cp.start()


def _exchange_comm(grads=(), parts=()):
    ins = list(grads) + list(parts)
    from_block = [True] * len(grads) + [False] * len(parts)
    out_shape = [_sds(g.shape, g.dtype) for g in grads] + [_sds((N_DEV,) + p.shape[1:], p.dtype) for p in parts]

    def start(ins_, outs_, sems_):
        _direct_copies(ins_, outs_, sems_, False, from_block)

    def finish(ins_, outs_, sems_):
        _direct_copies(ins_, outs_, sems_, True, from_block)

    return _Comm(ins, out_shape, _gather_sems(len(ins)), start, finish)


def _chip_copies(ins, outs, sems, wait, rels, own):
    send_sems, recv_sems, local_sems = sems
    x, y, c, _ = _my_place()
    for a in range(len(ins)):
        if own:
            local = pltpu.make_async_copy(ins[a].at[pl.ds(2 * x + y, 1)], outs[a].at[pl.ds(len(rels), 1)],
                                          local_sems.at[a])
            local.wait() if wait else local.start()
        for i, j in enumerate(rels):
            to = _peer(x, y, c, CHIPS[j])[0]
            cp = pltpu.make_async_remote_copy(
                src_ref=ins[a].at[pl.ds(2 * to[0] + to[1], 1)], dst_ref=outs[a].at[pl.ds(i, 1)],
                send_sem=send_sems.at[a, i], recv_sem=recv_sems.at[a, i], device_id=to, device_id_type=MESH)
            cp.wait() if wait else cp.start()


def _chip_exchange_comm(sums, rels=(0, 1, 2), own=True):
    def start(ins_, outs_, sems_):
        _chip_copies(ins_, outs_, sems_, False, rels, own)

    def finish(ins_, outs_, sems_):
        _chip_copies(ins_, outs_, sems_, True, rels, own)

    n = len(sums)
    sems = [pltpu.SemaphoreType.DMA((n, len(rels))), pltpu.SemaphoreType.DMA((n, len(rels))),
            pltpu.SemaphoreType.DMA((n,))]
    return _Comm(list(sums), [_sds((len(rels) + own,) + s.shape[1:], s.dtype) for s in sums], sems, start, finish)


def _pair_reduce(name, grads, after=()):
    n = len(grads)
    n_chips = N_DEV // 2

    def body(*refs):
        g_refs, gh_refs, refs = refs[:n], refs[n:2 * n], refs[2 * n + len(after):]
        p_refs, land = refs[:n], refs[n:2 * n]
        send_sems, recv_sems = refs[2 * n:]
        x, y, c, _ = _my_place()
        sib = _peer(x, y, c, 1)[0]
        q = pl.program_id(0)

        def to_sibling(a, j):
            return pltpu.make_async_remote_copy(
                src_ref=gh_refs[a].at[j, pl.ds(1 - c, 1)], dst_ref=land[a].at[pl.ds(j, 1)],
                send_sem=send_sems.at[a, j], recv_sem=recv_sems.at[a, j], device_id=sib, device_id_type=MESH)

        @pl.when(q == 0)
        def _():
            for j in range(n_chips):
                for a in range(n):
                    to_sibling(a, j).start()

        for a in range(n):
            to_sibling(a, q).wait_recv()
            p_refs[a][...] = (g_refs[a][0, pl.ds(c, 1)].astype(F32) + land[a][pl.ds(q, 1)].astype(F32)).astype(BF16)

        @pl.when(q == n_chips - 1)
        def _():
            for a in range(n):
                for j in range(n_chips):
                    to_sibling(a, j).wait_send()

    views = [g.reshape((n_chips, 2) + g.shape[1:]) for g in grads]
    res = pl.pallas_call(
        body, name=name, grid=(n_chips,),
        in_specs=[pl.BlockSpec((1, 2) + g.shape[1:], lambda q: (q, 0, 0, 0)) for g in grads]
        + [ANY] * (n + len(after)),
        out_specs=[pl.BlockSpec((1,) + g.shape[1:], lambda q: (q, 0, 0)) for g in grads],
        out_shape=[_sds((n_chips,) + g.shape[1:], BF16) for g in grads],
        scratch_shapes=[pltpu.VMEM((n_chips,) + g.shape[1:], BF16) for g in grads]
        + [pltpu.SemaphoreType.DMA((n, n_chips)), pltpu.SemaphoreType.DMA((n, n_chips))],
        compiler_params=pltpu.CompilerParams(dimension_semantics=("arbitrary",), vmem_limit_bytes=VMEM_LIMIT),
    )(*views, *views, *after)
    return list(res)


def _pair_exchange_comm(grads):
    n, n_chips = len(grads), N_DEV // 2

    def copies(ins, outs, sems, wait):
        x, y, c, _ = _my_place()
        for j in range(n_chips):
            for a in range(n):
                cp = pltpu.make_async_remote_copy(
                    src_ref=ins[a].at[j, pl.ds(1 - c, 1)], dst_ref=outs[a].at[pl.ds(j, 1)], send_sem=sems[0].at[a, j],
                    recv_sem=sems[1].at[a, j], device_id=_peer(x, y, c, 1)[0], device_id_type=MESH)
                cp.wait() if wait else cp.start()

    views = [g.reshape((n_chips, 2) + g.shape[1:]) for g in grads]
    sems = [pltpu.SemaphoreType.DMA((n, n_chips)), pltpu.SemaphoreType.DMA((n, n_chips))]
    return _Comm(views, [_sds((n_chips,) + g.shape[1:], g.dtype) for g in grads], sems,
                 lambda i, o, s: copies(i, o, s, False), lambda i, o, s: copies(i, o, s, True))


def _pair_add(name, grads, landed, after=()):
    n, n_chips = len(grads), N_DEV // 2

    def body(*refs):
        g_refs, l_refs, p_refs = refs[:n], refs[n:2 * n], refs[2 * n:]
        c = lax.axis_index("c")
        for a in range(n):
            p_refs[a][...] = (g_refs[a][0, pl.ds(c, 1)].astype(F32) + l_refs[a][...].astype(F32)).astype(BF16)

    views = [g.reshape((n_chips, 2) + g.shape[1:]) for g in grads]
    blocks = [pl.BlockSpec((1,) + g.shape[1:], lambda q: (q, 0, 0)) for g in grads]
    return _call(name, body, (n_chips,),
                 [pl.BlockSpec((1, 2) + g.shape[1:], lambda q: (q, 0, 0, 0)) for g in grads] + blocks, blocks,
                 [_sds((n_chips,) + g.shape[1:], BF16) for g in grads], (*views, *landed), after=after)[0]


def _call(name, body, grid, in_specs, out_specs, out_shape, args, scratch=(), comm=None, after=()):
    params = pltpu.CompilerParams(dimension_semantics=("arbitrary",) * len(grid), vmem_limit_bytes=VMEM_LIMIT)
    if comm is None:
        ni, na = len(in_specs), len(after)

        def ordered(*refs):
            body(*refs[:ni], *refs[ni + na:])

        outs = pl.pallas_call(ordered if after else body, name=name, grid=grid, in_specs=list(in_specs) + [ANY] * na,
                              out_specs=list(out_specs), out_shape=list(out_shape), scratch_shapes=list(scratch),
                              compiler_params=params)(*args, *after)
        return list(outs), []
    ni, nci, no, nco, ns = len(in_specs), len(comm.args), len(out_specs), len(comm.out_shape), len(scratch)

    def carrying(*refs):
        ins, refs = refs[:ni], refs[ni:]
        cin, refs = refs[:nci], refs[nci:]
        outs, refs = refs[:no], refs[no:]
        cout, refs = refs[:nco], refs[nco:]
        scr, csems = refs[:ns], refs[ns:]
        step = pl.program_id(0)
        for ax in range(1, len(grid)):
            step = step * grid[ax] + pl.program_id(ax)
        steps = math.prod(grid)

        @pl.when(step == 0)
        def _():
            comm.start(cin, cout, csems)

        if comm.relay is not None:
            @pl.when(step == (2 * steps) // 3)
            def _():
                comm.relay(cin, cout, csems)

        body(*ins, *outs, *scr)

        @pl.when(step == steps - 1)
        def _():
            comm.finish(cin, cout, csems)

    outs = pl.pallas_call(
        carrying, name=name, grid=grid, in_specs=list(in_specs) + [ANY] * nci, out_specs=list(out_specs) + [ANY] * nco,
        out_shape=list(out_shape) + list(comm.out_shape), scratch_shapes=list(scratch) + list(comm.scratch),
        compiler_params=params)(*args, *comm.args)
    return list(outs[:no]), list(outs[no:])


def _comm_only(name, comm):
    def body(*refs):
        nci, nco = len(comm.args), len(comm.out_shape)
        cin, cout, csems = refs[:nci], refs[nci:nci + nco], refs[nci + nco:]
        comm.start(cin, cout, csems)
        if comm.relay is not None:
            comm.relay(cin, cout, csems)
        comm.finish(cin, cout, csems)

    return pl.pallas_call(body, name=name, in_specs=[ANY] * len(comm.args), out_specs=[ANY] * len(comm.out_shape),
                          out_shape=list(comm.out_shape), scratch_shapes=list(comm.scratch))(*comm.args)


SIBLING_AND_NEIGHBOURS, OTHER_CHIPS, EVERYONE = (1, 4, 2), CHIPS, tuple(range(1, N_DEV))


def _by_sequencer(name, comm, peers, collective_id):
    src = [jax.new_ref(a, memory_space=pltpu.MemorySpace.HBM) for a in comm.args]
    dst = [jax.empty_ref(s, memory_space=pltpu.MemorySpace.HBM) for s in comm.out_shape]

    @pl.kernel(mesh=plsc.ScalarSubcoreMesh(axis_name="sequencer", num_cores=1), name=name,
               scratch_types=tuple(comm.scratch), compiler_params=pltpu.CompilerParams(collective_id=collective_id))
    def launch(*sems):
        x, y, c, _ = _my_place()
        barrier = pltpu.get_barrier_semaphore()
        for k in peers:
            pl.semaphore_signal(barrier, inc=1, device_id=_peer(x, y, c, k)[0], device_id_type=MESH)
        pl.semaphore_wait(barrier, len(peers))
        comm.start(src, dst, sems)
        if comm.relay is not None:
            comm.relay(src, dst, sems)
        comm.finish(src, dst, sems)

    launch()
    return [d[...] for d in dst]


def _gather_first(first, later):
    nf, nl = len(first), len(later)
    dts = [BF16] * (nf - 2) + [F32, F32]

    def body(*refs):
        ins, refs = refs[:nf + nl], refs[nf + nl:]
        outs, refs = refs[:nf], refs[nf:]
        casts, refs = refs[:nl], refs[nl:]
        stage, sems = refs[:nf], refs[nf:]
        for a in range(nf):
            stage[a][...] = ins[a][...].astype(dts[a])
            _gather_start(stage, outs, sems, only=a)
        for a in range(nl):
            casts[a][...] = ins[nf + a][...].astype(BF16)
        _gather_relay(stage, outs, sems)
        _gather_finish(stage, outs, sems)

    res = pl.pallas_call(
        body, name="gather_first",
        in_specs=[VMEM] * (nf + nl), out_specs=[ANY] * nf + [VMEM] * nl,
        out_shape=[_sds((N_DEV,) + s.shape[1:], dt) for s, dt in zip(first, dts)]
        + [_sds(s.shape, BF16) for s in later],
        scratch_shapes=[pltpu.VMEM(s.shape, dt) for s, dt in zip(first, dts)] + _relay_sems(nf),
        compiler_params=pltpu.CompilerParams(vmem_limit_bytes=VMEM_LIMIT),
    )(*first, *later)
    return list(res[:nf]), list(res[nf:])


def _a_mix_fwd(x, g, w_in, ln_g, ln_b, w_s, b_st, w_out, comm=None):
    t = x.shape[0]
    nblk = TM // GMLP_BLOCK

    def body(x_ref, g_ref, win_ref, lng_ref, lnb_ref, ws_ref, bst_ref, wout_ref, h_ref, z_ref, gated_scr):
        xv = x_ref[...]
        hb = _rms_fwd(xv, g_ref[...])[0].astype(BF16)
        for d in range(N_DEV):
            z_ref[:, d * FF_SLOT:(d + 1) * FF_SLOT] = _dot(hb, win_ref[d])
        u = _gelu(z_ref[:, :GATE_DIM])
        vb = _ln_fwd(_gelu(z_ref[:, GATE_DIM:]), lng_ref[...], lnb_ref[...])[0].astype(BF16)
        mask = _gate_mask()
        for gi in range(A_GROUPS):
            wm = jnp.where(mask, ws_ref[gi], 0.0).astype(BF16)
            bias = bst_ref[:, gi:gi + 1]
            cs = slice(gi * A_GROUP_DIM, (gi + 1) * A_GROUP_DIM)
            for n in range(nblk):
                rs = slice(n * GMLP_BLOCK, (n + 1) * GMLP_BLOCK)
                sv = _dot(wm, vb[rs, cs]) + bias
                gated_scr[rs, cs] = (u[rs, cs] * sv).astype(BF16)
        h_ref[...] = xv + _dot(gated_scr[...], wout_ref[...])

    return _call(
        "a_mix_fwd", body, (t // TM,),
        [_row(D_MODEL), _res((1, D_MODEL)), _res((N_DEV, D_MODEL, FF_SLOT)), _res((1, GATE_DIM)),
         _res((1, GATE_DIM)), _res((A_GROUPS, GMLP_BLOCK, GMLP_BLOCK)), _res((GMLP_BLOCK, A_GROUPS)),
         _res((GATE_DIM, D_MODEL))],
        [_row(D_MODEL), _row(2 * GATE_DIM), _row(GATE_DIM)],
        [_sds((t, D_MODEL), F32), _sds((t, 2 * GATE_DIM), F32), _sds((t, GATE_DIM), BF16)],
        (x, g, w_in, ln_g, ln_b, w_s, b_st, w_out), comm=comm)


MLP_W_SPECS = (_res((N_DEV, D_MODEL, FF_SLOT)), _res((N_DEV, FF_SLOT, D_MODEL)))


def _mlp_fwd(h, g, w1, w2, layer, comm=None):
    t = h.shape[0]

    def body(h_ref, g_ref, w1_ref, w2_ref, o_ref, a_ref):
        hv = h_ref[...]
        hb = _rms_fwd(hv, g_ref[...])[0].astype(BF16)
        o_ref[...] = hv
        for d in range(N_DEV):
            a = _dot(hb, w1_ref[d])
            a_ref[:, d * FF_SLOT:(d + 1) * FF_SLOT] = a
            r = jnp.maximum(a, 0.0)
            o_ref[...] += _dot((r * r).astype(BF16), w2_ref[d])

    return _call(
        f"mlp_fwd_{layer}", body, (t // TM,), [_row(D_MODEL), _res((1, D_MODEL)), *MLP_W_SPECS],
        [_row(D_MODEL), _row(D_FF)], [_sds((t, D_MODEL), F32), _sds((t, D_FF), F32)], (h, g, w1, w2), comm=comm)


def _mlp_fwd_loss(h, g, w1, w2, final_g, target):
    t = h.shape[0]

    def body(h_ref, g_ref, w1_ref, w2_ref, fg_ref, t_ref, a_ref, loss_ref, dh_ref, dg_ref):
        hv = h_ref[...]
        hb = _rms_fwd(hv, g_ref[...])[0].astype(BF16)
        out = hv
        for d in range(N_DEV):
            a = _dot(hb, w1_ref[d])
            a_ref[:, d * FF_SLOT:(d + 1) * FF_SLOT] = a
            r = jnp.maximum(a, 0.0)
            out = out + _dot((r * r).astype(BF16), w2_ref[d])
        y, xhat, rstd = _rms_fwd(out, fg_ref[...])
        err = y - t_ref[...]
        part = 0.5 * jnp.sum(jnp.mean(err * err, axis=-1, keepdims=True), axis=0, keepdims=True)
        dx, dg = _rms_bwd(err * (1.0 / D_MODEL), xhat, rstd, fg_ref[...])
        dh_ref[...] = dx
        _acc(dg_ref, dg)
        _acc(loss_ref, part)

    return _call(
        "mlp_fwd_loss", body, (t // TM,),
        [_row(D_MODEL), _res((1, D_MODEL)), *MLP_W_SPECS, _res((1, D_MODEL)), _row(D_MODEL)],
        [_row(D_FF), _const((1, 1)), _row(D_MODEL), _const((1, D_MODEL))],
        [_sds((t, D_FF), F32), _sds((1, 1), F32), _sds((t, D_MODEL), F32), _sds((1, D_MODEL), F32)],
        (h, g, w1, w2, final_g, target))[0]


KVQ_W_SPECS = (_res((1, D_MODEL)), _res((D_MODEL, KV_LORA + QK_ROPE)), _res((1, KV_LORA)),
               _res((B_HEADS, KV_LORA, QK_NOPE + V_HEAD)), _res((1, D_MODEL)), _res((D_MODEL, Q_LORA)),
               _res((1, Q_LORA)), _res((B_HEADS, Q_LORA, QK_NOPE + QK_ROPE)))


def _kvq_fwd(h, pos, inv_freq, kvq_w):
    t = h.shape[0]
    half = QK_ROPE // 2

    def body(h_ref, pos_ref, invf_ref, srcg_ref, wkva_ref, kvag_ref, wkvb_ref, mixg_ref, wqa_ref, qg_ref, wqb_ref,
             ckv_ref, k_ref, v_ref, cqpre_ref, q_ref, cos_ref, sin_ref):
        hv = h_ref[...]
        xhat = hv * lax.rsqrt(jnp.mean(hv * hv, axis=-1, keepdims=True) + EPS)
        ang = pos_ref[...].astype(F32) * invf_ref[...]
        cos, sin = jnp.cos(ang), jnp.sin(ang)
        cos_ref[...] = cos
        sin_ref[...] = sin
        ckv = _dot((xhat * srcg_ref[...]).astype(BF16), wkva_ref[...])
        ckv_ref[...] = ckv
        cb = _rms_fwd(ckv[:, :KV_LORA], kvag_ref[...])[0].astype(BF16)
        kpe = _rope(ckv[:, KV_LORA:], cos, sin).astype(BF16)
        for hd in range(B_HEADS):
            kv = _dot(cb, wkvb_ref[hd])
            k_ref[hd, :, 0:QK_NOPE] = kv[:, :QK_NOPE].astype(BF16)
            k_ref[hd, :, QK_NOPE:] = kpe
            v_ref[hd] = kv[:, QK_NOPE:].astype(BF16)
        cqpre = _dot((xhat * mixg_ref[...]).astype(BF16), wqa_ref[...])
        cqpre_ref[...] = cqpre
        cqb = _rms_fwd(cqpre, qg_ref[...])[0].astype(BF16)
        for hd in range(B_HEADS):
            q = _dot(cqb, wqb_ref[hd])
            q_ref[hd, :, 0:QK_NOPE] = q[:, :QK_NOPE].astype(BF16)
            q_ref[hd, :, QK_NOPE:] = _rope(q[:, QK_NOPE:], cos, sin).astype(BF16)

    return _call(
        "kvq_fwd", body, (t // TM,), [_row(D_MODEL), _row(1), _res((1, half)), *KVQ_W_SPECS],
        [_row(KV_LORA + QK_ROPE), _heads(QK_NOPE + QK_ROPE), _heads(V_HEAD), _row(Q_LORA),
         _heads(QK_NOPE + QK_ROPE), _row(half), _row(half)],
        [_sds((t, KV_LORA + QK_ROPE), F32), _sds((B_HEADS, t, QK_NOPE + QK_ROPE), BF16),
         _sds((B_HEADS, t, V_HEAD), BF16), _sds((t, Q_LORA), F32), _sds((B_HEADS, t, QK_NOPE + QK_ROPE), BF16),
         _sds((t, half), F32), _sds((t, half), F32)],
        (h, pos, inv_freq, *kvq_w))[0]


def _softmax_rows(q, k_ref, k):
    past, upto = k * TM, (k + 1) * TM
    s = _dot_nt(q, k_ref[0:upto, :])
    own = jnp.where(_att_mask(0, TM, TM), s[:, past:], jnp.finfo(F32).min)
    s = own if k == 0 else jnp.concatenate([s[:, :past], own], axis=1)
    e = jnp.exp2((s - jnp.max(s, axis=-1, keepdims=True)) * (ATT_SCALE * LOG2_E))
    return e * (1.0 / jnp.sum(e, axis=-1, keepdims=True))


def _for_my_tile(i, nq, fn):
    for k in range(nq):
        @pl.when(i == k)
        def _(k=k):
            fn(k)


def _attn_fwd(h, q, k, v, w_o, comm=None):
    t = h.shape[0]
    nq, hps = t // TM, HEADS_PER_STEP

    def body(h_ref, q_ref, k_ref, v_ref, wo_ref, o_ref, att_ref):
        i, pair = pl.program_id(0), pl.program_id(1)

        @pl.when(pair == 0)
        def _():
            o_ref[...] = h_ref[...]

        def tile(kt):
            proj = None
            for j in range(hps):
                hd = pair * hps + j
                p = _softmax_rows(q_ref[j], k_ref.at[hd], kt)
                ob = _dot(p.astype(BF16), v_ref[hd, 0:(kt + 1) * TM, :]).astype(BF16)
                att_ref[j] = ob
                proj = _dot(ob, wo_ref[hd]) if proj is None else proj + _dot(ob, wo_ref[hd])
            o_ref[...] += proj

        _for_my_tile(i, nq, tile)

    def per_head(d):
        return pl.BlockSpec((hps, TM, d), lambda i, pair: (pair, i, 0))

    def resident(shape):
        zeros = (0,) * len(shape)
        return pl.BlockSpec(shape, lambda i, pair: zeros, pipeline_mode=pl.Buffered(1))

    tile_spec = pl.BlockSpec((TM, D_MODEL), lambda i, pair: (i, 0))
    return _call(
        "attn_fwd", body, (nq, B_HEADS // hps),
        [tile_spec, per_head(QK_NOPE + QK_ROPE), resident((B_HEADS, t, QK_NOPE + QK_ROPE)),
         resident((B_HEADS, t, V_HEAD)), resident((B_HEADS, V_HEAD, D_MODEL))],
        [tile_spec, per_head(V_HEAD)], [_sds((t, D_MODEL), F32), _sds((B_HEADS, t, V_HEAD), BF16)],
        (h, q, k, v, w_o), comm=comm)


def _mlp_bwd(h, a, dho, g, w1, w2, layer, after=()):
    t = h.shape[0]

    def body(h_ref, a_ref, dho_ref, g_ref, w1_ref, w2_ref, dhi_ref, dg_ref, hn_ref, f_ref, da_ref, dhib_ref):
        gv = g_ref[...]
        y, xhat, rstd = _rms_fwd(h_ref[...], gv)
        hn_ref[...] = y.astype(BF16)
        dho_v = dho_ref[...]
        dhob = dho_v.astype(BF16)
        dhn = jnp.zeros((TM, D_MODEL), F32)
        for d in range(N_DEV):
            cs = slice(d * FF_SLOT, (d + 1) * FF_SLOT)
            r = jnp.maximum(a_ref[:, cs], 0.0)
            f_ref[:, cs] = (r * r).astype(BF16)
            da = (_dot_nt(dhob, w2_ref[d]) * (2.0 * r)).astype(BF16)
            da_ref[:, cs] = da
            dhn = dhn + _dot_nt(da, w1_ref[d])
        dx, dg = _rms_bwd(dhn, xhat, rstd, gv)
        dhi = dho_v + dx
        dhi_ref[...] = dhi
        dhib_ref[...] = dhi.astype(BF16)
        _acc(dg_ref, dg)

    return _call(
        f"mlp_bwd_{layer}", body, (t // TM,),
        [_row(D_MODEL), _row(D_FF), _row(D_MODEL), _res((1, D_MODEL)), *MLP_W_SPECS],
        [_row(D_MODEL), _const((1, D_MODEL)), _row(D_MODEL), _row(D_FF), _row(D_FF), _row(D_MODEL)],
        [_sds((t, D_MODEL), F32), _sds((1, D_MODEL), F32), _sds((t, D_MODEL), BF16), _sds((t, D_FF), BF16),
         _sds((t, D_FF), BF16), _sds((t, D_MODEL), BF16)],
        (h, a, dho, g, w1, w2), after=after)


def _attn_bwd(dh, q, k, v, w_o, cos, sin, after=()):
    t = dh.shape[0]
    half, hps = QK_ROPE // 2, HEADS_PER_STEP

    def body(dh_ref, q_ref, k_ref, v_ref, wo_ref, cos_ref, sin_ref, dq_ref, dk_ref, dv_ref):
        i = pl.program_id(1)

        @pl.when(i == 0)
        def _():
            dk_ref[...] = jnp.zeros_like(dk_ref)
            dv_ref[...] = jnp.zeros_like(dv_ref)

        def tile(kt):
            keys = slice(0, (kt + 1) * TM)
            for j in range(hps):
                qj = q_ref[j]
                do = _dot_nt(dh_ref[kt * TM:(kt + 1) * TM, :], wo_ref[j]).astype(BF16)
                p = _softmax_rows(qj, k_ref.at[j], kt)
                dp = _dot_nt(do, v_ref[j, keys, :])
                ds = (p * (dp - jnp.sum(p * dp, axis=-1, keepdims=True)) * ATT_SCALE).astype(BF16)
                dq = _dot(ds, k_ref[j, keys, :])
                dq_ref[j, :, 0:QK_NOPE] = dq[:, :QK_NOPE].astype(BF16)
                dq_ref[j, :, QK_NOPE:] = _rope(dq[:, QK_NOPE:], cos_ref[...], -sin_ref[...]).astype(BF16)
                dk_ref[j, keys, :] += _dot_tn(ds, qj)
                dv_ref[j, keys, :] += _dot_tn(p.astype(BF16), do)

        _for_my_tile(i, t // TM, tile)

    def per_pair(rows, d, tiled):
        return pl.BlockSpec((hps, rows, d), (lambda pair, i: (pair, i, 0)) if tiled else (lambda pair, i: (pair, 0, 0)))

    def tile(d):
        return pl.BlockSpec((TM, d), lambda pair, i: (i, 0))

    return _call(
        "attn_bwd", body, (B_HEADS // hps, t // TM),
        [pl.BlockSpec((t, D_MODEL), lambda pair, i: (0, 0), pipeline_mode=pl.Buffered(1)),
         per_pair(TM, QK_NOPE + QK_ROPE, True), per_pair(t, QK_NOPE + QK_ROPE, False), per_pair(t, V_HEAD, False),
         per_pair(V_HEAD, D_MODEL, False), tile(half), tile(half)],
        [per_pair(TM, QK_NOPE + QK_ROPE, True), per_pair(t, QK_NOPE + QK_ROPE, False), per_pair(t, V_HEAD, False)],
        [_sds((B_HEADS, t, QK_NOPE + QK_ROPE), BF16), _sds((B_HEADS, t, QK_NOPE + QK_ROPE), F32),
         _sds((B_HEADS, t, V_HEAD), F32)],
        (dh, q, k, v, w_o, cos, sin), after=after)


def _kvq_bwd(h, dh, ckv, cqpre, dq, dk, dv, cos, sin, kvq_w):
    t = h.shape[0]
    half = QK_ROPE // 2

    def body(h_ref, dh_ref, ckv_ref, cqpre_ref, dq_ref, dk_ref, dv_ref, cos_ref, sin_ref,
             srcg_ref, wkva_ref, kvag_ref, wkvb_ref, mixg_ref, wqa_ref, qg_ref, wqb_ref,
             dhi_ref, hq_ref, hk_ref, cq_ref, dcqpre_ref, c_ref, dkv_ref, dckv_ref,
             dmixg_ref, dsrcg_ref, dqg_ref, dkvag_ref):
        hv = h_ref[...]
        rstd = lax.rsqrt(jnp.mean(hv * hv, axis=-1, keepdims=True) + EPS)
        xhat = hv * rstd
        mixg, srcg, qg, kvag = mixg_ref[...], srcg_ref[...], qg_ref[...], kvag_ref[...]
        hq_ref[...] = (xhat * mixg).astype(BF16)
        hk_ref[...] = (xhat * srcg).astype(BF16)
        cq, cqhat, crstd = _rms_fwd(cqpre_ref[...], qg)
        cq_ref[...] = cq.astype(BF16)
        dcq = jnp.zeros((TM, Q_LORA), F32)
        for hd in range(B_HEADS):
            dcq = dcq + _dot_nt(dq_ref[hd], wqb_ref[hd])
        dcqpre, dqg = _rms_bwd(dcq, cqhat, crstd, qg)
        dcqpre_b = dcqpre.astype(BF16)
        dcqpre_ref[...] = dcqpre_b
        dxq, dmixg = _rms_bwd(_dot_nt(dcqpre_b, wqa_ref[...]), xhat, rstd, mixg)
        ckv = ckv_ref[...]
        c, chat, krstd = _rms_fwd(ckv[:, :KV_LORA], kvag)
        c_ref[...] = c.astype(BF16)
        dc = jnp.zeros((TM, KV_LORA), F32)
        dkpe = jnp.zeros((TM, QK_ROPE), F32)
        for hd in range(B_HEADS):
            dkv = jnp.concatenate([dk_ref[hd, :, 0:QK_NOPE], dv_ref[hd]], axis=-1).astype(BF16)
            dkv_ref[hd] = dkv
            dc = dc + _dot_nt(dkv, wkvb_ref[hd])
            dkpe = dkpe + dk_ref[hd, :, QK_NOPE:]
        dlat, dkvag = _rms_bwd(dc, chat, krstd, kvag)
        dpe = _rope(dkpe, cos_ref[...], -sin_ref[...])
        dckv_b = jnp.concatenate([dlat, dpe], axis=-1).astype(BF16)
        dckv_ref[...] = dckv_b
        dxk, dsrcg = _rms_bwd(_dot_nt(dckv_b, wkva_ref[...]), xhat, rstd, srcg)
        dhi_ref[...] = dh_ref[...] + dxq + dxk
        _acc(dmixg_ref, dmixg)
        _acc(dsrcg_ref, dsrcg)
        _acc(dqg_ref, dqg)
        _acc(dkvag_ref, dkvag)

    return _call(
        "kvq_bwd", body, (t // TM,),
        [_row(D_MODEL), _row(D_MODEL), _row(KV_LORA + QK_ROPE), _row(Q_LORA), _heads(QK_NOPE + QK_ROPE),
         _heads(QK_NOPE + QK_ROPE), _heads(V_HEAD), _row(half), _row(half), *KVQ_W_SPECS],
        [_row(D_MODEL), _row(D_MODEL), _row(D_MODEL), _row(Q_LORA), _row(Q_LORA), _row(KV_LORA),
         _heads(QK_NOPE + V_HEAD), _row(KV_LORA + QK_ROPE),
         _const((1, D_MODEL)), _const((1, D_MODEL)), _const((1, Q_LORA)), _const((1, KV_LORA))],
        [_sds((t, D_MODEL), F32), _sds((t, D_MODEL), BF16), _sds((t, D_MODEL), BF16), _sds((t, Q_LORA), BF16),
         _sds((t, Q_LORA), BF16), _sds((t, KV_LORA), BF16), _sds((B_HEADS, t, QK_NOPE + V_HEAD), BF16),
         _sds((t, KV_LORA + QK_ROPE), BF16),
         _sds((1, D_MODEL), F32), _sds((1, D_MODEL), F32), _sds((1, Q_LORA), F32), _sds((1, KV_LORA), F32)],
        (h, dh, ckv, cqpre, dq, dk, dv, cos, sin, *kvq_w))[0]


def _a_mix_bwd(x, z, dh, g, w_in, ln_g, ln_b, w_s, b_st, w_out, after=()):
    t = x.shape[0]
    tm = TM_GATE
    nblk = tm // GMLP_BLOCK

    def body(x_ref, z_ref, dh_ref, g_ref, win_ref, lng_ref, lnb_ref, ws_ref, bst_ref, wout_ref,
             dx_ref, hn_ref, dz_ref, dg_ref, dlng_ref, dlnb_ref, dws_ref, dbs_ref, dvn_scr, gelu_grad_v):
        @pl.when(pl.program_id(0) == 0)
        def _():
            dws_ref[...] = jnp.zeros_like(dws_ref)
            dbs_ref[...] = jnp.zeros_like(dbs_ref)

        gv, lng = g_ref[...], lng_ref[...]
        y, xhat, rstd = _rms_fwd(x_ref[...], gv)
        hn_ref[...] = y.astype(BF16)
        dhv = dh_ref[...]
        dgated = _dot_nt(dhv.astype(BF16), wout_ref[...])
        u, gelu_grad_u = _gelu_and_grad(z_ref[:, :GATE_DIM])
        v, gelu_grad_v[...] = _gelu_and_grad(z_ref[:, GATE_DIM:])
        vn, vhat, lrstd = _ln_fwd(v, lng, lnb_ref[...])
        vb = vn.astype(BF16)
        mask = _gate_mask()
        for gi in range(A_GROUPS):
            wm = jnp.where(mask, ws_ref[gi], 0.0).astype(BF16)
            bias = bst_ref[:, gi:gi + 1]
            cs = slice(gi * A_GROUP_DIM, (gi + 1) * A_GROUP_DIM)
            dws = jnp.zeros((GMLP_BLOCK, GMLP_BLOCK), F32)
            dbs = jnp.zeros((GMLP_BLOCK, 1), F32)
            for n in range(nblk):
                rs = slice(n * GMLP_BLOCK, (n + 1) * GMLP_BLOCK)
                sv = _dot(wm, vb[rs, cs]) + bias
                dz_ref[rs, cs] = (dgated[rs, cs] * sv * gelu_grad_u[rs, cs]).astype(BF16)
                dsv = dgated[rs, cs] * u[rs, cs]
                dsvb = dsv.astype(BF16)
                dws = dws + _dot_nt(dsvb, vb[rs, cs])
                dbs = dbs + jnp.sum(dsv, axis=-1, keepdims=True)
                dvn_scr[rs, cs] = _dot_tn(wm, dsvb)
            dws_ref[gi] += jnp.where(mask, dws, 0.0)
            dbs_ref[gi] += dbs
        dvn = dvn_scr[...]
        dvhat = dvn * lng
        dv = lrstd * (dvhat - jnp.mean(dvhat, axis=-1, keepdims=True)
                      - vhat * jnp.mean(dvhat * vhat, axis=-1, keepdims=True))
        dz_ref[:, GATE_DIM:] = (dv * gelu_grad_v[...]).astype(BF16)
        dhn = jnp.zeros((tm, D_MODEL), F32)
        for d in range(N_DEV):
            dhn = dhn + _dot_nt(dz_ref[:, d * FF_SLOT:(d + 1) * FF_SLOT], win_ref[d])
        dx, dg = _rms_bwd(dhn, xhat, rstd, gv)
        dx_ref[...] = dhv + dx
        _acc(dg_ref, dg)
        _acc(dlng_ref, jnp.sum(dvn * vhat, axis=0, keepdims=True))
        _acc(dlnb_ref, jnp.sum(dvn, axis=0, keepdims=True))

    return _call(
        "a_mix_bwd", body, (t // tm,),
        [_row(D_MODEL, tm), _row(2 * GATE_DIM, tm), _row(D_MODEL, tm), _res((1, D_MODEL)),
         _res((N_DEV, D_MODEL, FF_SLOT)), _res((1, GATE_DIM)), _res((1, GATE_DIM)),
         _res((A_GROUPS, GMLP_BLOCK, GMLP_BLOCK)), _res((GMLP_BLOCK, A_GROUPS)), _res((GATE_DIM, D_MODEL))],
        [_row(D_MODEL, tm), _row(D_MODEL, tm), _row(2 * GATE_DIM, tm),
         _const((1, D_MODEL)), _const((1, GATE_DIM)), _const((1, GATE_DIM)),
         _const((A_GROUPS, GMLP_BLOCK, GMLP_BLOCK)), _const((A_GROUPS, GMLP_BLOCK, 1))],
        [_sds((t, D_MODEL), F32), _sds((t, D_MODEL), BF16),
         _sds((t, 2 * GATE_DIM), BF16), _sds((1, D_MODEL), F32), _sds((1, GATE_DIM), F32),
         _sds((1, GATE_DIM), F32), _sds((A_GROUPS, GMLP_BLOCK, GMLP_BLOCK), F32),
         _sds((A_GROUPS, GMLP_BLOCK, 1), F32)],
        (x, z, dh, g, w_in, ln_g, ln_b, w_s, b_st, w_out),
        scratch=[pltpu.VMEM((tm, GATE_DIM), F32), pltpu.VMEM((tm, GATE_DIM), F32)], after=after)


def _wgrad(name, a, b, a_spec, b_spec, m, n, comm=None):
    def body(a_ref, b_ref, o_ref):
        o_ref[0] = _dot_tn(a_ref[...].astype(BF16), b_ref[...].astype(BF16)).astype(BF16)

    outs, got = _call(name, body, (N_DEV,), [a_spec, b_spec], [pl.BlockSpec((1, m, n), lambda d: (d, 0, 0))],
                      [_sds((N_DEV, m, n), BF16)], (a, b), comm=comm)
    return outs[0] if comm is None else (outs[0], got)


def _full(t, d):
    return pl.BlockSpec((t, d), lambda i: (0, 0), pipeline_mode=pl.Buffered(1))


def _cols(t, d):
    return pl.BlockSpec((t, d), lambda i: (0, i))


def _head(t, d):
    return pl.BlockSpec((None, t, d), lambda i: (i, 0, 0))


def _local_step(x, pos, target, inv_freq, wg, sm, shards=None):
    t = x.shape[0]
    wg = dict(wg)
    dist = shards is not None
    mix_g = [sm["norm_mix_g"][l:l + 1] for l in range(2)]
    mlp_g = [sm["norm_mlp_g"][l:l + 1] for l in range(2)]

    ids = iter(range(2, 2 + 9))

    def gather(names):
        if dist:
            got = _by_sequencer("gather_" + names[0], _gather_comm([shards[k] for k in names]),
                                SIBLING_AND_NEIGHBOURS, next(ids))
            wg.update(zip(names, got))

    def send(name, names):
        if dist:
            comm = _exchange_comm(grads=[g[k] for k in names])
            g.update(zip(names, _by_sequencer("exchange_" + name, comm, EVERYONE, next(ids))))

    def send_sums(name, names, meanwhile):
        if not dist:
            meanwhile()
            return ()
        grads = [g[k] for k in names]
        landed = _by_sequencer("pair_exchange_" + name, _pair_exchange_comm(grads), (1,), next(ids))
        sums = _pair_add("pair_add_" + name, grads, landed, after=meanwhile())
        g.update(zip(names, _by_sequencer("exchange_" + name, _chip_exchange_comm(sums), OTHER_CHIPS, next(ids))))
        return sums

    def a_args():
        return (wg["a_w_in"], wg["a_ln_v_g"], wg["a_ln_v_b"], sm["a_w_s"], sm["a_b_st"], wg["a_w_out"])

    def kvq_w():
        return (sm["kv_src_norm_g"], wg["kv_w_a"], sm["kv_a_norm_g"], wg["kv_w_b"], mix_g[1], wg["b_w_q_a"],
                sm["b_q_norm_g"], wg["b_w_q_b"])

    gather(("mlp_w1_0", "mlp_w2_0"))
    (h1, z, gated), _ = _a_mix_fwd(x, mix_g[0], *a_args())
    gather(("kv_w_a", "kv_w_b", "b_w_q_a", "b_w_q_b", "b_w_o"))
    (h2, a0), _ = _mlp_fwd(h1, mlp_g[0], wg["mlp_w1_0"], wg["mlp_w2_0"], 0)
    if dist:
        wg["b_w_q_a"] = wg["b_w_q_a"].reshape(D_MODEL, Q_LORA)
        wg["kv_w_a"] = wg["kv_w_a"].reshape(D_MODEL, KV_LORA + QK_ROPE)
    gather(("mlp_w1_1", "mlp_w2_1"))
    ckv, k, v, cqpre, q, cos, sin = _kvq_fwd(h2, pos, inv_freq, kvq_w())
    (h3, att), _ = _attn_fwd(h2, q, k, v, wg["b_w_o"])
    a1, loss, dh4, d_final_g = _mlp_fwd_loss(h3, mlp_g[1], wg["mlp_w1_1"], wg["mlp_w2_1"], sm["final_norm_g"], target)

    g = {}
    (dh3, d_mlp_g1, hn, f, da, dh3_b), _ = _mlp_bwd(h3, a1, dh4, mlp_g[1], wg["mlp_w1_1"], wg["mlp_w2_1"], 1)
    g["mlp_w1_1"] = _wgrad("wgrad_w1_1", hn, da, _full(t, D_MODEL), _cols(t, FF_SLOT), D_MODEL, FF_SLOT)
    g["mlp_w2_1"] = _wgrad("wgrad_w2_1", f, dh4, _cols(t, FF_SLOT), _full(t, D_MODEL), FF_SLOT, D_MODEL)

    def wgrad_w_o():
        g["b_w_o"] = _wgrad("wgrad_w_o", att, dh3_b, _head(t, V_HEAD), _full(t, D_MODEL), V_HEAD, D_MODEL)
        return [g["b_w_o"]]

    sums = send_sums("mlp_1", ("mlp_w1_1", "mlp_w2_1"), wgrad_w_o)
    (dq, dk, dv), _ = _attn_bwd(dh3_b, q, k, v, wg["b_w_o"], cos, sin, after=sums)
    (dh2, hq, hk, cq, dcqpre, c, dkv, dckv, d_mix_g1, d_src_g, d_q_g, d_kv_a_g) = _kvq_bwd(
        h2, dh3, ckv, cqpre, dq, dk, dv, cos, sin, kvq_w())
    g["b_w_q_a"] = _wgrad("wgrad_w_q_a", hq, dcqpre, _cols(t, D_MODEL // N_DEV), _full(t, Q_LORA),
                          D_MODEL // N_DEV, Q_LORA)
    g["b_w_q_b"] = _wgrad("wgrad_w_q_b", cq, dq, _full(t, Q_LORA), _head(t, QK_NOPE + QK_ROPE),
                          Q_LORA, QK_NOPE + QK_ROPE)
    g["kv_w_a"] = _wgrad("wgrad_kv_w_a", hk, dckv, _cols(t, D_MODEL // N_DEV), _full(t, KV_LORA + QK_ROPE),
                         D_MODEL // N_DEV, KV_LORA + QK_ROPE)
    g["kv_w_b"] = _wgrad("wgrad_kv_w_b", c, dkv, _full(t, KV_LORA), _head(t, QK_NOPE + V_HEAD),
                         KV_LORA, QK_NOPE + V_HEAD)
    qkv = ("b_w_o", "b_w_q_a", "b_w_q_b", "kv_w_a", "kv_w_b")
    landed = [g[k] for k in qkv]
    send("qkv", qkv)
    landed = (landed + [g["mlp_w1_1"], g["mlp_w2_1"]]) if dist else ()
    (dh1, d_mlp_g0, hn, f, da, dh1_b), _ = _mlp_bwd(h1, a0, dh2, mlp_g[0], wg["mlp_w1_0"], wg["mlp_w2_0"], 0,
                                                    after=landed)
    g["mlp_w1_0"] = _wgrad("wgrad_w1_0", hn, da, _full(t, D_MODEL), _cols(t, FF_SLOT), D_MODEL, FF_SLOT)
    g["mlp_w2_0"] = _wgrad("wgrad_w2_0", f, dh2, _cols(t, FF_SLOT), _full(t, D_MODEL), FF_SLOT, D_MODEL)

    def wgrad_a_w_out():
        g["a_w_out"] = _wgrad("wgrad_a_w_out", gated, dh1_b, _cols(t, GATE_DIM // N_DEV), _full(t, D_MODEL),
                              GATE_DIM // N_DEV, D_MODEL)
        return [g["a_w_out"]] + [g[k] for k in qkv]

    sums = send_sums("mlp_0", ("mlp_w1_0", "mlp_w2_0"), wgrad_a_w_out)
    (dx, hn, dz, d_mix_g0, d_ln_g, d_ln_b, d_ws, d_bs), _ = _a_mix_bwd(x, z, dh1, mix_g[0], *a_args(), after=sums)
    small = {
        "norm_mix_g": jnp.concatenate([d_mix_g0, d_mix_g1], axis=0),
        "norm_mlp_g": jnp.concatenate([d_mlp_g0, d_mlp_g1], axis=0),
        "a_ln_v_g": d_ln_g.reshape(N_DEV, GATE_DIM // N_DEV),
        "a_ln_v_b": d_ln_b.reshape(N_DEV, GATE_DIM // N_DEV),
        "a_w_s": d_ws.astype(BF16) if dist else d_ws,
        "a_b_s": d_bs.reshape(A_GROUPS, GMLP_BLOCK),
        "b_q_norm_g": d_q_g,
        "kv_src_norm_g": d_src_g,
        "kv_a_norm_g": d_kv_a_g,
        "final_norm_g": d_final_g,
    }
    if dist:
        parts = [small[k].reshape((1,) + small[k].shape) for k in SMALL] + [loss.reshape(1, 1, 1)]
        got = _by_sequencer("gather_small", _exchange_comm(parts=parts), EVERYONE, next(ids))
        small, loss = dict(zip(SMALL, got)), got[-1]
    g["a_w_in"] = _wgrad("wgrad_a_w_in", hn, dz, _full(t, D_MODEL), _cols(t, FF_SLOT), D_MODEL, FF_SLOT)
    return loss, dx, g, small


def _adamw(w, g, m, v):
    m = ADAM_B1 * m + (1.0 - ADAM_B1) * g
    v = ADAM_B2 * v + (1.0 - ADAM_B2) * (g * g)
    m_hat = m / (1.0 - ADAM_B1 ** ADAM_STEP)
    v_hat = v / (1.0 - ADAM_B2 ** ADAM_STEP)
    return -ADAM_LR * (m_hat / (jnp.sqrt(v_hat) + ADAM_EPS) + ADAM_WD * w), m, v


def _sum_in_device_order(r_ref):
    g = r_ref[0].astype(F32)
    for j in range(1, r_ref.shape[0]):
        g = g + r_ref[j].astype(F32)
    return g


def _adamw_sharded(name, recvs, w, m, v, comm=None):
    layers, r, c = w.shape
    tr = math.gcd(r, 512)
    flat = [a for per_layer in recvs for a in per_layer]

    def body(*refs):
        r_refs, (w_ref, m_ref, v_ref) = refs[:len(flat)], refs[len(flat):len(flat) + 3]
        g_ref, d_ref, nm_ref, nv_ref = refs[-4:]
        layer = pl.program_id(0)
        g, pos = None, 0
        for li, per_layer in enumerate(recvs):
            total = None
            for ref in r_refs[pos:pos + len(per_layer)]:
                part = _sum_in_device_order(ref)
                total = part if total is None else total + part
            pos += len(per_layer)
            g = total if g is None else jnp.where(layer == li, total, g)
        g_ref[...] = g
        d_ref[...], nm_ref[...], nv_ref[...] = _adamw(w_ref[...], g, m_ref[...], v_ref[...])

    blk = pl.BlockSpec((None, tr, c), lambda l, i: (l, i, 0))
    return _call(name, body, (layers, r // tr),
                 [pl.BlockSpec((a.shape[0], tr, c), lambda l, i: (0, i, 0)) for a in flat] + [blk] * 3,
                 [blk] * 4, [_sds(w.shape, F32)] * 4, (*flat, w, m, v), comm=comm)


def _adamw_small(recvs, ws, ms, vs, own_row, losses):
    n = len(recvs)

    def body(*refs):
        r_refs, w_refs, m_refs, v_refs = (refs[i * n:(i + 1) * n] for i in range(4))
        outs, scr = refs[4 * n + 1:8 * n + 2], refs[8 * n + 2:]
        outs[-1][...] = _sum_in_device_order(refs[4 * n])
        me = _my_place()[3]
        for a in range(n):
            g = _sum_in_device_order(r_refs[a])
            if own_row[a]:
                scr[0][...] = g
                g = scr[0][pl.ds(me, 1), :]
            g_ref, d_ref, nm_ref, nv_ref = outs[4 * a:4 * a + 4]
            g_ref[...] = g
            d_ref[...], nm_ref[...], nv_ref[...] = _adamw(w_refs[a][...], g, m_refs[a][...], v_refs[a][...])

    out_shape = []
    for w in ws:
        out_shape += [_sds(w.shape, F32)] * 4
    return pl.pallas_call(
        body, name="adamw_small", in_specs=[VMEM] * (4 * n + 1), out_specs=[VMEM] * (4 * n + 1),
        out_shape=out_shape + [_sds((1, 1), F32)], scratch_shapes=[pltpu.VMEM((N_DEV, GATE_DIM // N_DEV), F32)],
    )(*recvs, *ws, *ms, *vs, losses)


BIG = ("a_w_in", "a_w_out", "b_w_q_a", "b_w_q_b", "b_w_o", "kv_w_a", "kv_w_b", "mlp_w1", "mlp_w2")
SMALL = ("norm_mix_g", "norm_mlp_g", "a_ln_v_g", "a_ln_v_b", "a_w_s", "a_b_s", "b_q_norm_g", "kv_src_norm_g",
         "kv_a_norm_g", "final_norm_g")
WEIGHTS = ("norm_mix_g", "norm_mlp_g", "a_w_in", "a_ln_v_g", "a_ln_v_b", "a_w_s", "a_b_s", "a_w_out", "b_w_q_a",
           "b_q_norm_g", "b_w_q_b", "b_w_o", "kv_src_norm_g", "kv_w_a", "kv_a_norm_g", "kv_w_b", "mlp_w1", "mlp_w2",
           "final_norm_g")


def _two_d(name, a):
    if name in ("a_w_s", "a_b_s"):
        return a.reshape(a.shape[1:])
    return a.reshape(1, -1) if a.ndim == 1 else a


def _three_d(a):
    return a if a.ndim == 3 else a.reshape((1,) + a.shape)


def kernel(x, positions, norm_mix_g, norm_mlp_g, a_w_in, a_ln_v_g, a_ln_v_b, a_w_s, a_b_s, a_w_out, b_w_q_a, b_q_norm_g, b_w_q_b, b_w_o, kv_src_norm_g, kv_w_a, kv_a_norm_g, kv_w_b, mlp_w1, mlp_w2, final_norm_g, loss_target, m_norm_mix_g, m_norm_mlp_g, m_a_w_in, m_a_ln_v_g, m_a_ln_v_b, m_a_w_s, m_a_b_s, m_a_w_out, m_b_w_q_a, m_b_q_norm_g, m_b_w_q_b, m_b_w_o, m_kv_src_norm_g, m_kv_w_a, m_kv_a_norm_g, m_kv_w_b, m_mlp_w1, m_mlp_w2, m_final_norm_g, v_norm_mix_g, v_norm_mlp_g, v_a_w_in, v_a_ln_v_g, v_a_ln_v_b, v_a_w_s, v_a_b_s, v_a_w_out, v_b_w_q_a, v_b_q_norm_g, v_b_w_q_b, v_b_w_o, v_kv_src_norm_g, v_kv_w_a, v_kv_a_norm_g, v_kv_w_b, v_mlp_w1, v_mlp_w2, v_final_norm_g):
    w = dict(norm_mix_g=norm_mix_g, norm_mlp_g=norm_mlp_g, a_w_in=a_w_in, a_ln_v_g=a_ln_v_g, a_ln_v_b=a_ln_v_b,
             a_w_s=a_w_s, a_b_s=a_b_s, a_w_out=a_w_out, b_w_q_a=b_w_q_a, b_q_norm_g=b_q_norm_g, b_w_q_b=b_w_q_b,
             b_w_o=b_w_o, kv_src_norm_g=kv_src_norm_g, kv_w_a=kv_w_a, kv_a_norm_g=kv_a_norm_g, kv_w_b=kv_w_b,
             mlp_w1=mlp_w1, mlp_w2=mlp_w2, final_norm_g=final_norm_g)
    m = dict(norm_mix_g=m_norm_mix_g, norm_mlp_g=m_norm_mlp_g, a_w_in=m_a_w_in, a_ln_v_g=m_a_ln_v_g,
             a_ln_v_b=m_a_ln_v_b, a_w_s=m_a_w_s, a_b_s=m_a_b_s, a_w_out=m_a_w_out, b_w_q_a=m_b_w_q_a,
             b_q_norm_g=m_b_q_norm_g, b_w_q_b=m_b_w_q_b, b_w_o=m_b_w_o, kv_src_norm_g=m_kv_src_norm_g,
             kv_w_a=m_kv_w_a, kv_a_norm_g=m_kv_a_norm_g, kv_w_b=m_kv_w_b, mlp_w1=m_mlp_w1, mlp_w2=m_mlp_w2,
             final_norm_g=m_final_norm_g)
    v = dict(norm_mix_g=v_norm_mix_g, norm_mlp_g=v_norm_mlp_g, a_w_in=v_a_w_in, a_ln_v_g=v_a_ln_v_g,
             a_ln_v_b=v_a_ln_v_b, a_w_s=v_a_w_s, a_b_s=v_a_b_s, a_w_out=v_a_w_out, b_w_q_a=v_b_w_q_a,
             b_q_norm_g=v_b_q_norm_g, b_w_q_b=v_b_w_q_b, b_w_o=v_b_w_o, kv_src_norm_g=v_kv_src_norm_g,
             kv_w_a=v_kv_w_a, kv_a_norm_g=v_kv_a_norm_g, kv_w_b=v_kv_w_b, mlp_w1=v_mlp_w1, mlp_w2=v_mlp_w2,
             final_norm_g=v_final_norm_g)
    t = x.shape[1]

    first = ("a_w_in", "a_w_out", "a_ln_v_g", "a_ln_v_b")
    later = ("mlp_w1_0", "mlp_w2_0", "mlp_w1_1", "mlp_w2_1", "kv_w_a", "kv_w_b", "b_w_q_a", "b_w_q_b", "b_w_o")
    blocks = {k: _three_d(w[k]) for k in BIG if not k.startswith("mlp")}
    for k in ("mlp_w1", "mlp_w2"):
        blocks[k + "_0"], blocks[k + "_1"] = w[k][0:1], w[k][1:2]
    got, casts = _gather_first([blocks[k] if k in blocks else w[k] for k in first], [blocks[k] for k in later])
    wg = dict(zip(first, got))
    wg["a_w_out"] = wg["a_w_out"].reshape(GATE_DIM, D_MODEL)
    wg["a_ln_v_g"] = wg["a_ln_v_g"].reshape(1, GATE_DIM)
    wg["a_ln_v_b"] = wg["a_ln_v_b"].reshape(1, GATE_DIM)
    shards = dict(zip(later, casts))

    sm = {k: _two_d(k, w[k]) for k in SMALL if k not in ("a_ln_v_g", "a_ln_v_b")}
    sm["a_b_st"] = sm["a_b_s"].T
    inv_freq = (ROPE_THETA ** (-jnp.arange(0, QK_ROPE, 2, dtype=F32) / QK_ROPE)).reshape(1, QK_ROPE // 2)

    losses, dx, g, small = _local_step(x[0], positions.reshape(t, 1), loss_target[0], inv_freq, wg, sm, shards)

    names = ("a_w_in", "a_w_out")
    sums = _pair_reduce("pair_reduce_a", [g[k] for k in names], after=[g["mlp_w1_0"], g["mlp_w2_0"]])
    g.update(zip(names, _by_sequencer("exchange_last", _chip_exchange_comm(sums), OTHER_CHIPS, collective_id=1)))

    out = {}
    for k in BIG:
        recvs = [[g[k + "_0"]], [g[k + "_1"]]] if k.startswith("mlp") else [[g[k]]]
        res, _ = _adamw_sharded("adamw_" + k, recvs, _three_d(w[k]), _three_d(m[k]), _three_d(v[k]))
        out[k] = [o.reshape(w[k].shape) for o in res]
    own_row = [k in ("a_ln_v_g", "a_ln_v_b") for k in SMALL]
    res = _adamw_small([small[k] for k in SMALL], [_two_d(k, w[k]) for k in SMALL], [_two_d(k, m[k]) for k in SMALL],
                       [_two_d(k, v[k]) for k in SMALL], own_row, losses)
    for i, k in enumerate(SMALL):
        out[k] = [o.reshape(w[k].shape) for o in res[4 * i:4 * i + 4]]

    return (res[-1].reshape(()), dx.reshape(x.shape), *[out[k][0] for k in WEIGHTS], *[out[k][1] for k in WEIGHTS],
            *[out[k][2] for k in WEIGHTS], *[out[k][3] for k in WEIGHTS])
```

```python
import math

import jax
import jax.numpy as jnp
from jax import lax
from jax.experimental import pallas as pl
from jax.experimental.pallas import tpu as pltpu
from jax.experimental.pallas import tpu_sc as plsc

F32, BF16 = jnp.float32, jnp.bfloat16
MESH = pl.DeviceIdType.MESH
ANY = pl.BlockSpec(memory_space=pl.ANY)
VMEM = pl.BlockSpec(memory_space=pltpu.VMEM)

N_DEV = 8
D_MODEL = 1024
CHUNK = 64
GMLP_BLOCK = 128
GATE_DIM = 2048
A_GROUPS = 8
A_GROUP_DIM = GATE_DIM // A_GROUPS
B_HEADS = 8
QK_NOPE, QK_ROPE, V_HEAD = 128, 64, 128
Q_LORA, KV_LORA = 384, 256
ROPE_THETA = 10000.0
D_FF = 4096
FF_SLOT = D_FF // N_DEV
EPS = 1e-6
ATT_SCALE = (QK_NOPE + QK_ROPE) ** -0.5

ADAM_LR, ADAM_B1, ADAM_B2, ADAM_EPS, ADAM_WD, ADAM_STEP = 0.001, 0.9, 0.999, 1e-08, 0.01, 10

TM = 256
TM_GATE = 128
VMEM_LIMIT = 56 * 1024 * 1024
INV_SQRT2 = 1.0 / math.sqrt(2.0)
INV_SQRT_2PI = 1.0 / math.sqrt(2.0 * math.pi)
LOG2_E = 1.0 / math.log(2.0)
HEADS_PER_STEP = 2


def _dot(a, b):
    return jnp.dot(a, b, preferred_element_type=F32)


def _dot_nt(a, b):
    return lax.dot_general(a, b, (((1,), (1,)), ((), ())), preferred_element_type=F32)


def _dot_tn(a, b):
    return lax.dot_general(a, b, (((0,), (0,)), ((), ())), preferred_element_type=F32)


def _rms_fwd(x, g):
    rstd = lax.rsqrt(jnp.mean(x * x, axis=-1, keepdims=True) + EPS)
    xhat = x * rstd
    return xhat * g, xhat, rstd


def _rms_bwd(dy, xhat, rstd, g):
    dxhat = dy * g
    dx = rstd * (dxhat - xhat * jnp.mean(dxhat * xhat, axis=-1, keepdims=True))
    return dx, jnp.sum(dy * xhat, axis=0, keepdims=True)


def _ln_fwd(v, g, b):
    mu = jnp.mean(v, axis=-1, keepdims=True)
    vc = v - mu
    rstd = lax.rsqrt(jnp.mean(vc * vc, axis=-1, keepdims=True) + EPS)
    vhat = vc * rstd
    return vhat * g + b, vhat, rstd


def _gelu(x):
    return 0.5 * x * (1.0 + lax.erf(x * INV_SQRT2))


def _gelu_and_grad(x):
    cdf = 0.5 * (1.0 + lax.erf(x * INV_SQRT2))
    return x * cdf, cdf + x * jnp.exp(-0.5 * x * x) * INV_SQRT_2PI


def _rope(x, cos, sin):
    x1, x2 = x[:, :QK_ROPE // 2], x[:, QK_ROPE // 2:]
    return jnp.concatenate([x1 * cos - x2 * sin, x2 * cos + x1 * sin], axis=-1)


def _gate_mask():
    row = lax.broadcasted_iota(jnp.int32, (GMLP_BLOCK, GMLP_BLOCK), 0)
    col = lax.broadcasted_iota(jnp.int32, (GMLP_BLOCK, GMLP_BLOCK), 1)
    return (col < CHUNK) | (row >= CHUNK)


def _att_mask(q0, tq, t):
    q = q0 + lax.broadcasted_iota(jnp.int32, (tq, t), 0)
    k = lax.broadcasted_iota(jnp.int32, (tq, t), 1)
    return jnp.right_shift(k, 6) <= jnp.right_shift(q, 6)


def _res(shape, imap=None):
    zeros = (0,) * len(shape)
    return pl.BlockSpec(shape, imap or (lambda i: zeros), pipeline_mode=pl.Buffered(1))


def _const(shape):
    zeros = (0,) * len(shape)
    return pl.BlockSpec(shape, lambda i: zeros)


def _row(d, tm=TM):
    return pl.BlockSpec((tm, d), lambda i: (i, 0))


def _heads(d):
    return pl.BlockSpec((B_HEADS, TM, d), lambda i: (0, i, 0))


def _sds(shape, dt):
    return jax.ShapeDtypeStruct(shape, dt)


def _acc(ref, val):
    @pl.when(pl.program_id(0) == 0)
    def _():
        ref[...] = jnp.zeros_like(ref)
    ref[...] += val


def _my_place():
    x, y, c = lax.axis_index("x"), lax.axis_index("y"), lax.axis_index("c")
    return x, y, c, 4 * x + 2 * y + c


def _peer(x, y, c, k):
    px = 1 - x if k & 4 else x
    py = 1 - y if k & 2 else y
    pc = 1 - c if k & 1 else c
    return (px, py, pc), 4 * px + 2 * py + pc


CHIPS = (2, 4, 6)


def _splits(ref):
    return len(ref.shape) >= 3 and ref.shape[1] % 32 == 0


def _piece(ref, block, half=None):
    if half is None or not _splits(ref):
        return ref.at[pl.ds(block, 1)]
    rows = ref.shape[1] // 2
    return ref.at[pl.ds(block, 1), pl.ds(half * rows, rows)]


def _gather_copy(sems, a, k, piece, to, src=None):
    return pltpu.make_async_remote_copy(
        src_ref=piece if src is None else src, dst_ref=piece, send_sem=sems[0].at[a, k], recv_sem=sems[1].at[a, k],
        device_id=to, device_id_type=MESH)


def _gather_start(srcs, outs, sems, only=None):
    x, y, c, me = _my_place()
    for a in range(len(srcs)) if only is None else (only,):
        mine = _piece(outs[a], me)
        pltpu.make_async_copy(srcs[a], mine, sems[2].at[a]).start()
        for k, rel in enumerate((1, 4, 2)):
            _gather_copy(sems, a, k, mine, _peer(x, y, c, rel)[0], src=srcs[a]).start()


def _gather_relay(srcs, outs, sems):
    x, y, c, _ = _my_place()
    sib = _peer(x, y, c, 1)[0]
    (xn, xn_i), (yn, yn_i) = _peer(x, y, c, 4), _peer(x, y, c, 2)
    for a in range(len(srcs)):
        out = outs[a]
        _gather_copy(sems, a, 1, _piece(out, xn_i), xn).wait_recv()
        _gather_copy(sems, a, 3, _piece(out, xn_i, 0), yn).start()
        _gather_copy(sems, a, 5, _piece(out, xn_i), sib).start()
        _gather_copy(sems, a, 2, _piece(out, yn_i), yn).wait_recv()
        if _splits(out):
            _gather_copy(sems, a, 4, _piece(out, yn_i, 1), xn).start()
        _gather_copy(sems, a, 6, _piece(out, yn_i), sib).start()


def _gather_finish(srcs, outs, sems):
    x, y, c, me = _my_place()
    sib = _peer(x, y, c, 1)[0]
    xn, yn, dg_i = _peer(x, y, c, 4)[0], _peer(x, y, c, 2)[0], _peer(x, y, c, 6)[1]
    n = len(srcs)
    for a in range(n):
        out = outs[a]
        _gather_copy(sems, a, 3, _piece(out, dg_i, 0), yn).wait_recv()
        _gather_copy(sems, a, 7, _piece(out, dg_i, 0), sib).start()
        if _splits(out):
            _gather_copy(sems, a, 4, _piece(out, dg_i, 1), xn).wait_recv()
            _gather_copy(sems, a, 8, _piece(out, dg_i, 1), sib).start()
    for a in range(n):
        out = outs[a]
        whole, half = _piece(out, me), _piece(out, me, 0)
        for k in (0, 5, 6):
            _gather_copy(sems, a, k, whole, sib).wait_recv()
        for k in (7, 8) if _splits(out) else (7,):
            _gather_copy(sems, a, k, half, sib).wait_recv()
        for k in (0, 1, 2):
            _gather_copy(sems, a, k, whole, sib, src=srcs[a]).wait_send()
        for k in (5, 6):
            _gather_copy(sems, a, k, whole, sib).wait_send()
        for k in (3, 4, 7, 8) if _splits(out) else (3, 7):
            _gather_copy(sems, a, k, half, sib).wait_send()
        pltpu.make_async_copy(srcs[a], whole, sems[2].at[a]).wait()


def _relay_sems(n):
    return [pltpu.SemaphoreType.DMA((n, 9)), pltpu.SemaphoreType.DMA((n, 9)), pltpu.SemaphoreType.DMA((n,))]


def _gather_sems(n):
    return [pltpu.SemaphoreType.DMA((n, 7)), pltpu.SemaphoreType.DMA((n, 7)), pltpu.SemaphoreType.DMA((n,))]


class _Comm:
    def __init__(self, args, out_shape, scratch, start, finish, relay=None):
        self.args, self.out_shape, self.scratch, self.start, self.finish = args, out_shape, scratch, start, finish
        self.relay = relay


def _gather_comm(shards):
    return _Comm(list(shards), [_sds((N_DEV,) + s.shape[1:], s.dtype) for s in shards], _relay_sems(len(shards)),
                 _gather_start, _gather_finish, relay=_gather_relay)


def _direct_copies(ins, outs, sems, wait, from_block):
    send_sems, recv_sems, local_sems = sems
    x, y, c, me = _my_place()
    for a in range(len(ins)):
        src = ins[a].at[pl.ds(me, 1)] if from_block[a] else ins[a]
        local = pltpu.make_async_copy(src, outs[a].at[pl.ds(me, 1)], local_sems.at[a])
        local.wait() if wait else local.start()
        for k in range(1, N_DEV):
            to, to_i = _peer(x, y, c, k)
            cp = pltpu.make_async_remote_copy(
                src_ref=ins[a].at[pl.ds(to_i, 1)] if from_block[a] else ins[a], dst_ref=outs[a].at[pl.ds(me, 1)],
                send_sem=send_sems.at[a, k - 1], recv_sem=recv_sems.at[a, k - 1], device_id=to, device_id_type=MESH)
            cp.wait() if wait else cp.start()


def _exchange_comm(grads=(), parts=()):
    ins = list(grads) + list(parts)
    from_block = [True] * len(grads) + [False] * len(parts)
    out_shape = [_sds(g.shape, g.dtype) for g in grads] + [_sds((N_DEV,) + p.shape[1:], p.dtype) for p in parts]

    def start(ins_, outs_, sems_):
        _direct_copies(ins_, outs_, sems_, False, from_block)

    def finish(ins_, outs_, sems_):
        _direct_copies(ins_, outs_, sems_, True, from_block)

    return _Comm(ins, out_shape, _gather_sems(len(ins)), start, finish)


def _chip_copies(ins, outs, sems, wait, rels, own):
    send_sems, recv_sems, local_sems = sems
    x, y, c, _ = _my_place()
    for a in range(len(ins)):
        if own:
            local = pltpu.make_async_copy(ins[a].at[pl.ds(2 * x + y, 1)], outs[a].at[pl.ds(len(rels), 1)],
                                          local_sems.at[a])
            local.wait() if wait else local.start()
        for i, j in enumerate(rels):
            to = _peer(x, y, c, CHIPS[j])[0]
            cp = pltpu.make_async_remote_copy(
                src_ref=ins[a].at[pl.ds(2 * to[0] + to[1], 1)], dst_ref=outs[a].at[pl.ds(i, 1)],
                send_sem=send_sems.at[a, i], recv_sem=recv_sems.at[a, i], device_id=to, device_id_type=MESH)
            cp.wait() if wait else cp.start()


def _chip_exchange_comm(sums, rels=(0, 1, 2), own=True):
    def start(ins_, outs_, sems_):
        _chip_copies(ins_, outs_, sems_, False, rels, own)

    def finish(ins_, outs_, sems_):
        _chip_copies(ins_, outs_, sems_, True, rels, own)

    n = len(sums)
    sems = [pltpu.SemaphoreType.DMA((n, len(rels))), pltpu.SemaphoreType.DMA((n, len(rels))),
            pltpu.SemaphoreType.DMA((n,))]
    return _Comm(list(sums), [_sds((len(rels) + own,) + s.shape[1:], s.dtype) for s in sums], sems, start, finish)


def _pair_reduce(name, grads, after=()):
    n = len(grads)
    n_chips = N_DEV // 2

    def body(*refs):
        g_refs, gh_refs, refs = refs[:n], refs[n:2 * n], refs[2 * n + len(after):]
        p_refs, land = refs[:n], refs[n:2 * n]
        send_sems, recv_sems = refs[2 * n:]
        x, y, c, _ = _my_place()
        sib = _peer(x, y, c, 1)[0]
        q = pl.program_id(0)

        def to_sibling(a, j):
            return pltpu.make_async_remote_copy(
                src_ref=gh_refs[a].at[j, pl.ds(1 - c, 1)], dst_ref=land[a].at[pl.ds(j, 1)],
                send_sem=send_sems.at[a, j], recv_sem=recv_sems.at[a, j], device_id=sib, device_id_type=MESH)

        @pl.when(q == 0)
        def _():
            for j in range(n_chips):
                for a in range(n):
                    to_sibling(a, j).start()

        for a in range(n):
            to_sibling(a, q).wait_recv()
            p_refs[a][...] = (g_refs[a][0, pl.ds(c, 1)].astype(F32) + land[a][pl.ds(q, 1)].astype(F32)).astype(BF16)

        @pl.when(q == n_chips - 1)
        def _():
            for a in range(n):
                for j in range(n_chips):
                    to_sibling(a, j).wait_send()

    views = [g.reshape((n_chips, 2) + g.shape[1:]) for g in grads]
    res = pl.pallas_call(
        body, name=name, grid=(n_chips,),
        in_specs=[pl.BlockSpec((1, 2) + g.shape[1:], lambda q: (q, 0, 0, 0)) for g in grads]
        + [ANY] * (n + len(after)),
        out_specs=[pl.BlockSpec((1,) + g.shape[1:], lambda q: (q, 0, 0)) for g in grads],
        out_shape=[_sds((n_chips,) + g.shape[1:], BF16) for g in grads],
        scratch_shapes=[pltpu.VMEM((n_chips,) + g.shape[1:], BF16) for g in grads]
        + [pltpu.SemaphoreType.DMA((n, n_chips)), pltpu.SemaphoreType.DMA((n, n_chips))],
        compiler_params=pltpu.CompilerParams(dimension_semantics=("arbitrary",), vmem_limit_bytes=VMEM_LIMIT),
    )(*views, *views, *after)
    return list(res)


def _pair_exchange_comm(grads):
    n, n_chips = len(grads), N_DEV // 2

    def copies(ins, outs, sems, wait):
        x, y, c, _ = _my_place()
        for j in range(n_chips):
            for a in range(n):
                cp = pltpu.make_async_remote_copy(
                    src_ref=ins[a].at[j, pl.ds(1 - c, 1)], dst_ref=outs[a].at[pl.ds(j, 1)], send_sem=sems[0].at[a, j],
                    recv_sem=sems[1].at[a, j], device_id=_peer(x, y, c, 1)[0], device_id_type=MESH)
                cp.wait() if wait else cp.start()

    views = [g.reshape((n_chips, 2) + g.shape[1:]) for g in grads]
    sems = [pltpu.SemaphoreType.DMA((n, n_chips)), pltpu.SemaphoreType.DMA((n, n_chips))]
    return _Comm(views, [_sds((n_chips,) + g.shape[1:], g.dtype) for g in grads], sems,
                 lambda i, o, s: copies(i, o, s, False), lambda i, o, s: copies(i, o, s, True))


def _pair_add(name, grads, landed, after=()):
    n, n_chips = len(grads), N_DEV // 2

    def body(*refs):
        g_refs, l_refs, p_refs = refs[:n], refs[n:2 * n], refs[2 * n:]
        c = lax.axis_index("c")
        for a in range(n):
            p_refs[a][...] = (g_refs[a][0, pl.ds(c, 1)].astype(F32) + l_refs[a][...].astype(F32)).astype(BF16)

    views = [g.reshape((n_chips, 2) + g.shape[1:]) for g in grads]
    blocks = [pl.BlockSpec((1,) + g.shape[1:], lambda q: (q, 0, 0)) for g in grads]
    return _call(name, body, (n_chips,),
                 [pl.BlockSpec((1, 2) + g.shape[1:], lambda q: (q, 0, 0, 0)) for g in grads] + blocks, blocks,
                 [_sds((n_chips,) + g.shape[1:], BF16) for g in grads], (*views, *landed), after=after)[0]


def _call(name, body, grid, in_specs, out_specs, out_shape, args, scratch=(), comm=None, after=()):
    params = pltpu.CompilerParams(dimension_semantics=("arbitrary",) * len(grid), vmem_limit_bytes=VMEM_LIMIT)
    if comm is None:
        ni, na = len(in_specs), len(after)

        def ordered(*refs):
            body(*refs[:ni], *refs[ni + na:])

        outs = pl.pallas_call(ordered if after else body, name=name, grid=grid, in_specs=list(in_specs) + [ANY] * na,
                              out_specs=list(out_specs), out_shape=list(out_shape), scratch_shapes=list(scratch),
                              compiler_params=params)(*args, *after)
        return list(outs), []
    ni, nci, no, nco, ns = len(in_specs), len(comm.args), len(out_specs), len(comm.out_shape), len(scratch)

    def carrying(*refs):
        ins, refs = refs[:ni], refs[ni:]
        cin, refs = refs[:nci], refs[nci:]
        outs, refs = refs[:no], refs[no:]
        cout, refs = refs[:nco], refs[nco:]
        scr, csems = refs[:ns], refs[ns:]
        step = pl.program_id(0)
        for ax in range(1, len(grid)):
            step = step * grid[ax] + pl.program_id(ax)
        steps = math.prod(grid)

        @pl.when(step == 0)
        def _():
            comm.start(cin, cout, csems)

        if comm.relay is not None:
            @pl.when(step == (2 * steps) // 3)
            def _():
                comm.relay(cin, cout, csems)

        body(*ins, *outs, *scr)

        @pl.when(step == steps - 1)
        def _():
            comm.finish(cin, cout, csems)

    outs = pl.pallas_call(
        carrying, name=name, grid=grid, in_specs=list(in_specs) + [ANY] * nci, out_specs=list(out_specs) + [ANY] * nco,
        out_shape=list(out_shape) + list(comm.out_shape), scratch_shapes=list(scratch) + list(comm.scratch),
        compiler_params=params)(*args, *comm.args)
    return list(outs[:no]), list(outs[no:])


def _comm_only(name, comm):
    def body(*refs):
        nci, nco = len(comm.args), len(comm.out_shape)
        cin, cout, csems = refs[:nci], refs[nci:nci + nco], refs[nci + nco:]
        comm.start(cin, cout, csems)
        if comm.relay is not None:
            comm.relay(cin, cout, csems)
        comm.finish(cin, cout, csems)

    return pl.pallas_call(body, name=name, in_specs=[ANY] * len(comm.args), out_specs=[ANY] * len(comm.out_shape),
                          out_shape=list(comm.out_shape), scratch_shapes=list(comm.scratch))(*comm.args)


SIBLING_AND_NEIGHBOURS, OTHER_CHIPS, EVERYONE = (1, 4, 2), CHIPS, tuple(range(1, N_DEV))


def _by_sequencer(name, comm, peers, collective_id):
    src = [jax.new_ref(a, memory_space=pltpu.MemorySpace.HBM) for a in comm.args]
    dst = [jax.empty_ref(s, memory_space=pltpu.MemorySpace.HBM) for s in comm.out_shape]

    @pl.kernel(mesh=plsc.ScalarSubcoreMesh(axis_name="sequencer", num_cores=1), name=name,
               scratch_types=tuple(comm.scratch), compiler_params=pltpu.CompilerParams(collective_id=collective_id))
    def launch(*sems):
        x, y, c, _ = _my_place()
        barrier = pltpu.get_barrier_semaphore()
        for k in peers:
            pl.semaphore_signal(barrier, inc=1, device_id=_peer(x, y, c, k)[0], device_id_type=MESH)
        pl.semaphore_wait(barrier, len(peers))
        comm.start(src, dst, sems)
        if comm.relay is not None:
            comm.relay(src, dst, sems)
        comm.finish(src, dst, sems)

    launch()
    return [d[...] for d in dst]


def _gather_first(first, later):
    nf, nl = len(first), len(later)
    dts = [BF16] * (nf - 2) + [F32, F32]

    def body(*refs):
        ins, refs = refs[:nf + nl], refs[nf + nl:]
        outs, refs = refs[:nf], refs[nf:]
        casts, refs = refs[:nl], refs[nl:]
        stage, sems = refs[:nf], refs[nf:]
        for a in range(nf):
            stage[a][...] = ins[a][...].astype(dts[a])
            _gather_start(stage, outs, sems, only=a)
        for a in range(nl):
            casts[a][...] = ins[nf + a][...].astype(BF16)
        _gather_relay(stage, outs, sems)
        _gather_finish(stage, outs, sems)

    res = pl.pallas_call(
        body, name="gather_first",
        in_specs=[VMEM] * (nf + nl), out_specs=[ANY] * nf + [VMEM] * nl,
        out_shape=[_sds((N_DEV,) + s.shape[1:], dt) for s, dt in zip(first, dts)]
        + [_sds(s.shape, BF16) for s in later],
        scratch_shapes=[pltpu.VMEM(s.shape, dt) for s, dt in zip(first, dts)] + _relay_sems(nf),
        compiler_params=pltpu.CompilerParams(vmem_limit_bytes=VMEM_LIMIT),
    )(*first, *later)
    return list(res[:nf]), list(res[nf:])


def _a_mix_fwd(x, g, w_in, ln_g, ln_b, w_s, b_st, w_out, comm=None):
    t = x.shape[0]
    nblk = TM // GMLP_BLOCK

    def body(x_ref, g_ref, win_ref, lng_ref, lnb_ref, ws_ref, bst_ref, wout_ref, h_ref, z_ref, gated_scr):
        xv = x_ref[...]
        hb = _rms_fwd(xv, g_ref[...])[0].astype(BF16)
        for d in range(N_DEV):
            z_ref[:, d * FF_SLOT:(d + 1) * FF_SLOT] = _dot(hb, win_ref[d])
        u = _gelu(z_ref[:, :GATE_DIM])
        vb = _ln_fwd(_gelu(z_ref[:, GATE_DIM:]), lng_ref[...], lnb_ref[...])[0].astype(BF16)
        mask = _gate_mask()
        for gi in range(A_GROUPS):
            wm = jnp.where(mask, ws_ref[gi], 0.0).astype(BF16)
            bias = bst_ref[:, gi:gi + 1]
            cs = slice(gi * A_GROUP_DIM, (gi + 1) * A_GROUP_DIM)
            for n in range(nblk):
                rs = slice(n * GMLP_BLOCK, (n + 1) * GMLP_BLOCK)
                sv = _dot(wm, vb[rs, cs]) + bias
                gated_scr[rs, cs] = (u[rs, cs] * sv).astype(BF16)
        h_ref[...] = xv + _dot(gated_scr[...], wout_ref[...])

    return _call(
        "a_mix_fwd", body, (t // TM,),
        [_row(D_MODEL), _res((1, D_MODEL)), _res((N_DEV, D_MODEL, FF_SLOT)), _res((1, GATE_DIM)),
         _res((1, GATE_DIM)), _res((A_GROUPS, GMLP_BLOCK, GMLP_BLOCK)), _res((GMLP_BLOCK, A_GROUPS)),
         _res((GATE_DIM, D_MODEL))],
        [_row(D_MODEL), _row(2 * GATE_DIM), _row(GATE_DIM)],
        [_sds((t, D_MODEL), F32), _sds((t, 2 * GATE_DIM), F32), _sds((t, GATE_DIM), BF16)],
        (x, g, w_in, ln_g, ln_b, w_s, b_st, w_out), comm=comm)


MLP_W_SPECS = (_res((N_DEV, D_MODEL, FF_SLOT)), _res((N_DEV, FF_SLOT, D_MODEL)))


def _mlp_fwd(h, g, w1, w2, layer, comm=None):
    t = h.shape[0]

    def body(h_ref, g_ref, w1_ref, w2_ref, o_ref, a_ref):
        hv = h_ref[...]
        hb = _rms_fwd(hv, g_ref[...])[0].astype(BF16)
        o_ref[...] = hv
        for d in range(N_DEV):
            a = _dot(hb, w1_ref[d])
            a_ref[:, d * FF_SLOT:(d + 1) * FF_SLOT] = a
            r = jnp.maximum(a, 0.0)
            o_ref[...] += _dot((r * r).astype(BF16), w2_ref[d])

    return _call(
        f"mlp_fwd_{layer}", body, (t // TM,), [_row(D_MODEL), _res((1, D_MODEL)), *MLP_W_SPECS],
        [_row(D_MODEL), _row(D_FF)], [_sds((t, D_MODEL), F32), _sds((t, D_FF), F32)], (h, g, w1, w2), comm=comm)


def _mlp_fwd_loss(h, g, w1, w2, final_g, target):
    t = h.shape[0]

    def body(h_ref, g_ref, w1_ref, w2_ref, fg_ref, t_ref, a_ref, loss_ref, dh_ref, dg_ref):
        hv = h_ref[...]
        hb = _rms_fwd(hv, g_ref[...])[0].astype(BF16)
        out = hv
        for d in range(N_DEV):
            a = _dot(hb, w1_ref[d])
            a_ref[:, d * FF_SLOT:(d + 1) * FF_SLOT] = a
            r = jnp.maximum(a, 0.0)
            out = out + _dot((r * r).astype(BF16), w2_ref[d])
        y, xhat, rstd = _rms_fwd(out, fg_ref[...])
        err = y - t_ref[...]
        part = 0.5 * jnp.sum(jnp.mean(err * err, axis=-1, keepdims=True), axis=0, keepdims=True)
        dx, dg = _rms_bwd(err * (1.0 / D_MODEL), xhat, rstd, fg_ref[...])
        dh_ref[...] = dx
        _acc(dg_ref, dg)
        _acc(loss_ref, part)

    return _call(
        "mlp_fwd_loss", body, (t // TM,),
        [_row(D_MODEL), _res((1, D_MODEL)), *MLP_W_SPECS, _res((1, D_MODEL)), _row(D_MODEL)],
        [_row(D_FF), _const((1, 1)), _row(D_MODEL), _const((1, D_MODEL))],
        [_sds((t, D_FF), F32), _sds((1, 1), F32), _sds((t, D_MODEL), F32), _sds((1, D_MODEL), F32)],
        (h, g, w1, w2, final_g, target))[0]


KVQ_W_SPECS = (_res((1, D_MODEL)), _res((D_MODEL, KV_LORA + QK_ROPE)), _res((1, KV_LORA)),
               _res((B_HEADS, KV_LORA, QK_NOPE + V_HEAD)), _res((1, D_MODEL)), _res((D_MODEL, Q_LORA)),
               _res((1, Q_LORA)), _res((B_HEADS, Q_LORA, QK_NOPE + QK_ROPE)))


def _kvq_fwd(h, pos, inv_freq, kvq_w):
    t = h.shape[0]
    half = QK_ROPE // 2

    def body(h_ref, pos_ref, invf_ref, srcg_ref, wkva_ref, kvag_ref, wkvb_ref, mixg_ref, wqa_ref, qg_ref, wqb_ref,
             ckv_ref, k_ref, v_ref, cqpre_ref, q_ref, cos_ref, sin_ref):
        hv = h_ref[...]
        xhat = hv * lax.rsqrt(jnp.mean(hv * hv, axis=-1, keepdims=True) + EPS)
        ang = pos_ref[...].astype(F32) * invf_ref[...]
        cos, sin = jnp.cos(ang), jnp.sin(ang)
        cos_ref[...] = cos
        sin_ref[...] = sin
        ckv = _dot((xhat * srcg_ref[...]).astype(BF16), wkva_ref[...])
        ckv_ref[...] = ckv
        cb = _rms_fwd(ckv[:, :KV_LORA], kvag_ref[...])[0].astype(BF16)
        kpe = _rope(ckv[:, KV_LORA:], cos, sin).astype(BF16)
        for hd in range(B_HEADS):
            kv = _dot(cb, wkvb_ref[hd])
            k_ref[hd, :, 0:QK_NOPE] = kv[:, :QK_NOPE].astype(BF16)
            k_ref[hd, :, QK_NOPE:] = kpe
            v_ref[hd] = kv[:, QK_NOPE:].astype(BF16)
        cqpre = _dot((xhat * mixg_ref[...]).astype(BF16), wqa_ref[...])
        cqpre_ref[...] = cqpre
        cqb = _rms_fwd(cqpre, qg_ref[...])[0].astype(BF16)
        for hd in range(B_HEADS):
            q = _dot(cqb, wqb_ref[hd])
            q_ref[hd, :, 0:QK_NOPE] = q[:, :QK_NOPE].astype(BF16)
            q_ref[hd, :, QK_NOPE:] = _rope(q[:, QK_NOPE:], cos, sin).astype(BF16)

    return _call(
        "kvq_fwd", body, (t // TM,), [_row(D_MODEL), _row(1), _res((1, half)), *KVQ_W_SPECS],
        [_row(KV_LORA + QK_ROPE), _heads(QK_NOPE + QK_ROPE), _heads(V_HEAD), _row(Q_LORA),
         _heads(QK_NOPE + QK_ROPE), _row(half), _row(half)],
        [_sds((t, KV_LORA + QK_ROPE), F32), _sds((B_HEADS, t, QK_NOPE + QK_ROPE), BF16),
         _sds((B_HEADS, t, V_HEAD), BF16), _sds((t, Q_LORA), F32), _sds((B_HEADS, t, QK_NOPE + QK_ROPE), BF16),
         _sds((t, half), F32), _sds((t, half), F32)],
        (h, pos, inv_freq, *kvq_w))[0]


def _softmax_rows(q, k_ref, k):
    past, upto = k * TM, (k + 1) * TM
    s = _dot_nt(q, k_ref[0:upto, :])
    own = jnp.where(_att_mask(0, TM, TM), s[:, past:], jnp.finfo(F32).min)
    s = own if k == 0 else jnp.concatenate([s[:, :past], own], axis=1)
    e = jnp.exp2((s - jnp.max(s, axis=-1, keepdims=True)) * (ATT_SCALE * LOG2_E))
    return e * (1.0 / jnp.sum(e, axis=-1, keepdims=True))


def _for_my_tile(i, nq, fn):
    for k in range(nq):
        @pl.when(i == k)
        def _(k=k):
            fn(k)


def _attn_fwd(h, q, k, v, w_o, comm=None):
    t = h.shape[0]
    nq, hps = t // TM, HEADS_PER_STEP

    def body(h_ref, q_ref, k_ref, v_ref, wo_ref, o_ref, att_ref):
        i, pair = pl.program_id(0), pl.program_id(1)

        @pl.when(pair == 0)
        def _():
            o_ref[...] = h_ref[...]

        def tile(kt):
            proj = None
            for j in range(hps):
                hd = pair * hps + j
                p = _softmax_rows(q_ref[j], k_ref.at[hd], kt)
                ob = _dot(p.astype(BF16), v_ref[hd, 0:(kt + 1) * TM, :]).astype(BF16)
                att_ref[j] = ob
                proj = _dot(ob, wo_ref[hd]) if proj is None else proj + _dot(ob, wo_ref[hd])
            o_ref[...] += proj

        _for_my_tile(i, nq, tile)

    def per_head(d):
        return pl.BlockSpec((hps, TM, d), lambda i, pair: (pair, i, 0))

    def resident(shape):
        zeros = (0,) * len(shape)
        return pl.BlockSpec(shape, lambda i, pair: zeros, pipeline_mode=pl.Buffered(1))

    tile_spec = pl.BlockSpec((TM, D_MODEL), lambda i, pair: (i, 0))
    return _call(
        "attn_fwd", body, (nq, B_HEADS // hps),
        [tile_spec, per_head(QK_NOPE + QK_ROPE), resident((B_HEADS, t, QK_NOPE + QK_ROPE)),
         resident((B_HEADS, t, V_HEAD)), resident((B_HEADS, V_HEAD, D_MODEL))],
        [tile_spec, per_head(V_HEAD)], [_sds((t, D_MODEL), F32), _sds((B_HEADS, t, V_HEAD), BF16)],
        (h, q, k, v, w_o), comm=comm)


def _mlp_bwd(h, a, dho, g, w1, w2, layer, after=()):
    t = h.shape[0]

    def body(h_ref, a_ref, dho_ref, g_ref, w1_ref, w2_ref, dhi_ref, dg_ref, hn_ref, f_ref, da_ref, dhib_ref):
        gv = g_ref[...]
        y, xhat, rstd = _rms_fwd(h_ref[...], gv)
        hn_ref[...] = y.astype(BF16)
        dho_v = dho_ref[...]
        dhob = dho_v.astype(BF16)
        dhn = jnp.zeros((TM, D_MODEL), F32)
        for d in range(N_DEV):
            cs = slice(d * FF_SLOT, (d + 1) * FF_SLOT)
            r = jnp.maximum(a_ref[:, cs], 0.0)
            f_ref[:, cs] = (r * r).astype(BF16)
            da = (_dot_nt(dhob, w2_ref[d]) * (2.0 * r)).astype(BF16)
            da_ref[:, cs] = da
            dhn = dhn + _dot_nt(da, w1_ref[d])
        dx, dg = _rms_bwd(dhn, xhat, rstd, gv)
        dhi = dho_v + dx
        dhi_ref[...] = dhi
        dhib_ref[...] = dhi.astype(BF16)
        _acc(dg_ref, dg)

    return _call(
        f"mlp_bwd_{layer}", body, (t // TM,),
        [_row(D_MODEL), _row(D_FF), _row(D_MODEL), _res((1, D_MODEL)), *MLP_W_SPECS],
        [_row(D_MODEL), _const((1, D_MODEL)), _row(D_MODEL), _row(D_FF), _row(D_FF), _row(D_MODEL)],
        [_sds((t, D_MODEL), F32), _sds((1, D_MODEL), F32), _sds((t, D_MODEL), BF16), _sds((t, D_FF), BF16),
         _sds((t, D_FF), BF16), _sds((t, D_MODEL), BF16)],
        (h, a, dho, g, w1, w2), after=after)


def _attn_bwd(dh, q, k, v, w_o, cos, sin, after=()):
    t = dh.shape[0]
    half, hps = QK_ROPE // 2, HEADS_PER_STEP

    def body(dh_ref, q_ref, k_ref, v_ref, wo_ref, cos_ref, sin_ref, dq_ref, dk_ref, dv_ref):
        i = pl.program_id(1)

        @pl.when(i == 0)
        def _():
            dk_ref[...] = jnp.zeros_like(dk_ref)
            dv_ref[...] = jnp.zeros_like(dv_ref)

        def tile(kt):
            keys = slice(0, (kt + 1) * TM)
            for j in range(hps):
                qj = q_ref[j]
                do = _dot_nt(dh_ref[kt * TM:(kt + 1) * TM, :], wo_ref[j]).astype(BF16)
                p = _softmax_rows(qj, k_ref.at[j], kt)
                dp = _dot_nt(do, v_ref[j, keys, :])
                ds = (p * (dp - jnp.sum(p * dp, axis=-1, keepdims=True)) * ATT_SCALE).astype(BF16)
                dq = _dot(ds, k_ref[j, keys, :])
                dq_ref[j, :, 0:QK_NOPE] = dq[:, :QK_NOPE].astype(BF16)
                dq_ref[j, :, QK_NOPE:] = _rope(dq[:, QK_NOPE:], cos_ref[...], -sin_ref[...]).astype(BF16)
                dk_ref[j, keys, :] += _dot_tn(ds, qj)
                dv_ref[j, keys, :] += _dot_tn(p.astype(BF16), do)

        _for_my_tile(i, t // TM, tile)

    def per_pair(rows, d, tiled):
        return pl.BlockSpec((hps, rows, d), (lambda pair, i: (pair, i, 0)) if tiled else (lambda pair, i: (pair, 0, 0)))

    def tile(d):
        return pl.BlockSpec((TM, d), lambda pair, i: (i, 0))

    return _call(
        "attn_bwd", body, (B_HEADS // hps, t // TM),
        [pl.BlockSpec((t, D_MODEL), lambda pair, i: (0, 0), pipeline_mode=pl.Buffered(1)),
         per_pair(TM, QK_NOPE + QK_ROPE, True), per_pair(t, QK_NOPE + QK_ROPE, False), per_pair(t, V_HEAD, False),
         per_pair(V_HEAD, D_MODEL, False), tile(half), tile(half)],
        [per_pair(TM, QK_NOPE + QK_ROPE, True), per_pair(t, QK_NOPE + QK_ROPE, False), per_pair(t, V_HEAD, False)],
        [_sds((B_HEADS, t, QK_NOPE + QK_ROPE), BF16), _sds((B_HEADS, t, QK_NOPE + QK_ROPE), F32),
         _sds((B_HEADS, t, V_HEAD), F32)],
        (dh, q, k, v, w_o, cos, sin), after=after)


def _kvq_bwd(h, dh, ckv, cqpre, dq, dk, dv, cos, sin, kvq_w):
    t = h.shape[0]
    half = QK_ROPE // 2

    def body(h_ref, dh_ref, ckv_ref, cqpre_ref, dq_ref, dk_ref, dv_ref, cos_ref, sin_ref,
             srcg_ref, wkva_ref, kvag_ref, wkvb_ref, mixg_ref, wqa_ref, qg_ref, wqb_ref,
             dhi_ref, hq_ref, hk_ref, cq_ref, dcqpre_ref, c_ref, dkv_ref, dckv_ref,
             dmixg_ref, dsrcg_ref, dqg_ref, dkvag_ref):
        hv = h_ref[...]
        rstd = lax.rsqrt(jnp.mean(hv * hv, axis=-1, keepdims=True) + EPS)
        xhat = hv * rstd
        mixg, srcg, qg, kvag = mixg_ref[...], srcg_ref[...], qg_ref[...], kvag_ref[...]
        hq_ref[...] = (xhat * mixg).astype(BF16)
        hk_ref[...] = (xhat * srcg).astype(BF16)
        cq, cqhat, crstd = _rms_fwd(cqpre_ref[...], qg)
        cq_ref[...] = cq.astype(BF16)
        dcq = jnp.zeros((TM, Q_LORA), F32)
        for hd in range(B_HEADS):
            dcq = dcq + _dot_nt(dq_ref[hd], wqb_ref[hd])
        dcqpre, dqg = _rms_bwd(dcq, cqhat, crstd, qg)
        dcqpre_b = dcqpre.astype(BF16)
        dcqpre_ref[...] = dcqpre_b
        dxq, dmixg = _rms_bwd(_dot_nt(dcqpre_b, wqa_ref[...]), xhat, rstd, mixg)
        ckv = ckv_ref[...]
        c, chat, krstd = _rms_fwd(ckv[:, :KV_LORA], kvag)
        c_ref[...] = c.astype(BF16)
        dc = jnp.zeros((TM, KV_LORA), F32)
        dkpe = jnp.zeros((TM, QK_ROPE), F32)
        for hd in range(B_HEADS):
            dkv = jnp.concatenate([dk_ref[hd, :, 0:QK_NOPE], dv_ref[hd]], axis=-1).astype(BF16)
            dkv_ref[hd] = dkv
            dc = dc + _dot_nt(dkv, wkvb_ref[hd])
            dkpe = dkpe + dk_ref[hd, :, QK_NOPE:]
        dlat, dkvag = _rms_bwd(dc, chat, krstd, kvag)
        dpe = _rope(dkpe, cos_ref[...], -sin_ref[...])
        dckv_b = jnp.concatenate([dlat, dpe], axis=-1).astype(BF16)
        dckv_ref[...] = dckv_b
        dxk, dsrcg = _rms_bwd(_dot_nt(dckv_b, wkva_ref[...]), xhat, rstd, srcg)
        dhi_ref[...] = dh_ref[...] + dxq + dxk
        _acc(dmixg_ref, dmixg)
        _acc(dsrcg_ref, dsrcg)
        _acc(dqg_ref, dqg)
        _acc(dkvag_ref, dkvag)

    return _call(
        "kvq_bwd", body, (t // TM,),
        [_row(D_MODEL), _row(D_MODEL), _row(KV_LORA + QK_ROPE), _row(Q_LORA), _heads(QK_NOPE + QK_ROPE),
         _heads(QK_NOPE + QK_ROPE), _heads(V_HEAD), _row(half), _row(half), *KVQ_W_SPECS],
        [_row(D_MODEL), _row(D_MODEL), _row(D_MODEL), _row(Q_LORA), _row(Q_LORA), _row(KV_LORA),
         _heads(QK_NOPE + V_HEAD), _row(KV_LORA + QK_ROPE),
         _const((1, D_MODEL)), _const((1, D_MODEL)), _const((1, Q_LORA)), _const((1, KV_LORA))],
        [_sds((t, D_MODEL), F32), _sds((t, D_MODEL), BF16), _sds((t, D_MODEL), BF16), _sds((t, Q_LORA), BF16),
         _sds((t, Q_LORA), BF16), _sds((t, KV_LORA), BF16), _sds((B_HEADS, t, QK_NOPE + V_HEAD), BF16),
         _sds((t, KV_LORA + QK_ROPE), BF16),
         _sds((1, D_MODEL), F32), _sds((1, D_MODEL), F32), _sds((1, Q_LORA), F32), _sds((1, KV_LORA), F32)],
        (h, dh, ckv, cqpre, dq, dk, dv, cos, sin, *kvq_w))[0]


def _a_mix_bwd(x, z, dh, g, w_in, ln_g, ln_b, w_s, b_st, w_out, after=()):
    t = x.shape[0]
    tm = TM_GATE
    nblk = tm // GMLP_BLOCK

    def body(x_ref, z_ref, dh_ref, g_ref, win_ref, lng_ref, lnb_ref, ws_ref, bst_ref, wout_ref,
             dx_ref, hn_ref, dz_ref, dg_ref, dlng_ref, dlnb_ref, dws_ref, dbs_ref, dvn_scr, gelu_grad_v):
        @pl.when(pl.program_id(0) == 0)
        def _():
            dws_ref[...] = jnp.zeros_like(dws_ref)
            dbs_ref[...] = jnp.zeros_like(dbs_ref)

        gv, lng = g_ref[...], lng_ref[...]
        y, xhat, rstd = _rms_fwd(x_ref[...], gv)
        hn_ref[...] = y.astype(BF16)
        dhv = dh_ref[...]
        dgated = _dot_nt(dhv.astype(BF16), wout_ref[...])
        u, gelu_grad_u = _gelu_and_grad(z_ref[:, :GATE_DIM])
        v, gelu_grad_v[...] = _gelu_and_grad(z_ref[:, GATE_DIM:])
        vn, vhat, lrstd = _ln_fwd(v, lng, lnb_ref[...])
        vb = vn.astype(BF16)
        mask = _gate_mask()
        for gi in range(A_GROUPS):
            wm = jnp.where(mask, ws_ref[gi], 0.0).astype(BF16)
            bias = bst_ref[:, gi:gi + 1]
            cs = slice(gi * A_GROUP_DIM, (gi + 1) * A_GROUP_DIM)
            dws = jnp.zeros((GMLP_BLOCK, GMLP_BLOCK), F32)
            dbs = jnp.zeros((GMLP_BLOCK, 1), F32)
            for n in range(nblk):
                rs = slice(n * GMLP_BLOCK, (n + 1) * GMLP_BLOCK)
                sv = _dot(wm, vb[rs, cs]) + bias
                dz_ref[rs, cs] = (dgated[rs, cs] * sv * gelu_grad_u[rs, cs]).astype(BF16)
                dsv = dgated[rs, cs] * u[rs, cs]
                dsvb = dsv.astype(BF16)
                dws = dws + _dot_nt(dsvb, vb[rs, cs])
                dbs = dbs + jnp.sum(dsv, axis=-1, keepdims=True)
                dvn_scr[rs, cs] = _dot_tn(wm, dsvb)
            dws_ref[gi] += jnp.where(mask, dws, 0.0)
            dbs_ref[gi] += dbs
        dvn = dvn_scr[...]
        dvhat = dvn * lng
        dv = lrstd * (dvhat - jnp.mean(dvhat, axis=-1, keepdims=True)
                      - vhat * jnp.mean(dvhat * vhat, axis=-1, keepdims=True))
        dz_ref[:, GATE_DIM:] = (dv * gelu_grad_v[...]).astype(BF16)
        dhn = jnp.zeros((tm, D_MODEL), F32)
        for d in range(N_DEV):
            dhn = dhn + _dot_nt(dz_ref[:, d * FF_SLOT:(d + 1) * FF_SLOT], win_ref[d])
        dx, dg = _rms_bwd(dhn, xhat, rstd, gv)
        dx_ref[...] = dhv + dx
        _acc(dg_ref, dg)
        _acc(dlng_ref, jnp.sum(dvn * vhat, axis=0, keepdims=True))
        _acc(dlnb_ref, jnp.sum(dvn, axis=0, keepdims=True))

    return _call(
        "a_mix_bwd", body, (t // tm,),
        [_row(D_MODEL, tm), _row(2 * GATE_DIM, tm), _row(D_MODEL, tm), _res((1, D_MODEL)),
         _res((N_DEV, D_MODEL, FF_SLOT)), _res((1, GATE_DIM)), _res((1, GATE_DIM)),
         _res((A_GROUPS, GMLP_BLOCK, GMLP_BLOCK)), _res((GMLP_BLOCK, A_GROUPS)), _res((GATE_DIM, D_MODEL))],
        [_row(D_MODEL, tm), _row(D_MODEL, tm), _row(2 * GATE_DIM, tm),
         _const((1, D_MODEL)), _const((1, GATE_DIM)), _const((1, GATE_DIM)),
         _const((A_GROUPS, GMLP_BLOCK, GMLP_BLOCK)), _const((A_GROUPS, GMLP_BLOCK, 1))],
        [_sds((t, D_MODEL), F32), _sds((t, D_MODEL), BF16),
         _sds((t, 2 * GATE_DIM), BF16), _sds((1, D_MODEL), F32), _sds((1, GATE_DIM), F32),
         _sds((1, GATE_DIM), F32), _sds((A_GROUPS, GMLP_BLOCK, GMLP_BLOCK), F32),
         _sds((A_GROUPS, GMLP_BLOCK, 1), F32)],
        (x, z, dh, g, w_in, ln_g, ln_b, w_s, b_st, w_out),
        scratch=[pltpu.VMEM((tm, GATE_DIM), F32), pltpu.VMEM((tm, GATE_DIM), F32)], after=after)


def _wgrad(name, a, b, a_spec, b_spec, m, n, after=()):
    def body(a_ref, b_ref, o_ref):
        o_ref[0] = _dot_tn(a_ref[...].astype(BF16), b_ref[...].astype(BF16)).astype(BF16)

    return _call(name, body, (N_DEV,), [a_spec, b_spec], [pl.BlockSpec((1, m, n), lambda d: (d, 0, 0))],
                 [_sds((N_DEV, m, n), BF16)], (a, b), after=after)[0][0]


def _full(t, d):
    return pl.BlockSpec((t, d), lambda i: (0, 0), pipeline_mode=pl.Buffered(1))


def _cols(t, d):
    return pl.BlockSpec((t, d), lambda i: (0, i))


def _head(t, d):
    return pl.BlockSpec((None, t, d), lambda i: (i, 0, 0))


def _local_step(x, pos, target, inv_freq, wg, sm, shards=None):
    t = x.shape[0]
    wg = dict(wg)
    dist = shards is not None
    mix_g = [sm["norm_mix_g"][l:l + 1] for l in range(2)]
    mlp_g = [sm["norm_mlp_g"][l:l + 1] for l in range(2)]

    ids = iter(range(2, 2 + 9))

    def gather(names):
        if dist:
            got = _by_sequencer("gather_" + names[0], _gather_comm([shards[k] for k in names]),
                                SIBLING_AND_NEIGHBOURS, next(ids))
            wg.update(zip(names, got))

    def send(name, names):
        if dist:
            comm = _exchange_comm(grads=[g[k] for k in names])
            g.update(zip(names, _by_sequencer("exchange_" + name, comm, EVERYONE, next(ids))))

    def send_sums(name, names, meanwhile):
        if not dist:
            meanwhile()
            return ()
        grads = [g[k] for k in names]
        landed = _by_sequencer("pair_exchange_" + name, _pair_exchange_comm(grads), (1,), next(ids))
        sums = _pair_add("pair_add_" + name, grads, landed, after=meanwhile())
        g.update(zip(names, _by_sequencer("exchange_" + name, _chip_exchange_comm(sums), OTHER_CHIPS, next(ids))))
        return sums

    def a_args():
        return (wg["a_w_in"], wg["a_ln_v_g"], wg["a_ln_v_b"], sm["a_w_s"], sm["a_b_st"], wg["a_w_out"])

    def kvq_w():
        return (sm["kv_src_norm_g"], wg["kv_w_a"], sm["kv_a_norm_g"], wg["kv_w_b"], mix_g[1], wg["b_w_q_a"],
                sm["b_q_norm_g"], wg["b_w_q_b"])

    gather(("mlp_w1_0", "mlp_w2_0"))
    (h1, z, gated), _ = _a_mix_fwd(x, mix_g[0], *a_args())
    gather(("kv_w_a", "kv_w_b", "b_w_q_a", "b_w_q_b", "b_w_o"))
    (h2, a0), _ = _mlp_fwd(h1, mlp_g[0], wg["mlp_w1_0"], wg["mlp_w2_0"], 0)
    if dist:
        wg["b_w_q_a"] = wg["b_w_q_a"].reshape(D_MODEL, Q_LORA)
        wg["kv_w_a"] = wg["kv_w_a"].reshape(D_MODEL, KV_LORA + QK_ROPE)
    gather(("mlp_w1_1", "mlp_w2_1"))
    ckv, k, v, cqpre, q, cos, sin = _kvq_fwd(h2, pos, inv_freq, kvq_w())
    (h3, att), _ = _attn_fwd(h2, q, k, v, wg["b_w_o"])
    a1, loss, dh4, d_final_g = _mlp_fwd_loss(h3, mlp_g[1], wg["mlp_w1_1"], wg["mlp_w2_1"], sm["final_norm_g"], target)

    g = {}
    (dh3, d_mlp_g1, hn, f, da, dh3_b), _ = _mlp_bwd(h3, a1, dh4, mlp_g[1], wg["mlp_w1_1"], wg["mlp_w2_1"], 1)
    g["mlp_w1_1"] = _wgrad("wgrad_w1_1", hn, da, _full(t, D_MODEL), _cols(t, FF_SLOT), D_MODEL, FF_SLOT)
    g["mlp_w2_1"] = _wgrad("wgrad_w2_1", f, dh4, _cols(t, FF_SLOT), _full(t, D_MODEL), FF_SLOT, D_MODEL)

    attn = {}

    def attention_backward():
        g["b_w_o"] = _wgrad("wgrad_w_o", att, dh3_b, _head(t, V_HEAD), _full(t, D_MODEL), V_HEAD, D_MODEL)
        attn["grads"], _ = _attn_bwd(dh3_b, q, k, v, wg["b_w_o"], cos, sin)
        return [g["b_w_o"], attn["grads"][0]]

    send_sums("mlp_1", ("mlp_w1_1", "mlp_w2_1"), attention_backward)
    dq, dk, dv = attn["grads"]
    (dh2, hq, hk, cq, dcqpre, c, dkv, dckv, d_mix_g1, d_src_g, d_q_g, d_kv_a_g) = _kvq_bwd(
        h2, dh3, ckv, cqpre, dq, dk, dv, cos, sin, kvq_w())
    g["b_w_q_a"] = _wgrad("wgrad_w_q_a", hq, dcqpre, _cols(t, D_MODEL // N_DEV), _full(t, Q_LORA),
                          D_MODEL // N_DEV, Q_LORA)
    g["b_w_q_b"] = _wgrad("wgrad_w_q_b", cq, dq, _full(t, Q_LORA), _head(t, QK_NOPE + QK_ROPE),
                          Q_LORA, QK_NOPE + QK_ROPE)
    g["kv_w_a"] = _wgrad("wgrad_kv_w_a", hk, dckv, _cols(t, D_MODEL // N_DEV), _full(t, KV_LORA + QK_ROPE),
                         D_MODEL // N_DEV, KV_LORA + QK_ROPE)
    g["kv_w_b"] = _wgrad("wgrad_kv_w_b", c, dkv, _full(t, KV_LORA), _head(t, QK_NOPE + V_HEAD),
                         KV_LORA, QK_NOPE + V_HEAD)
    qkv = ("b_w_o", "b_w_q_a", "b_w_q_b", "kv_w_a", "kv_w_b")
    landed = [g[k] for k in qkv]
    send("qkv", qkv)
    (dh1, d_mlp_g0, hn, f, da, dh1_b), _ = _mlp_bwd(h1, a0, dh2, mlp_g[0], wg["mlp_w1_0"], wg["mlp_w2_0"], 0,
                                                    after=landed if dist else ())
    landed = [g["mlp_w1_1"], g["mlp_w2_1"]] if dist else ()
    g["mlp_w1_0"] = _wgrad("wgrad_w1_0", hn, da, _full(t, D_MODEL), _cols(t, FF_SLOT), D_MODEL, FF_SLOT, after=landed)
    g["mlp_w2_0"] = _wgrad("wgrad_w2_0", f, dh2, _cols(t, FF_SLOT), _full(t, D_MODEL), FF_SLOT, D_MODEL)

    def wgrad_a_w_out():
        g["a_w_out"] = _wgrad("wgrad_a_w_out", gated, dh1_b, _cols(t, GATE_DIM // N_DEV), _full(t, D_MODEL),
                              GATE_DIM // N_DEV, D_MODEL)
        return [g["a_w_out"]] + [g[k] for k in qkv]

    sums = send_sums("mlp_0", ("mlp_w1_0", "mlp_w2_0"), wgrad_a_w_out)
    (dx, hn, dz, d_mix_g0, d_ln_g, d_ln_b, d_ws, d_bs), _ = _a_mix_bwd(x, z, dh1, mix_g[0], *a_args(), after=sums)
    small = {
        "norm_mix_g": jnp.concatenate([d_mix_g0, d_mix_g1], axis=0),
        "norm_mlp_g": jnp.concatenate([d_mlp_g0, d_mlp_g1], axis=0),
        "a_ln_v_g": d_ln_g.reshape(N_DEV, GATE_DIM // N_DEV),
        "a_ln_v_b": d_ln_b.reshape(N_DEV, GATE_DIM // N_DEV),
        "a_w_s": d_ws.astype(BF16) if dist else d_ws,
        "a_b_s": d_bs.reshape(A_GROUPS, GMLP_BLOCK),
        "b_q_norm_g": d_q_g,
        "kv_src_norm_g": d_src_g,
        "kv_a_norm_g": d_kv_a_g,
        "final_norm_g": d_final_g,
    }
    if dist:
        parts = [small[k].reshape((1,) + small[k].shape) for k in SMALL] + [loss.reshape(1, 1, 1)]
        got = _by_sequencer("gather_small", _exchange_comm(parts=parts), EVERYONE, next(ids))
        small, loss = dict(zip(SMALL, got)), got[-1]
    g["a_w_in"] = _wgrad("wgrad_a_w_in", hn, dz, _full(t, D_MODEL), _cols(t, FF_SLOT), D_MODEL, FF_SLOT)
    return loss, dx, g, small


def _adamw(w, g, m, v):
    m = ADAM_B1 * m + (1.0 - ADAM_B1) * g
    v = ADAM_B2 * v + (1.0 - ADAM_B2) * (g * g)
    m_hat = m / (1.0 - ADAM_B1 ** ADAM_STEP)
    v_hat = v / (1.0 - ADAM_B2 ** ADAM_STEP)
    return -ADAM_LR * (m_hat / (jnp.sqrt(v_hat) + ADAM_EPS) + ADAM_WD * w), m, v


def _sum_in_device_order(r_ref):
    g = r_ref[0].astype(F32)
    for j in range(1, r_ref.shape[0]):
        g = g + r_ref[j].astype(F32)
    return g


def _adamw_sharded(name, recvs, w, m, v, comm=None):
    layers, r, c = w.shape
    tr = math.gcd(r, 512)
    flat = [a for per_layer in recvs for a in per_layer]

    def body(*refs):
        r_refs, (w_ref, m_ref, v_ref) = refs[:len(flat)], refs[len(flat):len(flat) + 3]
        g_ref, d_ref, nm_ref, nv_ref = refs[-4:]
        layer = pl.program_id(0)
        g, pos = None, 0
        for li, per_layer in enumerate(recvs):
            total = None
            for ref in r_refs[pos:pos + len(per_layer)]:
                part = _sum_in_device_order(ref)
                total = part if total is None else total + part
            pos += len(per_layer)
            g = total if g is None else jnp.where(layer == li, total, g)
        g_ref[...] = g
        d_ref[...], nm_ref[...], nv_ref[...] = _adamw(w_ref[...], g, m_ref[...], v_ref[...])

    blk = pl.BlockSpec((None, tr, c), lambda l, i: (l, i, 0))
    return _call(name, body, (layers, r // tr),
                 [pl.BlockSpec((a.shape[0], tr, c), lambda l, i: (0, i, 0)) for a in flat] + [blk] * 3,
                 [blk] * 4, [_sds(w.shape, F32)] * 4, (*flat, w, m, v), comm=comm)


def _adamw_small(recvs, ws, ms, vs, own_row, losses):
    n = len(recvs)

    def body(*refs):
        r_refs, w_refs, m_refs, v_refs = (refs[i * n:(i + 1) * n] for i in range(4))
        outs, scr = refs[4 * n + 1:8 * n + 2], refs[8 * n + 2:]
        outs[-1][...] = _sum_in_device_order(refs[4 * n])
        me = _my_place()[3]
        for a in range(n):
            g = _sum_in_device_order(r_refs[a])
            if own_row[a]:
                scr[0][...] = g
                g = scr[0][pl.ds(me, 1), :]
            g_ref, d_ref, nm_ref, nv_ref = outs[4 * a:4 * a + 4]
            g_ref[...] = g
            d_ref[...], nm_ref[...], nv_ref[...] = _adamw(w_refs[a][...], g, m_refs[a][...], v_refs[a][...])

    out_shape = []
    for w in ws:
        out_shape += [_sds(w.shape, F32)] * 4
    return pl.pallas_call(
        body, name="adamw_small", in_specs=[VMEM] * (4 * n + 1), out_specs=[VMEM] * (4 * n + 1),
        out_shape=out_shape + [_sds((1, 1), F32)], scratch_shapes=[pltpu.VMEM((N_DEV, GATE_DIM // N_DEV), F32)],
    )(*recvs, *ws, *ms, *vs, losses)


BIG = ("a_w_in", "a_w_out", "b_w_q_a", "b_w_q_b", "b_w_o", "kv_w_a", "kv_w_b", "mlp_w1", "mlp_w2")
SMALL = ("norm_mix_g", "norm_mlp_g", "a_ln_v_g", "a_ln_v_b", "a_w_s", "a_b_s", "b_q_norm_g", "kv_src_norm_g",
         "kv_a_norm_g", "final_norm_g")
WEIGHTS = ("norm_mix_g", "norm_mlp_g", "a_w_in", "a_ln_v_g", "a_ln_v_b", "a_w_s", "a_b_s", "a_w_out", "b_w_q_a",
           "b_q_norm_g", "b_w_q_b", "b_w_o", "kv_src_norm_g", "kv_w_a", "kv_a_norm_g", "kv_w_b", "mlp_w1", "mlp_w2",
           "final_norm_g")


def _two_d(name, a):
    if name in ("a_w_s", "a_b_s"):
        return a.reshape(a.shape[1:])
    return a.reshape(1, -1) if a.ndim == 1 else a


def _three_d(a):
    return a if a.ndim == 3 else a.reshape((1,) + a.shape)


def kernel(x, positions, norm_mix_g, norm_mlp_g, a_w_in, a_ln_v_g, a_ln_v_b, a_w_s, a_b_s, a_w_out, b_w_q_a, b_q_norm_g, b_w_q_b, b_w_o, kv_src_norm_g, kv_w_a, kv_a_norm_g, kv_w_b, mlp_w1, mlp_w2, final_norm_g, loss_target, m_norm_mix_g, m_norm_mlp_g, m_a_w_in, m_a_ln_v_g, m_a_ln_v_b, m_a_w_s, m_a_b_s, m_a_w_out, m_b_w_q_a, m_b_q_norm_g, m_b_w_q_b, m_b_w_o, m_kv_src_norm_g, m_kv_w_a, m_kv_a_norm_g, m_kv_w_b, m_mlp_w1, m_mlp_w2, m_final_norm_g, v_norm_mix_g, v_norm_mlp_g, v_a_w_in, v_a_ln_v_g, v_a_ln_v_b, v_a_w_s, v_a_b_s, v_a_w_out, v_b_w_q_a, v_b_q_norm_g, v_b_w_q_b, v_b_w_o, v_kv_src_norm_g, v_kv_w_a, v_kv_a_norm_g, v_kv_w_b, v_mlp_w1, v_mlp_w2, v_final_norm_g):
    w = dict(norm_mix_g=norm_mix_g, norm_mlp_g=norm_mlp_g, a_w_in=a_w_in, a_ln_v_g=a_ln_v_g, a_ln_v_b=a_ln_v_b,
             a_w_s=a_w_s, a_b_s=a_b_s, a_w_out=a_w_out, b_w_q_a=b_w_q_a, b_q_norm_g=b_q_norm_g, b_w_q_b=b_w_q_b,
             b_w_o=b_w_o, kv_src_norm_g=kv_src_norm_g, kv_w_a=kv_w_a, kv_a_norm_g=kv_a_norm_g, kv_w_b=kv_w_b,
             mlp_w1=mlp_w1, mlp_w2=mlp_w2, final_norm_g=final_norm_g)
    m = dict(norm_mix_g=m_norm_mix_g, norm_mlp_g=m_norm_mlp_g, a_w_in=m_a_w_in, a_ln_v_g=m_a_ln_v_g,
             a_ln_v_b=m_a_ln_v_b, a_w_s=m_a_w_s, a_b_s=m_a_b_s, a_w_out=m_a_w_out, b_w_q_a=m_b_w_q_a,
             b_q_norm_g=m_b_q_norm_g, b_w_q_b=m_b_w_q_b, b_w_o=m_b_w_o, kv_src_norm_g=m_kv_src_norm_g,
             kv_w_a=m_kv_w_a, kv_a_norm_g=m_kv_a_norm_g, kv_w_b=m_kv_w_b, mlp_w1=m_mlp_w1, mlp_w2=m_mlp_w2,
             final_norm_g=m_final_norm_g)
    v = dict(norm_mix_g=v_norm_mix_g, norm_mlp_g=v_norm_mlp_g, a_w_in=v_a_w_in, a_ln_v_g=v_a_ln_v_g,
             a_ln_v_b=v_a_ln_v_b, a_w_s=v_a_w_s, a_b_s=v_a_b_s, a_w_out=v_a_w_out, b_w_q_a=v_b_w_q_a,
             b_q_norm_g=v_b_q_norm_g, b_w_q_b=v_b_w_q_b, b_w_o=v_b_w_o, kv_src_norm_g=v_kv_src_norm_g,
             kv_w_a=v_kv_w_a, kv_a_norm_g=v_kv_a_norm_g, kv_w_b=v_kv_w_b, mlp_w1=v_mlp_w1, mlp_w2=v_mlp_w2,
             final_norm_g=v_final_norm_g)
    t = x.shape[1]

    first = ("a_w_in", "a_w_out", "a_ln_v_g", "a_ln_v_b")
    later = ("mlp_w1_0", "mlp_w2_0", "mlp_w1_1", "mlp_w2_1", "kv_w_a", "kv_w_b", "b_w_q_a", "b_w_q_b", "b_w_o")
    blocks = {k: _three_d(w[k]) for k in BIG if not k.startswith("mlp")}
    for k in ("mlp_w1", "mlp_w2"):
        blocks[k + "_0"], blocks[k + "_1"] = w[k][0:1], w[k][1:2]
    got, casts = _gather_first([blocks[k] if k in blocks else w[k] for k in first], [blocks[k] for k in later])
    wg = dict(zip(first, got))
    wg["a_w_out"] = wg["a_w_out"].reshape(GATE_DIM, D_MODEL)
    wg["a_ln_v_g"] = wg["a_ln_v_g"].reshape(1, GATE_DIM)
    wg["a_ln_v_b"] = wg["a_ln_v_b"].reshape(1, GATE_DIM)
    shards = dict(zip(later, casts))

    sm = {k: _two_d(k, w[k]) for k in SMALL if k not in ("a_ln_v_g", "a_ln_v_b")}
    sm["a_b_st"] = sm["a_b_s"].T
    inv_freq = (ROPE_THETA ** (-jnp.arange(0, QK_ROPE, 2, dtype=F32) / QK_ROPE)).reshape(1, QK_ROPE // 2)

    losses, dx, g, small = _local_step(x[0], positions.reshape(t, 1), loss_target[0], inv_freq, wg, sm, shards)

    names = ("a_w_in", "a_w_out")
    sums = _pair_reduce("pair_reduce_a", [g[k] for k in names], after=[g["mlp_w1_0"], g["mlp_w2_0"]])
    g.update(zip(names, _by_sequencer("exchange_last", _chip_exchange_comm(sums), OTHER_CHIPS, collective_id=1)))

    out = {}
    for k in BIG:
        recvs = [[g[k + "_0"]], [g[k + "_1"]]] if k.startswith("mlp") else [[g[k]]]
        res, _ = _adamw_sharded("adamw_" + k, recvs, _three_d(w[k]), _three_d(m[k]), _three_d(v[k]))
        out[k] = [o.reshape(w[k].shape) for o in res]
    own_row = [k in ("a_ln_v_g", "a_ln_v_b") for k in SMALL]
    res = _adamw_small([small[k] for k in SMALL], [_two_d(k, w[k]) for k in SMALL], [_two_d(k, m[k]) for k in SMALL],
                       [_two_d(k, v[k]) for k in SMALL], own_row, losses)
    for i, k in enumerate(SMALL):
        out[k] = [o.reshape(w[k].shape) for o in res[4 * i:4 * i + 4]]

    return (res[-1].reshape(()), dx.reshape(x.shape), *[out[k][0] for k in WEIGHTS], *[out[k][1] for k in WEIGHTS],
            *[out[k][2] for k in WEIGHTS], *[out[k][3] for k in WEIGHTS])
```

```python
import math

import jax
import jax.numpy as jnp
from jax import lax
from jax.experimental import pallas as pl
from jax.experimental.pallas import tpu as pltpu
from jax.experimental.pallas import tpu_sc as plsc

F32, BF16 = jnp.float32, jnp.bfloat16
MESH = pl.DeviceIdType.MESH
ANY = pl.BlockSpec(memory_space=pl.ANY)
VMEM = pl.BlockSpec(memory_space=pltpu.VMEM)

N_DEV = 8
D_MODEL = 1024
CHUNK = 64
GMLP_BLOCK = 128
GATE_DIM = 2048
A_GROUPS = 8
A_GROUP_DIM = GATE_DIM // A_GROUPS
B_HEADS = 8
QK_NOPE, QK_ROPE, V_HEAD = 128, 64, 128
Q_LORA, KV_LORA = 384, 256
ROPE_THETA = 10000.0
D_FF = 4096
FF_SLOT = D_FF // N_DEV
EPS = 1e-6
ATT_SCALE = (QK_NOPE + QK_ROPE) ** -0.5

ADAM_LR, ADAM_B1, ADAM_B2, ADAM_EPS, ADAM_WD, ADAM_STEP = 0.001, 0.9, 0.999, 1e-08, 0.01, 10

TM = 256
TM_GATE = 128
VMEM_LIMIT = 56 * 1024 * 1024
INV_SQRT2 = 1.0 / math.sqrt(2.0)
INV_SQRT_2PI = 1.0 / math.sqrt(2.0 * math.pi)
LOG2_E = 1.0 / math.log(2.0)
HEADS_PER_STEP = 2


def _dot(a, b):
    return jnp.dot(a, b, preferred_element_type=F32)


def _dot_nt(a, b):
    return lax.dot_general(a, b, (((1,), (1,)), ((), ())), preferred_element_type=F32)


def _dot_tn(a, b):
    return lax.dot_general(a, b, (((0,), (0,)), ((), ())), preferred_element_type=F32)


def _rms_fwd(x, g):
    rstd = lax.rsqrt(jnp.mean(x * x, axis=-1, keepdims=True) + EPS)
    xhat = x * rstd
    return xhat * g, xhat, rstd


def _rms_bwd(dy, xhat, rstd, g):
    dxhat = dy * g
    dx = rstd * (dxhat - xhat * jnp.mean(dxhat * xhat, axis=-1, keepdims=True))
    return dx, jnp.sum(dy * xhat, axis=0, keepdims=True)


def _ln_fwd(v, g, b):
    mu = jnp.mean(v, axis=-1, keepdims=True)
    vc = v - mu
    rstd = lax.rsqrt(jnp.mean(vc * vc, axis=-1, keepdims=True) + EPS)
    vhat = vc * rstd
    return vhat * g + b, vhat, rstd


def _gelu(x):
    return 0.5 * x * (1.0 + lax.erf(x * INV_SQRT2))


def _gelu_and_grad(x):
    cdf = 0.5 * (1.0 + lax.erf(x * INV_SQRT2))
    return x * cdf, cdf + x * jnp.exp(-0.5 * x * x) * INV_SQRT_2PI


def _rope(x, cos, sin):
    x1, x2 = x[:, :QK_ROPE // 2], x[:, QK_ROPE // 2:]
    return jnp.concatenate([x1 * cos - x2 * sin, x2 * cos + x1 * sin], axis=-1)


def _gate_mask():
    row = lax.broadcasted_iota(jnp.int32, (GMLP_BLOCK, GMLP_BLOCK), 0)
    col = lax.broadcasted_iota(jnp.int32, (GMLP_BLOCK, GMLP_BLOCK), 1)
    return (col < CHUNK) | (row >= CHUNK)


def _att_mask(q0, tq, t):
    q = q0 + lax.broadcasted_iota(jnp.int32, (tq, t), 0)
    k = lax.broadcasted_iota(jnp.int32, (tq, t), 1)
    return jnp.right_shift(k, 6) <= jnp.right_shift(q, 6)


def _res(shape, imap=None):
    zeros = (0,) * len(shape)
    return pl.BlockSpec(shape, imap or (lambda i: zeros), pipeline_mode=pl.Buffered(1))


def _const(shape):
    zeros = (0,) * len(shape)
    return pl.BlockSpec(shape, lambda i: zeros)


def _row(d, tm=TM):
    return pl.BlockSpec((tm, d), lambda i: (i, 0))


def _heads(d):
    return pl.BlockSpec((B_HEADS, TM, d), lambda i: (0, i, 0))


def _sds(shape, dt):
    return jax.ShapeDtypeStruct(shape, dt)


def _acc(ref, val):
    @pl.when(pl.program_id(0) == 0)
    def _():
        ref[...] = jnp.zeros_like(ref)
    ref[...] += val


def _my_place():
    x, y, c = lax.axis_index("x"), lax.axis_index("y"), lax.axis_index("c")
    return x, y, c, 4 * x + 2 * y + c


def _peer(x, y, c, k):
    px = 1 - x if k & 4 else x
    py = 1 - y if k & 2 else y
    pc = 1 - c if k & 1 else c
    return (px, py, pc), 4 * px + 2 * py + pc


CHIPS = (2, 4, 6)


def _splits(ref):
    return len(ref.shape) >= 3 and ref.shape[1] % 32 == 0


def _piece(ref, block, half=None):
    if half is None or not _splits(ref):
        return ref.at[pl.ds(block, 1)]
    rows = ref.shape[1] // 2
    return ref.at[pl.ds(block, 1), pl.ds(half * rows, rows)]


def _gather_copy(sems, a, k, piece, to, src=None):
    return pltpu.make_async_remote_copy(
        src_ref=piece if src is None else src, dst_ref=piece, send_sem=sems[0].at[a, k], recv_sem=sems[1].at[a, k],
        device_id=to, device_id_type=MESH)


def _gather_start(srcs, outs, sems, only=None):
    x, y, c, me = _my_place()
    for a in range(len(srcs)) if only is None else (only,):
        mine = _piece(outs[a], me)
        pltpu.make_async_copy(srcs[a], mine, sems[2].at[a]).start()
        for k, rel in enumerate((1, 4, 2)):
            _gather_copy(sems, a, k, mine, _peer(x, y, c, rel)[0], src=srcs[a]).start()


def _gather_relay(srcs, outs, sems):
    x, y, c, _ = _my_place()
    sib = _peer(x, y, c, 1)[0]
    (xn, xn_i), (yn, yn_i) = _peer(x, y, c, 4), _peer(x, y, c, 2)
    for a in range(len(srcs)):
        out = outs[a]
        _gather_copy(sems, a, 1, _piece(out, xn_i), xn).wait_recv()
        _gather_copy(sems, a, 3, _piece(out, xn_i, 0), yn).start()
        _gather_copy(sems, a, 5, _piece(out, xn_i), sib).start()
        _gather_copy(sems, a, 2, _piece(out, yn_i), yn).wait_recv()
        if _splits(out):
            _gather_copy(sems, a, 4, _piece(out, yn_i, 1), xn).start()
        _gather_copy(sems, a, 6, _piece(out, yn_i), sib).start()


def _gather_finish(srcs, outs, sems):
    x, y, c, me = _my_place()
    sib = _peer(x, y, c, 1)[0]
    xn, yn, dg_i = _peer(x, y, c, 4)[0], _peer(x, y, c, 2)[0], _peer(x, y, c, 6)[1]
    n = len(srcs)
    for a in range(n):
        out = outs[a]
        _gather_copy(sems, a, 3, _piece(out, dg_i, 0), yn).wait_recv()
        _gather_copy(sems, a, 7, _piece(out, dg_i, 0), sib).start()
        if _splits(out):
            _gather_copy(sems, a, 4, _piece(out, dg_i, 1), xn).wait_recv()
            _gather_copy(sems, a, 8, _piece(out, dg_i, 1), sib).start()
    for a in range(n):
        out = outs[a]
        whole, half = _piece(out, me), _piece(out, me, 0)
        for k in (0, 5, 6):
            _gather_copy(sems, a, k, whole, sib).wait_recv()
        for k in (7, 8) if _splits(out) else (7,):
            _gather_copy(sems, a, k, half, sib).wait_recv()
        for k in (0, 1, 2):
            _gather_copy(sems, a, k, whole, sib, src=srcs[a]).wait_send()
        for k in (5, 6):
            _gather_copy(sems, a, k, whole, sib).wait_send()
        for k in (3, 4, 7, 8) if _splits(out) else (3, 7):
            _gather_copy(sems, a, k, half, sib).wait_send()
        pltpu.make_async_copy(srcs[a], whole, sems[2].at[a]).wait()


def _relay_sems(n):
    return [pltpu.SemaphoreType.DMA((n, 9)), pltpu.SemaphoreType.DMA((n, 9)), pltpu.SemaphoreType.DMA((n,))]


def _gather_sems(n):
    return [pltpu.SemaphoreType.DMA((n, 7)), pltpu.SemaphoreType.DMA((n, 7)), pltpu.SemaphoreType.DMA((n,))]


class _Comm:
    def __init__(self, args, out_shape, scratch, start, finish, relay=None):
        self.args, self.out_shape, self.scratch, self.start, self.finish = args, out_shape, scratch, start, finish
        self.relay = relay


def _gather_comm(shards):
    return _Comm(list(shards), [_sds((N_DEV,) + s.shape[1:], s.dtype) for s in shards], _relay_sems(len(shards)),
                 _gather_start, _gather_finish, relay=_gather_relay)


def _direct_copies(ins, outs, sems, wait, from_block):
    send_sems, recv_sems, local_sems = sems
    x, y, c, me = _my_place()
    for a in range(len(ins)):
        src = ins[a].at[pl.ds(me, 1)] if from_block[a] else ins[a]
        local = pltpu.make_async_copy(src, outs[a].at[pl.ds(me, 1)], local_sems.at[a])
        local.wait() if wait else local.start()
        for k in range(1, N_DEV):
            to, to_i = _peer(x, y, c, k)
            cp = pltpu.make_async_remote_copy(
                src_ref=ins[a].at[pl.ds(to_i, 1)] if from_block[a] else ins[a], dst_ref=outs[a].at[pl.ds(me, 1)],
                send_sem=send_sems.at[a, k - 1], recv_sem=recv_sems.at[a, k - 1], device_id=to, device_id_type=MESH)
            cp.wait() if wait else cp.start()


def _exchange_comm(grads=(), parts=()):
    ins = list(grads) + list(parts)
    from_block = [True] * len(grads) + [False] * len(parts)
    out_shape = [_sds(g.shape, g.dtype) for g in grads] + [_sds((N_DEV,) + p.shape[1:], p.dtype) for p in parts]

    def start(ins_, outs_, sems_):
        _direct_copies(ins_, outs_, sems_, False, from_block)

    def finish(ins_, outs_, sems_):
        _direct_copies(ins_, outs_, sems_, True, from_block)

    return _Comm(ins, out_shape, _gather_sems(len(ins)), start, finish)


def _chip_copies(ins, outs, sems, wait, rels, own):
    send_sems, recv_sems, local_sems = sems
    x, y, c, _ = _my_place()
    for a in range(len(ins)):
        if own:
            local = pltpu.make_async_copy(ins[a].at[pl.ds(2 * x + y, 1)], outs[a].at[pl.ds(len(rels), 1)],
                                          local_sems.at[a])
            local.wait() if wait else local.start()
        for i, j in enumerate(rels):
            to = _peer(x, y, c, CHIPS[j])[0]
            cp = pltpu.make_async_remote_copy(
                src_ref=ins[a].at[pl.ds(2 * to[0] + to[1], 1)], dst_ref=outs[a].at[pl.ds(i, 1)],
                send_sem=send_sems.at[a, i], recv_sem=recv_sems.at[a, i], device_id=to, device_id_type=MESH)
            cp.wait() if wait else cp.start()


def _chip_exchange_comm(sums, rels=(0, 1, 2), own=True):
    def start(ins_, outs_, sems_):
        _chip_copies(ins_, outs_, sems_, False, rels, own)

    def finish(ins_, outs_, sems_):
        _chip_copies(ins_, outs_, sems_, True, rels, own)

    n = len(sums)
    sems = [pltpu.SemaphoreType.DMA((n, len(rels))), pltpu.SemaphoreType.DMA((n, len(rels))),
            pltpu.SemaphoreType.DMA((n,))]
    return _Comm(list(sums), [_sds((len(rels) + own,) + s.shape[1:], s.dtype) for s in sums], sems, start, finish)


def _pair_reduce(name, grads, after=()):
    n = len(grads)
    n_chips = N_DEV // 2

    def body(*refs):
        g_refs, gh_refs, refs = refs[:n], refs[n:2 * n], refs[2 * n + len(after):]
        p_refs, land = refs[:n], refs[n:2 * n]
        send_sems, recv_sems = refs[2 * n:]
        x, y, c, _ = _my_place()
        sib = _peer(x, y, c, 1)[0]
        q = pl.program_id(0)

        def to_sibling(a, j):
            return pltpu.make_async_remote_copy(
                src_ref=gh_refs[a].at[j, pl.ds(1 - c, 1)], dst_ref=land[a].at[pl.ds(j, 1)],
                send_sem=send_sems.at[a, j], recv_sem=recv_sems.at[a, j], device_id=sib, device_id_type=MESH)

        @pl.when(q == 0)
        def _():
            for j in range(n_chips):
                for a in range(n):
                    to_sibling(a, j).start()

        for a in range(n):
            to_sibling(a, q).wait_recv()
            p_refs[a][...] = (g_refs[a][0, pl.ds(c, 1)].astype(F32) + land[a][pl.ds(q, 1)].astype(F32)).astype(BF16)

        @pl.when(q == n_chips - 1)
        def _():
            for a in range(n):
                for j in range(n_chips):
                    to_sibling(a, j).wait_send()

    views = [g.reshape((n_chips, 2) + g.shape[1:]) for g in grads]
    res = pl.pallas_call(
        body, name=name, grid=(n_chips,),
        in_specs=[pl.BlockSpec((1, 2) + g.shape[1:], lambda q: (q, 0, 0, 0)) for g in grads]
        + [ANY] * (n + len(after)),
        out_specs=[pl.BlockSpec((1,) + g.shape[1:], lambda q: (q, 0, 0)) for g in grads],
        out_shape=[_sds((n_chips,) + g.shape[1:], BF16) for g in grads],
        scratch_shapes=[pltpu.VMEM((n_chips,) + g.shape[1:], BF16) for g in grads]
        + [pltpu.SemaphoreType.DMA((n, n_chips)), pltpu.SemaphoreType.DMA((n, n_chips))],
        compiler_params=pltpu.CompilerParams(dimension_semantics=("arbitrary",), vmem_limit_bytes=VMEM_LIMIT),
    )(*views, *views, *after)
    return list(res)


def _pair_exchange_comm(grads):
    n, n_chips = len(grads), N_DEV // 2

    def copies(ins, outs, sems, wait):
        x, y, c, _ = _my_place()
        for j in range(n_chips):
            for a in range(n):
                cp = pltpu.make_async_remote_copy(
                    src_ref=ins[a].at[j, pl.ds(1 - c, 1)], dst_ref=outs[a].at[pl.ds(j, 1)], send_sem=sems[0].at[a, j],
                    recv_sem=sems[1].at[a, j], device_id=_peer(x, y, c, 1)[0], device_id_type=MESH)
                cp.wait() if wait else cp.start()

    views = [g.reshape((n_chips, 2) + g.shape[1:]) for g in grads]
    sems = [pltpu.SemaphoreType.DMA((n, n_chips)), pltpu.SemaphoreType.DMA((n, n_chips))]
    return _Comm(views, [_sds((n_chips,) + g.shape[1:], g.dtype) for g in grads], sems,
                 lambda i, o, s: copies(i, o, s, False), lambda i, o, s: copies(i, o, s, True))


def _pair_add(name, grads, landed, after=()):
    n, n_chips = len(grads), N_DEV // 2

    def body(*refs):
        g_refs, l_refs, p_refs = refs[:n], refs[n:2 * n], refs[2 * n:]
        c = lax.axis_index("c")
        for a in range(n):
            p_refs[a][...] = (g_refs[a][0, pl.ds(c, 1)].astype(F32) + l_refs[a][...].astype(F32)).astype(BF16)

    views = [g.reshape((n_chips, 2) + g.shape[1:]) for g in grads]
    blocks = [pl.BlockSpec((1,) + g.shape[1:], lambda q: (q, 0, 0)) for g in grads]
    return _call(name, body, (n_chips,),
                 [pl.BlockSpec((1, 2) + g.shape[1:], lambda q: (q, 0, 0, 0)) for g in grads] + blocks, blocks,
                 [_sds((n_chips,) + g.shape[1:], BF16) for g in grads], (*views, *landed), after=after)[0]


def _call(name, body, grid, in_specs, out_specs, out_shape, args, scratch=(), comm=None, after=()):
    params = pltpu.CompilerParams(dimension_semantics=("arbitrary",) * len(grid), vmem_limit_bytes=VMEM_LIMIT)
    if comm is None:
        ni, na = len(in_specs), len(after)

        def ordered(*refs):
            body(*refs[:ni], *refs[ni + na:])

        outs = pl.pallas_call(ordered if after else body, name=name, grid=grid, in_specs=list(in_specs) + [ANY] * na,
                              out_specs=list(out_specs), out_shape=list(out_shape), scratch_shapes=list(scratch),
                              compiler_params=params)(*args, *after)
        return list(outs), []
    ni, nci, no, nco, ns = len(in_specs), len(comm.args), len(out_specs), len(comm.out_shape), len(scratch)

    def carrying(*refs):
        ins, refs = refs[:ni], refs[ni:]
        cin, refs = refs[:nci], refs[nci:]
        outs, refs = refs[:no], refs[no:]
        cout, refs = refs[:nco], refs[nco:]
        scr, csems = refs[:ns], refs[ns:]
        step = pl.program_id(0)
        for ax in range(1, len(grid)):
            step = step * grid[ax] + pl.program_id(ax)
        steps = math.prod(grid)

        @pl.when(step == 0)
        def _():
            comm.start(cin, cout, csems)

        if comm.relay is not None:
            @pl.when(step == (2 * steps) // 3)
            def _():
                comm.relay(cin, cout, csems)

        body(*ins, *outs, *scr)

        @pl.when(step == steps - 1)
        def _():
            comm.finish(cin, cout, csems)

    outs = pl.pallas_call(
        carrying, name=name, grid=grid, in_specs=list(in_specs) + [ANY] * nci, out_specs=list(out_specs) + [ANY] * nco,
        out_shape=list(out_shape) + list(comm.out_shape), scratch_shapes=list(scratch) + list(comm.scratch),
        compiler_params=params)(*args, *comm.args)
    return list(outs[:no]), list(outs[no:])


def _comm_only(name, comm):
    def body(*refs):
        nci, nco = len(comm.args), len(comm.out_shape)
        cin, cout, csems = refs[:nci], refs[nci:nci + nco], refs[nci + nco:]
        comm.start(cin, cout, csems)
        if comm.relay is not None:
            comm.relay(cin, cout, csems)
        comm.finish(cin, cout, csems)

    return pl.pallas_call(body, name=name, in_specs=[ANY] * len(comm.args), out_specs=[ANY] * len(comm.out_shape),
                          out_shape=list(comm.out_shape), scratch_shapes=list(comm.scratch))(*comm.args)


SIBLING_AND_NEIGHBOURS, OTHER_CHIPS, EVERYONE = (1, 4, 2), CHIPS, tuple(range(1, N_DEV))


def _by_sequencer(name, comm, peers, collective_id):
    src = [jax.new_ref(a, memory_space=pltpu.MemorySpace.HBM) for a in comm.args]
    dst = [jax.empty_ref(s, memory_space=pltpu.MemorySpace.HBM) for s in comm.out_shape]

    @pl.kernel(mesh=plsc.ScalarSubcoreMesh(axis_name="sequencer", num_cores=1), name=name,
               scratch_types=tuple(comm.scratch), compiler_params=pltpu.CompilerParams(collective_id=collective_id))
    def launch(*sems):
        x, y, c, _ = _my_place()
        barrier = pltpu.get_barrier_semaphore()
        for k in peers:
            pl.semaphore_signal(barrier, inc=1, device_id=_peer(x, y, c, k)[0], device_id_type=MESH)
        pl.semaphore_wait(barrier, len(peers))
        comm.start(src, dst, sems)
        if comm.relay is not None:
            comm.relay(src, dst, sems)
        comm.finish(src, dst, sems)

    launch()
    return [d[...] for d in dst]


def _gather_first(first, later):
    nf, nl = len(first), len(later)
    dts = [BF16] * (nf - 2) + [F32, F32]

    def body(*refs):
        ins, refs = refs[:nf + nl], refs[nf + nl:]
        outs, refs = refs[:nf], refs[nf:]
        casts, refs = refs[:nl], refs[nl:]
        stage, sems = refs[:nf], refs[nf:]
        for a in range(nf):
            stage[a][...] = ins[a][...].astype(dts[a])
            _gather_start(stage, outs, sems, only=a)
        for a in range(nl):
            casts[a][...] = ins[nf + a][...].astype(BF16)
        _gather_relay(stage, outs, sems)
        _gather_finish(stage, outs, sems)

    res = pl.pallas_call(
        body, name="gather_first",
        in_specs=[VMEM] * (nf + nl), out_specs=[ANY] * nf + [VMEM] * nl,
        out_shape=[_sds((N_DEV,) + s.shape[1:], dt) for s, dt in zip(first, dts)]
        + [_sds(s.shape, BF16) for s in later],
        scratch_shapes=[pltpu.VMEM(s.shape, dt) for s, dt in zip(first, dts)] + _relay_sems(nf),
        compiler_params=pltpu.CompilerParams(vmem_limit_bytes=VMEM_LIMIT),
    )(*first, *later)
    return list(res[:nf]), list(res[nf:])


def _a_mix_fwd(x, g, w_in, ln_g, ln_b, w_s, b_st, w_out, comm=None):
    t = x.shape[0]
    nblk = TM // GMLP_BLOCK

    def body(x_ref, g_ref, win_ref, lng_ref, lnb_ref, ws_ref, bst_ref, wout_ref, h_ref, z_ref, gated_scr):
        xv = x_ref[...]
        hb = _rms_fwd(xv, g_ref[...])[0].astype(BF16)
        for d in range(N_DEV):
            z_ref[:, d * FF_SLOT:(d + 1) * FF_SLOT] = _dot(hb, win_ref[d])
        u = _gelu(z_ref[:, :GATE_DIM])
        vb = _ln_fwd(_gelu(z_ref[:, GATE_DIM:]), lng_ref[...], lnb_ref[...])[0].astype(BF16)
        mask = _gate_mask()
        for gi in range(A_GROUPS):
            wm = jnp.where(mask, ws_ref[gi], 0.0).astype(BF16)
            bias = bst_ref[:, gi:gi + 1]
            cs = slice(gi * A_GROUP_DIM, (gi + 1) * A_GROUP_DIM)
            for n in range(nblk):
                rs = slice(n * GMLP_BLOCK, (n + 1) * GMLP_BLOCK)
                sv = _dot(wm, vb[rs, cs]) + bias
                gated_scr[rs, cs] = (u[rs, cs] * sv).astype(BF16)
        h_ref[...] = xv + _dot(gated_scr[...], wout_ref[...])

    return _call(
        "a_mix_fwd", body, (t // TM,),
        [_row(D_MODEL), _res((1, D_MODEL)), _res((N_DEV, D_MODEL, FF_SLOT)), _res((1, GATE_DIM)),
         _res((1, GATE_DIM)), _res((A_GROUPS, GMLP_BLOCK, GMLP_BLOCK)), _res((GMLP_BLOCK, A_GROUPS)),
         _res((GATE_DIM, D_MODEL))],
        [_row(D_MODEL), _row(2 * GATE_DIM), _row(GATE_DIM)],
        [_sds((t, D_MODEL), F32), _sds((t, 2 * GATE_DIM), F32), _sds((t, GATE_DIM), BF16)],
        (x, g, w_in, ln_g, ln_b, w_s, b_st, w_out), comm=comm)


MLP_W_SPECS = (_res((N_DEV, D_MODEL, FF_SLOT)), _res((N_DEV, FF_SLOT, D_MODEL)))


def _mlp_fwd(h, g, w1, w2, layer, comm=None):
    t = h.shape[0]

    def body(h_ref, g_ref, w1_ref, w2_ref, o_ref, a_ref):
        hv = h_ref[...]
        hb = _rms_fwd(hv, g_ref[...])[0].astype(BF16)
        o_ref[...] = hv
        for d in range(N_DEV):
            a = _dot(hb, w1_ref[d])
            a_ref[:, d * FF_SLOT:(d + 1) * FF_SLOT] = a
            r = jnp.maximum(a, 0.0)
            o_ref[...] += _dot((r * r).astype(BF16), w2_ref[d])

    return _call(
        f"mlp_fwd_{layer}", body, (t // TM,), [_row(D_MODEL), _res((1, D_MODEL)), *MLP_W_SPECS],
        [_row(D_MODEL), _row(D_FF)], [_sds((t, D_MODEL), F32), _sds((t, D_FF), F32)], (h, g, w1, w2), comm=comm)


def _mlp_fwd_loss(h, g, w1, w2, final_g, target):
    t = h.shape[0]

    def body(h_ref, g_ref, w1_ref, w2_ref, fg_ref, t_ref, a_ref, loss_ref, dh_ref, dg_ref):
        hv = h_ref[...]
        hb = _rms_fwd(hv, g_ref[...])[0].astype(BF16)
        out = hv
        for d in range(N_DEV):
            a = _dot(hb, w1_ref[d])
            a_ref[:, d * FF_SLOT:(d + 1) * FF_SLOT] = a
            r = jnp.maximum(a, 0.0)
            out = out + _dot((r * r).astype(BF16), w2_ref[d])
        y, xhat, rstd = _rms_fwd(out, fg_ref[...])
        err = y - t_ref[...]
        part = 0.5 * jnp.sum(jnp.mean(err * err, axis=-1, keepdims=True), axis=0, keepdims=True)
        dx, dg = _rms_bwd(err * (1.0 / D_MODEL), xhat, rstd, fg_ref[...])
        dh_ref[...] = dx
        _acc(dg_ref, dg)
        _acc(loss_ref, part)

    return _call(
        "mlp_fwd_loss", body, (t // TM,),
        [_row(D_MODEL), _res((1, D_MODEL)), *MLP_W_SPECS, _res((1, D_MODEL)), _row(D_MODEL)],
        [_row(D_FF), _const((1, 1)), _row(D_MODEL), _const((1, D_MODEL))],
        [_sds((t, D_FF), F32), _sds((1, 1), F32), _sds((t, D_MODEL), F32), _sds((1, D_MODEL), F32)],
        (h, g, w1, w2, final_g, target))[0]


KVQ_W_SPECS = (_res((1, D_MODEL)), _res((D_MODEL, KV_LORA + QK_ROPE)), _res((1, KV_LORA)),
               _res((B_HEADS, KV_LORA, QK_NOPE + V_HEAD)), _res((1, D_MODEL)), _res((D_MODEL, Q_LORA)),
               _res((1, Q_LORA)), _res((B_HEADS, Q_LORA, QK_NOPE + QK_ROPE)))


def _kvq_fwd(h, pos, inv_freq, kvq_w):
    t = h.shape[0]
    half = QK_ROPE // 2

    def body(h_ref, pos_ref, invf_ref, srcg_ref, wkva_ref, kvag_ref, wkvb_ref, mixg_ref, wqa_ref, qg_ref, wqb_ref,
             ckv_ref, k_ref, v_ref, cqpre_ref, q_ref, cos_ref, sin_ref):
        hv = h_ref[...]
        xhat = hv * lax.rsqrt(jnp.mean(hv * hv, axis=-1, keepdims=True) + EPS)
        ang = pos_ref[...].astype(F32) * invf_ref[...]
        cos, sin = jnp.cos(ang), jnp.sin(ang)
        cos_ref[...] = cos
        sin_ref[...] = sin
        ckv = _dot((xhat * srcg_ref[...]).astype(BF16), wkva_ref[...])
        ckv_ref[...] = ckv
        cb = _rms_fwd(ckv[:, :KV_LORA], kvag_ref[...])[0].astype(BF16)
        kpe = _rope(ckv[:, KV_LORA:], cos, sin).astype(BF16)
        for hd in range(B_HEADS):
            kv = _dot(cb, wkvb_ref[hd])
            k_ref[hd, :, 0:QK_NOPE] = kv[:, :QK_NOPE].astype(BF16)
            k_ref[hd, :, QK_NOPE:] = kpe
            v_ref[hd] = kv[:, QK_NOPE:].astype(BF16)
        cqpre = _dot((xhat * mixg_ref[...]).astype(BF16), wqa_ref[...])
        cqpre_ref[...] = cqpre
        cqb = _rms_fwd(cqpre, qg_ref[...])[0].astype(BF16)
        for hd in range(B_HEADS):
            q = _dot(cqb, wqb_ref[hd])
            q_ref[hd, :, 0:QK_NOPE] = q[:, :QK_NOPE].astype(BF16)
            q_ref[hd, :, QK_NOPE:] = _rope(q[:, QK_NOPE:], cos, sin).astype(BF16)

    return _call(
        "kvq_fwd", body, (t // TM,), [_row(D_MODEL), _row(1), _res((1, half)), *KVQ_W_SPECS],
        [_row(KV_LORA + QK_ROPE), _heads(QK_NOPE + QK_ROPE), _heads(V_HEAD), _row(Q_LORA),
         _heads(QK_NOPE + QK_ROPE), _row(half), _row(half)],
        [_sds((t, KV_LORA + QK_ROPE), F32), _sds((B_HEADS, t, QK_NOPE + QK_ROPE), BF16),
         _sds((B_HEADS, t, V_HEAD), BF16), _sds((t, Q_LORA), F32), _sds((B_HEADS, t, QK_NOPE + QK_ROPE), BF16),
         _sds((t, half), F32), _sds((t, half), F32)],
        (h, pos, inv_freq, *kvq_w))[0]


def _softmax_rows(q, k_ref, k):
    past, upto = k * TM, (k + 1) * TM
    s = _dot_nt(q, k_ref[0:upto, :])
    own = jnp.where(_att_mask(0, TM, TM), s[:, past:], jnp.finfo(F32).min)
    s = own if k == 0 else jnp.concatenate([s[:, :past], own], axis=1)
    e = jnp.exp2((s - jnp.max(s, axis=-1, keepdims=True)) * (ATT_SCALE * LOG2_E))
    return e * (1.0 / jnp.sum(e, axis=-1, keepdims=True))


def _for_my_tile(i, nq, fn):
    for k in range(nq):
        @pl.when(i == k)
        def _(k=k):
            fn(k)


def _attn_fwd(h, q, k, v, w_o, comm=None):
    t = h.shape[0]
    nq, hps = t // TM, HEADS_PER_STEP

    def body(h_ref, q_ref, k_ref, v_ref, wo_ref, o_ref, att_ref):
        i, pair = pl.program_id(0), pl.program_id(1)

        @pl.when(pair == 0)
        def _():
            o_ref[...] = h_ref[...]

        def tile(kt):
            proj = None
            for j in range(hps):
                hd = pair * hps + j
                p = _softmax_rows(q_ref[j], k_ref.at[hd], kt)
                ob = _dot(p.astype(BF16), v_ref[hd, 0:(kt + 1) * TM, :]).astype(BF16)
                att_ref[j] = ob
                proj = _dot(ob, wo_ref[hd]) if proj is None else proj + _dot(ob, wo_ref[hd])
            o_ref[...] += proj

        _for_my_tile(i, nq, tile)

    def per_head(d):
        return pl.BlockSpec((hps, TM, d), lambda i, pair: (pair, i, 0))

    def resident(shape):
        zeros = (0,) * len(shape)
        return pl.BlockSpec(shape, lambda i, pair: zeros, pipeline_mode=pl.Buffered(1))

    tile_spec = pl.BlockSpec((TM, D_MODEL), lambda i, pair: (i, 0))
    return _call(
        "attn_fwd", body, (nq, B_HEADS // hps),
        [tile_spec, per_head(QK_NOPE + QK_ROPE), resident((B_HEADS, t, QK_NOPE + QK_ROPE)),
         resident((B_HEADS, t, V_HEAD)), resident((B_HEADS, V_HEAD, D_MODEL))],
        [tile_spec, per_head(V_HEAD)], [_sds((t, D_MODEL), F32), _sds((B_HEADS, t, V_HEAD), BF16)],
        (h, q, k, v, w_o), comm=comm)


def _mlp_bwd(h, a, dho, g, w1, w2, layer, after=()):
    t = h.shape[0]

    def body(h_ref, a_ref, dho_ref, g_ref, w1_ref, w2_ref, dhi_ref, dg_ref, hn_ref, f_ref, da_ref, dhib_ref):
        gv = g_ref[...]
        y, xhat, rstd = _rms_fwd(h_ref[...], gv)
        hn_ref[...] = y.astype(BF16)
        dho_v = dho_ref[...]
        dhob = dho_v.astype(BF16)
        dhn = jnp.zeros((TM, D_MODEL), F32)
        for d in range(N_DEV):
            cs = slice(d * FF_SLOT, (d + 1) * FF_SLOT)
            r = jnp.maximum(a_ref[:, cs], 0.0)
            f_ref[:, cs] = (r * r).astype(BF16)
            da = (_dot_nt(dhob, w2_ref[d]) * (2.0 * r)).astype(BF16)
            da_ref[:, cs] = da
            dhn = dhn + _dot_nt(da, w1_ref[d])
        dx, dg = _rms_bwd(dhn, xhat, rstd, gv)
        dhi = dho_v + dx
        dhi_ref[...] = dhi
        dhib_ref[...] = dhi.astype(BF16)
        _acc(dg_ref, dg)

    return _call(
        f"mlp_bwd_{layer}", body, (t // TM,),
        [_row(D_MODEL), _row(D_FF), _row(D_MODEL), _res((1, D_MODEL)), *MLP_W_SPECS],
        [_row(D_MODEL), _const((1, D_MODEL)), _row(D_MODEL), _row(D_FF), _row(D_FF), _row(D_MODEL)],
        [_sds((t, D_MODEL), F32), _sds((1, D_MODEL), F32), _sds((t, D_MODEL), BF16), _sds((t, D_FF), BF16),
         _sds((t, D_FF), BF16), _sds((t, D_MODEL), BF16)],
        (h, a, dho, g, w1, w2), after=after)


def _attn_bwd(dh, q, k, v, w_o, cos, sin, after=()):
    t = dh.shape[0]
    half, hps = QK_ROPE // 2, HEADS_PER_STEP

    def body(dh_ref, q_ref, k_ref, v_ref, wo_ref, cos_ref, sin_ref, dq_ref, dk_ref, dv_ref):
        i = pl.program_id(1)

        @pl.when(i == 0)
        def _():
            dk_ref[...] = jnp.zeros_like(dk_ref)
            dv_ref[...] = jnp.zeros_like(dv_ref)

        def tile(kt):
            keys = slice(0, (kt + 1) * TM)
            for j in range(hps):
                qj = q_ref[j]
                do = _dot_nt(dh_ref[kt * TM:(kt + 1) * TM, :], wo_ref[j]).astype(BF16)
                p = _softmax_rows(qj, k_ref.at[j], kt)
                dp = _dot_nt(do, v_ref[j, keys, :])
                ds = (p * (dp - jnp.sum(p * dp, axis=-1, keepdims=True)) * ATT_SCALE).astype(BF16)
                dq = _dot(ds, k_ref[j, keys, :])
                dq_ref[j, :, 0:QK_NOPE] = dq[:, :QK_NOPE].astype(BF16)
                dq_ref[j, :, QK_NOPE:] = _rope(dq[:, QK_NOPE:], cos_ref[...], -sin_ref[...]).astype(BF16)
                dk_ref[j, keys, :] += _dot_tn(ds, qj)
                dv_ref[j, keys, :] += _dot_tn(p.astype(BF16), do)

        _for_my_tile(i, t // TM, tile)

    def per_pair(rows, d, tiled):
        return pl.BlockSpec((hps, rows, d), (lambda pair, i: (pair, i, 0)) if tiled else (lambda pair, i: (pair, 0, 0)))

    def tile(d):
        return pl.BlockSpec((TM, d), lambda pair, i: (i, 0))

    return _call(
        "attn_bwd", body, (B_HEADS // hps, t // TM),
        [pl.BlockSpec((t, D_MODEL), lambda pair, i: (0, 0), pipeline_mode=pl.Buffered(1)),
         per_pair(TM, QK_NOPE + QK_ROPE, True), per_pair(t, QK_NOPE + QK_ROPE, False), per_pair(t, V_HEAD, False),
         per_pair(V_HEAD, D_MODEL, False), tile(half), tile(half)],
        [per_pair(TM, QK_NOPE + QK_ROPE, True), per_pair(t, QK_NOPE + QK_ROPE, False), per_pair(t, V_HEAD, False)],
        [_sds((B_HEADS, t, QK_NOPE + QK_ROPE), BF16), _sds((B_HEADS, t, QK_NOPE + QK_ROPE), F32),
         _sds((B_HEADS, t, V_HEAD), F32)],
        (dh, q, k, v, w_o, cos, sin), after=after)


def _kvq_bwd(h, dh, ckv, cqpre, dq, dk, dv, cos, sin, kvq_w, after=()):
    t = h.shape[0]
    half = QK_ROPE // 2

    def body(h_ref, dh_ref, ckv_ref, cqpre_ref, dq_ref, dk_ref, dv_ref, cos_ref, sin_ref,
             srcg_ref, wkva_ref, kvag_ref, wkvb_ref, mixg_ref, wqa_ref, qg_ref, wqb_ref,
             dhi_ref, hq_ref, hk_ref, cq_ref, dcqpre_ref, c_ref, dkv_ref, dckv_ref,
             dmixg_ref, dsrcg_ref, dqg_ref, dkvag_ref):
        hv = h_ref[...]
        rstd = lax.rsqrt(jnp.mean(hv * hv, axis=-1, keepdims=True) + EPS)
        xhat = hv * rstd
        mixg, srcg, qg, kvag = mixg_ref[...], srcg_ref[...], qg_ref[...], kvag_ref[...]
        hq_ref[...] = (xhat * mixg).astype(BF16)
        hk_ref[...] = (xhat * srcg).astype(BF16)
        cq, cqhat, crstd = _rms_fwd(cqpre_ref[...], qg)
        cq_ref[...] = cq.astype(BF16)
        dcq = jnp.zeros((TM, Q_LORA), F32)
        for hd in range(B_HEADS):
            dcq = dcq + _dot_nt(dq_ref[hd], wqb_ref[hd])
        dcqpre, dqg = _rms_bwd(dcq, cqhat, crstd, qg)
        dcqpre_b = dcqpre.astype(BF16)
        dcqpre_ref[...] = dcqpre_b
        dxq, dmixg = _rms_bwd(_dot_nt(dcqpre_b, wqa_ref[...]), xhat, rstd, mixg)
        ckv = ckv_ref[...]
        c, chat, krstd = _rms_fwd(ckv[:, :KV_LORA], kvag)
        c_ref[...] = c.astype(BF16)
        dc = jnp.zeros((TM, KV_LORA), F32)
        dkpe = jnp.zeros((TM, QK_ROPE), F32)
        for hd in range(B_HEADS):
            dkv = jnp.concatenate([dk_ref[hd, :, 0:QK_NOPE], dv_ref[hd]], axis=-1).astype(BF16)
            dkv_ref[hd] = dkv
            dc = dc + _dot_nt(dkv, wkvb_ref[hd])
            dkpe = dkpe + dk_ref[hd, :, QK_NOPE:]
        dlat, dkvag = _rms_bwd(dc, chat, krstd, kvag)
        dpe = _rope(dkpe, cos_ref[...], -sin_ref[...])
        dckv_b = jnp.concatenate([dlat, dpe], axis=-1).astype(BF16)
        dckv_ref[...] = dckv_b
        dxk, dsrcg = _rms_bwd(_dot_nt(dckv_b, wkva_ref[...]), xhat, rstd, srcg)
        dhi_ref[...] = dh_ref[...] + dxq + dxk
        _acc(dmixg_ref, dmixg)
        _acc(dsrcg_ref, dsrcg)
        _acc(dqg_ref, dqg)
        _acc(dkvag_ref, dkvag)

    return _call(
        "kvq_bwd", body, (t // TM,),
        [_row(D_MODEL), _row(D_MODEL), _row(KV_LORA + QK_ROPE), _row(Q_LORA), _heads(QK_NOPE + QK_ROPE),
         _heads(QK_NOPE + QK_ROPE), _heads(V_HEAD), _row(half), _row(half), *KVQ_W_SPECS],
        [_row(D_MODEL), _row(D_MODEL), _row(D_MODEL), _row(Q_LORA), _row(Q_LORA), _row(KV_LORA),
         _heads(QK_NOPE + V_HEAD), _row(KV_LORA + QK_ROPE),
         _const((1, D_MODEL)), _const((1, D_MODEL)), _const((1, Q_LORA)), _const((1, KV_LORA))],
        [_sds((t, D_MODEL), F32), _sds((t, D_MODEL), BF16), _sds((t, D_MODEL), BF16), _sds((t, Q_LORA), BF16),
         _sds((t, Q_LORA), BF16), _sds((t, KV_LORA), BF16), _sds((B_HEADS, t, QK_NOPE + V_HEAD), BF16),
         _sds((t, KV_LORA + QK_ROPE), BF16),
         _sds((1, D_MODEL), F32), _sds((1, D_MODEL), F32), _sds((1, Q_LORA), F32), _sds((1, KV_LORA), F32)],
        (h, dh, ckv, cqpre, dq, dk, dv, cos, sin, *kvq_w), after=after)[0]


def _a_mix_bwd(x, z, dh, g, w_in, ln_g, ln_b, w_s, b_st, w_out, after=()):
    t = x.shape[0]
    tm = TM_GATE
    nblk = tm // GMLP_BLOCK

    def body(x_ref, z_ref, dh_ref, g_ref, win_ref, lng_ref, lnb_ref, ws_ref, bst_ref, wout_ref,
             dx_ref, hn_ref, dz_ref, dg_ref, dlng_ref, dlnb_ref, dws_ref, dbs_ref, dvn_scr, gelu_grad_v):
        @pl.when(pl.program_id(0) == 0)
        def _():
            dws_ref[...] = jnp.zeros_like(dws_ref)
            dbs_ref[...] = jnp.zeros_like(dbs_ref)

        gv, lng = g_ref[...], lng_ref[...]
        y, xhat, rstd = _rms_fwd(x_ref[...], gv)
        hn_ref[...] = y.astype(BF16)
        dhv = dh_ref[...]
        dgated = _dot_nt(dhv.astype(BF16), wout_ref[...])
        u, gelu_grad_u = _gelu_and_grad(z_ref[:, :GATE_DIM])
        v, gelu_grad_v[...] = _gelu_and_grad(z_ref[:, GATE_DIM:])
        vn, vhat, lrstd = _ln_fwd(v, lng, lnb_ref[...])
        vb = vn.astype(BF16)
        mask = _gate_mask()
        for gi in range(A_GROUPS):
            wm = jnp.where(mask, ws_ref[gi], 0.0).astype(BF16)
            bias = bst_ref[:, gi:gi + 1]
            cs = slice(gi * A_GROUP_DIM, (gi + 1) * A_GROUP_DIM)
            dws = jnp.zeros((GMLP_BLOCK, GMLP_BLOCK), F32)
            dbs = jnp.zeros((GMLP_BLOCK, 1), F32)
            for n in range(nblk):
                rs = slice(n * GMLP_BLOCK, (n + 1) * GMLP_BLOCK)
                sv = _dot(wm, vb[rs, cs]) + bias
                dz_ref[rs, cs] = (dgated[rs, cs] * sv * gelu_grad_u[rs, cs]).astype(BF16)
                dsv = dgated[rs, cs] * u[rs, cs]
                dsvb = dsv.astype(BF16)
                dws = dws + _dot_nt(dsvb, vb[rs, cs])
                dbs = dbs + jnp.sum(dsv, axis=-1, keepdims=True)
                dvn_scr[rs, cs] = _dot_tn(wm, dsvb)
            dws_ref[gi] += jnp.where(mask, dws, 0.0)
            dbs_ref[gi] += dbs
        dvn = dvn_scr[...]
        dvhat = dvn * lng
        dv = lrstd * (dvhat - jnp.mean(dvhat, axis=-1, keepdims=True)
                      - vhat * jnp.mean(dvhat * vhat, axis=-1, keepdims=True))
        dz_ref[:, GATE_DIM:] = (dv * gelu_grad_v[...]).astype(BF16)
        dhn = jnp.zeros((tm, D_MODEL), F32)
        for d in range(N_DEV):
            dhn = dhn + _dot_nt(dz_ref[:, d * FF_SLOT:(d + 1) * FF_SLOT], win_ref[d])
        dx, dg = _rms_bwd(dhn, xhat, rstd, gv)
        dx_ref[...] = dhv + dx
        _acc(dg_ref, dg)
        _acc(dlng_ref, jnp.sum(dvn * vhat, axis=0, keepdims=True))
        _acc(dlnb_ref, jnp.sum(dvn, axis=0, keepdims=True))

    return _call(
        "a_mix_bwd", body, (t // tm,),
        [_row(D_MODEL, tm), _row(2 * GATE_DIM, tm), _row(D_MODEL, tm), _res((1, D_MODEL)),
         _res((N_DEV, D_MODEL, FF_SLOT)), _res((1, GATE_DIM)), _res((1, GATE_DIM)),
         _res((A_GROUPS, GMLP_BLOCK, GMLP_BLOCK)), _res((GMLP_BLOCK, A_GROUPS)), _res((GATE_DIM, D_MODEL))],
        [_row(D_MODEL, tm), _row(D_MODEL, tm), _row(2 * GATE_DIM, tm),
         _const((1, D_MODEL)), _const((1, GATE_DIM)), _const((1, GATE_DIM)),
         _const((A_GROUPS, GMLP_BLOCK, GMLP_BLOCK)), _const((A_GROUPS, GMLP_BLOCK, 1))],
        [_sds((t, D_MODEL), F32), _sds((t, D_MODEL), BF16),
         _sds((t, 2 * GATE_DIM), BF16), _sds((1, D_MODEL), F32), _sds((1, GATE_DIM), F32),
         _sds((1, GATE_DIM), F32), _sds((A_GROUPS, GMLP_BLOCK, GMLP_BLOCK), F32),
         _sds((A_GROUPS, GMLP_BLOCK, 1), F32)],
        (x, z, dh, g, w_in, ln_g, ln_b, w_s, b_st, w_out),
        scratch=[pltpu.VMEM((tm, GATE_DIM), F32), pltpu.VMEM((tm, GATE_DIM), F32)], after=after)


def _wgrad(name, a, b, a_spec, b_spec, m, n, after=()):
    def body(a_ref, b_ref, o_ref):
        o_ref[0] = _dot_tn(a_ref[...].astype(BF16), b_ref[...].astype(BF16)).astype(BF16)

    return _call(name, body, (N_DEV,), [a_spec, b_spec], [pl.BlockSpec((1, m, n), lambda d: (d, 0, 0))],
                 [_sds((N_DEV, m, n), BF16)], (a, b), after=after)[0][0]


def _full(t, d):
    return pl.BlockSpec((t, d), lambda i: (0, 0), pipeline_mode=pl.Buffered(1))


def _cols(t, d):
    return pl.BlockSpec((t, d), lambda i: (0, i))


def _head(t, d):
    return pl.BlockSpec((None, t, d), lambda i: (i, 0, 0))


def _local_step(x, pos, target, inv_freq, wg, sm, shards=None):
    t = x.shape[0]
    wg = dict(wg)
    dist = shards is not None
    mix_g = [sm["norm_mix_g"][l:l + 1] for l in range(2)]
    mlp_g = [sm["norm_mlp_g"][l:l + 1] for l in range(2)]

    ids = iter(range(2, 2 + 9))

    def gather(names):
        if dist:
            got = _by_sequencer("gather_" + names[0], _gather_comm([shards[k] for k in names]),
                                SIBLING_AND_NEIGHBOURS, next(ids))
            wg.update(zip(names, got))

    def send(name, names):
        if dist:
            comm = _exchange_comm(grads=[g[k] for k in names])
            g.update(zip(names, _by_sequencer("exchange_" + name, comm, EVERYONE, next(ids))))

    def send_sums(name, names, meanwhile):
        if not dist:
            meanwhile()
            return ()
        grads = [g[k] for k in names]
        landed = _by_sequencer("pair_exchange_" + name, _pair_exchange_comm(grads), (1,), next(ids))
        sums = _pair_add("pair_add_" + name, grads, landed, after=meanwhile())
        g.update(zip(names, _by_sequencer("exchange_" + name, _chip_exchange_comm(sums), OTHER_CHIPS, next(ids))))
        return sums

    def a_args():
        return (wg["a_w_in"], wg["a_ln_v_g"], wg["a_ln_v_b"], sm["a_w_s"], sm["a_b_st"], wg["a_w_out"])

    def kvq_w():
        return (sm["kv_src_norm_g"], wg["kv_w_a"], sm["kv_a_norm_g"], wg["kv_w_b"], mix_g[1], wg["b_w_q_a"],
                sm["b_q_norm_g"], wg["b_w_q_b"])

    gather(("mlp_w1_0", "mlp_w2_0"))
    (h1, z, gated), _ = _a_mix_fwd(x, mix_g[0], *a_args())
    gather(("kv_w_a", "kv_w_b", "b_w_q_a", "b_w_q_b", "b_w_o"))
    (h2, a0), _ = _mlp_fwd(h1, mlp_g[0], wg["mlp_w1_0"], wg["mlp_w2_0"], 0)
    if dist:
        wg["b_w_q_a"] = wg["b_w_q_a"].reshape(D_MODEL, Q_LORA)
        wg["kv_w_a"] = wg["kv_w_a"].reshape(D_MODEL, KV_LORA + QK_ROPE)
    gather(("mlp_w1_1", "mlp_w2_1"))
    ckv, k, v, cqpre, q, cos, sin = _kvq_fwd(h2, pos, inv_freq, kvq_w())
    (h3, att), _ = _attn_fwd(h2, q, k, v, wg["b_w_o"])
    a1, loss, dh4, d_final_g = _mlp_fwd_loss(h3, mlp_g[1], wg["mlp_w1_1"], wg["mlp_w2_1"], sm["final_norm_g"], target)

    g = {}
    (dh3, d_mlp_g1, hn, f, da, dh3_b), _ = _mlp_bwd(h3, a1, dh4, mlp_g[1], wg["mlp_w1_1"], wg["mlp_w2_1"], 1)
    (dq, dk, dv), _ = _attn_bwd(dh3_b, q, k, v, wg["b_w_o"], cos, sin)
    g["mlp_w1_1"] = _wgrad("wgrad_w1_1", hn, da, _full(t, D_MODEL), _cols(t, FF_SLOT), D_MODEL, FF_SLOT, after=[dq])
    g["mlp_w2_1"] = _wgrad("wgrad_w2_1", f, dh4, _cols(t, FF_SLOT), _full(t, D_MODEL), FF_SLOT, D_MODEL)

    def wgrad_w_o():
        g["b_w_o"] = _wgrad("wgrad_w_o", att, dh3_b, _head(t, V_HEAD), _full(t, D_MODEL), V_HEAD, D_MODEL)
        return [g["b_w_o"]]

    sums = send_sums("mlp_1", ("mlp_w1_1", "mlp_w2_1"), wgrad_w_o)
    (dh2, hq, hk, cq, dcqpre, c, dkv, dckv, d_mix_g1, d_src_g, d_q_g, d_kv_a_g) = _kvq_bwd(
        h2, dh3, ckv, cqpre, dq, dk, dv, cos, sin, kvq_w(), after=sums)
    g["b_w_q_a"] = _wgrad("wgrad_w_q_a", hq, dcqpre, _cols(t, D_MODEL // N_DEV), _full(t, Q_LORA),
                          D_MODEL // N_DEV, Q_LORA)
    g["b_w_q_b"] = _wgrad("wgrad_w_q_b", cq, dq, _full(t, Q_LORA), _head(t, QK_NOPE + QK_ROPE),
                          Q_LORA, QK_NOPE + QK_ROPE)
    g["kv_w_a"] = _wgrad("wgrad_kv_w_a", hk, dckv, _cols(t, D_MODEL // N_DEV), _full(t, KV_LORA + QK_ROPE),
                         D_MODEL // N_DEV, KV_LORA + QK_ROPE)
    g["kv_w_b"] = _wgrad("wgrad_kv_w_b", c, dkv, _full(t, KV_LORA), _head(t, QK_NOPE + V_HEAD),
                         KV_LORA, QK_NOPE + V_HEAD)
    qkv = ("b_w_o", "b_w_q_a", "b_w_q_b", "kv_w_a", "kv_w_b")
    landed = [g[k] for k in qkv]
    send("qkv", qkv)
    (dh1, d_mlp_g0, hn, f, da, dh1_b), _ = _mlp_bwd(h1, a0, dh2, mlp_g[0], wg["mlp_w1_0"], wg["mlp_w2_0"], 0,
                                                    after=landed if dist else ())
    landed = [g["mlp_w1_1"], g["mlp_w2_1"]] if dist else ()
    g["mlp_w1_0"] = _wgrad("wgrad_w1_0", hn, da, _full(t, D_MODEL), _cols(t, FF_SLOT), D_MODEL, FF_SLOT, after=landed)
    g["mlp_w2_0"] = _wgrad("wgrad_w2_0", f, dh2, _cols(t, FF_SLOT), _full(t, D_MODEL), FF_SLOT, D_MODEL)

    def wgrad_a_w_out():
        g["a_w_out"] = _wgrad("wgrad_a_w_out", gated, dh1_b, _cols(t, GATE_DIM // N_DEV), _full(t, D_MODEL),
                              GATE_DIM // N_DEV, D_MODEL)
        return [g["a_w_out"]] + [g[k] for k in qkv]

    sums = send_sums("mlp_0", ("mlp_w1_0", "mlp_w2_0"), wgrad_a_w_out)
    (dx, hn, dz, d_mix_g0, d_ln_g, d_ln_b, d_ws, d_bs), _ = _a_mix_bwd(x, z, dh1, mix_g[0], *a_args(), after=sums)
    small = {
        "norm_mix_g": jnp.concatenate([d_mix_g0, d_mix_g1], axis=0),
        "norm_mlp_g": jnp.concatenate([d_mlp_g0, d_mlp_g1], axis=0),
        "a_ln_v_g": d_ln_g.reshape(N_DEV, GATE_DIM // N_DEV),
        "a_ln_v_b": d_ln_b.reshape(N_DEV, GATE_DIM // N_DEV),
        "a_w_s": d_ws.astype(BF16) if dist else d_ws,
        "a_b_s": d_bs.reshape(A_GROUPS, GMLP_BLOCK),
        "b_q_norm_g": d_q_g,
        "kv_src_norm_g": d_src_g,
        "kv_a_norm_g": d_kv_a_g,
        "final_norm_g": d_final_g,
    }
    if dist:
        parts = [small[k].reshape((1,) + small[k].shape) for k in SMALL] + [loss.reshape(1, 1, 1)]
        got = _by_sequencer("gather_small", _exchange_comm(parts=parts), EVERYONE, next(ids))
        small, loss = dict(zip(SMALL, got)), got[-1]
    g["a_w_in"] = _wgrad("wgrad_a_w_in", hn, dz, _full(t, D_MODEL), _cols(t, FF_SLOT), D_MODEL, FF_SLOT)
    return loss, dx, g, small


def _adamw(w, g, m, v):
    m = ADAM_B1 * m + (1.0 - ADAM_B1) * g
    v = ADAM_B2 * v + (1.0 - ADAM_B2) * (g * g)
    m_hat = m / (1.0 - ADAM_B1 ** ADAM_STEP)
    v_hat = v / (1.0 - ADAM_B2 ** ADAM_STEP)
    return -ADAM_LR * (m_hat / (jnp.sqrt(v_hat) + ADAM_EPS) + ADAM_WD * w), m, v


def _sum_in_device_order(r_ref):
    g = r_ref[0].astype(F32)
    for j in range(1, r_ref.shape[0]):
        g = g + r_ref[j].astype(F32)
    return g


def _adamw_sharded(name, recvs, w, m, v, comm=None):
    layers, r, c = w.shape
    tr = math.gcd(r, 512)
    flat = [a for per_layer in recvs for a in per_layer]

    def body(*refs):
        r_refs, (w_ref, m_ref, v_ref) = refs[:len(flat)], refs[len(flat):len(flat) + 3]
        g_ref, d_ref, nm_ref, nv_ref = refs[-4:]
        layer = pl.program_id(0)
        g, pos = None, 0
        for li, per_layer in enumerate(recvs):
            total = None
            for ref in r_refs[pos:pos + len(per_layer)]:
                part = _sum_in_device_order(ref)
                total = part if total is None else total + part
            pos += len(per_layer)
            g = total if g is None else jnp.where(layer == li, total, g)
        g_ref[...] = g
        d_ref[...], nm_ref[...], nv_ref[...] = _adamw(w_ref[...], g, m_ref[...], v_ref[...])

    blk = pl.BlockSpec((None, tr, c), lambda l, i: (l, i, 0))
    return _call(name, body, (layers, r // tr),
                 [pl.BlockSpec((a.shape[0], tr, c), lambda l, i: (0, i, 0)) for a in flat] + [blk] * 3,
                 [blk] * 4, [_sds(w.shape, F32)] * 4, (*flat, w, m, v), comm=comm)


def _adamw_small(recvs, ws, ms, vs, own_row, losses):
    n = len(recvs)

    def body(*refs):
        r_refs, w_refs, m_refs, v_refs = (refs[i * n:(i + 1) * n] for i in range(4))
        outs, scr = refs[4 * n + 1:8 * n + 2], refs[8 * n + 2:]
        outs[-1][...] = _sum_in_device_order(refs[4 * n])
        me = _my_place()[3]
        for a in range(n):
            g = _sum_in_device_order(r_refs[a])
            if own_row[a]:
                scr[0][...] = g
                g = scr[0][pl.ds(me, 1), :]
            g_ref, d_ref, nm_ref, nv_ref = outs[4 * a:4 * a + 4]
            g_ref[...] = g
            d_ref[...], nm_ref[...], nv_ref[...] = _adamw(w_refs[a][...], g, m_refs[a][...], v_refs[a][...])

    out_shape = []
    for w in ws:
        out_shape += [_sds(w.shape, F32)] * 4
    return pl.pallas_call(
        body, name="adamw_small", in_specs=[VMEM] * (4 * n + 1), out_specs=[VMEM] * (4 * n + 1),
        out_shape=out_shape + [_sds((1, 1), F32)], scratch_shapes=[pltpu.VMEM((N_DEV, GATE_DIM // N_DEV), F32)],
    )(*recvs, *ws, *ms, *vs, losses)


BIG = ("a_w_in", "a_w_out", "b_w_q_a", "b_w_q_b", "b_w_o", "kv_w_a", "kv_w_b", "mlp_w1", "mlp_w2")
SMALL = ("norm_mix_g", "norm_mlp_g", "a_ln_v_g", "a_ln_v_b", "a_w_s", "a_b_s", "b_q_norm_g", "kv_src_norm_g",
         "kv_a_norm_g", "final_norm_g")
WEIGHTS = ("norm_mix_g", "norm_mlp_g", "a_w_in", "a_ln_v_g", "a_ln_v_b", "a_w_s", "a_b_s", "a_w_out", "b_w_q_a",
           "b_q_norm_g", "b_w_q_b", "b_w_o", "kv_src_norm_g", "kv_w_a", "kv_a_norm_g", "kv_w_b", "mlp_w1", "mlp_w2",
           "final_norm_g")


def _two_d(name, a):
    if name in ("a_w_s", "a_b_s"):
        return a.reshape(a.shape[1:])
    return a.reshape(1, -1) if a.ndim == 1 else a


def _three_d(a):
    return a if a.ndim == 3 else a.reshape((1,) + a.shape)


def kernel(x, positions, norm_mix_g, norm_mlp_g, a_w_in, a_ln_v_g, a_ln_v_b, a_w_s, a_b_s, a_w_out, b_w_q_a, b_q_norm_g, b_w_q_b, b_w_o, kv_src_norm_g, kv_w_a, kv_a_norm_g, kv_w_b, mlp_w1, mlp_w2, final_norm_g, loss_target, m_norm_mix_g, m_norm_mlp_g, m_a_w_in, m_a_ln_v_g, m_a_ln_v_b, m_a_w_s, m_a_b_s, m_a_w_out, m_b_w_q_a, m_b_q_norm_g, m_b_w_q_b, m_b_w_o, m_kv_src_norm_g, m_kv_w_a, m_kv_a_norm_g, m_kv_w_b, m_mlp_w1, m_mlp_w2, m_final_norm_g, v_norm_mix_g, v_norm_mlp_g, v_a_w_in, v_a_ln_v_g, v_a_ln_v_b, v_a_w_s, v_a_b_s, v_a_w_out, v_b_w_q_a, v_b_q_norm_g, v_b_w_q_b, v_b_w_o, v_kv_src_norm_g, v_kv_w_a, v_kv_a_norm_g, v_kv_w_b, v_mlp_w1, v_mlp_w2, v_final_norm_g):
    w = dict(norm_mix_g=norm_mix_g, norm_mlp_g=norm_mlp_g, a_w_in=a_w_in, a_ln_v_g=a_ln_v_g, a_ln_v_b=a_ln_v_b,
             a_w_s=a_w_s, a_b_s=a_b_s, a_w_out=a_w_out, b_w_q_a=b_w_q_a, b_q_norm_g=b_q_norm_g, b_w_q_b=b_w_q_b,
             b_w_o=b_w_o, kv_src_norm_g=kv_src_norm_g, kv_w_a=kv_w_a, kv_a_norm_g=kv_a_norm_g, kv_w_b=kv_w_b,
             mlp_w1=mlp_w1, mlp_w2=mlp_w2, final_norm_g=final_norm_g)
    m = dict(norm_mix_g=m_norm_mix_g, norm_mlp_g=m_norm_mlp_g, a_w_in=m_a_w_in, a_ln_v_g=m_a_ln_v_g,
             a_ln_v_b=m_a_ln_v_b, a_w_s=m_a_w_s, a_b_s=m_a_b_s, a_w_out=m_a_w_out, b_w_q_a=m_b_w_q_a,
             b_q_norm_g=m_b_q_norm_g, b_w_q_b=m_b_w_q_b, b_w_o=m_b_w_o, kv_src_norm_g=m_kv_src_norm_g,
             kv_w_a=m_kv_w_a, kv_a_norm_g=m_kv_a_norm_g, kv_w_b=m_kv_w_b, mlp_w1=m_mlp_w1, mlp_w2=m_mlp_w2,
             final_norm_g=m_final_norm_g)
    v = dict(norm_mix_g=v_norm_mix_g, norm_mlp_g=v_norm_mlp_g, a_w_in=v_a_w_in, a_ln_v_g=v_a_ln_v_g,
             a_ln_v_b=v_a_ln_v_b, a_w_s=v_a_w_s, a_b_s=v_a_b_s, a_w_out=v_a_w_out, b_w_q_a=v_b_w_q_a,
             b_q_norm_g=v_b_q_norm_g, b_w_q_b=v_b_w_q_b, b_w_o=v_b_w_o, kv_src_norm_g=v_kv_src_norm_g,
             kv_w_a=v_kv_w_a, kv_a_norm_g=v_kv_a_norm_g, kv_w_b=v_kv_w_b, mlp_w1=v_mlp_w1, mlp_w2=v_mlp_w2,
             final_norm_g=v_final_norm_g)
    t = x.shape[1]

    first = ("a_w_in", "a_w_out", "a_ln_v_g", "a_ln_v_b")
    later = ("mlp_w1_0", "mlp_w2_0", "mlp_w1_1", "mlp_w2_1", "kv_w_a", "kv_w_b", "b_w_q_a", "b_w_q_b", "b_w_o")
    blocks = {k: _three_d(w[k]) for k in BIG if not k.startswith("mlp")}
    for k in ("mlp_w1", "mlp_w2"):
        blocks[k + "_0"], blocks[k + "_1"] = w[k][0:1], w[k][1:2]
    got, casts = _gather_first([blocks[k] if k in blocks else w[k] for k in first], [blocks[k] for k in later])
    wg = dict(zip(first, got))
    wg["a_w_out"] = wg["a_w_out"].reshape(GATE_DIM, D_MODEL)
    wg["a_ln_v_g"] = wg["a_ln_v_g"].reshape(1, GATE_DIM)
    wg["a_ln_v_b"] = wg["a_ln_v_b"].reshape(1, GATE_DIM)
    shards = dict(zip(later, casts))

    sm = {k: _two_d(k, w[k]) for k in SMALL if k not in ("a_ln_v_g", "a_ln_v_b")}
    sm["a_b_st"] = sm["a_b_s"].T
    inv_freq = (ROPE_THETA ** (-jnp.arange(0, QK_ROPE, 2, dtype=F32) / QK_ROPE)).reshape(1, QK_ROPE // 2)

    losses, dx, g, small = _local_step(x[0], positions.reshape(t, 1), loss_target[0], inv_freq, wg, sm, shards)

    names = ("a_w_in", "a_w_out")
    sums = _pair_reduce("pair_reduce_a", [g[k] for k in names], after=[g["mlp_w1_0"], g["mlp_w2_0"]])
    g.update(zip(names, _by_sequencer("exchange_last", _chip_exchange_comm(sums), OTHER_CHIPS, collective_id=1)))

    out = {}
    for k in BIG:
        recvs = [[g[k + "_0"]], [g[k + "_1"]]] if k.startswith("mlp") else [[g[k]]]
        res, _ = _adamw_sharded("adamw_" + k, recvs, _three_d(w[k]), _three_d(m[k]), _three_d(v[k]))
        out[k] = [o.reshape(w[k].shape) for o in res]
    own_row = [k in ("a_ln_v_g", "a_ln_v_b") for k in SMALL]
    res = _adamw_small([small[k] for k in SMALL], [_two_d(k, w[k]) for k in SMALL], [_two_d(k, m[k]) for k in SMALL],
                       [_two_d(k, v[k]) for k in SMALL], own_row, losses)
    for i, k in enumerate(SMALL):
        out[k] = [o.reshape(w[k].shape) for o in res[4 * i:4 * i + 4]]

    return (res[-1].reshape(()), dx.reshape(x.shape), *[out[k][0] for k in WEIGHTS], *[out[k][1] for k in WEIGHTS],
            *[out[k][2] for k in WEIGHTS], *[out[k][3] for k in WEIGHTS])
```

```python
import math

import jax
import jax.numpy as jnp
from jax import lax
from jax.experimental import pallas as pl
from jax.experimental.pallas import tpu as pltpu
from jax.experimental.pallas import tpu_sc as plsc

F32, BF16 = jnp.float32, jnp.bfloat16
MESH = pl.DeviceIdType.MESH
ANY = pl.BlockSpec(memory_space=pl.ANY)
VMEM = pl.BlockSpec(memory_space=pltpu.VMEM)

N_DEV = 8
D_MODEL = 1024
CHUNK = 64
GMLP_BLOCK = 128
GATE_DIM = 2048
A_GROUPS = 8
A_GROUP_DIM = GATE_DIM // A_GROUPS
B_HEADS = 8
QK_NOPE, QK_ROPE, V_HEAD = 128, 64, 128
Q_LORA, KV_LORA = 384, 256
ROPE_THETA = 10000.0
D_FF = 4096
FF_SLOT = D_FF // N_DEV
EPS = 1e-6
ATT_SCALE = (QK_NOPE + QK_ROPE) ** -0.5

ADAM_LR, ADAM_B1, ADAM_B2, ADAM_EPS, ADAM_WD, ADAM_STEP = 0.001, 0.9, 0.999, 1e-08, 0.01, 10

TM = 256
TM_GATE = 128
VMEM_LIMIT = 56 * 1024 * 1024
INV_SQRT2 = 1.0 / math.sqrt(2.0)
INV_SQRT_2PI = 1.0 / math.sqrt(2.0 * math.pi)
LOG2_E = 1.0 / math.log(2.0)
HEADS_PER_STEP = 2


def _dot(a, b):
    return jnp.dot(a, b, preferred_element_type=F32)


def _dot_nt(a, b):
    return lax.dot_general(a, b, (((1,), (1,)), ((), ())), preferred_element_type=F32)


def _dot_tn(a, b):
    return lax.dot_general(a, b, (((0,), (0,)), ((), ())), preferred_element_type=F32)


def _rms_fwd(x, g):
    rstd = lax.rsqrt(jnp.mean(x * x, axis=-1, keepdims=True) + EPS)
    xhat = x * rstd
    return xhat * g, xhat, rstd


def _rms_bwd(dy, xhat, rstd, g):
    dxhat = dy * g
    dx = rstd * (dxhat - xhat * jnp.mean(dxhat * xhat, axis=-1, keepdims=True))
    return dx, jnp.sum(dy * xhat, axis=0, keepdims=True)


def _ln_fwd(v, g, b):
    mu = jnp.mean(v, axis=-1, keepdims=True)
    vc = v - mu
    rstd = lax.rsqrt(jnp.mean(vc * vc, axis=-1, keepdims=True) + EPS)
    vhat = vc * rstd
    return vhat * g + b, vhat, rstd


def _gelu(x):
    return 0.5 * x * (1.0 + lax.erf(x * INV_SQRT2))


def _gelu_and_grad(x):
    cdf = 0.5 * (1.0 + lax.erf(x * INV_SQRT2))
    return x * cdf, cdf + x * jnp.exp(-0.5 * x * x) * INV_SQRT_2PI


def _rope(x, cos, sin):
    x1, x2 = x[:, :QK_ROPE // 2], x[:, QK_ROPE // 2:]
    return jnp.concatenate([x1 * cos - x2 * sin, x2 * cos + x1 * sin], axis=-1)


def _gate_mask():
    row = lax.broadcasted_iota(jnp.int32, (GMLP_BLOCK, GMLP_BLOCK), 0)
    col = lax.broadcasted_iota(jnp.int32, (GMLP_BLOCK, GMLP_BLOCK), 1)
    return (col < CHUNK) | (row >= CHUNK)


def _att_mask(q0, tq, t):
    q = q0 + lax.broadcasted_iota(jnp.int32, (tq, t), 0)
    k = lax.broadcasted_iota(jnp.int32, (tq, t), 1)
    return jnp.right_shift(k, 6) <= jnp.right_shift(q, 6)


def _res(shape, imap=None):
    zeros = (0,) * len(shape)
    return pl.BlockSpec(shape, imap or (lambda i: zeros), pipeline_mode=pl.Buffered(1))


def _const(shape):
    zeros = (0,) * len(shape)
    return pl.BlockSpec(shape, lambda i: zeros)


def _row(d, tm=TM):
    return pl.BlockSpec((tm, d), lambda i: (i, 0))


def _heads(d):
    return pl.BlockSpec((B_HEADS, TM, d), lambda i: (0, i, 0))


def _sds(shape, dt):
    return jax.ShapeDtypeStruct(shape, dt)


def _acc(ref, val):
    @pl.when(pl.program_id(0) == 0)
    def _():
        ref[...] = jnp.zeros_like(ref)
    ref[...] += val


def _my_place():
    x, y, c = lax.axis_index("x"), lax.axis_index("y"), lax.axis_index("c")
    return x, y, c, 4 * x + 2 * y + c


def _peer(x, y, c, k):
    px = 1 - x if k & 4 else x
    py = 1 - y if k & 2 else y
    pc = 1 - c if k & 1 else c
    return (px, py, pc), 4 * px + 2 * py + pc


CHIPS = (2, 4, 6)


def _splits(ref):
    return len(ref.shape) >= 3 and ref.shape[1] % 32 == 0


def _piece(ref, block, half=None):
    if half is None or not _splits(ref):
        return ref.at[pl.ds(block, 1)]
    rows = ref.shape[1] // 2
    return ref.at[pl.ds(block, 1), pl.ds(half * rows, rows)]


def _gather_copy(sems, a, k, piece, to, src=None):
    return pltpu.make_async_remote_copy(
        src_ref=piece if src is None else src, dst_ref=piece, send_sem=sems[0].at[a, k], recv_sem=sems[1].at[a, k],
        device_id=to, device_id_type=MESH)


def _gather_start(srcs, outs, sems, only=None):
    x, y, c, me = _my_place()
    for a in range(len(srcs)) if only is None else (only,):
        mine = _piece(outs[a], me)
        pltpu.make_async_copy(srcs[a], mine, sems[2].at[a]).start()
        for k, rel in enumerate((1, 4, 2)):
            _gather_copy(sems, a, k, mine, _peer(x, y, c, rel)[0], src=srcs[a]).start()


def _gather_relay(srcs, outs, sems):
    x, y, c, _ = _my_place()
    sib = _peer(x, y, c, 1)[0]
    (xn, xn_i), (yn, yn_i) = _peer(x, y, c, 4), _peer(x, y, c, 2)
    for a in range(len(srcs)):
        out = outs[a]
        _gather_copy(sems, a, 1, _piece(out, xn_i), xn).wait_recv()
        _gather_copy(sems, a, 3, _piece(out, xn_i, 0), yn).start()
        _gather_copy(sems, a, 5, _piece(out, xn_i), sib).start()
        _gather_copy(sems, a, 2, _piece(out, yn_i), yn).wait_recv()
        if _splits(out):
            _gather_copy(sems, a, 4, _piece(out, yn_i, 1), xn).start()
        _gather_copy(sems, a, 6, _piece(out, yn_i), sib).start()


def _gather_finish(srcs, outs, sems):
    x, y, c, me = _my_place()
    sib = _peer(x, y, c, 1)[0]
    xn, yn, dg_i = _peer(x, y, c, 4)[0], _peer(x, y, c, 2)[0], _peer(x, y, c, 6)[1]
    n = len(srcs)
    for a in range(n):
        out = outs[a]
        _gather_copy(sems, a, 3, _piece(out, dg_i, 0), yn).wait_recv()
        _gather_copy(sems, a, 7, _piece(out, dg_i, 0), sib).start()
        if _splits(out):
            _gather_copy(sems, a, 4, _piece(out, dg_i, 1), xn).wait_recv()
            _gather_copy(sems, a, 8, _piece(out, dg_i, 1), sib).start()
    for a in range(n):
        out = outs[a]
        whole, half = _piece(out, me), _piece(out, me, 0)
        for k in (0, 5, 6):
            _gather_copy(sems, a, k, whole, sib).wait_recv()
        for k in (7, 8) if _splits(out) else (7,):
            _gather_copy(sems, a, k, half, sib).wait_recv()
        for k in (0, 1, 2):
            _gather_copy(sems, a, k, whole, sib, src=srcs[a]).wait_send()
        for k in (5, 6):
            _gather_copy(sems, a, k, whole, sib).wait_send()
        for k in (3, 4, 7, 8) if _splits(out) else (3, 7):
            _gather_copy(sems, a, k, half, sib).wait_send()
        pltpu.make_async_copy(srcs[a], whole, sems[2].at[a]).wait()


def _relay_sems(n):
    return [pltpu.SemaphoreType.DMA((n, 9)), pltpu.SemaphoreType.DMA((n, 9)), pltpu.SemaphoreType.DMA((n,))]


def _gather_sems(n):
    return [pltpu.SemaphoreType.DMA((n, 7)), pltpu.SemaphoreType.DMA((n, 7)), pltpu.SemaphoreType.DMA((n,))]


class _Comm:
    def __init__(self, args, out_shape, scratch, start, finish, relay=None):
        self.args, self.out_shape, self.scratch, self.start, self.finish = args, out_shape, scratch, start, finish
        self.relay = relay


def _gather_comm(shards):
    return _Comm(list(shards), [_sds((N_DEV,) + s.shape[1:], s.dtype) for s in shards], _relay_sems(len(shards)),
                 _gather_start, _gather_finish, relay=_gather_relay)


def _direct_copies(ins, outs, sems, wait, from_block):
    send_sems, recv_sems, local_sems = sems
    x, y, c, me = _my_place()
    for a in range(len(ins)):
        src = ins[a].at[pl.ds(me, 1)] if from_block[a] else ins[a]
        local = pltpu.make_async_copy(src, outs[a].at[pl.ds(me, 1)], local_sems.at[a])
        local.wait() if wait else local.start()
        for k in range(1, N_DEV):
            to, to_i = _peer(x, y, c, k)
            cp = pltpu.make_async_remote_copy(
                src_ref=ins[a].at[pl.ds(to_i, 1)] if from_block[a] else ins[a], dst_ref=outs[a].at[pl.ds(me, 1)],
                send_sem=send_sems.at[a, k - 1], recv_sem=recv_sems.at[a, k - 1], device_id=to, device_id_type=MESH)
            cp.wait() if wait else cp.start()


def _exchange_comm(grads=(), parts=()):
    ins = list(grads) + list(parts)
    from_block = [True] * len(grads) + [False] * len(parts)
    out_shape = [_sds(g.shape, g.dtype) for g in grads] + [_sds((N_DEV,) + p.shape[1:], p.dtype) for p in parts]

    def start(ins_, outs_, sems_):
        _direct_copies(ins_, outs_, sems_, False, from_block)

    def finish(ins_, outs_, sems_):
        _direct_copies(ins_, outs_, sems_, True, from_block)

    return _Comm(ins, out_shape, _gather_sems(len(ins)), start, finish)


def _chip_copies(ins, outs, sems, wait, rels, own):
    send_sems, recv_sems, local_sems = sems
    x, y, c, _ = _my_place()
    for a in range(len(ins)):
        if own:
            local = pltpu.make_async_copy(ins[a].at[pl.ds(2 * x + y, 1)], outs[a].at[pl.ds(len(rels), 1)],
                                          local_sems.at[a])
            local.wait() if wait else local.start()
        for i, j in enumerate(rels):
            to = _peer(x, y, c, CHIPS[j])[0]
            cp = pltpu.make_async_remote_copy(
                src_ref=ins[a].at[pl.ds(2 * to[0] + to[1], 1)], dst_ref=outs[a].at[pl.ds(i, 1)],
                send_sem=send_sems.at[a, i], recv_sem=recv_sems.at[a, i], device_id=to, device_id_type=MESH)
            cp.wait() if wait else cp.start()


def _chip_exchange_comm(sums, rels=(0, 1, 2), own=True):
    def start(ins_, outs_, sems_):
        _chip_copies(ins_, outs_, sems_, False, rels, own)

    def finish(ins_, outs_, sems_):
        _chip_copies(ins_, outs_, sems_, True, rels, own)

    n = len(sums)
    sems = [pltpu.SemaphoreType.DMA((n, len(rels))), pltpu.SemaphoreType.DMA((n, len(rels))),
            pltpu.SemaphoreType.DMA((n,))]
    return _Comm(list(sums), [_sds((len(rels) + own,) + s.shape[1:], s.dtype) for s in sums], sems, start, finish)


def _pair_reduce(name, grads, after=()):
    n = len(grads)
    n_chips = N_DEV // 2

    def body(*refs):
        g_refs, gh_refs, refs = refs[:n], refs[n:2 * n], refs[2 * n + len(after):]
        p_refs, land = refs[:n], refs[n:2 * n]
        send_sems, recv_sems = refs[2 * n:]
        x, y, c, _ = _my_place()
        sib = _peer(x, y, c, 1)[0]
        q = pl.program_id(0)

        def to_sibling(a, j):
            return pltpu.make_async_remote_copy(
                src_ref=gh_refs[a].at[j, pl.ds(1 - c, 1)], dst_ref=land[a].at[pl.ds(j, 1)],
                send_sem=send_sems.at[a, j], recv_sem=recv_sems.at[a, j], device_id=sib, device_id_type=MESH)

        @pl.when(q == 0)
        def _():
            for j in range(n_chips):
                for a in range(n):
                    to_sibling(a, j).start()

        for a in range(n):
            to_sibling(a, q).wait_recv()
            p_refs[a][...] = (g_refs[a][0, pl.ds(c, 1)].astype(F32) + land[a][pl.ds(q, 1)].astype(F32)).astype(BF16)

        @pl.when(q == n_chips - 1)
        def _():
            for a in range(n):
                for j in range(n_chips):
                    to_sibling(a, j).wait_send()

    views = [g.reshape((n_chips, 2) + g.shape[1:]) for g in grads]
    res = pl.pallas_call(
        body, name=name, grid=(n_chips,),
        in_specs=[pl.BlockSpec((1, 2) + g.shape[1:], lambda q: (q, 0, 0, 0)) for g in grads]
        + [ANY] * (n + len(after)),
        out_specs=[pl.BlockSpec((1,) + g.shape[1:], lambda q: (q, 0, 0)) for g in grads],
        out_shape=[_sds((n_chips,) + g.shape[1:], BF16) for g in grads],
        scratch_shapes=[pltpu.VMEM((n_chips,) + g.shape[1:], BF16) for g in grads]
        + [pltpu.SemaphoreType.DMA((n, n_chips)), pltpu.SemaphoreType.DMA((n, n_chips))],
        compiler_params=pltpu.CompilerParams(dimension_semantics=("arbitrary",), vmem_limit_bytes=VMEM_LIMIT),
    )(*views, *views, *after)
    return list(res)


def _pair_exchange_comm(grads):
    n, n_chips = len(grads), N_DEV // 2

    def copies(ins, outs, sems, wait):
        x, y, c, _ = _my_place()
        for j in range(n_chips):
            for a in range(n):
                cp = pltpu.make_async_remote_copy(
                    src_ref=ins[a].at[j, pl.ds(1 - c, 1)], dst_ref=outs[a].at[pl.ds(j, 1)], send_sem=sems[0].at[a, j],
                    recv_sem=sems[1].at[a, j], device_id=_peer(x, y, c, 1)[0], device_id_type=MESH)
                cp.wait() if wait else cp.start()

    views = [g.reshape((n_chips, 2) + g.shape[1:]) for g in grads]
    sems = [pltpu.SemaphoreType.DMA((n, n_chips)), pltpu.SemaphoreType.DMA((n, n_chips))]
    return _Comm(views, [_sds((n_chips,) + g.shape[1:], g.dtype) for g in grads], sems,
                 lambda i, o, s: copies(i, o, s, False), lambda i, o, s: copies(i, o, s, True))


def _pair_add(name, grads, landed, after=()):
    n, n_chips = len(grads), N_DEV // 2

    def body(*refs):
        g_refs, l_refs, p_refs = refs[:n], refs[n:2 * n], refs[2 * n:]
        c = lax.axis_index("c")
        for a in range(n):
            p_refs[a][...] = (g_refs[a][0, pl.ds(c, 1)].astype(F32) + l_refs[a][...].astype(F32)).astype(BF16)

    views = [g.reshape((n_chips, 2) + g.shape[1:]) for g in grads]
    blocks = [pl.BlockSpec((1,) + g.shape[1:], lambda q: (q, 0, 0)) for g in grads]
    return _call(name, body, (n_chips,),
                 [pl.BlockSpec((1, 2) + g.shape[1:], lambda q: (q, 0, 0, 0)) for g in grads] + blocks, blocks,
                 [_sds((n_chips,) + g.shape[1:], BF16) for g in grads], (*views, *landed), after=after)


def _call(name, body, grid, in_specs, out_specs, out_shape, args, scratch=(), after=()):
    ni, na = len(in_specs), len(after)

    def ordered(*refs):
        body(*refs[:ni], *refs[ni + na:])

    return list(pl.pallas_call(
        ordered if after else body, name=name, grid=grid, in_specs=list(in_specs) + [ANY] * na,
        out_specs=list(out_specs), out_shape=list(out_shape), scratch_shapes=list(scratch),
        compiler_params=pltpu.CompilerParams(dimension_semantics=("arbitrary",) * len(grid),
                                             vmem_limit_bytes=VMEM_LIMIT))(*args, *after))


SIBLING_AND_NEIGHBOURS, OTHER_CHIPS, EVERYONE = (1, 4, 2), CHIPS, tuple(range(1, N_DEV))


def _by_sequencer(name, comm, peers, collective_id):
    src = [jax.new_ref(a, memory_space=pltpu.MemorySpace.HBM) for a in comm.args]
    dst = [jax.empty_ref(s, memory_space=pltpu.MemorySpace.HBM) for s in comm.out_shape]

    @pl.kernel(mesh=plsc.ScalarSubcoreMesh(axis_name="sequencer", num_cores=1), name=name,
               scratch_types=tuple(comm.scratch), compiler_params=pltpu.CompilerParams(collective_id=collective_id))
    def launch(*sems):
        x, y, c, _ = _my_place()
        barrier = pltpu.get_barrier_semaphore()
        for k in peers:
            pl.semaphore_signal(barrier, inc=1, device_id=_peer(x, y, c, k)[0], device_id_type=MESH)
        pl.semaphore_wait(barrier, len(peers))
        comm.start(src, dst, sems)
        if comm.relay is not None:
            comm.relay(src, dst, sems)
        comm.finish(src, dst, sems)

    launch()
    return [d[...] for d in dst]


def _gather_first(first, later):
    nf, nl = len(first), len(later)
    dts = [BF16] * (nf - 2) + [F32, F32]

    def body(*refs):
        ins, refs = refs[:nf + nl], refs[nf + nl:]
        outs, refs = refs[:nf], refs[nf:]
        casts, refs = refs[:nl], refs[nl:]
        stage, sems = refs[:nf], refs[nf:]
        for a in range(nf):
            stage[a][...] = ins[a][...].astype(dts[a])
            _gather_start(stage, outs, sems, only=a)
        for a in range(nl):
            casts[a][...] = ins[nf + a][...].astype(BF16)
        _gather_relay(stage, outs, sems)
        _gather_finish(stage, outs, sems)

    res = pl.pallas_call(
        body, name="gather_first",
        in_specs=[VMEM] * (nf + nl), out_specs=[ANY] * nf + [VMEM] * nl,
        out_shape=[_sds((N_DEV,) + s.shape[1:], dt) for s, dt in zip(first, dts)]
        + [_sds(s.shape, BF16) for s in later],
        scratch_shapes=[pltpu.VMEM(s.shape, dt) for s, dt in zip(first, dts)] + _relay_sems(nf),
        compiler_params=pltpu.CompilerParams(vmem_limit_bytes=VMEM_LIMIT),
    )(*first, *later)
    return list(res[:nf]), list(res[nf:])


def _a_mix_fwd(x, g, w_in, ln_g, ln_b, w_s, b_st, w_out):
    t = x.shape[0]
    nblk = TM // GMLP_BLOCK

    def body(x_ref, g_ref, win_ref, lng_ref, lnb_ref, ws_ref, bst_ref, wout_ref, h_ref, z_ref, gated_scr):
        xv = x_ref[...]
        hb = _rms_fwd(xv, g_ref[...])[0].astype(BF16)
        for d in range(N_DEV):
            z_ref[:, d * FF_SLOT:(d + 1) * FF_SLOT] = _dot(hb, win_ref[d])
        u = _gelu(z_ref[:, :GATE_DIM])
        vb = _ln_fwd(_gelu(z_ref[:, GATE_DIM:]), lng_ref[...], lnb_ref[...])[0].astype(BF16)
        mask = _gate_mask()
        for gi in range(A_GROUPS):
            wm = jnp.where(mask, ws_ref[gi], 0.0).astype(BF16)
            bias = bst_ref[:, gi:gi + 1]
            cs = slice(gi * A_GROUP_DIM, (gi + 1) * A_GROUP_DIM)
            for n in range(nblk):
                rs = slice(n * GMLP_BLOCK, (n + 1) * GMLP_BLOCK)
                sv = _dot(wm, vb[rs, cs]) + bias
                gated_scr[rs, cs] = (u[rs, cs] * sv).astype(BF16)
        h_ref[...] = xv + _dot(gated_scr[...], wout_ref[...])

    return _call(
        "a_mix_fwd", body, (t // TM,),
        [_row(D_MODEL), _res((1, D_MODEL)), _res((N_DEV, D_MODEL, FF_SLOT)), _res((1, GATE_DIM)),
         _res((1, GATE_DIM)), _res((A_GROUPS, GMLP_BLOCK, GMLP_BLOCK)), _res((GMLP_BLOCK, A_GROUPS)),
         _res((GATE_DIM, D_MODEL))],
        [_row(D_MODEL), _row(2 * GATE_DIM), _row(GATE_DIM)],
        [_sds((t, D_MODEL), F32), _sds((t, 2 * GATE_DIM), F32), _sds((t, GATE_DIM), BF16)],
        (x, g, w_in, ln_g, ln_b, w_s, b_st, w_out))


MLP_W_SPECS = (_res((N_DEV, D_MODEL, FF_SLOT)), _res((N_DEV, FF_SLOT, D_MODEL)))


def _mlp_fwd(h, g, w1, w2):
    t = h.shape[0]

    def body(h_ref, g_ref, w1_ref, w2_ref, o_ref, a_ref):
        hv = h_ref[...]
        hb = _rms_fwd(hv, g_ref[...])[0].astype(BF16)
        o_ref[...] = hv
        for d in range(N_DEV):
            a = _dot(hb, w1_ref[d])
            a_ref[:, d * FF_SLOT:(d + 1) * FF_SLOT] = a
            r = jnp.maximum(a, 0.0)
            o_ref[...] += _dot((r * r).astype(BF16), w2_ref[d])

    return _call(
        "mlp_fwd", body, (t // TM,), [_row(D_MODEL), _res((1, D_MODEL)), *MLP_W_SPECS],
        [_row(D_MODEL), _row(D_FF)], [_sds((t, D_MODEL), F32), _sds((t, D_FF), F32)], (h, g, w1, w2))


def _mlp_fwd_loss(h, g, w1, w2, final_g, target):
    t = h.shape[0]

    def body(h_ref, g_ref, w1_ref, w2_ref, fg_ref, t_ref, a_ref, loss_ref, dh_ref, dg_ref):
        hv = h_ref[...]
        hb = _rms_fwd(hv, g_ref[...])[0].astype(BF16)
        out = hv
        for d in range(N_DEV):
            a = _dot(hb, w1_ref[d])
            a_ref[:, d * FF_SLOT:(d + 1) * FF_SLOT] = a
            r = jnp.maximum(a, 0.0)
            out = out + _dot((r * r).astype(BF16), w2_ref[d])
        y, xhat, rstd = _rms_fwd(out, fg_ref[...])
        err = y - t_ref[...]
        part = 0.5 * jnp.sum(jnp.mean(err * err, axis=-1, keepdims=True), axis=0, keepdims=True)
        dx, dg = _rms_bwd(err * (1.0 / D_MODEL), xhat, rstd, fg_ref[...])
        dh_ref[...] = dx
        _acc(dg_ref, dg)
        _acc(loss_ref, part)

    return _call(
        "mlp_fwd_loss", body, (t // TM,),
        [_row(D_MODEL), _res((1, D_MODEL)), *MLP_W_SPECS, _res((1, D_MODEL)), _row(D_MODEL)],
        [_row(D_FF), _const((1, 1)), _row(D_MODEL), _const((1, D_MODEL))],
        [_sds((t, D_FF), F32), _sds((1, 1), F32), _sds((t, D_MODEL), F32), _sds((1, D_MODEL), F32)],
        (h, g, w1, w2, final_g, target))


KVQ_W_SPECS = (_res((1, D_MODEL)), _res((D_MODEL, KV_LORA + QK_ROPE)), _res((1, KV_LORA)),
               _res((B_HEADS, KV_LORA, QK_NOPE + V_HEAD)), _res((1, D_MODEL)), _res((D_MODEL, Q_LORA)),
               _res((1, Q_LORA)), _res((B_HEADS, Q_LORA, QK_NOPE + QK_ROPE)))


def _kvq_fwd(h, pos, inv_freq, kvq_w):
    t = h.shape[0]
    half = QK_ROPE // 2

    def body(h_ref, pos_ref, invf_ref, srcg_ref, wkva_ref, kvag_ref, wkvb_ref, mixg_ref, wqa_ref, qg_ref, wqb_ref,
             ckv_ref, k_ref, v_ref, cqpre_ref, q_ref, cos_ref, sin_ref):
        hv = h_ref[...]
        xhat = hv * lax.rsqrt(jnp.mean(hv * hv, axis=-1, keepdims=True) + EPS)
        ang = pos_ref[...].astype(F32) * invf_ref[...]
        cos, sin = jnp.cos(ang), jnp.sin(ang)
        cos_ref[...] = cos
        sin_ref[...] = sin
        ckv = _dot((xhat * srcg_ref[...]).astype(BF16), wkva_ref[...])
        ckv_ref[...] = ckv
        cb = _rms_fwd(ckv[:, :KV_LORA], kvag_ref[...])[0].astype(BF16)
        kpe = _rope(ckv[:, KV_LORA:], cos, sin).astype(BF16)
        for hd in range(B_HEADS):
            kv = _dot(cb, wkvb_ref[hd])
            k_ref[hd, :, 0:QK_NOPE] = kv[:, :QK_NOPE].astype(BF16)
            k_ref[hd, :, QK_NOPE:] = kpe
            v_ref[hd] = kv[:, QK_NOPE:].astype(BF16)
        cqpre = _dot((xhat * mixg_ref[...]).astype(BF16), wqa_ref[...])
        cqpre_ref[...] = cqpre
        cqb = _rms_fwd(cqpre, qg_ref[...])[0].astype(BF16)
        for hd in range(B_HEADS):
            q = _dot(cqb, wqb_ref[hd])
            q_ref[hd, :, 0:QK_NOPE] = q[:, :QK_NOPE].astype(BF16)
            q_ref[hd, :, QK_NOPE:] = _rope(q[:, QK_NOPE:], cos, sin).astype(BF16)

    return _call(
        "kvq_fwd", body, (t // TM,), [_row(D_MODEL), _row(1), _res((1, half)), *KVQ_W_SPECS],
        [_row(KV_LORA + QK_ROPE), _heads(QK_NOPE + QK_ROPE), _heads(V_HEAD), _row(Q_LORA),
         _heads(QK_NOPE + QK_ROPE), _row(half), _row(half)],
        [_sds((t, KV_LORA + QK_ROPE), F32), _sds((B_HEADS, t, QK_NOPE + QK_ROPE), BF16),
         _sds((B_HEADS, t, V_HEAD), BF16), _sds((t, Q_LORA), F32), _sds((B_HEADS, t, QK_NOPE + QK_ROPE), BF16),
         _sds((t, half), F32), _sds((t, half), F32)],
        (h, pos, inv_freq, *kvq_w))


def _softmax_rows(q, k_ref, k):
    past, upto = k * TM, (k + 1) * TM
    s = _dot_nt(q, k_ref[0:upto, :])
    own = jnp.where(_att_mask(0, TM, TM), s[:, past:], jnp.finfo(F32).min)
    s = own if k == 0 else jnp.concatenate([s[:, :past], own], axis=1)
    e = jnp.exp2((s - jnp.max(s, axis=-1, keepdims=True)) * (ATT_SCALE * LOG2_E))
    return e * (1.0 / jnp.sum(e, axis=-1, keepdims=True))


def _for_my_tile(i, nq, fn):
    for k in range(nq):
        @pl.when(i == k)
        def _(k=k):
            fn(k)


def _attn_fwd(h, q, k, v, w_o):
    t = h.shape[0]
    nq, hps = t // TM, HEADS_PER_STEP

    def body(h_ref, q_ref, k_ref, v_ref, wo_ref, o_ref, att_ref):
        i, pair = pl.program_id(0), pl.program_id(1)

        @pl.when(pair == 0)
        def _():
            o_ref[...] = h_ref[...]

        def tile(kt):
            proj = None
            for j in range(hps):
                hd = pair * hps + j
                p = _softmax_rows(q_ref[j], k_ref.at[hd], kt)
                ob = _dot(p.astype(BF16), v_ref[hd, 0:(kt + 1) * TM, :]).astype(BF16)
                att_ref[j] = ob
                proj = _dot(ob, wo_ref[hd]) if proj is None else proj + _dot(ob, wo_ref[hd])
            o_ref[...] += proj

        _for_my_tile(i, nq, tile)

    def per_head(d):
        return pl.BlockSpec((hps, TM, d), lambda i, pair: (pair, i, 0))

    def resident(shape):
        zeros = (0,) * len(shape)
        return pl.BlockSpec(shape, lambda i, pair: zeros, pipeline_mode=pl.Buffered(1))

    tile_spec = pl.BlockSpec((TM, D_MODEL), lambda i, pair: (i, 0))
    return _call(
        "attn_fwd", body, (nq, B_HEADS // hps),
        [tile_spec, per_head(QK_NOPE + QK_ROPE), resident((B_HEADS, t, QK_NOPE + QK_ROPE)),
         resident((B_HEADS, t, V_HEAD)), resident((B_HEADS, V_HEAD, D_MODEL))],
        [tile_spec, per_head(V_HEAD)], [_sds((t, D_MODEL), F32), _sds((B_HEADS, t, V_HEAD), BF16)],
        (h, q, k, v, w_o))


def _mlp_bwd(h, a, dho, g, w1, w2, layer, after=()):
    t = h.shape[0]

    def body(h_ref, a_ref, dho_ref, g_ref, w1_ref, w2_ref, dhi_ref, dg_ref, hn_ref, f_ref, da_ref, dhib_ref):
        gv = g_ref[...]
        y, xhat, rstd = _rms_fwd(h_ref[...], gv)
        hn_ref[...] = y.astype(BF16)
        dho_v = dho_ref[...]
        dhob = dho_v.astype(BF16)
        dhn = jnp.zeros((TM, D_MODEL), F32)
        for d in range(N_DEV):
            cs = slice(d * FF_SLOT, (d + 1) * FF_SLOT)
            r = jnp.maximum(a_ref[:, cs], 0.0)
            f_ref[:, cs] = (r * r).astype(BF16)
            da = (_dot_nt(dhob, w2_ref[d]) * (2.0 * r)).astype(BF16)
            da_ref[:, cs] = da
            dhn = dhn + _dot_nt(da, w1_ref[d])
        dx, dg = _rms_bwd(dhn, xhat, rstd, gv)
        dhi = dho_v + dx
        dhi_ref[...] = dhi
        dhib_ref[...] = dhi.astype(BF16)
        _acc(dg_ref, dg)

    return _call(
        f"mlp_bwd_{layer}", body, (t // TM,),
        [_row(D_MODEL), _row(D_FF), _row(D_MODEL), _res((1, D_MODEL)), *MLP_W_SPECS],
        [_row(D_MODEL), _const((1, D_MODEL)), _row(D_MODEL), _row(D_FF), _row(D_FF), _row(D_MODEL)],
        [_sds((t, D_MODEL), F32), _sds((1, D_MODEL), F32), _sds((t, D_MODEL), BF16), _sds((t, D_FF), BF16),
         _sds((t, D_FF), BF16), _sds((t, D_MODEL), BF16)],
        (h, a, dho, g, w1, w2), after=after)


def _attn_bwd(dh, q, k, v, w_o, cos, sin, after=()):
    t = dh.shape[0]
    half, hps = QK_ROPE // 2, HEADS_PER_STEP

    def body(dh_ref, q_ref, k_ref, v_ref, wo_ref, cos_ref, sin_ref, dq_ref, dk_ref, dv_ref):
        i = pl.program_id(1)

        @pl.when(i == 0)
        def _():
            dk_ref[...] = jnp.zeros_like(dk_ref)
            dv_ref[...] = jnp.zeros_like(dv_ref)

        def tile(kt):
            keys = slice(0, (kt + 1) * TM)
            for j in range(hps):
                qj = q_ref[j]
                do = _dot_nt(dh_ref[kt * TM:(kt + 1) * TM, :], wo_ref[j]).astype(BF16)
                p = _softmax_rows(qj, k_ref.at[j], kt)
                dp = _dot_nt(do, v_ref[j, keys, :])
                ds = (p * (dp - jnp.sum(p * dp, axis=-1, keepdims=True)) * ATT_SCALE).astype(BF16)
                dq = _dot(ds, k_ref[j, keys, :])
                dq_ref[j, :, 0:QK_NOPE] = dq[:, :QK_NOPE].astype(BF16)
                dq_ref[j, :, QK_NOPE:] = _rope(dq[:, QK_NOPE:], cos_ref[...], -sin_ref[...]).astype(BF16)
                dk_ref[j, keys, :] += _dot_tn(ds, qj)
                dv_ref[j, keys, :] += _dot_tn(p.astype(BF16), do)

        _for_my_tile(i, t // TM, tile)

    def per_pair(rows, d, tiled):
        return pl.BlockSpec((hps, rows, d), (lambda pair, i: (pair, i, 0)) if tiled else (lambda pair, i: (pair, 0, 0)))

    def tile(d):
        return pl.BlockSpec((TM, d), lambda pair, i: (i, 0))

    return _call(
        "attn_bwd", body, (B_HEADS // hps, t // TM),
        [pl.BlockSpec((t, D_MODEL), lambda pair, i: (0, 0), pipeline_mode=pl.Buffered(1)),
         per_pair(TM, QK_NOPE + QK_ROPE, True), per_pair(t, QK_NOPE + QK_ROPE, False), per_pair(t, V_HEAD, False),
         per_pair(V_HEAD, D_MODEL, False), tile(half), tile(half)],
        [per_pair(TM, QK_NOPE + QK_ROPE, True), per_pair(t, QK_NOPE + QK_ROPE, False), per_pair(t, V_HEAD, False)],
        [_sds((B_HEADS, t, QK_NOPE + QK_ROPE), BF16), _sds((B_HEADS, t, QK_NOPE + QK_ROPE), F32),
         _sds((B_HEADS, t, V_HEAD), F32)],
        (dh, q, k, v, w_o, cos, sin), after=after)


def _kvq_bwd(h, dh, ckv, cqpre, dq, dk, dv, cos, sin, kvq_w, after=()):
    t = h.shape[0]
    half = QK_ROPE // 2

    def body(h_ref, dh_ref, ckv_ref, cqpre_ref, dq_ref, dk_ref, dv_ref, cos_ref, sin_ref,
             srcg_ref, wkva_ref, kvag_ref, wkvb_ref, mixg_ref, wqa_ref, qg_ref, wqb_ref,
             dhi_ref, hq_ref, hk_ref, cq_ref, dcqpre_ref, c_ref, dkv_ref, dckv_ref,
             dmixg_ref, dsrcg_ref, dqg_ref, dkvag_ref):
        hv = h_ref[...]
        rstd = lax.rsqrt(jnp.mean(hv * hv, axis=-1, keepdims=True) + EPS)
        xhat = hv * rstd
        mixg, srcg, qg, kvag = mixg_ref[...], srcg_ref[...], qg_ref[...], kvag_ref[...]
        hq_ref[...] = (xhat * mixg).astype(BF16)
        hk_ref[...] = (xhat * srcg).astype(BF16)
        cq, cqhat, crstd = _rms_fwd(cqpre_ref[...], qg)
        cq_ref[...] = cq.astype(BF16)
        dcq = jnp.zeros((TM, Q_LORA), F32)
        for hd in range(B_HEADS):
            dcq = dcq + _dot_nt(dq_ref[hd], wqb_ref[hd])
        dcqpre, dqg = _rms_bwd(dcq, cqhat, crstd, qg)
        dcqpre_b = dcqpre.astype(BF16)
        dcqpre_ref[...] = dcqpre_b
        dxq, dmixg = _rms_bwd(_dot_nt(dcqpre_b, wqa_ref[...]), xhat, rstd, mixg)
        ckv = ckv_ref[...]
        c, chat, krstd = _rms_fwd(ckv[:, :KV_LORA], kvag)
        c_ref[...] = c.astype(BF16)
        dc = jnp.zeros((TM, KV_LORA), F32)
        dkpe = jnp.zeros((TM, QK_ROPE), F32)
        for hd in range(B_HEADS):
            dkv = jnp.concatenate([dk_ref[hd, :, 0:QK_NOPE], dv_ref[hd]], axis=-1).astype(BF16)
            dkv_ref[hd] = dkv
            dc = dc + _dot_nt(dkv, wkvb_ref[hd])
            dkpe = dkpe + dk_ref[hd, :, QK_NOPE:]
        dlat, dkvag = _rms_bwd(dc, chat, krstd, kvag)
        dpe = _rope(dkpe, cos_ref[...], -sin_ref[...])
        dckv_b = jnp.concatenate([dlat, dpe], axis=-1).astype(BF16)
        dckv_ref[...] = dckv_b
        dxk, dsrcg = _rms_bwd(_dot_nt(dckv_b, wkva_ref[...]), xhat, rstd, srcg)
        dhi_ref[...] = dh_ref[...] + dxq + dxk
        _acc(dmixg_ref, dmixg)
        _acc(dsrcg_ref, dsrcg)
        _acc(dqg_ref, dqg)
        _acc(dkvag_ref, dkvag)

    return _call(
        "kvq_bwd", body, (t // TM,),
        [_row(D_MODEL), _row(D_MODEL), _row(KV_LORA + QK_ROPE), _row(Q_LORA), _heads(QK_NOPE + QK_ROPE),
         _heads(QK_NOPE + QK_ROPE), _heads(V_HEAD), _row(half), _row(half), *KVQ_W_SPECS],
        [_row(D_MODEL), _row(D_MODEL), _row(D_MODEL), _row(Q_LORA), _row(Q_LORA), _row(KV_LORA),
         _heads(QK_NOPE + V_HEAD), _row(KV_LORA + QK_ROPE),
         _const((1, D_MODEL)), _const((1, D_MODEL)), _const((1, Q_LORA)), _const((1, KV_LORA))],
        [_sds((t, D_MODEL), F32), _sds((t, D_MODEL), BF16), _sds((t, D_MODEL), BF16), _sds((t, Q_LORA), BF16),
         _sds((t, Q_LORA), BF16), _sds((t, KV_LORA), BF16), _sds((B_HEADS, t, QK_NOPE + V_HEAD), BF16),
         _sds((t, KV_LORA + QK_ROPE), BF16),
         _sds((1, D_MODEL), F32), _sds((1, D_MODEL), F32), _sds((1, Q_LORA), F32), _sds((1, KV_LORA), F32)],
        (h, dh, ckv, cqpre, dq, dk, dv, cos, sin, *kvq_w), after=after)


def _a_mix_bwd(x, z, dh, g, w_in, ln_g, ln_b, w_s, b_st, w_out, after=()):
    t = x.shape[0]
    tm = TM_GATE
    nblk = tm // GMLP_BLOCK

    def body(x_ref, z_ref, dh_ref, g_ref, win_ref, lng_ref, lnb_ref, ws_ref, bst_ref, wout_ref,
             dx_ref, hn_ref, dz_ref, dg_ref, dlng_ref, dlnb_ref, dws_ref, dbs_ref, dvn_scr, gelu_grad_v):
        @pl.when(pl.program_id(0) == 0)
        def _():
            dws_ref[...] = jnp.zeros_like(dws_ref)
            dbs_ref[...] = jnp.zeros_like(dbs_ref)

        gv, lng = g_ref[...], lng_ref[...]
        y, xhat, rstd = _rms_fwd(x_ref[...], gv)
        hn_ref[...] = y.astype(BF16)
        dhv = dh_ref[...]
        dgated = _dot_nt(dhv.astype(BF16), wout_ref[...])
        u, gelu_grad_u = _gelu_and_grad(z_ref[:, :GATE_DIM])
        v, gelu_grad_v[...] = _gelu_and_grad(z_ref[:, GATE_DIM:])
        vn, vhat, lrstd = _ln_fwd(v, lng, lnb_ref[...])
        vb = vn.astype(BF16)
        mask = _gate_mask()
        for gi in range(A_GROUPS):
            wm = jnp.where(mask, ws_ref[gi], 0.0).astype(BF16)
            bias = bst_ref[:, gi:gi + 1]
            cs = slice(gi * A_GROUP_DIM, (gi + 1) * A_GROUP_DIM)
            dws = jnp.zeros((GMLP_BLOCK, GMLP_BLOCK), F32)
            dbs = jnp.zeros((GMLP_BLOCK, 1), F32)
            for n in range(nblk):
                rs = slice(n * GMLP_BLOCK, (n + 1) * GMLP_BLOCK)
                sv = _dot(wm, vb[rs, cs]) + bias
                dz_ref[rs, cs] = (dgated[rs, cs] * sv * gelu_grad_u[rs, cs]).astype(BF16)
                dsv = dgated[rs, cs] * u[rs, cs]
                dsvb = dsv.astype(BF16)
                dws = dws + _dot_nt(dsvb, vb[rs, cs])
                dbs = dbs + jnp.sum(dsv, axis=-1, keepdims=True)
                dvn_scr[rs, cs] = _dot_tn(wm, dsvb)
            dws_ref[gi] += jnp.where(mask, dws, 0.0)
            dbs_ref[gi] += dbs
        dvn = dvn_scr[...]
        dvhat = dvn * lng
        dv = lrstd * (dvhat - jnp.mean(dvhat, axis=-1, keepdims=True)
                      - vhat * jnp.mean(dvhat * vhat, axis=-1, keepdims=True))
        dz_ref[:, GATE_DIM:] = (dv * gelu_grad_v[...]).astype(BF16)
        dhn = jnp.zeros((tm, D_MODEL), F32)
        for d in range(N_DEV):
            dhn = dhn + _dot_nt(dz_ref[:, d * FF_SLOT:(d + 1) * FF_SLOT], win_ref[d])
        dx, dg = _rms_bwd(dhn, xhat, rstd, gv)
        dx_ref[...] = dhv + dx
        _acc(dg_ref, dg)
        _acc(dlng_ref, jnp.sum(dvn * vhat, axis=0, keepdims=True))
        _acc(dlnb_ref, jnp.sum(dvn, axis=0, keepdims=True))

    return _call(
        "a_mix_bwd", body, (t // tm,),
        [_row(D_MODEL, tm), _row(2 * GATE_DIM, tm), _row(D_MODEL, tm), _res((1, D_MODEL)),
         _res((N_DEV, D_MODEL, FF_SLOT)), _res((1, GATE_DIM)), _res((1, GATE_DIM)),
         _res((A_GROUPS, GMLP_BLOCK, GMLP_BLOCK)), _res((GMLP_BLOCK, A_GROUPS)), _res((GATE_DIM, D_MODEL))],
        [_row(D_MODEL, tm), _row(D_MODEL, tm), _row(2 * GATE_DIM, tm),
         _const((1, D_MODEL)), _const((1, GATE_DIM)), _const((1, GATE_DIM)),
         _const((A_GROUPS, GMLP_BLOCK, GMLP_BLOCK)), _const((A_GROUPS, GMLP_BLOCK, 1))],
        [_sds((t, D_MODEL), F32), _sds((t, D_MODEL), BF16),
         _sds((t, 2 * GATE_DIM), BF16), _sds((1, D_MODEL), F32), _sds((1, GATE_DIM), F32),
         _sds((1, GATE_DIM), F32), _sds((A_GROUPS, GMLP_BLOCK, GMLP_BLOCK), F32),
         _sds((A_GROUPS, GMLP_BLOCK, 1), F32)],
        (x, z, dh, g, w_in, ln_g, ln_b, w_s, b_st, w_out),
        scratch=[pltpu.VMEM((tm, GATE_DIM), F32), pltpu.VMEM((tm, GATE_DIM), F32)], after=after)


def _wgrad(name, a, b, a_spec, b_spec, m, n, after=()):
    def body(a_ref, b_ref, o_ref):
        o_ref[0] = _dot_tn(a_ref[...].astype(BF16), b_ref[...].astype(BF16)).astype(BF16)

    return _call(name, body, (N_DEV,), [a_spec, b_spec], [pl.BlockSpec((1, m, n), lambda d: (d, 0, 0))],
                 [_sds((N_DEV, m, n), BF16)], (a, b), after=after)[0]


def _full(t, d):
    return pl.BlockSpec((t, d), lambda i: (0, 0), pipeline_mode=pl.Buffered(1))


def _cols(t, d):
    return pl.BlockSpec((t, d), lambda i: (0, i))


def _head(t, d):
    return pl.BlockSpec((None, t, d), lambda i: (i, 0, 0))


def _local_step(x, pos, target, inv_freq, wg, sm, shards=None):
    t = x.shape[0]
    wg = dict(wg)
    dist = shards is not None
    mix_g = [sm["norm_mix_g"][l:l + 1] for l in range(2)]
    mlp_g = [sm["norm_mlp_g"][l:l + 1] for l in range(2)]

    ids = iter(range(2, 2 + 9))

    def gather(names):
        if dist:
            got = _by_sequencer("gather_" + names[0], _gather_comm([shards[k] for k in names]),
                                SIBLING_AND_NEIGHBOURS, next(ids))
            wg.update(zip(names, got))

    def send(name, names):
        if dist:
            comm = _exchange_comm(grads=[g[k] for k in names])
            g.update(zip(names, _by_sequencer("exchange_" + name, comm, EVERYONE, next(ids))))

    def send_sums(name, names, meanwhile):
        if not dist:
            meanwhile()
            return ()
        grads = [g[k] for k in names]
        landed = _by_sequencer("pair_exchange_" + name, _pair_exchange_comm(grads), (1,), next(ids))
        sums = _pair_add("pair_add_" + name, grads, landed, after=meanwhile())
        g.update(zip(names, _by_sequencer("exchange_" + name, _chip_exchange_comm(sums), OTHER_CHIPS, next(ids))))
        return sums

    def a_args():
        return (wg["a_w_in"], wg["a_ln_v_g"], wg["a_ln_v_b"], sm["a_w_s"], sm["a_b_st"], wg["a_w_out"])

    def kvq_w():
        return (sm["kv_src_norm_g"], wg["kv_w_a"], sm["kv_a_norm_g"], wg["kv_w_b"], mix_g[1], wg["b_w_q_a"],
                sm["b_q_norm_g"], wg["b_w_q_b"])

    gather(("mlp_w1_0", "mlp_w2_0"))
    h1, z, gated = _a_mix_fwd(x, mix_g[0], *a_args())
    gather(("kv_w_a", "kv_w_b", "b_w_q_a", "b_w_q_b", "b_w_o"))
    h2, a0 = _mlp_fwd(h1, mlp_g[0], wg["mlp_w1_0"], wg["mlp_w2_0"])
    if dist:
        wg["b_w_q_a"] = wg["b_w_q_a"].reshape(D_MODEL, Q_LORA)
        wg["kv_w_a"] = wg["kv_w_a"].reshape(D_MODEL, KV_LORA + QK_ROPE)
    gather(("mlp_w1_1", "mlp_w2_1"))
    ckv, k, v, cqpre, q, cos, sin = _kvq_fwd(h2, pos, inv_freq, kvq_w())
    h3, att = _attn_fwd(h2, q, k, v, wg["b_w_o"])
    a1, loss, dh4, d_final_g = _mlp_fwd_loss(h3, mlp_g[1], wg["mlp_w1_1"], wg["mlp_w2_1"], sm["final_norm_g"], target)

    g = {}
    dh3, d_mlp_g1, hn, f, da, dh3_b = _mlp_bwd(h3, a1, dh4, mlp_g[1], wg["mlp_w1_1"], wg["mlp_w2_1"], 1)
    dq, dk, dv = _attn_bwd(dh3_b, q, k, v, wg["b_w_o"], cos, sin)
    g["mlp_w1_1"] = _wgrad("wgrad_w1_1", hn, da, _full(t, D_MODEL), _cols(t, FF_SLOT), D_MODEL, FF_SLOT, after=[dq])
    g["mlp_w2_1"] = _wgrad("wgrad_w2_1", f, dh4, _cols(t, FF_SLOT), _full(t, D_MODEL), FF_SLOT, D_MODEL)

    def wgrad_w_o():
        g["b_w_o"] = _wgrad("wgrad_w_o", att, dh3_b, _head(t, V_HEAD), _full(t, D_MODEL), V_HEAD, D_MODEL)
        return [g["b_w_o"]]

    sums = send_sums("mlp_1", ("mlp_w1_1", "mlp_w2_1"), wgrad_w_o)
    (dh2, hq, hk, cq, dcqpre, c, dkv, dckv, d_mix_g1, d_src_g, d_q_g, d_kv_a_g) = _kvq_bwd(
        h2, dh3, ckv, cqpre, dq, dk, dv, cos, sin, kvq_w(), after=sums)
    g["b_w_q_a"] = _wgrad("wgrad_w_q_a", hq, dcqpre, _cols(t, D_MODEL // N_DEV), _full(t, Q_LORA),
                          D_MODEL // N_DEV, Q_LORA)
    g["b_w_q_b"] = _wgrad("wgrad_w_q_b", cq, dq, _full(t, Q_LORA), _head(t, QK_NOPE + QK_ROPE),
                          Q_LORA, QK_NOPE + QK_ROPE)
    g["kv_w_a"] = _wgrad("wgrad_kv_w_a", hk, dckv, _cols(t, D_MODEL // N_DEV), _full(t, KV_LORA + QK_ROPE),
                         D_MODEL // N_DEV, KV_LORA + QK_ROPE)
    g["kv_w_b"] = _wgrad("wgrad_kv_w_b", c, dkv, _full(t, KV_LORA), _head(t, QK_NOPE + V_HEAD),
                         KV_LORA, QK_NOPE + V_HEAD)
    qkv = ("b_w_o", "b_w_q_a", "b_w_q_b", "kv_w_a", "kv_w_b")
    landed = [g[k] for k in qkv]
    send("qkv", qkv)
    dh1, d_mlp_g0, hn, f, da, dh1_b = _mlp_bwd(h1, a0, dh2, mlp_g[0], wg["mlp_w1_0"], wg["mlp_w2_0"], 0,
                                               after=landed if dist else ())
    landed = [g["mlp_w1_1"], g["mlp_w2_1"]] if dist else ()
    g["mlp_w1_0"] = _wgrad("wgrad_w1_0", hn, da, _full(t, D_MODEL), _cols(t, FF_SLOT), D_MODEL, FF_SLOT, after=landed)
    g["mlp_w2_0"] = _wgrad("wgrad_w2_0", f, dh2, _cols(t, FF_SLOT), _full(t, D_MODEL), FF_SLOT, D_MODEL)

    def wgrad_a_w_out():
        g["a_w_out"] = _wgrad("wgrad_a_w_out", gated, dh1_b, _cols(t, GATE_DIM // N_DEV), _full(t, D_MODEL),
                              GATE_DIM // N_DEV, D_MODEL)
        return [g["a_w_out"]] + [g[k] for k in qkv]

    sums = send_sums("mlp_0", ("mlp_w1_0", "mlp_w2_0"), wgrad_a_w_out)
    dx, hn, dz, d_mix_g0, d_ln_g, d_ln_b, d_ws, d_bs = _a_mix_bwd(x, z, dh1, mix_g[0], *a_args(), after=sums)
    small = {
        "norm_mix_g": jnp.concatenate([d_mix_g0, d_mix_g1], axis=0),
        "norm_mlp_g": jnp.concatenate([d_mlp_g0, d_mlp_g1], axis=0),
        "a_ln_v_g": d_ln_g.reshape(N_DEV, GATE_DIM // N_DEV),
        "a_ln_v_b": d_ln_b.reshape(N_DEV, GATE_DIM // N_DEV),
        "a_w_s": d_ws.astype(BF16) if dist else d_ws,
        "a_b_s": d_bs.reshape(A_GROUPS, GMLP_BLOCK),
        "b_q_norm_g": d_q_g,
        "kv_src_norm_g": d_src_g,
        "kv_a_norm_g": d_kv_a_g,
        "final_norm_g": d_final_g,
    }
    if dist:
        parts = [small[k].reshape((1,) + small[k].shape) for k in SMALL] + [loss.reshape(1, 1, 1)]
        got = _by_sequencer("gather_small", _exchange_comm(parts=parts), EVERYONE, next(ids))
        small, loss = dict(zip(SMALL, got)), got[-1]
    g["a_w_in"] = _wgrad("wgrad_a_w_in", hn, dz, _full(t, D_MODEL), _cols(t, FF_SLOT), D_MODEL, FF_SLOT)
    return loss, dx, g, small


def _adamw(w, g, m, v):
    m = ADAM_B1 * m + (1.0 - ADAM_B1) * g
    v = ADAM_B2 * v + (1.0 - ADAM_B2) * (g * g)
    m_hat = m / (1.0 - ADAM_B1 ** ADAM_STEP)
    v_hat = v / (1.0 - ADAM_B2 ** ADAM_STEP)
    return -ADAM_LR * (m_hat / (jnp.sqrt(v_hat) + ADAM_EPS) + ADAM_WD * w), m, v


def _sum_in_device_order(r_ref):
    g = r_ref[0].astype(F32)
    for j in range(1, r_ref.shape[0]):
        g = g + r_ref[j].astype(F32)
    return g


def _adamw_sharded(name, recvs, w, m, v):
    layers, r, c = w.shape
    tr = math.gcd(r, 512)
    flat = [a for per_layer in recvs for a in per_layer]

    def body(*refs):
        r_refs, (w_ref, m_ref, v_ref) = refs[:len(flat)], refs[len(flat):len(flat) + 3]
        g_ref, d_ref, nm_ref, nv_ref = refs[-4:]
        layer = pl.program_id(0)
        g, pos = None, 0
        for li, per_layer in enumerate(recvs):
            total = None
            for ref in r_refs[pos:pos + len(per_layer)]:
                part = _sum_in_device_order(ref)
                total = part if total is None else total + part
            pos += len(per_layer)
            g = total if g is None else jnp.where(layer == li, total, g)
        g_ref[...] = g
        d_ref[...], nm_ref[...], nv_ref[...] = _adamw(w_ref[...], g, m_ref[...], v_ref[...])

    blk = pl.BlockSpec((None, tr, c), lambda l, i: (l, i, 0))
    return _call(name, body, (layers, r // tr),
                 [pl.BlockSpec((a.shape[0], tr, c), lambda l, i: (0, i, 0)) for a in flat] + [blk] * 3,
                 [blk] * 4, [_sds(w.shape, F32)] * 4, (*flat, w, m, v))


def _adamw_small(recvs, ws, ms, vs, own_row, losses):
    n = len(recvs)

    def body(*refs):
        r_refs, w_refs, m_refs, v_refs = (refs[i * n:(i + 1) * n] for i in range(4))
        outs, scr = refs[4 * n + 1:8 * n + 2], refs[8 * n + 2:]
        outs[-1][...] = _sum_in_device_order(refs[4 * n])
        me = _my_place()[3]
        for a in range(n):
            g = _sum_in_device_order(r_refs[a])
            if own_row[a]:
                scr[0][...] = g
                g = scr[0][pl.ds(me, 1), :]
            g_ref, d_ref, nm_ref, nv_ref = outs[4 * a:4 * a + 4]
            g_ref[...] = g
            d_ref[...], nm_ref[...], nv_ref[...] = _adamw(w_refs[a][...], g, m_refs[a][...], v_refs[a][...])

    out_shape = []
    for w in ws:
        out_shape += [_sds(w.shape, F32)] * 4
    return pl.pallas_call(
        body, name="adamw_small", in_specs=[VMEM] * (4 * n + 1), out_specs=[VMEM] * (4 * n + 1),
        out_shape=out_shape + [_sds((1, 1), F32)], scratch_shapes=[pltpu.VMEM((N_DEV, GATE_DIM // N_DEV), F32)],
    )(*recvs, *ws, *ms, *vs, losses)


BIG = ("a_w_in", "a_w_out", "b_w_q_a", "b_w_q_b", "b_w_o", "kv_w_a", "kv_w_b", "mlp_w1", "mlp_w2")
SMALL = ("norm_mix_g", "norm_mlp_g", "a_ln_v_g", "a_ln_v_b", "a_w_s", "a_b_s", "b_q_norm_g", "kv_src_norm_g",
         "kv_a_norm_g", "final_norm_g")
WEIGHTS = ("norm_mix_g", "norm_mlp_g", "a_w_in", "a_ln_v_g", "a_ln_v_b", "a_w_s", "a_b_s", "a_w_out", "b_w_q_a",
           "b_q_norm_g", "b_w_q_b", "b_w_o", "kv_src_norm_g", "kv_w_a", "kv_a_norm_g", "kv_w_b", "mlp_w1", "mlp_w2",
           "final_norm_g")


def _two_d(name, a):
    if name in ("a_w_s", "a_b_s"):
        return a.reshape(a.shape[1:])
    return a.reshape(1, -1) if a.ndim == 1 else a


def _three_d(a):
    return a if a.ndim == 3 else a.reshape((1,) + a.shape)


def kernel(x, positions, norm_mix_g, norm_mlp_g, a_w_in, a_ln_v_g, a_ln_v_b, a_w_s, a_b_s, a_w_out, b_w_q_a, b_q_norm_g, b_w_q_b, b_w_o, kv_src_norm_g, kv_w_a, kv_a_norm_g, kv_w_b, mlp_w1, mlp_w2, final_norm_g, loss_target, m_norm_mix_g, m_norm_mlp_g, m_a_w_in, m_a_ln_v_g, m_a_ln_v_b, m_a_w_s, m_a_b_s, m_a_w_out, m_b_w_q_a, m_b_q_norm_g, m_b_w_q_b, m_b_w_o, m_kv_src_norm_g, m_kv_w_a, m_kv_a_norm_g, m_kv_w_b, m_mlp_w1, m_mlp_w2, m_final_norm_g, v_norm_mix_g, v_norm_mlp_g, v_a_w_in, v_a_ln_v_g, v_a_ln_v_b, v_a_w_s, v_a_b_s, v_a_w_out, v_b_w_q_a, v_b_q_norm_g, v_b_w_q_b, v_b_w_o, v_kv_src_norm_g, v_kv_w_a, v_kv_a_norm_g, v_kv_w_b, v_mlp_w1, v_mlp_w2, v_final_norm_g):
    w = dict(norm_mix_g=norm_mix_g, norm_mlp_g=norm_mlp_g, a_w_in=a_w_in, a_ln_v_g=a_ln_v_g, a_ln_v_b=a_ln_v_b,
             a_w_s=a_w_s, a_b_s=a_b_s, a_w_out=a_w_out, b_w_q_a=b_w_q_a, b_q_norm_g=b_q_norm_g, b_w_q_b=b_w_q_b,
             b_w_o=b_w_o, kv_src_norm_g=kv_src_norm_g, kv_w_a=kv_w_a, kv_a_norm_g=kv_a_norm_g, kv_w_b=kv_w_b,
             mlp_w1=mlp_w1, mlp_w2=mlp_w2, final_norm_g=final_norm_g)
    m = dict(norm_mix_g=m_norm_mix_g, norm_mlp_g=m_norm_mlp_g, a_w_in=m_a_w_in, a_ln_v_g=m_a_ln_v_g,
             a_ln_v_b=m_a_ln_v_b, a_w_s=m_a_w_s, a_b_s=m_a_b_s, a_w_out=m_a_w_out, b_w_q_a=m_b_w_q_a,
             b_q_norm_g=m_b_q_norm_g, b_w_q_b=m_b_w_q_b, b_w_o=m_b_w_o, kv_src_norm_g=m_kv_src_norm_g,
             kv_w_a=m_kv_w_a, kv_a_norm_g=m_kv_a_norm_g, kv_w_b=m_kv_w_b, mlp_w1=m_mlp_w1, mlp_w2=m_mlp_w2,
             final_norm_g=m_final_norm_g)
    v = dict(norm_mix_g=v_norm_mix_g, norm_mlp_g=v_norm_mlp_g, a_w_in=v_a_w_in, a_ln_v_g=v_a_ln_v_g,
             a_ln_v_b=v_a_ln_v_b, a_w_s=v_a_w_s, a_b_s=v_a_b_s, a_w_out=v_a_w_out, b_w_q_a=v_b_w_q_a,
             b_q_norm_g=v_b_q_norm_g, b_w_q_b=v_b_w_q_b, b_w_o=v_b_w_o, kv_src_norm_g=v_kv_src_norm_g,
             kv_w_a=v_kv_w_a, kv_a_norm_g=v_kv_a_norm_g, kv_w_b=v_kv_w_b, mlp_w1=v_mlp_w1, mlp_w2=v_mlp_w2,
             final_norm_g=v_final_norm_g)
    t = x.shape[1]

    first = ("a_w_in", "a_w_out", "a_ln_v_g", "a_ln_v_b")
    later = ("mlp_w1_0", "mlp_w2_0", "mlp_w1_1", "mlp_w2_1", "kv_w_a", "kv_w_b", "b_w_q_a", "b_w_q_b", "b_w_o")
    blocks = {k: _three_d(w[k]) for k in BIG if not k.startswith("mlp")}
    for k in ("mlp_w1", "mlp_w2"):
        blocks[k + "_0"], blocks[k + "_1"] = w[k][0:1], w[k][1:2]
    got, casts = _gather_first([blocks[k] if k in blocks else w[k] for k in first], [blocks[k] for k in later])
    wg = dict(zip(first, got))
    wg["a_w_out"] = wg["a_w_out"].reshape(GATE_DIM, D_MODEL)
    wg["a_ln_v_g"] = wg["a_ln_v_g"].reshape(1, GATE_DIM)
    wg["a_ln_v_b"] = wg["a_ln_v_b"].reshape(1, GATE_DIM)
    shards = dict(zip(later, casts))

    sm = {k: _two_d(k, w[k]) for k in SMALL if k not in ("a_ln_v_g", "a_ln_v_b")}
    sm["a_b_st"] = sm["a_b_s"].T
    inv_freq = (ROPE_THETA ** (-jnp.arange(0, QK_ROPE, 2, dtype=F32) / QK_ROPE)).reshape(1, QK_ROPE // 2)

    losses, dx, g, small = _local_step(x[0], positions.reshape(t, 1), loss_target[0], inv_freq, wg, sm, shards)

    names = ("a_w_in", "a_w_out")
    sums = _pair_reduce("pair_reduce_a", [g[k] for k in names], after=[g["mlp_w1_0"], g["mlp_w2_0"]])
    g.update(zip(names, _by_sequencer("exchange_last", _chip_exchange_comm(sums), OTHER_CHIPS, collective_id=1)))

    out = {}
    for k in BIG:
        recvs = [[g[k + "_0"]], [g[k + "_1"]]] if k.startswith("mlp") else [[g[k]]]
        res = _adamw_sharded("adamw_" + k, recvs, _three_d(w[k]), _three_d(m[k]), _three_d(v[k]))
        out[k] = [o.reshape(w[k].shape) for o in res]
    own_row = [k in ("a_ln_v_g", "a_ln_v_b") for k in SMALL]
    res = _adamw_small([small[k] for k in SMALL], [_two_d(k, w[k]) for k in SMALL], [_two_d(k, m[k]) for k in SMALL],
                       [_two_d(k, v[k]) for k in SMALL], own_row, losses)
    for i, k in enumerate(SMALL):
        out[k] = [o.reshape(w[k].shape) for o in res[4 * i:4 * i + 4]]

    return (res[-1].reshape(()), dx.reshape(x.shape), *[out[k][0] for k in WEIGHTS], *[out[k][1] for k in WEIGHTS],
            *[out[k][2] for k in WEIGHTS], *[out[k][3] for k in WEIGHTS])
```

```python
import math

import jax
import jax.numpy as jnp
from jax import lax
from jax.experimental import pallas as pl
from jax.experimental.pallas import tpu as pltpu
from jax.experimental.pallas import tpu_sc as plsc

F32, BF16 = jnp.float32, jnp.bfloat16
MESH = pl.DeviceIdType.MESH
ANY = pl.BlockSpec(memory_space=pl.ANY)
VMEM = pl.BlockSpec(memory_space=pltpu.VMEM)

N_DEV = 8
D_MODEL = 1024
CHUNK = 64
GMLP_BLOCK = 128
GATE_DIM = 2048
A_GROUPS = 8
A_GROUP_DIM = GATE_DIM // A_GROUPS
B_HEADS = 8
QK_NOPE, QK_ROPE, V_HEAD = 128, 64, 128
Q_LORA, KV_LORA = 384, 256
ROPE_THETA = 10000.0
D_FF = 4096
FF_SLOT = D_FF // N_DEV
EPS = 1e-6
ATT_SCALE = (QK_NOPE + QK_ROPE) ** -0.5

ADAM_LR, ADAM_B1, ADAM_B2, ADAM_EPS, ADAM_WD, ADAM_STEP = 0.001, 0.9, 0.999, 1e-08, 0.01, 10

TM = 256
TM_GATE = 128
VMEM_LIMIT = 56 * 1024 * 1024
INV_SQRT2 = 1.0 / math.sqrt(2.0)
INV_SQRT_2PI = 1.0 / math.sqrt(2.0 * math.pi)
LOG2_E = 1.0 / math.log(2.0)
HEADS_PER_STEP = 2


def _dot(a, b):
    return jnp.dot(a, b, preferred_element_type=F32)


def _dot_nt(a, b):
    return lax.dot_general(a, b, (((1,), (1,)), ((), ())), preferred_element_type=F32)


def _dot_tn(a, b):
    return lax.dot_general(a, b, (((0,), (0,)), ((), ())), preferred_element_type=F32)


def _rms_fwd(x, g):
    rstd = lax.rsqrt(jnp.mean(x * x, axis=-1, keepdims=True) + EPS)
    xhat = x * rstd
    return xhat * g, xhat, rstd


def _rms_bwd(dy, xhat, rstd, g):
    dxhat = dy * g
    dx = rstd * (dxhat - xhat * jnp.mean(dxhat * xhat, axis=-1, keepdims=True))
    return dx, jnp.sum(dy * xhat, axis=0, keepdims=True)


def _ln_fwd(v, g, b):
    mu = jnp.mean(v, axis=-1, keepdims=True)
    vc = v - mu
    rstd = lax.rsqrt(jnp.mean(vc * vc, axis=-1, keepdims=True) + EPS)
    vhat = vc * rstd
    return vhat * g + b, vhat, rstd


def _gelu(x):
    return 0.5 * x * (1.0 + lax.erf(x * INV_SQRT2))


def _gelu_and_grad(x):
    cdf = 0.5 * (1.0 + lax.erf(x * INV_SQRT2))
    return x * cdf, cdf + x * jnp.exp(-0.5 * x * x) * INV_SQRT_2PI


def _rope(x, cos, sin):
    x1, x2 = x[:, :QK_ROPE // 2], x[:, QK_ROPE // 2:]
    return jnp.concatenate([x1 * cos - x2 * sin, x2 * cos + x1 * sin], axis=-1)


def _gate_mask():
    row = lax.broadcasted_iota(jnp.int32, (GMLP_BLOCK, GMLP_BLOCK), 0)
    col = lax.broadcasted_iota(jnp.int32, (GMLP_BLOCK, GMLP_BLOCK), 1)
    return (col < CHUNK) | (row >= CHUNK)


def _att_mask(q0, tq, t):
    q = q0 + lax.broadcasted_iota(jnp.int32, (tq, t), 0)
    k = lax.broadcasted_iota(jnp.int32, (tq, t), 1)
    return jnp.right_shift(k, 6) <= jnp.right_shift(q, 6)


def _res(shape, imap=None):
    zeros = (0,) * len(shape)
    return pl.BlockSpec(shape, imap or (lambda i: zeros), pipeline_mode=pl.Buffered(1))


def _const(shape):
    zeros = (0,) * len(shape)
    return pl.BlockSpec(shape, lambda i: zeros)


def _row(d, tm=TM):
    return pl.BlockSpec((tm, d), lambda i: (i, 0))


def _heads(d):
    return pl.BlockSpec((B_HEADS, TM, d), lambda i: (0, i, 0))


def _sds(shape, dt):
    return jax.ShapeDtypeStruct(shape, dt)


def _acc(ref, val):
    @pl.when(pl.program_id(0) == 0)
    def _():
        ref[...] = jnp.zeros_like(ref)
    ref[...] += val


def _my_place():
    x, y, c = lax.axis_index("x"), lax.axis_index("y"), lax.axis_index("c")
    return x, y, c, 4 * x + 2 * y + c


def _peer(x, y, c, k):
    px = 1 - x if k & 4 else x
    py = 1 - y if k & 2 else y
    pc = 1 - c if k & 1 else c
    return (px, py, pc), 4 * px + 2 * py + pc


CHIPS = (2, 4, 6)


def _splits(ref):
    return len(ref.shape) >= 3 and ref.shape[1] % 32 == 0


def _piece(ref, block, half=None):
    if half is None or not _splits(ref):
        return ref.at[pl.ds(block, 1)]
    rows = ref.shape[1] // 2
    return ref.at[pl.ds(block, 1), pl.ds(half * rows, rows)]


def _gather_copy(sems, a, k, piece, to, src=None):
    return pltpu.make_async_remote_copy(
        src_ref=piece if src is None else src, dst_ref=piece, send_sem=sems[0].at[a, k], recv_sem=sems[1].at[a, k],
        device_id=to, device_id_type=MESH)


def _gather_start(srcs, outs, sems, only=None):
    x, y, c, me = _my_place()
    for a in range(len(srcs)) if only is None else (only,):
        mine = _piece(outs[a], me)
        pltpu.make_async_copy(srcs[a], mine, sems[2].at[a]).start()
        for k, rel in enumerate((1, 4, 2)):
            _gather_copy(sems, a, k, mine, _peer(x, y, c, rel)[0], src=srcs[a]).start()


def _gather_relay(srcs, outs, sems):
    x, y, c, _ = _my_place()
    sib = _peer(x, y, c, 1)[0]
    (xn, xn_i), (yn, yn_i) = _peer(x, y, c, 4), _peer(x, y, c, 2)
    for a in range(len(srcs)):
        out = outs[a]
        _gather_copy(sems, a, 1, _piece(out, xn_i), xn).wait_recv()
        _gather_copy(sems, a, 3, _piece(out, xn_i, 0), yn).start()
        _gather_copy(sems, a, 5, _piece(out, xn_i), sib).start()
        _gather_copy(sems, a, 2, _piece(out, yn_i), yn).wait_recv()
        if _splits(out):
            _gather_copy(sems, a, 4, _piece(out, yn_i, 1), xn).start()
        _gather_copy(sems, a, 6, _piece(out, yn_i), sib).start()


def _gather_finish(srcs, outs, sems):
    x, y, c, me = _my_place()
    sib = _peer(x, y, c, 1)[0]
    xn, yn, dg_i = _peer(x, y, c, 4)[0], _peer(x, y, c, 2)[0], _peer(x, y, c, 6)[1]
    n = len(srcs)
    for a in range(n):
        out = outs[a]
        _gather_copy(sems, a, 3, _piece(out, dg_i, 0), yn).wait_recv()
        _gather_copy(sems, a, 7, _piece(out, dg_i, 0), sib).start()
        if _splits(out):
            _gather_copy(sems, a, 4, _piece(out, dg_i, 1), xn).wait_recv()
            _gather_copy(sems, a, 8, _piece(out, dg_i, 1), sib).start()
    for a in range(n):
        out = outs[a]
        whole, half = _piece(out, me), _piece(out, me, 0)
        for k in (0, 5, 6):
            _gather_copy(sems, a, k, whole, sib).wait_recv()
        for k in (7, 8) if _splits(out) else (7,):
            _gather_copy(sems, a, k, half, sib).wait_recv()
        for k in (0, 1, 2):
            _gather_copy(sems, a, k, whole, sib, src=srcs[a]).wait_send()
        for k in (5, 6):
            _gather_copy(sems, a, k, whole, sib).wait_send()
        for k in (3, 4, 7, 8) if _splits(out) else (3, 7):
            _gather_copy(sems, a, k, half, sib).wait_send()
        pltpu.make_async_copy(srcs[a], whole, sems[2].at[a]).wait()


def _relay_sems(n):
    return [pltpu.SemaphoreType.DMA((n, 9)), pltpu.SemaphoreType.DMA((n, 9)), pltpu.SemaphoreType.DMA((n,))]


def _gather_sems(n):
    return [pltpu.SemaphoreType.DMA((n, 7)), pltpu.SemaphoreType.DMA((n, 7)), pltpu.SemaphoreType.DMA((n,))]


class _Comm:
    def __init__(self, args, out_shape, scratch, start, finish, relay=None):
        self.args, self.out_shape, self.scratch, self.start, self.finish = args, out_shape, scratch, start, finish
        self.relay = relay


def _gather_comm(shards):
    return _Comm(list(shards), [_sds((N_DEV,) + s.shape[1:], s.dtype) for s in shards], _relay_sems(len(shards)),
                 _gather_start, _gather_finish, relay=_gather_relay)


def _direct_copies(ins, outs, sems, wait, from_block):
    send_sems, recv_sems, local_sems = sems
    x, y, c, me = _my_place()
    for a in range(len(ins)):
        src = ins[a].at[pl.ds(me, 1)] if from_block[a] else ins[a]
        local = pltpu.make_async_copy(src, outs[a].at[pl.ds(me, 1)], local_sems.at[a])
        local.wait() if wait else local.start()
        for k in range(1, N_DEV):
            to, to_i = _peer(x, y, c, k)
            cp = pltpu.make_async_remote_copy(
                src_ref=ins[a].at[pl.ds(to_i, 1)] if from_block[a] else ins[a], dst_ref=outs[a].at[pl.ds(me, 1)],
                send_sem=send_sems.at[a, k - 1], recv_sem=recv_sems.at[a, k - 1], device_id=to, device_id_type=MESH)
            cp.wait() if wait else cp.start()


def _exchange_comm(grads=(), parts=()):
    ins = list(grads) + list(parts)
    from_block = [True] * len(grads) + [False] * len(parts)
    out_shape = [_sds(g.shape, g.dtype) for g in grads] + [_sds((N_DEV,) + p.shape[1:], p.dtype) for p in parts]

    def start(ins_, outs_, sems_):
        _direct_copies(ins_, outs_, sems_, False, from_block)

    def finish(ins_, outs_, sems_):
        _direct_copies(ins_, outs_, sems_, True, from_block)

    return _Comm(ins, out_shape, _gather_sems(len(ins)), start, finish)


def _chip_copies(ins, outs, sems, wait, rels, own):
    send_sems, recv_sems, local_sems = sems
    x, y, c, _ = _my_place()
    for a in range(len(ins)):
        if own:
            local = pltpu.make_async_copy(ins[a].at[pl.ds(2 * x + y, 1)], outs[a].at[pl.ds(len(rels), 1)],
                                          local_sems.at[a])
            local.wait() if wait else local.start()
        for i, j in enumerate(rels):
            to = _peer(x, y, c, CHIPS[j])[0]
            cp = pltpu.make_async_remote_copy(
                src_ref=ins[a].at[pl.ds(2 * to[0] + to[1], 1)], dst_ref=outs[a].at[pl.ds(i, 1)],
                send_sem=send_sems.at[a, i], recv_sem=recv_sems.at[a, i], device_id=to, device_id_type=MESH)
            cp.wait() if wait else cp.start()


def _chip_exchange_comm(sums, rels=(0, 1, 2), own=True):
    def start(ins_, outs_, sems_):
        _chip_copies(ins_, outs_, sems_, False, rels, own)

    def finish(ins_, outs_, sems_):
        _chip_copies(ins_, outs_, sems_, True, rels, own)

    n = len(sums)
    sems = [pltpu.SemaphoreType.DMA((n, len(rels))), pltpu.SemaphoreType.DMA((n, len(rels))),
            pltpu.SemaphoreType.DMA((n,))]
    return _Comm(list(sums), [_sds((len(rels) + own,) + s.shape[1:], s.dtype) for s in sums], sems, start, finish)


def _pair_reduce(name, grads, after=()):
    n = len(grads)
    n_chips = N_DEV // 2

    def body(*refs):
        g_refs, gh_refs, refs = refs[:n], refs[n:2 * n], refs[2 * n + len(after):]
        p_refs, land = refs[:n], refs[n:2 * n]
        send_sems, recv_sems = refs[2 * n:]
        x, y, c, _ = _my_place()
        sib = _peer(x, y, c, 1)[0]
        q = pl.program_id(0)

        def to_sibling(a, j):
            return pltpu.make_async_remote_copy(
                src_ref=gh_refs[a].at[j, pl.ds(1 - c, 1)], dst_ref=land[a].at[pl.ds(j, 1)],
                send_sem=send_sems.at[a, j], recv_sem=recv_sems.at[a, j], device_id=sib, device_id_type=MESH)

        @pl.when(q == 0)
        def _():
            for j in range(n_chips):
                for a in range(n):
                    to_sibling(a, j).start()

        for a in range(n):
            to_sibling(a, q).wait_recv()
            p_refs[a][...] = (g_refs[a][0, pl.ds(c, 1)].astype(F32) + land[a][pl.ds(q, 1)].astype(F32)).astype(BF16)

        @pl.when(q == n_chips - 1)
        def _():
            for a in range(n):
                for j in range(n_chips):
                    to_sibling(a, j).wait_send()

    views = [g.reshape((n_chips, 2) + g.shape[1:]) for g in grads]
    res = pl.pallas_call(
        body, name=name, grid=(n_chips,),
        in_specs=[pl.BlockSpec((1, 2) + g.shape[1:], lambda q: (q, 0, 0, 0)) for g in grads]
        + [ANY] * (n + len(after)),
        out_specs=[pl.BlockSpec((1,) + g.shape[1:], lambda q: (q, 0, 0)) for g in grads],
        out_shape=[_sds((n_chips,) + g.shape[1:], BF16) for g in grads],
        scratch_shapes=[pltpu.VMEM((n_chips,) + g.shape[1:], BF16) for g in grads]
        + [pltpu.SemaphoreType.DMA((n, n_chips)), pltpu.SemaphoreType.DMA((n, n_chips))],
        compiler_params=pltpu.CompilerParams(dimension_semantics=("arbitrary",), vmem_limit_bytes=VMEM_LIMIT),
    )(*views, *views, *after)
    return list(res)


def _pair_exchange_comm(grads):
    n, n_chips = len(grads), N_DEV // 2

    def copies(ins, outs, sems, wait):
        x, y, c, _ = _my_place()
        for j in range(n_chips):
            for a in range(n):
                cp = pltpu.make_async_remote_copy(
                    src_ref=ins[a].at[j, pl.ds(1 - c, 1)], dst_ref=outs[a].at[pl.ds(j, 1)], send_sem=sems[0].at[a, j],
                    recv_sem=sems[1].at[a, j], device_id=_peer(x, y, c, 1)[0], device_id_type=MESH)
                cp.wait() if wait else cp.start()

    views = [g.reshape((n_chips, 2) + g.shape[1:]) for g in grads]
    sems = [pltpu.SemaphoreType.DMA((n, n_chips)), pltpu.SemaphoreType.DMA((n, n_chips))]
    return _Comm(views, [_sds((n_chips,) + g.shape[1:], g.dtype) for g in grads], sems,
                 lambda i, o, s: copies(i, o, s, False), lambda i, o, s: copies(i, o, s, True))


def _pair_add(name, grads, landed, after=()):
    n, n_chips = len(grads), N_DEV // 2

    def body(*refs):
        g_refs, l_refs, p_refs = refs[:n], refs[n:2 * n], refs[2 * n:]
        c = lax.axis_index("c")
        for a in range(n):
            p_refs[a][...] = (g_refs[a][0, pl.ds(c, 1)].astype(F32) + l_refs[a][...].astype(F32)).astype(BF16)

    views = [g.reshape((n_chips, 2) + g.shape[1:]) for g in grads]
    blocks = [pl.BlockSpec((1,) + g.shape[1:], lambda q: (q, 0, 0)) for g in grads]
    return _call(name, body, (n_chips,),
                 [pl.BlockSpec((1, 2) + g.shape[1:], lambda q: (q, 0, 0, 0)) for g in grads] + blocks, blocks,
                 [_sds((n_chips,) + g.shape[1:], BF16) for g in grads], (*views, *landed), after=after)


def _call(name, body, grid, in_specs, out_specs, out_shape, args, scratch=(), after=()):
    ni, na = len(in_specs), len(after)

    def ordered(*refs):
        body(*refs[:ni], *refs[ni + na:])

    return list(pl.pallas_call(
        ordered if after else body, name=name, grid=grid, in_specs=list(in_specs) + [ANY] * na,
        out_specs=list(out_specs), out_shape=list(out_shape), scratch_shapes=list(scratch),
        compiler_params=pltpu.CompilerParams(dimension_semantics=("arbitrary",) * len(grid),
                                             vmem_limit_bytes=VMEM_LIMIT))(*args, *after))


SIBLING_AND_NEIGHBOURS, OTHER_CHIPS, EVERYONE = (1, 4, 2), CHIPS, tuple(range(1, N_DEV))


def _by_sequencer(name, comm, peers, collective_id):
    src = [jax.new_ref(a, memory_space=pltpu.MemorySpace.HBM) for a in comm.args]
    dst = [jax.empty_ref(s, memory_space=pltpu.MemorySpace.HBM) for s in comm.out_shape]

    @pl.kernel(mesh=plsc.ScalarSubcoreMesh(axis_name="sequencer", num_cores=1), name=name,
               scratch_types=tuple(comm.scratch), compiler_params=pltpu.CompilerParams(collective_id=collective_id))
    def launch(*sems):
        x, y, c, _ = _my_place()
        barrier = pltpu.get_barrier_semaphore()
        for k in peers:
            pl.semaphore_signal(barrier, inc=1, device_id=_peer(x, y, c, k)[0], device_id_type=MESH)
        pl.semaphore_wait(barrier, len(peers))
        comm.start(src, dst, sems)
        if comm.relay is not None:
            comm.relay(src, dst, sems)
        comm.finish(src, dst, sems)

    launch()
    return [d[...] for d in dst]


def _gather_first(first, later):
    nf, nl = len(first), len(later)
    dts = [BF16] * (nf - 2) + [F32, F32]

    def body(*refs):
        ins, refs = refs[:nf + nl], refs[nf + nl:]
        outs, refs = refs[:nf], refs[nf:]
        casts, refs = refs[:nl], refs[nl:]
        stage, sems = refs[:nf], refs[nf:]
        for a in range(nf):
            stage[a][...] = ins[a][...].astype(dts[a])
            _gather_start(stage, outs, sems, only=a)
        for a in range(nl):
            casts[a][...] = ins[nf + a][...].astype(BF16)
        _gather_relay(stage, outs, sems)
        _gather_finish(stage, outs, sems)

    res = pl.pallas_call(
        body, name="gather_first",
        in_specs=[VMEM] * (nf + nl), out_specs=[ANY] * nf + [VMEM] * nl,
        out_shape=[_sds((N_DEV,) + s.shape[1:], dt) for s, dt in zip(first, dts)]
        + [_sds(s.shape, BF16) for s in later],
        scratch_shapes=[pltpu.VMEM(s.shape, dt) for s, dt in zip(first, dts)] + _relay_sems(nf),
        compiler_params=pltpu.CompilerParams(vmem_limit_bytes=VMEM_LIMIT),
    )(*first, *later)
    return list(res[:nf]), list(res[nf:])


def _a_mix_fwd(x, g, w_in, ln_g, ln_b, w_s, b_st, w_out):
    t = x.shape[0]
    nblk = TM // GMLP_BLOCK

    def body(x_ref, g_ref, win_ref, lng_ref, lnb_ref, ws_ref, bst_ref, wout_ref, h_ref, z_ref, gated_scr):
        xv = x_ref[...]
        hb = _rms_fwd(xv, g_ref[...])[0].astype(BF16)
        for d in range(N_DEV):
            z_ref[:, d * FF_SLOT:(d + 1) * FF_SLOT] = _dot(hb, win_ref[d])
        u = _gelu(z_ref[:, :GATE_DIM])
        vb = _ln_fwd(_gelu(z_ref[:, GATE_DIM:]), lng_ref[...], lnb_ref[...])[0].astype(BF16)
        mask = _gate_mask()
        for gi in range(A_GROUPS):
            wm = jnp.where(mask, ws_ref[gi], 0.0).astype(BF16)
            bias = bst_ref[:, gi:gi + 1]
            cs = slice(gi * A_GROUP_DIM, (gi + 1) * A_GROUP_DIM)
            for n in range(nblk):
                rs = slice(n * GMLP_BLOCK, (n + 1) * GMLP_BLOCK)
                sv = _dot(wm, vb[rs, cs]) + bias
                gated_scr[rs, cs] = (u[rs, cs] * sv).astype(BF16)
        h_ref[...] = xv + _dot(gated_scr[...], wout_ref[...])

    return _call(
        "a_mix_fwd", body, (t // TM,),
        [_row(D_MODEL), _res((1, D_MODEL)), _res((N_DEV, D_MODEL, FF_SLOT)), _res((1, GATE_DIM)),
         _res((1, GATE_DIM)), _res((A_GROUPS, GMLP_BLOCK, GMLP_BLOCK)), _res((GMLP_BLOCK, A_GROUPS)),
         _res((GATE_DIM, D_MODEL))],
        [_row(D_MODEL), _row(2 * GATE_DIM), _row(GATE_DIM)],
        [_sds((t, D_MODEL), F32), _sds((t, 2 * GATE_DIM), F32), _sds((t, GATE_DIM), BF16)],
        (x, g, w_in, ln_g, ln_b, w_s, b_st, w_out))


MLP_W_SPECS = (_res((N_DEV, D_MODEL, FF_SLOT)), _res((N_DEV, FF_SLOT, D_MODEL)))


def _mlp_fwd(h, g, w1, w2):
    t = h.shape[0]

    def body(h_ref, g_ref, w1_ref, w2_ref, o_ref, a_ref):
        hv = h_ref[...]
        hb = _rms_fwd(hv, g_ref[...])[0].astype(BF16)
        o_ref[...] = hv
        for d in range(N_DEV):
            a = _dot(hb, w1_ref[d])
            a_ref[:, d * FF_SLOT:(d + 1) * FF_SLOT] = a
            r = jnp.maximum(a, 0.0)
            o_ref[...] += _dot((r * r).astype(BF16), w2_ref[d])

    return _call(
        "mlp_fwd", body, (t // TM,), [_row(D_MODEL), _res((1, D_MODEL)), *MLP_W_SPECS],
        [_row(D_MODEL), _row(D_FF)], [_sds((t, D_MODEL), F32), _sds((t, D_FF), F32)], (h, g, w1, w2))


def _mlp_fwd_loss(h, g, w1, w2, final_g, target):
    t = h.shape[0]

    def body(h_ref, g_ref, w1_ref, w2_ref, fg_ref, t_ref, a_ref, loss_ref, dh_ref, dg_ref):
        hv = h_ref[...]
        hb = _rms_fwd(hv, g_ref[...])[0].astype(BF16)
        out = hv
        for d in range(N_DEV):
            a = _dot(hb, w1_ref[d])
            a_ref[:, d * FF_SLOT:(d + 1) * FF_SLOT] = a
            r = jnp.maximum(a, 0.0)
            out = out + _dot((r * r).astype(BF16), w2_ref[d])
        y, xhat, rstd = _rms_fwd(out, fg_ref[...])
        err = y - t_ref[...]
        part = 0.5 * jnp.sum(jnp.mean(err * err, axis=-1, keepdims=True), axis=0, keepdims=True)
        dx, dg = _rms_bwd(err * (1.0 / D_MODEL), xhat, rstd, fg_ref[...])
        dh_ref[...] = dx
        _acc(dg_ref, dg)
        _acc(loss_ref, part)

    return _call(
        "mlp_fwd_loss", body, (t // TM,),
        [_row(D_MODEL), _res((1, D_MODEL)), *MLP_W_SPECS, _res((1, D_MODEL)), _row(D_MODEL)],
        [_row(D_FF), _const((1, 1)), _row(D_MODEL), _const((1, D_MODEL))],
        [_sds((t, D_FF), F32), _sds((1, 1), F32), _sds((t, D_MODEL), F32), _sds((1, D_MODEL), F32)],
        (h, g, w1, w2, final_g, target))


KVQ_W_SPECS = (_res((1, D_MODEL)), _res((D_MODEL, KV_LORA + QK_ROPE)), _res((1, KV_LORA)),
               _res((B_HEADS, KV_LORA, QK_NOPE + V_HEAD)), _res((1, D_MODEL)), _res((D_MODEL, Q_LORA)),
               _res((1, Q_LORA)), _res((B_HEADS, Q_LORA, QK_NOPE + QK_ROPE)))


def _kvq_fwd(h, pos, inv_freq, kvq_w):
    t = h.shape[0]
    half = QK_ROPE // 2

    def body(h_ref, pos_ref, invf_ref, srcg_ref, wkva_ref, kvag_ref, wkvb_ref, mixg_ref, wqa_ref, qg_ref, wqb_ref,
             ckv_ref, k_ref, v_ref, cqpre_ref, q_ref, cos_ref, sin_ref):
        hv = h_ref[...]
        xhat = hv * lax.rsqrt(jnp.mean(hv * hv, axis=-1, keepdims=True) + EPS)
        ang = pos_ref[...].astype(F32) * invf_ref[...]
        cos, sin = jnp.cos(ang), jnp.sin(ang)
        cos_ref[...] = cos
        sin_ref[...] = sin
        ckv = _dot((xhat * srcg_ref[...]).astype(BF16), wkva_ref[...])
        ckv_ref[...] = ckv
        cb = _rms_fwd(ckv[:, :KV_LORA], kvag_ref[...])[0].astype(BF16)
        kpe = _rope(ckv[:, KV_LORA:], cos, sin).astype(BF16)
        for hd in range(B_HEADS):
            kv = _dot(cb, wkvb_ref[hd])
            k_ref[hd, :, 0:QK_NOPE] = kv[:, :QK_NOPE].astype(BF16)
            k_ref[hd, :, QK_NOPE:] = kpe
            v_ref[hd] = kv[:, QK_NOPE:].astype(BF16)
        cqpre = _dot((xhat * mixg_ref[...]).astype(BF16), wqa_ref[...])
        cqpre_ref[...] = cqpre
        cqb = _rms_fwd(cqpre, qg_ref[...])[0].astype(BF16)
        for hd in range(B_HEADS):
            q = _dot(cqb, wqb_ref[hd])
            q_ref[hd, :, 0:QK_NOPE] = q[:, :QK_NOPE].astype(BF16)
            q_ref[hd, :, QK_NOPE:] = _rope(q[:, QK_NOPE:], cos, sin).astype(BF16)

    return _call(
        "kvq_fwd", body, (t // TM,), [_row(D_MODEL), _row(1), _res((1, half)), *KVQ_W_SPECS],
        [_row(KV_LORA + QK_ROPE), _heads(QK_NOPE + QK_ROPE), _heads(V_HEAD), _row(Q_LORA),
         _heads(QK_NOPE + QK_ROPE), _row(half), _row(half)],
        [_sds((t, KV_LORA + QK_ROPE), F32), _sds((B_HEADS, t, QK_NOPE + QK_ROPE), BF16),
         _sds((B_HEADS, t, V_HEAD), BF16), _sds((t, Q_LORA), F32), _sds((B_HEADS, t, QK_NOPE + QK_ROPE), BF16),
         _sds((t, half), F32), _sds((t, half), F32)],
        (h, pos, inv_freq, *kvq_w))


def _softmax_rows(q, k_ref, k):
    past, upto = k * TM, (k + 1) * TM
    s = _dot_nt(q, k_ref[0:upto, :])
    own = jnp.where(_att_mask(0, TM, TM), s[:, past:], jnp.finfo(F32).min)
    s = own if k == 0 else jnp.concatenate([s[:, :past], own], axis=1)
    e = jnp.exp2((s - jnp.max(s, axis=-1, keepdims=True)) * (ATT_SCALE * LOG2_E))
    return e * (1.0 / jnp.sum(e, axis=-1, keepdims=True))


def _for_my_tile(i, nq, fn):
    for k in range(nq):
        @pl.when(i == k)
        def _(k=k):
            fn(k)


def _attn_fwd(h, q, k, v, w_o):
    t = h.shape[0]
    nq, hps = t // TM, HEADS_PER_STEP

    def body(h_ref, q_ref, k_ref, v_ref, wo_ref, o_ref, att_ref):
        i, pair = pl.program_id(0), pl.program_id(1)

        @pl.when(pair == 0)
        def _():
            o_ref[...] = h_ref[...]

        def tile(kt):
            proj = None
            for j in range(hps):
                hd = pair * hps + j
                p = _softmax_rows(q_ref[j], k_ref.at[hd], kt)
                ob = _dot(p.astype(BF16), v_ref[hd, 0:(kt + 1) * TM, :]).astype(BF16)
                att_ref[j] = ob
                proj = _dot(ob, wo_ref[hd]) if proj is None else proj + _dot(ob, wo_ref[hd])
            o_ref[...] += proj

        _for_my_tile(i, nq, tile)

    def per_head(d):
        return pl.BlockSpec((hps, TM, d), lambda i, pair: (pair, i, 0))

    def resident(shape):
        zeros = (0,) * len(shape)
        return pl.BlockSpec(shape, lambda i, pair: zeros, pipeline_mode=pl.Buffered(1))

    tile_spec = pl.BlockSpec((TM, D_MODEL), lambda i, pair: (i, 0))
    return _call(
        "attn_fwd", body, (nq, B_HEADS // hps),
        [tile_spec, per_head(QK_NOPE + QK_ROPE), resident((B_HEADS, t, QK_NOPE + QK_ROPE)),
         resident((B_HEADS, t, V_HEAD)), resident((B_HEADS, V_HEAD, D_MODEL))],
        [tile_spec, per_head(V_HEAD)], [_sds((t, D_MODEL), F32), _sds((B_HEADS, t, V_HEAD), BF16)],
        (h, q, k, v, w_o))


def _mlp_bwd(h, a, dho, g, w1, w2, layer, after=()):
    t = h.shape[0]

    def body(h_ref, a_ref, dho_ref, g_ref, w1_ref, w2_ref, dhi_ref, dg_ref, hn_ref, f_ref, da_ref, dhib_ref):
        gv = g_ref[...]
        y, xhat, rstd = _rms_fwd(h_ref[...], gv)
        hn_ref[...] = y.astype(BF16)
        dho_v = dho_ref[...]
        dhob = dho_v.astype(BF16)
        dhn = jnp.zeros((TM, D_MODEL), F32)
        for d in range(N_DEV):
            cs = slice(d * FF_SLOT, (d + 1) * FF_SLOT)
            r = jnp.maximum(a_ref[:, cs], 0.0)
            f_ref[:, cs] = (r * r).astype(BF16)
            da = (_dot_nt(dhob, w2_ref[d]) * (2.0 * r)).astype(BF16)
            da_ref[:, cs] = da
            dhn = dhn + _dot_nt(da, w1_ref[d])
        dx, dg = _rms_bwd(dhn, xhat, rstd, gv)
        dhi = dho_v + dx
        dhi_ref[...] = dhi
        dhib_ref[...] = dhi.astype(BF16)
        _acc(dg_ref, dg)

    return _call(
        f"mlp_bwd_{layer}", body, (t // TM,),
        [_row(D_MODEL), _row(D_FF), _row(D_MODEL), _res((1, D_MODEL)), *MLP_W_SPECS],
        [_row(D_MODEL), _const((1, D_MODEL)), _row(D_MODEL), _row(D_FF), _row(D_FF), _row(D_MODEL)],
        [_sds((t, D_MODEL), F32), _sds((1, D_MODEL), F32), _sds((t, D_MODEL), BF16), _sds((t, D_FF), BF16),
         _sds((t, D_FF), BF16), _sds((t, D_MODEL), BF16)],
        (h, a, dho, g, w1, w2), after=after)


def _attn_bwd(dh, q, k, v, w_o, cos, sin, after=()):
    t = dh.shape[0]
    half, hps = QK_ROPE // 2, HEADS_PER_STEP

    def body(dh_ref, q_ref, k_ref, v_ref, wo_ref, cos_ref, sin_ref, dq_ref, dk_ref, dv_ref):
        i = pl.program_id(1)

        @pl.when(i == 0)
        def _():
            dk_ref[...] = jnp.zeros_like(dk_ref)
            dv_ref[...] = jnp.zeros_like(dv_ref)

        def tile(kt):
            keys = slice(0, (kt + 1) * TM)
            for j in range(hps):
                qj = q_ref[j]
                do = _dot_nt(dh_ref[kt * TM:(kt + 1) * TM, :], wo_ref[j]).astype(BF16)
                p = _softmax_rows(qj, k_ref.at[j], kt)
                dp = _dot_nt(do, v_ref[j, keys, :])
                ds = (p * (dp - jnp.sum(p * dp, axis=-1, keepdims=True)) * ATT_SCALE).astype(BF16)
                dq = _dot(ds, k_ref[j, keys, :])
                dq_ref[j, :, 0:QK_NOPE] = dq[:, :QK_NOPE].astype(BF16)
                dq_ref[j, :, QK_NOPE:] = _rope(dq[:, QK_NOPE:], cos_ref[...], -sin_ref[...]).astype(BF16)
                dk_ref[j, keys, :] += _dot_tn(ds, qj)
                dv_ref[j, keys, :] += _dot_tn(p.astype(BF16), do)

        _for_my_tile(i, t // TM, tile)

    def per_pair(rows, d, tiled):
        return pl.BlockSpec((hps, rows, d), (lambda pair, i: (pair, i, 0)) if tiled else (lambda pair, i: (pair, 0, 0)))

    def tile(d):
        return pl.BlockSpec((TM, d), lambda pair, i: (i, 0))

    return _call(
        "attn_bwd", body, (B_HEADS // hps, t // TM),
        [pl.BlockSpec((t, D_MODEL), lambda pair, i: (0, 0), pipeline_mode=pl.Buffered(1)),
         per_pair(TM, QK_NOPE + QK_ROPE, True), per_pair(t, QK_NOPE + QK_ROPE, False), per_pair(t, V_HEAD, False),
         per_pair(V_HEAD, D_MODEL, False), tile(half), tile(half)],
        [per_pair(TM, QK_NOPE + QK_ROPE, True), per_pair(t, QK_NOPE + QK_ROPE, False), per_pair(t, V_HEAD, False)],
        [_sds((B_HEADS, t, QK_NOPE + QK_ROPE), BF16), _sds((B_HEADS, t, QK_NOPE + QK_ROPE), F32),
         _sds((B_HEADS, t, V_HEAD), F32)],
        (dh, q, k, v, w_o, cos, sin), after=after)


def _kvq_bwd(h, dh, ckv, cqpre, dq, dk, dv, cos, sin, kvq_w, after=()):
    t = h.shape[0]
    half, last = QK_ROPE // 2, t // TM - 1
    grad_shapes = [(D_MODEL, Q_LORA), (B_HEADS, Q_LORA, QK_NOPE + QK_ROPE), (D_MODEL, KV_LORA + QK_ROPE),
                   (B_HEADS, KV_LORA, QK_NOPE + V_HEAD)]

    def body(h_ref, dh_ref, ckv_ref, cqpre_ref, dq_ref, dk_ref, dv_ref, cos_ref, sin_ref,
             srcg_ref, wkva_ref, kvag_ref, wkvb_ref, mixg_ref, wqa_ref, qg_ref, wqb_ref,
             dhi_ref, dmixg_ref, dsrcg_ref, dqg_ref, dkvag_ref, gqa_ref, gqb_ref, gkva_ref, gkvb_ref,
             aqa, aqb, akva, akvb):
        @pl.when(pl.program_id(0) == 0)
        def _():
            for acc in (aqa, aqb, akva, akvb):
                acc[...] = jnp.zeros_like(acc)

        hv = h_ref[...]
        rstd = lax.rsqrt(jnp.mean(hv * hv, axis=-1, keepdims=True) + EPS)
        xhat = hv * rstd
        mixg, srcg, qg, kvag = mixg_ref[...], srcg_ref[...], qg_ref[...], kvag_ref[...]
        cq, cqhat, crstd = _rms_fwd(cqpre_ref[...], qg)
        cqb = cq.astype(BF16)
        dcq = jnp.zeros((TM, Q_LORA), F32)
        for hd in range(B_HEADS):
            dcq = dcq + _dot_nt(dq_ref[hd], wqb_ref[hd])
            aqb[hd] += _dot_tn(cqb, dq_ref[hd])
        dcqpre, dqg = _rms_bwd(dcq, cqhat, crstd, qg)
        dcqpre_b = dcqpre.astype(BF16)
        aqa[...] += _dot_tn((xhat * mixg).astype(BF16), dcqpre_b)
        dxq, dmixg = _rms_bwd(_dot_nt(dcqpre_b, wqa_ref[...]), xhat, rstd, mixg)
        ckv = ckv_ref[...]
        c, chat, krstd = _rms_fwd(ckv[:, :KV_LORA], kvag)
        cb = c.astype(BF16)
        dc = jnp.zeros((TM, KV_LORA), F32)
        dkpe = jnp.zeros((TM, QK_ROPE), F32)
        for hd in range(B_HEADS):
            dkv = jnp.concatenate([dk_ref[hd, :, 0:QK_NOPE], dv_ref[hd]], axis=-1).astype(BF16)
            akvb[hd] += _dot_tn(cb, dkv)
            dc = dc + _dot_nt(dkv, wkvb_ref[hd])
            dkpe = dkpe + dk_ref[hd, :, QK_NOPE:]
        dlat, dkvag = _rms_bwd(dc, chat, krstd, kvag)
        dpe = _rope(dkpe, cos_ref[...], -sin_ref[...])
        dckv_b = jnp.concatenate([dlat, dpe], axis=-1).astype(BF16)
        akva[...] += _dot_tn((xhat * srcg).astype(BF16), dckv_b)
        dxk, dsrcg = _rms_bwd(_dot_nt(dckv_b, wkva_ref[...]), xhat, rstd, srcg)
        dhi_ref[...] = dh_ref[...] + dxq + dxk
        _acc(dmixg_ref, dmixg)
        _acc(dsrcg_ref, dsrcg)
        _acc(dqg_ref, dqg)
        _acc(dkvag_ref, dkvag)

        @pl.when(pl.program_id(0) == last)
        def _():
            for out, acc in ((gqa_ref, aqa), (gqb_ref, aqb), (gkva_ref, akva), (gkvb_ref, akvb)):
                out[...] = acc[...].astype(BF16)

    return _call(
        "kvq_bwd", body, (t // TM,),
        [_row(D_MODEL), _row(D_MODEL), _row(KV_LORA + QK_ROPE), _row(Q_LORA), _heads(QK_NOPE + QK_ROPE),
         _heads(QK_NOPE + QK_ROPE), _heads(V_HEAD), _row(half), _row(half), *KVQ_W_SPECS],
        [_row(D_MODEL), _const((1, D_MODEL)), _const((1, D_MODEL)), _const((1, Q_LORA)), _const((1, KV_LORA))]
        + [_const(s) for s in grad_shapes],
        [_sds((t, D_MODEL), F32), _sds((1, D_MODEL), F32), _sds((1, D_MODEL), F32), _sds((1, Q_LORA), F32),
         _sds((1, KV_LORA), F32)] + [_sds(s, BF16) for s in grad_shapes],
        (h, dh, ckv, cqpre, dq, dk, dv, cos, sin, *kvq_w), scratch=[pltpu.VMEM(s, F32) for s in grad_shapes],
        after=after)


def _a_mix_bwd(x, z, dh, g, w_in, ln_g, ln_b, w_s, b_st, w_out, after=()):
    t = x.shape[0]
    tm = TM_GATE
    nblk = tm // GMLP_BLOCK

    def body(x_ref, z_ref, dh_ref, g_ref, win_ref, lng_ref, lnb_ref, ws_ref, bst_ref, wout_ref,
             dx_ref, hn_ref, dz_ref, dg_ref, dlng_ref, dlnb_ref, dws_ref, dbs_ref, dvn_scr, gelu_grad_v):
        @pl.when(pl.program_id(0) == 0)
        def _():
            dws_ref[...] = jnp.zeros_like(dws_ref)
            dbs_ref[...] = jnp.zeros_like(dbs_ref)

        gv, lng = g_ref[...], lng_ref[...]
        y, xhat, rstd = _rms_fwd(x_ref[...], gv)
        hn_ref[...] = y.astype(BF16)
        dhv = dh_ref[...]
        dgated = _dot_nt(dhv.astype(BF16), wout_ref[...])
        u, gelu_grad_u = _gelu_and_grad(z_ref[:, :GATE_DIM])
        v, gelu_grad_v[...] = _gelu_and_grad(z_ref[:, GATE_DIM:])
        vn, vhat, lrstd = _ln_fwd(v, lng, lnb_ref[...])
        vb = vn.astype(BF16)
        mask = _gate_mask()
        for gi in range(A_GROUPS):
            wm = jnp.where(mask, ws_ref[gi], 0.0).astype(BF16)
            bias = bst_ref[:, gi:gi + 1]
            cs = slice(gi * A_GROUP_DIM, (gi + 1) * A_GROUP_DIM)
            dws = jnp.zeros((GMLP_BLOCK, GMLP_BLOCK), F32)
            dbs = jnp.zeros((GMLP_BLOCK, 1), F32)
            for n in range(nblk):
                rs = slice(n * GMLP_BLOCK, (n + 1) * GMLP_BLOCK)
                sv = _dot(wm, vb[rs, cs]) + bias
                dz_ref[rs, cs] = (dgated[rs, cs] * sv * gelu_grad_u[rs, cs]).astype(BF16)
                dsv = dgated[rs, cs] * u[rs, cs]
                dsvb = dsv.astype(BF16)
                dws = dws + _dot_nt(dsvb, vb[rs, cs])
                dbs = dbs + jnp.sum(dsv, axis=-1, keepdims=True)
                dvn_scr[rs, cs] = _dot_tn(wm, dsvb)
            dws_ref[gi] += jnp.where(mask, dws, 0.0)
            dbs_ref[gi] += dbs
        dvn = dvn_scr[...]
        dvhat = dvn * lng
        dv = lrstd * (dvhat - jnp.mean(dvhat, axis=-1, keepdims=True)
                      - vhat * jnp.mean(dvhat * vhat, axis=-1, keepdims=True))
        dz_ref[:, GATE_DIM:] = (dv * gelu_grad_v[...]).astype(BF16)
        dhn = jnp.zeros((tm, D_MODEL), F32)
        for d in range(N_DEV):
            dhn = dhn + _dot_nt(dz_ref[:, d * FF_SLOT:(d + 1) * FF_SLOT], win_ref[d])
        dx, dg = _rms_bwd(dhn, xhat, rstd, gv)
        dx_ref[...] = dhv + dx
        _acc(dg_ref, dg)
        _acc(dlng_ref, jnp.sum(dvn * vhat, axis=0, keepdims=True))
        _acc(dlnb_ref, jnp.sum(dvn, axis=0, keepdims=True))

    return _call(
        "a_mix_bwd", body, (t // tm,),
        [_row(D_MODEL, tm), _row(2 * GATE_DIM, tm), _row(D_MODEL, tm), _res((1, D_MODEL)),
         _res((N_DEV, D_MODEL, FF_SLOT)), _res((1, GATE_DIM)), _res((1, GATE_DIM)),
         _res((A_GROUPS, GMLP_BLOCK, GMLP_BLOCK)), _res((GMLP_BLOCK, A_GROUPS)), _res((GATE_DIM, D_MODEL))],
        [_row(D_MODEL, tm), _row(D_MODEL, tm), _row(2 * GATE_DIM, tm),
         _const((1, D_MODEL)), _const((1, GATE_DIM)), _const((1, GATE_DIM)),
         _const((A_GROUPS, GMLP_BLOCK, GMLP_BLOCK)), _const((A_GROUPS, GMLP_BLOCK, 1))],
        [_sds((t, D_MODEL), F32), _sds((t, D_MODEL), BF16),
         _sds((t, 2 * GATE_DIM), BF16), _sds((1, D_MODEL), F32), _sds((1, GATE_DIM), F32),
         _sds((1, GATE_DIM), F32), _sds((A_GROUPS, GMLP_BLOCK, GMLP_BLOCK), F32),
         _sds((A_GROUPS, GMLP_BLOCK, 1), F32)],
        (x, z, dh, g, w_in, ln_g, ln_b, w_s, b_st, w_out),
        scratch=[pltpu.VMEM((tm, GATE_DIM), F32), pltpu.VMEM((tm, GATE_DIM), F32)], after=after)


def _wgrad(name, a, b, a_spec, b_spec, m, n, after=()):
    def body(a_ref, b_ref, o_ref):
        o_ref[0] = _dot_tn(a_ref[...].astype(BF16), b_ref[...].astype(BF16)).astype(BF16)

    return _call(name, body, (N_DEV,), [a_spec, b_spec], [pl.BlockSpec((1, m, n), lambda d: (d, 0, 0))],
                 [_sds((N_DEV, m, n), BF16)], (a, b), after=after)[0]


def _full(t, d):
    return pl.BlockSpec((t, d), lambda i: (0, 0), pipeline_mode=pl.Buffered(1))


def _cols(t, d):
    return pl.BlockSpec((t, d), lambda i: (0, i))


def _head(t, d):
    return pl.BlockSpec((None, t, d), lambda i: (i, 0, 0))


def _local_step(x, pos, target, inv_freq, wg, sm, shards=None):
    t = x.shape[0]
    wg = dict(wg)
    dist = shards is not None
    mix_g = [sm["norm_mix_g"][l:l + 1] for l in range(2)]
    mlp_g = [sm["norm_mlp_g"][l:l + 1] for l in range(2)]

    ids = iter(range(2, 2 + 9))

    def gather(names):
        if dist:
            got = _by_sequencer("gather_" + names[0], _gather_comm([shards[k] for k in names]),
                                SIBLING_AND_NEIGHBOURS, next(ids))
            wg.update(zip(names, got))

    def send(name, names):
        if dist:
            comm = _exchange_comm(grads=[g[k] for k in names])
            g.update(zip(names, _by_sequencer("exchange_" + name, comm, EVERYONE, next(ids))))

    def send_sums(name, names, meanwhile):
        if not dist:
            meanwhile()
            return ()
        grads = [g[k] for k in names]
        landed = _by_sequencer("pair_exchange_" + name, _pair_exchange_comm(grads), (1,), next(ids))
        sums = _pair_add("pair_add_" + name, grads, landed, after=meanwhile())
        g.update(zip(names, _by_sequencer("exchange_" + name, _chip_exchange_comm(sums), OTHER_CHIPS, next(ids))))
        return sums

    def a_args():
        return (wg["a_w_in"], wg["a_ln_v_g"], wg["a_ln_v_b"], sm["a_w_s"], sm["a_b_st"], wg["a_w_out"])

    def kvq_w():
        return (sm["kv_src_norm_g"], wg["kv_w_a"], sm["kv_a_norm_g"], wg["kv_w_b"], mix_g[1], wg["b_w_q_a"],
                sm["b_q_norm_g"], wg["b_w_q_b"])

    gather(("mlp_w1_0", "mlp_w2_0"))
    h1, z, gated = _a_mix_fwd(x, mix_g[0], *a_args())
    gather(("kv_w_a", "kv_w_b", "b_w_q_a", "b_w_q_b", "b_w_o"))
    h2, a0 = _mlp_fwd(h1, mlp_g[0], wg["mlp_w1_0"], wg["mlp_w2_0"])
    if dist:
        wg["b_w_q_a"] = wg["b_w_q_a"].reshape(D_MODEL, Q_LORA)
        wg["kv_w_a"] = wg["kv_w_a"].reshape(D_MODEL, KV_LORA + QK_ROPE)
    gather(("mlp_w1_1", "mlp_w2_1"))
    ckv, k, v, cqpre, q, cos, sin = _kvq_fwd(h2, pos, inv_freq, kvq_w())
    h3, att = _attn_fwd(h2, q, k, v, wg["b_w_o"])
    a1, loss, dh4, d_final_g = _mlp_fwd_loss(h3, mlp_g[1], wg["mlp_w1_1"], wg["mlp_w2_1"], sm["final_norm_g"], target)

    g = {}
    dh3, d_mlp_g1, hn, f, da, dh3_b = _mlp_bwd(h3, a1, dh4, mlp_g[1], wg["mlp_w1_1"], wg["mlp_w2_1"], 1)
    dq, dk, dv = _attn_bwd(dh3_b, q, k, v, wg["b_w_o"], cos, sin)
    g["mlp_w1_1"] = _wgrad("wgrad_w1_1", hn, da, _full(t, D_MODEL), _cols(t, FF_SLOT), D_MODEL, FF_SLOT, after=[dq])
    g["mlp_w2_1"] = _wgrad("wgrad_w2_1", f, dh4, _cols(t, FF_SLOT), _full(t, D_MODEL), FF_SLOT, D_MODEL)

    def wgrad_w_o():
        g["b_w_o"] = _wgrad("wgrad_w_o", att, dh3_b, _head(t, V_HEAD), _full(t, D_MODEL), V_HEAD, D_MODEL)
        return [g["b_w_o"]]

    sums = send_sums("mlp_1", ("mlp_w1_1", "mlp_w2_1"), wgrad_w_o)
    dh2, d_mix_g1, d_src_g, d_q_g, d_kv_a_g, g_q_a, g["b_w_q_b"], g_kv_a, g["kv_w_b"] = _kvq_bwd(
        h2, dh3, ckv, cqpre, dq, dk, dv, cos, sin, kvq_w(), after=sums)
    g["b_w_q_a"] = g_q_a.reshape(N_DEV, D_MODEL // N_DEV, Q_LORA)
    g["kv_w_a"] = g_kv_a.reshape(N_DEV, D_MODEL // N_DEV, KV_LORA + QK_ROPE)
    qkv = ("b_w_o", "b_w_q_a", "b_w_q_b", "kv_w_a", "kv_w_b")
    landed = [g[k] for k in qkv]
    send("qkv", qkv)
    dh1, d_mlp_g0, hn, f, da, dh1_b = _mlp_bwd(h1, a0, dh2, mlp_g[0], wg["mlp_w1_0"], wg["mlp_w2_0"], 0,
                                               after=landed if dist else ())
    landed = [g["mlp_w1_1"], g["mlp_w2_1"]] if dist else ()
    g["mlp_w1_0"] = _wgrad("wgrad_w1_0", hn, da, _full(t, D_MODEL), _cols(t, FF_SLOT), D_MODEL, FF_SLOT, after=landed)
    g["mlp_w2_0"] = _wgrad("wgrad_w2_0", f, dh2, _cols(t, FF_SLOT), _full(t, D_MODEL), FF_SLOT, D_MODEL)

    def wgrad_a_w_out():
        g["a_w_out"] = _wgrad("wgrad_a_w_out", gated, dh1_b, _cols(t, GATE_DIM // N_DEV), _full(t, D_MODEL),
                              GATE_DIM // N_DEV, D_MODEL)
        return [g["a_w_out"]] + [g[k] for k in qkv]

    sums = send_sums("mlp_0", ("mlp_w1_0", "mlp_w2_0"), wgrad_a_w_out)
    dx, hn, dz, d_mix_g0, d_ln_g, d_ln_b, d_ws, d_bs = _a_mix_bwd(x, z, dh1, mix_g[0], *a_args(), after=sums)
    small = {
        "norm_mix_g": jnp.concatenate([d_mix_g0, d_mix_g1], axis=0),
        "norm_mlp_g": jnp.concatenate([d_mlp_g0, d_mlp_g1], axis=0),
        "a_ln_v_g": d_ln_g.reshape(N_DEV, GATE_DIM // N_DEV),
        "a_ln_v_b": d_ln_b.reshape(N_DEV, GATE_DIM // N_DEV),
        "a_w_s": d_ws.astype(BF16) if dist else d_ws,
        "a_b_s": d_bs.reshape(A_GROUPS, GMLP_BLOCK),
        "b_q_norm_g": d_q_g,
        "kv_src_norm_g": d_src_g,
        "kv_a_norm_g": d_kv_a_g,
        "final_norm_g": d_final_g,
    }
    if dist:
        parts = [small[k].reshape((1,) + small[k].shape) for k in SMALL] + [loss.reshape(1, 1, 1)]
        got = _by_sequencer("gather_small", _exchange_comm(parts=parts), EVERYONE, next(ids))
        small, loss = dict(zip(SMALL, got)), got[-1]
    g["a_w_in"] = _wgrad("wgrad_a_w_in", hn, dz, _full(t, D_MODEL), _cols(t, FF_SLOT), D_MODEL, FF_SLOT)
    return loss, dx, g, small


def _adamw(w, g, m, v):
    m = ADAM_B1 * m + (1.0 - ADAM_B1) * g
    v = ADAM_B2 * v + (1.0 - ADAM_B2) * (g * g)
    m_hat = m / (1.0 - ADAM_B1 ** ADAM_STEP)
    v_hat = v / (1.0 - ADAM_B2 ** ADAM_STEP)
    return -ADAM_LR * (m_hat / (jnp.sqrt(v_hat) + ADAM_EPS) + ADAM_WD * w), m, v


def _sum_in_device_order(r_ref):
    g = r_ref[0].astype(F32)
    for j in range(1, r_ref.shape[0]):
        g = g + r_ref[j].astype(F32)
    return g


def _adamw_sharded(name, recvs, w, m, v):
    layers, r, c = w.shape
    tr = math.gcd(r, 512)
    flat = [a for per_layer in recvs for a in per_layer]

    def body(*refs):
        r_refs, (w_ref, m_ref, v_ref) = refs[:len(flat)], refs[len(flat):len(flat) + 3]
        g_ref, d_ref, nm_ref, nv_ref = refs[-4:]
        layer = pl.program_id(0)
        g, pos = None, 0
        for li, per_layer in enumerate(recvs):
            total = None
            for ref in r_refs[pos:pos + len(per_layer)]:
                part = _sum_in_device_order(ref)
                total = part if total is None else total + part
            pos += len(per_layer)
            g = total if g is None else jnp.where(layer == li, total, g)
        g_ref[...] = g
        d_ref[...], nm_ref[...], nv_ref[...] = _adamw(w_ref[...], g, m_ref[...], v_ref[...])

    blk = pl.BlockSpec((None, tr, c), lambda l, i: (l, i, 0))
    return _call(name, body, (layers, r // tr),
                 [pl.BlockSpec((a.shape[0], tr, c), lambda l, i: (0, i, 0)) for a in flat] + [blk] * 3,
                 [blk] * 4, [_sds(w.shape, F32)] * 4, (*flat, w, m, v))


def _adamw_small(recvs, ws, ms, vs, own_row, losses):
    n = len(recvs)

    def body(*refs):
        r_refs, w_refs, m_refs, v_refs = (refs[i * n:(i + 1) * n] for i in range(4))
        outs, scr = refs[4 * n + 1:8 * n + 2], refs[8 * n + 2:]
        outs[-1][...] = _sum_in_device_order(refs[4 * n])
        me = _my_place()[3]
        for a in range(n):
            g = _sum_in_device_order(r_refs[a])
            if own_row[a]:
                scr[0][...] = g
                g = scr[0][pl.ds(me, 1), :]
            g_ref, d_ref, nm_ref, nv_ref = outs[4 * a:4 * a + 4]
            g_ref[...] = g
            d_ref[...], nm_ref[...], nv_ref[...] = _adamw(w_refs[a][...], g, m_refs[a][...], v_refs[a][...])

    out_shape = []
    for w in ws:
        out_shape += [_sds(w.shape, F32)] * 4
    return pl.pallas_call(
        body, name="adamw_small", in_specs=[VMEM] * (4 * n + 1), out_specs=[VMEM] * (4 * n + 1),
        out_shape=out_shape + [_sds((1, 1), F32)], scratch_shapes=[pltpu.VMEM((N_DEV, GATE_DIM // N_DEV), F32)],
    )(*recvs, *ws, *ms, *vs, losses)


BIG = ("a_w_in", "a_w_out", "b_w_q_a", "b_w_q_b", "b_w_o", "kv_w_a", "kv_w_b", "mlp_w1", "mlp_w2")
SMALL = ("norm_mix_g", "norm_mlp_g", "a_ln_v_g", "a_ln_v_b", "a_w_s", "a_b_s", "b_q_norm_g", "kv_src_norm_g",
         "kv_a_norm_g", "final_norm_g")
WEIGHTS = ("norm_mix_g", "norm_mlp_g", "a_w_in", "a_ln_v_g", "a_ln_v_b", "a_w_s", "a_b_s", "a_w_out", "b_w_q_a",
           "b_q_norm_g", "b_w_q_b", "b_w_o", "kv_src_norm_g", "kv_w_a", "kv_a_norm_g", "kv_w_b", "mlp_w1", "mlp_w2",
           "final_norm_g")


def _two_d(name, a):
    if name in ("a_w_s", "a_b_s"):
        return a.reshape(a.shape[1:])
    return a.reshape(1, -1) if a.ndim == 1 else a


def _three_d(a):
    return a if a.ndim == 3 else a.reshape((1,) + a.shape)


def kernel(x, positions, norm_mix_g, norm_mlp_g, a_w_in, a_ln_v_g, a_ln_v_b, a_w_s, a_b_s, a_w_out, b_w_q_a, b_q_norm_g, b_w_q_b, b_w_o, kv_src_norm_g, kv_w_a, kv_a_norm_g, kv_w_b, mlp_w1, mlp_w2, final_norm_g, loss_target, m_norm_mix_g, m_norm_mlp_g, m_a_w_in, m_a_ln_v_g, m_a_ln_v_b, m_a_w_s, m_a_b_s, m_a_w_out, m_b_w_q_a, m_b_q_norm_g, m_b_w_q_b, m_b_w_o, m_kv_src_norm_g, m_kv_w_a, m_kv_a_norm_g, m_kv_w_b, m_mlp_w1, m_mlp_w2, m_final_norm_g, v_norm_mix_g, v_norm_mlp_g, v_a_w_in, v_a_ln_v_g, v_a_ln_v_b, v_a_w_s, v_a_b_s, v_a_w_out, v_b_w_q_a, v_b_q_norm_g, v_b_w_q_b, v_b_w_o, v_kv_src_norm_g, v_kv_w_a, v_kv_a_norm_g, v_kv_w_b, v_mlp_w1, v_mlp_w2, v_final_norm_g):
    w = dict(norm_mix_g=norm_mix_g, norm_mlp_g=norm_mlp_g, a_w_in=a_w_in, a_ln_v_g=a_ln_v_g, a_ln_v_b=a_ln_v_b,
             a_w_s=a_w_s, a_b_s=a_b_s, a_w_out=a_w_out, b_w_q_a=b_w_q_a, b_q_norm_g=b_q_norm_g, b_w_q_b=b_w_q_b,
             b_w_o=b_w_o, kv_src_norm_g=kv_src_norm_g, kv_w_a=kv_w_a, kv_a_norm_g=kv_a_norm_g, kv_w_b=kv_w_b,
             mlp_w1=mlp_w1, mlp_w2=mlp_w2, final_norm_g=final_norm_g)
    m = dict(norm_mix_g=m_norm_mix_g, norm_mlp_g=m_norm_mlp_g, a_w_in=m_a_w_in, a_ln_v_g=m_a_ln_v_g,
             a_ln_v_b=m_a_ln_v_b, a_w_s=m_a_w_s, a_b_s=m_a_b_s, a_w_out=m_a_w_out, b_w_q_a=m_b_w_q_a,
             b_q_norm_g=m_b_q_norm_g, b_w_q_b=m_b_w_q_b, b_w_o=m_b_w_o, kv_src_norm_g=m_kv_src_norm_g,
             kv_w_a=m_kv_w_a, kv_a_norm_g=m_kv_a_norm_g, kv_w_b=m_kv_w_b, mlp_w1=m_mlp_w1, mlp_w2=m_mlp_w2,
             final_norm_g=m_final_norm_g)
    v = dict(norm_mix_g=v_norm_mix_g, norm_mlp_g=v_norm_mlp_g, a_w_in=v_a_w_in, a_ln_v_g=v_a_ln_v_g,
             a_ln_v_b=v_a_ln_v_b, a_w_s=v_a_w_s, a_b_s=v_a_b_s, a_w_out=v_a_w_out, b_w_q_a=v_b_w_q_a,
             b_q_norm_g=v_b_q_norm_g, b_w_q_b=v_b_w_q_b, b_w_o=v_b_w_o, kv_src_norm_g=v_kv_src_norm_g,
             kv_w_a=v_kv_w_a, kv_a_norm_g=v_kv_a_norm_g, kv_w_b=v_kv_w_b, mlp_w1=v_mlp_w1, mlp_w2=v_mlp_w2,
             final_norm_g=v_final_norm_g)
    t = x.shape[1]

    first = ("a_w_in", "a_w_out", "a_ln_v_g", "a_ln_v_b")
    later = ("mlp_w1_0", "mlp_w2_0", "mlp_w1_1", "mlp_w2_1", "kv_w_a", "kv_w_b", "b_w_q_a", "b_w_q_b", "b_w_o")
    blocks = {k: _three_d(w[k]) for k in BIG if not k.startswith("mlp")}
    for k in ("mlp_w1", "mlp_w2"):
        blocks[k + "_0"], blocks[k + "_1"] = w[k][0:1], w[k][1:2]
    got, casts = _gather_first([blocks[k] if k in blocks else w[k] for k in first], [blocks[k] for k in later])
    wg = dict(zip(first, got))
    wg["a_w_out"] = wg["a_w_out"].reshape(GATE_DIM, D_MODEL)
    wg["a_ln_v_g"] = wg["a_ln_v_g"].reshape(1, GATE_DIM)
    wg["a_ln_v_b"] = wg["a_ln_v_b"].reshape(1, GATE_DIM)
    shards = dict(zip(later, casts))

    sm = {k: _two_d(k, w[k]) for k in SMALL if k not in ("a_ln_v_g", "a_ln_v_b")}
    sm["a_b_st"] = sm["a_b_s"].T
    inv_freq = (ROPE_THETA ** (-jnp.arange(0, QK_ROPE, 2, dtype=F32) / QK_ROPE)).reshape(1, QK_ROPE // 2)

    losses, dx, g, small = _local_step(x[0], positions.reshape(t, 1), loss_target[0], inv_freq, wg, sm, shards)

    names = ("a_w_in", "a_w_out")
    sums = _pair_reduce("pair_reduce_a", [g[k] for k in names], after=[g["mlp_w1_0"], g["mlp_w2_0"]])
    g.update(zip(names, _by_sequencer("exchange_last", _chip_exchange_comm(sums), OTHER_CHIPS, collective_id=1)))

    out = {}
    for k in BIG:
        recvs = [[g[k + "_0"]], [g[k + "_1"]]] if k.startswith("mlp") else [[g[k]]]
        res = _adamw_sharded("adamw_" + k, recvs, _three_d(w[k]), _three_d(m[k]), _three_d(v[k]))
        out[k] = [o.reshape(w[k].shape) for o in res]
    own_row = [k in ("a_ln_v_g", "a_ln_v_b") for k in SMALL]
    res = _adamw_small([small[k] for k in SMALL], [_two_d(k, w[k]) for k in SMALL], [_two_d(k, m[k]) for k in SMALL],
                       [_two_d(k, v[k]) for k in SMALL], own_row, losses)
    for i, k in enumerate(SMALL):
        out[k] = [o.reshape(w[k].shape) for o in res[4 * i:4 * i + 4]]

    return (res[-1].reshape(()), dx.reshape(x.shape), *[out[k][0] for k in WEIGHTS], *[out[k][1] for k in WEIGHTS],
            *[out[k][2] for k in WEIGHTS], *[out[k][3] for k in WEIGHTS])
```

```python
import math

import jax
import jax.numpy as jnp
from jax import lax
from jax.experimental import pallas as pl
from jax.experimental.pallas import tpu as pltpu
from jax.experimental.pallas import tpu_sc as plsc

F32, BF16 = jnp.float32, jnp.bfloat16
MESH = pl.DeviceIdType.MESH
ANY = pl.BlockSpec(memory_space=pl.ANY)
VMEM = pl.BlockSpec(memory_space=pltpu.VMEM)

N_DEV = 8
D_MODEL = 1024
CHUNK = 64
GMLP_BLOCK = 128
GATE_DIM = 2048
A_GROUPS = 8
A_GROUP_DIM = GATE_DIM // A_GROUPS
B_HEADS = 8
QK_NOPE, QK_ROPE, V_HEAD = 128, 64, 128
Q_LORA, KV_LORA = 384, 256
ROPE_THETA = 10000.0
D_FF = 4096
FF_SLOT = D_FF // N_DEV
EPS = 1e-6
ATT_SCALE = (QK_NOPE + QK_ROPE) ** -0.5

ADAM_LR, ADAM_B1, ADAM_B2, ADAM_EPS, ADAM_WD, ADAM_STEP = 0.001, 0.9, 0.999, 1e-08, 0.01, 10

TM = 256
TM_GATE = 128
VMEM_LIMIT = 56 * 1024 * 1024
INV_SQRT2 = 1.0 / math.sqrt(2.0)
INV_SQRT_2PI = 1.0 / math.sqrt(2.0 * math.pi)
LOG2_E = 1.0 / math.log(2.0)
HEADS_PER_STEP = 2


def _dot(a, b):
    return jnp.dot(a, b, preferred_element_type=F32)


def _dot_nt(a, b):
    return lax.dot_general(a, b, (((1,), (1,)), ((), ())), preferred_element_type=F32)


def _dot_tn(a, b):
    return lax.dot_general(a, b, (((0,), (0,)), ((), ())), preferred_element_type=F32)


def _rms_fwd(x, g):
    rstd = lax.rsqrt(jnp.mean(x * x, axis=-1, keepdims=True) + EPS)
    xhat = x * rstd
    return xhat * g, xhat, rstd


def _rms_bwd(dy, xhat, rstd, g):
    dxhat = dy * g
    dx = rstd * (dxhat - xhat * jnp.mean(dxhat * xhat, axis=-1, keepdims=True))
    return dx, jnp.sum(dy * xhat, axis=0, keepdims=True)


def _ln_fwd(v, g, b):
    mu = jnp.mean(v, axis=-1, keepdims=True)
    vc = v - mu
    rstd = lax.rsqrt(jnp.mean(vc * vc, axis=-1, keepdims=True) + EPS)
    vhat = vc * rstd
    return vhat * g + b, vhat, rstd


def _gelu(x):
    return 0.5 * x * (1.0 + lax.erf(x * INV_SQRT2))


def _gelu_and_grad(x):
    cdf = 0.5 * (1.0 + lax.erf(x * INV_SQRT2))
    return x * cdf, cdf + x * jnp.exp(-0.5 * x * x) * INV_SQRT_2PI


def _rope(x, cos, sin):
    x1, x2 = x[:, :QK_ROPE // 2], x[:, QK_ROPE // 2:]
    return jnp.concatenate([x1 * cos - x2 * sin, x2 * cos + x1 * sin], axis=-1)


def _gate_mask():
    row = lax.broadcasted_iota(jnp.int32, (GMLP_BLOCK, GMLP_BLOCK), 0)
    col = lax.broadcasted_iota(jnp.int32, (GMLP_BLOCK, GMLP_BLOCK), 1)
    return (col < CHUNK) | (row >= CHUNK)


def _att_mask(q0, tq, t):
    q = q0 + lax.broadcasted_iota(jnp.int32, (tq, t), 0)
    k = lax.broadcasted_iota(jnp.int32, (tq, t), 1)
    return jnp.right_shift(k, 6) <= jnp.right_shift(q, 6)


def _res(shape, imap=None):
    zeros = (0,) * len(shape)
    return pl.BlockSpec(shape, imap or (lambda i: zeros), pipeline_mode=pl.Buffered(1))


def _const(shape):
    zeros = (0,) * len(shape)
    return pl.BlockSpec(shape, lambda i: zeros)


def _row(d, tm=TM):
    return pl.BlockSpec((tm, d), lambda i: (i, 0))


def _heads(d):
    return pl.BlockSpec((B_HEADS, TM, d), lambda i: (0, i, 0))


def _sds(shape, dt):
    return jax.ShapeDtypeStruct(shape, dt)


def _acc(ref, val):
    @pl.when(pl.program_id(0) == 0)
    def _():
        ref[...] = jnp.zeros_like(ref)
    ref[...] += val


def _my_place():
    x, y, c = lax.axis_index("x"), lax.axis_index("y"), lax.axis_index("c")
    return x, y, c, 4 * x + 2 * y + c


def _peer(x, y, c, k):
    px = 1 - x if k & 4 else x
    py = 1 - y if k & 2 else y
    pc = 1 - c if k & 1 else c
    return (px, py, pc), 4 * px + 2 * py + pc


CHIPS = (2, 4, 6)


def _splits(ref):
    return len(ref.shape) >= 3 and ref.shape[1] % 32 == 0


def _piece(ref, block, half=None):
    if half is None or not _splits(ref):
        return ref.at[pl.ds(block, 1)]
    rows = ref.shape[1] // 2
    return ref.at[pl.ds(block, 1), pl.ds(half * rows, rows)]


def _gather_copy(sems, a, k, piece, to, src=None):
    return pltpu.make_async_remote_copy(
        src_ref=piece if src is None else src, dst_ref=piece, send_sem=sems[0].at[a, k], recv_sem=sems[1].at[a, k],
        device_id=to, device_id_type=MESH)


def _gather_start(srcs, outs, sems, only=None):
    x, y, c, me = _my_place()
    for a in range(len(srcs)) if only is None else (only,):
        mine = _piece(outs[a], me)
        pltpu.make_async_copy(srcs[a], mine, sems[2].at[a]).start()
        for k, rel in enumerate((1, 4, 2)):
            _gather_copy(sems, a, k, mine, _peer(x, y, c, rel)[0], src=srcs[a]).start()


def _gather_relay(srcs, outs, sems):
    x, y, c, _ = _my_place()
    sib = _peer(x, y, c, 1)[0]
    (xn, xn_i), (yn, yn_i) = _peer(x, y, c, 4), _peer(x, y, c, 2)
    for a in range(len(srcs)):
        out = outs[a]
        _gather_copy(sems, a, 1, _piece(out, xn_i), xn).wait_recv()
        _gather_copy(sems, a, 3, _piece(out, xn_i, 0), yn).start()
        _gather_copy(sems, a, 5, _piece(out, xn_i), sib).start()
        _gather_copy(sems, a, 2, _piece(out, yn_i), yn).wait_recv()
        if _splits(out):
            _gather_copy(sems, a, 4, _piece(out, yn_i, 1), xn).start()
        _gather_copy(sems, a, 6, _piece(out, yn_i), sib).start()


def _gather_finish(srcs, outs, sems):
    x, y, c, me = _my_place()
    sib = _peer(x, y, c, 1)[0]
    xn, yn, dg_i = _peer(x, y, c, 4)[0], _peer(x, y, c, 2)[0], _peer(x, y, c, 6)[1]
    n = len(srcs)
    for a in range(n):
        out = outs[a]
        _gather_copy(sems, a, 3, _piece(out, dg_i, 0), yn).wait_recv()
        _gather_copy(sems, a, 7, _piece(out, dg_i, 0), sib).start()
        if _splits(out):
            _gather_copy(sems, a, 4, _piece(out, dg_i, 1), xn).wait_recv()
            _gather_copy(sems, a, 8, _piece(out, dg_i, 1), sib).start()
    for a in range(n):
        out = outs[a]
        whole, half = _piece(out, me), _piece(out, me, 0)
        for k in (0, 5, 6):
            _gather_copy(sems, a, k, whole, sib).wait_recv()
        for k in (7, 8) if _splits(out) else (7,):
            _gather_copy(sems, a, k, half, sib).wait_recv()
        for k in (0, 1, 2):
            _gather_copy(sems, a, k, whole, sib, src=srcs[a]).wait_send()
        for k in (5, 6):
            _gather_copy(sems, a, k, whole, sib).wait_send()
        for k in (3, 4, 7, 8) if _splits(out) else (3, 7):
            _gather_copy(sems, a, k, half, sib).wait_send()
        pltpu.make_async_copy(srcs[a], whole, sems[2].at[a]).wait()


def _relay_sems(n):
    return [pltpu.SemaphoreType.DMA((n, 9)), pltpu.SemaphoreType.DMA((n, 9)), pltpu.SemaphoreType.DMA((n,))]


def _gather_sems(n):
    return [pltpu.SemaphoreType.DMA((n, 7)), pltpu.SemaphoreType.DMA((n, 7)), pltpu.SemaphoreType.DMA((n,))]


class _Comm:
    def __init__(self, args, out_shape, scratch, start, finish, relay=None):
        self.args, self.out_shape, self.scratch, self.start, self.finish = args, out_shape, scratch, start, finish
        self.relay = relay


def _gather_comm(shards):
    return _Comm(list(shards), [_sds((N_DEV,) + s.shape[1:], s.dtype) for s in shards], _relay_sems(len(shards)),
                 _gather_start, _gather_finish, relay=_gather_relay)


def _direct_copies(ins, outs, sems, wait, from_block):
    send_sems, recv_sems, local_sems = sems
    x, y, c, me = _my_place()
    for a in range(len(ins)):
        src = ins[a].at[pl.ds(me, 1)] if from_block[a] else ins[a]
        local = pltpu.make_async_copy(src, outs[a].at[pl.ds(me, 1)], local_sems.at[a])
        local.wait() if wait else local.start()
        for k in range(1, N_DEV):
            to, to_i = _peer(x, y, c, k)
            cp = pltpu.make_async_remote_copy(
                src_ref=ins[a].at[pl.ds(to_i, 1)] if from_block[a] else ins[a], dst_ref=outs[a].at[pl.ds(me, 1)],
                send_sem=send_sems.at[a, k - 1], recv_sem=recv_sems.at[a, k - 1], device_id=to, device_id_type=MESH)
            cp.wait() if wait else cp.start()


def _exchange_comm(grads=(), parts=()):
    ins = list(grads) + list(parts)
    from_block = [True] * len(grads) + [False] * len(parts)
    out_shape = [_sds(g.shape, g.dtype) for g in grads] + [_sds((N_DEV,) + p.shape[1:], p.dtype) for p in parts]

    def start(ins_, outs_, sems_):
        _direct_copies(ins_, outs_, sems_, False, from_block)

    def finish(ins_, outs_, sems_):
        _direct_copies(ins_, outs_, sems_, True, from_block)

    return _Comm(ins, out_shape, _gather_sems(len(ins)), start, finish)


def _chip_copies(ins, outs, sems, wait, rels, own):
    send_sems, recv_sems, local_sems = sems
    x, y, c, _ = _my_place()
    for a in range(len(ins)):
        if own:
            local = pltpu.make_async_copy(ins[a].at[pl.ds(2 * x + y, 1)], outs[a].at[pl.ds(len(rels), 1)],
                                          local_sems.at[a])
            local.wait() if wait else local.start()
        for i, j in enumerate(rels):
            to = _peer(x, y, c, CHIPS[j])[0]
            cp = pltpu.make_async_remote_copy(
                src_ref=ins[a].at[pl.ds(2 * to[0] + to[1], 1)], dst_ref=outs[a].at[pl.ds(i, 1)],
                send_sem=send_sems.at[a, i], recv_sem=recv_sems.at[a, i], device_id=to, device_id_type=MESH)
            cp.wait() if wait else cp.start()


def _chip_exchange_comm(sums, rels=(0, 1, 2), own=True):
    def start(ins_, outs_, sems_):
        _chip_copies(ins_, outs_, sems_, False, rels, own)

    def finish(ins_, outs_, sems_):
        _chip_copies(ins_, outs_, sems_, True, rels, own)

    n = len(sums)
    sems = [pltpu.SemaphoreType.DMA((n, len(rels))), pltpu.SemaphoreType.DMA((n, len(rels))),
            pltpu.SemaphoreType.DMA((n,))]
    return _Comm(list(sums), [_sds((len(rels) + own,) + s.shape[1:], s.dtype) for s in sums], sems, start, finish)


def _pair_reduce(name, grads, after=()):
    n = len(grads)
    n_chips = N_DEV // 2

    def body(*refs):
        g_refs, gh_refs, refs = refs[:n], refs[n:2 * n], refs[2 * n + len(after):]
        p_refs, land = refs[:n], refs[n:2 * n]
        send_sems, recv_sems = refs[2 * n:]
        x, y, c, _ = _my_place()
        sib = _peer(x, y, c, 1)[0]
        q = pl.program_id(0)

        def to_sibling(a, j):
            return pltpu.make_async_remote_copy(
                src_ref=gh_refs[a].at[j, pl.ds(1 - c, 1)], dst_ref=land[a].at[pl.ds(j, 1)],
                send_sem=send_sems.at[a, j], recv_sem=recv_sems.at[a, j], device_id=sib, device_id_type=MESH)

        @pl.when(q == 0)
        def _():
            for j in range(n_chips):
                for a in range(n):
                    to_sibling(a, j).start()

        for a in range(n):
            to_sibling(a, q).wait_recv()
            p_refs[a][...] = (g_refs[a][0, pl.ds(c, 1)].astype(F32) + land[a][pl.ds(q, 1)].astype(F32)).astype(BF16)

        @pl.when(q == n_chips - 1)
        def _():
            for a in range(n):
                for j in range(n_chips):
                    to_sibling(a, j).wait_send()

    views = [g.reshape((n_chips, 2) + g.shape[1:]) for g in grads]
    res = pl.pallas_call(
        body, name=name, grid=(n_chips,),
        in_specs=[pl.BlockSpec((1, 2) + g.shape[1:], lambda q: (q, 0, 0, 0)) for g in grads]
        + [ANY] * (n + len(after)),
        out_specs=[pl.BlockSpec((1,) + g.shape[1:], lambda q: (q, 0, 0)) for g in grads],
        out_shape=[_sds((n_chips,) + g.shape[1:], BF16) for g in grads],
        scratch_shapes=[pltpu.VMEM((n_chips,) + g.shape[1:], BF16) for g in grads]
        + [pltpu.SemaphoreType.DMA((n, n_chips)), pltpu.SemaphoreType.DMA((n, n_chips))],
        compiler_params=pltpu.CompilerParams(dimension_semantics=("arbitrary",), vmem_limit_bytes=VMEM_LIMIT),
    )(*views, *views, *after)
    return list(res)


def _pair_exchange_comm(grads):
    n, n_chips = len(grads), N_DEV // 2

    def copies(ins, outs, sems, wait):
        x, y, c, _ = _my_place()
        for j in range(n_chips):
            for a in range(n):
                cp = pltpu.make_async_remote_copy(
                    src_ref=ins[a].at[j, pl.ds(1 - c, 1)], dst_ref=outs[a].at[pl.ds(j, 1)], send_sem=sems[0].at[a, j],
                    recv_sem=sems[1].at[a, j], device_id=_peer(x, y, c, 1)[0], device_id_type=MESH)
                cp.wait() if wait else cp.start()

    views = [g.reshape((n_chips, 2) + g.shape[1:]) for g in grads]
    sems = [pltpu.SemaphoreType.DMA((n, n_chips)), pltpu.SemaphoreType.DMA((n, n_chips))]
    return _Comm(views, [_sds((n_chips,) + g.shape[1:], g.dtype) for g in grads], sems,
                 lambda i, o, s: copies(i, o, s, False), lambda i, o, s: copies(i, o, s, True))


def _pair_add(name, grads, landed, after=()):
    n, n_chips = len(grads), N_DEV // 2

    def body(core_ref, *refs):
        g_refs, l_refs, p_refs = refs[:n], refs[n:2 * n], refs[2 * n + len(after):]
        for a in range(n):
            p_refs[a][...] = (g_refs[a][...].astype(F32) + l_refs[a][...].astype(F32)).astype(BF16)

    views = [g.reshape((n_chips, 2) + g.shape[1:]) for g in grads]
    blocks = [pl.BlockSpec((1,) + g.shape[1:], lambda q, core: (q, 0, 0)) for g in grads]
    mine = [pl.BlockSpec((1, None) + g.shape[1:], lambda q, core: (q, core[0], 0, 0)) for g in grads]
    return list(pl.pallas_call(
        body, name=name, out_shape=[_sds((n_chips,) + g.shape[1:], BF16) for g in grads],
        grid_spec=pltpu.PrefetchScalarGridSpec(num_scalar_prefetch=1, grid=(n_chips,),
                                               in_specs=mine + blocks + [ANY] * len(after), out_specs=blocks),
        compiler_params=pltpu.CompilerParams(dimension_semantics=("arbitrary",), vmem_limit_bytes=VMEM_LIMIT),
    )(lax.axis_index("c").reshape(1), *views, *landed, *after))


def _call(name, body, grid, in_specs, out_specs, out_shape, args, scratch=(), after=()):
    ni, na = len(in_specs), len(after)

    def ordered(*refs):
        body(*refs[:ni], *refs[ni + na:])

    return list(pl.pallas_call(
        ordered if after else body, name=name, grid=grid, in_specs=list(in_specs) + [ANY] * na,
        out_specs=list(out_specs), out_shape=list(out_shape), scratch_shapes=list(scratch),
        compiler_params=pltpu.CompilerParams(dimension_semantics=("arbitrary",) * len(grid),
                                             vmem_limit_bytes=VMEM_LIMIT))(*args, *after))


SIBLING_AND_NEIGHBOURS, OTHER_CHIPS, EVERYONE = (1, 4, 2), CHIPS, tuple(range(1, N_DEV))


def _by_sequencer(name, comm, peers, collective_id):
    src = [jax.new_ref(a, memory_space=pltpu.MemorySpace.HBM) for a in comm.args]
    dst = [jax.empty_ref(s, memory_space=pltpu.MemorySpace.HBM) for s in comm.out_shape]

    @pl.kernel(mesh=plsc.ScalarSubcoreMesh(axis_name="sequencer", num_cores=1), name=name,
               scratch_types=tuple(comm.scratch), compiler_params=pltpu.CompilerParams(collective_id=collective_id))
    def launch(*sems):
        x, y, c, _ = _my_place()
        barrier = pltpu.get_barrier_semaphore()
        for k in peers:
            pl.semaphore_signal(barrier, inc=1, device_id=_peer(x, y, c, k)[0], device_id_type=MESH)
        pl.semaphore_wait(barrier, len(peers))
        comm.start(src, dst, sems)
        if comm.relay is not None:
            comm.relay(src, dst, sems)
        comm.finish(src, dst, sems)

    launch()
    return [d[...] for d in dst]


def _gather_first(first, later):
    nf, nl = len(first), len(later)
    dts = [BF16] * (nf - 2) + [F32, F32]

    def body(*refs):
        ins, refs = refs[:nf + nl], refs[nf + nl:]
        outs, refs = refs[:nf], refs[nf:]
        casts, refs = refs[:nl], refs[nl:]
        stage, sems = refs[:nf], refs[nf:]
        for a in range(nf):
            stage[a][...] = ins[a][...].astype(dts[a])
            _gather_start(stage, outs, sems, only=a)
        for a in range(nl):
            casts[a][...] = ins[nf + a][...].astype(BF16)
        _gather_relay(stage, outs, sems)
        _gather_finish(stage, outs, sems)

    res = pl.pallas_call(
        body, name="gather_first",
        in_specs=[VMEM] * (nf + nl), out_specs=[ANY] * nf + [VMEM] * nl,
        out_shape=[_sds((N_DEV,) + s.shape[1:], dt) for s, dt in zip(first, dts)]
        + [_sds(s.shape, BF16) for s in later],
        scratch_shapes=[pltpu.VMEM(s.shape, dt) for s, dt in zip(first, dts)] + _relay_sems(nf),
        compiler_params=pltpu.CompilerParams(vmem_limit_bytes=VMEM_LIMIT),
    )(*first, *later)
    return list(res[:nf]), list(res[nf:])


def _a_mix_fwd(x, g, w_in, ln_g, ln_b, w_s, b_st, w_out):
    t = x.shape[0]
    nblk = TM // GMLP_BLOCK

    def body(x_ref, g_ref, win_ref, lng_ref, lnb_ref, ws_ref, bst_ref, wout_ref, h_ref, z_ref, gated_scr):
        xv = x_ref[...]
        hb = _rms_fwd(xv, g_ref[...])[0].astype(BF16)
        for d in range(N_DEV):
            z_ref[:, d * FF_SLOT:(d + 1) * FF_SLOT] = _dot(hb, win_ref[d])
        u = _gelu(z_ref[:, :GATE_DIM])
        vb = _ln_fwd(_gelu(z_ref[:, GATE_DIM:]), lng_ref[...], lnb_ref[...])[0].astype(BF16)
        mask = _gate_mask()
        for gi in range(A_GROUPS):
            wm = jnp.where(mask, ws_ref[gi], 0.0).astype(BF16)
            bias = bst_ref[:, gi:gi + 1]
            cs = slice(gi * A_GROUP_DIM, (gi + 1) * A_GROUP_DIM)
            for n in range(nblk):
                rs = slice(n * GMLP_BLOCK, (n + 1) * GMLP_BLOCK)
                sv = _dot(wm, vb[rs, cs]) + bias
                gated_scr[rs, cs] = (u[rs, cs] * sv).astype(BF16)
        h_ref[...] = xv + _dot(gated_scr[...], wout_ref[...])

    return _call(
        "a_mix_fwd", body, (t // TM,),
        [_row(D_MODEL), _res((1, D_MODEL)), _res((N_DEV, D_MODEL, FF_SLOT)), _res((1, GATE_DIM)),
         _res((1, GATE_DIM)), _res((A_GROUPS, GMLP_BLOCK, GMLP_BLOCK)), _res((GMLP_BLOCK, A_GROUPS)),
         _res((GATE_DIM, D_MODEL))],
        [_row(D_MODEL), _row(2 * GATE_DIM), _row(GATE_DIM)],
        [_sds((t, D_MODEL), F32), _sds((t, 2 * GATE_DIM), F32), _sds((t, GATE_DIM), BF16)],
        (x, g, w_in, ln_g, ln_b, w_s, b_st, w_out))


MLP_W_SPECS = (_res((N_DEV, D_MODEL, FF_SLOT)), _res((N_DEV, FF_SLOT, D_MODEL)))


def _mlp_fwd(h, g, w1, w2):
    t = h.shape[0]

    def body(h_ref, g_ref, w1_ref, w2_ref, o_ref, a_ref):
        hv = h_ref[...]
        hb = _rms_fwd(hv, g_ref[...])[0].astype(BF16)
        o_ref[...] = hv
        for d in range(N_DEV):
            a = _dot(hb, w1_ref[d])
            a_ref[:, d * FF_SLOT:(d + 1) * FF_SLOT] = a
            r = jnp.maximum(a, 0.0)
            o_ref[...] += _dot((r * r).astype(BF16), w2_ref[d])

    return _call(
        "mlp_fwd", body, (t // TM,), [_row(D_MODEL), _res((1, D_MODEL)), *MLP_W_SPECS],
        [_row(D_MODEL), _row(D_FF)], [_sds((t, D_MODEL), F32), _sds((t, D_FF), F32)], (h, g, w1, w2))


def _mlp_fwd_loss(h, g, w1, w2, final_g, target):
    t = h.shape[0]

    def body(h_ref, g_ref, w1_ref, w2_ref, fg_ref, t_ref, a_ref, loss_ref, dh_ref, dg_ref):
        hv = h_ref[...]
        hb = _rms_fwd(hv, g_ref[...])[0].astype(BF16)
        out = hv
        for d in range(N_DEV):
            a = _dot(hb, w1_ref[d])
            a_ref[:, d * FF_SLOT:(d + 1) * FF_SLOT] = a
            r = jnp.maximum(a, 0.0)
            out = out + _dot((r * r).astype(BF16), w2_ref[d])
        y, xhat, rstd = _rms_fwd(out, fg_ref[...])
        err = y - t_ref[...]
        part = 0.5 * jnp.sum(jnp.mean(err * err, axis=-1, keepdims=True), axis=0, keepdims=True)
        dx, dg = _rms_bwd(err * (1.0 / D_MODEL), xhat, rstd, fg_ref[...])
        dh_ref[...] = dx
        _acc(dg_ref, dg)
        _acc(loss_ref, part)

    return _call(
        "mlp_fwd_loss", body, (t // TM,),
        [_row(D_MODEL), _res((1, D_MODEL)), *MLP_W_SPECS, _res((1, D_MODEL)), _row(D_MODEL)],
        [_row(D_FF), _const((1, 1)), _row(D_MODEL), _const((1, D_MODEL))],
        [_sds((t, D_FF), F32), _sds((1, 1), F32), _sds((t, D_MODEL), F32), _sds((1, D_MODEL), F32)],
        (h, g, w1, w2, final_g, target))


KVQ_W_SPECS = (_res((1, D_MODEL)), _res((D_MODEL, KV_LORA + QK_ROPE)), _res((1, KV_LORA)),
               _res((B_HEADS, KV_LORA, QK_NOPE + V_HEAD)), _res((1, D_MODEL)), _res((D_MODEL, Q_LORA)),
               _res((1, Q_LORA)), _res((B_HEADS, Q_LORA, QK_NOPE + QK_ROPE)))


def _kvq_fwd(h, pos, inv_freq, kvq_w):
    t = h.shape[0]
    half = QK_ROPE // 2

    def body(h_ref, pos_ref, invf_ref, srcg_ref, wkva_ref, kvag_ref, wkvb_ref, mixg_ref, wqa_ref, qg_ref, wqb_ref,
             ckv_ref, k_ref, v_ref, cqpre_ref, q_ref, cos_ref, sin_ref):
        hv = h_ref[...]
        xhat = hv * lax.rsqrt(jnp.mean(hv * hv, axis=-1, keepdims=True) + EPS)
        ang = pos_ref[...].astype(F32) * invf_ref[...]
        cos, sin = jnp.cos(ang), jnp.sin(ang)
        cos_ref[...] = cos
        sin_ref[...] = sin
        ckv = _dot((xhat * srcg_ref[...]).astype(BF16), wkva_ref[...])
        ckv_ref[...] = ckv
        cb = _rms_fwd(ckv[:, :KV_LORA], kvag_ref[...])[0].astype(BF16)
        kpe = _rope(ckv[:, KV_LORA:], cos, sin).astype(BF16)
        for hd in range(B_HEADS):
            kv = _dot(cb, wkvb_ref[hd])
            k_ref[hd, :, 0:QK_NOPE] = kv[:, :QK_NOPE].astype(BF16)
            k_ref[hd, :, QK_NOPE:] = kpe
            v_ref[hd] = kv[:, QK_NOPE:].astype(BF16)
        cqpre = _dot((xhat * mixg_ref[...]).astype(BF16), wqa_ref[...])
        cqpre_ref[...] = cqpre
        cqb = _rms_fwd(cqpre, qg_ref[...])[0].astype(BF16)
        for hd in range(B_HEADS):
            q = _dot(cqb, wqb_ref[hd])
            q_ref[hd, :, 0:QK_NOPE] = q[:, :QK_NOPE].astype(BF16)
            q_ref[hd, :, QK_NOPE:] = _rope(q[:, QK_NOPE:], cos, sin).astype(BF16)

    return _call(
        "kvq_fwd", body, (t // TM,), [_row(D_MODEL), _row(1), _res((1, half)), *KVQ_W_SPECS],
        [_row(KV_LORA + QK_ROPE), _heads(QK_NOPE + QK_ROPE), _heads(V_HEAD), _row(Q_LORA),
         _heads(QK_NOPE + QK_ROPE), _row(half), _row(half)],
        [_sds((t, KV_LORA + QK_ROPE), F32), _sds((B_HEADS, t, QK_NOPE + QK_ROPE), BF16),
         _sds((B_HEADS, t, V_HEAD), BF16), _sds((t, Q_LORA), F32), _sds((B_HEADS, t, QK_NOPE + QK_ROPE), BF16),
         _sds((t, half), F32), _sds((t, half), F32)],
        (h, pos, inv_freq, *kvq_w))


def _softmax_rows(q, k_ref, k):
    past, upto = k * TM, (k + 1) * TM
    s = _dot_nt(q, k_ref[0:upto, :])
    own = jnp.where(_att_mask(0, TM, TM), s[:, past:], jnp.finfo(F32).min)
    s = own if k == 0 else jnp.concatenate([s[:, :past], own], axis=1)
    e = jnp.exp2((s - jnp.max(s, axis=-1, keepdims=True)) * (ATT_SCALE * LOG2_E))
    return e * (1.0 / jnp.sum(e, axis=-1, keepdims=True))


def _for_my_tile(i, nq, fn):
    for k in range(nq):
        @pl.when(i == k)
        def _(k=k):
            fn(k)


def _attn_fwd(h, q, k, v, w_o):
    t = h.shape[0]
    nq, hps = t // TM, HEADS_PER_STEP

    def body(h_ref, q_ref, k_ref, v_ref, wo_ref, o_ref, att_ref):
        i, pair = pl.program_id(0), pl.program_id(1)

        @pl.when(pair == 0)
        def _():
            o_ref[...] = h_ref[...]

        def tile(kt):
            proj = None
            for j in range(hps):
                hd = pair * hps + j
                p = _softmax_rows(q_ref[j], k_ref.at[hd], kt)
                ob = _dot(p.astype(BF16), v_ref[hd, 0:(kt + 1) * TM, :]).astype(BF16)
                att_ref[j] = ob
                proj = _dot(ob, wo_ref[hd]) if proj is None else proj + _dot(ob, wo_ref[hd])
            o_ref[...] += proj

        _for_my_tile(i, nq, tile)

    def per_head(d):
        return pl.BlockSpec((hps, TM, d), lambda i, pair: (pair, i, 0))

    def resident(shape):
        zeros = (0,) * len(shape)
        return pl.BlockSpec(shape, lambda i, pair: zeros, pipeline_mode=pl.Buffered(1))

    tile_spec = pl.BlockSpec((TM, D_MODEL), lambda i, pair: (i, 0))
    return _call(
        "attn_fwd", body, (nq, B_HEADS // hps),
        [tile_spec, per_head(QK_NOPE + QK_ROPE), resident((B_HEADS, t, QK_NOPE + QK_ROPE)),
         resident((B_HEADS, t, V_HEAD)), resident((B_HEADS, V_HEAD, D_MODEL))],
        [tile_spec, per_head(V_HEAD)], [_sds((t, D_MODEL), F32), _sds((B_HEADS, t, V_HEAD), BF16)],
        (h, q, k, v, w_o))


def _mlp_bwd(h, a, dho, g, w1, w2, layer, after=()):
    t = h.shape[0]

    def body(h_ref, a_ref, dho_ref, g_ref, w1_ref, w2_ref, dhi_ref, dg_ref, hn_ref, f_ref, da_ref, dhib_ref):
        gv = g_ref[...]
        y, xhat, rstd = _rms_fwd(h_ref[...], gv)
        hn_ref[...] = y.astype(BF16)
        dho_v = dho_ref[...]
        dhob = dho_v.astype(BF16)
        dhn = jnp.zeros((TM, D_MODEL), F32)
        for d in range(N_DEV):
            cs = slice(d * FF_SLOT, (d + 1) * FF_SLOT)
            r = jnp.maximum(a_ref[:, cs], 0.0)
            f_ref[:, cs] = (r * r).astype(BF16)
            da = (_dot_nt(dhob, w2_ref[d]) * (2.0 * r)).astype(BF16)
            da_ref[:, cs] = da
            dhn = dhn + _dot_nt(da, w1_ref[d])
        dx, dg = _rms_bwd(dhn, xhat, rstd, gv)
        dhi = dho_v + dx
        dhi_ref[...] = dhi
        dhib_ref[...] = dhi.astype(BF16)
        _acc(dg_ref, dg)

    return _call(
        f"mlp_bwd_{layer}", body, (t // TM,),
        [_row(D_MODEL), _row(D_FF), _row(D_MODEL), _res((1, D_MODEL)), *MLP_W_SPECS],
        [_row(D_MODEL), _const((1, D_MODEL)), _row(D_MODEL), _row(D_FF), _row(D_FF), _row(D_MODEL)],
        [_sds((t, D_MODEL), F32), _sds((1, D_MODEL), F32), _sds((t, D_MODEL), BF16), _sds((t, D_FF), BF16),
         _sds((t, D_FF), BF16), _sds((t, D_MODEL), BF16)],
        (h, a, dho, g, w1, w2), after=after)


def _attn_bwd(dh, q, k, v, w_o, cos, sin, after=()):
    t = dh.shape[0]
    half, hps = QK_ROPE // 2, HEADS_PER_STEP

    def body(dh_ref, q_ref, k_ref, v_ref, wo_ref, cos_ref, sin_ref, dq_ref, dk_ref, dv_ref):
        i = pl.program_id(1)

        @pl.when(i == 0)
        def _():
            dk_ref[...] = jnp.zeros_like(dk_ref)
            dv_ref[...] = jnp.zeros_like(dv_ref)

        def tile(kt):
            keys = slice(0, (kt + 1) * TM)
            for j in range(hps):
                qj = q_ref[j]
                do = _dot_nt(dh_ref[kt * TM:(kt + 1) * TM, :], wo_ref[j]).astype(BF16)
                p = _softmax_rows(qj, k_ref.at[j], kt)
                dp = _dot_nt(do, v_ref[j, keys, :])
                ds = (p * (dp - jnp.sum(p * dp, axis=-1, keepdims=True)) * ATT_SCALE).astype(BF16)
                dq = _dot(ds, k_ref[j, keys, :])
                dq_ref[j, :, 0:QK_NOPE] = dq[:, :QK_NOPE].astype(BF16)
                dq_ref[j, :, QK_NOPE:] = _rope(dq[:, QK_NOPE:], cos_ref[...], -sin_ref[...]).astype(BF16)
                dk_ref[j, keys, :] += _dot_tn(ds, qj)
                dv_ref[j, keys, :] += _dot_tn(p.astype(BF16), do)

        _for_my_tile(i, t // TM, tile)

    def per_pair(rows, d, tiled):
        return pl.BlockSpec((hps, rows, d), (lambda pair, i: (pair, i, 0)) if tiled else (lambda pair, i: (pair, 0, 0)))

    def tile(d):
        return pl.BlockSpec((TM, d), lambda pair, i: (i, 0))

    return _call(
        "attn_bwd", body, (B_HEADS // hps, t // TM),
        [pl.BlockSpec((t, D_MODEL), lambda pair, i: (0, 0), pipeline_mode=pl.Buffered(1)),
         per_pair(TM, QK_NOPE + QK_ROPE, True), per_pair(t, QK_NOPE + QK_ROPE, False), per_pair(t, V_HEAD, False),
         per_pair(V_HEAD, D_MODEL, False), tile(half), tile(half)],
        [per_pair(TM, QK_NOPE + QK_ROPE, True), per_pair(t, QK_NOPE + QK_ROPE, False), per_pair(t, V_HEAD, False)],
        [_sds((B_HEADS, t, QK_NOPE + QK_ROPE), BF16), _sds((B_HEADS, t, QK_NOPE + QK_ROPE), F32),
         _sds((B_HEADS, t, V_HEAD), F32)],
        (dh, q, k, v, w_o, cos, sin), after=after)


def _kvq_bwd(h, dh, ckv, cqpre, dq, dk, dv, cos, sin, kvq_w, after=()):
    t = h.shape[0]
    half, last = QK_ROPE // 2, t // TM - 1
    grad_shapes = [(D_MODEL, Q_LORA), (B_HEADS, Q_LORA, QK_NOPE + QK_ROPE), (D_MODEL, KV_LORA + QK_ROPE),
                   (B_HEADS, KV_LORA, QK_NOPE + V_HEAD)]

    def body(h_ref, dh_ref, ckv_ref, cqpre_ref, dq_ref, dk_ref, dv_ref, cos_ref, sin_ref,
             srcg_ref, wkva_ref, kvag_ref, wkvb_ref, mixg_ref, wqa_ref, qg_ref, wqb_ref,
             dhi_ref, dmixg_ref, dsrcg_ref, dqg_ref, dkvag_ref, gqa_ref, gqb_ref, gkva_ref, gkvb_ref,
             aqa, aqb, akva, akvb):
        @pl.when(pl.program_id(0) == 0)
        def _():
            for acc in (aqa, aqb, akva, akvb):
                acc[...] = jnp.zeros_like(acc)

        hv = h_ref[...]
        rstd = lax.rsqrt(jnp.mean(hv * hv, axis=-1, keepdims=True) + EPS)
        xhat = hv * rstd
        mixg, srcg, qg, kvag = mixg_ref[...], srcg_ref[...], qg_ref[...], kvag_ref[...]
        cq, cqhat, crstd = _rms_fwd(cqpre_ref[...], qg)
        cqb = cq.astype(BF16)
        dcq = jnp.zeros((TM, Q_LORA), F32)
        for hd in range(B_HEADS):
            dcq = dcq + _dot_nt(dq_ref[hd], wqb_ref[hd])
            aqb[hd] += _dot_tn(cqb, dq_ref[hd])
        dcqpre, dqg = _rms_bwd(dcq, cqhat, crstd, qg)
        dcqpre_b = dcqpre.astype(BF16)
        aqa[...] += _dot_tn((xhat * mixg).astype(BF16), dcqpre_b)
        dxq, dmixg = _rms_bwd(_dot_nt(dcqpre_b, wqa_ref[...]), xhat, rstd, mixg)
        ckv = ckv_ref[...]
        c, chat, krstd = _rms_fwd(ckv[:, :KV_LORA], kvag)
        cb = c.astype(BF16)
        dc = jnp.zeros((TM, KV_LORA), F32)
        dkpe = jnp.zeros((TM, QK_ROPE), F32)
        for hd in range(B_HEADS):
            dkv = jnp.concatenate([dk_ref[hd, :, 0:QK_NOPE], dv_ref[hd]], axis=-1).astype(BF16)
            akvb[hd] += _dot_tn(cb, dkv)
            dc = dc + _dot_nt(dkv, wkvb_ref[hd])
            dkpe = dkpe + dk_ref[hd, :, QK_NOPE:]
        dlat, dkvag = _rms_bwd(dc, chat, krstd, kvag)
        dpe = _rope(dkpe, cos_ref[...], -sin_ref[...])
        dckv_b = jnp.concatenate([dlat, dpe], axis=-1).astype(BF16)
        akva[...] += _dot_tn((xhat * srcg).astype(BF16), dckv_b)
        dxk, dsrcg = _rms_bwd(_dot_nt(dckv_b, wkva_ref[...]), xhat, rstd, srcg)
        dhi_ref[...] = dh_ref[...] + dxq + dxk
        _acc(dmixg_ref, dmixg)
        _acc(dsrcg_ref, dsrcg)
        _acc(dqg_ref, dqg)
        _acc(dkvag_ref, dkvag)

        @pl.when(pl.program_id(0) == last)
        def _():
            for out, acc in ((gqa_ref, aqa), (gqb_ref, aqb), (gkva_ref, akva), (gkvb_ref, akvb)):
                out[...] = acc[...].astype(BF16)

    return _call(
        "kvq_bwd", body, (t // TM,),
        [_row(D_MODEL), _row(D_MODEL), _row(KV_LORA + QK_ROPE), _row(Q_LORA), _heads(QK_NOPE + QK_ROPE),
         _heads(QK_NOPE + QK_ROPE), _heads(V_HEAD), _row(half), _row(half), *KVQ_W_SPECS],
        [_row(D_MODEL), _const((1, D_MODEL)), _const((1, D_MODEL)), _const((1, Q_LORA)), _const((1, KV_LORA))]
        + [_const(s) for s in grad_shapes],
        [_sds((t, D_MODEL), F32), _sds((1, D_MODEL), F32), _sds((1, D_MODEL), F32), _sds((1, Q_LORA), F32),
         _sds((1, KV_LORA), F32)] + [_sds(s, BF16) for s in grad_shapes],
        (h, dh, ckv, cqpre, dq, dk, dv, cos, sin, *kvq_w), scratch=[pltpu.VMEM(s, F32) for s in grad_shapes],
        after=after)


def _a_mix_bwd(x, z, dh, g, w_in, ln_g, ln_b, w_s, b_st, w_out, after=()):
    t = x.shape[0]
    tm = TM_GATE
    nblk = tm // GMLP_BLOCK

    def body(x_ref, z_ref, dh_ref, g_ref, win_ref, lng_ref, lnb_ref, ws_ref, bst_ref, wout_ref,
             dx_ref, hn_ref, dz_ref, dg_ref, dlng_ref, dlnb_ref, dws_ref, dbs_ref, dvn_scr, gelu_grad_v):
        @pl.when(pl.program_id(0) == 0)
        def _():
            dws_ref[...] = jnp.zeros_like(dws_ref)
            dbs_ref[...] = jnp.zeros_like(dbs_ref)

        gv, lng = g_ref[...], lng_ref[...]
        y, xhat, rstd = _rms_fwd(x_ref[...], gv)
        hn_ref[...] = y.astype(BF16)
        dhv = dh_ref[...]
        dgated = _dot_nt(dhv.astype(BF16), wout_ref[...])
        u, gelu_grad_u = _gelu_and_grad(z_ref[:, :GATE_DIM])
        v, gelu_grad_v[...] = _gelu_and_grad(z_ref[:, GATE_DIM:])
        vn, vhat, lrstd = _ln_fwd(v, lng, lnb_ref[...])
        vb = vn.astype(BF16)
        mask = _gate_mask()
        for gi in range(A_GROUPS):
            wm = jnp.where(mask, ws_ref[gi], 0.0).astype(BF16)
            bias = bst_ref[:, gi:gi + 1]
            cs = slice(gi * A_GROUP_DIM, (gi + 1) * A_GROUP_DIM)
            dws = jnp.zeros((GMLP_BLOCK, GMLP_BLOCK), F32)
            dbs = jnp.zeros((GMLP_BLOCK, 1), F32)
            for n in range(nblk):
                rs = slice(n * GMLP_BLOCK, (n + 1) * GMLP_BLOCK)
                sv = _dot(wm, vb[rs, cs]) + bias
                dz_ref[rs, cs] = (dgated[rs, cs] * sv * gelu_grad_u[rs, cs]).astype(BF16)
                dsv = dgated[rs, cs] * u[rs, cs]
                dsvb = dsv.astype(BF16)
                dws = dws + _dot_nt(dsvb, vb[rs, cs])
                dbs = dbs + jnp.sum(dsv, axis=-1, keepdims=True)
                dvn_scr[rs, cs] = _dot_tn(wm, dsvb)
            dws_ref[gi] += jnp.where(mask, dws, 0.0)
            dbs_ref[gi] += dbs
        dvn = dvn_scr[...]
        dvhat = dvn * lng
        dv = lrstd * (dvhat - jnp.mean(dvhat, axis=-1, keepdims=True)
                      - vhat * jnp.mean(dvhat * vhat, axis=-1, keepdims=True))
        dz_ref[:, GATE_DIM:] = (dv * gelu_grad_v[...]).astype(BF16)
        dhn = jnp.zeros((tm, D_MODEL), F32)
        for d in range(N_DEV):
            dhn = dhn + _dot_nt(dz_ref[:, d * FF_SLOT:(d + 1) * FF_SLOT], win_ref[d])
        dx, dg = _rms_bwd(dhn, xhat, rstd, gv)
        dx_ref[...] = dhv + dx
        _acc(dg_ref, dg)
        _acc(dlng_ref, jnp.sum(dvn * vhat, axis=0, keepdims=True))
        _acc(dlnb_ref, jnp.sum(dvn, axis=0, keepdims=True))

    return _call(
        "a_mix_bwd", body, (t // tm,),
        [_row(D_MODEL, tm), _row(2 * GATE_DIM, tm), _row(D_MODEL, tm), _res((1, D_MODEL)),
         _res((N_DEV, D_MODEL, FF_SLOT)), _res((1, GATE_DIM)), _res((1, GATE_DIM)),
         _res((A_GROUPS, GMLP_BLOCK, GMLP_BLOCK)), _res((GMLP_BLOCK, A_GROUPS)), _res((GATE_DIM, D_MODEL))],
        [_row(D_MODEL, tm), _row(D_MODEL, tm), _row(2 * GATE_DIM, tm),
         _const((1, D_MODEL)), _const((1, GATE_DIM)), _const((1, GATE_DIM)),
         _const((A_GROUPS, GMLP_BLOCK, GMLP_BLOCK)), _const((A_GROUPS, GMLP_BLOCK, 1))],
        [_sds((t, D_MODEL), F32), _sds((t, D_MODEL), BF16),
         _sds((t, 2 * GATE_DIM), BF16), _sds((1, D_MODEL), F32), _sds((1, GATE_DIM), F32),
         _sds((1, GATE_DIM), F32), _sds((A_GROUPS, GMLP_BLOCK, GMLP_BLOCK), F32),
         _sds((A_GROUPS, GMLP_BLOCK, 1), F32)],
        (x, z, dh, g, w_in, ln_g, ln_b, w_s, b_st, w_out),
        scratch=[pltpu.VMEM((tm, GATE_DIM), F32), pltpu.VMEM((tm, GATE_DIM), F32)], after=after)


def _wgrad(name, a, b, a_spec, b_spec, m, n, after=()):
    def body(a_ref, b_ref, o_ref):
        o_ref[0] = _dot_tn(a_ref[...].astype(BF16), b_ref[...].astype(BF16)).astype(BF16)

    return _call(name, body, (N_DEV,), [a_spec, b_spec], [pl.BlockSpec((1, m, n), lambda d: (d, 0, 0))],
                 [_sds((N_DEV, m, n), BF16)], (a, b), after=after)[0]


def _full(t, d):
    return pl.BlockSpec((t, d), lambda i: (0, 0), pipeline_mode=pl.Buffered(1))


def _cols(t, d):
    return pl.BlockSpec((t, d), lambda i: (0, i))


def _head(t, d):
    return pl.BlockSpec((None, t, d), lambda i: (i, 0, 0))


def _local_step(x, pos, target, inv_freq, wg, sm, shards=None):
    t = x.shape[0]
    wg = dict(wg)
    dist = shards is not None
    mix_g = [sm["norm_mix_g"][l:l + 1] for l in range(2)]
    mlp_g = [sm["norm_mlp_g"][l:l + 1] for l in range(2)]

    ids = iter(range(2, 2 + 9))

    def gather(names):
        if dist:
            got = _by_sequencer("gather_" + names[0], _gather_comm([shards[k] for k in names]),
                                SIBLING_AND_NEIGHBOURS, next(ids))
            wg.update(zip(names, got))

    def send(name, names):
        if dist:
            comm = _exchange_comm(grads=[g[k] for k in names])
            g.update(zip(names, _by_sequencer("exchange_" + name, comm, EVERYONE, next(ids))))

    def send_sums(name, names, meanwhile):
        if not dist:
            meanwhile()
            return ()
        grads = [g[k] for k in names]
        landed = _by_sequencer("pair_exchange_" + name, _pair_exchange_comm(grads), (1,), next(ids))
        sums = _pair_add("pair_add_" + name, grads, landed, after=meanwhile())
        g.update(zip(names, _by_sequencer("exchange_" + name, _chip_exchange_comm(sums), OTHER_CHIPS, next(ids))))
        return sums

    def a_args():
        return (wg["a_w_in"], wg["a_ln_v_g"], wg["a_ln_v_b"], sm["a_w_s"], sm["a_b_st"], wg["a_w_out"])

    def kvq_w():
        return (sm["kv_src_norm_g"], wg["kv_w_a"], sm["kv_a_norm_g"], wg["kv_w_b"], mix_g[1], wg["b_w_q_a"],
                sm["b_q_norm_g"], wg["b_w_q_b"])

    gather(("mlp_w1_0", "mlp_w2_0"))
    h1, z, gated = _a_mix_fwd(x, mix_g[0], *a_args())
    gather(("kv_w_a", "kv_w_b", "b_w_q_a", "b_w_q_b", "b_w_o"))
    h2, a0 = _mlp_fwd(h1, mlp_g[0], wg["mlp_w1_0"], wg["mlp_w2_0"])
    if dist:
        wg["b_w_q_a"] = wg["b_w_q_a"].reshape(D_MODEL, Q_LORA)
        wg["kv_w_a"] = wg["kv_w_a"].reshape(D_MODEL, KV_LORA + QK_ROPE)
    gather(("mlp_w1_1", "mlp_w2_1"))
    ckv, k, v, cqpre, q, cos, sin = _kvq_fwd(h2, pos, inv_freq, kvq_w())
    h3, att = _attn_fwd(h2, q, k, v, wg["b_w_o"])
    a1, loss, dh4, d_final_g = _mlp_fwd_loss(h3, mlp_g[1], wg["mlp_w1_1"], wg["mlp_w2_1"], sm["final_norm_g"], target)

    g = {}
    dh3, d_mlp_g1, hn, f, da, dh3_b = _mlp_bwd(h3, a1, dh4, mlp_g[1], wg["mlp_w1_1"], wg["mlp_w2_1"], 1)
    dq, dk, dv = _attn_bwd(dh3_b, q, k, v, wg["b_w_o"], cos, sin)
    g["mlp_w1_1"] = _wgrad("wgrad_w1_1", hn, da, _full(t, D_MODEL), _cols(t, FF_SLOT), D_MODEL, FF_SLOT, after=[dq])
    g["mlp_w2_1"] = _wgrad("wgrad_w2_1", f, dh4, _cols(t, FF_SLOT), _full(t, D_MODEL), FF_SLOT, D_MODEL)

    def wgrad_w_o():
        g["b_w_o"] = _wgrad("wgrad_w_o", att, dh3_b, _head(t, V_HEAD), _full(t, D_MODEL), V_HEAD, D_MODEL)
        return [g["b_w_o"]]

    sums = send_sums("mlp_1", ("mlp_w1_1", "mlp_w2_1"), wgrad_w_o)
    dh2, d_mix_g1, d_src_g, d_q_g, d_kv_a_g, g_q_a, g["b_w_q_b"], g_kv_a, g["kv_w_b"] = _kvq_bwd(
        h2, dh3, ckv, cqpre, dq, dk, dv, cos, sin, kvq_w(), after=sums)
    g["b_w_q_a"] = g_q_a.reshape(N_DEV, D_MODEL // N_DEV, Q_LORA)
    g["kv_w_a"] = g_kv_a.reshape(N_DEV, D_MODEL // N_DEV, KV_LORA + QK_ROPE)
    qkv = ("b_w_q_a", "b_w_q_b", "kv_w_a", "kv_w_b")
    landed = [g[k] for k in qkv]
    send("qkv", qkv)
    dh1, d_mlp_g0, hn, f, da, dh1_b = _mlp_bwd(h1, a0, dh2, mlp_g[0], wg["mlp_w1_0"], wg["mlp_w2_0"], 0,
                                               after=landed if dist else ())
    landed = [g["mlp_w1_1"], g["mlp_w2_1"]] if dist else ()
    g["mlp_w1_0"] = _wgrad("wgrad_w1_0", hn, da, _full(t, D_MODEL), _cols(t, FF_SLOT), D_MODEL, FF_SLOT, after=landed)
    g["mlp_w2_0"] = _wgrad("wgrad_w2_0", f, dh2, _cols(t, FF_SLOT), _full(t, D_MODEL), FF_SLOT, D_MODEL)

    def wgrad_a_w_out():
        g["a_w_out"] = _wgrad("wgrad_a_w_out", gated, dh1_b, _cols(t, GATE_DIM // N_DEV), _full(t, D_MODEL),
                              GATE_DIM // N_DEV, D_MODEL)
        return [g["a_w_out"]] + [g[k] for k in qkv]

    sums = send_sums("mlp_0", ("mlp_w1_0", "mlp_w2_0", "b_w_o"), wgrad_a_w_out)
    dx, hn, dz, d_mix_g0, d_ln_g, d_ln_b, d_ws, d_bs = _a_mix_bwd(x, z, dh1, mix_g[0], *a_args(), after=sums)
    small = {
        "norm_mix_g": jnp.concatenate([d_mix_g0, d_mix_g1], axis=0),
        "norm_mlp_g": jnp.concatenate([d_mlp_g0, d_mlp_g1], axis=0),
        "a_ln_v_g": d_ln_g.reshape(N_DEV, GATE_DIM // N_DEV),
        "a_ln_v_b": d_ln_b.reshape(N_DEV, GATE_DIM // N_DEV),
        "a_w_s": d_ws.astype(BF16) if dist else d_ws,
        "a_b_s": d_bs.reshape(A_GROUPS, GMLP_BLOCK),
        "b_q_norm_g": d_q_g,
        "kv_src_norm_g": d_src_g,
        "kv_a_norm_g": d_kv_a_g,
        "final_norm_g": d_final_g,
    }
    if dist:
        parts = [small[k].reshape((1,) + small[k].shape) for k in SMALL] + [loss.reshape(1, 1, 1)]
        got = _by_sequencer("gather_small", _exchange_comm(parts=parts), EVERYONE, next(ids))
        small, loss = dict(zip(SMALL, got)), got[-1]
    g["a_w_in"] = _wgrad("wgrad_a_w_in", hn, dz, _full(t, D_MODEL), _cols(t, FF_SLOT), D_MODEL, FF_SLOT)
    return loss, dx, g, small


def _adamw(w, g, m, v):
    m = ADAM_B1 * m + (1.0 - ADAM_B1) * g
    v = ADAM_B2 * v + (1.0 - ADAM_B2) * (g * g)
    m_hat = m / (1.0 - ADAM_B1 ** ADAM_STEP)
    v_hat = v / (1.0 - ADAM_B2 ** ADAM_STEP)
    return -ADAM_LR * (m_hat / (jnp.sqrt(v_hat) + ADAM_EPS) + ADAM_WD * w), m, v


def _sum_in_device_order(r_ref):
    g = r_ref[0].astype(F32)
    for j in range(1, r_ref.shape[0]):
        g = g + r_ref[j].astype(F32)
    return g


def _adamw_sharded(name, recvs, w, m, v):
    layers, r, c = w.shape
    tr = math.gcd(r, 512)
    flat = [a for per_layer in recvs for a in per_layer]

    def body(*refs):
        r_refs, (w_ref, m_ref, v_ref) = refs[:len(flat)], refs[len(flat):len(flat) + 3]
        g_ref, d_ref, nm_ref, nv_ref = refs[-4:]
        layer = pl.program_id(0)
        g, pos = None, 0
        for li, per_layer in enumerate(recvs):
            total = None
            for ref in r_refs[pos:pos + len(per_layer)]:
                part = _sum_in_device_order(ref)
                total = part if total is None else total + part
            pos += len(per_layer)
            g = total if g is None else jnp.where(layer == li, total, g)
        g_ref[...] = g
        d_ref[...], nm_ref[...], nv_ref[...] = _adamw(w_ref[...], g, m_ref[...], v_ref[...])

    blk = pl.BlockSpec((None, tr, c), lambda l, i: (l, i, 0))
    return _call(name, body, (layers, r // tr),
                 [pl.BlockSpec((a.shape[0], tr, c), lambda l, i: (0, i, 0)) for a in flat] + [blk] * 3,
                 [blk] * 4, [_sds(w.shape, F32)] * 4, (*flat, w, m, v))


def _adamw_small(recvs, ws, ms, vs, own_row, losses):
    n = len(recvs)

    def body(*refs):
        r_refs, w_refs, m_refs, v_refs = (refs[i * n:(i + 1) * n] for i in range(4))
        outs, scr = refs[4 * n + 1:8 * n + 2], refs[8 * n + 2:]
        outs[-1][...] = _sum_in_device_order(refs[4 * n])
        me = _my_place()[3]
        for a in range(n):
            g = _sum_in_device_order(r_refs[a])
            if own_row[a]:
                scr[0][...] = g
                g = scr[0][pl.ds(me, 1), :]
            g_ref, d_ref, nm_ref, nv_ref = outs[4 * a:4 * a + 4]
            g_ref[...] = g
            d_ref[...], nm_ref[...], nv_ref[...] = _adamw(w_refs[a][...], g, m_refs[a][...], v_refs[a][...])

    out_shape = []
    for w in ws:
        out_shape += [_sds(w.shape, F32)] * 4
    return pl.pallas_call(
        body, name="adamw_small", in_specs=[VMEM] * (4 * n + 1), out_specs=[VMEM] * (4 * n + 1),
        out_shape=out_shape + [_sds((1, 1), F32)], scratch_shapes=[pltpu.VMEM((N_DEV, GATE_DIM // N_DEV), F32)],
    )(*recvs, *ws, *ms, *vs, losses)


BIG = ("a_w_in", "a_w_out", "b_w_q_a", "b_w_q_b", "b_w_o", "kv_w_a", "kv_w_b", "mlp_w1", "mlp_w2")
SMALL = ("norm_mix_g", "norm_mlp_g", "a_ln_v_g", "a_ln_v_b", "a_w_s", "a_b_s", "b_q_norm_g", "kv_src_norm_g",
         "kv_a_norm_g", "final_norm_g")
WEIGHTS = ("norm_mix_g", "norm_mlp_g", "a_w_in", "a_ln_v_g", "a_ln_v_b", "a_w_s", "a_b_s", "a_w_out", "b_w_q_a",
           "b_q_norm_g", "b_w_q_b", "b_w_o", "kv_src_norm_g", "kv_w_a", "kv_a_norm_g", "kv_w_b", "mlp_w1", "mlp_w2",
           "final_norm_g")


def _two_d(name, a):
    if name in ("a_w_s", "a_b_s"):
        return a.reshape(a.shape[1:])
    return a.reshape(1, -1) if a.ndim == 1 else a


def _three_d(a):
    return a if a.ndim == 3 else a.reshape((1,) + a.shape)


def kernel(x, positions, norm_mix_g, norm_mlp_g, a_w_in, a_ln_v_g, a_ln_v_b, a_w_s, a_b_s, a_w_out, b_w_q_a, b_q_norm_g, b_w_q_b, b_w_o, kv_src_norm_g, kv_w_a, kv_a_norm_g, kv_w_b, mlp_w1, mlp_w2, final_norm_g, loss_target, m_norm_mix_g, m_norm_mlp_g, m_a_w_in, m_a_ln_v_g, m_a_ln_v_b, m_a_w_s, m_a_b_s, m_a_w_out, m_b_w_q_a, m_b_q_norm_g, m_b_w_q_b, m_b_w_o, m_kv_src_norm_g, m_kv_w_a, m_kv_a_norm_g, m_kv_w_b, m_mlp_w1, m_mlp_w2, m_final_norm_g, v_norm_mix_g, v_norm_mlp_g, v_a_w_in, v_a_ln_v_g, v_a_ln_v_b, v_a_w_s, v_a_b_s, v_a_w_out, v_b_w_q_a, v_b_q_norm_g, v_b_w_q_b, v_b_w_o, v_kv_src_norm_g, v_kv_w_a, v_kv_a_norm_g, v_kv_w_b, v_mlp_w1, v_mlp_w2, v_final_norm_g):
    w = dict(norm_mix_g=norm_mix_g, norm_mlp_g=norm_mlp_g, a_w_in=a_w_in, a_ln_v_g=a_ln_v_g, a_ln_v_b=a_ln_v_b,
             a_w_s=a_w_s, a_b_s=a_b_s, a_w_out=a_w_out, b_w_q_a=b_w_q_a, b_q_norm_g=b_q_norm_g, b_w_q_b=b_w_q_b,
             b_w_o=b_w_o, kv_src_norm_g=kv_src_norm_g, kv_w_a=kv_w_a, kv_a_norm_g=kv_a_norm_g, kv_w_b=kv_w_b,
             mlp_w1=mlp_w1, mlp_w2=mlp_w2, final_norm_g=final_norm_g)
    m = dict(norm_mix_g=m_norm_mix_g, norm_mlp_g=m_norm_mlp_g, a_w_in=m_a_w_in, a_ln_v_g=m_a_ln_v_g,
             a_ln_v_b=m_a_ln_v_b, a_w_s=m_a_w_s, a_b_s=m_a_b_s, a_w_out=m_a_w_out, b_w_q_a=m_b_w_q_a,
             b_q_norm_g=m_b_q_norm_g, b_w_q_b=m_b_w_q_b, b_w_o=m_b_w_o, kv_src_norm_g=m_kv_src_norm_g,
             kv_w_a=m_kv_w_a, kv_a_norm_g=m_kv_a_norm_g, kv_w_b=m_kv_w_b, mlp_w1=m_mlp_w1, mlp_w2=m_mlp_w2,
             final_norm_g=m_final_norm_g)
    v = dict(norm_mix_g=v_norm_mix_g, norm_mlp_g=v_norm_mlp_g, a_w_in=v_a_w_in, a_ln_v_g=v_a_ln_v_g,
             a_ln_v_b=v_a_ln_v_b, a_w_s=v_a_w_s, a_b_s=v_a_b_s, a_w_out=v_a_w_out, b_w_q_a=v_b_w_q_a,
             b_q_norm_g=v_b_q_norm_g, b_w_q_b=v_b_w_q_b, b_w_o=v_b_w_o, kv_src_norm_g=v_kv_src_norm_g,
             kv_w_a=v_kv_w_a, kv_a_norm_g=v_kv_a_norm_g, kv_w_b=v_kv_w_b, mlp_w1=v_mlp_w1, mlp_w2=v_mlp_w2,
             final_norm_g=v_final_norm_g)
    t = x.shape[1]

    first = ("a_w_in", "a_w_out", "a_ln_v_g", "a_ln_v_b")
    later = ("mlp_w1_0", "mlp_w2_0", "mlp_w1_1", "mlp_w2_1", "kv_w_a", "kv_w_b", "b_w_q_a", "b_w_q_b", "b_w_o")
    blocks = {k: _three_d(w[k]) for k in BIG if not k.startswith("mlp")}
    for k in ("mlp_w1", "mlp_w2"):
        blocks[k + "_0"], blocks[k + "_1"] = w[k][0:1], w[k][1:2]
    got, casts = _gather_first([blocks[k] if k in blocks else w[k] for k in first], [blocks[k] for k in later])
    wg = dict(zip(first, got))
    wg["a_w_out"] = wg["a_w_out"].reshape(GATE_DIM, D_MODEL)
    wg["a_ln_v_g"] = wg["a_ln_v_g"].reshape(1, GATE_DIM)
    wg["a_ln_v_b"] = wg["a_ln_v_b"].reshape(1, GATE_DIM)
    shards = dict(zip(later, casts))

    sm = {k: _two_d(k, w[k]) for k in SMALL if k not in ("a_ln_v_g", "a_ln_v_b")}
    sm["a_b_st"] = sm["a_b_s"].T
    inv_freq = (ROPE_THETA ** (-jnp.arange(0, QK_ROPE, 2, dtype=F32) / QK_ROPE)).reshape(1, QK_ROPE // 2)

    losses, dx, g, small = _local_step(x[0], positions.reshape(t, 1), loss_target[0], inv_freq, wg, sm, shards)

    names = ("a_w_in", "a_w_out")
    sums = _pair_reduce("pair_reduce_a", [g[k] for k in names], after=[g["mlp_w1_0"], g["mlp_w2_0"]])
    g.update(zip(names, _by_sequencer("exchange_last", _chip_exchange_comm(sums), OTHER_CHIPS, collective_id=1)))

    out = {}
    for k in BIG:
        recvs = [[g[k + "_0"]], [g[k + "_1"]]] if k.startswith("mlp") else [[g[k]]]
        res = _adamw_sharded("adamw_" + k, recvs, _three_d(w[k]), _three_d(m[k]), _three_d(v[k]))
        out[k] = [o.reshape(w[k].shape) for o in res]
    own_row = [k in ("a_ln_v_g", "a_ln_v_b") for k in SMALL]
    res = _adamw_small([small[k] for k in SMALL], [_two_d(k, w[k]) for k in SMALL], [_two_d(k, m[k]) for k in SMALL],
                       [_two_d(k, v[k]) for k in SMALL], own_row, losses)
    for i, k in enumerate(SMALL):
        out[k] = [o.reshape(w[k].shape) for o in res[4 * i:4 * i + 4]]

    return (res[-1].reshape(()), dx.reshape(x.shape), *[out[k][0] for k in WEIGHTS], *[out[k][1] for k in WEIGHTS],
            *[out[k][2] for k in WEIGHTS], *[out[k][3] for k in WEIGHTS])
```

```python
import math

import jax
import jax.numpy as jnp
from jax import lax
from jax.experimental import pallas as pl
from jax.experimental.pallas import tpu as pltpu
from jax.experimental.pallas import tpu_sc as plsc

F32, BF16 = jnp.float32, jnp.bfloat16
MESH = pl.DeviceIdType.MESH
ANY = pl.BlockSpec(memory_space=pl.ANY)
VMEM = pl.BlockSpec(memory_space=pltpu.VMEM)

N_DEV = 8
D_MODEL = 1024
CHUNK = 64
GMLP_BLOCK = 128
GATE_DIM = 2048
A_GROUPS = 8
A_GROUP_DIM = GATE_DIM // A_GROUPS
B_HEADS = 8
QK_NOPE, QK_ROPE, V_HEAD = 128, 64, 128
Q_LORA, KV_LORA = 384, 256
ROPE_THETA = 10000.0
D_FF = 4096
FF_SLOT = D_FF // N_DEV
EPS = 1e-6
ATT_SCALE = (QK_NOPE + QK_ROPE) ** -0.5

ADAM_LR, ADAM_B1, ADAM_B2, ADAM_EPS, ADAM_WD, ADAM_STEP = 0.001, 0.9, 0.999, 1e-08, 0.01, 10

TM = 256
TM_GATE = 128
VMEM_LIMIT = 56 * 1024 * 1024
INV_SQRT2 = 1.0 / math.sqrt(2.0)
INV_SQRT_2PI = 1.0 / math.sqrt(2.0 * math.pi)
LOG2_E = 1.0 / math.log(2.0)
HEADS_PER_STEP = 2


def _dot(a, b):
    return jnp.dot(a, b, preferred_element_type=F32)


def _dot_nt(a, b):
    return lax.dot_general(a, b, (((1,), (1,)), ((), ())), preferred_element_type=F32)


def _dot_tn(a, b):
    return lax.dot_general(a, b, (((0,), (0,)), ((), ())), preferred_element_type=F32)


def _rms_fwd(x, g):
    rstd = lax.rsqrt(jnp.mean(x * x, axis=-1, keepdims=True) + EPS)
    xhat = x * rstd
    return xhat * g, xhat, rstd


def _rms_bwd(dy, xhat, rstd, g):
    dxhat = dy * g
    dx = rstd * (dxhat - xhat * jnp.mean(dxhat * xhat, axis=-1, keepdims=True))
    return dx, jnp.sum(dy * xhat, axis=0, keepdims=True)


def _ln_fwd(v, g, b):
    mu = jnp.mean(v, axis=-1, keepdims=True)
    vc = v - mu
    rstd = lax.rsqrt(jnp.mean(vc * vc, axis=-1, keepdims=True) + EPS)
    vhat = vc * rstd
    return vhat * g + b, vhat, rstd


def _gelu(x):
    return 0.5 * x * (1.0 + lax.erf(x * INV_SQRT2))


def _gelu_and_grad(x):
    cdf = 0.5 * (1.0 + lax.erf(x * INV_SQRT2))
    return x * cdf, cdf + x * jnp.exp(-0.5 * x * x) * INV_SQRT_2PI


def _rope(x, cos, sin):
    x1, x2 = x[:, :QK_ROPE // 2], x[:, QK_ROPE // 2:]
    return jnp.concatenate([x1 * cos - x2 * sin, x2 * cos + x1 * sin], axis=-1)


def _gate_mask():
    row = lax.broadcasted_iota(jnp.int32, (GMLP_BLOCK, GMLP_BLOCK), 0)
    col = lax.broadcasted_iota(jnp.int32, (GMLP_BLOCK, GMLP_BLOCK), 1)
    return (col < CHUNK) | (row >= CHUNK)


def _att_mask(q0, tq, t):
    q = q0 + lax.broadcasted_iota(jnp.int32, (tq, t), 0)
    k = lax.broadcasted_iota(jnp.int32, (tq, t), 1)
    return jnp.right_shift(k, 6) <= jnp.right_shift(q, 6)


def _res(shape, imap=None):
    zeros = (0,) * len(shape)
    return pl.BlockSpec(shape, imap or (lambda i: zeros), pipeline_mode=pl.Buffered(1))


def _const(shape):
    zeros = (0,) * len(shape)
    return pl.BlockSpec(shape, lambda i: zeros)


def _row(d, tm=TM):
    return pl.BlockSpec((tm, d), lambda i: (i, 0))


def _heads(d):
    return pl.BlockSpec((B_HEADS, TM, d), lambda i: (0, i, 0))


def _sds(shape, dt):
    return jax.ShapeDtypeStruct(shape, dt)


def _acc(ref, val):
    @pl.when(pl.program_id(0) == 0)
    def _():
        ref[...] = jnp.zeros_like(ref)
    ref[...] += val


def _my_place():
    x, y, c = lax.axis_index("x"), lax.axis_index("y"), lax.axis_index("c")
    return x, y, c, 4 * x + 2 * y + c


def _peer(x, y, c, k):
    px = 1 - x if k & 4 else x
    py = 1 - y if k & 2 else y
    pc = 1 - c if k & 1 else c
    return (px, py, pc), 4 * px + 2 * py + pc


CHIPS = (2, 4, 6)


def _splits(ref):
    return len(ref.shape) >= 3 and ref.shape[1] % 32 == 0


def _piece(ref, block, half=None):
    if half is None or not _splits(ref):
        return ref.at[pl.ds(block, 1)]
    rows = ref.shape[1] // 2
    return ref.at[pl.ds(block, 1), pl.ds(half * rows, rows)]


def _gather_copy(sems, a, k, piece, to, src=None):
    return pltpu.make_async_remote_copy(
        src_ref=piece if src is None else src, dst_ref=piece, send_sem=sems[0].at[a, k], recv_sem=sems[1].at[a, k],
        device_id=to, device_id_type=MESH)


def _gather_start(srcs, outs, sems, only=None):
    x, y, c, me = _my_place()
    for a in range(len(srcs)) if only is None else (only,):
        mine = _piece(outs[a], me)
        pltpu.make_async_copy(srcs[a], mine, sems[2].at[a]).start()
        for k, rel in enumerate((1, 4, 2)):
            _gather_copy(sems, a, k, mine, _peer(x, y, c, rel)[0], src=srcs[a]).start()


def _gather_relay(srcs, outs, sems):
    x, y, c, _ = _my_place()
    sib = _peer(x, y, c, 1)[0]
    (xn, xn_i), (yn, yn_i) = _peer(x, y, c, 4), _peer(x, y, c, 2)
    for a in range(len(srcs)):
        out = outs[a]
        _gather_copy(sems, a, 1, _piece(out, xn_i), xn).wait_recv()
        _gather_copy(sems, a, 3, _piece(out, xn_i, 0), yn).start()
        _gather_copy(sems, a, 5, _piece(out, xn_i), sib).start()
        _gather_copy(sems, a, 2, _piece(out, yn_i), yn).wait_recv()
        if _splits(out):
            _gather_copy(sems, a, 4, _piece(out, yn_i, 1), xn).start()
        _gather_copy(sems, a, 6, _piece(out, yn_i), sib).start()


def _gather_finish(srcs, outs, sems):
    x, y, c, me = _my_place()
    sib = _peer(x, y, c, 1)[0]
    xn, yn, dg_i = _peer(x, y, c, 4)[0], _peer(x, y, c, 2)[0], _peer(x, y, c, 6)[1]
    n = len(srcs)
    for a in range(n):
        out = outs[a]
        _gather_copy(sems, a, 3, _piece(out, dg_i, 0), yn).wait_recv()
        _gather_copy(sems, a, 7, _piece(out, dg_i, 0), sib).start()
        if _splits(out):
            _gather_copy(sems, a, 4, _piece(out, dg_i, 1), xn).wait_recv()
            _gather_copy(sems, a, 8, _piece(out, dg_i, 1), sib).start()
    for a in range(n):
        out = outs[a]
        whole, half = _piece(out, me), _piece(out, me, 0)
        for k in (0, 5, 6):
            _gather_copy(sems, a, k, whole, sib).wait_recv()
        for k in (7, 8) if _splits(out) else (7,):
            _gather_copy(sems, a, k, half, sib).wait_recv()
        for k in (0, 1, 2):
            _gather_copy(sems, a, k, whole, sib, src=srcs[a]).wait_send()
        for k in (5, 6):
            _gather_copy(sems, a, k, whole, sib).wait_send()
        for k in (3, 4, 7, 8) if _splits(out) else (3, 7):
            _gather_copy(sems, a, k, half, sib).wait_send()
        pltpu.make_async_copy(srcs[a], whole, sems[2].at[a]).wait()


def _relay_sems(n):
    return [pltpu.SemaphoreType.DMA((n, 9)), pltpu.SemaphoreType.DMA((n, 9)), pltpu.SemaphoreType.DMA((n,))]


def _gather_sems(n):
    return [pltpu.SemaphoreType.DMA((n, 7)), pltpu.SemaphoreType.DMA((n, 7)), pltpu.SemaphoreType.DMA((n,))]


class _Comm:
    def __init__(self, args, out_shape, scratch, start, finish, relay=None):
        self.args, self.out_shape, self.scratch, self.start, self.finish = args, out_shape, scratch, start, finish
        self.relay = relay


def _gather_comm(shards):
    return _Comm(list(shards), [_sds((N_DEV,) + s.shape[1:], s.dtype) for s in shards], _relay_sems(len(shards)),
                 _gather_start, _gather_finish, relay=_gather_relay)


def _direct_copies(ins, outs, sems, wait, from_block):
    send_sems, recv_sems, local_sems = sems
    x, y, c, me = _my_place()
    for a in range(len(ins)):
        src = ins[a].at[pl.ds(me, 1)] if from_block[a] else ins[a]
        local = pltpu.make_async_copy(src, outs[a].at[pl.ds(me, 1)], local_sems.at[a])
        local.wait() if wait else local.start()
        for k in range(1, N_DEV):
            to, to_i = _peer(x, y, c, k)
            cp = pltpu.make_async_remote_copy(
                src_ref=ins[a].at[pl.ds(to_i, 1)] if from_block[a] else ins[a], dst_ref=outs[a].at[pl.ds(me, 1)],
                send_sem=send_sems.at[a, k - 1], recv_sem=recv_sems.at[a, k - 1], device_id=to, device_id_type=MESH)
            cp.wait() if wait else cp.start()


def _exchange_comm(grads=(), parts=()):
    ins = list(grads) + list(parts)
    from_block = [True] * len(grads) + [False] * len(parts)
    out_shape = [_sds(g.shape, g.dtype) for g in grads] + [_sds((N_DEV,) + p.shape[1:], p.dtype) for p in parts]

    def start(ins_, outs_, sems_):
        _direct_copies(ins_, outs_, sems_, False, from_block)

    def finish(ins_, outs_, sems_):
        _direct_copies(ins_, outs_, sems_, True, from_block)

    return _Comm(ins, out_shape, _gather_sems(len(ins)), start, finish)


def _chip_copies(ins, outs, sems, wait, rels, own):
    send_sems, recv_sems, local_sems = sems
    x, y, c, _ = _my_place()
    for a in range(len(ins)):
        if own:
            local = pltpu.make_async_copy(ins[a].at[pl.ds(2 * x + y, 1)], outs[a].at[pl.ds(len(rels), 1)],
                                          local_sems.at[a])
            local.wait() if wait else local.start()
        for i, j in enumerate(rels):
            to = _peer(x, y, c, CHIPS[j])[0]
            cp = pltpu.make_async_remote_copy(
                src_ref=ins[a].at[pl.ds(2 * to[0] + to[1], 1)], dst_ref=outs[a].at[pl.ds(i, 1)],
                send_sem=send_sems.at[a, i], recv_sem=recv_sems.at[a, i], device_id=to, device_id_type=MESH)
            cp.wait() if wait else cp.start()


def _chip_exchange_comm(sums, rels=(0, 1, 2), own=True):
    def start(ins_, outs_, sems_):
        _chip_copies(ins_, outs_, sems_, False, rels, own)

    def finish(ins_, outs_, sems_):
        _chip_copies(ins_, outs_, sems_, True, rels, own)

    n = len(sums)
    sems = [pltpu.SemaphoreType.DMA((n, len(rels))), pltpu.SemaphoreType.DMA((n, len(rels))),
            pltpu.SemaphoreType.DMA((n,))]
    return _Comm(list(sums), [_sds((len(rels) + own,) + s.shape[1:], s.dtype) for s in sums], sems, start, finish)


def _pair_reduce(name, grads, after=()):
    n = len(grads)
    n_chips = N_DEV // 2

    def body(*refs):
        g_refs, gh_refs, refs = refs[:n], refs[n:2 * n], refs[2 * n + len(after):]
        p_refs, land = refs[:n], refs[n:2 * n]
        send_sems, recv_sems = refs[2 * n:]
        x, y, c, _ = _my_place()
        sib = _peer(x, y, c, 1)[0]
        q = pl.program_id(0)

        def to_sibling(a, j):
            return pltpu.make_async_remote_copy(
                src_ref=gh_refs[a].at[j, pl.ds(1 - c, 1)], dst_ref=land[a].at[pl.ds(j, 1)],
                send_sem=send_sems.at[a, j], recv_sem=recv_sems.at[a, j], device_id=sib, device_id_type=MESH)

        @pl.when(q == 0)
        def _():
            for j in range(n_chips):
                for a in range(n):
                    to_sibling(a, j).start()

        for a in range(n):
            to_sibling(a, q).wait_recv()
            p_refs[a][...] = (g_refs[a][0, pl.ds(c, 1)].astype(F32) + land[a][pl.ds(q, 1)].astype(F32)).astype(BF16)

        @pl.when(q == n_chips - 1)
        def _():
            for a in range(n):
                for j in range(n_chips):
                    to_sibling(a, j).wait_send()

    views = [g.reshape((n_chips, 2) + g.shape[1:]) for g in grads]
    res = pl.pallas_call(
        body, name=name, grid=(n_chips,),
        in_specs=[pl.BlockSpec((1, 2) + g.shape[1:], lambda q: (q, 0, 0, 0)) for g in grads]
        + [ANY] * (n + len(after)),
        out_specs=[pl.BlockSpec((1,) + g.shape[1:], lambda q: (q, 0, 0)) for g in grads],
        out_shape=[_sds((n_chips,) + g.shape[1:], BF16) for g in grads],
        scratch_shapes=[pltpu.VMEM((n_chips,) + g.shape[1:], BF16) for g in grads]
        + [pltpu.SemaphoreType.DMA((n, n_chips)), pltpu.SemaphoreType.DMA((n, n_chips))],
        compiler_params=pltpu.CompilerParams(dimension_semantics=("arbitrary",), vmem_limit_bytes=VMEM_LIMIT),
    )(*views, *views, *after)
    return list(res)


def _pair_exchange_comm(grads):
    n, n_chips = len(grads), N_DEV // 2

    def copies(ins, outs, sems, wait):
        x, y, c, _ = _my_place()
        for j in range(n_chips):
            for a in range(n):
                cp = pltpu.make_async_remote_copy(
                    src_ref=ins[a].at[j, pl.ds(1 - c, 1)], dst_ref=outs[a].at[pl.ds(j, 1)], send_sem=sems[0].at[a, j],
                    recv_sem=sems[1].at[a, j], device_id=_peer(x, y, c, 1)[0], device_id_type=MESH)
                cp.wait() if wait else cp.start()

    views = [g.reshape((n_chips, 2) + g.shape[1:]) for g in grads]
    sems = [pltpu.SemaphoreType.DMA((n, n_chips)), pltpu.SemaphoreType.DMA((n, n_chips))]
    return _Comm(views, [_sds((n_chips,) + g.shape[1:], g.dtype) for g in grads], sems,
                 lambda i, o, s: copies(i, o, s, False), lambda i, o, s: copies(i, o, s, True))


def _pair_add(name, grads, landed, after=()):
    n, n_chips = len(grads), N_DEV // 2

    def body(core_ref, *refs):
        g_refs, l_refs, p_refs = refs[:n], refs[n:2 * n], refs[2 * n + len(after):]
        for a in range(n):
            p_refs[a][...] = (g_refs[a][...].astype(F32) + l_refs[a][...].astype(F32)).astype(BF16)

    views = [g.reshape((n_chips, 2) + g.shape[1:]) for g in grads]
    blocks = [pl.BlockSpec((1,) + g.shape[1:], lambda q, core: (q, 0, 0)) for g in grads]
    mine = [pl.BlockSpec((1, None) + g.shape[1:], lambda q, core: (q, core[0], 0, 0)) for g in grads]
    return list(pl.pallas_call(
        body, name=name, out_shape=[_sds((n_chips,) + g.shape[1:], BF16) for g in grads],
        grid_spec=pltpu.PrefetchScalarGridSpec(num_scalar_prefetch=1, grid=(n_chips,),
                                               in_specs=mine + blocks + [ANY] * len(after), out_specs=blocks),
        compiler_params=pltpu.CompilerParams(dimension_semantics=("arbitrary",), vmem_limit_bytes=VMEM_LIMIT),
    )(lax.axis_index("c").reshape(1), *views, *landed, *after))


def _call(name, body, grid, in_specs, out_specs, out_shape, args, scratch=(), after=()):
    ni, na = len(in_specs), len(after)

    def ordered(*refs):
        body(*refs[:ni], *refs[ni + na:])

    return list(pl.pallas_call(
        ordered if after else body, name=name, grid=grid, in_specs=list(in_specs) + [ANY] * na,
        out_specs=list(out_specs), out_shape=list(out_shape), scratch_shapes=list(scratch),
        compiler_params=pltpu.CompilerParams(dimension_semantics=("arbitrary",) * len(grid),
                                             vmem_limit_bytes=VMEM_LIMIT))(*args, *after))


SIBLING_AND_NEIGHBOURS, OTHER_CHIPS, EVERYONE = (1, 4, 2), CHIPS, tuple(range(1, N_DEV))


def _by_sequencer(name, comm, peers, collective_id):
    src = [jax.new_ref(a, memory_space=pltpu.MemorySpace.HBM) for a in comm.args]
    dst = [jax.empty_ref(s, memory_space=pltpu.MemorySpace.HBM) for s in comm.out_shape]

    @pl.kernel(mesh=plsc.ScalarSubcoreMesh(axis_name="sequencer", num_cores=1), name=name,
               scratch_types=tuple(comm.scratch), compiler_params=pltpu.CompilerParams(collective_id=collective_id))
    def launch(*sems):
        x, y, c, _ = _my_place()
        barrier = pltpu.get_barrier_semaphore()
        for k in peers:
            pl.semaphore_signal(barrier, inc=1, device_id=_peer(x, y, c, k)[0], device_id_type=MESH)
        pl.semaphore_wait(barrier, len(peers))
        comm.start(src, dst, sems)
        if comm.relay is not None:
            comm.relay(src, dst, sems)
        comm.finish(src, dst, sems)

    launch()
    return [d[...] for d in dst]


def _gather_first(first, later):
    nf = len(first)
    layer_of = [(a, l) for a, s in enumerate(later) for l in range(s.shape[0])]
    nl = len(layer_of)
    dts = [BF16] * (nf - 2) + [F32, F32]

    def body(*refs):
        ins, refs = refs[:nf + len(later)], refs[nf + len(later):]
        outs, refs = refs[:nf], refs[nf:]
        casts, refs = refs[:nl], refs[nl:]
        stage, sems = refs[:nf], refs[nf:]
        for a in range(nf):
            stage[a][...] = ins[a][...].astype(dts[a])
            _gather_start(stage, outs, sems, only=a)
        for k, (a, l) in enumerate(layer_of):
            casts[k][...] = ins[nf + a][l:l + 1].astype(BF16)
        _gather_relay(stage, outs, sems)
        _gather_finish(stage, outs, sems)

    res = pl.pallas_call(
        body, name="gather_first",
        in_specs=[VMEM] * (nf + len(later)), out_specs=[ANY] * nf + [VMEM] * nl,
        out_shape=[_sds((N_DEV,) + s.shape[1:], dt) for s, dt in zip(first, dts)]
        + [_sds((1,) + later[a].shape[1:], BF16) for a, _ in layer_of],
        scratch_shapes=[pltpu.VMEM(s.shape, dt) for s, dt in zip(first, dts)] + _relay_sems(nf),
        compiler_params=pltpu.CompilerParams(vmem_limit_bytes=VMEM_LIMIT),
    )(*first, *later)
    return list(res[:nf]), list(res[nf:])


def _a_mix_fwd(x, g, w_in, ln_g, ln_b, w_s, b_st, w_out):
    t = x.shape[0]
    nblk = TM // GMLP_BLOCK

    def body(x_ref, g_ref, win_ref, lng_ref, lnb_ref, ws_ref, bst_ref, wout_ref, h_ref, z_ref, gated_scr):
        xv = x_ref[...]
        hb = _rms_fwd(xv, g_ref[...])[0].astype(BF16)
        for d in range(N_DEV):
            z_ref[:, d * FF_SLOT:(d + 1) * FF_SLOT] = _dot(hb, win_ref[d])
        u = _gelu(z_ref[:, :GATE_DIM])
        vb = _ln_fwd(_gelu(z_ref[:, GATE_DIM:]), lng_ref[...], lnb_ref[...])[0].astype(BF16)
        mask = _gate_mask()
        for gi in range(A_GROUPS):
            wm = jnp.where(mask, ws_ref[gi], 0.0).astype(BF16)
            bias = bst_ref[:, gi:gi + 1]
            cs = slice(gi * A_GROUP_DIM, (gi + 1) * A_GROUP_DIM)
            for n in range(nblk):
                rs = slice(n * GMLP_BLOCK, (n + 1) * GMLP_BLOCK)
                sv = _dot(wm, vb[rs, cs]) + bias
                gated_scr[rs, cs] = (u[rs, cs] * sv).astype(BF16)
        h_ref[...] = xv + _dot(gated_scr[...], wout_ref[...])

    return _call(
        "a_mix_fwd", body, (t // TM,),
        [_row(D_MODEL), _res((1, D_MODEL)), _res((N_DEV, D_MODEL, FF_SLOT)), _res((1, GATE_DIM)),
         _res((1, GATE_DIM)), _res((A_GROUPS, GMLP_BLOCK, GMLP_BLOCK)), _res((GMLP_BLOCK, A_GROUPS)),
         _res((GATE_DIM, D_MODEL))],
        [_row(D_MODEL), _row(2 * GATE_DIM), _row(GATE_DIM)],
        [_sds((t, D_MODEL), F32), _sds((t, 2 * GATE_DIM), F32), _sds((t, GATE_DIM), BF16)],
        (x, g, w_in, ln_g, ln_b, w_s, b_st, w_out))


MLP_W_SPECS = (_res((N_DEV, D_MODEL, FF_SLOT)), _res((N_DEV, FF_SLOT, D_MODEL)))


def _mlp_fwd(h, g, w1, w2):
    t = h.shape[0]

    def body(h_ref, g_ref, w1_ref, w2_ref, o_ref, a_ref):
        hv = h_ref[...]
        hb = _rms_fwd(hv, g_ref[...])[0].astype(BF16)
        o_ref[...] = hv
        for d in range(N_DEV):
            a = _dot(hb, w1_ref[d])
            a_ref[:, d * FF_SLOT:(d + 1) * FF_SLOT] = a
            r = jnp.maximum(a, 0.0)
            o_ref[...] += _dot((r * r).astype(BF16), w2_ref[d])

    return _call(
        "mlp_fwd", body, (t // TM,), [_row(D_MODEL), _res((1, D_MODEL)), *MLP_W_SPECS],
        [_row(D_MODEL), _row(D_FF)], [_sds((t, D_MODEL), F32), _sds((t, D_FF), F32)], (h, g, w1, w2))


def _mlp_fwd_loss(h, g, w1, w2, final_g, target):
    t = h.shape[0]

    def body(h_ref, g_ref, w1_ref, w2_ref, fg_ref, t_ref, a_ref, loss_ref, dh_ref, dg_ref):
        hv = h_ref[...]
        hb = _rms_fwd(hv, g_ref[...])[0].astype(BF16)
        out = hv
        for d in range(N_DEV):
            a = _dot(hb, w1_ref[d])
            a_ref[:, d * FF_SLOT:(d + 1) * FF_SLOT] = a
            r = jnp.maximum(a, 0.0)
            out = out + _dot((r * r).astype(BF16), w2_ref[d])
        y, xhat, rstd = _rms_fwd(out, fg_ref[...])
        err = y - t_ref[...]
        part = 0.5 * jnp.sum(jnp.mean(err * err, axis=-1, keepdims=True), axis=0, keepdims=True)
        dx, dg = _rms_bwd(err * (1.0 / D_MODEL), xhat, rstd, fg_ref[...])
        dh_ref[...] = dx
        _acc(dg_ref, dg)
        _acc(loss_ref, part)

    return _call(
        "mlp_fwd_loss", body, (t // TM,),
        [_row(D_MODEL), _res((1, D_MODEL)), *MLP_W_SPECS, _res((1, D_MODEL)), _row(D_MODEL)],
        [_row(D_FF), _const((1, 1)), _row(D_MODEL), _const((1, D_MODEL))],
        [_sds((t, D_FF), F32), _sds((1, 1), F32), _sds((t, D_MODEL), F32), _sds((1, D_MODEL), F32)],
        (h, g, w1, w2, final_g, target))


KVQ_W_SPECS = (_res((1, D_MODEL)), _res((D_MODEL, KV_LORA + QK_ROPE)), _res((1, KV_LORA)),
               _res((B_HEADS, KV_LORA, QK_NOPE + V_HEAD)), _res((1, D_MODEL)), _res((D_MODEL, Q_LORA)),
               _res((1, Q_LORA)), _res((B_HEADS, Q_LORA, QK_NOPE + QK_ROPE)))


def _kvq_fwd(h, pos, inv_freq, kvq_w):
    t = h.shape[0]
    half = QK_ROPE // 2

    def body(h_ref, pos_ref, invf_ref, srcg_ref, wkva_ref, kvag_ref, wkvb_ref, mixg_ref, wqa_ref, qg_ref, wqb_ref,
             ckv_ref, k_ref, v_ref, cqpre_ref, q_ref, cos_ref, sin_ref):
        hv = h_ref[...]
        xhat = hv * lax.rsqrt(jnp.mean(hv * hv, axis=-1, keepdims=True) + EPS)
        ang = pos_ref[...].astype(F32) * invf_ref[...]
        cos, sin = jnp.cos(ang), jnp.sin(ang)
        cos_ref[...] = cos
        sin_ref[...] = sin
        ckv = _dot((xhat * srcg_ref[...]).astype(BF16), wkva_ref[...])
        ckv_ref[...] = ckv
        cb = _rms_fwd(ckv[:, :KV_LORA], kvag_ref[...])[0].astype(BF16)
        kpe = _rope(ckv[:, KV_LORA:], cos, sin).astype(BF16)
        for hd in range(B_HEADS):
            kv = _dot(cb, wkvb_ref[hd])
            k_ref[hd, :, 0:QK_NOPE] = kv[:, :QK_NOPE].astype(BF16)
            k_ref[hd, :, QK_NOPE:] = kpe
            v_ref[hd] = kv[:, QK_NOPE:].astype(BF16)
        cqpre = _dot((xhat * mixg_ref[...]).astype(BF16), wqa_ref[...])
        cqpre_ref[...] = cqpre
        cqb = _rms_fwd(cqpre, qg_ref[...])[0].astype(BF16)
        for hd in range(B_HEADS):
            q = _dot(cqb, wqb_ref[hd])
            q_ref[hd, :, 0:QK_NOPE] = q[:, :QK_NOPE].astype(BF16)
            q_ref[hd, :, QK_NOPE:] = _rope(q[:, QK_NOPE:], cos, sin).astype(BF16)

    return _call(
        "kvq_fwd", body, (t // TM,), [_row(D_MODEL), _row(1), _res((1, half)), *KVQ_W_SPECS],
        [_row(KV_LORA + QK_ROPE), _heads(QK_NOPE + QK_ROPE), _heads(V_HEAD), _row(Q_LORA),
         _heads(QK_NOPE + QK_ROPE), _row(half), _row(half)],
        [_sds((t, KV_LORA + QK_ROPE), F32), _sds((B_HEADS, t, QK_NOPE + QK_ROPE), BF16),
         _sds((B_HEADS, t, V_HEAD), BF16), _sds((t, Q_LORA), F32), _sds((B_HEADS, t, QK_NOPE + QK_ROPE), BF16),
         _sds((t, half), F32), _sds((t, half), F32)],
        (h, pos, inv_freq, *kvq_w))


def _softmax_rows(q, k_ref, k):
    past, upto = k * TM, (k + 1) * TM
    s = _dot_nt(q, k_ref[0:upto, :])
    own = jnp.where(_att_mask(0, TM, TM), s[:, past:], jnp.finfo(F32).min)
    s = own if k == 0 else jnp.concatenate([s[:, :past], own], axis=1)
    e = jnp.exp2((s - jnp.max(s, axis=-1, keepdims=True)) * (ATT_SCALE * LOG2_E))
    return e * (1.0 / jnp.sum(e, axis=-1, keepdims=True))


def _for_my_tile(i, nq, fn):
    for k in range(nq):
        @pl.when(i == k)
        def _(k=k):
            fn(k)


def _attn_fwd(h, q, k, v, w_o):
    t = h.shape[0]
    nq, hps = t // TM, HEADS_PER_STEP

    def body(h_ref, q_ref, k_ref, v_ref, wo_ref, o_ref, att_ref):
        i, pair = pl.program_id(0), pl.program_id(1)

        @pl.when(pair == 0)
        def _():
            o_ref[...] = h_ref[...]

        def tile(kt):
            proj = None
            for j in range(hps):
                hd = pair * hps + j
                p = _softmax_rows(q_ref[j], k_ref.at[hd], kt)
                ob = _dot(p.astype(BF16), v_ref[hd, 0:(kt + 1) * TM, :]).astype(BF16)
                att_ref[j] = ob
                proj = _dot(ob, wo_ref[hd]) if proj is None else proj + _dot(ob, wo_ref[hd])
            o_ref[...] += proj

        _for_my_tile(i, nq, tile)

    def per_head(d):
        return pl.BlockSpec((hps, TM, d), lambda i, pair: (pair, i, 0))

    def resident(shape):
        zeros = (0,) * len(shape)
        return pl.BlockSpec(shape, lambda i, pair: zeros, pipeline_mode=pl.Buffered(1))

    tile_spec = pl.BlockSpec((TM, D_MODEL), lambda i, pair: (i, 0))
    return _call(
        "attn_fwd", body, (nq, B_HEADS // hps),
        [tile_spec, per_head(QK_NOPE + QK_ROPE), resident((B_HEADS, t, QK_NOPE + QK_ROPE)),
         resident((B_HEADS, t, V_HEAD)), resident((B_HEADS, V_HEAD, D_MODEL))],
        [tile_spec, per_head(V_HEAD)], [_sds((t, D_MODEL), F32), _sds((B_HEADS, t, V_HEAD), BF16)],
        (h, q, k, v, w_o))


def _mlp_bwd(h, a, dho, g, w1, w2, layer, after=()):
    t = h.shape[0]

    def body(h_ref, a_ref, dho_ref, g_ref, w1_ref, w2_ref, dhi_ref, dg_ref, hn_ref, f_ref, da_ref, dhib_ref):
        gv = g_ref[...]
        y, xhat, rstd = _rms_fwd(h_ref[...], gv)
        hn_ref[...] = y.astype(BF16)
        dho_v = dho_ref[...]
        dhob = dho_v.astype(BF16)
        dhn = jnp.zeros((TM, D_MODEL), F32)
        for d in range(N_DEV):
            cs = slice(d * FF_SLOT, (d + 1) * FF_SLOT)
            r = jnp.maximum(a_ref[:, cs], 0.0)
            f_ref[:, cs] = (r * r).astype(BF16)
            da = (_dot_nt(dhob, w2_ref[d]) * (2.0 * r)).astype(BF16)
            da_ref[:, cs] = da
            dhn = dhn + _dot_nt(da, w1_ref[d])
        dx, dg = _rms_bwd(dhn, xhat, rstd, gv)
        dhi = dho_v + dx
        dhi_ref[...] = dhi
        dhib_ref[...] = dhi.astype(BF16)
        _acc(dg_ref, dg)

    return _call(
        f"mlp_bwd_{layer}", body, (t // TM,),
        [_row(D_MODEL), _row(D_FF), _row(D_MODEL), _res((1, D_MODEL)), *MLP_W_SPECS],
        [_row(D_MODEL), _const((1, D_MODEL)), _row(D_MODEL), _row(D_FF), _row(D_FF), _row(D_MODEL)],
        [_sds((t, D_MODEL), F32), _sds((1, D_MODEL), F32), _sds((t, D_MODEL), BF16), _sds((t, D_FF), BF16),
         _sds((t, D_FF), BF16), _sds((t, D_MODEL), BF16)],
        (h, a, dho, g, w1, w2), after=after)


def _attn_bwd(dh, q, k, v, w_o, cos, sin, after=()):
    t = dh.shape[0]
    half, hps = QK_ROPE // 2, HEADS_PER_STEP

    def body(dh_ref, q_ref, k_ref, v_ref, wo_ref, cos_ref, sin_ref, dq_ref, dk_ref, dv_ref):
        i = pl.program_id(1)

        @pl.when(i == 0)
        def _():
            dk_ref[...] = jnp.zeros_like(dk_ref)
            dv_ref[...] = jnp.zeros_like(dv_ref)

        def tile(kt):
            keys = slice(0, (kt + 1) * TM)
            for j in range(hps):
                qj = q_ref[j]
                do = _dot_nt(dh_ref[kt * TM:(kt + 1) * TM, :], wo_ref[j]).astype(BF16)
                p = _softmax_rows(qj, k_ref.at[j], kt)
                dp = _dot_nt(do, v_ref[j, keys, :])
                ds = (p * (dp - jnp.sum(p * dp, axis=-1, keepdims=True)) * ATT_SCALE).astype(BF16)
                dq = _dot(ds, k_ref[j, keys, :])
                dq_ref[j, :, 0:QK_NOPE] = dq[:, :QK_NOPE].astype(BF16)
                dq_ref[j, :, QK_NOPE:] = _rope(dq[:, QK_NOPE:], cos_ref[...], -sin_ref[...]).astype(BF16)
                dk_ref[j, keys, :] += _dot_tn(ds, qj)
                dv_ref[j, keys, :] += _dot_tn(p.astype(BF16), do)

        _for_my_tile(i, t // TM, tile)

    def per_pair(rows, d, tiled):
        return pl.BlockSpec((hps, rows, d), (lambda pair, i: (pair, i, 0)) if tiled else (lambda pair, i: (pair, 0, 0)))

    def tile(d):
        return pl.BlockSpec((TM, d), lambda pair, i: (i, 0))

    return _call(
        "attn_bwd", body, (B_HEADS // hps, t // TM),
        [pl.BlockSpec((t, D_MODEL), lambda pair, i: (0, 0), pipeline_mode=pl.Buffered(1)),
         per_pair(TM, QK_NOPE + QK_ROPE, True), per_pair(t, QK_NOPE + QK_ROPE, False), per_pair(t, V_HEAD, False),
         per_pair(V_HEAD, D_MODEL, False), tile(half), tile(half)],
        [per_pair(TM, QK_NOPE + QK_ROPE, True), per_pair(t, QK_NOPE + QK_ROPE, False), per_pair(t, V_HEAD, False)],
        [_sds((B_HEADS, t, QK_NOPE + QK_ROPE), BF16), _sds((B_HEADS, t, QK_NOPE + QK_ROPE), F32),
         _sds((B_HEADS, t, V_HEAD), F32)],
        (dh, q, k, v, w_o, cos, sin), after=after)


def _kvq_bwd(h, dh, ckv, cqpre, dq, dk, dv, cos, sin, kvq_w, after=()):
    t = h.shape[0]
    half, last = QK_ROPE // 2, t // TM - 1
    grad_shapes = [(D_MODEL, Q_LORA), (B_HEADS, Q_LORA, QK_NOPE + QK_ROPE), (D_MODEL, KV_LORA + QK_ROPE),
                   (B_HEADS, KV_LORA, QK_NOPE + V_HEAD)]

    def body(h_ref, dh_ref, ckv_ref, cqpre_ref, dq_ref, dk_ref, dv_ref, cos_ref, sin_ref,
             srcg_ref, wkva_ref, kvag_ref, wkvb_ref, mixg_ref, wqa_ref, qg_ref, wqb_ref,
             dhi_ref, dmixg_ref, dsrcg_ref, dqg_ref, dkvag_ref, gqa_ref, gqb_ref, gkva_ref, gkvb_ref,
             aqa, aqb, akva, akvb):
        @pl.when(pl.program_id(0) == 0)
        def _():
            for acc in (aqa, aqb, akva, akvb):
                acc[...] = jnp.zeros_like(acc)

        hv = h_ref[...]
        rstd = lax.rsqrt(jnp.mean(hv * hv, axis=-1, keepdims=True) + EPS)
        xhat = hv * rstd
        mixg, srcg, qg, kvag = mixg_ref[...], srcg_ref[...], qg_ref[...], kvag_ref[...]
        cq, cqhat, crstd = _rms_fwd(cqpre_ref[...], qg)
        cqb = cq.astype(BF16)
        dcq = jnp.zeros((TM, Q_LORA), F32)
        for hd in range(B_HEADS):
            dcq = dcq + _dot_nt(dq_ref[hd], wqb_ref[hd])
            aqb[hd] += _dot_tn(cqb, dq_ref[hd])
        dcqpre, dqg = _rms_bwd(dcq, cqhat, crstd, qg)
        dcqpre_b = dcqpre.astype(BF16)
        aqa[...] += _dot_tn((xhat * mixg).astype(BF16), dcqpre_b)
        dxq, dmixg = _rms_bwd(_dot_nt(dcqpre_b, wqa_ref[...]), xhat, rstd, mixg)
        ckv = ckv_ref[...]
        c, chat, krstd = _rms_fwd(ckv[:, :KV_LORA], kvag)
        cb = c.astype(BF16)
        dc = jnp.zeros((TM, KV_LORA), F32)
        dkpe = jnp.zeros((TM, QK_ROPE), F32)
        for hd in range(B_HEADS):
            dkv = jnp.concatenate([dk_ref[hd, :, 0:QK_NOPE], dv_ref[hd]], axis=-1).astype(BF16)
            akvb[hd] += _dot_tn(cb, dkv)
            dc = dc + _dot_nt(dkv, wkvb_ref[hd])
            dkpe = dkpe + dk_ref[hd, :, QK_NOPE:]
        dlat, dkvag = _rms_bwd(dc, chat, krstd, kvag)
        dpe = _rope(dkpe, cos_ref[...], -sin_ref[...])
        dckv_b = jnp.concatenate([dlat, dpe], axis=-1).astype(BF16)
        akva[...] += _dot_tn((xhat * srcg).astype(BF16), dckv_b)
        dxk, dsrcg = _rms_bwd(_dot_nt(dckv_b, wkva_ref[...]), xhat, rstd, srcg)
        dhi_ref[...] = dh_ref[...] + dxq + dxk
        _acc(dmixg_ref, dmixg)
        _acc(dsrcg_ref, dsrcg)
        _acc(dqg_ref, dqg)
        _acc(dkvag_ref, dkvag)

        @pl.when(pl.program_id(0) == last)
        def _():
            for out, acc in ((gqa_ref, aqa), (gqb_ref, aqb), (gkva_ref, akva), (gkvb_ref, akvb)):
                out[...] = acc[...].astype(BF16)

    return _call(
        "kvq_bwd", body, (t // TM,),
        [_row(D_MODEL), _row(D_MODEL), _row(KV_LORA + QK_ROPE), _row(Q_LORA), _heads(QK_NOPE + QK_ROPE),
         _heads(QK_NOPE + QK_ROPE), _heads(V_HEAD), _row(half), _row(half), *KVQ_W_SPECS],
        [_row(D_MODEL), _const((1, D_MODEL)), _const((1, D_MODEL)), _const((1, Q_LORA)), _const((1, KV_LORA))]
        + [_const(s) for s in grad_shapes],
        [_sds((t, D_MODEL), F32), _sds((1, D_MODEL), F32), _sds((1, D_MODEL), F32), _sds((1, Q_LORA), F32),
         _sds((1, KV_LORA), F32)] + [_sds(s, BF16) for s in grad_shapes],
        (h, dh, ckv, cqpre, dq, dk, dv, cos, sin, *kvq_w), scratch=[pltpu.VMEM(s, F32) for s in grad_shapes],
        after=after)


def _a_mix_bwd(x, z, dh, g, w_in, ln_g, ln_b, w_s, b_st, w_out, after=()):
    t = x.shape[0]
    tm = TM_GATE
    nblk = tm // GMLP_BLOCK

    def body(x_ref, z_ref, dh_ref, g_ref, win_ref, lng_ref, lnb_ref, ws_ref, bst_ref, wout_ref,
             dx_ref, hn_ref, dz_ref, dg_ref, dlng_ref, dlnb_ref, dws_ref, dbs_ref, dvn_scr, gelu_grad_v):
        @pl.when(pl.program_id(0) == 0)
        def _():
            dws_ref[...] = jnp.zeros_like(dws_ref)
            dbs_ref[...] = jnp.zeros_like(dbs_ref)

        gv, lng = g_ref[...], lng_ref[...]
        y, xhat, rstd = _rms_fwd(x_ref[...], gv)
        hn_ref[...] = y.astype(BF16)
        dhv = dh_ref[...]
        dgated = _dot_nt(dhv.astype(BF16), wout_ref[...])
        u, gelu_grad_u = _gelu_and_grad(z_ref[:, :GATE_DIM])
        v, gelu_grad_v[...] = _gelu_and_grad(z_ref[:, GATE_DIM:])
        vn, vhat, lrstd = _ln_fwd(v, lng, lnb_ref[...])
        vb = vn.astype(BF16)
        mask = _gate_mask()
        for gi in range(A_GROUPS):
            wm = jnp.where(mask, ws_ref[gi], 0.0).astype(BF16)
            bias = bst_ref[:, gi:gi + 1]
            cs = slice(gi * A_GROUP_DIM, (gi + 1) * A_GROUP_DIM)
            dws = jnp.zeros((GMLP_BLOCK, GMLP_BLOCK), F32)
            dbs = jnp.zeros((GMLP_BLOCK, 1), F32)
            for n in range(nblk):
                rs = slice(n * GMLP_BLOCK, (n + 1) * GMLP_BLOCK)
                sv = _dot(wm, vb[rs, cs]) + bias
                dz_ref[rs, cs] = (dgated[rs, cs] * sv * gelu_grad_u[rs, cs]).astype(BF16)
                dsv = dgated[rs, cs] * u[rs, cs]
                dsvb = dsv.astype(BF16)
                dws = dws + _dot_nt(dsvb, vb[rs, cs])
                dbs = dbs + jnp.sum(dsv, axis=-1, keepdims=True)
                dvn_scr[rs, cs] = _dot_tn(wm, dsvb)
            dws_ref[gi] += jnp.where(mask, dws, 0.0)
            dbs_ref[gi] += dbs
        dvn = dvn_scr[...]
        dvhat = dvn * lng
        dv = lrstd * (dvhat - jnp.mean(dvhat, axis=-1, keepdims=True)
                      - vhat * jnp.mean(dvhat * vhat, axis=-1, keepdims=True))
        dz_ref[:, GATE_DIM:] = (dv * gelu_grad_v[...]).astype(BF16)
        dhn = jnp.zeros((tm, D_MODEL), F32)
        for d in range(N_DEV):
            dhn = dhn + _dot_nt(dz_ref[:, d * FF_SLOT:(d + 1) * FF_SLOT], win_ref[d])
        dx, dg = _rms_bwd(dhn, xhat, rstd, gv)
        dx_ref[...] = dhv + dx
        _acc(dg_ref, dg)
        _acc(dlng_ref, jnp.sum(dvn * vhat, axis=0, keepdims=True))
        _acc(dlnb_ref, jnp.sum(dvn, axis=0, keepdims=True))

    return _call(
        "a_mix_bwd", body, (t // tm,),
        [_row(D_MODEL, tm), _row(2 * GATE_DIM, tm), _row(D_MODEL, tm), _res((1, D_MODEL)),
         _res((N_DEV, D_MODEL, FF_SLOT)), _res((1, GATE_DIM)), _res((1, GATE_DIM)),
         _res((A_GROUPS, GMLP_BLOCK, GMLP_BLOCK)), _res((GMLP_BLOCK, A_GROUPS)), _res((GATE_DIM, D_MODEL))],
        [_row(D_MODEL, tm), _row(D_MODEL, tm), _row(2 * GATE_DIM, tm),
         _const((1, D_MODEL)), _const((1, GATE_DIM)), _const((1, GATE_DIM)),
         _const((A_GROUPS, GMLP_BLOCK, GMLP_BLOCK)), _const((A_GROUPS, GMLP_BLOCK, 1))],
        [_sds((t, D_MODEL), F32), _sds((t, D_MODEL), BF16),
         _sds((t, 2 * GATE_DIM), BF16), _sds((1, D_MODEL), F32), _sds((1, GATE_DIM), F32),
         _sds((1, GATE_DIM), F32), _sds((A_GROUPS, GMLP_BLOCK, GMLP_BLOCK), F32),
         _sds((A_GROUPS, GMLP_BLOCK, 1), F32)],
        (x, z, dh, g, w_in, ln_g, ln_b, w_s, b_st, w_out),
        scratch=[pltpu.VMEM((tm, GATE_DIM), F32), pltpu.VMEM((tm, GATE_DIM), F32)], after=after)


def _wgrad(name, a, b, a_spec, b_spec, m, n, after=()):
    def body(a_ref, b_ref, o_ref):
        o_ref[0] = _dot_tn(a_ref[...].astype(BF16), b_ref[...].astype(BF16)).astype(BF16)

    return _call(name, body, (N_DEV,), [a_spec, b_spec], [pl.BlockSpec((1, m, n), lambda d: (d, 0, 0))],
                 [_sds((N_DEV, m, n), BF16)], (a, b), after=after)[0]


def _full(t, d):
    return pl.BlockSpec((t, d), lambda i: (0, 0), pipeline_mode=pl.Buffered(1))


def _cols(t, d):
    return pl.BlockSpec((t, d), lambda i: (0, i))


def _head(t, d):
    return pl.BlockSpec((None, t, d), lambda i: (i, 0, 0))


def _local_step(x, pos, target, inv_freq, wg, sm, shards=None):
    t = x.shape[0]
    wg = dict(wg)
    dist = shards is not None
    mix_g = [sm["norm_mix_g"][l:l + 1] for l in range(2)]
    mlp_g = [sm["norm_mlp_g"][l:l + 1] for l in range(2)]

    ids = iter(range(2, 2 + 9))

    def gather(names):
        if dist:
            got = _by_sequencer("gather_" + names[0], _gather_comm([shards[k] for k in names]),
                                SIBLING_AND_NEIGHBOURS, next(ids))
            wg.update(zip(names, got))

    def send(name, names):
        if dist:
            comm = _exchange_comm(grads=[g[k] for k in names])
            g.update(zip(names, _by_sequencer("exchange_" + name, comm, EVERYONE, next(ids))))

    def send_sums(name, names, meanwhile):
        if not dist:
            meanwhile()
            return ()
        grads = [g[k] for k in names]
        landed = _by_sequencer("pair_exchange_" + name, _pair_exchange_comm(grads), (1,), next(ids))
        sums = _pair_add("pair_add_" + name, grads, landed, after=meanwhile())
        g.update(zip(names, _by_sequencer("exchange_" + name, _chip_exchange_comm(sums), OTHER_CHIPS, next(ids))))
        return sums

    def a_args():
        return (wg["a_w_in"], wg["a_ln_v_g"], wg["a_ln_v_b"], sm["a_w_s"], sm["a_b_st"], wg["a_w_out"])

    def kvq_w():
        return (sm["kv_src_norm_g"], wg["kv_w_a"], sm["kv_a_norm_g"], wg["kv_w_b"], mix_g[1], wg["b_w_q_a"],
                sm["b_q_norm_g"], wg["b_w_q_b"])

    gather(("mlp_w1_0", "mlp_w2_0"))
    h1, z, gated = _a_mix_fwd(x, mix_g[0], *a_args())
    gather(("kv_w_a", "kv_w_b", "b_w_q_a", "b_w_q_b", "b_w_o"))
    h2, a0 = _mlp_fwd(h1, mlp_g[0], wg["mlp_w1_0"], wg["mlp_w2_0"])
    if dist:
        wg["b_w_q_a"] = wg["b_w_q_a"].reshape(D_MODEL, Q_LORA)
        wg["kv_w_a"] = wg["kv_w_a"].reshape(D_MODEL, KV_LORA + QK_ROPE)
    gather(("mlp_w1_1", "mlp_w2_1"))
    ckv, k, v, cqpre, q, cos, sin = _kvq_fwd(h2, pos, inv_freq, kvq_w())
    h3, att = _attn_fwd(h2, q, k, v, wg["b_w_o"])
    a1, loss, dh4, d_final_g = _mlp_fwd_loss(h3, mlp_g[1], wg["mlp_w1_1"], wg["mlp_w2_1"], sm["final_norm_g"], target)

    g = {}
    dh3, d_mlp_g1, hn, f, da, dh3_b = _mlp_bwd(h3, a1, dh4, mlp_g[1], wg["mlp_w1_1"], wg["mlp_w2_1"], 1)
    dq, dk, dv = _attn_bwd(dh3_b, q, k, v, wg["b_w_o"], cos, sin)
    g["mlp_w1_1"] = _wgrad("wgrad_w1_1", hn, da, _full(t, D_MODEL), _cols(t, FF_SLOT), D_MODEL, FF_SLOT, after=[dq])
    g["mlp_w2_1"] = _wgrad("wgrad_w2_1", f, dh4, _cols(t, FF_SLOT), _full(t, D_MODEL), FF_SLOT, D_MODEL)

    def wgrad_w_o():
        g["b_w_o"] = _wgrad("wgrad_w_o", att, dh3_b, _head(t, V_HEAD), _full(t, D_MODEL), V_HEAD, D_MODEL)
        return [g["b_w_o"]]

    sums = send_sums("mlp_1", ("mlp_w1_1", "mlp_w2_1"), wgrad_w_o)
    dh2, d_mix_g1, d_src_g, d_q_g, d_kv_a_g, g_q_a, g["b_w_q_b"], g_kv_a, g["kv_w_b"] = _kvq_bwd(
        h2, dh3, ckv, cqpre, dq, dk, dv, cos, sin, kvq_w(), after=sums)
    g["b_w_q_a"] = g_q_a.reshape(N_DEV, D_MODEL // N_DEV, Q_LORA)
    g["kv_w_a"] = g_kv_a.reshape(N_DEV, D_MODEL // N_DEV, KV_LORA + QK_ROPE)
    qkv = ("b_w_q_a", "b_w_q_b", "kv_w_a", "kv_w_b")
    landed = [g[k] for k in qkv]
    send("qkv", qkv)
    dh1, d_mlp_g0, hn, f, da, dh1_b = _mlp_bwd(h1, a0, dh2, mlp_g[0], wg["mlp_w1_0"], wg["mlp_w2_0"], 0,
                                               after=landed if dist else ())
    landed = [g["mlp_w1_1"], g["mlp_w2_1"]] if dist else ()
    g["mlp_w1_0"] = _wgrad("wgrad_w1_0", hn, da, _full(t, D_MODEL), _cols(t, FF_SLOT), D_MODEL, FF_SLOT, after=landed)
    g["mlp_w2_0"] = _wgrad("wgrad_w2_0", f, dh2, _cols(t, FF_SLOT), _full(t, D_MODEL), FF_SLOT, D_MODEL)

    def wgrad_a_w_out():
        g["a_w_out"] = _wgrad("wgrad_a_w_out", gated, dh1_b, _cols(t, GATE_DIM // N_DEV), _full(t, D_MODEL),
                              GATE_DIM // N_DEV, D_MODEL)
        return [g["a_w_out"]] + [g[k] for k in qkv]

    sums = send_sums("mlp_0", ("mlp_w1_0", "mlp_w2_0", "b_w_o"), wgrad_a_w_out)
    dx, hn, dz, d_mix_g0, d_ln_g, d_ln_b, d_ws, d_bs = _a_mix_bwd(x, z, dh1, mix_g[0], *a_args(), after=sums)
    small = {
        "norm_mix_g": jnp.concatenate([d_mix_g0, d_mix_g1], axis=0),
        "norm_mlp_g": jnp.concatenate([d_mlp_g0, d_mlp_g1], axis=0),
        "a_ln_v_g": d_ln_g.reshape(N_DEV, GATE_DIM // N_DEV),
        "a_ln_v_b": d_ln_b.reshape(N_DEV, GATE_DIM // N_DEV),
        "a_w_s": d_ws.astype(BF16) if dist else d_ws,
        "a_b_s": d_bs.reshape(A_GROUPS, GMLP_BLOCK),
        "b_q_norm_g": d_q_g,
        "kv_src_norm_g": d_src_g,
        "kv_a_norm_g": d_kv_a_g,
        "final_norm_g": d_final_g,
    }
    if dist:
        parts = [small[k].reshape((1,) + small[k].shape) for k in SMALL] + [loss.reshape(1, 1, 1)]
        got = _by_sequencer("gather_small", _exchange_comm(parts=parts), EVERYONE, next(ids))
        small, loss = dict(zip(SMALL, got)), got[-1]
    g["a_w_in"] = _wgrad("wgrad_a_w_in", hn, dz, _full(t, D_MODEL), _cols(t, FF_SLOT), D_MODEL, FF_SLOT)
    return loss, dx, g, small


def _adamw(w, g, m, v):
    m = ADAM_B1 * m + (1.0 - ADAM_B1) * g
    v = ADAM_B2 * v + (1.0 - ADAM_B2) * (g * g)
    m_hat = m / (1.0 - ADAM_B1 ** ADAM_STEP)
    v_hat = v / (1.0 - ADAM_B2 ** ADAM_STEP)
    return -ADAM_LR * (m_hat / (jnp.sqrt(v_hat) + ADAM_EPS) + ADAM_WD * w), m, v


def _sum_in_device_order(r_ref):
    g = r_ref[0].astype(F32)
    for j in range(1, r_ref.shape[0]):
        g = g + r_ref[j].astype(F32)
    return g


def _adamw_sharded(name, recvs, w, m, v):
    layers, r, c = w.shape
    tr = math.gcd(r, 512)
    flat = [a for per_layer in recvs for a in per_layer]

    def body(*refs):
        r_refs, (w_ref, m_ref, v_ref) = refs[:len(flat)], refs[len(flat):len(flat) + 3]
        g_ref, d_ref, nm_ref, nv_ref = refs[-4:]
        layer = pl.program_id(0)
        g, pos = None, 0
        for li, per_layer in enumerate(recvs):
            total = None
            for ref in r_refs[pos:pos + len(per_layer)]:
                part = _sum_in_device_order(ref)
                total = part if total is None else total + part
            pos += len(per_layer)
            g = total if g is None else jnp.where(layer == li, total, g)
        g_ref[...] = g
        d_ref[...], nm_ref[...], nv_ref[...] = _adamw(w_ref[...], g, m_ref[...], v_ref[...])

    blk = pl.BlockSpec((None, tr, c), lambda l, i: (l, i, 0))
    return _call(name, body, (layers, r // tr),
                 [pl.BlockSpec((a.shape[0], tr, c), lambda l, i: (0, i, 0)) for a in flat] + [blk] * 3,
                 [blk] * 4, [_sds(w.shape, F32)] * 4, (*flat, w, m, v))


def _adamw_small(recvs, ws, ms, vs, own_row, losses):
    n = len(recvs)

    def body(*refs):
        r_refs, w_refs, m_refs, v_refs = (refs[i * n:(i + 1) * n] for i in range(4))
        outs, scr = refs[4 * n + 1:8 * n + 2], refs[8 * n + 2:]
        outs[-1][...] = _sum_in_device_order(refs[4 * n])
        me = _my_place()[3]
        for a in range(n):
            g = _sum_in_device_order(r_refs[a])
            if own_row[a]:
                scr[0][...] = g
                g = scr[0][pl.ds(me, 1), :]
            g_ref, d_ref, nm_ref, nv_ref = outs[4 * a:4 * a + 4]
            g_ref[...] = g
            d_ref[...], nm_ref[...], nv_ref[...] = _adamw(w_refs[a][...], g, m_refs[a][...], v_refs[a][...])

    out_shape = []
    for w in ws:
        out_shape += [_sds(w.shape, F32)] * 4
    return pl.pallas_call(
        body, name="adamw_small", in_specs=[VMEM] * (4 * n + 1), out_specs=[VMEM] * (4 * n + 1),
        out_shape=out_shape + [_sds((1, 1), F32)], scratch_shapes=[pltpu.VMEM((N_DEV, GATE_DIM // N_DEV), F32)],
    )(*recvs, *ws, *ms, *vs, losses)


BIG = ("a_w_in", "a_w_out", "b_w_q_a", "b_w_q_b", "b_w_o", "kv_w_a", "kv_w_b", "mlp_w1", "mlp_w2")
SMALL = ("norm_mix_g", "norm_mlp_g", "a_ln_v_g", "a_ln_v_b", "a_w_s", "a_b_s", "b_q_norm_g", "kv_src_norm_g",
         "kv_a_norm_g", "final_norm_g")
WEIGHTS = ("norm_mix_g", "norm_mlp_g", "a_w_in", "a_ln_v_g", "a_ln_v_b", "a_w_s", "a_b_s", "a_w_out", "b_w_q_a",
           "b_q_norm_g", "b_w_q_b", "b_w_o", "kv_src_norm_g", "kv_w_a", "kv_a_norm_g", "kv_w_b", "mlp_w1", "mlp_w2",
           "final_norm_g")


def _two_d(name, a):
    if name in ("a_w_s", "a_b_s"):
        return a.reshape(a.shape[1:])
    return a.reshape(1, -1) if a.ndim == 1 else a


def _three_d(a):
    return a if a.ndim == 3 else a.reshape((1,) + a.shape)


def kernel(x, positions, norm_mix_g, norm_mlp_g, a_w_in, a_ln_v_g, a_ln_v_b, a_w_s, a_b_s, a_w_out, b_w_q_a, b_q_norm_g, b_w_q_b, b_w_o, kv_src_norm_g, kv_w_a, kv_a_norm_g, kv_w_b, mlp_w1, mlp_w2, final_norm_g, loss_target, m_norm_mix_g, m_norm_mlp_g, m_a_w_in, m_a_ln_v_g, m_a_ln_v_b, m_a_w_s, m_a_b_s, m_a_w_out, m_b_w_q_a, m_b_q_norm_g, m_b_w_q_b, m_b_w_o, m_kv_src_norm_g, m_kv_w_a, m_kv_a_norm_g, m_kv_w_b, m_mlp_w1, m_mlp_w2, m_final_norm_g, v_norm_mix_g, v_norm_mlp_g, v_a_w_in, v_a_ln_v_g, v_a_ln_v_b, v_a_w_s, v_a_b_s, v_a_w_out, v_b_w_q_a, v_b_q_norm_g, v_b_w_q_b, v_b_w_o, v_kv_src_norm_g, v_kv_w_a, v_kv_a_norm_g, v_kv_w_b, v_mlp_w1, v_mlp_w2, v_final_norm_g):
    w = dict(norm_mix_g=norm_mix_g, norm_mlp_g=norm_mlp_g, a_w_in=a_w_in, a_ln_v_g=a_ln_v_g, a_ln_v_b=a_ln_v_b,
             a_w_s=a_w_s, a_b_s=a_b_s, a_w_out=a_w_out, b_w_q_a=b_w_q_a, b_q_norm_g=b_q_norm_g, b_w_q_b=b_w_q_b,
             b_w_o=b_w_o, kv_src_norm_g=kv_src_norm_g, kv_w_a=kv_w_a, kv_a_norm_g=kv_a_norm_g, kv_w_b=kv_w_b,
             mlp_w1=mlp_w1, mlp_w2=mlp_w2, final_norm_g=final_norm_g)
    m = dict(norm_mix_g=m_norm_mix_g, norm_mlp_g=m_norm_mlp_g, a_w_in=m_a_w_in, a_ln_v_g=m_a_ln_v_g,
             a_ln_v_b=m_a_ln_v_b, a_w_s=m_a_w_s, a_b_s=m_a_b_s, a_w_out=m_a_w_out, b_w_q_a=m_b_w_q_a,
             b_q_norm_g=m_b_q_norm_g, b_w_q_b=m_b_w_q_b, b_w_o=m_b_w_o, kv_src_norm_g=m_kv_src_norm_g,
             kv_w_a=m_kv_w_a, kv_a_norm_g=m_kv_a_norm_g, kv_w_b=m_kv_w_b, mlp_w1=m_mlp_w1, mlp_w2=m_mlp_w2,
             final_norm_g=m_final_norm_g)
    v = dict(norm_mix_g=v_norm_mix_g, norm_mlp_g=v_norm_mlp_g, a_w_in=v_a_w_in, a_ln_v_g=v_a_ln_v_g,
             a_ln_v_b=v_a_ln_v_b, a_w_s=v_a_w_s, a_b_s=v_a_b_s, a_w_out=v_a_w_out, b_w_q_a=v_b_w_q_a,
             b_q_norm_g=v_b_q_norm_g, b_w_q_b=v_b_w_q_b, b_w_o=v_b_w_o, kv_src_norm_g=v_kv_src_norm_g,
             kv_w_a=v_kv_w_a, kv_a_norm_g=v_kv_a_norm_g, kv_w_b=v_kv_w_b, mlp_w1=v_mlp_w1, mlp_w2=v_mlp_w2,
             final_norm_g=v_final_norm_g)
    t = x.shape[1]

    first = ("a_w_in", "a_w_out", "a_ln_v_g", "a_ln_v_b")
    later = ("mlp_w1", "mlp_w2", "kv_w_a", "kv_w_b", "b_w_q_a", "b_w_q_b", "b_w_o")
    later_blocks = ("mlp_w1_0", "mlp_w1_1", "mlp_w2_0", "mlp_w2_1") + later[2:]
    got, casts = _gather_first([_three_d(w[k]) if k in BIG else w[k] for k in first], [_three_d(w[k]) for k in later])
    wg = dict(zip(first, got))
    wg["a_w_out"] = wg["a_w_out"].reshape(GATE_DIM, D_MODEL)
    wg["a_ln_v_g"] = wg["a_ln_v_g"].reshape(1, GATE_DIM)
    wg["a_ln_v_b"] = wg["a_ln_v_b"].reshape(1, GATE_DIM)
    shards = dict(zip(later_blocks, casts))

    sm = {k: _two_d(k, w[k]) for k in SMALL if k not in ("a_ln_v_g", "a_ln_v_b")}
    sm["a_b_st"] = sm["a_b_s"].T
    inv_freq = (ROPE_THETA ** (-jnp.arange(0, QK_ROPE, 2, dtype=F32) / QK_ROPE)).reshape(1, QK_ROPE // 2)

    losses, dx, g, small = _local_step(x[0], positions.reshape(t, 1), loss_target[0], inv_freq, wg, sm, shards)

    names = ("a_w_in", "a_w_out")
    sums = _pair_reduce("pair_reduce_a", [g[k] for k in names], after=[g["mlp_w1_0"], g["mlp_w2_0"]])
    g.update(zip(names, _by_sequencer("exchange_last", _chip_exchange_comm(sums), OTHER_CHIPS, collective_id=1)))

    out = {}
    for k in BIG:
        recvs = [[g[k + "_0"]], [g[k + "_1"]]] if k.startswith("mlp") else [[g[k]]]
        res = _adamw_sharded("adamw_" + k, recvs, _three_d(w[k]), _three_d(m[k]), _three_d(v[k]))
        out[k] = [o.reshape(w[k].shape) for o in res]
    own_row = [k in ("a_ln_v_g", "a_ln_v_b") for k in SMALL]
    res = _adamw_small([small[k] for k in SMALL], [_two_d(k, w[k]) for k in SMALL], [_two_d(k, m[k]) for k in SMALL],
                       [_two_d(k, v[k]) for k in SMALL], own_row, losses)
    for i, k in enumerate(SMALL):
        out[k] = [o.reshape(w[k].shape) for o in res[4 * i:4 * i + 4]]

    return (res[-1].reshape(()), dx.reshape(x.shape), *[out[k][0] for k in WEIGHTS], *[out[k][1] for k in WEIGHTS],
            *[out[k][2] for k in WEIGHTS], *[out[k][3] for k in WEIGHTS])
```

```python
import math

import jax
import jax.numpy as jnp
from jax import lax
from jax.experimental import pallas as pl
from jax.experimental.pallas import tpu as pltpu
from jax.experimental.pallas import tpu_sc as plsc

F32, BF16 = jnp.float32, jnp.bfloat16
MESH = pl.DeviceIdType.MESH
ANY = pl.BlockSpec(memory_space=pl.ANY)
VMEM = pl.BlockSpec(memory_space=pltpu.VMEM)

N_DEV = 8
D_MODEL = 1024
CHUNK = 64
GMLP_BLOCK = 128
GATE_DIM = 2048
A_GROUPS = 8
A_GROUP_DIM = GATE_DIM // A_GROUPS
B_HEADS = 8
QK_NOPE, QK_ROPE, V_HEAD = 128, 64, 128
Q_LORA, KV_LORA = 384, 256
ROPE_THETA = 10000.0
D_FF = 4096
FF_SLOT = D_FF // N_DEV
EPS = 1e-6
ATT_SCALE = (QK_NOPE + QK_ROPE) ** -0.5

ADAM_LR, ADAM_B1, ADAM_B2, ADAM_EPS, ADAM_WD, ADAM_STEP = 0.001, 0.9, 0.999, 1e-08, 0.01, 10

TM = 256
TM_GATE = 256
VMEM_LIMIT = 56 * 1024 * 1024
INV_SQRT2 = 1.0 / math.sqrt(2.0)
INV_SQRT_2PI = 1.0 / math.sqrt(2.0 * math.pi)
LOG2_E = 1.0 / math.log(2.0)
HEADS_PER_STEP = 2


def _dot(a, b):
    return jnp.dot(a, b, preferred_element_type=F32)


def _dot_nt(a, b):
    return lax.dot_general(a, b, (((1,), (1,)), ((), ())), preferred_element_type=F32)


def _dot_tn(a, b):
    return lax.dot_general(a, b, (((0,), (0,)), ((), ())), preferred_element_type=F32)


def _rms_fwd(x, g):
    rstd = lax.rsqrt(jnp.mean(x * x, axis=-1, keepdims=True) + EPS)
    xhat = x * rstd
    return xhat * g, xhat, rstd


def _rms_bwd(dy, xhat, rstd, g):
    dxhat = dy * g
    dx = rstd * (dxhat - xhat * jnp.mean(dxhat * xhat, axis=-1, keepdims=True))
    return dx, jnp.sum(dy * xhat, axis=0, keepdims=True)


def _ln_fwd(v, g, b):
    mu = jnp.mean(v, axis=-1, keepdims=True)
    vc = v - mu
    rstd = lax.rsqrt(jnp.mean(vc * vc, axis=-1, keepdims=True) + EPS)
    vhat = vc * rstd
    return vhat * g + b, vhat, rstd


def _gelu(x):
    return 0.5 * x * (1.0 + lax.erf(x * INV_SQRT2))


def _gelu_and_grad(x):
    cdf = 0.5 * (1.0 + lax.erf(x * INV_SQRT2))
    return x * cdf, cdf + x * jnp.exp(-0.5 * x * x) * INV_SQRT_2PI


def _rope(x, cos, sin):
    x1, x2 = x[:, :QK_ROPE // 2], x[:, QK_ROPE // 2:]
    return jnp.concatenate([x1 * cos - x2 * sin, x2 * cos + x1 * sin], axis=-1)


def _gate_mask():
    row = lax.broadcasted_iota(jnp.int32, (GMLP_BLOCK, GMLP_BLOCK), 0)
    col = lax.broadcasted_iota(jnp.int32, (GMLP_BLOCK, GMLP_BLOCK), 1)
    return (col < CHUNK) | (row >= CHUNK)


def _att_mask(q0, tq, t):
    q = q0 + lax.broadcasted_iota(jnp.int32, (tq, t), 0)
    k = lax.broadcasted_iota(jnp.int32, (tq, t), 1)
    return jnp.right_shift(k, 6) <= jnp.right_shift(q, 6)


def _res(shape, imap=None):
    zeros = (0,) * len(shape)
    return pl.BlockSpec(shape, imap or (lambda i: zeros), pipeline_mode=pl.Buffered(1))


def _const(shape):
    zeros = (0,) * len(shape)
    return pl.BlockSpec(shape, lambda i: zeros)


def _row(d, tm=TM):
    return pl.BlockSpec((tm, d), lambda i: (i, 0))


def _heads(d):
    return pl.BlockSpec((B_HEADS, TM, d), lambda i: (0, i, 0))


def _sds(shape, dt):
    return jax.ShapeDtypeStruct(shape, dt)


def _acc(ref, val):
    @pl.when(pl.program_id(0) == 0)
    def _():
        ref[...] = jnp.zeros_like(ref)
    ref[...] += val


def _my_place():
    x, y, c = lax.axis_index("x"), lax.axis_index("y"), lax.axis_index("c")
    return x, y, c, 4 * x + 2 * y + c


def _peer(x, y, c, k):
    px = 1 - x if k & 4 else x
    py = 1 - y if k & 2 else y
    pc = 1 - c if k & 1 else c
    return (px, py, pc), 4 * px + 2 * py + pc


CHIPS = (2, 4, 6)


def _splits(ref):
    return len(ref.shape) >= 3 and ref.shape[1] % 32 == 0


def _piece(ref, block, half=None):
    if half is None or not _splits(ref):
        return ref.at[pl.ds(block, 1)]
    rows = ref.shape[1] // 2
    return ref.at[pl.ds(block, 1), pl.ds(half * rows, rows)]


def _gather_copy(sems, a, k, piece, to, src=None):
    return pltpu.make_async_remote_copy(
        src_ref=piece if src is None else src, dst_ref=piece, send_sem=sems[0].at[a, k], recv_sem=sems[1].at[a, k],
        device_id=to, device_id_type=MESH)


def _gather_start(srcs, outs, sems, only=None):
    x, y, c, me = _my_place()
    for a in range(len(srcs)) if only is None else (only,):
        mine = _piece(outs[a], me)
        pltpu.make_async_copy(srcs[a], mine, sems[2].at[a]).start()
        for k, rel in enumerate((1, 4, 2)):
            _gather_copy(sems, a, k, mine, _peer(x, y, c, rel)[0], src=srcs[a]).start()


def _gather_relay(srcs, outs, sems):
    x, y, c, _ = _my_place()
    sib = _peer(x, y, c, 1)[0]
    (xn, xn_i), (yn, yn_i) = _peer(x, y, c, 4), _peer(x, y, c, 2)
    for a in range(len(srcs)):
        out = outs[a]
        _gather_copy(sems, a, 1, _piece(out, xn_i), xn).wait_recv()
        _gather_copy(sems, a, 3, _piece(out, xn_i, 0), yn).start()
        _gather_copy(sems, a, 5, _piece(out, xn_i), sib).start()
        _gather_copy(sems, a, 2, _piece(out, yn_i), yn).wait_recv()
        if _splits(out):
            _gather_copy(sems, a, 4, _piece(out, yn_i, 1), xn).start()
        _gather_copy(sems, a, 6, _piece(out, yn_i), sib).start()


def _gather_finish(srcs, outs, sems):
    x, y, c, me = _my_place()
    sib = _peer(x, y, c, 1)[0]
    xn, yn, dg_i = _peer(x, y, c, 4)[0], _peer(x, y, c, 2)[0], _peer(x, y, c, 6)[1]
    n = len(srcs)
    for a in range(n):
        out = outs[a]
        _gather_copy(sems, a, 3, _piece(out, dg_i, 0), yn).wait_recv()
        _gather_copy(sems, a, 7, _piece(out, dg_i, 0), sib).start()
        if _splits(out):
            _gather_copy(sems, a, 4, _piece(out, dg_i, 1), xn).wait_recv()
            _gather_copy(sems, a, 8, _piece(out, dg_i, 1), sib).start()
    for a in range(n):
        out = outs[a]
        whole, half = _piece(out, me), _piece(out, me, 0)
        for k in (0, 5, 6):
            _gather_copy(sems, a, k, whole, sib).wait_recv()
        for k in (7, 8) if _splits(out) else (7,):
            _gather_copy(sems, a, k, half, sib).wait_recv()
        for k in (0, 1, 2):
            _gather_copy(sems, a, k, whole, sib, src=srcs[a]).wait_send()
        for k in (5, 6):
            _gather_copy(sems, a, k, whole, sib).wait_send()
        for k in (3, 4, 7, 8) if _splits(out) else (3, 7):
            _gather_copy(sems, a, k, half, sib).wait_send()
        pltpu.make_async_copy(srcs[a], whole, sems[2].at[a]).wait()


def _relay_sems(n):
    return [pltpu.SemaphoreType.DMA((n, 9)), pltpu.SemaphoreType.DMA((n, 9)), pltpu.SemaphoreType.DMA((n,))]


def _gather_sems(n):
    return [pltpu.SemaphoreType.DMA((n, 7)), pltpu.SemaphoreType.DMA((n, 7)), pltpu.SemaphoreType.DMA((n,))]


class _Comm:
    def __init__(self, args, out_shape, scratch, start, finish, relay=None):
        self.args, self.out_shape, self.scratch, self.start, self.finish = args, out_shape, scratch, start, finish
        self.relay = relay


def _gather_comm(shards):
    return _Comm(list(shards), [_sds((N_DEV,) + s.shape[1:], s.dtype) for s in shards], _relay_sems(len(shards)),
                 _gather_start, _gather_finish, relay=_gather_relay)


def _direct_copies(ins, outs, sems, wait, from_block):
    send_sems, recv_sems, local_sems = sems
    x, y, c, me = _my_place()
    for a in range(len(ins)):
        src = ins[a].at[pl.ds(me, 1)] if from_block[a] else ins[a]
        local = pltpu.make_async_copy(src, outs[a].at[pl.ds(me, 1)], local_sems.at[a])
        local.wait() if wait else local.start()
        for k in range(1, N_DEV):
            to, to_i = _peer(x, y, c, k)
            cp = pltpu.make_async_remote_copy(
                src_ref=ins[a].at[pl.ds(to_i, 1)] if from_block[a] else ins[a], dst_ref=outs[a].at[pl.ds(me, 1)],
                send_sem=send_sems.at[a, k - 1], recv_sem=recv_sems.at[a, k - 1], device_id=to, device_id_type=MESH)
            cp.wait() if wait else cp.start()


def _exchange_comm(grads=(), parts=()):
    ins = list(grads) + list(parts)
    from_block = [True] * len(grads) + [False] * len(parts)
    out_shape = [_sds(g.shape, g.dtype) for g in grads] + [_sds((N_DEV,) + p.shape[1:], p.dtype) for p in parts]

    def start(ins_, outs_, sems_):
        _direct_copies(ins_, outs_, sems_, False, from_block)

    def finish(ins_, outs_, sems_):
        _direct_copies(ins_, outs_, sems_, True, from_block)

    return _Comm(ins, out_shape, _gather_sems(len(ins)), start, finish)


def _chip_copies(ins, outs, sems, wait, rels, own):
    send_sems, recv_sems, local_sems = sems
    x, y, c, _ = _my_place()
    for a in range(len(ins)):
        if own:
            local = pltpu.make_async_copy(ins[a].at[pl.ds(2 * x + y, 1)], outs[a].at[pl.ds(len(rels), 1)],
                                          local_sems.at[a])
            local.wait() if wait else local.start()
        for i, j in enumerate(rels):
            to = _peer(x, y, c, CHIPS[j])[0]
            cp = pltpu.make_async_remote_copy(
                src_ref=ins[a].at[pl.ds(2 * to[0] + to[1], 1)], dst_ref=outs[a].at[pl.ds(i, 1)],
                send_sem=send_sems.at[a, i], recv_sem=recv_sems.at[a, i], device_id=to, device_id_type=MESH)
            cp.wait() if wait else cp.start()


def _chip_exchange_comm(sums, rels=(0, 1, 2), own=True):
    def start(ins_, outs_, sems_):
        _chip_copies(ins_, outs_, sems_, False, rels, own)

    def finish(ins_, outs_, sems_):
        _chip_copies(ins_, outs_, sems_, True, rels, own)

    n = len(sums)
    sems = [pltpu.SemaphoreType.DMA((n, len(rels))), pltpu.SemaphoreType.DMA((n, len(rels))),
            pltpu.SemaphoreType.DMA((n,))]
    return _Comm(list(sums), [_sds((len(rels) + own,) + s.shape[1:], s.dtype) for s in sums], sems, start, finish)


def _pair_reduce(name, grads, after=()):
    n = len(grads)
    n_chips = N_DEV // 2

    def body(*refs):
        g_refs, gh_refs, refs = refs[:n], refs[n:2 * n], refs[2 * n + len(after):]
        p_refs, land = refs[:n], refs[n:2 * n]
        send_sems, recv_sems = refs[2 * n:]
        x, y, c, _ = _my_place()
        sib = _peer(x, y, c, 1)[0]
        q = pl.program_id(0)

        def to_sibling(a, j):
            return pltpu.make_async_remote_copy(
                src_ref=gh_refs[a].at[j, pl.ds(1 - c, 1)], dst_ref=land[a].at[pl.ds(j, 1)],
                send_sem=send_sems.at[a, j], recv_sem=recv_sems.at[a, j], device_id=sib, device_id_type=MESH)

        @pl.when(q == 0)
        def _():
            for j in range(n_chips):
                for a in range(n):
                    to_sibling(a, j).start()

        for a in range(n):
            to_sibling(a, q).wait_recv()
            p_refs[a][...] = (g_refs[a][0, pl.ds(c, 1)].astype(F32) + land[a][pl.ds(q, 1)].astype(F32)).astype(BF16)

        @pl.when(q == n_chips - 1)
        def _():
            for a in range(n):
                for j in range(n_chips):
                    to_sibling(a, j).wait_send()

    views = [g.reshape((n_chips, 2) + g.shape[1:]) for g in grads]
    res = pl.pallas_call(
        body, name=name, grid=(n_chips,),
        in_specs=[pl.BlockSpec((1, 2) + g.shape[1:], lambda q: (q, 0, 0, 0)) for g in grads]
        + [ANY] * (n + len(after)),
        out_specs=[pl.BlockSpec((1,) + g.shape[1:], lambda q: (q, 0, 0)) for g in grads],
        out_shape=[_sds((n_chips,) + g.shape[1:], BF16) for g in grads],
        scratch_shapes=[pltpu.VMEM((n_chips,) + g.shape[1:], BF16) for g in grads]
        + [pltpu.SemaphoreType.DMA((n, n_chips)), pltpu.SemaphoreType.DMA((n, n_chips))],
        compiler_params=pltpu.CompilerParams(dimension_semantics=("arbitrary",), vmem_limit_bytes=VMEM_LIMIT),
    )(*views, *views, *after)
    return list(res)


def _pair_exchange_comm(grads):
    n, n_chips = len(grads), N_DEV // 2

    def copies(ins, outs, sems, wait):
        x, y, c, _ = _my_place()
        for j in range(n_chips):
            for a in range(n):
                cp = pltpu.make_async_remote_copy(
                    src_ref=ins[a].at[j, pl.ds(1 - c, 1)], dst_ref=outs[a].at[pl.ds(j, 1)], send_sem=sems[0].at[a, j],
                    recv_sem=sems[1].at[a, j], device_id=_peer(x, y, c, 1)[0], device_id_type=MESH)
                cp.wait() if wait else cp.start()

    views = [g.reshape((n_chips, 2) + g.shape[1:]) for g in grads]
    sems = [pltpu.SemaphoreType.DMA((n, n_chips)), pltpu.SemaphoreType.DMA((n, n_chips))]
    return _Comm(views, [_sds((n_chips,) + g.shape[1:], g.dtype) for g in grads], sems,
                 lambda i, o, s: copies(i, o, s, False), lambda i, o, s: copies(i, o, s, True))


def _pair_add(name, grads, landed, after=()):
    n, n_chips = len(grads), N_DEV // 2

    def body(core_ref, *refs):
        g_refs, l_refs, p_refs = refs[:n], refs[n:2 * n], refs[2 * n + len(after):]
        for a in range(n):
            p_refs[a][...] = (g_refs[a][...].astype(F32) + l_refs[a][...].astype(F32)).astype(BF16)

    views = [g.reshape((n_chips, 2) + g.shape[1:]) for g in grads]
    blocks = [pl.BlockSpec((1,) + g.shape[1:], lambda q, core: (q, 0, 0)) for g in grads]
    mine = [pl.BlockSpec((1, None) + g.shape[1:], lambda q, core: (q, core[0], 0, 0)) for g in grads]
    return list(pl.pallas_call(
        body, name=name, out_shape=[_sds((n_chips,) + g.shape[1:], BF16) for g in grads],
        grid_spec=pltpu.PrefetchScalarGridSpec(num_scalar_prefetch=1, grid=(n_chips,),
                                               in_specs=mine + blocks + [ANY] * len(after), out_specs=blocks),
        compiler_params=pltpu.CompilerParams(dimension_semantics=("arbitrary",), vmem_limit_bytes=VMEM_LIMIT),
    )(lax.axis_index("c").reshape(1), *views, *landed, *after))


def _call(name, body, grid, in_specs, out_specs, out_shape, args, scratch=(), after=()):
    ni, na = len(in_specs), len(after)

    def ordered(*refs):
        body(*refs[:ni], *refs[ni + na:])

    return list(pl.pallas_call(
        ordered if after else body, name=name, grid=grid, in_specs=list(in_specs) + [ANY] * na,
        out_specs=list(out_specs), out_shape=list(out_shape), scratch_shapes=list(scratch),
        compiler_params=pltpu.CompilerParams(dimension_semantics=("arbitrary",) * len(grid),
                                             vmem_limit_bytes=VMEM_LIMIT))(*args, *after))


SIBLING_AND_NEIGHBOURS, OTHER_CHIPS, EVERYONE = (1, 4, 2), CHIPS, tuple(range(1, N_DEV))


def _by_sequencer(name, comm, peers, collective_id):
    src = [jax.new_ref(a, memory_space=pltpu.MemorySpace.HBM) for a in comm.args]
    dst = [jax.empty_ref(s, memory_space=pltpu.MemorySpace.HBM) for s in comm.out_shape]

    @pl.kernel(mesh=plsc.ScalarSubcoreMesh(axis_name="sequencer", num_cores=1), name=name,
               scratch_types=tuple(comm.scratch), compiler_params=pltpu.CompilerParams(collective_id=collective_id))
    def launch(*sems):
        x, y, c, _ = _my_place()
        barrier = pltpu.get_barrier_semaphore()
        for k in peers:
            pl.semaphore_signal(barrier, inc=1, device_id=_peer(x, y, c, k)[0], device_id_type=MESH)
        pl.semaphore_wait(barrier, len(peers))
        comm.start(src, dst, sems)
        if comm.relay is not None:
            comm.relay(src, dst, sems)
        comm.finish(src, dst, sems)

    launch()
    return [d[...] for d in dst]


def _gather_first(first, later):
    nf = len(first)
    layer_of = [(a, l) for a, s in enumerate(later) for l in range(s.shape[0])]
    nl = len(layer_of)
    dts = [BF16] * (nf - 2) + [F32, F32]

    def body(*refs):
        ins, refs = refs[:nf + len(later)], refs[nf + len(later):]
        outs, refs = refs[:nf], refs[nf:]
        casts, refs = refs[:nl], refs[nl:]
        stage, sems = refs[:nf], refs[nf:]
        for a in range(nf):
            stage[a][...] = ins[a][...].astype(dts[a])
            _gather_start(stage, outs, sems, only=a)
        for k, (a, l) in enumerate(layer_of):
            casts[k][...] = ins[nf + a][l:l + 1].astype(BF16)
        _gather_relay(stage, outs, sems)
        _gather_finish(stage, outs, sems)

    res = pl.pallas_call(
        body, name="gather_first",
        in_specs=[VMEM] * (nf + len(later)), out_specs=[ANY] * nf + [VMEM] * nl,
        out_shape=[_sds((N_DEV,) + s.shape[1:], dt) for s, dt in zip(first, dts)]
        + [_sds((1,) + later[a].shape[1:], BF16) for a, _ in layer_of],
        scratch_shapes=[pltpu.VMEM(s.shape, dt) for s, dt in zip(first, dts)] + _relay_sems(nf),
        compiler_params=pltpu.CompilerParams(vmem_limit_bytes=VMEM_LIMIT),
    )(*first, *later)
    return list(res[:nf]), list(res[nf:])


def _a_mix_fwd(x, g, w_in, ln_g, ln_b, w_s, b_st, w_out):
    t = x.shape[0]
    nblk = TM // GMLP_BLOCK

    def body(x_ref, g_ref, win_ref, lng_ref, lnb_ref, ws_ref, bst_ref, wout_ref, h_ref, z_ref, gated_scr):
        xv = x_ref[...]
        hb = _rms_fwd(xv, g_ref[...])[0].astype(BF16)
        for d in range(N_DEV):
            z_ref[:, d * FF_SLOT:(d + 1) * FF_SLOT] = _dot(hb, win_ref[d])
        u = _gelu(z_ref[:, :GATE_DIM])
        vb = _ln_fwd(_gelu(z_ref[:, GATE_DIM:]), lng_ref[...], lnb_ref[...])[0].astype(BF16)
        mask = _gate_mask()
        for gi in range(A_GROUPS):
            wm = jnp.where(mask, ws_ref[gi], 0.0).astype(BF16)
            bias = bst_ref[:, gi:gi + 1]
            cs = slice(gi * A_GROUP_DIM, (gi + 1) * A_GROUP_DIM)
            for n in range(nblk):
                rs = slice(n * GMLP_BLOCK, (n + 1) * GMLP_BLOCK)
                sv = _dot(wm, vb[rs, cs]) + bias
                gated_scr[rs, cs] = (u[rs, cs] * sv).astype(BF16)
        h_ref[...] = xv + _dot(gated_scr[...], wout_ref[...])

    return _call(
        "a_mix_fwd", body, (t // TM,),
        [_row(D_MODEL), _res((1, D_MODEL)), _res((N_DEV, D_MODEL, FF_SLOT)), _res((1, GATE_DIM)),
         _res((1, GATE_DIM)), _res((A_GROUPS, GMLP_BLOCK, GMLP_BLOCK)), _res((GMLP_BLOCK, A_GROUPS)),
         _res((GATE_DIM, D_MODEL))],
        [_row(D_MODEL), _row(2 * GATE_DIM), _row(GATE_DIM)],
        [_sds((t, D_MODEL), F32), _sds((t, 2 * GATE_DIM), F32), _sds((t, GATE_DIM), BF16)],
        (x, g, w_in, ln_g, ln_b, w_s, b_st, w_out))


MLP_W_SPECS = (_res((N_DEV, D_MODEL, FF_SLOT)), _res((N_DEV, FF_SLOT, D_MODEL)))


def _mlp_fwd(h, g, w1, w2):
    t = h.shape[0]

    def body(h_ref, g_ref, w1_ref, w2_ref, o_ref, a_ref):
        hv = h_ref[...]
        hb = _rms_fwd(hv, g_ref[...])[0].astype(BF16)
        o_ref[...] = hv
        for d in range(N_DEV):
            a = _dot(hb, w1_ref[d])
            a_ref[:, d * FF_SLOT:(d + 1) * FF_SLOT] = a
            r = jnp.maximum(a, 0.0)
            o_ref[...] += _dot((r * r).astype(BF16), w2_ref[d])

    return _call(
        "mlp_fwd", body, (t // TM,), [_row(D_MODEL), _res((1, D_MODEL)), *MLP_W_SPECS],
        [_row(D_MODEL), _row(D_FF)], [_sds((t, D_MODEL), F32), _sds((t, D_FF), F32)], (h, g, w1, w2))


def _mlp_fwd_loss(h, g, w1, w2, final_g, target):
    t = h.shape[0]

    def body(h_ref, g_ref, w1_ref, w2_ref, fg_ref, t_ref, a_ref, loss_ref, dh_ref, dg_ref):
        hv = h_ref[...]
        hb = _rms_fwd(hv, g_ref[...])[0].astype(BF16)
        out = hv
        for d in range(N_DEV):
            a = _dot(hb, w1_ref[d])
            a_ref[:, d * FF_SLOT:(d + 1) * FF_SLOT] = a
            r = jnp.maximum(a, 0.0)
            out = out + _dot((r * r).astype(BF16), w2_ref[d])
        y, xhat, rstd = _rms_fwd(out, fg_ref[...])
        err = y - t_ref[...]
        part = 0.5 * jnp.sum(jnp.mean(err * err, axis=-1, keepdims=True), axis=0, keepdims=True)
        dx, dg = _rms_bwd(err * (1.0 / D_MODEL), xhat, rstd, fg_ref[...])
        dh_ref[...] = dx
        _acc(dg_ref, dg)
        _acc(loss_ref, part)

    return _call(
        "mlp_fwd_loss", body, (t // TM,),
        [_row(D_MODEL), _res((1, D_MODEL)), *MLP_W_SPECS, _res((1, D_MODEL)), _row(D_MODEL)],
        [_row(D_FF), _const((1, 1)), _row(D_MODEL), _const((1, D_MODEL))],
        [_sds((t, D_FF), F32), _sds((1, 1), F32), _sds((t, D_MODEL), F32), _sds((1, D_MODEL), F32)],
        (h, g, w1, w2, final_g, target))


KVQ_W_SPECS = (_res((1, D_MODEL)), _res((D_MODEL, KV_LORA + QK_ROPE)), _res((1, KV_LORA)),
               _res((B_HEADS, KV_LORA, QK_NOPE + V_HEAD)), _res((1, D_MODEL)), _res((D_MODEL, Q_LORA)),
               _res((1, Q_LORA)), _res((B_HEADS, Q_LORA, QK_NOPE + QK_ROPE)))


def _kvq_fwd(h, pos, inv_freq, kvq_w):
    t = h.shape[0]
    half = QK_ROPE // 2

    def body(h_ref, pos_ref, invf_ref, srcg_ref, wkva_ref, kvag_ref, wkvb_ref, mixg_ref, wqa_ref, qg_ref, wqb_ref,
             ckv_ref, k_ref, v_ref, cqpre_ref, q_ref, cos_ref, sin_ref):
        hv = h_ref[...]
        xhat = hv * lax.rsqrt(jnp.mean(hv * hv, axis=-1, keepdims=True) + EPS)
        ang = pos_ref[...].astype(F32) * invf_ref[...]
        cos, sin = jnp.cos(ang), jnp.sin(ang)
        cos_ref[...] = cos
        sin_ref[...] = sin
        ckv = _dot((xhat * srcg_ref[...]).astype(BF16), wkva_ref[...])
        ckv_ref[...] = ckv
        cb = _rms_fwd(ckv[:, :KV_LORA], kvag_ref[...])[0].astype(BF16)
        kpe = _rope(ckv[:, KV_LORA:], cos, sin).astype(BF16)
        for hd in range(B_HEADS):
            kv = _dot(cb, wkvb_ref[hd])
            k_ref[hd, :, 0:QK_NOPE] = kv[:, :QK_NOPE].astype(BF16)
            k_ref[hd, :, QK_NOPE:] = kpe
            v_ref[hd] = kv[:, QK_NOPE:].astype(BF16)
        cqpre = _dot((xhat * mixg_ref[...]).astype(BF16), wqa_ref[...])
        cqpre_ref[...] = cqpre
        cqb = _rms_fwd(cqpre, qg_ref[...])[0].astype(BF16)
        for hd in range(B_HEADS):
            q = _dot(cqb, wqb_ref[hd])
            q_ref[hd, :, 0:QK_NOPE] = q[:, :QK_NOPE].astype(BF16)
            q_ref[hd, :, QK_NOPE:] = _rope(q[:, QK_NOPE:], cos, sin).astype(BF16)

    return _call(
        "kvq_fwd", body, (t // TM,), [_row(D_MODEL), _row(1), _res((1, half)), *KVQ_W_SPECS],
        [_row(KV_LORA + QK_ROPE), _heads(QK_NOPE + QK_ROPE), _heads(V_HEAD), _row(Q_LORA),
         _heads(QK_NOPE + QK_ROPE), _row(half), _row(half)],
        [_sds((t, KV_LORA + QK_ROPE), F32), _sds((B_HEADS, t, QK_NOPE + QK_ROPE), BF16),
         _sds((B_HEADS, t, V_HEAD), BF16), _sds((t, Q_LORA), F32), _sds((B_HEADS, t, QK_NOPE + QK_ROPE), BF16),
         _sds((t, half), F32), _sds((t, half), F32)],
        (h, pos, inv_freq, *kvq_w))


def _softmax_rows(q, k_ref, k):
    past, upto = k * TM, (k + 1) * TM
    s = _dot_nt(q, k_ref[0:upto, :])
    own = jnp.where(_att_mask(0, TM, TM), s[:, past:], jnp.finfo(F32).min)
    s = own if k == 0 else jnp.concatenate([s[:, :past], own], axis=1)
    e = jnp.exp2((s - jnp.max(s, axis=-1, keepdims=True)) * (ATT_SCALE * LOG2_E))
    return e * (1.0 / jnp.sum(e, axis=-1, keepdims=True))


def _for_my_tile(i, nq, fn):
    for k in range(nq):
        @pl.when(i == k)
        def _(k=k):
            fn(k)


def _attn_fwd(h, q, k, v, w_o):
    t = h.shape[0]
    nq, hps = t // TM, HEADS_PER_STEP

    def body(h_ref, q_ref, k_ref, v_ref, wo_ref, o_ref, att_ref):
        i, pair = pl.program_id(0), pl.program_id(1)

        @pl.when(pair == 0)
        def _():
            o_ref[...] = h_ref[...]

        def tile(kt):
            proj = None
            for j in range(hps):
                hd = pair * hps + j
                p = _softmax_rows(q_ref[j], k_ref.at[hd], kt)
                ob = _dot(p.astype(BF16), v_ref[hd, 0:(kt + 1) * TM, :]).astype(BF16)
                att_ref[j] = ob
                proj = _dot(ob, wo_ref[hd]) if proj is None else proj + _dot(ob, wo_ref[hd])
            o_ref[...] += proj

        _for_my_tile(i, nq, tile)

    def per_head(d):
        return pl.BlockSpec((hps, TM, d), lambda i, pair: (pair, i, 0))

    def resident(shape):
        zeros = (0,) * len(shape)
        return pl.BlockSpec(shape, lambda i, pair: zeros, pipeline_mode=pl.Buffered(1))

    tile_spec = pl.BlockSpec((TM, D_MODEL), lambda i, pair: (i, 0))
    return _call(
        "attn_fwd", body, (nq, B_HEADS // hps),
        [tile_spec, per_head(QK_NOPE + QK_ROPE), resident((B_HEADS, t, QK_NOPE + QK_ROPE)),
         resident((B_HEADS, t, V_HEAD)), resident((B_HEADS, V_HEAD, D_MODEL))],
        [tile_spec, per_head(V_HEAD)], [_sds((t, D_MODEL), F32), _sds((B_HEADS, t, V_HEAD), BF16)],
        (h, q, k, v, w_o))


def _mlp_bwd(h, a, dho, g, w1, w2, layer, after=()):
    t = h.shape[0]

    def body(h_ref, a_ref, dho_ref, g_ref, w1_ref, w2_ref, dhi_ref, dg_ref, hn_ref, f_ref, da_ref, dhib_ref):
        gv = g_ref[...]
        y, xhat, rstd = _rms_fwd(h_ref[...], gv)
        hn_ref[...] = y.astype(BF16)
        dho_v = dho_ref[...]
        dhob = dho_v.astype(BF16)
        dhn = jnp.zeros((TM, D_MODEL), F32)
        for d in range(N_DEV):
            cs = slice(d * FF_SLOT, (d + 1) * FF_SLOT)
            r = jnp.maximum(a_ref[:, cs], 0.0)
            f_ref[:, cs] = (r * r).astype(BF16)
            da = (_dot_nt(dhob, w2_ref[d]) * (2.0 * r)).astype(BF16)
            da_ref[:, cs] = da
            dhn = dhn + _dot_nt(da, w1_ref[d])
        dx, dg = _rms_bwd(dhn, xhat, rstd, gv)
        dhi = dho_v + dx
        dhi_ref[...] = dhi
        dhib_ref[...] = dhi.astype(BF16)
        _acc(dg_ref, dg)

    return _call(
        f"mlp_bwd_{layer}", body, (t // TM,),
        [_row(D_MODEL), _row(D_FF), _row(D_MODEL), _res((1, D_MODEL)), *MLP_W_SPECS],
        [_row(D_MODEL), _const((1, D_MODEL)), _row(D_MODEL), _row(D_FF), _row(D_FF), _row(D_MODEL)],
        [_sds((t, D_MODEL), F32), _sds((1, D_MODEL), F32), _sds((t, D_MODEL), BF16), _sds((t, D_FF), BF16),
         _sds((t, D_FF), BF16), _sds((t, D_MODEL), BF16)],
        (h, a, dho, g, w1, w2), after=after)


def _attn_bwd(dh, q, k, v, w_o, cos, sin, after=()):
    t = dh.shape[0]
    half, hps = QK_ROPE // 2, HEADS_PER_STEP

    def body(dh_ref, q_ref, k_ref, v_ref, wo_ref, cos_ref, sin_ref, dq_ref, dk_ref, dv_ref):
        i = pl.program_id(1)

        @pl.when(i == 0)
        def _():
            dk_ref[...] = jnp.zeros_like(dk_ref)
            dv_ref[...] = jnp.zeros_like(dv_ref)

        def tile(kt):
            keys = slice(0, (kt + 1) * TM)
            for j in range(hps):
                qj = q_ref[j]
                do = _dot_nt(dh_ref[kt * TM:(kt + 1) * TM, :], wo_ref[j]).astype(BF16)
                p = _softmax_rows(qj, k_ref.at[j], kt)
                dp = _dot_nt(do, v_ref[j, keys, :])
                ds = (p * (dp - jnp.sum(p * dp, axis=-1, keepdims=True)) * ATT_SCALE).astype(BF16)
                dq = _dot(ds, k_ref[j, keys, :])
                dq_ref[j, :, 0:QK_NOPE] = dq[:, :QK_NOPE].astype(BF16)
                dq_ref[j, :, QK_NOPE:] = _rope(dq[:, QK_NOPE:], cos_ref[...], -sin_ref[...]).astype(BF16)
                dk_ref[j, keys, :] += _dot_tn(ds, qj)
                dv_ref[j, keys, :] += _dot_tn(p.astype(BF16), do)

        _for_my_tile(i, t // TM, tile)

    def per_pair(rows, d, tiled):
        return pl.BlockSpec((hps, rows, d), (lambda pair, i: (pair, i, 0)) if tiled else (lambda pair, i: (pair, 0, 0)))

    def tile(d):
        return pl.BlockSpec((TM, d), lambda pair, i: (i, 0))

    return _call(
        "attn_bwd", body, (B_HEADS // hps, t // TM),
        [pl.BlockSpec((t, D_MODEL), lambda pair, i: (0, 0), pipeline_mode=pl.Buffered(1)),
         per_pair(TM, QK_NOPE + QK_ROPE, True), per_pair(t, QK_NOPE + QK_ROPE, False), per_pair(t, V_HEAD, False),
         per_pair(V_HEAD, D_MODEL, False), tile(half), tile(half)],
        [per_pair(TM, QK_NOPE + QK_ROPE, True), per_pair(t, QK_NOPE + QK_ROPE, False), per_pair(t, V_HEAD, False)],
        [_sds((B_HEADS, t, QK_NOPE + QK_ROPE), BF16), _sds((B_HEADS, t, QK_NOPE + QK_ROPE), F32),
         _sds((B_HEADS, t, V_HEAD), F32)],
        (dh, q, k, v, w_o, cos, sin), after=after)


def _kvq_bwd(h, dh, ckv, cqpre, dq, dk, dv, cos, sin, kvq_w, after=()):
    t = h.shape[0]
    half, last = QK_ROPE // 2, t // TM - 1
    grad_shapes = [(D_MODEL, Q_LORA), (B_HEADS, Q_LORA, QK_NOPE + QK_ROPE), (D_MODEL, KV_LORA + QK_ROPE),
                   (B_HEADS, KV_LORA, QK_NOPE + V_HEAD)]

    def body(h_ref, dh_ref, ckv_ref, cqpre_ref, dq_ref, dk_ref, dv_ref, cos_ref, sin_ref,
             srcg_ref, wkva_ref, kvag_ref, wkvb_ref, mixg_ref, wqa_ref, qg_ref, wqb_ref,
             dhi_ref, dmixg_ref, dsrcg_ref, dqg_ref, dkvag_ref, gqa_ref, gqb_ref, gkva_ref, gkvb_ref,
             aqa, aqb, akva, akvb):
        @pl.when(pl.program_id(0) == 0)
        def _():
            for acc in (aqa, aqb, akva, akvb):
                acc[...] = jnp.zeros_like(acc)

        hv = h_ref[...]
        rstd = lax.rsqrt(jnp.mean(hv * hv, axis=-1, keepdims=True) + EPS)
        xhat = hv * rstd
        mixg, srcg, qg, kvag = mixg_ref[...], srcg_ref[...], qg_ref[...], kvag_ref[...]
        cq, cqhat, crstd = _rms_fwd(cqpre_ref[...], qg)
        cqb = cq.astype(BF16)
        dcq = jnp.zeros((TM, Q_LORA), F32)
        for hd in range(B_HEADS):
            dcq = dcq + _dot_nt(dq_ref[hd], wqb_ref[hd])
            aqb[hd] += _dot_tn(cqb, dq_ref[hd])
        dcqpre, dqg = _rms_bwd(dcq, cqhat, crstd, qg)
        dcqpre_b = dcqpre.astype(BF16)
        aqa[...] += _dot_tn((xhat * mixg).astype(BF16), dcqpre_b)
        dxq, dmixg = _rms_bwd(_dot_nt(dcqpre_b, wqa_ref[...]), xhat, rstd, mixg)
        ckv = ckv_ref[...]
        c, chat, krstd = _rms_fwd(ckv[:, :KV_LORA], kvag)
        cb = c.astype(BF16)
        dc = jnp.zeros((TM, KV_LORA), F32)
        dkpe = jnp.zeros((TM, QK_ROPE), F32)
        for hd in range(B_HEADS):
            dkv = jnp.concatenate([dk_ref[hd, :, 0:QK_NOPE], dv_ref[hd]], axis=-1).astype(BF16)
            akvb[hd] += _dot_tn(cb, dkv)
            dc = dc + _dot_nt(dkv, wkvb_ref[hd])
            dkpe = dkpe + dk_ref[hd, :, QK_NOPE:]
        dlat, dkvag = _rms_bwd(dc, chat, krstd, kvag)
        dpe = _rope(dkpe, cos_ref[...], -sin_ref[...])
        dckv_b = jnp.concatenate([dlat, dpe], axis=-1).astype(BF16)
        akva[...] += _dot_tn((xhat * srcg).astype(BF16), dckv_b)
        dxk, dsrcg = _rms_bwd(_dot_nt(dckv_b, wkva_ref[...]), xhat, rstd, srcg)
        dhi_ref[...] = dh_ref[...] + dxq + dxk
        _acc(dmixg_ref, dmixg)
        _acc(dsrcg_ref, dsrcg)
        _acc(dqg_ref, dqg)
        _acc(dkvag_ref, dkvag)

        @pl.when(pl.program_id(0) == last)
        def _():
            for out, acc in ((gqa_ref, aqa), (gqb_ref, aqb), (gkva_ref, akva), (gkvb_ref, akvb)):
                out[...] = acc[...].astype(BF16)

    return _call(
        "kvq_bwd", body, (t // TM,),
        [_row(D_MODEL), _row(D_MODEL), _row(KV_LORA + QK_ROPE), _row(Q_LORA), _heads(QK_NOPE + QK_ROPE),
         _heads(QK_NOPE + QK_ROPE), _heads(V_HEAD), _row(half), _row(half), *KVQ_W_SPECS],
        [_row(D_MODEL), _const((1, D_MODEL)), _const((1, D_MODEL)), _const((1, Q_LORA)), _const((1, KV_LORA))]
        + [_const(s) for s in grad_shapes],
        [_sds((t, D_MODEL), F32), _sds((1, D_MODEL), F32), _sds((1, D_MODEL), F32), _sds((1, Q_LORA), F32),
         _sds((1, KV_LORA), F32)] + [_sds(s, BF16) for s in grad_shapes],
        (h, dh, ckv, cqpre, dq, dk, dv, cos, sin, *kvq_w), scratch=[pltpu.VMEM(s, F32) for s in grad_shapes],
        after=after)


def _a_mix_bwd(x, z, dh, g, w_in, ln_g, ln_b, w_s, b_st, w_out, after=()):
    t = x.shape[0]
    tm = TM_GATE
    nblk = tm // GMLP_BLOCK

    def body(x_ref, z_ref, dh_ref, g_ref, win_ref, lng_ref, lnb_ref, ws_ref, bst_ref, wout_ref,
             dx_ref, hn_ref, dz_ref, dg_ref, dlng_ref, dlnb_ref, dws_ref, dbs_ref, dvn_scr, gelu_grad_v):
        @pl.when(pl.program_id(0) == 0)
        def _():
            dws_ref[...] = jnp.zeros_like(dws_ref)
            dbs_ref[...] = jnp.zeros_like(dbs_ref)

        gv, lng = g_ref[...], lng_ref[...]
        y, xhat, rstd = _rms_fwd(x_ref[...], gv)
        hn_ref[...] = y.astype(BF16)
        dhv = dh_ref[...]
        dgated = _dot_nt(dhv.astype(BF16), wout_ref[...])
        u, gelu_grad_u = _gelu_and_grad(z_ref[:, :GATE_DIM])
        v, gelu_grad_v[...] = _gelu_and_grad(z_ref[:, GATE_DIM:])
        vn, vhat, lrstd = _ln_fwd(v, lng, lnb_ref[...])
        vb = vn.astype(BF16)
        mask = _gate_mask()
        for gi in range(A_GROUPS):
            wm = jnp.where(mask, ws_ref[gi], 0.0).astype(BF16)
            bias = bst_ref[:, gi:gi + 1]
            cs = slice(gi * A_GROUP_DIM, (gi + 1) * A_GROUP_DIM)
            dws = jnp.zeros((GMLP_BLOCK, GMLP_BLOCK), F32)
            dbs = jnp.zeros((GMLP_BLOCK, 1), F32)
            for n in range(nblk):
                rs = slice(n * GMLP_BLOCK, (n + 1) * GMLP_BLOCK)
                sv = _dot(wm, vb[rs, cs]) + bias
                dz_ref[rs, cs] = (dgated[rs, cs] * sv * gelu_grad_u[rs, cs]).astype(BF16)
                dsv = dgated[rs, cs] * u[rs, cs]
                dsvb = dsv.astype(BF16)
                dws = dws + _dot_nt(dsvb, vb[rs, cs])
                dbs = dbs + jnp.sum(dsv, axis=-1, keepdims=True)
                dvn_scr[rs, cs] = _dot_tn(wm, dsvb)
            dws_ref[gi] += jnp.where(mask, dws, 0.0)
            dbs_ref[gi] += dbs
        dvn = dvn_scr[...]
        dvhat = dvn * lng
        dv = lrstd * (dvhat - jnp.mean(dvhat, axis=-1, keepdims=True)
                      - vhat * jnp.mean(dvhat * vhat, axis=-1, keepdims=True))
        dz_ref[:, GATE_DIM:] = (dv * gelu_grad_v[...]).astype(BF16)
        dhn = jnp.zeros((tm, D_MODEL), F32)
        for d in range(N_DEV):
            dhn = dhn + _dot_nt(dz_ref[:, d * FF_SLOT:(d + 1) * FF_SLOT], win_ref[d])
        dx, dg = _rms_bwd(dhn, xhat, rstd, gv)
        dx_ref[...] = dhv + dx
        _acc(dg_ref, dg)
        _acc(dlng_ref, jnp.sum(dvn * vhat, axis=0, keepdims=True))
        _acc(dlnb_ref, jnp.sum(dvn, axis=0, keepdims=True))

    return _call(
        "a_mix_bwd", body, (t // tm,),
        [_row(D_MODEL, tm), _row(2 * GATE_DIM, tm), _row(D_MODEL, tm), _res((1, D_MODEL)),
         _res((N_DEV, D_MODEL, FF_SLOT)), _res((1, GATE_DIM)), _res((1, GATE_DIM)),
         _res((A_GROUPS, GMLP_BLOCK, GMLP_BLOCK)), _res((GMLP_BLOCK, A_GROUPS)), _res((GATE_DIM, D_MODEL))],
        [_row(D_MODEL, tm), _row(D_MODEL, tm), _row(2 * GATE_DIM, tm),
         _const((1, D_MODEL)), _const((1, GATE_DIM)), _const((1, GATE_DIM)),
         _const((A_GROUPS, GMLP_BLOCK, GMLP_BLOCK)), _const((A_GROUPS, GMLP_BLOCK, 1))],
        [_sds((t, D_MODEL), F32), _sds((t, D_MODEL), BF16),
         _sds((t, 2 * GATE_DIM), BF16), _sds((1, D_MODEL), F32), _sds((1, GATE_DIM), F32),
         _sds((1, GATE_DIM), F32), _sds((A_GROUPS, GMLP_BLOCK, GMLP_BLOCK), F32),
         _sds((A_GROUPS, GMLP_BLOCK, 1), F32)],
        (x, z, dh, g, w_in, ln_g, ln_b, w_s, b_st, w_out),
        scratch=[pltpu.VMEM((tm, GATE_DIM), F32), pltpu.VMEM((tm, GATE_DIM), F32)], after=after)


def _wgrad(name, a, b, a_spec, b_spec, m, n, after=()):
    def body(a_ref, b_ref, o_ref):
        o_ref[0] = _dot_tn(a_ref[...].astype(BF16), b_ref[...].astype(BF16)).astype(BF16)

    return _call(name, body, (N_DEV,), [a_spec, b_spec], [pl.BlockSpec((1, m, n), lambda d: (d, 0, 0))],
                 [_sds((N_DEV, m, n), BF16)], (a, b), after=after)[0]


def _full(t, d):
    return pl.BlockSpec((t, d), lambda i: (0, 0), pipeline_mode=pl.Buffered(1))


def _cols(t, d):
    return pl.BlockSpec((t, d), lambda i: (0, i))


def _head(t, d):
    return pl.BlockSpec((None, t, d), lambda i: (i, 0, 0))


def _local_step(x, pos, target, inv_freq, wg, sm, shards=None):
    t = x.shape[0]
    wg = dict(wg)
    dist = shards is not None
    mix_g = [sm["norm_mix_g"][l:l + 1] for l in range(2)]
    mlp_g = [sm["norm_mlp_g"][l:l + 1] for l in range(2)]

    ids = iter(range(2, 2 + 9))

    def gather(names):
        if dist:
            got = _by_sequencer("gather_" + names[0], _gather_comm([shards[k] for k in names]),
                                SIBLING_AND_NEIGHBOURS, next(ids))
            wg.update(zip(names, got))

    def send(name, names):
        if dist:
            comm = _exchange_comm(grads=[g[k] for k in names])
            g.update(zip(names, _by_sequencer("exchange_" + name, comm, EVERYONE, next(ids))))

    def send_sums(name, names, meanwhile):
        if not dist:
            meanwhile()
            return ()
        grads = [g[k] for k in names]
        landed = _by_sequencer("pair_exchange_" + name, _pair_exchange_comm(grads), (1,), next(ids))
        sums = _pair_add("pair_add_" + name, grads, landed, after=meanwhile())
        g.update(zip(names, _by_sequencer("exchange_" + name, _chip_exchange_comm(sums), OTHER_CHIPS, next(ids))))
        return sums

    def a_args():
        return (wg["a_w_in"], wg["a_ln_v_g"], wg["a_ln_v_b"], sm["a_w_s"], sm["a_b_st"], wg["a_w_out"])

    def kvq_w():
        return (sm["kv_src_norm_g"], wg["kv_w_a"], sm["kv_a_norm_g"], wg["kv_w_b"], mix_g[1], wg["b_w_q_a"],
                sm["b_q_norm_g"], wg["b_w_q_b"])

    gather(("mlp_w1_0", "mlp_w2_0"))
    h1, z, gated = _a_mix_fwd(x, mix_g[0], *a_args())
    gather(("kv_w_a", "kv_w_b", "b_w_q_a", "b_w_q_b", "b_w_o"))
    h2, a0 = _mlp_fwd(h1, mlp_g[0], wg["mlp_w1_0"], wg["mlp_w2_0"])
    if dist:
        wg["b_w_q_a"] = wg["b_w_q_a"].reshape(D_MODEL, Q_LORA)
        wg["kv_w_a"] = wg["kv_w_a"].reshape(D_MODEL, KV_LORA + QK_ROPE)
    gather(("mlp_w1_1", "mlp_w2_1"))
    ckv, k, v, cqpre, q, cos, sin = _kvq_fwd(h2, pos, inv_freq, kvq_w())
    h3, att = _attn_fwd(h2, q, k, v, wg["b_w_o"])
    a1, loss, dh4, d_final_g = _mlp_fwd_loss(h3, mlp_g[1], wg["mlp_w1_1"], wg["mlp_w2_1"], sm["final_norm_g"], target)

    g = {}
    dh3, d_mlp_g1, hn, f, da, dh3_b = _mlp_bwd(h3, a1, dh4, mlp_g[1], wg["mlp_w1_1"], wg["mlp_w2_1"], 1)
    dq, dk, dv = _attn_bwd(dh3_b, q, k, v, wg["b_w_o"], cos, sin)
    g["mlp_w1_1"] = _wgrad("wgrad_w1_1", hn, da, _full(t, D_MODEL), _cols(t, FF_SLOT), D_MODEL, FF_SLOT, after=[dq])
    g["mlp_w2_1"] = _wgrad("wgrad_w2_1", f, dh4, _cols(t, FF_SLOT), _full(t, D_MODEL), FF_SLOT, D_MODEL)

    def wgrad_w_o():
        g["b_w_o"] = _wgrad("wgrad_w_o", att, dh3_b, _head(t, V_HEAD), _full(t, D_MODEL), V_HEAD, D_MODEL)
        return [g["b_w_o"]]

    sums = send_sums("mlp_1", ("mlp_w1_1", "mlp_w2_1"), wgrad_w_o)
    dh2, d_mix_g1, d_src_g, d_q_g, d_kv_a_g, g_q_a, g["b_w_q_b"], g_kv_a, g["kv_w_b"] = _kvq_bwd(
        h2, dh3, ckv, cqpre, dq, dk, dv, cos, sin, kvq_w(), after=sums)
    g["b_w_q_a"] = g_q_a.reshape(N_DEV, D_MODEL // N_DEV, Q_LORA)
    g["kv_w_a"] = g_kv_a.reshape(N_DEV, D_MODEL // N_DEV, KV_LORA + QK_ROPE)
    qkv = ("b_w_q_a", "b_w_q_b", "kv_w_a", "kv_w_b")
    landed = [g[k] for k in qkv]
    send("qkv", qkv)
    dh1, d_mlp_g0, hn, f, da, dh1_b = _mlp_bwd(h1, a0, dh2, mlp_g[0], wg["mlp_w1_0"], wg["mlp_w2_0"], 0,
                                               after=landed if dist else ())
    landed = [g["mlp_w1_1"], g["mlp_w2_1"]] if dist else ()
    g["mlp_w1_0"] = _wgrad("wgrad_w1_0", hn, da, _full(t, D_MODEL), _cols(t, FF_SLOT), D_MODEL, FF_SLOT, after=landed)
    g["mlp_w2_0"] = _wgrad("wgrad_w2_0", f, dh2, _cols(t, FF_SLOT), _full(t, D_MODEL), FF_SLOT, D_MODEL)

    def wgrad_a_w_out():
        g["a_w_out"] = _wgrad("wgrad_a_w_out", gated, dh1_b, _cols(t, GATE_DIM // N_DEV), _full(t, D_MODEL),
                              GATE_DIM // N_DEV, D_MODEL)
        return [g["a_w_out"]] + [g[k] for k in qkv]

    sums = send_sums("mlp_0", ("mlp_w1_0", "mlp_w2_0", "b_w_o"), wgrad_a_w_out)
    dx, hn, dz, d_mix_g0, d_ln_g, d_ln_b, d_ws, d_bs = _a_mix_bwd(x, z, dh1, mix_g[0], *a_args(), after=sums)
    small = {
        "norm_mix_g": jnp.concatenate([d_mix_g0, d_mix_g1], axis=0),
        "norm_mlp_g": jnp.concatenate([d_mlp_g0, d_mlp_g1], axis=0),
        "a_ln_v_g": d_ln_g.reshape(N_DEV, GATE_DIM // N_DEV),
        "a_ln_v_b": d_ln_b.reshape(N_DEV, GATE_DIM // N_DEV),
        "a_w_s": d_ws.astype(BF16) if dist else d_ws,
        "a_b_s": d_bs.reshape(A_GROUPS, GMLP_BLOCK),
        "b_q_norm_g": d_q_g,
        "kv_src_norm_g": d_src_g,
        "kv_a_norm_g": d_kv_a_g,
        "final_norm_g": d_final_g,
    }
    if dist:
        parts = [small[k].reshape((1,) + small[k].shape) for k in SMALL] + [loss.reshape(1, 1, 1)]
        got = _by_sequencer("gather_small", _exchange_comm(parts=parts), EVERYONE, next(ids))
        small, loss = dict(zip(SMALL, got)), got[-1]
    g["a_w_in"] = _wgrad("wgrad_a_w_in", hn, dz, _full(t, D_MODEL), _cols(t, FF_SLOT), D_MODEL, FF_SLOT)
    return loss, dx, g, small


def _adamw(w, g, m, v):
    m = ADAM_B1 * m + (1.0 - ADAM_B1) * g
    v = ADAM_B2 * v + (1.0 - ADAM_B2) * (g * g)
    m_hat = m / (1.0 - ADAM_B1 ** ADAM_STEP)
    v_hat = v / (1.0 - ADAM_B2 ** ADAM_STEP)
    return -ADAM_LR * (m_hat / (jnp.sqrt(v_hat) + ADAM_EPS) + ADAM_WD * w), m, v


def _sum_in_device_order(r_ref):
    g = r_ref[0].astype(F32)
    for j in range(1, r_ref.shape[0]):
        g = g + r_ref[j].astype(F32)
    return g


def _adamw_sharded(name, recvs, w, m, v):
    layers, r, c = w.shape
    tr = math.gcd(r, 512)
    flat = [a for per_layer in recvs for a in per_layer]

    def body(*refs):
        r_refs, (w_ref, m_ref, v_ref) = refs[:len(flat)], refs[len(flat):len(flat) + 3]
        g_ref, d_ref, nm_ref, nv_ref = refs[-4:]
        layer = pl.program_id(0)
        g, pos = None, 0
        for li, per_layer in enumerate(recvs):
            total = None
            for ref in r_refs[pos:pos + len(per_layer)]:
                part = _sum_in_device_order(ref)
                total = part if total is None else total + part
            pos += len(per_layer)
            g = total if g is None else jnp.where(layer == li, total, g)
        g_ref[...] = g
        d_ref[...], nm_ref[...], nv_ref[...] = _adamw(w_ref[...], g, m_ref[...], v_ref[...])

    blk = pl.BlockSpec((None, tr, c), lambda l, i: (l, i, 0))
    return _call(name, body, (layers, r // tr),
                 [pl.BlockSpec((a.shape[0], tr, c), lambda l, i: (0, i, 0)) for a in flat] + [blk] * 3,
                 [blk] * 4, [_sds(w.shape, F32)] * 4, (*flat, w, m, v))


def _adamw_small(recvs, ws, ms, vs, own_row, losses):
    n = len(recvs)

    def body(*refs):
        r_refs, w_refs, m_refs, v_refs = (refs[i * n:(i + 1) * n] for i in range(4))
        outs, scr = refs[4 * n + 1:8 * n + 2], refs[8 * n + 2:]
        outs[-1][...] = _sum_in_device_order(refs[4 * n])
        me = _my_place()[3]
        for a in range(n):
            g = _sum_in_device_order(r_refs[a])
            if own_row[a]:
                scr[0][...] = g
                g = scr[0][pl.ds(me, 1), :]
            g_ref, d_ref, nm_ref, nv_ref = outs[4 * a:4 * a + 4]
            g_ref[...] = g
            d_ref[...], nm_ref[...], nv_ref[...] = _adamw(w_refs[a][...], g, m_refs[a][...], v_refs[a][...])

    out_shape = []
    for w in ws:
        out_shape += [_sds(w.shape, F32)] * 4
    return pl.pallas_call(
        body, name="adamw_small", in_specs=[VMEM] * (4 * n + 1), out_specs=[VMEM] * (4 * n + 1),
        out_shape=out_shape + [_sds((1, 1), F32)], scratch_shapes=[pltpu.VMEM((N_DEV, GATE_DIM // N_DEV), F32)],
    )(*recvs, *ws, *ms, *vs, losses)


BIG = ("a_w_in", "a_w_out", "b_w_q_a", "b_w_q_b", "b_w_o", "kv_w_a", "kv_w_b", "mlp_w1", "mlp_w2")
SMALL = ("norm_mix_g", "norm_mlp_g", "a_ln_v_g", "a_ln_v_b", "a_w_s", "a_b_s", "b_q_norm_g", "kv_src_norm_g",
         "kv_a_norm_g", "final_norm_g")
WEIGHTS = ("norm_mix_g", "norm_mlp_g", "a_w_in", "a_ln_v_g", "a_ln_v_b", "a_w_s", "a_b_s", "a_w_out", "b_w_q_a",
           "b_q_norm_g", "b_w_q_b", "b_w_o", "kv_src_norm_g", "kv_w_a", "kv_a_norm_g", "kv_w_b", "mlp_w1", "mlp_w2",
           "final_norm_g")


def _two_d(name, a):
    if name in ("a_w_s", "a_b_s"):
        return a.reshape(a.shape[1:])
    return a.reshape(1, -1) if a.ndim == 1 else a


def _three_d(a):
    return a if a.ndim == 3 else a.reshape((1,) + a.shape)


def kernel(x, positions, norm_mix_g, norm_mlp_g, a_w_in, a_ln_v_g, a_ln_v_b, a_w_s, a_b_s, a_w_out, b_w_q_a, b_q_norm_g, b_w_q_b, b_w_o, kv_src_norm_g, kv_w_a, kv_a_norm_g, kv_w_b, mlp_w1, mlp_w2, final_norm_g, loss_target, m_norm_mix_g, m_norm_mlp_g, m_a_w_in, m_a_ln_v_g, m_a_ln_v_b, m_a_w_s, m_a_b_s, m_a_w_out, m_b_w_q_a, m_b_q_norm_g, m_b_w_q_b, m_b_w_o, m_kv_src_norm_g, m_kv_w_a, m_kv_a_norm_g, m_kv_w_b, m_mlp_w1, m_mlp_w2, m_final_norm_g, v_norm_mix_g, v_norm_mlp_g, v_a_w_in, v_a_ln_v_g, v_a_ln_v_b, v_a_w_s, v_a_b_s, v_a_w_out, v_b_w_q_a, v_b_q_norm_g, v_b_w_q_b, v_b_w_o, v_kv_src_norm_g, v_kv_w_a, v_kv_a_norm_g, v_kv_w_b, v_mlp_w1, v_mlp_w2, v_final_norm_g):
    w = dict(norm_mix_g=norm_mix_g, norm_mlp_g=norm_mlp_g, a_w_in=a_w_in, a_ln_v_g=a_ln_v_g, a_ln_v_b=a_ln_v_b,
             a_w_s=a_w_s, a_b_s=a_b_s, a_w_out=a_w_out, b_w_q_a=b_w_q_a, b_q_norm_g=b_q_norm_g, b_w_q_b=b_w_q_b,
             b_w_o=b_w_o, kv_src_norm_g=kv_src_norm_g, kv_w_a=kv_w_a, kv_a_norm_g=kv_a_norm_g, kv_w_b=kv_w_b,
             mlp_w1=mlp_w1, mlp_w2=mlp_w2, final_norm_g=final_norm_g)
    m = dict(norm_mix_g=m_norm_mix_g, norm_mlp_g=m_norm_mlp_g, a_w_in=m_a_w_in, a_ln_v_g=m_a_ln_v_g,
             a_ln_v_b=m_a_ln_v_b, a_w_s=m_a_w_s, a_b_s=m_a_b_s, a_w_out=m_a_w_out, b_w_q_a=m_b_w_q_a,
             b_q_norm_g=m_b_q_norm_g, b_w_q_b=m_b_w_q_b, b_w_o=m_b_w_o, kv_src_norm_g=m_kv_src_norm_g,
             kv_w_a=m_kv_w_a, kv_a_norm_g=m_kv_a_norm_g, kv_w_b=m_kv_w_b, mlp_w1=m_mlp_w1, mlp_w2=m_mlp_w2,
             final_norm_g=m_final_norm_g)
    v = dict(norm_mix_g=v_norm_mix_g, norm_mlp_g=v_norm_mlp_g, a_w_in=v_a_w_in, a_ln_v_g=v_a_ln_v_g,
             a_ln_v_b=v_a_ln_v_b, a_w_s=v_a_w_s, a_b_s=v_a_b_s, a_w_out=v_a_w_out, b_w_q_a=v_b_w_q_a,
             b_q_norm_g=v_b_q_norm_g, b_w_q_b=v_b_w_q_b, b_w_o=v_b_w_o, kv_src_norm_g=v_kv_src_norm_g,
             kv_w_a=v_kv_w_a, kv_a_norm_g=v_kv_a_norm_g, kv_w_b=v_kv_w_b, mlp_w1=v_mlp_w1, mlp_w2=v_mlp_w2,
             final_norm_g=v_final_norm_g)
    t = x.shape[1]

    first = ("a_w_in", "a_w_out", "a_ln_v_g", "a_ln_v_b")
    later = ("mlp_w1", "mlp_w2", "kv_w_a", "kv_w_b", "b_w_q_a", "b_w_q_b", "b_w_o")
    later_blocks = ("mlp_w1_0", "mlp_w1_1", "mlp_w2_0", "mlp_w2_1") + later[2:]
    got, casts = _gather_first([_three_d(w[k]) if k in BIG else w[k] for k in first], [_three_d(w[k]) for k in later])
    wg = dict(zip(first, got))
    wg["a_w_out"] = wg["a_w_out"].reshape(GATE_DIM, D_MODEL)
    wg["a_ln_v_g"] = wg["a_ln_v_g"].reshape(1, GATE_DIM)
    wg["a_ln_v_b"] = wg["a_ln_v_b"].reshape(1, GATE_DIM)
    shards = dict(zip(later_blocks, casts))

    sm = {k: _two_d(k, w[k]) for k in SMALL if k not in ("a_ln_v_g", "a_ln_v_b")}
    sm["a_b_st"] = sm["a_b_s"].T
    inv_freq = (ROPE_THETA ** (-jnp.arange(0, QK_ROPE, 2, dtype=F32) / QK_ROPE)).reshape(1, QK_ROPE // 2)

    losses, dx, g, small = _local_step(x[0], positions.reshape(t, 1), loss_target[0], inv_freq, wg, sm, shards)

    names = ("a_w_in", "a_w_out")
    sums = _pair_reduce("pair_reduce_a", [g[k] for k in names], after=[g["mlp_w1_0"], g["mlp_w2_0"]])
    g.update(zip(names, _by_sequencer("exchange_last", _chip_exchange_comm(sums), OTHER_CHIPS, collective_id=1)))

    out = {}
    for k in BIG:
        recvs = [[g[k + "_0"]], [g[k + "_1"]]] if k.startswith("mlp") else [[g[k]]]
        res = _adamw_sharded("adamw_" + k, recvs, _three_d(w[k]), _three_d(m[k]), _three_d(v[k]))
        out[k] = [o.reshape(w[k].shape) for o in res]
    own_row = [k in ("a_ln_v_g", "a_ln_v_b") for k in SMALL]
    res = _adamw_small([small[k] for k in SMALL], [_two_d(k, w[k]) for k in SMALL], [_two_d(k, m[k]) for k in SMALL],
                       [_two_d(k, v[k]) for k in SMALL], own_row, losses)
    for i, k in enumerate(SMALL):
        out[k] = [o.reshape(w[k].shape) for o in res[4 * i:4 * i + 4]]

    return (res[-1].reshape(()), dx.reshape(x.shape), *[out[k][0] for k in WEIGHTS], *[out[k][1] for k in WEIGHTS],
            *[out[k][2] for k in WEIGHTS], *[out[k][3] for k in WEIGHTS])
```

```python
import math

import jax
import jax.numpy as jnp
from jax import lax
from jax.experimental import pallas as pl
from jax.experimental.pallas import tpu as pltpu
from jax.experimental.pallas import tpu_sc as plsc

F32, BF16 = jnp.float32, jnp.bfloat16
MESH = pl.DeviceIdType.MESH
ANY = pl.BlockSpec(memory_space=pl.ANY)
VMEM = pl.BlockSpec(memory_space=pltpu.VMEM)

N_DEV = 8
D_MODEL = 1024
CHUNK = 64
GMLP_BLOCK = 128
GATE_DIM = 2048
A_GROUPS = 8
A_GROUP_DIM = GATE_DIM // A_GROUPS
B_HEADS = 8
QK_NOPE, QK_ROPE, V_HEAD = 128, 64, 128
Q_LORA, KV_LORA = 384, 256
ROPE_THETA = 10000.0
D_FF = 4096
FF_SLOT = D_FF // N_DEV
EPS = 1e-6
ATT_SCALE = (QK_NOPE + QK_ROPE) ** -0.5

ADAM_LR, ADAM_B1, ADAM_B2, ADAM_EPS, ADAM_WD, ADAM_STEP = 0.001, 0.9, 0.999, 1e-08, 0.01, 10

TM = 256
TM_GATE = 256
VMEM_LIMIT = 56 * 1024 * 1024
INV_SQRT2 = 1.0 / math.sqrt(2.0)
INV_SQRT_2PI = 1.0 / math.sqrt(2.0 * math.pi)
LOG2_E = 1.0 / math.log(2.0)
HEADS_PER_STEP = 2


def _dot(a, b):
    return jnp.dot(a, b, preferred_element_type=F32)


def _dot_nt(a, b):
    return lax.dot_general(a, b, (((1,), (1,)), ((), ())), preferred_element_type=F32)


def _dot_tn(a, b):
    return lax.dot_general(a, b, (((0,), (0,)), ((), ())), preferred_element_type=F32)


def _rms_fwd(x, g):
    rstd = lax.rsqrt(jnp.mean(x * x, axis=-1, keepdims=True) + EPS)
    xhat = x * rstd
    return xhat * g, xhat, rstd


def _rms_bwd(dy, xhat, rstd, g):
    dxhat = dy * g
    dx = rstd * (dxhat - xhat * jnp.mean(dxhat * xhat, axis=-1, keepdims=True))
    return dx, jnp.sum(dy * xhat, axis=0, keepdims=True)


def _ln_fwd(v, g, b):
    mu = jnp.mean(v, axis=-1, keepdims=True)
    vc = v - mu
    rstd = lax.rsqrt(jnp.mean(vc * vc, axis=-1, keepdims=True) + EPS)
    vhat = vc * rstd
    return vhat * g + b, vhat, rstd


def _gelu(x):
    return 0.5 * x * (1.0 + lax.erf(x * INV_SQRT2))


def _gelu_and_grad(x):
    cdf = 0.5 * (1.0 + lax.erf(x * INV_SQRT2))
    return x * cdf, cdf + x * jnp.exp(-0.5 * x * x) * INV_SQRT_2PI


def _rope(x, cos, sin):
    x1, x2 = x[:, :QK_ROPE // 2], x[:, QK_ROPE // 2:]
    return jnp.concatenate([x1 * cos - x2 * sin, x2 * cos + x1 * sin], axis=-1)


def _gate_mask():
    row = lax.broadcasted_iota(jnp.int32, (GMLP_BLOCK, GMLP_BLOCK), 0)
    col = lax.broadcasted_iota(jnp.int32, (GMLP_BLOCK, GMLP_BLOCK), 1)
    return (col < CHUNK) | (row >= CHUNK)


def _att_mask(q0, tq, t):
    q = q0 + lax.broadcasted_iota(jnp.int32, (tq, t), 0)
    k = lax.broadcasted_iota(jnp.int32, (tq, t), 1)
    return jnp.right_shift(k, 6) <= jnp.right_shift(q, 6)


def _res(shape, imap=None):
    zeros = (0,) * len(shape)
    return pl.BlockSpec(shape, imap or (lambda i: zeros), pipeline_mode=pl.Buffered(1))


def _const(shape):
    zeros = (0,) * len(shape)
    return pl.BlockSpec(shape, lambda i: zeros)


def _row(d, tm=TM):
    return pl.BlockSpec((tm, d), lambda i: (i, 0))


def _heads(d):
    return pl.BlockSpec((B_HEADS, TM, d), lambda i: (0, i, 0))


def _sds(shape, dt):
    return jax.ShapeDtypeStruct(shape, dt)


def _acc(ref, val):
    @pl.when(pl.program_id(0) == 0)
    def _():
        ref[...] = jnp.zeros_like(ref)
    ref[...] += val


def _my_place():
    x, y, c = lax.axis_index("x"), lax.axis_index("y"), lax.axis_index("c")
    return x, y, c, 4 * x + 2 * y + c


def _peer(x, y, c, k):
    px = 1 - x if k & 4 else x
    py = 1 - y if k & 2 else y
    pc = 1 - c if k & 1 else c
    return (px, py, pc), 4 * px + 2 * py + pc


CHIPS = (2, 4, 6)


def _splits(ref):
    return len(ref.shape) >= 3 and ref.shape[1] % 32 == 0


def _piece(ref, block, half=None):
    if half is None or not _splits(ref):
        return ref.at[pl.ds(block, 1)]
    rows = ref.shape[1] // 2
    return ref.at[pl.ds(block, 1), pl.ds(half * rows, rows)]


def _gather_copy(sems, a, k, piece, to, src=None):
    return pltpu.make_async_remote_copy(
        src_ref=piece if src is None else src, dst_ref=piece, send_sem=sems[0].at[a, k], recv_sem=sems[1].at[a, k],
        device_id=to, device_id_type=MESH)


def _gather_start(srcs, outs, sems, only=None):
    x, y, c, me = _my_place()
    for a in range(len(srcs)) if only is None else (only,):
        mine = _piece(outs[a], me)
        pltpu.make_async_copy(srcs[a], mine, sems[2].at[a]).start()
        for k, rel in enumerate((1, 4, 2)):
            _gather_copy(sems, a, k, mine, _peer(x, y, c, rel)[0], src=srcs[a]).start()


def _gather_relay(srcs, outs, sems):
    x, y, c, _ = _my_place()
    sib = _peer(x, y, c, 1)[0]
    (xn, xn_i), (yn, yn_i) = _peer(x, y, c, 4), _peer(x, y, c, 2)
    for a in range(len(srcs)):
        out = outs[a]
        _gather_copy(sems, a, 1, _piece(out, xn_i), xn).wait_recv()
        _gather_copy(sems, a, 3, _piece(out, xn_i, 0), yn).start()
        _gather_copy(sems, a, 5, _piece(out, xn_i), sib).start()
        _gather_copy(sems, a, 2, _piece(out, yn_i), yn).wait_recv()
        if _splits(out):
            _gather_copy(sems, a, 4, _piece(out, yn_i, 1), xn).start()
        _gather_copy(sems, a, 6, _piece(out, yn_i), sib).start()


def _gather_finish(srcs, outs, sems):
    x, y, c, me = _my_place()
    sib = _peer(x, y, c, 1)[0]
    xn, yn, dg_i = _peer(x, y, c, 4)[0], _peer(x, y, c, 2)[0], _peer(x, y, c, 6)[1]
    n = len(srcs)
    for a in range(n):
        out = outs[a]
        _gather_copy(sems, a, 3, _piece(out, dg_i, 0), yn).wait_recv()
        _gather_copy(sems, a, 7, _piece(out, dg_i, 0), sib).start()
        if _splits(out):
            _gather_copy(sems, a, 4, _piece(out, dg_i, 1), xn).wait_recv()
            _gather_copy(sems, a, 8, _piece(out, dg_i, 1), sib).start()
    for a in range(n):
        out = outs[a]
        whole, half = _piece(out, me), _piece(out, me, 0)
        for k in (0, 5, 6):
            _gather_copy(sems, a, k, whole, sib).wait_recv()
        for k in (7, 8) if _splits(out) else (7,):
            _gather_copy(sems, a, k, half, sib).wait_recv()
        for k in (0, 1, 2):
            _gather_copy(sems, a, k, whole, sib, src=srcs[a]).wait_send()
        for k in (5, 6):
            _gather_copy(sems, a, k, whole, sib).wait_send()
        for k in (3, 4, 7, 8) if _splits(out) else (3, 7):
            _gather_copy(sems, a, k, half, sib).wait_send()
        pltpu.make_async_copy(srcs[a], whole, sems[2].at[a]).wait()


def _relay_sems(n):
    return [pltpu.SemaphoreType.DMA((n, 9)), pltpu.SemaphoreType.DMA((n, 9)), pltpu.SemaphoreType.DMA((n,))]


def _gather_sems(n):
    return [pltpu.SemaphoreType.DMA((n, 7)), pltpu.SemaphoreType.DMA((n, 7)), pltpu.SemaphoreType.DMA((n,))]


class _Comm:
    def __init__(self, args, out_shape, scratch, start, finish, relay=None):
        self.args, self.out_shape, self.scratch, self.start, self.finish = args, out_shape, scratch, start, finish
        self.relay = relay


def _gather_comm(shards):
    return _Comm(list(shards), [_sds((N_DEV,) + s.shape[1:], s.dtype) for s in shards], _relay_sems(len(shards)),
                 _gather_start, _gather_finish, relay=_gather_relay)


def _direct_copies(ins, outs, sems, wait):
    send_sems, recv_sems, local_sems = sems
    x, y, c, me = _my_place()
    for a in range(len(ins)):
        local = pltpu.make_async_copy(ins[a].at[pl.ds(me, 1)], outs[a].at[pl.ds(me, 1)], local_sems.at[a])
        local.wait() if wait else local.start()
        for k in range(1, N_DEV):
            to, to_i = _peer(x, y, c, k)
            cp = pltpu.make_async_remote_copy(
                src_ref=ins[a].at[pl.ds(to_i, 1)], dst_ref=outs[a].at[pl.ds(me, 1)],
                send_sem=send_sems.at[a, k - 1], recv_sem=recv_sems.at[a, k - 1], device_id=to, device_id_type=MESH)
            cp.wait() if wait else cp.start()


def _exchange_comm(grads):
    return _Comm(list(grads), [_sds(g.shape, g.dtype) for g in grads], _gather_sems(len(grads)),
                 lambda i, o, s: _direct_copies(i, o, s, False), lambda i, o, s: _direct_copies(i, o, s, True))


def _chip_copies(ins, outs, sems, wait, rels, own):
    send_sems, recv_sems, local_sems = sems
    x, y, c, _ = _my_place()
    for a in range(len(ins)):
        if own:
            local = pltpu.make_async_copy(ins[a].at[pl.ds(2 * x + y, 1)], outs[a].at[pl.ds(len(rels), 1)],
                                          local_sems.at[a])
            local.wait() if wait else local.start()
        for i, j in enumerate(rels):
            to = _peer(x, y, c, CHIPS[j])[0]
            cp = pltpu.make_async_remote_copy(
                src_ref=ins[a].at[pl.ds(2 * to[0] + to[1], 1)], dst_ref=outs[a].at[pl.ds(i, 1)],
                send_sem=send_sems.at[a, i], recv_sem=recv_sems.at[a, i], device_id=to, device_id_type=MESH)
            cp.wait() if wait else cp.start()


def _chip_exchange_comm(sums, rels=(0, 1, 2), own=True):
    def start(ins_, outs_, sems_):
        _chip_copies(ins_, outs_, sems_, False, rels, own)

    def finish(ins_, outs_, sems_):
        _chip_copies(ins_, outs_, sems_, True, rels, own)

    n = len(sums)
    sems = [pltpu.SemaphoreType.DMA((n, len(rels))), pltpu.SemaphoreType.DMA((n, len(rels))),
            pltpu.SemaphoreType.DMA((n,))]
    return _Comm(list(sums), [_sds((len(rels) + own,) + s.shape[1:], s.dtype) for s in sums], sems, start, finish)


def _pair_reduce(name, grads, after=()):
    n = len(grads)
    n_chips = N_DEV // 2

    def body(*refs):
        g_refs, gh_refs, refs = refs[:n], refs[n:2 * n], refs[2 * n + len(after):]
        p_refs, land = refs[:n], refs[n:2 * n]
        send_sems, recv_sems = refs[2 * n:]
        x, y, c, _ = _my_place()
        sib = _peer(x, y, c, 1)[0]
        q = pl.program_id(0)

        def to_sibling(a, j):
            return pltpu.make_async_remote_copy(
                src_ref=gh_refs[a].at[j, pl.ds(1 - c, 1)], dst_ref=land[a].at[pl.ds(j, 1)],
                send_sem=send_sems.at[a, j], recv_sem=recv_sems.at[a, j], device_id=sib, device_id_type=MESH)

        @pl.when(q == 0)
        def _():
            for j in range(n_chips):
                for a in range(n):
                    to_sibling(a, j).start()

        for a in range(n):
            to_sibling(a, q).wait_recv()
            p_refs[a][...] = (g_refs[a][0, pl.ds(c, 1)].astype(F32) + land[a][pl.ds(q, 1)].astype(F32)).astype(BF16)

        @pl.when(q == n_chips - 1)
        def _():
            for a in range(n):
                for j in range(n_chips):
                    to_sibling(a, j).wait_send()

    views = [g.reshape((n_chips, 2) + g.shape[1:]) for g in grads]
    res = pl.pallas_call(
        body, name=name, grid=(n_chips,),
        in_specs=[pl.BlockSpec((1, 2) + g.shape[1:], lambda q: (q, 0, 0, 0)) for g in grads]
        + [ANY] * (n + len(after)),
        out_specs=[pl.BlockSpec((1,) + g.shape[1:], lambda q: (q, 0, 0)) for g in grads],
        out_shape=[_sds((n_chips,) + g.shape[1:], BF16) for g in grads],
        scratch_shapes=[pltpu.VMEM((n_chips,) + g.shape[1:], BF16) for g in grads]
        + [pltpu.SemaphoreType.DMA((n, n_chips)), pltpu.SemaphoreType.DMA((n, n_chips))],
        compiler_params=pltpu.CompilerParams(dimension_semantics=("arbitrary",), vmem_limit_bytes=VMEM_LIMIT),
    )(*views, *views, *after)
    return list(res)


def _pair_exchange_comm(grads):
    n, n_chips = len(grads), N_DEV // 2

    def copies(ins, outs, sems, wait):
        x, y, c, _ = _my_place()
        for j in range(n_chips):
            for a in range(n):
                cp = pltpu.make_async_remote_copy(
                    src_ref=ins[a].at[j, pl.ds(1 - c, 1)], dst_ref=outs[a].at[pl.ds(j, 1)], send_sem=sems[0].at[a, j],
                    recv_sem=sems[1].at[a, j], device_id=_peer(x, y, c, 1)[0], device_id_type=MESH)
                cp.wait() if wait else cp.start()

    views = [g.reshape((n_chips, 2) + g.shape[1:]) for g in grads]
    sems = [pltpu.SemaphoreType.DMA((n, n_chips)), pltpu.SemaphoreType.DMA((n, n_chips))]
    return _Comm(views, [_sds((n_chips,) + g.shape[1:], g.dtype) for g in grads], sems,
                 lambda i, o, s: copies(i, o, s, False), lambda i, o, s: copies(i, o, s, True))


def _pair_add(name, grads, landed, after=()):
    n, n_chips = len(grads), N_DEV // 2

    def body(core_ref, *refs):
        g_refs, l_refs, p_refs = refs[:n], refs[n:2 * n], refs[2 * n + len(after):]
        for a in range(n):
            p_refs[a][...] = (g_refs[a][...].astype(F32) + l_refs[a][...].astype(F32)).astype(BF16)

    views = [g.reshape((n_chips, 2) + g.shape[1:]) for g in grads]
    blocks = [pl.BlockSpec((1,) + g.shape[1:], lambda q, core: (q, 0, 0)) for g in grads]
    mine = [pl.BlockSpec((1, None) + g.shape[1:], lambda q, core: (q, core[0], 0, 0)) for g in grads]
    return list(pl.pallas_call(
        body, name=name, out_shape=[_sds((n_chips,) + g.shape[1:], BF16) for g in grads],
        grid_spec=pltpu.PrefetchScalarGridSpec(num_scalar_prefetch=1, grid=(n_chips,),
                                               in_specs=mine + blocks + [ANY] * len(after), out_specs=blocks),
        compiler_params=pltpu.CompilerParams(dimension_semantics=("arbitrary",), vmem_limit_bytes=VMEM_LIMIT),
    )(lax.axis_index("c").reshape(1), *views, *landed, *after))


def _call(name, body, grid, in_specs, out_specs, out_shape, args, scratch=(), after=()):
    ni, na = len(in_specs), len(after)

    def ordered(*refs):
        body(*refs[:ni], *refs[ni + na:])

    return list(pl.pallas_call(
        ordered if after else body, name=name, grid=grid, in_specs=list(in_specs) + [ANY] * na,
        out_specs=list(out_specs), out_shape=list(out_shape), scratch_shapes=list(scratch),
        compiler_params=pltpu.CompilerParams(dimension_semantics=("arbitrary",) * len(grid),
                                             vmem_limit_bytes=VMEM_LIMIT))(*args, *after))


SIBLING_AND_NEIGHBOURS, OTHER_CHIPS, EVERYONE = (1, 4, 2), CHIPS, tuple(range(1, N_DEV))


def _by_sequencer(name, comm, peers, collective_id):
    src = [jax.new_ref(a, memory_space=pltpu.MemorySpace.HBM) for a in comm.args]
    dst = [jax.empty_ref(s, memory_space=pltpu.MemorySpace.HBM) for s in comm.out_shape]

    @pl.kernel(mesh=plsc.ScalarSubcoreMesh(axis_name="sequencer", num_cores=1), name=name,
               scratch_types=tuple(comm.scratch), compiler_params=pltpu.CompilerParams(collective_id=collective_id))
    def launch(*sems):
        x, y, c, _ = _my_place()
        barrier = pltpu.get_barrier_semaphore()
        for k in peers:
            pl.semaphore_signal(barrier, inc=1, device_id=_peer(x, y, c, k)[0], device_id_type=MESH)
        pl.semaphore_wait(barrier, len(peers))
        comm.start(src, dst, sems)
        if comm.relay is not None:
            comm.relay(src, dst, sems)
        comm.finish(src, dst, sems)

    launch()
    return [d[...] for d in dst]


def _gather_first(first, later):
    nf = len(first)
    layer_of = [(a, l) for a, s in enumerate(later) for l in range(s.shape[0])]
    nl = len(layer_of)
    dts = [BF16] * (nf - 2) + [F32, F32]

    def body(*refs):
        ins, refs = refs[:nf + len(later)], refs[nf + len(later):]
        outs, refs = refs[:nf], refs[nf:]
        casts, refs = refs[:nl], refs[nl:]
        stage, sems = refs[:nf], refs[nf:]
        for a in range(nf):
            stage[a][...] = ins[a][...].astype(dts[a])
            _gather_start(stage, outs, sems, only=a)
        for k, (a, l) in enumerate(layer_of):
            casts[k][...] = ins[nf + a][l:l + 1].astype(BF16)
        _gather_relay(stage, outs, sems)
        _gather_finish(stage, outs, sems)

    res = pl.pallas_call(
        body, name="gather_first",
        in_specs=[VMEM] * (nf + len(later)), out_specs=[ANY] * nf + [VMEM] * nl,
        out_shape=[_sds((N_DEV,) + s.shape[1:], dt) for s, dt in zip(first, dts)]
        + [_sds((1,) + later[a].shape[1:], BF16) for a, _ in layer_of],
        scratch_shapes=[pltpu.VMEM(s.shape, dt) for s, dt in zip(first, dts)] + _relay_sems(nf),
        compiler_params=pltpu.CompilerParams(vmem_limit_bytes=VMEM_LIMIT),
    )(*first, *later)
    return list(res[:nf]), list(res[nf:])


def _a_mix_fwd(x, g, w_in, ln_g, ln_b, w_s, b_st, w_out):
    t = x.shape[0]
    nblk = TM // GMLP_BLOCK

    def body(x_ref, g_ref, win_ref, lng_ref, lnb_ref, ws_ref, bst_ref, wout_ref, h_ref, z_ref, gated_scr):
        xv = x_ref[...]
        hb = _rms_fwd(xv, g_ref[...])[0].astype(BF16)
        for d in range(N_DEV):
            z_ref[:, d * FF_SLOT:(d + 1) * FF_SLOT] = _dot(hb, win_ref[d])
        u = _gelu(z_ref[:, :GATE_DIM])
        vb = _ln_fwd(_gelu(z_ref[:, GATE_DIM:]), lng_ref[...], lnb_ref[...])[0].astype(BF16)
        mask = _gate_mask()
        for gi in range(A_GROUPS):
            wm = jnp.where(mask, ws_ref[gi], 0.0).astype(BF16)
            bias = bst_ref[:, gi:gi + 1]
            cs = slice(gi * A_GROUP_DIM, (gi + 1) * A_GROUP_DIM)
            for n in range(nblk):
                rs = slice(n * GMLP_BLOCK, (n + 1) * GMLP_BLOCK)
                sv = _dot(wm, vb[rs, cs]) + bias
                gated_scr[rs, cs] = (u[rs, cs] * sv).astype(BF16)
        h_ref[...] = xv + _dot(gated_scr[...], wout_ref[...])

    return _call(
        "a_mix_fwd", body, (t // TM,),
        [_row(D_MODEL), _res((1, D_MODEL)), _res((N_DEV, D_MODEL, FF_SLOT)), _res((1, GATE_DIM)),
         _res((1, GATE_DIM)), _res((A_GROUPS, GMLP_BLOCK, GMLP_BLOCK)), _res((GMLP_BLOCK, A_GROUPS)),
         _res((GATE_DIM, D_MODEL))],
        [_row(D_MODEL), _row(2 * GATE_DIM), _row(GATE_DIM)],
        [_sds((t, D_MODEL), F32), _sds((t, 2 * GATE_DIM), F32), _sds((t, GATE_DIM), BF16)],
        (x, g, w_in, ln_g, ln_b, w_s, b_st, w_out))


MLP_W_SPECS = (_res((N_DEV, D_MODEL, FF_SLOT)), _res((N_DEV, FF_SLOT, D_MODEL)))


def _mlp_fwd(h, g, w1, w2):
    t = h.shape[0]

    def body(h_ref, g_ref, w1_ref, w2_ref, o_ref, a_ref):
        hv = h_ref[...]
        hb = _rms_fwd(hv, g_ref[...])[0].astype(BF16)
        o_ref[...] = hv
        for d in range(N_DEV):
            a = _dot(hb, w1_ref[d])
            a_ref[:, d * FF_SLOT:(d + 1) * FF_SLOT] = a
            r = jnp.maximum(a, 0.0)
            o_ref[...] += _dot((r * r).astype(BF16), w2_ref[d])

    return _call(
        "mlp_fwd", body, (t // TM,), [_row(D_MODEL), _res((1, D_MODEL)), *MLP_W_SPECS],
        [_row(D_MODEL), _row(D_FF)], [_sds((t, D_MODEL), F32), _sds((t, D_FF), F32)], (h, g, w1, w2))


def _mlp_fwd_loss(h, g, w1, w2, final_g, target):
    t = h.shape[0]

    def body(h_ref, g_ref, w1_ref, w2_ref, fg_ref, t_ref, a_ref, loss_ref, dh_ref, dg_ref):
        hv = h_ref[...]
        hb = _rms_fwd(hv, g_ref[...])[0].astype(BF16)
        out = hv
        for d in range(N_DEV):
            a = _dot(hb, w1_ref[d])
            a_ref[:, d * FF_SLOT:(d + 1) * FF_SLOT] = a
            r = jnp.maximum(a, 0.0)
            out = out + _dot((r * r).astype(BF16), w2_ref[d])
        y, xhat, rstd = _rms_fwd(out, fg_ref[...])
        err = y - t_ref[...]
        part = 0.5 * jnp.sum(jnp.mean(err * err, axis=-1, keepdims=True), axis=0, keepdims=True)
        dx, dg = _rms_bwd(err * (1.0 / D_MODEL), xhat, rstd, fg_ref[...])
        dh_ref[...] = dx
        _acc(dg_ref, dg)
        _acc(loss_ref, part)

    return _call(
        "mlp_fwd_loss", body, (t // TM,),
        [_row(D_MODEL), _res((1, D_MODEL)), *MLP_W_SPECS, _res((1, D_MODEL)), _row(D_MODEL)],
        [_row(D_FF), _const((1, 1)), _row(D_MODEL), _const((1, D_MODEL))],
        [_sds((t, D_FF), F32), _sds((1, 1), F32), _sds((t, D_MODEL), F32), _sds((1, D_MODEL), F32)],
        (h, g, w1, w2, final_g, target))


KVQ_W_SPECS = (_res((1, D_MODEL)), _res((D_MODEL, KV_LORA + QK_ROPE)), _res((1, KV_LORA)),
               _res((B_HEADS, KV_LORA, QK_NOPE + V_HEAD)), _res((1, D_MODEL)), _res((D_MODEL, Q_LORA)),
               _res((1, Q_LORA)), _res((B_HEADS, Q_LORA, QK_NOPE + QK_ROPE)))


def _kvq_fwd(h, pos, inv_freq, kvq_w):
    t = h.shape[0]
    half = QK_ROPE // 2

    def body(h_ref, pos_ref, invf_ref, srcg_ref, wkva_ref, kvag_ref, wkvb_ref, mixg_ref, wqa_ref, qg_ref, wqb_ref,
             ckv_ref, k_ref, v_ref, cqpre_ref, q_ref, cos_ref, sin_ref):
        hv = h_ref[...]
        xhat = hv * lax.rsqrt(jnp.mean(hv * hv, axis=-1, keepdims=True) + EPS)
        ang = pos_ref[...].astype(F32) * invf_ref[...]
        cos, sin = jnp.cos(ang), jnp.sin(ang)
        cos_ref[...] = cos
        sin_ref[...] = sin
        ckv = _dot((xhat * srcg_ref[...]).astype(BF16), wkva_ref[...])
        ckv_ref[...] = ckv
        cb = _rms_fwd(ckv[:, :KV_LORA], kvag_ref[...])[0].astype(BF16)
        kpe = _rope(ckv[:, KV_LORA:], cos, sin).astype(BF16)
        for hd in range(B_HEADS):
            kv = _dot(cb, wkvb_ref[hd])
            k_ref[hd, :, 0:QK_NOPE] = kv[:, :QK_NOPE].astype(BF16)
            k_ref[hd, :, QK_NOPE:] = kpe
            v_ref[hd] = kv[:, QK_NOPE:].astype(BF16)
        cqpre = _dot((xhat * mixg_ref[...]).astype(BF16), wqa_ref[...])
        cqpre_ref[...] = cqpre
        cqb = _rms_fwd(cqpre, qg_ref[...])[0].astype(BF16)
        for hd in range(B_HEADS):
            q = _dot(cqb, wqb_ref[hd])
            q_ref[hd, :, 0:QK_NOPE] = q[:, :QK_NOPE].astype(BF16)
            q_ref[hd, :, QK_NOPE:] = _rope(q[:, QK_NOPE:], cos, sin).astype(BF16)

    return _call(
        "kvq_fwd", body, (t // TM,), [_row(D_MODEL), _row(1), _res((1, half)), *KVQ_W_SPECS],
        [_row(KV_LORA + QK_ROPE), _heads(QK_NOPE + QK_ROPE), _heads(V_HEAD), _row(Q_LORA),
         _heads(QK_NOPE + QK_ROPE), _row(half), _row(half)],
        [_sds((t, KV_LORA + QK_ROPE), F32), _sds((B_HEADS, t, QK_NOPE + QK_ROPE), BF16),
         _sds((B_HEADS, t, V_HEAD), BF16), _sds((t, Q_LORA), F32), _sds((B_HEADS, t, QK_NOPE + QK_ROPE), BF16),
         _sds((t, half), F32), _sds((t, half), F32)],
        (h, pos, inv_freq, *kvq_w))


def _softmax_rows(q, k_ref, k):
    past, upto = k * TM, (k + 1) * TM
    s = _dot_nt(q, k_ref[0:upto, :])
    own = jnp.where(_att_mask(0, TM, TM), s[:, past:], jnp.finfo(F32).min)
    s = own if k == 0 else jnp.concatenate([s[:, :past], own], axis=1)
    e = jnp.exp2((s - jnp.max(s, axis=-1, keepdims=True)) * (ATT_SCALE * LOG2_E))
    return e * (1.0 / jnp.sum(e, axis=-1, keepdims=True))


def _for_my_tile(i, nq, fn):
    for k in range(nq):
        @pl.when(i == k)
        def _(k=k):
            fn(k)


def _attn_fwd(h, q, k, v, w_o):
    t = h.shape[0]
    nq, hps = t // TM, HEADS_PER_STEP

    def body(h_ref, q_ref, k_ref, v_ref, wo_ref, o_ref, att_ref):
        i, pair = pl.program_id(0), pl.program_id(1)

        @pl.when(pair == 0)
        def _():
            o_ref[...] = h_ref[...]

        def tile(kt):
            proj = None
            for j in range(hps):
                hd = pair * hps + j
                p = _softmax_rows(q_ref[j], k_ref.at[hd], kt)
                ob = _dot(p.astype(BF16), v_ref[hd, 0:(kt + 1) * TM, :]).astype(BF16)
                att_ref[j] = ob
                proj = _dot(ob, wo_ref[hd]) if proj is None else proj + _dot(ob, wo_ref[hd])
            o_ref[...] += proj

        _for_my_tile(i, nq, tile)

    def per_head(d):
        return pl.BlockSpec((hps, TM, d), lambda i, pair: (pair, i, 0))

    def resident(shape):
        zeros = (0,) * len(shape)
        return pl.BlockSpec(shape, lambda i, pair: zeros, pipeline_mode=pl.Buffered(1))

    tile_spec = pl.BlockSpec((TM, D_MODEL), lambda i, pair: (i, 0))
    return _call(
        "attn_fwd", body, (nq, B_HEADS // hps),
        [tile_spec, per_head(QK_NOPE + QK_ROPE), resident((B_HEADS, t, QK_NOPE + QK_ROPE)),
         resident((B_HEADS, t, V_HEAD)), resident((B_HEADS, V_HEAD, D_MODEL))],
        [tile_spec, per_head(V_HEAD)], [_sds((t, D_MODEL), F32), _sds((B_HEADS, t, V_HEAD), BF16)],
        (h, q, k, v, w_o))


def _mlp_bwd(h, a, dho, g, w1, w2, layer, after=()):
    t = h.shape[0]

    def body(h_ref, a_ref, dho_ref, g_ref, w1_ref, w2_ref, dhi_ref, dg_ref, hn_ref, f_ref, da_ref, dhib_ref):
        gv = g_ref[...]
        y, xhat, rstd = _rms_fwd(h_ref[...], gv)
        hn_ref[...] = y.astype(BF16)
        dho_v = dho_ref[...]
        dhob = dho_v.astype(BF16)
        dhn = jnp.zeros((TM, D_MODEL), F32)
        for d in range(N_DEV):
            cs = slice(d * FF_SLOT, (d + 1) * FF_SLOT)
            r = jnp.maximum(a_ref[:, cs], 0.0)
            f_ref[:, cs] = (r * r).astype(BF16)
            da = (_dot_nt(dhob, w2_ref[d]) * (2.0 * r)).astype(BF16)
            da_ref[:, cs] = da
            dhn = dhn + _dot_nt(da, w1_ref[d])
        dx, dg = _rms_bwd(dhn, xhat, rstd, gv)
        dhi = dho_v + dx
        dhi_ref[...] = dhi
        dhib_ref[...] = dhi.astype(BF16)
        _acc(dg_ref, dg)

    return _call(
        f"mlp_bwd_{layer}", body, (t // TM,),
        [_row(D_MODEL), _row(D_FF), _row(D_MODEL), _res((1, D_MODEL)), *MLP_W_SPECS],
        [_row(D_MODEL), _const((1, D_MODEL)), _row(D_MODEL), _row(D_FF), _row(D_FF), _row(D_MODEL)],
        [_sds((t, D_MODEL), F32), _sds((1, D_MODEL), F32), _sds((t, D_MODEL), BF16), _sds((t, D_FF), BF16),
         _sds((t, D_FF), BF16), _sds((t, D_MODEL), BF16)],
        (h, a, dho, g, w1, w2), after=after)


def _attn_bwd(dh, q, k, v, w_o, cos, sin, after=()):
    t = dh.shape[0]
    half, hps = QK_ROPE // 2, HEADS_PER_STEP

    def body(dh_ref, q_ref, k_ref, v_ref, wo_ref, cos_ref, sin_ref, dq_ref, dk_ref, dv_ref):
        i = pl.program_id(1)

        @pl.when(i == 0)
        def _():
            dk_ref[...] = jnp.zeros_like(dk_ref)
            dv_ref[...] = jnp.zeros_like(dv_ref)

        def tile(kt):
            keys = slice(0, (kt + 1) * TM)
            for j in range(hps):
                qj = q_ref[j]
                do = _dot_nt(dh_ref[kt * TM:(kt + 1) * TM, :], wo_ref[j]).astype(BF16)
                p = _softmax_rows(qj, k_ref.at[j], kt)
                dp = _dot_nt(do, v_ref[j, keys, :])
                ds = (p * (dp - jnp.sum(p * dp, axis=-1, keepdims=True)) * ATT_SCALE).astype(BF16)
                dq = _dot(ds, k_ref[j, keys, :])
                dq_ref[j, :, 0:QK_NOPE] = dq[:, :QK_NOPE].astype(BF16)
                dq_ref[j, :, QK_NOPE:] = _rope(dq[:, QK_NOPE:], cos_ref[...], -sin_ref[...]).astype(BF16)
                dk_ref[j, keys, :] += _dot_tn(ds, qj)
                dv_ref[j, keys, :] += _dot_tn(p.astype(BF16), do)

        _for_my_tile(i, t // TM, tile)

    def per_pair(rows, d, tiled):
        return pl.BlockSpec((hps, rows, d), (lambda pair, i: (pair, i, 0)) if tiled else (lambda pair, i: (pair, 0, 0)))

    def tile(d):
        return pl.BlockSpec((TM, d), lambda pair, i: (i, 0))

    return _call(
        "attn_bwd", body, (B_HEADS // hps, t // TM),
        [pl.BlockSpec((t, D_MODEL), lambda pair, i: (0, 0), pipeline_mode=pl.Buffered(1)),
         per_pair(TM, QK_NOPE + QK_ROPE, True), per_pair(t, QK_NOPE + QK_ROPE, False), per_pair(t, V_HEAD, False),
         per_pair(V_HEAD, D_MODEL, False), tile(half), tile(half)],
        [per_pair(TM, QK_NOPE + QK_ROPE, True), per_pair(t, QK_NOPE + QK_ROPE, False), per_pair(t, V_HEAD, False)],
        [_sds((B_HEADS, t, QK_NOPE + QK_ROPE), BF16), _sds((B_HEADS, t, QK_NOPE + QK_ROPE), F32),
         _sds((B_HEADS, t, V_HEAD), F32)],
        (dh, q, k, v, w_o, cos, sin), after=after)


def _kvq_bwd(h, dh, ckv, cqpre, dq, dk, dv, cos, sin, kvq_w, after=()):
    t = h.shape[0]
    half, last = QK_ROPE // 2, t // TM - 1
    grad_shapes = [(D_MODEL, Q_LORA), (B_HEADS, Q_LORA, QK_NOPE + QK_ROPE), (D_MODEL, KV_LORA + QK_ROPE),
                   (B_HEADS, KV_LORA, QK_NOPE + V_HEAD)]

    def body(h_ref, dh_ref, ckv_ref, cqpre_ref, dq_ref, dk_ref, dv_ref, cos_ref, sin_ref,
             srcg_ref, wkva_ref, kvag_ref, wkvb_ref, mixg_ref, wqa_ref, qg_ref, wqb_ref,
             dhi_ref, dmixg_ref, dsrcg_ref, dqg_ref, dkvag_ref, gqa_ref, gqb_ref, gkva_ref, gkvb_ref,
             aqa, aqb, akva, akvb):
        @pl.when(pl.program_id(0) == 0)
        def _():
            for acc in (aqa, aqb, akva, akvb):
                acc[...] = jnp.zeros_like(acc)

        hv = h_ref[...]
        rstd = lax.rsqrt(jnp.mean(hv * hv, axis=-1, keepdims=True) + EPS)
        xhat = hv * rstd
        mixg, srcg, qg, kvag = mixg_ref[...], srcg_ref[...], qg_ref[...], kvag_ref[...]
        cq, cqhat, crstd = _rms_fwd(cqpre_ref[...], qg)
        cqb = cq.astype(BF16)
        dcq = jnp.zeros((TM, Q_LORA), F32)
        for hd in range(B_HEADS):
            dcq = dcq + _dot_nt(dq_ref[hd], wqb_ref[hd])
            aqb[hd] += _dot_tn(cqb, dq_ref[hd])
        dcqpre, dqg = _rms_bwd(dcq, cqhat, crstd, qg)
        dcqpre_b = dcqpre.astype(BF16)
        aqa[...] += _dot_tn((xhat * mixg).astype(BF16), dcqpre_b)
        dxq, dmixg = _rms_bwd(_dot_nt(dcqpre_b, wqa_ref[...]), xhat, rstd, mixg)
        ckv = ckv_ref[...]
        c, chat, krstd = _rms_fwd(ckv[:, :KV_LORA], kvag)
        cb = c.astype(BF16)
        dc = jnp.zeros((TM, KV_LORA), F32)
        dkpe = jnp.zeros((TM, QK_ROPE), F32)
        for hd in range(B_HEADS):
            dkv = jnp.concatenate([dk_ref[hd, :, 0:QK_NOPE], dv_ref[hd]], axis=-1).astype(BF16)
            akvb[hd] += _dot_tn(cb, dkv)
            dc = dc + _dot_nt(dkv, wkvb_ref[hd])
            dkpe = dkpe + dk_ref[hd, :, QK_NOPE:]
        dlat, dkvag = _rms_bwd(dc, chat, krstd, kvag)
        dpe = _rope(dkpe, cos_ref[...], -sin_ref[...])
        dckv_b = jnp.concatenate([dlat, dpe], axis=-1).astype(BF16)
        akva[...] += _dot_tn((xhat * srcg).astype(BF16), dckv_b)
        dxk, dsrcg = _rms_bwd(_dot_nt(dckv_b, wkva_ref[...]), xhat, rstd, srcg)
        dhi_ref[...] = dh_ref[...] + dxq + dxk
        _acc(dmixg_ref, dmixg)
        _acc(dsrcg_ref, dsrcg)
        _acc(dqg_ref, dqg)
        _acc(dkvag_ref, dkvag)

        @pl.when(pl.program_id(0) == last)
        def _():
            for out, acc in ((gqa_ref, aqa), (gqb_ref, aqb), (gkva_ref, akva), (gkvb_ref, akvb)):
                out[...] = acc[...].astype(BF16)

    return _call(
        "kvq_bwd", body, (t // TM,),
        [_row(D_MODEL), _row(D_MODEL), _row(KV_LORA + QK_ROPE), _row(Q_LORA), _heads(QK_NOPE + QK_ROPE),
         _heads(QK_NOPE + QK_ROPE), _heads(V_HEAD), _row(half), _row(half), *KVQ_W_SPECS],
        [_row(D_MODEL), _const((1, D_MODEL)), _const((1, D_MODEL)), _const((1, Q_LORA)), _const((1, KV_LORA))]
        + [_const(s) for s in grad_shapes],
        [_sds((t, D_MODEL), F32), _sds((1, D_MODEL), F32), _sds((1, D_MODEL), F32), _sds((1, Q_LORA), F32),
         _sds((1, KV_LORA), F32)] + [_sds(s, BF16) for s in grad_shapes],
        (h, dh, ckv, cqpre, dq, dk, dv, cos, sin, *kvq_w), scratch=[pltpu.VMEM(s, F32) for s in grad_shapes],
        after=after)


def _a_mix_bwd(x, z, dh, g, w_in, ln_g, ln_b, w_s, b_st, w_out, after=()):
    t = x.shape[0]
    tm = TM_GATE
    nblk = tm // GMLP_BLOCK

    def body(x_ref, z_ref, dh_ref, g_ref, win_ref, lng_ref, lnb_ref, ws_ref, bst_ref, wout_ref,
             dx_ref, hn_ref, dz_ref, dg_ref, dlng_ref, dlnb_ref, dws_ref, dbs_ref, dvn_scr, gelu_grad_v):
        @pl.when(pl.program_id(0) == 0)
        def _():
            dws_ref[...] = jnp.zeros_like(dws_ref)
            dbs_ref[...] = jnp.zeros_like(dbs_ref)

        gv, lng = g_ref[...], lng_ref[...]
        y, xhat, rstd = _rms_fwd(x_ref[...], gv)
        hn_ref[...] = y.astype(BF16)
        dhv = dh_ref[...]
        dgated = _dot_nt(dhv.astype(BF16), wout_ref[...])
        u, gelu_grad_u = _gelu_and_grad(z_ref[:, :GATE_DIM])
        v, gelu_grad_v[...] = _gelu_and_grad(z_ref[:, GATE_DIM:])
        vn, vhat, lrstd = _ln_fwd(v, lng, lnb_ref[...])
        vb = vn.astype(BF16)
        mask = _gate_mask()
        for gi in range(A_GROUPS):
            wm = jnp.where(mask, ws_ref[gi], 0.0).astype(BF16)
            bias = bst_ref[:, gi:gi + 1]
            cs = slice(gi * A_GROUP_DIM, (gi + 1) * A_GROUP_DIM)
            dws = jnp.zeros((GMLP_BLOCK, GMLP_BLOCK), F32)
            dbs = jnp.zeros((GMLP_BLOCK, 1), F32)
            for n in range(nblk):
                rs = slice(n * GMLP_BLOCK, (n + 1) * GMLP_BLOCK)
                sv = _dot(wm, vb[rs, cs]) + bias
                dz_ref[rs, cs] = (dgated[rs, cs] * sv * gelu_grad_u[rs, cs]).astype(BF16)
                dsv = dgated[rs, cs] * u[rs, cs]
                dsvb = dsv.astype(BF16)
                dws = dws + _dot_nt(dsvb, vb[rs, cs])
                dbs = dbs + jnp.sum(dsv, axis=-1, keepdims=True)
                dvn_scr[rs, cs] = _dot_tn(wm, dsvb)
            dws_ref[gi] += jnp.where(mask, dws, 0.0)
            dbs_ref[gi] += dbs
        dvn = dvn_scr[...]
        dvhat = dvn * lng
        dv = lrstd * (dvhat - jnp.mean(dvhat, axis=-1, keepdims=True)
                      - vhat * jnp.mean(dvhat * vhat, axis=-1, keepdims=True))
        dz_ref[:, GATE_DIM:] = (dv * gelu_grad_v[...]).astype(BF16)
        dhn = jnp.zeros((tm, D_MODEL), F32)
        for d in range(N_DEV):
            dhn = dhn + _dot_nt(dz_ref[:, d * FF_SLOT:(d + 1) * FF_SLOT], win_ref[d])
        dx, dg = _rms_bwd(dhn, xhat, rstd, gv)
        dx_ref[...] = dhv + dx
        _acc(dg_ref, dg)
        _acc(dlng_ref, jnp.sum(dvn * vhat, axis=0, keepdims=True))
        _acc(dlnb_ref, jnp.sum(dvn, axis=0, keepdims=True))

    return _call(
        "a_mix_bwd", body, (t // tm,),
        [_row(D_MODEL, tm), _row(2 * GATE_DIM, tm), _row(D_MODEL, tm), _res((1, D_MODEL)),
         _res((N_DEV, D_MODEL, FF_SLOT)), _res((1, GATE_DIM)), _res((1, GATE_DIM)),
         _res((A_GROUPS, GMLP_BLOCK, GMLP_BLOCK)), _res((GMLP_BLOCK, A_GROUPS)), _res((GATE_DIM, D_MODEL))],
        [_row(D_MODEL, tm), _row(D_MODEL, tm), _row(2 * GATE_DIM, tm),
         _const((1, D_MODEL)), _const((1, GATE_DIM)), _const((1, GATE_DIM)),
         _const((A_GROUPS, GMLP_BLOCK, GMLP_BLOCK)), _const((A_GROUPS, GMLP_BLOCK, 1))],
        [_sds((t, D_MODEL), F32), _sds((t, D_MODEL), BF16),
         _sds((t, 2 * GATE_DIM), BF16), _sds((1, D_MODEL), F32), _sds((1, GATE_DIM), F32),
         _sds((1, GATE_DIM), F32), _sds((A_GROUPS, GMLP_BLOCK, GMLP_BLOCK), F32),
         _sds((A_GROUPS, GMLP_BLOCK, 1), F32)],
        (x, z, dh, g, w_in, ln_g, ln_b, w_s, b_st, w_out),
        scratch=[pltpu.VMEM((tm, GATE_DIM), F32), pltpu.VMEM((tm, GATE_DIM), F32)], after=after)


def _wgrad(name, a, b, a_spec, b_spec, m, n, after=()):
    def body(a_ref, b_ref, o_ref):
        o_ref[0] = _dot_tn(a_ref[...].astype(BF16), b_ref[...].astype(BF16)).astype(BF16)

    return _call(name, body, (N_DEV,), [a_spec, b_spec], [pl.BlockSpec((1, m, n), lambda d: (d, 0, 0))],
                 [_sds((N_DEV, m, n), BF16)], (a, b), after=after)[0]


def _full(t, d):
    return pl.BlockSpec((t, d), lambda i: (0, 0), pipeline_mode=pl.Buffered(1))


def _cols(t, d):
    return pl.BlockSpec((t, d), lambda i: (0, i))


def _head(t, d):
    return pl.BlockSpec((None, t, d), lambda i: (i, 0, 0))


def _local_step(x, pos, target, inv_freq, wg, sm, shards=None):
    t = x.shape[0]
    wg = dict(wg)
    dist = shards is not None
    mix_g = [sm["norm_mix_g"][l:l + 1] for l in range(2)]
    mlp_g = [sm["norm_mlp_g"][l:l + 1] for l in range(2)]

    ids = iter(range(2, 2 + 9))

    def gather(names):
        if dist:
            got = _by_sequencer("gather_" + names[0], _gather_comm([shards[k] for k in names]),
                                SIBLING_AND_NEIGHBOURS, next(ids))
            wg.update(zip(names, got))

    def send(name, names):
        if dist:
            comm = _exchange_comm(grads=[g[k] for k in names])
            g.update(zip(names, _by_sequencer("exchange_" + name, comm, EVERYONE, next(ids))))

    def send_sums(name, names, meanwhile):
        if not dist:
            meanwhile()
            return ()
        grads = [g[k] for k in names]
        landed = _by_sequencer("pair_exchange_" + name, _pair_exchange_comm(grads), (1,), next(ids))
        sums = _pair_add("pair_add_" + name, grads, landed, after=meanwhile())
        g.update(zip(names, _by_sequencer("exchange_" + name, _chip_exchange_comm(sums), OTHER_CHIPS, next(ids))))
        return sums

    def a_args():
        return (wg["a_w_in"], wg["a_ln_v_g"], wg["a_ln_v_b"], sm["a_w_s"], sm["a_b_st"], wg["a_w_out"])

    def kvq_w():
        return (sm["kv_src_norm_g"], wg["kv_w_a"], sm["kv_a_norm_g"], wg["kv_w_b"], mix_g[1], wg["b_w_q_a"],
                sm["b_q_norm_g"], wg["b_w_q_b"])

    gather(("mlp_w1_0", "mlp_w2_0"))
    h1, z, gated = _a_mix_fwd(x, mix_g[0], *a_args())
    gather(("kv_w_a", "kv_w_b", "b_w_q_a", "b_w_q_b", "b_w_o"))
    h2, a0 = _mlp_fwd(h1, mlp_g[0], wg["mlp_w1_0"], wg["mlp_w2_0"])
    if dist:
        wg["b_w_q_a"] = wg["b_w_q_a"].reshape(D_MODEL, Q_LORA)
        wg["kv_w_a"] = wg["kv_w_a"].reshape(D_MODEL, KV_LORA + QK_ROPE)
    gather(("mlp_w1_1", "mlp_w2_1"))
    ckv, k, v, cqpre, q, cos, sin = _kvq_fwd(h2, pos, inv_freq, kvq_w())
    h3, att = _attn_fwd(h2, q, k, v, wg["b_w_o"])
    a1, loss, dh4, d_final_g = _mlp_fwd_loss(h3, mlp_g[1], wg["mlp_w1_1"], wg["mlp_w2_1"], sm["final_norm_g"], target)

    g = {}
    dh3, d_mlp_g1, hn, f, da, dh3_b = _mlp_bwd(h3, a1, dh4, mlp_g[1], wg["mlp_w1_1"], wg["mlp_w2_1"], 1)
    dq, dk, dv = _attn_bwd(dh3_b, q, k, v, wg["b_w_o"], cos, sin)
    g["mlp_w1_1"] = _wgrad("wgrad_w1_1", hn, da, _full(t, D_MODEL), _cols(t, FF_SLOT), D_MODEL, FF_SLOT, after=[dq])
    g["mlp_w2_1"] = _wgrad("wgrad_w2_1", f, dh4, _cols(t, FF_SLOT), _full(t, D_MODEL), FF_SLOT, D_MODEL)

    def wgrad_w_o():
        g["b_w_o"] = _wgrad("wgrad_w_o", att, dh3_b, _head(t, V_HEAD), _full(t, D_MODEL), V_HEAD, D_MODEL)
        return [g["b_w_o"]]

    sums = send_sums("mlp_1", ("mlp_w1_1", "mlp_w2_1"), wgrad_w_o)
    dh2, d_mix_g1, d_src_g, d_q_g, d_kv_a_g, g_q_a, g["b_w_q_b"], g_kv_a, g["kv_w_b"] = _kvq_bwd(
        h2, dh3, ckv, cqpre, dq, dk, dv, cos, sin, kvq_w(), after=sums)
    g["b_w_q_a"] = g_q_a.reshape(N_DEV, D_MODEL // N_DEV, Q_LORA)
    g["kv_w_a"] = g_kv_a.reshape(N_DEV, D_MODEL // N_DEV, KV_LORA + QK_ROPE)
    qkv = ("b_w_q_a", "b_w_q_b", "kv_w_a", "kv_w_b")
    landed = [g[k] for k in qkv]
    send("qkv", qkv)
    dh1, d_mlp_g0, hn, f, da, dh1_b = _mlp_bwd(h1, a0, dh2, mlp_g[0], wg["mlp_w1_0"], wg["mlp_w2_0"], 0,
                                               after=landed if dist else ())
    landed = [g["mlp_w1_1"], g["mlp_w2_1"]] if dist else ()
    g["mlp_w1_0"] = _wgrad("wgrad_w1_0", hn, da, _full(t, D_MODEL), _cols(t, FF_SLOT), D_MODEL, FF_SLOT, after=landed)
    g["mlp_w2_0"] = _wgrad("wgrad_w2_0", f, dh2, _cols(t, FF_SLOT), _full(t, D_MODEL), FF_SLOT, D_MODEL)

    def wgrad_a_w_out():
        g["a_w_out"] = _wgrad("wgrad_a_w_out", gated, dh1_b, _cols(t, GATE_DIM // N_DEV), _full(t, D_MODEL),
                              GATE_DIM // N_DEV, D_MODEL)
        return [g["a_w_out"]] + [g[k] for k in qkv]

    sums = send_sums("mlp_0", ("mlp_w1_0", "mlp_w2_0", "b_w_o"), wgrad_a_w_out)
    dx, hn, dz, d_mix_g0, d_ln_g, d_ln_b, d_ws, d_bs = _a_mix_bwd(x, z, dh1, mix_g[0], *a_args(), after=sums)
    small = {
        "norm_mix_g": jnp.concatenate([d_mix_g0, d_mix_g1], axis=0),
        "norm_mlp_g": jnp.concatenate([d_mlp_g0, d_mlp_g1], axis=0),
        "a_ln_v_g": d_ln_g.reshape(N_DEV, GATE_DIM // N_DEV),
        "a_ln_v_b": d_ln_b.reshape(N_DEV, GATE_DIM // N_DEV),
        "a_w_s": d_ws.astype(BF16) if dist else d_ws,
        "a_b_s": d_bs.reshape(A_GROUPS, GMLP_BLOCK),
        "b_q_norm_g": d_q_g,
        "kv_src_norm_g": d_src_g,
        "kv_a_norm_g": d_kv_a_g,
        "final_norm_g": d_final_g,
    }
    if dist:
        parts = [small[k].reshape((1,) + small[k].shape) for k in SMALL] + [loss.reshape(1, 1, 1)]
        got = _by_sequencer("gather_small", _gather_comm(parts), SIBLING_AND_NEIGHBOURS, next(ids))
        small, loss = dict(zip(SMALL, got)), got[-1]
    g["a_w_in"] = _wgrad("wgrad_a_w_in", hn, dz, _full(t, D_MODEL), _cols(t, FF_SLOT), D_MODEL, FF_SLOT)
    return loss, dx, g, small


def _adamw(w, g, m, v):
    m = ADAM_B1 * m + (1.0 - ADAM_B1) * g
    v = ADAM_B2 * v + (1.0 - ADAM_B2) * (g * g)
    m_hat = m / (1.0 - ADAM_B1 ** ADAM_STEP)
    v_hat = v / (1.0 - ADAM_B2 ** ADAM_STEP)
    return -ADAM_LR * (m_hat / (jnp.sqrt(v_hat) + ADAM_EPS) + ADAM_WD * w), m, v


def _sum_in_device_order(r_ref):
    g = r_ref[0].astype(F32)
    for j in range(1, r_ref.shape[0]):
        g = g + r_ref[j].astype(F32)
    return g


def _adamw_sharded(name, recvs, w, m, v):
    layers, r, c = w.shape
    tr = math.gcd(r, 512)
    flat = [a for per_layer in recvs for a in per_layer]

    def body(*refs):
        r_refs, (w_ref, m_ref, v_ref) = refs[:len(flat)], refs[len(flat):len(flat) + 3]
        g_ref, d_ref, nm_ref, nv_ref = refs[-4:]
        layer = pl.program_id(0)
        g, pos = None, 0
        for li, per_layer in enumerate(recvs):
            total = None
            for ref in r_refs[pos:pos + len(per_layer)]:
                part = _sum_in_device_order(ref)
                total = part if total is None else total + part
            pos += len(per_layer)
            g = total if g is None else jnp.where(layer == li, total, g)
        g_ref[...] = g
        d_ref[...], nm_ref[...], nv_ref[...] = _adamw(w_ref[...], g, m_ref[...], v_ref[...])

    blk = pl.BlockSpec((None, tr, c), lambda l, i: (l, i, 0))
    return _call(name, body, (layers, r // tr),
                 [pl.BlockSpec((a.shape[0], tr, c), lambda l, i: (0, i, 0)) for a in flat] + [blk] * 3,
                 [blk] * 4, [_sds(w.shape, F32)] * 4, (*flat, w, m, v))


def _adamw_small(recvs, ws, ms, vs, own_row, losses):
    n = len(recvs)

    def body(*refs):
        r_refs, w_refs, m_refs, v_refs = (refs[i * n:(i + 1) * n] for i in range(4))
        outs, scr = refs[4 * n + 1:8 * n + 2], refs[8 * n + 2:]
        outs[-1][...] = _sum_in_device_order(refs[4 * n])
        me = _my_place()[3]
        for a in range(n):
            g = _sum_in_device_order(r_refs[a])
            if own_row[a]:
                scr[0][...] = g
                g = scr[0][pl.ds(me, 1), :]
            g_ref, d_ref, nm_ref, nv_ref = outs[4 * a:4 * a + 4]
            g_ref[...] = g
            d_ref[...], nm_ref[...], nv_ref[...] = _adamw(w_refs[a][...], g, m_refs[a][...], v_refs[a][...])

    out_shape = []
    for w in ws:
        out_shape += [_sds(w.shape, F32)] * 4
    return pl.pallas_call(
        body, name="adamw_small", in_specs=[VMEM] * (4 * n + 1), out_specs=[VMEM] * (4 * n + 1),
        out_shape=out_shape + [_sds((1, 1), F32)], scratch_shapes=[pltpu.VMEM((N_DEV, GATE_DIM // N_DEV), F32)],
    )(*recvs, *ws, *ms, *vs, losses)


BIG = ("a_w_in", "a_w_out", "b_w_q_a", "b_w_q_b", "b_w_o", "kv_w_a", "kv_w_b", "mlp_w1", "mlp_w2")
SMALL = ("norm_mix_g", "norm_mlp_g", "a_ln_v_g", "a_ln_v_b", "a_w_s", "a_b_s", "b_q_norm_g", "kv_src_norm_g",
         "kv_a_norm_g", "final_norm_g")
WEIGHTS = ("norm_mix_g", "norm_mlp_g", "a_w_in", "a_ln_v_g", "a_ln_v_b", "a_w_s", "a_b_s", "a_w_out", "b_w_q_a",
           "b_q_norm_g", "b_w_q_b", "b_w_o", "kv_src_norm_g", "kv_w_a", "kv_a_norm_g", "kv_w_b", "mlp_w1", "mlp_w2",
           "final_norm_g")


def _two_d(name, a):
    if name in ("a_w_s", "a_b_s"):
        return a.reshape(a.shape[1:])
    return a.reshape(1, -1) if a.ndim == 1 else a


def _three_d(a):
    return a if a.ndim == 3 else a.reshape((1,) + a.shape)


def kernel(x, positions, norm_mix_g, norm_mlp_g, a_w_in, a_ln_v_g, a_ln_v_b, a_w_s, a_b_s, a_w_out, b_w_q_a, b_q_norm_g, b_w_q_b, b_w_o, kv_src_norm_g, kv_w_a, kv_a_norm_g, kv_w_b, mlp_w1, mlp_w2, final_norm_g, loss_target, m_norm_mix_g, m_norm_mlp_g, m_a_w_in, m_a_ln_v_g, m_a_ln_v_b, m_a_w_s, m_a_b_s, m_a_w_out, m_b_w_q_a, m_b_q_norm_g, m_b_w_q_b, m_b_w_o, m_kv_src_norm_g, m_kv_w_a, m_kv_a_norm_g, m_kv_w_b, m_mlp_w1, m_mlp_w2, m_final_norm_g, v_norm_mix_g, v_norm_mlp_g, v_a_w_in, v_a_ln_v_g, v_a_ln_v_b, v_a_w_s, v_a_b_s, v_a_w_out, v_b_w_q_a, v_b_q_norm_g, v_b_w_q_b, v_b_w_o, v_kv_src_norm_g, v_kv_w_a, v_kv_a_norm_g, v_kv_w_b, v_mlp_w1, v_mlp_w2, v_final_norm_g):
    w = dict(norm_mix_g=norm_mix_g, norm_mlp_g=norm_mlp_g, a_w_in=a_w_in, a_ln_v_g=a_ln_v_g, a_ln_v_b=a_ln_v_b,
             a_w_s=a_w_s, a_b_s=a_b_s, a_w_out=a_w_out, b_w_q_a=b_w_q_a, b_q_norm_g=b_q_norm_g, b_w_q_b=b_w_q_b,
             b_w_o=b_w_o, kv_src_norm_g=kv_src_norm_g, kv_w_a=kv_w_a, kv_a_norm_g=kv_a_norm_g, kv_w_b=kv_w_b,
             mlp_w1=mlp_w1, mlp_w2=mlp_w2, final_norm_g=final_norm_g)
    m = dict(norm_mix_g=m_norm_mix_g, norm_mlp_g=m_norm_mlp_g, a_w_in=m_a_w_in, a_ln_v_g=m_a_ln_v_g,
             a_ln_v_b=m_a_ln_v_b, a_w_s=m_a_w_s, a_b_s=m_a_b_s, a_w_out=m_a_w_out, b_w_q_a=m_b_w_q_a,
             b_q_norm_g=m_b_q_norm_g, b_w_q_b=m_b_w_q_b, b_w_o=m_b_w_o, kv_src_norm_g=m_kv_src_norm_g,
             kv_w_a=m_kv_w_a, kv_a_norm_g=m_kv_a_norm_g, kv_w_b=m_kv_w_b, mlp_w1=m_mlp_w1, mlp_w2=m_mlp_w2,
             final_norm_g=m_final_norm_g)
    v = dict(norm_mix_g=v_norm_mix_g, norm_mlp_g=v_norm_mlp_g, a_w_in=v_a_w_in, a_ln_v_g=v_a_ln_v_g,
             a_ln_v_b=v_a_ln_v_b, a_w_s=v_a_w_s, a_b_s=v_a_b_s, a_w_out=v_a_w_out, b_w_q_a=v_b_w_q_a,
             b_q_norm_g=v_b_q_norm_g, b_w_q_b=v_b_w_q_b, b_w_o=v_b_w_o, kv_src_norm_g=v_kv_src_norm_g,
             kv_w_a=v_kv_w_a, kv_a_norm_g=v_kv_a_norm_g, kv_w_b=v_kv_w_b, mlp_w1=v_mlp_w1, mlp_w2=v_mlp_w2,
             final_norm_g=v_final_norm_g)
    t = x.shape[1]

    first = ("a_w_in", "a_w_out", "a_ln_v_g", "a_ln_v_b")
    later = ("mlp_w1", "mlp_w2", "kv_w_a", "kv_w_b", "b_w_q_a", "b_w_q_b", "b_w_o")
    later_blocks = ("mlp_w1_0", "mlp_w1_1", "mlp_w2_0", "mlp_w2_1") + later[2:]
    got, casts = _gather_first([_three_d(w[k]) if k in BIG else w[k] for k in first], [_three_d(w[k]) for k in later])
    wg = dict(zip(first, got))
    wg["a_w_out"] = wg["a_w_out"].reshape(GATE_DIM, D_MODEL)
    wg["a_ln_v_g"] = wg["a_ln_v_g"].reshape(1, GATE_DIM)
    wg["a_ln_v_b"] = wg["a_ln_v_b"].reshape(1, GATE_DIM)
    shards = dict(zip(later_blocks, casts))

    sm = {k: _two_d(k, w[k]) for k in SMALL if k not in ("a_ln_v_g", "a_ln_v_b")}
    sm["a_b_st"] = sm["a_b_s"].T
    inv_freq = (ROPE_THETA ** (-jnp.arange(0, QK_ROPE, 2, dtype=F32) / QK_ROPE)).reshape(1, QK_ROPE // 2)

    losses, dx, g, small = _local_step(x[0], positions.reshape(t, 1), loss_target[0], inv_freq, wg, sm, shards)

    names = ("a_w_in", "a_w_out")
    sums = _pair_reduce("pair_reduce_a", [g[k] for k in names], after=[g["mlp_w1_0"], g["mlp_w2_0"]])
    g.update(zip(names, _by_sequencer("exchange_last", _chip_exchange_comm(sums), OTHER_CHIPS, collective_id=1)))

    out = {}
    for k in BIG:
        recvs = [[g[k + "_0"]], [g[k + "_1"]]] if k.startswith("mlp") else [[g[k]]]
        res = _adamw_sharded("adamw_" + k, recvs, _three_d(w[k]), _three_d(m[k]), _three_d(v[k]))
        out[k] = [o.reshape(w[k].shape) for o in res]
    own_row = [k in ("a_ln_v_g", "a_ln_v_b") for k in SMALL]
    res = _adamw_small([small[k] for k in SMALL], [_two_d(k, w[k]) for k in SMALL], [_two_d(k, m[k]) for k in SMALL],
                       [_two_d(k, v[k]) for k in SMALL], own_row, losses)
    for i, k in enumerate(SMALL):
        out[k] = [o.reshape(w[k].shape) for o in res[4 * i:4 * i + 4]]

    return (res[-1].reshape(()), dx.reshape(x.shape), *[out[k][0] for k in WEIGHTS], *[out[k][1] for k in WEIGHTS],
            *[out[k][2] for k in WEIGHTS], *[out[k][3] for k in WEIGHTS])
```

```python
import math

import jax
import jax.numpy as jnp
from jax import lax
from jax.experimental import pallas as pl
from jax.experimental.pallas import tpu as pltpu
from jax.experimental.pallas import tpu_sc as plsc

F32, BF16 = jnp.float32, jnp.bfloat16
MESH = pl.DeviceIdType.MESH
ANY = pl.BlockSpec(memory_space=pl.ANY)
VMEM = pl.BlockSpec(memory_space=pltpu.VMEM)

N_DEV = 8
D_MODEL = 1024
CHUNK = 64
GMLP_BLOCK = 128
GATE_DIM = 2048
A_GROUPS = 8
A_GROUP_DIM = GATE_DIM // A_GROUPS
B_HEADS = 8
QK_NOPE, QK_ROPE, V_HEAD = 128, 64, 128
Q_LORA, KV_LORA = 384, 256
ROPE_THETA = 10000.0
D_FF = 4096
FF_SLOT = D_FF // N_DEV
EPS = 1e-6
ATT_SCALE = (QK_NOPE + QK_ROPE) ** -0.5

ADAM_LR, ADAM_B1, ADAM_B2, ADAM_EPS, ADAM_WD, ADAM_STEP = 0.001, 0.9, 0.999, 1e-08, 0.01, 10

TM = 256
TM_GATE = 256
TM_MLP_FWD = 512
VMEM_LIMIT = 56 * 1024 * 1024
INV_SQRT2 = 1.0 / math.sqrt(2.0)
INV_SQRT_2PI = 1.0 / math.sqrt(2.0 * math.pi)
LOG2_E = 1.0 / math.log(2.0)
HEADS_PER_STEP = 2


def _dot(a, b):
    return jnp.dot(a, b, preferred_element_type=F32)


def _dot_nt(a, b):
    return lax.dot_general(a, b, (((1,), (1,)), ((), ())), preferred_element_type=F32)


def _dot_tn(a, b):
    return lax.dot_general(a, b, (((0,), (0,)), ((), ())), preferred_element_type=F32)


def _rms_fwd(x, g):
    rstd = lax.rsqrt(jnp.mean(x * x, axis=-1, keepdims=True) + EPS)
    xhat = x * rstd
    return xhat * g, xhat, rstd


def _rms_bwd(dy, xhat, rstd, g):
    dxhat = dy * g
    dx = rstd * (dxhat - xhat * jnp.mean(dxhat * xhat, axis=-1, keepdims=True))
    return dx, jnp.sum(dy * xhat, axis=0, keepdims=True)


def _ln_fwd(v, g, b):
    mu = jnp.mean(v, axis=-1, keepdims=True)
    vc = v - mu
    rstd = lax.rsqrt(jnp.mean(vc * vc, axis=-1, keepdims=True) + EPS)
    vhat = vc * rstd
    return vhat * g + b, vhat, rstd


def _gelu(x):
    return 0.5 * x * (1.0 + lax.erf(x * INV_SQRT2))


def _gelu_and_grad(x):
    cdf = 0.5 * (1.0 + lax.erf(x * INV_SQRT2))
    return x * cdf, cdf + x * jnp.exp(-0.5 * x * x) * INV_SQRT_2PI


def _rope(x, cos, sin):
    x1, x2 = x[:, :QK_ROPE // 2], x[:, QK_ROPE // 2:]
    return jnp.concatenate([x1 * cos - x2 * sin, x2 * cos + x1 * sin], axis=-1)


def _gate_mask():
    row = lax.broadcasted_iota(jnp.int32, (GMLP_BLOCK, GMLP_BLOCK), 0)
    col = lax.broadcasted_iota(jnp.int32, (GMLP_BLOCK, GMLP_BLOCK), 1)
    return (col < CHUNK) | (row >= CHUNK)


def _att_mask(q0, tq, t):
    q = q0 + lax.broadcasted_iota(jnp.int32, (tq, t), 0)
    k = lax.broadcasted_iota(jnp.int32, (tq, t), 1)
    return jnp.right_shift(k, 6) <= jnp.right_shift(q, 6)


def _res(shape, imap=None):
    zeros = (0,) * len(shape)
    return pl.BlockSpec(shape, imap or (lambda i: zeros), pipeline_mode=pl.Buffered(1))


def _const(shape):
    zeros = (0,) * len(shape)
    return pl.BlockSpec(shape, lambda i: zeros)


def _row(d, tm=TM):
    return pl.BlockSpec((tm, d), lambda i: (i, 0))


def _heads(d):
    return pl.BlockSpec((B_HEADS, TM, d), lambda i: (0, i, 0))


def _sds(shape, dt):
    return jax.ShapeDtypeStruct(shape, dt)


def _acc(ref, val):
    @pl.when(pl.program_id(0) == 0)
    def _():
        ref[...] = jnp.zeros_like(ref)
    ref[...] += val


def _my_place():
    x, y, c = lax.axis_index("x"), lax.axis_index("y"), lax.axis_index("c")
    return x, y, c, 4 * x + 2 * y + c


def _peer(x, y, c, k):
    px = 1 - x if k & 4 else x
    py = 1 - y if k & 2 else y
    pc = 1 - c if k & 1 else c
    return (px, py, pc), 4 * px + 2 * py + pc


CHIPS = (2, 4, 6)


def _splits(ref):
    return len(ref.shape) >= 3 and ref.shape[1] % 32 == 0


def _piece(ref, block, half=None):
    if half is None or not _splits(ref):
        return ref.at[pl.ds(block, 1)]
    rows = ref.shape[1] // 2
    return ref.at[pl.ds(block, 1), pl.ds(half * rows, rows)]


def _gather_copy(sems, a, k, piece, to, src=None):
    return pltpu.make_async_remote_copy(
        src_ref=piece if src is None else src, dst_ref=piece, send_sem=sems[0].at[a, k], recv_sem=sems[1].at[a, k],
        device_id=to, device_id_type=MESH)


def _gather_start(srcs, outs, sems, only=None):
    x, y, c, me = _my_place()
    for a in range(len(srcs)) if only is None else (only,):
        mine = _piece(outs[a], me)
        pltpu.make_async_copy(srcs[a], mine, sems[2].at[a]).start()
        for k, rel in enumerate((1, 4, 2)):
            _gather_copy(sems, a, k, mine, _peer(x, y, c, rel)[0], src=srcs[a]).start()


def _gather_relay(srcs, outs, sems):
    x, y, c, _ = _my_place()
    sib = _peer(x, y, c, 1)[0]
    (xn, xn_i), (yn, yn_i) = _peer(x, y, c, 4), _peer(x, y, c, 2)
    for a in range(len(srcs)):
        out = outs[a]
        _gather_copy(sems, a, 1, _piece(out, xn_i), xn).wait_recv()
        _gather_copy(sems, a, 3, _piece(out, xn_i, 0), yn).start()
        _gather_copy(sems, a, 5, _piece(out, xn_i), sib).start()
        _gather_copy(sems, a, 2, _piece(out, yn_i), yn).wait_recv()
        if _splits(out):
            _gather_copy(sems, a, 4, _piece(out, yn_i, 1), xn).start()
        _gather_copy(sems, a, 6, _piece(out, yn_i), sib).start()


def _gather_finish(srcs, outs, sems):
    x, y, c, me = _my_place()
    sib = _peer(x, y, c, 1)[0]
    xn, yn, dg_i = _peer(x, y, c, 4)[0], _peer(x, y, c, 2)[0], _peer(x, y, c, 6)[1]
    n = len(srcs)
    for a in range(n):
        out = outs[a]
        _gather_copy(sems, a, 3, _piece(out, dg_i, 0), yn).wait_recv()
        _gather_copy(sems, a, 7, _piece(out, dg_i, 0), sib).start()
        if _splits(out):
            _gather_copy(sems, a, 4, _piece(out, dg_i, 1), xn).wait_recv()
            _gather_copy(sems, a, 8, _piece(out, dg_i, 1), sib).start()
    for a in range(n):
        out = outs[a]
        whole, half = _piece(out, me), _piece(out, me, 0)
        for k in (0, 5, 6):
            _gather_copy(sems, a, k, whole, sib).wait_recv()
        for k in (7, 8) if _splits(out) else (7,):
            _gather_copy(sems, a, k, half, sib).wait_recv()
        for k in (0, 1, 2):
            _gather_copy(sems, a, k, whole, sib, src=srcs[a]).wait_send()
        for k in (5, 6):
            _gather_copy(sems, a, k, whole, sib).wait_send()
        for k in (3, 4, 7, 8) if _splits(out) else (3, 7):
            _gather_copy(sems, a, k, half, sib).wait_send()
        pltpu.make_async_copy(srcs[a], whole, sems[2].at[a]).wait()


def _relay_sems(n):
    return [pltpu.SemaphoreType.DMA((n, 9)), pltpu.SemaphoreType.DMA((n, 9)), pltpu.SemaphoreType.DMA((n,))]


def _gather_sems(n):
    return [pltpu.SemaphoreType.DMA((n, 7)), pltpu.SemaphoreType.DMA((n, 7)), pltpu.SemaphoreType.DMA((n,))]


class _Comm:
    def __init__(self, args, out_shape, scratch, start, finish, relay=None):
        self.args, self.out_shape, self.scratch, self.start, self.finish = args, out_shape, scratch, start, finish
        self.relay = relay


def _gather_comm(shards):
    return _Comm(list(shards), [_sds((N_DEV,) + s.shape[1:], s.dtype) for s in shards], _relay_sems(len(shards)),
                 _gather_start, _gather_finish, relay=_gather_relay)


def _direct_copies(ins, outs, sems, wait):
    send_sems, recv_sems, local_sems = sems
    x, y, c, me = _my_place()
    for a in range(len(ins)):
        local = pltpu.make_async_copy(ins[a].at[pl.ds(me, 1)], outs[a].at[pl.ds(me, 1)], local_sems.at[a])
        local.wait() if wait else local.start()
        for k in range(1, N_DEV):
            to, to_i = _peer(x, y, c, k)
            cp = pltpu.make_async_remote_copy(
                src_ref=ins[a].at[pl.ds(to_i, 1)], dst_ref=outs[a].at[pl.ds(me, 1)],
                send_sem=send_sems.at[a, k - 1], recv_sem=recv_sems.at[a, k - 1], device_id=to, device_id_type=MESH)
            cp.wait() if wait else cp.start()


def _exchange_comm(grads):
    return _Comm(list(grads), [_sds(g.shape, g.dtype) for g in grads], _gather_sems(len(grads)),
                 lambda i, o, s: _direct_copies(i, o, s, False), lambda i, o, s: _direct_copies(i, o, s, True))


def _chip_copies(ins, outs, sems, wait):
    send_sems, recv_sems, local_sems = sems
    x, y, c, _ = _my_place()
    for a in range(len(ins)):
        local = pltpu.make_async_copy(ins[a].at[pl.ds(2 * x + y, 1)], outs[a].at[pl.ds(len(CHIPS), 1)],
                                      local_sems.at[a])
        local.wait() if wait else local.start()
        for i, k in enumerate(CHIPS):
            to = _peer(x, y, c, k)[0]
            cp = pltpu.make_async_remote_copy(
                src_ref=ins[a].at[pl.ds(2 * to[0] + to[1], 1)], dst_ref=outs[a].at[pl.ds(i, 1)],
                send_sem=send_sems.at[a, i], recv_sem=recv_sems.at[a, i], device_id=to, device_id_type=MESH)
            cp.wait() if wait else cp.start()


def _chip_exchange_comm(sums):
    n = len(sums)
    sems = [pltpu.SemaphoreType.DMA((n, len(CHIPS))), pltpu.SemaphoreType.DMA((n, len(CHIPS))),
            pltpu.SemaphoreType.DMA((n,))]
    return _Comm(list(sums), [_sds(s.shape, s.dtype) for s in sums], sems,
                 lambda i, o, s: _chip_copies(i, o, s, False), lambda i, o, s: _chip_copies(i, o, s, True))


def _pair_reduce(name, grads, after=()):
    n = len(grads)
    n_chips = N_DEV // 2

    def body(*refs):
        g_refs, gh_refs, refs = refs[:n], refs[n:2 * n], refs[2 * n + len(after):]
        p_refs, land = refs[:n], refs[n:2 * n]
        send_sems, recv_sems = refs[2 * n:]
        x, y, c, _ = _my_place()
        sib = _peer(x, y, c, 1)[0]
        q = pl.program_id(0)

        def to_sibling(a, j):
            return pltpu.make_async_remote_copy(
                src_ref=gh_refs[a].at[j, pl.ds(1 - c, 1)], dst_ref=land[a].at[pl.ds(j, 1)],
                send_sem=send_sems.at[a, j], recv_sem=recv_sems.at[a, j], device_id=sib, device_id_type=MESH)

        @pl.when(q == 0)
        def _():
            for j in range(n_chips):
                for a in range(n):
                    to_sibling(a, j).start()

        for a in range(n):
            to_sibling(a, q).wait_recv()
            p_refs[a][...] = (g_refs[a][0, pl.ds(c, 1)].astype(F32) + land[a][pl.ds(q, 1)].astype(F32)).astype(BF16)

        @pl.when(q == n_chips - 1)
        def _():
            for a in range(n):
                for j in range(n_chips):
                    to_sibling(a, j).wait_send()

    views = [g.reshape((n_chips, 2) + g.shape[1:]) for g in grads]
    res = pl.pallas_call(
        body, name=name, grid=(n_chips,),
        in_specs=[pl.BlockSpec((1, 2) + g.shape[1:], lambda q: (q, 0, 0, 0)) for g in grads]
        + [ANY] * (n + len(after)),
        out_specs=[pl.BlockSpec((1,) + g.shape[1:], lambda q: (q, 0, 0)) for g in grads],
        out_shape=[_sds((n_chips,) + g.shape[1:], BF16) for g in grads],
        scratch_shapes=[pltpu.VMEM((n_chips,) + g.shape[1:], BF16) for g in grads]
        + [pltpu.SemaphoreType.DMA((n, n_chips)), pltpu.SemaphoreType.DMA((n, n_chips))],
        compiler_params=pltpu.CompilerParams(dimension_semantics=("arbitrary",), vmem_limit_bytes=VMEM_LIMIT),
    )(*views, *views, *after)
    return list(res)


def _pair_exchange_comm(grads):
    n, n_chips = len(grads), N_DEV // 2

    def copies(ins, outs, sems, wait):
        x, y, c, _ = _my_place()
        for j in range(n_chips):
            for a in range(n):
                cp = pltpu.make_async_remote_copy(
                    src_ref=ins[a].at[j, pl.ds(1 - c, 1)], dst_ref=outs[a].at[pl.ds(j, 1)], send_sem=sems[0].at[a, j],
                    recv_sem=sems[1].at[a, j], device_id=_peer(x, y, c, 1)[0], device_id_type=MESH)
                cp.wait() if wait else cp.start()

    views = [g.reshape((n_chips, 2) + g.shape[1:]) for g in grads]
    sems = [pltpu.SemaphoreType.DMA((n, n_chips)), pltpu.SemaphoreType.DMA((n, n_chips))]
    return _Comm(views, [_sds((n_chips,) + g.shape[1:], g.dtype) for g in grads], sems,
                 lambda i, o, s: copies(i, o, s, False), lambda i, o, s: copies(i, o, s, True))


def _pair_add(name, grads, landed, after=()):
    n, n_chips = len(grads), N_DEV // 2

    def body(core_ref, *refs):
        g_refs, l_refs, p_refs = refs[:n], refs[n:2 * n], refs[2 * n + len(after):]
        for a in range(n):
            p_refs[a][...] = (g_refs[a][...].astype(F32) + l_refs[a][...].astype(F32)).astype(BF16)

    views = [g.reshape((n_chips, 2) + g.shape[1:]) for g in grads]
    blocks = [pl.BlockSpec((1,) + g.shape[1:], lambda q, core: (q, 0, 0)) for g in grads]
    mine = [pl.BlockSpec((1, None) + g.shape[1:], lambda q, core: (q, core[0], 0, 0)) for g in grads]
    return list(pl.pallas_call(
        body, name=name, out_shape=[_sds((n_chips,) + g.shape[1:], BF16) for g in grads],
        grid_spec=pltpu.PrefetchScalarGridSpec(num_scalar_prefetch=1, grid=(n_chips,),
                                               in_specs=mine + blocks + [ANY] * len(after), out_specs=blocks),
        compiler_params=pltpu.CompilerParams(dimension_semantics=("arbitrary",), vmem_limit_bytes=VMEM_LIMIT),
    )(lax.axis_index("c").reshape(1), *views, *landed, *after))


def _call(name, body, grid, in_specs, out_specs, out_shape, args, scratch=(), after=()):
    ni, na = len(in_specs), len(after)

    def ordered(*refs):
        body(*refs[:ni], *refs[ni + na:])

    return list(pl.pallas_call(
        ordered if after else body, name=name, grid=grid, in_specs=list(in_specs) + [ANY] * na,
        out_specs=list(out_specs), out_shape=list(out_shape), scratch_shapes=list(scratch),
        compiler_params=pltpu.CompilerParams(dimension_semantics=("arbitrary",) * len(grid),
                                             vmem_limit_bytes=VMEM_LIMIT))(*args, *after))


SIBLING_AND_NEIGHBOURS, OTHER_CHIPS, EVERYONE = (1, 4, 2), CHIPS, tuple(range(1, N_DEV))


def _by_sequencer(name, comm, peers, collective_id):
    src = [jax.new_ref(a, memory_space=pltpu.MemorySpace.HBM) for a in comm.args]
    dst = [jax.empty_ref(s, memory_space=pltpu.MemorySpace.HBM) for s in comm.out_shape]

    @pl.kernel(mesh=plsc.ScalarSubcoreMesh(axis_name="sequencer", num_cores=1), name=name,
               scratch_types=tuple(comm.scratch), compiler_params=pltpu.CompilerParams(collective_id=collective_id))
    def launch(*sems):
        x, y, c, _ = _my_place()
        barrier = pltpu.get_barrier_semaphore()
        for k in peers:
            pl.semaphore_signal(barrier, inc=1, device_id=_peer(x, y, c, k)[0], device_id_type=MESH)
        pl.semaphore_wait(barrier, len(peers))
        comm.start(src, dst, sems)
        if comm.relay is not None:
            comm.relay(src, dst, sems)
        comm.finish(src, dst, sems)

    launch()
    return [d[...] for d in dst]


def _gather_first(first, later):
    nf = len(first)
    layer_of = [(a, l) for a, s in enumerate(later) for l in range(s.shape[0])]
    nl = len(layer_of)
    dts = [BF16] * (nf - 2) + [F32, F32]

    def body(*refs):
        ins, refs = refs[:nf + len(later)], refs[nf + len(later):]
        outs, refs = refs[:nf], refs[nf:]
        casts, refs = refs[:nl], refs[nl:]
        stage, sems = refs[:nf], refs[nf:]
        for a in range(nf):
            stage[a][...] = ins[a][...].astype(dts[a])
            _gather_start(stage, outs, sems, only=a)
        for k, (a, l) in enumerate(layer_of):
            casts[k][...] = ins[nf + a][l:l + 1].astype(BF16)
        _gather_relay(stage, outs, sems)
        _gather_finish(stage, outs, sems)

    res = pl.pallas_call(
        body, name="gather_first",
        in_specs=[VMEM] * (nf + len(later)), out_specs=[ANY] * nf + [VMEM] * nl,
        out_shape=[_sds((N_DEV,) + s.shape[1:], dt) for s, dt in zip(first, dts)]
        + [_sds((1,) + later[a].shape[1:], BF16) for a, _ in layer_of],
        scratch_shapes=[pltpu.VMEM(s.shape, dt) for s, dt in zip(first, dts)] + _relay_sems(nf),
        compiler_params=pltpu.CompilerParams(vmem_limit_bytes=VMEM_LIMIT),
    )(*first, *later)
    return list(res[:nf]), list(res[nf:])


def _a_mix_fwd(x, g, w_in, ln_g, ln_b, w_s, b_st, w_out):
    t = x.shape[0]
    nblk = TM // GMLP_BLOCK

    def body(x_ref, g_ref, win_ref, lng_ref, lnb_ref, ws_ref, bst_ref, wout_ref, h_ref, z_ref, gated_scr):
        xv = x_ref[...]
        hb = _rms_fwd(xv, g_ref[...])[0].astype(BF16)
        for d in range(N_DEV):
            z_ref[:, d * FF_SLOT:(d + 1) * FF_SLOT] = _dot(hb, win_ref[d])
        u = _gelu(z_ref[:, :GATE_DIM])
        vb = _ln_fwd(_gelu(z_ref[:, GATE_DIM:]), lng_ref[...], lnb_ref[...])[0].astype(BF16)
        mask = _gate_mask()
        for gi in range(A_GROUPS):
            wm = jnp.where(mask, ws_ref[gi], 0.0).astype(BF16)
            bias = bst_ref[:, gi:gi + 1]
            cs = slice(gi * A_GROUP_DIM, (gi + 1) * A_GROUP_DIM)
            for n in range(nblk):
                rs = slice(n * GMLP_BLOCK, (n + 1) * GMLP_BLOCK)
                sv = _dot(wm, vb[rs, cs]) + bias
                gated_scr[rs, cs] = (u[rs, cs] * sv).astype(BF16)
        h_ref[...] = xv + _dot(gated_scr[...], wout_ref[...])

    return _call(
        "a_mix_fwd", body, (t // TM,),
        [_row(D_MODEL), _res((1, D_MODEL)), _res((N_DEV, D_MODEL, FF_SLOT)), _res((1, GATE_DIM)),
         _res((1, GATE_DIM)), _res((A_GROUPS, GMLP_BLOCK, GMLP_BLOCK)), _res((GMLP_BLOCK, A_GROUPS)),
         _res((GATE_DIM, D_MODEL))],
        [_row(D_MODEL), _row(2 * GATE_DIM), _row(GATE_DIM)],
        [_sds((t, D_MODEL), F32), _sds((t, 2 * GATE_DIM), F32), _sds((t, GATE_DIM), BF16)],
        (x, g, w_in, ln_g, ln_b, w_s, b_st, w_out))


MLP_W_SPECS = (_res((N_DEV, D_MODEL, FF_SLOT)), _res((N_DEV, FF_SLOT, D_MODEL)))


def _mlp_fwd(h, g, w1, w2):
    t = h.shape[0]

    def body(h_ref, g_ref, w1_ref, w2_ref, o_ref, a_ref):
        hv = h_ref[...]
        hb = _rms_fwd(hv, g_ref[...])[0].astype(BF16)
        o_ref[...] = hv
        for d in range(N_DEV):
            a = _dot(hb, w1_ref[d])
            a_ref[:, d * FF_SLOT:(d + 1) * FF_SLOT] = a
            r = jnp.maximum(a, 0.0)
            o_ref[...] += _dot((r * r).astype(BF16), w2_ref[d])

    return _call(
        "mlp_fwd", body, (t // TM_MLP_FWD,), [_row(D_MODEL, TM_MLP_FWD), _res((1, D_MODEL)), *MLP_W_SPECS],
        [_row(D_MODEL, TM_MLP_FWD), _row(D_FF, TM_MLP_FWD)], [_sds((t, D_MODEL), F32), _sds((t, D_FF), F32)],
        (h, g, w1, w2))


def _mlp_fwd_loss(h, g, w1, w2, final_g, target):
    t = h.shape[0]

    def body(h_ref, g_ref, w1_ref, w2_ref, fg_ref, t_ref, a_ref, loss_ref, dh_ref, dg_ref):
        hv = h_ref[...]
        hb = _rms_fwd(hv, g_ref[...])[0].astype(BF16)
        out = hv
        for d in range(N_DEV):
            a = _dot(hb, w1_ref[d])
            a_ref[:, d * FF_SLOT:(d + 1) * FF_SLOT] = a
            r = jnp.maximum(a, 0.0)
            out = out + _dot((r * r).astype(BF16), w2_ref[d])
        y, xhat, rstd = _rms_fwd(out, fg_ref[...])
        err = y - t_ref[...]
        part = 0.5 * jnp.sum(jnp.mean(err * err, axis=-1, keepdims=True), axis=0, keepdims=True)
        dx, dg = _rms_bwd(err * (1.0 / D_MODEL), xhat, rstd, fg_ref[...])
        dh_ref[...] = dx
        _acc(dg_ref, dg)
        _acc(loss_ref, part)

    return _call(
        "mlp_fwd_loss", body, (t // TM,),
        [_row(D_MODEL), _res((1, D_MODEL)), *MLP_W_SPECS, _res((1, D_MODEL)), _row(D_MODEL)],
        [_row(D_FF), _const((1, 1)), _row(D_MODEL), _const((1, D_MODEL))],
        [_sds((t, D_FF), F32), _sds((1, 1), F32), _sds((t, D_MODEL), F32), _sds((1, D_MODEL), F32)],
        (h, g, w1, w2, final_g, target))


KVQ_W_SPECS = (_res((1, D_MODEL)), _res((D_MODEL, KV_LORA + QK_ROPE)), _res((1, KV_LORA)),
               _res((B_HEADS, KV_LORA, QK_NOPE + V_HEAD)), _res((1, D_MODEL)), _res((D_MODEL, Q_LORA)),
               _res((1, Q_LORA)), _res((B_HEADS, Q_LORA, QK_NOPE + QK_ROPE)))


def _kvq_fwd(h, pos, inv_freq, kvq_w):
    t = h.shape[0]
    half = QK_ROPE // 2

    def body(h_ref, pos_ref, invf_ref, srcg_ref, wkva_ref, kvag_ref, wkvb_ref, mixg_ref, wqa_ref, qg_ref, wqb_ref,
             ckv_ref, k_ref, v_ref, cqpre_ref, q_ref, cos_ref, sin_ref):
        hv = h_ref[...]
        xhat = hv * lax.rsqrt(jnp.mean(hv * hv, axis=-1, keepdims=True) + EPS)
        ang = pos_ref[...].astype(F32) * invf_ref[...]
        cos, sin = jnp.cos(ang), jnp.sin(ang)
        cos_ref[...] = cos
        sin_ref[...] = sin
        ckv = _dot((xhat * srcg_ref[...]).astype(BF16), wkva_ref[...])
        ckv_ref[...] = ckv
        cb = _rms_fwd(ckv[:, :KV_LORA], kvag_ref[...])[0].astype(BF16)
        kpe = _rope(ckv[:, KV_LORA:], cos, sin).astype(BF16)
        for hd in range(B_HEADS):
            kv = _dot(cb, wkvb_ref[hd])
            k_ref[hd, :, 0:QK_NOPE] = kv[:, :QK_NOPE].astype(BF16)
            k_ref[hd, :, QK_NOPE:] = kpe
            v_ref[hd] = kv[:, QK_NOPE:].astype(BF16)
        cqpre = _dot((xhat * mixg_ref[...]).astype(BF16), wqa_ref[...])
        cqpre_ref[...] = cqpre
        cqb = _rms_fwd(cqpre, qg_ref[...])[0].astype(BF16)
        for hd in range(B_HEADS):
            q = _dot(cqb, wqb_ref[hd])
            q_ref[hd, :, 0:QK_NOPE] = q[:, :QK_NOPE].astype(BF16)
            q_ref[hd, :, QK_NOPE:] = _rope(q[:, QK_NOPE:], cos, sin).astype(BF16)

    return _call(
        "kvq_fwd", body, (t // TM,), [_row(D_MODEL), _row(1), _res((1, half)), *KVQ_W_SPECS],
        [_row(KV_LORA + QK_ROPE), _heads(QK_NOPE + QK_ROPE), _heads(V_HEAD), _row(Q_LORA),
         _heads(QK_NOPE + QK_ROPE), _row(half), _row(half)],
        [_sds((t, KV_LORA + QK_ROPE), F32), _sds((B_HEADS, t, QK_NOPE + QK_ROPE), BF16),
         _sds((B_HEADS, t, V_HEAD), BF16), _sds((t, Q_LORA), F32), _sds((B_HEADS, t, QK_NOPE + QK_ROPE), BF16),
         _sds((t, half), F32), _sds((t, half), F32)],
        (h, pos, inv_freq, *kvq_w))


def _softmax_rows(q, k_ref, k):
    past, upto = k * TM, (k + 1) * TM
    s = _dot_nt(q, k_ref[0:upto, :])
    own = jnp.where(_att_mask(0, TM, TM), s[:, past:], jnp.finfo(F32).min)
    s = own if k == 0 else jnp.concatenate([s[:, :past], own], axis=1)
    e = jnp.exp2((s - jnp.max(s, axis=-1, keepdims=True)) * (ATT_SCALE * LOG2_E))
    return e * (1.0 / jnp.sum(e, axis=-1, keepdims=True))


def _for_my_tile(i, nq, fn):
    for k in range(nq):
        @pl.when(i == k)
        def _(k=k):
            fn(k)


def _attn_fwd(h, q, k, v, w_o):
    t = h.shape[0]
    nq, hps = t // TM, HEADS_PER_STEP

    def body(h_ref, q_ref, k_ref, v_ref, wo_ref, o_ref, att_ref):
        i, pair = pl.program_id(0), pl.program_id(1)

        @pl.when(pair == 0)
        def _():
            o_ref[...] = h_ref[...]

        def tile(kt):
            proj = None
            for j in range(hps):
                hd = pair * hps + j
                p = _softmax_rows(q_ref[j], k_ref.at[hd], kt)
                ob = _dot(p.astype(BF16), v_ref[hd, 0:(kt + 1) * TM, :]).astype(BF16)
                att_ref[j] = ob
                proj = _dot(ob, wo_ref[hd]) if proj is None else proj + _dot(ob, wo_ref[hd])
            o_ref[...] += proj

        _for_my_tile(i, nq, tile)

    def per_head(d):
        return pl.BlockSpec((hps, TM, d), lambda i, pair: (pair, i, 0))

    def resident(shape):
        zeros = (0,) * len(shape)
        return pl.BlockSpec(shape, lambda i, pair: zeros, pipeline_mode=pl.Buffered(1))

    tile_spec = pl.BlockSpec((TM, D_MODEL), lambda i, pair: (i, 0))
    return _call(
        "attn_fwd", body, (nq, B_HEADS // hps),
        [tile_spec, per_head(QK_NOPE + QK_ROPE), resident((B_HEADS, t, QK_NOPE + QK_ROPE)),
         resident((B_HEADS, t, V_HEAD)), resident((B_HEADS, V_HEAD, D_MODEL))],
        [tile_spec, per_head(V_HEAD)], [_sds((t, D_MODEL), F32), _sds((B_HEADS, t, V_HEAD), BF16)],
        (h, q, k, v, w_o))


def _mlp_bwd(h, a, dho, g, w1, w2, layer, after=()):
    t = h.shape[0]

    def body(h_ref, a_ref, dho_ref, g_ref, w1_ref, w2_ref, dhi_ref, dg_ref, hn_ref, f_ref, da_ref, dhib_ref):
        gv = g_ref[...]
        y, xhat, rstd = _rms_fwd(h_ref[...], gv)
        hn_ref[...] = y.astype(BF16)
        dho_v = dho_ref[...]
        dhob = dho_v.astype(BF16)
        dhn = jnp.zeros((TM, D_MODEL), F32)
        for d in range(N_DEV):
            cs = slice(d * FF_SLOT, (d + 1) * FF_SLOT)
            r = jnp.maximum(a_ref[:, cs], 0.0)
            f_ref[:, cs] = (r * r).astype(BF16)
            da = (_dot_nt(dhob, w2_ref[d]) * (2.0 * r)).astype(BF16)
            da_ref[:, cs] = da
            dhn = dhn + _dot_nt(da, w1_ref[d])
        dx, dg = _rms_bwd(dhn, xhat, rstd, gv)
        dhi = dho_v + dx
        dhi_ref[...] = dhi
        dhib_ref[...] = dhi.astype(BF16)
        _acc(dg_ref, dg)

    return _call(
        f"mlp_bwd_{layer}", body, (t // TM,),
        [_row(D_MODEL), _row(D_FF), _row(D_MODEL), _res((1, D_MODEL)), *MLP_W_SPECS],
        [_row(D_MODEL), _const((1, D_MODEL)), _row(D_MODEL), _row(D_FF), _row(D_FF), _row(D_MODEL)],
        [_sds((t, D_MODEL), F32), _sds((1, D_MODEL), F32), _sds((t, D_MODEL), BF16), _sds((t, D_FF), BF16),
         _sds((t, D_FF), BF16), _sds((t, D_MODEL), BF16)],
        (h, a, dho, g, w1, w2), after=after)


def _attn_bwd(dh, q, k, v, w_o, cos, sin, after=()):
    t = dh.shape[0]
    half, hps = QK_ROPE // 2, HEADS_PER_STEP

    def body(dh_ref, q_ref, k_ref, v_ref, wo_ref, cos_ref, sin_ref, dq_ref, dk_ref, dv_ref):
        i = pl.program_id(1)

        @pl.when(i == 0)
        def _():
            dk_ref[...] = jnp.zeros_like(dk_ref)
            dv_ref[...] = jnp.zeros_like(dv_ref)

        def tile(kt):
            keys = slice(0, (kt + 1) * TM)
            for j in range(hps):
                qj = q_ref[j]
                do = _dot_nt(dh_ref[kt * TM:(kt + 1) * TM, :], wo_ref[j]).astype(BF16)
                p = _softmax_rows(qj, k_ref.at[j], kt)
                dp = _dot_nt(do, v_ref[j, keys, :])
                ds = (p * (dp - jnp.sum(p * dp, axis=-1, keepdims=True)) * ATT_SCALE).astype(BF16)
                dq = _dot(ds, k_ref[j, keys, :])
                dq_ref[j, :, 0:QK_NOPE] = dq[:, :QK_NOPE].astype(BF16)
                dq_ref[j, :, QK_NOPE:] = _rope(dq[:, QK_NOPE:], cos_ref[...], -sin_ref[...]).astype(BF16)
                dk_ref[j, keys, :] += _dot_tn(ds, qj)
                dv_ref[j, keys, :] += _dot_tn(p.astype(BF16), do)

        _for_my_tile(i, t // TM, tile)

    def per_pair(rows, d, tiled):
        return pl.BlockSpec((hps, rows, d), (lambda pair, i: (pair, i, 0)) if tiled else (lambda pair, i: (pair, 0, 0)))

    def tile(d):
        return pl.BlockSpec((TM, d), lambda pair, i: (i, 0))

    return _call(
        "attn_bwd", body, (B_HEADS // hps, t // TM),
        [pl.BlockSpec((t, D_MODEL), lambda pair, i: (0, 0), pipeline_mode=pl.Buffered(1)),
         per_pair(TM, QK_NOPE + QK_ROPE, True), per_pair(t, QK_NOPE + QK_ROPE, False), per_pair(t, V_HEAD, False),
         per_pair(V_HEAD, D_MODEL, False), tile(half), tile(half)],
        [per_pair(TM, QK_NOPE + QK_ROPE, True), per_pair(t, QK_NOPE + QK_ROPE, False), per_pair(t, V_HEAD, False)],
        [_sds((B_HEADS, t, QK_NOPE + QK_ROPE), BF16), _sds((B_HEADS, t, QK_NOPE + QK_ROPE), F32),
         _sds((B_HEADS, t, V_HEAD), F32)],
        (dh, q, k, v, w_o, cos, sin), after=after)


def _kvq_bwd(h, dh, ckv, cqpre, dq, dk, dv, cos, sin, kvq_w, after=()):
    t = h.shape[0]
    half, last = QK_ROPE // 2, t // TM - 1
    grad_shapes = [(D_MODEL, Q_LORA), (B_HEADS, Q_LORA, QK_NOPE + QK_ROPE), (D_MODEL, KV_LORA + QK_ROPE),
                   (B_HEADS, KV_LORA, QK_NOPE + V_HEAD)]

    def body(h_ref, dh_ref, ckv_ref, cqpre_ref, dq_ref, dk_ref, dv_ref, cos_ref, sin_ref,
             srcg_ref, wkva_ref, kvag_ref, wkvb_ref, mixg_ref, wqa_ref, qg_ref, wqb_ref,
             dhi_ref, dmixg_ref, dsrcg_ref, dqg_ref, dkvag_ref, gqa_ref, gqb_ref, gkva_ref, gkvb_ref,
             aqa, aqb, akva, akvb):
        @pl.when(pl.program_id(0) == 0)
        def _():
            for acc in (aqa, aqb, akva, akvb):
                acc[...] = jnp.zeros_like(acc)

        hv = h_ref[...]
        rstd = lax.rsqrt(jnp.mean(hv * hv, axis=-1, keepdims=True) + EPS)
        xhat = hv * rstd
        mixg, srcg, qg, kvag = mixg_ref[...], srcg_ref[...], qg_ref[...], kvag_ref[...]
        cq, cqhat, crstd = _rms_fwd(cqpre_ref[...], qg)
        cqb = cq.astype(BF16)
        dcq = jnp.zeros((TM, Q_LORA), F32)
        for hd in range(B_HEADS):
            dcq = dcq + _dot_nt(dq_ref[hd], wqb_ref[hd])
            aqb[hd] += _dot_tn(cqb, dq_ref[hd])
        dcqpre, dqg = _rms_bwd(dcq, cqhat, crstd, qg)
        dcqpre_b = dcqpre.astype(BF16)
        aqa[...] += _dot_tn((xhat * mixg).astype(BF16), dcqpre_b)
        dxq, dmixg = _rms_bwd(_dot_nt(dcqpre_b, wqa_ref[...]), xhat, rstd, mixg)
        ckv = ckv_ref[...]
        c, chat, krstd = _rms_fwd(ckv[:, :KV_LORA], kvag)
        cb = c.astype(BF16)
        dc = jnp.zeros((TM, KV_LORA), F32)
        dkpe = jnp.zeros((TM, QK_ROPE), F32)
        for hd in range(B_HEADS):
            dkv = jnp.concatenate([dk_ref[hd, :, 0:QK_NOPE], dv_ref[hd]], axis=-1).astype(BF16)
            akvb[hd] += _dot_tn(cb, dkv)
            dc = dc + _dot_nt(dkv, wkvb_ref[hd])
            dkpe = dkpe + dk_ref[hd, :, QK_NOPE:]
        dlat, dkvag = _rms_bwd(dc, chat, krstd, kvag)
        dpe = _rope(dkpe, cos_ref[...], -sin_ref[...])
        dckv_b = jnp.concatenate([dlat, dpe], axis=-1).astype(BF16)
        akva[...] += _dot_tn((xhat * srcg).astype(BF16), dckv_b)
        dxk, dsrcg = _rms_bwd(_dot_nt(dckv_b, wkva_ref[...]), xhat, rstd, srcg)
        dhi_ref[...] = dh_ref[...] + dxq + dxk
        _acc(dmixg_ref, dmixg)
        _acc(dsrcg_ref, dsrcg)
        _acc(dqg_ref, dqg)
        _acc(dkvag_ref, dkvag)

        @pl.when(pl.program_id(0) == last)
        def _():
            for out, acc in ((gqa_ref, aqa), (gqb_ref, aqb), (gkva_ref, akva), (gkvb_ref, akvb)):
                out[...] = acc[...].astype(BF16)

    return _call(
        "kvq_bwd", body, (t // TM,),
        [_row(D_MODEL), _row(D_MODEL), _row(KV_LORA + QK_ROPE), _row(Q_LORA), _heads(QK_NOPE + QK_ROPE),
         _heads(QK_NOPE + QK_ROPE), _heads(V_HEAD), _row(half), _row(half), *KVQ_W_SPECS],
        [_row(D_MODEL), _const((1, D_MODEL)), _const((1, D_MODEL)), _const((1, Q_LORA)), _const((1, KV_LORA))]
        + [_const(s) for s in grad_shapes],
        [_sds((t, D_MODEL), F32), _sds((1, D_MODEL), F32), _sds((1, D_MODEL), F32), _sds((1, Q_LORA), F32),
         _sds((1, KV_LORA), F32)] + [_sds(s, BF16) for s in grad_shapes],
        (h, dh, ckv, cqpre, dq, dk, dv, cos, sin, *kvq_w), scratch=[pltpu.VMEM(s, F32) for s in grad_shapes],
        after=after)


def _a_mix_bwd(x, z, dh, g, w_in, ln_g, ln_b, w_s, b_st, w_out, after=()):
    t = x.shape[0]
    tm = TM_GATE
    nblk = tm // GMLP_BLOCK

    def body(x_ref, z_ref, dh_ref, g_ref, win_ref, lng_ref, lnb_ref, ws_ref, bst_ref, wout_ref,
             dx_ref, hn_ref, dz_ref, dg_ref, dlng_ref, dlnb_ref, dws_ref, dbs_ref, dvn_scr, gelu_grad_v):
        @pl.when(pl.program_id(0) == 0)
        def _():
            dws_ref[...] = jnp.zeros_like(dws_ref)
            dbs_ref[...] = jnp.zeros_like(dbs_ref)

        gv, lng = g_ref[...], lng_ref[...]
        y, xhat, rstd = _rms_fwd(x_ref[...], gv)
        hn_ref[...] = y.astype(BF16)
        dhv = dh_ref[...]
        dgated = _dot_nt(dhv.astype(BF16), wout_ref[...])
        u, gelu_grad_u = _gelu_and_grad(z_ref[:, :GATE_DIM])
        v, gelu_grad_v[...] = _gelu_and_grad(z_ref[:, GATE_DIM:])
        vn, vhat, lrstd = _ln_fwd(v, lng, lnb_ref[...])
        vb = vn.astype(BF16)
        mask = _gate_mask()
        for gi in range(A_GROUPS):
            wm = jnp.where(mask, ws_ref[gi], 0.0).astype(BF16)
            bias = bst_ref[:, gi:gi + 1]
            cs = slice(gi * A_GROUP_DIM, (gi + 1) * A_GROUP_DIM)
            dws = jnp.zeros((GMLP_BLOCK, GMLP_BLOCK), F32)
            dbs = jnp.zeros((GMLP_BLOCK, 1), F32)
            for n in range(nblk):
                rs = slice(n * GMLP_BLOCK, (n + 1) * GMLP_BLOCK)
                sv = _dot(wm, vb[rs, cs]) + bias
                dz_ref[rs, cs] = (dgated[rs, cs] * sv * gelu_grad_u[rs, cs]).astype(BF16)
                dsv = dgated[rs, cs] * u[rs, cs]
                dsvb = dsv.astype(BF16)
                dws = dws + _dot_nt(dsvb, vb[rs, cs])
                dbs = dbs + jnp.sum(dsv, axis=-1, keepdims=True)
                dvn_scr[rs, cs] = _dot_tn(wm, dsvb)
            dws_ref[gi] += jnp.where(mask, dws, 0.0)
            dbs_ref[gi] += dbs
        dvn = dvn_scr[...]
        dvhat = dvn * lng
        dv = lrstd * (dvhat - jnp.mean(dvhat, axis=-1, keepdims=True)
                      - vhat * jnp.mean(dvhat * vhat, axis=-1, keepdims=True))
        dz_ref[:, GATE_DIM:] = (dv * gelu_grad_v[...]).astype(BF16)
        dhn = jnp.zeros((tm, D_MODEL), F32)
        for d in range(N_DEV):
            dhn = dhn + _dot_nt(dz_ref[:, d * FF_SLOT:(d + 1) * FF_SLOT], win_ref[d])
        dx, dg = _rms_bwd(dhn, xhat, rstd, gv)
        dx_ref[...] = dhv + dx
        _acc(dg_ref, dg)
        _acc(dlng_ref, jnp.sum(dvn * vhat, axis=0, keepdims=True))
        _acc(dlnb_ref, jnp.sum(dvn, axis=0, keepdims=True))

    return _call(
        "a_mix_bwd", body, (t // tm,),
        [_row(D_MODEL, tm), _row(2 * GATE_DIM, tm), _row(D_MODEL, tm), _res((1, D_MODEL)),
         _res((N_DEV, D_MODEL, FF_SLOT)), _res((1, GATE_DIM)), _res((1, GATE_DIM)),
         _res((A_GROUPS, GMLP_BLOCK, GMLP_BLOCK)), _res((GMLP_BLOCK, A_GROUPS)), _res((GATE_DIM, D_MODEL))],
        [_row(D_MODEL, tm), _row(D_MODEL, tm), _row(2 * GATE_DIM, tm),
         _const((1, D_MODEL)), _const((1, GATE_DIM)), _const((1, GATE_DIM)),
         _const((A_GROUPS, GMLP_BLOCK, GMLP_BLOCK)), _const((A_GROUPS, GMLP_BLOCK, 1))],
        [_sds((t, D_MODEL), F32), _sds((t, D_MODEL), BF16),
         _sds((t, 2 * GATE_DIM), BF16), _sds((1, D_MODEL), F32), _sds((1, GATE_DIM), F32),
         _sds((1, GATE_DIM), F32), _sds((A_GROUPS, GMLP_BLOCK, GMLP_BLOCK), F32),
         _sds((A_GROUPS, GMLP_BLOCK, 1), F32)],
        (x, z, dh, g, w_in, ln_g, ln_b, w_s, b_st, w_out),
        scratch=[pltpu.VMEM((tm, GATE_DIM), F32), pltpu.VMEM((tm, GATE_DIM), F32)], after=after)


def _wgrad(name, a, b, a_spec, b_spec, m, n, after=()):
    def body(a_ref, b_ref, o_ref):
        o_ref[0] = _dot_tn(a_ref[...].astype(BF16), b_ref[...].astype(BF16)).astype(BF16)

    return _call(name, body, (N_DEV,), [a_spec, b_spec], [pl.BlockSpec((1, m, n), lambda d: (d, 0, 0))],
                 [_sds((N_DEV, m, n), BF16)], (a, b), after=after)[0]


def _full(t, d):
    return pl.BlockSpec((t, d), lambda i: (0, 0), pipeline_mode=pl.Buffered(1))


def _cols(t, d):
    return pl.BlockSpec((t, d), lambda i: (0, i))


def _head(t, d):
    return pl.BlockSpec((None, t, d), lambda i: (i, 0, 0))


def _local_step(x, pos, target, inv_freq, wg, sm, shards=None):
    t = x.shape[0]
    wg = dict(wg)
    dist = shards is not None
    mix_g = [sm["norm_mix_g"][l:l + 1] for l in range(2)]
    mlp_g = [sm["norm_mlp_g"][l:l + 1] for l in range(2)]

    ids = iter(range(2, 2 + 9))

    def gather(names):
        if dist:
            got = _by_sequencer("gather_" + names[0], _gather_comm([shards[k] for k in names]),
                                SIBLING_AND_NEIGHBOURS, next(ids))
            wg.update(zip(names, got))

    def send(name, names):
        if dist:
            comm = _exchange_comm(grads=[g[k] for k in names])
            g.update(zip(names, _by_sequencer("exchange_" + name, comm, EVERYONE, next(ids))))

    def send_sums(name, names, meanwhile):
        if not dist:
            meanwhile()
            return ()
        grads = [g[k] for k in names]
        landed = _by_sequencer("pair_exchange_" + name, _pair_exchange_comm(grads), (1,), next(ids))
        sums = _pair_add("pair_add_" + name, grads, landed, after=meanwhile())
        g.update(zip(names, _by_sequencer("exchange_" + name, _chip_exchange_comm(sums), OTHER_CHIPS, next(ids))))
        return sums

    def a_args():
        return (wg["a_w_in"], wg["a_ln_v_g"], wg["a_ln_v_b"], sm["a_w_s"], sm["a_b_st"], wg["a_w_out"])

    def kvq_w():
        return (sm["kv_src_norm_g"], wg["kv_w_a"], sm["kv_a_norm_g"], wg["kv_w_b"], mix_g[1], wg["b_w_q_a"],
                sm["b_q_norm_g"], wg["b_w_q_b"])

    gather(("mlp_w1_0", "mlp_w2_0"))
    h1, z, gated = _a_mix_fwd(x, mix_g[0], *a_args())
    gather(("kv_w_a", "kv_w_b", "b_w_q_a", "b_w_q_b", "b_w_o"))
    h2, a0 = _mlp_fwd(h1, mlp_g[0], wg["mlp_w1_0"], wg["mlp_w2_0"])
    if dist:
        wg["b_w_q_a"] = wg["b_w_q_a"].reshape(D_MODEL, Q_LORA)
        wg["kv_w_a"] = wg["kv_w_a"].reshape(D_MODEL, KV_LORA + QK_ROPE)
    gather(("mlp_w1_1", "mlp_w2_1"))
    ckv, k, v, cqpre, q, cos, sin = _kvq_fwd(h2, pos, inv_freq, kvq_w())
    h3, att = _attn_fwd(h2, q, k, v, wg["b_w_o"])
    a1, loss, dh4, d_final_g = _mlp_fwd_loss(h3, mlp_g[1], wg["mlp_w1_1"], wg["mlp_w2_1"], sm["final_norm_g"], target)

    g = {}
    dh3, d_mlp_g1, hn, f, da, dh3_b = _mlp_bwd(h3, a1, dh4, mlp_g[1], wg["mlp_w1_1"], wg["mlp_w2_1"], 1)
    dq, dk, dv = _attn_bwd(dh3_b, q, k, v, wg["b_w_o"], cos, sin)
    g["mlp_w1_1"] = _wgrad("wgrad_w1_1", hn, da, _full(t, D_MODEL), _cols(t, FF_SLOT), D_MODEL, FF_SLOT, after=[dq])
    g["mlp_w2_1"] = _wgrad("wgrad_w2_1", f, dh4, _cols(t, FF_SLOT), _full(t, D_MODEL), FF_SLOT, D_MODEL)

    def wgrad_w_o():
        g["b_w_o"] = _wgrad("wgrad_w_o", att, dh3_b, _head(t, V_HEAD), _full(t, D_MODEL), V_HEAD, D_MODEL)
        return [g["b_w_o"]]

    sums = send_sums("mlp_1", ("mlp_w1_1", "mlp_w2_1"), wgrad_w_o)
    dh2, d_mix_g1, d_src_g, d_q_g, d_kv_a_g, g_q_a, g["b_w_q_b"], g_kv_a, g["kv_w_b"] = _kvq_bwd(
        h2, dh3, ckv, cqpre, dq, dk, dv, cos, sin, kvq_w(), after=sums)
    g["b_w_q_a"] = g_q_a.reshape(N_DEV, D_MODEL // N_DEV, Q_LORA)
    g["kv_w_a"] = g_kv_a.reshape(N_DEV, D_MODEL // N_DEV, KV_LORA + QK_ROPE)
    qkv = ("b_w_q_a", "b_w_q_b", "kv_w_a", "kv_w_b")
    landed = [g[k] for k in qkv]
    send("qkv", qkv)
    dh1, d_mlp_g0, hn, f, da, dh1_b = _mlp_bwd(h1, a0, dh2, mlp_g[0], wg["mlp_w1_0"], wg["mlp_w2_0"], 0,
                                               after=landed if dist else ())
    landed = [g["mlp_w1_1"], g["mlp_w2_1"]] if dist else ()
    g["mlp_w1_0"] = _wgrad("wgrad_w1_0", hn, da, _full(t, D_MODEL), _cols(t, FF_SLOT), D_MODEL, FF_SLOT, after=landed)
    g["mlp_w2_0"] = _wgrad("wgrad_w2_0", f, dh2, _cols(t, FF_SLOT), _full(t, D_MODEL), FF_SLOT, D_MODEL)

    def wgrad_a_w_out():
        g["a_w_out"] = _wgrad("wgrad_a_w_out", gated, dh1_b, _cols(t, GATE_DIM // N_DEV), _full(t, D_MODEL),
                              GATE_DIM // N_DEV, D_MODEL)
        return [g["a_w_out"]] + [g[k] for k in qkv]

    sums = send_sums("mlp_0", ("mlp_w1_0", "mlp_w2_0", "b_w_o"), wgrad_a_w_out)
    dx, hn, dz, d_mix_g0, d_ln_g, d_ln_b, d_ws, d_bs = _a_mix_bwd(x, z, dh1, mix_g[0], *a_args(), after=sums)
    small = {
        "norm_mix_g": jnp.concatenate([d_mix_g0, d_mix_g1], axis=0),
        "norm_mlp_g": jnp.concatenate([d_mlp_g0, d_mlp_g1], axis=0),
        "a_ln_v_g": d_ln_g.reshape(N_DEV, GATE_DIM // N_DEV),
        "a_ln_v_b": d_ln_b.reshape(N_DEV, GATE_DIM // N_DEV),
        "a_w_s": d_ws.astype(BF16) if dist else d_ws,
        "a_b_s": d_bs.reshape(A_GROUPS, GMLP_BLOCK),
        "b_q_norm_g": d_q_g,
        "kv_src_norm_g": d_src_g,
        "kv_a_norm_g": d_kv_a_g,
        "final_norm_g": d_final_g,
    }
    if dist:
        parts = [small[k].reshape((1,) + small[k].shape) for k in SMALL] + [loss.reshape(1, 1, 1)]
        got = _by_sequencer("gather_small", _gather_comm(parts), SIBLING_AND_NEIGHBOURS, next(ids))
        small, loss = dict(zip(SMALL, got)), got[-1]
    g["a_w_in"] = _wgrad("wgrad_a_w_in", hn, dz, _full(t, D_MODEL), _cols(t, FF_SLOT), D_MODEL, FF_SLOT)
    return loss, dx, g, small


def _adamw(w, g, m, v):
    m = ADAM_B1 * m + (1.0 - ADAM_B1) * g
    v = ADAM_B2 * v + (1.0 - ADAM_B2) * (g * g)
    m_hat = m / (1.0 - ADAM_B1 ** ADAM_STEP)
    v_hat = v / (1.0 - ADAM_B2 ** ADAM_STEP)
    return -ADAM_LR * (m_hat / (jnp.sqrt(v_hat) + ADAM_EPS) + ADAM_WD * w), m, v


def _sum_in_device_order(r_ref):
    g = r_ref[0].astype(F32)
    for j in range(1, r_ref.shape[0]):
        g = g + r_ref[j].astype(F32)
    return g


def _adamw_sharded(name, recvs, w, m, v):
    layers, r, c = w.shape
    tr = math.gcd(r, 512)
    flat = [a for per_layer in recvs for a in per_layer]

    def body(*refs):
        r_refs, (w_ref, m_ref, v_ref) = refs[:len(flat)], refs[len(flat):len(flat) + 3]
        g_ref, d_ref, nm_ref, nv_ref = refs[-4:]
        layer = pl.program_id(0)
        g, pos = None, 0
        for li, per_layer in enumerate(recvs):
            total = None
            for ref in r_refs[pos:pos + len(per_layer)]:
                part = _sum_in_device_order(ref)
                total = part if total is None else total + part
            pos += len(per_layer)
            g = total if g is None else jnp.where(layer == li, total, g)
        g_ref[...] = g
        d_ref[...], nm_ref[...], nv_ref[...] = _adamw(w_ref[...], g, m_ref[...], v_ref[...])

    blk = pl.BlockSpec((None, tr, c), lambda l, i: (l, i, 0))
    return _call(name, body, (layers, r // tr),
                 [pl.BlockSpec((a.shape[0], tr, c), lambda l, i: (0, i, 0)) for a in flat] + [blk] * 3,
                 [blk] * 4, [_sds(w.shape, F32)] * 4, (*flat, w, m, v))


def _adamw_small(recvs, ws, ms, vs, own_row, losses):
    n = len(recvs)

    def body(*refs):
        r_refs, w_refs, m_refs, v_refs = (refs[i * n:(i + 1) * n] for i in range(4))
        outs, scr = refs[4 * n + 1:8 * n + 2], refs[8 * n + 2:]
        outs[-1][...] = _sum_in_device_order(refs[4 * n])
        me = _my_place()[3]
        for a in range(n):
            g = _sum_in_device_order(r_refs[a])
            if own_row[a]:
                scr[0][...] = g
                g = scr[0][pl.ds(me, 1), :]
            g_ref, d_ref, nm_ref, nv_ref = outs[4 * a:4 * a + 4]
            g_ref[...] = g
            d_ref[...], nm_ref[...], nv_ref[...] = _adamw(w_refs[a][...], g, m_refs[a][...], v_refs[a][...])

    out_shape = []
    for w in ws:
        out_shape += [_sds(w.shape, F32)] * 4
    return pl.pallas_call(
        body, name="adamw_small", in_specs=[VMEM] * (4 * n + 1), out_specs=[VMEM] * (4 * n + 1),
        out_shape=out_shape + [_sds((1, 1), F32)], scratch_shapes=[pltpu.VMEM((N_DEV, GATE_DIM // N_DEV), F32)],
    )(*recvs, *ws, *ms, *vs, losses)


BIG = ("a_w_in", "a_w_out", "b_w_q_a", "b_w_q_b", "b_w_o", "kv_w_a", "kv_w_b", "mlp_w1", "mlp_w2")
SMALL = ("norm_mix_g", "norm_mlp_g", "a_ln_v_g", "a_ln_v_b", "a_w_s", "a_b_s", "b_q_norm_g", "kv_src_norm_g",
         "kv_a_norm_g", "final_norm_g")
WEIGHTS = ("norm_mix_g", "norm_mlp_g", "a_w_in", "a_ln_v_g", "a_ln_v_b", "a_w_s", "a_b_s", "a_w_out", "b_w_q_a",
           "b_q_norm_g", "b_w_q_b", "b_w_o", "kv_src_norm_g", "kv_w_a", "kv_a_norm_g", "kv_w_b", "mlp_w1", "mlp_w2",
           "final_norm_g")


def _two_d(name, a):
    if name in ("a_w_s", "a_b_s"):
        return a.reshape(a.shape[1:])
    return a.reshape(1, -1) if a.ndim == 1 else a


def _three_d(a):
    return a if a.ndim == 3 else a.reshape((1,) + a.shape)


def kernel(x, positions, norm_mix_g, norm_mlp_g, a_w_in, a_ln_v_g, a_ln_v_b, a_w_s, a_b_s, a_w_out, b_w_q_a, b_q_norm_g, b_w_q_b, b_w_o, kv_src_norm_g, kv_w_a, kv_a_norm_g, kv_w_b, mlp_w1, mlp_w2, final_norm_g, loss_target, m_norm_mix_g, m_norm_mlp_g, m_a_w_in, m_a_ln_v_g, m_a_ln_v_b, m_a_w_s, m_a_b_s, m_a_w_out, m_b_w_q_a, m_b_q_norm_g, m_b_w_q_b, m_b_w_o, m_kv_src_norm_g, m_kv_w_a, m_kv_a_norm_g, m_kv_w_b, m_mlp_w1, m_mlp_w2, m_final_norm_g, v_norm_mix_g, v_norm_mlp_g, v_a_w_in, v_a_ln_v_g, v_a_ln_v_b, v_a_w_s, v_a_b_s, v_a_w_out, v_b_w_q_a, v_b_q_norm_g, v_b_w_q_b, v_b_w_o, v_kv_src_norm_g, v_kv_w_a, v_kv_a_norm_g, v_kv_w_b, v_mlp_w1, v_mlp_w2, v_final_norm_g):
    w = dict(norm_mix_g=norm_mix_g, norm_mlp_g=norm_mlp_g, a_w_in=a_w_in, a_ln_v_g=a_ln_v_g, a_ln_v_b=a_ln_v_b,
             a_w_s=a_w_s, a_b_s=a_b_s, a_w_out=a_w_out, b_w_q_a=b_w_q_a, b_q_norm_g=b_q_norm_g, b_w_q_b=b_w_q_b,
             b_w_o=b_w_o, kv_src_norm_g=kv_src_norm_g, kv_w_a=kv_w_a, kv_a_norm_g=kv_a_norm_g, kv_w_b=kv_w_b,
             mlp_w1=mlp_w1, mlp_w2=mlp_w2, final_norm_g=final_norm_g)
    m = dict(norm_mix_g=m_norm_mix_g, norm_mlp_g=m_norm_mlp_g, a_w_in=m_a_w_in, a_ln_v_g=m_a_ln_v_g,
             a_ln_v_b=m_a_ln_v_b, a_w_s=m_a_w_s, a_b_s=m_a_b_s, a_w_out=m_a_w_out, b_w_q_a=m_b_w_q_a,
             b_q_norm_g=m_b_q_norm_g, b_w_q_b=m_b_w_q_b, b_w_o=m_b_w_o, kv_src_norm_g=m_kv_src_norm_g,
             kv_w_a=m_kv_w_a, kv_a_norm_g=m_kv_a_norm_g, kv_w_b=m_kv_w_b, mlp_w1=m_mlp_w1, mlp_w2=m_mlp_w2,
             final_norm_g=m_final_norm_g)
    v = dict(norm_mix_g=v_norm_mix_g, norm_mlp_g=v_norm_mlp_g, a_w_in=v_a_w_in, a_ln_v_g=v_a_ln_v_g,
             a_ln_v_b=v_a_ln_v_b, a_w_s=v_a_w_s, a_b_s=v_a_b_s, a_w_out=v_a_w_out, b_w_q_a=v_b_w_q_a,
             b_q_norm_g=v_b_q_norm_g, b_w_q_b=v_b_w_q_b, b_w_o=v_b_w_o, kv_src_norm_g=v_kv_src_norm_g,
             kv_w_a=v_kv_w_a, kv_a_norm_g=v_kv_a_norm_g, kv_w_b=v_kv_w_b, mlp_w1=v_mlp_w1, mlp_w2=v_mlp_w2,
             final_norm_g=v_final_norm_g)
    t = x.shape[1]

    first = ("a_w_in", "a_w_out", "a_ln_v_g", "a_ln_v_b")
    later = ("mlp_w1", "mlp_w2", "kv_w_a", "kv_w_b", "b_w_q_a", "b_w_q_b", "b_w_o")
    later_blocks = ("mlp_w1_0", "mlp_w1_1", "mlp_w2_0", "mlp_w2_1") + later[2:]
    got, casts = _gather_first([_three_d(w[k]) if k in BIG else w[k] for k in first], [_three_d(w[k]) for k in later])
    wg = dict(zip(first, got))
    wg["a_w_out"] = wg["a_w_out"].reshape(GATE_DIM, D_MODEL)
    wg["a_ln_v_g"] = wg["a_ln_v_g"].reshape(1, GATE_DIM)
    wg["a_ln_v_b"] = wg["a_ln_v_b"].reshape(1, GATE_DIM)
    shards = dict(zip(later_blocks, casts))

    sm = {k: _two_d(k, w[k]) for k in SMALL if k not in ("a_ln_v_g", "a_ln_v_b")}
    sm["a_b_st"] = sm["a_b_s"].T
    inv_freq = (ROPE_THETA ** (-jnp.arange(0, QK_ROPE, 2, dtype=F32) / QK_ROPE)).reshape(1, QK_ROPE // 2)

    losses, dx, g, small = _local_step(x[0], positions.reshape(t, 1), loss_target[0], inv_freq, wg, sm, shards)

    names = ("a_w_in", "a_w_out")
    sums = _pair_reduce("pair_reduce_a", [g[k] for k in names], after=[g["mlp_w1_0"], g["mlp_w2_0"]])
    g.update(zip(names, _by_sequencer("exchange_last", _chip_exchange_comm(sums), OTHER_CHIPS, collective_id=1)))

    out = {}
    for k in BIG:
        recvs = [[g[k + "_0"]], [g[k + "_1"]]] if k.startswith("mlp") else [[g[k]]]
        res = _adamw_sharded("adamw_" + k, recvs, _three_d(w[k]), _three_d(m[k]), _three_d(v[k]))
        out[k] = [o.reshape(w[k].shape) for o in res]
    own_row = [k in ("a_ln_v_g", "a_ln_v_b") for k in SMALL]
    res = _adamw_small([small[k] for k in SMALL], [_two_d(k, w[k]) for k in SMALL], [_two_d(k, m[k]) for k in SMALL],
                       [_two_d(k, v[k]) for k in SMALL], own_row, losses)
    for i, k in enumerate(SMALL):
        out[k] = [o.reshape(w[k].shape) for o in res[4 * i:4 * i + 4]]

    return (res[-1].reshape(()), dx.reshape(x.shape), *[out[k][0] for k in WEIGHTS], *[out[k][1] for k in WEIGHTS],
            *[out[k][2] for k in WEIGHTS], *[out[k][3] for k in WEIGHTS])
```

```python
import math

import jax
import jax.numpy as jnp
from jax import lax
from jax.experimental import pallas as pl
from jax.experimental.pallas import tpu as pltpu
from jax.experimental.pallas import tpu_sc as plsc

F32, BF16 = jnp.float32, jnp.bfloat16
MESH = pl.DeviceIdType.MESH
ANY = pl.BlockSpec(memory_space=pl.ANY)
VMEM = pl.BlockSpec(memory_space=pltpu.VMEM)

N_DEV = 8
D_MODEL = 1024
CHUNK = 64
GMLP_BLOCK = 128
GATE_DIM = 2048
A_GROUPS = 8
A_GROUP_DIM = GATE_DIM // A_GROUPS
B_HEADS = 8
QK_NOPE, QK_ROPE, V_HEAD = 128, 64, 128
Q_LORA, KV_LORA = 384, 256
ROPE_THETA = 10000.0
D_FF = 4096
FF_SLOT = D_FF // N_DEV
EPS = 1e-6
ATT_SCALE = (QK_NOPE + QK_ROPE) ** -0.5

ADAM_LR, ADAM_B1, ADAM_B2, ADAM_EPS, ADAM_WD, ADAM_STEP = 0.001, 0.9, 0.999, 1e-08, 0.01, 10

TM = 256
TM_GATE = 256
TM_MLP_FWD = 512
TM_KVQ = 512
VMEM_LIMIT = 56 * 1024 * 1024
INV_SQRT2 = 1.0 / math.sqrt(2.0)
INV_SQRT_2PI = 1.0 / math.sqrt(2.0 * math.pi)
LOG2_E = 1.0 / math.log(2.0)
HEADS_PER_STEP = 2


def _dot(a, b):
    return jnp.dot(a, b, preferred_element_type=F32)


def _dot_nt(a, b):
    return lax.dot_general(a, b, (((1,), (1,)), ((), ())), preferred_element_type=F32)


def _dot_tn(a, b):
    return lax.dot_general(a, b, (((0,), (0,)), ((), ())), preferred_element_type=F32)


def _rms_fwd(x, g):
    rstd = lax.rsqrt(jnp.mean(x * x, axis=-1, keepdims=True) + EPS)
    xhat = x * rstd
    return xhat * g, xhat, rstd


def _rms_bwd(dy, xhat, rstd, g):
    dxhat = dy * g
    dx = rstd * (dxhat - xhat * jnp.mean(dxhat * xhat, axis=-1, keepdims=True))
    return dx, jnp.sum(dy * xhat, axis=0, keepdims=True)


def _ln_fwd(v, g, b):
    mu = jnp.mean(v, axis=-1, keepdims=True)
    vc = v - mu
    rstd = lax.rsqrt(jnp.mean(vc * vc, axis=-1, keepdims=True) + EPS)
    vhat = vc * rstd
    return vhat * g + b, vhat, rstd


def _gelu(x):
    return 0.5 * x * (1.0 + lax.erf(x * INV_SQRT2))


def _gelu_and_grad(x):
    cdf = 0.5 * (1.0 + lax.erf(x * INV_SQRT2))
    return x * cdf, cdf + x * jnp.exp(-0.5 * x * x) * INV_SQRT_2PI


def _rope(x, cos, sin):
    x1, x2 = x[:, :QK_ROPE // 2], x[:, QK_ROPE // 2:]
    return jnp.concatenate([x1 * cos - x2 * sin, x2 * cos + x1 * sin], axis=-1)


def _gate_mask():
    row = lax.broadcasted_iota(jnp.int32, (GMLP_BLOCK, GMLP_BLOCK), 0)
    col = lax.broadcasted_iota(jnp.int32, (GMLP_BLOCK, GMLP_BLOCK), 1)
    return (col < CHUNK) | (row >= CHUNK)


def _att_mask(q0, tq, t):
    q = q0 + lax.broadcasted_iota(jnp.int32, (tq, t), 0)
    k = lax.broadcasted_iota(jnp.int32, (tq, t), 1)
    return jnp.right_shift(k, 6) <= jnp.right_shift(q, 6)


def _res(shape, imap=None):
    zeros = (0,) * len(shape)
    return pl.BlockSpec(shape, imap or (lambda i: zeros), pipeline_mode=pl.Buffered(1))


def _const(shape):
    zeros = (0,) * len(shape)
    return pl.BlockSpec(shape, lambda i: zeros)


def _row(d, tm=TM):
    return pl.BlockSpec((tm, d), lambda i: (i, 0))


def _heads(d, tm=TM):
    return pl.BlockSpec((B_HEADS, tm, d), lambda i: (0, i, 0))


def _sds(shape, dt):
    return jax.ShapeDtypeStruct(shape, dt)


def _acc(ref, val):
    @pl.when(pl.program_id(0) == 0)
    def _():
        ref[...] = jnp.zeros_like(ref)
    ref[...] += val


def _my_place():
    x, y, c = lax.axis_index("x"), lax.axis_index("y"), lax.axis_index("c")
    return x, y, c, 4 * x + 2 * y + c


def _peer(x, y, c, k):
    px = 1 - x if k & 4 else x
    py = 1 - y if k & 2 else y
    pc = 1 - c if k & 1 else c
    return (px, py, pc), 4 * px + 2 * py + pc


CHIPS = (2, 4, 6)


def _splits(ref):
    return len(ref.shape) >= 3 and ref.shape[1] % 32 == 0


def _piece(ref, block, half=None):
    if half is None or not _splits(ref):
        return ref.at[pl.ds(block, 1)]
    rows = ref.shape[1] // 2
    return ref.at[pl.ds(block, 1), pl.ds(half * rows, rows)]


def _gather_copy(sems, a, k, piece, to, src=None):
    return pltpu.make_async_remote_copy(
        src_ref=piece if src is None else src, dst_ref=piece, send_sem=sems[0].at[a, k], recv_sem=sems[1].at[a, k],
        device_id=to, device_id_type=MESH)


def _gather_start(srcs, outs, sems, only=None):
    x, y, c, me = _my_place()
    for a in range(len(srcs)) if only is None else (only,):
        mine = _piece(outs[a], me)
        pltpu.make_async_copy(srcs[a], mine, sems[2].at[a]).start()
        for k, rel in enumerate((1, 4, 2)):
            _gather_copy(sems, a, k, mine, _peer(x, y, c, rel)[0], src=srcs[a]).start()


def _gather_relay(srcs, outs, sems):
    x, y, c, _ = _my_place()
    sib = _peer(x, y, c, 1)[0]
    (xn, xn_i), (yn, yn_i) = _peer(x, y, c, 4), _peer(x, y, c, 2)
    for a in range(len(srcs)):
        out = outs[a]
        _gather_copy(sems, a, 1, _piece(out, xn_i), xn).wait_recv()
        _gather_copy(sems, a, 3, _piece(out, xn_i, 0), yn).start()
        _gather_copy(sems, a, 5, _piece(out, xn_i), sib).start()
        _gather_copy(sems, a, 2, _piece(out, yn_i), yn).wait_recv()
        if _splits(out):
            _gather_copy(sems, a, 4, _piece(out, yn_i, 1), xn).start()
        _gather_copy(sems, a, 6, _piece(out, yn_i), sib).start()


def _gather_finish(srcs, outs, sems):
    x, y, c, me = _my_place()
    sib = _peer(x, y, c, 1)[0]
    xn, yn, dg_i = _peer(x, y, c, 4)[0], _peer(x, y, c, 2)[0], _peer(x, y, c, 6)[1]
    n = len(srcs)
    for a in range(n):
        out = outs[a]
        _gather_copy(sems, a, 3, _piece(out, dg_i, 0), yn).wait_recv()
        _gather_copy(sems, a, 7, _piece(out, dg_i, 0), sib).start()
        if _splits(out):
            _gather_copy(sems, a, 4, _piece(out, dg_i, 1), xn).wait_recv()
            _gather_copy(sems, a, 8, _piece(out, dg_i, 1), sib).start()
    for a in range(n):
        out = outs[a]
        whole, half = _piece(out, me), _piece(out, me, 0)
        for k in (0, 5, 6):
            _gather_copy(sems, a, k, whole, sib).wait_recv()
        for k in (7, 8) if _splits(out) else (7,):
            _gather_copy(sems, a, k, half, sib).wait_recv()
        for k in (0, 1, 2):
            _gather_copy(sems, a, k, whole, sib, src=srcs[a]).wait_send()
        for k in (5, 6):
            _gather_copy(sems, a, k, whole, sib).wait_send()
        for k in (3, 4, 7, 8) if _splits(out) else (3, 7):
            _gather_copy(sems, a, k, half, sib).wait_send()
        pltpu.make_async_copy(srcs[a], whole, sems[2].at[a]).wait()


def _relay_sems(n):
    return [pltpu.SemaphoreType.DMA((n, 9)), pltpu.SemaphoreType.DMA((n, 9)), pltpu.SemaphoreType.DMA((n,))]


def _gather_sems(n):
    return [pltpu.SemaphoreType.DMA((n, 7)), pltpu.SemaphoreType.DMA((n, 7)), pltpu.SemaphoreType.DMA((n,))]


class _Comm:
    def __init__(self, args, out_shape, scratch, start, finish, relay=None):
        self.args, self.out_shape, self.scratch, self.start, self.finish = args, out_shape, scratch, start, finish
        self.relay = relay


def _gather_comm(shards):
    return _Comm(list(shards), [_sds((N_DEV,) + s.shape[1:], s.dtype) for s in shards], _relay_sems(len(shards)),
                 _gather_start, _gather_finish, relay=_gather_relay)


def _direct_copies(ins, outs, sems, wait):
    send_sems, recv_sems, local_sems = sems
    x, y, c, me = _my_place()
    for a in range(len(ins)):
        local = pltpu.make_async_copy(ins[a].at[pl.ds(me, 1)], outs[a].at[pl.ds(me, 1)], local_sems.at[a])
        local.wait() if wait else local.start()
        for k in range(1, N_DEV):
            to, to_i = _peer(x, y, c, k)
            cp = pltpu.make_async_remote_copy(
                src_ref=ins[a].at[pl.ds(to_i, 1)], dst_ref=outs[a].at[pl.ds(me, 1)],
                send_sem=send_sems.at[a, k - 1], recv_sem=recv_sems.at[a, k - 1], device_id=to, device_id_type=MESH)
            cp.wait() if wait else cp.start()


def _exchange_comm(grads):
    return _Comm(list(grads), [_sds(g.shape, g.dtype) for g in grads], _gather_sems(len(grads)),
                 lambda i, o, s: _direct_copies(i, o, s, False), lambda i, o, s: _direct_copies(i, o, s, True))


def _chip_copies(ins, outs, sems, wait):
    send_sems, recv_sems, local_sems = sems
    x, y, c, _ = _my_place()
    for a in range(len(ins)):
        local = pltpu.make_async_copy(ins[a].at[pl.ds(2 * x + y, 1)], outs[a].at[pl.ds(len(CHIPS), 1)],
                                      local_sems.at[a])
        local.wait() if wait else local.start()
        for i, k in enumerate(CHIPS):
            to = _peer(x, y, c, k)[0]
            cp = pltpu.make_async_remote_copy(
                src_ref=ins[a].at[pl.ds(2 * to[0] + to[1], 1)], dst_ref=outs[a].at[pl.ds(i, 1)],
                send_sem=send_sems.at[a, i], recv_sem=recv_sems.at[a, i], device_id=to, device_id_type=MESH)
            cp.wait() if wait else cp.start()


def _chip_exchange_comm(sums):
    n = len(sums)
    sems = [pltpu.SemaphoreType.DMA((n, len(CHIPS))), pltpu.SemaphoreType.DMA((n, len(CHIPS))),
            pltpu.SemaphoreType.DMA((n,))]
    return _Comm(list(sums), [_sds(s.shape, s.dtype) for s in sums], sems,
                 lambda i, o, s: _chip_copies(i, o, s, False), lambda i, o, s: _chip_copies(i, o, s, True))


def _pair_reduce(name, grads, after=()):
    n = len(grads)
    n_chips = N_DEV // 2

    def body(*refs):
        g_refs, gh_refs, refs = refs[:n], refs[n:2 * n], refs[2 * n + len(after):]
        p_refs, land = refs[:n], refs[n:2 * n]
        send_sems, recv_sems = refs[2 * n:]
        x, y, c, _ = _my_place()
        sib = _peer(x, y, c, 1)[0]
        q = pl.program_id(0)

        def to_sibling(a, j):
            return pltpu.make_async_remote_copy(
                src_ref=gh_refs[a].at[j, pl.ds(1 - c, 1)], dst_ref=land[a].at[pl.ds(j, 1)],
                send_sem=send_sems.at[a, j], recv_sem=recv_sems.at[a, j], device_id=sib, device_id_type=MESH)

        @pl.when(q == 0)
        def _():
            for j in range(n_chips):
                for a in range(n):
                    to_sibling(a, j).start()

        for a in range(n):
            to_sibling(a, q).wait_recv()
            p_refs[a][...] = (g_refs[a][0, pl.ds(c, 1)].astype(F32) + land[a][pl.ds(q, 1)].astype(F32)).astype(BF16)

        @pl.when(q == n_chips - 1)
        def _():
            for a in range(n):
                for j in range(n_chips):
                    to_sibling(a, j).wait_send()

    views = [g.reshape((n_chips, 2) + g.shape[1:]) for g in grads]
    res = pl.pallas_call(
        body, name=name, grid=(n_chips,),
        in_specs=[pl.BlockSpec((1, 2) + g.shape[1:], lambda q: (q, 0, 0, 0)) for g in grads]
        + [ANY] * (n + len(after)),
        out_specs=[pl.BlockSpec((1,) + g.shape[1:], lambda q: (q, 0, 0)) for g in grads],
        out_shape=[_sds((n_chips,) + g.shape[1:], BF16) for g in grads],
        scratch_shapes=[pltpu.VMEM((n_chips,) + g.shape[1:], BF16) for g in grads]
        + [pltpu.SemaphoreType.DMA((n, n_chips)), pltpu.SemaphoreType.DMA((n, n_chips))],
        compiler_params=pltpu.CompilerParams(dimension_semantics=("arbitrary",), vmem_limit_bytes=VMEM_LIMIT),
    )(*views, *views, *after)
    return list(res)


def _pair_exchange_comm(grads):
    n, n_chips = len(grads), N_DEV // 2

    def copies(ins, outs, sems, wait):
        x, y, c, _ = _my_place()
        for j in range(n_chips):
            for a in range(n):
                cp = pltpu.make_async_remote_copy(
                    src_ref=ins[a].at[j, pl.ds(1 - c, 1)], dst_ref=outs[a].at[pl.ds(j, 1)], send_sem=sems[0].at[a, j],
                    recv_sem=sems[1].at[a, j], device_id=_peer(x, y, c, 1)[0], device_id_type=MESH)
                cp.wait() if wait else cp.start()

    views = [g.reshape((n_chips, 2) + g.shape[1:]) for g in grads]
    sems = [pltpu.SemaphoreType.DMA((n, n_chips)), pltpu.SemaphoreType.DMA((n, n_chips))]
    return _Comm(views, [_sds((n_chips,) + g.shape[1:], g.dtype) for g in grads], sems,
                 lambda i, o, s: copies(i, o, s, False), lambda i, o, s: copies(i, o, s, True))


def _pair_add(name, grads, landed, after=()):
    n, n_chips = len(grads), N_DEV // 2

    def body(core_ref, *refs):
        g_refs, l_refs, p_refs = refs[:n], refs[n:2 * n], refs[2 * n + len(after):]
        for a in range(n):
            p_refs[a][...] = (g_refs[a][...].astype(F32) + l_refs[a][...].astype(F32)).astype(BF16)

    views = [g.reshape((n_chips, 2) + g.shape[1:]) for g in grads]
    blocks = [pl.BlockSpec((1,) + g.shape[1:], lambda q, core: (q, 0, 0)) for g in grads]
    mine = [pl.BlockSpec((1, None) + g.shape[1:], lambda q, core: (q, core[0], 0, 0)) for g in grads]
    return list(pl.pallas_call(
        body, name=name, out_shape=[_sds((n_chips,) + g.shape[1:], BF16) for g in grads],
        grid_spec=pltpu.PrefetchScalarGridSpec(num_scalar_prefetch=1, grid=(n_chips,),
                                               in_specs=mine + blocks + [ANY] * len(after), out_specs=blocks),
        compiler_params=pltpu.CompilerParams(dimension_semantics=("arbitrary",), vmem_limit_bytes=VMEM_LIMIT),
    )(lax.axis_index("c").reshape(1), *views, *landed, *after))


def _call(name, body, grid, in_specs, out_specs, out_shape, args, scratch=(), after=()):
    ni, na = len(in_specs), len(after)

    def ordered(*refs):
        body(*refs[:ni], *refs[ni + na:])

    return list(pl.pallas_call(
        ordered if after else body, name=name, grid=grid, in_specs=list(in_specs) + [ANY] * na,
        out_specs=list(out_specs), out_shape=list(out_shape), scratch_shapes=list(scratch),
        compiler_params=pltpu.CompilerParams(dimension_semantics=("arbitrary",) * len(grid),
                                             vmem_limit_bytes=VMEM_LIMIT))(*args, *after))


SIBLING_AND_NEIGHBOURS, OTHER_CHIPS, EVERYONE = (1, 4, 2), CHIPS, tuple(range(1, N_DEV))


def _by_sequencer(name, comm, peers, collective_id):
    src = [jax.new_ref(a, memory_space=pltpu.MemorySpace.HBM) for a in comm.args]
    dst = [jax.empty_ref(s, memory_space=pltpu.MemorySpace.HBM) for s in comm.out_shape]

    @pl.kernel(mesh=plsc.ScalarSubcoreMesh(axis_name="sequencer", num_cores=1), name=name,
               scratch_types=tuple(comm.scratch), compiler_params=pltpu.CompilerParams(collective_id=collective_id))
    def launch(*sems):
        x, y, c, _ = _my_place()
        barrier = pltpu.get_barrier_semaphore()
        for k in peers:
            pl.semaphore_signal(barrier, inc=1, device_id=_peer(x, y, c, k)[0], device_id_type=MESH)
        pl.semaphore_wait(barrier, len(peers))
        comm.start(src, dst, sems)
        if comm.relay is not None:
            comm.relay(src, dst, sems)
        comm.finish(src, dst, sems)

    launch()
    return [d[...] for d in dst]


def _gather_first(first, later):
    nf = len(first)
    layer_of = [(a, l) for a, s in enumerate(later) for l in range(s.shape[0])]
    nl = len(layer_of)
    dts = [BF16] * (nf - 2) + [F32, F32]

    def body(*refs):
        ins, refs = refs[:nf + len(later)], refs[nf + len(later):]
        outs, refs = refs[:nf], refs[nf:]
        casts, refs = refs[:nl], refs[nl:]
        stage, sems = refs[:nf], refs[nf:]
        for a in range(nf):
            stage[a][...] = ins[a][...].astype(dts[a])
            _gather_start(stage, outs, sems, only=a)
        for k, (a, l) in enumerate(layer_of):
            casts[k][...] = ins[nf + a][l:l + 1].astype(BF16)
        _gather_relay(stage, outs, sems)
        _gather_finish(stage, outs, sems)

    res = pl.pallas_call(
        body, name="gather_first",
        in_specs=[VMEM] * (nf + len(later)), out_specs=[ANY] * nf + [VMEM] * nl,
        out_shape=[_sds((N_DEV,) + s.shape[1:], dt) for s, dt in zip(first, dts)]
        + [_sds((1,) + later[a].shape[1:], BF16) for a, _ in layer_of],
        scratch_shapes=[pltpu.VMEM(s.shape, dt) for s, dt in zip(first, dts)] + _relay_sems(nf),
        compiler_params=pltpu.CompilerParams(vmem_limit_bytes=VMEM_LIMIT),
    )(*first, *later)
    return list(res[:nf]), list(res[nf:])


def _a_mix_fwd(x, g, w_in, ln_g, ln_b, w_s, b_st, w_out):
    t = x.shape[0]
    nblk = TM // GMLP_BLOCK

    def body(x_ref, g_ref, win_ref, lng_ref, lnb_ref, ws_ref, bst_ref, wout_ref, h_ref, z_ref, gated_scr):
        xv = x_ref[...]
        hb = _rms_fwd(xv, g_ref[...])[0].astype(BF16)
        for d in range(N_DEV):
            z_ref[:, d * FF_SLOT:(d + 1) * FF_SLOT] = _dot(hb, win_ref[d])
        u = _gelu(z_ref[:, :GATE_DIM])
        vb = _ln_fwd(_gelu(z_ref[:, GATE_DIM:]), lng_ref[...], lnb_ref[...])[0].astype(BF16)
        mask = _gate_mask()
        for gi in range(A_GROUPS):
            wm = jnp.where(mask, ws_ref[gi], 0.0).astype(BF16)
            bias = bst_ref[:, gi:gi + 1]
            cs = slice(gi * A_GROUP_DIM, (gi + 1) * A_GROUP_DIM)
            for n in range(nblk):
                rs = slice(n * GMLP_BLOCK, (n + 1) * GMLP_BLOCK)
                sv = _dot(wm, vb[rs, cs]) + bias
                gated_scr[rs, cs] = (u[rs, cs] * sv).astype(BF16)
        h_ref[...] = xv + _dot(gated_scr[...], wout_ref[...])

    return _call(
        "a_mix_fwd", body, (t // TM,),
        [_row(D_MODEL), _res((1, D_MODEL)), _res((N_DEV, D_MODEL, FF_SLOT)), _res((1, GATE_DIM)),
         _res((1, GATE_DIM)), _res((A_GROUPS, GMLP_BLOCK, GMLP_BLOCK)), _res((GMLP_BLOCK, A_GROUPS)),
         _res((GATE_DIM, D_MODEL))],
        [_row(D_MODEL), _row(2 * GATE_DIM), _row(GATE_DIM)],
        [_sds((t, D_MODEL), F32), _sds((t, 2 * GATE_DIM), F32), _sds((t, GATE_DIM), BF16)],
        (x, g, w_in, ln_g, ln_b, w_s, b_st, w_out))


MLP_W_SPECS = (_res((N_DEV, D_MODEL, FF_SLOT)), _res((N_DEV, FF_SLOT, D_MODEL)))


def _mlp_fwd(h, g, w1, w2):
    t = h.shape[0]

    def body(h_ref, g_ref, w1_ref, w2_ref, o_ref, a_ref):
        hv = h_ref[...]
        hb = _rms_fwd(hv, g_ref[...])[0].astype(BF16)
        o_ref[...] = hv
        for d in range(N_DEV):
            a = _dot(hb, w1_ref[d])
            a_ref[:, d * FF_SLOT:(d + 1) * FF_SLOT] = a
            r = jnp.maximum(a, 0.0)
            o_ref[...] += _dot((r * r).astype(BF16), w2_ref[d])

    return _call(
        "mlp_fwd", body, (t // TM_MLP_FWD,), [_row(D_MODEL, TM_MLP_FWD), _res((1, D_MODEL)), *MLP_W_SPECS],
        [_row(D_MODEL, TM_MLP_FWD), _row(D_FF, TM_MLP_FWD)], [_sds((t, D_MODEL), F32), _sds((t, D_FF), F32)],
        (h, g, w1, w2))


def _mlp_fwd_loss(h, g, w1, w2, final_g, target):
    t = h.shape[0]

    def body(h_ref, g_ref, w1_ref, w2_ref, fg_ref, t_ref, a_ref, loss_ref, dh_ref, dg_ref):
        hv = h_ref[...]
        hb = _rms_fwd(hv, g_ref[...])[0].astype(BF16)
        out = hv
        for d in range(N_DEV):
            a = _dot(hb, w1_ref[d])
            a_ref[:, d * FF_SLOT:(d + 1) * FF_SLOT] = a
            r = jnp.maximum(a, 0.0)
            out = out + _dot((r * r).astype(BF16), w2_ref[d])
        y, xhat, rstd = _rms_fwd(out, fg_ref[...])
        err = y - t_ref[...]
        part = 0.5 * jnp.sum(jnp.mean(err * err, axis=-1, keepdims=True), axis=0, keepdims=True)
        dx, dg = _rms_bwd(err * (1.0 / D_MODEL), xhat, rstd, fg_ref[...])
        dh_ref[...] = dx
        _acc(dg_ref, dg)
        _acc(loss_ref, part)

    return _call(
        "mlp_fwd_loss", body, (t // TM_MLP_FWD,),
        [_row(D_MODEL, TM_MLP_FWD), _res((1, D_MODEL)), *MLP_W_SPECS, _res((1, D_MODEL)), _row(D_MODEL, TM_MLP_FWD)],
        [_row(D_FF, TM_MLP_FWD), _const((1, 1)), _row(D_MODEL, TM_MLP_FWD), _const((1, D_MODEL))],
        [_sds((t, D_FF), F32), _sds((1, 1), F32), _sds((t, D_MODEL), F32), _sds((1, D_MODEL), F32)],
        (h, g, w1, w2, final_g, target))


KVQ_W_SPECS = (_res((1, D_MODEL)), _res((D_MODEL, KV_LORA + QK_ROPE)), _res((1, KV_LORA)),
               _res((B_HEADS, KV_LORA, QK_NOPE + V_HEAD)), _res((1, D_MODEL)), _res((D_MODEL, Q_LORA)),
               _res((1, Q_LORA)), _res((B_HEADS, Q_LORA, QK_NOPE + QK_ROPE)))


def _kvq_fwd(h, pos, inv_freq, kvq_w):
    t = h.shape[0]
    half = QK_ROPE // 2

    def body(h_ref, pos_ref, invf_ref, srcg_ref, wkva_ref, kvag_ref, wkvb_ref, mixg_ref, wqa_ref, qg_ref, wqb_ref,
             ckv_ref, k_ref, v_ref, cqpre_ref, q_ref, cos_ref, sin_ref):
        hv = h_ref[...]
        xhat = hv * lax.rsqrt(jnp.mean(hv * hv, axis=-1, keepdims=True) + EPS)
        ang = pos_ref[...].astype(F32) * invf_ref[...]
        cos, sin = jnp.cos(ang), jnp.sin(ang)
        cos_ref[...] = cos
        sin_ref[...] = sin
        ckv = _dot((xhat * srcg_ref[...]).astype(BF16), wkva_ref[...])
        ckv_ref[...] = ckv
        cb = _rms_fwd(ckv[:, :KV_LORA], kvag_ref[...])[0].astype(BF16)
        kpe = _rope(ckv[:, KV_LORA:], cos, sin).astype(BF16)
        for hd in range(B_HEADS):
            kv = _dot(cb, wkvb_ref[hd])
            k_ref[hd, :, 0:QK_NOPE] = kv[:, :QK_NOPE].astype(BF16)
            k_ref[hd, :, QK_NOPE:] = kpe
            v_ref[hd] = kv[:, QK_NOPE:].astype(BF16)
        cqpre = _dot((xhat * mixg_ref[...]).astype(BF16), wqa_ref[...])
        cqpre_ref[...] = cqpre
        cqb = _rms_fwd(cqpre, qg_ref[...])[0].astype(BF16)
        for hd in range(B_HEADS):
            q = _dot(cqb, wqb_ref[hd])
            q_ref[hd, :, 0:QK_NOPE] = q[:, :QK_NOPE].astype(BF16)
            q_ref[hd, :, QK_NOPE:] = _rope(q[:, QK_NOPE:], cos, sin).astype(BF16)

    tm = TM_KVQ
    return _call(
        "kvq_fwd", body, (t // tm,), [_row(D_MODEL, tm), _row(1, tm), _res((1, half)), *KVQ_W_SPECS],
        [_row(KV_LORA + QK_ROPE, tm), _heads(QK_NOPE + QK_ROPE, tm), _heads(V_HEAD, tm), _row(Q_LORA, tm),
         _heads(QK_NOPE + QK_ROPE, tm), _row(half, tm), _row(half, tm)],
        [_sds((t, KV_LORA + QK_ROPE), F32), _sds((B_HEADS, t, QK_NOPE + QK_ROPE), BF16),
         _sds((B_HEADS, t, V_HEAD), BF16), _sds((t, Q_LORA), F32), _sds((B_HEADS, t, QK_NOPE + QK_ROPE), BF16),
         _sds((t, half), F32), _sds((t, half), F32)],
        (h, pos, inv_freq, *kvq_w))


def _softmax_rows(q, k_ref, k):
    past, upto = k * TM, (k + 1) * TM
    s = _dot_nt(q, k_ref[0:upto, :])
    own = jnp.where(_att_mask(0, TM, TM), s[:, past:], jnp.finfo(F32).min)
    s = own if k == 0 else jnp.concatenate([s[:, :past], own], axis=1)
    e = jnp.exp2((s - jnp.max(s, axis=-1, keepdims=True)) * (ATT_SCALE * LOG2_E))
    return e * (1.0 / jnp.sum(e, axis=-1, keepdims=True))


def _for_my_tile(i, nq, fn):
    for k in range(nq):
        @pl.when(i == k)
        def _(k=k):
            fn(k)


def _attn_fwd(h, q, k, v, w_o):
    t = h.shape[0]
    nq, hps = t // TM, HEADS_PER_STEP

    def body(h_ref, q_ref, k_ref, v_ref, wo_ref, o_ref, att_ref):
        i, pair = pl.program_id(0), pl.program_id(1)

        @pl.when(pair == 0)
        def _():
            o_ref[...] = h_ref[...]

        def tile(kt):
            proj = None
            for j in range(hps):
                hd = pair * hps + j
                p = _softmax_rows(q_ref[j], k_ref.at[hd], kt)
                ob = _dot(p.astype(BF16), v_ref[hd, 0:(kt + 1) * TM, :]).astype(BF16)
                att_ref[j] = ob
                proj = _dot(ob, wo_ref[hd]) if proj is None else proj + _dot(ob, wo_ref[hd])
            o_ref[...] += proj

        _for_my_tile(i, nq, tile)

    def per_head(d):
        return pl.BlockSpec((hps, TM, d), lambda i, pair: (pair, i, 0))

    def resident(shape):
        zeros = (0,) * len(shape)
        return pl.BlockSpec(shape, lambda i, pair: zeros, pipeline_mode=pl.Buffered(1))

    tile_spec = pl.BlockSpec((TM, D_MODEL), lambda i, pair: (i, 0))
    return _call(
        "attn_fwd", body, (nq, B_HEADS // hps),
        [tile_spec, per_head(QK_NOPE + QK_ROPE), resident((B_HEADS, t, QK_NOPE + QK_ROPE)),
         resident((B_HEADS, t, V_HEAD)), resident((B_HEADS, V_HEAD, D_MODEL))],
        [tile_spec, per_head(V_HEAD)], [_sds((t, D_MODEL), F32), _sds((B_HEADS, t, V_HEAD), BF16)],
        (h, q, k, v, w_o))


def _mlp_bwd(h, a, dho, g, w1, w2, layer, after=()):
    t = h.shape[0]

    def body(h_ref, a_ref, dho_ref, g_ref, w1_ref, w2_ref, dhi_ref, dg_ref, hn_ref, f_ref, da_ref, dhib_ref):
        gv = g_ref[...]
        y, xhat, rstd = _rms_fwd(h_ref[...], gv)
        hn_ref[...] = y.astype(BF16)
        dho_v = dho_ref[...]
        dhob = dho_v.astype(BF16)
        dhn = jnp.zeros((TM, D_MODEL), F32)
        for d in range(N_DEV):
            cs = slice(d * FF_SLOT, (d + 1) * FF_SLOT)
            r = jnp.maximum(a_ref[:, cs], 0.0)
            f_ref[:, cs] = (r * r).astype(BF16)
            da = (_dot_nt(dhob, w2_ref[d]) * (2.0 * r)).astype(BF16)
            da_ref[:, cs] = da
            dhn = dhn + _dot_nt(da, w1_ref[d])
        dx, dg = _rms_bwd(dhn, xhat, rstd, gv)
        dhi = dho_v + dx
        dhi_ref[...] = dhi
        dhib_ref[...] = dhi.astype(BF16)
        _acc(dg_ref, dg)

    return _call(
        f"mlp_bwd_{layer}", body, (t // TM,),
        [_row(D_MODEL), _row(D_FF), _row(D_MODEL), _res((1, D_MODEL)), *MLP_W_SPECS],
        [_row(D_MODEL), _const((1, D_MODEL)), _row(D_MODEL), _row(D_FF), _row(D_FF), _row(D_MODEL)],
        [_sds((t, D_MODEL), F32), _sds((1, D_MODEL), F32), _sds((t, D_MODEL), BF16), _sds((t, D_FF), BF16),
         _sds((t, D_FF), BF16), _sds((t, D_MODEL), BF16)],
        (h, a, dho, g, w1, w2), after=after)


def _attn_bwd(dh, q, k, v, w_o, cos, sin, after=()):
    t = dh.shape[0]
    half, hps = QK_ROPE // 2, HEADS_PER_STEP

    def body(dh_ref, q_ref, k_ref, v_ref, wo_ref, cos_ref, sin_ref, dq_ref, dk_ref, dv_ref):
        i = pl.program_id(1)

        @pl.when(i == 0)
        def _():
            dk_ref[...] = jnp.zeros_like(dk_ref)
            dv_ref[...] = jnp.zeros_like(dv_ref)

        def tile(kt):
            keys = slice(0, (kt + 1) * TM)
            for j in range(hps):
                qj = q_ref[j]
                do = _dot_nt(dh_ref[kt * TM:(kt + 1) * TM, :], wo_ref[j]).astype(BF16)
                p = _softmax_rows(qj, k_ref.at[j], kt)
                dp = _dot_nt(do, v_ref[j, keys, :])
                ds = (p * (dp - jnp.sum(p * dp, axis=-1, keepdims=True)) * ATT_SCALE).astype(BF16)
                dq = _dot(ds, k_ref[j, keys, :])
                dq_ref[j, :, 0:QK_NOPE] = dq[:, :QK_NOPE].astype(BF16)
                dq_ref[j, :, QK_NOPE:] = _rope(dq[:, QK_NOPE:], cos_ref[...], -sin_ref[...]).astype(BF16)
                dk_ref[j, keys, :] += _dot_tn(ds, qj)
                dv_ref[j, keys, :] += _dot_tn(p.astype(BF16), do)

        _for_my_tile(i, t // TM, tile)

    def per_pair(rows, d, tiled):
        return pl.BlockSpec((hps, rows, d), (lambda pair, i: (pair, i, 0)) if tiled else (lambda pair, i: (pair, 0, 0)))

    def tile(d):
        return pl.BlockSpec((TM, d), lambda pair, i: (i, 0))

    return _call(
        "attn_bwd", body, (B_HEADS // hps, t // TM),
        [pl.BlockSpec((t, D_MODEL), lambda pair, i: (0, 0), pipeline_mode=pl.Buffered(1)),
         per_pair(TM, QK_NOPE + QK_ROPE, True), per_pair(t, QK_NOPE + QK_ROPE, False), per_pair(t, V_HEAD, False),
         per_pair(V_HEAD, D_MODEL, False), tile(half), tile(half)],
        [per_pair(TM, QK_NOPE + QK_ROPE, True), per_pair(t, QK_NOPE + QK_ROPE, False), per_pair(t, V_HEAD, False)],
        [_sds((B_HEADS, t, QK_NOPE + QK_ROPE), BF16), _sds((B_HEADS, t, QK_NOPE + QK_ROPE), F32),
         _sds((B_HEADS, t, V_HEAD), F32)],
        (dh, q, k, v, w_o, cos, sin), after=after)


def _kvq_bwd(h, dh, ckv, cqpre, dq, dk, dv, cos, sin, kvq_w, after=()):
    t = h.shape[0]
    tm = TM_KVQ
    half, last = QK_ROPE // 2, t // tm - 1
    grad_shapes = [(D_MODEL, Q_LORA), (B_HEADS, Q_LORA, QK_NOPE + QK_ROPE), (D_MODEL, KV_LORA + QK_ROPE),
                   (B_HEADS, KV_LORA, QK_NOPE + V_HEAD)]

    def body(h_ref, dh_ref, ckv_ref, cqpre_ref, dq_ref, dk_ref, dv_ref, cos_ref, sin_ref,
             srcg_ref, wkva_ref, kvag_ref, wkvb_ref, mixg_ref, wqa_ref, qg_ref, wqb_ref,
             dhi_ref, dmixg_ref, dsrcg_ref, dqg_ref, dkvag_ref, gqa_ref, gqb_ref, gkva_ref, gkvb_ref,
             aqa, aqb, akva, akvb):
        @pl.when(pl.program_id(0) == 0)
        def _():
            for acc in (aqa, aqb, akva, akvb):
                acc[...] = jnp.zeros_like(acc)

        hv = h_ref[...]
        rstd = lax.rsqrt(jnp.mean(hv * hv, axis=-1, keepdims=True) + EPS)
        xhat = hv * rstd
        mixg, srcg, qg, kvag = mixg_ref[...], srcg_ref[...], qg_ref[...], kvag_ref[...]
        cq, cqhat, crstd = _rms_fwd(cqpre_ref[...], qg)
        cqb = cq.astype(BF16)
        dcq = jnp.zeros((tm, Q_LORA), F32)
        for hd in range(B_HEADS):
            dcq = dcq + _dot_nt(dq_ref[hd], wqb_ref[hd])
            aqb[hd] += _dot_tn(cqb, dq_ref[hd])
        dcqpre, dqg = _rms_bwd(dcq, cqhat, crstd, qg)
        dcqpre_b = dcqpre.astype(BF16)
        aqa[...] += _dot_tn((xhat * mixg).astype(BF16), dcqpre_b)
        dxq, dmixg = _rms_bwd(_dot_nt(dcqpre_b, wqa_ref[...]), xhat, rstd, mixg)
        ckv = ckv_ref[...]
        c, chat, krstd = _rms_fwd(ckv[:, :KV_LORA], kvag)
        cb = c.astype(BF16)
        dc = jnp.zeros((tm, KV_LORA), F32)
        dkpe = jnp.zeros((tm, QK_ROPE), F32)
        for hd in range(B_HEADS):
            dkv = jnp.concatenate([dk_ref[hd, :, 0:QK_NOPE], dv_ref[hd]], axis=-1).astype(BF16)
            akvb[hd] += _dot_tn(cb, dkv)
            dc = dc + _dot_nt(dkv, wkvb_ref[hd])
            dkpe = dkpe + dk_ref[hd, :, QK_NOPE:]
        dlat, dkvag = _rms_bwd(dc, chat, krstd, kvag)
        dpe = _rope(dkpe, cos_ref[...], -sin_ref[...])
        dckv_b = jnp.concatenate([dlat, dpe], axis=-1).astype(BF16)
        akva[...] += _dot_tn((xhat * srcg).astype(BF16), dckv_b)
        dxk, dsrcg = _rms_bwd(_dot_nt(dckv_b, wkva_ref[...]), xhat, rstd, srcg)
        dhi_ref[...] = dh_ref[...] + dxq + dxk
        _acc(dmixg_ref, dmixg)
        _acc(dsrcg_ref, dsrcg)
        _acc(dqg_ref, dqg)
        _acc(dkvag_ref, dkvag)

        @pl.when(pl.program_id(0) == last)
        def _():
            for out, acc in ((gqa_ref, aqa), (gqb_ref, aqb), (gkva_ref, akva), (gkvb_ref, akvb)):
                out[...] = acc[...].astype(BF16)

    return _call(
        "kvq_bwd", body, (t // tm,),
        [_row(D_MODEL, tm), _row(D_MODEL, tm), _row(KV_LORA + QK_ROPE, tm), _row(Q_LORA, tm),
         _heads(QK_NOPE + QK_ROPE, tm), _heads(QK_NOPE + QK_ROPE, tm), _heads(V_HEAD, tm), _row(half, tm),
         _row(half, tm), *KVQ_W_SPECS],
        [_row(D_MODEL, tm), _const((1, D_MODEL)), _const((1, D_MODEL)), _const((1, Q_LORA)), _const((1, KV_LORA))]
        + [_const(s) for s in grad_shapes],
        [_sds((t, D_MODEL), F32), _sds((1, D_MODEL), F32), _sds((1, D_MODEL), F32), _sds((1, Q_LORA), F32),
         _sds((1, KV_LORA), F32)] + [_sds(s, BF16) for s in grad_shapes],
        (h, dh, ckv, cqpre, dq, dk, dv, cos, sin, *kvq_w), scratch=[pltpu.VMEM(s, F32) for s in grad_shapes],
        after=after)


def _a_mix_bwd(x, z, dh, g, w_in, ln_g, ln_b, w_s, b_st, w_out, after=()):
    t = x.shape[0]
    tm = TM_GATE
    nblk = tm // GMLP_BLOCK

    def body(x_ref, z_ref, dh_ref, g_ref, win_ref, lng_ref, lnb_ref, ws_ref, bst_ref, wout_ref,
             dx_ref, hn_ref, dz_ref, dg_ref, dlng_ref, dlnb_ref, dws_ref, dbs_ref, dvn_scr, gelu_grad_v):
        @pl.when(pl.program_id(0) == 0)
        def _():
            dws_ref[...] = jnp.zeros_like(dws_ref)
            dbs_ref[...] = jnp.zeros_like(dbs_ref)

        gv, lng = g_ref[...], lng_ref[...]
        y, xhat, rstd = _rms_fwd(x_ref[...], gv)
        hn_ref[...] = y.astype(BF16)
        dhv = dh_ref[...]
        dgated = _dot_nt(dhv.astype(BF16), wout_ref[...])
        u, gelu_grad_u = _gelu_and_grad(z_ref[:, :GATE_DIM])
        v, gelu_grad_v[...] = _gelu_and_grad(z_ref[:, GATE_DIM:])
        vn, vhat, lrstd = _ln_fwd(v, lng, lnb_ref[...])
        vb = vn.astype(BF16)
        mask = _gate_mask()
        for gi in range(A_GROUPS):
            wm = jnp.where(mask, ws_ref[gi], 0.0).astype(BF16)
            bias = bst_ref[:, gi:gi + 1]
            cs = slice(gi * A_GROUP_DIM, (gi + 1) * A_GROUP_DIM)
            dws = jnp.zeros((GMLP_BLOCK, GMLP_BLOCK), F32)
            dbs = jnp.zeros((GMLP_BLOCK, 1), F32)
            for n in range(nblk):
                rs = slice(n * GMLP_BLOCK, (n + 1) * GMLP_BLOCK)
                sv = _dot(wm, vb[rs, cs]) + bias
                dz_ref[rs, cs] = (dgated[rs, cs] * sv * gelu_grad_u[rs, cs]).astype(BF16)
                dsv = dgated[rs, cs] * u[rs, cs]
                dsvb = dsv.astype(BF16)
                dws = dws + _dot_nt(dsvb, vb[rs, cs])
                dbs = dbs + jnp.sum(dsv, axis=-1, keepdims=True)
                dvn_scr[rs, cs] = _dot_tn(wm, dsvb)
            dws_ref[gi] += jnp.where(mask, dws, 0.0)
            dbs_ref[gi] += dbs
        dvn = dvn_scr[...]
        dvhat = dvn * lng
        dv = lrstd * (dvhat - jnp.mean(dvhat, axis=-1, keepdims=True)
                      - vhat * jnp.mean(dvhat * vhat, axis=-1, keepdims=True))
        dz_ref[:, GATE_DIM:] = (dv * gelu_grad_v[...]).astype(BF16)
        dhn = jnp.zeros((tm, D_MODEL), F32)
        for d in range(N_DEV):
            dhn = dhn + _dot_nt(dz_ref[:, d * FF_SLOT:(d + 1) * FF_SLOT], win_ref[d])
        dx, dg = _rms_bwd(dhn, xhat, rstd, gv)
        dx_ref[...] = dhv + dx
        _acc(dg_ref, dg)
        _acc(dlng_ref, jnp.sum(dvn * vhat, axis=0, keepdims=True))
        _acc(dlnb_ref, jnp.sum(dvn, axis=0, keepdims=True))

    return _call(
        "a_mix_bwd", body, (t // tm,),
        [_row(D_MODEL, tm), _row(2 * GATE_DIM, tm), _row(D_MODEL, tm), _res((1, D_MODEL)),
         _res((N_DEV, D_MODEL, FF_SLOT)), _res((1, GATE_DIM)), _res((1, GATE_DIM)),
         _res((A_GROUPS, GMLP_BLOCK, GMLP_BLOCK)), _res((GMLP_BLOCK, A_GROUPS)), _res((GATE_DIM, D_MODEL))],
        [_row(D_MODEL, tm), _row(D_MODEL, tm), _row(2 * GATE_DIM, tm),
         _const((1, D_MODEL)), _const((1, GATE_DIM)), _const((1, GATE_DIM)),
         _const((A_GROUPS, GMLP_BLOCK, GMLP_BLOCK)), _const((A_GROUPS, GMLP_BLOCK, 1))],
        [_sds((t, D_MODEL), F32), _sds((t, D_MODEL), BF16),
         _sds((t, 2 * GATE_DIM), BF16), _sds((1, D_MODEL), F32), _sds((1, GATE_DIM), F32),
         _sds((1, GATE_DIM), F32), _sds((A_GROUPS, GMLP_BLOCK, GMLP_BLOCK), F32),
         _sds((A_GROUPS, GMLP_BLOCK, 1), F32)],
        (x, z, dh, g, w_in, ln_g, ln_b, w_s, b_st, w_out),
        scratch=[pltpu.VMEM((tm, GATE_DIM), F32), pltpu.VMEM((tm, GATE_DIM), F32)], after=after)


def _wgrad(name, a, b, a_spec, b_spec, m, n, after=()):
    def body(a_ref, b_ref, o_ref):
        o_ref[0] = _dot_tn(a_ref[...].astype(BF16), b_ref[...].astype(BF16)).astype(BF16)

    return _call(name, body, (N_DEV,), [a_spec, b_spec], [pl.BlockSpec((1, m, n), lambda d: (d, 0, 0))],
                 [_sds((N_DEV, m, n), BF16)], (a, b), after=after)[0]


def _full(t, d):
    return pl.BlockSpec((t, d), lambda i: (0, 0), pipeline_mode=pl.Buffered(1))


def _cols(t, d):
    return pl.BlockSpec((t, d), lambda i: (0, i))


def _head(t, d):
    return pl.BlockSpec((None, t, d), lambda i: (i, 0, 0))


def _local_step(x, pos, target, inv_freq, wg, sm, shards=None):
    t = x.shape[0]
    wg = dict(wg)
    dist = shards is not None
    mix_g = [sm["norm_mix_g"][l:l + 1] for l in range(2)]
    mlp_g = [sm["norm_mlp_g"][l:l + 1] for l in range(2)]

    ids = iter(range(2, 2 + 9))

    def gather(names):
        if dist:
            got = _by_sequencer("gather_" + names[0], _gather_comm([shards[k] for k in names]),
                                SIBLING_AND_NEIGHBOURS, next(ids))
            wg.update(zip(names, got))

    def send(name, names):
        if dist:
            comm = _exchange_comm(grads=[g[k] for k in names])
            g.update(zip(names, _by_sequencer("exchange_" + name, comm, EVERYONE, next(ids))))

    def send_sums(name, names, meanwhile):
        if not dist:
            meanwhile()
            return ()
        grads = [g[k] for k in names]
        landed = _by_sequencer("pair_exchange_" + name, _pair_exchange_comm(grads), (1,), next(ids))
        sums = _pair_add("pair_add_" + name, grads, landed, after=meanwhile())
        g.update(zip(names, _by_sequencer("exchange_" + name, _chip_exchange_comm(sums), OTHER_CHIPS, next(ids))))
        return sums

    def a_args():
        return (wg["a_w_in"], wg["a_ln_v_g"], wg["a_ln_v_b"], sm["a_w_s"], sm["a_b_st"], wg["a_w_out"])

    def kvq_w():
        return (sm["kv_src_norm_g"], wg["kv_w_a"], sm["kv_a_norm_g"], wg["kv_w_b"], mix_g[1], wg["b_w_q_a"],
                sm["b_q_norm_g"], wg["b_w_q_b"])

    gather(("mlp_w1_0", "mlp_w2_0"))
    h1, z, gated = _a_mix_fwd(x, mix_g[0], *a_args())
    gather(("kv_w_a", "kv_w_b", "b_w_q_a", "b_w_q_b", "b_w_o"))
    h2, a0 = _mlp_fwd(h1, mlp_g[0], wg["mlp_w1_0"], wg["mlp_w2_0"])
    if dist:
        wg["b_w_q_a"] = wg["b_w_q_a"].reshape(D_MODEL, Q_LORA)
        wg["kv_w_a"] = wg["kv_w_a"].reshape(D_MODEL, KV_LORA + QK_ROPE)
    gather(("mlp_w1_1", "mlp_w2_1"))
    ckv, k, v, cqpre, q, cos, sin = _kvq_fwd(h2, pos, inv_freq, kvq_w())
    h3, att = _attn_fwd(h2, q, k, v, wg["b_w_o"])
    a1, loss, dh4, d_final_g = _mlp_fwd_loss(h3, mlp_g[1], wg["mlp_w1_1"], wg["mlp_w2_1"], sm["final_norm_g"], target)

    g = {}
    dh3, d_mlp_g1, hn, f, da, dh3_b = _mlp_bwd(h3, a1, dh4, mlp_g[1], wg["mlp_w1_1"], wg["mlp_w2_1"], 1)
    dq, dk, dv = _attn_bwd(dh3_b, q, k, v, wg["b_w_o"], cos, sin)
    g["mlp_w1_1"] = _wgrad("wgrad_w1_1", hn, da, _full(t, D_MODEL), _cols(t, FF_SLOT), D_MODEL, FF_SLOT, after=[dq])
    g["mlp_w2_1"] = _wgrad("wgrad_w2_1", f, dh4, _cols(t, FF_SLOT), _full(t, D_MODEL), FF_SLOT, D_MODEL)

    def wgrad_w_o():
        g["b_w_o"] = _wgrad("wgrad_w_o", att, dh3_b, _head(t, V_HEAD), _full(t, D_MODEL), V_HEAD, D_MODEL)
        return [g["b_w_o"]]

    sums = send_sums("mlp_1", ("mlp_w1_1", "mlp_w2_1"), wgrad_w_o)
    dh2, d_mix_g1, d_src_g, d_q_g, d_kv_a_g, g_q_a, g["b_w_q_b"], g_kv_a, g["kv_w_b"] = _kvq_bwd(
        h2, dh3, ckv, cqpre, dq, dk, dv, cos, sin, kvq_w(), after=sums)
    g["b_w_q_a"] = g_q_a.reshape(N_DEV, D_MODEL // N_DEV, Q_LORA)
    g["kv_w_a"] = g_kv_a.reshape(N_DEV, D_MODEL // N_DEV, KV_LORA + QK_ROPE)
    qkv = ("b_w_q_a", "b_w_q_b", "kv_w_a", "kv_w_b")
    landed = [g[k] for k in qkv]
    send("qkv", qkv)
    dh1, d_mlp_g0, hn, f, da, dh1_b = _mlp_bwd(h1, a0, dh2, mlp_g[0], wg["mlp_w1_0"], wg["mlp_w2_0"], 0,
                                               after=landed if dist else ())
    landed = [g["mlp_w1_1"], g["mlp_w2_1"]] if dist else ()
    g["mlp_w1_0"] = _wgrad("wgrad_w1_0", hn, da, _full(t, D_MODEL), _cols(t, FF_SLOT), D_MODEL, FF_SLOT, after=landed)
    g["mlp_w2_0"] = _wgrad("wgrad_w2_0", f, dh2, _cols(t, FF_SLOT), _full(t, D_MODEL), FF_SLOT, D_MODEL)

    def wgrad_a_w_out():
        g["a_w_out"] = _wgrad("wgrad_a_w_out", gated, dh1_b, _cols(t, GATE_DIM // N_DEV), _full(t, D_MODEL),
                              GATE_DIM // N_DEV, D_MODEL)
        return [g["a_w_out"]] + [g[k] for k in qkv]

    sums = send_sums("mlp_0", ("mlp_w1_0", "mlp_w2_0", "b_w_o"), wgrad_a_w_out)
    dx, hn, dz, d_mix_g0, d_ln_g, d_ln_b, d_ws, d_bs = _a_mix_bwd(x, z, dh1, mix_g[0], *a_args(), after=sums)
    small = {
        "norm_mix_g": jnp.concatenate([d_mix_g0, d_mix_g1], axis=0),
        "norm_mlp_g": jnp.concatenate([d_mlp_g0, d_mlp_g1], axis=0),
        "a_ln_v_g": d_ln_g.reshape(N_DEV, GATE_DIM // N_DEV),
        "a_ln_v_b": d_ln_b.reshape(N_DEV, GATE_DIM // N_DEV),
        "a_w_s": d_ws.astype(BF16) if dist else d_ws,
        "a_b_s": d_bs.reshape(A_GROUPS, GMLP_BLOCK),
        "b_q_norm_g": d_q_g,
        "kv_src_norm_g": d_src_g,
        "kv_a_norm_g": d_kv_a_g,
        "final_norm_g": d_final_g,
    }
    if dist:
        parts = [small[k].reshape((1,) + small[k].shape) for k in SMALL] + [loss.reshape(1, 1, 1)]
        got = _by_sequencer("gather_small", _gather_comm(parts), SIBLING_AND_NEIGHBOURS, next(ids))
        small, loss = dict(zip(SMALL, got)), got[-1]
    g["a_w_in"] = _wgrad("wgrad_a_w_in", hn, dz, _full(t, D_MODEL), _cols(t, FF_SLOT), D_MODEL, FF_SLOT)
    return loss, dx, g, small


def _adamw(w, g, m, v):
    m = ADAM_B1 * m + (1.0 - ADAM_B1) * g
    v = ADAM_B2 * v + (1.0 - ADAM_B2) * (g * g)
    m_hat = m / (1.0 - ADAM_B1 ** ADAM_STEP)
    v_hat = v / (1.0 - ADAM_B2 ** ADAM_STEP)
    return -ADAM_LR * (m_hat / (jnp.sqrt(v_hat) + ADAM_EPS) + ADAM_WD * w), m, v


def _sum_in_device_order(r_ref):
    g = r_ref[0].astype(F32)
    for j in range(1, r_ref.shape[0]):
        g = g + r_ref[j].astype(F32)
    return g


def _adamw_sharded(name, recvs, w, m, v):
    layers, r, c = w.shape
    tr = math.gcd(r, 512)
    flat = [a for per_layer in recvs for a in per_layer]

    def body(*refs):
        r_refs, (w_ref, m_ref, v_ref) = refs[:len(flat)], refs[len(flat):len(flat) + 3]
        g_ref, d_ref, nm_ref, nv_ref = refs[-4:]
        layer = pl.program_id(0)
        g, pos = None, 0
        for li, per_layer in enumerate(recvs):
            total = None
            for ref in r_refs[pos:pos + len(per_layer)]:
                part = _sum_in_device_order(ref)
                total = part if total is None else total + part
            pos += len(per_layer)
            g = total if g is None else jnp.where(layer == li, total, g)
        g_ref[...] = g
        d_ref[...], nm_ref[...], nv_ref[...] = _adamw(w_ref[...], g, m_ref[...], v_ref[...])

    blk = pl.BlockSpec((None, tr, c), lambda l, i: (l, i, 0))
    return _call(name, body, (layers, r // tr),
                 [pl.BlockSpec((a.shape[0], tr, c), lambda l, i: (0, i, 0)) for a in flat] + [blk] * 3,
                 [blk] * 4, [_sds(w.shape, F32)] * 4, (*flat, w, m, v))


def _adamw_small(recvs, ws, ms, vs, own_row, losses):
    n = len(recvs)

    def body(*refs):
        r_refs, w_refs, m_refs, v_refs = (refs[i * n:(i + 1) * n] for i in range(4))
        outs, scr = refs[4 * n + 1:8 * n + 2], refs[8 * n + 2:]
        outs[-1][...] = _sum_in_device_order(refs[4 * n])
        me = _my_place()[3]
        for a in range(n):
            g = _sum_in_device_order(r_refs[a])
            if own_row[a]:
                scr[0][...] = g
                g = scr[0][pl.ds(me, 1), :]
            g_ref, d_ref, nm_ref, nv_ref = outs[4 * a:4 * a + 4]
            g_ref[...] = g
            d_ref[...], nm_ref[...], nv_ref[...] = _adamw(w_refs[a][...], g, m_refs[a][...], v_refs[a][...])

    out_shape = []
    for w in ws:
        out_shape += [_sds(w.shape, F32)] * 4
    return pl.pallas_call(
        body, name="adamw_small", in_specs=[VMEM] * (4 * n + 1), out_specs=[VMEM] * (4 * n + 1),
        out_shape=out_shape + [_sds((1, 1), F32)], scratch_shapes=[pltpu.VMEM((N_DEV, GATE_DIM // N_DEV), F32)],
    )(*recvs, *ws, *ms, *vs, losses)


BIG = ("a_w_in", "a_w_out", "b_w_q_a", "b_w_q_b", "b_w_o", "kv_w_a", "kv_w_b", "mlp_w1", "mlp_w2")
SMALL = ("norm_mix_g", "norm_mlp_g", "a_ln_v_g", "a_ln_v_b", "a_w_s", "a_b_s", "b_q_norm_g", "kv_src_norm_g",
         "kv_a_norm_g", "final_norm_g")
WEIGHTS = ("norm_mix_g", "norm_mlp_g", "a_w_in", "a_ln_v_g", "a_ln_v_b", "a_w_s", "a_b_s", "a_w_out", "b_w_q_a",
           "b_q_norm_g", "b_w_q_b", "b_w_o", "kv_src_norm_g", "kv_w_a", "kv_a_norm_g", "kv_w_b", "mlp_w1", "mlp_w2",
           "final_norm_g")


def _two_d(name, a):
    if name in ("a_w_s", "a_b_s"):
        return a.reshape(a.shape[1:])
    return a.reshape(1, -1) if a.ndim == 1 else a


def _three_d(a):
    return a if a.ndim == 3 else a.reshape((1,) + a.shape)


def kernel(x, positions, norm_mix_g, norm_mlp_g, a_w_in, a_ln_v_g, a_ln_v_b, a_w_s, a_b_s, a_w_out, b_w_q_a, b_q_norm_g, b_w_q_b, b_w_o, kv_src_norm_g, kv_w_a, kv_a_norm_g, kv_w_b, mlp_w1, mlp_w2, final_norm_g, loss_target, m_norm_mix_g, m_norm_mlp_g, m_a_w_in, m_a_ln_v_g, m_a_ln_v_b, m_a_w_s, m_a_b_s, m_a_w_out, m_b_w_q_a, m_b_q_norm_g, m_b_w_q_b, m_b_w_o, m_kv_src_norm_g, m_kv_w_a, m_kv_a_norm_g, m_kv_w_b, m_mlp_w1, m_mlp_w2, m_final_norm_g, v_norm_mix_g, v_norm_mlp_g, v_a_w_in, v_a_ln_v_g, v_a_ln_v_b, v_a_w_s, v_a_b_s, v_a_w_out, v_b_w_q_a, v_b_q_norm_g, v_b_w_q_b, v_b_w_o, v_kv_src_norm_g, v_kv_w_a, v_kv_a_norm_g, v_kv_w_b, v_mlp_w1, v_mlp_w2, v_final_norm_g):
    w = dict(norm_mix_g=norm_mix_g, norm_mlp_g=norm_mlp_g, a_w_in=a_w_in, a_ln_v_g=a_ln_v_g, a_ln_v_b=a_ln_v_b,
             a_w_s=a_w_s, a_b_s=a_b_s, a_w_out=a_w_out, b_w_q_a=b_w_q_a, b_q_norm_g=b_q_norm_g, b_w_q_b=b_w_q_b,
             b_w_o=b_w_o, kv_src_norm_g=kv_src_norm_g, kv_w_a=kv_w_a, kv_a_norm_g=kv_a_norm_g, kv_w_b=kv_w_b,
             mlp_w1=mlp_w1, mlp_w2=mlp_w2, final_norm_g=final_norm_g)
    m = dict(norm_mix_g=m_norm_mix_g, norm_mlp_g=m_norm_mlp_g, a_w_in=m_a_w_in, a_ln_v_g=m_a_ln_v_g,
             a_ln_v_b=m_a_ln_v_b, a_w_s=m_a_w_s, a_b_s=m_a_b_s, a_w_out=m_a_w_out, b_w_q_a=m_b_w_q_a,
             b_q_norm_g=m_b_q_norm_g, b_w_q_b=m_b_w_q_b, b_w_o=m_b_w_o, kv_src_norm_g=m_kv_src_norm_g,
             kv_w_a=m_kv_w_a, kv_a_norm_g=m_kv_a_norm_g, kv_w_b=m_kv_w_b, mlp_w1=m_mlp_w1, mlp_w2=m_mlp_w2,
             final_norm_g=m_final_norm_g)
    v = dict(norm_mix_g=v_norm_mix_g, norm_mlp_g=v_norm_mlp_g, a_w_in=v_a_w_in, a_ln_v_g=v_a_ln_v_g,
             a_ln_v_b=v_a_ln_v_b, a_w_s=v_a_w_s, a_b_s=v_a_b_s, a_w_out=v_a_w_out, b_w_q_a=v_b_w_q_a,
             b_q_norm_g=v_b_q_norm_g, b_w_q_b=v_b_w_q_b, b_w_o=v_b_w_o, kv_src_norm_g=v_kv_src_norm_g,
             kv_w_a=v_kv_w_a, kv_a_norm_g=v_kv_a_norm_g, kv_w_b=v_kv_w_b, mlp_w1=v_mlp_w1, mlp_w2=v_mlp_w2,
             final_norm_g=v_final_norm_g)
    t = x.shape[1]

    first = ("a_w_in", "a_w_out", "a_ln_v_g", "a_ln_v_b")
    later = ("mlp_w1", "mlp_w2", "kv_w_a", "kv_w_b", "b_w_q_a", "b_w_q_b", "b_w_o")
    later_blocks = ("mlp_w1_0", "mlp_w1_1", "mlp_w2_0", "mlp_w2_1") + later[2:]
    got, casts = _gather_first([_three_d(w[k]) if k in BIG else w[k] for k in first], [_three_d(w[k]) for k in later])
    wg = dict(zip(first, got))
    wg["a_w_out"] = wg["a_w_out"].reshape(GATE_DIM, D_MODEL)
    wg["a_ln_v_g"] = wg["a_ln_v_g"].reshape(1, GATE_DIM)
    wg["a_ln_v_b"] = wg["a_ln_v_b"].reshape(1, GATE_DIM)
    shards = dict(zip(later_blocks, casts))

    sm = {k: _two_d(k, w[k]) for k in SMALL if k not in ("a_ln_v_g", "a_ln_v_b")}
    sm["a_b_st"] = sm["a_b_s"].T
    inv_freq = (ROPE_THETA ** (-jnp.arange(0, QK_ROPE, 2, dtype=F32) / QK_ROPE)).reshape(1, QK_ROPE // 2)

    losses, dx, g, small = _local_step(x[0], positions.reshape(t, 1), loss_target[0], inv_freq, wg, sm, shards)

    names = ("a_w_in", "a_w_out")
    sums = _pair_reduce("pair_reduce_a", [g[k] for k in names], after=[g["mlp_w1_0"], g["mlp_w2_0"]])
    g.update(zip(names, _by_sequencer("exchange_last", _chip_exchange_comm(sums), OTHER_CHIPS, collective_id=1)))

    out = {}
    for k in BIG:
        recvs = [[g[k + "_0"]], [g[k + "_1"]]] if k.startswith("mlp") else [[g[k]]]
        res = _adamw_sharded("adamw_" + k, recvs, _three_d(w[k]), _three_d(m[k]), _three_d(v[k]))
        out[k] = [o.reshape(w[k].shape) for o in res]
    own_row = [k in ("a_ln_v_g", "a_ln_v_b") for k in SMALL]
    res = _adamw_small([small[k] for k in SMALL], [_two_d(k, w[k]) for k in SMALL], [_two_d(k, m[k]) for k in SMALL],
                       [_two_d(k, v[k]) for k in SMALL], own_row, losses)
    for i, k in enumerate(SMALL):
        out[k] = [o.reshape(w[k].shape) for o in res[4 * i:4 * i + 4]]

    return (res[-1].reshape(()), dx.reshape(x.shape), *[out[k][0] for k in WEIGHTS], *[out[k][1] for k in WEIGHTS],
            *[out[k][2] for k in WEIGHTS], *[out[k][3] for k in WEIGHTS])
```

```python
import math

import jax
import jax.numpy as jnp
from jax import lax
from jax.experimental import pallas as pl
from jax.experimental.pallas import tpu as pltpu
from jax.experimental.pallas import tpu_sc as plsc

F32, BF16 = jnp.float32, jnp.bfloat16
MESH = pl.DeviceIdType.MESH
ANY = pl.BlockSpec(memory_space=pl.ANY)
VMEM = pl.BlockSpec(memory_space=pltpu.VMEM)

N_DEV = 8
D_MODEL = 1024
CHUNK = 64
GMLP_BLOCK = 128
GATE_DIM = 2048
A_GROUPS = 8
A_GROUP_DIM = GATE_DIM // A_GROUPS
B_HEADS = 8
QK_NOPE, QK_ROPE, V_HEAD = 128, 64, 128
Q_LORA, KV_LORA = 384, 256
ROPE_THETA = 10000.0
D_FF = 4096
FF_SLOT = D_FF // N_DEV
EPS = 1e-6
ATT_SCALE = (QK_NOPE + QK_ROPE) ** -0.5

ADAM_LR, ADAM_B1, ADAM_B2, ADAM_EPS, ADAM_WD, ADAM_STEP = 0.001, 0.9, 0.999, 1e-08, 0.01, 10

TM = 256
TM_GATE = 256
TM_MLP_FWD = 512
TM_KVQ = 512
VMEM_LIMIT = 56 * 1024 * 1024
INV_SQRT2 = 1.0 / math.sqrt(2.0)
INV_SQRT_2PI = 1.0 / math.sqrt(2.0 * math.pi)
LOG2_E = 1.0 / math.log(2.0)
HEADS_PER_STEP = 2


def _dot(a, b):
    return jnp.dot(a, b, preferred_element_type=F32)


def _dot_nt(a, b):
    return lax.dot_general(a, b, (((1,), (1,)), ((), ())), preferred_element_type=F32)


def _dot_tn(a, b):
    return lax.dot_general(a, b, (((0,), (0,)), ((), ())), preferred_element_type=F32)


def _rms_fwd(x, g):
    rstd = lax.rsqrt(jnp.mean(x * x, axis=-1, keepdims=True) + EPS)
    xhat = x * rstd
    return xhat * g, xhat, rstd


def _rms_bwd(dy, xhat, rstd, g):
    dxhat = dy * g
    dx = rstd * (dxhat - xhat * jnp.mean(dxhat * xhat, axis=-1, keepdims=True))
    return dx, jnp.sum(dy * xhat, axis=0, keepdims=True)


def _ln_fwd(v, g, b):
    mu = jnp.mean(v, axis=-1, keepdims=True)
    vc = v - mu
    rstd = lax.rsqrt(jnp.mean(vc * vc, axis=-1, keepdims=True) + EPS)
    vhat = vc * rstd
    return vhat * g + b, vhat, rstd


def _gelu(x):
    return 0.5 * x * (1.0 + lax.erf(x * INV_SQRT2))


def _gelu_and_grad(x):
    cdf = 0.5 * (1.0 + lax.erf(x * INV_SQRT2))
    return x * cdf, cdf + x * jnp.exp(-0.5 * x * x) * INV_SQRT_2PI


def _rope(x, cos, sin):
    x1, x2 = x[:, :QK_ROPE // 2], x[:, QK_ROPE // 2:]
    return jnp.concatenate([x1 * cos - x2 * sin, x2 * cos + x1 * sin], axis=-1)


def _gate_mask():
    row = lax.broadcasted_iota(jnp.int32, (GMLP_BLOCK, GMLP_BLOCK), 0)
    col = lax.broadcasted_iota(jnp.int32, (GMLP_BLOCK, GMLP_BLOCK), 1)
    return (col < CHUNK) | (row >= CHUNK)


def _att_mask(q0, tq, t):
    q = q0 + lax.broadcasted_iota(jnp.int32, (tq, t), 0)
    k = lax.broadcasted_iota(jnp.int32, (tq, t), 1)
    return jnp.right_shift(k, 6) <= jnp.right_shift(q, 6)


def _res(shape, imap=None):
    zeros = (0,) * len(shape)
    return pl.BlockSpec(shape, imap or (lambda i: zeros), pipeline_mode=pl.Buffered(1))


def _const(shape):
    zeros = (0,) * len(shape)
    return pl.BlockSpec(shape, lambda i: zeros)


def _row(d, tm=TM):
    return pl.BlockSpec((tm, d), lambda i: (i, 0))


def _heads(d, tm=TM):
    return pl.BlockSpec((B_HEADS, tm, d), lambda i: (0, i, 0))


def _sds(shape, dt):
    return jax.ShapeDtypeStruct(shape, dt)


def _acc(ref, val):
    @pl.when(pl.program_id(0) == 0)
    def _():
        ref[...] = jnp.zeros_like(ref)
    ref[...] += val


def _my_place():
    x, y, c = lax.axis_index("x"), lax.axis_index("y"), lax.axis_index("c")
    return x, y, c, 4 * x + 2 * y + c


def _peer(x, y, c, k):
    px = 1 - x if k & 4 else x
    py = 1 - y if k & 2 else y
    pc = 1 - c if k & 1 else c
    return (px, py, pc), 4 * px + 2 * py + pc


CHIPS = (2, 4, 6)


def _splits(ref):
    return len(ref.shape) >= 3 and ref.shape[1] % 32 == 0


def _piece(ref, block, half=None):
    if half is None or not _splits(ref):
        return ref.at[pl.ds(block, 1)]
    rows = ref.shape[1] // 2
    return ref.at[pl.ds(block, 1), pl.ds(half * rows, rows)]


def _gather_copy(sems, a, k, piece, to, src=None):
    return pltpu.make_async_remote_copy(
        src_ref=piece if src is None else src, dst_ref=piece, send_sem=sems[0].at[a, k], recv_sem=sems[1].at[a, k],
        device_id=to, device_id_type=MESH)


def _gather_start(srcs, outs, sems, only=None):
    x, y, c, me = _my_place()
    for a in range(len(srcs)) if only is None else (only,):
        mine = _piece(outs[a], me)
        pltpu.make_async_copy(srcs[a], mine, sems[2].at[a]).start()
        for k, rel in enumerate((1, 4, 2)):
            _gather_copy(sems, a, k, mine, _peer(x, y, c, rel)[0], src=srcs[a]).start()


def _gather_relay(srcs, outs, sems):
    x, y, c, _ = _my_place()
    sib = _peer(x, y, c, 1)[0]
    (xn, xn_i), (yn, yn_i) = _peer(x, y, c, 4), _peer(x, y, c, 2)
    for a in range(len(srcs)):
        out = outs[a]
        _gather_copy(sems, a, 1, _piece(out, xn_i), xn).wait_recv()
        _gather_copy(sems, a, 3, _piece(out, xn_i, 0), yn).start()
        _gather_copy(sems, a, 5, _piece(out, xn_i), sib).start()
        _gather_copy(sems, a, 2, _piece(out, yn_i), yn).wait_recv()
        if _splits(out):
            _gather_copy(sems, a, 4, _piece(out, yn_i, 1), xn).start()
        _gather_copy(sems, a, 6, _piece(out, yn_i), sib).start()


def _gather_finish(srcs, outs, sems):
    x, y, c, me = _my_place()
    sib = _peer(x, y, c, 1)[0]
    xn, yn, dg_i = _peer(x, y, c, 4)[0], _peer(x, y, c, 2)[0], _peer(x, y, c, 6)[1]
    n = len(srcs)
    for a in range(n):
        out = outs[a]
        _gather_copy(sems, a, 3, _piece(out, dg_i, 0), yn).wait_recv()
        _gather_copy(sems, a, 7, _piece(out, dg_i, 0), sib).start()
        if _splits(out):
            _gather_copy(sems, a, 4, _piece(out, dg_i, 1), xn).wait_recv()
            _gather_copy(sems, a, 8, _piece(out, dg_i, 1), sib).start()
    for a in range(n):
        out = outs[a]
        whole, half = _piece(out, me), _piece(out, me, 0)
        for k in (0, 5, 6):
            _gather_copy(sems, a, k, whole, sib).wait_recv()
        for k in (7, 8) if _splits(out) else (7,):
            _gather_copy(sems, a, k, half, sib).wait_recv()
        for k in (0, 1, 2):
            _gather_copy(sems, a, k, whole, sib, src=srcs[a]).wait_send()
        for k in (5, 6):
            _gather_copy(sems, a, k, whole, sib).wait_send()
        for k in (3, 4, 7, 8) if _splits(out) else (3, 7):
            _gather_copy(sems, a, k, half, sib).wait_send()
        pltpu.make_async_copy(srcs[a], whole, sems[2].at[a]).wait()


def _relay_sems(n):
    return [pltpu.SemaphoreType.DMA((n, 9)), pltpu.SemaphoreType.DMA((n, 9)), pltpu.SemaphoreType.DMA((n,))]


def _gather_sems(n):
    return [pltpu.SemaphoreType.DMA((n, 7)), pltpu.SemaphoreType.DMA((n, 7)), pltpu.SemaphoreType.DMA((n,))]


class _Comm:
    def __init__(self, args, out_shape, scratch, start, finish, relay=None):
        self.args, self.out_shape, self.scratch, self.start, self.finish = args, out_shape, scratch, start, finish
        self.relay = relay


def _gather_comm(shards):
    return _Comm(list(shards), [_sds((N_DEV,) + s.shape[1:], s.dtype) for s in shards], _relay_sems(len(shards)),
                 _gather_start, _gather_finish, relay=_gather_relay)


def _direct_copies(ins, outs, sems, wait):
    send_sems, recv_sems, local_sems = sems
    x, y, c, me = _my_place()
    for a in range(len(ins)):
        local = pltpu.make_async_copy(ins[a].at[pl.ds(me, 1)], outs[a].at[pl.ds(me, 1)], local_sems.at[a])
        local.wait() if wait else local.start()
        for k in range(1, N_DEV):
            to, to_i = _peer(x, y, c, k)
            cp = pltpu.make_async_remote_copy(
                src_ref=ins[a].at[pl.ds(to_i, 1)], dst_ref=outs[a].at[pl.ds(me, 1)],
                send_sem=send_sems.at[a, k - 1], recv_sem=recv_sems.at[a, k - 1], device_id=to, device_id_type=MESH)
            cp.wait() if wait else cp.start()


def _exchange_comm(grads):
    return _Comm(list(grads), [_sds(g.shape, g.dtype) for g in grads], _gather_sems(len(grads)),
                 lambda i, o, s: _direct_copies(i, o, s, False), lambda i, o, s: _direct_copies(i, o, s, True))


def _chip_copies(ins, outs, sems, wait):
    send_sems, recv_sems, local_sems = sems
    x, y, c, _ = _my_place()
    for a in range(len(ins)):
        local = pltpu.make_async_copy(ins[a].at[pl.ds(2 * x + y, 1)], outs[a].at[pl.ds(len(CHIPS), 1)],
                                      local_sems.at[a])
        local.wait() if wait else local.start()
        for i, k in enumerate(CHIPS):
            to = _peer(x, y, c, k)[0]
            cp = pltpu.make_async_remote_copy(
                src_ref=ins[a].at[pl.ds(2 * to[0] + to[1], 1)], dst_ref=outs[a].at[pl.ds(i, 1)],
                send_sem=send_sems.at[a, i], recv_sem=recv_sems.at[a, i], device_id=to, device_id_type=MESH)
            cp.wait() if wait else cp.start()


def _chip_exchange_comm(sums):
    n = len(sums)
    sems = [pltpu.SemaphoreType.DMA((n, len(CHIPS))), pltpu.SemaphoreType.DMA((n, len(CHIPS))),
            pltpu.SemaphoreType.DMA((n,))]
    return _Comm(list(sums), [_sds(s.shape, s.dtype) for s in sums], sems,
                 lambda i, o, s: _chip_copies(i, o, s, False), lambda i, o, s: _chip_copies(i, o, s, True))


def _pair_reduce(name, grads, after=()):
    n = len(grads)
    n_chips = N_DEV // 2

    def body(*refs):
        g_refs, gh_refs, refs = refs[:n], refs[n:2 * n], refs[2 * n + len(after):]
        p_refs, land = refs[:n], refs[n:2 * n]
        send_sems, recv_sems = refs[2 * n:]
        x, y, c, _ = _my_place()
        sib = _peer(x, y, c, 1)[0]
        q = pl.program_id(0)

        def to_sibling(a, j):
            return pltpu.make_async_remote_copy(
                src_ref=gh_refs[a].at[j, pl.ds(1 - c, 1)], dst_ref=land[a].at[pl.ds(j, 1)],
                send_sem=send_sems.at[a, j], recv_sem=recv_sems.at[a, j], device_id=sib, device_id_type=MESH)

        @pl.when(q == 0)
        def _():
            for j in range(n_chips):
                for a in range(n):
                    to_sibling(a, j).start()

        for a in range(n):
            to_sibling(a, q).wait_recv()
            p_refs[a][...] = (g_refs[a][0, pl.ds(c, 1)].astype(F32) + land[a][pl.ds(q, 1)].astype(F32)).astype(BF16)

        @pl.when(q == n_chips - 1)
        def _():
            for a in range(n):
                for j in range(n_chips):
                    to_sibling(a, j).wait_send()

    views = [g.reshape((n_chips, 2) + g.shape[1:]) for g in grads]
    res = pl.pallas_call(
        body, name=name, grid=(n_chips,),
        in_specs=[pl.BlockSpec((1, 2) + g.shape[1:], lambda q: (q, 0, 0, 0)) for g in grads]
        + [ANY] * (n + len(after)),
        out_specs=[pl.BlockSpec((1,) + g.shape[1:], lambda q: (q, 0, 0)) for g in grads],
        out_shape=[_sds((n_chips,) + g.shape[1:], BF16) for g in grads],
        scratch_shapes=[pltpu.VMEM((n_chips,) + g.shape[1:], BF16) for g in grads]
        + [pltpu.SemaphoreType.DMA((n, n_chips)), pltpu.SemaphoreType.DMA((n, n_chips))],
        compiler_params=pltpu.CompilerParams(dimension_semantics=("arbitrary",), vmem_limit_bytes=VMEM_LIMIT),
    )(*views, *views, *after)
    return list(res)


def _pair_exchange_comm(grads):
    n, n_chips = len(grads), N_DEV // 2

    def copies(ins, outs, sems, wait):
        x, y, c, _ = _my_place()
        for j in range(n_chips):
            for a in range(n):
                cp = pltpu.make_async_remote_copy(
                    src_ref=ins[a].at[j, pl.ds(1 - c, 1)], dst_ref=outs[a].at[pl.ds(j, 1)], send_sem=sems[0].at[a, j],
                    recv_sem=sems[1].at[a, j], device_id=_peer(x, y, c, 1)[0], device_id_type=MESH)
                cp.wait() if wait else cp.start()

    views = [g.reshape((n_chips, 2) + g.shape[1:]) for g in grads]
    sems = [pltpu.SemaphoreType.DMA((n, n_chips)), pltpu.SemaphoreType.DMA((n, n_chips))]
    return _Comm(views, [_sds((n_chips,) + g.shape[1:], g.dtype) for g in grads], sems,
                 lambda i, o, s: copies(i, o, s, False), lambda i, o, s: copies(i, o, s, True))


def _pair_add(name, grads, landed, after=()):
    n, n_chips = len(grads), N_DEV // 2

    def body(core_ref, *refs):
        g_refs, l_refs, p_refs = refs[:n], refs[n:2 * n], refs[2 * n + len(after):]
        for a in range(n):
            p_refs[a][...] = (g_refs[a][...].astype(F32) + l_refs[a][...].astype(F32)).astype(BF16)

    views = [g.reshape((n_chips, 2) + g.shape[1:]) for g in grads]
    blocks = [pl.BlockSpec((1,) + g.shape[1:], lambda q, core: (q, 0, 0)) for g in grads]
    mine = [pl.BlockSpec((1, None) + g.shape[1:], lambda q, core: (q, core[0], 0, 0)) for g in grads]
    return list(pl.pallas_call(
        body, name=name, out_shape=[_sds((n_chips,) + g.shape[1:], BF16) for g in grads],
        grid_spec=pltpu.PrefetchScalarGridSpec(num_scalar_prefetch=1, grid=(n_chips,),
                                               in_specs=mine + blocks + [ANY] * len(after), out_specs=blocks),
        compiler_params=pltpu.CompilerParams(dimension_semantics=("arbitrary",), vmem_limit_bytes=VMEM_LIMIT),
    )(lax.axis_index("c").reshape(1), *views, *landed, *after))


def _call(name, body, grid, in_specs, out_specs, out_shape, args, scratch=(), after=()):
    ni, na = len(in_specs), len(after)

    def ordered(*refs):
        body(*refs[:ni], *refs[ni + na:])

    return list(pl.pallas_call(
        ordered if after else body, name=name, grid=grid, in_specs=list(in_specs) + [ANY] * na,
        out_specs=list(out_specs), out_shape=list(out_shape), scratch_shapes=list(scratch),
        compiler_params=pltpu.CompilerParams(dimension_semantics=("arbitrary",) * len(grid),
                                             vmem_limit_bytes=VMEM_LIMIT))(*args, *after))


SIBLING_AND_NEIGHBOURS, OTHER_CHIPS, EVERYONE = (1, 4, 2), CHIPS, tuple(range(1, N_DEV))


def _by_sequencer(name, comm, peers, collective_id):
    src = [jax.new_ref(a, memory_space=pltpu.MemorySpace.HBM) for a in comm.args]
    dst = [jax.empty_ref(s, memory_space=pltpu.MemorySpace.HBM) for s in comm.out_shape]

    @pl.kernel(mesh=plsc.ScalarSubcoreMesh(axis_name="sequencer", num_cores=1), name=name,
               scratch_types=tuple(comm.scratch), compiler_params=pltpu.CompilerParams(collective_id=collective_id))
    def launch(*sems):
        x, y, c, _ = _my_place()
        barrier = pltpu.get_barrier_semaphore()
        for k in peers:
            pl.semaphore_signal(barrier, inc=1, device_id=_peer(x, y, c, k)[0], device_id_type=MESH)
        pl.semaphore_wait(barrier, len(peers))
        comm.start(src, dst, sems)
        if comm.relay is not None:
            comm.relay(src, dst, sems)
        comm.finish(src, dst, sems)

    launch()
    return [d[...] for d in dst]


def _gather_first(first, later):
    nf = len(first)
    layer_of = [(a, l) for a, s in enumerate(later) for l in range(s.shape[0])]
    nl = len(layer_of)
    dts = [BF16] * (nf - 2) + [F32, F32]

    def body(*refs):
        ins, refs = refs[:nf + len(later)], refs[nf + len(later):]
        outs, refs = refs[:nf], refs[nf:]
        casts, refs = refs[:nl], refs[nl:]
        stage, sems = refs[:nf], refs[nf:]
        for a in range(nf):
            stage[a][...] = ins[a][...].astype(dts[a])
            _gather_start(stage, outs, sems, only=a)
        for k, (a, l) in enumerate(layer_of):
            casts[k][...] = ins[nf + a][l:l + 1].astype(BF16)
        _gather_relay(stage, outs, sems)
        _gather_finish(stage, outs, sems)

    res = pl.pallas_call(
        body, name="gather_first",
        in_specs=[VMEM] * (nf + len(later)), out_specs=[ANY] * nf + [VMEM] * nl,
        out_shape=[_sds((N_DEV,) + s.shape[1:], dt) for s, dt in zip(first, dts)]
        + [_sds((1,) + later[a].shape[1:], BF16) for a, _ in layer_of],
        scratch_shapes=[pltpu.VMEM(s.shape, dt) for s, dt in zip(first, dts)] + _relay_sems(nf),
        compiler_params=pltpu.CompilerParams(vmem_limit_bytes=VMEM_LIMIT),
    )(*first, *later)
    return list(res[:nf]), list(res[nf:])


def _a_mix_fwd(x, g, w_in, ln_g, ln_b, w_s, b_st, w_out):
    t = x.shape[0]
    nblk = TM // GMLP_BLOCK

    def body(x_ref, g_ref, win_ref, lng_ref, lnb_ref, ws_ref, bst_ref, wout_ref, h_ref, z_ref, gated_scr):
        xv = x_ref[...]
        hb = _rms_fwd(xv, g_ref[...])[0].astype(BF16)
        for d in range(N_DEV):
            z_ref[:, d * FF_SLOT:(d + 1) * FF_SLOT] = _dot(hb, win_ref[d])
        u = _gelu(z_ref[:, :GATE_DIM])
        vb = _ln_fwd(_gelu(z_ref[:, GATE_DIM:]), lng_ref[...], lnb_ref[...])[0].astype(BF16)
        mask = _gate_mask()
        for gi in range(A_GROUPS):
            wm = jnp.where(mask, ws_ref[gi], 0.0).astype(BF16)
            bias = bst_ref[:, gi:gi + 1]
            cs = slice(gi * A_GROUP_DIM, (gi + 1) * A_GROUP_DIM)
            for n in range(nblk):
                rs = slice(n * GMLP_BLOCK, (n + 1) * GMLP_BLOCK)
                sv = _dot(wm, vb[rs, cs]) + bias
                gated_scr[rs, cs] = (u[rs, cs] * sv).astype(BF16)
        h_ref[...] = xv + _dot(gated_scr[...], wout_ref[...])

    return _call(
        "a_mix_fwd", body, (t // TM,),
        [_row(D_MODEL), _res((1, D_MODEL)), _res((N_DEV, D_MODEL, FF_SLOT)), _res((1, GATE_DIM)),
         _res((1, GATE_DIM)), _res((A_GROUPS, GMLP_BLOCK, GMLP_BLOCK)), _res((GMLP_BLOCK, A_GROUPS)),
         _res((GATE_DIM, D_MODEL))],
        [_row(D_MODEL), _row(2 * GATE_DIM), _row(GATE_DIM)],
        [_sds((t, D_MODEL), F32), _sds((t, 2 * GATE_DIM), F32), _sds((t, GATE_DIM), BF16)],
        (x, g, w_in, ln_g, ln_b, w_s, b_st, w_out))


MLP_W_SPECS = (_res((N_DEV, D_MODEL, FF_SLOT)), _res((N_DEV, FF_SLOT, D_MODEL)))


def _mlp_fwd(h, g, w1, w2):
    t = h.shape[0]

    def body(h_ref, g_ref, w1_ref, w2_ref, o_ref, a_ref):
        hv = h_ref[...]
        hb = _rms_fwd(hv, g_ref[...])[0].astype(BF16)
        o_ref[...] = hv
        for d in range(N_DEV):
            a = _dot(hb, w1_ref[d])
            a_ref[:, d * FF_SLOT:(d + 1) * FF_SLOT] = a
            r = jnp.maximum(a, 0.0)
            o_ref[...] += _dot((r * r).astype(BF16), w2_ref[d])

    return _call(
        "mlp_fwd", body, (t // TM_MLP_FWD,), [_row(D_MODEL, TM_MLP_FWD), _res((1, D_MODEL)), *MLP_W_SPECS],
        [_row(D_MODEL, TM_MLP_FWD), _row(D_FF, TM_MLP_FWD)], [_sds((t, D_MODEL), F32), _sds((t, D_FF), F32)],
        (h, g, w1, w2))


def _mlp_fwd_loss(h, g, w1, w2, final_g, target):
    t = h.shape[0]

    def body(h_ref, g_ref, w1_ref, w2_ref, fg_ref, t_ref, a_ref, loss_ref, dh_ref, dg_ref):
        hv = h_ref[...]
        hb = _rms_fwd(hv, g_ref[...])[0].astype(BF16)
        out = hv
        for d in range(N_DEV):
            a = _dot(hb, w1_ref[d])
            a_ref[:, d * FF_SLOT:(d + 1) * FF_SLOT] = a
            r = jnp.maximum(a, 0.0)
            out = out + _dot((r * r).astype(BF16), w2_ref[d])
        y, xhat, rstd = _rms_fwd(out, fg_ref[...])
        err = y - t_ref[...]
        part = 0.5 * jnp.sum(jnp.mean(err * err, axis=-1, keepdims=True), axis=0, keepdims=True)
        dx, dg = _rms_bwd(err * (1.0 / D_MODEL), xhat, rstd, fg_ref[...])
        dh_ref[...] = dx
        _acc(dg_ref, dg)
        _acc(loss_ref, part)

    return _call(
        "mlp_fwd_loss", body, (t // TM,),
        [_row(D_MODEL), _res((1, D_MODEL)), *MLP_W_SPECS, _res((1, D_MODEL)), _row(D_MODEL)],
        [_row(D_FF), _const((1, 1)), _row(D_MODEL), _const((1, D_MODEL))],
        [_sds((t, D_FF), F32), _sds((1, 1), F32), _sds((t, D_MODEL), F32), _sds((1, D_MODEL), F32)],
        (h, g, w1, w2, final_g, target))


KVQ_W_SPECS = (_res((1, D_MODEL)), _res((D_MODEL, KV_LORA + QK_ROPE)), _res((1, KV_LORA)),
               _res((B_HEADS, KV_LORA, QK_NOPE + V_HEAD)), _res((1, D_MODEL)), _res((D_MODEL, Q_LORA)),
               _res((1, Q_LORA)), _res((B_HEADS, Q_LORA, QK_NOPE + QK_ROPE)))


def _kvq_fwd(h, pos, inv_freq, kvq_w):
    t = h.shape[0]
    half = QK_ROPE // 2

    def body(h_ref, pos_ref, invf_ref, srcg_ref, wkva_ref, kvag_ref, wkvb_ref, mixg_ref, wqa_ref, qg_ref, wqb_ref,
             ckv_ref, k_ref, v_ref, cqpre_ref, q_ref, cos_ref, sin_ref):
        hv = h_ref[...]
        xhat = hv * lax.rsqrt(jnp.mean(hv * hv, axis=-1, keepdims=True) + EPS)
        ang = pos_ref[...].astype(F32) * invf_ref[...]
        cos, sin = jnp.cos(ang), jnp.sin(ang)
        cos_ref[...] = cos
        sin_ref[...] = sin
        ckv = _dot((xhat * srcg_ref[...]).astype(BF16), wkva_ref[...])
        ckv_ref[...] = ckv
        cb = _rms_fwd(ckv[:, :KV_LORA], kvag_ref[...])[0].astype(BF16)
        kpe = _rope(ckv[:, KV_LORA:], cos, sin).astype(BF16)
        for hd in range(B_HEADS):
            kv = _dot(cb, wkvb_ref[hd])
            k_ref[hd, :, 0:QK_NOPE] = kv[:, :QK_NOPE].astype(BF16)
            k_ref[hd, :, QK_NOPE:] = kpe
            v_ref[hd] = kv[:, QK_NOPE:].astype(BF16)
        cqpre = _dot((xhat * mixg_ref[...]).astype(BF16), wqa_ref[...])
        cqpre_ref[...] = cqpre
        cqb = _rms_fwd(cqpre, qg_ref[...])[0].astype(BF16)
        for hd in range(B_HEADS):
            q = _dot(cqb, wqb_ref[hd])
            q_ref[hd, :, 0:QK_NOPE] = q[:, :QK_NOPE].astype(BF16)
            q_ref[hd, :, QK_NOPE:] = _rope(q[:, QK_NOPE:], cos, sin).astype(BF16)

    tm = TM_KVQ
    return _call(
        "kvq_fwd", body, (t // tm,), [_row(D_MODEL, tm), _row(1, tm), _res((1, half)), *KVQ_W_SPECS],
        [_row(KV_LORA + QK_ROPE, tm), _heads(QK_NOPE + QK_ROPE, tm), _heads(V_HEAD, tm), _row(Q_LORA, tm),
         _heads(QK_NOPE + QK_ROPE, tm), _row(half, tm), _row(half, tm)],
        [_sds((t, KV_LORA + QK_ROPE), F32), _sds((B_HEADS, t, QK_NOPE + QK_ROPE), BF16),
         _sds((B_HEADS, t, V_HEAD), BF16), _sds((t, Q_LORA), F32), _sds((B_HEADS, t, QK_NOPE + QK_ROPE), BF16),
         _sds((t, half), F32), _sds((t, half), F32)],
        (h, pos, inv_freq, *kvq_w))


def _softmax_rows(q, k_ref, k):
    past, upto = k * TM, (k + 1) * TM
    s = _dot_nt(q, k_ref[0:upto, :])
    own = jnp.where(_att_mask(0, TM, TM), s[:, past:], jnp.finfo(F32).min)
    s = own if k == 0 else jnp.concatenate([s[:, :past], own], axis=1)
    e = jnp.exp2((s - jnp.max(s, axis=-1, keepdims=True)) * (ATT_SCALE * LOG2_E))
    return e * (1.0 / jnp.sum(e, axis=-1, keepdims=True))


def _for_my_tile(i, nq, fn):
    for k in range(nq):
        @pl.when(i == k)
        def _(k=k):
            fn(k)


def _attn_fwd(h, q, k, v, w_o):
    t = h.shape[0]
    nq, hps = t // TM, HEADS_PER_STEP

    def body(h_ref, q_ref, k_ref, v_ref, wo_ref, o_ref, att_ref):
        i, pair = pl.program_id(0), pl.program_id(1)

        @pl.when(pair == 0)
        def _():
            o_ref[...] = h_ref[...]

        def tile(kt):
            proj = None
            for j in range(hps):
                hd = pair * hps + j
                p = _softmax_rows(q_ref[j], k_ref.at[hd], kt)
                ob = _dot(p.astype(BF16), v_ref[hd, 0:(kt + 1) * TM, :]).astype(BF16)
                att_ref[j] = ob
                proj = _dot(ob, wo_ref[hd]) if proj is None else proj + _dot(ob, wo_ref[hd])
            o_ref[...] += proj

        _for_my_tile(i, nq, tile)

    def per_head(d):
        return pl.BlockSpec((hps, TM, d), lambda i, pair: (pair, i, 0))

    def resident(shape):
        zeros = (0,) * len(shape)
        return pl.BlockSpec(shape, lambda i, pair: zeros, pipeline_mode=pl.Buffered(1))

    tile_spec = pl.BlockSpec((TM, D_MODEL), lambda i, pair: (i, 0))
    return _call(
        "attn_fwd", body, (nq, B_HEADS // hps),
        [tile_spec, per_head(QK_NOPE + QK_ROPE), resident((B_HEADS, t, QK_NOPE + QK_ROPE)),
         resident((B_HEADS, t, V_HEAD)), resident((B_HEADS, V_HEAD, D_MODEL))],
        [tile_spec, per_head(V_HEAD)], [_sds((t, D_MODEL), F32), _sds((B_HEADS, t, V_HEAD), BF16)],
        (h, q, k, v, w_o))


def _mlp_bwd(h, a, dho, g, w1, w2, layer, after=()):
    t = h.shape[0]

    def body(h_ref, a_ref, dho_ref, g_ref, w1_ref, w2_ref, dhi_ref, dg_ref, hn_ref, f_ref, da_ref, dhib_ref):
        gv = g_ref[...]
        y, xhat, rstd = _rms_fwd(h_ref[...], gv)
        hn_ref[...] = y.astype(BF16)
        dho_v = dho_ref[...]
        dhob = dho_v.astype(BF16)
        dhn = jnp.zeros((TM, D_MODEL), F32)
        for d in range(N_DEV):
            cs = slice(d * FF_SLOT, (d + 1) * FF_SLOT)
            r = jnp.maximum(a_ref[:, cs], 0.0)
            f_ref[:, cs] = (r * r).astype(BF16)
            da = (_dot_nt(dhob, w2_ref[d]) * (2.0 * r)).astype(BF16)
            da_ref[:, cs] = da
            dhn = dhn + _dot_nt(da, w1_ref[d])
        dx, dg = _rms_bwd(dhn, xhat, rstd, gv)
        dhi = dho_v + dx
        dhi_ref[...] = dhi
        dhib_ref[...] = dhi.astype(BF16)
        _acc(dg_ref, dg)

    return _call(
        f"mlp_bwd_{layer}", body, (t // TM,),
        [_row(D_MODEL), _row(D_FF), _row(D_MODEL), _res((1, D_MODEL)), *MLP_W_SPECS],
        [_row(D_MODEL), _const((1, D_MODEL)), _row(D_MODEL), _row(D_FF), _row(D_FF), _row(D_MODEL)],
        [_sds((t, D_MODEL), F32), _sds((1, D_MODEL), F32), _sds((t, D_MODEL), BF16), _sds((t, D_FF), BF16),
         _sds((t, D_FF), BF16), _sds((t, D_MODEL), BF16)],
        (h, a, dho, g, w1, w2), after=after)


def _attn_bwd(dh, q, k, v, w_o, cos, sin, after=()):
    t = dh.shape[0]
    half, hps = QK_ROPE // 2, HEADS_PER_STEP

    def body(dh_ref, q_ref, k_ref, v_ref, wo_ref, cos_ref, sin_ref, dq_ref, dk_ref, dv_ref):
        i = pl.program_id(1)

        @pl.when(i == 0)
        def _():
            dk_ref[...] = jnp.zeros_like(dk_ref)
            dv_ref[...] = jnp.zeros_like(dv_ref)

        def tile(kt):
            keys = slice(0, (kt + 1) * TM)
            for j in range(hps):
                qj = q_ref[j]
                do = _dot_nt(dh_ref[kt * TM:(kt + 1) * TM, :], wo_ref[j]).astype(BF16)
                p = _softmax_rows(qj, k_ref.at[j], kt)
                dp = _dot_nt(do, v_ref[j, keys, :])
                ds = (p * (dp - jnp.sum(p * dp, axis=-1, keepdims=True)) * ATT_SCALE).astype(BF16)
                dq = _dot(ds, k_ref[j, keys, :])
                dq_ref[j, :, 0:QK_NOPE] = dq[:, :QK_NOPE].astype(BF16)
                dq_ref[j, :, QK_NOPE:] = _rope(dq[:, QK_NOPE:], cos_ref[...], -sin_ref[...]).astype(BF16)
                dk_ref[j, keys, :] += _dot_tn(ds, qj)
                dv_ref[j, keys, :] += _dot_tn(p.astype(BF16), do)

        _for_my_tile(i, t // TM, tile)

    def per_pair(rows, d, tiled):
        return pl.BlockSpec((hps, rows, d), (lambda pair, i: (pair, i, 0)) if tiled else (lambda pair, i: (pair, 0, 0)))

    def tile(d):
        return pl.BlockSpec((TM, d), lambda pair, i: (i, 0))

    return _call(
        "attn_bwd", body, (B_HEADS // hps, t // TM),
        [pl.BlockSpec((t, D_MODEL), lambda pair, i: (0, 0), pipeline_mode=pl.Buffered(1)),
         per_pair(TM, QK_NOPE + QK_ROPE, True), per_pair(t, QK_NOPE + QK_ROPE, False), per_pair(t, V_HEAD, False),
         per_pair(V_HEAD, D_MODEL, False), tile(half), tile(half)],
        [per_pair(TM, QK_NOPE + QK_ROPE, True), per_pair(t, QK_NOPE + QK_ROPE, False), per_pair(t, V_HEAD, False)],
        [_sds((B_HEADS, t, QK_NOPE + QK_ROPE), BF16), _sds((B_HEADS, t, QK_NOPE + QK_ROPE), F32),
         _sds((B_HEADS, t, V_HEAD), F32)],
        (dh, q, k, v, w_o, cos, sin), after=after)


def _kvq_bwd(h, dh, ckv, cqpre, dq, dk, dv, cos, sin, kvq_w, after=()):
    t = h.shape[0]
    tm = TM
    half, last = QK_ROPE // 2, t // tm - 1
    grad_shapes = [(D_MODEL, Q_LORA), (B_HEADS, Q_LORA, QK_NOPE + QK_ROPE), (D_MODEL, KV_LORA + QK_ROPE),
                   (B_HEADS, KV_LORA, QK_NOPE + V_HEAD)]

    def body(h_ref, dh_ref, ckv_ref, cqpre_ref, dq_ref, dk_ref, dv_ref, cos_ref, sin_ref,
             srcg_ref, wkva_ref, kvag_ref, wkvb_ref, mixg_ref, wqa_ref, qg_ref, wqb_ref,
             dhi_ref, dmixg_ref, dsrcg_ref, dqg_ref, dkvag_ref, gqa_ref, gqb_ref, gkva_ref, gkvb_ref,
             aqa, aqb, akva, akvb):
        @pl.when(pl.program_id(0) == 0)
        def _():
            for acc in (aqa, aqb, akva, akvb):
                acc[...] = jnp.zeros_like(acc)

        hv = h_ref[...]
        rstd = lax.rsqrt(jnp.mean(hv * hv, axis=-1, keepdims=True) + EPS)
        xhat = hv * rstd
        mixg, srcg, qg, kvag = mixg_ref[...], srcg_ref[...], qg_ref[...], kvag_ref[...]
        cq, cqhat, crstd = _rms_fwd(cqpre_ref[...], qg)
        cqb = cq.astype(BF16)
        dcq = jnp.zeros((tm, Q_LORA), F32)
        for hd in range(B_HEADS):
            dcq = dcq + _dot_nt(dq_ref[hd], wqb_ref[hd])
            aqb[hd] += _dot_tn(cqb, dq_ref[hd])
        dcqpre, dqg = _rms_bwd(dcq, cqhat, crstd, qg)
        dcqpre_b = dcqpre.astype(BF16)
        aqa[...] += _dot_tn((xhat * mixg).astype(BF16), dcqpre_b)
        dxq, dmixg = _rms_bwd(_dot_nt(dcqpre_b, wqa_ref[...]), xhat, rstd, mixg)
        ckv = ckv_ref[...]
        c, chat, krstd = _rms_fwd(ckv[:, :KV_LORA], kvag)
        cb = c.astype(BF16)
        dc = jnp.zeros((tm, KV_LORA), F32)
        dkpe = jnp.zeros((tm, QK_ROPE), F32)
        for hd in range(B_HEADS):
            dkv = jnp.concatenate([dk_ref[hd, :, 0:QK_NOPE], dv_ref[hd]], axis=-1).astype(BF16)
            akvb[hd] += _dot_tn(cb, dkv)
            dc = dc + _dot_nt(dkv, wkvb_ref[hd])
            dkpe = dkpe + dk_ref[hd, :, QK_NOPE:]
        dlat, dkvag = _rms_bwd(dc, chat, krstd, kvag)
        dpe = _rope(dkpe, cos_ref[...], -sin_ref[...])
        dckv_b = jnp.concatenate([dlat, dpe], axis=-1).astype(BF16)
        akva[...] += _dot_tn((xhat * srcg).astype(BF16), dckv_b)
        dxk, dsrcg = _rms_bwd(_dot_nt(dckv_b, wkva_ref[...]), xhat, rstd, srcg)
        dhi_ref[...] = dh_ref[...] + dxq + dxk
        _acc(dmixg_ref, dmixg)
        _acc(dsrcg_ref, dsrcg)
        _acc(dqg_ref, dqg)
        _acc(dkvag_ref, dkvag)

        @pl.when(pl.program_id(0) == last)
        def _():
            for out, acc in ((gqa_ref, aqa), (gqb_ref, aqb), (gkva_ref, akva), (gkvb_ref, akvb)):
                out[...] = acc[...].astype(BF16)

    return _call(
        "kvq_bwd", body, (t // tm,),
        [_row(D_MODEL, tm), _row(D_MODEL, tm), _row(KV_LORA + QK_ROPE, tm), _row(Q_LORA, tm),
         _heads(QK_NOPE + QK_ROPE, tm), _heads(QK_NOPE + QK_ROPE, tm), _heads(V_HEAD, tm), _row(half, tm),
         _row(half, tm), *KVQ_W_SPECS],
        [_row(D_MODEL, tm), _const((1, D_MODEL)), _const((1, D_MODEL)), _const((1, Q_LORA)), _const((1, KV_LORA))]
        + [_const(s) for s in grad_shapes],
        [_sds((t, D_MODEL), F32), _sds((1, D_MODEL), F32), _sds((1, D_MODEL), F32), _sds((1, Q_LORA), F32),
         _sds((1, KV_LORA), F32)] + [_sds(s, BF16) for s in grad_shapes],
        (h, dh, ckv, cqpre, dq, dk, dv, cos, sin, *kvq_w), scratch=[pltpu.VMEM(s, F32) for s in grad_shapes],
        after=after)


def _a_mix_bwd(x, z, dh, g, w_in, ln_g, ln_b, w_s, b_st, w_out, after=()):
    t = x.shape[0]
    tm = TM_GATE
    nblk = tm // GMLP_BLOCK

    def body(x_ref, z_ref, dh_ref, g_ref, win_ref, lng_ref, lnb_ref, ws_ref, bst_ref, wout_ref,
             dx_ref, hn_ref, dz_ref, dg_ref, dlng_ref, dlnb_ref, dws_ref, dbs_ref, dvn_scr, gelu_grad_v):
        @pl.when(pl.program_id(0) == 0)
        def _():
            dws_ref[...] = jnp.zeros_like(dws_ref)
            dbs_ref[...] = jnp.zeros_like(dbs_ref)

        gv, lng = g_ref[...], lng_ref[...]
        y, xhat, rstd = _rms_fwd(x_ref[...], gv)
        hn_ref[...] = y.astype(BF16)
        dhv = dh_ref[...]
        dgated = _dot_nt(dhv.astype(BF16), wout_ref[...])
        u, gelu_grad_u = _gelu_and_grad(z_ref[:, :GATE_DIM])
        v, gelu_grad_v[...] = _gelu_and_grad(z_ref[:, GATE_DIM:])
        vn, vhat, lrstd = _ln_fwd(v, lng, lnb_ref[...])
        vb = vn.astype(BF16)
        mask = _gate_mask()
        for gi in range(A_GROUPS):
            wm = jnp.where(mask, ws_ref[gi], 0.0).astype(BF16)
            bias = bst_ref[:, gi:gi + 1]
            cs = slice(gi * A_GROUP_DIM, (gi + 1) * A_GROUP_DIM)
            dws = jnp.zeros((GMLP_BLOCK, GMLP_BLOCK), F32)
            dbs = jnp.zeros((GMLP_BLOCK, 1), F32)
            for n in range(nblk):
                rs = slice(n * GMLP_BLOCK, (n + 1) * GMLP_BLOCK)
                sv = _dot(wm, vb[rs, cs]) + bias
                dz_ref[rs, cs] = (dgated[rs, cs] * sv * gelu_grad_u[rs, cs]).astype(BF16)
                dsv = dgated[rs, cs] * u[rs, cs]
                dsvb = dsv.astype(BF16)
                dws = dws + _dot_nt(dsvb, vb[rs, cs])
                dbs = dbs + jnp.sum(dsv, axis=-1, keepdims=True)
                dvn_scr[rs, cs] = _dot_tn(wm, dsvb)
            dws_ref[gi] += jnp.where(mask, dws, 0.0)
            dbs_ref[gi] += dbs
        dvn = dvn_scr[...]
        dvhat = dvn * lng
        dv = lrstd * (dvhat - jnp.mean(dvhat, axis=-1, keepdims=True)
                      - vhat * jnp.mean(dvhat * vhat, axis=-1, keepdims=True))
        dz_ref[:, GATE_DIM:] = (dv * gelu_grad_v[...]).astype(BF16)
        dhn = jnp.zeros((tm, D_MODEL), F32)
        for d in range(N_DEV):
            dhn = dhn + _dot_nt(dz_ref[:, d * FF_SLOT:(d + 1) * FF_SLOT], win_ref[d])
        dx, dg = _rms_bwd(dhn, xhat, rstd, gv)
        dx_ref[...] = dhv + dx
        _acc(dg_ref, dg)
        _acc(dlng_ref, jnp.sum(dvn * vhat, axis=0, keepdims=True))
        _acc(dlnb_ref, jnp.sum(dvn, axis=0, keepdims=True))

    return _call(
        "a_mix_bwd", body, (t // tm,),
        [_row(D_MODEL, tm), _row(2 * GATE_DIM, tm), _row(D_MODEL, tm), _res((1, D_MODEL)),
         _res((N_DEV, D_MODEL, FF_SLOT)), _res((1, GATE_DIM)), _res((1, GATE_DIM)),
         _res((A_GROUPS, GMLP_BLOCK, GMLP_BLOCK)), _res((GMLP_BLOCK, A_GROUPS)), _res((GATE_DIM, D_MODEL))],
        [_row(D_MODEL, tm), _row(D_MODEL, tm), _row(2 * GATE_DIM, tm),
         _const((1, D_MODEL)), _const((1, GATE_DIM)), _const((1, GATE_DIM)),
         _const((A_GROUPS, GMLP_BLOCK, GMLP_BLOCK)), _const((A_GROUPS, GMLP_BLOCK, 1))],
        [_sds((t, D_MODEL), F32), _sds((t, D_MODEL), BF16),
         _sds((t, 2 * GATE_DIM), BF16), _sds((1, D_MODEL), F32), _sds((1, GATE_DIM), F32),
         _sds((1, GATE_DIM), F32), _sds((A_GROUPS, GMLP_BLOCK, GMLP_BLOCK), F32),
         _sds((A_GROUPS, GMLP_BLOCK, 1), F32)],
        (x, z, dh, g, w_in, ln_g, ln_b, w_s, b_st, w_out),
        scratch=[pltpu.VMEM((tm, GATE_DIM), F32), pltpu.VMEM((tm, GATE_DIM), F32)], after=after)


def _wgrad(name, a, b, a_spec, b_spec, m, n, after=()):
    def body(a_ref, b_ref, o_ref):
        o_ref[0] = _dot_tn(a_ref[...].astype(BF16), b_ref[...].astype(BF16)).astype(BF16)

    return _call(name, body, (N_DEV,), [a_spec, b_spec], [pl.BlockSpec((1, m, n), lambda d: (d, 0, 0))],
                 [_sds((N_DEV, m, n), BF16)], (a, b), after=after)[0]


def _full(t, d):
    return pl.BlockSpec((t, d), lambda i: (0, 0), pipeline_mode=pl.Buffered(1))


def _cols(t, d):
    return pl.BlockSpec((t, d), lambda i: (0, i))


def _head(t, d):
    return pl.BlockSpec((None, t, d), lambda i: (i, 0, 0))


def _local_step(x, pos, target, inv_freq, wg, sm, shards=None):
    t = x.shape[0]
    wg = dict(wg)
    dist = shards is not None
    mix_g = [sm["norm_mix_g"][l:l + 1] for l in range(2)]
    mlp_g = [sm["norm_mlp_g"][l:l + 1] for l in range(2)]

    ids = iter(range(2, 2 + 9))

    def gather(names):
        if dist:
            got = _by_sequencer("gather_" + names[0], _gather_comm([shards[k] for k in names]),
                                SIBLING_AND_NEIGHBOURS, next(ids))
            wg.update(zip(names, got))

    def send(name, names):
        if dist:
            comm = _exchange_comm(grads=[g[k] for k in names])
            g.update(zip(names, _by_sequencer("exchange_" + name, comm, EVERYONE, next(ids))))

    def send_sums(name, names, meanwhile):
        if not dist:
            meanwhile()
            return ()
        grads = [g[k] for k in names]
        landed = _by_sequencer("pair_exchange_" + name, _pair_exchange_comm(grads), (1,), next(ids))
        sums = _pair_add("pair_add_" + name, grads, landed, after=meanwhile())
        g.update(zip(names, _by_sequencer("exchange_" + name, _chip_exchange_comm(sums), OTHER_CHIPS, next(ids))))
        return sums

    def a_args():
        return (wg["a_w_in"], wg["a_ln_v_g"], wg["a_ln_v_b"], sm["a_w_s"], sm["a_b_st"], wg["a_w_out"])

    def kvq_w():
        return (sm["kv_src_norm_g"], wg["kv_w_a"], sm["kv_a_norm_g"], wg["kv_w_b"], mix_g[1], wg["b_w_q_a"],
                sm["b_q_norm_g"], wg["b_w_q_b"])

    gather(("mlp_w1_0", "mlp_w2_0"))
    h1, z, gated = _a_mix_fwd(x, mix_g[0], *a_args())
    gather(("kv_w_a", "kv_w_b", "b_w_q_a", "b_w_q_b", "b_w_o"))
    h2, a0 = _mlp_fwd(h1, mlp_g[0], wg["mlp_w1_0"], wg["mlp_w2_0"])
    if dist:
        wg["b_w_q_a"] = wg["b_w_q_a"].reshape(D_MODEL, Q_LORA)
        wg["kv_w_a"] = wg["kv_w_a"].reshape(D_MODEL, KV_LORA + QK_ROPE)
    gather(("mlp_w1_1", "mlp_w2_1"))
    ckv, k, v, cqpre, q, cos, sin = _kvq_fwd(h2, pos, inv_freq, kvq_w())
    h3, att = _attn_fwd(h2, q, k, v, wg["b_w_o"])
    a1, loss, dh4, d_final_g = _mlp_fwd_loss(h3, mlp_g[1], wg["mlp_w1_1"], wg["mlp_w2_1"], sm["final_norm_g"], target)

    g = {}
    dh3, d_mlp_g1, hn, f, da, dh3_b = _mlp_bwd(h3, a1, dh4, mlp_g[1], wg["mlp_w1_1"], wg["mlp_w2_1"], 1)
    dq, dk, dv = _attn_bwd(dh3_b, q, k, v, wg["b_w_o"], cos, sin)
    g["mlp_w1_1"] = _wgrad("wgrad_w1_1", hn, da, _full(t, D_MODEL), _cols(t, FF_SLOT), D_MODEL, FF_SLOT, after=[dq])
    g["mlp_w2_1"] = _wgrad("wgrad_w2_1", f, dh4, _cols(t, FF_SLOT), _full(t, D_MODEL), FF_SLOT, D_MODEL)

    def wgrad_w_o():
        g["b_w_o"] = _wgrad("wgrad_w_o", att, dh3_b, _head(t, V_HEAD), _full(t, D_MODEL), V_HEAD, D_MODEL)
        return [g["b_w_o"]]

    sums = send_sums("mlp_1", ("mlp_w1_1", "mlp_w2_1"), wgrad_w_o)
    dh2, d_mix_g1, d_src_g, d_q_g, d_kv_a_g, g_q_a, g["b_w_q_b"], g_kv_a, g["kv_w_b"] = _kvq_bwd(
        h2, dh3, ckv, cqpre, dq, dk, dv, cos, sin, kvq_w(), after=sums)
    g["b_w_q_a"] = g_q_a.reshape(N_DEV, D_MODEL // N_DEV, Q_LORA)
    g["kv_w_a"] = g_kv_a.reshape(N_DEV, D_MODEL // N_DEV, KV_LORA + QK_ROPE)
    qkv = ("b_w_q_a", "b_w_q_b", "kv_w_a", "kv_w_b")
    landed = [g[k] for k in qkv]
    send("qkv", qkv)
    dh1, d_mlp_g0, hn, f, da, dh1_b = _mlp_bwd(h1, a0, dh2, mlp_g[0], wg["mlp_w1_0"], wg["mlp_w2_0"], 0,
                                               after=landed if dist else ())
    landed = [g["mlp_w1_1"], g["mlp_w2_1"]] if dist else ()
    g["mlp_w1_0"] = _wgrad("wgrad_w1_0", hn, da, _full(t, D_MODEL), _cols(t, FF_SLOT), D_MODEL, FF_SLOT, after=landed)
    g["mlp_w2_0"] = _wgrad("wgrad_w2_0", f, dh2, _cols(t, FF_SLOT), _full(t, D_MODEL), FF_SLOT, D_MODEL)

    def wgrad_a_w_out():
        g["a_w_out"] = _wgrad("wgrad_a_w_out", gated, dh1_b, _cols(t, GATE_DIM // N_DEV), _full(t, D_MODEL),
                              GATE_DIM // N_DEV, D_MODEL)
        return [g["a_w_out"]] + [g[k] for k in qkv]

    sums = send_sums("mlp_0", ("mlp_w1_0", "mlp_w2_0", "b_w_o"), wgrad_a_w_out)
    dx, hn, dz, d_mix_g0, d_ln_g, d_ln_b, d_ws, d_bs = _a_mix_bwd(x, z, dh1, mix_g[0], *a_args(), after=sums)
    small = {
        "norm_mix_g": jnp.concatenate([d_mix_g0, d_mix_g1], axis=0),
        "norm_mlp_g": jnp.concatenate([d_mlp_g0, d_mlp_g1], axis=0),
        "a_ln_v_g": d_ln_g.reshape(N_DEV, GATE_DIM // N_DEV),
        "a_ln_v_b": d_ln_b.reshape(N_DEV, GATE_DIM // N_DEV),
        "a_w_s": d_ws.astype(BF16) if dist else d_ws,
        "a_b_s": d_bs.reshape(A_GROUPS, GMLP_BLOCK),
        "b_q_norm_g": d_q_g,
        "kv_src_norm_g": d_src_g,
        "kv_a_norm_g": d_kv_a_g,
        "final_norm_g": d_final_g,
    }
    if dist:
        parts = [small[k].reshape((1,) + small[k].shape) for k in SMALL] + [loss.reshape(1, 1, 1)]
        got = _by_sequencer("gather_small", _gather_comm(parts), SIBLING_AND_NEIGHBOURS, next(ids))
        small, loss = dict(zip(SMALL, got)), got[-1]
    g["a_w_in"] = _wgrad("wgrad_a_w_in", hn, dz, _full(t, D_MODEL), _cols(t, FF_SLOT), D_MODEL, FF_SLOT)
    return loss, dx, g, small


def _adamw(w, g, m, v):
    m = ADAM_B1 * m + (1.0 - ADAM_B1) * g
    v = ADAM_B2 * v + (1.0 - ADAM_B2) * (g * g)
    m_hat = m / (1.0 - ADAM_B1 ** ADAM_STEP)
    v_hat = v / (1.0 - ADAM_B2 ** ADAM_STEP)
    return -ADAM_LR * (m_hat / (jnp.sqrt(v_hat) + ADAM_EPS) + ADAM_WD * w), m, v


def _sum_in_device_order(r_ref):
    g = r_ref[0].astype(F32)
    for j in range(1, r_ref.shape[0]):
        g = g + r_ref[j].astype(F32)
    return g


def _adamw_sharded(name, recvs, w, m, v):
    layers, r, c = w.shape
    tr = math.gcd(r, 512)
    flat = [a for per_layer in recvs for a in per_layer]

    def body(*refs):
        r_refs, (w_ref, m_ref, v_ref) = refs[:len(flat)], refs[len(flat):len(flat) + 3]
        g_ref, d_ref, nm_ref, nv_ref = refs[-4:]
        layer = pl.program_id(0)
        g, pos = None, 0
        for li, per_layer in enumerate(recvs):
            total = None
            for ref in r_refs[pos:pos + len(per_layer)]:
                part = _sum_in_device_order(ref)
                total = part if total is None else total + part
            pos += len(per_layer)
            g = total if g is None else jnp.where(layer == li, total, g)
        g_ref[...] = g
        d_ref[...], nm_ref[...], nv_ref[...] = _adamw(w_ref[...], g, m_ref[...], v_ref[...])

    blk = pl.BlockSpec((None, tr, c), lambda l, i: (l, i, 0))
    return _call(name, body, (layers, r // tr),
                 [pl.BlockSpec((a.shape[0], tr, c), lambda l, i: (0, i, 0)) for a in flat] + [blk] * 3,
                 [blk] * 4, [_sds(w.shape, F32)] * 4, (*flat, w, m, v))


def _adamw_small(recvs, ws, ms, vs, own_row, losses):
    n = len(recvs)

    def body(*refs):
        r_refs, w_refs, m_refs, v_refs = (refs[i * n:(i + 1) * n] for i in range(4))
        outs, scr = refs[4 * n + 1:8 * n + 2], refs[8 * n + 2:]
        outs[-1][...] = _sum_in_device_order(refs[4 * n])
        me = _my_place()[3]
        for a in range(n):
            g = _sum_in_device_order(r_refs[a])
            if own_row[a]:
                scr[0][...] = g
                g = scr[0][pl.ds(me, 1), :]
            g_ref, d_ref, nm_ref, nv_ref = outs[4 * a:4 * a + 4]
            g_ref[...] = g
            d_ref[...], nm_ref[...], nv_ref[...] = _adamw(w_refs[a][...], g, m_refs[a][...], v_refs[a][...])

    out_shape = []
    for w in ws:
        out_shape += [_sds(w.shape, F32)] * 4
    return pl.pallas_call(
        body, name="adamw_small", in_specs=[VMEM] * (4 * n + 1), out_specs=[VMEM] * (4 * n + 1),
        out_shape=out_shape + [_sds((1, 1), F32)], scratch_shapes=[pltpu.VMEM((N_DEV, GATE_DIM // N_DEV), F32)],
    )(*recvs, *ws, *ms, *vs, losses)


BIG = ("a_w_in", "a_w_out", "b_w_q_a", "b_w_q_b", "b_w_o", "kv_w_a", "kv_w_b", "mlp_w1", "mlp_w2")
SMALL = ("norm_mix_g", "norm_mlp_g", "a_ln_v_g", "a_ln_v_b", "a_w_s", "a_b_s", "b_q_norm_g", "kv_src_norm_g",
         "kv_a_norm_g", "final_norm_g")
WEIGHTS = ("norm_mix_g", "norm_mlp_g", "a_w_in", "a_ln_v_g", "a_ln_v_b", "a_w_s", "a_b_s", "a_w_out", "b_w_q_a",
           "b_q_norm_g", "b_w_q_b", "b_w_o", "kv_src_norm_g", "kv_w_a", "kv_a_norm_g", "kv_w_b", "mlp_w1", "mlp_w2",
           "final_norm_g")


def _two_d(name, a):
    if name in ("a_w_s", "a_b_s"):
        return a.reshape(a.shape[1:])
    return a.reshape(1, -1) if a.ndim == 1 else a


def _three_d(a):
    return a if a.ndim == 3 else a.reshape((1,) + a.shape)


def kernel(x, positions, norm_mix_g, norm_mlp_g, a_w_in, a_ln_v_g, a_ln_v_b, a_w_s, a_b_s, a_w_out, b_w_q_a, b_q_norm_g, b_w_q_b, b_w_o, kv_src_norm_g, kv_w_a, kv_a_norm_g, kv_w_b, mlp_w1, mlp_w2, final_norm_g, loss_target, m_norm_mix_g, m_norm_mlp_g, m_a_w_in, m_a_ln_v_g, m_a_ln_v_b, m_a_w_s, m_a_b_s, m_a_w_out, m_b_w_q_a, m_b_q_norm_g, m_b_w_q_b, m_b_w_o, m_kv_src_norm_g, m_kv_w_a, m_kv_a_norm_g, m_kv_w_b, m_mlp_w1, m_mlp_w2, m_final_norm_g, v_norm_mix_g, v_norm_mlp_g, v_a_w_in, v_a_ln_v_g, v_a_ln_v_b, v_a_w_s, v_a_b_s, v_a_w_out, v_b_w_q_a, v_b_q_norm_g, v_b_w_q_b, v_b_w_o, v_kv_src_norm_g, v_kv_w_a, v_kv_a_norm_g, v_kv_w_b, v_mlp_w1, v_mlp_w2, v_final_norm_g):
    w = dict(norm_mix_g=norm_mix_g, norm_mlp_g=norm_mlp_g, a_w_in=a_w_in, a_ln_v_g=a_ln_v_g, a_ln_v_b=a_ln_v_b,
             a_w_s=a_w_s, a_b_s=a_b_s, a_w_out=a_w_out, b_w_q_a=b_w_q_a, b_q_norm_g=b_q_norm_g, b_w_q_b=b_w_q_b,
             b_w_o=b_w_o, kv_src_norm_g=kv_src_norm_g, kv_w_a=kv_w_a, kv_a_norm_g=kv_a_norm_g, kv_w_b=kv_w_b,
             mlp_w1=mlp_w1, mlp_w2=mlp_w2, final_norm_g=final_norm_g)
    m = dict(norm_mix_g=m_norm_mix_g, norm_mlp_g=m_norm_mlp_g, a_w_in=m_a_w_in, a_ln_v_g=m_a_ln_v_g,
             a_ln_v_b=m_a_ln_v_b, a_w_s=m_a_w_s, a_b_s=m_a_b_s, a_w_out=m_a_w_out, b_w_q_a=m_b_w_q_a,
             b_q_norm_g=m_b_q_norm_g, b_w_q_b=m_b_w_q_b, b_w_o=m_b_w_o, kv_src_norm_g=m_kv_src_norm_g,
             kv_w_a=m_kv_w_a, kv_a_norm_g=m_kv_a_norm_g, kv_w_b=m_kv_w_b, mlp_w1=m_mlp_w1, mlp_w2=m_mlp_w2,
             final_norm_g=m_final_norm_g)
    v = dict(norm_mix_g=v_norm_mix_g, norm_mlp_g=v_norm_mlp_g, a_w_in=v_a_w_in, a_ln_v_g=v_a_ln_v_g,
             a_ln_v_b=v_a_ln_v_b, a_w_s=v_a_w_s, a_b_s=v_a_b_s, a_w_out=v_a_w_out, b_w_q_a=v_b_w_q_a,
             b_q_norm_g=v_b_q_norm_g, b_w_q_b=v_b_w_q_b, b_w_o=v_b_w_o, kv_src_norm_g=v_kv_src_norm_g,
             kv_w_a=v_kv_w_a, kv_a_norm_g=v_kv_a_norm_g, kv_w_b=v_kv_w_b, mlp_w1=v_mlp_w1, mlp_w2=v_mlp_w2,
             final_norm_g=v_final_norm_g)
    t = x.shape[1]

    first = ("a_w_in", "a_w_out", "a_ln_v_g", "a_ln_v_b")
    later = ("mlp_w1", "mlp_w2", "kv_w_a", "kv_w_b", "b_w_q_a", "b_w_q_b", "b_w_o")
    later_blocks = ("mlp_w1_0", "mlp_w1_1", "mlp_w2_0", "mlp_w2_1") + later[2:]
    got, casts = _gather_first([_three_d(w[k]) if k in BIG else w[k] for k in first], [_three_d(w[k]) for k in later])
    wg = dict(zip(first, got))
    wg["a_w_out"] = wg["a_w_out"].reshape(GATE_DIM, D_MODEL)
    wg["a_ln_v_g"] = wg["a_ln_v_g"].reshape(1, GATE_DIM)
    wg["a_ln_v_b"] = wg["a_ln_v_b"].reshape(1, GATE_DIM)
    shards = dict(zip(later_blocks, casts))

    sm = {k: _two_d(k, w[k]) for k in SMALL if k not in ("a_ln_v_g", "a_ln_v_b")}
    sm["a_b_st"] = sm["a_b_s"].T
    inv_freq = (ROPE_THETA ** (-jnp.arange(0, QK_ROPE, 2, dtype=F32) / QK_ROPE)).reshape(1, QK_ROPE // 2)

    losses, dx, g, small = _local_step(x[0], positions.reshape(t, 1), loss_target[0], inv_freq, wg, sm, shards)

    names = ("a_w_in", "a_w_out")
    sums = _pair_reduce("pair_reduce_a", [g[k] for k in names], after=[g["mlp_w1_0"], g["mlp_w2_0"]])
    g.update(zip(names, _by_sequencer("exchange_last", _chip_exchange_comm(sums), OTHER_CHIPS, collective_id=1)))

    out = {}
    for k in BIG:
        recvs = [[g[k + "_0"]], [g[k + "_1"]]] if k.startswith("mlp") else [[g[k]]]
        res = _adamw_sharded("adamw_" + k, recvs, _three_d(w[k]), _three_d(m[k]), _three_d(v[k]))
        out[k] = [o.reshape(w[k].shape) for o in res]
    own_row = [k in ("a_ln_v_g", "a_ln_v_b") for k in SMALL]
    res = _adamw_small([small[k] for k in SMALL], [_two_d(k, w[k]) for k in SMALL], [_two_d(k, m[k]) for k in SMALL],
                       [_two_d(k, v[k]) for k in SMALL], own_row, losses)
    for i, k in enumerate(SMALL):
        out[k] = [o.reshape(w[k].shape) for o in res[4 * i:4 * i + 4]]

    return (res[-1].reshape(()), dx.reshape(x.shape), *[out[k][0] for k in WEIGHTS], *[out[k][1] for k in WEIGHTS],
            *[out[k][2] for k in WEIGHTS], *[out[k][3] for k in WEIGHTS])
```

```python
import math

import jax
import jax.numpy as jnp
from jax import lax
from jax.experimental import pallas as pl
from jax.experimental.pallas import tpu as pltpu
from jax.experimental.pallas import tpu_sc as plsc

F32, BF16 = jnp.float32, jnp.bfloat16
MESH = pl.DeviceIdType.MESH
ANY = pl.BlockSpec(memory_space=pl.ANY)
VMEM = pl.BlockSpec(memory_space=pltpu.VMEM)

N_DEV = 8
D_MODEL = 1024
CHUNK = 64
GMLP_BLOCK = 128
GATE_DIM = 2048
A_GROUPS = 8
A_GROUP_DIM = GATE_DIM // A_GROUPS
B_HEADS = 8
QK_NOPE, QK_ROPE, V_HEAD = 128, 64, 128
Q_LORA, KV_LORA = 384, 256
ROPE_THETA = 10000.0
D_FF = 4096
FF_SLOT = D_FF // N_DEV
EPS = 1e-6
ATT_SCALE = (QK_NOPE + QK_ROPE) ** -0.5

ADAM_LR, ADAM_B1, ADAM_B2, ADAM_EPS, ADAM_WD, ADAM_STEP = 0.001, 0.9, 0.999, 1e-08, 0.01, 10

TM = 256
TM_GATE = 256
TM_MLP_FWD = 512
TM_KVQ = 512
VMEM_LIMIT = 56 * 1024 * 1024
INV_SQRT2 = 1.0 / math.sqrt(2.0)
INV_SQRT_2PI = 1.0 / math.sqrt(2.0 * math.pi)
LOG2_E = 1.0 / math.log(2.0)
HEADS_PER_STEP = 2


def _dot(a, b):
    return jnp.dot(a, b, preferred_element_type=F32)


def _dot_nt(a, b):
    return lax.dot_general(a, b, (((1,), (1,)), ((), ())), preferred_element_type=F32)


def _dot_tn(a, b):
    return lax.dot_general(a, b, (((0,), (0,)), ((), ())), preferred_element_type=F32)


def _rms_fwd(x, g):
    rstd = lax.rsqrt(jnp.mean(x * x, axis=-1, keepdims=True) + EPS)
    xhat = x * rstd
    return xhat * g, xhat, rstd


def _rms_bwd(dy, xhat, rstd, g):
    dxhat = dy * g
    dx = rstd * (dxhat - xhat * jnp.mean(dxhat * xhat, axis=-1, keepdims=True))
    return dx, jnp.sum(dy * xhat, axis=0, keepdims=True)


def _ln_fwd(v, g, b):
    mu = jnp.mean(v, axis=-1, keepdims=True)
    vc = v - mu
    rstd = lax.rsqrt(jnp.mean(vc * vc, axis=-1, keepdims=True) + EPS)
    vhat = vc * rstd
    return vhat * g + b, vhat, rstd


def _gelu(x):
    return 0.5 * x * (1.0 + lax.erf(x * INV_SQRT2))


def _gelu_and_grad(x):
    cdf = 0.5 * (1.0 + lax.erf(x * INV_SQRT2))
    return x * cdf, cdf + x * jnp.exp(-0.5 * x * x) * INV_SQRT_2PI


def _rope(x, cos, sin):
    x1, x2 = x[:, :QK_ROPE // 2], x[:, QK_ROPE // 2:]
    return jnp.concatenate([x1 * cos - x2 * sin, x2 * cos + x1 * sin], axis=-1)


def _gate_mask():
    row = lax.broadcasted_iota(jnp.int32, (GMLP_BLOCK, GMLP_BLOCK), 0)
    col = lax.broadcasted_iota(jnp.int32, (GMLP_BLOCK, GMLP_BLOCK), 1)
    return (col < CHUNK) | (row >= CHUNK)


def _att_mask(q0, tq, t):
    q = q0 + lax.broadcasted_iota(jnp.int32, (tq, t), 0)
    k = lax.broadcasted_iota(jnp.int32, (tq, t), 1)
    return jnp.right_shift(k, 6) <= jnp.right_shift(q, 6)


def _res(shape, imap=None):
    zeros = (0,) * len(shape)
    return pl.BlockSpec(shape, imap or (lambda i: zeros), pipeline_mode=pl.Buffered(1))


def _const(shape):
    zeros = (0,) * len(shape)
    return pl.BlockSpec(shape, lambda i: zeros)


def _row(d, tm=TM):
    return pl.BlockSpec((tm, d), lambda i: (i, 0))


def _heads(d, tm=TM):
    return pl.BlockSpec((B_HEADS, tm, d), lambda i: (0, i, 0))


def _sds(shape, dt):
    return jax.ShapeDtypeStruct(shape, dt)


def _acc(ref, val):
    @pl.when(pl.program_id(0) == 0)
    def _():
        ref[...] = jnp.zeros_like(ref)
    ref[...] += val


def _my_place():
    x, y, c = lax.axis_index("x"), lax.axis_index("y"), lax.axis_index("c")
    return x, y, c, 4 * x + 2 * y + c


def _peer(x, y, c, k):
    px = 1 - x if k & 4 else x
    py = 1 - y if k & 2 else y
    pc = 1 - c if k & 1 else c
    return (px, py, pc), 4 * px + 2 * py + pc


CHIPS = (2, 4, 6)


def _splits(ref):
    return len(ref.shape) >= 3 and ref.shape[1] % 32 == 0


def _piece(ref, block, half=None):
    if half is None or not _splits(ref):
        return ref.at[pl.ds(block, 1)]
    rows = ref.shape[1] // 2
    return ref.at[pl.ds(block, 1), pl.ds(half * rows, rows)]


def _gather_copy(sems, a, k, piece, to, src=None):
    return pltpu.make_async_remote_copy(
        src_ref=piece if src is None else src, dst_ref=piece, send_sem=sems[0].at[a, k], recv_sem=sems[1].at[a, k],
        device_id=to, device_id_type=MESH)


def _gather_start(srcs, outs, sems, only=None):
    x, y, c, me = _my_place()
    for a in range(len(srcs)) if only is None else (only,):
        mine = _piece(outs[a], me)
        pltpu.make_async_copy(srcs[a], mine, sems[2].at[a]).start()
        for k, rel in enumerate((1, 4, 2)):
            _gather_copy(sems, a, k, mine, _peer(x, y, c, rel)[0], src=srcs[a]).start()


def _gather_relay(srcs, outs, sems):
    x, y, c, _ = _my_place()
    sib = _peer(x, y, c, 1)[0]
    (xn, xn_i), (yn, yn_i) = _peer(x, y, c, 4), _peer(x, y, c, 2)
    for a in range(len(srcs)):
        out = outs[a]
        _gather_copy(sems, a, 1, _piece(out, xn_i), xn).wait_recv()
        _gather_copy(sems, a, 3, _piece(out, xn_i, 0), yn).start()
        _gather_copy(sems, a, 5, _piece(out, xn_i), sib).start()
        _gather_copy(sems, a, 2, _piece(out, yn_i), yn).wait_recv()
        if _splits(out):
            _gather_copy(sems, a, 4, _piece(out, yn_i, 1), xn).start()
        _gather_copy(sems, a, 6, _piece(out, yn_i), sib).start()


def _gather_finish(srcs, outs, sems):
    x, y, c, me = _my_place()
    sib = _peer(x, y, c, 1)[0]
    xn, yn, dg_i = _peer(x, y, c, 4)[0], _peer(x, y, c, 2)[0], _peer(x, y, c, 6)[1]
    n = len(srcs)
    for a in range(n):
        out = outs[a]
        _gather_copy(sems, a, 3, _piece(out, dg_i, 0), yn).wait_recv()
        _gather_copy(sems, a, 7, _piece(out, dg_i, 0), sib).start()
        if _splits(out):
            _gather_copy(sems, a, 4, _piece(out, dg_i, 1), xn).wait_recv()
            _gather_copy(sems, a, 8, _piece(out, dg_i, 1), sib).start()
    for a in range(n):
        out = outs[a]
        whole, half = _piece(out, me), _piece(out, me, 0)
        for k in (0, 5, 6):
            _gather_copy(sems, a, k, whole, sib).wait_recv()
        for k in (7, 8) if _splits(out) else (7,):
            _gather_copy(sems, a, k, half, sib).wait_recv()
        for k in (0, 1, 2):
            _gather_copy(sems, a, k, whole, sib, src=srcs[a]).wait_send()
        for k in (5, 6):
            _gather_copy(sems, a, k, whole, sib).wait_send()
        for k in (3, 4, 7, 8) if _splits(out) else (3, 7):
            _gather_copy(sems, a, k, half, sib).wait_send()
        pltpu.make_async_copy(srcs[a], whole, sems[2].at[a]).wait()


def _relay_sems(n):
    return [pltpu.SemaphoreType.DMA((n, 9)), pltpu.SemaphoreType.DMA((n, 9)), pltpu.SemaphoreType.DMA((n,))]


def _gather_sems(n):
    return [pltpu.SemaphoreType.DMA((n, 7)), pltpu.SemaphoreType.DMA((n, 7)), pltpu.SemaphoreType.DMA((n,))]


class _Comm:
    def __init__(self, args, out_shape, scratch, start, finish, relay=None):
        self.args, self.out_shape, self.scratch, self.start, self.finish = args, out_shape, scratch, start, finish
        self.relay = relay


def _gather_comm(shards):
    return _Comm(list(shards), [_sds((N_DEV,) + s.shape[1:], s.dtype) for s in shards], _relay_sems(len(shards)),
                 _gather_start, _gather_finish, relay=_gather_relay)


def _direct_copies(ins, outs, sems, wait):
    send_sems, recv_sems, local_sems = sems
    x, y, c, me = _my_place()
    for a in range(len(ins)):
        local = pltpu.make_async_copy(ins[a].at[pl.ds(me, 1)], outs[a].at[pl.ds(me, 1)], local_sems.at[a])
        local.wait() if wait else local.start()
        for k in range(1, N_DEV):
            to, to_i = _peer(x, y, c, k)
            cp = pltpu.make_async_remote_copy(
                src_ref=ins[a].at[pl.ds(to_i, 1)], dst_ref=outs[a].at[pl.ds(me, 1)],
                send_sem=send_sems.at[a, k - 1], recv_sem=recv_sems.at[a, k - 1], device_id=to, device_id_type=MESH)
            cp.wait() if wait else cp.start()


def _exchange_comm(grads):
    return _Comm(list(grads), [_sds(g.shape, g.dtype) for g in grads], _gather_sems(len(grads)),
                 lambda i, o, s: _direct_copies(i, o, s, False), lambda i, o, s: _direct_copies(i, o, s, True))


def _chip_copies(ins, outs, sems, wait):
    send_sems, recv_sems, local_sems = sems
    x, y, c, _ = _my_place()
    for a in range(len(ins)):
        local = pltpu.make_async_copy(ins[a].at[pl.ds(2 * x + y, 1)], outs[a].at[pl.ds(len(CHIPS), 1)],
                                      local_sems.at[a])
        local.wait() if wait else local.start()
        for i, k in enumerate(CHIPS):
            to = _peer(x, y, c, k)[0]
            cp = pltpu.make_async_remote_copy(
                src_ref=ins[a].at[pl.ds(2 * to[0] + to[1], 1)], dst_ref=outs[a].at[pl.ds(i, 1)],
                send_sem=send_sems.at[a, i], recv_sem=recv_sems.at[a, i], device_id=to, device_id_type=MESH)
            cp.wait() if wait else cp.start()


def _chip_exchange_comm(sums):
    n = len(sums)
    sems = [pltpu.SemaphoreType.DMA((n, len(CHIPS))), pltpu.SemaphoreType.DMA((n, len(CHIPS))),
            pltpu.SemaphoreType.DMA((n,))]
    return _Comm(list(sums), [_sds(s.shape, s.dtype) for s in sums], sems,
                 lambda i, o, s: _chip_copies(i, o, s, False), lambda i, o, s: _chip_copies(i, o, s, True))


def _pair_reduce(name, grads, after=()):
    n = len(grads)
    n_chips = N_DEV // 2

    def body(*refs):
        g_refs, gh_refs, refs = refs[:n], refs[n:2 * n], refs[2 * n + len(after):]
        p_refs, land = refs[:n], refs[n:2 * n]
        send_sems, recv_sems = refs[2 * n:]
        x, y, c, _ = _my_place()
        sib = _peer(x, y, c, 1)[0]
        q = pl.program_id(0)

        def to_sibling(a, j):
            return pltpu.make_async_remote_copy(
                src_ref=gh_refs[a].at[j, pl.ds(1 - c, 1)], dst_ref=land[a].at[pl.ds(j, 1)],
                send_sem=send_sems.at[a, j], recv_sem=recv_sems.at[a, j], device_id=sib, device_id_type=MESH)

        @pl.when(q == 0)
        def _():
            for j in range(n_chips):
                for a in range(n):
                    to_sibling(a, j).start()

        for a in range(n):
            to_sibling(a, q).wait_recv()
            p_refs[a][...] = (g_refs[a][0, pl.ds(c, 1)].astype(F32) + land[a][pl.ds(q, 1)].astype(F32)).astype(BF16)

        @pl.when(q == n_chips - 1)
        def _():
            for a in range(n):
                for j in range(n_chips):
                    to_sibling(a, j).wait_send()

    views = [g.reshape((n_chips, 2) + g.shape[1:]) for g in grads]
    res = pl.pallas_call(
        body, name=name, grid=(n_chips,),
        in_specs=[pl.BlockSpec((1, 2) + g.shape[1:], lambda q: (q, 0, 0, 0)) for g in grads]
        + [ANY] * (n + len(after)),
        out_specs=[pl.BlockSpec((1,) + g.shape[1:], lambda q: (q, 0, 0)) for g in grads],
        out_shape=[_sds((n_chips,) + g.shape[1:], BF16) for g in grads],
        scratch_shapes=[pltpu.VMEM((n_chips,) + g.shape[1:], BF16) for g in grads]
        + [pltpu.SemaphoreType.DMA((n, n_chips)), pltpu.SemaphoreType.DMA((n, n_chips))],
        compiler_params=pltpu.CompilerParams(dimension_semantics=("arbitrary",), vmem_limit_bytes=VMEM_LIMIT),
    )(*views, *views, *after)
    return list(res)


def _pair_exchange_comm(grads):
    n, n_chips = len(grads), N_DEV // 2

    def copies(ins, outs, sems, wait):
        x, y, c, _ = _my_place()
        for j in range(n_chips):
            for a in range(n):
                cp = pltpu.make_async_remote_copy(
                    src_ref=ins[a].at[j, pl.ds(1 - c, 1)], dst_ref=outs[a].at[pl.ds(j, 1)], send_sem=sems[0].at[a, j],
                    recv_sem=sems[1].at[a, j], device_id=_peer(x, y, c, 1)[0], device_id_type=MESH)
                cp.wait() if wait else cp.start()

    views = [g.reshape((n_chips, 2) + g.shape[1:]) for g in grads]
    sems = [pltpu.SemaphoreType.DMA((n, n_chips)), pltpu.SemaphoreType.DMA((n, n_chips))]
    return _Comm(views, [_sds((n_chips,) + g.shape[1:], g.dtype) for g in grads], sems,
                 lambda i, o, s: copies(i, o, s, False), lambda i, o, s: copies(i, o, s, True))


def _pair_add(name, grads, landed, after=()):
    n, n_chips = len(grads), N_DEV // 2

    def body(core_ref, *refs):
        g_refs, l_refs, p_refs = refs[:n], refs[n:2 * n], refs[2 * n + len(after):]
        for a in range(n):
            p_refs[a][...] = (g_refs[a][...].astype(F32) + l_refs[a][...].astype(F32)).astype(BF16)

    views = [g.reshape((n_chips, 2) + g.shape[1:]) for g in grads]
    blocks = [pl.BlockSpec((1,) + g.shape[1:], lambda q, core: (q, 0, 0)) for g in grads]
    mine = [pl.BlockSpec((1, None) + g.shape[1:], lambda q, core: (q, core[0], 0, 0)) for g in grads]
    return list(pl.pallas_call(
        body, name=name, out_shape=[_sds((n_chips,) + g.shape[1:], BF16) for g in grads],
        grid_spec=pltpu.PrefetchScalarGridSpec(num_scalar_prefetch=1, grid=(n_chips,),
                                               in_specs=mine + blocks + [ANY] * len(after), out_specs=blocks),
        compiler_params=pltpu.CompilerParams(dimension_semantics=("arbitrary",), vmem_limit_bytes=VMEM_LIMIT),
    )(lax.axis_index("c").reshape(1), *views, *landed, *after))


def _call(name, body, grid, in_specs, out_specs, out_shape, args, scratch=(), after=()):
    ni, na = len(in_specs), len(after)

    def ordered(*refs):
        body(*refs[:ni], *refs[ni + na:])

    return list(pl.pallas_call(
        ordered if after else body, name=name, grid=grid, in_specs=list(in_specs) + [ANY] * na,
        out_specs=list(out_specs), out_shape=list(out_shape), scratch_shapes=list(scratch),
        compiler_params=pltpu.CompilerParams(dimension_semantics=("arbitrary",) * len(grid),
                                             vmem_limit_bytes=VMEM_LIMIT))(*args, *after))


SIBLING_AND_NEIGHBOURS, OTHER_CHIPS, EVERYONE = (1, 4, 2), CHIPS, tuple(range(1, N_DEV))


def _by_sequencer(name, comm, peers, collective_id):
    src = [jax.new_ref(a, memory_space=pltpu.MemorySpace.HBM) for a in comm.args]
    dst = [jax.empty_ref(s, memory_space=pltpu.MemorySpace.HBM) for s in comm.out_shape]

    @pl.kernel(mesh=plsc.ScalarSubcoreMesh(axis_name="sequencer", num_cores=1), name=name,
               scratch_types=tuple(comm.scratch), compiler_params=pltpu.CompilerParams(collective_id=collective_id))
    def launch(*sems):
        x, y, c, _ = _my_place()
        barrier = pltpu.get_barrier_semaphore()
        for k in peers:
            pl.semaphore_signal(barrier, inc=1, device_id=_peer(x, y, c, k)[0], device_id_type=MESH)
        pl.semaphore_wait(barrier, len(peers))
        comm.start(src, dst, sems)
        if comm.relay is not None:
            comm.relay(src, dst, sems)
        comm.finish(src, dst, sems)

    launch()
    return [d[...] for d in dst]


def _gather_first(first, later):
    nf = len(first)
    layer_of = [(a, l) for a, s in enumerate(later) for l in range(s.shape[0])]
    nl = len(layer_of)
    dts = [BF16] * (nf - 2) + [F32, F32]

    def body(*refs):
        ins, refs = refs[:nf + len(later)], refs[nf + len(later):]
        outs, refs = refs[:nf], refs[nf:]
        casts, refs = refs[:nl], refs[nl:]
        stage, sems = refs[:nf], refs[nf:]
        for a in range(nf):
            stage[a][...] = ins[a][...].astype(dts[a])
            _gather_start(stage, outs, sems, only=a)
        for k, (a, l) in enumerate(layer_of):
            casts[k][...] = ins[nf + a][l:l + 1].astype(BF16)
        _gather_relay(stage, outs, sems)
        _gather_finish(stage, outs, sems)

    res = pl.pallas_call(
        body, name="gather_first",
        in_specs=[VMEM] * (nf + len(later)), out_specs=[ANY] * nf + [VMEM] * nl,
        out_shape=[_sds((N_DEV,) + s.shape[1:], dt) for s, dt in zip(first, dts)]
        + [_sds((1,) + later[a].shape[1:], BF16) for a, _ in layer_of],
        scratch_shapes=[pltpu.VMEM(s.shape, dt) for s, dt in zip(first, dts)] + _relay_sems(nf),
        compiler_params=pltpu.CompilerParams(vmem_limit_bytes=VMEM_LIMIT),
    )(*first, *later)
    return list(res[:nf]), list(res[nf:])


def _a_mix_fwd(x, g, w_in, ln_g, ln_b, w_s, b_st, w_out):
    t = x.shape[0]
    nblk = TM // GMLP_BLOCK

    def body(x_ref, g_ref, win_ref, lng_ref, lnb_ref, ws_ref, bst_ref, wout_ref, h_ref, z_ref, gated_scr):
        xv = x_ref[...]
        hb = _rms_fwd(xv, g_ref[...])[0].astype(BF16)
        for d in range(N_DEV):
            z_ref[:, d * FF_SLOT:(d + 1) * FF_SLOT] = _dot(hb, win_ref[d])
        u = _gelu(z_ref[:, :GATE_DIM])
        vb = _ln_fwd(_gelu(z_ref[:, GATE_DIM:]), lng_ref[...], lnb_ref[...])[0].astype(BF16)
        mask = _gate_mask()
        for gi in range(A_GROUPS):
            wm = jnp.where(mask, ws_ref[gi], 0.0).astype(BF16)
            bias = bst_ref[:, gi:gi + 1]
            cs = slice(gi * A_GROUP_DIM, (gi + 1) * A_GROUP_DIM)
            for n in range(nblk):
                rs = slice(n * GMLP_BLOCK, (n + 1) * GMLP_BLOCK)
                sv = _dot(wm, vb[rs, cs]) + bias
                gated_scr[rs, cs] = (u[rs, cs] * sv).astype(BF16)
        h_ref[...] = xv + _dot(gated_scr[...], wout_ref[...])

    return _call(
        "a_mix_fwd", body, (t // TM,),
        [_row(D_MODEL), _res((1, D_MODEL)), _res((N_DEV, D_MODEL, FF_SLOT)), _res((1, GATE_DIM)),
         _res((1, GATE_DIM)), _res((A_GROUPS, GMLP_BLOCK, GMLP_BLOCK)), _res((GMLP_BLOCK, A_GROUPS)),
         _res((GATE_DIM, D_MODEL))],
        [_row(D_MODEL), _row(2 * GATE_DIM), _row(GATE_DIM)],
        [_sds((t, D_MODEL), F32), _sds((t, 2 * GATE_DIM), F32), _sds((t, GATE_DIM), BF16)],
        (x, g, w_in, ln_g, ln_b, w_s, b_st, w_out))


MLP_W_SPECS = (_res((N_DEV, D_MODEL, FF_SLOT)), _res((N_DEV, FF_SLOT, D_MODEL)))


def _mlp_fwd(h, g, w1, w2):
    t = h.shape[0]

    def body(h_ref, g_ref, w1_ref, w2_ref, o_ref, a_ref):
        hv = h_ref[...]
        hb = _rms_fwd(hv, g_ref[...])[0].astype(BF16)
        o_ref[...] = hv
        for d in range(N_DEV):
            a = _dot(hb, w1_ref[d])
            a_ref[:, d * FF_SLOT:(d + 1) * FF_SLOT] = a
            r = jnp.maximum(a, 0.0)
            o_ref[...] += _dot((r * r).astype(BF16), w2_ref[d])

    return _call(
        "mlp_fwd", body, (t // TM_MLP_FWD,), [_row(D_MODEL, TM_MLP_FWD), _res((1, D_MODEL)), *MLP_W_SPECS],
        [_row(D_MODEL, TM_MLP_FWD), _row(D_FF, TM_MLP_FWD)], [_sds((t, D_MODEL), F32), _sds((t, D_FF), F32)],
        (h, g, w1, w2))


def _mlp_fwd_loss(h, g, w1, w2, final_g, target):
    t = h.shape[0]

    def body(h_ref, g_ref, w1_ref, w2_ref, fg_ref, t_ref, a_ref, loss_ref, dh_ref, dg_ref):
        hv = h_ref[...]
        hb = _rms_fwd(hv, g_ref[...])[0].astype(BF16)
        out = hv
        for d in range(N_DEV):
            a = _dot(hb, w1_ref[d])
            a_ref[:, d * FF_SLOT:(d + 1) * FF_SLOT] = a
            r = jnp.maximum(a, 0.0)
            out = out + _dot((r * r).astype(BF16), w2_ref[d])
        y, xhat, rstd = _rms_fwd(out, fg_ref[...])
        err = y - t_ref[...]
        part = 0.5 * jnp.sum(jnp.mean(err * err, axis=-1, keepdims=True), axis=0, keepdims=True)
        dx, dg = _rms_bwd(err * (1.0 / D_MODEL), xhat, rstd, fg_ref[...])
        dh_ref[...] = dx
        _acc(dg_ref, dg)
        _acc(loss_ref, part)

    return _call(
        "mlp_fwd_loss", body, (t // TM,),
        [_row(D_MODEL), _res((1, D_MODEL)), *MLP_W_SPECS, _res((1, D_MODEL)), _row(D_MODEL)],
        [_row(D_FF), _const((1, 1)), _row(D_MODEL), _const((1, D_MODEL))],
        [_sds((t, D_FF), F32), _sds((1, 1), F32), _sds((t, D_MODEL), F32), _sds((1, D_MODEL), F32)],
        (h, g, w1, w2, final_g, target))


KVQ_W_SPECS = (_res((1, D_MODEL)), _res((D_MODEL, KV_LORA + QK_ROPE)), _res((1, KV_LORA)),
               _res((B_HEADS, KV_LORA, QK_NOPE + V_HEAD)), _res((1, D_MODEL)), _res((D_MODEL, Q_LORA)),
               _res((1, Q_LORA)), _res((B_HEADS, Q_LORA, QK_NOPE + QK_ROPE)))


def _kvq_fwd(h, pos, inv_freq, kvq_w):
    t = h.shape[0]
    half = QK_ROPE // 2

    def body(h_ref, pos_ref, invf_ref, srcg_ref, wkva_ref, kvag_ref, wkvb_ref, mixg_ref, wqa_ref, qg_ref, wqb_ref,
             ckv_ref, k_ref, v_ref, cqpre_ref, q_ref, cos_ref, sin_ref):
        hv = h_ref[...]
        xhat = hv * lax.rsqrt(jnp.mean(hv * hv, axis=-1, keepdims=True) + EPS)
        ang = pos_ref[...].astype(F32) * invf_ref[...]
        cos, sin = jnp.cos(ang), jnp.sin(ang)
        cos_ref[...] = cos
        sin_ref[...] = sin
        ckv = _dot((xhat * srcg_ref[...]).astype(BF16), wkva_ref[...])
        ckv_ref[...] = ckv
        cb = _rms_fwd(ckv[:, :KV_LORA], kvag_ref[...])[0].astype(BF16)
        kpe = _rope(ckv[:, KV_LORA:], cos, sin).astype(BF16)
        for hd in range(B_HEADS):
            kv = _dot(cb, wkvb_ref[hd])
            k_ref[hd, :, 0:QK_NOPE] = kv[:, :QK_NOPE].astype(BF16)
            k_ref[hd, :, QK_NOPE:] = kpe
            v_ref[hd] = kv[:, QK_NOPE:].astype(BF16)
        cqpre = _dot((xhat * mixg_ref[...]).astype(BF16), wqa_ref[...])
        cqpre_ref[...] = cqpre
        cqb = _rms_fwd(cqpre, qg_ref[...])[0].astype(BF16)
        for hd in range(B_HEADS):
            q = _dot(cqb, wqb_ref[hd])
            q_ref[hd, :, 0:QK_NOPE] = q[:, :QK_NOPE].astype(BF16)
            q_ref[hd, :, QK_NOPE:] = _rope(q[:, QK_NOPE:], cos, sin).astype(BF16)

    tm = TM_KVQ
    return _call(
        "kvq_fwd", body, (t // tm,), [_row(D_MODEL, tm), _row(1, tm), _res((1, half)), *KVQ_W_SPECS],
        [_row(KV_LORA + QK_ROPE, tm), _heads(QK_NOPE + QK_ROPE, tm), _heads(V_HEAD, tm), _row(Q_LORA, tm),
         _heads(QK_NOPE + QK_ROPE, tm), _row(half, tm), _row(half, tm)],
        [_sds((t, KV_LORA + QK_ROPE), F32), _sds((B_HEADS, t, QK_NOPE + QK_ROPE), BF16),
         _sds((B_HEADS, t, V_HEAD), BF16), _sds((t, Q_LORA), F32), _sds((B_HEADS, t, QK_NOPE + QK_ROPE), BF16),
         _sds((t, half), F32), _sds((t, half), F32)],
        (h, pos, inv_freq, *kvq_w))


def _softmax_rows(q, k_ref, k):
    past, upto = k * TM, (k + 1) * TM
    s = _dot_nt(q, k_ref[0:upto, :])
    own = jnp.where(_att_mask(0, TM, TM), s[:, past:], jnp.finfo(F32).min)
    s = own if k == 0 else jnp.concatenate([s[:, :past], own], axis=1)
    e = jnp.exp2((s - jnp.max(s, axis=-1, keepdims=True)) * (ATT_SCALE * LOG2_E))
    return e * (1.0 / jnp.sum(e, axis=-1, keepdims=True))


def _for_my_tile(i, nq, fn):
    for k in range(nq):
        @pl.when(i == k)
        def _(k=k):
            fn(k)


def _attn_fwd(h, q, k, v, w_o):
    t = h.shape[0]
    nq, hps = t // TM, HEADS_PER_STEP

    def body(h_ref, q_ref, k_ref, v_ref, wo_ref, o_ref, att_ref):
        i, pair = pl.program_id(0), pl.program_id(1)

        @pl.when(pair == 0)
        def _():
            o_ref[...] = h_ref[...]

        def tile(kt):
            proj = None
            for j in range(hps):
                hd = pair * hps + j
                p = _softmax_rows(q_ref[j], k_ref.at[hd], kt)
                ob = _dot(p.astype(BF16), v_ref[hd, 0:(kt + 1) * TM, :]).astype(BF16)
                att_ref[j] = ob
                proj = _dot(ob, wo_ref[hd]) if proj is None else proj + _dot(ob, wo_ref[hd])
            o_ref[...] += proj

        _for_my_tile(i, nq, tile)

    def per_head(d):
        return pl.BlockSpec((hps, TM, d), lambda i, pair: (pair, i, 0))

    def resident(shape):
        zeros = (0,) * len(shape)
        return pl.BlockSpec(shape, lambda i, pair: zeros, pipeline_mode=pl.Buffered(1))

    tile_spec = pl.BlockSpec((TM, D_MODEL), lambda i, pair: (i, 0))
    return _call(
        "attn_fwd", body, (nq, B_HEADS // hps),
        [tile_spec, per_head(QK_NOPE + QK_ROPE), resident((B_HEADS, t, QK_NOPE + QK_ROPE)),
         resident((B_HEADS, t, V_HEAD)), resident((B_HEADS, V_HEAD, D_MODEL))],
        [tile_spec, per_head(V_HEAD)], [_sds((t, D_MODEL), F32), _sds((B_HEADS, t, V_HEAD), BF16)],
        (h, q, k, v, w_o))


def _mlp_bwd(h, a, dho, g, w1, w2, layer, after=()):
    t = h.shape[0]

    def body(h_ref, a_ref, dho_ref, g_ref, w1_ref, w2_ref, dhi_ref, dg_ref, hn_ref, f_ref, da_ref, dhib_ref):
        gv = g_ref[...]
        y, xhat, rstd = _rms_fwd(h_ref[...], gv)
        hn_ref[...] = y.astype(BF16)
        dho_v = dho_ref[...]
        dhob = dho_v.astype(BF16)
        dhn = jnp.zeros((TM, D_MODEL), F32)
        for d in range(N_DEV):
            cs = slice(d * FF_SLOT, (d + 1) * FF_SLOT)
            r = jnp.maximum(a_ref[:, cs], 0.0)
            f_ref[:, cs] = (r * r).astype(BF16)
            da = (_dot_nt(dhob, w2_ref[d]) * (2.0 * r)).astype(BF16)
            da_ref[:, cs] = da
            dhn = dhn + _dot_nt(da, w1_ref[d])
        dx, dg = _rms_bwd(dhn, xhat, rstd, gv)
        dhi = dho_v + dx
        dhi_ref[...] = dhi
        dhib_ref[...] = dhi.astype(BF16)
        _acc(dg_ref, dg)

    return _call(
        f"mlp_bwd_{layer}", body, (t // TM,),
        [_row(D_MODEL), _row(D_FF), _row(D_MODEL), _res((1, D_MODEL)), *MLP_W_SPECS],
        [_row(D_MODEL), _const((1, D_MODEL)), _row(D_MODEL), _row(D_FF), _row(D_FF), _row(D_MODEL)],
        [_sds((t, D_MODEL), F32), _sds((1, D_MODEL), F32), _sds((t, D_MODEL), BF16), _sds((t, D_FF), BF16),
         _sds((t, D_FF), BF16), _sds((t, D_MODEL), BF16)],
        (h, a, dho, g, w1, w2), after=after)


def _attn_bwd(dh, q, k, v, w_o, cos, sin, after=()):
    t = dh.shape[0]
    half, hps = QK_ROPE // 2, HEADS_PER_STEP

    def body(dh_ref, q_ref, k_ref, v_ref, wo_ref, cos_ref, sin_ref, dq_ref, dk_ref, dv_ref):
        i = pl.program_id(1)

        @pl.when(i == 0)
        def _():
            dk_ref[...] = jnp.zeros_like(dk_ref)
            dv_ref[...] = jnp.zeros_like(dv_ref)

        def tile(kt):
            keys = slice(0, (kt + 1) * TM)
            for j in range(hps):
                qj = q_ref[j]
                do = _dot_nt(dh_ref[kt * TM:(kt + 1) * TM, :], wo_ref[j]).astype(BF16)
                p = _softmax_rows(qj, k_ref.at[j], kt)
                dp = _dot_nt(do, v_ref[j, keys, :])
                ds = (p * (dp - jnp.sum(p * dp, axis=-1, keepdims=True)) * ATT_SCALE).astype(BF16)
                dq = _dot(ds, k_ref[j, keys, :])
                dq_ref[j, :, 0:QK_NOPE] = dq[:, :QK_NOPE].astype(BF16)
                dq_ref[j, :, QK_NOPE:] = _rope(dq[:, QK_NOPE:], cos_ref[...], -sin_ref[...]).astype(BF16)
                dk_ref[j, keys, :] += _dot_tn(ds, qj)
                dv_ref[j, keys, :] += _dot_tn(p.astype(BF16), do)

        _for_my_tile(i, t // TM, tile)

    def per_pair(rows, d, tiled):
        return pl.BlockSpec((hps, rows, d), (lambda pair, i: (pair, i, 0)) if tiled else (lambda pair, i: (pair, 0, 0)))

    def tile(d):
        return pl.BlockSpec((TM, d), lambda pair, i: (i, 0))

    return _call(
        "attn_bwd", body, (B_HEADS // hps, t // TM),
        [pl.BlockSpec((t, D_MODEL), lambda pair, i: (0, 0), pipeline_mode=pl.Buffered(1)),
         per_pair(TM, QK_NOPE + QK_ROPE, True), per_pair(t, QK_NOPE + QK_ROPE, False), per_pair(t, V_HEAD, False),
         per_pair(V_HEAD, D_MODEL, False), tile(half), tile(half)],
        [per_pair(TM, QK_NOPE + QK_ROPE, True), per_pair(t, QK_NOPE + QK_ROPE, False), per_pair(t, V_HEAD, False)],
        [_sds((B_HEADS, t, QK_NOPE + QK_ROPE), BF16), _sds((B_HEADS, t, QK_NOPE + QK_ROPE), F32),
         _sds((B_HEADS, t, V_HEAD), F32)],
        (dh, q, k, v, w_o, cos, sin), after=after)


def _kvq_bwd(h, dh, ckv, cqpre, dq, dk, dv, cos, sin, kvq_w, after=()):
    t = h.shape[0]
    tm = TM
    half, last = QK_ROPE // 2, t // tm - 1
    grad_shapes = [(D_MODEL, Q_LORA), (B_HEADS, Q_LORA, QK_NOPE + QK_ROPE), (D_MODEL, KV_LORA + QK_ROPE),
                   (B_HEADS, KV_LORA, QK_NOPE + V_HEAD)]

    def body(h_ref, dh_ref, ckv_ref, cqpre_ref, dq_ref, dk_ref, dv_ref, cos_ref, sin_ref,
             srcg_ref, wkva_ref, kvag_ref, wkvb_ref, mixg_ref, wqa_ref, qg_ref, wqb_ref,
             dhi_ref, dmixg_ref, dsrcg_ref, dqg_ref, dkvag_ref, gqa_ref, gqb_ref, gkva_ref, gkvb_ref,
             aqa, aqb, akva, akvb):
        @pl.when(pl.program_id(0) == 0)
        def _():
            for acc in (aqa, aqb, akva, akvb):
                acc[...] = jnp.zeros_like(acc)

        hv = h_ref[...]
        rstd = lax.rsqrt(jnp.mean(hv * hv, axis=-1, keepdims=True) + EPS)
        xhat = hv * rstd
        mixg, srcg, qg, kvag = mixg_ref[...], srcg_ref[...], qg_ref[...], kvag_ref[...]
        cq, cqhat, crstd = _rms_fwd(cqpre_ref[...], qg)
        cqb = cq.astype(BF16)
        dcq = jnp.zeros((tm, Q_LORA), F32)
        for hd in range(B_HEADS):
            dcq = dcq + _dot_nt(dq_ref[hd], wqb_ref[hd])
            aqb[hd] += _dot_tn(cqb, dq_ref[hd])
        dcqpre, dqg = _rms_bwd(dcq, cqhat, crstd, qg)
        dcqpre_b = dcqpre.astype(BF16)
        aqa[...] += _dot_tn((xhat * mixg).astype(BF16), dcqpre_b)
        dxq, dmixg = _rms_bwd(_dot_nt(dcqpre_b, wqa_ref[...]), xhat, rstd, mixg)
        ckv = ckv_ref[...]
        c, chat, krstd = _rms_fwd(ckv[:, :KV_LORA], kvag)
        cb = c.astype(BF16)
        dc = jnp.zeros((tm, KV_LORA), F32)
        dkpe = jnp.zeros((tm, QK_ROPE), F32)
        for hd in range(B_HEADS):
            dkv = jnp.concatenate([dk_ref[hd, :, 0:QK_NOPE], dv_ref[hd]], axis=-1).astype(BF16)
            akvb[hd] += _dot_tn(cb, dkv)
            dc = dc + _dot_nt(dkv, wkvb_ref[hd])
            dkpe = dkpe + dk_ref[hd, :, QK_NOPE:]
        dlat, dkvag = _rms_bwd(dc, chat, krstd, kvag)
        dpe = _rope(dkpe, cos_ref[...], -sin_ref[...])
        dckv_b = jnp.concatenate([dlat, dpe], axis=-1).astype(BF16)
        akva[...] += _dot_tn((xhat * srcg).astype(BF16), dckv_b)
        dxk, dsrcg = _rms_bwd(_dot_nt(dckv_b, wkva_ref[...]), xhat, rstd, srcg)
        dhi_ref[...] = dh_ref[...] + dxq + dxk
        _acc(dmixg_ref, dmixg)
        _acc(dsrcg_ref, dsrcg)
        _acc(dqg_ref, dqg)
        _acc(dkvag_ref, dkvag)

        @pl.when(pl.program_id(0) == last)
        def _():
            for out, acc in ((gqa_ref, aqa), (gqb_ref, aqb), (gkva_ref, akva), (gkvb_ref, akvb)):
                out[...] = acc[...].astype(BF16)

    return _call(
        "kvq_bwd", body, (t // tm,),
        [_row(D_MODEL, tm), _row(D_MODEL, tm), _row(KV_LORA + QK_ROPE, tm), _row(Q_LORA, tm),
         _heads(QK_NOPE + QK_ROPE, tm), _heads(QK_NOPE + QK_ROPE, tm), _heads(V_HEAD, tm), _row(half, tm),
         _row(half, tm), *KVQ_W_SPECS],
        [_row(D_MODEL, tm), _const((1, D_MODEL)), _const((1, D_MODEL)), _const((1, Q_LORA)), _const((1, KV_LORA))]
        + [_const(s) for s in grad_shapes],
        [_sds((t, D_MODEL), F32), _sds((1, D_MODEL), F32), _sds((1, D_MODEL), F32), _sds((1, Q_LORA), F32),
         _sds((1, KV_LORA), F32)] + [_sds(s, BF16) for s in grad_shapes],
        (h, dh, ckv, cqpre, dq, dk, dv, cos, sin, *kvq_w), scratch=[pltpu.VMEM(s, F32) for s in grad_shapes],
        after=after)


def _a_mix_bwd(x, z, dh, g, w_in, ln_g, ln_b, w_s, b_st, w_out, after=()):
    t = x.shape[0]
    tm = TM_GATE
    nblk = tm // GMLP_BLOCK

    def body(x_ref, z_ref, dh_ref, g_ref, win_ref, lng_ref, lnb_ref, ws_ref, bst_ref, wout_ref,
             dx_ref, hn_ref, dz_ref, dg_ref, dlng_ref, dlnb_ref, dws_ref, dbs_ref, dvn_scr, gelu_grad_v, dws_acc):
        @pl.when(pl.program_id(0) == 0)
        def _():
            dws_acc[...] = jnp.zeros_like(dws_acc)
            dbs_ref[...] = jnp.zeros_like(dbs_ref)

        gv, lng = g_ref[...], lng_ref[...]
        y, xhat, rstd = _rms_fwd(x_ref[...], gv)
        hn_ref[...] = y.astype(BF16)
        dhv = dh_ref[...]
        dgated = _dot_nt(dhv.astype(BF16), wout_ref[...])
        u, gelu_grad_u = _gelu_and_grad(z_ref[:, :GATE_DIM])
        v, gelu_grad_v[...] = _gelu_and_grad(z_ref[:, GATE_DIM:])
        vn, vhat, lrstd = _ln_fwd(v, lng, lnb_ref[...])
        vb = vn.astype(BF16)
        mask = _gate_mask()
        for gi in range(A_GROUPS):
            wm = jnp.where(mask, ws_ref[gi], 0.0).astype(BF16)
            bias = bst_ref[:, gi:gi + 1]
            cs = slice(gi * A_GROUP_DIM, (gi + 1) * A_GROUP_DIM)
            dws = jnp.zeros((GMLP_BLOCK, GMLP_BLOCK), F32)
            dbs = jnp.zeros((GMLP_BLOCK, 1), F32)
            for n in range(nblk):
                rs = slice(n * GMLP_BLOCK, (n + 1) * GMLP_BLOCK)
                sv = _dot(wm, vb[rs, cs]) + bias
                dz_ref[rs, cs] = (dgated[rs, cs] * sv * gelu_grad_u[rs, cs]).astype(BF16)
                dsv = dgated[rs, cs] * u[rs, cs]
                dsvb = dsv.astype(BF16)
                dws = dws + _dot_nt(dsvb, vb[rs, cs])
                dbs = dbs + jnp.sum(dsv, axis=-1, keepdims=True)
                dvn_scr[rs, cs] = _dot_tn(wm, dsvb)
            dws_acc[gi] += jnp.where(mask, dws, 0.0)
            dbs_ref[gi] += dbs
        dvn = dvn_scr[...]
        dvhat = dvn * lng
        dv = lrstd * (dvhat - jnp.mean(dvhat, axis=-1, keepdims=True)
                      - vhat * jnp.mean(dvhat * vhat, axis=-1, keepdims=True))
        dz_ref[:, GATE_DIM:] = (dv * gelu_grad_v[...]).astype(BF16)
        dhn = jnp.zeros((tm, D_MODEL), F32)
        for d in range(N_DEV):
            dhn = dhn + _dot_nt(dz_ref[:, d * FF_SLOT:(d + 1) * FF_SLOT], win_ref[d])
        dx, dg = _rms_bwd(dhn, xhat, rstd, gv)
        dx_ref[...] = dhv + dx
        _acc(dg_ref, dg)
        dlng, dlnb = jnp.sum(dvn * vhat, axis=0, keepdims=True), jnp.sum(dvn, axis=0, keepdims=True)

        @pl.when(pl.program_id(0) == 0)
        def _():
            dlng_ref[...] = jnp.zeros_like(dlng_ref)
            dlnb_ref[...] = jnp.zeros_like(dlnb_ref)

        width = GATE_DIM // N_DEV
        for d in range(N_DEV):
            dlng_ref[d:d + 1, :] += dlng[:, d * width:(d + 1) * width]
            dlnb_ref[d:d + 1, :] += dlnb[:, d * width:(d + 1) * width]

        @pl.when(pl.program_id(0) == t // tm - 1)
        def _():
            dws_ref[...] = dws_acc[...].astype(BF16)

    return _call(
        "a_mix_bwd", body, (t // tm,),
        [_row(D_MODEL, tm), _row(2 * GATE_DIM, tm), _row(D_MODEL, tm), _res((1, D_MODEL)),
         _res((N_DEV, D_MODEL, FF_SLOT)), _res((1, GATE_DIM)), _res((1, GATE_DIM)),
         _res((A_GROUPS, GMLP_BLOCK, GMLP_BLOCK)), _res((GMLP_BLOCK, A_GROUPS)), _res((GATE_DIM, D_MODEL))],
        [_row(D_MODEL, tm), _row(D_MODEL, tm), _row(2 * GATE_DIM, tm),
         _const((1, D_MODEL)), _const((N_DEV, GATE_DIM // N_DEV)), _const((N_DEV, GATE_DIM // N_DEV)),
         _const((A_GROUPS, GMLP_BLOCK, GMLP_BLOCK)), _const((A_GROUPS, GMLP_BLOCK, 1))],
        [_sds((t, D_MODEL), F32), _sds((t, D_MODEL), BF16),
         _sds((t, 2 * GATE_DIM), BF16), _sds((1, D_MODEL), F32), _sds((N_DEV, GATE_DIM // N_DEV), F32),
         _sds((N_DEV, GATE_DIM // N_DEV), F32), _sds((A_GROUPS, GMLP_BLOCK, GMLP_BLOCK), BF16),
         _sds((A_GROUPS, GMLP_BLOCK, 1), F32)],
        (x, z, dh, g, w_in, ln_g, ln_b, w_s, b_st, w_out),
        scratch=[pltpu.VMEM((tm, GATE_DIM), F32), pltpu.VMEM((tm, GATE_DIM), F32),
                 pltpu.VMEM((A_GROUPS, GMLP_BLOCK, GMLP_BLOCK), F32)], after=after)


def _wgrad(name, a, b, a_spec, b_spec, m, n, after=()):
    def body(a_ref, b_ref, o_ref):
        o_ref[0] = _dot_tn(a_ref[...].astype(BF16), b_ref[...].astype(BF16)).astype(BF16)

    return _call(name, body, (N_DEV,), [a_spec, b_spec], [pl.BlockSpec((1, m, n), lambda d: (d, 0, 0))],
                 [_sds((N_DEV, m, n), BF16)], (a, b), after=after)[0]


def _full(t, d):
    return pl.BlockSpec((t, d), lambda i: (0, 0), pipeline_mode=pl.Buffered(1))


def _cols(t, d):
    return pl.BlockSpec((t, d), lambda i: (0, i))


def _head(t, d):
    return pl.BlockSpec((None, t, d), lambda i: (i, 0, 0))


def _local_step(x, pos, target, inv_freq, wg, sm, shards=None):
    t = x.shape[0]
    wg = dict(wg)
    dist = shards is not None
    mix_g = [sm["norm_mix_g"][l:l + 1] for l in range(2)]
    mlp_g = [sm["norm_mlp_g"][l:l + 1] for l in range(2)]

    ids = iter(range(2, 2 + 9))

    def gather(names):
        if dist:
            got = _by_sequencer("gather_" + names[0], _gather_comm([shards[k] for k in names]),
                                SIBLING_AND_NEIGHBOURS, next(ids))
            wg.update(zip(names, got))

    def send(name, names):
        if dist:
            comm = _exchange_comm(grads=[g[k] for k in names])
            g.update(zip(names, _by_sequencer("exchange_" + name, comm, EVERYONE, next(ids))))

    def send_sums(name, names, meanwhile):
        if not dist:
            meanwhile()
            return ()
        grads = [g[k] for k in names]
        landed = _by_sequencer("pair_exchange_" + name, _pair_exchange_comm(grads), (1,), next(ids))
        sums = _pair_add("pair_add_" + name, grads, landed, after=meanwhile())
        g.update(zip(names, _by_sequencer("exchange_" + name, _chip_exchange_comm(sums), OTHER_CHIPS, next(ids))))
        return sums

    def a_args():
        return (wg["a_w_in"], wg["a_ln_v_g"], wg["a_ln_v_b"], sm["a_w_s"], sm["a_b_st"], wg["a_w_out"])

    def kvq_w():
        return (sm["kv_src_norm_g"], wg["kv_w_a"], sm["kv_a_norm_g"], wg["kv_w_b"], mix_g[1], wg["b_w_q_a"],
                sm["b_q_norm_g"], wg["b_w_q_b"])

    gather(("mlp_w1_0", "mlp_w2_0"))
    h1, z, gated = _a_mix_fwd(x, mix_g[0], *a_args())
    gather(("kv_w_a", "kv_w_b", "b_w_q_a", "b_w_q_b", "b_w_o"))
    h2, a0 = _mlp_fwd(h1, mlp_g[0], wg["mlp_w1_0"], wg["mlp_w2_0"])
    if dist:
        wg["b_w_q_a"] = wg["b_w_q_a"].reshape(D_MODEL, Q_LORA)
        wg["kv_w_a"] = wg["kv_w_a"].reshape(D_MODEL, KV_LORA + QK_ROPE)
    gather(("mlp_w1_1", "mlp_w2_1"))
    ckv, k, v, cqpre, q, cos, sin = _kvq_fwd(h2, pos, inv_freq, kvq_w())
    h3, att = _attn_fwd(h2, q, k, v, wg["b_w_o"])
    a1, loss, dh4, d_final_g = _mlp_fwd_loss(h3, mlp_g[1], wg["mlp_w1_1"], wg["mlp_w2_1"], sm["final_norm_g"], target)

    g = {}
    dh3, d_mlp_g1, hn, f, da, dh3_b = _mlp_bwd(h3, a1, dh4, mlp_g[1], wg["mlp_w1_1"], wg["mlp_w2_1"], 1)
    dq, dk, dv = _attn_bwd(dh3_b, q, k, v, wg["b_w_o"], cos, sin)
    g["mlp_w1_1"] = _wgrad("wgrad_w1_1", hn, da, _full(t, D_MODEL), _cols(t, FF_SLOT), D_MODEL, FF_SLOT, after=[dq])
    g["mlp_w2_1"] = _wgrad("wgrad_w2_1", f, dh4, _cols(t, FF_SLOT), _full(t, D_MODEL), FF_SLOT, D_MODEL)

    def wgrad_w_o():
        g["b_w_o"] = _wgrad("wgrad_w_o", att, dh3_b, _head(t, V_HEAD), _full(t, D_MODEL), V_HEAD, D_MODEL)
        return [g["b_w_o"]]

    sums = send_sums("mlp_1", ("mlp_w1_1", "mlp_w2_1"), wgrad_w_o)
    dh2, d_mix_g1, d_src_g, d_q_g, d_kv_a_g, g_q_a, g["b_w_q_b"], g_kv_a, g["kv_w_b"] = _kvq_bwd(
        h2, dh3, ckv, cqpre, dq, dk, dv, cos, sin, kvq_w(), after=sums)
    g["b_w_q_a"] = g_q_a.reshape(N_DEV, D_MODEL // N_DEV, Q_LORA)
    g["kv_w_a"] = g_kv_a.reshape(N_DEV, D_MODEL // N_DEV, KV_LORA + QK_ROPE)
    qkv = ("b_w_q_a", "b_w_q_b", "kv_w_a", "kv_w_b")
    landed = [g[k] for k in qkv]
    send("qkv", qkv)
    dh1, d_mlp_g0, hn, f, da, dh1_b = _mlp_bwd(h1, a0, dh2, mlp_g[0], wg["mlp_w1_0"], wg["mlp_w2_0"], 0,
                                               after=landed if dist else ())
    landed = [g["mlp_w1_1"], g["mlp_w2_1"]] if dist else ()
    g["mlp_w1_0"] = _wgrad("wgrad_w1_0", hn, da, _full(t, D_MODEL), _cols(t, FF_SLOT), D_MODEL, FF_SLOT, after=landed)
    g["mlp_w2_0"] = _wgrad("wgrad_w2_0", f, dh2, _cols(t, FF_SLOT), _full(t, D_MODEL), FF_SLOT, D_MODEL)

    def wgrad_a_w_out():
        g["a_w_out"] = _wgrad("wgrad_a_w_out", gated, dh1_b, _cols(t, GATE_DIM // N_DEV), _full(t, D_MODEL),
                              GATE_DIM // N_DEV, D_MODEL)
        return [g["a_w_out"]] + [g[k] for k in qkv]

    sums = send_sums("mlp_0", ("mlp_w1_0", "mlp_w2_0", "b_w_o"), wgrad_a_w_out)
    dx, hn, dz, d_mix_g0, d_ln_g, d_ln_b, d_ws, d_bs = _a_mix_bwd(x, z, dh1, mix_g[0], *a_args(), after=sums)
    small = {
        "norm_mix_g": jnp.concatenate([d_mix_g0, d_mix_g1], axis=0),
        "norm_mlp_g": jnp.concatenate([d_mlp_g0, d_mlp_g1], axis=0),
        "a_ln_v_g": d_ln_g,
        "a_ln_v_b": d_ln_b,
        "a_w_s": d_ws,
        "a_b_s": d_bs.reshape(A_GROUPS, GMLP_BLOCK),
        "b_q_norm_g": d_q_g,
        "kv_src_norm_g": d_src_g,
        "kv_a_norm_g": d_kv_a_g,
        "final_norm_g": d_final_g,
    }
    if dist:
        parts = [small[k].reshape((1,) + small[k].shape) for k in SMALL] + [loss.reshape(1, 1, 1)]
        got = _by_sequencer("gather_small", _gather_comm(parts), SIBLING_AND_NEIGHBOURS, next(ids))
        small, loss = dict(zip(SMALL, got)), got[-1]
    g["a_w_in"] = _wgrad("wgrad_a_w_in", hn, dz, _full(t, D_MODEL), _cols(t, FF_SLOT), D_MODEL, FF_SLOT)
    return loss, dx, g, small


def _adamw(w, g, m, v):
    m = ADAM_B1 * m + (1.0 - ADAM_B1) * g
    v = ADAM_B2 * v + (1.0 - ADAM_B2) * (g * g)
    m_hat = m / (1.0 - ADAM_B1 ** ADAM_STEP)
    v_hat = v / (1.0 - ADAM_B2 ** ADAM_STEP)
    return -ADAM_LR * (m_hat / (jnp.sqrt(v_hat) + ADAM_EPS) + ADAM_WD * w), m, v


def _sum_in_device_order(r_ref):
    g = r_ref[0].astype(F32)
    for j in range(1, r_ref.shape[0]):
        g = g + r_ref[j].astype(F32)
    return g


def _adamw_sharded(name, recvs, w, m, v):
    layers, r, c = w.shape
    tr = math.gcd(r, 512)
    flat = [a for per_layer in recvs for a in per_layer]

    def body(*refs):
        r_refs, (w_ref, m_ref, v_ref) = refs[:len(flat)], refs[len(flat):len(flat) + 3]
        g_ref, d_ref, nm_ref, nv_ref = refs[-4:]
        layer = pl.program_id(0)
        g, pos = None, 0
        for li, per_layer in enumerate(recvs):
            total = None
            for ref in r_refs[pos:pos + len(per_layer)]:
                part = _sum_in_device_order(ref)
                total = part if total is None else total + part
            pos += len(per_layer)
            g = total if g is None else jnp.where(layer == li, total, g)
        g_ref[...] = g
        d_ref[...], nm_ref[...], nv_ref[...] = _adamw(w_ref[...], g, m_ref[...], v_ref[...])

    blk = pl.BlockSpec((None, tr, c), lambda l, i: (l, i, 0))
    return _call(name, body, (layers, r // tr),
                 [pl.BlockSpec((a.shape[0], tr, c), lambda l, i: (0, i, 0)) for a in flat] + [blk] * 3,
                 [blk] * 4, [_sds(w.shape, F32)] * 4, (*flat, w, m, v))


def _adamw_small(recvs, ws, ms, vs, own_row, losses):
    n = len(recvs)

    def body(*refs):
        r_refs, w_refs, m_refs, v_refs = (refs[i * n:(i + 1) * n] for i in range(4))
        outs, scr = refs[4 * n + 1:8 * n + 2], refs[8 * n + 2:]
        outs[-1][...] = _sum_in_device_order(refs[4 * n])
        me = _my_place()[3]
        for a in range(n):
            g = _sum_in_device_order(r_refs[a])
            if own_row[a]:
                scr[0][...] = g
                g = scr[0][pl.ds(me, 1), :]
            g_ref, d_ref, nm_ref, nv_ref = outs[4 * a:4 * a + 4]
            g_ref[...] = g
            d_ref[...], nm_ref[...], nv_ref[...] = _adamw(w_refs[a][...], g, m_refs[a][...], v_refs[a][...])

    out_shape = []
    for w in ws:
        out_shape += [_sds(w.shape, F32)] * 4
    return pl.pallas_call(
        body, name="adamw_small", in_specs=[VMEM] * (4 * n + 1), out_specs=[VMEM] * (4 * n + 1),
        out_shape=out_shape + [_sds((1, 1), F32)], scratch_shapes=[pltpu.VMEM((N_DEV, GATE_DIM // N_DEV), F32)],
    )(*recvs, *ws, *ms, *vs, losses)


BIG = ("a_w_in", "a_w_out", "b_w_q_a", "b_w_q_b", "b_w_o", "kv_w_a", "kv_w_b", "mlp_w1", "mlp_w2")
SMALL = ("norm_mix_g", "norm_mlp_g", "a_ln_v_g", "a_ln_v_b", "a_w_s", "a_b_s", "b_q_norm_g", "kv_src_norm_g",
         "kv_a_norm_g", "final_norm_g")
WEIGHTS = ("norm_mix_g", "norm_mlp_g", "a_w_in", "a_ln_v_g", "a_ln_v_b", "a_w_s", "a_b_s", "a_w_out", "b_w_q_a",
           "b_q_norm_g", "b_w_q_b", "b_w_o", "kv_src_norm_g", "kv_w_a", "kv_a_norm_g", "kv_w_b", "mlp_w1", "mlp_w2",
           "final_norm_g")


def _two_d(name, a):
    if name in ("a_w_s", "a_b_s"):
        return a.reshape(a.shape[1:])
    return a.reshape(1, -1) if a.ndim == 1 else a


def _three_d(a):
    return a if a.ndim == 3 else a.reshape((1,) + a.shape)


def kernel(x, positions, norm_mix_g, norm_mlp_g, a_w_in, a_ln_v_g, a_ln_v_b, a_w_s, a_b_s, a_w_out, b_w_q_a, b_q_norm_g, b_w_q_b, b_w_o, kv_src_norm_g, kv_w_a, kv_a_norm_g, kv_w_b, mlp_w1, mlp_w2, final_norm_g, loss_target, m_norm_mix_g, m_norm_mlp_g, m_a_w_in, m_a_ln_v_g, m_a_ln_v_b, m_a_w_s, m_a_b_s, m_a_w_out, m_b_w_q_a, m_b_q_norm_g, m_b_w_q_b, m_b_w_o, m_kv_src_norm_g, m_kv_w_a, m_kv_a_norm_g, m_kv_w_b, m_mlp_w1, m_mlp_w2, m_final_norm_g, v_norm_mix_g, v_norm_mlp_g, v_a_w_in, v_a_ln_v_g, v_a_ln_v_b, v_a_w_s, v_a_b_s, v_a_w_out, v_b_w_q_a, v_b_q_norm_g, v_b_w_q_b, v_b_w_o, v_kv_src_norm_g, v_kv_w_a, v_kv_a_norm_g, v_kv_w_b, v_mlp_w1, v_mlp_w2, v_final_norm_g):
    w = dict(norm_mix_g=norm_mix_g, norm_mlp_g=norm_mlp_g, a_w_in=a_w_in, a_ln_v_g=a_ln_v_g, a_ln_v_b=a_ln_v_b,
             a_w_s=a_w_s, a_b_s=a_b_s, a_w_out=a_w_out, b_w_q_a=b_w_q_a, b_q_norm_g=b_q_norm_g, b_w_q_b=b_w_q_b,
             b_w_o=b_w_o, kv_src_norm_g=kv_src_norm_g, kv_w_a=kv_w_a, kv_a_norm_g=kv_a_norm_g, kv_w_b=kv_w_b,
             mlp_w1=mlp_w1, mlp_w2=mlp_w2, final_norm_g=final_norm_g)
    m = dict(norm_mix_g=m_norm_mix_g, norm_mlp_g=m_norm_mlp_g, a_w_in=m_a_w_in, a_ln_v_g=m_a_ln_v_g,
             a_ln_v_b=m_a_ln_v_b, a_w_s=m_a_w_s, a_b_s=m_a_b_s, a_w_out=m_a_w_out, b_w_q_a=m_b_w_q_a,
             b_q_norm_g=m_b_q_norm_g, b_w_q_b=m_b_w_q_b, b_w_o=m_b_w_o, kv_src_norm_g=m_kv_src_norm_g,
             kv_w_a=m_kv_w_a, kv_a_norm_g=m_kv_a_norm_g, kv_w_b=m_kv_w_b, mlp_w1=m_mlp_w1, mlp_w2=m_mlp_w2,
             final_norm_g=m_final_norm_g)
    v = dict(norm_mix_g=v_norm_mix_g, norm_mlp_g=v_norm_mlp_g, a_w_in=v_a_w_in, a_ln_v_g=v_a_ln_v_g,
             a_ln_v_b=v_a_ln_v_b, a_w_s=v_a_w_s, a_b_s=v_a_b_s, a_w_out=v_a_w_out, b_w_q_a=v_b_w_q_a,
             b_q_norm_g=v_b_q_norm_g, b_w_q_b=v_b_w_q_b, b_w_o=v_b_w_o, kv_src_norm_g=v_kv_src_norm_g,
             kv_w_a=v_kv_w_a, kv_a_norm_g=v_kv_a_norm_g, kv_w_b=v_kv_w_b, mlp_w1=v_mlp_w1, mlp_w2=v_mlp_w2,
             final_norm_g=v_final_norm_g)
    t = x.shape[1]

    first = ("a_w_in", "a_w_out", "a_ln_v_g", "a_ln_v_b")
    later = ("mlp_w1", "mlp_w2", "kv_w_a", "kv_w_b", "b_w_q_a", "b_w_q_b", "b_w_o")
    later_blocks = ("mlp_w1_0", "mlp_w1_1", "mlp_w2_0", "mlp_w2_1") + later[2:]
    got, casts = _gather_first([_three_d(w[k]) if k in BIG else w[k] for k in first], [_three_d(w[k]) for k in later])
    wg = dict(zip(first, got))
    wg["a_w_out"] = wg["a_w_out"].reshape(GATE_DIM, D_MODEL)
    wg["a_ln_v_g"] = wg["a_ln_v_g"].reshape(1, GATE_DIM)
    wg["a_ln_v_b"] = wg["a_ln_v_b"].reshape(1, GATE_DIM)
    shards = dict(zip(later_blocks, casts))

    sm = {k: _two_d(k, w[k]) for k in SMALL if k not in ("a_ln_v_g", "a_ln_v_b")}
    sm["a_b_st"] = sm["a_b_s"].T
    inv_freq = (ROPE_THETA ** (-jnp.arange(0, QK_ROPE, 2, dtype=F32) / QK_ROPE)).reshape(1, QK_ROPE // 2)

    losses, dx, g, small = _local_step(x[0], positions.reshape(t, 1), loss_target[0], inv_freq, wg, sm, shards)

    names = ("a_w_in", "a_w_out")
    sums = _pair_reduce("pair_reduce_a", [g[k] for k in names], after=[g["mlp_w1_0"], g["mlp_w2_0"]])
    g.update(zip(names, _by_sequencer("exchange_last", _chip_exchange_comm(sums), OTHER_CHIPS, collective_id=1)))

    out = {}
    for k in BIG:
        recvs = [[g[k + "_0"]], [g[k + "_1"]]] if k.startswith("mlp") else [[g[k]]]
        res = _adamw_sharded("adamw_" + k, recvs, _three_d(w[k]), _three_d(m[k]), _three_d(v[k]))
        out[k] = [o.reshape(w[k].shape) for o in res]
    own_row = [k in ("a_ln_v_g", "a_ln_v_b") for k in SMALL]
    res = _adamw_small([small[k] for k in SMALL], [_two_d(k, w[k]) for k in SMALL], [_two_d(k, m[k]) for k in SMALL],
                       [_two_d(k, v[k]) for k in SMALL], own_row, losses)
    for i, k in enumerate(SMALL):
        out[k] = [o.reshape(w[k].shape) for o in res[4 * i:4 * i + 4]]

    return (res[-1].reshape(()), dx.reshape(x.shape), *[out[k][0] for k in WEIGHTS], *[out[k][1] for k in WEIGHTS],
            *[out[k][2] for k in WEIGHTS], *[out[k][3] for k in WEIGHTS])
```

```python
import math

import jax
import jax.numpy as jnp
from jax import lax
from jax.experimental import pallas as pl
from jax.experimental.pallas import tpu as pltpu
from jax.experimental.pallas import tpu_sc as plsc

F32, BF16 = jnp.float32, jnp.bfloat16
MESH = pl.DeviceIdType.MESH
ANY = pl.BlockSpec(memory_space=pl.ANY)
VMEM = pl.BlockSpec(memory_space=pltpu.VMEM)

N_DEV = 8
D_MODEL = 1024
CHUNK = 64
GMLP_BLOCK = 128
GATE_DIM = 2048
A_GROUPS = 8
A_GROUP_DIM = GATE_DIM // A_GROUPS
B_HEADS = 8
QK_NOPE, QK_ROPE, V_HEAD = 128, 64, 128
Q_LORA, KV_LORA = 384, 256
ROPE_THETA = 10000.0
D_FF = 4096
FF_SLOT = D_FF // N_DEV
EPS = 1e-6
ATT_SCALE = (QK_NOPE + QK_ROPE) ** -0.5

ADAM_LR, ADAM_B1, ADAM_B2, ADAM_EPS, ADAM_WD, ADAM_STEP = 0.001, 0.9, 0.999, 1e-08, 0.01, 10

TM = 256
TM_GATE = 256
TM_MLP_FWD = 512
TM_KVQ = 512
VMEM_LIMIT = 56 * 1024 * 1024
INV_SQRT2 = 1.0 / math.sqrt(2.0)
INV_SQRT_2PI = 1.0 / math.sqrt(2.0 * math.pi)
LOG2_E = 1.0 / math.log(2.0)
HEADS_PER_STEP = 2


def _dot(a, b):
    return jnp.dot(a, b, preferred_element_type=F32)


def _dot_nt(a, b):
    return lax.dot_general(a, b, (((1,), (1,)), ((), ())), preferred_element_type=F32)


def _dot_tn(a, b):
    return lax.dot_general(a, b, (((0,), (0,)), ((), ())), preferred_element_type=F32)


def _rms_fwd(x, g):
    rstd = lax.rsqrt(jnp.mean(x * x, axis=-1, keepdims=True) + EPS)
    xhat = x * rstd
    return xhat * g, xhat, rstd


def _rms_bwd(dy, xhat, rstd, g):
    dxhat = dy * g
    dx = rstd * (dxhat - xhat * jnp.mean(dxhat * xhat, axis=-1, keepdims=True))
    return dx, jnp.sum(dy * xhat, axis=0, keepdims=True)


def _ln_fwd(v, g, b):
    mu = jnp.mean(v, axis=-1, keepdims=True)
    vc = v - mu
    rstd = lax.rsqrt(jnp.mean(vc * vc, axis=-1, keepdims=True) + EPS)
    vhat = vc * rstd
    return vhat * g + b, vhat, rstd


def _gelu(x):
    return 0.5 * x * (1.0 + lax.erf(x * INV_SQRT2))


def _gelu_and_grad(x):
    cdf = 0.5 * (1.0 + lax.erf(x * INV_SQRT2))
    return x * cdf, cdf + x * jnp.exp(-0.5 * x * x) * INV_SQRT_2PI


def _rope(x, cos, sin):
    x1, x2 = x[:, :QK_ROPE // 2], x[:, QK_ROPE // 2:]
    return jnp.concatenate([x1 * cos - x2 * sin, x2 * cos + x1 * sin], axis=-1)


def _gate_mask():
    row = lax.broadcasted_iota(jnp.int32, (GMLP_BLOCK, GMLP_BLOCK), 0)
    col = lax.broadcasted_iota(jnp.int32, (GMLP_BLOCK, GMLP_BLOCK), 1)
    return (col < CHUNK) | (row >= CHUNK)


def _att_mask(q0, tq, t):
    q = q0 + lax.broadcasted_iota(jnp.int32, (tq, t), 0)
    k = lax.broadcasted_iota(jnp.int32, (tq, t), 1)
    return jnp.right_shift(k, 6) <= jnp.right_shift(q, 6)


def _res(shape, imap=None):
    zeros = (0,) * len(shape)
    return pl.BlockSpec(shape, imap or (lambda i: zeros), pipeline_mode=pl.Buffered(1))


def _const(shape):
    zeros = (0,) * len(shape)
    return pl.BlockSpec(shape, lambda i: zeros)


def _row(d, tm=TM):
    return pl.BlockSpec((tm, d), lambda i: (i, 0))


def _heads(d, tm=TM):
    return pl.BlockSpec((B_HEADS, tm, d), lambda i: (0, i, 0))


def _sds(shape, dt):
    return jax.ShapeDtypeStruct(shape, dt)


def _acc(ref, val):
    @pl.when(pl.program_id(0) == 0)
    def _():
        ref[...] = jnp.zeros_like(ref)
    ref[...] += val


def _my_place():
    x, y, c = lax.axis_index("x"), lax.axis_index("y"), lax.axis_index("c")
    return x, y, c, 4 * x + 2 * y + c


def _peer(x, y, c, k):
    px = 1 - x if k & 4 else x
    py = 1 - y if k & 2 else y
    pc = 1 - c if k & 1 else c
    return (px, py, pc), 4 * px + 2 * py + pc


CHIPS = (2, 4, 6)


def _splits(ref):
    return len(ref.shape) >= 3 and ref.shape[1] % 32 == 0


def _piece(ref, block, half=None):
    if half is None or not _splits(ref):
        return ref.at[pl.ds(block, 1)]
    rows = ref.shape[1] // 2
    return ref.at[pl.ds(block, 1), pl.ds(half * rows, rows)]


def _gather_copy(sems, a, k, piece, to, src=None):
    return pltpu.make_async_remote_copy(
        src_ref=piece if src is None else src, dst_ref=piece, send_sem=sems[0].at[a, k], recv_sem=sems[1].at[a, k],
        device_id=to, device_id_type=MESH)


def _gather_start(srcs, outs, sems, only=None):
    x, y, c, me = _my_place()
    for a in range(len(srcs)) if only is None else (only,):
        mine = _piece(outs[a], me)
        pltpu.make_async_copy(srcs[a], mine, sems[2].at[a]).start()
        for k, rel in enumerate((1, 4, 2)):
            _gather_copy(sems, a, k, mine, _peer(x, y, c, rel)[0], src=srcs[a]).start()


def _gather_relay(srcs, outs, sems):
    x, y, c, _ = _my_place()
    sib = _peer(x, y, c, 1)[0]
    (xn, xn_i), (yn, yn_i) = _peer(x, y, c, 4), _peer(x, y, c, 2)
    for a in range(len(srcs)):
        out = outs[a]
        _gather_copy(sems, a, 1, _piece(out, xn_i), xn).wait_recv()
        _gather_copy(sems, a, 3, _piece(out, xn_i, 0), yn).start()
        _gather_copy(sems, a, 5, _piece(out, xn_i), sib).start()
        _gather_copy(sems, a, 2, _piece(out, yn_i), yn).wait_recv()
        if _splits(out):
            _gather_copy(sems, a, 4, _piece(out, yn_i, 1), xn).start()
        _gather_copy(sems, a, 6, _piece(out, yn_i), sib).start()


def _gather_finish(srcs, outs, sems):
    x, y, c, me = _my_place()
    sib = _peer(x, y, c, 1)[0]
    xn, yn, dg_i = _peer(x, y, c, 4)[0], _peer(x, y, c, 2)[0], _peer(x, y, c, 6)[1]
    n = len(srcs)
    for a in range(n):
        out = outs[a]
        _gather_copy(sems, a, 3, _piece(out, dg_i, 0), yn).wait_recv()
        _gather_copy(sems, a, 7, _piece(out, dg_i, 0), sib).start()
        if _splits(out):
            _gather_copy(sems, a, 4, _piece(out, dg_i, 1), xn).wait_recv()
            _gather_copy(sems, a, 8, _piece(out, dg_i, 1), sib).start()
    for a in range(n):
        out = outs[a]
        whole, half = _piece(out, me), _piece(out, me, 0)
        for k in (0, 5, 6):
            _gather_copy(sems, a, k, whole, sib).wait_recv()
        for k in (7, 8) if _splits(out) else (7,):
            _gather_copy(sems, a, k, half, sib).wait_recv()
        for k in (0, 1, 2):
            _gather_copy(sems, a, k, whole, sib, src=srcs[a]).wait_send()
        for k in (5, 6):
            _gather_copy(sems, a, k, whole, sib).wait_send()
        for k in (3, 4, 7, 8) if _splits(out) else (3, 7):
            _gather_copy(sems, a, k, half, sib).wait_send()
        pltpu.make_async_copy(srcs[a], whole, sems[2].at[a]).wait()


def _relay_sems(n):
    return [pltpu.SemaphoreType.DMA((n, 9)), pltpu.SemaphoreType.DMA((n, 9)), pltpu.SemaphoreType.DMA((n,))]


def _gather_sems(n):
    return [pltpu.SemaphoreType.DMA((n, 7)), pltpu.SemaphoreType.DMA((n, 7)), pltpu.SemaphoreType.DMA((n,))]


class _Comm:
    def __init__(self, args, out_shape, scratch, start, finish, relay=None):
        self.args, self.out_shape, self.scratch, self.start, self.finish = args, out_shape, scratch, start, finish
        self.relay = relay


def _gather_comm(shards):
    return _Comm(list(shards), [_sds((N_DEV,) + s.shape[1:], s.dtype) for s in shards], _relay_sems(len(shards)),
                 _gather_start, _gather_finish, relay=_gather_relay)


def _direct_copies(ins, outs, sems, wait):
    send_sems, recv_sems, local_sems = sems
    x, y, c, me = _my_place()
    for a in range(len(ins)):
        local = pltpu.make_async_copy(ins[a].at[pl.ds(me, 1)], outs[a].at[pl.ds(me, 1)], local_sems.at[a])
        local.wait() if wait else local.start()
        for k in range(1, N_DEV):
            to, to_i = _peer(x, y, c, k)
            cp = pltpu.make_async_remote_copy(
                src_ref=ins[a].at[pl.ds(to_i, 1)], dst_ref=outs[a].at[pl.ds(me, 1)],
                send_sem=send_sems.at[a, k - 1], recv_sem=recv_sems.at[a, k - 1], device_id=to, device_id_type=MESH)
            cp.wait() if wait else cp.start()


def _exchange_comm(grads):
    return _Comm(list(grads), [_sds(g.shape, g.dtype) for g in grads], _gather_sems(len(grads)),
                 lambda i, o, s: _direct_copies(i, o, s, False), lambda i, o, s: _direct_copies(i, o, s, True))


def _chip_copies(ins, outs, sems, wait):
    send_sems, recv_sems, local_sems = sems
    x, y, c, _ = _my_place()
    for a in range(len(ins)):
        local = pltpu.make_async_copy(ins[a].at[pl.ds(2 * x + y, 1)], outs[a].at[pl.ds(len(CHIPS), 1)],
                                      local_sems.at[a])
        local.wait() if wait else local.start()
        for i, k in enumerate(CHIPS):
            to = _peer(x, y, c, k)[0]
            cp = pltpu.make_async_remote_copy(
                src_ref=ins[a].at[pl.ds(2 * to[0] + to[1], 1)], dst_ref=outs[a].at[pl.ds(i, 1)],
                send_sem=send_sems.at[a, i], recv_sem=recv_sems.at[a, i], device_id=to, device_id_type=MESH)
            cp.wait() if wait else cp.start()


def _chip_exchange_comm(sums):
    n = len(sums)
    sems = [pltpu.SemaphoreType.DMA((n, len(CHIPS))), pltpu.SemaphoreType.DMA((n, len(CHIPS))),
            pltpu.SemaphoreType.DMA((n,))]
    return _Comm(list(sums), [_sds(s.shape, s.dtype) for s in sums], sems,
                 lambda i, o, s: _chip_copies(i, o, s, False), lambda i, o, s: _chip_copies(i, o, s, True))


def _pair_reduce(name, grads, after=()):
    n = len(grads)
    n_chips = N_DEV // 2

    def body(*refs):
        g_refs, gh_refs, refs = refs[:n], refs[n:2 * n], refs[2 * n + len(after):]
        p_refs, land = refs[:n], refs[n:2 * n]
        send_sems, recv_sems = refs[2 * n:]
        x, y, c, _ = _my_place()
        sib = _peer(x, y, c, 1)[0]
        q = pl.program_id(0)

        def to_sibling(a, j):
            return pltpu.make_async_remote_copy(
                src_ref=gh_refs[a].at[j, pl.ds(1 - c, 1)], dst_ref=land[a].at[pl.ds(j, 1)],
                send_sem=send_sems.at[a, j], recv_sem=recv_sems.at[a, j], device_id=sib, device_id_type=MESH)

        @pl.when(q == 0)
        def _():
            for j in range(n_chips):
                for a in range(n):
                    to_sibling(a, j).start()

        for a in range(n):
            to_sibling(a, q).wait_recv()
            p_refs[a][...] = (g_refs[a][0, pl.ds(c, 1)].astype(F32) + land[a][pl.ds(q, 1)].astype(F32)).astype(BF16)

        @pl.when(q == n_chips - 1)
        def _():
            for a in range(n):
                for j in range(n_chips):
                    to_sibling(a, j).wait_send()

    views = [g.reshape((n_chips, 2) + g.shape[1:]) for g in grads]
    res = pl.pallas_call(
        body, name=name, grid=(n_chips,),
        in_specs=[pl.BlockSpec((1, 2) + g.shape[1:], lambda q: (q, 0, 0, 0)) for g in grads]
        + [ANY] * (n + len(after)),
        out_specs=[pl.BlockSpec((1,) + g.shape[1:], lambda q: (q, 0, 0)) for g in grads],
        out_shape=[_sds((n_chips,) + g.shape[1:], BF16) for g in grads],
        scratch_shapes=[pltpu.VMEM((n_chips,) + g.shape[1:], BF16) for g in grads]
        + [pltpu.SemaphoreType.DMA((n, n_chips)), pltpu.SemaphoreType.DMA((n, n_chips))],
        compiler_params=pltpu.CompilerParams(dimension_semantics=("arbitrary",), vmem_limit_bytes=VMEM_LIMIT),
    )(*views, *views, *after)
    return list(res)


def _pair_exchange_comm(grads):
    n, n_chips = len(grads), N_DEV // 2

    def copies(ins, outs, sems, wait):
        x, y, c, _ = _my_place()
        for j in range(n_chips):
            for a in range(n):
                cp = pltpu.make_async_remote_copy(
                    src_ref=ins[a].at[j, pl.ds(1 - c, 1)], dst_ref=outs[a].at[pl.ds(j, 1)], send_sem=sems[0].at[a, j],
                    recv_sem=sems[1].at[a, j], device_id=_peer(x, y, c, 1)[0], device_id_type=MESH)
                cp.wait() if wait else cp.start()

    views = [g.reshape((n_chips, 2) + g.shape[1:]) for g in grads]
    sems = [pltpu.SemaphoreType.DMA((n, n_chips)), pltpu.SemaphoreType.DMA((n, n_chips))]
    return _Comm(views, [_sds((n_chips,) + g.shape[1:], g.dtype) for g in grads], sems,
                 lambda i, o, s: copies(i, o, s, False), lambda i, o, s: copies(i, o, s, True))


def _pair_add(name, grads, landed, after=()):
    n, n_chips = len(grads), N_DEV // 2

    def body(core_ref, *refs):
        g_refs, l_refs, p_refs = refs[:n], refs[n:2 * n], refs[2 * n + len(after):]
        for a in range(n):
            p_refs[a][...] = (g_refs[a][...].astype(F32) + l_refs[a][...].astype(F32)).astype(BF16)

    views = [g.reshape((n_chips, 2) + g.shape[1:]) for g in grads]
    blocks = [pl.BlockSpec((1,) + g.shape[1:], lambda q, core: (q, 0, 0)) for g in grads]
    mine = [pl.BlockSpec((1, None) + g.shape[1:], lambda q, core: (q, core[0], 0, 0)) for g in grads]
    return list(pl.pallas_call(
        body, name=name, out_shape=[_sds((n_chips,) + g.shape[1:], BF16) for g in grads],
        grid_spec=pltpu.PrefetchScalarGridSpec(num_scalar_prefetch=1, grid=(n_chips,),
                                               in_specs=mine + blocks + [ANY] * len(after), out_specs=blocks),
        compiler_params=pltpu.CompilerParams(dimension_semantics=("arbitrary",), vmem_limit_bytes=VMEM_LIMIT),
    )(lax.axis_index("c").reshape(1), *views, *landed, *after))


def _call(name, body, grid, in_specs, out_specs, out_shape, args, scratch=(), after=()):
    ni, na = len(in_specs), len(after)

    def ordered(*refs):
        body(*refs[:ni], *refs[ni + na:])

    return list(pl.pallas_call(
        ordered if after else body, name=name, grid=grid, in_specs=list(in_specs) + [ANY] * na,
        out_specs=list(out_specs), out_shape=list(out_shape), scratch_shapes=list(scratch),
        compiler_params=pltpu.CompilerParams(dimension_semantics=("arbitrary",) * len(grid),
                                             vmem_limit_bytes=VMEM_LIMIT))(*args, *after))


SIBLING_AND_NEIGHBOURS, OTHER_CHIPS, EVERYONE = (1, 4, 2), CHIPS, tuple(range(1, N_DEV))


def _by_sequencer(name, comm, peers, collective_id):
    src = [jax.new_ref(a, memory_space=pltpu.MemorySpace.HBM) for a in comm.args]
    dst = [jax.empty_ref(s, memory_space=pltpu.MemorySpace.HBM) for s in comm.out_shape]

    @pl.kernel(mesh=plsc.ScalarSubcoreMesh(axis_name="sequencer", num_cores=1), name=name,
               scratch_types=tuple(comm.scratch), compiler_params=pltpu.CompilerParams(collective_id=collective_id))
    def launch(*sems):
        x, y, c, _ = _my_place()
        barrier = pltpu.get_barrier_semaphore()
        for k in peers:
            pl.semaphore_signal(barrier, inc=1, device_id=_peer(x, y, c, k)[0], device_id_type=MESH)
        pl.semaphore_wait(barrier, len(peers))
        comm.start(src, dst, sems)
        if comm.relay is not None:
            comm.relay(src, dst, sems)
        comm.finish(src, dst, sems)

    launch()
    return [d[...] for d in dst]


def _gather_first(first, later, swapped):
    nf = len(first)
    layer_of = [(a, l) for a, s in enumerate(later) for l in range(s.shape[0])]
    nl = len(layer_of)
    dts = [BF16] * (nf - 2) + [F32, F32]
    shard = [s.shape[:0:-1] if sw else s.shape[1:] for s, sw in zip(later, swapped)]

    def body(*refs):
        ins, refs = refs[:nf + len(later)], refs[nf + len(later):]
        outs, refs = refs[:nf], refs[nf:]
        casts, refs = refs[:nl], refs[nl:]
        stage, sems = refs[:nf], refs[nf:]
        for a in range(nf):
            stage[a][...] = ins[a][...].astype(dts[a])
            _gather_start(stage, outs, sems, only=a)
        for k, (a, l) in enumerate(layer_of):
            block = ins[nf + a][l]
            casts[k][0] = (block.T if swapped[a] else block).astype(BF16)
        _gather_relay(stage, outs, sems)
        _gather_finish(stage, outs, sems)

    res = pl.pallas_call(
        body, name="gather_first",
        in_specs=[VMEM] * (nf + len(later)), out_specs=[ANY] * nf + [VMEM] * nl,
        out_shape=[_sds((N_DEV,) + s.shape[1:], dt) for s, dt in zip(first, dts)]
        + [_sds((1,) + shard[a], BF16) for a, _ in layer_of],
        scratch_shapes=[pltpu.VMEM(s.shape, dt) for s, dt in zip(first, dts)] + _relay_sems(nf),
        compiler_params=pltpu.CompilerParams(vmem_limit_bytes=VMEM_LIMIT),
    )(*first, *later)
    return list(res[:nf]), list(res[nf:])


def _a_mix_fwd(x, g, w_in, ln_g, ln_b, w_s, b_st, w_out):
    t = x.shape[0]
    nblk = TM // GMLP_BLOCK

    def body(x_ref, g_ref, win_ref, lng_ref, lnb_ref, ws_ref, bst_ref, wout_ref, h_ref, z_ref, gated_scr):
        xv = x_ref[...]
        hb = _rms_fwd(xv, g_ref[...])[0].astype(BF16)
        for d in range(N_DEV):
            z_ref[:, d * FF_SLOT:(d + 1) * FF_SLOT] = _dot(hb, win_ref[d])
        u = _gelu(z_ref[:, :GATE_DIM])
        vb = _ln_fwd(_gelu(z_ref[:, GATE_DIM:]), lng_ref[...], lnb_ref[...])[0].astype(BF16)
        mask = _gate_mask()
        for gi in range(A_GROUPS):
            wm = jnp.where(mask, ws_ref[gi], 0.0).astype(BF16)
            bias = bst_ref[:, gi:gi + 1]
            cs = slice(gi * A_GROUP_DIM, (gi + 1) * A_GROUP_DIM)
            for n in range(nblk):
                rs = slice(n * GMLP_BLOCK, (n + 1) * GMLP_BLOCK)
                sv = _dot(wm, vb[rs, cs]) + bias
                gated_scr[rs, cs] = (u[rs, cs] * sv).astype(BF16)
        h_ref[...] = xv + _dot(gated_scr[...], wout_ref[...])

    return _call(
        "a_mix_fwd", body, (t // TM,),
        [_row(D_MODEL), _res((1, D_MODEL)), _res((N_DEV, D_MODEL, FF_SLOT)), _res((1, GATE_DIM)),
         _res((1, GATE_DIM)), _res((A_GROUPS, GMLP_BLOCK, GMLP_BLOCK)), _res((GMLP_BLOCK, A_GROUPS)),
         _res((GATE_DIM, D_MODEL))],
        [_row(D_MODEL), _row(2 * GATE_DIM), _row(GATE_DIM)],
        [_sds((t, D_MODEL), F32), _sds((t, 2 * GATE_DIM), F32), _sds((t, GATE_DIM), BF16)],
        (x, g, w_in, ln_g, ln_b, w_s, b_st, w_out))


MLP_W_SPECS = (_res((N_DEV, D_MODEL, FF_SLOT)), _res((N_DEV, FF_SLOT, D_MODEL)))


def _mlp_fwd(h, g, w1, w2):
    t = h.shape[0]

    def body(h_ref, g_ref, w1_ref, w2_ref, o_ref, a_ref):
        hv = h_ref[...]
        hb = _rms_fwd(hv, g_ref[...])[0].astype(BF16)
        o_ref[...] = hv
        for d in range(N_DEV):
            a = _dot(hb, w1_ref[d])
            a_ref[:, d * FF_SLOT:(d + 1) * FF_SLOT] = a
            r = jnp.maximum(a, 0.0)
            o_ref[...] += _dot((r * r).astype(BF16), w2_ref[d])

    return _call(
        "mlp_fwd", body, (t // TM_MLP_FWD,), [_row(D_MODEL, TM_MLP_FWD), _res((1, D_MODEL)), *MLP_W_SPECS],
        [_row(D_MODEL, TM_MLP_FWD), _row(D_FF, TM_MLP_FWD)], [_sds((t, D_MODEL), F32), _sds((t, D_FF), F32)],
        (h, g, w1, w2))


def _mlp_fwd_loss(h, g, w1, w2, final_g, target):
    t = h.shape[0]

    def body(h_ref, g_ref, w1_ref, w2_ref, fg_ref, t_ref, a_ref, loss_ref, dh_ref, dg_ref):
        hv = h_ref[...]
        hb = _rms_fwd(hv, g_ref[...])[0].astype(BF16)
        out = hv
        for d in range(N_DEV):
            a = _dot(hb, w1_ref[d])
            a_ref[:, d * FF_SLOT:(d + 1) * FF_SLOT] = a
            r = jnp.maximum(a, 0.0)
            out = out + _dot((r * r).astype(BF16), w2_ref[d])
        y, xhat, rstd = _rms_fwd(out, fg_ref[...])
        err = y - t_ref[...]
        part = 0.5 * jnp.sum(jnp.mean(err * err, axis=-1, keepdims=True), axis=0, keepdims=True)
        dx, dg = _rms_bwd(err * (1.0 / D_MODEL), xhat, rstd, fg_ref[...])
        dh_ref[...] = dx
        _acc(dg_ref, dg)
        _acc(loss_ref, part)

    return _call(
        "mlp_fwd_loss", body, (t // TM,),
        [_row(D_MODEL), _res((1, D_MODEL)), *MLP_W_SPECS, _res((1, D_MODEL)), _row(D_MODEL)],
        [_row(D_FF), _const((1, 1)), _row(D_MODEL), _const((1, D_MODEL))],
        [_sds((t, D_FF), F32), _sds((1, 1), F32), _sds((t, D_MODEL), F32), _sds((1, D_MODEL), F32)],
        (h, g, w1, w2, final_g, target))


KVQ_W_SPECS = (_res((1, D_MODEL)), _res((D_MODEL, KV_LORA + QK_ROPE)), _res((1, KV_LORA)),
               _res((B_HEADS, KV_LORA, QK_NOPE + V_HEAD)), _res((1, D_MODEL)), _res((D_MODEL, Q_LORA)),
               _res((1, Q_LORA)), _res((B_HEADS, Q_LORA, QK_NOPE + QK_ROPE)))


def _kvq_fwd(h, pos, inv_freq, kvq_w):
    t = h.shape[0]
    half = QK_ROPE // 2

    def body(h_ref, pos_ref, invf_ref, srcg_ref, wkva_ref, kvag_ref, wkvb_ref, mixg_ref, wqa_ref, qg_ref, wqb_ref,
             ckv_ref, k_ref, v_ref, cqpre_ref, q_ref, cos_ref, sin_ref):
        hv = h_ref[...]
        xhat = hv * lax.rsqrt(jnp.mean(hv * hv, axis=-1, keepdims=True) + EPS)
        ang = pos_ref[...].astype(F32) * invf_ref[...]
        cos, sin = jnp.cos(ang), jnp.sin(ang)
        cos_ref[...] = cos
        sin_ref[...] = sin
        ckv = _dot((xhat * srcg_ref[...]).astype(BF16), wkva_ref[...])
        ckv_ref[...] = ckv
        cb = _rms_fwd(ckv[:, :KV_LORA], kvag_ref[...])[0].astype(BF16)
        kpe = _rope(ckv[:, KV_LORA:], cos, sin).astype(BF16)
        for hd in range(B_HEADS):
            kv = _dot(cb, wkvb_ref[hd])
            k_ref[hd, :, 0:QK_NOPE] = kv[:, :QK_NOPE].astype(BF16)
            k_ref[hd, :, QK_NOPE:] = kpe
            v_ref[hd] = kv[:, QK_NOPE:].astype(BF16)
        cqpre = _dot((xhat * mixg_ref[...]).astype(BF16), wqa_ref[...])
        cqpre_ref[...] = cqpre
        cqb = _rms_fwd(cqpre, qg_ref[...])[0].astype(BF16)
        for hd in range(B_HEADS):
            q = _dot(cqb, wqb_ref[hd])
            q_ref[hd, :, 0:QK_NOPE] = q[:, :QK_NOPE].astype(BF16)
            q_ref[hd, :, QK_NOPE:] = _rope(q[:, QK_NOPE:], cos, sin).astype(BF16)

    tm = TM_KVQ
    return _call(
        "kvq_fwd", body, (t // tm,), [_row(D_MODEL, tm), _row(1, tm), _res((1, half)), *KVQ_W_SPECS],
        [_row(KV_LORA + QK_ROPE, tm), _heads(QK_NOPE + QK_ROPE, tm), _heads(V_HEAD, tm), _row(Q_LORA, tm),
         _heads(QK_NOPE + QK_ROPE, tm), _row(half, tm), _row(half, tm)],
        [_sds((t, KV_LORA + QK_ROPE), F32), _sds((B_HEADS, t, QK_NOPE + QK_ROPE), BF16),
         _sds((B_HEADS, t, V_HEAD), BF16), _sds((t, Q_LORA), F32), _sds((B_HEADS, t, QK_NOPE + QK_ROPE), BF16),
         _sds((t, half), F32), _sds((t, half), F32)],
        (h, pos, inv_freq, *kvq_w))


def _softmax_rows(q, k_ref, k):
    past, upto = k * TM, (k + 1) * TM
    s = _dot_nt(q, k_ref[0:upto, :])
    own = jnp.where(_att_mask(0, TM, TM), s[:, past:], jnp.finfo(F32).min)
    s = own if k == 0 else jnp.concatenate([s[:, :past], own], axis=1)
    e = jnp.exp2((s - jnp.max(s, axis=-1, keepdims=True)) * (ATT_SCALE * LOG2_E))
    return e * (1.0 / jnp.sum(e, axis=-1, keepdims=True))


def _for_my_tile(i, nq, fn):
    for k in range(nq):
        @pl.when(i == k)
        def _(k=k):
            fn(k)


def _attn_fwd(h, q, k, v, w_o):
    t = h.shape[0]
    nq, hps = t // TM, HEADS_PER_STEP

    def body(h_ref, q_ref, k_ref, v_ref, wo_ref, o_ref, att_ref):
        i, pair = pl.program_id(0), pl.program_id(1)

        @pl.when(pair == 0)
        def _():
            o_ref[...] = h_ref[...]

        def tile(kt):
            proj = None
            for j in range(hps):
                hd = pair * hps + j
                p = _softmax_rows(q_ref[j], k_ref.at[hd], kt)
                ob = _dot(p.astype(BF16), v_ref[hd, 0:(kt + 1) * TM, :]).astype(BF16)
                att_ref[j] = ob
                proj = _dot(ob, wo_ref[hd]) if proj is None else proj + _dot(ob, wo_ref[hd])
            o_ref[...] += proj

        _for_my_tile(i, nq, tile)

    def per_head(d):
        return pl.BlockSpec((hps, TM, d), lambda i, pair: (pair, i, 0))

    def resident(shape):
        zeros = (0,) * len(shape)
        return pl.BlockSpec(shape, lambda i, pair: zeros, pipeline_mode=pl.Buffered(1))

    tile_spec = pl.BlockSpec((TM, D_MODEL), lambda i, pair: (i, 0))
    return _call(
        "attn_fwd", body, (nq, B_HEADS // hps),
        [tile_spec, per_head(QK_NOPE + QK_ROPE), resident((B_HEADS, t, QK_NOPE + QK_ROPE)),
         resident((B_HEADS, t, V_HEAD)), resident((B_HEADS, V_HEAD, D_MODEL))],
        [tile_spec, per_head(V_HEAD)], [_sds((t, D_MODEL), F32), _sds((B_HEADS, t, V_HEAD), BF16)],
        (h, q, k, v, w_o))


def _mlp_bwd(h, a, dho, g, w1, w2, layer, after=()):
    t = h.shape[0]

    def body(h_ref, a_ref, dho_ref, g_ref, w1_ref, w2_ref, dhi_ref, dg_ref, hn_ref, f_ref, da_ref, dhib_ref):
        gv = g_ref[...]
        y, xhat, rstd = _rms_fwd(h_ref[...], gv)
        hn_ref[...] = y.astype(BF16)
        dho_v = dho_ref[...]
        dhob = dho_v.astype(BF16)
        dhn = jnp.zeros((TM, D_MODEL), F32)
        for d in range(N_DEV):
            cs = slice(d * FF_SLOT, (d + 1) * FF_SLOT)
            r = jnp.maximum(a_ref[:, cs], 0.0)
            f_ref[:, cs] = (r * r).astype(BF16)
            da = (_dot_nt(dhob, w2_ref[d]) * (2.0 * r)).astype(BF16)
            da_ref[:, cs] = da
            dhn = dhn + _dot_nt(da, w1_ref[d])
        dx, dg = _rms_bwd(dhn, xhat, rstd, gv)
        dhi = dho_v + dx
        dhi_ref[...] = dhi
        dhib_ref[...] = dhi.astype(BF16)
        _acc(dg_ref, dg)

    return _call(
        f"mlp_bwd_{layer}", body, (t // TM,),
        [_row(D_MODEL), _row(D_FF), _row(D_MODEL), _res((1, D_MODEL)), *MLP_W_SPECS],
        [_row(D_MODEL), _const((1, D_MODEL)), _row(D_MODEL), _row(D_FF), _row(D_FF), _row(D_MODEL)],
        [_sds((t, D_MODEL), F32), _sds((1, D_MODEL), F32), _sds((t, D_MODEL), BF16), _sds((t, D_FF), BF16),
         _sds((t, D_FF), BF16), _sds((t, D_MODEL), BF16)],
        (h, a, dho, g, w1, w2), after=after)


def _attn_bwd(dh, q, k, v, w_o, cos, sin, after=()):
    t = dh.shape[0]
    half, hps = QK_ROPE // 2, HEADS_PER_STEP

    def body(dh_ref, q_ref, k_ref, v_ref, wo_ref, cos_ref, sin_ref, dq_ref, dk_ref, dv_ref):
        i = pl.program_id(1)

        @pl.when(i == 0)
        def _():
            dk_ref[...] = jnp.zeros_like(dk_ref)
            dv_ref[...] = jnp.zeros_like(dv_ref)

        def tile(kt):
            keys = slice(0, (kt + 1) * TM)
            for j in range(hps):
                qj = q_ref[j]
                do = _dot_nt(dh_ref[kt * TM:(kt + 1) * TM, :], wo_ref[j]).astype(BF16)
                p = _softmax_rows(qj, k_ref.at[j], kt)
                dp = _dot_nt(do, v_ref[j, keys, :])
                ds = (p * (dp - jnp.sum(p * dp, axis=-1, keepdims=True)) * ATT_SCALE).astype(BF16)
                dq = _dot(ds, k_ref[j, keys, :])
                dq_ref[j, :, 0:QK_NOPE] = dq[:, :QK_NOPE].astype(BF16)
                dq_ref[j, :, QK_NOPE:] = _rope(dq[:, QK_NOPE:], cos_ref[...], -sin_ref[...]).astype(BF16)
                dk_ref[j, keys, :] += _dot_tn(ds, qj)
                dv_ref[j, keys, :] += _dot_tn(p.astype(BF16), do)

        _for_my_tile(i, t // TM, tile)

    def per_pair(rows, d, tiled):
        return pl.BlockSpec((hps, rows, d), (lambda pair, i: (pair, i, 0)) if tiled else (lambda pair, i: (pair, 0, 0)))

    def tile(d):
        return pl.BlockSpec((TM, d), lambda pair, i: (i, 0))

    return _call(
        "attn_bwd", body, (B_HEADS // hps, t // TM),
        [pl.BlockSpec((t, D_MODEL), lambda pair, i: (0, 0), pipeline_mode=pl.Buffered(1)),
         per_pair(TM, QK_NOPE + QK_ROPE, True), per_pair(t, QK_NOPE + QK_ROPE, False), per_pair(t, V_HEAD, False),
         per_pair(V_HEAD, D_MODEL, False), tile(half), tile(half)],
        [per_pair(TM, QK_NOPE + QK_ROPE, True), per_pair(t, QK_NOPE + QK_ROPE, False), per_pair(t, V_HEAD, False)],
        [_sds((B_HEADS, t, QK_NOPE + QK_ROPE), BF16), _sds((B_HEADS, t, QK_NOPE + QK_ROPE), F32),
         _sds((B_HEADS, t, V_HEAD), F32)],
        (dh, q, k, v, w_o, cos, sin), after=after)


def _kvq_bwd(h, dh, ckv, cqpre, dq, dk, dv, cos, sin, kvq_w, after=()):
    t = h.shape[0]
    tm = TM
    half, last = QK_ROPE // 2, t // tm - 1
    grad_shapes = [(D_MODEL, Q_LORA), (B_HEADS, Q_LORA, QK_NOPE + QK_ROPE), (D_MODEL, KV_LORA + QK_ROPE),
                   (B_HEADS, KV_LORA, QK_NOPE + V_HEAD)]

    def body(h_ref, dh_ref, ckv_ref, cqpre_ref, dq_ref, dk_ref, dv_ref, cos_ref, sin_ref,
             srcg_ref, wkva_ref, kvag_ref, wkvb_ref, mixg_ref, wqa_ref, qg_ref, wqb_ref,
             dhi_ref, dmixg_ref, dsrcg_ref, dqg_ref, dkvag_ref, gqa_ref, gqb_ref, gkva_ref, gkvb_ref,
             aqa, aqb, akva, akvb):
        @pl.when(pl.program_id(0) == 0)
        def _():
            for acc in (aqa, aqb, akva, akvb):
                acc[...] = jnp.zeros_like(acc)

        hv = h_ref[...]
        rstd = lax.rsqrt(jnp.mean(hv * hv, axis=-1, keepdims=True) + EPS)
        xhat = hv * rstd
        mixg, srcg, qg, kvag = mixg_ref[...], srcg_ref[...], qg_ref[...], kvag_ref[...]
        cq, cqhat, crstd = _rms_fwd(cqpre_ref[...], qg)
        cqb = cq.astype(BF16)
        dcq = jnp.zeros((tm, Q_LORA), F32)
        for hd in range(B_HEADS):
            dcq = dcq + _dot_nt(dq_ref[hd], wqb_ref[hd])
            aqb[hd] += _dot_tn(cqb, dq_ref[hd])
        dcqpre, dqg = _rms_bwd(dcq, cqhat, crstd, qg)
        dcqpre_b = dcqpre.astype(BF16)
        aqa[...] += _dot_tn((xhat * mixg).astype(BF16), dcqpre_b)
        dxq, dmixg = _rms_bwd(_dot_nt(dcqpre_b, wqa_ref[...]), xhat, rstd, mixg)
        ckv = ckv_ref[...]
        c, chat, krstd = _rms_fwd(ckv[:, :KV_LORA], kvag)
        cb = c.astype(BF16)
        dc = jnp.zeros((tm, KV_LORA), F32)
        dkpe = jnp.zeros((tm, QK_ROPE), F32)
        for hd in range(B_HEADS):
            dkv = jnp.concatenate([dk_ref[hd, :, 0:QK_NOPE], dv_ref[hd]], axis=-1).astype(BF16)
            akvb[hd] += _dot_tn(cb, dkv)
            dc = dc + _dot_nt(dkv, wkvb_ref[hd])
            dkpe = dkpe + dk_ref[hd, :, QK_NOPE:]
        dlat, dkvag = _rms_bwd(dc, chat, krstd, kvag)
        dpe = _rope(dkpe, cos_ref[...], -sin_ref[...])
        dckv_b = jnp.concatenate([dlat, dpe], axis=-1).astype(BF16)
        akva[...] += _dot_tn((xhat * srcg).astype(BF16), dckv_b)
        dxk, dsrcg = _rms_bwd(_dot_nt(dckv_b, wkva_ref[...]), xhat, rstd, srcg)
        dhi_ref[...] = dh_ref[...] + dxq + dxk
        _acc(dmixg_ref, dmixg)
        _acc(dsrcg_ref, dsrcg)
        _acc(dqg_ref, dqg)
        _acc(dkvag_ref, dkvag)

        @pl.when(pl.program_id(0) == last)
        def _():
            for out, acc in ((gqa_ref, aqa), (gqb_ref, aqb), (gkva_ref, akva), (gkvb_ref, akvb)):
                out[...] = acc[...].astype(BF16)

    return _call(
        "kvq_bwd", body, (t // tm,),
        [_row(D_MODEL, tm), _row(D_MODEL, tm), _row(KV_LORA + QK_ROPE, tm), _row(Q_LORA, tm),
         _heads(QK_NOPE + QK_ROPE, tm), _heads(QK_NOPE + QK_ROPE, tm), _heads(V_HEAD, tm), _row(half, tm),
         _row(half, tm), *KVQ_W_SPECS],
        [_row(D_MODEL, tm), _const((1, D_MODEL)), _const((1, D_MODEL)), _const((1, Q_LORA)), _const((1, KV_LORA))]
        + [_const(s) for s in grad_shapes],
        [_sds((t, D_MODEL), F32), _sds((1, D_MODEL), F32), _sds((1, D_MODEL), F32), _sds((1, Q_LORA), F32),
         _sds((1, KV_LORA), F32)] + [_sds(s, BF16) for s in grad_shapes],
        (h, dh, ckv, cqpre, dq, dk, dv, cos, sin, *kvq_w), scratch=[pltpu.VMEM(s, F32) for s in grad_shapes],
        after=after)


def _a_mix_bwd(x, z, dh, g, w_in, ln_g, ln_b, w_s, b_st, w_out, after=()):
    t = x.shape[0]
    tm = TM_GATE
    nblk = tm // GMLP_BLOCK

    def body(x_ref, z_ref, dh_ref, g_ref, win_ref, lng_ref, lnb_ref, ws_ref, bst_ref, wout_ref,
             dx_ref, hn_ref, dz_ref, dg_ref, dlng_ref, dlnb_ref, dws_ref, dbs_ref, dvn_scr, gelu_grad_v):
        @pl.when(pl.program_id(0) == 0)
        def _():
            dws_ref[...] = jnp.zeros_like(dws_ref)
            dbs_ref[...] = jnp.zeros_like(dbs_ref)

        gv, lng = g_ref[...], lng_ref[...]
        y, xhat, rstd = _rms_fwd(x_ref[...], gv)
        hn_ref[...] = y.astype(BF16)
        dhv = dh_ref[...]
        dgated = _dot_nt(dhv.astype(BF16), wout_ref[...])
        u, gelu_grad_u = _gelu_and_grad(z_ref[:, :GATE_DIM])
        v, gelu_grad_v[...] = _gelu_and_grad(z_ref[:, GATE_DIM:])
        vn, vhat, lrstd = _ln_fwd(v, lng, lnb_ref[...])
        vb = vn.astype(BF16)
        mask = _gate_mask()
        for gi in range(A_GROUPS):
            wm = jnp.where(mask, ws_ref[gi], 0.0).astype(BF16)
            bias = bst_ref[:, gi:gi + 1]
            cs = slice(gi * A_GROUP_DIM, (gi + 1) * A_GROUP_DIM)
            dws = jnp.zeros((GMLP_BLOCK, GMLP_BLOCK), F32)
            dbs = jnp.zeros((GMLP_BLOCK, 1), F32)
            for n in range(nblk):
                rs = slice(n * GMLP_BLOCK, (n + 1) * GMLP_BLOCK)
                sv = _dot(wm, vb[rs, cs]) + bias
                dz_ref[rs, cs] = (dgated[rs, cs] * sv * gelu_grad_u[rs, cs]).astype(BF16)
                dsv = dgated[rs, cs] * u[rs, cs]
                dsvb = dsv.astype(BF16)
                dws = dws + _dot_nt(dsvb, vb[rs, cs])
                dbs = dbs + jnp.sum(dsv, axis=-1, keepdims=True)
                dvn_scr[rs, cs] = _dot_tn(wm, dsvb)
            dws_ref[gi] += jnp.where(mask, dws, 0.0)
            dbs_ref[gi] += dbs
        dvn = dvn_scr[...]
        dvhat = dvn * lng
        dv = lrstd * (dvhat - jnp.mean(dvhat, axis=-1, keepdims=True)
                      - vhat * jnp.mean(dvhat * vhat, axis=-1, keepdims=True))
        dz_ref[:, GATE_DIM:] = (dv * gelu_grad_v[...]).astype(BF16)
        dhn = jnp.zeros((tm, D_MODEL), F32)
        for d in range(N_DEV):
            dhn = dhn + _dot_nt(dz_ref[:, d * FF_SLOT:(d + 1) * FF_SLOT], win_ref[d])
        dx, dg = _rms_bwd(dhn, xhat, rstd, gv)
        dx_ref[...] = dhv + dx
        _acc(dg_ref, dg)
        _acc(dlng_ref, jnp.sum(dvn * vhat, axis=0, keepdims=True))
        _acc(dlnb_ref, jnp.sum(dvn, axis=0, keepdims=True))

    return _call(
        "a_mix_bwd", body, (t // tm,),
        [_row(D_MODEL, tm), _row(2 * GATE_DIM, tm), _row(D_MODEL, tm), _res((1, D_MODEL)),
         _res((N_DEV, D_MODEL, FF_SLOT)), _res((1, GATE_DIM)), _res((1, GATE_DIM)),
         _res((A_GROUPS, GMLP_BLOCK, GMLP_BLOCK)), _res((GMLP_BLOCK, A_GROUPS)), _res((GATE_DIM, D_MODEL))],
        [_row(D_MODEL, tm), _row(D_MODEL, tm), _row(2 * GATE_DIM, tm),
         _const((1, D_MODEL)), _const((1, GATE_DIM)), _const((1, GATE_DIM)),
         _const((A_GROUPS, GMLP_BLOCK, GMLP_BLOCK)), _const((A_GROUPS, GMLP_BLOCK, 1))],
        [_sds((t, D_MODEL), F32), _sds((t, D_MODEL), BF16),
         _sds((t, 2 * GATE_DIM), BF16), _sds((1, D_MODEL), F32), _sds((1, GATE_DIM), F32),
         _sds((1, GATE_DIM), F32), _sds((A_GROUPS, GMLP_BLOCK, GMLP_BLOCK), F32),
         _sds((A_GROUPS, GMLP_BLOCK, 1), F32)],
        (x, z, dh, g, w_in, ln_g, ln_b, w_s, b_st, w_out),
        scratch=[pltpu.VMEM((tm, GATE_DIM), F32), pltpu.VMEM((tm, GATE_DIM), F32)], after=after)


def _wgrad(name, a, b, a_spec, b_spec, m, n, after=()):
    def body(a_ref, b_ref, o_ref):
        o_ref[0] = _dot_tn(a_ref[...].astype(BF16), b_ref[...].astype(BF16)).astype(BF16)

    return _call(name, body, (N_DEV,), [a_spec, b_spec], [pl.BlockSpec((1, m, n), lambda d: (d, 0, 0))],
                 [_sds((N_DEV, m, n), BF16)], (a, b), after=after)[0]


def _full(t, d):
    return pl.BlockSpec((t, d), lambda i: (0, 0), pipeline_mode=pl.Buffered(1))


def _cols(t, d):
    return pl.BlockSpec((t, d), lambda i: (0, i))


def _head(t, d):
    return pl.BlockSpec((None, t, d), lambda i: (i, 0, 0))


def _local_step(x, pos, target, inv_freq, wg, sm, shards=None):
    t = x.shape[0]
    wg = dict(wg)
    dist = shards is not None
    mix_g = [sm["norm_mix_g"][l:l + 1] for l in range(2)]
    mlp_g = [sm["norm_mlp_g"][l:l + 1] for l in range(2)]

    ids = iter(range(2, 2 + 9))

    def gather(names):
        if dist:
            got = _by_sequencer("gather_" + names[0], _gather_comm([shards[k] for k in names]),
                                SIBLING_AND_NEIGHBOURS, next(ids))
            wg.update(zip(names, got))

    def send(name, names):
        if dist:
            comm = _exchange_comm(grads=[g[k] for k in names])
            g.update(zip(names, _by_sequencer("exchange_" + name, comm, EVERYONE, next(ids))))

    def send_sums(name, names, meanwhile):
        if not dist:
            meanwhile()
            return ()
        grads = [g[k] for k in names]
        landed = _by_sequencer("pair_exchange_" + name, _pair_exchange_comm(grads), (1,), next(ids))
        sums = _pair_add("pair_add_" + name, grads, landed, after=meanwhile())
        g.update(zip(names, _by_sequencer("exchange_" + name, _chip_exchange_comm(sums), OTHER_CHIPS, next(ids))))
        return sums

    def a_args():
        return (wg["a_w_in"], wg["a_ln_v_g"], wg["a_ln_v_b"], sm["a_w_s"], sm["a_b_st"], wg["a_w_out"])

    def kvq_w():
        return (sm["kv_src_norm_g"], wg["kv_w_a"], sm["kv_a_norm_g"], wg["kv_w_b"], mix_g[1], wg["b_w_q_a"],
                sm["b_q_norm_g"], wg["b_w_q_b"])

    gather(("mlp_w1_0", "mlp_w2_0"))
    h1, z, gated = _a_mix_fwd(x, mix_g[0], *a_args())
    gather(("kv_w_a", "kv_w_b", "b_w_q_a", "b_w_q_b", "b_w_o"))
    h2, a0 = _mlp_fwd(h1, mlp_g[0], wg["mlp_w1_0"], wg["mlp_w2_0"])
    if dist:
        wg["b_w_q_a"] = wg["b_w_q_a"].reshape(D_MODEL, Q_LORA)
        wg["kv_w_a"] = wg["kv_w_a"].reshape(D_MODEL, KV_LORA + QK_ROPE)
    gather(("mlp_w1_1", "mlp_w2_1"))
    ckv, k, v, cqpre, q, cos, sin = _kvq_fwd(h2, pos, inv_freq, kvq_w())
    h3, att = _attn_fwd(h2, q, k, v, wg["b_w_o"])
    a1, loss, dh4, d_final_g = _mlp_fwd_loss(h3, mlp_g[1], wg["mlp_w1_1"], wg["mlp_w2_1"], sm["final_norm_g"], target)

    g = {}
    dh3, d_mlp_g1, hn, f, da, dh3_b = _mlp_bwd(h3, a1, dh4, mlp_g[1], wg["mlp_w1_1"], wg["mlp_w2_1"], 1)
    dq, dk, dv = _attn_bwd(dh3_b, q, k, v, wg["b_w_o"], cos, sin)
    g["mlp_w1_1"] = _wgrad("wgrad_w1_1", hn, da, _full(t, D_MODEL), _cols(t, FF_SLOT), D_MODEL, FF_SLOT, after=[dq])
    g["mlp_w2_1"] = _wgrad("wgrad_w2_1", f, dh4, _cols(t, FF_SLOT), _full(t, D_MODEL), FF_SLOT, D_MODEL)

    def wgrad_w_o():
        g["b_w_o"] = _wgrad("wgrad_w_o", att, dh3_b, _head(t, V_HEAD), _full(t, D_MODEL), V_HEAD, D_MODEL)
        return [g["b_w_o"]]

    sums = send_sums("mlp_1", ("mlp_w1_1", "mlp_w2_1"), wgrad_w_o)
    dh2, d_mix_g1, d_src_g, d_q_g, d_kv_a_g, g_q_a, g["b_w_q_b"], g_kv_a, g["kv_w_b"] = _kvq_bwd(
        h2, dh3, ckv, cqpre, dq, dk, dv, cos, sin, kvq_w(), after=sums)
    g["b_w_q_a"] = g_q_a.reshape(N_DEV, D_MODEL // N_DEV, Q_LORA)
    g["kv_w_a"] = g_kv_a.reshape(N_DEV, D_MODEL // N_DEV, KV_LORA + QK_ROPE)
    qkv = ("b_w_q_a", "b_w_q_b", "kv_w_a", "kv_w_b")
    landed = [g[k] for k in qkv]
    send("qkv", qkv)
    dh1, d_mlp_g0, hn, f, da, dh1_b = _mlp_bwd(h1, a0, dh2, mlp_g[0], wg["mlp_w1_0"], wg["mlp_w2_0"], 0,
                                               after=landed if dist else ())
    landed = [g["mlp_w1_1"], g["mlp_w2_1"]] if dist else ()
    g["mlp_w1_0"] = _wgrad("wgrad_w1_0", hn, da, _full(t, D_MODEL), _cols(t, FF_SLOT), D_MODEL, FF_SLOT, after=landed)
    g["mlp_w2_0"] = _wgrad("wgrad_w2_0", f, dh2, _cols(t, FF_SLOT), _full(t, D_MODEL), FF_SLOT, D_MODEL)

    def wgrad_a_w_out():
        g["a_w_out"] = _wgrad("wgrad_a_w_out", gated, dh1_b, _cols(t, GATE_DIM // N_DEV), _full(t, D_MODEL),
                              GATE_DIM // N_DEV, D_MODEL)
        return [g["a_w_out"]] + [g[k] for k in qkv]

    sums = send_sums("mlp_0", ("mlp_w1_0", "mlp_w2_0", "b_w_o"), wgrad_a_w_out)
    dx, hn, dz, d_mix_g0, d_ln_g, d_ln_b, d_ws, d_bs = _a_mix_bwd(x, z, dh1, mix_g[0], *a_args(), after=sums)
    small = {
        "norm_mix_g": jnp.concatenate([d_mix_g0, d_mix_g1], axis=0),
        "norm_mlp_g": jnp.concatenate([d_mlp_g0, d_mlp_g1], axis=0),
        "a_ln_v_g": d_ln_g.reshape(N_DEV, GATE_DIM // N_DEV),
        "a_ln_v_b": d_ln_b.reshape(N_DEV, GATE_DIM // N_DEV),
        "a_w_s": d_ws.astype(BF16) if dist else d_ws,
        "a_b_s": d_bs.reshape(A_GROUPS, GMLP_BLOCK),
        "b_q_norm_g": d_q_g,
        "kv_src_norm_g": d_src_g,
        "kv_a_norm_g": d_kv_a_g,
        "final_norm_g": d_final_g,
    }
    if dist:
        parts = [small[k].reshape((1,) + small[k].shape) for k in SMALL] + [loss.reshape(1, 1, 1)]
        got = _by_sequencer("gather_small", _gather_comm(parts), SIBLING_AND_NEIGHBOURS, next(ids))
        small, loss = dict(zip(SMALL, got)), got[-1]
    g["a_w_in"] = _wgrad("wgrad_a_w_in", hn, dz, _full(t, D_MODEL), _cols(t, FF_SLOT), D_MODEL, FF_SLOT)
    return loss, dx, g, small


def _adamw(w, g, m, v):
    m = ADAM_B1 * m + (1.0 - ADAM_B1) * g
    v = ADAM_B2 * v + (1.0 - ADAM_B2) * (g * g)
    m_hat = m / (1.0 - ADAM_B1 ** ADAM_STEP)
    v_hat = v / (1.0 - ADAM_B2 ** ADAM_STEP)
    return -ADAM_LR * (m_hat / (jnp.sqrt(v_hat) + ADAM_EPS) + ADAM_WD * w), m, v


def _sum_in_device_order(r_ref):
    g = r_ref[0].astype(F32)
    for j in range(1, r_ref.shape[0]):
        g = g + r_ref[j].astype(F32)
    return g


def _adamw_sharded(name, recvs, w, m, v, swapped=False):
    layers, r, c = w.shape[0], *recvs[0][0].shape[1:]
    tr = r if swapped else math.gcd(r, 512)
    flat = [a for per_layer in recvs for a in per_layer]

    def body(*refs):
        r_refs, (w_ref, m_ref, v_ref) = refs[:len(flat)], refs[len(flat):len(flat) + 3]
        g_ref, d_ref, nm_ref, nv_ref = refs[-4:]
        layer = pl.program_id(0)
        g, pos = None, 0
        for li, per_layer in enumerate(recvs):
            total = None
            for ref in r_refs[pos:pos + len(per_layer)]:
                part = _sum_in_device_order(ref)
                total = part if total is None else total + part
            pos += len(per_layer)
            g = total if g is None else jnp.where(layer == li, total, g)
        if swapped:
            g = g.T
        g_ref[...] = g
        d_ref[...], nm_ref[...], nv_ref[...] = _adamw(w_ref[...], g, m_ref[...], v_ref[...])

    blk = pl.BlockSpec((None, tr, c), lambda l, i: (l, i, 0))
    if swapped:
        blk = pl.BlockSpec((None, c, r), lambda l, i: (l, 0, 0))
    return _call(name, body, (layers, r // tr),
                 [pl.BlockSpec((a.shape[0], tr, c), lambda l, i: (0, i, 0)) for a in flat] + [blk] * 3,
                 [blk] * 4, [_sds(w.shape, F32)] * 4, (*flat, w, m, v))


def _adamw_small(recvs, ws, ms, vs, own_row, losses):
    n = len(recvs)

    def body(*refs):
        r_refs, w_refs, m_refs, v_refs = (refs[i * n:(i + 1) * n] for i in range(4))
        outs, scr = refs[4 * n + 1:8 * n + 2], refs[8 * n + 2:]
        outs[-1][...] = _sum_in_device_order(refs[4 * n])
        me = _my_place()[3]
        for a in range(n):
            g = _sum_in_device_order(r_refs[a])
            if own_row[a]:
                scr[0][...] = g
                g = scr[0][pl.ds(me, 1), :]
            g_ref, d_ref, nm_ref, nv_ref = outs[4 * a:4 * a + 4]
            g_ref[...] = g
            d_ref[...], nm_ref[...], nv_ref[...] = _adamw(w_refs[a][...], g, m_refs[a][...], v_refs[a][...])

    out_shape = []
    for w in ws:
        out_shape += [_sds(w.shape, F32)] * 4
    return pl.pallas_call(
        body, name="adamw_small", in_specs=[VMEM] * (4 * n + 1), out_specs=[VMEM] * (4 * n + 1),
        out_shape=out_shape + [_sds((1, 1), F32)], scratch_shapes=[pltpu.VMEM((N_DEV, GATE_DIM // N_DEV), F32)],
    )(*recvs, *ws, *ms, *vs, losses)


BIG = ("a_w_in", "a_w_out", "b_w_q_a", "b_w_q_b", "b_w_o", "kv_w_a", "kv_w_b", "mlp_w1", "mlp_w2")
SMALL = ("norm_mix_g", "norm_mlp_g", "a_ln_v_g", "a_ln_v_b", "a_w_s", "a_b_s", "b_q_norm_g", "kv_src_norm_g",
         "kv_a_norm_g", "final_norm_g")
WEIGHTS = ("norm_mix_g", "norm_mlp_g", "a_w_in", "a_ln_v_g", "a_ln_v_b", "a_w_s", "a_b_s", "a_w_out", "b_w_q_a",
           "b_q_norm_g", "b_w_q_b", "b_w_o", "kv_src_norm_g", "kv_w_a", "kv_a_norm_g", "kv_w_b", "mlp_w1", "mlp_w2",
           "final_norm_g")


def _two_d(name, a):
    if name in ("a_w_s", "a_b_s"):
        return a.reshape(a.shape[1:])
    return a.reshape(1, -1) if a.ndim == 1 else a


def _three_d(a):
    return a if a.ndim == 3 else a.reshape((1,) + a.shape)


SWAPPED = ("b_w_q_b", "kv_w_a")


def _swapped(a):
    return jnp.swapaxes(_three_d(a), 1, 2)


def kernel(x, positions, norm_mix_g, norm_mlp_g, a_w_in, a_ln_v_g, a_ln_v_b, a_w_s, a_b_s, a_w_out, b_w_q_a, b_q_norm_g, b_w_q_b, b_w_o, kv_src_norm_g, kv_w_a, kv_a_norm_g, kv_w_b, mlp_w1, mlp_w2, final_norm_g, loss_target, m_norm_mix_g, m_norm_mlp_g, m_a_w_in, m_a_ln_v_g, m_a_ln_v_b, m_a_w_s, m_a_b_s, m_a_w_out, m_b_w_q_a, m_b_q_norm_g, m_b_w_q_b, m_b_w_o, m_kv_src_norm_g, m_kv_w_a, m_kv_a_norm_g, m_kv_w_b, m_mlp_w1, m_mlp_w2, m_final_norm_g, v_norm_mix_g, v_norm_mlp_g, v_a_w_in, v_a_ln_v_g, v_a_ln_v_b, v_a_w_s, v_a_b_s, v_a_w_out, v_b_w_q_a, v_b_q_norm_g, v_b_w_q_b, v_b_w_o, v_kv_src_norm_g, v_kv_w_a, v_kv_a_norm_g, v_kv_w_b, v_mlp_w1, v_mlp_w2, v_final_norm_g):
    w = dict(norm_mix_g=norm_mix_g, norm_mlp_g=norm_mlp_g, a_w_in=a_w_in, a_ln_v_g=a_ln_v_g, a_ln_v_b=a_ln_v_b,
             a_w_s=a_w_s, a_b_s=a_b_s, a_w_out=a_w_out, b_w_q_a=b_w_q_a, b_q_norm_g=b_q_norm_g, b_w_q_b=b_w_q_b,
             b_w_o=b_w_o, kv_src_norm_g=kv_src_norm_g, kv_w_a=kv_w_a, kv_a_norm_g=kv_a_norm_g, kv_w_b=kv_w_b,
             mlp_w1=mlp_w1, mlp_w2=mlp_w2, final_norm_g=final_norm_g)
    m = dict(norm_mix_g=m_norm_mix_g, norm_mlp_g=m_norm_mlp_g, a_w_in=m_a_w_in, a_ln_v_g=m_a_ln_v_g,
             a_ln_v_b=m_a_ln_v_b, a_w_s=m_a_w_s, a_b_s=m_a_b_s, a_w_out=m_a_w_out, b_w_q_a=m_b_w_q_a,
             b_q_norm_g=m_b_q_norm_g, b_w_q_b=m_b_w_q_b, b_w_o=m_b_w_o, kv_src_norm_g=m_kv_src_norm_g,
             kv_w_a=m_kv_w_a, kv_a_norm_g=m_kv_a_norm_g, kv_w_b=m_kv_w_b, mlp_w1=m_mlp_w1, mlp_w2=m_mlp_w2,
             final_norm_g=m_final_norm_g)
    v = dict(norm_mix_g=v_norm_mix_g, norm_mlp_g=v_norm_mlp_g, a_w_in=v_a_w_in, a_ln_v_g=v_a_ln_v_g,
             a_ln_v_b=v_a_ln_v_b, a_w_s=v_a_w_s, a_b_s=v_a_b_s, a_w_out=v_a_w_out, b_w_q_a=v_b_w_q_a,
             b_q_norm_g=v_b_q_norm_g, b_w_q_b=v_b_w_q_b, b_w_o=v_b_w_o, kv_src_norm_g=v_kv_src_norm_g,
             kv_w_a=v_kv_w_a, kv_a_norm_g=v_kv_a_norm_g, kv_w_b=v_kv_w_b, mlp_w1=v_mlp_w1, mlp_w2=v_mlp_w2,
             final_norm_g=v_final_norm_g)
    t = x.shape[1]

    first = ("a_w_in", "a_w_out", "a_ln_v_g", "a_ln_v_b")
    later = ("mlp_w1", "mlp_w2", "kv_w_a", "kv_w_b", "b_w_q_a", "b_w_q_b", "b_w_o")
    later_blocks = ("mlp_w1_0", "mlp_w1_1", "mlp_w2_0", "mlp_w2_1") + later[2:]
    got, casts = _gather_first([_three_d(w[k]) if k in BIG else w[k] for k in first],
                               [_swapped(w[k]) if k in SWAPPED else _three_d(w[k]) for k in later],
                               [k in SWAPPED for k in later])
    wg = dict(zip(first, got))
    wg["a_w_out"] = wg["a_w_out"].reshape(GATE_DIM, D_MODEL)
    wg["a_ln_v_g"] = wg["a_ln_v_g"].reshape(1, GATE_DIM)
    wg["a_ln_v_b"] = wg["a_ln_v_b"].reshape(1, GATE_DIM)
    shards = dict(zip(later_blocks, casts))

    sm = {k: _two_d(k, w[k]) for k in SMALL if k not in ("a_ln_v_g", "a_ln_v_b")}
    sm["a_b_st"] = sm["a_b_s"].T
    inv_freq = (ROPE_THETA ** (-jnp.arange(0, QK_ROPE, 2, dtype=F32) / QK_ROPE)).reshape(1, QK_ROPE // 2)

    losses, dx, g, small = _local_step(x[0], positions.reshape(t, 1), loss_target[0], inv_freq, wg, sm, shards)

    names = ("a_w_in", "a_w_out")
    sums = _pair_reduce("pair_reduce_a", [g[k] for k in names], after=[g["mlp_w1_0"], g["mlp_w2_0"]])
    g.update(zip(names, _by_sequencer("exchange_last", _chip_exchange_comm(sums), OTHER_CHIPS, collective_id=1)))

    out = {}
    for k in BIG:
        recvs = [[g[k + "_0"]], [g[k + "_1"]]] if k.startswith("mlp") else [[g[k]]]
        view = _swapped if k in SWAPPED else _three_d
        res = _adamw_sharded("adamw_" + k, recvs, view(w[k]), view(m[k]), view(v[k]), swapped=k in SWAPPED)
        out[k] = [view(o).reshape(w[k].shape) for o in res]
    own_row = [k in ("a_ln_v_g", "a_ln_v_b") for k in SMALL]
    res = _adamw_small([small[k] for k in SMALL], [_two_d(k, w[k]) for k in SMALL], [_two_d(k, m[k]) for k in SMALL],
                       [_two_d(k, v[k]) for k in SMALL], own_row, losses)
    for i, k in enumerate(SMALL):
        out[k] = [o.reshape(w[k].shape) for o in res[4 * i:4 * i + 4]]

    return (res[-1].reshape(()), dx.reshape(x.shape), *[out[k][0] for k in WEIGHTS], *[out[k][1] for k in WEIGHTS],
            *[out[k][2] for k in WEIGHTS], *[out[k][3] for k in WEIGHTS])
```

```python
import math

import jax
import jax.numpy as jnp
from jax import lax
from jax.experimental import pallas as pl
from jax.experimental.pallas import tpu as pltpu
from jax.experimental.pallas import tpu_sc as plsc

F32, BF16 = jnp.float32, jnp.bfloat16
MESH = pl.DeviceIdType.MESH
ANY = pl.BlockSpec(memory_space=pl.ANY)
VMEM = pl.BlockSpec(memory_space=pltpu.VMEM)

N_DEV = 8
D_MODEL = 1024
CHUNK = 64
GMLP_BLOCK = 128
GATE_DIM = 2048
A_GROUPS = 8
A_GROUP_DIM = GATE_DIM // A_GROUPS
B_HEADS = 8
QK_NOPE, QK_ROPE, V_HEAD = 128, 64, 128
Q_LORA, KV_LORA = 384, 256
ROPE_THETA = 10000.0
D_FF = 4096
FF_SLOT = D_FF // N_DEV
EPS = 1e-6
ATT_SCALE = (QK_NOPE + QK_ROPE) ** -0.5

ADAM_LR, ADAM_B1, ADAM_B2, ADAM_EPS, ADAM_WD, ADAM_STEP = 0.001, 0.9, 0.999, 1e-08, 0.01, 10

TM = 256
TM_GATE = 256
TM_MLP_FWD = 512
TM_KVQ = 512
VMEM_LIMIT = 56 * 1024 * 1024
INV_SQRT2 = 1.0 / math.sqrt(2.0)
INV_SQRT_2PI = 1.0 / math.sqrt(2.0 * math.pi)
LOG2_E = 1.0 / math.log(2.0)
HEADS_PER_STEP = 2


def _dot(a, b):
    return jnp.dot(a, b, preferred_element_type=F32)


def _dot_nt(a, b):
    return lax.dot_general(a, b, (((1,), (1,)), ((), ())), preferred_element_type=F32)


def _dot_tn(a, b):
    return lax.dot_general(a, b, (((0,), (0,)), ((), ())), preferred_element_type=F32)


def _rms_fwd(x, g):
    rstd = lax.rsqrt(jnp.mean(x * x, axis=-1, keepdims=True) + EPS)
    xhat = x * rstd
    return xhat * g, xhat, rstd


def _rms_bwd(dy, xhat, rstd, g):
    dxhat = dy * g
    dx = rstd * (dxhat - xhat * jnp.mean(dxhat * xhat, axis=-1, keepdims=True))
    return dx, jnp.sum(dy * xhat, axis=0, keepdims=True)


def _ln_fwd(v, g, b):
    mu = jnp.mean(v, axis=-1, keepdims=True)
    vc = v - mu
    rstd = lax.rsqrt(jnp.mean(vc * vc, axis=-1, keepdims=True) + EPS)
    vhat = vc * rstd
    return vhat * g + b, vhat, rstd


def _gelu(x):
    return 0.5 * x * (1.0 + lax.erf(x * INV_SQRT2))


def _gelu_and_grad(x):
    cdf = 0.5 * (1.0 + lax.erf(x * INV_SQRT2))
    return x * cdf, cdf + x * jnp.exp(-0.5 * x * x) * INV_SQRT_2PI


def _rope(x, cos, sin):
    x1, x2 = x[:, :QK_ROPE // 2], x[:, QK_ROPE // 2:]
    return jnp.concatenate([x1 * cos - x2 * sin, x2 * cos + x1 * sin], axis=-1)


def _gate_mask():
    row = lax.broadcasted_iota(jnp.int32, (GMLP_BLOCK, GMLP_BLOCK), 0)
    col = lax.broadcasted_iota(jnp.int32, (GMLP_BLOCK, GMLP_BLOCK), 1)
    return (col < CHUNK) | (row >= CHUNK)


def _att_mask(q0, tq, t):
    q = q0 + lax.broadcasted_iota(jnp.int32, (tq, t), 0)
    k = lax.broadcasted_iota(jnp.int32, (tq, t), 1)
    return jnp.right_shift(k, 6) <= jnp.right_shift(q, 6)


def _res(shape, imap=None):
    zeros = (0,) * len(shape)
    return pl.BlockSpec(shape, imap or (lambda i: zeros), pipeline_mode=pl.Buffered(1))


def _const(shape):
    zeros = (0,) * len(shape)
    return pl.BlockSpec(shape, lambda i: zeros)


def _row(d, tm=TM):
    return pl.BlockSpec((tm, d), lambda i: (i, 0))


def _heads(d, tm=TM):
    return pl.BlockSpec((B_HEADS, tm, d), lambda i: (0, i, 0))


def _sds(shape, dt):
    return jax.ShapeDtypeStruct(shape, dt)


def _acc(ref, val):
    @pl.when(pl.program_id(0) == 0)
    def _():
        ref[...] = jnp.zeros_like(ref)
    ref[...] += val


def _my_place():
    x, y, c = lax.axis_index("x"), lax.axis_index("y"), lax.axis_index("c")
    return x, y, c, 4 * x + 2 * y + c


def _peer(x, y, c, k):
    px = 1 - x if k & 4 else x
    py = 1 - y if k & 2 else y
    pc = 1 - c if k & 1 else c
    return (px, py, pc), 4 * px + 2 * py + pc


CHIPS = (2, 4, 6)


def _splits(ref):
    return len(ref.shape) >= 3 and ref.shape[1] % 32 == 0


def _piece(ref, block, half=None):
    if half is None or not _splits(ref):
        return ref.at[pl.ds(block, 1)]
    rows = ref.shape[1] // 2
    return ref.at[pl.ds(block, 1), pl.ds(half * rows, rows)]


def _gather_copy(sems, a, k, piece, to, src=None):
    return pltpu.make_async_remote_copy(
        src_ref=piece if src is None else src, dst_ref=piece, send_sem=sems[0].at[a, k], recv_sem=sems[1].at[a, k],
        device_id=to, device_id_type=MESH)


def _gather_start(srcs, outs, sems, only=None):
    x, y, c, me = _my_place()
    for a in range(len(srcs)) if only is None else (only,):
        mine = _piece(outs[a], me)
        pltpu.make_async_copy(srcs[a], mine, sems[2].at[a]).start()
        for k, rel in enumerate((1, 4, 2)):
            _gather_copy(sems, a, k, mine, _peer(x, y, c, rel)[0], src=srcs[a]).start()


def _gather_relay(srcs, outs, sems):
    x, y, c, _ = _my_place()
    sib = _peer(x, y, c, 1)[0]
    (xn, xn_i), (yn, yn_i) = _peer(x, y, c, 4), _peer(x, y, c, 2)
    for a in range(len(srcs)):
        out = outs[a]
        _gather_copy(sems, a, 1, _piece(out, xn_i), xn).wait_recv()
        _gather_copy(sems, a, 3, _piece(out, xn_i, 0), yn).start()
        _gather_copy(sems, a, 5, _piece(out, xn_i), sib).start()
        _gather_copy(sems, a, 2, _piece(out, yn_i), yn).wait_recv()
        if _splits(out):
            _gather_copy(sems, a, 4, _piece(out, yn_i, 1), xn).start()
        _gather_copy(sems, a, 6, _piece(out, yn_i), sib).start()


def _gather_finish(srcs, outs, sems):
    x, y, c, me = _my_place()
    sib = _peer(x, y, c, 1)[0]
    xn, yn, dg_i = _peer(x, y, c, 4)[0], _peer(x, y, c, 2)[0], _peer(x, y, c, 6)[1]
    n = len(srcs)
    for a in range(n):
        out = outs[a]
        _gather_copy(sems, a, 3, _piece(out, dg_i, 0), yn).wait_recv()
        _gather_copy(sems, a, 7, _piece(out, dg_i, 0), sib).start()
        if _splits(out):
            _gather_copy(sems, a, 4, _piece(out, dg_i, 1), xn).wait_recv()
            _gather_copy(sems, a, 8, _piece(out, dg_i, 1), sib).start()
    for a in range(n):
        out = outs[a]
        whole, half = _piece(out, me), _piece(out, me, 0)
        for k in (0, 5, 6):
            _gather_copy(sems, a, k, whole, sib).wait_recv()
        for k in (7, 8) if _splits(out) else (7,):
            _gather_copy(sems, a, k, half, sib).wait_recv()
        for k in (0, 1, 2):
            _gather_copy(sems, a, k, whole, sib, src=srcs[a]).wait_send()
        for k in (5, 6):
            _gather_copy(sems, a, k, whole, sib).wait_send()
        for k in (3, 4, 7, 8) if _splits(out) else (3, 7):
            _gather_copy(sems, a, k, half, sib).wait_send()
        pltpu.make_async_copy(srcs[a], whole, sems[2].at[a]).wait()


def _relay_sems(n):
    return [pltpu.SemaphoreType.DMA((n, 9)), pltpu.SemaphoreType.DMA((n, 9)), pltpu.SemaphoreType.DMA((n,))]


def _gather_sems(n):
    return [pltpu.SemaphoreType.DMA((n, 7)), pltpu.SemaphoreType.DMA((n, 7)), pltpu.SemaphoreType.DMA((n,))]


class _Comm:
    def __init__(self, args, out_shape, scratch, start, finish, relay=None):
        self.args, self.out_shape, self.scratch, self.start, self.finish = args, out_shape, scratch, start, finish
        self.relay = relay


def _gather_comm(shards):
    return _Comm(list(shards), [_sds((N_DEV,) + s.shape[1:], s.dtype) for s in shards], _relay_sems(len(shards)),
                 _gather_start, _gather_finish, relay=_gather_relay)


def _direct_copies(ins, outs, sems, wait):
    send_sems, recv_sems, local_sems = sems
    x, y, c, me = _my_place()
    for a in range(len(ins)):
        local = pltpu.make_async_copy(ins[a].at[pl.ds(me, 1)], outs[a].at[pl.ds(me, 1)], local_sems.at[a])
        local.wait() if wait else local.start()
        for k in range(1, N_DEV):
            to, to_i = _peer(x, y, c, k)
            cp = pltpu.make_async_remote_copy(
                src_ref=ins[a].at[pl.ds(to_i, 1)], dst_ref=outs[a].at[pl.ds(me, 1)],
                send_sem=send_sems.at[a, k - 1], recv_sem=recv_sems.at[a, k - 1], device_id=to, device_id_type=MESH)
            cp.wait() if wait else cp.start()


def _exchange_comm(grads):
    return _Comm(list(grads), [_sds(g.shape, g.dtype) for g in grads], _gather_sems(len(grads)),
                 lambda i, o, s: _direct_copies(i, o, s, False), lambda i, o, s: _direct_copies(i, o, s, True))


def _chip_copies(ins, outs, sems, wait):
    send_sems, recv_sems, local_sems = sems
    x, y, c, _ = _my_place()
    for a in range(len(ins)):
        local = pltpu.make_async_copy(ins[a].at[pl.ds(2 * x + y, 1)], outs[a].at[pl.ds(len(CHIPS), 1)],
                                      local_sems.at[a])
        local.wait() if wait else local.start()
        for i, k in enumerate(CHIPS):
            to = _peer(x, y, c, k)[0]
            cp = pltpu.make_async_remote_copy(
                src_ref=ins[a].at[pl.ds(2 * to[0] + to[1], 1)], dst_ref=outs[a].at[pl.ds(i, 1)],
                send_sem=send_sems.at[a, i], recv_sem=recv_sems.at[a, i], device_id=to, device_id_type=MESH)
            cp.wait() if wait else cp.start()


def _chip_exchange_comm(sums):
    n = len(sums)
    sems = [pltpu.SemaphoreType.DMA((n, len(CHIPS))), pltpu.SemaphoreType.DMA((n, len(CHIPS))),
            pltpu.SemaphoreType.DMA((n,))]
    return _Comm(list(sums), [_sds(s.shape, s.dtype) for s in sums], sems,
                 lambda i, o, s: _chip_copies(i, o, s, False), lambda i, o, s: _chip_copies(i, o, s, True))


def _pair_reduce(name, grads, after=()):
    n = len(grads)
    n_chips = N_DEV // 2

    def body(*refs):
        g_refs, gh_refs, refs = refs[:n], refs[n:2 * n], refs[2 * n + len(after):]
        p_refs, land = refs[:n], refs[n:2 * n]
        send_sems, recv_sems = refs[2 * n:]
        x, y, c, _ = _my_place()
        sib = _peer(x, y, c, 1)[0]
        q = pl.program_id(0)

        def to_sibling(a, j):
            return pltpu.make_async_remote_copy(
                src_ref=gh_refs[a].at[j, pl.ds(1 - c, 1)], dst_ref=land[a].at[pl.ds(j, 1)],
                send_sem=send_sems.at[a, j], recv_sem=recv_sems.at[a, j], device_id=sib, device_id_type=MESH)

        @pl.when(q == 0)
        def _():
            for j in range(n_chips):
                for a in range(n):
                    to_sibling(a, j).start()

        for a in range(n):
            to_sibling(a, q).wait_recv()
            p_refs[a][...] = (g_refs[a][0, pl.ds(c, 1)].astype(F32) + land[a][pl.ds(q, 1)].astype(F32)).astype(BF16)

        @pl.when(q == n_chips - 1)
        def _():
            for a in range(n):
                for j in range(n_chips):
                    to_sibling(a, j).wait_send()

    views = [g.reshape((n_chips, 2) + g.shape[1:]) for g in grads]
    res = pl.pallas_call(
        body, name=name, grid=(n_chips,),
        in_specs=[pl.BlockSpec((1, 2) + g.shape[1:], lambda q: (q, 0, 0, 0)) for g in grads]
        + [ANY] * (n + len(after)),
        out_specs=[pl.BlockSpec((1,) + g.shape[1:], lambda q: (q, 0, 0)) for g in grads],
        out_shape=[_sds((n_chips,) + g.shape[1:], BF16) for g in grads],
        scratch_shapes=[pltpu.VMEM((n_chips,) + g.shape[1:], BF16) for g in grads]
        + [pltpu.SemaphoreType.DMA((n, n_chips)), pltpu.SemaphoreType.DMA((n, n_chips))],
        compiler_params=pltpu.CompilerParams(dimension_semantics=("arbitrary",), vmem_limit_bytes=VMEM_LIMIT),
    )(*views, *views, *after)
    return list(res)


def _pair_exchange_comm(grads):
    n, n_chips = len(grads), N_DEV // 2

    def copies(ins, outs, sems, wait):
        x, y, c, _ = _my_place()
        for j in range(n_chips):
            for a in range(n):
                cp = pltpu.make_async_remote_copy(
                    src_ref=ins[a].at[j, pl.ds(1 - c, 1)], dst_ref=outs[a].at[pl.ds(j, 1)], send_sem=sems[0].at[a, j],
                    recv_sem=sems[1].at[a, j], device_id=_peer(x, y, c, 1)[0], device_id_type=MESH)
                cp.wait() if wait else cp.start()

    views = [g.reshape((n_chips, 2) + g.shape[1:]) for g in grads]
    sems = [pltpu.SemaphoreType.DMA((n, n_chips)), pltpu.SemaphoreType.DMA((n, n_chips))]
    return _Comm(views, [_sds((n_chips,) + g.shape[1:], g.dtype) for g in grads], sems,
                 lambda i, o, s: copies(i, o, s, False), lambda i, o, s: copies(i, o, s, True))


def _pair_add(name, grads, landed, after=()):
    n, n_chips = len(grads), N_DEV // 2

    def body(core_ref, *refs):
        g_refs, l_refs, p_refs = refs[:n], refs[n:2 * n], refs[2 * n + len(after):]
        for a in range(n):
            p_refs[a][...] = (g_refs[a][...].astype(F32) + l_refs[a][...].astype(F32)).astype(BF16)

    views = [g.reshape((n_chips, 2) + g.shape[1:]) for g in grads]
    blocks = [pl.BlockSpec((1,) + g.shape[1:], lambda q, core: (q, 0, 0)) for g in grads]
    mine = [pl.BlockSpec((1, None) + g.shape[1:], lambda q, core: (q, core[0], 0, 0)) for g in grads]
    return list(pl.pallas_call(
        body, name=name, out_shape=[_sds((n_chips,) + g.shape[1:], BF16) for g in grads],
        grid_spec=pltpu.PrefetchScalarGridSpec(num_scalar_prefetch=1, grid=(n_chips,),
                                               in_specs=mine + blocks + [ANY] * len(after), out_specs=blocks),
        compiler_params=pltpu.CompilerParams(dimension_semantics=("arbitrary",), vmem_limit_bytes=VMEM_LIMIT),
    )(lax.axis_index("c").reshape(1), *views, *landed, *after))


def _call(name, body, grid, in_specs, out_specs, out_shape, args, scratch=(), after=()):
    ni, na = len(in_specs), len(after)

    def ordered(*refs):
        body(*refs[:ni], *refs[ni + na:])

    return list(pl.pallas_call(
        ordered if after else body, name=name, grid=grid, in_specs=list(in_specs) + [ANY] * na,
        out_specs=list(out_specs), out_shape=list(out_shape), scratch_shapes=list(scratch),
        compiler_params=pltpu.CompilerParams(dimension_semantics=("arbitrary",) * len(grid),
                                             vmem_limit_bytes=VMEM_LIMIT))(*args, *after))


SIBLING_AND_NEIGHBOURS, OTHER_CHIPS, EVERYONE = (1, 4, 2), CHIPS, tuple(range(1, N_DEV))


def _by_sequencer(name, comm, peers, collective_id):
    src = [jax.new_ref(a, memory_space=pltpu.MemorySpace.HBM) for a in comm.args]
    dst = [jax.empty_ref(s, memory_space=pltpu.MemorySpace.HBM) for s in comm.out_shape]

    @pl.kernel(mesh=plsc.ScalarSubcoreMesh(axis_name="sequencer", num_cores=1), name=name,
               scratch_types=tuple(comm.scratch), compiler_params=pltpu.CompilerParams(collective_id=collective_id))
    def launch(*sems):
        x, y, c, _ = _my_place()
        barrier = pltpu.get_barrier_semaphore()
        for k in peers:
            pl.semaphore_signal(barrier, inc=1, device_id=_peer(x, y, c, k)[0], device_id_type=MESH)
        pl.semaphore_wait(barrier, len(peers))
        comm.start(src, dst, sems)
        if comm.relay is not None:
            comm.relay(src, dst, sems)
        comm.finish(src, dst, sems)

    launch()
    return [d[...] for d in dst]


def _gather_first(first, later, swapped):
    nf = len(first)
    layer_of = [(a, l) for a, s in enumerate(later) for l in range(s.shape[0])]
    nl = len(layer_of)
    dts = [BF16] * (nf - 2) + [F32, F32]
    shard = [s.shape[:0:-1] if sw else s.shape[1:] for s, sw in zip(later, swapped)]

    def body(*refs):
        ins, refs = refs[:nf + len(later)], refs[nf + len(later):]
        outs, refs = refs[:nf], refs[nf:]
        casts, refs = refs[:nl], refs[nl:]
        stage, sems = refs[:nf], refs[nf:]
        for a in range(nf):
            stage[a][...] = ins[a][...].astype(dts[a])
            _gather_start(stage, outs, sems, only=a)
        for k, (a, l) in enumerate(layer_of):
            block = ins[nf + a][l]
            casts[k][0] = (block.T if swapped[a] else block).astype(BF16)
        _gather_relay(stage, outs, sems)
        _gather_finish(stage, outs, sems)

    res = pl.pallas_call(
        body, name="gather_first",
        in_specs=[VMEM] * (nf + len(later)), out_specs=[ANY] * nf + [VMEM] * nl,
        out_shape=[_sds((N_DEV,) + s.shape[1:], dt) for s, dt in zip(first, dts)]
        + [_sds((1,) + shard[a], BF16) for a, _ in layer_of],
        scratch_shapes=[pltpu.VMEM(s.shape, dt) for s, dt in zip(first, dts)] + _relay_sems(nf),
        compiler_params=pltpu.CompilerParams(vmem_limit_bytes=VMEM_LIMIT),
    )(*first, *later)
    return list(res[:nf]), list(res[nf:])


def _a_mix_fwd(x, g, w_in, ln_g, ln_b, w_s, b_st, w_out):
    t = x.shape[0]
    nblk = TM // GMLP_BLOCK

    def body(x_ref, g_ref, win_ref, lng_ref, lnb_ref, ws_ref, bst_ref, wout_ref, h_ref, z_ref, gated_scr):
        xv = x_ref[...]
        hb = _rms_fwd(xv, g_ref[...])[0].astype(BF16)
        for d in range(N_DEV):
            z_ref[:, d * FF_SLOT:(d + 1) * FF_SLOT] = _dot(hb, win_ref[d])
        u = _gelu(z_ref[:, :GATE_DIM])
        vb = _ln_fwd(_gelu(z_ref[:, GATE_DIM:]), lng_ref[...], lnb_ref[...])[0].astype(BF16)
        mask = _gate_mask()
        for gi in range(A_GROUPS):
            wm = jnp.where(mask, ws_ref[gi], 0.0).astype(BF16)
            bias = bst_ref[:, gi:gi + 1]
            cs = slice(gi * A_GROUP_DIM, (gi + 1) * A_GROUP_DIM)
            for n in range(nblk):
                rs = slice(n * GMLP_BLOCK, (n + 1) * GMLP_BLOCK)
                sv = _dot(wm, vb[rs, cs]) + bias
                gated_scr[rs, cs] = (u[rs, cs] * sv).astype(BF16)
        h_ref[...] = xv + _dot(gated_scr[...], wout_ref[...])

    return _call(
        "a_mix_fwd", body, (t // TM,),
        [_row(D_MODEL), _res((1, D_MODEL)), _res((N_DEV, D_MODEL, FF_SLOT)), _res((1, GATE_DIM)),
         _res((1, GATE_DIM)), _res((A_GROUPS, GMLP_BLOCK, GMLP_BLOCK)), _res((GMLP_BLOCK, A_GROUPS)),
         _res((GATE_DIM, D_MODEL))],
        [_row(D_MODEL), _row(2 * GATE_DIM), _row(GATE_DIM)],
        [_sds((t, D_MODEL), F32), _sds((t, 2 * GATE_DIM), F32), _sds((t, GATE_DIM), BF16)],
        (x, g, w_in, ln_g, ln_b, w_s, b_st, w_out))


MLP_W_SPECS = (_res((N_DEV, D_MODEL, FF_SLOT)), _res((N_DEV, FF_SLOT, D_MODEL)))


def _mlp_fwd(h, g, w1, w2):
    t = h.shape[0]

    def body(h_ref, g_ref, w1_ref, w2_ref, o_ref, a_ref):
        hv = h_ref[...]
        hb = _rms_fwd(hv, g_ref[...])[0].astype(BF16)
        o_ref[...] = hv
        for d in range(N_DEV):
            a = _dot(hb, w1_ref[d])
            a_ref[:, d * FF_SLOT:(d + 1) * FF_SLOT] = a
            r = jnp.maximum(a, 0.0)
            o_ref[...] += _dot((r * r).astype(BF16), w2_ref[d])

    return _call(
        "mlp_fwd", body, (t // TM_MLP_FWD,), [_row(D_MODEL, TM_MLP_FWD), _res((1, D_MODEL)), *MLP_W_SPECS],
        [_row(D_MODEL, TM_MLP_FWD), _row(D_FF, TM_MLP_FWD)], [_sds((t, D_MODEL), F32), _sds((t, D_FF), F32)],
        (h, g, w1, w2))


def _mlp_fwd_loss(h, g, w1, w2, final_g, target):
    t = h.shape[0]

    def body(h_ref, g_ref, w1_ref, w2_ref, fg_ref, t_ref, a_ref, loss_ref, dh_ref, dg_ref):
        hv = h_ref[...]
        hb = _rms_fwd(hv, g_ref[...])[0].astype(BF16)
        out = hv
        for d in range(N_DEV):
            a = _dot(hb, w1_ref[d])
            a_ref[:, d * FF_SLOT:(d + 1) * FF_SLOT] = a
            r = jnp.maximum(a, 0.0)
            out = out + _dot((r * r).astype(BF16), w2_ref[d])
        y, xhat, rstd = _rms_fwd(out, fg_ref[...])
        err = y - t_ref[...]
        part = 0.5 * jnp.sum(jnp.mean(err * err, axis=-1, keepdims=True), axis=0, keepdims=True)
        dx, dg = _rms_bwd(err * (1.0 / D_MODEL), xhat, rstd, fg_ref[...])
        dh_ref[...] = dx
        _acc(dg_ref, dg)
        _acc(loss_ref, part)

    return _call(
        "mlp_fwd_loss", body, (t // TM,),
        [_row(D_MODEL), _res((1, D_MODEL)), *MLP_W_SPECS, _res((1, D_MODEL)), _row(D_MODEL)],
        [_row(D_FF), _const((1, 1)), _row(D_MODEL), _const((1, D_MODEL))],
        [_sds((t, D_FF), F32), _sds((1, 1), F32), _sds((t, D_MODEL), F32), _sds((1, D_MODEL), F32)],
        (h, g, w1, w2, final_g, target))


KVQ_W_SPECS = (_res((1, D_MODEL)), _res((D_MODEL, KV_LORA + QK_ROPE)), _res((1, KV_LORA)),
               _res((B_HEADS, KV_LORA, QK_NOPE + V_HEAD)), _res((1, D_MODEL)), _res((D_MODEL, Q_LORA)),
               _res((1, Q_LORA)), _res((B_HEADS, Q_LORA, QK_NOPE + QK_ROPE)))


def _kvq_fwd(h, pos, inv_freq, kvq_w):
    t = h.shape[0]
    half = QK_ROPE // 2

    def body(h_ref, pos_ref, invf_ref, srcg_ref, wkva_ref, kvag_ref, wkvb_ref, mixg_ref, wqa_ref, qg_ref, wqb_ref,
             ckv_ref, k_ref, v_ref, cqpre_ref, q_ref, cos_ref, sin_ref):
        hv = h_ref[...]
        xhat = hv * lax.rsqrt(jnp.mean(hv * hv, axis=-1, keepdims=True) + EPS)
        ang = pos_ref[...].astype(F32) * invf_ref[...]
        cos, sin = jnp.cos(ang), jnp.sin(ang)
        cos_ref[...] = cos
        sin_ref[...] = sin
        ckv = _dot((xhat * srcg_ref[...]).astype(BF16), wkva_ref[...])
        ckv_ref[...] = ckv
        cb = _rms_fwd(ckv[:, :KV_LORA], kvag_ref[...])[0].astype(BF16)
        kpe = _rope(ckv[:, KV_LORA:], cos, sin).astype(BF16)
        for hd in range(B_HEADS):
            kv = _dot(cb, wkvb_ref[hd])
            k_ref[hd, :, 0:QK_NOPE] = kv[:, :QK_NOPE].astype(BF16)
            k_ref[hd, :, QK_NOPE:] = kpe
            v_ref[hd] = kv[:, QK_NOPE:].astype(BF16)
        cqpre = _dot((xhat * mixg_ref[...]).astype(BF16), wqa_ref[...])
        cqpre_ref[...] = cqpre
        cqb = _rms_fwd(cqpre, qg_ref[...])[0].astype(BF16)
        for hd in range(B_HEADS):
            q = _dot(cqb, wqb_ref[hd])
            q_ref[hd, :, 0:QK_NOPE] = q[:, :QK_NOPE].astype(BF16)
            q_ref[hd, :, QK_NOPE:] = _rope(q[:, QK_NOPE:], cos, sin).astype(BF16)

    tm = TM_KVQ
    return _call(
        "kvq_fwd", body, (t // tm,), [_row(D_MODEL, tm), _row(1, tm), _res((1, half)), *KVQ_W_SPECS],
        [_row(KV_LORA + QK_ROPE, tm), _heads(QK_NOPE + QK_ROPE, tm), _heads(V_HEAD, tm), _row(Q_LORA, tm),
         _heads(QK_NOPE + QK_ROPE, tm), _row(half, tm), _row(half, tm)],
        [_sds((t, KV_LORA + QK_ROPE), F32), _sds((B_HEADS, t, QK_NOPE + QK_ROPE), BF16),
         _sds((B_HEADS, t, V_HEAD), BF16), _sds((t, Q_LORA), F32), _sds((B_HEADS, t, QK_NOPE + QK_ROPE), BF16),
         _sds((t, half), F32), _sds((t, half), F32)],
        (h, pos, inv_freq, *kvq_w))


def _softmax_rows(q, k_ref, k):
    past, upto = k * TM, (k + 1) * TM
    s = _dot_nt(q, k_ref[0:upto, :])
    own = jnp.where(_att_mask(0, TM, TM), s[:, past:], jnp.finfo(F32).min)
    s = own if k == 0 else jnp.concatenate([s[:, :past], own], axis=1)
    e = jnp.exp2((s - jnp.max(s, axis=-1, keepdims=True)) * (ATT_SCALE * LOG2_E))
    return e * (1.0 / jnp.sum(e, axis=-1, keepdims=True))


def _for_my_tile(i, nq, fn):
    for k in range(nq):
        @pl.when(i == k)
        def _(k=k):
            fn(k)


def _attn_fwd(h, q, k, v, w_o):
    t = h.shape[0]
    nq, hps = t // TM, HEADS_PER_STEP

    def body(h_ref, q_ref, k_ref, v_ref, wo_ref, o_ref, att_ref):
        i, pair = pl.program_id(0), pl.program_id(1)

        @pl.when(pair == 0)
        def _():
            o_ref[...] = h_ref[...]

        def tile(kt):
            proj = None
            for j in range(hps):
                hd = pair * hps + j
                p = _softmax_rows(q_ref[j], k_ref.at[hd], kt)
                ob = _dot(p.astype(BF16), v_ref[hd, 0:(kt + 1) * TM, :]).astype(BF16)
                att_ref[j] = ob
                proj = _dot(ob, wo_ref[hd]) if proj is None else proj + _dot(ob, wo_ref[hd])
            o_ref[...] += proj

        _for_my_tile(i, nq, tile)

    def per_head(d):
        return pl.BlockSpec((hps, TM, d), lambda i, pair: (pair, i, 0))

    def resident(shape):
        zeros = (0,) * len(shape)
        return pl.BlockSpec(shape, lambda i, pair: zeros, pipeline_mode=pl.Buffered(1))

    tile_spec = pl.BlockSpec((TM, D_MODEL), lambda i, pair: (i, 0))
    return _call(
        "attn_fwd", body, (nq, B_HEADS // hps),
        [tile_spec, per_head(QK_NOPE + QK_ROPE), resident((B_HEADS, t, QK_NOPE + QK_ROPE)),
         resident((B_HEADS, t, V_HEAD)), resident((B_HEADS, V_HEAD, D_MODEL))],
        [tile_spec, per_head(V_HEAD)], [_sds((t, D_MODEL), F32), _sds((B_HEADS, t, V_HEAD), BF16)],
        (h, q, k, v, w_o))


def _mlp_bwd(h, a, dho, g, w1, w2, layer, after=()):
    t = h.shape[0]

    def body(h_ref, a_ref, dho_ref, g_ref, w1_ref, w2_ref, dhi_ref, dg_ref, hn_ref, f_ref, da_ref, dhib_ref):
        gv = g_ref[...]
        y, xhat, rstd = _rms_fwd(h_ref[...], gv)
        hn_ref[...] = y.astype(BF16)
        dho_v = dho_ref[...]
        dhob = dho_v.astype(BF16)
        dhn = jnp.zeros((TM, D_MODEL), F32)
        for d in range(N_DEV):
            cs = slice(d * FF_SLOT, (d + 1) * FF_SLOT)
            r = jnp.maximum(a_ref[:, cs], 0.0)
            f_ref[:, cs] = (r * r).astype(BF16)
            da = (_dot_nt(dhob, w2_ref[d]) * (2.0 * r)).astype(BF16)
            da_ref[:, cs] = da
            dhn = dhn + _dot_nt(da, w1_ref[d])
        dx, dg = _rms_bwd(dhn, xhat, rstd, gv)
        dhi = dho_v + dx
        dhi_ref[...] = dhi
        dhib_ref[...] = dhi.astype(BF16)
        _acc(dg_ref, dg)

    return _call(
        f"mlp_bwd_{layer}", body, (t // TM,),
        [_row(D_MODEL), _row(D_FF), _row(D_MODEL), _res((1, D_MODEL)), *MLP_W_SPECS],
        [_row(D_MODEL), _const((1, D_MODEL)), _row(D_MODEL), _row(D_FF), _row(D_FF), _row(D_MODEL)],
        [_sds((t, D_MODEL), F32), _sds((1, D_MODEL), F32), _sds((t, D_MODEL), BF16), _sds((t, D_FF), BF16),
         _sds((t, D_FF), BF16), _sds((t, D_MODEL), BF16)],
        (h, a, dho, g, w1, w2), after=after)


def _attn_bwd(dh, q, k, v, w_o, cos, sin, after=()):
    t = dh.shape[0]
    half, hps = QK_ROPE // 2, HEADS_PER_STEP

    def body(dh_ref, q_ref, k_ref, v_ref, wo_ref, cos_ref, sin_ref, dq_ref, dk_ref, dv_ref):
        i = pl.program_id(1)

        @pl.when(i == 0)
        def _():
            dk_ref[...] = jnp.zeros_like(dk_ref)
            dv_ref[...] = jnp.zeros_like(dv_ref)

        def tile(kt):
            keys = slice(0, (kt + 1) * TM)
            for j in range(hps):
                qj = q_ref[j]
                do = _dot_nt(dh_ref[kt * TM:(kt + 1) * TM, :], wo_ref[j]).astype(BF16)
                p = _softmax_rows(qj, k_ref.at[j], kt)
                dp = _dot_nt(do, v_ref[j, keys, :])
                ds = (p * (dp - jnp.sum(p * dp, axis=-1, keepdims=True)) * ATT_SCALE).astype(BF16)
                dq = _dot(ds, k_ref[j, keys, :])
                dq_ref[j, :, 0:QK_NOPE] = dq[:, :QK_NOPE].astype(BF16)
                dq_ref[j, :, QK_NOPE:] = _rope(dq[:, QK_NOPE:], cos_ref[...], -sin_ref[...]).astype(BF16)
                dk_ref[j, keys, :] += _dot_tn(ds, qj)
                dv_ref[j, keys, :] += _dot_tn(p.astype(BF16), do)

        _for_my_tile(i, t // TM, tile)

    def per_pair(rows, d, tiled):
        return pl.BlockSpec((hps, rows, d), (lambda pair, i: (pair, i, 0)) if tiled else (lambda pair, i: (pair, 0, 0)))

    def tile(d):
        return pl.BlockSpec((TM, d), lambda pair, i: (i, 0))

    return _call(
        "attn_bwd", body, (B_HEADS // hps, t // TM),
        [pl.BlockSpec((t, D_MODEL), lambda pair, i: (0, 0), pipeline_mode=pl.Buffered(1)),
         per_pair(TM, QK_NOPE + QK_ROPE, True), per_pair(t, QK_NOPE + QK_ROPE, False), per_pair(t, V_HEAD, False),
         per_pair(V_HEAD, D_MODEL, False), tile(half), tile(half)],
        [per_pair(TM, QK_NOPE + QK_ROPE, True), per_pair(t, QK_NOPE + QK_ROPE, False), per_pair(t, V_HEAD, False)],
        [_sds((B_HEADS, t, QK_NOPE + QK_ROPE), BF16), _sds((B_HEADS, t, QK_NOPE + QK_ROPE), F32),
         _sds((B_HEADS, t, V_HEAD), F32)],
        (dh, q, k, v, w_o, cos, sin), after=after)


def _kvq_bwd(h, dh, ckv, cqpre, dq, dk, dv, cos, sin, kvq_w, after=()):
    t = h.shape[0]
    tm = TM
    half, last = QK_ROPE // 2, t // tm - 1
    grad_shapes = [(D_MODEL, Q_LORA), (B_HEADS, Q_LORA, QK_NOPE + QK_ROPE), (D_MODEL, KV_LORA + QK_ROPE),
                   (B_HEADS, KV_LORA, QK_NOPE + V_HEAD)]

    def body(h_ref, dh_ref, ckv_ref, cqpre_ref, dq_ref, dk_ref, dv_ref, cos_ref, sin_ref,
             srcg_ref, wkva_ref, kvag_ref, wkvb_ref, mixg_ref, wqa_ref, qg_ref, wqb_ref,
             dhi_ref, dmixg_ref, dsrcg_ref, dqg_ref, dkvag_ref, gqa_ref, gqb_ref, gkva_ref, gkvb_ref,
             aqa, aqb, akva, akvb):
        @pl.when(pl.program_id(0) == 0)
        def _():
            for acc in (aqa, aqb, akva, akvb):
                acc[...] = jnp.zeros_like(acc)

        hv = h_ref[...]
        rstd = lax.rsqrt(jnp.mean(hv * hv, axis=-1, keepdims=True) + EPS)
        xhat = hv * rstd
        mixg, srcg, qg, kvag = mixg_ref[...], srcg_ref[...], qg_ref[...], kvag_ref[...]
        cq, cqhat, crstd = _rms_fwd(cqpre_ref[...], qg)
        cqb = cq.astype(BF16)
        dcq = jnp.zeros((tm, Q_LORA), F32)
        for hd in range(B_HEADS):
            dcq = dcq + _dot_nt(dq_ref[hd], wqb_ref[hd])
            aqb[hd] += _dot_tn(cqb, dq_ref[hd])
        dcqpre, dqg = _rms_bwd(dcq, cqhat, crstd, qg)
        dcqpre_b = dcqpre.astype(BF16)
        aqa[...] += _dot_tn((xhat * mixg).astype(BF16), dcqpre_b)
        dxq, dmixg = _rms_bwd(_dot_nt(dcqpre_b, wqa_ref[...]), xhat, rstd, mixg)
        ckv = ckv_ref[...]
        c, chat, krstd = _rms_fwd(ckv[:, :KV_LORA], kvag)
        cb = c.astype(BF16)
        dc = jnp.zeros((tm, KV_LORA), F32)
        dkpe = jnp.zeros((tm, QK_ROPE), F32)
        for hd in range(B_HEADS):
            dkv = jnp.concatenate([dk_ref[hd, :, 0:QK_NOPE], dv_ref[hd]], axis=-1).astype(BF16)
            akvb[hd] += _dot_tn(cb, dkv)
            dc = dc + _dot_nt(dkv, wkvb_ref[hd])
            dkpe = dkpe + dk_ref[hd, :, QK_NOPE:]
        dlat, dkvag = _rms_bwd(dc, chat, krstd, kvag)
        dpe = _rope(dkpe, cos_ref[...], -sin_ref[...])
        dckv_b = jnp.concatenate([dlat, dpe], axis=-1).astype(BF16)
        akva[...] += _dot_tn((xhat * srcg).astype(BF16), dckv_b)
        dxk, dsrcg = _rms_bwd(_dot_nt(dckv_b, wkva_ref[...]), xhat, rstd, srcg)
        dhi_ref[...] = dh_ref[...] + dxq + dxk
        _acc(dmixg_ref, dmixg)
        _acc(dsrcg_ref, dsrcg)
        _acc(dqg_ref, dqg)
        _acc(dkvag_ref, dkvag)

        @pl.when(pl.program_id(0) == last)
        def _():
            for out, acc in ((gqa_ref, aqa), (gqb_ref, aqb), (gkva_ref, akva), (gkvb_ref, akvb)):
                out[...] = acc[...].astype(BF16)

    return _call(
        "kvq_bwd", body, (t // tm,),
        [_row(D_MODEL, tm), _row(D_MODEL, tm), _row(KV_LORA + QK_ROPE, tm), _row(Q_LORA, tm),
         _heads(QK_NOPE + QK_ROPE, tm), _heads(QK_NOPE + QK_ROPE, tm), _heads(V_HEAD, tm), _row(half, tm),
         _row(half, tm), *KVQ_W_SPECS],
        [_row(D_MODEL, tm), _const((1, D_MODEL)), _const((1, D_MODEL)), _const((1, Q_LORA)), _const((1, KV_LORA))]
        + [_const(s) for s in grad_shapes],
        [_sds((t, D_MODEL), F32), _sds((1, D_MODEL), F32), _sds((1, D_MODEL), F32), _sds((1, Q_LORA), F32),
         _sds((1, KV_LORA), F32)] + [_sds(s, BF16) for s in grad_shapes],
        (h, dh, ckv, cqpre, dq, dk, dv, cos, sin, *kvq_w), scratch=[pltpu.VMEM(s, F32) for s in grad_shapes],
        after=after)


def _a_mix_bwd(x, z, dh, g, w_in, ln_g, ln_b, w_s, b_st, w_out, after=()):
    t = x.shape[0]
    tm = TM_GATE
    nblk = tm // GMLP_BLOCK

    def body(x_ref, z_ref, dh_ref, g_ref, win_ref, lng_ref, lnb_ref, ws_ref, bst_ref, wout_ref,
             dx_ref, hn_ref, dz_ref, dg_ref, dlng_ref, dlnb_ref, dws_ref, dbs_ref, dvn_scr, gelu_grad_v):
        @pl.when(pl.program_id(0) == 0)
        def _():
            dws_ref[...] = jnp.zeros_like(dws_ref)
            dbs_ref[...] = jnp.zeros_like(dbs_ref)

        gv, lng = g_ref[...], lng_ref[...]
        y, xhat, rstd = _rms_fwd(x_ref[...], gv)
        hn_ref[...] = y.astype(BF16)
        dhv = dh_ref[...]
        dgated = _dot_nt(dhv.astype(BF16), wout_ref[...])
        u, gelu_grad_u = _gelu_and_grad(z_ref[:, :GATE_DIM])
        v, gelu_grad_v[...] = _gelu_and_grad(z_ref[:, GATE_DIM:])
        vn, vhat, lrstd = _ln_fwd(v, lng, lnb_ref[...])
        vb = vn.astype(BF16)
        mask = _gate_mask()
        for gi in range(A_GROUPS):
            wm = jnp.where(mask, ws_ref[gi], 0.0).astype(BF16)
            bias = bst_ref[:, gi:gi + 1]
            cs = slice(gi * A_GROUP_DIM, (gi + 1) * A_GROUP_DIM)
            dws = jnp.zeros((GMLP_BLOCK, GMLP_BLOCK), F32)
            dbs = jnp.zeros((GMLP_BLOCK, 1), F32)
            for n in range(nblk):
                rs = slice(n * GMLP_BLOCK, (n + 1) * GMLP_BLOCK)
                sv = _dot(wm, vb[rs, cs]) + bias
                dz_ref[rs, cs] = (dgated[rs, cs] * sv * gelu_grad_u[rs, cs]).astype(BF16)
                dsv = dgated[rs, cs] * u[rs, cs]
                dsvb = dsv.astype(BF16)
                dws = dws + _dot_nt(dsvb, vb[rs, cs])
                dbs = dbs + jnp.sum(dsv, axis=-1, keepdims=True)
                dvn_scr[rs, cs] = _dot_tn(wm, dsvb)
            dws_ref[gi] += jnp.where(mask, dws, 0.0)
            dbs_ref[gi] += dbs
        dvn = dvn_scr[...]
        dvhat = dvn * lng
        dv = lrstd * (dvhat - jnp.mean(dvhat, axis=-1, keepdims=True)
                      - vhat * jnp.mean(dvhat * vhat, axis=-1, keepdims=True))
        dz_ref[:, GATE_DIM:] = (dv * gelu_grad_v[...]).astype(BF16)
        dhn = jnp.zeros((tm, D_MODEL), F32)
        for d in range(N_DEV):
            dhn = dhn + _dot_nt(dz_ref[:, d * FF_SLOT:(d + 1) * FF_SLOT], win_ref[d])
        dx, dg = _rms_bwd(dhn, xhat, rstd, gv)
        dx_ref[...] = dhv + dx
        _acc(dg_ref, dg)
        _acc(dlng_ref, jnp.sum(dvn * vhat, axis=0, keepdims=True))
        _acc(dlnb_ref, jnp.sum(dvn, axis=0, keepdims=True))

    return _call(
        "a_mix_bwd", body, (t // tm,),
        [_row(D_MODEL, tm), _row(2 * GATE_DIM, tm), _row(D_MODEL, tm), _res((1, D_MODEL)),
         _res((N_DEV, D_MODEL, FF_SLOT)), _res((1, GATE_DIM)), _res((1, GATE_DIM)),
         _res((A_GROUPS, GMLP_BLOCK, GMLP_BLOCK)), _res((GMLP_BLOCK, A_GROUPS)), _res((GATE_DIM, D_MODEL))],
        [_row(D_MODEL, tm), _row(D_MODEL, tm), _row(2 * GATE_DIM, tm),
         _const((1, D_MODEL)), _const((1, GATE_DIM)), _const((1, GATE_DIM)),
         _const((A_GROUPS, GMLP_BLOCK, GMLP_BLOCK)), _const((A_GROUPS, GMLP_BLOCK, 1))],
        [_sds((t, D_MODEL), F32), _sds((t, D_MODEL), BF16),
         _sds((t, 2 * GATE_DIM), BF16), _sds((1, D_MODEL), F32), _sds((1, GATE_DIM), F32),
         _sds((1, GATE_DIM), F32), _sds((A_GROUPS, GMLP_BLOCK, GMLP_BLOCK), F32),
         _sds((A_GROUPS, GMLP_BLOCK, 1), F32)],
        (x, z, dh, g, w_in, ln_g, ln_b, w_s, b_st, w_out),
        scratch=[pltpu.VMEM((tm, GATE_DIM), F32), pltpu.VMEM((tm, GATE_DIM), F32)], after=after)


def _wgrad(name, a, b, a_spec, b_spec, m, n, after=()):
    def body(a_ref, b_ref, o_ref):
        o_ref[0] = _dot_tn(a_ref[...].astype(BF16), b_ref[...].astype(BF16)).astype(BF16)

    return _call(name, body, (N_DEV,), [a_spec, b_spec], [pl.BlockSpec((1, m, n), lambda d: (d, 0, 0))],
                 [_sds((N_DEV, m, n), BF16)], (a, b), after=after)[0]


def _full(t, d):
    return pl.BlockSpec((t, d), lambda i: (0, 0), pipeline_mode=pl.Buffered(1))


def _cols(t, d):
    return pl.BlockSpec((t, d), lambda i: (0, i))


def _head(t, d):
    return pl.BlockSpec((None, t, d), lambda i: (i, 0, 0))


def _local_step(x, pos, target, inv_freq, wg, sm, shards=None):
    t = x.shape[0]
    wg = dict(wg)
    dist = shards is not None
    mix_g = [sm["norm_mix_g"][l:l + 1] for l in range(2)]
    mlp_g = [sm["norm_mlp_g"][l:l + 1] for l in range(2)]

    ids = iter(range(2, 2 + 10))

    def gather(names):
        if dist:
            got = _by_sequencer("gather_" + names[0], _gather_comm([shards[k] for k in names]),
                                SIBLING_AND_NEIGHBOURS, next(ids))
            wg.update(zip(names, got))

    def send(name, names):
        if dist:
            comm = _exchange_comm(grads=[g[k] for k in names])
            g.update(zip(names, _by_sequencer("exchange_" + name, comm, EVERYONE, next(ids))))

    def send_sums(name, names, meanwhile):
        if not dist:
            meanwhile()
            return ()
        grads = [g[k] for k in names]
        landed = _by_sequencer("pair_exchange_" + name, _pair_exchange_comm(grads), (1,), next(ids))
        sums = _pair_add("pair_add_" + name, grads, landed, after=meanwhile())
        g.update(zip(names, _by_sequencer("exchange_" + name, _chip_exchange_comm(sums), OTHER_CHIPS, next(ids))))
        return sums

    def a_args():
        return (wg["a_w_in"], wg["a_ln_v_g"], wg["a_ln_v_b"], sm["a_w_s"], sm["a_b_st"], wg["a_w_out"])

    def kvq_w():
        return (sm["kv_src_norm_g"], wg["kv_w_a"], sm["kv_a_norm_g"], wg["kv_w_b"], mix_g[1], wg["b_w_q_a"],
                sm["b_q_norm_g"], wg["b_w_q_b"])

    gather(("mlp_w1_0", "mlp_w2_0"))
    h1, z, gated = _a_mix_fwd(x, mix_g[0], *a_args())
    gather(("kv_w_a", "kv_w_b", "b_w_q_a", "b_w_q_b", "b_w_o"))
    h2, a0 = _mlp_fwd(h1, mlp_g[0], wg["mlp_w1_0"], wg["mlp_w2_0"])
    if dist:
        wg["b_w_q_a"] = wg["b_w_q_a"].reshape(D_MODEL, Q_LORA)
        wg["kv_w_a"] = wg["kv_w_a"].reshape(D_MODEL, KV_LORA + QK_ROPE)
    gather(("mlp_w1_1", "mlp_w2_1"))
    ckv, k, v, cqpre, q, cos, sin = _kvq_fwd(h2, pos, inv_freq, kvq_w())
    h3, att = _attn_fwd(h2, q, k, v, wg["b_w_o"])
    a1, loss, dh4, d_final_g = _mlp_fwd_loss(h3, mlp_g[1], wg["mlp_w1_1"], wg["mlp_w2_1"], sm["final_norm_g"], target)

    g = {}
    dh3, d_mlp_g1, hn, f, da, dh3_b = _mlp_bwd(h3, a1, dh4, mlp_g[1], wg["mlp_w1_1"], wg["mlp_w2_1"], 1)
    dq, dk, dv = _attn_bwd(dh3_b, q, k, v, wg["b_w_o"], cos, sin)
    g["mlp_w1_1"] = _wgrad("wgrad_w1_1", hn, da, _full(t, D_MODEL), _cols(t, FF_SLOT), D_MODEL, FF_SLOT, after=[dq])
    g["mlp_w2_1"] = _wgrad("wgrad_w2_1", f, dh4, _cols(t, FF_SLOT), _full(t, D_MODEL), FF_SLOT, D_MODEL)

    def wgrad_w_o():
        g["b_w_o"] = _wgrad("wgrad_w_o", att, dh3_b, _head(t, V_HEAD), _full(t, D_MODEL), V_HEAD, D_MODEL)
        return [g["b_w_o"]]

    sums = send_sums("mlp_1", ("mlp_w1_1", "mlp_w2_1"), wgrad_w_o)
    dh2, d_mix_g1, d_src_g, d_q_g, d_kv_a_g, g_q_a, g["b_w_q_b"], g_kv_a, g["kv_w_b"] = _kvq_bwd(
        h2, dh3, ckv, cqpre, dq, dk, dv, cos, sin, kvq_w(), after=sums)
    g["b_w_q_a"] = g_q_a.reshape(N_DEV, D_MODEL // N_DEV, Q_LORA)
    g["kv_w_a"] = g_kv_a.reshape(N_DEV, D_MODEL // N_DEV, KV_LORA + QK_ROPE)
    qkv = ("b_w_q_a", "b_w_q_b", "kv_w_a", "kv_w_b")
    landed = [g[k] for k in qkv]
    send("qkv", qkv)
    dh1, d_mlp_g0, hn, f, da, dh1_b = _mlp_bwd(h1, a0, dh2, mlp_g[0], wg["mlp_w1_0"], wg["mlp_w2_0"], 0,
                                               after=landed if dist else ())
    landed = [g["mlp_w1_1"], g["mlp_w2_1"]] if dist else ()
    g["mlp_w1_0"] = _wgrad("wgrad_w1_0", hn, da, _full(t, D_MODEL), _cols(t, FF_SLOT), D_MODEL, FF_SLOT, after=landed)
    g["mlp_w2_0"] = _wgrad("wgrad_w2_0", f, dh2, _cols(t, FF_SLOT), _full(t, D_MODEL), FF_SLOT, D_MODEL)

    def wgrad_a_w_out():
        g["a_w_out"] = _wgrad("wgrad_a_w_out", gated, dh1_b, _cols(t, GATE_DIM // N_DEV), _full(t, D_MODEL),
                              GATE_DIM // N_DEV, D_MODEL)
        return [g["a_w_out"]] + [g[k] for k in qkv]

    sums = send_sums("mlp_0", ("mlp_w1_0", "mlp_w2_0", "b_w_o"), wgrad_a_w_out)
    if dist:
        sums = _pair_reduce("pair_reduce_a_w_out", [g["a_w_out"]], after=sums)
        g["a_w_out"], = _by_sequencer("exchange_a_w_out", _chip_exchange_comm(sums), OTHER_CHIPS, next(ids))
    dx, hn, dz, d_mix_g0, d_ln_g, d_ln_b, d_ws, d_bs = _a_mix_bwd(x, z, dh1, mix_g[0], *a_args(), after=sums)
    small = {
        "norm_mix_g": jnp.concatenate([d_mix_g0, d_mix_g1], axis=0),
        "norm_mlp_g": jnp.concatenate([d_mlp_g0, d_mlp_g1], axis=0),
        "a_ln_v_g": d_ln_g.reshape(N_DEV, GATE_DIM // N_DEV),
        "a_ln_v_b": d_ln_b.reshape(N_DEV, GATE_DIM // N_DEV),
        "a_w_s": d_ws.astype(BF16) if dist else d_ws,
        "a_b_s": d_bs.reshape(A_GROUPS, GMLP_BLOCK),
        "b_q_norm_g": d_q_g,
        "kv_src_norm_g": d_src_g,
        "kv_a_norm_g": d_kv_a_g,
        "final_norm_g": d_final_g,
    }
    if dist:
        parts = [small[k].reshape((1,) + small[k].shape) for k in SMALL] + [loss.reshape(1, 1, 1)]
        got = _by_sequencer("gather_small", _gather_comm(parts), SIBLING_AND_NEIGHBOURS, next(ids))
        small, loss = dict(zip(SMALL, got)), got[-1]
    g["a_w_in"] = _wgrad("wgrad_a_w_in", hn, dz, _full(t, D_MODEL), _cols(t, FF_SLOT), D_MODEL, FF_SLOT)
    return loss, dx, g, small


def _adamw(w, g, m, v):
    m = ADAM_B1 * m + (1.0 - ADAM_B1) * g
    v = ADAM_B2 * v + (1.0 - ADAM_B2) * (g * g)
    m_hat = m / (1.0 - ADAM_B1 ** ADAM_STEP)
    v_hat = v / (1.0 - ADAM_B2 ** ADAM_STEP)
    return -ADAM_LR * (m_hat / (jnp.sqrt(v_hat) + ADAM_EPS) + ADAM_WD * w), m, v


def _sum_in_device_order(r_ref):
    g = r_ref[0].astype(F32)
    for j in range(1, r_ref.shape[0]):
        g = g + r_ref[j].astype(F32)
    return g


def _adamw_sharded(name, recvs, w, m, v, swapped=False):
    layers, r, c = w.shape[0], *recvs[0][0].shape[1:]
    tr = r if swapped else math.gcd(r, 512)
    flat = [a for per_layer in recvs for a in per_layer]

    def body(*refs):
        r_refs, (w_ref, m_ref, v_ref) = refs[:len(flat)], refs[len(flat):len(flat) + 3]
        g_ref, d_ref, nm_ref, nv_ref = refs[-4:]
        layer = pl.program_id(0)
        g, pos = None, 0
        for li, per_layer in enumerate(recvs):
            total = None
            for ref in r_refs[pos:pos + len(per_layer)]:
                part = _sum_in_device_order(ref)
                total = part if total is None else total + part
            pos += len(per_layer)
            g = total if g is None else jnp.where(layer == li, total, g)
        if swapped:
            g = g.T
        g_ref[...] = g
        d_ref[...], nm_ref[...], nv_ref[...] = _adamw(w_ref[...], g, m_ref[...], v_ref[...])

    blk = pl.BlockSpec((None, tr, c), lambda l, i: (l, i, 0))
    if swapped:
        blk = pl.BlockSpec((None, c, r), lambda l, i: (l, 0, 0))
    return _call(name, body, (layers, r // tr),
                 [pl.BlockSpec((a.shape[0], tr, c), lambda l, i: (0, i, 0)) for a in flat] + [blk] * 3,
                 [blk] * 4, [_sds(w.shape, F32)] * 4, (*flat, w, m, v))


def _adamw_small(recvs, ws, ms, vs, own_row, losses):
    n = len(recvs)

    def body(*refs):
        r_refs, w_refs, m_refs, v_refs = (refs[i * n:(i + 1) * n] for i in range(4))
        outs, scr = refs[4 * n + 1:8 * n + 2], refs[8 * n + 2:]
        outs[-1][...] = _sum_in_device_order(refs[4 * n])
        me = _my_place()[3]
        for a in range(n):
            g = _sum_in_device_order(r_refs[a])
            if own_row[a]:
                scr[0][...] = g
                g = scr[0][pl.ds(me, 1), :]
            g_ref, d_ref, nm_ref, nv_ref = outs[4 * a:4 * a + 4]
            g_ref[...] = g
            d_ref[...], nm_ref[...], nv_ref[...] = _adamw(w_refs[a][...], g, m_refs[a][...], v_refs[a][...])

    out_shape = []
    for w in ws:
        out_shape += [_sds(w.shape, F32)] * 4
    return pl.pallas_call(
        body, name="adamw_small", in_specs=[VMEM] * (4 * n + 1), out_specs=[VMEM] * (4 * n + 1),
        out_shape=out_shape + [_sds((1, 1), F32)], scratch_shapes=[pltpu.VMEM((N_DEV, GATE_DIM // N_DEV), F32)],
    )(*recvs, *ws, *ms, *vs, losses)


BIG = ("a_w_in", "a_w_out", "b_w_q_a", "b_w_q_b", "b_w_o", "kv_w_a", "kv_w_b", "mlp_w1", "mlp_w2")
SMALL = ("norm_mix_g", "norm_mlp_g", "a_ln_v_g", "a_ln_v_b", "a_w_s", "a_b_s", "b_q_norm_g", "kv_src_norm_g",
         "kv_a_norm_g", "final_norm_g")
WEIGHTS = ("norm_mix_g", "norm_mlp_g", "a_w_in", "a_ln_v_g", "a_ln_v_b", "a_w_s", "a_b_s", "a_w_out", "b_w_q_a",
           "b_q_norm_g", "b_w_q_b", "b_w_o", "kv_src_norm_g", "kv_w_a", "kv_a_norm_g", "kv_w_b", "mlp_w1", "mlp_w2",
           "final_norm_g")


def _two_d(name, a):
    if name in ("a_w_s", "a_b_s"):
        return a.reshape(a.shape[1:])
    return a.reshape(1, -1) if a.ndim == 1 else a


def _three_d(a):
    return a if a.ndim == 3 else a.reshape((1,) + a.shape)


SWAPPED = ("b_w_q_b", "kv_w_a")


def _swapped(a):
    return jnp.swapaxes(_three_d(a), 1, 2)


def kernel(x, positions, norm_mix_g, norm_mlp_g, a_w_in, a_ln_v_g, a_ln_v_b, a_w_s, a_b_s, a_w_out, b_w_q_a, b_q_norm_g, b_w_q_b, b_w_o, kv_src_norm_g, kv_w_a, kv_a_norm_g, kv_w_b, mlp_w1, mlp_w2, final_norm_g, loss_target, m_norm_mix_g, m_norm_mlp_g, m_a_w_in, m_a_ln_v_g, m_a_ln_v_b, m_a_w_s, m_a_b_s, m_a_w_out, m_b_w_q_a, m_b_q_norm_g, m_b_w_q_b, m_b_w_o, m_kv_src_norm_g, m_kv_w_a, m_kv_a_norm_g, m_kv_w_b, m_mlp_w1, m_mlp_w2, m_final_norm_g, v_norm_mix_g, v_norm_mlp_g, v_a_w_in, v_a_ln_v_g, v_a_ln_v_b, v_a_w_s, v_a_b_s, v_a_w_out, v_b_w_q_a, v_b_q_norm_g, v_b_w_q_b, v_b_w_o, v_kv_src_norm_g, v_kv_w_a, v_kv_a_norm_g, v_kv_w_b, v_mlp_w1, v_mlp_w2, v_final_norm_g):
    w = dict(norm_mix_g=norm_mix_g, norm_mlp_g=norm_mlp_g, a_w_in=a_w_in, a_ln_v_g=a_ln_v_g, a_ln_v_b=a_ln_v_b,
             a_w_s=a_w_s, a_b_s=a_b_s, a_w_out=a_w_out, b_w_q_a=b_w_q_a, b_q_norm_g=b_q_norm_g, b_w_q_b=b_w_q_b,
             b_w_o=b_w_o, kv_src_norm_g=kv_src_norm_g, kv_w_a=kv_w_a, kv_a_norm_g=kv_a_norm_g, kv_w_b=kv_w_b,
             mlp_w1=mlp_w1, mlp_w2=mlp_w2, final_norm_g=final_norm_g)
    m = dict(norm_mix_g=m_norm_mix_g, norm_mlp_g=m_norm_mlp_g, a_w_in=m_a_w_in, a_ln_v_g=m_a_ln_v_g,
             a_ln_v_b=m_a_ln_v_b, a_w_s=m_a_w_s, a_b_s=m_a_b_s, a_w_out=m_a_w_out, b_w_q_a=m_b_w_q_a,
             b_q_norm_g=m_b_q_norm_g, b_w_q_b=m_b_w_q_b, b_w_o=m_b_w_o, kv_src_norm_g=m_kv_src_norm_g,
             kv_w_a=m_kv_w_a, kv_a_norm_g=m_kv_a_norm_g, kv_w_b=m_kv_w_b, mlp_w1=m_mlp_w1, mlp_w2=m_mlp_w2,
             final_norm_g=m_final_norm_g)
    v = dict(norm_mix_g=v_norm_mix_g, norm_mlp_g=v_norm_mlp_g, a_w_in=v_a_w_in, a_ln_v_g=v_a_ln_v_g,
             a_ln_v_b=v_a_ln_v_b, a_w_s=v_a_w_s, a_b_s=v_a_b_s, a_w_out=v_a_w_out, b_w_q_a=v_b_w_q_a,
             b_q_norm_g=v_b_q_norm_g, b_w_q_b=v_b_w_q_b, b_w_o=v_b_w_o, kv_src_norm_g=v_kv_src_norm_g,
             kv_w_a=v_kv_w_a, kv_a_norm_g=v_kv_a_norm_g, kv_w_b=v_kv_w_b, mlp_w1=v_mlp_w1, mlp_w2=v_mlp_w2,
             final_norm_g=v_final_norm_g)
    t = x.shape[1]

    first = ("a_w_in", "a_w_out", "a_ln_v_g", "a_ln_v_b")
    later = ("mlp_w1", "mlp_w2", "kv_w_a", "kv_w_b", "b_w_q_a", "b_w_q_b", "b_w_o")
    later_blocks = ("mlp_w1_0", "mlp_w1_1", "mlp_w2_0", "mlp_w2_1") + later[2:]
    got, casts = _gather_first([_three_d(w[k]) if k in BIG else w[k] for k in first],
                               [_swapped(w[k]) if k in SWAPPED else _three_d(w[k]) for k in later],
                               [k in SWAPPED for k in later])
    wg = dict(zip(first, got))
    wg["a_w_out"] = wg["a_w_out"].reshape(GATE_DIM, D_MODEL)
    wg["a_ln_v_g"] = wg["a_ln_v_g"].reshape(1, GATE_DIM)
    wg["a_ln_v_b"] = wg["a_ln_v_b"].reshape(1, GATE_DIM)
    shards = dict(zip(later_blocks, casts))

    sm = {k: _two_d(k, w[k]) for k in SMALL if k not in ("a_ln_v_g", "a_ln_v_b")}
    sm["a_b_st"] = sm["a_b_s"].T
    inv_freq = (ROPE_THETA ** (-jnp.arange(0, QK_ROPE, 2, dtype=F32) / QK_ROPE)).reshape(1, QK_ROPE // 2)

    losses, dx, g, small = _local_step(x[0], positions.reshape(t, 1), loss_target[0], inv_freq, wg, sm, shards)

    sums = _pair_reduce("pair_reduce_a_w_in", [g["a_w_in"]], after=[g["mlp_w1_0"], g["mlp_w2_0"]])
    g["a_w_in"], = _by_sequencer("exchange_last", _chip_exchange_comm(sums), OTHER_CHIPS, collective_id=1)

    out = {}
    for k in BIG:
        recvs = [[g[k + "_0"]], [g[k + "_1"]]] if k.startswith("mlp") else [[g[k]]]
        view = _swapped if k in SWAPPED else _three_d
        res = _adamw_sharded("adamw_" + k, recvs, view(w[k]), view(m[k]), view(v[k]), swapped=k in SWAPPED)
        out[k] = [view(o).reshape(w[k].shape) for o in res]
    own_row = [k in ("a_ln_v_g", "a_ln_v_b") for k in SMALL]
    res = _adamw_small([small[k] for k in SMALL], [_two_d(k, w[k]) for k in SMALL], [_two_d(k, m[k]) for k in SMALL],
                       [_two_d(k, v[k]) for k in SMALL], own_row, losses)
    for i, k in enumerate(SMALL):
        out[k] = [o.reshape(w[k].shape) for o in res[4 * i:4 * i + 4]]

    return (res[-1].reshape(()), dx.reshape(x.shape), *[out[k][0] for k in WEIGHTS], *[out[k][1] for k in WEIGHTS],
            *[out[k][2] for k in WEIGHTS], *[out[k][3] for k in WEIGHTS])
```

```python
import math

import jax
import jax.numpy as jnp
from jax import lax
from jax.experimental import pallas as pl
from jax.experimental.pallas import tpu as pltpu
from jax.experimental.pallas import tpu_sc as plsc

F32, BF16 = jnp.float32, jnp.bfloat16
MESH = pl.DeviceIdType.MESH
ANY = pl.BlockSpec(memory_space=pl.ANY)
VMEM = pl.BlockSpec(memory_space=pltpu.VMEM)

N_DEV = 8
D_MODEL = 1024
CHUNK = 64
GMLP_BLOCK = 128
GATE_DIM = 2048
A_GROUPS = 8
A_GROUP_DIM = GATE_DIM // A_GROUPS
B_HEADS = 8
QK_NOPE, QK_ROPE, V_HEAD = 128, 64, 128
Q_LORA, KV_LORA = 384, 256
ROPE_THETA = 10000.0
D_FF = 4096
FF_SLOT = D_FF // N_DEV
EPS = 1e-6
ATT_SCALE = (QK_NOPE + QK_ROPE) ** -0.5

ADAM_LR, ADAM_B1, ADAM_B2, ADAM_EPS, ADAM_WD, ADAM_STEP = 0.001, 0.9, 0.999, 1e-08, 0.01, 10

TM = 256
TM_GATE = 256
TM_MLP_FWD = 512
TM_KVQ = 512
VMEM_LIMIT = 56 * 1024 * 1024
INV_SQRT2 = 1.0 / math.sqrt(2.0)
INV_SQRT_2PI = 1.0 / math.sqrt(2.0 * math.pi)
LOG2_E = 1.0 / math.log(2.0)
HEADS_PER_STEP = 2


def _dot(a, b):
    return jnp.dot(a, b, preferred_element_type=F32)


def _dot_nt(a, b):
    return lax.dot_general(a, b, (((1,), (1,)), ((), ())), preferred_element_type=F32)


def _dot_tn(a, b):
    return lax.dot_general(a, b, (((0,), (0,)), ((), ())), preferred_element_type=F32)


def _rms_fwd(x, g):
    rstd = lax.rsqrt(jnp.mean(x * x, axis=-1, keepdims=True) + EPS)
    xhat = x * rstd
    return xhat * g, xhat, rstd


def _rms_bwd(dy, xhat, rstd, g):
    dxhat = dy * g
    dx = rstd * (dxhat - xhat * jnp.mean(dxhat * xhat, axis=-1, keepdims=True))
    return dx, jnp.sum(dy * xhat, axis=0, keepdims=True)


def _ln_fwd(v, g, b):
    mu = jnp.mean(v, axis=-1, keepdims=True)
    vc = v - mu
    rstd = lax.rsqrt(jnp.mean(vc * vc, axis=-1, keepdims=True) + EPS)
    vhat = vc * rstd
    return vhat * g + b, vhat, rstd


def _gelu(x):
    return 0.5 * x * (1.0 + lax.erf(x * INV_SQRT2))


def _gelu_and_grad(x):
    cdf = 0.5 * (1.0 + lax.erf(x * INV_SQRT2))
    return x * cdf, cdf + x * jnp.exp(-0.5 * x * x) * INV_SQRT_2PI


def _rope(x, cos, sin):
    x1, x2 = x[:, :QK_ROPE // 2], x[:, QK_ROPE // 2:]
    return jnp.concatenate([x1 * cos - x2 * sin, x2 * cos + x1 * sin], axis=-1)


def _gate_mask():
    row = lax.broadcasted_iota(jnp.int32, (GMLP_BLOCK, GMLP_BLOCK), 0)
    col = lax.broadcasted_iota(jnp.int32, (GMLP_BLOCK, GMLP_BLOCK), 1)
    return (col < CHUNK) | (row >= CHUNK)


def _att_mask(q0, tq, t):
    q = q0 + lax.broadcasted_iota(jnp.int32, (tq, t), 0)
    k = lax.broadcasted_iota(jnp.int32, (tq, t), 1)
    return jnp.right_shift(k, 6) <= jnp.right_shift(q, 6)


def _res(shape, imap=None):
    zeros = (0,) * len(shape)
    return pl.BlockSpec(shape, imap or (lambda i: zeros), pipeline_mode=pl.Buffered(1))


def _const(shape):
    zeros = (0,) * len(shape)
    return pl.BlockSpec(shape, lambda i: zeros)


def _row(d, tm=TM):
    return pl.BlockSpec((tm, d), lambda i: (i, 0))


def _heads(d, tm=TM):
    return pl.BlockSpec((B_HEADS, tm, d), lambda i: (0, i, 0))


def _sds(shape, dt):
    return jax.ShapeDtypeStruct(shape, dt)


def _acc(ref, val):
    @pl.when(pl.program_id(0) == 0)
    def _():
        ref[...] = jnp.zeros_like(ref)
    ref[...] += val


def _my_place():
    x, y, c = lax.axis_index("x"), lax.axis_index("y"), lax.axis_index("c")
    return x, y, c, 4 * x + 2 * y + c


def _peer(x, y, c, k):
    px = 1 - x if k & 4 else x
    py = 1 - y if k & 2 else y
    pc = 1 - c if k & 1 else c
    return (px, py, pc), 4 * px + 2 * py + pc


CHIPS = (2, 4, 6)


def _splits(ref):
    return len(ref.shape) >= 3 and ref.shape[1] % 32 == 0


def _piece(ref, block, half=None):
    if half is None or not _splits(ref):
        return ref.at[pl.ds(block, 1)]
    rows = ref.shape[1] // 2
    return ref.at[pl.ds(block, 1), pl.ds(half * rows, rows)]


def _gather_copy(sems, a, k, piece, to, src=None):
    return pltpu.make_async_remote_copy(
        src_ref=piece if src is None else src, dst_ref=piece, send_sem=sems[0].at[a, k], recv_sem=sems[1].at[a, k],
        device_id=to, device_id_type=MESH)


def _gather_start(srcs, outs, sems, only=None):
    x, y, c, me = _my_place()
    for a in range(len(srcs)) if only is None else (only,):
        mine = _piece(outs[a], me)
        pltpu.make_async_copy(srcs[a], mine, sems[2].at[a]).start()
        for k, rel in enumerate((1, 4, 2)):
            _gather_copy(sems, a, k, mine, _peer(x, y, c, rel)[0], src=srcs[a]).start()


def _gather_relay(srcs, outs, sems):
    x, y, c, _ = _my_place()
    sib = _peer(x, y, c, 1)[0]
    (xn, xn_i), (yn, yn_i) = _peer(x, y, c, 4), _peer(x, y, c, 2)
    for a in range(len(srcs)):
        out = outs[a]
        _gather_copy(sems, a, 1, _piece(out, xn_i), xn).wait_recv()
        _gather_copy(sems, a, 3, _piece(out, xn_i, 0), yn).start()
        _gather_copy(sems, a, 5, _piece(out, xn_i), sib).start()
        _gather_copy(sems, a, 2, _piece(out, yn_i), yn).wait_recv()
        if _splits(out):
            _gather_copy(sems, a, 4, _piece(out, yn_i, 1), xn).start()
        _gather_copy(sems, a, 6, _piece(out, yn_i), sib).start()


def _gather_finish(srcs, outs, sems):
    x, y, c, me = _my_place()
    sib = _peer(x, y, c, 1)[0]
    xn, yn, dg_i = _peer(x, y, c, 4)[0], _peer(x, y, c, 2)[0], _peer(x, y, c, 6)[1]
    n = len(srcs)
    for a in range(n):
        out = outs[a]
        _gather_copy(sems, a, 3, _piece(out, dg_i, 0), yn).wait_recv()
        _gather_copy(sems, a, 7, _piece(out, dg_i, 0), sib).start()
        if _splits(out):
            _gather_copy(sems, a, 4, _piece(out, dg_i, 1), xn).wait_recv()
            _gather_copy(sems, a, 8, _piece(out, dg_i, 1), sib).start()
    for a in range(n):
        out = outs[a]
        whole, half = _piece(out, me), _piece(out, me, 0)
        for k in (0, 5, 6):
            _gather_copy(sems, a, k, whole, sib).wait_recv()
        for k in (7, 8) if _splits(out) else (7,):
            _gather_copy(sems, a, k, half, sib).wait_recv()
        for k in (0, 1, 2):
            _gather_copy(sems, a, k, whole, sib, src=srcs[a]).wait_send()
        for k in (5, 6):
            _gather_copy(sems, a, k, whole, sib).wait_send()
        for k in (3, 4, 7, 8) if _splits(out) else (3, 7):
            _gather_copy(sems, a, k, half, sib).wait_send()
        pltpu.make_async_copy(srcs[a], whole, sems[2].at[a]).wait()


def _relay_sems(n):
    return [pltpu.SemaphoreType.DMA((n, 9)), pltpu.SemaphoreType.DMA((n, 9)), pltpu.SemaphoreType.DMA((n,))]


def _gather_sems(n):
    return [pltpu.SemaphoreType.DMA((n, 7)), pltpu.SemaphoreType.DMA((n, 7)), pltpu.SemaphoreType.DMA((n,))]


class _Comm:
    def __init__(self, args, out_shape, scratch, start, finish, relay=None):
        self.args, self.out_shape, self.scratch, self.start, self.finish = args, out_shape, scratch, start, finish
        self.relay = relay


def _gather_comm(shards):
    return _Comm(list(shards), [_sds((N_DEV,) + s.shape[1:], s.dtype) for s in shards], _relay_sems(len(shards)),
                 _gather_start, _gather_finish, relay=_gather_relay)


def _together(big, small):
    na, no, ns = len(big.args), len(big.out_shape), len(big.scratch)

    def start(src, dst, sems):
        small.start(src[na:], dst[no:], sems[ns:])
        big.start(src[:na], dst[:no], sems[:ns])

    def relay(src, dst, sems):
        small.relay(src[na:], dst[no:], sems[ns:])

    def finish(src, dst, sems):
        small.finish(src[na:], dst[no:], sems[ns:])
        big.finish(src[:na], dst[:no], sems[:ns])

    assert big.relay is None and small.relay is not None
    return _Comm(big.args + small.args, big.out_shape + small.out_shape, list(big.scratch) + list(small.scratch),
                 start, finish, relay=relay)


def _direct_copies(ins, outs, sems, wait):
    send_sems, recv_sems, local_sems = sems
    x, y, c, me = _my_place()
    for a in range(len(ins)):
        local = pltpu.make_async_copy(ins[a].at[pl.ds(me, 1)], outs[a].at[pl.ds(me, 1)], local_sems.at[a])
        local.wait() if wait else local.start()
        for k in range(1, N_DEV):
            to, to_i = _peer(x, y, c, k)
            cp = pltpu.make_async_remote_copy(
                src_ref=ins[a].at[pl.ds(to_i, 1)], dst_ref=outs[a].at[pl.ds(me, 1)],
                send_sem=send_sems.at[a, k - 1], recv_sem=recv_sems.at[a, k - 1], device_id=to, device_id_type=MESH)
            cp.wait() if wait else cp.start()


def _exchange_comm(grads):
    return _Comm(list(grads), [_sds(g.shape, g.dtype) for g in grads], _gather_sems(len(grads)),
                 lambda i, o, s: _direct_copies(i, o, s, False), lambda i, o, s: _direct_copies(i, o, s, True))


def _chip_copies(ins, outs, sems, wait):
    send_sems, recv_sems, local_sems = sems
    x, y, c, _ = _my_place()
    for a in range(len(ins)):
        local = pltpu.make_async_copy(ins[a].at[pl.ds(2 * x + y, 1)], outs[a].at[pl.ds(len(CHIPS), 1)],
                                      local_sems.at[a])
        local.wait() if wait else local.start()
        for i, k in enumerate(CHIPS):
            to = _peer(x, y, c, k)[0]
            cp = pltpu.make_async_remote_copy(
                src_ref=ins[a].at[pl.ds(2 * to[0] + to[1], 1)], dst_ref=outs[a].at[pl.ds(i, 1)],
                send_sem=send_sems.at[a, i], recv_sem=recv_sems.at[a, i], device_id=to, device_id_type=MESH)
            cp.wait() if wait else cp.start()


def _chip_exchange_comm(sums):
    n = len(sums)
    sems = [pltpu.SemaphoreType.DMA((n, len(CHIPS))), pltpu.SemaphoreType.DMA((n, len(CHIPS))),
            pltpu.SemaphoreType.DMA((n,))]
    return _Comm(list(sums), [_sds(s.shape, s.dtype) for s in sums], sems,
                 lambda i, o, s: _chip_copies(i, o, s, False), lambda i, o, s: _chip_copies(i, o, s, True))


def _pair_reduce(name, grads, after=()):
    n = len(grads)
    n_chips = N_DEV // 2

    def body(*refs):
        g_refs, gh_refs, refs = refs[:n], refs[n:2 * n], refs[2 * n + len(after):]
        p_refs, land = refs[:n], refs[n:2 * n]
        send_sems, recv_sems = refs[2 * n:]
        x, y, c, _ = _my_place()
        sib = _peer(x, y, c, 1)[0]
        q = pl.program_id(0)

        def to_sibling(a, j):
            return pltpu.make_async_remote_copy(
                src_ref=gh_refs[a].at[j, pl.ds(1 - c, 1)], dst_ref=land[a].at[pl.ds(j, 1)],
                send_sem=send_sems.at[a, j], recv_sem=recv_sems.at[a, j], device_id=sib, device_id_type=MESH)

        @pl.when(q == 0)
        def _():
            for j in range(n_chips):
                for a in range(n):
                    to_sibling(a, j).start()

        for a in range(n):
            to_sibling(a, q).wait_recv()
            p_refs[a][...] = (g_refs[a][0, pl.ds(c, 1)].astype(F32) + land[a][pl.ds(q, 1)].astype(F32)).astype(BF16)

        @pl.when(q == n_chips - 1)
        def _():
            for a in range(n):
                for j in range(n_chips):
                    to_sibling(a, j).wait_send()

    views = [g.reshape((n_chips, 2) + g.shape[1:]) for g in grads]
    res = pl.pallas_call(
        body, name=name, grid=(n_chips,),
        in_specs=[pl.BlockSpec((1, 2) + g.shape[1:], lambda q: (q, 0, 0, 0)) for g in grads]
        + [ANY] * (n + len(after)),
        out_specs=[pl.BlockSpec((1,) + g.shape[1:], lambda q: (q, 0, 0)) for g in grads],
        out_shape=[_sds((n_chips,) + g.shape[1:], BF16) for g in grads],
        scratch_shapes=[pltpu.VMEM((n_chips,) + g.shape[1:], BF16) for g in grads]
        + [pltpu.SemaphoreType.DMA((n, n_chips)), pltpu.SemaphoreType.DMA((n, n_chips))],
        compiler_params=pltpu.CompilerParams(dimension_semantics=("arbitrary",), vmem_limit_bytes=VMEM_LIMIT),
    )(*views, *views, *after)
    return list(res)


def _pair_exchange_comm(grads):
    n, n_chips = len(grads), N_DEV // 2

    def copies(ins, outs, sems, wait):
        x, y, c, _ = _my_place()
        for j in range(n_chips):
            for a in range(n):
                cp = pltpu.make_async_remote_copy(
                    src_ref=ins[a].at[j, pl.ds(1 - c, 1)], dst_ref=outs[a].at[pl.ds(j, 1)], send_sem=sems[0].at[a, j],
                    recv_sem=sems[1].at[a, j], device_id=_peer(x, y, c, 1)[0], device_id_type=MESH)
                cp.wait() if wait else cp.start()

    views = [g.reshape((n_chips, 2) + g.shape[1:]) for g in grads]
    sems = [pltpu.SemaphoreType.DMA((n, n_chips)), pltpu.SemaphoreType.DMA((n, n_chips))]
    return _Comm(views, [_sds((n_chips,) + g.shape[1:], g.dtype) for g in grads], sems,
                 lambda i, o, s: copies(i, o, s, False), lambda i, o, s: copies(i, o, s, True))


def _pair_add(name, grads, landed, after=()):
    n, n_chips = len(grads), N_DEV // 2

    def body(core_ref, *refs):
        g_refs, l_refs, p_refs = refs[:n], refs[n:2 * n], refs[2 * n + len(after):]
        for a in range(n):
            p_refs[a][...] = (g_refs[a][...].astype(F32) + l_refs[a][...].astype(F32)).astype(BF16)

    views = [g.reshape((n_chips, 2) + g.shape[1:]) for g in grads]
    blocks = [pl.BlockSpec((1,) + g.shape[1:], lambda q, core: (q, 0, 0)) for g in grads]
    mine = [pl.BlockSpec((1, None) + g.shape[1:], lambda q, core: (q, core[0], 0, 0)) for g in grads]
    return list(pl.pallas_call(
        body, name=name, out_shape=[_sds((n_chips,) + g.shape[1:], BF16) for g in grads],
        grid_spec=pltpu.PrefetchScalarGridSpec(num_scalar_prefetch=1, grid=(n_chips,),
                                               in_specs=mine + blocks + [ANY] * len(after), out_specs=blocks),
        compiler_params=pltpu.CompilerParams(dimension_semantics=("arbitrary",), vmem_limit_bytes=VMEM_LIMIT),
    )(lax.axis_index("c").reshape(1), *views, *landed, *after))


def _call(name, body, grid, in_specs, out_specs, out_shape, args, scratch=(), after=()):
    ni, na = len(in_specs), len(after)

    def ordered(*refs):
        body(*refs[:ni], *refs[ni + na:])

    return list(pl.pallas_call(
        ordered if after else body, name=name, grid=grid, in_specs=list(in_specs) + [ANY] * na,
        out_specs=list(out_specs), out_shape=list(out_shape), scratch_shapes=list(scratch),
        compiler_params=pltpu.CompilerParams(dimension_semantics=("arbitrary",) * len(grid),
                                             vmem_limit_bytes=VMEM_LIMIT))(*args, *after))


SIBLING_AND_NEIGHBOURS, OTHER_CHIPS, EVERYONE = (1, 4, 2), CHIPS, tuple(range(1, N_DEV))


def _by_sequencer(name, comm, peers, collective_id):
    src = [jax.new_ref(a, memory_space=pltpu.MemorySpace.HBM) for a in comm.args]
    dst = [jax.empty_ref(s, memory_space=pltpu.MemorySpace.HBM) for s in comm.out_shape]

    @pl.kernel(mesh=plsc.ScalarSubcoreMesh(axis_name="sequencer", num_cores=1), name=name,
               scratch_types=tuple(comm.scratch), compiler_params=pltpu.CompilerParams(collective_id=collective_id))
    def launch(*sems):
        x, y, c, _ = _my_place()
        barrier = pltpu.get_barrier_semaphore()
        for k in peers:
            pl.semaphore_signal(barrier, inc=1, device_id=_peer(x, y, c, k)[0], device_id_type=MESH)
        pl.semaphore_wait(barrier, len(peers))
        comm.start(src, dst, sems)
        if comm.relay is not None:
            comm.relay(src, dst, sems)
        comm.finish(src, dst, sems)

    launch()
    return [d[...] for d in dst]


def _gather_first(first, later, swapped):
    nf = len(first)
    layer_of = [(a, l) for a, s in enumerate(later) for l in range(s.shape[0])]
    nl = len(layer_of)
    dts = [BF16] * (nf - 2) + [F32, F32]
    shard = [s.shape[:0:-1] if sw else s.shape[1:] for s, sw in zip(later, swapped)]

    def body(*refs):
        ins, refs = refs[:nf + len(later)], refs[nf + len(later):]
        outs, refs = refs[:nf], refs[nf:]
        casts, refs = refs[:nl], refs[nl:]
        stage, sems = refs[:nf], refs[nf:]
        for a in range(nf):
            stage[a][...] = ins[a][...].astype(dts[a])
            _gather_start(stage, outs, sems, only=a)
        for k, (a, l) in enumerate(layer_of):
            block = ins[nf + a][l]
            casts[k][0] = (block.T if swapped[a] else block).astype(BF16)
        _gather_relay(stage, outs, sems)
        _gather_finish(stage, outs, sems)

    res = pl.pallas_call(
        body, name="gather_first",
        in_specs=[VMEM] * (nf + len(later)), out_specs=[ANY] * nf + [VMEM] * nl,
        out_shape=[_sds((N_DEV,) + s.shape[1:], dt) for s, dt in zip(first, dts)]
        + [_sds((1,) + shard[a], BF16) for a, _ in layer_of],
        scratch_shapes=[pltpu.VMEM(s.shape, dt) for s, dt in zip(first, dts)] + _relay_sems(nf),
        compiler_params=pltpu.CompilerParams(vmem_limit_bytes=VMEM_LIMIT),
    )(*first, *later)
    return list(res[:nf]), list(res[nf:])


def _a_mix_fwd(x, g, w_in, ln_g, ln_b, w_s, b_st, w_out):
    t = x.shape[0]
    nblk = TM // GMLP_BLOCK

    def body(x_ref, g_ref, win_ref, lng_ref, lnb_ref, ws_ref, bst_ref, wout_ref, h_ref, z_ref, gated_scr):
        xv = x_ref[...]
        hb = _rms_fwd(xv, g_ref[...])[0].astype(BF16)
        for d in range(N_DEV):
            z_ref[:, d * FF_SLOT:(d + 1) * FF_SLOT] = _dot(hb, win_ref[d])
        u = _gelu(z_ref[:, :GATE_DIM])
        vb = _ln_fwd(_gelu(z_ref[:, GATE_DIM:]), lng_ref[...], lnb_ref[...])[0].astype(BF16)
        mask = _gate_mask()
        for gi in range(A_GROUPS):
            wm = jnp.where(mask, ws_ref[gi], 0.0).astype(BF16)
            bias = bst_ref[:, gi:gi + 1]
            cs = slice(gi * A_GROUP_DIM, (gi + 1) * A_GROUP_DIM)
            for n in range(nblk):
                rs = slice(n * GMLP_BLOCK, (n + 1) * GMLP_BLOCK)
                sv = _dot(wm, vb[rs, cs]) + bias
                gated_scr[rs, cs] = (u[rs, cs] * sv).astype(BF16)
        h_ref[...] = xv + _dot(gated_scr[...], wout_ref[...])

    return _call(
        "a_mix_fwd", body, (t // TM,),
        [_row(D_MODEL), _res((1, D_MODEL)), _res((N_DEV, D_MODEL, FF_SLOT)), _res((1, GATE_DIM)),
         _res((1, GATE_DIM)), _res((A_GROUPS, GMLP_BLOCK, GMLP_BLOCK)), _res((GMLP_BLOCK, A_GROUPS)),
         _res((GATE_DIM, D_MODEL))],
        [_row(D_MODEL), _row(2 * GATE_DIM), _row(GATE_DIM)],
        [_sds((t, D_MODEL), F32), _sds((t, 2 * GATE_DIM), F32), _sds((t, GATE_DIM), BF16)],
        (x, g, w_in, ln_g, ln_b, w_s, b_st, w_out))


MLP_W_SPECS = (_res((N_DEV, D_MODEL, FF_SLOT)), _res((N_DEV, FF_SLOT, D_MODEL)))


def _mlp_fwd(h, g, w1, w2):
    t = h.shape[0]

    def body(h_ref, g_ref, w1_ref, w2_ref, o_ref, a_ref):
        hv = h_ref[...]
        hb = _rms_fwd(hv, g_ref[...])[0].astype(BF16)
        o_ref[...] = hv
        for d in range(N_DEV):
            a = _dot(hb, w1_ref[d])
            a_ref[:, d * FF_SLOT:(d + 1) * FF_SLOT] = a
            r = jnp.maximum(a, 0.0)
            o_ref[...] += _dot((r * r).astype(BF16), w2_ref[d])

    return _call(
        "mlp_fwd", body, (t // TM_MLP_FWD,), [_row(D_MODEL, TM_MLP_FWD), _res((1, D_MODEL)), *MLP_W_SPECS],
        [_row(D_MODEL, TM_MLP_FWD), _row(D_FF, TM_MLP_FWD)], [_sds((t, D_MODEL), F32), _sds((t, D_FF), F32)],
        (h, g, w1, w2))


def _mlp_fwd_loss(h, g, w1, w2, final_g, target):
    t = h.shape[0]

    def body(h_ref, g_ref, w1_ref, w2_ref, fg_ref, t_ref, a_ref, loss_ref, dh_ref, dg_ref):
        hv = h_ref[...]
        hb = _rms_fwd(hv, g_ref[...])[0].astype(BF16)
        out = hv
        for d in range(N_DEV):
            a = _dot(hb, w1_ref[d])
            a_ref[:, d * FF_SLOT:(d + 1) * FF_SLOT] = a
            r = jnp.maximum(a, 0.0)
            out = out + _dot((r * r).astype(BF16), w2_ref[d])
        y, xhat, rstd = _rms_fwd(out, fg_ref[...])
        err = y - t_ref[...]
        part = 0.5 * jnp.sum(jnp.mean(err * err, axis=-1, keepdims=True), axis=0, keepdims=True)
        dx, dg = _rms_bwd(err * (1.0 / D_MODEL), xhat, rstd, fg_ref[...])
        dh_ref[...] = dx
        _acc(dg_ref, dg)
        _acc(loss_ref, part)

    return _call(
        "mlp_fwd_loss", body, (t // TM,),
        [_row(D_MODEL), _res((1, D_MODEL)), *MLP_W_SPECS, _res((1, D_MODEL)), _row(D_MODEL)],
        [_row(D_FF), _const((1, 1)), _row(D_MODEL), _const((1, D_MODEL))],
        [_sds((t, D_FF), F32), _sds((1, 1), F32), _sds((t, D_MODEL), F32), _sds((1, D_MODEL), F32)],
        (h, g, w1, w2, final_g, target))


KVQ_W_SPECS = (_res((1, D_MODEL)), _res((D_MODEL, KV_LORA + QK_ROPE)), _res((1, KV_LORA)),
               _res((B_HEADS, KV_LORA, QK_NOPE + V_HEAD)), _res((1, D_MODEL)), _res((D_MODEL, Q_LORA)),
               _res((1, Q_LORA)), _res((B_HEADS, Q_LORA, QK_NOPE + QK_ROPE)))


def _kvq_fwd(h, pos, inv_freq, kvq_w):
    t = h.shape[0]
    half = QK_ROPE // 2

    def body(h_ref, pos_ref, invf_ref, srcg_ref, wkva_ref, kvag_ref, wkvb_ref, mixg_ref, wqa_ref, qg_ref, wqb_ref,
             ckv_ref, k_ref, v_ref, cqpre_ref, q_ref, cos_ref, sin_ref):
        hv = h_ref[...]
        xhat = hv * lax.rsqrt(jnp.mean(hv * hv, axis=-1, keepdims=True) + EPS)
        ang = pos_ref[...].astype(F32) * invf_ref[...]
        cos, sin = jnp.cos(ang), jnp.sin(ang)
        cos_ref[...] = cos
        sin_ref[...] = sin
        ckv = _dot((xhat * srcg_ref[...]).astype(BF16), wkva_ref[...])
        ckv_ref[...] = ckv
        cb = _rms_fwd(ckv[:, :KV_LORA], kvag_ref[...])[0].astype(BF16)
        kpe = _rope(ckv[:, KV_LORA:], cos, sin).astype(BF16)
        for hd in range(B_HEADS):
            kv = _dot(cb, wkvb_ref[hd])
            k_ref[hd, :, 0:QK_NOPE] = kv[:, :QK_NOPE].astype(BF16)
            k_ref[hd, :, QK_NOPE:] = kpe
            v_ref[hd] = kv[:, QK_NOPE:].astype(BF16)
        cqpre = _dot((xhat * mixg_ref[...]).astype(BF16), wqa_ref[...])
        cqpre_ref[...] = cqpre
        cqb = _rms_fwd(cqpre, qg_ref[...])[0].astype(BF16)
        for hd in range(B_HEADS):
            q = _dot(cqb, wqb_ref[hd])
            q_ref[hd, :, 0:QK_NOPE] = q[:, :QK_NOPE].astype(BF16)
            q_ref[hd, :, QK_NOPE:] = _rope(q[:, QK_NOPE:], cos, sin).astype(BF16)

    tm = TM_KVQ
    return _call(
        "kvq_fwd", body, (t // tm,), [_row(D_MODEL, tm), _row(1, tm), _res((1, half)), *KVQ_W_SPECS],
        [_row(KV_LORA + QK_ROPE, tm), _heads(QK_NOPE + QK_ROPE, tm), _heads(V_HEAD, tm), _row(Q_LORA, tm),
         _heads(QK_NOPE + QK_ROPE, tm), _row(half, tm), _row(half, tm)],
        [_sds((t, KV_LORA + QK_ROPE), F32), _sds((B_HEADS, t, QK_NOPE + QK_ROPE), BF16),
         _sds((B_HEADS, t, V_HEAD), BF16), _sds((t, Q_LORA), F32), _sds((B_HEADS, t, QK_NOPE + QK_ROPE), BF16),
         _sds((t, half), F32), _sds((t, half), F32)],
        (h, pos, inv_freq, *kvq_w))


def _softmax_rows(q, k_ref, k):
    past, upto = k * TM, (k + 1) * TM
    s = _dot_nt(q, k_ref[0:upto, :])
    own = jnp.where(_att_mask(0, TM, TM), s[:, past:], jnp.finfo(F32).min)
    s = own if k == 0 else jnp.concatenate([s[:, :past], own], axis=1)
    e = jnp.exp2((s - jnp.max(s, axis=-1, keepdims=True)) * (ATT_SCALE * LOG2_E))
    return e * (1.0 / jnp.sum(e, axis=-1, keepdims=True))


def _for_my_tile(i, nq, fn):
    for k in range(nq):
        @pl.when(i == k)
        def _(k=k):
            fn(k)


def _attn_fwd(h, q, k, v, w_o):
    t = h.shape[0]
    nq, hps = t // TM, HEADS_PER_STEP

    def body(h_ref, q_ref, k_ref, v_ref, wo_ref, o_ref, att_ref):
        i, pair = pl.program_id(0), pl.program_id(1)

        @pl.when(pair == 0)
        def _():
            o_ref[...] = h_ref[...]

        def tile(kt):
            proj = None
            for j in range(hps):
                hd = pair * hps + j
                p = _softmax_rows(q_ref[j], k_ref.at[hd], kt)
                ob = _dot(p.astype(BF16), v_ref[hd, 0:(kt + 1) * TM, :]).astype(BF16)
                att_ref[j] = ob
                proj = _dot(ob, wo_ref[hd]) if proj is None else proj + _dot(ob, wo_ref[hd])
            o_ref[...] += proj

        _for_my_tile(i, nq, tile)

    def per_head(d):
        return pl.BlockSpec((hps, TM, d), lambda i, pair: (pair, i, 0))

    def resident(shape):
        zeros = (0,) * len(shape)
        return pl.BlockSpec(shape, lambda i, pair: zeros, pipeline_mode=pl.Buffered(1))

    tile_spec = pl.BlockSpec((TM, D_MODEL), lambda i, pair: (i, 0))
    return _call(
        "attn_fwd", body, (nq, B_HEADS // hps),
        [tile_spec, per_head(QK_NOPE + QK_ROPE), resident((B_HEADS, t, QK_NOPE + QK_ROPE)),
         resident((B_HEADS, t, V_HEAD)), resident((B_HEADS, V_HEAD, D_MODEL))],
        [tile_spec, per_head(V_HEAD)], [_sds((t, D_MODEL), F32), _sds((B_HEADS, t, V_HEAD), BF16)],
        (h, q, k, v, w_o))


def _mlp_bwd(h, a, dho, g, w1, w2, layer, after=()):
    t = h.shape[0]

    def body(h_ref, a_ref, dho_ref, g_ref, w1_ref, w2_ref, dhi_ref, dg_ref, hn_ref, f_ref, da_ref, dhib_ref):
        gv = g_ref[...]
        y, xhat, rstd = _rms_fwd(h_ref[...], gv)
        hn_ref[...] = y.astype(BF16)
        dho_v = dho_ref[...]
        dhob = dho_v.astype(BF16)
        dhn = jnp.zeros((TM, D_MODEL), F32)
        for d in range(N_DEV):
            cs = slice(d * FF_SLOT, (d + 1) * FF_SLOT)
            r = jnp.maximum(a_ref[:, cs], 0.0)
            f_ref[:, cs] = (r * r).astype(BF16)
            da = (_dot_nt(dhob, w2_ref[d]) * (2.0 * r)).astype(BF16)
            da_ref[:, cs] = da
            dhn = dhn + _dot_nt(da, w1_ref[d])
        dx, dg = _rms_bwd(dhn, xhat, rstd, gv)
        dhi = dho_v + dx
        dhi_ref[...] = dhi
        dhib_ref[...] = dhi.astype(BF16)
        _acc(dg_ref, dg)

    return _call(
        f"mlp_bwd_{layer}", body, (t // TM,),
        [_row(D_MODEL), _row(D_FF), _row(D_MODEL), _res((1, D_MODEL)), *MLP_W_SPECS],
        [_row(D_MODEL), _const((1, D_MODEL)), _row(D_MODEL), _row(D_FF), _row(D_FF), _row(D_MODEL)],
        [_sds((t, D_MODEL), F32), _sds((1, D_MODEL), F32), _sds((t, D_MODEL), BF16), _sds((t, D_FF), BF16),
         _sds((t, D_FF), BF16), _sds((t, D_MODEL), BF16)],
        (h, a, dho, g, w1, w2), after=after)


def _attn_bwd(dh, q, k, v, w_o, cos, sin, after=()):
    t = dh.shape[0]
    half, hps = QK_ROPE // 2, HEADS_PER_STEP

    def body(dh_ref, q_ref, k_ref, v_ref, wo_ref, cos_ref, sin_ref, dq_ref, dk_ref, dv_ref):
        i = pl.program_id(1)

        @pl.when(i == 0)
        def _():
            dk_ref[...] = jnp.zeros_like(dk_ref)
            dv_ref[...] = jnp.zeros_like(dv_ref)

        def tile(kt):
            keys = slice(0, (kt + 1) * TM)
            for j in range(hps):
                qj = q_ref[j]
                do = _dot_nt(dh_ref[kt * TM:(kt + 1) * TM, :], wo_ref[j]).astype(BF16)
                p = _softmax_rows(qj, k_ref.at[j], kt)
                dp = _dot_nt(do, v_ref[j, keys, :])
                ds = (p * (dp - jnp.sum(p * dp, axis=-1, keepdims=True)) * ATT_SCALE).astype(BF16)
                dq = _dot(ds, k_ref[j, keys, :])
                dq_ref[j, :, 0:QK_NOPE] = dq[:, :QK_NOPE].astype(BF16)
                dq_ref[j, :, QK_NOPE:] = _rope(dq[:, QK_NOPE:], cos_ref[...], -sin_ref[...]).astype(BF16)
                dk_ref[j, keys, :] += _dot_tn(ds, qj)
                dv_ref[j, keys, :] += _dot_tn(p.astype(BF16), do)

        _for_my_tile(i, t // TM, tile)

    def per_pair(rows, d, tiled):
        return pl.BlockSpec((hps, rows, d), (lambda pair, i: (pair, i, 0)) if tiled else (lambda pair, i: (pair, 0, 0)))

    def tile(d):
        return pl.BlockSpec((TM, d), lambda pair, i: (i, 0))

    return _call(
        "attn_bwd", body, (B_HEADS // hps, t // TM),
        [pl.BlockSpec((t, D_MODEL), lambda pair, i: (0, 0), pipeline_mode=pl.Buffered(1)),
         per_pair(TM, QK_NOPE + QK_ROPE, True), per_pair(t, QK_NOPE + QK_ROPE, False), per_pair(t, V_HEAD, False),
         per_pair(V_HEAD, D_MODEL, False), tile(half), tile(half)],
        [per_pair(TM, QK_NOPE + QK_ROPE, True), per_pair(t, QK_NOPE + QK_ROPE, False), per_pair(t, V_HEAD, False)],
        [_sds((B_HEADS, t, QK_NOPE + QK_ROPE), BF16), _sds((B_HEADS, t, QK_NOPE + QK_ROPE), F32),
         _sds((B_HEADS, t, V_HEAD), F32)],
        (dh, q, k, v, w_o, cos, sin), after=after)


def _kvq_bwd(h, dh, ckv, cqpre, dq, dk, dv, cos, sin, kvq_w, after=()):
    t = h.shape[0]
    tm = TM
    half, last = QK_ROPE // 2, t // tm - 1
    grad_shapes = [(D_MODEL, Q_LORA), (B_HEADS, Q_LORA, QK_NOPE + QK_ROPE), (D_MODEL, KV_LORA + QK_ROPE),
                   (B_HEADS, KV_LORA, QK_NOPE + V_HEAD)]

    def body(h_ref, dh_ref, ckv_ref, cqpre_ref, dq_ref, dk_ref, dv_ref, cos_ref, sin_ref,
             srcg_ref, wkva_ref, kvag_ref, wkvb_ref, mixg_ref, wqa_ref, qg_ref, wqb_ref,
             dhi_ref, dmixg_ref, dsrcg_ref, dqg_ref, dkvag_ref, gqa_ref, gqb_ref, gkva_ref, gkvb_ref,
             aqa, aqb, akva, akvb):
        @pl.when(pl.program_id(0) == 0)
        def _():
            for acc in (aqa, aqb, akva, akvb):
                acc[...] = jnp.zeros_like(acc)

        hv = h_ref[...]
        rstd = lax.rsqrt(jnp.mean(hv * hv, axis=-1, keepdims=True) + EPS)
        xhat = hv * rstd
        mixg, srcg, qg, kvag = mixg_ref[...], srcg_ref[...], qg_ref[...], kvag_ref[...]
        cq, cqhat, crstd = _rms_fwd(cqpre_ref[...], qg)
        cqb = cq.astype(BF16)
        dcq = jnp.zeros((tm, Q_LORA), F32)
        for hd in range(B_HEADS):
            dcq = dcq + _dot_nt(dq_ref[hd], wqb_ref[hd])
            aqb[hd] += _dot_tn(cqb, dq_ref[hd])
        dcqpre, dqg = _rms_bwd(dcq, cqhat, crstd, qg)
        dcqpre_b = dcqpre.astype(BF16)
        aqa[...] += _dot_tn((xhat * mixg).astype(BF16), dcqpre_b)
        dxq, dmixg = _rms_bwd(_dot_nt(dcqpre_b, wqa_ref[...]), xhat, rstd, mixg)
        ckv = ckv_ref[...]
        c, chat, krstd = _rms_fwd(ckv[:, :KV_LORA], kvag)
        cb = c.astype(BF16)
        dc = jnp.zeros((tm, KV_LORA), F32)
        dkpe = jnp.zeros((tm, QK_ROPE), F32)
        for hd in range(B_HEADS):
            dkv = jnp.concatenate([dk_ref[hd, :, 0:QK_NOPE], dv_ref[hd]], axis=-1).astype(BF16)
            akvb[hd] += _dot_tn(cb, dkv)
            dc = dc + _dot_nt(dkv, wkvb_ref[hd])
            dkpe = dkpe + dk_ref[hd, :, QK_NOPE:]
        dlat, dkvag = _rms_bwd(dc, chat, krstd, kvag)
        dpe = _rope(dkpe, cos_ref[...], -sin_ref[...])
        dckv_b = jnp.concatenate([dlat, dpe], axis=-1).astype(BF16)
        akva[...] += _dot_tn((xhat * srcg).astype(BF16), dckv_b)
        dxk, dsrcg = _rms_bwd(_dot_nt(dckv_b, wkva_ref[...]), xhat, rstd, srcg)
        dhi_ref[...] = dh_ref[...] + dxq + dxk
        _acc(dmixg_ref, dmixg)
        _acc(dsrcg_ref, dsrcg)
        _acc(dqg_ref, dqg)
        _acc(dkvag_ref, dkvag)

        @pl.when(pl.program_id(0) == last)
        def _():
            for out, acc in ((gqa_ref, aqa), (gqb_ref, aqb), (gkva_ref, akva), (gkvb_ref, akvb)):
                out[...] = acc[...].astype(BF16)

    return _call(
        "kvq_bwd", body, (t // tm,),
        [_row(D_MODEL, tm), _row(D_MODEL, tm), _row(KV_LORA + QK_ROPE, tm), _row(Q_LORA, tm),
         _heads(QK_NOPE + QK_ROPE, tm), _heads(QK_NOPE + QK_ROPE, tm), _heads(V_HEAD, tm), _row(half, tm),
         _row(half, tm), *KVQ_W_SPECS],
        [_row(D_MODEL, tm), _const((1, D_MODEL)), _const((1, D_MODEL)), _const((1, Q_LORA)), _const((1, KV_LORA))]
        + [_const(s) for s in grad_shapes],
        [_sds((t, D_MODEL), F32), _sds((1, D_MODEL), F32), _sds((1, D_MODEL), F32), _sds((1, Q_LORA), F32),
         _sds((1, KV_LORA), F32)] + [_sds(s, BF16) for s in grad_shapes],
        (h, dh, ckv, cqpre, dq, dk, dv, cos, sin, *kvq_w), scratch=[pltpu.VMEM(s, F32) for s in grad_shapes],
        after=after)


def _a_mix_bwd(x, z, dh, g, w_in, ln_g, ln_b, w_s, b_st, w_out, after=()):
    t = x.shape[0]
    tm = TM_GATE
    nblk = tm // GMLP_BLOCK

    def body(x_ref, z_ref, dh_ref, g_ref, win_ref, lng_ref, lnb_ref, ws_ref, bst_ref, wout_ref,
             dx_ref, hn_ref, dz_ref, dg_ref, dlng_ref, dlnb_ref, dws_ref, dbs_ref, dvn_scr, gelu_grad_v):
        @pl.when(pl.program_id(0) == 0)
        def _():
            dws_ref[...] = jnp.zeros_like(dws_ref)
            dbs_ref[...] = jnp.zeros_like(dbs_ref)

        gv, lng = g_ref[...], lng_ref[...]
        y, xhat, rstd = _rms_fwd(x_ref[...], gv)
        hn_ref[...] = y.astype(BF16)
        dhv = dh_ref[...]
        dgated = _dot_nt(dhv.astype(BF16), wout_ref[...])
        u, gelu_grad_u = _gelu_and_grad(z_ref[:, :GATE_DIM])
        v, gelu_grad_v[...] = _gelu_and_grad(z_ref[:, GATE_DIM:])
        vn, vhat, lrstd = _ln_fwd(v, lng, lnb_ref[...])
        vb = vn.astype(BF16)
        mask = _gate_mask()
        for gi in range(A_GROUPS):
            wm = jnp.where(mask, ws_ref[gi], 0.0).astype(BF16)
            bias = bst_ref[:, gi:gi + 1]
            cs = slice(gi * A_GROUP_DIM, (gi + 1) * A_GROUP_DIM)
            dws = jnp.zeros((GMLP_BLOCK, GMLP_BLOCK), F32)
            dbs = jnp.zeros((GMLP_BLOCK, 1), F32)
            for n in range(nblk):
                rs = slice(n * GMLP_BLOCK, (n + 1) * GMLP_BLOCK)
                sv = _dot(wm, vb[rs, cs]) + bias
                dz_ref[rs, cs] = (dgated[rs, cs] * sv * gelu_grad_u[rs, cs]).astype(BF16)
                dsv = dgated[rs, cs] * u[rs, cs]
                dsvb = dsv.astype(BF16)
                dws = dws + _dot_nt(dsvb, vb[rs, cs])
                dbs = dbs + jnp.sum(dsv, axis=-1, keepdims=True)
                dvn_scr[rs, cs] = _dot_tn(wm, dsvb)
            dws_ref[gi] += jnp.where(mask, dws, 0.0)
            dbs_ref[gi] += dbs
        dvn = dvn_scr[...]
        dvhat = dvn * lng
        dv = lrstd * (dvhat - jnp.mean(dvhat, axis=-1, keepdims=True)
                      - vhat * jnp.mean(dvhat * vhat, axis=-1, keepdims=True))
        dz_ref[:, GATE_DIM:] = (dv * gelu_grad_v[...]).astype(BF16)
        dhn = jnp.zeros((tm, D_MODEL), F32)
        for d in range(N_DEV):
            dhn = dhn + _dot_nt(dz_ref[:, d * FF_SLOT:(d + 1) * FF_SLOT], win_ref[d])
        dx, dg = _rms_bwd(dhn, xhat, rstd, gv)
        dx_ref[...] = dhv + dx
        _acc(dg_ref, dg)
        _acc(dlng_ref, jnp.sum(dvn * vhat, axis=0, keepdims=True))
        _acc(dlnb_ref, jnp.sum(dvn, axis=0, keepdims=True))

    return _call(
        "a_mix_bwd", body, (t // tm,),
        [_row(D_MODEL, tm), _row(2 * GATE_DIM, tm), _row(D_MODEL, tm), _res((1, D_MODEL)),
         _res((N_DEV, D_MODEL, FF_SLOT)), _res((1, GATE_DIM)), _res((1, GATE_DIM)),
         _res((A_GROUPS, GMLP_BLOCK, GMLP_BLOCK)), _res((GMLP_BLOCK, A_GROUPS)), _res((GATE_DIM, D_MODEL))],
        [_row(D_MODEL, tm), _row(D_MODEL, tm), _row(2 * GATE_DIM, tm),
         _const((1, D_MODEL)), _const((1, GATE_DIM)), _const((1, GATE_DIM)),
         _const((A_GROUPS, GMLP_BLOCK, GMLP_BLOCK)), _const((A_GROUPS, GMLP_BLOCK, 1))],
        [_sds((t, D_MODEL), F32), _sds((t, D_MODEL), BF16),
         _sds((t, 2 * GATE_DIM), BF16), _sds((1, D_MODEL), F32), _sds((1, GATE_DIM), F32),
         _sds((1, GATE_DIM), F32), _sds((A_GROUPS, GMLP_BLOCK, GMLP_BLOCK), F32),
         _sds((A_GROUPS, GMLP_BLOCK, 1), F32)],
        (x, z, dh, g, w_in, ln_g, ln_b, w_s, b_st, w_out),
        scratch=[pltpu.VMEM((tm, GATE_DIM), F32), pltpu.VMEM((tm, GATE_DIM), F32)], after=after)


def _wgrad(name, a, b, a_spec, b_spec, m, n, after=()):
    def body(a_ref, b_ref, o_ref):
        o_ref[0] = _dot_tn(a_ref[...].astype(BF16), b_ref[...].astype(BF16)).astype(BF16)

    return _call(name, body, (N_DEV,), [a_spec, b_spec], [pl.BlockSpec((1, m, n), lambda d: (d, 0, 0))],
                 [_sds((N_DEV, m, n), BF16)], (a, b), after=after)[0]


def _full(t, d):
    return pl.BlockSpec((t, d), lambda i: (0, 0), pipeline_mode=pl.Buffered(1))


def _cols(t, d):
    return pl.BlockSpec((t, d), lambda i: (0, i))


def _head(t, d):
    return pl.BlockSpec((None, t, d), lambda i: (i, 0, 0))


def _local_step(x, pos, target, inv_freq, wg, sm, shards=None):
    t = x.shape[0]
    wg = dict(wg)
    dist = shards is not None
    mix_g = [sm["norm_mix_g"][l:l + 1] for l in range(2)]
    mlp_g = [sm["norm_mlp_g"][l:l + 1] for l in range(2)]

    ids = iter(range(2, 2 + 10))

    def gather(names):
        if dist:
            got = _by_sequencer("gather_" + names[0], _gather_comm([shards[k] for k in names]),
                                SIBLING_AND_NEIGHBOURS, next(ids))
            wg.update(zip(names, got))

    def send(name, names):
        if dist:
            comm = _exchange_comm(grads=[g[k] for k in names])
            g.update(zip(names, _by_sequencer("exchange_" + name, comm, EVERYONE, next(ids))))

    def send_sums(name, names, meanwhile):
        if not dist:
            meanwhile()
            return ()
        grads = [g[k] for k in names]
        landed = _by_sequencer("pair_exchange_" + name, _pair_exchange_comm(grads), (1,), next(ids))
        sums = _pair_add("pair_add_" + name, grads, landed, after=meanwhile())
        g.update(zip(names, _by_sequencer("exchange_" + name, _chip_exchange_comm(sums), OTHER_CHIPS, next(ids))))
        return sums

    def a_args():
        return (wg["a_w_in"], wg["a_ln_v_g"], wg["a_ln_v_b"], sm["a_w_s"], sm["a_b_st"], wg["a_w_out"])

    def kvq_w():
        return (sm["kv_src_norm_g"], wg["kv_w_a"], sm["kv_a_norm_g"], wg["kv_w_b"], mix_g[1], wg["b_w_q_a"],
                sm["b_q_norm_g"], wg["b_w_q_b"])

    gather(("mlp_w1_0", "mlp_w2_0"))
    h1, z, gated = _a_mix_fwd(x, mix_g[0], *a_args())
    gather(("kv_w_a", "kv_w_b", "b_w_q_a", "b_w_q_b", "b_w_o"))
    h2, a0 = _mlp_fwd(h1, mlp_g[0], wg["mlp_w1_0"], wg["mlp_w2_0"])
    if dist:
        wg["b_w_q_a"] = wg["b_w_q_a"].reshape(D_MODEL, Q_LORA)
        wg["kv_w_a"] = wg["kv_w_a"].reshape(D_MODEL, KV_LORA + QK_ROPE)
    gather(("mlp_w1_1", "mlp_w2_1"))
    ckv, k, v, cqpre, q, cos, sin = _kvq_fwd(h2, pos, inv_freq, kvq_w())
    h3, att = _attn_fwd(h2, q, k, v, wg["b_w_o"])
    a1, loss, dh4, d_final_g = _mlp_fwd_loss(h3, mlp_g[1], wg["mlp_w1_1"], wg["mlp_w2_1"], sm["final_norm_g"], target)

    g = {}
    dh3, d_mlp_g1, hn, f, da, dh3_b = _mlp_bwd(h3, a1, dh4, mlp_g[1], wg["mlp_w1_1"], wg["mlp_w2_1"], 1)
    dq, dk, dv = _attn_bwd(dh3_b, q, k, v, wg["b_w_o"], cos, sin)
    g["mlp_w1_1"] = _wgrad("wgrad_w1_1", hn, da, _full(t, D_MODEL), _cols(t, FF_SLOT), D_MODEL, FF_SLOT, after=[dq])
    g["mlp_w2_1"] = _wgrad("wgrad_w2_1", f, dh4, _cols(t, FF_SLOT), _full(t, D_MODEL), FF_SLOT, D_MODEL)

    def wgrad_w_o():
        g["b_w_o"] = _wgrad("wgrad_w_o", att, dh3_b, _head(t, V_HEAD), _full(t, D_MODEL), V_HEAD, D_MODEL)
        return [g["b_w_o"]]

    sums = send_sums("mlp_1", ("mlp_w1_1", "mlp_w2_1"), wgrad_w_o)
    dh2, d_mix_g1, d_src_g, d_q_g, d_kv_a_g, g_q_a, g["b_w_q_b"], g_kv_a, g["kv_w_b"] = _kvq_bwd(
        h2, dh3, ckv, cqpre, dq, dk, dv, cos, sin, kvq_w(), after=sums)
    g["b_w_q_a"] = g_q_a.reshape(N_DEV, D_MODEL // N_DEV, Q_LORA)
    g["kv_w_a"] = g_kv_a.reshape(N_DEV, D_MODEL // N_DEV, KV_LORA + QK_ROPE)
    qkv = ("b_w_q_a", "b_w_q_b", "kv_w_a", "kv_w_b")
    landed = [g[k] for k in qkv]
    send("qkv", qkv)
    dh1, d_mlp_g0, hn, f, da, dh1_b = _mlp_bwd(h1, a0, dh2, mlp_g[0], wg["mlp_w1_0"], wg["mlp_w2_0"], 0,
                                               after=landed if dist else ())
    landed = [g["mlp_w1_1"], g["mlp_w2_1"]] if dist else ()
    g["mlp_w1_0"] = _wgrad("wgrad_w1_0", hn, da, _full(t, D_MODEL), _cols(t, FF_SLOT), D_MODEL, FF_SLOT, after=landed)
    g["mlp_w2_0"] = _wgrad("wgrad_w2_0", f, dh2, _cols(t, FF_SLOT), _full(t, D_MODEL), FF_SLOT, D_MODEL)

    def wgrad_a_w_out():
        g["a_w_out"] = _wgrad("wgrad_a_w_out", gated, dh1_b, _cols(t, GATE_DIM // N_DEV), _full(t, D_MODEL),
                              GATE_DIM // N_DEV, D_MODEL)
        return [g["a_w_out"]] + [g[k] for k in qkv]

    sums = send_sums("mlp_0", ("mlp_w1_0", "mlp_w2_0", "b_w_o"), wgrad_a_w_out)
    if dist:
        sums = _pair_reduce("pair_reduce_a_w_out", [g["a_w_out"]], after=sums)
    dx, hn, dz, d_mix_g0, d_ln_g, d_ln_b, d_ws, d_bs = _a_mix_bwd(x, z, dh1, mix_g[0], *a_args(), after=sums)
    small = {
        "norm_mix_g": jnp.concatenate([d_mix_g0, d_mix_g1], axis=0),
        "norm_mlp_g": jnp.concatenate([d_mlp_g0, d_mlp_g1], axis=0),
        "a_ln_v_g": d_ln_g.reshape(N_DEV, GATE_DIM // N_DEV),
        "a_ln_v_b": d_ln_b.reshape(N_DEV, GATE_DIM // N_DEV),
        "a_w_s": d_ws.astype(BF16) if dist else d_ws,
        "a_b_s": d_bs.reshape(A_GROUPS, GMLP_BLOCK),
        "b_q_norm_g": d_q_g,
        "kv_src_norm_g": d_src_g,
        "kv_a_norm_g": d_kv_a_g,
        "final_norm_g": d_final_g,
    }
    if dist:
        parts = [small[k].reshape((1,) + small[k].shape) for k in SMALL] + [loss.reshape(1, 1, 1)]
        comm = _together(_chip_exchange_comm(sums), _gather_comm(parts))
        g["a_w_out"], *got = _by_sequencer("exchange_a_w_out", comm, EVERYONE, next(ids))
        small, loss = dict(zip(SMALL, got)), got[-1]
    g["a_w_in"] = _wgrad("wgrad_a_w_in", hn, dz, _full(t, D_MODEL), _cols(t, FF_SLOT), D_MODEL, FF_SLOT)
    return loss, dx, g, small


def _adamw(w, g, m, v):
    m = ADAM_B1 * m + (1.0 - ADAM_B1) * g
    v = ADAM_B2 * v + (1.0 - ADAM_B2) * (g * g)
    m_hat = m / (1.0 - ADAM_B1 ** ADAM_STEP)
    v_hat = v / (1.0 - ADAM_B2 ** ADAM_STEP)
    return -ADAM_LR * (m_hat / (jnp.sqrt(v_hat) + ADAM_EPS) + ADAM_WD * w), m, v


def _sum_in_device_order(r_ref):
    g = r_ref[0].astype(F32)
    for j in range(1, r_ref.shape[0]):
        g = g + r_ref[j].astype(F32)
    return g


def _adamw_sharded(name, recvs, w, m, v, swapped=False):
    layers, r, c = w.shape[0], *recvs[0][0].shape[1:]
    tr = r if swapped else math.gcd(r, 512)
    flat = [a for per_layer in recvs for a in per_layer]

    def body(*refs):
        r_refs, (w_ref, m_ref, v_ref) = refs[:len(flat)], refs[len(flat):len(flat) + 3]
        g_ref, d_ref, nm_ref, nv_ref = refs[-4:]
        layer = pl.program_id(0)
        g, pos = None, 0
        for li, per_layer in enumerate(recvs):
            total = None
            for ref in r_refs[pos:pos + len(per_layer)]:
                part = _sum_in_device_order(ref)
                total = part if total is None else total + part
            pos += len(per_layer)
            g = total if g is None else jnp.where(layer == li, total, g)
        if swapped:
            g = g.T
        g_ref[...] = g
        d_ref[...], nm_ref[...], nv_ref[...] = _adamw(w_ref[...], g, m_ref[...], v_ref[...])

    blk = pl.BlockSpec((None, tr, c), lambda l, i: (l, i, 0))
    if swapped:
        blk = pl.BlockSpec((None, c, r), lambda l, i: (l, 0, 0))
    return _call(name, body, (layers, r // tr),
                 [pl.BlockSpec((a.shape[0], tr, c), lambda l, i: (0, i, 0)) for a in flat] + [blk] * 3,
                 [blk] * 4, [_sds(w.shape, F32)] * 4, (*flat, w, m, v))


def _adamw_small(recvs, ws, ms, vs, own_row, losses):
    n = len(recvs)

    def body(*refs):
        r_refs, w_refs, m_refs, v_refs = (refs[i * n:(i + 1) * n] for i in range(4))
        outs, scr = refs[4 * n + 1:8 * n + 2], refs[8 * n + 2:]
        outs[-1][...] = _sum_in_device_order(refs[4 * n])
        me = _my_place()[3]
        for a in range(n):
            g = _sum_in_device_order(r_refs[a])
            if own_row[a]:
                scr[0][...] = g
                g = scr[0][pl.ds(me, 1), :]
            g_ref, d_ref, nm_ref, nv_ref = outs[4 * a:4 * a + 4]
            g_ref[...] = g
            d_ref[...], nm_ref[...], nv_ref[...] = _adamw(w_refs[a][...], g, m_refs[a][...], v_refs[a][...])

    out_shape = []
    for w in ws:
        out_shape += [_sds(w.shape, F32)] * 4
    return pl.pallas_call(
        body, name="adamw_small", in_specs=[VMEM] * (4 * n + 1), out_specs=[VMEM] * (4 * n + 1),
        out_shape=out_shape + [_sds((1, 1), F32)], scratch_shapes=[pltpu.VMEM((N_DEV, GATE_DIM // N_DEV), F32)],
    )(*recvs, *ws, *ms, *vs, losses)


BIG = ("a_w_in", "a_w_out", "b_w_q_a", "b_w_q_b", "b_w_o", "kv_w_a", "kv_w_b", "mlp_w1", "mlp_w2")
SMALL = ("norm_mix_g", "norm_mlp_g", "a_ln_v_g", "a_ln_v_b", "a_w_s", "a_b_s", "b_q_norm_g", "kv_src_norm_g",
         "kv_a_norm_g", "final_norm_g")
WEIGHTS = ("norm_mix_g", "norm_mlp_g", "a_w_in", "a_ln_v_g", "a_ln_v_b", "a_w_s", "a_b_s", "a_w_out", "b_w_q_a",
           "b_q_norm_g", "b_w_q_b", "b_w_o", "kv_src_norm_g", "kv_w_a", "kv_a_norm_g", "kv_w_b", "mlp_w1", "mlp_w2",
           "final_norm_g")


def _two_d(name, a):
    if name in ("a_w_s", "a_b_s"):
        return a.reshape(a.shape[1:])
    return a.reshape(1, -1) if a.ndim == 1 else a


def _three_d(a):
    return a if a.ndim == 3 else a.reshape((1,) + a.shape)


SWAPPED = ("b_w_q_b", "kv_w_a")


def _swapped(a):
    return jnp.swapaxes(_three_d(a), 1, 2)


def kernel(x, positions, norm_mix_g, norm_mlp_g, a_w_in, a_ln_v_g, a_ln_v_b, a_w_s, a_b_s, a_w_out, b_w_q_a, b_q_norm_g, b_w_q_b, b_w_o, kv_src_norm_g, kv_w_a, kv_a_norm_g, kv_w_b, mlp_w1, mlp_w2, final_norm_g, loss_target, m_norm_mix_g, m_norm_mlp_g, m_a_w_in, m_a_ln_v_g, m_a_ln_v_b, m_a_w_s, m_a_b_s, m_a_w_out, m_b_w_q_a, m_b_q_norm_g, m_b_w_q_b, m_b_w_o, m_kv_src_norm_g, m_kv_w_a, m_kv_a_norm_g, m_kv_w_b, m_mlp_w1, m_mlp_w2, m_final_norm_g, v_norm_mix_g, v_norm_mlp_g, v_a_w_in, v_a_ln_v_g, v_a_ln_v_b, v_a_w_s, v_a_b_s, v_a_w_out, v_b_w_q_a, v_b_q_norm_g, v_b_w_q_b, v_b_w_o, v_kv_src_norm_g, v_kv_w_a, v_kv_a_norm_g, v_kv_w_b, v_mlp_w1, v_mlp_w2, v_final_norm_g):
    w = dict(norm_mix_g=norm_mix_g, norm_mlp_g=norm_mlp_g, a_w_in=a_w_in, a_ln_v_g=a_ln_v_g, a_ln_v_b=a_ln_v_b,
             a_w_s=a_w_s, a_b_s=a_b_s, a_w_out=a_w_out, b_w_q_a=b_w_q_a, b_q_norm_g=b_q_norm_g, b_w_q_b=b_w_q_b,
             b_w_o=b_w_o, kv_src_norm_g=kv_src_norm_g, kv_w_a=kv_w_a, kv_a_norm_g=kv_a_norm_g, kv_w_b=kv_w_b,
             mlp_w1=mlp_w1, mlp_w2=mlp_w2, final_norm_g=final_norm_g)
    m = dict(norm_mix_g=m_norm_mix_g, norm_mlp_g=m_norm_mlp_g, a_w_in=m_a_w_in, a_ln_v_g=m_a_ln_v_g,
             a_ln_v_b=m_a_ln_v_b, a_w_s=m_a_w_s, a_b_s=m_a_b_s, a_w_out=m_a_w_out, b_w_q_a=m_b_w_q_a,
             b_q_norm_g=m_b_q_norm_g, b_w_q_b=m_b_w_q_b, b_w_o=m_b_w_o, kv_src_norm_g=m_kv_src_norm_g,
             kv_w_a=m_kv_w_a, kv_a_norm_g=m_kv_a_norm_g, kv_w_b=m_kv_w_b, mlp_w1=m_mlp_w1, mlp_w2=m_mlp_w2,
             final_norm_g=m_final_norm_g)
    v = dict(norm_mix_g=v_norm_mix_g, norm_mlp_g=v_norm_mlp_g, a_w_in=v_a_w_in, a_ln_v_g=v_a_ln_v_g,
             a_ln_v_b=v_a_ln_v_b, a_w_s=v_a_w_s, a_b_s=v_a_b_s, a_w_out=v_a_w_out, b_w_q_a=v_b_w_q_a,
             b_q_norm_g=v_b_q_norm_g, b_w_q_b=v_b_w_q_b, b_w_o=v_b_w_o, kv_src_norm_g=v_kv_src_norm_g,
             kv_w_a=v_kv_w_a, kv_a_norm_g=v_kv_a_norm_g, kv_w_b=v_kv_w_b, mlp_w1=v_mlp_w1, mlp_w2=v_mlp_w2,
             final_norm_g=v_final_norm_g)
    t = x.shape[1]

    first = ("a_w_in", "a_w_out", "a_ln_v_g", "a_ln_v_b")
    later = ("mlp_w1", "mlp_w2", "kv_w_a", "kv_w_b", "b_w_q_a", "b_w_q_b", "b_w_o")
    later_blocks = ("mlp_w1_0", "mlp_w1_1", "mlp_w2_0", "mlp_w2_1") + later[2:]
    got, casts = _gather_first([_three_d(w[k]) if k in BIG else w[k] for k in first],
                               [_swapped(w[k]) if k in SWAPPED else _three_d(w[k]) for k in later],
                               [k in SWAPPED for k in later])
    wg = dict(zip(first, got))
    wg["a_w_out"] = wg["a_w_out"].reshape(GATE_DIM, D_MODEL)
    wg["a_ln_v_g"] = wg["a_ln_v_g"].reshape(1, GATE_DIM)
    wg["a_ln_v_b"] = wg["a_ln_v_b"].reshape(1, GATE_DIM)
    shards = dict(zip(later_blocks, casts))

    sm = {k: _two_d(k, w[k]) for k in SMALL if k not in ("a_ln_v_g", "a_ln_v_b")}
    sm["a_b_st"] = sm["a_b_s"].T
    inv_freq = (ROPE_THETA ** (-jnp.arange(0, QK_ROPE, 2, dtype=F32) / QK_ROPE)).reshape(1, QK_ROPE // 2)

    losses, dx, g, small = _local_step(x[0], positions.reshape(t, 1), loss_target[0], inv_freq, wg, sm, shards)

    sums = _pair_reduce("pair_reduce_a_w_in", [g["a_w_in"]], after=[g["mlp_w1_0"], g["mlp_w2_0"]])
    g["a_w_in"], = _by_sequencer("exchange_last", _chip_exchange_comm(sums), OTHER_CHIPS, collective_id=1)

    out = {}
    for k in BIG:
        recvs = [[g[k + "_0"]], [g[k + "_1"]]] if k.startswith("mlp") else [[g[k]]]
        view = _swapped if k in SWAPPED else _three_d
        res = _adamw_sharded("adamw_" + k, recvs, view(w[k]), view(m[k]), view(v[k]), swapped=k in SWAPPED)
        out[k] = [view(o).reshape(w[k].shape) for o in res]
    own_row = [k in ("a_ln_v_g", "a_ln_v_b") for k in SMALL]
    res = _adamw_small([small[k] for k in SMALL], [_two_d(k, w[k]) for k in SMALL], [_two_d(k, m[k]) for k in SMALL],
                       [_two_d(k, v[k]) for k in SMALL], own_row, losses)
    for i, k in enumerate(SMALL):
        out[k] = [o.reshape(w[k].shape) for o in res[4 * i:4 * i + 4]]

    return (res[-1].reshape(()), dx.reshape(x.shape), *[out[k][0] for k in WEIGHTS], *[out[k][1] for k in WEIGHTS],
            *[out[k][2] for k in WEIGHTS], *[out[k][3] for k in WEIGHTS])
```

```python
import math

import jax
import jax.numpy as jnp
from jax import lax
from jax.experimental import pallas as pl
from jax.experimental.pallas import tpu as pltpu
from jax.experimental.pallas import tpu_sc as plsc

F32, BF16 = jnp.float32, jnp.bfloat16
MESH = pl.DeviceIdType.MESH
ANY = pl.BlockSpec(memory_space=pl.ANY)
VMEM = pl.BlockSpec(memory_space=pltpu.VMEM)

N_DEV = 8
D_MODEL = 1024
CHUNK = 64
GMLP_BLOCK = 128
GATE_DIM = 2048
A_GROUPS = 8
A_GROUP_DIM = GATE_DIM // A_GROUPS
B_HEADS = 8
QK_NOPE, QK_ROPE, V_HEAD = 128, 64, 128
Q_LORA, KV_LORA = 384, 256
ROPE_THETA = 10000.0
D_FF = 4096
FF_SLOT = D_FF // N_DEV
EPS = 1e-6
ATT_SCALE = (QK_NOPE + QK_ROPE) ** -0.5

ADAM_LR, ADAM_B1, ADAM_B2, ADAM_EPS, ADAM_WD, ADAM_STEP = 0.001, 0.9, 0.999, 1e-08, 0.01, 10

TM = 256
TM_GATE = 256
TM_MLP_FWD = 512
TM_KVQ = 512
VMEM_LIMIT = 56 * 1024 * 1024
INV_SQRT2 = 1.0 / math.sqrt(2.0)
INV_SQRT_2PI = 1.0 / math.sqrt(2.0 * math.pi)
LOG2_E = 1.0 / math.log(2.0)
HEADS_PER_STEP = 2


def _dot(a, b):
    return jnp.dot(a, b, preferred_element_type=F32)


def _dot_nt(a, b):
    return lax.dot_general(a, b, (((1,), (1,)), ((), ())), preferred_element_type=F32)


def _dot_tn(a, b):
    return lax.dot_general(a, b, (((0,), (0,)), ((), ())), preferred_element_type=F32)


def _rms_fwd(x, g):
    rstd = lax.rsqrt(jnp.mean(x * x, axis=-1, keepdims=True) + EPS)
    xhat = x * rstd
    return xhat * g, xhat, rstd


def _rms_bwd(dy, xhat, rstd, g):
    dxhat = dy * g
    dx = rstd * (dxhat - xhat * jnp.mean(dxhat * xhat, axis=-1, keepdims=True))
    return dx, jnp.sum(dy * xhat, axis=0, keepdims=True)


def _ln_fwd(v, g, b):
    mu = jnp.mean(v, axis=-1, keepdims=True)
    vc = v - mu
    rstd = lax.rsqrt(jnp.mean(vc * vc, axis=-1, keepdims=True) + EPS)
    vhat = vc * rstd
    return vhat * g + b, vhat, rstd


def _gelu(x):
    return 0.5 * x * (1.0 + lax.erf(x * INV_SQRT2))


def _gelu_and_grad(x):
    cdf = 0.5 * (1.0 + lax.erf(x * INV_SQRT2))
    return x * cdf, cdf + x * jnp.exp(-0.5 * x * x) * INV_SQRT_2PI


def _rope(x, cos, sin):
    x1, x2 = x[:, :QK_ROPE // 2], x[:, QK_ROPE // 2:]
    return jnp.concatenate([x1 * cos - x2 * sin, x2 * cos + x1 * sin], axis=-1)


def _gate_mask():
    row = lax.broadcasted_iota(jnp.int32, (GMLP_BLOCK, GMLP_BLOCK), 0)
    col = lax.broadcasted_iota(jnp.int32, (GMLP_BLOCK, GMLP_BLOCK), 1)
    return (col < CHUNK) | (row >= CHUNK)


def _att_mask(q0, tq, t):
    q = q0 + lax.broadcasted_iota(jnp.int32, (tq, t), 0)
    k = lax.broadcasted_iota(jnp.int32, (tq, t), 1)
    return jnp.right_shift(k, 6) <= jnp.right_shift(q, 6)


def _res(shape, imap=None):
    zeros = (0,) * len(shape)
    return pl.BlockSpec(shape, imap or (lambda i: zeros), pipeline_mode=pl.Buffered(1))


def _const(shape):
    zeros = (0,) * len(shape)
    return pl.BlockSpec(shape, lambda i: zeros)


def _row(d, tm=TM):
    return pl.BlockSpec((tm, d), lambda i: (i, 0))


def _heads(d, tm=TM):
    return pl.BlockSpec((B_HEADS, tm, d), lambda i: (0, i, 0))


def _sds(shape, dt):
    return jax.ShapeDtypeStruct(shape, dt)


def _acc(ref, val):
    @pl.when(pl.program_id(0) == 0)
    def _():
        ref[...] = jnp.zeros_like(ref)
    ref[...] += val


def _my_place():
    x, y, c = lax.axis_index("x"), lax.axis_index("y"), lax.axis_index("c")
    return x, y, c, 4 * x + 2 * y + c


def _peer(x, y, c, k):
    px = 1 - x if k & 4 else x
    py = 1 - y if k & 2 else y
    pc = 1 - c if k & 1 else c
    return (px, py, pc), 4 * px + 2 * py + pc


CHIPS = (2, 4, 6)


def _splits(ref):
    return len(ref.shape) >= 3 and ref.shape[1] % 32 == 0


def _piece(ref, block, half=None):
    if half is None or not _splits(ref):
        return ref.at[pl.ds(block, 1)]
    rows = ref.shape[1] // 2
    return ref.at[pl.ds(block, 1), pl.ds(half * rows, rows)]


def _gather_copy(sems, a, k, piece, to, src=None):
    return pltpu.make_async_remote_copy(
        src_ref=piece if src is None else src, dst_ref=piece, send_sem=sems[0].at[a, k], recv_sem=sems[1].at[a, k],
        device_id=to, device_id_type=MESH)


def _gather_start(srcs, outs, sems, only=None):
    x, y, c, me = _my_place()
    for a in range(len(srcs)) if only is None else (only,):
        mine = _piece(outs[a], me)
        pltpu.make_async_copy(srcs[a], mine, sems[2].at[a]).start()
        for k, rel in enumerate((1, 4, 2)):
            _gather_copy(sems, a, k, mine, _peer(x, y, c, rel)[0], src=srcs[a]).start()


def _gather_relay(srcs, outs, sems):
    x, y, c, _ = _my_place()
    sib = _peer(x, y, c, 1)[0]
    (xn, xn_i), (yn, yn_i) = _peer(x, y, c, 4), _peer(x, y, c, 2)
    for a in range(len(srcs)):
        out = outs[a]
        _gather_copy(sems, a, 1, _piece(out, xn_i), xn).wait_recv()
        _gather_copy(sems, a, 3, _piece(out, xn_i, 0), yn).start()
        _gather_copy(sems, a, 5, _piece(out, xn_i), sib).start()
        _gather_copy(sems, a, 2, _piece(out, yn_i), yn).wait_recv()
        if _splits(out):
            _gather_copy(sems, a, 4, _piece(out, yn_i, 1), xn).start()
        _gather_copy(sems, a, 6, _piece(out, yn_i), sib).start()


def _gather_finish(srcs, outs, sems):
    x, y, c, me = _my_place()
    sib = _peer(x, y, c, 1)[0]
    xn, yn, dg_i = _peer(x, y, c, 4)[0], _peer(x, y, c, 2)[0], _peer(x, y, c, 6)[1]
    n = len(srcs)
    for a in range(n):
        out = outs[a]
        _gather_copy(sems, a, 3, _piece(out, dg_i, 0), yn).wait_recv()
        _gather_copy(sems, a, 7, _piece(out, dg_i, 0), sib).start()
        if _splits(out):
            _gather_copy(sems, a, 4, _piece(out, dg_i, 1), xn).wait_recv()
            _gather_copy(sems, a, 8, _piece(out, dg_i, 1), sib).start()
    for a in range(n):
        out = outs[a]
        whole, half = _piece(out, me), _piece(out, me, 0)
        for k in (0, 5, 6):
            _gather_copy(sems, a, k, whole, sib).wait_recv()
        for k in (7, 8) if _splits(out) else (7,):
            _gather_copy(sems, a, k, half, sib).wait_recv()
        for k in (0, 1, 2):
            _gather_copy(sems, a, k, whole, sib, src=srcs[a]).wait_send()
        for k in (5, 6):
            _gather_copy(sems, a, k, whole, sib).wait_send()
        for k in (3, 4, 7, 8) if _splits(out) else (3, 7):
            _gather_copy(sems, a, k, half, sib).wait_send()
        pltpu.make_async_copy(srcs[a], whole, sems[2].at[a]).wait()


def _relay_sems(n):
    return [pltpu.SemaphoreType.DMA((n, 9)), pltpu.SemaphoreType.DMA((n, 9)), pltpu.SemaphoreType.DMA((n,))]


def _gather_sems(n):
    return [pltpu.SemaphoreType.DMA((n, 7)), pltpu.SemaphoreType.DMA((n, 7)), pltpu.SemaphoreType.DMA((n,))]


class _Comm:
    def __init__(self, args, out_shape, scratch, start, finish, relay=None):
        self.args, self.out_shape, self.scratch, self.start, self.finish = args, out_shape, scratch, start, finish
        self.relay = relay


def _gather_comm(shards):
    return _Comm(list(shards), [_sds((N_DEV,) + s.shape[1:], s.dtype) for s in shards], _relay_sems(len(shards)),
                 _gather_start, _gather_finish, relay=_gather_relay)


def _spread_copies(ins, outs, sems, wait):
    send_sems, recv_sems, local_sems = sems
    x, y, c, me = _my_place()
    for a in range(len(ins)):
        mine = outs[a].at[pl.ds(me, 1)]
        local = pltpu.make_async_copy(ins[a], mine, local_sems.at[a])
        local.wait() if wait else local.start()
        for k in range(1, N_DEV):
            cp = pltpu.make_async_remote_copy(
                src_ref=ins[a], dst_ref=mine, send_sem=send_sems.at[a, k - 1], recv_sem=recv_sems.at[a, k - 1],
                device_id=_peer(x, y, c, k)[0], device_id_type=MESH)
            cp.wait() if wait else cp.start()


def _spread_comm(parts):
    return _Comm(list(parts), [_sds((N_DEV,) + p.shape[1:], p.dtype) for p in parts], _gather_sems(len(parts)),
                 lambda i, o, s: _spread_copies(i, o, s, False), lambda i, o, s: _spread_copies(i, o, s, True))


def _together(big, small):
    na, no, ns = len(big.args), len(big.out_shape), len(big.scratch)

    def start(src, dst, sems):
        small.start(src[na:], dst[no:], sems[ns:])
        big.start(src[:na], dst[:no], sems[:ns])

    def finish(src, dst, sems):
        small.finish(src[na:], dst[no:], sems[ns:])
        big.finish(src[:na], dst[:no], sems[:ns])

    assert big.relay is None and small.relay is None
    return _Comm(big.args + small.args, big.out_shape + small.out_shape, list(big.scratch) + list(small.scratch),
                 start, finish)


def _direct_copies(ins, outs, sems, wait):
    send_sems, recv_sems, local_sems = sems
    x, y, c, me = _my_place()
    for a in range(len(ins)):
        local = pltpu.make_async_copy(ins[a].at[pl.ds(me, 1)], outs[a].at[pl.ds(me, 1)], local_sems.at[a])
        local.wait() if wait else local.start()
        for k in range(1, N_DEV):
            to, to_i = _peer(x, y, c, k)
            cp = pltpu.make_async_remote_copy(
                src_ref=ins[a].at[pl.ds(to_i, 1)], dst_ref=outs[a].at[pl.ds(me, 1)],
                send_sem=send_sems.at[a, k - 1], recv_sem=recv_sems.at[a, k - 1], device_id=to, device_id_type=MESH)
            cp.wait() if wait else cp.start()


def _exchange_comm(grads):
    return _Comm(list(grads), [_sds(g.shape, g.dtype) for g in grads], _gather_sems(len(grads)),
                 lambda i, o, s: _direct_copies(i, o, s, False), lambda i, o, s: _direct_copies(i, o, s, True))


def _chip_copies(ins, outs, sems, wait):
    send_sems, recv_sems, local_sems = sems
    x, y, c, _ = _my_place()
    for a in range(len(ins)):
        local = pltpu.make_async_copy(ins[a].at[pl.ds(2 * x + y, 1)], outs[a].at[pl.ds(len(CHIPS), 1)],
                                      local_sems.at[a])
        local.wait() if wait else local.start()
        for i, k in enumerate(CHIPS):
            to = _peer(x, y, c, k)[0]
            cp = pltpu.make_async_remote_copy(
                src_ref=ins[a].at[pl.ds(2 * to[0] + to[1], 1)], dst_ref=outs[a].at[pl.ds(i, 1)],
                send_sem=send_sems.at[a, i], recv_sem=recv_sems.at[a, i], device_id=to, device_id_type=MESH)
            cp.wait() if wait else cp.start()


def _chip_exchange_comm(sums):
    n = len(sums)
    sems = [pltpu.SemaphoreType.DMA((n, len(CHIPS))), pltpu.SemaphoreType.DMA((n, len(CHIPS))),
            pltpu.SemaphoreType.DMA((n,))]
    return _Comm(list(sums), [_sds(s.shape, s.dtype) for s in sums], sems,
                 lambda i, o, s: _chip_copies(i, o, s, False), lambda i, o, s: _chip_copies(i, o, s, True))


def _pair_reduce(name, grads, after=()):
    n = len(grads)
    n_chips = N_DEV // 2

    def body(*refs):
        g_refs, gh_refs, refs = refs[:n], refs[n:2 * n], refs[2 * n + len(after):]
        p_refs, land = refs[:n], refs[n:2 * n]
        send_sems, recv_sems = refs[2 * n:]
        x, y, c, _ = _my_place()
        sib = _peer(x, y, c, 1)[0]
        q = pl.program_id(0)

        def to_sibling(a, j):
            return pltpu.make_async_remote_copy(
                src_ref=gh_refs[a].at[j, pl.ds(1 - c, 1)], dst_ref=land[a].at[pl.ds(j, 1)],
                send_sem=send_sems.at[a, j], recv_sem=recv_sems.at[a, j], device_id=sib, device_id_type=MESH)

        @pl.when(q == 0)
        def _():
            for j in range(n_chips):
                for a in range(n):
                    to_sibling(a, j).start()

        for a in range(n):
            to_sibling(a, q).wait_recv()
            p_refs[a][...] = (g_refs[a][0, pl.ds(c, 1)].astype(F32) + land[a][pl.ds(q, 1)].astype(F32)).astype(BF16)

        @pl.when(q == n_chips - 1)
        def _():
            for a in range(n):
                for j in range(n_chips):
                    to_sibling(a, j).wait_send()

    views = [g.reshape((n_chips, 2) + g.shape[1:]) for g in grads]
    res = pl.pallas_call(
        body, name=name, grid=(n_chips,),
        in_specs=[pl.BlockSpec((1, 2) + g.shape[1:], lambda q: (q, 0, 0, 0)) for g in grads]
        + [ANY] * (n + len(after)),
        out_specs=[pl.BlockSpec((1,) + g.shape[1:], lambda q: (q, 0, 0)) for g in grads],
        out_shape=[_sds((n_chips,) + g.shape[1:], BF16) for g in grads],
        scratch_shapes=[pltpu.VMEM((n_chips,) + g.shape[1:], BF16) for g in grads]
        + [pltpu.SemaphoreType.DMA((n, n_chips)), pltpu.SemaphoreType.DMA((n, n_chips))],
        compiler_params=pltpu.CompilerParams(dimension_semantics=("arbitrary",), vmem_limit_bytes=VMEM_LIMIT),
    )(*views, *views, *after)
    return list(res)


def _pair_exchange_comm(grads):
    n, n_chips = len(grads), N_DEV // 2

    def copies(ins, outs, sems, wait):
        x, y, c, _ = _my_place()
        for j in range(n_chips):
            for a in range(n):
                cp = pltpu.make_async_remote_copy(
                    src_ref=ins[a].at[j, pl.ds(1 - c, 1)], dst_ref=outs[a].at[pl.ds(j, 1)], send_sem=sems[0].at[a, j],
                    recv_sem=sems[1].at[a, j], device_id=_peer(x, y, c, 1)[0], device_id_type=MESH)
                cp.wait() if wait else cp.start()

    views = [g.reshape((n_chips, 2) + g.shape[1:]) for g in grads]
    sems = [pltpu.SemaphoreType.DMA((n, n_chips)), pltpu.SemaphoreType.DMA((n, n_chips))]
    return _Comm(views, [_sds((n_chips,) + g.shape[1:], g.dtype) for g in grads], sems,
                 lambda i, o, s: copies(i, o, s, False), lambda i, o, s: copies(i, o, s, True))


def _pair_add(name, grads, landed, after=()):
    n, n_chips = len(grads), N_DEV // 2

    def body(core_ref, *refs):
        g_refs, l_refs, p_refs = refs[:n], refs[n:2 * n], refs[2 * n + len(after):]
        for a in range(n):
            p_refs[a][...] = (g_refs[a][...].astype(F32) + l_refs[a][...].astype(F32)).astype(BF16)

    views = [g.reshape((n_chips, 2) + g.shape[1:]) for g in grads]
    blocks = [pl.BlockSpec((1,) + g.shape[1:], lambda q, core: (q, 0, 0)) for g in grads]
    mine = [pl.BlockSpec((1, None) + g.shape[1:], lambda q, core: (q, core[0], 0, 0)) for g in grads]
    return list(pl.pallas_call(
        body, name=name, out_shape=[_sds((n_chips,) + g.shape[1:], BF16) for g in grads],
        grid_spec=pltpu.PrefetchScalarGridSpec(num_scalar_prefetch=1, grid=(n_chips,),
                                               in_specs=mine + blocks + [ANY] * len(after), out_specs=blocks),
        compiler_params=pltpu.CompilerParams(dimension_semantics=("arbitrary",), vmem_limit_bytes=VMEM_LIMIT),
    )(lax.axis_index("c").reshape(1), *views, *landed, *after))


def _call(name, body, grid, in_specs, out_specs, out_shape, args, scratch=(), after=()):
    ni, na = len(in_specs), len(after)

    def ordered(*refs):
        body(*refs[:ni], *refs[ni + na:])

    return list(pl.pallas_call(
        ordered if after else body, name=name, grid=grid, in_specs=list(in_specs) + [ANY] * na,
        out_specs=list(out_specs), out_shape=list(out_shape), scratch_shapes=list(scratch),
        compiler_params=pltpu.CompilerParams(dimension_semantics=("arbitrary",) * len(grid),
                                             vmem_limit_bytes=VMEM_LIMIT))(*args, *after))


SIBLING_AND_NEIGHBOURS, OTHER_CHIPS, EVERYONE = (1, 4, 2), CHIPS, tuple(range(1, N_DEV))


def _by_sequencer(name, comm, peers, collective_id):
    assert comm.relay is None or set(peers) == set(SIBLING_AND_NEIGHBOURS)
    src = [jax.new_ref(a, memory_space=pltpu.MemorySpace.HBM) for a in comm.args]
    dst = [jax.empty_ref(s, memory_space=pltpu.MemorySpace.HBM) for s in comm.out_shape]

    @pl.kernel(mesh=plsc.ScalarSubcoreMesh(axis_name="sequencer", num_cores=1), name=name,
               scratch_types=tuple(comm.scratch), compiler_params=pltpu.CompilerParams(collective_id=collective_id))
    def launch(*sems):
        x, y, c, _ = _my_place()
        barrier = pltpu.get_barrier_semaphore()
        for k in peers:
            pl.semaphore_signal(barrier, inc=1, device_id=_peer(x, y, c, k)[0], device_id_type=MESH)
        pl.semaphore_wait(barrier, len(peers))
        comm.start(src, dst, sems)
        if comm.relay is not None:
            comm.relay(src, dst, sems)
        comm.finish(src, dst, sems)

    launch()
    return [d[...] for d in dst]


def _gather_first(first, later, swapped):
    nf = len(first)
    layer_of = [(a, l) for a, s in enumerate(later) for l in range(s.shape[0])]
    nl = len(layer_of)
    dts = [BF16] * (nf - 2) + [F32, F32]
    shard = [s.shape[:0:-1] if sw else s.shape[1:] for s, sw in zip(later, swapped)]

    def body(*refs):
        ins, refs = refs[:nf + len(later)], refs[nf + len(later):]
        outs, refs = refs[:nf], refs[nf:]
        casts, refs = refs[:nl], refs[nl:]
        stage, sems = refs[:nf], refs[nf:]
        for a in range(nf):
            stage[a][...] = ins[a][...].astype(dts[a])
            _gather_start(stage, outs, sems, only=a)
        for k, (a, l) in enumerate(layer_of):
            block = ins[nf + a][l]
            casts[k][0] = (block.T if swapped[a] else block).astype(BF16)
        _gather_relay(stage, outs, sems)
        _gather_finish(stage, outs, sems)

    res = pl.pallas_call(
        body, name="gather_first",
        in_specs=[VMEM] * (nf + len(later)), out_specs=[ANY] * nf + [VMEM] * nl,
        out_shape=[_sds((N_DEV,) + s.shape[1:], dt) for s, dt in zip(first, dts)]
        + [_sds((1,) + shard[a], BF16) for a, _ in layer_of],
        scratch_shapes=[pltpu.VMEM(s.shape, dt) for s, dt in zip(first, dts)] + _relay_sems(nf),
        compiler_params=pltpu.CompilerParams(vmem_limit_bytes=VMEM_LIMIT),
    )(*first, *later)
    return list(res[:nf]), list(res[nf:])


def _a_mix_fwd(x, g, w_in, ln_g, ln_b, w_s, b_st, w_out):
    t = x.shape[0]
    nblk = TM // GMLP_BLOCK

    def body(x_ref, g_ref, win_ref, lng_ref, lnb_ref, ws_ref, bst_ref, wout_ref, h_ref, z_ref, gated_scr):
        xv = x_ref[...]
        hb = _rms_fwd(xv, g_ref[...])[0].astype(BF16)
        for d in range(N_DEV):
            z_ref[:, d * FF_SLOT:(d + 1) * FF_SLOT] = _dot(hb, win_ref[d])
        u = _gelu(z_ref[:, :GATE_DIM])
        vb = _ln_fwd(_gelu(z_ref[:, GATE_DIM:]), lng_ref[...], lnb_ref[...])[0].astype(BF16)
        mask = _gate_mask()
        for gi in range(A_GROUPS):
            wm = jnp.where(mask, ws_ref[gi], 0.0).astype(BF16)
            bias = bst_ref[:, gi:gi + 1]
            cs = slice(gi * A_GROUP_DIM, (gi + 1) * A_GROUP_DIM)
            for n in range(nblk):
                rs = slice(n * GMLP_BLOCK, (n + 1) * GMLP_BLOCK)
                sv = _dot(wm, vb[rs, cs]) + bias
                gated_scr[rs, cs] = (u[rs, cs] * sv).astype(BF16)
        h_ref[...] = xv + _dot(gated_scr[...], wout_ref[...])

    return _call(
        "a_mix_fwd", body, (t // TM,),
        [_row(D_MODEL), _res((1, D_MODEL)), _res((N_DEV, D_MODEL, FF_SLOT)), _res((1, GATE_DIM)),
         _res((1, GATE_DIM)), _res((A_GROUPS, GMLP_BLOCK, GMLP_BLOCK)), _res((GMLP_BLOCK, A_GROUPS)),
         _res((GATE_DIM, D_MODEL))],
        [_row(D_MODEL), _row(2 * GATE_DIM), _row(GATE_DIM)],
        [_sds((t, D_MODEL), F32), _sds((t, 2 * GATE_DIM), F32), _sds((t, GATE_DIM), BF16)],
        (x, g, w_in, ln_g, ln_b, w_s, b_st, w_out))


MLP_W_SPECS = (_res((N_DEV, D_MODEL, FF_SLOT)), _res((N_DEV, FF_SLOT, D_MODEL)))


def _mlp_fwd(h, g, w1, w2):
    t = h.shape[0]

    def body(h_ref, g_ref, w1_ref, w2_ref, o_ref, a_ref):
        hv = h_ref[...]
        hb = _rms_fwd(hv, g_ref[...])[0].astype(BF16)
        o_ref[...] = hv
        for d in range(N_DEV):
            a = _dot(hb, w1_ref[d])
            a_ref[:, d * FF_SLOT:(d + 1) * FF_SLOT] = a
            r = jnp.maximum(a, 0.0)
            o_ref[...] += _dot((r * r).astype(BF16), w2_ref[d])

    return _call(
        "mlp_fwd", body, (t // TM_MLP_FWD,), [_row(D_MODEL, TM_MLP_FWD), _res((1, D_MODEL)), *MLP_W_SPECS],
        [_row(D_MODEL, TM_MLP_FWD), _row(D_FF, TM_MLP_FWD)], [_sds((t, D_MODEL), F32), _sds((t, D_FF), F32)],
        (h, g, w1, w2))


def _mlp_fwd_loss(h, g, w1, w2, final_g, target):
    t = h.shape[0]

    def body(h_ref, g_ref, w1_ref, w2_ref, fg_ref, t_ref, a_ref, loss_ref, dh_ref, dg_ref):
        hv = h_ref[...]
        hb = _rms_fwd(hv, g_ref[...])[0].astype(BF16)
        out = hv
        for d in range(N_DEV):
            a = _dot(hb, w1_ref[d])
            a_ref[:, d * FF_SLOT:(d + 1) * FF_SLOT] = a
            r = jnp.maximum(a, 0.0)
            out = out + _dot((r * r).astype(BF16), w2_ref[d])
        y, xhat, rstd = _rms_fwd(out, fg_ref[...])
        err = y - t_ref[...]
        part = 0.5 * jnp.sum(jnp.mean(err * err, axis=-1, keepdims=True), axis=0, keepdims=True)
        dx, dg = _rms_bwd(err * (1.0 / D_MODEL), xhat, rstd, fg_ref[...])
        dh_ref[...] = dx
        _acc(dg_ref, dg)
        _acc(loss_ref, part)

    return _call(
        "mlp_fwd_loss", body, (t // TM,),
        [_row(D_MODEL), _res((1, D_MODEL)), *MLP_W_SPECS, _res((1, D_MODEL)), _row(D_MODEL)],
        [_row(D_FF), _const((1, 1)), _row(D_MODEL), _const((1, D_MODEL))],
        [_sds((t, D_FF), F32), _sds((1, 1), F32), _sds((t, D_MODEL), F32), _sds((1, D_MODEL), F32)],
        (h, g, w1, w2, final_g, target))


KVQ_W_SPECS = (_res((1, D_MODEL)), _res((D_MODEL, KV_LORA + QK_ROPE)), _res((1, KV_LORA)),
               _res((B_HEADS, KV_LORA, QK_NOPE + V_HEAD)), _res((1, D_MODEL)), _res((D_MODEL, Q_LORA)),
               _res((1, Q_LORA)), _res((B_HEADS, Q_LORA, QK_NOPE + QK_ROPE)))


def _kvq_fwd(h, pos, inv_freq, kvq_w):
    t = h.shape[0]
    half = QK_ROPE // 2

    def body(h_ref, pos_ref, invf_ref, srcg_ref, wkva_ref, kvag_ref, wkvb_ref, mixg_ref, wqa_ref, qg_ref, wqb_ref,
             ckv_ref, k_ref, v_ref, cqpre_ref, q_ref, cos_ref, sin_ref):
        hv = h_ref[...]
        xhat = hv * lax.rsqrt(jnp.mean(hv * hv, axis=-1, keepdims=True) + EPS)
        ang = pos_ref[...].astype(F32) * invf_ref[...]
        cos, sin = jnp.cos(ang), jnp.sin(ang)
        cos_ref[...] = cos
        sin_ref[...] = sin
        ckv = _dot((xhat * srcg_ref[...]).astype(BF16), wkva_ref[...])
        ckv_ref[...] = ckv
        cb = _rms_fwd(ckv[:, :KV_LORA], kvag_ref[...])[0].astype(BF16)
        kpe = _rope(ckv[:, KV_LORA:], cos, sin).astype(BF16)
        for hd in range(B_HEADS):
            kv = _dot(cb, wkvb_ref[hd])
            k_ref[hd, :, 0:QK_NOPE] = kv[:, :QK_NOPE].astype(BF16)
            k_ref[hd, :, QK_NOPE:] = kpe
            v_ref[hd] = kv[:, QK_NOPE:].astype(BF16)
        cqpre = _dot((xhat * mixg_ref[...]).astype(BF16), wqa_ref[...])
        cqpre_ref[...] = cqpre
        cqb = _rms_fwd(cqpre, qg_ref[...])[0].astype(BF16)
        for hd in range(B_HEADS):
            q = _dot(cqb, wqb_ref[hd])
            q_ref[hd, :, 0:QK_NOPE] = q[:, :QK_NOPE].astype(BF16)
            q_ref[hd, :, QK_NOPE:] = _rope(q[:, QK_NOPE:], cos, sin).astype(BF16)

    tm = TM_KVQ
    return _call(
        "kvq_fwd", body, (t // tm,), [_row(D_MODEL, tm), _row(1, tm), _res((1, half)), *KVQ_W_SPECS],
        [_row(KV_LORA + QK_ROPE, tm), _heads(QK_NOPE + QK_ROPE, tm), _heads(V_HEAD, tm), _row(Q_LORA, tm),
         _heads(QK_NOPE + QK_ROPE, tm), _row(half, tm), _row(half, tm)],
        [_sds((t, KV_LORA + QK_ROPE), F32), _sds((B_HEADS, t, QK_NOPE + QK_ROPE), BF16),
         _sds((B_HEADS, t, V_HEAD), BF16), _sds((t, Q_LORA), F32), _sds((B_HEADS, t, QK_NOPE + QK_ROPE), BF16),
         _sds((t, half), F32), _sds((t, half), F32)],
        (h, pos, inv_freq, *kvq_w))


def _softmax_rows(q, k_ref, k):
    past, upto = k * TM, (k + 1) * TM
    s = _dot_nt(q, k_ref[0:upto, :])
    own = jnp.where(_att_mask(0, TM, TM), s[:, past:], jnp.finfo(F32).min)
    s = own if k == 0 else jnp.concatenate([s[:, :past], own], axis=1)
    e = jnp.exp2((s - jnp.max(s, axis=-1, keepdims=True)) * (ATT_SCALE * LOG2_E))
    return e * (1.0 / jnp.sum(e, axis=-1, keepdims=True))


def _for_my_tile(i, nq, fn):
    for k in range(nq):
        @pl.when(i == k)
        def _(k=k):
            fn(k)


def _attn_fwd(h, q, k, v, w_o):
    t = h.shape[0]
    nq, hps = t // TM, HEADS_PER_STEP

    def body(h_ref, q_ref, k_ref, v_ref, wo_ref, o_ref, att_ref):
        i, pair = pl.program_id(0), pl.program_id(1)

        @pl.when(pair == 0)
        def _():
            o_ref[...] = h_ref[...]

        def tile(kt):
            proj = None
            for j in range(hps):
                hd = pair * hps + j
                p = _softmax_rows(q_ref[j], k_ref.at[hd], kt)
                ob = _dot(p.astype(BF16), v_ref[hd, 0:(kt + 1) * TM, :]).astype(BF16)
                att_ref[j] = ob
                proj = _dot(ob, wo_ref[hd]) if proj is None else proj + _dot(ob, wo_ref[hd])
            o_ref[...] += proj

        _for_my_tile(i, nq, tile)

    def per_head(d):
        return pl.BlockSpec((hps, TM, d), lambda i, pair: (pair, i, 0))

    def resident(shape):
        zeros = (0,) * len(shape)
        return pl.BlockSpec(shape, lambda i, pair: zeros, pipeline_mode=pl.Buffered(1))

    tile_spec = pl.BlockSpec((TM, D_MODEL), lambda i, pair: (i, 0))
    return _call(
        "attn_fwd", body, (nq, B_HEADS // hps),
        [tile_spec, per_head(QK_NOPE + QK_ROPE), resident((B_HEADS, t, QK_NOPE + QK_ROPE)),
         resident((B_HEADS, t, V_HEAD)), resident((B_HEADS, V_HEAD, D_MODEL))],
        [tile_spec, per_head(V_HEAD)], [_sds((t, D_MODEL), F32), _sds((B_HEADS, t, V_HEAD), BF16)],
        (h, q, k, v, w_o))


def _mlp_bwd(h, a, dho, g, w1, w2, layer, after=()):
    t = h.shape[0]

    def body(h_ref, a_ref, dho_ref, g_ref, w1_ref, w2_ref, dhi_ref, dg_ref, hn_ref, f_ref, da_ref, dhib_ref):
        gv = g_ref[...]
        y, xhat, rstd = _rms_fwd(h_ref[...], gv)
        hn_ref[...] = y.astype(BF16)
        dho_v = dho_ref[...]
        dhob = dho_v.astype(BF16)
        dhn = jnp.zeros((TM, D_MODEL), F32)
        for d in range(N_DEV):
            cs = slice(d * FF_SLOT, (d + 1) * FF_SLOT)
            r = jnp.maximum(a_ref[:, cs], 0.0)
            f_ref[:, cs] = (r * r).astype(BF16)
            da = (_dot_nt(dhob, w2_ref[d]) * (2.0 * r)).astype(BF16)
            da_ref[:, cs] = da
            dhn = dhn + _dot_nt(da, w1_ref[d])
        dx, dg = _rms_bwd(dhn, xhat, rstd, gv)
        dhi = dho_v + dx
        dhi_ref[...] = dhi
        dhib_ref[...] = dhi.astype(BF16)
        _acc(dg_ref, dg)

    return _call(
        f"mlp_bwd_{layer}", body, (t // TM,),
        [_row(D_MODEL), _row(D_FF), _row(D_MODEL), _res((1, D_MODEL)), *MLP_W_SPECS],
        [_row(D_MODEL), _const((1, D_MODEL)), _row(D_MODEL), _row(D_FF), _row(D_FF), _row(D_MODEL)],
        [_sds((t, D_MODEL), F32), _sds((1, D_MODEL), F32), _sds((t, D_MODEL), BF16), _sds((t, D_FF), BF16),
         _sds((t, D_FF), BF16), _sds((t, D_MODEL), BF16)],
        (h, a, dho, g, w1, w2), after=after)


def _attn_bwd(dh, q, k, v, w_o, cos, sin, after=()):
    t = dh.shape[0]
    half, hps = QK_ROPE // 2, HEADS_PER_STEP

    def body(dh_ref, q_ref, k_ref, v_ref, wo_ref, cos_ref, sin_ref, dq_ref, dk_ref, dv_ref):
        i = pl.program_id(1)

        @pl.when(i == 0)
        def _():
            dk_ref[...] = jnp.zeros_like(dk_ref)
            dv_ref[...] = jnp.zeros_like(dv_ref)

        def tile(kt):
            keys = slice(0, (kt + 1) * TM)
            for j in range(hps):
                qj = q_ref[j]
                do = _dot_nt(dh_ref[kt * TM:(kt + 1) * TM, :], wo_ref[j]).astype(BF16)
                p = _softmax_rows(qj, k_ref.at[j], kt)
                dp = _dot_nt(do, v_ref[j, keys, :])
                ds = (p * (dp - jnp.sum(p * dp, axis=-1, keepdims=True)) * ATT_SCALE).astype(BF16)
                dq = _dot(ds, k_ref[j, keys, :])
                dq_ref[j, :, 0:QK_NOPE] = dq[:, :QK_NOPE].astype(BF16)
                dq_ref[j, :, QK_NOPE:] = _rope(dq[:, QK_NOPE:], cos_ref[...], -sin_ref[...]).astype(BF16)
                dk_ref[j, keys, :] += _dot_tn(ds, qj)
                dv_ref[j, keys, :] += _dot_tn(p.astype(BF16), do)

        _for_my_tile(i, t // TM, tile)

    def per_pair(rows, d, tiled):
        return pl.BlockSpec((hps, rows, d), (lambda pair, i: (pair, i, 0)) if tiled else (lambda pair, i: (pair, 0, 0)))

    def tile(d):
        return pl.BlockSpec((TM, d), lambda pair, i: (i, 0))

    return _call(
        "attn_bwd", body, (B_HEADS // hps, t // TM),
        [pl.BlockSpec((t, D_MODEL), lambda pair, i: (0, 0), pipeline_mode=pl.Buffered(1)),
         per_pair(TM, QK_NOPE + QK_ROPE, True), per_pair(t, QK_NOPE + QK_ROPE, False), per_pair(t, V_HEAD, False),
         per_pair(V_HEAD, D_MODEL, False), tile(half), tile(half)],
        [per_pair(TM, QK_NOPE + QK_ROPE, True), per_pair(t, QK_NOPE + QK_ROPE, False), per_pair(t, V_HEAD, False)],
        [_sds((B_HEADS, t, QK_NOPE + QK_ROPE), BF16), _sds((B_HEADS, t, QK_NOPE + QK_ROPE), F32),
         _sds((B_HEADS, t, V_HEAD), F32)],
        (dh, q, k, v, w_o, cos, sin), after=after)


def _kvq_bwd(h, dh, ckv, cqpre, dq, dk, dv, cos, sin, kvq_w, after=()):
    t = h.shape[0]
    tm = TM
    half, last = QK_ROPE // 2, t // tm - 1
    grad_shapes = [(D_MODEL, Q_LORA), (B_HEADS, Q_LORA, QK_NOPE + QK_ROPE), (D_MODEL, KV_LORA + QK_ROPE),
                   (B_HEADS, KV_LORA, QK_NOPE + V_HEAD)]

    def body(h_ref, dh_ref, ckv_ref, cqpre_ref, dq_ref, dk_ref, dv_ref, cos_ref, sin_ref,
             srcg_ref, wkva_ref, kvag_ref, wkvb_ref, mixg_ref, wqa_ref, qg_ref, wqb_ref,
             dhi_ref, dmixg_ref, dsrcg_ref, dqg_ref, dkvag_ref, gqa_ref, gqb_ref, gkva_ref, gkvb_ref,
             aqa, aqb, akva, akvb):
        @pl.when(pl.program_id(0) == 0)
        def _():
            for acc in (aqa, aqb, akva, akvb):
                acc[...] = jnp.zeros_like(acc)

        hv = h_ref[...]
        rstd = lax.rsqrt(jnp.mean(hv * hv, axis=-1, keepdims=True) + EPS)
        xhat = hv * rstd
        mixg, srcg, qg, kvag = mixg_ref[...], srcg_ref[...], qg_ref[...], kvag_ref[...]
        cq, cqhat, crstd = _rms_fwd(cqpre_ref[...], qg)
        cqb = cq.astype(BF16)
        dcq = jnp.zeros((tm, Q_LORA), F32)
        for hd in range(B_HEADS):
            dcq = dcq + _dot_nt(dq_ref[hd], wqb_ref[hd])
            aqb[hd] += _dot_tn(cqb, dq_ref[hd])
        dcqpre, dqg = _rms_bwd(dcq, cqhat, crstd, qg)
        dcqpre_b = dcqpre.astype(BF16)
        aqa[...] += _dot_tn((xhat * mixg).astype(BF16), dcqpre_b)
        dxq, dmixg = _rms_bwd(_dot_nt(dcqpre_b, wqa_ref[...]), xhat, rstd, mixg)
        ckv = ckv_ref[...]
        c, chat, krstd = _rms_fwd(ckv[:, :KV_LORA], kvag)
        cb = c.astype(BF16)
        dc = jnp.zeros((tm, KV_LORA), F32)
        dkpe = jnp.zeros((tm, QK_ROPE), F32)
        for hd in range(B_HEADS):
            dkv = jnp.concatenate([dk_ref[hd, :, 0:QK_NOPE], dv_ref[hd]], axis=-1).astype(BF16)
            akvb[hd] += _dot_tn(cb, dkv)
            dc = dc + _dot_nt(dkv, wkvb_ref[hd])
            dkpe = dkpe + dk_ref[hd, :, QK_NOPE:]
        dlat, dkvag = _rms_bwd(dc, chat, krstd, kvag)
        dpe = _rope(dkpe, cos_ref[...], -sin_ref[...])
        dckv_b = jnp.concatenate([dlat, dpe], axis=-1).astype(BF16)
        akva[...] += _dot_tn((xhat * srcg).astype(BF16), dckv_b)
        dxk, dsrcg = _rms_bwd(_dot_nt(dckv_b, wkva_ref[...]), xhat, rstd, srcg)
        dhi_ref[...] = dh_ref[...] + dxq + dxk
        _acc(dmixg_ref, dmixg)
        _acc(dsrcg_ref, dsrcg)
        _acc(dqg_ref, dqg)
        _acc(dkvag_ref, dkvag)

        @pl.when(pl.program_id(0) == last)
        def _():
            for out, acc in ((gqa_ref, aqa), (gqb_ref, aqb), (gkva_ref, akva), (gkvb_ref, akvb)):
                out[...] = acc[...].astype(BF16)

    return _call(
        "kvq_bwd", body, (t // tm,),
        [_row(D_MODEL, tm), _row(D_MODEL, tm), _row(KV_LORA + QK_ROPE, tm), _row(Q_LORA, tm),
         _heads(QK_NOPE + QK_ROPE, tm), _heads(QK_NOPE + QK_ROPE, tm), _heads(V_HEAD, tm), _row(half, tm),
         _row(half, tm), *KVQ_W_SPECS],
        [_row(D_MODEL, tm), _const((1, D_MODEL)), _const((1, D_MODEL)), _const((1, Q_LORA)), _const((1, KV_LORA))]
        + [_const(s) for s in grad_shapes],
        [_sds((t, D_MODEL), F32), _sds((1, D_MODEL), F32), _sds((1, D_MODEL), F32), _sds((1, Q_LORA), F32),
         _sds((1, KV_LORA), F32)] + [_sds(s, BF16) for s in grad_shapes],
        (h, dh, ckv, cqpre, dq, dk, dv, cos, sin, *kvq_w), scratch=[pltpu.VMEM(s, F32) for s in grad_shapes],
        after=after)


def _a_mix_bwd(x, z, dh, g, w_in, ln_g, ln_b, w_s, b_st, w_out, after=()):
    t = x.shape[0]
    tm = TM_GATE
    nblk = tm // GMLP_BLOCK

    def body(x_ref, z_ref, dh_ref, g_ref, win_ref, lng_ref, lnb_ref, ws_ref, bst_ref, wout_ref,
             dx_ref, hn_ref, dz_ref, dg_ref, dlng_ref, dlnb_ref, dws_ref, dbs_ref, dvn_scr, gelu_grad_v):
        @pl.when(pl.program_id(0) == 0)
        def _():
            dws_ref[...] = jnp.zeros_like(dws_ref)
            dbs_ref[...] = jnp.zeros_like(dbs_ref)

        gv, lng = g_ref[...], lng_ref[...]
        y, xhat, rstd = _rms_fwd(x_ref[...], gv)
        hn_ref[...] = y.astype(BF16)
        dhv = dh_ref[...]
        dgated = _dot_nt(dhv.astype(BF16), wout_ref[...])
        u, gelu_grad_u = _gelu_and_grad(z_ref[:, :GATE_DIM])
        v, gelu_grad_v[...] = _gelu_and_grad(z_ref[:, GATE_DIM:])
        vn, vhat, lrstd = _ln_fwd(v, lng, lnb_ref[...])
        vb = vn.astype(BF16)
        mask = _gate_mask()
        for gi in range(A_GROUPS):
            wm = jnp.where(mask, ws_ref[gi], 0.0).astype(BF16)
            bias = bst_ref[:, gi:gi + 1]
            cs = slice(gi * A_GROUP_DIM, (gi + 1) * A_GROUP_DIM)
            dws = jnp.zeros((GMLP_BLOCK, GMLP_BLOCK), F32)
            dbs = jnp.zeros((GMLP_BLOCK, 1), F32)
            for n in range(nblk):
                rs = slice(n * GMLP_BLOCK, (n + 1) * GMLP_BLOCK)
                sv = _dot(wm, vb[rs, cs]) + bias
                dz_ref[rs, cs] = (dgated[rs, cs] * sv * gelu_grad_u[rs, cs]).astype(BF16)
                dsv = dgated[rs, cs] * u[rs, cs]
                dsvb = dsv.astype(BF16)
                dws = dws + _dot_nt(dsvb, vb[rs, cs])
                dbs = dbs + jnp.sum(dsv, axis=-1, keepdims=True)
                dvn_scr[rs, cs] = _dot_tn(wm, dsvb)
            dws_ref[gi] += jnp.where(mask, dws, 0.0)
            dbs_ref[gi] += dbs
        dvn = dvn_scr[...]
        dvhat = dvn * lng
        dv = lrstd * (dvhat - jnp.mean(dvhat, axis=-1, keepdims=True)
                      - vhat * jnp.mean(dvhat * vhat, axis=-1, keepdims=True))
        dz_ref[:, GATE_DIM:] = (dv * gelu_grad_v[...]).astype(BF16)
        dhn = jnp.zeros((tm, D_MODEL), F32)
        for d in range(N_DEV):
            dhn = dhn + _dot_nt(dz_ref[:, d * FF_SLOT:(d + 1) * FF_SLOT], win_ref[d])
        dx, dg = _rms_bwd(dhn, xhat, rstd, gv)
        dx_ref[...] = dhv + dx
        _acc(dg_ref, dg)
        _acc(dlng_ref, jnp.sum(dvn * vhat, axis=0, keepdims=True))
        _acc(dlnb_ref, jnp.sum(dvn, axis=0, keepdims=True))

    return _call(
        "a_mix_bwd", body, (t // tm,),
        [_row(D_MODEL, tm), _row(2 * GATE_DIM, tm), _row(D_MODEL, tm), _res((1, D_MODEL)),
         _res((N_DEV, D_MODEL, FF_SLOT)), _res((1, GATE_DIM)), _res((1, GATE_DIM)),
         _res((A_GROUPS, GMLP_BLOCK, GMLP_BLOCK)), _res((GMLP_BLOCK, A_GROUPS)), _res((GATE_DIM, D_MODEL))],
        [_row(D_MODEL, tm), _row(D_MODEL, tm), _row(2 * GATE_DIM, tm),
         _const((1, D_MODEL)), _const((1, GATE_DIM)), _const((1, GATE_DIM)),
         _const((A_GROUPS, GMLP_BLOCK, GMLP_BLOCK)), _const((A_GROUPS, GMLP_BLOCK, 1))],
        [_sds((t, D_MODEL), F32), _sds((t, D_MODEL), BF16),
         _sds((t, 2 * GATE_DIM), BF16), _sds((1, D_MODEL), F32), _sds((1, GATE_DIM), F32),
         _sds((1, GATE_DIM), F32), _sds((A_GROUPS, GMLP_BLOCK, GMLP_BLOCK), F32),
         _sds((A_GROUPS, GMLP_BLOCK, 1), F32)],
        (x, z, dh, g, w_in, ln_g, ln_b, w_s, b_st, w_out),
        scratch=[pltpu.VMEM((tm, GATE_DIM), F32), pltpu.VMEM((tm, GATE_DIM), F32)], after=after)


def _wgrad(name, a, b, a_spec, b_spec, m, n, after=()):
    def body(a_ref, b_ref, o_ref):
        o_ref[0] = _dot_tn(a_ref[...].astype(BF16), b_ref[...].astype(BF16)).astype(BF16)

    return _call(name, body, (N_DEV,), [a_spec, b_spec], [pl.BlockSpec((1, m, n), lambda d: (d, 0, 0))],
                 [_sds((N_DEV, m, n), BF16)], (a, b), after=after)[0]


def _full(t, d):
    return pl.BlockSpec((t, d), lambda i: (0, 0), pipeline_mode=pl.Buffered(1))


def _cols(t, d):
    return pl.BlockSpec((t, d), lambda i: (0, i))


def _head(t, d):
    return pl.BlockSpec((None, t, d), lambda i: (i, 0, 0))


def _local_step(x, pos, target, inv_freq, wg, sm, shards=None):
    t = x.shape[0]
    wg = dict(wg)
    dist = shards is not None
    mix_g = [sm["norm_mix_g"][l:l + 1] for l in range(2)]
    mlp_g = [sm["norm_mlp_g"][l:l + 1] for l in range(2)]

    ids = iter(range(2, 2 + 10))

    def gather(names):
        if dist:
            got = _by_sequencer("gather_" + names[0], _gather_comm([shards[k] for k in names]),
                                SIBLING_AND_NEIGHBOURS, next(ids))
            wg.update(zip(names, got))

    def send(name, names):
        if dist:
            comm = _exchange_comm(grads=[g[k] for k in names])
            g.update(zip(names, _by_sequencer("exchange_" + name, comm, EVERYONE, next(ids))))

    def send_sums(name, names, meanwhile):
        if not dist:
            meanwhile()
            return ()
        grads = [g[k] for k in names]
        landed = _by_sequencer("pair_exchange_" + name, _pair_exchange_comm(grads), (1,), next(ids))
        sums = _pair_add("pair_add_" + name, grads, landed, after=meanwhile())
        g.update(zip(names, _by_sequencer("exchange_" + name, _chip_exchange_comm(sums), OTHER_CHIPS, next(ids))))
        return sums

    def a_args():
        return (wg["a_w_in"], wg["a_ln_v_g"], wg["a_ln_v_b"], sm["a_w_s"], sm["a_b_st"], wg["a_w_out"])

    def kvq_w():
        return (sm["kv_src_norm_g"], wg["kv_w_a"], sm["kv_a_norm_g"], wg["kv_w_b"], mix_g[1], wg["b_w_q_a"],
                sm["b_q_norm_g"], wg["b_w_q_b"])

    gather(("mlp_w1_0", "mlp_w2_0"))
    h1, z, gated = _a_mix_fwd(x, mix_g[0], *a_args())
    gather(("kv_w_a", "kv_w_b", "b_w_q_a", "b_w_q_b", "b_w_o"))
    h2, a0 = _mlp_fwd(h1, mlp_g[0], wg["mlp_w1_0"], wg["mlp_w2_0"])
    if dist:
        wg["b_w_q_a"] = wg["b_w_q_a"].reshape(D_MODEL, Q_LORA)
        wg["kv_w_a"] = wg["kv_w_a"].reshape(D_MODEL, KV_LORA + QK_ROPE)
    gather(("mlp_w1_1", "mlp_w2_1"))
    ckv, k, v, cqpre, q, cos, sin = _kvq_fwd(h2, pos, inv_freq, kvq_w())
    h3, att = _attn_fwd(h2, q, k, v, wg["b_w_o"])
    a1, loss, dh4, d_final_g = _mlp_fwd_loss(h3, mlp_g[1], wg["mlp_w1_1"], wg["mlp_w2_1"], sm["final_norm_g"], target)

    g = {}
    dh3, d_mlp_g1, hn, f, da, dh3_b = _mlp_bwd(h3, a1, dh4, mlp_g[1], wg["mlp_w1_1"], wg["mlp_w2_1"], 1)
    dq, dk, dv = _attn_bwd(dh3_b, q, k, v, wg["b_w_o"], cos, sin)
    g["mlp_w1_1"] = _wgrad("wgrad_w1_1", hn, da, _full(t, D_MODEL), _cols(t, FF_SLOT), D_MODEL, FF_SLOT, after=[dq])
    g["mlp_w2_1"] = _wgrad("wgrad_w2_1", f, dh4, _cols(t, FF_SLOT), _full(t, D_MODEL), FF_SLOT, D_MODEL)

    def wgrad_w_o():
        g["b_w_o"] = _wgrad("wgrad_w_o", att, dh3_b, _head(t, V_HEAD), _full(t, D_MODEL), V_HEAD, D_MODEL)
        return [g["b_w_o"]]

    sums = send_sums("mlp_1", ("mlp_w1_1", "mlp_w2_1"), wgrad_w_o)
    dh2, d_mix_g1, d_src_g, d_q_g, d_kv_a_g, g_q_a, g["b_w_q_b"], g_kv_a, g["kv_w_b"] = _kvq_bwd(
        h2, dh3, ckv, cqpre, dq, dk, dv, cos, sin, kvq_w(), after=sums)
    g["b_w_q_a"] = g_q_a.reshape(N_DEV, D_MODEL // N_DEV, Q_LORA)
    g["kv_w_a"] = g_kv_a.reshape(N_DEV, D_MODEL // N_DEV, KV_LORA + QK_ROPE)
    qkv = ("b_w_q_a", "b_w_q_b", "kv_w_a", "kv_w_b")
    landed = [g[k] for k in qkv]
    send("qkv", qkv)
    dh1, d_mlp_g0, hn, f, da, dh1_b = _mlp_bwd(h1, a0, dh2, mlp_g[0], wg["mlp_w1_0"], wg["mlp_w2_0"], 0,
                                               after=landed if dist else ())
    landed = [g["mlp_w1_1"], g["mlp_w2_1"]] if dist else ()
    g["mlp_w1_0"] = _wgrad("wgrad_w1_0", hn, da, _full(t, D_MODEL), _cols(t, FF_SLOT), D_MODEL, FF_SLOT, after=landed)
    g["mlp_w2_0"] = _wgrad("wgrad_w2_0", f, dh2, _cols(t, FF_SLOT), _full(t, D_MODEL), FF_SLOT, D_MODEL)

    def wgrad_a_w_out():
        g["a_w_out"] = _wgrad("wgrad_a_w_out", gated, dh1_b, _cols(t, GATE_DIM // N_DEV), _full(t, D_MODEL),
                              GATE_DIM // N_DEV, D_MODEL)
        return [g["a_w_out"]] + [g[k] for k in qkv]

    sums = send_sums("mlp_0", ("mlp_w1_0", "mlp_w2_0", "b_w_o"), wgrad_a_w_out)
    if dist:
        sums = _pair_reduce("pair_reduce_a_w_out", [g["a_w_out"]], after=sums)
    dx, hn, dz, d_mix_g0, d_ln_g, d_ln_b, d_ws, d_bs = _a_mix_bwd(x, z, dh1, mix_g[0], *a_args(), after=sums)
    small = {
        "norm_mix_g": jnp.concatenate([d_mix_g0, d_mix_g1], axis=0),
        "norm_mlp_g": jnp.concatenate([d_mlp_g0, d_mlp_g1], axis=0),
        "a_ln_v_g": d_ln_g.reshape(N_DEV, GATE_DIM // N_DEV),
        "a_ln_v_b": d_ln_b.reshape(N_DEV, GATE_DIM // N_DEV),
        "a_w_s": d_ws.astype(BF16) if dist else d_ws,
        "a_b_s": d_bs.reshape(A_GROUPS, GMLP_BLOCK),
        "b_q_norm_g": d_q_g,
        "kv_src_norm_g": d_src_g,
        "kv_a_norm_g": d_kv_a_g,
        "final_norm_g": d_final_g,
    }
    if dist:
        parts = [small[k].reshape((1,) + small[k].shape) for k in SMALL] + [loss.reshape(1, 1, 1)]
        comm = _together(_chip_exchange_comm(sums), _spread_comm(parts))
        g["a_w_out"], *got = _by_sequencer("exchange_a_w_out", comm, EVERYONE, next(ids))
        small, loss = dict(zip(SMALL, got)), got[-1]
    g["a_w_in"] = _wgrad("wgrad_a_w_in", hn, dz, _full(t, D_MODEL), _cols(t, FF_SLOT), D_MODEL, FF_SLOT)
    return loss, dx, g, small


def _adamw(w, g, m, v):
    m = ADAM_B1 * m + (1.0 - ADAM_B1) * g
    v = ADAM_B2 * v + (1.0 - ADAM_B2) * (g * g)
    m_hat = m / (1.0 - ADAM_B1 ** ADAM_STEP)
    v_hat = v / (1.0 - ADAM_B2 ** ADAM_STEP)
    return -ADAM_LR * (m_hat / (jnp.sqrt(v_hat) + ADAM_EPS) + ADAM_WD * w), m, v


def _sum_in_device_order(r_ref):
    g = r_ref[0].astype(F32)
    for j in range(1, r_ref.shape[0]):
        g = g + r_ref[j].astype(F32)
    return g


def _adamw_sharded(name, recvs, w, m, v, swapped=False):
    layers, r, c = w.shape[0], *recvs[0][0].shape[1:]
    tr = r if swapped else math.gcd(r, 512)
    flat = [a for per_layer in recvs for a in per_layer]

    def body(*refs):
        r_refs, (w_ref, m_ref, v_ref) = refs[:len(flat)], refs[len(flat):len(flat) + 3]
        g_ref, d_ref, nm_ref, nv_ref = refs[-4:]
        layer = pl.program_id(0)
        g, pos = None, 0
        for li, per_layer in enumerate(recvs):
            total = None
            for ref in r_refs[pos:pos + len(per_layer)]:
                part = _sum_in_device_order(ref)
                total = part if total is None else total + part
            pos += len(per_layer)
            g = total if g is None else jnp.where(layer == li, total, g)
        if swapped:
            g = g.T
        g_ref[...] = g
        d_ref[...], nm_ref[...], nv_ref[...] = _adamw(w_ref[...], g, m_ref[...], v_ref[...])

    blk = pl.BlockSpec((None, tr, c), lambda l, i: (l, i, 0))
    if swapped:
        blk = pl.BlockSpec((None, c, r), lambda l, i: (l, 0, 0))
    return _call(name, body, (layers, r // tr),
                 [pl.BlockSpec((a.shape[0], tr, c), lambda l, i: (0, i, 0)) for a in flat] + [blk] * 3,
                 [blk] * 4, [_sds(w.shape, F32)] * 4, (*flat, w, m, v))


def _adamw_small(recvs, ws, ms, vs, own_row, losses):
    n = len(recvs)

    def body(*refs):
        r_refs, w_refs, m_refs, v_refs = (refs[i * n:(i + 1) * n] for i in range(4))
        outs, scr = refs[4 * n + 1:8 * n + 2], refs[8 * n + 2:]
        outs[-1][...] = _sum_in_device_order(refs[4 * n])
        me = _my_place()[3]
        for a in range(n):
            g = _sum_in_device_order(r_refs[a])
            if own_row[a]:
                scr[0][...] = g
                g = scr[0][pl.ds(me, 1), :]
            g_ref, d_ref, nm_ref, nv_ref = outs[4 * a:4 * a + 4]
            g_ref[...] = g
            d_ref[...], nm_ref[...], nv_ref[...] = _adamw(w_refs[a][...], g, m_refs[a][...], v_refs[a][...])

    out_shape = []
    for w in ws:
        out_shape += [_sds(w.shape, F32)] * 4
    return pl.pallas_call(
        body, name="adamw_small", in_specs=[VMEM] * (4 * n + 1), out_specs=[VMEM] * (4 * n + 1),
        out_shape=out_shape + [_sds((1, 1), F32)], scratch_shapes=[pltpu.VMEM((N_DEV, GATE_DIM // N_DEV), F32)],
    )(*recvs, *ws, *ms, *vs, losses)


BIG = ("a_w_in", "a_w_out", "b_w_q_a", "b_w_q_b", "b_w_o", "kv_w_a", "kv_w_b", "mlp_w1", "mlp_w2")
SMALL = ("norm_mix_g", "norm_mlp_g", "a_ln_v_g", "a_ln_v_b", "a_w_s", "a_b_s", "b_q_norm_g", "kv_src_norm_g",
         "kv_a_norm_g", "final_norm_g")
WEIGHTS = ("norm_mix_g", "norm_mlp_g", "a_w_in", "a_ln_v_g", "a_ln_v_b", "a_w_s", "a_b_s", "a_w_out", "b_w_q_a",
           "b_q_norm_g", "b_w_q_b", "b_w_o", "kv_src_norm_g", "kv_w_a", "kv_a_norm_g", "kv_w_b", "mlp_w1", "mlp_w2",
           "final_norm_g")


def _two_d(name, a):
    if name in ("a_w_s", "a_b_s"):
        return a.reshape(a.shape[1:])
    return a.reshape(1, -1) if a.ndim == 1 else a


def _three_d(a):
    return a if a.ndim == 3 else a.reshape((1,) + a.shape)


SWAPPED = ("b_w_q_b", "kv_w_a")


def _swapped(a):
    return jnp.swapaxes(_three_d(a), 1, 2)


def kernel(x, positions, norm_mix_g, norm_mlp_g, a_w_in, a_ln_v_g, a_ln_v_b, a_w_s, a_b_s, a_w_out, b_w_q_a, b_q_norm_g, b_w_q_b, b_w_o, kv_src_norm_g, kv_w_a, kv_a_norm_g, kv_w_b, mlp_w1, mlp_w2, final_norm_g, loss_target, m_norm_mix_g, m_norm_mlp_g, m_a_w_in, m_a_ln_v_g, m_a_ln_v_b, m_a_w_s, m_a_b_s, m_a_w_out, m_b_w_q_a, m_b_q_norm_g, m_b_w_q_b, m_b_w_o, m_kv_src_norm_g, m_kv_w_a, m_kv_a_norm_g, m_kv_w_b, m_mlp_w1, m_mlp_w2, m_final_norm_g, v_norm_mix_g, v_norm_mlp_g, v_a_w_in, v_a_ln_v_g, v_a_ln_v_b, v_a_w_s, v_a_b_s, v_a_w_out, v_b_w_q_a, v_b_q_norm_g, v_b_w_q_b, v_b_w_o, v_kv_src_norm_g, v_kv_w_a, v_kv_a_norm_g, v_kv_w_b, v_mlp_w1, v_mlp_w2, v_final_norm_g):
    w = dict(norm_mix_g=norm_mix_g, norm_mlp_g=norm_mlp_g, a_w_in=a_w_in, a_ln_v_g=a_ln_v_g, a_ln_v_b=a_ln_v_b,
             a_w_s=a_w_s, a_b_s=a_b_s, a_w_out=a_w_out, b_w_q_a=b_w_q_a, b_q_norm_g=b_q_norm_g, b_w_q_b=b_w_q_b,
             b_w_o=b_w_o, kv_src_norm_g=kv_src_norm_g, kv_w_a=kv_w_a, kv_a_norm_g=kv_a_norm_g, kv_w_b=kv_w_b,
             mlp_w1=mlp_w1, mlp_w2=mlp_w2, final_norm_g=final_norm_g)
    m = dict(norm_mix_g=m_norm_mix_g, norm_mlp_g=m_norm_mlp_g, a_w_in=m_a_w_in, a_ln_v_g=m_a_ln_v_g,
             a_ln_v_b=m_a_ln_v_b, a_w_s=m_a_w_s, a_b_s=m_a_b_s, a_w_out=m_a_w_out, b_w_q_a=m_b_w_q_a,
             b_q_norm_g=m_b_q_norm_g, b_w_q_b=m_b_w_q_b, b_w_o=m_b_w_o, kv_src_norm_g=m_kv_src_norm_g,
             kv_w_a=m_kv_w_a, kv_a_norm_g=m_kv_a_norm_g, kv_w_b=m_kv_w_b, mlp_w1=m_mlp_w1, mlp_w2=m_mlp_w2,
             final_norm_g=m_final_norm_g)
    v = dict(norm_mix_g=v_norm_mix_g, norm_mlp_g=v_norm_mlp_g, a_w_in=v_a_w_in, a_ln_v_g=v_a_ln_v_g,
             a_ln_v_b=v_a_ln_v_b, a_w_s=v_a_w_s, a_b_s=v_a_b_s, a_w_out=v_a_w_out, b_w_q_a=v_b_w_q_a,
             b_q_norm_g=v_b_q_norm_g, b_w_q_b=v_b_w_q_b, b_w_o=v_b_w_o, kv_src_norm_g=v_kv_src_norm_g,
             kv_w_a=v_kv_w_a, kv_a_norm_g=v_kv_a_norm_g, kv_w_b=v_kv_w_b, mlp_w1=v_mlp_w1, mlp_w2=v_mlp_w2,
             final_norm_g=v_final_norm_g)
    t = x.shape[1]

    first = ("a_w_in", "a_w_out", "a_ln_v_g", "a_ln_v_b")
    later = ("mlp_w1", "mlp_w2", "kv_w_a", "kv_w_b", "b_w_q_a", "b_w_q_b", "b_w_o")
    later_blocks = ("mlp_w1_0", "mlp_w1_1", "mlp_w2_0", "mlp_w2_1") + later[2:]
    got, casts = _gather_first([_three_d(w[k]) if k in BIG else w[k] for k in first],
                               [_swapped(w[k]) if k in SWAPPED else _three_d(w[k]) for k in later],
                               [k in SWAPPED for k in later])
    wg = dict(zip(first, got))
    wg["a_w_out"] = wg["a_w_out"].reshape(GATE_DIM, D_MODEL)
    wg["a_ln_v_g"] = wg["a_ln_v_g"].reshape(1, GATE_DIM)
    wg["a_ln_v_b"] = wg["a_ln_v_b"].reshape(1, GATE_DIM)
    shards = dict(zip(later_blocks, casts))

    sm = {k: _two_d(k, w[k]) for k in SMALL if k not in ("a_ln_v_g", "a_ln_v_b")}
    sm["a_b_st"] = sm["a_b_s"].T
    inv_freq = (ROPE_THETA ** (-jnp.arange(0, QK_ROPE, 2, dtype=F32) / QK_ROPE)).reshape(1, QK_ROPE // 2)

    losses, dx, g, small = _local_step(x[0], positions.reshape(t, 1), loss_target[0], inv_freq, wg, sm, shards)

    sums = _pair_reduce("pair_reduce_a_w_in", [g["a_w_in"]], after=[g["mlp_w1_0"], g["mlp_w2_0"]])
    g["a_w_in"], = _by_sequencer("exchange_last", _chip_exchange_comm(sums), OTHER_CHIPS, collective_id=1)

    out = {}
    for k in BIG:
        recvs = [[g[k + "_0"]], [g[k + "_1"]]] if k.startswith("mlp") else [[g[k]]]
        view = _swapped if k in SWAPPED else _three_d
        res = _adamw_sharded("adamw_" + k, recvs, view(w[k]), view(m[k]), view(v[k]), swapped=k in SWAPPED)
        out[k] = [view(o).reshape(w[k].shape) for o in res]
    own_row = [k in ("a_ln_v_g", "a_ln_v_b") for k in SMALL]
    res = _adamw_small([small[k] for k in SMALL], [_two_d(k, w[k]) for k in SMALL], [_two_d(k, m[k]) for k in SMALL],
                       [_two_d(k, v[k]) for k in SMALL], own_row, losses)
    for i, k in enumerate(SMALL):
        out[k] = [o.reshape(w[k].shape) for o in res[4 * i:4 * i + 4]]

    return (res[-1].reshape(()), dx.reshape(x.shape), *[out[k][0] for k in WEIGHTS], *[out[k][1] for k in WEIGHTS],
            *[out[k][2] for k in WEIGHTS], *[out[k][3] for k in WEIGHTS])
```

```python
import math

import jax
import jax.numpy as jnp
from jax import lax
from jax.experimental import pallas as pl
from jax.experimental.pallas import tpu as pltpu
from jax.experimental.pallas import tpu_sc as plsc

F32, BF16 = jnp.float32, jnp.bfloat16
MESH = pl.DeviceIdType.MESH
ANY = pl.BlockSpec(memory_space=pl.ANY)
VMEM = pl.BlockSpec(memory_space=pltpu.VMEM)

N_DEV = 8
D_MODEL = 1024
CHUNK = 64
GMLP_BLOCK = 128
GATE_DIM = 2048
A_GROUPS = 8
A_GROUP_DIM = GATE_DIM // A_GROUPS
B_HEADS = 8
QK_NOPE, QK_ROPE, V_HEAD = 128, 64, 128
Q_LORA, KV_LORA = 384, 256
ROPE_THETA = 10000.0
D_FF = 4096
FF_SLOT = D_FF // N_DEV
EPS = 1e-6
ATT_SCALE = (QK_NOPE + QK_ROPE) ** -0.5

ADAM_LR, ADAM_B1, ADAM_B2, ADAM_EPS, ADAM_WD, ADAM_STEP = 0.001, 0.9, 0.999, 1e-08, 0.01, 10

TM = 256
TM_GATE = 256
TM_MLP_FWD = 512
TM_KVQ = 512
VMEM_LIMIT = 56 * 1024 * 1024
INV_SQRT2 = 1.0 / math.sqrt(2.0)
INV_SQRT_2PI = 1.0 / math.sqrt(2.0 * math.pi)
LOG2_E = 1.0 / math.log(2.0)
HEADS_PER_STEP = 2


def _dot(a, b):
    return jnp.dot(a, b, preferred_element_type=F32)


def _dot_nt(a, b):
    return lax.dot_general(a, b, (((1,), (1,)), ((), ())), preferred_element_type=F32)


def _dot_tn(a, b):
    return lax.dot_general(a, b, (((0,), (0,)), ((), ())), preferred_element_type=F32)


def _rms_fwd(x, g):
    rstd = lax.rsqrt(jnp.mean(x * x, axis=-1, keepdims=True) + EPS)
    xhat = x * rstd
    return xhat * g, xhat, rstd


def _rms_bwd(dy, xhat, rstd, g):
    dxhat = dy * g
    dx = rstd * (dxhat - xhat * jnp.mean(dxhat * xhat, axis=-1, keepdims=True))
    return dx, jnp.sum(dy * xhat, axis=0, keepdims=True)


def _ln_fwd(v, g, b):
    mu = jnp.mean(v, axis=-1, keepdims=True)
    vc = v - mu
    rstd = lax.rsqrt(jnp.mean(vc * vc, axis=-1, keepdims=True) + EPS)
    vhat = vc * rstd
    return vhat * g + b, vhat, rstd


def _gelu(x):
    return 0.5 * x * (1.0 + lax.erf(x * INV_SQRT2))


def _gelu_and_grad(x):
    cdf = 0.5 * (1.0 + lax.erf(x * INV_SQRT2))
    return x * cdf, cdf + x * jnp.exp(-0.5 * x * x) * INV_SQRT_2PI


def _rope(x, cos, sin):
    x1, x2 = x[:, :QK_ROPE // 2], x[:, QK_ROPE // 2:]
    return jnp.concatenate([x1 * cos - x2 * sin, x2 * cos + x1 * sin], axis=-1)


def _gate_mask():
    row = lax.broadcasted_iota(jnp.int32, (GMLP_BLOCK, GMLP_BLOCK), 0)
    col = lax.broadcasted_iota(jnp.int32, (GMLP_BLOCK, GMLP_BLOCK), 1)
    return (col < CHUNK) | (row >= CHUNK)


def _att_mask(q0, tq, t):
    q = q0 + lax.broadcasted_iota(jnp.int32, (tq, t), 0)
    k = lax.broadcasted_iota(jnp.int32, (tq, t), 1)
    return jnp.right_shift(k, 6) <= jnp.right_shift(q, 6)


def _res(shape, imap=None):
    zeros = (0,) * len(shape)
    return pl.BlockSpec(shape, imap or (lambda i: zeros), pipeline_mode=pl.Buffered(1))


def _const(shape):
    zeros = (0,) * len(shape)
    return pl.BlockSpec(shape, lambda i: zeros)


def _row(d, tm=TM):
    return pl.BlockSpec((tm, d), lambda i: (i, 0))


def _heads(d, tm=TM):
    return pl.BlockSpec((B_HEADS, tm, d), lambda i: (0, i, 0))


def _sds(shape, dt):
    return jax.ShapeDtypeStruct(shape, dt)


def _acc(ref, val):
    @pl.when(pl.program_id(0) == 0)
    def _():
        ref[...] = jnp.zeros_like(ref)
    ref[...] += val


def _my_place():
    x, y, c = lax.axis_index("x"), lax.axis_index("y"), lax.axis_index("c")
    return x, y, c, 4 * x + 2 * y + c


def _peer(x, y, c, k):
    px = 1 - x if k & 4 else x
    py = 1 - y if k & 2 else y
    pc = 1 - c if k & 1 else c
    return (px, py, pc), 4 * px + 2 * py + pc


CHIPS = (2, 4, 6)


def _splits(ref):
    return len(ref.shape) >= 3 and ref.shape[1] % 32 == 0


def _piece(ref, block, half=None):
    if half is None or not _splits(ref):
        return ref.at[pl.ds(block, 1)]
    rows = ref.shape[1] // 2
    return ref.at[pl.ds(block, 1), pl.ds(half * rows, rows)]


def _gather_copy(sems, a, k, piece, to, src=None):
    return pltpu.make_async_remote_copy(
        src_ref=piece if src is None else src, dst_ref=piece, send_sem=sems[0].at[a, k], recv_sem=sems[1].at[a, k],
        device_id=to, device_id_type=MESH)


def _gather_start(srcs, outs, sems, only=None):
    x, y, c, me = _my_place()
    for a in range(len(srcs)) if only is None else (only,):
        mine = _piece(outs[a], me)
        pltpu.make_async_copy(srcs[a], mine, sems[2].at[a]).start()
        for k, rel in enumerate((1, 4, 2)):
            _gather_copy(sems, a, k, mine, _peer(x, y, c, rel)[0], src=srcs[a]).start()


def _gather_relay(srcs, outs, sems):
    x, y, c, _ = _my_place()
    sib = _peer(x, y, c, 1)[0]
    (xn, xn_i), (yn, yn_i) = _peer(x, y, c, 4), _peer(x, y, c, 2)
    for a in range(len(srcs)):
        out = outs[a]
        _gather_copy(sems, a, 1, _piece(out, xn_i), xn).wait_recv()
        _gather_copy(sems, a, 3, _piece(out, xn_i, 0), yn).start()
        _gather_copy(sems, a, 5, _piece(out, xn_i), sib).start()
        _gather_copy(sems, a, 2, _piece(out, yn_i), yn).wait_recv()
        if _splits(out):
            _gather_copy(sems, a, 4, _piece(out, yn_i, 1), xn).start()
        _gather_copy(sems, a, 6, _piece(out, yn_i), sib).start()


def _gather_finish(srcs, outs, sems):
    x, y, c, me = _my_place()
    sib = _peer(x, y, c, 1)[0]
    xn, yn, dg_i = _peer(x, y, c, 4)[0], _peer(x, y, c, 2)[0], _peer(x, y, c, 6)[1]
    n = len(srcs)
    for a in range(n):
        out = outs[a]
        _gather_copy(sems, a, 3, _piece(out, dg_i, 0), yn).wait_recv()
        _gather_copy(sems, a, 7, _piece(out, dg_i, 0), sib).start()
        if _splits(out):
            _gather_copy(sems, a, 4, _piece(out, dg_i, 1), xn).wait_recv()
            _gather_copy(sems, a, 8, _piece(out, dg_i, 1), sib).start()
    for a in range(n):
        out = outs[a]
        whole, half = _piece(out, me), _piece(out, me, 0)
        for k in (0, 5, 6):
            _gather_copy(sems, a, k, whole, sib).wait_recv()
        for k in (7, 8) if _splits(out) else (7,):
            _gather_copy(sems, a, k, half, sib).wait_recv()
        for k in (0, 1, 2):
            _gather_copy(sems, a, k, whole, sib, src=srcs[a]).wait_send()
        for k in (5, 6):
            _gather_copy(sems, a, k, whole, sib).wait_send()
        for k in (3, 4, 7, 8) if _splits(out) else (3, 7):
            _gather_copy(sems, a, k, half, sib).wait_send()
        pltpu.make_async_copy(srcs[a], whole, sems[2].at[a]).wait()


def _relay_sems(n):
    return [pltpu.SemaphoreType.DMA((n, 9)), pltpu.SemaphoreType.DMA((n, 9)), pltpu.SemaphoreType.DMA((n,))]


def _gather_sems(n):
    return [pltpu.SemaphoreType.DMA((n, 7)), pltpu.SemaphoreType.DMA((n, 7)), pltpu.SemaphoreType.DMA((n,))]


class _Comm:
    def __init__(self, args, out_shape, scratch, start, finish, relay=None):
        self.args, self.out_shape, self.scratch, self.start, self.finish = args, out_shape, scratch, start, finish
        self.relay = relay


def _gather_comm(shards):
    return _Comm(list(shards), [_sds((N_DEV,) + s.shape[1:], s.dtype) for s in shards], _relay_sems(len(shards)),
                 _gather_start, _gather_finish, relay=_gather_relay)


def _together(big, small):
    na, no, ns = len(big.args), len(big.out_shape), len(big.scratch)

    def start(src, dst, sems):
        small.start(src[na:], dst[no:], sems[ns:])

    def relay(src, dst, sems):
        small.relay(src[na:], dst[no:], sems[ns:])
        big.start(src[:na], dst[:no], sems[:ns])

    def finish(src, dst, sems):
        small.finish(src[na:], dst[no:], sems[ns:])
        big.finish(src[:na], dst[:no], sems[:ns])

    assert big.relay is None and small.relay is not None
    return _Comm(big.args + small.args, big.out_shape + small.out_shape, list(big.scratch) + list(small.scratch),
                 start, finish, relay=relay)


def _direct_copies(ins, outs, sems, wait):
    send_sems, recv_sems, local_sems = sems
    x, y, c, me = _my_place()
    for a in range(len(ins)):
        local = pltpu.make_async_copy(ins[a].at[pl.ds(me, 1)], outs[a].at[pl.ds(me, 1)], local_sems.at[a])
        local.wait() if wait else local.start()
        for k in range(1, N_DEV):
            to, to_i = _peer(x, y, c, k)
            cp = pltpu.make_async_remote_copy(
                src_ref=ins[a].at[pl.ds(to_i, 1)], dst_ref=outs[a].at[pl.ds(me, 1)],
                send_sem=send_sems.at[a, k - 1], recv_sem=recv_sems.at[a, k - 1], device_id=to, device_id_type=MESH)
            cp.wait() if wait else cp.start()


def _exchange_comm(grads):
    return _Comm(list(grads), [_sds(g.shape, g.dtype) for g in grads], _gather_sems(len(grads)),
                 lambda i, o, s: _direct_copies(i, o, s, False), lambda i, o, s: _direct_copies(i, o, s, True))


def _chip_copies(ins, outs, sems, wait):
    send_sems, recv_sems, local_sems = sems
    x, y, c, _ = _my_place()
    for a in range(len(ins)):
        local = pltpu.make_async_copy(ins[a].at[pl.ds(2 * x + y, 1)], outs[a].at[pl.ds(len(CHIPS), 1)],
                                      local_sems.at[a])
        local.wait() if wait else local.start()
        for i, k in enumerate(CHIPS):
            to = _peer(x, y, c, k)[0]
            cp = pltpu.make_async_remote_copy(
                src_ref=ins[a].at[pl.ds(2 * to[0] + to[1], 1)], dst_ref=outs[a].at[pl.ds(i, 1)],
                send_sem=send_sems.at[a, i], recv_sem=recv_sems.at[a, i], device_id=to, device_id_type=MESH)
            cp.wait() if wait else cp.start()


def _chip_exchange_comm(sums):
    n = len(sums)
    sems = [pltpu.SemaphoreType.DMA((n, len(CHIPS))), pltpu.SemaphoreType.DMA((n, len(CHIPS))),
            pltpu.SemaphoreType.DMA((n,))]
    return _Comm(list(sums), [_sds(s.shape, s.dtype) for s in sums], sems,
                 lambda i, o, s: _chip_copies(i, o, s, False), lambda i, o, s: _chip_copies(i, o, s, True))


def _pair_reduce(name, grads, after=()):
    n = len(grads)
    n_chips = N_DEV // 2

    def body(*refs):
        g_refs, gh_refs, refs = refs[:n], refs[n:2 * n], refs[2 * n + len(after):]
        p_refs, land = refs[:n], refs[n:2 * n]
        send_sems, recv_sems = refs[2 * n:]
        x, y, c, _ = _my_place()
        sib = _peer(x, y, c, 1)[0]
        q = pl.program_id(0)

        def to_sibling(a, j):
            return pltpu.make_async_remote_copy(
                src_ref=gh_refs[a].at[j, pl.ds(1 - c, 1)], dst_ref=land[a].at[pl.ds(j, 1)],
                send_sem=send_sems.at[a, j], recv_sem=recv_sems.at[a, j], device_id=sib, device_id_type=MESH)

        @pl.when(q == 0)
        def _():
            for j in range(n_chips):
                for a in range(n):
                    to_sibling(a, j).start()

        for a in range(n):
            to_sibling(a, q).wait_recv()
            p_refs[a][...] = (g_refs[a][0, pl.ds(c, 1)].astype(F32) + land[a][pl.ds(q, 1)].astype(F32)).astype(BF16)

        @pl.when(q == n_chips - 1)
        def _():
            for a in range(n):
                for j in range(n_chips):
                    to_sibling(a, j).wait_send()

    views = [g.reshape((n_chips, 2) + g.shape[1:]) for g in grads]
    res = pl.pallas_call(
        body, name=name, grid=(n_chips,),
        in_specs=[pl.BlockSpec((1, 2) + g.shape[1:], lambda q: (q, 0, 0, 0)) for g in grads]
        + [ANY] * (n + len(after)),
        out_specs=[pl.BlockSpec((1,) + g.shape[1:], lambda q: (q, 0, 0)) for g in grads],
        out_shape=[_sds((n_chips,) + g.shape[1:], BF16) for g in grads],
        scratch_shapes=[pltpu.VMEM((n_chips,) + g.shape[1:], BF16) for g in grads]
        + [pltpu.SemaphoreType.DMA((n, n_chips)), pltpu.SemaphoreType.DMA((n, n_chips))],
        compiler_params=pltpu.CompilerParams(dimension_semantics=("arbitrary",), vmem_limit_bytes=VMEM_LIMIT),
    )(*views, *views, *after)
    return list(res)


def _pair_exchange_comm(grads):
    n, n_chips = len(grads), N_DEV // 2

    def copies(ins, outs, sems, wait):
        x, y, c, _ = _my_place()
        for j in range(n_chips):
            for a in range(n):
                cp = pltpu.make_async_remote_copy(
                    src_ref=ins[a].at[j, pl.ds(1 - c, 1)], dst_ref=outs[a].at[pl.ds(j, 1)], send_sem=sems[0].at[a, j],
                    recv_sem=sems[1].at[a, j], device_id=_peer(x, y, c, 1)[0], device_id_type=MESH)
                cp.wait() if wait else cp.start()

    views = [g.reshape((n_chips, 2) + g.shape[1:]) for g in grads]
    sems = [pltpu.SemaphoreType.DMA((n, n_chips)), pltpu.SemaphoreType.DMA((n, n_chips))]
    return _Comm(views, [_sds((n_chips,) + g.shape[1:], g.dtype) for g in grads], sems,
                 lambda i, o, s: copies(i, o, s, False), lambda i, o, s: copies(i, o, s, True))


def _pair_add(name, grads, landed, after=()):
    n, n_chips = len(grads), N_DEV // 2

    def body(core_ref, *refs):
        g_refs, l_refs, p_refs = refs[:n], refs[n:2 * n], refs[2 * n + len(after):]
        for a in range(n):
            p_refs[a][...] = (g_refs[a][...].astype(F32) + l_refs[a][...].astype(F32)).astype(BF16)

    views = [g.reshape((n_chips, 2) + g.shape[1:]) for g in grads]
    blocks = [pl.BlockSpec((1,) + g.shape[1:], lambda q, core: (q, 0, 0)) for g in grads]
    mine = [pl.BlockSpec((1, None) + g.shape[1:], lambda q, core: (q, core[0], 0, 0)) for g in grads]
    return list(pl.pallas_call(
        body, name=name, out_shape=[_sds((n_chips,) + g.shape[1:], BF16) for g in grads],
        grid_spec=pltpu.PrefetchScalarGridSpec(num_scalar_prefetch=1, grid=(n_chips,),
                                               in_specs=mine + blocks + [ANY] * len(after), out_specs=blocks),
        compiler_params=pltpu.CompilerParams(dimension_semantics=("arbitrary",), vmem_limit_bytes=VMEM_LIMIT),
    )(lax.axis_index("c").reshape(1), *views, *landed, *after))


def _call(name, body, grid, in_specs, out_specs, out_shape, args, scratch=(), after=()):
    ni, na = len(in_specs), len(after)

    def ordered(*refs):
        body(*refs[:ni], *refs[ni + na:])

    return list(pl.pallas_call(
        ordered if after else body, name=name, grid=grid, in_specs=list(in_specs) + [ANY] * na,
        out_specs=list(out_specs), out_shape=list(out_shape), scratch_shapes=list(scratch),
        compiler_params=pltpu.CompilerParams(dimension_semantics=("arbitrary",) * len(grid),
                                             vmem_limit_bytes=VMEM_LIMIT))(*args, *after))


SIBLING_AND_NEIGHBOURS, OTHER_CHIPS, EVERYONE = (1, 4, 2), CHIPS, tuple(range(1, N_DEV))


def _by_sequencer(name, comm, peers, collective_id):
    src = [jax.new_ref(a, memory_space=pltpu.MemorySpace.HBM) for a in comm.args]
    dst = [jax.empty_ref(s, memory_space=pltpu.MemorySpace.HBM) for s in comm.out_shape]

    @pl.kernel(mesh=plsc.ScalarSubcoreMesh(axis_name="sequencer", num_cores=1), name=name,
               scratch_types=tuple(comm.scratch), compiler_params=pltpu.CompilerParams(collective_id=collective_id))
    def launch(*sems):
        x, y, c, _ = _my_place()
        barrier = pltpu.get_barrier_semaphore()
        for k in peers:
            pl.semaphore_signal(barrier, inc=1, device_id=_peer(x, y, c, k)[0], device_id_type=MESH)
        pl.semaphore_wait(barrier, len(peers))
        comm.start(src, dst, sems)
        if comm.relay is not None:
            comm.relay(src, dst, sems)
        comm.finish(src, dst, sems)

    launch()
    return [d[...] for d in dst]


def _gather_first(first, later, swapped):
    nf = len(first)
    layer_of = [(a, l) for a, s in enumerate(later) for l in range(s.shape[0])]
    nl = len(layer_of)
    dts = [BF16] * (nf - 2) + [F32, F32]
    shard = [s.shape[:0:-1] if sw else s.shape[1:] for s, sw in zip(later, swapped)]

    def body(*refs):
        ins, refs = refs[:nf + len(later)], refs[nf + len(later):]
        outs, refs = refs[:nf], refs[nf:]
        casts, refs = refs[:nl], refs[nl:]
        stage, sems = refs[:nf], refs[nf:]
        for a in range(nf):
            stage[a][...] = ins[a][...].astype(dts[a])
            _gather_start(stage, outs, sems, only=a)
        for k, (a, l) in enumerate(layer_of):
            block = ins[nf + a][l]
            casts[k][0] = (block.T if swapped[a] else block).astype(BF16)
        _gather_relay(stage, outs, sems)
        _gather_finish(stage, outs, sems)

    res = pl.pallas_call(
        body, name="gather_first",
        in_specs=[VMEM] * (nf + len(later)), out_specs=[ANY] * nf + [VMEM] * nl,
        out_shape=[_sds((N_DEV,) + s.shape[1:], dt) for s, dt in zip(first, dts)]
        + [_sds((1,) + shard[a], BF16) for a, _ in layer_of],
        scratch_shapes=[pltpu.VMEM(s.shape, dt) for s, dt in zip(first, dts)] + _relay_sems(nf),
        compiler_params=pltpu.CompilerParams(vmem_limit_bytes=VMEM_LIMIT),
    )(*first, *later)
    return list(res[:nf]), list(res[nf:])


def _a_mix_fwd(x, g, w_in, ln_g, ln_b, w_s, b_st, w_out):
    t = x.shape[0]
    nblk = TM // GMLP_BLOCK

    def body(x_ref, g_ref, win_ref, lng_ref, lnb_ref, ws_ref, bst_ref, wout_ref, h_ref, z_ref, gated_scr):
        xv = x_ref[...]
        hb = _rms_fwd(xv, g_ref[...])[0].astype(BF16)
        for d in range(N_DEV):
            z_ref[:, d * FF_SLOT:(d + 1) * FF_SLOT] = _dot(hb, win_ref[d])
        u = _gelu(z_ref[:, :GATE_DIM])
        vb = _ln_fwd(_gelu(z_ref[:, GATE_DIM:]), lng_ref[...], lnb_ref[...])[0].astype(BF16)
        mask = _gate_mask()
        for gi in range(A_GROUPS):
            wm = jnp.where(mask, ws_ref[gi], 0.0).astype(BF16)
            bias = bst_ref[:, gi:gi + 1]
            cs = slice(gi * A_GROUP_DIM, (gi + 1) * A_GROUP_DIM)
            for n in range(nblk):
                rs = slice(n * GMLP_BLOCK, (n + 1) * GMLP_BLOCK)
                sv = _dot(wm, vb[rs, cs]) + bias
                gated_scr[rs, cs] = (u[rs, cs] * sv).astype(BF16)
        h_ref[...] = xv + _dot(gated_scr[...], wout_ref[...])

    return _call(
        "a_mix_fwd", body, (t // TM,),
        [_row(D_MODEL), _res((1, D_MODEL)), _res((N_DEV, D_MODEL, FF_SLOT)), _res((1, GATE_DIM)),
         _res((1, GATE_DIM)), _res((A_GROUPS, GMLP_BLOCK, GMLP_BLOCK)), _res((GMLP_BLOCK, A_GROUPS)),
         _res((GATE_DIM, D_MODEL))],
        [_row(D_MODEL), _row(2 * GATE_DIM), _row(GATE_DIM)],
        [_sds((t, D_MODEL), F32), _sds((t, 2 * GATE_DIM), F32), _sds((t, GATE_DIM), BF16)],
        (x, g, w_in, ln_g, ln_b, w_s, b_st, w_out))


MLP_W_SPECS = (_res((N_DEV, D_MODEL, FF_SLOT)), _res((N_DEV, FF_SLOT, D_MODEL)))


def _mlp_fwd(h, g, w1, w2):
    t = h.shape[0]

    def body(h_ref, g_ref, w1_ref, w2_ref, o_ref, a_ref):
        hv = h_ref[...]
        hb = _rms_fwd(hv, g_ref[...])[0].astype(BF16)
        o_ref[...] = hv
        for d in range(N_DEV):
            a = _dot(hb, w1_ref[d])
            a_ref[:, d * FF_SLOT:(d + 1) * FF_SLOT] = a
            r = jnp.maximum(a, 0.0)
            o_ref[...] += _dot((r * r).astype(BF16), w2_ref[d])

    return _call(
        "mlp_fwd", body, (t // TM_MLP_FWD,), [_row(D_MODEL, TM_MLP_FWD), _res((1, D_MODEL)), *MLP_W_SPECS],
        [_row(D_MODEL, TM_MLP_FWD), _row(D_FF, TM_MLP_FWD)], [_sds((t, D_MODEL), F32), _sds((t, D_FF), F32)],
        (h, g, w1, w2))


def _mlp_fwd_loss(h, g, w1, w2, final_g, target):
    t = h.shape[0]

    def body(h_ref, g_ref, w1_ref, w2_ref, fg_ref, t_ref, a_ref, loss_ref, dh_ref, dg_ref):
        hv = h_ref[...]
        hb = _rms_fwd(hv, g_ref[...])[0].astype(BF16)
        out = hv
        for d in range(N_DEV):
            a = _dot(hb, w1_ref[d])
            a_ref[:, d * FF_SLOT:(d + 1) * FF_SLOT] = a
            r = jnp.maximum(a, 0.0)
            out = out + _dot((r * r).astype(BF16), w2_ref[d])
        y, xhat, rstd = _rms_fwd(out, fg_ref[...])
        err = y - t_ref[...]
        part = 0.5 * jnp.sum(jnp.mean(err * err, axis=-1, keepdims=True), axis=0, keepdims=True)
        dx, dg = _rms_bwd(err * (1.0 / D_MODEL), xhat, rstd, fg_ref[...])
        dh_ref[...] = dx
        _acc(dg_ref, dg)
        _acc(loss_ref, part)

    return _call(
        "mlp_fwd_loss", body, (t // TM,),
        [_row(D_MODEL), _res((1, D_MODEL)), *MLP_W_SPECS, _res((1, D_MODEL)), _row(D_MODEL)],
        [_row(D_FF), _const((1, 1)), _row(D_MODEL), _const((1, D_MODEL))],
        [_sds((t, D_FF), F32), _sds((1, 1), F32), _sds((t, D_MODEL), F32), _sds((1, D_MODEL), F32)],
        (h, g, w1, w2, final_g, target))


KVQ_W_SPECS = (_res((1, D_MODEL)), _res((D_MODEL, KV_LORA + QK_ROPE)), _res((1, KV_LORA)),
               _res((B_HEADS, KV_LORA, QK_NOPE + V_HEAD)), _res((1, D_MODEL)), _res((D_MODEL, Q_LORA)),
               _res((1, Q_LORA)), _res((B_HEADS, Q_LORA, QK_NOPE + QK_ROPE)))


def _kvq_fwd(h, pos, inv_freq, kvq_w):
    t = h.shape[0]
    half = QK_ROPE // 2

    def body(h_ref, pos_ref, invf_ref, srcg_ref, wkva_ref, kvag_ref, wkvb_ref, mixg_ref, wqa_ref, qg_ref, wqb_ref,
             ckv_ref, k_ref, v_ref, cqpre_ref, q_ref, cos_ref, sin_ref):
        hv = h_ref[...]
        xhat = hv * lax.rsqrt(jnp.mean(hv * hv, axis=-1, keepdims=True) + EPS)
        ang = pos_ref[...].astype(F32) * invf_ref[...]
        cos, sin = jnp.cos(ang), jnp.sin(ang)
        cos_ref[...] = cos
        sin_ref[...] = sin
        ckv = _dot((xhat * srcg_ref[...]).astype(BF16), wkva_ref[...])
        ckv_ref[...] = ckv
        cb = _rms_fwd(ckv[:, :KV_LORA], kvag_ref[...])[0].astype(BF16)
        kpe = _rope(ckv[:, KV_LORA:], cos, sin).astype(BF16)
        for hd in range(B_HEADS):
            kv = _dot(cb, wkvb_ref[hd])
            k_ref[hd, :, 0:QK_NOPE] = kv[:, :QK_NOPE].astype(BF16)
            k_ref[hd, :, QK_NOPE:] = kpe
            v_ref[hd] = kv[:, QK_NOPE:].astype(BF16)
        cqpre = _dot((xhat * mixg_ref[...]).astype(BF16), wqa_ref[...])
        cqpre_ref[...] = cqpre
        cqb = _rms_fwd(cqpre, qg_ref[...])[0].astype(BF16)
        for hd in range(B_HEADS):
            q = _dot(cqb, wqb_ref[hd])
            q_ref[hd, :, 0:QK_NOPE] = q[:, :QK_NOPE].astype(BF16)
            q_ref[hd, :, QK_NOPE:] = _rope(q[:, QK_NOPE:], cos, sin).astype(BF16)

    tm = TM_KVQ
    return _call(
        "kvq_fwd", body, (t // tm,), [_row(D_MODEL, tm), _row(1, tm), _res((1, half)), *KVQ_W_SPECS],
        [_row(KV_LORA + QK_ROPE, tm), _heads(QK_NOPE + QK_ROPE, tm), _heads(V_HEAD, tm), _row(Q_LORA, tm),
         _heads(QK_NOPE + QK_ROPE, tm), _row(half, tm), _row(half, tm)],
        [_sds((t, KV_LORA + QK_ROPE), F32), _sds((B_HEADS, t, QK_NOPE + QK_ROPE), BF16),
         _sds((B_HEADS, t, V_HEAD), BF16), _sds((t, Q_LORA), F32), _sds((B_HEADS, t, QK_NOPE + QK_ROPE), BF16),
         _sds((t, half), F32), _sds((t, half), F32)],
        (h, pos, inv_freq, *kvq_w))


def _softmax_rows(q, k_ref, k):
    past, upto = k * TM, (k + 1) * TM
    s = _dot_nt(q, k_ref[0:upto, :])
    own = jnp.where(_att_mask(0, TM, TM), s[:, past:], jnp.finfo(F32).min)
    s = own if k == 0 else jnp.concatenate([s[:, :past], own], axis=1)
    e = jnp.exp2((s - jnp.max(s, axis=-1, keepdims=True)) * (ATT_SCALE * LOG2_E))
    return e * (1.0 / jnp.sum(e, axis=-1, keepdims=True))


def _for_my_tile(i, nq, fn):
    for k in range(nq):
        @pl.when(i == k)
        def _(k=k):
            fn(k)


def _attn_fwd(h, q, k, v, w_o):
    t = h.shape[0]
    nq, hps = t // TM, HEADS_PER_STEP

    def body(h_ref, q_ref, k_ref, v_ref, wo_ref, o_ref, att_ref):
        i, pair = pl.program_id(0), pl.program_id(1)

        @pl.when(pair == 0)
        def _():
            o_ref[...] = h_ref[...]

        def tile(kt):
            proj = None
            for j in range(hps):
                hd = pair * hps + j
                p = _softmax_rows(q_ref[j], k_ref.at[hd], kt)
                ob = _dot(p.astype(BF16), v_ref[hd, 0:(kt + 1) * TM, :]).astype(BF16)
                att_ref[j] = ob
                proj = _dot(ob, wo_ref[hd]) if proj is None else proj + _dot(ob, wo_ref[hd])
            o_ref[...] += proj

        _for_my_tile(i, nq, tile)

    def per_head(d):
        return pl.BlockSpec((hps, TM, d), lambda i, pair: (pair, i, 0))

    def resident(shape):
        zeros = (0,) * len(shape)
        return pl.BlockSpec(shape, lambda i, pair: zeros, pipeline_mode=pl.Buffered(1))

    tile_spec = pl.BlockSpec((TM, D_MODEL), lambda i, pair: (i, 0))
    return _call(
        "attn_fwd", body, (nq, B_HEADS // hps),
        [tile_spec, per_head(QK_NOPE + QK_ROPE), resident((B_HEADS, t, QK_NOPE + QK_ROPE)),
         resident((B_HEADS, t, V_HEAD)), resident((B_HEADS, V_HEAD, D_MODEL))],
        [tile_spec, per_head(V_HEAD)], [_sds((t, D_MODEL), F32), _sds((B_HEADS, t, V_HEAD), BF16)],
        (h, q, k, v, w_o))


def _mlp_bwd(h, a, dho, g, w1, w2, layer, after=()):
    t = h.shape[0]

    def body(h_ref, a_ref, dho_ref, g_ref, w1_ref, w2_ref, dhi_ref, dg_ref, hn_ref, f_ref, da_ref, dhib_ref):
        gv = g_ref[...]
        y, xhat, rstd = _rms_fwd(h_ref[...], gv)
        hn_ref[...] = y.astype(BF16)
        dho_v = dho_ref[...]
        dhob = dho_v.astype(BF16)
        dhn = jnp.zeros((TM, D_MODEL), F32)
        for d in range(N_DEV):
            cs = slice(d * FF_SLOT, (d + 1) * FF_SLOT)
            r = jnp.maximum(a_ref[:, cs], 0.0)
            f_ref[:, cs] = (r * r).astype(BF16)
            da = (_dot_nt(dhob, w2_ref[d]) * (2.0 * r)).astype(BF16)
            da_ref[:, cs] = da
            dhn = dhn + _dot_nt(da, w1_ref[d])
        dx, dg = _rms_bwd(dhn, xhat, rstd, gv)
        dhi = dho_v + dx
        dhi_ref[...] = dhi
        dhib_ref[...] = dhi.astype(BF16)
        _acc(dg_ref, dg)

    return _call(
        f"mlp_bwd_{layer}", body, (t // TM,),
        [_row(D_MODEL), _row(D_FF), _row(D_MODEL), _res((1, D_MODEL)), *MLP_W_SPECS],
        [_row(D_MODEL), _const((1, D_MODEL)), _row(D_MODEL), _row(D_FF), _row(D_FF), _row(D_MODEL)],
        [_sds((t, D_MODEL), F32), _sds((1, D_MODEL), F32), _sds((t, D_MODEL), BF16), _sds((t, D_FF), BF16),
         _sds((t, D_FF), BF16), _sds((t, D_MODEL), BF16)],
        (h, a, dho, g, w1, w2), after=after)


def _attn_bwd(dh, q, k, v, w_o, cos, sin, after=()):
    t = dh.shape[0]
    half, hps = QK_ROPE // 2, HEADS_PER_STEP

    def body(dh_ref, q_ref, k_ref, v_ref, wo_ref, cos_ref, sin_ref, dq_ref, dk_ref, dv_ref):
        i = pl.program_id(1)

        @pl.when(i == 0)
        def _():
            dk_ref[...] = jnp.zeros_like(dk_ref)
            dv_ref[...] = jnp.zeros_like(dv_ref)

        def tile(kt):
            keys = slice(0, (kt + 1) * TM)
            for j in range(hps):
                qj = q_ref[j]
                do = _dot_nt(dh_ref[kt * TM:(kt + 1) * TM, :], wo_ref[j]).astype(BF16)
                p = _softmax_rows(qj, k_ref.at[j], kt)
                dp = _dot_nt(do, v_ref[j, keys, :])
                ds = (p * (dp - jnp.sum(p * dp, axis=-1, keepdims=True)) * ATT_SCALE).astype(BF16)
                dq = _dot(ds, k_ref[j, keys, :])
                dq_ref[j, :, 0:QK_NOPE] = dq[:, :QK_NOPE].astype(BF16)
                dq_ref[j, :, QK_NOPE:] = _rope(dq[:, QK_NOPE:], cos_ref[...], -sin_ref[...]).astype(BF16)
                dk_ref[j, keys, :] += _dot_tn(ds, qj)
                dv_ref[j, keys, :] += _dot_tn(p.astype(BF16), do)

        _for_my_tile(i, t // TM, tile)

    def per_pair(rows, d, tiled):
        return pl.BlockSpec((hps, rows, d), (lambda pair, i: (pair, i, 0)) if tiled else (lambda pair, i: (pair, 0, 0)))

    def tile(d):
        return pl.BlockSpec((TM, d), lambda pair, i: (i, 0))

    return _call(
        "attn_bwd", body, (B_HEADS // hps, t // TM),
        [pl.BlockSpec((t, D_MODEL), lambda pair, i: (0, 0), pipeline_mode=pl.Buffered(1)),
         per_pair(TM, QK_NOPE + QK_ROPE, True), per_pair(t, QK_NOPE + QK_ROPE, False), per_pair(t, V_HEAD, False),
         per_pair(V_HEAD, D_MODEL, False), tile(half), tile(half)],
        [per_pair(TM, QK_NOPE + QK_ROPE, True), per_pair(t, QK_NOPE + QK_ROPE, False), per_pair(t, V_HEAD, False)],
        [_sds((B_HEADS, t, QK_NOPE + QK_ROPE), BF16), _sds((B_HEADS, t, QK_NOPE + QK_ROPE), F32),
         _sds((B_HEADS, t, V_HEAD), F32)],
        (dh, q, k, v, w_o, cos, sin), after=after)


def _kvq_bwd(h, dh, ckv, cqpre, dq, dk, dv, cos, sin, kvq_w, after=()):
    t = h.shape[0]
    tm = TM
    half, last = QK_ROPE // 2, t // tm - 1
    grad_shapes = [(D_MODEL, Q_LORA), (B_HEADS, Q_LORA, QK_NOPE + QK_ROPE), (D_MODEL, KV_LORA + QK_ROPE),
                   (B_HEADS, KV_LORA, QK_NOPE + V_HEAD)]

    def body(h_ref, dh_ref, ckv_ref, cqpre_ref, dq_ref, dk_ref, dv_ref, cos_ref, sin_ref,
             srcg_ref, wkva_ref, kvag_ref, wkvb_ref, mixg_ref, wqa_ref, qg_ref, wqb_ref,
             dhi_ref, dmixg_ref, dsrcg_ref, dqg_ref, dkvag_ref, gqa_ref, gqb_ref, gkva_ref, gkvb_ref,
             aqa, aqb, akva, akvb):
        @pl.when(pl.program_id(0) == 0)
        def _():
            for acc in (aqa, aqb, akva, akvb):
                acc[...] = jnp.zeros_like(acc)

        hv = h_ref[...]
        rstd = lax.rsqrt(jnp.mean(hv * hv, axis=-1, keepdims=True) + EPS)
        xhat = hv * rstd
        mixg, srcg, qg, kvag = mixg_ref[...], srcg_ref[...], qg_ref[...], kvag_ref[...]
        cq, cqhat, crstd = _rms_fwd(cqpre_ref[...], qg)
        cqb = cq.astype(BF16)
        dcq = jnp.zeros((tm, Q_LORA), F32)
        for hd in range(B_HEADS):
            dcq = dcq + _dot_nt(dq_ref[hd], wqb_ref[hd])
            aqb[hd] += _dot_tn(cqb, dq_ref[hd])
        dcqpre, dqg = _rms_bwd(dcq, cqhat, crstd, qg)
        dcqpre_b = dcqpre.astype(BF16)
        aqa[...] += _dot_tn((xhat * mixg).astype(BF16), dcqpre_b)
        dxq, dmixg = _rms_bwd(_dot_nt(dcqpre_b, wqa_ref[...]), xhat, rstd, mixg)
        ckv = ckv_ref[...]
        c, chat, krstd = _rms_fwd(ckv[:, :KV_LORA], kvag)
        cb = c.astype(BF16)
        dc = jnp.zeros((tm, KV_LORA), F32)
        dkpe = jnp.zeros((tm, QK_ROPE), F32)
        for hd in range(B_HEADS):
            dkv = jnp.concatenate([dk_ref[hd, :, 0:QK_NOPE], dv_ref[hd]], axis=-1).astype(BF16)
            akvb[hd] += _dot_tn(cb, dkv)
            dc = dc + _dot_nt(dkv, wkvb_ref[hd])
            dkpe = dkpe + dk_ref[hd, :, QK_NOPE:]
        dlat, dkvag = _rms_bwd(dc, chat, krstd, kvag)
        dpe = _rope(dkpe, cos_ref[...], -sin_ref[...])
        dckv_b = jnp.concatenate([dlat, dpe], axis=-1).astype(BF16)
        akva[...] += _dot_tn((xhat * srcg).astype(BF16), dckv_b)
        dxk, dsrcg = _rms_bwd(_dot_nt(dckv_b, wkva_ref[...]), xhat, rstd, srcg)
        dhi_ref[...] = dh_ref[...] + dxq + dxk
        _acc(dmixg_ref, dmixg)
        _acc(dsrcg_ref, dsrcg)
        _acc(dqg_ref, dqg)
        _acc(dkvag_ref, dkvag)

        @pl.when(pl.program_id(0) == last)
        def _():
            for out, acc in ((gqa_ref, aqa), (gqb_ref, aqb), (gkva_ref, akva), (gkvb_ref, akvb)):
                out[...] = acc[...].astype(BF16)

    return _call(
        "kvq_bwd", body, (t // tm,),
        [_row(D_MODEL, tm), _row(D_MODEL, tm), _row(KV_LORA + QK_ROPE, tm), _row(Q_LORA, tm),
         _heads(QK_NOPE + QK_ROPE, tm), _heads(QK_NOPE + QK_ROPE, tm), _heads(V_HEAD, tm), _row(half, tm),
         _row(half, tm), *KVQ_W_SPECS],
        [_row(D_MODEL, tm), _const((1, D_MODEL)), _const((1, D_MODEL)), _const((1, Q_LORA)), _const((1, KV_LORA))]
        + [_const(s) for s in grad_shapes],
        [_sds((t, D_MODEL), F32), _sds((1, D_MODEL), F32), _sds((1, D_MODEL), F32), _sds((1, Q_LORA), F32),
         _sds((1, KV_LORA), F32)] + [_sds(s, BF16) for s in grad_shapes],
        (h, dh, ckv, cqpre, dq, dk, dv, cos, sin, *kvq_w), scratch=[pltpu.VMEM(s, F32) for s in grad_shapes],
        after=after)


def _a_mix_bwd(x, z, dh, g, w_in, ln_g, ln_b, w_s, b_st, w_out, after=()):
    t = x.shape[0]
    tm = TM_GATE
    nblk = tm // GMLP_BLOCK

    def body(x_ref, z_ref, dh_ref, g_ref, win_ref, lng_ref, lnb_ref, ws_ref, bst_ref, wout_ref,
             dx_ref, hn_ref, dz_ref, dg_ref, dlng_ref, dlnb_ref, dws_ref, dbs_ref, dvn_scr, gelu_grad_v):
        @pl.when(pl.program_id(0) == 0)
        def _():
            dws_ref[...] = jnp.zeros_like(dws_ref)
            dbs_ref[...] = jnp.zeros_like(dbs_ref)

        gv, lng = g_ref[...], lng_ref[...]
        y, xhat, rstd = _rms_fwd(x_ref[...], gv)
        hn_ref[...] = y.astype(BF16)
        dhv = dh_ref[...]
        dgated = _dot_nt(dhv.astype(BF16), wout_ref[...])
        u, gelu_grad_u = _gelu_and_grad(z_ref[:, :GATE_DIM])
        v, gelu_grad_v[...] = _gelu_and_grad(z_ref[:, GATE_DIM:])
        vn, vhat, lrstd = _ln_fwd(v, lng, lnb_ref[...])
        vb = vn.astype(BF16)
        mask = _gate_mask()
        for gi in range(A_GROUPS):
            wm = jnp.where(mask, ws_ref[gi], 0.0).astype(BF16)
            bias = bst_ref[:, gi:gi + 1]
            cs = slice(gi * A_GROUP_DIM, (gi + 1) * A_GROUP_DIM)
            dws = jnp.zeros((GMLP_BLOCK, GMLP_BLOCK), F32)
            dbs = jnp.zeros((GMLP_BLOCK, 1), F32)
            for n in range(nblk):
                rs = slice(n * GMLP_BLOCK, (n + 1) * GMLP_BLOCK)
                sv = _dot(wm, vb[rs, cs]) + bias
                dz_ref[rs, cs] = (dgated[rs, cs] * sv * gelu_grad_u[rs, cs]).astype(BF16)
                dsv = dgated[rs, cs] * u[rs, cs]
                dsvb = dsv.astype(BF16)
                dws = dws + _dot_nt(dsvb, vb[rs, cs])
                dbs = dbs + jnp.sum(dsv, axis=-1, keepdims=True)
                dvn_scr[rs, cs] = _dot_tn(wm, dsvb)
            dws_ref[gi] += jnp.where(mask, dws, 0.0)
            dbs_ref[gi] += dbs
        dvn = dvn_scr[...]
        dvhat = dvn * lng
        dv = lrstd * (dvhat - jnp.mean(dvhat, axis=-1, keepdims=True)
                      - vhat * jnp.mean(dvhat * vhat, axis=-1, keepdims=True))
        dz_ref[:, GATE_DIM:] = (dv * gelu_grad_v[...]).astype(BF16)
        dhn = jnp.zeros((tm, D_MODEL), F32)
        for d in range(N_DEV):
            dhn = dhn + _dot_nt(dz_ref[:, d * FF_SLOT:(d + 1) * FF_SLOT], win_ref[d])
        dx, dg = _rms_bwd(dhn, xhat, rstd, gv)
        dx_ref[...] = dhv + dx
        _acc(dg_ref, dg)
        _acc(dlng_ref, jnp.sum(dvn * vhat, axis=0, keepdims=True))
        _acc(dlnb_ref, jnp.sum(dvn, axis=0, keepdims=True))

    return _call(
        "a_mix_bwd", body, (t // tm,),
        [_row(D_MODEL, tm), _row(2 * GATE_DIM, tm), _row(D_MODEL, tm), _res((1, D_MODEL)),
         _res((N_DEV, D_MODEL, FF_SLOT)), _res((1, GATE_DIM)), _res((1, GATE_DIM)),
         _res((A_GROUPS, GMLP_BLOCK, GMLP_BLOCK)), _res((GMLP_BLOCK, A_GROUPS)), _res((GATE_DIM, D_MODEL))],
        [_row(D_MODEL, tm), _row(D_MODEL, tm), _row(2 * GATE_DIM, tm),
         _const((1, D_MODEL)), _const((1, GATE_DIM)), _const((1, GATE_DIM)),
         _const((A_GROUPS, GMLP_BLOCK, GMLP_BLOCK)), _const((A_GROUPS, GMLP_BLOCK, 1))],
        [_sds((t, D_MODEL), F32), _sds((t, D_MODEL), BF16),
         _sds((t, 2 * GATE_DIM), BF16), _sds((1, D_MODEL), F32), _sds((1, GATE_DIM), F32),
         _sds((1, GATE_DIM), F32), _sds((A_GROUPS, GMLP_BLOCK, GMLP_BLOCK), F32),
         _sds((A_GROUPS, GMLP_BLOCK, 1), F32)],
        (x, z, dh, g, w_in, ln_g, ln_b, w_s, b_st, w_out),
        scratch=[pltpu.VMEM((tm, GATE_DIM), F32), pltpu.VMEM((tm, GATE_DIM), F32)], after=after)


def _wgrad(name, a, b, a_spec, b_spec, m, n, after=()):
    def body(a_ref, b_ref, o_ref):
        o_ref[0] = _dot_tn(a_ref[...].astype(BF16), b_ref[...].astype(BF16)).astype(BF16)

    return _call(name, body, (N_DEV,), [a_spec, b_spec], [pl.BlockSpec((1, m, n), lambda d: (d, 0, 0))],
                 [_sds((N_DEV, m, n), BF16)], (a, b), after=after)[0]


def _full(t, d):
    return pl.BlockSpec((t, d), lambda i: (0, 0), pipeline_mode=pl.Buffered(1))


def _cols(t, d):
    return pl.BlockSpec((t, d), lambda i: (0, i))


def _head(t, d):
    return pl.BlockSpec((None, t, d), lambda i: (i, 0, 0))


def _local_step(x, pos, target, inv_freq, wg, sm, shards=None):
    t = x.shape[0]
    wg = dict(wg)
    dist = shards is not None
    mix_g = [sm["norm_mix_g"][l:l + 1] for l in range(2)]
    mlp_g = [sm["norm_mlp_g"][l:l + 1] for l in range(2)]

    ids = iter(range(2, 2 + 9))

    def gather(names):
        if dist:
            got = _by_sequencer("gather_" + names[0], _gather_comm([shards[k] for k in names]),
                                SIBLING_AND_NEIGHBOURS, next(ids))
            wg.update(zip(names, got))

    def send(name, names):
        if dist:
            comm = _exchange_comm(grads=[g[k] for k in names])
            g.update(zip(names, _by_sequencer("exchange_" + name, comm, EVERYONE, next(ids))))

    def send_sums(name, names, meanwhile):
        if not dist:
            meanwhile()
            return ()
        grads = [g[k] for k in names]
        landed = _by_sequencer("pair_exchange_" + name, _pair_exchange_comm(grads), (1,), next(ids))
        sums = _pair_add("pair_add_" + name, grads, landed, after=meanwhile())
        g.update(zip(names, _by_sequencer("exchange_" + name, _chip_exchange_comm(sums), OTHER_CHIPS, next(ids))))
        return sums

    def a_args():
        return (wg["a_w_in"], wg["a_ln_v_g"], wg["a_ln_v_b"], sm["a_w_s"], sm["a_b_st"], wg["a_w_out"])

    def kvq_w():
        return (sm["kv_src_norm_g"], wg["kv_w_a"], sm["kv_a_norm_g"], wg["kv_w_b"], mix_g[1], wg["b_w_q_a"],
                sm["b_q_norm_g"], wg["b_w_q_b"])

    gather(("mlp_w1_0", "mlp_w2_0"))
    h1, z, gated = _a_mix_fwd(x, mix_g[0], *a_args())
    gather(("kv_w_a", "kv_w_b", "b_w_q_a", "b_w_q_b", "b_w_o"))
    h2, a0 = _mlp_fwd(h1, mlp_g[0], wg["mlp_w1_0"], wg["mlp_w2_0"])
    if dist:
        wg["b_w_q_a"] = wg["b_w_q_a"].reshape(D_MODEL, Q_LORA)
        wg["kv_w_a"] = wg["kv_w_a"].reshape(D_MODEL, KV_LORA + QK_ROPE)
    gather(("mlp_w1_1", "mlp_w2_1"))
    ckv, k, v, cqpre, q, cos, sin = _kvq_fwd(h2, pos, inv_freq, kvq_w())
    h3, att = _attn_fwd(h2, q, k, v, wg["b_w_o"])
    a1, loss, dh4, d_final_g = _mlp_fwd_loss(h3, mlp_g[1], wg["mlp_w1_1"], wg["mlp_w2_1"], sm["final_norm_g"], target)

    g = {}
    dh3, d_mlp_g1, hn, f, da, dh3_b = _mlp_bwd(h3, a1, dh4, mlp_g[1], wg["mlp_w1_1"], wg["mlp_w2_1"], 1)
    dq, dk, dv = _attn_bwd(dh3_b, q, k, v, wg["b_w_o"], cos, sin)
    g["mlp_w1_1"] = _wgrad("wgrad_w1_1", hn, da, _full(t, D_MODEL), _cols(t, FF_SLOT), D_MODEL, FF_SLOT, after=[dq])
    g["mlp_w2_1"] = _wgrad("wgrad_w2_1", f, dh4, _cols(t, FF_SLOT), _full(t, D_MODEL), FF_SLOT, D_MODEL)

    def wgrad_w_o():
        g["b_w_o"] = _wgrad("wgrad_w_o", att, dh3_b, _head(t, V_HEAD), _full(t, D_MODEL), V_HEAD, D_MODEL)
        return [g["b_w_o"]]

    sums = send_sums("mlp_1", ("mlp_w1_1", "mlp_w2_1"), wgrad_w_o)
    dh2, d_mix_g1, d_src_g, d_q_g, d_kv_a_g, g_q_a, g["b_w_q_b"], g_kv_a, g["kv_w_b"] = _kvq_bwd(
        h2, dh3, ckv, cqpre, dq, dk, dv, cos, sin, kvq_w(), after=sums)
    g["b_w_q_a"] = g_q_a.reshape(N_DEV, D_MODEL // N_DEV, Q_LORA)
    g["kv_w_a"] = g_kv_a.reshape(N_DEV, D_MODEL // N_DEV, KV_LORA + QK_ROPE)
    qkv = ("b_w_q_a", "b_w_q_b", "kv_w_a", "kv_w_b")
    landed = [g[k] for k in qkv]
    send("qkv", qkv)
    dh1, d_mlp_g0, hn, f, da, dh1_b = _mlp_bwd(h1, a0, dh2, mlp_g[0], wg["mlp_w1_0"], wg["mlp_w2_0"], 0,
                                               after=landed if dist else ())
    landed = [g["mlp_w1_1"], g["mlp_w2_1"]] if dist else ()
    g["mlp_w1_0"] = _wgrad("wgrad_w1_0", hn, da, _full(t, D_MODEL), _cols(t, FF_SLOT), D_MODEL, FF_SLOT, after=landed)
    g["mlp_w2_0"] = _wgrad("wgrad_w2_0", f, dh2, _cols(t, FF_SLOT), _full(t, D_MODEL), FF_SLOT, D_MODEL)

    def wgrad_a_w_out():
        g["a_w_out"] = _wgrad("wgrad_a_w_out", gated, dh1_b, _cols(t, GATE_DIM // N_DEV), _full(t, D_MODEL),
                              GATE_DIM // N_DEV, D_MODEL)
        return [g["a_w_out"]] + [g[k] for k in qkv]

    sums = send_sums("mlp_0", ("mlp_w1_0", "mlp_w2_0", "b_w_o"), wgrad_a_w_out)
    if dist:
        sums = _pair_reduce("pair_reduce_a_w_out", [g["a_w_out"]], after=sums)
    dx, hn, dz, d_mix_g0, d_ln_g, d_ln_b, d_ws, d_bs = _a_mix_bwd(x, z, dh1, mix_g[0], *a_args(), after=sums)
    small = {
        "norm_mix_g": jnp.concatenate([d_mix_g0, d_mix_g1], axis=0),
        "norm_mlp_g": jnp.concatenate([d_mlp_g0, d_mlp_g1], axis=0),
        "a_ln_v_g": d_ln_g.reshape(N_DEV, GATE_DIM // N_DEV),
        "a_ln_v_b": d_ln_b.reshape(N_DEV, GATE_DIM // N_DEV),
        "a_w_s": d_ws.astype(BF16) if dist else d_ws,
        "a_b_s": d_bs.reshape(A_GROUPS, GMLP_BLOCK),
        "b_q_norm_g": d_q_g,
        "kv_src_norm_g": d_src_g,
        "kv_a_norm_g": d_kv_a_g,
        "final_norm_g": d_final_g,
    }
    if dist:
        parts = [small[k].reshape((1,) + small[k].shape) for k in SMALL] + [loss.reshape(1, 1, 1)]
        comm = _together(_chip_exchange_comm(sums), _gather_comm(parts))
        g["a_w_out"], *got = _by_sequencer("exchange_a_w_out", comm, EVERYONE, next(ids))
        small, loss = dict(zip(SMALL, got)), got[-1]
    g["a_w_in"] = _wgrad("wgrad_a_w_in", hn, dz, _full(t, D_MODEL), _cols(t, FF_SLOT), D_MODEL, FF_SLOT)
    return loss, dx, g, small


def _adamw(w, g, m, v):
    m = ADAM_B1 * m + (1.0 - ADAM_B1) * g
    v = ADAM_B2 * v + (1.0 - ADAM_B2) * (g * g)
    m_hat = m / (1.0 - ADAM_B1 ** ADAM_STEP)
    v_hat = v / (1.0 - ADAM_B2 ** ADAM_STEP)
    return -ADAM_LR * (m_hat / (jnp.sqrt(v_hat) + ADAM_EPS) + ADAM_WD * w), m, v


def _sum_in_device_order(r_ref):
    g = r_ref[0].astype(F32)
    for j in range(1, r_ref.shape[0]):
        g = g + r_ref[j].astype(F32)
    return g


def _adamw_sharded(name, recvs, w, m, v, swapped=False):
    layers, r, c = w.shape[0], *recvs[0][0].shape[1:]
    tr = r if swapped else math.gcd(r, 512)
    flat = [a for per_layer in recvs for a in per_layer]

    def body(*refs):
        r_refs, (w_ref, m_ref, v_ref) = refs[:len(flat)], refs[len(flat):len(flat) + 3]
        g_ref, d_ref, nm_ref, nv_ref = refs[-4:]
        layer = pl.program_id(0)
        g, pos = None, 0
        for li, per_layer in enumerate(recvs):
            total = None
            for ref in r_refs[pos:pos + len(per_layer)]:
                part = _sum_in_device_order(ref)
                total = part if total is None else total + part
            pos += len(per_layer)
            g = total if g is None else jnp.where(layer == li, total, g)
        if swapped:
            g = g.T
        g_ref[...] = g
        d_ref[...], nm_ref[...], nv_ref[...] = _adamw(w_ref[...], g, m_ref[...], v_ref[...])

    blk = pl.BlockSpec((None, tr, c), lambda l, i: (l, i, 0))
    if swapped:
        blk = pl.BlockSpec((None, c, r), lambda l, i: (l, 0, 0))
    return _call(name, body, (layers, r // tr),
                 [pl.BlockSpec((a.shape[0], tr, c), lambda l, i: (0, i, 0)) for a in flat] + [blk] * 3,
                 [blk] * 4, [_sds(w.shape, F32)] * 4, (*flat, w, m, v))


def _adamw_small(recvs, ws, ms, vs, own_row, losses):
    n = len(recvs)

    def body(*refs):
        r_refs, w_refs, m_refs, v_refs = (refs[i * n:(i + 1) * n] for i in range(4))
        outs, scr = refs[4 * n + 1:8 * n + 2], refs[8 * n + 2:]
        outs[-1][...] = _sum_in_device_order(refs[4 * n])
        me = _my_place()[3]
        for a in range(n):
            g = _sum_in_device_order(r_refs[a])
            if own_row[a]:
                scr[0][...] = g
                g = scr[0][pl.ds(me, 1), :]
            g_ref, d_ref, nm_ref, nv_ref = outs[4 * a:4 * a + 4]
            g_ref[...] = g
            d_ref[...], nm_ref[...], nv_ref[...] = _adamw(w_refs[a][...], g, m_refs[a][...], v_refs[a][...])

    out_shape = []
    for w in ws:
        out_shape += [_sds(w.shape, F32)] * 4
    return pl.pallas_call(
        body, name="adamw_small", in_specs=[VMEM] * (4 * n + 1), out_specs=[VMEM] * (4 * n + 1),
        out_shape=out_shape + [_sds((1, 1), F32)], scratch_shapes=[pltpu.VMEM((N_DEV, GATE_DIM // N_DEV), F32)],
    )(*recvs, *ws, *ms, *vs, losses)


BIG = ("a_w_in", "a_w_out", "b_w_q_a", "b_w_q_b", "b_w_o", "kv_w_a", "kv_w_b", "mlp_w1", "mlp_w2")
SMALL = ("norm_mix_g", "norm_mlp_g", "a_ln_v_g", "a_ln_v_b", "a_w_s", "a_b_s", "b_q_norm_g", "kv_src_norm_g",
         "kv_a_norm_g", "final_norm_g")
WEIGHTS = ("norm_mix_g", "norm_mlp_g", "a_w_in", "a_ln_v_g", "a_ln_v_b", "a_w_s", "a_b_s", "a_w_out", "b_w_q_a",
           "b_q_norm_g", "b_w_q_b", "b_w_o", "kv_src_norm_g", "kv_w_a", "kv_a_norm_g", "kv_w_b", "mlp_w1", "mlp_w2",
           "final_norm_g")


def _two_d(name, a):
    if name in ("a_w_s", "a_b_s"):
        return a.reshape(a.shape[1:])
    return a.reshape(1, -1) if a.ndim == 1 else a


def _three_d(a):
    return a if a.ndim == 3 else a.reshape((1,) + a.shape)


SWAPPED = ("b_w_q_b", "kv_w_a")


def _swapped(a):
    return jnp.swapaxes(_three_d(a), 1, 2)


def kernel(x, positions, norm_mix_g, norm_mlp_g, a_w_in, a_ln_v_g, a_ln_v_b, a_w_s, a_b_s, a_w_out, b_w_q_a, b_q_norm_g, b_w_q_b, b_w_o, kv_src_norm_g, kv_w_a, kv_a_norm_g, kv_w_b, mlp_w1, mlp_w2, final_norm_g, loss_target, m_norm_mix_g, m_norm_mlp_g, m_a_w_in, m_a_ln_v_g, m_a_ln_v_b, m_a_w_s, m_a_b_s, m_a_w_out, m_b_w_q_a, m_b_q_norm_g, m_b_w_q_b, m_b_w_o, m_kv_src_norm_g, m_kv_w_a, m_kv_a_norm_g, m_kv_w_b, m_mlp_w1, m_mlp_w2, m_final_norm_g, v_norm_mix_g, v_norm_mlp_g, v_a_w_in, v_a_ln_v_g, v_a_ln_v_b, v_a_w_s, v_a_b_s, v_a_w_out, v_b_w_q_a, v_b_q_norm_g, v_b_w_q_b, v_b_w_o, v_kv_src_norm_g, v_kv_w_a, v_kv_a_norm_g, v_kv_w_b, v_mlp_w1, v_mlp_w2, v_final_norm_g):
    w = dict(norm_mix_g=norm_mix_g, norm_mlp_g=norm_mlp_g, a_w_in=a_w_in, a_ln_v_g=a_ln_v_g, a_ln_v_b=a_ln_v_b,
             a_w_s=a_w_s, a_b_s=a_b_s, a_w_out=a_w_out, b_w_q_a=b_w_q_a, b_q_norm_g=b_q_norm_g, b_w_q_b=b_w_q_b,
             b_w_o=b_w_o, kv_src_norm_g=kv_src_norm_g, kv_w_a=kv_w_a, kv_a_norm_g=kv_a_norm_g, kv_w_b=kv_w_b,
             mlp_w1=mlp_w1, mlp_w2=mlp_w2, final_norm_g=final_norm_g)
    m = dict(norm_mix_g=m_norm_mix_g, norm_mlp_g=m_norm_mlp_g, a_w_in=m_a_w_in, a_ln_v_g=m_a_ln_v_g,
             a_ln_v_b=m_a_ln_v_b, a_w_s=m_a_w_s, a_b_s=m_a_b_s, a_w_out=m_a_w_out, b_w_q_a=m_b_w_q_a,
             b_q_norm_g=m_b_q_norm_g, b_w_q_b=m_b_w_q_b, b_w_o=m_b_w_o, kv_src_norm_g=m_kv_src_norm_g,
             kv_w_a=m_kv_w_a, kv_a_norm_g=m_kv_a_norm_g, kv_w_b=m_kv_w_b, mlp_w1=m_mlp_w1, mlp_w2=m_mlp_w2,
             final_norm_g=m_final_norm_g)
    v = dict(norm_mix_g=v_norm_mix_g, norm_mlp_g=v_norm_mlp_g, a_w_in=v_a_w_in, a_ln_v_g=v_a_ln_v_g,
             a_ln_v_b=v_a_ln_v_b, a_w_s=v_a_w_s, a_b_s=v_a_b_s, a_w_out=v_a_w_out, b_w_q_a=v_b_w_q_a,
             b_q_norm_g=v_b_q_norm_g, b_w_q_b=v_b_w_q_b, b_w_o=v_b_w_o, kv_src_norm_g=v_kv_src_norm_g,
             kv_w_a=v_kv_w_a, kv_a_norm_g=v_kv_a_norm_g, kv_w_b=v_kv_w_b, mlp_w1=v_mlp_w1, mlp_w2=v_mlp_w2,
             final_norm_g=v_final_norm_g)
    t = x.shape[1]

    first = ("a_w_in", "a_w_out", "a_ln_v_g", "a_ln_v_b")
    later = ("mlp_w1", "mlp_w2", "kv_w_a", "kv_w_b", "b_w_q_a", "b_w_q_b", "b_w_o")
    later_blocks = ("mlp_w1_0", "mlp_w1_1", "mlp_w2_0", "mlp_w2_1") + later[2:]
    got, casts = _gather_first([_three_d(w[k]) if k in BIG else w[k] for k in first],
                               [_swapped(w[k]) if k in SWAPPED else _three_d(w[k]) for k in later],
                               [k in SWAPPED for k in later])
    wg = dict(zip(first, got))
    wg["a_w_out"] = wg["a_w_out"].reshape(GATE_DIM, D_MODEL)
    wg["a_ln_v_g"] = wg["a_ln_v_g"].reshape(1, GATE_DIM)
    wg["a_ln_v_b"] = wg["a_ln_v_b"].reshape(1, GATE_DIM)
    shards = dict(zip(later_blocks, casts))

    sm = {k: _two_d(k, w[k]) for k in SMALL if k not in ("a_ln_v_g", "a_ln_v_b")}
    sm["a_b_st"] = sm["a_b_s"].T
    inv_freq = (ROPE_THETA ** (-jnp.arange(0, QK_ROPE, 2, dtype=F32) / QK_ROPE)).reshape(1, QK_ROPE // 2)

    losses, dx, g, small = _local_step(x[0], positions.reshape(t, 1), loss_target[0], inv_freq, wg, sm, shards)

    sums = _pair_reduce("pair_reduce_a_w_in", [g["a_w_in"]], after=[g["mlp_w1_0"], g["mlp_w2_0"]])
    g["a_w_in"], = _by_sequencer("exchange_last", _chip_exchange_comm(sums), OTHER_CHIPS, collective_id=1)

    out = {}
    for k in BIG:
        recvs = [[g[k + "_0"]], [g[k + "_1"]]] if k.startswith("mlp") else [[g[k]]]
        view = _swapped if k in SWAPPED else _three_d
        res = _adamw_sharded("adamw_" + k, recvs, view(w[k]), view(m[k]), view(v[k]), swapped=k in SWAPPED)
        out[k] = [view(o).reshape(w[k].shape) for o in res]
    own_row = [k in ("a_ln_v_g", "a_ln_v_b") for k in SMALL]
    res = _adamw_small([small[k] for k in SMALL], [_two_d(k, w[k]) for k in SMALL], [_two_d(k, m[k]) for k in SMALL],
                       [_two_d(k, v[k]) for k in SMALL], own_row, losses)
    for i, k in enumerate(SMALL):
        out[k] = [o.reshape(w[k].shape) for o in res[4 * i:4 * i + 4]]

    return (res[-1].reshape(()), dx.reshape(x.shape), *[out[k][0] for k in WEIGHTS], *[out[k][1] for k in WEIGHTS],
            *[out[k][2] for k in WEIGHTS], *[out[k][3] for k in WEIGHTS])
```

```python
import math

import jax
import jax.numpy as jnp
from jax import lax
from jax.experimental import pallas as pl
from jax.experimental.pallas import tpu as pltpu
from jax.experimental.pallas import tpu_sc as plsc

F32, BF16 = jnp.float32, jnp.bfloat16
MESH = pl.DeviceIdType.MESH
ANY = pl.BlockSpec(memory_space=pl.ANY)
VMEM = pl.BlockSpec(memory_space=pltpu.VMEM)

N_DEV = 8
D_MODEL = 1024
CHUNK = 64
GMLP_BLOCK = 128
GATE_DIM = 2048
A_GROUPS = 8
A_GROUP_DIM = GATE_DIM // A_GROUPS
B_HEADS = 8
QK_NOPE, QK_ROPE, V_HEAD = 128, 64, 128
Q_LORA, KV_LORA = 384, 256
ROPE_THETA = 10000.0
D_FF = 4096
FF_SLOT = D_FF // N_DEV
EPS = 1e-6
ATT_SCALE = (QK_NOPE + QK_ROPE) ** -0.5

ADAM_LR, ADAM_B1, ADAM_B2, ADAM_EPS, ADAM_WD, ADAM_STEP = 0.001, 0.9, 0.999, 1e-08, 0.01, 10

TM = 256
TM_GATE = 256
TM_MLP_FWD = 512
TM_KVQ = 512
VMEM_LIMIT = 56 * 1024 * 1024
INV_SQRT2 = 1.0 / math.sqrt(2.0)
INV_SQRT_2PI = 1.0 / math.sqrt(2.0 * math.pi)
LOG2_E = 1.0 / math.log(2.0)
HEADS_PER_STEP = 2


def _dot(a, b):
    return jnp.dot(a, b, preferred_element_type=F32)


def _dot_nt(a, b):
    return lax.dot_general(a, b, (((1,), (1,)), ((), ())), preferred_element_type=F32)


def _dot_tn(a, b):
    return lax.dot_general(a, b, (((0,), (0,)), ((), ())), preferred_element_type=F32)


def _rms_fwd(x, g):
    rstd = lax.rsqrt(jnp.mean(x * x, axis=-1, keepdims=True) + EPS)
    xhat = x * rstd
    return xhat * g, xhat, rstd


def _rms_bwd(dy, xhat, rstd, g):
    dxhat = dy * g
    dx = rstd * (dxhat - xhat * jnp.mean(dxhat * xhat, axis=-1, keepdims=True))
    return dx, jnp.sum(dy * xhat, axis=0, keepdims=True)


def _ln_fwd(v, g, b):
    mu = jnp.mean(v, axis=-1, keepdims=True)
    vc = v - mu
    rstd = lax.rsqrt(jnp.mean(vc * vc, axis=-1, keepdims=True) + EPS)
    vhat = vc * rstd
    return vhat * g + b, vhat, rstd


def _gelu(x):
    return 0.5 * x * (1.0 + lax.erf(x * INV_SQRT2))


def _gelu_and_grad(x):
    cdf = 0.5 * (1.0 + lax.erf(x * INV_SQRT2))
    return x * cdf, cdf + x * jnp.exp(-0.5 * x * x) * INV_SQRT_2PI


def _rope(x, cos, sin):
    x1, x2 = x[:, :QK_ROPE // 2], x[:, QK_ROPE // 2:]
    return jnp.concatenate([x1 * cos - x2 * sin, x2 * cos + x1 * sin], axis=-1)


def _gate_mask():
    row = lax.broadcasted_iota(jnp.int32, (GMLP_BLOCK, GMLP_BLOCK), 0)
    col = lax.broadcasted_iota(jnp.int32, (GMLP_BLOCK, GMLP_BLOCK), 1)
    return (col < CHUNK) | (row >= CHUNK)


def _att_mask(q0, tq, t):
    q = q0 + lax.broadcasted_iota(jnp.int32, (tq, t), 0)
    k = lax.broadcasted_iota(jnp.int32, (tq, t), 1)
    return jnp.right_shift(k, 6) <= jnp.right_shift(q, 6)


def _res(shape, imap=None):
    zeros = (0,) * len(shape)
    return pl.BlockSpec(shape, imap or (lambda i: zeros), pipeline_mode=pl.Buffered(1))


def _const(shape):
    zeros = (0,) * len(shape)
    return pl.BlockSpec(shape, lambda i: zeros)


def _row(d, tm=TM):
    return pl.BlockSpec((tm, d), lambda i: (i, 0))


def _heads(d, tm=TM):
    return pl.BlockSpec((B_HEADS, tm, d), lambda i: (0, i, 0))


def _sds(shape, dt):
    return jax.ShapeDtypeStruct(shape, dt)


def _acc(ref, val):
    @pl.when(pl.program_id(0) == 0)
    def _():
        ref[...] = jnp.zeros_like(ref)
    ref[...] += val


def _my_place():
    x, y, c = lax.axis_index("x"), lax.axis_index("y"), lax.axis_index("c")
    return x, y, c, 4 * x + 2 * y + c


def _peer(x, y, c, k):
    px = 1 - x if k & 4 else x
    py = 1 - y if k & 2 else y
    pc = 1 - c if k & 1 else c
    return (px, py, pc), 4 * px + 2 * py + pc


CHIPS = (2, 4, 6)


def _splits(ref):
    return len(ref.shape) >= 3 and ref.shape[1] % 32 == 0


def _piece(ref, block, half=None):
    if half is None or not _splits(ref):
        return ref.at[pl.ds(block, 1)]
    rows = ref.shape[1] // 2
    return ref.at[pl.ds(block, 1), pl.ds(half * rows, rows)]


def _gather_copy(sems, a, k, piece, to, src=None):
    return pltpu.make_async_remote_copy(
        src_ref=piece if src is None else src, dst_ref=piece, send_sem=sems[0].at[a, k], recv_sem=sems[1].at[a, k],
        device_id=to, device_id_type=MESH)


def _gather_start(srcs, outs, sems, only=None):
    x, y, c, me = _my_place()
    for a in range(len(srcs)) if only is None else (only,):
        mine = _piece(outs[a], me)
        pltpu.make_async_copy(srcs[a], mine, sems[2].at[a]).start()
        for k, rel in enumerate((1, 4, 2)):
            _gather_copy(sems, a, k, mine, _peer(x, y, c, rel)[0], src=srcs[a]).start()


def _gather_relay(srcs, outs, sems):
    x, y, c, _ = _my_place()
    sib = _peer(x, y, c, 1)[0]
    (xn, xn_i), (yn, yn_i) = _peer(x, y, c, 4), _peer(x, y, c, 2)
    for a in range(len(srcs)):
        out = outs[a]
        _gather_copy(sems, a, 1, _piece(out, xn_i), xn).wait_recv()
        _gather_copy(sems, a, 3, _piece(out, xn_i, 0), yn).start()
        _gather_copy(sems, a, 5, _piece(out, xn_i), sib).start()
        _gather_copy(sems, a, 2, _piece(out, yn_i), yn).wait_recv()
        if _splits(out):
            _gather_copy(sems, a, 4, _piece(out, yn_i, 1), xn).start()
        _gather_copy(sems, a, 6, _piece(out, yn_i), sib).start()


def _gather_finish(srcs, outs, sems):
    x, y, c, me = _my_place()
    sib = _peer(x, y, c, 1)[0]
    xn, yn, dg_i = _peer(x, y, c, 4)[0], _peer(x, y, c, 2)[0], _peer(x, y, c, 6)[1]
    n = len(srcs)
    for a in range(n):
        out = outs[a]
        _gather_copy(sems, a, 3, _piece(out, dg_i, 0), yn).wait_recv()
        _gather_copy(sems, a, 7, _piece(out, dg_i, 0), sib).start()
        if _splits(out):
            _gather_copy(sems, a, 4, _piece(out, dg_i, 1), xn).wait_recv()
            _gather_copy(sems, a, 8, _piece(out, dg_i, 1), sib).start()
    for a in range(n):
        out = outs[a]
        whole, half = _piece(out, me), _piece(out, me, 0)
        for k in (0, 5, 6):
            _gather_copy(sems, a, k, whole, sib).wait_recv()
        for k in (7, 8) if _splits(out) else (7,):
            _gather_copy(sems, a, k, half, sib).wait_recv()
        for k in (0, 1, 2):
            _gather_copy(sems, a, k, whole, sib, src=srcs[a]).wait_send()
        for k in (5, 6):
            _gather_copy(sems, a, k, whole, sib).wait_send()
        for k in (3, 4, 7, 8) if _splits(out) else (3, 7):
            _gather_copy(sems, a, k, half, sib).wait_send()
        pltpu.make_async_copy(srcs[a], whole, sems[2].at[a]).wait()


def _relay_sems(n):
    return [pltpu.SemaphoreType.DMA((n, 9)), pltpu.SemaphoreType.DMA((n, 9)), pltpu.SemaphoreType.DMA((n,))]


def _gather_sems(n):
    return [pltpu.SemaphoreType.DMA((n, 7)), pltpu.SemaphoreType.DMA((n, 7)), pltpu.SemaphoreType.DMA((n,))]


class _Comm:
    def __init__(self, args, out_shape, scratch, start, finish, relay=None):
        self.args, self.out_shape, self.scratch, self.start, self.finish = args, out_shape, scratch, start, finish
        self.relay = relay


def _gather_comm(shards):
    return _Comm(list(shards), [_sds((N_DEV,) + s.shape[1:], s.dtype) for s in shards], _relay_sems(len(shards)),
                 _gather_start, _gather_finish, relay=_gather_relay)


def _together(big, small):
    na, no, ns = len(big.args), len(big.out_shape), len(big.scratch)

    def start(src, dst, sems):
        small.start(src[na:], dst[no:], sems[ns:])

    def relay(src, dst, sems):
        small.relay(src[na:], dst[no:], sems[ns:])
        big.start(src[:na], dst[:no], sems[:ns])

    def finish(src, dst, sems):
        small.finish(src[na:], dst[no:], sems[ns:])
        big.finish(src[:na], dst[:no], sems[:ns])

    assert big.relay is None and small.relay is not None
    return _Comm(big.args + small.args, big.out_shape + small.out_shape, list(big.scratch) + list(small.scratch),
                 start, finish, relay=relay)


def _direct_copies(ins, outs, sems, wait):
    send_sems, recv_sems, local_sems = sems
    x, y, c, me = _my_place()
    for a in range(len(ins)):
        local = pltpu.make_async_copy(ins[a].at[pl.ds(me, 1)], outs[a].at[pl.ds(me, 1)], local_sems.at[a])
        local.wait() if wait else local.start()
        for k in range(1, N_DEV):
            to, to_i = _peer(x, y, c, k)
            cp = pltpu.make_async_remote_copy(
                src_ref=ins[a].at[pl.ds(to_i, 1)], dst_ref=outs[a].at[pl.ds(me, 1)],
                send_sem=send_sems.at[a, k - 1], recv_sem=recv_sems.at[a, k - 1], device_id=to, device_id_type=MESH)
            cp.wait() if wait else cp.start()


def _exchange_comm(grads):
    return _Comm(list(grads), [_sds(g.shape, g.dtype) for g in grads], _gather_sems(len(grads)),
                 lambda i, o, s: _direct_copies(i, o, s, False), lambda i, o, s: _direct_copies(i, o, s, True))


def _chip_copies(ins, outs, sems, wait):
    send_sems, recv_sems, local_sems = sems
    x, y, c, _ = _my_place()
    for a in range(len(ins)):
        local = pltpu.make_async_copy(ins[a].at[pl.ds(2 * x + y, 1)], outs[a].at[pl.ds(len(CHIPS), 1)],
                                      local_sems.at[a])
        local.wait() if wait else local.start()
        for i, k in enumerate(CHIPS):
            to = _peer(x, y, c, k)[0]
            cp = pltpu.make_async_remote_copy(
                src_ref=ins[a].at[pl.ds(2 * to[0] + to[1], 1)], dst_ref=outs[a].at[pl.ds(i, 1)],
                send_sem=send_sems.at[a, i], recv_sem=recv_sems.at[a, i], device_id=to, device_id_type=MESH)
            cp.wait() if wait else cp.start()


def _chip_exchange_comm(sums):
    n = len(sums)
    sems = [pltpu.SemaphoreType.DMA((n, len(CHIPS))), pltpu.SemaphoreType.DMA((n, len(CHIPS))),
            pltpu.SemaphoreType.DMA((n,))]
    return _Comm(list(sums), [_sds(s.shape, s.dtype) for s in sums], sems,
                 lambda i, o, s: _chip_copies(i, o, s, False), lambda i, o, s: _chip_copies(i, o, s, True))


def _pair_reduce(name, grads, after=()):
    n = len(grads)
    n_chips = N_DEV // 2

    def body(*refs):
        g_refs, gh_refs, refs = refs[:n], refs[n:2 * n], refs[2 * n + len(after):]
        p_refs, land = refs[:n], refs[n:2 * n]
        send_sems, recv_sems = refs[2 * n:]
        x, y, c, _ = _my_place()
        sib = _peer(x, y, c, 1)[0]
        q = pl.program_id(0)

        def to_sibling(a, j):
            return pltpu.make_async_remote_copy(
                src_ref=gh_refs[a].at[j, pl.ds(1 - c, 1)], dst_ref=land[a].at[pl.ds(j, 1)],
                send_sem=send_sems.at[a, j], recv_sem=recv_sems.at[a, j], device_id=sib, device_id_type=MESH)

        @pl.when(q == 0)
        def _():
            for j in range(n_chips):
                for a in range(n):
                    to_sibling(a, j).start()

        for a in range(n):
            to_sibling(a, q).wait_recv()
            p_refs[a][...] = (g_refs[a][0, pl.ds(c, 1)].astype(F32) + land[a][pl.ds(q, 1)].astype(F32)).astype(BF16)

        @pl.when(q == n_chips - 1)
        def _():
            for a in range(n):
                for j in range(n_chips):
                    to_sibling(a, j).wait_send()

    views = [g.reshape((n_chips, 2) + g.shape[1:]) for g in grads]
    res = pl.pallas_call(
        body, name=name, grid=(n_chips,),
        in_specs=[pl.BlockSpec((1, 2) + g.shape[1:], lambda q: (q, 0, 0, 0)) for g in grads]
        + [ANY] * (n + len(after)),
        out_specs=[pl.BlockSpec((1,) + g.shape[1:], lambda q: (q, 0, 0)) for g in grads],
        out_shape=[_sds((n_chips,) + g.shape[1:], BF16) for g in grads],
        scratch_shapes=[pltpu.VMEM((n_chips,) + g.shape[1:], BF16) for g in grads]
        + [pltpu.SemaphoreType.DMA((n, n_chips)), pltpu.SemaphoreType.DMA((n, n_chips))],
        compiler_params=pltpu.CompilerParams(dimension_semantics=("arbitrary",), vmem_limit_bytes=VMEM_LIMIT),
    )(*views, *views, *after)
    return list(res)


def _pair_exchange_comm(grads):
    n, n_chips = len(grads), N_DEV // 2

    def copies(ins, outs, sems, wait):
        x, y, c, _ = _my_place()
        for j in range(n_chips):
            for a in range(n):
                cp = pltpu.make_async_remote_copy(
                    src_ref=ins[a].at[j, pl.ds(1 - c, 1)], dst_ref=outs[a].at[pl.ds(j, 1)], send_sem=sems[0].at[a, j],
                    recv_sem=sems[1].at[a, j], device_id=_peer(x, y, c, 1)[0], device_id_type=MESH)
                cp.wait() if wait else cp.start()

    views = [g.reshape((n_chips, 2) + g.shape[1:]) for g in grads]
    sems = [pltpu.SemaphoreType.DMA((n, n_chips)), pltpu.SemaphoreType.DMA((n, n_chips))]
    return _Comm(views, [_sds((n_chips,) + g.shape[1:], g.dtype) for g in grads], sems,
                 lambda i, o, s: copies(i, o, s, False), lambda i, o, s: copies(i, o, s, True))


def _pair_add(name, grads, landed, after=()):
    n, n_chips = len(grads), N_DEV // 2

    def body(core_ref, *refs):
        g_refs, l_refs, p_refs = refs[:n], refs[n:2 * n], refs[2 * n + len(after):]
        for a in range(n):
            p_refs[a][...] = (g_refs[a][...].astype(F32) + l_refs[a][...].astype(F32)).astype(BF16)

    views = [g.reshape((n_chips, 2) + g.shape[1:]) for g in grads]
    blocks = [pl.BlockSpec((1,) + g.shape[1:], lambda q, core: (q, 0, 0)) for g in grads]
    mine = [pl.BlockSpec((1, None) + g.shape[1:], lambda q, core: (q, core[0], 0, 0)) for g in grads]
    return list(pl.pallas_call(
        body, name=name, out_shape=[_sds((n_chips,) + g.shape[1:], BF16) for g in grads],
        grid_spec=pltpu.PrefetchScalarGridSpec(num_scalar_prefetch=1, grid=(n_chips,),
                                               in_specs=mine + blocks + [ANY] * len(after), out_specs=blocks),
        compiler_params=pltpu.CompilerParams(dimension_semantics=("arbitrary",), vmem_limit_bytes=VMEM_LIMIT),
    )(lax.axis_index("c").reshape(1), *views, *landed, *after))


def _call(name, body, grid, in_specs, out_specs, out_shape, args, scratch=(), after=()):
    ni, na = len(in_specs), len(after)

    def ordered(*refs):
        body(*refs[:ni], *refs[ni + na:])

    return list(pl.pallas_call(
        ordered if after else body, name=name, grid=grid, in_specs=list(in_specs) + [ANY] * na,
        out_specs=list(out_specs), out_shape=list(out_shape), scratch_shapes=list(scratch),
        compiler_params=pltpu.CompilerParams(dimension_semantics=("arbitrary",) * len(grid),
                                             vmem_limit_bytes=VMEM_LIMIT))(*args, *after))


SIBLING_AND_NEIGHBOURS, OTHER_CHIPS, EVERYONE = (1, 4, 2), CHIPS, tuple(range(1, N_DEV))


def _by_sequencer(name, comm, peers, collective_id):
    src = [jax.new_ref(a, memory_space=pltpu.MemorySpace.HBM) for a in comm.args]
    dst = [jax.empty_ref(s, memory_space=pltpu.MemorySpace.HBM) for s in comm.out_shape]

    @pl.kernel(mesh=plsc.ScalarSubcoreMesh(axis_name="sequencer", num_cores=1), name=name,
               scratch_types=tuple(comm.scratch), compiler_params=pltpu.CompilerParams(collective_id=collective_id))
    def launch(*sems):
        x, y, c, _ = _my_place()
        barrier = pltpu.get_barrier_semaphore()
        for k in peers:
            pl.semaphore_signal(barrier, inc=1, device_id=_peer(x, y, c, k)[0], device_id_type=MESH)
        pl.semaphore_wait(barrier, len(peers))
        comm.start(src, dst, sems)
        if comm.relay is not None:
            comm.relay(src, dst, sems)
        comm.finish(src, dst, sems)

    launch()
    return [d[...] for d in dst]


def _gather_first(first, later, swapped):
    nf = len(first)
    layer_of = [(a, l) for a, s in enumerate(later) for l in range(s.shape[0])]
    nl = len(layer_of)
    dts = [BF16] * (nf - 2) + [F32, F32]
    shard = [s.shape[:0:-1] if sw else s.shape[1:] for s, sw in zip(later, swapped)]

    def body(*refs):
        ins, refs = refs[:nf + len(later)], refs[nf + len(later):]
        outs, refs = refs[:nf], refs[nf:]
        casts, refs = refs[:nl], refs[nl:]
        stage, sems = refs[:nf], refs[nf:]
        for a in range(nf):
            stage[a][...] = ins[a][...].astype(dts[a])
            _gather_start(stage, outs, sems, only=a)
        for k, (a, l) in enumerate(layer_of):
            block = ins[nf + a][l]
            casts[k][0] = (block.T if swapped[a] else block).astype(BF16)
        _gather_relay(stage, outs, sems)
        _gather_finish(stage, outs, sems)

    res = pl.pallas_call(
        body, name="gather_first",
        in_specs=[VMEM] * (nf + len(later)), out_specs=[ANY] * nf + [VMEM] * nl,
        out_shape=[_sds((N_DEV,) + s.shape[1:], dt) for s, dt in zip(first, dts)]
        + [_sds((1,) + shard[a], BF16) for a, _ in layer_of],
        scratch_shapes=[pltpu.VMEM(s.shape, dt) for s, dt in zip(first, dts)] + _relay_sems(nf),
        compiler_params=pltpu.CompilerParams(vmem_limit_bytes=VMEM_LIMIT),
    )(*first, *later)
    return list(res[:nf]), list(res[nf:])


def _a_mix_fwd(x, g, w_in, ln_g, ln_b, w_s, b_st, w_out):
    t = x.shape[0]
    nblk = TM // GMLP_BLOCK

    def body(x_ref, g_ref, win_ref, lng_ref, lnb_ref, ws_ref, bst_ref, wout_ref, h_ref, z_ref, gated_scr):
        xv = x_ref[...]
        hb = _rms_fwd(xv, g_ref[...])[0].astype(BF16)
        for d in range(N_DEV):
            z_ref[:, d * FF_SLOT:(d + 1) * FF_SLOT] = _dot(hb, win_ref[d])
        u = _gelu(z_ref[:, :GATE_DIM])
        vb = _ln_fwd(_gelu(z_ref[:, GATE_DIM:]), lng_ref[...], lnb_ref[...])[0].astype(BF16)
        mask = _gate_mask()
        for gi in range(A_GROUPS):
            wm = jnp.where(mask, ws_ref[gi], 0.0).astype(BF16)
            bias = bst_ref[:, gi:gi + 1]
            cs = slice(gi * A_GROUP_DIM, (gi + 1) * A_GROUP_DIM)
            for n in range(nblk):
                rs = slice(n * GMLP_BLOCK, (n + 1) * GMLP_BLOCK)
                sv = _dot(wm, vb[rs, cs]) + bias
                gated_scr[rs, cs] = (u[rs, cs] * sv).astype(BF16)
        h_ref[...] = xv + _dot(gated_scr[...], wout_ref[...])

    return _call(
        "a_mix_fwd", body, (t // TM,),
        [_row(D_MODEL), _res((1, D_MODEL)), _res((N_DEV, D_MODEL, FF_SLOT)), _res((1, GATE_DIM)),
         _res((1, GATE_DIM)), _res((A_GROUPS, GMLP_BLOCK, GMLP_BLOCK)), _res((GMLP_BLOCK, A_GROUPS)),
         _res((GATE_DIM, D_MODEL))],
        [_row(D_MODEL), _row(2 * GATE_DIM), _row(GATE_DIM)],
        [_sds((t, D_MODEL), F32), _sds((t, 2 * GATE_DIM), F32), _sds((t, GATE_DIM), BF16)],
        (x, g, w_in, ln_g, ln_b, w_s, b_st, w_out))


MLP_W_SPECS = (ANY, ANY)
MLP_W_SCRATCH = (pltpu.VMEM((N_DEV, D_MODEL, FF_SLOT), BF16), pltpu.VMEM((N_DEV, FF_SLOT, D_MODEL), BF16),
                 pltpu.SemaphoreType.DMA((2, N_DEV)))


def _with_streamed_weights(hbm, vmem, sems, order, compute):
    copies = {(j, d): pltpu.make_async_copy(hbm[j].at[d], vmem[j].at[d], sems.at[j, d])
              for d in range(N_DEV) for j in order}
    first = pl.program_id(0) == 0

    def assign(ref, val):
        ref[...] = val

    def add(ref, val):
        ref[...] += val

    @pl.when(first)
    def _():
        for c in copies.values():
            c.start()
        compute(lambda j, d: copies[j, d].wait(), assign)

    @pl.when(jnp.logical_not(first))
    def _():
        compute(lambda j, d: None, add)


def _mlp_fwd(h, g, w1, w2):
    t = h.shape[0]

    def body(h_ref, g_ref, w1_hbm, w2_hbm, o_ref, a_ref, w1_ref, w2_ref, sems):
        def compute(arrived, acc):
            hv = h_ref[...]
            hb = _rms_fwd(hv, g_ref[...])[0].astype(BF16)
            o_ref[...] = hv
            for d in range(N_DEV):
                arrived(0, d)
                a = _dot(hb, w1_ref[d])
                a_ref[:, d * FF_SLOT:(d + 1) * FF_SLOT] = a
                r = jnp.maximum(a, 0.0)
                arrived(1, d)
                o_ref[...] += _dot((r * r).astype(BF16), w2_ref[d])

        _with_streamed_weights((w1_hbm, w2_hbm), (w1_ref, w2_ref), sems, (0, 1), compute)

    return _call(
        "mlp_fwd", body, (t // TM_MLP_FWD,), [_row(D_MODEL, TM_MLP_FWD), _res((1, D_MODEL)), *MLP_W_SPECS],
        [_row(D_MODEL, TM_MLP_FWD), _row(D_FF, TM_MLP_FWD)], [_sds((t, D_MODEL), F32), _sds((t, D_FF), F32)],
        (h, g, w1, w2), scratch=MLP_W_SCRATCH)


def _mlp_fwd_loss(h, g, w1, w2, final_g, target):
    t = h.shape[0]

    def body(h_ref, g_ref, w1_hbm, w2_hbm, fg_ref, t_ref, a_ref, loss_ref, dh_ref, dg_ref, w1_ref, w2_ref, sems):
        def compute(arrived, acc):
            hv = h_ref[...]
            hb = _rms_fwd(hv, g_ref[...])[0].astype(BF16)
            out = hv
            for d in range(N_DEV):
                arrived(0, d)
                a = _dot(hb, w1_ref[d])
                a_ref[:, d * FF_SLOT:(d + 1) * FF_SLOT] = a
                r = jnp.maximum(a, 0.0)
                arrived(1, d)
                out = out + _dot((r * r).astype(BF16), w2_ref[d])
            y, xhat, rstd = _rms_fwd(out, fg_ref[...])
            err = y - t_ref[...]
            part = 0.5 * jnp.sum(jnp.mean(err * err, axis=-1, keepdims=True), axis=0, keepdims=True)
            dx, dg = _rms_bwd(err * (1.0 / D_MODEL), xhat, rstd, fg_ref[...])
            dh_ref[...] = dx
            acc(dg_ref, dg)
            acc(loss_ref, part)

        _with_streamed_weights((w1_hbm, w2_hbm), (w1_ref, w2_ref), sems, (0, 1), compute)

    return _call(
        "mlp_fwd_loss", body, (t // TM,),
        [_row(D_MODEL), _res((1, D_MODEL)), *MLP_W_SPECS, _res((1, D_MODEL)), _row(D_MODEL)],
        [_row(D_FF), _const((1, 1)), _row(D_MODEL), _const((1, D_MODEL))],
        [_sds((t, D_FF), F32), _sds((1, 1), F32), _sds((t, D_MODEL), F32), _sds((1, D_MODEL), F32)],
        (h, g, w1, w2, final_g, target), scratch=MLP_W_SCRATCH)


KVQ_W_SPECS = (_res((1, D_MODEL)), _res((D_MODEL, KV_LORA + QK_ROPE)), _res((1, KV_LORA)),
               _res((B_HEADS, KV_LORA, QK_NOPE + V_HEAD)), _res((1, D_MODEL)), _res((D_MODEL, Q_LORA)),
               _res((1, Q_LORA)), _res((B_HEADS, Q_LORA, QK_NOPE + QK_ROPE)))


def _kvq_fwd(h, pos, inv_freq, kvq_w):
    t = h.shape[0]
    half = QK_ROPE // 2

    def body(h_ref, pos_ref, invf_ref, srcg_ref, wkva_ref, kvag_ref, wkvb_ref, mixg_ref, wqa_ref, qg_ref, wqb_ref,
             ckv_ref, k_ref, v_ref, cqpre_ref, q_ref, cos_ref, sin_ref):
        hv = h_ref[...]
        xhat = hv * lax.rsqrt(jnp.mean(hv * hv, axis=-1, keepdims=True) + EPS)
        ang = pos_ref[...].astype(F32) * invf_ref[...]
        cos, sin = jnp.cos(ang), jnp.sin(ang)
        cos_ref[...] = cos
        sin_ref[...] = sin
        ckv = _dot((xhat * srcg_ref[...]).astype(BF16), wkva_ref[...])
        ckv_ref[...] = ckv
        cb = _rms_fwd(ckv[:, :KV_LORA], kvag_ref[...])[0].astype(BF16)
        kpe = _rope(ckv[:, KV_LORA:], cos, sin).astype(BF16)
        for hd in range(B_HEADS):
            kv = _dot(cb, wkvb_ref[hd])
            k_ref[hd, :, 0:QK_NOPE] = kv[:, :QK_NOPE].astype(BF16)
            k_ref[hd, :, QK_NOPE:] = kpe
            v_ref[hd] = kv[:, QK_NOPE:].astype(BF16)
        cqpre = _dot((xhat * mixg_ref[...]).astype(BF16), wqa_ref[...])
        cqpre_ref[...] = cqpre
        cqb = _rms_fwd(cqpre, qg_ref[...])[0].astype(BF16)
        for hd in range(B_HEADS):
            q = _dot(cqb, wqb_ref[hd])
            q_ref[hd, :, 0:QK_NOPE] = q[:, :QK_NOPE].astype(BF16)
            q_ref[hd, :, QK_NOPE:] = _rope(q[:, QK_NOPE:], cos, sin).astype(BF16)

    tm = TM_KVQ
    return _call(
        "kvq_fwd", body, (t // tm,), [_row(D_MODEL, tm), _row(1, tm), _res((1, half)), *KVQ_W_SPECS],
        [_row(KV_LORA + QK_ROPE, tm), _heads(QK_NOPE + QK_ROPE, tm), _heads(V_HEAD, tm), _row(Q_LORA, tm),
         _heads(QK_NOPE + QK_ROPE, tm), _row(half, tm), _row(half, tm)],
        [_sds((t, KV_LORA + QK_ROPE), F32), _sds((B_HEADS, t, QK_NOPE + QK_ROPE), BF16),
         _sds((B_HEADS, t, V_HEAD), BF16), _sds((t, Q_LORA), F32), _sds((B_HEADS, t, QK_NOPE + QK_ROPE), BF16),
         _sds((t, half), F32), _sds((t, half), F32)],
        (h, pos, inv_freq, *kvq_w))


def _softmax_rows(q, k_ref, k):
    past, upto = k * TM, (k + 1) * TM
    s = _dot_nt(q, k_ref[0:upto, :])
    own = jnp.where(_att_mask(0, TM, TM), s[:, past:], jnp.finfo(F32).min)
    s = own if k == 0 else jnp.concatenate([s[:, :past], own], axis=1)
    e = jnp.exp2((s - jnp.max(s, axis=-1, keepdims=True)) * (ATT_SCALE * LOG2_E))
    return e * (1.0 / jnp.sum(e, axis=-1, keepdims=True))


def _for_my_tile(i, nq, fn):
    for k in range(nq):
        @pl.when(i == k)
        def _(k=k):
            fn(k)


def _attn_fwd(h, q, k, v, w_o):
    t = h.shape[0]
    nq, hps = t // TM, HEADS_PER_STEP

    def body(h_ref, q_ref, k_ref, v_ref, wo_ref, o_ref, att_ref):
        i, pair = pl.program_id(0), pl.program_id(1)

        @pl.when(pair == 0)
        def _():
            o_ref[...] = h_ref[...]

        def tile(kt):
            proj = None
            for j in range(hps):
                hd = pair * hps + j
                p = _softmax_rows(q_ref[j], k_ref.at[hd], kt)
                ob = _dot(p.astype(BF16), v_ref[hd, 0:(kt + 1) * TM, :]).astype(BF16)
                att_ref[j] = ob
                proj = _dot(ob, wo_ref[hd]) if proj is None else proj + _dot(ob, wo_ref[hd])
            o_ref[...] += proj

        _for_my_tile(i, nq, tile)

    def per_head(d):
        return pl.BlockSpec((hps, TM, d), lambda i, pair: (pair, i, 0))

    def resident(shape):
        zeros = (0,) * len(shape)
        return pl.BlockSpec(shape, lambda i, pair: zeros, pipeline_mode=pl.Buffered(1))

    tile_spec = pl.BlockSpec((TM, D_MODEL), lambda i, pair: (i, 0))
    return _call(
        "attn_fwd", body, (nq, B_HEADS // hps),
        [tile_spec, per_head(QK_NOPE + QK_ROPE), resident((B_HEADS, t, QK_NOPE + QK_ROPE)),
         resident((B_HEADS, t, V_HEAD)), resident((B_HEADS, V_HEAD, D_MODEL))],
        [tile_spec, per_head(V_HEAD)], [_sds((t, D_MODEL), F32), _sds((B_HEADS, t, V_HEAD), BF16)],
        (h, q, k, v, w_o))


def _mlp_bwd(h, a, dho, g, w1, w2, layer, after=()):
    t = h.shape[0]

    def body(h_ref, a_ref, dho_ref, g_ref, w1_hbm, w2_hbm, dhi_ref, dg_ref, hn_ref, f_ref, da_ref, dhib_ref,
             w1_ref, w2_ref, sems):
        def compute(arrived, acc):
            gv = g_ref[...]
            y, xhat, rstd = _rms_fwd(h_ref[...], gv)
            hn_ref[...] = y.astype(BF16)
            dho_v = dho_ref[...]
            dhob = dho_v.astype(BF16)
            dhn = jnp.zeros((TM, D_MODEL), F32)
            for d in range(N_DEV):
                cs = slice(d * FF_SLOT, (d + 1) * FF_SLOT)
                r = jnp.maximum(a_ref[:, cs], 0.0)
                f_ref[:, cs] = (r * r).astype(BF16)
                arrived(1, d)
                da = (_dot_nt(dhob, w2_ref[d]) * (2.0 * r)).astype(BF16)
                da_ref[:, cs] = da
                arrived(0, d)
                dhn = dhn + _dot_nt(da, w1_ref[d])
            dx, dg = _rms_bwd(dhn, xhat, rstd, gv)
            dhi = dho_v + dx
            dhi_ref[...] = dhi
            dhib_ref[...] = dhi.astype(BF16)
            acc(dg_ref, dg)

        _with_streamed_weights((w1_hbm, w2_hbm), (w1_ref, w2_ref), sems, (1, 0), compute)

    return _call(
        f"mlp_bwd_{layer}", body, (t // TM,),
        [_row(D_MODEL), _row(D_FF), _row(D_MODEL), _res((1, D_MODEL)), *MLP_W_SPECS],
        [_row(D_MODEL), _const((1, D_MODEL)), _row(D_MODEL), _row(D_FF), _row(D_FF), _row(D_MODEL)],
        [_sds((t, D_MODEL), F32), _sds((1, D_MODEL), F32), _sds((t, D_MODEL), BF16), _sds((t, D_FF), BF16),
         _sds((t, D_FF), BF16), _sds((t, D_MODEL), BF16)],
        (h, a, dho, g, w1, w2), scratch=MLP_W_SCRATCH, after=after)


def _attn_bwd(dh, q, k, v, w_o, cos, sin, after=()):
    t = dh.shape[0]
    half, hps = QK_ROPE // 2, HEADS_PER_STEP

    def body(dh_ref, q_ref, k_ref, v_ref, wo_ref, cos_ref, sin_ref, dq_ref, dk_ref, dv_ref):
        i = pl.program_id(1)

        @pl.when(i == 0)
        def _():
            dk_ref[...] = jnp.zeros_like(dk_ref)
            dv_ref[...] = jnp.zeros_like(dv_ref)

        def tile(kt):
            keys = slice(0, (kt + 1) * TM)
            for j in range(hps):
                qj = q_ref[j]
                do = _dot_nt(dh_ref[kt * TM:(kt + 1) * TM, :], wo_ref[j]).astype(BF16)
                p = _softmax_rows(qj, k_ref.at[j], kt)
                dp = _dot_nt(do, v_ref[j, keys, :])
                ds = (p * (dp - jnp.sum(p * dp, axis=-1, keepdims=True)) * ATT_SCALE).astype(BF16)
                dq = _dot(ds, k_ref[j, keys, :])
                dq_ref[j, :, 0:QK_NOPE] = dq[:, :QK_NOPE].astype(BF16)
                dq_ref[j, :, QK_NOPE:] = _rope(dq[:, QK_NOPE:], cos_ref[...], -sin_ref[...]).astype(BF16)
                dk_ref[j, keys, :] += _dot_tn(ds, qj)
                dv_ref[j, keys, :] += _dot_tn(p.astype(BF16), do)

        _for_my_tile(i, t // TM, tile)

    def per_pair(rows, d, tiled):
        return pl.BlockSpec((hps, rows, d), (lambda pair, i: (pair, i, 0)) if tiled else (lambda pair, i: (pair, 0, 0)))

    def tile(d):
        return pl.BlockSpec((TM, d), lambda pair, i: (i, 0))

    return _call(
        "attn_bwd", body, (B_HEADS // hps, t // TM),
        [pl.BlockSpec((t, D_MODEL), lambda pair, i: (0, 0), pipeline_mode=pl.Buffered(1)),
         per_pair(TM, QK_NOPE + QK_ROPE, True), per_pair(t, QK_NOPE + QK_ROPE, False), per_pair(t, V_HEAD, False),
         per_pair(V_HEAD, D_MODEL, False), tile(half), tile(half)],
        [per_pair(TM, QK_NOPE + QK_ROPE, True), per_pair(t, QK_NOPE + QK_ROPE, False), per_pair(t, V_HEAD, False)],
        [_sds((B_HEADS, t, QK_NOPE + QK_ROPE), BF16), _sds((B_HEADS, t, QK_NOPE + QK_ROPE), F32),
         _sds((B_HEADS, t, V_HEAD), F32)],
        (dh, q, k, v, w_o, cos, sin), after=after)


def _kvq_bwd(h, dh, ckv, cqpre, dq, dk, dv, cos, sin, kvq_w, after=()):
    t = h.shape[0]
    tm = TM
    half, last = QK_ROPE // 2, t // tm - 1
    grad_shapes = [(D_MODEL, Q_LORA), (B_HEADS, Q_LORA, QK_NOPE + QK_ROPE), (D_MODEL, KV_LORA + QK_ROPE),
                   (B_HEADS, KV_LORA, QK_NOPE + V_HEAD)]

    def body(h_ref, dh_ref, ckv_ref, cqpre_ref, dq_ref, dk_ref, dv_ref, cos_ref, sin_ref,
             srcg_ref, wkva_ref, kvag_ref, wkvb_ref, mixg_ref, wqa_ref, qg_ref, wqb_ref,
             dhi_ref, dmixg_ref, dsrcg_ref, dqg_ref, dkvag_ref, gqa_ref, gqb_ref, gkva_ref, gkvb_ref,
             aqa, aqb, akva, akvb):
        @pl.when(pl.program_id(0) == 0)
        def _():
            for acc in (aqa, aqb, akva, akvb):
                acc[...] = jnp.zeros_like(acc)

        hv = h_ref[...]
        rstd = lax.rsqrt(jnp.mean(hv * hv, axis=-1, keepdims=True) + EPS)
        xhat = hv * rstd
        mixg, srcg, qg, kvag = mixg_ref[...], srcg_ref[...], qg_ref[...], kvag_ref[...]
        cq, cqhat, crstd = _rms_fwd(cqpre_ref[...], qg)
        cqb = cq.astype(BF16)
        dcq = jnp.zeros((tm, Q_LORA), F32)
        for hd in range(B_HEADS):
            dcq = dcq + _dot_nt(dq_ref[hd], wqb_ref[hd])
            aqb[hd] += _dot_tn(cqb, dq_ref[hd])
        dcqpre, dqg = _rms_bwd(dcq, cqhat, crstd, qg)
        dcqpre_b = dcqpre.astype(BF16)
        aqa[...] += _dot_tn((xhat * mixg).astype(BF16), dcqpre_b)
        dxq, dmixg = _rms_bwd(_dot_nt(dcqpre_b, wqa_ref[...]), xhat, rstd, mixg)
        ckv = ckv_ref[...]
        c, chat, krstd = _rms_fwd(ckv[:, :KV_LORA], kvag)
        cb = c.astype(BF16)
        dc = jnp.zeros((tm, KV_LORA), F32)
        dkpe = jnp.zeros((tm, QK_ROPE), F32)
        for hd in range(B_HEADS):
            dkv = jnp.concatenate([dk_ref[hd, :, 0:QK_NOPE], dv_ref[hd]], axis=-1).astype(BF16)
            akvb[hd] += _dot_tn(cb, dkv)
            dc = dc + _dot_nt(dkv, wkvb_ref[hd])
            dkpe = dkpe + dk_ref[hd, :, QK_NOPE:]
        dlat, dkvag = _rms_bwd(dc, chat, krstd, kvag)
        dpe = _rope(dkpe, cos_ref[...], -sin_ref[...])
        dckv_b = jnp.concatenate([dlat, dpe], axis=-1).astype(BF16)
        akva[...] += _dot_tn((xhat * srcg).astype(BF16), dckv_b)
        dxk, dsrcg = _rms_bwd(_dot_nt(dckv_b, wkva_ref[...]), xhat, rstd, srcg)
        dhi_ref[...] = dh_ref[...] + dxq + dxk
        _acc(dmixg_ref, dmixg)
        _acc(dsrcg_ref, dsrcg)
        _acc(dqg_ref, dqg)
        _acc(dkvag_ref, dkvag)

        @pl.when(pl.program_id(0) == last)
        def _():
            for out, acc in ((gqa_ref, aqa), (gqb_ref, aqb), (gkva_ref, akva), (gkvb_ref, akvb)):
                out[...] = acc[...].astype(BF16)

    return _call(
        "kvq_bwd", body, (t // tm,),
        [_row(D_MODEL, tm), _row(D_MODEL, tm), _row(KV_LORA + QK_ROPE, tm), _row(Q_LORA, tm),
         _heads(QK_NOPE + QK_ROPE, tm), _heads(QK_NOPE + QK_ROPE, tm), _heads(V_HEAD, tm), _row(half, tm),
         _row(half, tm), *KVQ_W_SPECS],
        [_row(D_MODEL, tm), _const((1, D_MODEL)), _const((1, D_MODEL)), _const((1, Q_LORA)), _const((1, KV_LORA))]
        + [_const(s) for s in grad_shapes],
        [_sds((t, D_MODEL), F32), _sds((1, D_MODEL), F32), _sds((1, D_MODEL), F32), _sds((1, Q_LORA), F32),
         _sds((1, KV_LORA), F32)] + [_sds(s, BF16) for s in grad_shapes],
        (h, dh, ckv, cqpre, dq, dk, dv, cos, sin, *kvq_w), scratch=[pltpu.VMEM(s, F32) for s in grad_shapes],
        after=after)


def _a_mix_bwd(x, z, dh, g, w_in, ln_g, ln_b, w_s, b_st, w_out, after=()):
    t = x.shape[0]
    tm = TM_GATE
    nblk = tm // GMLP_BLOCK

    def body(x_ref, z_ref, dh_ref, g_ref, win_ref, lng_ref, lnb_ref, ws_ref, bst_ref, wout_ref,
             dx_ref, hn_ref, dz_ref, dg_ref, dlng_ref, dlnb_ref, dws_ref, dbs_ref, dvn_scr, gelu_grad_v):
        @pl.when(pl.program_id(0) == 0)
        def _():
            dws_ref[...] = jnp.zeros_like(dws_ref)
            dbs_ref[...] = jnp.zeros_like(dbs_ref)

        gv, lng = g_ref[...], lng_ref[...]
        y, xhat, rstd = _rms_fwd(x_ref[...], gv)
        hn_ref[...] = y.astype(BF16)
        dhv = dh_ref[...]
        dgated = _dot_nt(dhv.astype(BF16), wout_ref[...])
        u, gelu_grad_u = _gelu_and_grad(z_ref[:, :GATE_DIM])
        v, gelu_grad_v[...] = _gelu_and_grad(z_ref[:, GATE_DIM:])
        vn, vhat, lrstd = _ln_fwd(v, lng, lnb_ref[...])
        vb = vn.astype(BF16)
        mask = _gate_mask()
        for gi in range(A_GROUPS):
            wm = jnp.where(mask, ws_ref[gi], 0.0).astype(BF16)
            bias = bst_ref[:, gi:gi + 1]
            cs = slice(gi * A_GROUP_DIM, (gi + 1) * A_GROUP_DIM)
            dws = jnp.zeros((GMLP_BLOCK, GMLP_BLOCK), F32)
            dbs = jnp.zeros((GMLP_BLOCK, 1), F32)
            for n in range(nblk):
                rs = slice(n * GMLP_BLOCK, (n + 1) * GMLP_BLOCK)
                sv = _dot(wm, vb[rs, cs]) + bias
                dz_ref[rs, cs] = (dgated[rs, cs] * sv * gelu_grad_u[rs, cs]).astype(BF16)
                dsv = dgated[rs, cs] * u[rs, cs]
                dsvb = dsv.astype(BF16)
                dws = dws + _dot_nt(dsvb, vb[rs, cs])
                dbs = dbs + jnp.sum(dsv, axis=-1, keepdims=True)
                dvn_scr[rs, cs] = _dot_tn(wm, dsvb)
            dws_ref[gi] += jnp.where(mask, dws, 0.0)
            dbs_ref[gi] += dbs
        dvn = dvn_scr[...]
        dvhat = dvn * lng
        dv = lrstd * (dvhat - jnp.mean(dvhat, axis=-1, keepdims=True)
                      - vhat * jnp.mean(dvhat * vhat, axis=-1, keepdims=True))
        dz_ref[:, GATE_DIM:] = (dv * gelu_grad_v[...]).astype(BF16)
        dhn = jnp.zeros((tm, D_MODEL), F32)
        for d in range(N_DEV):
            dhn = dhn + _dot_nt(dz_ref[:, d * FF_SLOT:(d + 1) * FF_SLOT], win_ref[d])
        dx, dg = _rms_bwd(dhn, xhat, rstd, gv)
        dx_ref[...] = dhv + dx
        _acc(dg_ref, dg)
        _acc(dlng_ref, jnp.sum(dvn * vhat, axis=0, keepdims=True))
        _acc(dlnb_ref, jnp.sum(dvn, axis=0, keepdims=True))

    return _call(
        "a_mix_bwd", body, (t // tm,),
        [_row(D_MODEL, tm), _row(2 * GATE_DIM, tm), _row(D_MODEL, tm), _res((1, D_MODEL)),
         _res((N_DEV, D_MODEL, FF_SLOT)), _res((1, GATE_DIM)), _res((1, GATE_DIM)),
         _res((A_GROUPS, GMLP_BLOCK, GMLP_BLOCK)), _res((GMLP_BLOCK, A_GROUPS)), _res((GATE_DIM, D_MODEL))],
        [_row(D_MODEL, tm), _row(D_MODEL, tm), _row(2 * GATE_DIM, tm),
         _const((1, D_MODEL)), _const((1, GATE_DIM)), _const((1, GATE_DIM)),
         _const((A_GROUPS, GMLP_BLOCK, GMLP_BLOCK)), _const((A_GROUPS, GMLP_BLOCK, 1))],
        [_sds((t, D_MODEL), F32), _sds((t, D_MODEL), BF16),
         _sds((t, 2 * GATE_DIM), BF16), _sds((1, D_MODEL), F32), _sds((1, GATE_DIM), F32),
         _sds((1, GATE_DIM), F32), _sds((A_GROUPS, GMLP_BLOCK, GMLP_BLOCK), F32),
         _sds((A_GROUPS, GMLP_BLOCK, 1), F32)],
        (x, z, dh, g, w_in, ln_g, ln_b, w_s, b_st, w_out),
        scratch=[pltpu.VMEM((tm, GATE_DIM), F32), pltpu.VMEM((tm, GATE_DIM), F32)], after=after)


def _wgrad(name, a, b, a_spec, b_spec, m, n, after=()):
    def body(a_ref, b_ref, o_ref):
        o_ref[0] = _dot_tn(a_ref[...].astype(BF16), b_ref[...].astype(BF16)).astype(BF16)

    return _call(name, body, (N_DEV,), [a_spec, b_spec], [pl.BlockSpec((1, m, n), lambda d: (d, 0, 0))],
                 [_sds((N_DEV, m, n), BF16)], (a, b), after=after)[0]


def _full(t, d):
    return pl.BlockSpec((t, d), lambda i: (0, 0), pipeline_mode=pl.Buffered(1))


def _cols(t, d):
    return pl.BlockSpec((t, d), lambda i: (0, i))


def _head(t, d):
    return pl.BlockSpec((None, t, d), lambda i: (i, 0, 0))


def _local_step(x, pos, target, inv_freq, wg, sm, shards=None):
    t = x.shape[0]
    wg = dict(wg)
    dist = shards is not None
    mix_g = [sm["norm_mix_g"][l:l + 1] for l in range(2)]
    mlp_g = [sm["norm_mlp_g"][l:l + 1] for l in range(2)]

    ids = iter(range(2, 2 + 9))

    def gather(names):
        if dist:
            got = _by_sequencer("gather_" + names[0], _gather_comm([shards[k] for k in names]),
                                SIBLING_AND_NEIGHBOURS, next(ids))
            wg.update(zip(names, got))

    def send(name, names):
        if dist:
            comm = _exchange_comm(grads=[g[k] for k in names])
            g.update(zip(names, _by_sequencer("exchange_" + name, comm, EVERYONE, next(ids))))

    def send_sums(name, names, meanwhile):
        if not dist:
            meanwhile()
            return ()
        grads = [g[k] for k in names]
        landed = _by_sequencer("pair_exchange_" + name, _pair_exchange_comm(grads), (1,), next(ids))
        sums = _pair_add("pair_add_" + name, grads, landed, after=meanwhile())
        g.update(zip(names, _by_sequencer("exchange_" + name, _chip_exchange_comm(sums), OTHER_CHIPS, next(ids))))
        return sums

    def a_args():
        return (wg["a_w_in"], wg["a_ln_v_g"], wg["a_ln_v_b"], sm["a_w_s"], sm["a_b_st"], wg["a_w_out"])

    def kvq_w():
        return (sm["kv_src_norm_g"], wg["kv_w_a"], sm["kv_a_norm_g"], wg["kv_w_b"], mix_g[1], wg["b_w_q_a"],
                sm["b_q_norm_g"], wg["b_w_q_b"])

    gather(("mlp_w1_0", "mlp_w2_0"))
    h1, z, gated = _a_mix_fwd(x, mix_g[0], *a_args())
    gather(("kv_w_a", "kv_w_b", "b_w_q_a", "b_w_q_b", "b_w_o"))
    h2, a0 = _mlp_fwd(h1, mlp_g[0], wg["mlp_w1_0"], wg["mlp_w2_0"])
    if dist:
        wg["b_w_q_a"] = wg["b_w_q_a"].reshape(D_MODEL, Q_LORA)
        wg["kv_w_a"] = wg["kv_w_a"].reshape(D_MODEL, KV_LORA + QK_ROPE)
    gather(("mlp_w1_1", "mlp_w2_1"))
    ckv, k, v, cqpre, q, cos, sin = _kvq_fwd(h2, pos, inv_freq, kvq_w())
    h3, att = _attn_fwd(h2, q, k, v, wg["b_w_o"])
    a1, loss, dh4, d_final_g = _mlp_fwd_loss(h3, mlp_g[1], wg["mlp_w1_1"], wg["mlp_w2_1"], sm["final_norm_g"], target)

    g = {}
    dh3, d_mlp_g1, hn, f, da, dh3_b = _mlp_bwd(h3, a1, dh4, mlp_g[1], wg["mlp_w1_1"], wg["mlp_w2_1"], 1)
    dq, dk, dv = _attn_bwd(dh3_b, q, k, v, wg["b_w_o"], cos, sin)
    g["mlp_w1_1"] = _wgrad("wgrad_w1_1", hn, da, _full(t, D_MODEL), _cols(t, FF_SLOT), D_MODEL, FF_SLOT, after=[dq])
    g["mlp_w2_1"] = _wgrad("wgrad_w2_1", f, dh4, _cols(t, FF_SLOT), _full(t, D_MODEL), FF_SLOT, D_MODEL)

    def wgrad_w_o():
        g["b_w_o"] = _wgrad("wgrad_w_o", att, dh3_b, _head(t, V_HEAD), _full(t, D_MODEL), V_HEAD, D_MODEL)
        return [g["b_w_o"]]

    sums = send_sums("mlp_1", ("mlp_w1_1", "mlp_w2_1"), wgrad_w_o)
    dh2, d_mix_g1, d_src_g, d_q_g, d_kv_a_g, g_q_a, g["b_w_q_b"], g_kv_a, g["kv_w_b"] = _kvq_bwd(
        h2, dh3, ckv, cqpre, dq, dk, dv, cos, sin, kvq_w(), after=sums)
    g["b_w_q_a"] = g_q_a.reshape(N_DEV, D_MODEL // N_DEV, Q_LORA)
    g["kv_w_a"] = g_kv_a.reshape(N_DEV, D_MODEL // N_DEV, KV_LORA + QK_ROPE)
    qkv = ("b_w_q_a", "b_w_q_b", "kv_w_a", "kv_w_b")
    landed = [g[k] for k in qkv]
    send("qkv", qkv)
    dh1, d_mlp_g0, hn, f, da, dh1_b = _mlp_bwd(h1, a0, dh2, mlp_g[0], wg["mlp_w1_0"], wg["mlp_w2_0"], 0,
                                               after=landed if dist else ())
    landed = [g["mlp_w1_1"], g["mlp_w2_1"]] if dist else ()
    g["mlp_w1_0"] = _wgrad("wgrad_w1_0", hn, da, _full(t, D_MODEL), _cols(t, FF_SLOT), D_MODEL, FF_SLOT, after=landed)
    g["mlp_w2_0"] = _wgrad("wgrad_w2_0", f, dh2, _cols(t, FF_SLOT), _full(t, D_MODEL), FF_SLOT, D_MODEL)

    def wgrad_a_w_out():
        g["a_w_out"] = _wgrad("wgrad_a_w_out", gated, dh1_b, _cols(t, GATE_DIM // N_DEV), _full(t, D_MODEL),
                              GATE_DIM // N_DEV, D_MODEL)
        return [g["a_w_out"]] + [g[k] for k in qkv]

    sums = send_sums("mlp_0", ("mlp_w1_0", "mlp_w2_0", "b_w_o"), wgrad_a_w_out)
    if dist:
        sums = _pair_reduce("pair_reduce_a_w_out", [g["a_w_out"]], after=sums)
    dx, hn, dz, d_mix_g0, d_ln_g, d_ln_b, d_ws, d_bs = _a_mix_bwd(x, z, dh1, mix_g[0], *a_args(), after=sums)
    small = {
        "norm_mix_g": jnp.concatenate([d_mix_g0, d_mix_g1], axis=0),
        "norm_mlp_g": jnp.concatenate([d_mlp_g0, d_mlp_g1], axis=0),
        "a_ln_v_g": d_ln_g.reshape(N_DEV, GATE_DIM // N_DEV),
        "a_ln_v_b": d_ln_b.reshape(N_DEV, GATE_DIM // N_DEV),
        "a_w_s": d_ws.astype(BF16) if dist else d_ws,
        "a_b_s": d_bs.reshape(A_GROUPS, GMLP_BLOCK),
        "b_q_norm_g": d_q_g,
        "kv_src_norm_g": d_src_g,
        "kv_a_norm_g": d_kv_a_g,
        "final_norm_g": d_final_g,
    }
    if dist:
        parts = [small[k].reshape((1,) + small[k].shape) for k in SMALL] + [loss.reshape(1, 1, 1)]
        comm = _together(_chip_exchange_comm(sums), _gather_comm(parts))
        g["a_w_out"], *got = _by_sequencer("exchange_a_w_out", comm, EVERYONE, next(ids))
        small, loss = dict(zip(SMALL, got)), got[-1]
    g["a_w_in"] = _wgrad("wgrad_a_w_in", hn, dz, _full(t, D_MODEL), _cols(t, FF_SLOT), D_MODEL, FF_SLOT)
    return loss, dx, g, small


def _adamw(w, g, m, v):
    m = ADAM_B1 * m + (1.0 - ADAM_B1) * g
    v = ADAM_B2 * v + (1.0 - ADAM_B2) * (g * g)
    m_hat = m / (1.0 - ADAM_B1 ** ADAM_STEP)
    v_hat = v / (1.0 - ADAM_B2 ** ADAM_STEP)
    return -ADAM_LR * (m_hat / (jnp.sqrt(v_hat) + ADAM_EPS) + ADAM_WD * w), m, v


def _sum_in_device_order(r_ref):
    g = r_ref[0].astype(F32)
    for j in range(1, r_ref.shape[0]):
        g = g + r_ref[j].astype(F32)
    return g


def _adamw_sharded(name, recvs, w, m, v, swapped=False):
    layers, r, c = w.shape[0], *recvs[0][0].shape[1:]
    tr = r if swapped else math.gcd(r, 512)
    flat = [a for per_layer in recvs for a in per_layer]

    def body(*refs):
        r_refs, (w_ref, m_ref, v_ref) = refs[:len(flat)], refs[len(flat):len(flat) + 3]
        g_ref, d_ref, nm_ref, nv_ref = refs[-4:]
        layer = pl.program_id(0)
        g, pos = None, 0
        for li, per_layer in enumerate(recvs):
            total = None
            for ref in r_refs[pos:pos + len(per_layer)]:
                part = _sum_in_device_order(ref)
                total = part if total is None else total + part
            pos += len(per_layer)
            g = total if g is None else jnp.where(layer == li, total, g)
        if swapped:
            g = g.T
        g_ref[...] = g
        d_ref[...], nm_ref[...], nv_ref[...] = _adamw(w_ref[...], g, m_ref[...], v_ref[...])

    blk = pl.BlockSpec((None, tr, c), lambda l, i: (l, i, 0))
    if swapped:
        blk = pl.BlockSpec((None, c, r), lambda l, i: (l, 0, 0))
    return _call(name, body, (layers, r // tr),
                 [pl.BlockSpec((a.shape[0], tr, c), lambda l, i: (0, i, 0)) for a in flat] + [blk] * 3,
                 [blk] * 4, [_sds(w.shape, F32)] * 4, (*flat, w, m, v))


def _adamw_small(recvs, ws, ms, vs, own_row, losses):
    n = len(recvs)

    def body(*refs):
        r_refs, w_refs, m_refs, v_refs = (refs[i * n:(i + 1) * n] for i in range(4))
        outs, scr = refs[4 * n + 1:8 * n + 2], refs[8 * n + 2:]
        outs[-1][...] = _sum_in_device_order(refs[4 * n])
        me = _my_place()[3]
        for a in range(n):
            g = _sum_in_device_order(r_refs[a])
            if own_row[a]:
                scr[0][...] = g
                g = scr[0][pl.ds(me, 1), :]
            g_ref, d_ref, nm_ref, nv_ref = outs[4 * a:4 * a + 4]
            g_ref[...] = g
            d_ref[...], nm_ref[...], nv_ref[...] = _adamw(w_refs[a][...], g, m_refs[a][...], v_refs[a][...])

    out_shape = []
    for w in ws:
        out_shape += [_sds(w.shape, F32)] * 4
    return pl.pallas_call(
        body, name="adamw_small", in_specs=[VMEM] * (4 * n + 1), out_specs=[VMEM] * (4 * n + 1),
        out_shape=out_shape + [_sds((1, 1), F32)], scratch_shapes=[pltpu.VMEM((N_DEV, GATE_DIM // N_DEV), F32)],
    )(*recvs, *ws, *ms, *vs, losses)


BIG = ("a_w_in", "a_w_out", "b_w_q_a", "b_w_q_b", "b_w_o", "kv_w_a", "kv_w_b", "mlp_w1", "mlp_w2")
SMALL = ("norm_mix_g", "norm_mlp_g", "a_ln_v_g", "a_ln_v_b", "a_w_s", "a_b_s", "b_q_norm_g", "kv_src_norm_g",
         "kv_a_norm_g", "final_norm_g")
WEIGHTS = ("norm_mix_g", "norm_mlp_g", "a_w_in", "a_ln_v_g", "a_ln_v_b", "a_w_s", "a_b_s", "a_w_out", "b_w_q_a",
           "b_q_norm_g", "b_w_q_b", "b_w_o", "kv_src_norm_g", "kv_w_a", "kv_a_norm_g", "kv_w_b", "mlp_w1", "mlp_w2",
           "final_norm_g")


def _two_d(name, a):
    if name in ("a_w_s", "a_b_s"):
        return a.reshape(a.shape[1:])
    return a.reshape(1, -1) if a.ndim == 1 else a


def _three_d(a):
    return a if a.ndim == 3 else a.reshape((1,) + a.shape)


SWAPPED = ("b_w_q_b", "kv_w_a")


def _swapped(a):
    return jnp.swapaxes(_three_d(a), 1, 2)


def kernel(x, positions, norm_mix_g, norm_mlp_g, a_w_in, a_ln_v_g, a_ln_v_b, a_w_s, a_b_s, a_w_out, b_w_q_a, b_q_norm_g, b_w_q_b, b_w_o, kv_src_norm_g, kv_w_a, kv_a_norm_g, kv_w_b, mlp_w1, mlp_w2, final_norm_g, loss_target, m_norm_mix_g, m_norm_mlp_g, m_a_w_in, m_a_ln_v_g, m_a_ln_v_b, m_a_w_s, m_a_b_s, m_a_w_out, m_b_w_q_a, m_b_q_norm_g, m_b_w_q_b, m_b_w_o, m_kv_src_norm_g, m_kv_w_a, m_kv_a_norm_g, m_kv_w_b, m_mlp_w1, m_mlp_w2, m_final_norm_g, v_norm_mix_g, v_norm_mlp_g, v_a_w_in, v_a_ln_v_g, v_a_ln_v_b, v_a_w_s, v_a_b_s, v_a_w_out, v_b_w_q_a, v_b_q_norm_g, v_b_w_q_b, v_b_w_o, v_kv_src_norm_g, v_kv_w_a, v_kv_a_norm_g, v_kv_w_b, v_mlp_w1, v_mlp_w2, v_final_norm_g):
    w = dict(norm_mix_g=norm_mix_g, norm_mlp_g=norm_mlp_g, a_w_in=a_w_in, a_ln_v_g=a_ln_v_g, a_ln_v_b=a_ln_v_b,
             a_w_s=a_w_s, a_b_s=a_b_s, a_w_out=a_w_out, b_w_q_a=b_w_q_a, b_q_norm_g=b_q_norm_g, b_w_q_b=b_w_q_b,
             b_w_o=b_w_o, kv_src_norm_g=kv_src_norm_g, kv_w_a=kv_w_a, kv_a_norm_g=kv_a_norm_g, kv_w_b=kv_w_b,
             mlp_w1=mlp_w1, mlp_w2=mlp_w2, final_norm_g=final_norm_g)
    m = dict(norm_mix_g=m_norm_mix_g, norm_mlp_g=m_norm_mlp_g, a_w_in=m_a_w_in, a_ln_v_g=m_a_ln_v_g,
             a_ln_v_b=m_a_ln_v_b, a_w_s=m_a_w_s, a_b_s=m_a_b_s, a_w_out=m_a_w_out, b_w_q_a=m_b_w_q_a,
             b_q_norm_g=m_b_q_norm_g, b_w_q_b=m_b_w_q_b, b_w_o=m_b_w_o, kv_src_norm_g=m_kv_src_norm_g,
             kv_w_a=m_kv_w_a, kv_a_norm_g=m_kv_a_norm_g, kv_w_b=m_kv_w_b, mlp_w1=m_mlp_w1, mlp_w2=m_mlp_w2,
             final_norm_g=m_final_norm_g)
    v = dict(norm_mix_g=v_norm_mix_g, norm_mlp_g=v_norm_mlp_g, a_w_in=v_a_w_in, a_ln_v_g=v_a_ln_v_g,
             a_ln_v_b=v_a_ln_v_b, a_w_s=v_a_w_s, a_b_s=v_a_b_s, a_w_out=v_a_w_out, b_w_q_a=v_b_w_q_a,
             b_q_norm_g=v_b_q_norm_g, b_w_q_b=v_b_w_q_b, b_w_o=v_b_w_o, kv_src_norm_g=v_kv_src_norm_g,
             kv_w_a=v_kv_w_a, kv_a_norm_g=v_kv_a_norm_g, kv_w_b=v_kv_w_b, mlp_w1=v_mlp_w1, mlp_w2=v_mlp_w2,
             final_norm_g=v_final_norm_g)
    t = x.shape[1]

    first = ("a_w_in", "a_w_out", "a_ln_v_g", "a_ln_v_b")
    later = ("mlp_w1", "mlp_w2", "kv_w_a", "kv_w_b", "b_w_q_a", "b_w_q_b", "b_w_o")
    later_blocks = ("mlp_w1_0", "mlp_w1_1", "mlp_w2_0", "mlp_w2_1") + later[2:]
    got, casts = _gather_first([_three_d(w[k]) if k in BIG else w[k] for k in first],
                               [_swapped(w[k]) if k in SWAPPED else _three_d(w[k]) for k in later],
                               [k in SWAPPED for k in later])
    wg = dict(zip(first, got))
    wg["a_w_out"] = wg["a_w_out"].reshape(GATE_DIM, D_MODEL)
    wg["a_ln_v_g"] = wg["a_ln_v_g"].reshape(1, GATE_DIM)
    wg["a_ln_v_b"] = wg["a_ln_v_b"].reshape(1, GATE_DIM)
    shards = dict(zip(later_blocks, casts))

    sm = {k: _two_d(k, w[k]) for k in SMALL if k not in ("a_ln_v_g", "a_ln_v_b")}
    sm["a_b_st"] = sm["a_b_s"].T
    inv_freq = (ROPE_THETA ** (-jnp.arange(0, QK_ROPE, 2, dtype=F32) / QK_ROPE)).reshape(1, QK_ROPE // 2)

    losses, dx, g, small = _local_step(x[0], positions.reshape(t, 1), loss_target[0], inv_freq, wg, sm, shards)

    sums = _pair_reduce("pair_reduce_a_w_in", [g["a_w_in"]], after=[g["mlp_w1_0"], g["mlp_w2_0"]])
    g["a_w_in"], = _by_sequencer("exchange_last", _chip_exchange_comm(sums), OTHER_CHIPS, collective_id=1)

    out = {}
    for k in BIG:
        recvs = [[g[k + "_0"]], [g[k + "_1"]]] if k.startswith("mlp") else [[g[k]]]
        view = _swapped if k in SWAPPED else _three_d
        res = _adamw_sharded("adamw_" + k, recvs, view(w[k]), view(m[k]), view(v[k]), swapped=k in SWAPPED)
        out[k] = [view(o).reshape(w[k].shape) for o in res]
    own_row = [k in ("a_ln_v_g", "a_ln_v_b") for k in SMALL]
    res = _adamw_small([small[k] for k in SMALL], [_two_d(k, w[k]) for k in SMALL], [_two_d(k, m[k]) for k in SMALL],
                       [_two_d(k, v[k]) for k in SMALL], own_row, losses)
    for i, k in enumerate(SMALL):
        out[k] = [o.reshape(w[k].shape) for o in res[4 * i:4 * i + 4]]

    return (res[-1].reshape(()), dx.reshape(x.shape), *[out[k][0] for k in WEIGHTS], *[out[k][1] for k in WEIGHTS],
            *[out[k][2] for k in WEIGHTS], *[out[k][3] for k in WEIGHTS])
```

```python
import math

import jax
import jax.numpy as jnp
from jax import lax
from jax.experimental import pallas as pl
from jax.experimental.pallas import tpu as pltpu
from jax.experimental.pallas import tpu_sc as plsc

F32, BF16 = jnp.float32, jnp.bfloat16
MESH = pl.DeviceIdType.MESH
ANY = pl.BlockSpec(memory_space=pl.ANY)
VMEM = pl.BlockSpec(memory_space=pltpu.VMEM)

N_DEV = 8
D_MODEL = 1024
CHUNK = 64
GMLP_BLOCK = 128
GATE_DIM = 2048
A_GROUPS = 8
A_GROUP_DIM = GATE_DIM // A_GROUPS
B_HEADS = 8
QK_NOPE, QK_ROPE, V_HEAD = 128, 64, 128
Q_LORA, KV_LORA = 384, 256
ROPE_THETA = 10000.0
D_FF = 4096
FF_SLOT = D_FF // N_DEV
EPS = 1e-6
ATT_SCALE = (QK_NOPE + QK_ROPE) ** -0.5

ADAM_LR, ADAM_B1, ADAM_B2, ADAM_EPS, ADAM_WD, ADAM_STEP = 0.001, 0.9, 0.999, 1e-08, 0.01, 10

TM = 256
TM_GATE = 256
TM_MLP_FWD = 512
TM_KVQ = 512
VMEM_LIMIT = 56 * 1024 * 1024
INV_SQRT2 = 1.0 / math.sqrt(2.0)
INV_SQRT_2PI = 1.0 / math.sqrt(2.0 * math.pi)
LOG2_E = 1.0 / math.log(2.0)
HEADS_PER_STEP = 2


def _dot(a, b):
    return jnp.dot(a, b, preferred_element_type=F32)


def _dot_nt(a, b):
    return lax.dot_general(a, b, (((1,), (1,)), ((), ())), preferred_element_type=F32)


def _dot_tn(a, b):
    return lax.dot_general(a, b, (((0,), (0,)), ((), ())), preferred_element_type=F32)


def _rms_fwd(x, g):
    rstd = lax.rsqrt(jnp.mean(x * x, axis=-1, keepdims=True) + EPS)
    xhat = x * rstd
    return xhat * g, xhat, rstd


def _rms_bwd(dy, xhat, rstd, g):
    dxhat = dy * g
    dx = rstd * (dxhat - xhat * jnp.mean(dxhat * xhat, axis=-1, keepdims=True))
    return dx, jnp.sum(dy * xhat, axis=0, keepdims=True)


def _ln_fwd(v, g, b):
    mu = jnp.mean(v, axis=-1, keepdims=True)
    vc = v - mu
    rstd = lax.rsqrt(jnp.mean(vc * vc, axis=-1, keepdims=True) + EPS)
    vhat = vc * rstd
    return vhat * g + b, vhat, rstd


def _gelu(x):
    return 0.5 * x * (1.0 + lax.erf(x * INV_SQRT2))


def _gelu_and_grad(x):
    cdf = 0.5 * (1.0 + lax.erf(x * INV_SQRT2))
    return x * cdf, cdf + x * jnp.exp(-0.5 * x * x) * INV_SQRT_2PI


def _rope(x, cos, sin):
    x1, x2 = x[:, :QK_ROPE // 2], x[:, QK_ROPE // 2:]
    return jnp.concatenate([x1 * cos - x2 * sin, x2 * cos + x1 * sin], axis=-1)


def _gate_mask():
    row = lax.broadcasted_iota(jnp.int32, (GMLP_BLOCK, GMLP_BLOCK), 0)
    col = lax.broadcasted_iota(jnp.int32, (GMLP_BLOCK, GMLP_BLOCK), 1)
    return (col < CHUNK) | (row >= CHUNK)


def _att_mask(q0, tq, t):
    q = q0 + lax.broadcasted_iota(jnp.int32, (tq, t), 0)
    k = lax.broadcasted_iota(jnp.int32, (tq, t), 1)
    return jnp.right_shift(k, 6) <= jnp.right_shift(q, 6)


def _res(shape, imap=None):
    zeros = (0,) * len(shape)
    return pl.BlockSpec(shape, imap or (lambda i: zeros), pipeline_mode=pl.Buffered(1))


def _const(shape):
    zeros = (0,) * len(shape)
    return pl.BlockSpec(shape, lambda i: zeros)


def _row(d, tm=TM):
    return pl.BlockSpec((tm, d), lambda i: (i, 0))


def _heads(d, tm=TM):
    return pl.BlockSpec((B_HEADS, tm, d), lambda i: (0, i, 0))


def _sds(shape, dt):
    return jax.ShapeDtypeStruct(shape, dt)


def _acc(ref, val):
    @pl.when(pl.program_id(0) == 0)
    def _():
        ref[...] = jnp.zeros_like(ref)
    ref[...] += val


def _my_place():
    x, y, c = lax.axis_index("x"), lax.axis_index("y"), lax.axis_index("c")
    return x, y, c, 4 * x + 2 * y + c


def _peer(x, y, c, k):
    px = 1 - x if k & 4 else x
    py = 1 - y if k & 2 else y
    pc = 1 - c if k & 1 else c
    return (px, py, pc), 4 * px + 2 * py + pc


CHIPS = (2, 4, 6)


def _splits(ref):
    return len(ref.shape) >= 3 and ref.shape[1] % 32 == 0


def _piece(ref, block, half=None):
    if half is None or not _splits(ref):
        return ref.at[pl.ds(block, 1)]
    rows = ref.shape[1] // 2
    return ref.at[pl.ds(block, 1), pl.ds(half * rows, rows)]


def _gather_copy(sems, a, k, piece, to, src=None):
    return pltpu.make_async_remote_copy(
        src_ref=piece if src is None else src, dst_ref=piece, send_sem=sems[0].at[a, k], recv_sem=sems[1].at[a, k],
        device_id=to, device_id_type=MESH)


def _gather_start(srcs, outs, sems, only=None):
    x, y, c, me = _my_place()
    for a in range(len(srcs)) if only is None else (only,):
        mine = _piece(outs[a], me)
        pltpu.make_async_copy(srcs[a], mine, sems[2].at[a]).start()
        for k, rel in enumerate((1, 4, 2)):
            _gather_copy(sems, a, k, mine, _peer(x, y, c, rel)[0], src=srcs[a]).start()


def _gather_relay(srcs, outs, sems):
    x, y, c, _ = _my_place()
    sib = _peer(x, y, c, 1)[0]
    (xn, xn_i), (yn, yn_i) = _peer(x, y, c, 4), _peer(x, y, c, 2)
    for a in range(len(srcs)):
        out = outs[a]
        _gather_copy(sems, a, 1, _piece(out, xn_i), xn).wait_recv()
        _gather_copy(sems, a, 3, _piece(out, xn_i, 0), yn).start()
        _gather_copy(sems, a, 5, _piece(out, xn_i), sib).start()
        _gather_copy(sems, a, 2, _piece(out, yn_i), yn).wait_recv()
        if _splits(out):
            _gather_copy(sems, a, 4, _piece(out, yn_i, 1), xn).start()
        _gather_copy(sems, a, 6, _piece(out, yn_i), sib).start()


def _gather_finish(srcs, outs, sems):
    x, y, c, me = _my_place()
    sib = _peer(x, y, c, 1)[0]
    xn, yn, dg_i = _peer(x, y, c, 4)[0], _peer(x, y, c, 2)[0], _peer(x, y, c, 6)[1]
    n = len(srcs)
    for a in range(n):
        out = outs[a]
        _gather_copy(sems, a, 3, _piece(out, dg_i, 0), yn).wait_recv()
        _gather_copy(sems, a, 7, _piece(out, dg_i, 0), sib).start()
        if _splits(out):
            _gather_copy(sems, a, 4, _piece(out, dg_i, 1), xn).wait_recv()
            _gather_copy(sems, a, 8, _piece(out, dg_i, 1), sib).start()
    for a in range(n):
        out = outs[a]
        whole, half = _piece(out, me), _piece(out, me, 0)
        for k in (0, 5, 6):
            _gather_copy(sems, a, k, whole, sib).wait_recv()
        for k in (7, 8) if _splits(out) else (7,):
            _gather_copy(sems, a, k, half, sib).wait_recv()
        for k in (0, 1, 2):
            _gather_copy(sems, a, k, whole, sib, src=srcs[a]).wait_send()
        for k in (5, 6):
            _gather_copy(sems, a, k, whole, sib).wait_send()
        for k in (3, 4, 7, 8) if _splits(out) else (3, 7):
            _gather_copy(sems, a, k, half, sib).wait_send()
        pltpu.make_async_copy(srcs[a], whole, sems[2].at[a]).wait()


def _relay_sems(n):
    return [pltpu.SemaphoreType.DMA((n, 9)), pltpu.SemaphoreType.DMA((n, 9)), pltpu.SemaphoreType.DMA((n,))]


def _gather_sems(n):
    return [pltpu.SemaphoreType.DMA((n, 7)), pltpu.SemaphoreType.DMA((n, 7)), pltpu.SemaphoreType.DMA((n,))]


class _Comm:
    def __init__(self, args, out_shape, scratch, start, finish, relay=None):
        self.args, self.out_shape, self.scratch, self.start, self.finish = args, out_shape, scratch, start, finish
        self.relay = relay


def _gather_comm(shards):
    return _Comm(list(shards), [_sds((N_DEV,) + s.shape[1:], s.dtype) for s in shards], _relay_sems(len(shards)),
                 _gather_start, _gather_finish, relay=_gather_relay)


def _together(big, small):
    na, no, ns = len(big.args), len(big.out_shape), len(big.scratch)

    def start(src, dst, sems):
        small.start(src[na:], dst[no:], sems[ns:])

    def relay(src, dst, sems):
        small.relay(src[na:], dst[no:], sems[ns:])
        big.start(src[:na], dst[:no], sems[:ns])

    def finish(src, dst, sems):
        small.finish(src[na:], dst[no:], sems[ns:])
        big.finish(src[:na], dst[:no], sems[:ns])

    assert big.relay is None and small.relay is not None
    return _Comm(big.args + small.args, big.out_shape + small.out_shape, list(big.scratch) + list(small.scratch),
                 start, finish, relay=relay)


def _direct_copies(ins, outs, sems, wait):
    send_sems, recv_sems, local_sems = sems
    x, y, c, me = _my_place()
    for a in range(len(ins)):
        local = pltpu.make_async_copy(ins[a].at[pl.ds(me, 1)], outs[a].at[pl.ds(me, 1)], local_sems.at[a])
        local.wait() if wait else local.start()
        for k in range(1, N_DEV):
            to, to_i = _peer(x, y, c, k)
            cp = pltpu.make_async_remote_copy(
                src_ref=ins[a].at[pl.ds(to_i, 1)], dst_ref=outs[a].at[pl.ds(me, 1)],
                send_sem=send_sems.at[a, k - 1], recv_sem=recv_sems.at[a, k - 1], device_id=to, device_id_type=MESH)
            cp.wait() if wait else cp.start()


def _exchange_comm(grads):
    return _Comm(list(grads), [_sds(g.shape, g.dtype) for g in grads], _gather_sems(len(grads)),
                 lambda i, o, s: _direct_copies(i, o, s, False), lambda i, o, s: _direct_copies(i, o, s, True))


def _chip_copies(ins, outs, sems, wait):
    send_sems, recv_sems, local_sems = sems
    x, y, c, _ = _my_place()
    for a in range(len(ins)):
        local = pltpu.make_async_copy(ins[a].at[pl.ds(2 * x + y, 1)], outs[a].at[pl.ds(len(CHIPS), 1)],
                                      local_sems.at[a])
        local.wait() if wait else local.start()
        for i, k in enumerate(CHIPS):
            to = _peer(x, y, c, k)[0]
            cp = pltpu.make_async_remote_copy(
                src_ref=ins[a].at[pl.ds(2 * to[0] + to[1], 1)], dst_ref=outs[a].at[pl.ds(i, 1)],
                send_sem=send_sems.at[a, i], recv_sem=recv_sems.at[a, i], device_id=to, device_id_type=MESH)
            cp.wait() if wait else cp.start()


def _chip_exchange_comm(sums):
    n = len(sums)
    sems = [pltpu.SemaphoreType.DMA((n, len(CHIPS))), pltpu.SemaphoreType.DMA((n, len(CHIPS))),
            pltpu.SemaphoreType.DMA((n,))]
    return _Comm(list(sums), [_sds(s.shape, s.dtype) for s in sums], sems,
                 lambda i, o, s: _chip_copies(i, o, s, False), lambda i, o, s: _chip_copies(i, o, s, True))


def _pair_reduce(name, grads, after=()):
    n = len(grads)
    n_chips = N_DEV // 2

    def body(*refs):
        g_refs, gh_refs, refs = refs[:n], refs[n:2 * n], refs[2 * n + len(after):]
        p_refs, land = refs[:n], refs[n:2 * n]
        send_sems, recv_sems = refs[2 * n:]
        x, y, c, _ = _my_place()
        sib = _peer(x, y, c, 1)[0]
        q = pl.program_id(0)

        def to_sibling(a, j):
            return pltpu.make_async_remote_copy(
                src_ref=gh_refs[a].at[j, pl.ds(1 - c, 1)], dst_ref=land[a].at[pl.ds(j, 1)],
                send_sem=send_sems.at[a, j], recv_sem=recv_sems.at[a, j], device_id=sib, device_id_type=MESH)

        @pl.when(q == 0)
        def _():
            for j in range(n_chips):
                for a in range(n):
                    to_sibling(a, j).start()

        for a in range(n):
            to_sibling(a, q).wait_recv()
            p_refs[a][...] = (g_refs[a][0, pl.ds(c, 1)].astype(F32) + land[a][pl.ds(q, 1)].astype(F32)).astype(BF16)

        @pl.when(q == n_chips - 1)
        def _():
            for a in range(n):
                for j in range(n_chips):
                    to_sibling(a, j).wait_send()

    views = [g.reshape((n_chips, 2) + g.shape[1:]) for g in grads]
    res = pl.pallas_call(
        body, name=name, grid=(n_chips,),
        in_specs=[pl.BlockSpec((1, 2) + g.shape[1:], lambda q: (q, 0, 0, 0)) for g in grads]
        + [ANY] * (n + len(after)),
        out_specs=[pl.BlockSpec((1,) + g.shape[1:], lambda q: (q, 0, 0)) for g in grads],
        out_shape=[_sds((n_chips,) + g.shape[1:], BF16) for g in grads],
        scratch_shapes=[pltpu.VMEM((n_chips,) + g.shape[1:], BF16) for g in grads]
        + [pltpu.SemaphoreType.DMA((n, n_chips)), pltpu.SemaphoreType.DMA((n, n_chips))],
        compiler_params=pltpu.CompilerParams(dimension_semantics=("arbitrary",), vmem_limit_bytes=VMEM_LIMIT),
    )(*views, *views, *after)
    return list(res)


def _pair_exchange_comm(grads):
    n, n_chips = len(grads), N_DEV // 2

    def copies(ins, outs, sems, wait):
        x, y, c, _ = _my_place()
        for j in range(n_chips):
            for a in range(n):
                cp = pltpu.make_async_remote_copy(
                    src_ref=ins[a].at[j, pl.ds(1 - c, 1)], dst_ref=outs[a].at[pl.ds(j, 1)], send_sem=sems[0].at[a, j],
                    recv_sem=sems[1].at[a, j], device_id=_peer(x, y, c, 1)[0], device_id_type=MESH)
                cp.wait() if wait else cp.start()

    views = [g.reshape((n_chips, 2) + g.shape[1:]) for g in grads]
    sems = [pltpu.SemaphoreType.DMA((n, n_chips)), pltpu.SemaphoreType.DMA((n, n_chips))]
    return _Comm(views, [_sds((n_chips,) + g.shape[1:], g.dtype) for g in grads], sems,
                 lambda i, o, s: copies(i, o, s, False), lambda i, o, s: copies(i, o, s, True))


def _pair_add(name, grads, landed, after=()):
    n, n_chips = len(grads), N_DEV // 2

    def body(core_ref, *refs):
        g_refs, l_refs, p_refs = refs[:n], refs[n:2 * n], refs[2 * n + len(after):]
        for a in range(n):
            p_refs[a][...] = (g_refs[a][...].astype(F32) + l_refs[a][...].astype(F32)).astype(BF16)

    views = [g.reshape((n_chips, 2) + g.shape[1:]) for g in grads]
    blocks = [pl.BlockSpec((1,) + g.shape[1:], lambda q, core: (q, 0, 0)) for g in grads]
    mine = [pl.BlockSpec((1, None) + g.shape[1:], lambda q, core: (q, core[0], 0, 0)) for g in grads]
    return list(pl.pallas_call(
        body, name=name, out_shape=[_sds((n_chips,) + g.shape[1:], BF16) for g in grads],
        grid_spec=pltpu.PrefetchScalarGridSpec(num_scalar_prefetch=1, grid=(n_chips,),
                                               in_specs=mine + blocks + [ANY] * len(after), out_specs=blocks),
        compiler_params=pltpu.CompilerParams(dimension_semantics=("arbitrary",), vmem_limit_bytes=VMEM_LIMIT),
    )(lax.axis_index("c").reshape(1), *views, *landed, *after))


def _call(name, body, grid, in_specs, out_specs, out_shape, args, scratch=(), after=()):
    ni, na = len(in_specs), len(after)

    def ordered(*refs):
        body(*refs[:ni], *refs[ni + na:])

    return list(pl.pallas_call(
        ordered if after else body, name=name, grid=grid, in_specs=list(in_specs) + [ANY] * na,
        out_specs=list(out_specs), out_shape=list(out_shape), scratch_shapes=list(scratch),
        compiler_params=pltpu.CompilerParams(dimension_semantics=("arbitrary",) * len(grid),
                                             vmem_limit_bytes=VMEM_LIMIT))(*args, *after))


SIBLING_AND_NEIGHBOURS, OTHER_CHIPS, EVERYONE = (1, 4, 2), CHIPS, tuple(range(1, N_DEV))


def _by_sequencer(name, comm, peers, collective_id):
    src = [jax.new_ref(a, memory_space=pltpu.MemorySpace.HBM) for a in comm.args]
    dst = [jax.empty_ref(s, memory_space=pltpu.MemorySpace.HBM) for s in comm.out_shape]

    @pl.kernel(mesh=plsc.ScalarSubcoreMesh(axis_name="sequencer", num_cores=1), name=name,
               scratch_types=tuple(comm.scratch), compiler_params=pltpu.CompilerParams(collective_id=collective_id))
    def launch(*sems):
        x, y, c, _ = _my_place()
        barrier = pltpu.get_barrier_semaphore()
        for k in peers:
            pl.semaphore_signal(barrier, inc=1, device_id=_peer(x, y, c, k)[0], device_id_type=MESH)
        pl.semaphore_wait(barrier, len(peers))
        comm.start(src, dst, sems)
        if comm.relay is not None:
            comm.relay(src, dst, sems)
        comm.finish(src, dst, sems)

    launch()
    return [d[...] for d in dst]


def _gather_first(first, later, swapped):
    nf = len(first)
    layer_of = [(a, l) for a, s in enumerate(later) for l in range(s.shape[0])]
    nl = len(layer_of)
    dts = [BF16] * (nf - 2) + [F32, F32]
    shard = [s.shape[:0:-1] if sw else s.shape[1:] for s, sw in zip(later, swapped)]

    def body(*refs):
        ins, refs = refs[:nf + len(later)], refs[nf + len(later):]
        outs, refs = refs[:nf], refs[nf:]
        casts, refs = refs[:nl], refs[nl:]
        stage, sems = refs[:nf], refs[nf:]
        for a in range(nf):
            stage[a][...] = ins[a][...].astype(dts[a])
            _gather_start(stage, outs, sems, only=a)
        for k, (a, l) in enumerate(layer_of):
            block = ins[nf + a][l]
            casts[k][0] = (block.T if swapped[a] else block).astype(BF16)
        _gather_relay(stage, outs, sems)
        _gather_finish(stage, outs, sems)

    res = pl.pallas_call(
        body, name="gather_first",
        in_specs=[VMEM] * (nf + len(later)), out_specs=[ANY] * nf + [VMEM] * nl,
        out_shape=[_sds((N_DEV,) + s.shape[1:], dt) for s, dt in zip(first, dts)]
        + [_sds((1,) + shard[a], BF16) for a, _ in layer_of],
        scratch_shapes=[pltpu.VMEM(s.shape, dt) for s, dt in zip(first, dts)] + _relay_sems(nf),
        compiler_params=pltpu.CompilerParams(vmem_limit_bytes=VMEM_LIMIT),
    )(*first, *later)
    return list(res[:nf]), list(res[nf:])


def _a_mix_fwd(x, g, w_in, ln_g, ln_b, w_s, b_st, w_out):
    t = x.shape[0]
    nblk = TM // GMLP_BLOCK

    def body(x_ref, g_ref, win_ref, lng_ref, lnb_ref, ws_ref, bst_ref, wout_ref, h_ref, z_ref, gated_scr):
        xv = x_ref[...]
        hb = _rms_fwd(xv, g_ref[...])[0].astype(BF16)
        for d in range(N_DEV):
            z_ref[:, d * FF_SLOT:(d + 1) * FF_SLOT] = _dot(hb, win_ref[d])
        u = _gelu(z_ref[:, :GATE_DIM])
        vb = _ln_fwd(_gelu(z_ref[:, GATE_DIM:]), lng_ref[...], lnb_ref[...])[0].astype(BF16)
        mask = _gate_mask()
        for gi in range(A_GROUPS):
            wm = jnp.where(mask, ws_ref[gi], 0.0).astype(BF16)
            bias = bst_ref[:, gi:gi + 1]
            cs = slice(gi * A_GROUP_DIM, (gi + 1) * A_GROUP_DIM)
            for n in range(nblk):
                rs = slice(n * GMLP_BLOCK, (n + 1) * GMLP_BLOCK)
                sv = _dot(wm, vb[rs, cs]) + bias
                gated_scr[rs, cs] = (u[rs, cs] * sv).astype(BF16)
        h_ref[...] = xv + _dot(gated_scr[...], wout_ref[...])

    return _call(
        "a_mix_fwd", body, (t // TM,),
        [_row(D_MODEL), _res((1, D_MODEL)), _res((N_DEV, D_MODEL, FF_SLOT)), _res((1, GATE_DIM)),
         _res((1, GATE_DIM)), _res((A_GROUPS, GMLP_BLOCK, GMLP_BLOCK)), _res((GMLP_BLOCK, A_GROUPS)),
         _res((GATE_DIM, D_MODEL))],
        [_row(D_MODEL), _row(2 * GATE_DIM), _row(GATE_DIM)],
        [_sds((t, D_MODEL), F32), _sds((t, 2 * GATE_DIM), F32), _sds((t, GATE_DIM), BF16)],
        (x, g, w_in, ln_g, ln_b, w_s, b_st, w_out))


MLP_W_SPECS = (ANY, ANY)
MLP_W_SCRATCH = (pltpu.VMEM((N_DEV, D_MODEL, FF_SLOT), BF16), pltpu.VMEM((N_DEV, FF_SLOT, D_MODEL), BF16),
                 pltpu.SemaphoreType.DMA((2, N_DEV)))


def _with_streamed_weights(hbm, vmem, sems, order, compute):
    copies = {(j, d): pltpu.make_async_copy(hbm[j].at[d], vmem[j].at[d], sems.at[j, d])
              for d in range(N_DEV) for j in order}
    first = pl.program_id(0) == 0

    def assign(ref, val):
        ref[...] = val

    def add(ref, val):
        ref[...] += val

    @pl.when(first)
    def _():
        for c in copies.values():
            c.start()
        compute(lambda j, d: copies[j, d].wait(), assign)

    @pl.when(jnp.logical_not(first))
    def _():
        compute(lambda j, d: None, add)


def _mlp_fwd(h, g, w1, w2):
    t = h.shape[0]

    def body(h_ref, g_ref, w1_hbm, w2_hbm, o_ref, a_ref, w1_ref, w2_ref, sems):
        def compute(arrived, acc):
            hv = h_ref[...]
            hb = _rms_fwd(hv, g_ref[...])[0].astype(BF16)
            o_ref[...] = hv
            for d in range(N_DEV):
                arrived(0, d)
                a = _dot(hb, w1_ref[d])
                a_ref[:, d * FF_SLOT:(d + 1) * FF_SLOT] = a
                r = jnp.maximum(a, 0.0)
                arrived(1, d)
                o_ref[...] += _dot((r * r).astype(BF16), w2_ref[d])

        _with_streamed_weights((w1_hbm, w2_hbm), (w1_ref, w2_ref), sems, (0, 1), compute)

    return _call(
        "mlp_fwd", body, (t // TM_MLP_FWD,), [_row(D_MODEL, TM_MLP_FWD), _res((1, D_MODEL)), *MLP_W_SPECS],
        [_row(D_MODEL, TM_MLP_FWD), _row(D_FF, TM_MLP_FWD)], [_sds((t, D_MODEL), F32), _sds((t, D_FF), F32)],
        (h, g, w1, w2), scratch=MLP_W_SCRATCH)


def _mlp_fwd_loss(h, g, w1, w2, final_g, target):
    t = h.shape[0]

    def body(h_ref, g_ref, w1_hbm, w2_hbm, fg_ref, t_ref, a_ref, loss_ref, dh_ref, dg_ref, w1_ref, w2_ref, sems):
        def compute(arrived, acc):
            hv = h_ref[...]
            hb = _rms_fwd(hv, g_ref[...])[0].astype(BF16)
            out = hv
            for d in range(N_DEV):
                arrived(0, d)
                a = _dot(hb, w1_ref[d])
                a_ref[:, d * FF_SLOT:(d + 1) * FF_SLOT] = a
                r = jnp.maximum(a, 0.0)
                arrived(1, d)
                out = out + _dot((r * r).astype(BF16), w2_ref[d])
            y, xhat, rstd = _rms_fwd(out, fg_ref[...])
            err = y - t_ref[...]
            part = 0.5 * jnp.sum(jnp.mean(err * err, axis=-1, keepdims=True), axis=0, keepdims=True)
            dx, dg = _rms_bwd(err * (1.0 / D_MODEL), xhat, rstd, fg_ref[...])
            dh_ref[...] = dx
            acc(dg_ref, dg)
            acc(loss_ref, part)

        _with_streamed_weights((w1_hbm, w2_hbm), (w1_ref, w2_ref), sems, (0, 1), compute)

    return _call(
        "mlp_fwd_loss", body, (t // TM,),
        [_row(D_MODEL), _res((1, D_MODEL)), *MLP_W_SPECS, _res((1, D_MODEL)), _row(D_MODEL)],
        [_row(D_FF), _const((1, 1)), _row(D_MODEL), _const((1, D_MODEL))],
        [_sds((t, D_FF), F32), _sds((1, 1), F32), _sds((t, D_MODEL), F32), _sds((1, D_MODEL), F32)],
        (h, g, w1, w2, final_g, target), scratch=MLP_W_SCRATCH)


KVQ_W_SPECS = (_res((1, D_MODEL)), _res((D_MODEL, KV_LORA + QK_ROPE)), _res((1, KV_LORA)),
               _res((B_HEADS, KV_LORA, QK_NOPE + V_HEAD)), _res((1, D_MODEL)), _res((D_MODEL, Q_LORA)),
               _res((1, Q_LORA)), _res((B_HEADS, Q_LORA, QK_NOPE + QK_ROPE)))


def _kvq_fwd(h, pos, inv_freq, kvq_w):
    t = h.shape[0]
    half = QK_ROPE // 2

    def body(h_ref, pos_ref, invf_ref, srcg_ref, wkva_ref, kvag_ref, wkvb_ref, mixg_ref, wqa_ref, qg_ref, wqb_ref,
             ckv_ref, k_ref, v_ref, cqpre_ref, q_ref, cos_ref, sin_ref):
        hv = h_ref[...]
        xhat = hv * lax.rsqrt(jnp.mean(hv * hv, axis=-1, keepdims=True) + EPS)
        ang = pos_ref[...].astype(F32) * invf_ref[...]
        cos, sin = jnp.cos(ang), jnp.sin(ang)
        cos_ref[...] = cos
        sin_ref[...] = sin
        ckv = _dot((xhat * srcg_ref[...]).astype(BF16), wkva_ref[...])
        ckv_ref[...] = ckv
        cb = _rms_fwd(ckv[:, :KV_LORA], kvag_ref[...])[0].astype(BF16)
        kpe = _rope(ckv[:, KV_LORA:], cos, sin).astype(BF16)
        for hd in range(B_HEADS):
            kv = _dot(cb, wkvb_ref[hd])
            k_ref[hd, :, 0:QK_NOPE] = kv[:, :QK_NOPE].astype(BF16)
            k_ref[hd, :, QK_NOPE:] = kpe
            v_ref[hd] = kv[:, QK_NOPE:].astype(BF16)
        cqpre = _dot((xhat * mixg_ref[...]).astype(BF16), wqa_ref[...])
        cqpre_ref[...] = cqpre
        cqb = _rms_fwd(cqpre, qg_ref[...])[0].astype(BF16)
        for hd in range(B_HEADS):
            q = _dot(cqb, wqb_ref[hd])
            q_ref[hd, :, 0:QK_NOPE] = q[:, :QK_NOPE].astype(BF16)
            q_ref[hd, :, QK_NOPE:] = _rope(q[:, QK_NOPE:], cos, sin).astype(BF16)

    tm = TM_KVQ
    return _call(
        "kvq_fwd", body, (t // tm,), [_row(D_MODEL, tm), _row(1, tm), _res((1, half)), *KVQ_W_SPECS],
        [_row(KV_LORA + QK_ROPE, tm), _heads(QK_NOPE + QK_ROPE, tm), _heads(V_HEAD, tm), _row(Q_LORA, tm),
         _heads(QK_NOPE + QK_ROPE, tm), _row(half, tm), _row(half, tm)],
        [_sds((t, KV_LORA + QK_ROPE), F32), _sds((B_HEADS, t, QK_NOPE + QK_ROPE), BF16),
         _sds((B_HEADS, t, V_HEAD), BF16), _sds((t, Q_LORA), F32), _sds((B_HEADS, t, QK_NOPE + QK_ROPE), BF16),
         _sds((t, half), F32), _sds((t, half), F32)],
        (h, pos, inv_freq, *kvq_w))


def _softmax_rows(q, k_ref, k):
    past, upto = k * TM, (k + 1) * TM
    s = _dot_nt(q, k_ref[0:upto, :])
    own = jnp.where(_att_mask(0, TM, TM), s[:, past:], jnp.finfo(F32).min)
    s = own if k == 0 else jnp.concatenate([s[:, :past], own], axis=1)
    e = jnp.exp2((s - jnp.max(s, axis=-1, keepdims=True)) * (ATT_SCALE * LOG2_E))
    return e * (1.0 / jnp.sum(e, axis=-1, keepdims=True))


def _for_my_tile(i, nq, fn):
    for k in range(nq):
        @pl.when(i == k)
        def _(k=k):
            fn(k)


def _attn_fwd(h, q, k, v, w_o):
    t = h.shape[0]
    nq, hps = t // TM, HEADS_PER_STEP

    def body(h_ref, q_ref, k_ref, v_ref, wo_ref, o_ref, att_ref):
        i, pair = pl.program_id(0), pl.program_id(1)

        @pl.when(pair == 0)
        def _():
            o_ref[...] = h_ref[...]

        def tile(kt):
            proj = None
            for j in range(hps):
                hd = pair * hps + j
                p = _softmax_rows(q_ref[j], k_ref.at[hd], kt)
                ob = _dot(p.astype(BF16), v_ref[hd, 0:(kt + 1) * TM, :]).astype(BF16)
                att_ref[j] = ob
                proj = _dot(ob, wo_ref[hd]) if proj is None else proj + _dot(ob, wo_ref[hd])
            o_ref[...] += proj

        _for_my_tile(i, nq, tile)

    def per_head(d):
        return pl.BlockSpec((hps, TM, d), lambda i, pair: (pair, i, 0))

    def resident(shape):
        zeros = (0,) * len(shape)
        return pl.BlockSpec(shape, lambda i, pair: zeros, pipeline_mode=pl.Buffered(1))

    tile_spec = pl.BlockSpec((TM, D_MODEL), lambda i, pair: (i, 0))
    return _call(
        "attn_fwd", body, (nq, B_HEADS // hps),
        [tile_spec, per_head(QK_NOPE + QK_ROPE), resident((B_HEADS, t, QK_NOPE + QK_ROPE)),
         resident((B_HEADS, t, V_HEAD)), resident((B_HEADS, V_HEAD, D_MODEL))],
        [tile_spec, per_head(V_HEAD)], [_sds((t, D_MODEL), F32), _sds((B_HEADS, t, V_HEAD), BF16)],
        (h, q, k, v, w_o))


def _mlp_bwd(h, a, dho, g, w1, w2, layer, after=()):
    t = h.shape[0]

    def body(h_ref, a_ref, dho_ref, g_ref, w1_hbm, w2_hbm, dhi_ref, dg_ref, hn_ref, f_ref, da_ref, dhib_ref,
             w1_ref, w2_ref, sems):
        def compute(arrived, acc):
            gv = g_ref[...]
            y, xhat, rstd = _rms_fwd(h_ref[...], gv)
            hn_ref[...] = y.astype(BF16)
            dho_v = dho_ref[...]
            dhob = dho_v.astype(BF16)
            dhn = jnp.zeros((TM, D_MODEL), F32)
            for d in range(N_DEV):
                cs = slice(d * FF_SLOT, (d + 1) * FF_SLOT)
                r = jnp.maximum(a_ref[:, cs], 0.0)
                f_ref[:, cs] = (r * r).astype(BF16)
                arrived(1, d)
                da = (_dot_nt(dhob, w2_ref[d]) * (2.0 * r)).astype(BF16)
                da_ref[:, cs] = da
                arrived(0, d)
                dhn = dhn + _dot_nt(da, w1_ref[d])
            dx, dg = _rms_bwd(dhn, xhat, rstd, gv)
            dhi = dho_v + dx
            dhi_ref[...] = dhi
            dhib_ref[...] = dhi.astype(BF16)
            acc(dg_ref, dg)

        _with_streamed_weights((w1_hbm, w2_hbm), (w1_ref, w2_ref), sems, (1, 0), compute)

    return _call(
        f"mlp_bwd_{layer}", body, (t // TM,),
        [_row(D_MODEL), _row(D_FF), _row(D_MODEL), _res((1, D_MODEL)), *MLP_W_SPECS],
        [_row(D_MODEL), _const((1, D_MODEL)), _row(D_MODEL), _row(D_FF), _row(D_FF), _row(D_MODEL)],
        [_sds((t, D_MODEL), F32), _sds((1, D_MODEL), F32), _sds((t, D_MODEL), BF16), _sds((t, D_FF), BF16),
         _sds((t, D_FF), BF16), _sds((t, D_MODEL), BF16)],
        (h, a, dho, g, w1, w2), scratch=MLP_W_SCRATCH, after=after)


def _attn_bwd(dh, q, k, v, w_o, cos, sin, after=()):
    t = dh.shape[0]
    half, hps = QK_ROPE // 2, HEADS_PER_STEP

    def body(dh_ref, q_ref, k_ref, v_ref, wo_ref, cos_ref, sin_ref, dq_ref, dk_ref, dv_ref):
        i = pl.program_id(1)

        @pl.when(i == 0)
        def _():
            dk_ref[...] = jnp.zeros_like(dk_ref)
            dv_ref[...] = jnp.zeros_like(dv_ref)

        def tile(kt):
            keys = slice(0, (kt + 1) * TM)
            for j in range(hps):
                qj = q_ref[j]
                do = _dot_nt(dh_ref[kt * TM:(kt + 1) * TM, :], wo_ref[j]).astype(BF16)
                p = _softmax_rows(qj, k_ref.at[j], kt)
                dp = _dot_nt(do, v_ref[j, keys, :])
                ds = (p * (dp - jnp.sum(p * dp, axis=-1, keepdims=True)) * ATT_SCALE).astype(BF16)
                dq = _dot(ds, k_ref[j, keys, :])
                dq_ref[j, :, 0:QK_NOPE] = dq[:, :QK_NOPE].astype(BF16)
                dq_ref[j, :, QK_NOPE:] = _rope(dq[:, QK_NOPE:], cos_ref[...], -sin_ref[...]).astype(BF16)
                dk_ref[j, keys, :] += _dot_tn(ds, qj)
                dv_ref[j, keys, :] += _dot_tn(p.astype(BF16), do)

        _for_my_tile(i, t // TM, tile)

    def per_pair(rows, d, tiled):
        return pl.BlockSpec((hps, rows, d), (lambda pair, i: (pair, i, 0)) if tiled else (lambda pair, i: (pair, 0, 0)))

    def tile(d):
        return pl.BlockSpec((TM, d), lambda pair, i: (i, 0))

    return _call(
        "attn_bwd", body, (B_HEADS // hps, t // TM),
        [pl.BlockSpec((t, D_MODEL), lambda pair, i: (0, 0), pipeline_mode=pl.Buffered(1)),
         per_pair(TM, QK_NOPE + QK_ROPE, True), per_pair(t, QK_NOPE + QK_ROPE, False), per_pair(t, V_HEAD, False),
         per_pair(V_HEAD, D_MODEL, False), tile(half), tile(half)],
        [per_pair(TM, QK_NOPE + QK_ROPE, True), per_pair(t, QK_NOPE + QK_ROPE, False), per_pair(t, V_HEAD, False)],
        [_sds((B_HEADS, t, QK_NOPE + QK_ROPE), BF16), _sds((B_HEADS, t, QK_NOPE + QK_ROPE), F32),
         _sds((B_HEADS, t, V_HEAD), F32)],
        (dh, q, k, v, w_o, cos, sin), after=after)


def _kvq_bwd(h, dh, ckv, cqpre, dq, dk, dv, cos, sin, kvq_w, after=()):
    t = h.shape[0]
    tm = TM
    half, last = QK_ROPE // 2, t // tm - 1
    grad_shapes = [(D_MODEL, Q_LORA), (B_HEADS, Q_LORA, QK_NOPE + QK_ROPE), (D_MODEL, KV_LORA + QK_ROPE),
                   (B_HEADS, KV_LORA, QK_NOPE + V_HEAD)]

    def body(h_ref, dh_ref, ckv_ref, cqpre_ref, dq_ref, dk_ref, dv_ref, cos_ref, sin_ref,
             srcg_ref, wkva_ref, kvag_ref, wkvb_ref, mixg_ref, wqa_ref, qg_ref, wqb_ref,
             dhi_ref, dmixg_ref, dsrcg_ref, dqg_ref, dkvag_ref, gqa_ref, gqb_ref, gkva_ref, gkvb_ref,
             aqa, aqb, akva, akvb):
        @pl.when(pl.program_id(0) == 0)
        def _():
            for acc in (aqa, aqb, akva, akvb):
                acc[...] = jnp.zeros_like(acc)

        hv = h_ref[...]
        rstd = lax.rsqrt(jnp.mean(hv * hv, axis=-1, keepdims=True) + EPS)
        xhat = hv * rstd
        mixg, srcg, qg, kvag = mixg_ref[...], srcg_ref[...], qg_ref[...], kvag_ref[...]
        cq, cqhat, crstd = _rms_fwd(cqpre_ref[...], qg)
        cqb = cq.astype(BF16)
        dcq = jnp.zeros((tm, Q_LORA), F32)
        for hd in range(B_HEADS):
            dcq = dcq + _dot_nt(dq_ref[hd], wqb_ref[hd])
            aqb[hd] += _dot_tn(cqb, dq_ref[hd])
        dcqpre, dqg = _rms_bwd(dcq, cqhat, crstd, qg)
        dcqpre_b = dcqpre.astype(BF16)
        aqa[...] += _dot_tn((xhat * mixg).astype(BF16), dcqpre_b)
        dxq, dmixg = _rms_bwd(_dot_nt(dcqpre_b, wqa_ref[...]), xhat, rstd, mixg)
        ckv = ckv_ref[...]
        c, chat, krstd = _rms_fwd(ckv[:, :KV_LORA], kvag)
        cb = c.astype(BF16)
        dc = jnp.zeros((tm, KV_LORA), F32)
        dkpe = jnp.zeros((tm, QK_ROPE), F32)
        for hd in range(B_HEADS):
            dkv = jnp.concatenate([dk_ref[hd, :, 0:QK_NOPE], dv_ref[hd]], axis=-1).astype(BF16)
            akvb[hd] += _dot_tn(cb, dkv)
            dc = dc + _dot_nt(dkv, wkvb_ref[hd])
            dkpe = dkpe + dk_ref[hd, :, QK_NOPE:]
        dlat, dkvag = _rms_bwd(dc, chat, krstd, kvag)
        dpe = _rope(dkpe, cos_ref[...], -sin_ref[...])
        dckv_b = jnp.concatenate([dlat, dpe], axis=-1).astype(BF16)
        akva[...] += _dot_tn((xhat * srcg).astype(BF16), dckv_b)
        dxk, dsrcg = _rms_bwd(_dot_nt(dckv_b, wkva_ref[...]), xhat, rstd, srcg)
        dhi_ref[...] = dh_ref[...] + dxq + dxk
        _acc(dmixg_ref, dmixg)
        _acc(dsrcg_ref, dsrcg)
        _acc(dqg_ref, dqg)
        _acc(dkvag_ref, dkvag)

        @pl.when(pl.program_id(0) == last)
        def _():
            for out, acc in ((gqa_ref, aqa), (gqb_ref, aqb), (gkva_ref, akva), (gkvb_ref, akvb)):
                out[...] = acc[...].astype(BF16)

    return _call(
        "kvq_bwd", body, (t // tm,),
        [_row(D_MODEL, tm), _row(D_MODEL, tm), _row(KV_LORA + QK_ROPE, tm), _row(Q_LORA, tm),
         _heads(QK_NOPE + QK_ROPE, tm), _heads(QK_NOPE + QK_ROPE, tm), _heads(V_HEAD, tm), _row(half, tm),
         _row(half, tm), *KVQ_W_SPECS],
        [_row(D_MODEL, tm), _const((1, D_MODEL)), _const((1, D_MODEL)), _const((1, Q_LORA)), _const((1, KV_LORA))]
        + [_const(s) for s in grad_shapes],
        [_sds((t, D_MODEL), F32), _sds((1, D_MODEL), F32), _sds((1, D_MODEL), F32), _sds((1, Q_LORA), F32),
         _sds((1, KV_LORA), F32)] + [_sds(s, BF16) for s in grad_shapes],
        (h, dh, ckv, cqpre, dq, dk, dv, cos, sin, *kvq_w), scratch=[pltpu.VMEM(s, F32) for s in grad_shapes],
        after=after)


def _a_mix_bwd(x, z, dh, g, w_in, ln_g, ln_b, w_s, b_st, w_out, after=()):
    t = x.shape[0]
    tm = TM_GATE
    nblk = tm // GMLP_BLOCK

    def body(x_ref, z_ref, dh_ref, g_ref, win_ref, lng_ref, lnb_ref, ws_ref, bst_ref, wout_ref,
             dx_ref, hn_ref, dz_ref, dg_ref, dlng_ref, dlnb_ref, dws_ref, dbs_ref, dvn_scr, gelu_grad_v):
        @pl.when(pl.program_id(0) == 0)
        def _():
            dws_ref[...] = jnp.zeros_like(dws_ref)
            dbs_ref[...] = jnp.zeros_like(dbs_ref)

        gv, lng = g_ref[...], lng_ref[...]
        y, xhat, rstd = _rms_fwd(x_ref[...], gv)
        hn_ref[...] = y.astype(BF16)
        dhv = dh_ref[...]
        dgated = _dot_nt(dhv.astype(BF16), wout_ref[...])
        u, gelu_grad_u = _gelu_and_grad(z_ref[:, :GATE_DIM])
        v, gelu_grad_v[...] = _gelu_and_grad(z_ref[:, GATE_DIM:])
        vn, vhat, lrstd = _ln_fwd(v, lng, lnb_ref[...])
        vb = vn.astype(BF16)
        mask = _gate_mask()
        for gi in range(A_GROUPS):
            wm = jnp.where(mask, ws_ref[gi], 0.0).astype(BF16)
            bias = bst_ref[:, gi:gi + 1]
            cs = slice(gi * A_GROUP_DIM, (gi + 1) * A_GROUP_DIM)
            dws = jnp.zeros((GMLP_BLOCK, GMLP_BLOCK), F32)
            dbs = jnp.zeros((GMLP_BLOCK, 1), F32)
            for n in range(nblk):
                rs = slice(n * GMLP_BLOCK, (n + 1) * GMLP_BLOCK)
                sv = _dot(wm, vb[rs, cs]) + bias
                dz_ref[rs, cs] = (dgated[rs, cs] * sv * gelu_grad_u[rs, cs]).astype(BF16)
                dsv = dgated[rs, cs] * u[rs, cs]
                dsvb = dsv.astype(BF16)
                dws = dws + _dot_nt(dsvb, vb[rs, cs])
                dbs = dbs + jnp.sum(dsv, axis=-1, keepdims=True)
                dvn_scr[rs, cs] = _dot_tn(wm, dsvb)
            dws_ref[gi] += jnp.where(mask, dws, 0.0)
            dbs_ref[gi] += dbs
        dvn = dvn_scr[...]
        dvhat = dvn * lng
        dv = lrstd * (dvhat - jnp.mean(dvhat, axis=-1, keepdims=True)
                      - vhat * jnp.mean(dvhat * vhat, axis=-1, keepdims=True))
        dz_ref[:, GATE_DIM:] = (dv * gelu_grad_v[...]).astype(BF16)
        dhn = jnp.zeros((tm, D_MODEL), F32)
        for d in range(N_DEV):
            dhn = dhn + _dot_nt(dz_ref[:, d * FF_SLOT:(d + 1) * FF_SLOT], win_ref[d])
        dx, dg = _rms_bwd(dhn, xhat, rstd, gv)
        dx_ref[...] = dhv + dx
        _acc(dg_ref, dg)
        _acc(dlng_ref, jnp.sum(dvn * vhat, axis=0, keepdims=True))
        _acc(dlnb_ref, jnp.sum(dvn, axis=0, keepdims=True))

    return _call(
        "a_mix_bwd", body, (t // tm,),
        [_row(D_MODEL, tm), _row(2 * GATE_DIM, tm), _row(D_MODEL, tm), _res((1, D_MODEL)),
         _res((N_DEV, D_MODEL, FF_SLOT)), _res((1, GATE_DIM)), _res((1, GATE_DIM)),
         _res((A_GROUPS, GMLP_BLOCK, GMLP_BLOCK)), _res((GMLP_BLOCK, A_GROUPS)), _res((GATE_DIM, D_MODEL))],
        [_row(D_MODEL, tm), _row(D_MODEL, tm), _row(2 * GATE_DIM, tm),
         _const((1, D_MODEL)), _const((1, GATE_DIM)), _const((1, GATE_DIM)),
         _const((A_GROUPS, GMLP_BLOCK, GMLP_BLOCK)), _const((A_GROUPS, GMLP_BLOCK, 1))],
        [_sds((t, D_MODEL), F32), _sds((t, D_MODEL), BF16),
         _sds((t, 2 * GATE_DIM), BF16), _sds((1, D_MODEL), F32), _sds((1, GATE_DIM), F32),
         _sds((1, GATE_DIM), F32), _sds((A_GROUPS, GMLP_BLOCK, GMLP_BLOCK), F32),
         _sds((A_GROUPS, GMLP_BLOCK, 1), F32)],
        (x, z, dh, g, w_in, ln_g, ln_b, w_s, b_st, w_out),
        scratch=[pltpu.VMEM((tm, GATE_DIM), F32), pltpu.VMEM((tm, GATE_DIM), F32)], after=after)


def _wgrad(name, a, b, a_spec, b_spec, m, n, after=(), slots=1):
    def body(a_ref, b_ref, o_ref):
        res = _dot_tn(a_ref[...].astype(BF16), b_ref[...].astype(BF16)).astype(BF16)
        for s in range(slots):
            o_ref[s] = res[s * m:(s + 1) * m] if res.shape[0] == slots * m else res[:, s * n:(s + 1) * n]

    return _call(name, body, (N_DEV // slots,), [a_spec, b_spec],
                 [pl.BlockSpec((slots, m, n), lambda d: (d, 0, 0))], [_sds((N_DEV, m, n), BF16)], (a, b),
                 after=after)[0]


def _full(t, d):
    return pl.BlockSpec((t, d), lambda i: (0, 0), pipeline_mode=pl.Buffered(1))


def _cols(t, d):
    return pl.BlockSpec((t, d), lambda i: (0, i))


def _head(t, d):
    return pl.BlockSpec((None, t, d), lambda i: (i, 0, 0))


def _local_step(x, pos, target, inv_freq, wg, sm, shards=None):
    t = x.shape[0]
    wg = dict(wg)
    dist = shards is not None
    mix_g = [sm["norm_mix_g"][l:l + 1] for l in range(2)]
    mlp_g = [sm["norm_mlp_g"][l:l + 1] for l in range(2)]

    ids = iter(range(2, 2 + 9))

    def gather(names):
        if dist:
            got = _by_sequencer("gather_" + names[0], _gather_comm([shards[k] for k in names]),
                                SIBLING_AND_NEIGHBOURS, next(ids))
            wg.update(zip(names, got))

    def send(name, names):
        if dist:
            comm = _exchange_comm(grads=[g[k] for k in names])
            g.update(zip(names, _by_sequencer("exchange_" + name, comm, EVERYONE, next(ids))))

    def send_sums(name, names, meanwhile):
        if not dist:
            meanwhile()
            return ()
        grads = [g[k] for k in names]
        landed = _by_sequencer("pair_exchange_" + name, _pair_exchange_comm(grads), (1,), next(ids))
        sums = _pair_add("pair_add_" + name, grads, landed, after=meanwhile())
        g.update(zip(names, _by_sequencer("exchange_" + name, _chip_exchange_comm(sums), OTHER_CHIPS, next(ids))))
        return sums

    def a_args():
        return (wg["a_w_in"], wg["a_ln_v_g"], wg["a_ln_v_b"], sm["a_w_s"], sm["a_b_st"], wg["a_w_out"])

    def kvq_w():
        return (sm["kv_src_norm_g"], wg["kv_w_a"], sm["kv_a_norm_g"], wg["kv_w_b"], mix_g[1], wg["b_w_q_a"],
                sm["b_q_norm_g"], wg["b_w_q_b"])

    gather(("mlp_w1_0", "mlp_w2_0"))
    h1, z, gated = _a_mix_fwd(x, mix_g[0], *a_args())
    gather(("kv_w_a", "kv_w_b", "b_w_q_a", "b_w_q_b", "b_w_o"))
    h2, a0 = _mlp_fwd(h1, mlp_g[0], wg["mlp_w1_0"], wg["mlp_w2_0"])
    if dist:
        wg["b_w_q_a"] = wg["b_w_q_a"].reshape(D_MODEL, Q_LORA)
        wg["kv_w_a"] = wg["kv_w_a"].reshape(D_MODEL, KV_LORA + QK_ROPE)
    gather(("mlp_w1_1", "mlp_w2_1"))
    ckv, k, v, cqpre, q, cos, sin = _kvq_fwd(h2, pos, inv_freq, kvq_w())
    h3, att = _attn_fwd(h2, q, k, v, wg["b_w_o"])
    a1, loss, dh4, d_final_g = _mlp_fwd_loss(h3, mlp_g[1], wg["mlp_w1_1"], wg["mlp_w2_1"], sm["final_norm_g"], target)

    g = {}
    dh3, d_mlp_g1, hn, f, da, dh3_b = _mlp_bwd(h3, a1, dh4, mlp_g[1], wg["mlp_w1_1"], wg["mlp_w2_1"], 1)
    dq, dk, dv = _attn_bwd(dh3_b, q, k, v, wg["b_w_o"], cos, sin)
    g["mlp_w1_1"] = _wgrad("wgrad_w1_1", hn, da, _full(t, D_MODEL), _cols(t, 2 * FF_SLOT), D_MODEL, FF_SLOT,
                           after=[dq], slots=2)
    g["mlp_w2_1"] = _wgrad("wgrad_w2_1", f, dh4, _cols(t, 2 * FF_SLOT), _full(t, D_MODEL), FF_SLOT, D_MODEL, slots=2)

    def wgrad_w_o():
        g["b_w_o"] = _wgrad("wgrad_w_o", att, dh3_b, _head(t, V_HEAD), _full(t, D_MODEL), V_HEAD, D_MODEL)
        return [g["b_w_o"]]

    sums = send_sums("mlp_1", ("mlp_w1_1", "mlp_w2_1"), wgrad_w_o)
    dh2, d_mix_g1, d_src_g, d_q_g, d_kv_a_g, g_q_a, g["b_w_q_b"], g_kv_a, g["kv_w_b"] = _kvq_bwd(
        h2, dh3, ckv, cqpre, dq, dk, dv, cos, sin, kvq_w(), after=sums)
    g["b_w_q_a"] = g_q_a.reshape(N_DEV, D_MODEL // N_DEV, Q_LORA)
    g["kv_w_a"] = g_kv_a.reshape(N_DEV, D_MODEL // N_DEV, KV_LORA + QK_ROPE)
    qkv = ("b_w_q_a", "b_w_q_b", "kv_w_a", "kv_w_b")
    landed = [g[k] for k in qkv]
    send("qkv", qkv)
    dh1, d_mlp_g0, hn, f, da, dh1_b = _mlp_bwd(h1, a0, dh2, mlp_g[0], wg["mlp_w1_0"], wg["mlp_w2_0"], 0,
                                               after=landed if dist else ())
    landed = [g["mlp_w1_1"], g["mlp_w2_1"]] if dist else ()
    g["mlp_w1_0"] = _wgrad("wgrad_w1_0", hn, da, _full(t, D_MODEL), _cols(t, 2 * FF_SLOT), D_MODEL, FF_SLOT,
                           after=landed, slots=2)
    g["mlp_w2_0"] = _wgrad("wgrad_w2_0", f, dh2, _cols(t, 2 * FF_SLOT), _full(t, D_MODEL), FF_SLOT, D_MODEL, slots=2)

    def wgrad_a_w_out():
        g["a_w_out"] = _wgrad("wgrad_a_w_out", gated, dh1_b, _cols(t, GATE_DIM // N_DEV), _full(t, D_MODEL),
                              GATE_DIM // N_DEV, D_MODEL)
        return [g["a_w_out"]] + [g[k] for k in qkv]

    sums = send_sums("mlp_0", ("mlp_w1_0", "mlp_w2_0", "b_w_o"), wgrad_a_w_out)
    if dist:
        sums = _pair_reduce("pair_reduce_a_w_out", [g["a_w_out"]], after=sums)
    dx, hn, dz, d_mix_g0, d_ln_g, d_ln_b, d_ws, d_bs = _a_mix_bwd(x, z, dh1, mix_g[0], *a_args(), after=sums)
    small = {
        "norm_mix_g": jnp.concatenate([d_mix_g0, d_mix_g1], axis=0),
        "norm_mlp_g": jnp.concatenate([d_mlp_g0, d_mlp_g1], axis=0),
        "a_ln_v_g": d_ln_g.reshape(N_DEV, GATE_DIM // N_DEV),
        "a_ln_v_b": d_ln_b.reshape(N_DEV, GATE_DIM // N_DEV),
        "a_w_s": d_ws.astype(BF16) if dist else d_ws,
        "a_b_s": d_bs.reshape(A_GROUPS, GMLP_BLOCK),
        "b_q_norm_g": d_q_g,
        "kv_src_norm_g": d_src_g,
        "kv_a_norm_g": d_kv_a_g,
        "final_norm_g": d_final_g,
    }
    if dist:
        parts = [small[k].reshape((1,) + small[k].shape) for k in SMALL] + [loss.reshape(1, 1, 1)]
        comm = _together(_chip_exchange_comm(sums), _gather_comm(parts))
        g["a_w_out"], *got = _by_sequencer("exchange_a_w_out", comm, EVERYONE, next(ids))
        small, loss = dict(zip(SMALL, got)), got[-1]
    g["a_w_in"] = _wgrad("wgrad_a_w_in", hn, dz, _full(t, D_MODEL), _cols(t, 2 * FF_SLOT), D_MODEL, FF_SLOT, slots=2)
    return loss, dx, g, small


def _adamw(w, g, m, v):
    m = ADAM_B1 * m + (1.0 - ADAM_B1) * g
    v = ADAM_B2 * v + (1.0 - ADAM_B2) * (g * g)
    m_hat = m / (1.0 - ADAM_B1 ** ADAM_STEP)
    v_hat = v / (1.0 - ADAM_B2 ** ADAM_STEP)
    return -ADAM_LR * (m_hat / (jnp.sqrt(v_hat) + ADAM_EPS) + ADAM_WD * w), m, v


def _sum_in_device_order(r_ref):
    g = r_ref[0].astype(F32)
    for j in range(1, r_ref.shape[0]):
        g = g + r_ref[j].astype(F32)
    return g


def _adamw_sharded(name, recvs, w, m, v, swapped=False):
    layers, r, c = w.shape[0], *recvs[0][0].shape[1:]
    tr = r if swapped else math.gcd(r, 512)
    flat = [a for per_layer in recvs for a in per_layer]

    def body(*refs):
        r_refs, (w_ref, m_ref, v_ref) = refs[:len(flat)], refs[len(flat):len(flat) + 3]
        g_ref, d_ref, nm_ref, nv_ref = refs[-4:]
        layer = pl.program_id(0)
        g, pos = None, 0
        for li, per_layer in enumerate(recvs):
            total = None
            for ref in r_refs[pos:pos + len(per_layer)]:
                part = _sum_in_device_order(ref)
                total = part if total is None else total + part
            pos += len(per_layer)
            g = total if g is None else jnp.where(layer == li, total, g)
        if swapped:
            g = g.T
        g_ref[...] = g
        d_ref[...], nm_ref[...], nv_ref[...] = _adamw(w_ref[...], g, m_ref[...], v_ref[...])

    blk = pl.BlockSpec((None, tr, c), lambda l, i: (l, i, 0))
    if swapped:
        blk = pl.BlockSpec((None, c, r), lambda l, i: (l, 0, 0))
    return _call(name, body, (layers, r // tr),
                 [pl.BlockSpec((a.shape[0], tr, c), lambda l, i: (0, i, 0)) for a in flat] + [blk] * 3,
                 [blk] * 4, [_sds(w.shape, F32)] * 4, (*flat, w, m, v))


def _adamw_small(recvs, ws, ms, vs, own_row, losses):
    n = len(recvs)

    def body(*refs):
        r_refs, w_refs, m_refs, v_refs = (refs[i * n:(i + 1) * n] for i in range(4))
        outs, scr = refs[4 * n + 1:8 * n + 2], refs[8 * n + 2:]
        outs[-1][...] = _sum_in_device_order(refs[4 * n])
        me = _my_place()[3]
        for a in range(n):
            g = _sum_in_device_order(r_refs[a])
            if own_row[a]:
                scr[0][...] = g
                g = scr[0][pl.ds(me, 1), :]
            g_ref, d_ref, nm_ref, nv_ref = outs[4 * a:4 * a + 4]
            g_ref[...] = g
            d_ref[...], nm_ref[...], nv_ref[...] = _adamw(w_refs[a][...], g, m_refs[a][...], v_refs[a][...])

    out_shape = []
    for w in ws:
        out_shape += [_sds(w.shape, F32)] * 4
    return pl.pallas_call(
        body, name="adamw_small", in_specs=[VMEM] * (4 * n + 1), out_specs=[VMEM] * (4 * n + 1),
        out_shape=out_shape + [_sds((1, 1), F32)], scratch_shapes=[pltpu.VMEM((N_DEV, GATE_DIM // N_DEV), F32)],
    )(*recvs, *ws, *ms, *vs, losses)


BIG = ("a_w_in", "a_w_out", "b_w_q_a", "b_w_q_b", "b_w_o", "kv_w_a", "kv_w_b", "mlp_w1", "mlp_w2")
SMALL = ("norm_mix_g", "norm_mlp_g", "a_ln_v_g", "a_ln_v_b", "a_w_s", "a_b_s", "b_q_norm_g", "kv_src_norm_g",
         "kv_a_norm_g", "final_norm_g")
WEIGHTS = ("norm_mix_g", "norm_mlp_g", "a_w_in", "a_ln_v_g", "a_ln_v_b", "a_w_s", "a_b_s", "a_w_out", "b_w_q_a",
           "b_q_norm_g", "b_w_q_b", "b_w_o", "kv_src_norm_g", "kv_w_a", "kv_a_norm_g", "kv_w_b", "mlp_w1", "mlp_w2",
           "final_norm_g")


def _two_d(name, a):
    if name in ("a_w_s", "a_b_s"):
        return a.reshape(a.shape[1:])
    return a.reshape(1, -1) if a.ndim == 1 else a


def _three_d(a):
    return a if a.ndim == 3 else a.reshape((1,) + a.shape)


SWAPPED = ("b_w_q_b", "kv_w_a")


def _swapped(a):
    return jnp.swapaxes(_three_d(a), 1, 2)


def kernel(x, positions, norm_mix_g, norm_mlp_g, a_w_in, a_ln_v_g, a_ln_v_b, a_w_s, a_b_s, a_w_out, b_w_q_a, b_q_norm_g, b_w_q_b, b_w_o, kv_src_norm_g, kv_w_a, kv_a_norm_g, kv_w_b, mlp_w1, mlp_w2, final_norm_g, loss_target, m_norm_mix_g, m_norm_mlp_g, m_a_w_in, m_a_ln_v_g, m_a_ln_v_b, m_a_w_s, m_a_b_s, m_a_w_out, m_b_w_q_a, m_b_q_norm_g, m_b_w_q_b, m_b_w_o, m_kv_src_norm_g, m_kv_w_a, m_kv_a_norm_g, m_kv_w_b, m_mlp_w1, m_mlp_w2, m_final_norm_g, v_norm_mix_g, v_norm_mlp_g, v_a_w_in, v_a_ln_v_g, v_a_ln_v_b, v_a_w_s, v_a_b_s, v_a_w_out, v_b_w_q_a, v_b_q_norm_g, v_b_w_q_b, v_b_w_o, v_kv_src_norm_g, v_kv_w_a, v_kv_a_norm_g, v_kv_w_b, v_mlp_w1, v_mlp_w2, v_final_norm_g):
    w = dict(norm_mix_g=norm_mix_g, norm_mlp_g=norm_mlp_g, a_w_in=a_w_in, a_ln_v_g=a_ln_v_g, a_ln_v_b=a_ln_v_b,
             a_w_s=a_w_s, a_b_s=a_b_s, a_w_out=a_w_out, b_w_q_a=b_w_q_a, b_q_norm_g=b_q_norm_g, b_w_q_b=b_w_q_b,
             b_w_o=b_w_o, kv_src_norm_g=kv_src_norm_g, kv_w_a=kv_w_a, kv_a_norm_g=kv_a_norm_g, kv_w_b=kv_w_b,
             mlp_w1=mlp_w1, mlp_w2=mlp_w2, final_norm_g=final_norm_g)
    m = dict(norm_mix_g=m_norm_mix_g, norm_mlp_g=m_norm_mlp_g, a_w_in=m_a_w_in, a_ln_v_g=m_a_ln_v_g,
             a_ln_v_b=m_a_ln_v_b, a_w_s=m_a_w_s, a_b_s=m_a_b_s, a_w_out=m_a_w_out, b_w_q_a=m_b_w_q_a,
             b_q_norm_g=m_b_q_norm_g, b_w_q_b=m_b_w_q_b, b_w_o=m_b_w_o, kv_src_norm_g=m_kv_src_norm_g,
             kv_w_a=m_kv_w_a, kv_a_norm_g=m_kv_a_norm_g, kv_w_b=m_kv_w_b, mlp_w1=m_mlp_w1, mlp_w2=m_mlp_w2,
             final_norm_g=m_final_norm_g)
    v = dict(norm_mix_g=v_norm_mix_g, norm_mlp_g=v_norm_mlp_g, a_w_in=v_a_w_in, a_ln_v_g=v_a_ln_v_g,
             a_ln_v_b=v_a_ln_v_b, a_w_s=v_a_w_s, a_b_s=v_a_b_s, a_w_out=v_a_w_out, b_w_q_a=v_b_w_q_a,
             b_q_norm_g=v_b_q_norm_g, b_w_q_b=v_b_w_q_b, b_w_o=v_b_w_o, kv_src_norm_g=v_kv_src_norm_g,
             kv_w_a=v_kv_w_a, kv_a_norm_g=v_kv_a_norm_g, kv_w_b=v_kv_w_b, mlp_w1=v_mlp_w1, mlp_w2=v_mlp_w2,
             final_norm_g=v_final_norm_g)
    t = x.shape[1]

    first = ("a_w_in", "a_w_out", "a_ln_v_g", "a_ln_v_b")
    later = ("mlp_w1", "mlp_w2", "kv_w_a", "kv_w_b", "b_w_q_a", "b_w_q_b", "b_w_o")
    later_blocks = ("mlp_w1_0", "mlp_w1_1", "mlp_w2_0", "mlp_w2_1") + later[2:]
    got, casts = _gather_first([_three_d(w[k]) if k in BIG else w[k] for k in first],
                               [_swapped(w[k]) if k in SWAPPED else _three_d(w[k]) for k in later],
                               [k in SWAPPED for k in later])
    wg = dict(zip(first, got))
    wg["a_w_out"] = wg["a_w_out"].reshape(GATE_DIM, D_MODEL)
    wg["a_ln_v_g"] = wg["a_ln_v_g"].reshape(1, GATE_DIM)
    wg["a_ln_v_b"] = wg["a_ln_v_b"].reshape(1, GATE_DIM)
    shards = dict(zip(later_blocks, casts))

    sm = {k: _two_d(k, w[k]) for k in SMALL if k not in ("a_ln_v_g", "a_ln_v_b")}
    sm["a_b_st"] = sm["a_b_s"].T
    inv_freq = (ROPE_THETA ** (-jnp.arange(0, QK_ROPE, 2, dtype=F32) / QK_ROPE)).reshape(1, QK_ROPE // 2)

    losses, dx, g, small = _local_step(x[0], positions.reshape(t, 1), loss_target[0], inv_freq, wg, sm, shards)

    sums = _pair_reduce("pair_reduce_a_w_in", [g["a_w_in"]], after=[g["mlp_w1_0"], g["mlp_w2_0"]])
    g["a_w_in"], = _by_sequencer("exchange_last", _chip_exchange_comm(sums), OTHER_CHIPS, collective_id=1)

    out = {}
    for k in BIG:
        recvs = [[g[k + "_0"]], [g[k + "_1"]]] if k.startswith("mlp") else [[g[k]]]
        view = _swapped if k in SWAPPED else _three_d
        res = _adamw_sharded("adamw_" + k, recvs, view(w[k]), view(m[k]), view(v[k]), swapped=k in SWAPPED)
        out[k] = [view(o).reshape(w[k].shape) for o in res]
    own_row = [k in ("a_ln_v_g", "a_ln_v_b") for k in SMALL]
    res = _adamw_small([small[k] for k in SMALL], [_two_d(k, w[k]) for k in SMALL], [_two_d(k, m[k]) for k in SMALL],
                       [_two_d(k, v[k]) for k in SMALL], own_row, losses)
    for i, k in enumerate(SMALL):
        out[k] = [o.reshape(w[k].shape) for o in res[4 * i:4 * i + 4]]

    return (res[-1].reshape(()), dx.reshape(x.shape), *[out[k][0] for k in WEIGHTS], *[out[k][1] for k in WEIGHTS],
            *[out[k][2] for k in WEIGHTS], *[out[k][3] for k in WEIGHTS])
```

```python
import math

import jax
import jax.numpy as jnp
from jax import lax
from jax.experimental import pallas as pl
from jax.experimental.pallas import tpu as pltpu
from jax.experimental.pallas import tpu_sc as plsc

F32, BF16 = jnp.float32, jnp.bfloat16
MESH = pl.DeviceIdType.MESH
ANY = pl.BlockSpec(memory_space=pl.ANY)
VMEM = pl.BlockSpec(memory_space=pltpu.VMEM)

N_DEV = 8
D_MODEL = 1024
CHUNK = 64
GMLP_BLOCK = 128
GATE_DIM = 2048
A_GROUPS = 8
A_GROUP_DIM = GATE_DIM // A_GROUPS
B_HEADS = 8
QK_NOPE, QK_ROPE, V_HEAD = 128, 64, 128
Q_LORA, KV_LORA = 384, 256
ROPE_THETA = 10000.0
D_FF = 4096
FF_SLOT = D_FF // N_DEV
EPS = 1e-6
ATT_SCALE = (QK_NOPE + QK_ROPE) ** -0.5

ADAM_LR, ADAM_B1, ADAM_B2, ADAM_EPS, ADAM_WD, ADAM_STEP = 0.001, 0.9, 0.999, 1e-08, 0.01, 10

TM = 256
TM_GATE = 256
TM_MLP_FWD = 512
TM_KVQ = 512
VMEM_LIMIT = 56 * 1024 * 1024
INV_SQRT2 = 1.0 / math.sqrt(2.0)
INV_SQRT_2PI = 1.0 / math.sqrt(2.0 * math.pi)
LOG2_E = 1.0 / math.log(2.0)
HEADS_PER_STEP = 2


def _dot(a, b):
    return jnp.dot(a, b, preferred_element_type=F32)


def _dot_nt(a, b):
    return lax.dot_general(a, b, (((1,), (1,)), ((), ())), preferred_element_type=F32)


def _dot_tn(a, b):
    return lax.dot_general(a, b, (((0,), (0,)), ((), ())), preferred_element_type=F32)


def _rms_fwd(x, g):
    rstd = lax.rsqrt(jnp.mean(x * x, axis=-1, keepdims=True) + EPS)
    xhat = x * rstd
    return xhat * g, xhat, rstd


def _rms_bwd(dy, xhat, rstd, g):
    dxhat = dy * g
    dx = rstd * (dxhat - xhat * jnp.mean(dxhat * xhat, axis=-1, keepdims=True))
    return dx, jnp.sum(dy * xhat, axis=0, keepdims=True)


def _ln_fwd(v, g, b):
    mu = jnp.mean(v, axis=-1, keepdims=True)
    vc = v - mu
    rstd = lax.rsqrt(jnp.mean(vc * vc, axis=-1, keepdims=True) + EPS)
    vhat = vc * rstd
    return vhat * g + b, vhat, rstd


def _gelu(x):
    return 0.5 * x * (1.0 + lax.erf(x * INV_SQRT2))


def _gelu_and_grad(x):
    cdf = 0.5 * (1.0 + lax.erf(x * INV_SQRT2))
    return x * cdf, cdf + x * jnp.exp(-0.5 * x * x) * INV_SQRT_2PI


def _rope(x, cos, sin):
    x1, x2 = x[:, :QK_ROPE // 2], x[:, QK_ROPE // 2:]
    return jnp.concatenate([x1 * cos - x2 * sin, x2 * cos + x1 * sin], axis=-1)


def _gate_mask():
    row = lax.broadcasted_iota(jnp.int32, (GMLP_BLOCK, GMLP_BLOCK), 0)
    col = lax.broadcasted_iota(jnp.int32, (GMLP_BLOCK, GMLP_BLOCK), 1)
    return (col < CHUNK) | (row >= CHUNK)


def _att_mask(q0, tq, t):
    q = q0 + lax.broadcasted_iota(jnp.int32, (tq, t), 0)
    k = lax.broadcasted_iota(jnp.int32, (tq, t), 1)
    return jnp.right_shift(k, 6) <= jnp.right_shift(q, 6)


def _res(shape, imap=None):
    zeros = (0,) * len(shape)
    return pl.BlockSpec(shape, imap or (lambda i: zeros), pipeline_mode=pl.Buffered(1))


def _const(shape):
    zeros = (0,) * len(shape)
    return pl.BlockSpec(shape, lambda i: zeros)


def _row(d, tm=TM):
    return pl.BlockSpec((tm, d), lambda i: (i, 0))


def _heads(d, tm=TM):
    return pl.BlockSpec((B_HEADS, tm, d), lambda i: (0, i, 0))


def _sds(shape, dt):
    return jax.ShapeDtypeStruct(shape, dt)


def _acc(ref, val):
    @pl.when(pl.program_id(0) == 0)
    def _():
        ref[...] = jnp.zeros_like(ref)
    ref[...] += val


def _my_place():
    x, y, c = lax.axis_index("x"), lax.axis_index("y"), lax.axis_index("c")
    return x, y, c, 4 * x + 2 * y + c


def _peer(x, y, c, k):
    px = 1 - x if k & 4 else x
    py = 1 - y if k & 2 else y
    pc = 1 - c if k & 1 else c
    return (px, py, pc), 4 * px + 2 * py + pc


CHIPS = (2, 4, 6)


def _splits(ref):
    return len(ref.shape) >= 3 and ref.shape[1] % 32 == 0


def _piece(ref, block, half=None):
    if half is None or not _splits(ref):
        return ref.at[pl.ds(block, 1)]
    rows = ref.shape[1] // 2
    return ref.at[pl.ds(block, 1), pl.ds(half * rows, rows)]


def _gather_copy(sems, a, k, piece, to, src=None):
    return pltpu.make_async_remote_copy(
        src_ref=piece if src is None else src, dst_ref=piece, send_sem=sems[0].at[a, k], recv_sem=sems[1].at[a, k],
        device_id=to, device_id_type=MESH)


def _gather_start(srcs, outs, sems, only=None):
    x, y, c, me = _my_place()
    for a in range(len(srcs)) if only is None else (only,):
        mine = _piece(outs[a], me)
        pltpu.make_async_copy(srcs[a], mine, sems[2].at[a]).start()
        for k, rel in enumerate((1, 4, 2)):
            _gather_copy(sems, a, k, mine, _peer(x, y, c, rel)[0], src=srcs[a]).start()


def _gather_relay(srcs, outs, sems):
    x, y, c, _ = _my_place()
    sib = _peer(x, y, c, 1)[0]
    (xn, xn_i), (yn, yn_i) = _peer(x, y, c, 4), _peer(x, y, c, 2)
    for a in range(len(srcs)):
        out = outs[a]
        _gather_copy(sems, a, 1, _piece(out, xn_i), xn).wait_recv()
        _gather_copy(sems, a, 3, _piece(out, xn_i, 0), yn).start()
        _gather_copy(sems, a, 5, _piece(out, xn_i), sib).start()
        _gather_copy(sems, a, 2, _piece(out, yn_i), yn).wait_recv()
        if _splits(out):
            _gather_copy(sems, a, 4, _piece(out, yn_i, 1), xn).start()
        _gather_copy(sems, a, 6, _piece(out, yn_i), sib).start()


def _gather_finish(srcs, outs, sems):
    x, y, c, me = _my_place()
    sib = _peer(x, y, c, 1)[0]
    xn, yn, dg_i = _peer(x, y, c, 4)[0], _peer(x, y, c, 2)[0], _peer(x, y, c, 6)[1]
    n = len(srcs)
    for a in range(n):
        out = outs[a]
        _gather_copy(sems, a, 3, _piece(out, dg_i, 0), yn).wait_recv()
        _gather_copy(sems, a, 7, _piece(out, dg_i, 0), sib).start()
        if _splits(out):
            _gather_copy(sems, a, 4, _piece(out, dg_i, 1), xn).wait_recv()
            _gather_copy(sems, a, 8, _piece(out, dg_i, 1), sib).start()
    for a in range(n):
        out = outs[a]
        whole, half = _piece(out, me), _piece(out, me, 0)
        for k in (0, 5, 6):
            _gather_copy(sems, a, k, whole, sib).wait_recv()
        for k in (7, 8) if _splits(out) else (7,):
            _gather_copy(sems, a, k, half, sib).wait_recv()
        for k in (0, 1, 2):
            _gather_copy(sems, a, k, whole, sib, src=srcs[a]).wait_send()
        for k in (5, 6):
            _gather_copy(sems, a, k, whole, sib).wait_send()
        for k in (3, 4, 7, 8) if _splits(out) else (3, 7):
            _gather_copy(sems, a, k, half, sib).wait_send()
        pltpu.make_async_copy(srcs[a], whole, sems[2].at[a]).wait()


def _relay_sems(n):
    return [pltpu.SemaphoreType.DMA((n, 9)), pltpu.SemaphoreType.DMA((n, 9)), pltpu.SemaphoreType.DMA((n,))]


def _gather_sems(n):
    return [pltpu.SemaphoreType.DMA((n, 7)), pltpu.SemaphoreType.DMA((n, 7)), pltpu.SemaphoreType.DMA((n,))]


class _Comm:
    def __init__(self, args, out_shape, scratch, start, finish, relay=None):
        self.args, self.out_shape, self.scratch, self.start, self.finish = args, out_shape, scratch, start, finish
        self.relay = relay


def _gather_comm(shards):
    return _Comm(list(shards), [_sds((N_DEV,) + s.shape[1:], s.dtype) for s in shards], _relay_sems(len(shards)),
                 _gather_start, _gather_finish, relay=_gather_relay)


def _together(big, small):
    na, no, ns = len(big.args), len(big.out_shape), len(big.scratch)

    def start(src, dst, sems):
        small.start(src[na:], dst[no:], sems[ns:])

    def relay(src, dst, sems):
        small.relay(src[na:], dst[no:], sems[ns:])
        big.start(src[:na], dst[:no], sems[:ns])

    def finish(src, dst, sems):
        small.finish(src[na:], dst[no:], sems[ns:])
        big.finish(src[:na], dst[:no], sems[:ns])

    assert big.relay is None and small.relay is not None
    return _Comm(big.args + small.args, big.out_shape + small.out_shape, list(big.scratch) + list(small.scratch),
                 start, finish, relay=relay)


def _direct_copies(ins, outs, sems, wait):
    send_sems, recv_sems, local_sems = sems
    x, y, c, me = _my_place()
    for a in range(len(ins)):
        local = pltpu.make_async_copy(ins[a].at[pl.ds(me, 1)], outs[a].at[pl.ds(me, 1)], local_sems.at[a])
        local.wait() if wait else local.start()
        for k in range(1, N_DEV):
            to, to_i = _peer(x, y, c, k)
            cp = pltpu.make_async_remote_copy(
                src_ref=ins[a].at[pl.ds(to_i, 1)], dst_ref=outs[a].at[pl.ds(me, 1)],
                send_sem=send_sems.at[a, k - 1], recv_sem=recv_sems.at[a, k - 1], device_id=to, device_id_type=MESH)
            cp.wait() if wait else cp.start()


def _exchange_comm(grads):
    return _Comm(list(grads), [_sds(g.shape, g.dtype) for g in grads], _gather_sems(len(grads)),
                 lambda i, o, s: _direct_copies(i, o, s, False), lambda i, o, s: _direct_copies(i, o, s, True))


def _chip_copies(ins, outs, sems, wait):
    send_sems, recv_sems, local_sems = sems
    x, y, c, _ = _my_place()
    for a in range(len(ins)):
        local = pltpu.make_async_copy(ins[a].at[pl.ds(2 * x + y, 1)], outs[a].at[pl.ds(len(CHIPS), 1)],
                                      local_sems.at[a])
        local.wait() if wait else local.start()
        for i, k in enumerate(CHIPS):
            to = _peer(x, y, c, k)[0]
            cp = pltpu.make_async_remote_copy(
                src_ref=ins[a].at[pl.ds(2 * to[0] + to[1], 1)], dst_ref=outs[a].at[pl.ds(i, 1)],
                send_sem=send_sems.at[a, i], recv_sem=recv_sems.at[a, i], device_id=to, device_id_type=MESH)
            cp.wait() if wait else cp.start()


def _chip_exchange_comm(sums):
    n = len(sums)
    sems = [pltpu.SemaphoreType.DMA((n, len(CHIPS))), pltpu.SemaphoreType.DMA((n, len(CHIPS))),
            pltpu.SemaphoreType.DMA((n,))]
    return _Comm(list(sums), [_sds(s.shape, s.dtype) for s in sums], sems,
                 lambda i, o, s: _chip_copies(i, o, s, False), lambda i, o, s: _chip_copies(i, o, s, True))


def _pair_reduce(name, grads, after=()):
    n = len(grads)
    n_chips = N_DEV // 2

    def body(*refs):
        g_refs, gh_refs, refs = refs[:n], refs[n:2 * n], refs[2 * n + len(after):]
        p_refs, land = refs[:n], refs[n:2 * n]
        send_sems, recv_sems = refs[2 * n:]
        x, y, c, _ = _my_place()
        sib = _peer(x, y, c, 1)[0]
        q = pl.program_id(0)

        def to_sibling(a, j):
            return pltpu.make_async_remote_copy(
                src_ref=gh_refs[a].at[j, pl.ds(1 - c, 1)], dst_ref=land[a].at[pl.ds(j, 1)],
                send_sem=send_sems.at[a, j], recv_sem=recv_sems.at[a, j], device_id=sib, device_id_type=MESH)

        @pl.when(q == 0)
        def _():
            for j in range(n_chips):
                for a in range(n):
                    to_sibling(a, j).start()

        for a in range(n):
            to_sibling(a, q).wait_recv()
            p_refs[a][...] = (g_refs[a][0, pl.ds(c, 1)].astype(F32) + land[a][pl.ds(q, 1)].astype(F32)).astype(BF16)

        @pl.when(q == n_chips - 1)
        def _():
            for a in range(n):
                for j in range(n_chips):
                    to_sibling(a, j).wait_send()

    views = [g.reshape((n_chips, 2) + g.shape[1:]) for g in grads]
    res = pl.pallas_call(
        body, name=name, grid=(n_chips,),
        in_specs=[pl.BlockSpec((1, 2) + g.shape[1:], lambda q: (q, 0, 0, 0)) for g in grads]
        + [ANY] * (n + len(after)),
        out_specs=[pl.BlockSpec((1,) + g.shape[1:], lambda q: (q, 0, 0)) for g in grads],
        out_shape=[_sds((n_chips,) + g.shape[1:], BF16) for g in grads],
        scratch_shapes=[pltpu.VMEM((n_chips,) + g.shape[1:], BF16) for g in grads]
        + [pltpu.SemaphoreType.DMA((n, n_chips)), pltpu.SemaphoreType.DMA((n, n_chips))],
        compiler_params=pltpu.CompilerParams(dimension_semantics=("arbitrary",), vmem_limit_bytes=VMEM_LIMIT),
    )(*views, *views, *after)
    return list(res)


def _pair_exchange_comm(grads):
    n, n_chips = len(grads), N_DEV // 2

    def copies(ins, outs, sems, wait):
        x, y, c, _ = _my_place()
        for j in range(n_chips):
            for a in range(n):
                cp = pltpu.make_async_remote_copy(
                    src_ref=ins[a].at[j, pl.ds(1 - c, 1)], dst_ref=outs[a].at[pl.ds(j, 1)], send_sem=sems[0].at[a, j],
                    recv_sem=sems[1].at[a, j], device_id=_peer(x, y, c, 1)[0], device_id_type=MESH)
                cp.wait() if wait else cp.start()

    views = [g.reshape((n_chips, 2) + g.shape[1:]) for g in grads]
    sems = [pltpu.SemaphoreType.DMA((n, n_chips)), pltpu.SemaphoreType.DMA((n, n_chips))]
    return _Comm(views, [_sds((n_chips,) + g.shape[1:], g.dtype) for g in grads], sems,
                 lambda i, o, s: copies(i, o, s, False), lambda i, o, s: copies(i, o, s, True))


def _pair_add(name, grads, landed, after=()):
    n, n_chips = len(grads), N_DEV // 2

    def body(core_ref, *refs):
        g_refs, l_refs, p_refs = refs[:n], refs[n:2 * n], refs[2 * n + len(after):]
        for a in range(n):
            p_refs[a][...] = (g_refs[a][...].astype(F32) + l_refs[a][...].astype(F32)).astype(BF16)

    views = [g.reshape((n_chips, 2) + g.shape[1:]) for g in grads]
    blocks = [pl.BlockSpec((1,) + g.shape[1:], lambda q, core: (q, 0, 0)) for g in grads]
    mine = [pl.BlockSpec((1, None) + g.shape[1:], lambda q, core: (q, core[0], 0, 0)) for g in grads]
    return list(pl.pallas_call(
        body, name=name, out_shape=[_sds((n_chips,) + g.shape[1:], BF16) for g in grads],
        grid_spec=pltpu.PrefetchScalarGridSpec(num_scalar_prefetch=1, grid=(n_chips,),
                                               in_specs=mine + blocks + [ANY] * len(after), out_specs=blocks),
        compiler_params=pltpu.CompilerParams(dimension_semantics=("arbitrary",), vmem_limit_bytes=VMEM_LIMIT),
    )(lax.axis_index("c").reshape(1), *views, *landed, *after))


def _call(name, body, grid, in_specs, out_specs, out_shape, args, scratch=(), after=()):
    ni, na = len(in_specs), len(after)

    def ordered(*refs):
        body(*refs[:ni], *refs[ni + na:])

    return list(pl.pallas_call(
        ordered if after else body, name=name, grid=grid, in_specs=list(in_specs) + [ANY] * na,
        out_specs=list(out_specs), out_shape=list(out_shape), scratch_shapes=list(scratch),
        compiler_params=pltpu.CompilerParams(dimension_semantics=("arbitrary",) * len(grid),
                                             vmem_limit_bytes=VMEM_LIMIT))(*args, *after))


SIBLING_AND_NEIGHBOURS, OTHER_CHIPS, EVERYONE = (1, 4, 2), CHIPS, tuple(range(1, N_DEV))


def _by_sequencer(name, comm, peers, collective_id):
    src = [jax.new_ref(a, memory_space=pltpu.MemorySpace.HBM) for a in comm.args]
    dst = [jax.empty_ref(s, memory_space=pltpu.MemorySpace.HBM) for s in comm.out_shape]

    @pl.kernel(mesh=plsc.ScalarSubcoreMesh(axis_name="sequencer", num_cores=1), name=name,
               scratch_types=tuple(comm.scratch), compiler_params=pltpu.CompilerParams(collective_id=collective_id))
    def launch(*sems):
        x, y, c, _ = _my_place()
        barrier = pltpu.get_barrier_semaphore()
        for k in peers:
            pl.semaphore_signal(barrier, inc=1, device_id=_peer(x, y, c, k)[0], device_id_type=MESH)
        pl.semaphore_wait(barrier, len(peers))
        comm.start(src, dst, sems)
        if comm.relay is not None:
            comm.relay(src, dst, sems)
        comm.finish(src, dst, sems)

    launch()
    return [d[...] for d in dst]


def _gather_first(first, later, swapped):
    nf = len(first)
    layer_of = [(a, l) for a, s in enumerate(later) for l in range(s.shape[0])]
    nl = len(layer_of)
    dts = [BF16] * (nf - 2) + [F32, F32]
    shard = [s.shape[:0:-1] if sw else s.shape[1:] for s, sw in zip(later, swapped)]

    def body(*refs):
        ins, refs = refs[:nf + len(later)], refs[nf + len(later):]
        outs, refs = refs[:nf], refs[nf:]
        casts, refs = refs[:nl], refs[nl:]
        stage, sems = refs[:nf], refs[nf:]
        for a in range(nf):
            stage[a][...] = ins[a][...].astype(dts[a])
            _gather_start(stage, outs, sems, only=a)
        for k, (a, l) in enumerate(layer_of):
            block = ins[nf + a][l]
            casts[k][0] = (block.T if swapped[a] else block).astype(BF16)
        _gather_relay(stage, outs, sems)
        _gather_finish(stage, outs, sems)

    res = pl.pallas_call(
        body, name="gather_first",
        in_specs=[VMEM] * (nf + len(later)), out_specs=[ANY] * nf + [VMEM] * nl,
        out_shape=[_sds((N_DEV,) + s.shape[1:], dt) for s, dt in zip(first, dts)]
        + [_sds((1,) + shard[a], BF16) for a, _ in layer_of],
        scratch_shapes=[pltpu.VMEM(s.shape, dt) for s, dt in zip(first, dts)] + _relay_sems(nf),
        compiler_params=pltpu.CompilerParams(vmem_limit_bytes=VMEM_LIMIT),
    )(*first, *later)
    return list(res[:nf]), list(res[nf:])


def _a_mix_fwd(x, g, w_in, ln_g, ln_b, w_s, b_st, w_out):
    t = x.shape[0]
    nblk = TM // GMLP_BLOCK

    def body(x_ref, g_ref, win_ref, lng_ref, lnb_ref, ws_ref, bst_ref, wout_ref, h_ref, z_ref, gated_scr):
        xv = x_ref[...]
        hb = _rms_fwd(xv, g_ref[...])[0].astype(BF16)
        for d in range(N_DEV):
            z_ref[:, d * FF_SLOT:(d + 1) * FF_SLOT] = _dot(hb, win_ref[d])
        u = _gelu(z_ref[:, :GATE_DIM])
        vb = _ln_fwd(_gelu(z_ref[:, GATE_DIM:]), lng_ref[...], lnb_ref[...])[0].astype(BF16)
        mask = _gate_mask()
        for gi in range(A_GROUPS):
            wm = jnp.where(mask, ws_ref[gi], 0.0).astype(BF16)
            bias = bst_ref[:, gi:gi + 1]
            cs = slice(gi * A_GROUP_DIM, (gi + 1) * A_GROUP_DIM)
            for n in range(nblk):
                rs = slice(n * GMLP_BLOCK, (n + 1) * GMLP_BLOCK)
                sv = _dot(wm, vb[rs, cs]) + bias
                gated_scr[rs, cs] = (u[rs, cs] * sv).astype(BF16)
        h_ref[...] = xv + _dot(gated_scr[...], wout_ref[...])

    return _call(
        "a_mix_fwd", body, (t // TM,),
        [_row(D_MODEL), _res((1, D_MODEL)), _res((N_DEV, D_MODEL, FF_SLOT)), _res((1, GATE_DIM)),
         _res((1, GATE_DIM)), _res((A_GROUPS, GMLP_BLOCK, GMLP_BLOCK)), _res((GMLP_BLOCK, A_GROUPS)),
         _res((GATE_DIM, D_MODEL))],
        [_row(D_MODEL), _row(2 * GATE_DIM), _row(GATE_DIM)],
        [_sds((t, D_MODEL), F32), _sds((t, 2 * GATE_DIM), F32), _sds((t, GATE_DIM), BF16)],
        (x, g, w_in, ln_g, ln_b, w_s, b_st, w_out))


MLP_W_SPECS = (ANY, ANY)
MLP_W_SCRATCH = (pltpu.VMEM((N_DEV, D_MODEL, FF_SLOT), BF16), pltpu.VMEM((N_DEV, FF_SLOT, D_MODEL), BF16),
                 pltpu.SemaphoreType.DMA((2, N_DEV)))


def _with_streamed_weights(hbm, vmem, sems, order, compute):
    copies = {(j, d): pltpu.make_async_copy(hbm[j].at[d], vmem[j].at[d], sems.at[j, d])
              for d in range(N_DEV) for j in order}
    first = pl.program_id(0) == 0

    def assign(ref, val):
        ref[...] = val

    def add(ref, val):
        ref[...] += val

    @pl.when(first)
    def _():
        for c in copies.values():
            c.start()
        compute(lambda j, d: copies[j, d].wait(), assign)

    @pl.when(jnp.logical_not(first))
    def _():
        compute(lambda j, d: None, add)


def _mlp_fwd(h, g, w1, w2):
    t = h.shape[0]

    def body(h_ref, g_ref, w1_hbm, w2_hbm, o_ref, a_ref, f_ref, w1_ref, w2_ref, sems):
        def compute(arrived, acc):
            hv = h_ref[...]
            hb = _rms_fwd(hv, g_ref[...])[0].astype(BF16)
            o_ref[...] = hv
            for d in range(N_DEV):
                cs = slice(d * FF_SLOT, (d + 1) * FF_SLOT)
                arrived(0, d)
                a = _dot(hb, w1_ref[d])
                a_ref[:, cs] = a
                r = jnp.maximum(a, 0.0)
                f = (r * r).astype(BF16)
                f_ref[:, cs] = f
                arrived(1, d)
                o_ref[...] += _dot(f, w2_ref[d])

        _with_streamed_weights((w1_hbm, w2_hbm), (w1_ref, w2_ref), sems, (0, 1), compute)

    return _call(
        "mlp_fwd", body, (t // TM_MLP_FWD,), [_row(D_MODEL, TM_MLP_FWD), _res((1, D_MODEL)), *MLP_W_SPECS],
        [_row(D_MODEL, TM_MLP_FWD), _row(D_FF, TM_MLP_FWD), _row(D_FF, TM_MLP_FWD)],
        [_sds((t, D_MODEL), F32), _sds((t, D_FF), F32), _sds((t, D_FF), BF16)],
        (h, g, w1, w2), scratch=MLP_W_SCRATCH)


def _mlp_fwd_loss(h, g, w1, w2, final_g, target):
    t = h.shape[0]

    def body(h_ref, g_ref, w1_hbm, w2_hbm, fg_ref, t_ref, a_ref, loss_ref, dh_ref, dg_ref, f_ref,
             w1_ref, w2_ref, sems):
        def compute(arrived, acc):
            hv = h_ref[...]
            hb = _rms_fwd(hv, g_ref[...])[0].astype(BF16)
            out = hv
            for d in range(N_DEV):
                cs = slice(d * FF_SLOT, (d + 1) * FF_SLOT)
                arrived(0, d)
                a = _dot(hb, w1_ref[d])
                a_ref[:, cs] = a
                r = jnp.maximum(a, 0.0)
                f = (r * r).astype(BF16)
                f_ref[:, cs] = f
                arrived(1, d)
                out = out + _dot(f, w2_ref[d])
            y, xhat, rstd = _rms_fwd(out, fg_ref[...])
            err = y - t_ref[...]
            part = 0.5 * jnp.sum(jnp.mean(err * err, axis=-1, keepdims=True), axis=0, keepdims=True)
            dx, dg = _rms_bwd(err * (1.0 / D_MODEL), xhat, rstd, fg_ref[...])
            dh_ref[...] = dx
            acc(dg_ref, dg)
            acc(loss_ref, part)

        _with_streamed_weights((w1_hbm, w2_hbm), (w1_ref, w2_ref), sems, (0, 1), compute)

    return _call(
        "mlp_fwd_loss", body, (t // TM,),
        [_row(D_MODEL), _res((1, D_MODEL)), *MLP_W_SPECS, _res((1, D_MODEL)), _row(D_MODEL)],
        [_row(D_FF), _const((1, 1)), _row(D_MODEL), _const((1, D_MODEL)), _row(D_FF)],
        [_sds((t, D_FF), F32), _sds((1, 1), F32), _sds((t, D_MODEL), F32), _sds((1, D_MODEL), F32),
         _sds((t, D_FF), BF16)],
        (h, g, w1, w2, final_g, target), scratch=MLP_W_SCRATCH)


KVQ_W_SPECS = (_res((1, D_MODEL)), _res((D_MODEL, KV_LORA + QK_ROPE)), _res((1, KV_LORA)),
               _res((B_HEADS, KV_LORA, QK_NOPE + V_HEAD)), _res((1, D_MODEL)), _res((D_MODEL, Q_LORA)),
               _res((1, Q_LORA)), _res((B_HEADS, Q_LORA, QK_NOPE + QK_ROPE)))


def _kvq_fwd(h, pos, inv_freq, kvq_w):
    t = h.shape[0]
    half = QK_ROPE // 2

    def body(h_ref, pos_ref, invf_ref, srcg_ref, wkva_ref, kvag_ref, wkvb_ref, mixg_ref, wqa_ref, qg_ref, wqb_ref,
             ckv_ref, k_ref, v_ref, cqpre_ref, q_ref, cos_ref, sin_ref):
        hv = h_ref[...]
        xhat = hv * lax.rsqrt(jnp.mean(hv * hv, axis=-1, keepdims=True) + EPS)
        ang = pos_ref[...].astype(F32) * invf_ref[...]
        cos, sin = jnp.cos(ang), jnp.sin(ang)
        cos_ref[...] = cos
        sin_ref[...] = sin
        ckv = _dot((xhat * srcg_ref[...]).astype(BF16), wkva_ref[...])
        ckv_ref[...] = ckv
        cb = _rms_fwd(ckv[:, :KV_LORA], kvag_ref[...])[0].astype(BF16)
        kpe = _rope(ckv[:, KV_LORA:], cos, sin).astype(BF16)
        for hd in range(B_HEADS):
            kv = _dot(cb, wkvb_ref[hd])
            k_ref[hd, :, 0:QK_NOPE] = kv[:, :QK_NOPE].astype(BF16)
            k_ref[hd, :, QK_NOPE:] = kpe
            v_ref[hd] = kv[:, QK_NOPE:].astype(BF16)
        cqpre = _dot((xhat * mixg_ref[...]).astype(BF16), wqa_ref[...])
        cqpre_ref[...] = cqpre
        cqb = _rms_fwd(cqpre, qg_ref[...])[0].astype(BF16)
        for hd in range(B_HEADS):
            q = _dot(cqb, wqb_ref[hd])
            q_ref[hd, :, 0:QK_NOPE] = q[:, :QK_NOPE].astype(BF16)
            q_ref[hd, :, QK_NOPE:] = _rope(q[:, QK_NOPE:], cos, sin).astype(BF16)

    tm = TM_KVQ
    return _call(
        "kvq_fwd", body, (t // tm,), [_row(D_MODEL, tm), _row(1, tm), _res((1, half)), *KVQ_W_SPECS],
        [_row(KV_LORA + QK_ROPE, tm), _heads(QK_NOPE + QK_ROPE, tm), _heads(V_HEAD, tm), _row(Q_LORA, tm),
         _heads(QK_NOPE + QK_ROPE, tm), _row(half, tm), _row(half, tm)],
        [_sds((t, KV_LORA + QK_ROPE), F32), _sds((B_HEADS, t, QK_NOPE + QK_ROPE), BF16),
         _sds((B_HEADS, t, V_HEAD), BF16), _sds((t, Q_LORA), F32), _sds((B_HEADS, t, QK_NOPE + QK_ROPE), BF16),
         _sds((t, half), F32), _sds((t, half), F32)],
        (h, pos, inv_freq, *kvq_w))


def _softmax_rows(q, k_ref, k):
    past, upto = k * TM, (k + 1) * TM
    s = _dot_nt(q, k_ref[0:upto, :])
    own = jnp.where(_att_mask(0, TM, TM), s[:, past:], jnp.finfo(F32).min)
    s = own if k == 0 else jnp.concatenate([s[:, :past], own], axis=1)
    e = jnp.exp2((s - jnp.max(s, axis=-1, keepdims=True)) * (ATT_SCALE * LOG2_E))
    return e * (1.0 / jnp.sum(e, axis=-1, keepdims=True))


def _for_my_tile(i, nq, fn):
    for k in range(nq):
        @pl.when(i == k)
        def _(k=k):
            fn(k)


def _attn_fwd(h, q, k, v, w_o):
    t = h.shape[0]
    nq, hps = t // TM, HEADS_PER_STEP

    def body(h_ref, q_ref, k_ref, v_ref, wo_ref, o_ref, att_ref):
        i, pair = pl.program_id(0), pl.program_id(1)

        @pl.when(pair == 0)
        def _():
            o_ref[...] = h_ref[...]

        def tile(kt):
            proj = None
            for j in range(hps):
                hd = pair * hps + j
                p = _softmax_rows(q_ref[j], k_ref.at[hd], kt)
                ob = _dot(p.astype(BF16), v_ref[hd, 0:(kt + 1) * TM, :]).astype(BF16)
                att_ref[j] = ob
                proj = _dot(ob, wo_ref[hd]) if proj is None else proj + _dot(ob, wo_ref[hd])
            o_ref[...] += proj

        _for_my_tile(i, nq, tile)

    def per_head(d):
        return pl.BlockSpec((hps, TM, d), lambda i, pair: (pair, i, 0))

    def resident(shape):
        zeros = (0,) * len(shape)
        return pl.BlockSpec(shape, lambda i, pair: zeros, pipeline_mode=pl.Buffered(1))

    tile_spec = pl.BlockSpec((TM, D_MODEL), lambda i, pair: (i, 0))
    return _call(
        "attn_fwd", body, (nq, B_HEADS // hps),
        [tile_spec, per_head(QK_NOPE + QK_ROPE), resident((B_HEADS, t, QK_NOPE + QK_ROPE)),
         resident((B_HEADS, t, V_HEAD)), resident((B_HEADS, V_HEAD, D_MODEL))],
        [tile_spec, per_head(V_HEAD)], [_sds((t, D_MODEL), F32), _sds((B_HEADS, t, V_HEAD), BF16)],
        (h, q, k, v, w_o))


def _mlp_bwd(h, a, dho, g, w1, w2, layer, after=()):
    t = h.shape[0]

    def body(h_ref, a_ref, dho_ref, g_ref, w1_hbm, w2_hbm, dhi_ref, dg_ref, hn_ref, da_ref, dhib_ref,
             w1_ref, w2_ref, sems):
        def compute(arrived, acc):
            gv = g_ref[...]
            y, xhat, rstd = _rms_fwd(h_ref[...], gv)
            hn_ref[...] = y.astype(BF16)
            dho_v = dho_ref[...]
            dhob = dho_v.astype(BF16)
            dhn = jnp.zeros((TM, D_MODEL), F32)
            for d in range(N_DEV):
                cs = slice(d * FF_SLOT, (d + 1) * FF_SLOT)
                r = jnp.maximum(a_ref[:, cs], 0.0)
                arrived(1, d)
                da = (_dot_nt(dhob, w2_ref[d]) * (2.0 * r)).astype(BF16)
                da_ref[:, cs] = da
                arrived(0, d)
                dhn = dhn + _dot_nt(da, w1_ref[d])
            dx, dg = _rms_bwd(dhn, xhat, rstd, gv)
            dhi = dho_v + dx
            dhi_ref[...] = dhi
            dhib_ref[...] = dhi.astype(BF16)
            acc(dg_ref, dg)

        _with_streamed_weights((w1_hbm, w2_hbm), (w1_ref, w2_ref), sems, (1, 0), compute)

    return _call(
        f"mlp_bwd_{layer}", body, (t // TM,),
        [_row(D_MODEL), _row(D_FF), _row(D_MODEL), _res((1, D_MODEL)), *MLP_W_SPECS],
        [_row(D_MODEL), _const((1, D_MODEL)), _row(D_MODEL), _row(D_FF), _row(D_MODEL)],
        [_sds((t, D_MODEL), F32), _sds((1, D_MODEL), F32), _sds((t, D_MODEL), BF16), _sds((t, D_FF), BF16),
         _sds((t, D_MODEL), BF16)],
        (h, a, dho, g, w1, w2), scratch=MLP_W_SCRATCH, after=after)


def _attn_bwd(dh, q, k, v, w_o, cos, sin, after=()):
    t = dh.shape[0]
    half, hps = QK_ROPE // 2, HEADS_PER_STEP

    def body(dh_ref, q_ref, k_ref, v_ref, wo_ref, cos_ref, sin_ref, dq_ref, dk_ref, dv_ref):
        i = pl.program_id(1)

        @pl.when(i == 0)
        def _():
            dk_ref[...] = jnp.zeros_like(dk_ref)
            dv_ref[...] = jnp.zeros_like(dv_ref)

        def tile(kt):
            keys = slice(0, (kt + 1) * TM)
            for j in range(hps):
                qj = q_ref[j]
                do = _dot_nt(dh_ref[kt * TM:(kt + 1) * TM, :], wo_ref[j]).astype(BF16)
                p = _softmax_rows(qj, k_ref.at[j], kt)
                dp = _dot_nt(do, v_ref[j, keys, :])
                ds = (p * (dp - jnp.sum(p * dp, axis=-1, keepdims=True)) * ATT_SCALE).astype(BF16)
                dq = _dot(ds, k_ref[j, keys, :])
                dq_ref[j, :, 0:QK_NOPE] = dq[:, :QK_NOPE].astype(BF16)
                dq_ref[j, :, QK_NOPE:] = _rope(dq[:, QK_NOPE:], cos_ref[...], -sin_ref[...]).astype(BF16)
                dk_ref[j, keys, :] += _dot_tn(ds, qj)
                dv_ref[j, keys, :] += _dot_tn(p.astype(BF16), do)

        _for_my_tile(i, t // TM, tile)

    def per_pair(rows, d, tiled):
        return pl.BlockSpec((hps, rows, d), (lambda pair, i: (pair, i, 0)) if tiled else (lambda pair, i: (pair, 0, 0)))

    def tile(d):
        return pl.BlockSpec((TM, d), lambda pair, i: (i, 0))

    return _call(
        "attn_bwd", body, (B_HEADS // hps, t // TM),
        [pl.BlockSpec((t, D_MODEL), lambda pair, i: (0, 0), pipeline_mode=pl.Buffered(1)),
         per_pair(TM, QK_NOPE + QK_ROPE, True), per_pair(t, QK_NOPE + QK_ROPE, False), per_pair(t, V_HEAD, False),
         per_pair(V_HEAD, D_MODEL, False), tile(half), tile(half)],
        [per_pair(TM, QK_NOPE + QK_ROPE, True), per_pair(t, QK_NOPE + QK_ROPE, False), per_pair(t, V_HEAD, False)],
        [_sds((B_HEADS, t, QK_NOPE + QK_ROPE), BF16), _sds((B_HEADS, t, QK_NOPE + QK_ROPE), F32),
         _sds((B_HEADS, t, V_HEAD), F32)],
        (dh, q, k, v, w_o, cos, sin), after=after)


def _kvq_bwd(h, dh, ckv, cqpre, dq, dk, dv, cos, sin, kvq_w, after=()):
    t = h.shape[0]
    tm = TM
    half, last = QK_ROPE // 2, t // tm - 1
    grad_shapes = [(D_MODEL, Q_LORA), (B_HEADS, Q_LORA, QK_NOPE + QK_ROPE), (D_MODEL, KV_LORA + QK_ROPE),
                   (B_HEADS, KV_LORA, QK_NOPE + V_HEAD)]

    def body(h_ref, dh_ref, ckv_ref, cqpre_ref, dq_ref, dk_ref, dv_ref, cos_ref, sin_ref,
             srcg_ref, wkva_ref, kvag_ref, wkvb_ref, mixg_ref, wqa_ref, qg_ref, wqb_ref,
             dhi_ref, dmixg_ref, dsrcg_ref, dqg_ref, dkvag_ref, gqa_ref, gqb_ref, gkva_ref, gkvb_ref,
             aqa, aqb, akva, akvb):
        @pl.when(pl.program_id(0) == 0)
        def _():
            for acc in (aqa, aqb, akva, akvb):
                acc[...] = jnp.zeros_like(acc)

        hv = h_ref[...]
        rstd = lax.rsqrt(jnp.mean(hv * hv, axis=-1, keepdims=True) + EPS)
        xhat = hv * rstd
        mixg, srcg, qg, kvag = mixg_ref[...], srcg_ref[...], qg_ref[...], kvag_ref[...]
        cq, cqhat, crstd = _rms_fwd(cqpre_ref[...], qg)
        cqb = cq.astype(BF16)
        dcq = jnp.zeros((tm, Q_LORA), F32)
        for hd in range(B_HEADS):
            dcq = dcq + _dot_nt(dq_ref[hd], wqb_ref[hd])
            aqb[hd] += _dot_tn(cqb, dq_ref[hd])
        dcqpre, dqg = _rms_bwd(dcq, cqhat, crstd, qg)
        dcqpre_b = dcqpre.astype(BF16)
        aqa[...] += _dot_tn((xhat * mixg).astype(BF16), dcqpre_b)
        dxq, dmixg = _rms_bwd(_dot_nt(dcqpre_b, wqa_ref[...]), xhat, rstd, mixg)
        ckv = ckv_ref[...]
        c, chat, krstd = _rms_fwd(ckv[:, :KV_LORA], kvag)
        cb = c.astype(BF16)
        dc = jnp.zeros((tm, KV_LORA), F32)
        dkpe = jnp.zeros((tm, QK_ROPE), F32)
        for hd in range(B_HEADS):
            dkv = jnp.concatenate([dk_ref[hd, :, 0:QK_NOPE], dv_ref[hd]], axis=-1).astype(BF16)
            akvb[hd] += _dot_tn(cb, dkv)
            dc = dc + _dot_nt(dkv, wkvb_ref[hd])
            dkpe = dkpe + dk_ref[hd, :, QK_NOPE:]
        dlat, dkvag = _rms_bwd(dc, chat, krstd, kvag)
        dpe = _rope(dkpe, cos_ref[...], -sin_ref[...])
        dckv_b = jnp.concatenate([dlat, dpe], axis=-1).astype(BF16)
        akva[...] += _dot_tn((xhat * srcg).astype(BF16), dckv_b)
        dxk, dsrcg = _rms_bwd(_dot_nt(dckv_b, wkva_ref[...]), xhat, rstd, srcg)
        dhi_ref[...] = dh_ref[...] + dxq + dxk
        _acc(dmixg_ref, dmixg)
        _acc(dsrcg_ref, dsrcg)
        _acc(dqg_ref, dqg)
        _acc(dkvag_ref, dkvag)

        @pl.when(pl.program_id(0) == last)
        def _():
            for out, acc in ((gqa_ref, aqa), (gqb_ref, aqb), (gkva_ref, akva), (gkvb_ref, akvb)):
                out[...] = acc[...].astype(BF16)

    return _call(
        "kvq_bwd", body, (t // tm,),
        [_row(D_MODEL, tm), _row(D_MODEL, tm), _row(KV_LORA + QK_ROPE, tm), _row(Q_LORA, tm),
         _heads(QK_NOPE + QK_ROPE, tm), _heads(QK_NOPE + QK_ROPE, tm), _heads(V_HEAD, tm), _row(half, tm),
         _row(half, tm), *KVQ_W_SPECS],
        [_row(D_MODEL, tm), _const((1, D_MODEL)), _const((1, D_MODEL)), _const((1, Q_LORA)), _const((1, KV_LORA))]
        + [_const(s) for s in grad_shapes],
        [_sds((t, D_MODEL), F32), _sds((1, D_MODEL), F32), _sds((1, D_MODEL), F32), _sds((1, Q_LORA), F32),
         _sds((1, KV_LORA), F32)] + [_sds(s, BF16) for s in grad_shapes],
        (h, dh, ckv, cqpre, dq, dk, dv, cos, sin, *kvq_w), scratch=[pltpu.VMEM(s, F32) for s in grad_shapes],
        after=after)


def _a_mix_bwd(x, z, dh, g, w_in, ln_g, ln_b, w_s, b_st, w_out, after=()):
    t = x.shape[0]
    tm = TM_GATE
    nblk = tm // GMLP_BLOCK

    def body(x_ref, z_ref, dh_ref, g_ref, win_ref, lng_ref, lnb_ref, ws_ref, bst_ref, wout_ref,
             dx_ref, hn_ref, dz_ref, dg_ref, dlng_ref, dlnb_ref, dws_ref, dbs_ref, dvn_scr, gelu_grad_v):
        @pl.when(pl.program_id(0) == 0)
        def _():
            dws_ref[...] = jnp.zeros_like(dws_ref)
            dbs_ref[...] = jnp.zeros_like(dbs_ref)

        gv, lng = g_ref[...], lng_ref[...]
        y, xhat, rstd = _rms_fwd(x_ref[...], gv)
        hn_ref[...] = y.astype(BF16)
        dhv = dh_ref[...]
        dgated = _dot_nt(dhv.astype(BF16), wout_ref[...])
        u, gelu_grad_u = _gelu_and_grad(z_ref[:, :GATE_DIM])
        v, gelu_grad_v[...] = _gelu_and_grad(z_ref[:, GATE_DIM:])
        vn, vhat, lrstd = _ln_fwd(v, lng, lnb_ref[...])
        vb = vn.astype(BF16)
        mask = _gate_mask()
        for gi in range(A_GROUPS):
            wm = jnp.where(mask, ws_ref[gi], 0.0).astype(BF16)
            bias = bst_ref[:, gi:gi + 1]
            cs = slice(gi * A_GROUP_DIM, (gi + 1) * A_GROUP_DIM)
            dws = jnp.zeros((GMLP_BLOCK, GMLP_BLOCK), F32)
            dbs = jnp.zeros((GMLP_BLOCK, 1), F32)
            for n in range(nblk):
                rs = slice(n * GMLP_BLOCK, (n + 1) * GMLP_BLOCK)
                sv = _dot(wm, vb[rs, cs]) + bias
                dz_ref[rs, cs] = (dgated[rs, cs] * sv * gelu_grad_u[rs, cs]).astype(BF16)
                dsv = dgated[rs, cs] * u[rs, cs]
                dsvb = dsv.astype(BF16)
                dws = dws + _dot_nt(dsvb, vb[rs, cs])
                dbs = dbs + jnp.sum(dsv, axis=-1, keepdims=True)
                dvn_scr[rs, cs] = _dot_tn(wm, dsvb)
            dws_ref[gi] += jnp.where(mask, dws, 0.0)
            dbs_ref[gi] += dbs
        dvn = dvn_scr[...]
        dvhat = dvn * lng
        dv = lrstd * (dvhat - jnp.mean(dvhat, axis=-1, keepdims=True)
                      - vhat * jnp.mean(dvhat * vhat, axis=-1, keepdims=True))
        dz_ref[:, GATE_DIM:] = (dv * gelu_grad_v[...]).astype(BF16)
        dhn = jnp.zeros((tm, D_MODEL), F32)
        for d in range(N_DEV):
            dhn = dhn + _dot_nt(dz_ref[:, d * FF_SLOT:(d + 1) * FF_SLOT], win_ref[d])
        dx, dg = _rms_bwd(dhn, xhat, rstd, gv)
        dx_ref[...] = dhv + dx
        _acc(dg_ref, dg)
        _acc(dlng_ref, jnp.sum(dvn * vhat, axis=0, keepdims=True))
        _acc(dlnb_ref, jnp.sum(dvn, axis=0, keepdims=True))

    return _call(
        "a_mix_bwd", body, (t // tm,),
        [_row(D_MODEL, tm), _row(2 * GATE_DIM, tm), _row(D_MODEL, tm), _res((1, D_MODEL)),
         _res((N_DEV, D_MODEL, FF_SLOT)), _res((1, GATE_DIM)), _res((1, GATE_DIM)),
         _res((A_GROUPS, GMLP_BLOCK, GMLP_BLOCK)), _res((GMLP_BLOCK, A_GROUPS)), _res((GATE_DIM, D_MODEL))],
        [_row(D_MODEL, tm), _row(D_MODEL, tm), _row(2 * GATE_DIM, tm),
         _const((1, D_MODEL)), _const((1, GATE_DIM)), _const((1, GATE_DIM)),
         _const((A_GROUPS, GMLP_BLOCK, GMLP_BLOCK)), _const((A_GROUPS, GMLP_BLOCK, 1))],
        [_sds((t, D_MODEL), F32), _sds((t, D_MODEL), BF16),
         _sds((t, 2 * GATE_DIM), BF16), _sds((1, D_MODEL), F32), _sds((1, GATE_DIM), F32),
         _sds((1, GATE_DIM), F32), _sds((A_GROUPS, GMLP_BLOCK, GMLP_BLOCK), F32),
         _sds((A_GROUPS, GMLP_BLOCK, 1), F32)],
        (x, z, dh, g, w_in, ln_g, ln_b, w_s, b_st, w_out),
        scratch=[pltpu.VMEM((tm, GATE_DIM), F32), pltpu.VMEM((tm, GATE_DIM), F32)], after=after)


def _wgrad(name, a, b, a_spec, b_spec, m, n, after=(), slots=1):
    def body(a_ref, b_ref, o_ref):
        res = _dot_tn(a_ref[...].astype(BF16), b_ref[...].astype(BF16)).astype(BF16)
        for s in range(slots):
            o_ref[s] = res[s * m:(s + 1) * m] if res.shape[0] == slots * m else res[:, s * n:(s + 1) * n]

    return _call(name, body, (N_DEV // slots,), [a_spec, b_spec],
                 [pl.BlockSpec((slots, m, n), lambda d: (d, 0, 0))], [_sds((N_DEV, m, n), BF16)], (a, b),
                 after=after)[0]


def _full(t, d):
    return pl.BlockSpec((t, d), lambda i: (0, 0), pipeline_mode=pl.Buffered(1))


def _cols(t, d):
    return pl.BlockSpec((t, d), lambda i: (0, i))


def _head(t, d):
    return pl.BlockSpec((None, t, d), lambda i: (i, 0, 0))


def _local_step(x, pos, target, inv_freq, wg, sm, shards=None):
    t = x.shape[0]
    wg = dict(wg)
    dist = shards is not None
    mix_g = [sm["norm_mix_g"][l:l + 1] for l in range(2)]
    mlp_g = [sm["norm_mlp_g"][l:l + 1] for l in range(2)]

    ids = iter(range(2, 2 + 9))

    def gather(names):
        if dist:
            got = _by_sequencer("gather_" + names[0], _gather_comm([shards[k] for k in names]),
                                SIBLING_AND_NEIGHBOURS, next(ids))
            wg.update(zip(names, got))

    def send(name, names):
        if dist:
            comm = _exchange_comm(grads=[g[k] for k in names])
            g.update(zip(names, _by_sequencer("exchange_" + name, comm, EVERYONE, next(ids))))

    def send_sums(name, names, meanwhile):
        if not dist:
            meanwhile()
            return ()
        grads = [g[k] for k in names]
        landed = _by_sequencer("pair_exchange_" + name, _pair_exchange_comm(grads), (1,), next(ids))
        sums = _pair_add("pair_add_" + name, grads, landed, after=meanwhile())
        g.update(zip(names, _by_sequencer("exchange_" + name, _chip_exchange_comm(sums), OTHER_CHIPS, next(ids))))
        return sums

    def a_args():
        return (wg["a_w_in"], wg["a_ln_v_g"], wg["a_ln_v_b"], sm["a_w_s"], sm["a_b_st"], wg["a_w_out"])

    def kvq_w():
        return (sm["kv_src_norm_g"], wg["kv_w_a"], sm["kv_a_norm_g"], wg["kv_w_b"], mix_g[1], wg["b_w_q_a"],
                sm["b_q_norm_g"], wg["b_w_q_b"])

    gather(("mlp_w1_0", "mlp_w2_0"))
    h1, z, gated = _a_mix_fwd(x, mix_g[0], *a_args())
    gather(("kv_w_a", "kv_w_b", "b_w_q_a", "b_w_q_b", "b_w_o"))
    h2, a0, f0 = _mlp_fwd(h1, mlp_g[0], wg["mlp_w1_0"], wg["mlp_w2_0"])
    if dist:
        wg["b_w_q_a"] = wg["b_w_q_a"].reshape(D_MODEL, Q_LORA)
        wg["kv_w_a"] = wg["kv_w_a"].reshape(D_MODEL, KV_LORA + QK_ROPE)
    gather(("mlp_w1_1", "mlp_w2_1"))
    ckv, k, v, cqpre, q, cos, sin = _kvq_fwd(h2, pos, inv_freq, kvq_w())
    h3, att = _attn_fwd(h2, q, k, v, wg["b_w_o"])
    a1, loss, dh4, d_final_g, f1 = _mlp_fwd_loss(h3, mlp_g[1], wg["mlp_w1_1"], wg["mlp_w2_1"], sm["final_norm_g"],
                                                 target)

    g = {}
    dh3, d_mlp_g1, hn, da, dh3_b = _mlp_bwd(h3, a1, dh4, mlp_g[1], wg["mlp_w1_1"], wg["mlp_w2_1"], 1)
    dq, dk, dv = _attn_bwd(dh3_b, q, k, v, wg["b_w_o"], cos, sin)
    g["mlp_w1_1"] = _wgrad("wgrad_w1_1", hn, da, _full(t, D_MODEL), _cols(t, 2 * FF_SLOT), D_MODEL, FF_SLOT,
                           after=[dq], slots=2)
    g["mlp_w2_1"] = _wgrad("wgrad_w2_1", f1, dh4, _cols(t, 2 * FF_SLOT), _full(t, D_MODEL), FF_SLOT, D_MODEL, slots=2)

    def wgrad_w_o():
        g["b_w_o"] = _wgrad("wgrad_w_o", att, dh3_b, _head(t, V_HEAD), _full(t, D_MODEL), V_HEAD, D_MODEL)
        return [g["b_w_o"]]

    sums = send_sums("mlp_1", ("mlp_w1_1", "mlp_w2_1"), wgrad_w_o)
    dh2, d_mix_g1, d_src_g, d_q_g, d_kv_a_g, g_q_a, g["b_w_q_b"], g_kv_a, g["kv_w_b"] = _kvq_bwd(
        h2, dh3, ckv, cqpre, dq, dk, dv, cos, sin, kvq_w(), after=sums)
    g["b_w_q_a"] = g_q_a.reshape(N_DEV, D_MODEL // N_DEV, Q_LORA)
    g["kv_w_a"] = g_kv_a.reshape(N_DEV, D_MODEL // N_DEV, KV_LORA + QK_ROPE)
    qkv = ("b_w_q_a", "b_w_q_b", "kv_w_a", "kv_w_b")
    landed = [g[k] for k in qkv]
    send("qkv", qkv)
    dh1, d_mlp_g0, hn, da, dh1_b = _mlp_bwd(h1, a0, dh2, mlp_g[0], wg["mlp_w1_0"], wg["mlp_w2_0"], 0,
                                               after=landed if dist else ())
    landed = [g["mlp_w1_1"], g["mlp_w2_1"]] if dist else ()
    g["mlp_w1_0"] = _wgrad("wgrad_w1_0", hn, da, _full(t, D_MODEL), _cols(t, 2 * FF_SLOT), D_MODEL, FF_SLOT,
                           after=landed, slots=2)
    g["mlp_w2_0"] = _wgrad("wgrad_w2_0", f0, dh2, _cols(t, 2 * FF_SLOT), _full(t, D_MODEL), FF_SLOT, D_MODEL, slots=2)

    def wgrad_a_w_out():
        g["a_w_out"] = _wgrad("wgrad_a_w_out", gated, dh1_b, _cols(t, GATE_DIM // N_DEV), _full(t, D_MODEL),
                              GATE_DIM // N_DEV, D_MODEL)
        return [g["a_w_out"]] + [g[k] for k in qkv]

    sums = send_sums("mlp_0", ("mlp_w1_0", "mlp_w2_0", "b_w_o"), wgrad_a_w_out)
    if dist:
        sums = _pair_reduce("pair_reduce_a_w_out", [g["a_w_out"]], after=sums)
    dx, hn, dz, d_mix_g0, d_ln_g, d_ln_b, d_ws, d_bs = _a_mix_bwd(x, z, dh1, mix_g[0], *a_args(), after=sums)
    small = {
        "norm_mix_g": jnp.concatenate([d_mix_g0, d_mix_g1], axis=0),
        "norm_mlp_g": jnp.concatenate([d_mlp_g0, d_mlp_g1], axis=0),
        "a_ln_v_g": d_ln_g.reshape(N_DEV, GATE_DIM // N_DEV),
        "a_ln_v_b": d_ln_b.reshape(N_DEV, GATE_DIM // N_DEV),
        "a_w_s": d_ws.astype(BF16) if dist else d_ws,
        "a_b_s": d_bs.reshape(A_GROUPS, GMLP_BLOCK),
        "b_q_norm_g": d_q_g,
        "kv_src_norm_g": d_src_g,
        "kv_a_norm_g": d_kv_a_g,
        "final_norm_g": d_final_g,
    }
    if dist:
        parts = [small[k].reshape((1,) + small[k].shape) for k in SMALL] + [loss.reshape(1, 1, 1)]
        comm = _together(_chip_exchange_comm(sums), _gather_comm(parts))
        g["a_w_out"], *got = _by_sequencer("exchange_a_w_out", comm, EVERYONE, next(ids))
        small, loss = dict(zip(SMALL, got)), got[-1]
    g["a_w_in"] = _wgrad("wgrad_a_w_in", hn, dz, _full(t, D_MODEL), _cols(t, 2 * FF_SLOT), D_MODEL, FF_SLOT, slots=2)
    return loss, dx, g, small


def _adamw(w, g, m, v):
    m = ADAM_B1 * m + (1.0 - ADAM_B1) * g
    v = ADAM_B2 * v + (1.0 - ADAM_B2) * (g * g)
    m_hat = m / (1.0 - ADAM_B1 ** ADAM_STEP)
    v_hat = v / (1.0 - ADAM_B2 ** ADAM_STEP)
    return -ADAM_LR * (m_hat / (jnp.sqrt(v_hat) + ADAM_EPS) + ADAM_WD * w), m, v


def _sum_in_device_order(r_ref):
    g = r_ref[0].astype(F32)
    for j in range(1, r_ref.shape[0]):
        g = g + r_ref[j].astype(F32)
    return g


def _adamw_sharded(name, recvs, w, m, v, swapped=False):
    layers, r, c = w.shape[0], *recvs[0][0].shape[1:]
    tr = r if swapped else math.gcd(r, 512)
    flat = [a for per_layer in recvs for a in per_layer]

    def body(*refs):
        r_refs, (w_ref, m_ref, v_ref) = refs[:len(flat)], refs[len(flat):len(flat) + 3]
        g_ref, d_ref, nm_ref, nv_ref = refs[-4:]
        layer = pl.program_id(0)
        g, pos = None, 0
        for li, per_layer in enumerate(recvs):
            total = None
            for ref in r_refs[pos:pos + len(per_layer)]:
                part = _sum_in_device_order(ref)
                total = part if total is None else total + part
            pos += len(per_layer)
            g = total if g is None else jnp.where(layer == li, total, g)
        if swapped:
            g = g.T
        g_ref[...] = g
        d_ref[...], nm_ref[...], nv_ref[...] = _adamw(w_ref[...], g, m_ref[...], v_ref[...])

    blk = pl.BlockSpec((None, tr, c), lambda l, i: (l, i, 0))
    if swapped:
        blk = pl.BlockSpec((None, c, r), lambda l, i: (l, 0, 0))
    return _call(name, body, (layers, r // tr),
                 [pl.BlockSpec((a.shape[0], tr, c), lambda l, i: (0, i, 0)) for a in flat] + [blk] * 3,
                 [blk] * 4, [_sds(w.shape, F32)] * 4, (*flat, w, m, v))


def _adamw_small(recvs, ws, ms, vs, own_row, losses):
    n = len(recvs)

    def body(*refs):
        r_refs, w_refs, m_refs, v_refs = (refs[i * n:(i + 1) * n] for i in range(4))
        outs, scr = refs[4 * n + 1:8 * n + 2], refs[8 * n + 2:]
        outs[-1][...] = _sum_in_device_order(refs[4 * n])
        me = _my_place()[3]
        for a in range(n):
            g = _sum_in_device_order(r_refs[a])
            if own_row[a]:
                scr[0][...] = g
                g = scr[0][pl.ds(me, 1), :]
            g_ref, d_ref, nm_ref, nv_ref = outs[4 * a:4 * a + 4]
            g_ref[...] = g
            d_ref[...], nm_ref[...], nv_ref[...] = _adamw(w_refs[a][...], g, m_refs[a][...], v_refs[a][...])

    out_shape = []
    for w in ws:
        out_shape += [_sds(w.shape, F32)] * 4
    return pl.pallas_call(
        body, name="adamw_small", in_specs=[VMEM] * (4 * n + 1), out_specs=[VMEM] * (4 * n + 1),
        out_shape=out_shape + [_sds((1, 1), F32)], scratch_shapes=[pltpu.VMEM((N_DEV, GATE_DIM // N_DEV), F32)],
    )(*recvs, *ws, *ms, *vs, losses)


BIG = ("a_w_in", "a_w_out", "b_w_q_a", "b_w_q_b", "b_w_o", "kv_w_a", "kv_w_b", "mlp_w1", "mlp_w2")
SMALL = ("norm_mix_g", "norm_mlp_g", "a_ln_v_g", "a_ln_v_b", "a_w_s", "a_b_s", "b_q_norm_g", "kv_src_norm_g",
         "kv_a_norm_g", "final_norm_g")
WEIGHTS = ("norm_mix_g", "norm_mlp_g", "a_w_in", "a_ln_v_g", "a_ln_v_b", "a_w_s", "a_b_s", "a_w_out", "b_w_q_a",
           "b_q_norm_g", "b_w_q_b", "b_w_o", "kv_src_norm_g", "kv_w_a", "kv_a_norm_g", "kv_w_b", "mlp_w1", "mlp_w2",
           "final_norm_g")


def _two_d(name, a):
    if name in ("a_w_s", "a_b_s"):
        return a.reshape(a.shape[1:])
    return a.reshape(1, -1) if a.ndim == 1 else a


def _three_d(a):
    return a if a.ndim == 3 else a.reshape((1,) + a.shape)


SWAPPED = ("b_w_q_b", "kv_w_a")


def _swapped(a):
    return jnp.swapaxes(_three_d(a), 1, 2)


def kernel(x, positions, norm_mix_g, norm_mlp_g, a_w_in, a_ln_v_g, a_ln_v_b, a_w_s, a_b_s, a_w_out, b_w_q_a, b_q_norm_g, b_w_q_b, b_w_o, kv_src_norm_g, kv_w_a, kv_a_norm_g, kv_w_b, mlp_w1, mlp_w2, final_norm_g, loss_target, m_norm_mix_g, m_norm_mlp_g, m_a_w_in, m_a_ln_v_g, m_a_ln_v_b, m_a_w_s, m_a_b_s, m_a_w_out, m_b_w_q_a, m_b_q_norm_g, m_b_w_q_b, m_b_w_o, m_kv_src_norm_g, m_kv_w_a, m_kv_a_norm_g, m_kv_w_b, m_mlp_w1, m_mlp_w2, m_final_norm_g, v_norm_mix_g, v_norm_mlp_g, v_a_w_in, v_a_ln_v_g, v_a_ln_v_b, v_a_w_s, v_a_b_s, v_a_w_out, v_b_w_q_a, v_b_q_norm_g, v_b_w_q_b, v_b_w_o, v_kv_src_norm_g, v_kv_w_a, v_kv_a_norm_g, v_kv_w_b, v_mlp_w1, v_mlp_w2, v_final_norm_g):
    w = dict(norm_mix_g=norm_mix_g, norm_mlp_g=norm_mlp_g, a_w_in=a_w_in, a_ln_v_g=a_ln_v_g, a_ln_v_b=a_ln_v_b,
             a_w_s=a_w_s, a_b_s=a_b_s, a_w_out=a_w_out, b_w_q_a=b_w_q_a, b_q_norm_g=b_q_norm_g, b_w_q_b=b_w_q_b,
             b_w_o=b_w_o, kv_src_norm_g=kv_src_norm_g, kv_w_a=kv_w_a, kv_a_norm_g=kv_a_norm_g, kv_w_b=kv_w_b,
             mlp_w1=mlp_w1, mlp_w2=mlp_w2, final_norm_g=final_norm_g)
    m = dict(norm_mix_g=m_norm_mix_g, norm_mlp_g=m_norm_mlp_g, a_w_in=m_a_w_in, a_ln_v_g=m_a_ln_v_g,
             a_ln_v_b=m_a_ln_v_b, a_w_s=m_a_w_s, a_b_s=m_a_b_s, a_w_out=m_a_w_out, b_w_q_a=m_b_w_q_a,
             b_q_norm_g=m_b_q_norm_g, b_w_q_b=m_b_w_q_b, b_w_o=m_b_w_o, kv_src_norm_g=m_kv_src_norm_g,
             kv_w_a=m_kv_w_a, kv_a_norm_g=m_kv_a_norm_g, kv_w_b=m_kv_w_b, mlp_w1=m_mlp_w1, mlp_w2=m_mlp_w2,
             final_norm_g=m_final_norm_g)
    v = dict(norm_mix_g=v_norm_mix_g, norm_mlp_g=v_norm_mlp_g, a_w_in=v_a_w_in, a_ln_v_g=v_a_ln_v_g,
             a_ln_v_b=v_a_ln_v_b, a_w_s=v_a_w_s, a_b_s=v_a_b_s, a_w_out=v_a_w_out, b_w_q_a=v_b_w_q_a,
             b_q_norm_g=v_b_q_norm_g, b_w_q_b=v_b_w_q_b, b_w_o=v_b_w_o, kv_src_norm_g=v_kv_src_norm_g,
             kv_w_a=v_kv_w_a, kv_a_norm_g=v_kv_a_norm_g, kv_w_b=v_kv_w_b, mlp_w1=v_mlp_w1, mlp_w2=v_mlp_w2,
             final_norm_g=v_final_norm_g)
    t = x.shape[1]

    first = ("a_w_in", "a_w_out", "a_ln_v_g", "a_ln_v_b")
    later = ("mlp_w1", "mlp_w2", "kv_w_a", "kv_w_b", "b_w_q_a", "b_w_q_b", "b_w_o")
    later_blocks = ("mlp_w1_0", "mlp_w1_1", "mlp_w2_0", "mlp_w2_1") + later[2:]
    got, casts = _gather_first([_three_d(w[k]) if k in BIG else w[k] for k in first],
                               [_swapped(w[k]) if k in SWAPPED else _three_d(w[k]) for k in later],
                               [k in SWAPPED for k in later])
    wg = dict(zip(first, got))
    wg["a_w_out"] = wg["a_w_out"].reshape(GATE_DIM, D_MODEL)
    wg["a_ln_v_g"] = wg["a_ln_v_g"].reshape(1, GATE_DIM)
    wg["a_ln_v_b"] = wg["a_ln_v_b"].reshape(1, GATE_DIM)
    shards = dict(zip(later_blocks, casts))

    sm = {k: _two_d(k, w[k]) for k in SMALL if k not in ("a_ln_v_g", "a_ln_v_b")}
    sm["a_b_st"] = sm["a_b_s"].T
    inv_freq = (ROPE_THETA ** (-jnp.arange(0, QK_ROPE, 2, dtype=F32) / QK_ROPE)).reshape(1, QK_ROPE // 2)

    losses, dx, g, small = _local_step(x[0], positions.reshape(t, 1), loss_target[0], inv_freq, wg, sm, shards)

    sums = _pair_reduce("pair_reduce_a_w_in", [g["a_w_in"]], after=[g["mlp_w1_0"], g["mlp_w2_0"]])
    g["a_w_in"], = _by_sequencer("exchange_last", _chip_exchange_comm(sums), OTHER_CHIPS, collective_id=1)

    out = {}
    for k in BIG:
        recvs = [[g[k + "_0"]], [g[k + "_1"]]] if k.startswith("mlp") else [[g[k]]]
        view = _swapped if k in SWAPPED else _three_d
        res = _adamw_sharded("adamw_" + k, recvs, view(w[k]), view(m[k]), view(v[k]), swapped=k in SWAPPED)
        out[k] = [view(o).reshape(w[k].shape) for o in res]
    own_row = [k in ("a_ln_v_g", "a_ln_v_b") for k in SMALL]
    res = _adamw_small([small[k] for k in SMALL], [_two_d(k, w[k]) for k in SMALL], [_two_d(k, m[k]) for k in SMALL],
                       [_two_d(k, v[k]) for k in SMALL], own_row, losses)
    for i, k in enumerate(SMALL):
        out[k] = [o.reshape(w[k].shape) for o in res[4 * i:4 * i + 4]]

    return (res[-1].reshape(()), dx.reshape(x.shape), *[out[k][0] for k in WEIGHTS], *[out[k][1] for k in WEIGHTS],
            *[out[k][2] for k in WEIGHTS], *[out[k][3] for k in WEIGHTS])
```

```python
import math

import jax
import jax.numpy as jnp
from jax import lax
from jax.experimental import pallas as pl
from jax.experimental.pallas import tpu as pltpu
from jax.experimental.pallas import tpu_sc as plsc

F32, BF16 = jnp.float32, jnp.bfloat16
MESH = pl.DeviceIdType.MESH
ANY = pl.BlockSpec(memory_space=pl.ANY)
VMEM = pl.BlockSpec(memory_space=pltpu.VMEM)

N_DEV = 8
D_MODEL = 1024
CHUNK = 64
GMLP_BLOCK = 128
GATE_DIM = 2048
A_GROUPS = 8
A_GROUP_DIM = GATE_DIM // A_GROUPS
B_HEADS = 8
QK_NOPE, QK_ROPE, V_HEAD = 128, 64, 128
Q_LORA, KV_LORA = 384, 256
ROPE_THETA = 10000.0
D_FF = 4096
FF_SLOT = D_FF // N_DEV
EPS = 1e-6
ATT_SCALE = (QK_NOPE + QK_ROPE) ** -0.5

ADAM_LR, ADAM_B1, ADAM_B2, ADAM_EPS, ADAM_WD, ADAM_STEP = 0.001, 0.9, 0.999, 1e-08, 0.01, 10

TM = 256
TM_GATE = 256
TM_MLP_FWD = 512
TM_KVQ = 512
VMEM_LIMIT = 56 * 1024 * 1024
INV_SQRT2 = 1.0 / math.sqrt(2.0)
INV_SQRT_2PI = 1.0 / math.sqrt(2.0 * math.pi)
LOG2_E = 1.0 / math.log(2.0)
HEADS_PER_STEP = 2


def _dot(a, b):
    return jnp.dot(a, b, preferred_element_type=F32)


def _dot_nt(a, b):
    return lax.dot_general(a, b, (((1,), (1,)), ((), ())), preferred_element_type=F32)


def _dot_tn(a, b):
    return lax.dot_general(a, b, (((0,), (0,)), ((), ())), preferred_element_type=F32)


def _rms_fwd(x, g):
    rstd = lax.rsqrt(jnp.mean(x * x, axis=-1, keepdims=True) + EPS)
    xhat = x * rstd
    return xhat * g, xhat, rstd


def _rms_bwd(dy, xhat, rstd, g):
    dxhat = dy * g
    dx = rstd * (dxhat - xhat * jnp.mean(dxhat * xhat, axis=-1, keepdims=True))
    return dx, jnp.sum(dy * xhat, axis=0, keepdims=True)


def _ln_fwd(v, g, b):
    mu = jnp.mean(v, axis=-1, keepdims=True)
    vc = v - mu
    rstd = lax.rsqrt(jnp.mean(vc * vc, axis=-1, keepdims=True) + EPS)
    vhat = vc * rstd
    return vhat * g + b, vhat, rstd


def _gelu(x):
    return 0.5 * x * (1.0 + lax.erf(x * INV_SQRT2))


def _gelu_and_grad(x):
    cdf = 0.5 * (1.0 + lax.erf(x * INV_SQRT2))
    return x * cdf, cdf + x * jnp.exp(-0.5 * x * x) * INV_SQRT_2PI


def _rope(x, cos, sin):
    x1, x2 = x[:, :QK_ROPE // 2], x[:, QK_ROPE // 2:]
    return jnp.concatenate([x1 * cos - x2 * sin, x2 * cos + x1 * sin], axis=-1)


def _gate_mask():
    row = lax.broadcasted_iota(jnp.int32, (GMLP_BLOCK, GMLP_BLOCK), 0)
    col = lax.broadcasted_iota(jnp.int32, (GMLP_BLOCK, GMLP_BLOCK), 1)
    return (col < CHUNK) | (row >= CHUNK)


def _att_mask(q0, tq, t):
    q = q0 + lax.broadcasted_iota(jnp.int32, (tq, t), 0)
    k = lax.broadcasted_iota(jnp.int32, (tq, t), 1)
    return jnp.right_shift(k, 6) <= jnp.right_shift(q, 6)


def _res(shape, imap=None):
    zeros = (0,) * len(shape)
    return pl.BlockSpec(shape, imap or (lambda i: zeros), pipeline_mode=pl.Buffered(1))


def _const(shape):
    zeros = (0,) * len(shape)
    return pl.BlockSpec(shape, lambda i: zeros)


def _row(d, tm=TM):
    return pl.BlockSpec((tm, d), lambda i: (i, 0))


def _heads(d, tm=TM):
    return pl.BlockSpec((B_HEADS, tm, d), lambda i: (0, i, 0))


def _sds(shape, dt):
    return jax.ShapeDtypeStruct(shape, dt)


def _acc(ref, val):
    @pl.when(pl.program_id(0) == 0)
    def _():
        ref[...] = jnp.zeros_like(ref)
    ref[...] += val


def _my_place():
    x, y, c = lax.axis_index("x"), lax.axis_index("y"), lax.axis_index("c")
    return x, y, c, 4 * x + 2 * y + c


def _peer(x, y, c, k):
    px = 1 - x if k & 4 else x
    py = 1 - y if k & 2 else y
    pc = 1 - c if k & 1 else c
    return (px, py, pc), 4 * px + 2 * py + pc


CHIPS = (2, 4, 6)


def _splits(ref):
    return len(ref.shape) >= 3 and ref.shape[1] % 32 == 0


def _piece(ref, block, half=None):
    if half is None or not _splits(ref):
        return ref.at[pl.ds(block, 1)]
    rows = ref.shape[1] // 2
    return ref.at[pl.ds(block, 1), pl.ds(half * rows, rows)]


def _gather_copy(sems, a, k, piece, to, src=None):
    return pltpu.make_async_remote_copy(
        src_ref=piece if src is None else src, dst_ref=piece, send_sem=sems[0].at[a, k], recv_sem=sems[1].at[a, k],
        device_id=to, device_id_type=MESH)


def _gather_start(srcs, outs, sems, only=None):
    x, y, c, me = _my_place()
    for a in range(len(srcs)) if only is None else (only,):
        mine = _piece(outs[a], me)
        pltpu.make_async_copy(srcs[a], mine, sems[2].at[a]).start()
        for k, rel in enumerate((1, 4, 2)):
            _gather_copy(sems, a, k, mine, _peer(x, y, c, rel)[0], src=srcs[a]).start()


def _gather_relay(srcs, outs, sems):
    x, y, c, _ = _my_place()
    sib = _peer(x, y, c, 1)[0]
    (xn, xn_i), (yn, yn_i) = _peer(x, y, c, 4), _peer(x, y, c, 2)
    for a in range(len(srcs)):
        out = outs[a]
        _gather_copy(sems, a, 1, _piece(out, xn_i), xn).wait_recv()
        _gather_copy(sems, a, 3, _piece(out, xn_i, 0), yn).start()
        _gather_copy(sems, a, 5, _piece(out, xn_i), sib).start()
        _gather_copy(sems, a, 2, _piece(out, yn_i), yn).wait_recv()
        if _splits(out):
            _gather_copy(sems, a, 4, _piece(out, yn_i, 1), xn).start()
        _gather_copy(sems, a, 6, _piece(out, yn_i), sib).start()


def _gather_finish(srcs, outs, sems):
    x, y, c, me = _my_place()
    sib = _peer(x, y, c, 1)[0]
    xn, yn, dg_i = _peer(x, y, c, 4)[0], _peer(x, y, c, 2)[0], _peer(x, y, c, 6)[1]
    n = len(srcs)
    for a in range(n):
        out = outs[a]
        _gather_copy(sems, a, 3, _piece(out, dg_i, 0), yn).wait_recv()
        _gather_copy(sems, a, 7, _piece(out, dg_i, 0), sib).start()
        if _splits(out):
            _gather_copy(sems, a, 4, _piece(out, dg_i, 1), xn).wait_recv()
            _gather_copy(sems, a, 8, _piece(out, dg_i, 1), sib).start()
    for a in range(n):
        out = outs[a]
        whole, half = _piece(out, me), _piece(out, me, 0)
        for k in (0, 5, 6):
            _gather_copy(sems, a, k, whole, sib).wait_recv()
        for k in (7, 8) if _splits(out) else (7,):
            _gather_copy(sems, a, k, half, sib).wait_recv()
        for k in (0, 1, 2):
            _gather_copy(sems, a, k, whole, sib, src=srcs[a]).wait_send()
        for k in (5, 6):
            _gather_copy(sems, a, k, whole, sib).wait_send()
        for k in (3, 4, 7, 8) if _splits(out) else (3, 7):
            _gather_copy(sems, a, k, half, sib).wait_send()
        pltpu.make_async_copy(srcs[a], whole, sems[2].at[a]).wait()


def _relay_sems(n):
    return [pltpu.SemaphoreType.DMA((n, 9)), pltpu.SemaphoreType.DMA((n, 9)), pltpu.SemaphoreType.DMA((n,))]


def _gather_sems(n):
    return [pltpu.SemaphoreType.DMA((n, 7)), pltpu.SemaphoreType.DMA((n, 7)), pltpu.SemaphoreType.DMA((n,))]


class _Comm:
    def __init__(self, args, out_shape, scratch, start, finish, relay=None):
        self.args, self.out_shape, self.scratch, self.start, self.finish = args, out_shape, scratch, start, finish
        self.relay = relay


def _gather_comm(shards):
    return _Comm(list(shards), [_sds((N_DEV,) + s.shape[1:], s.dtype) for s in shards], _relay_sems(len(shards)),
                 _gather_start, _gather_finish, relay=_gather_relay)


def _together(big, small):
    na, no, ns = len(big.args), len(big.out_shape), len(big.scratch)

    def start(src, dst, sems):
        small.start(src[na:], dst[no:], sems[ns:])

    def relay(src, dst, sems):
        small.relay(src[na:], dst[no:], sems[ns:])
        big.start(src[:na], dst[:no], sems[:ns])

    def finish(src, dst, sems):
        small.finish(src[na:], dst[no:], sems[ns:])
        big.finish(src[:na], dst[:no], sems[:ns])

    assert big.relay is None and small.relay is not None
    return _Comm(big.args + small.args, big.out_shape + small.out_shape, list(big.scratch) + list(small.scratch),
                 start, finish, relay=relay)


def _direct_copies(ins, outs, sems, wait):
    send_sems, recv_sems, local_sems = sems
    x, y, c, me = _my_place()
    for a in range(len(ins)):
        local = pltpu.make_async_copy(ins[a].at[pl.ds(me, 1)], outs[a].at[pl.ds(me, 1)], local_sems.at[a])
        local.wait() if wait else local.start()
        for k in range(1, N_DEV):
            to, to_i = _peer(x, y, c, k)
            cp = pltpu.make_async_remote_copy(
                src_ref=ins[a].at[pl.ds(to_i, 1)], dst_ref=outs[a].at[pl.ds(me, 1)],
                send_sem=send_sems.at[a, k - 1], recv_sem=recv_sems.at[a, k - 1], device_id=to, device_id_type=MESH)
            cp.wait() if wait else cp.start()


def _exchange_comm(grads):
    return _Comm(list(grads), [_sds(g.shape, g.dtype) for g in grads], _gather_sems(len(grads)),
                 lambda i, o, s: _direct_copies(i, o, s, False), lambda i, o, s: _direct_copies(i, o, s, True))


def _chip_copies(ins, outs, sems, wait):
    send_sems, recv_sems, local_sems = sems
    x, y, c, _ = _my_place()
    for a in range(len(ins)):
        local = pltpu.make_async_copy(ins[a].at[pl.ds(2 * x + y, 1)], outs[a].at[pl.ds(len(CHIPS), 1)],
                                      local_sems.at[a])
        local.wait() if wait else local.start()
        for i, k in enumerate(CHIPS):
            to = _peer(x, y, c, k)[0]
            cp = pltpu.make_async_remote_copy(
                src_ref=ins[a].at[pl.ds(2 * to[0] + to[1], 1)], dst_ref=outs[a].at[pl.ds(i, 1)],
                send_sem=send_sems.at[a, i], recv_sem=recv_sems.at[a, i], device_id=to, device_id_type=MESH)
            cp.wait() if wait else cp.start()


def _chip_exchange_comm(sums):
    n = len(sums)
    sems = [pltpu.SemaphoreType.DMA((n, len(CHIPS))), pltpu.SemaphoreType.DMA((n, len(CHIPS))),
            pltpu.SemaphoreType.DMA((n,))]
    return _Comm(list(sums), [_sds(s.shape, s.dtype) for s in sums], sems,
                 lambda i, o, s: _chip_copies(i, o, s, False), lambda i, o, s: _chip_copies(i, o, s, True))


def _pair_reduce(name, grads, after=()):
    n = len(grads)
    n_chips = N_DEV // 2

    def body(*refs):
        g_refs, gh_refs, refs = refs[:n], refs[n:2 * n], refs[2 * n + len(after):]
        p_refs, land = refs[:n], refs[n:2 * n]
        send_sems, recv_sems = refs[2 * n:]
        x, y, c, _ = _my_place()
        sib = _peer(x, y, c, 1)[0]
        q = pl.program_id(0)

        def to_sibling(a, j):
            return pltpu.make_async_remote_copy(
                src_ref=gh_refs[a].at[j, pl.ds(1 - c, 1)], dst_ref=land[a].at[pl.ds(j, 1)],
                send_sem=send_sems.at[a, j], recv_sem=recv_sems.at[a, j], device_id=sib, device_id_type=MESH)

        @pl.when(q == 0)
        def _():
            for j in range(n_chips):
                for a in range(n):
                    to_sibling(a, j).start()

        for a in range(n):
            to_sibling(a, q).wait_recv()
            p_refs[a][...] = (g_refs[a][0, pl.ds(c, 1)].astype(F32) + land[a][pl.ds(q, 1)].astype(F32)).astype(BF16)

        @pl.when(q == n_chips - 1)
        def _():
            for a in range(n):
                for j in range(n_chips):
                    to_sibling(a, j).wait_send()

    views = [g.reshape((n_chips, 2) + g.shape[1:]) for g in grads]
    res = pl.pallas_call(
        body, name=name, grid=(n_chips,),
        in_specs=[pl.BlockSpec((1, 2) + g.shape[1:], lambda q: (q, 0, 0, 0)) for g in grads]
        + [ANY] * (n + len(after)),
        out_specs=[pl.BlockSpec((1,) + g.shape[1:], lambda q: (q, 0, 0)) for g in grads],
        out_shape=[_sds((n_chips,) + g.shape[1:], BF16) for g in grads],
        scratch_shapes=[pltpu.VMEM((n_chips,) + g.shape[1:], BF16) for g in grads]
        + [pltpu.SemaphoreType.DMA((n, n_chips)), pltpu.SemaphoreType.DMA((n, n_chips))],
        compiler_params=pltpu.CompilerParams(dimension_semantics=("arbitrary",), vmem_limit_bytes=VMEM_LIMIT),
    )(*views, *views, *after)
    return list(res)


def _pair_exchange_comm(grads):
    n, n_chips = len(grads), N_DEV // 2

    def copies(ins, outs, sems, wait):
        x, y, c, _ = _my_place()
        for j in range(n_chips):
            for a in range(n):
                cp = pltpu.make_async_remote_copy(
                    src_ref=ins[a].at[j, pl.ds(1 - c, 1)], dst_ref=outs[a].at[pl.ds(j, 1)], send_sem=sems[0].at[a, j],
                    recv_sem=sems[1].at[a, j], device_id=_peer(x, y, c, 1)[0], device_id_type=MESH)
                cp.wait() if wait else cp.start()

    views = [g.reshape((n_chips, 2) + g.shape[1:]) for g in grads]
    sems = [pltpu.SemaphoreType.DMA((n, n_chips)), pltpu.SemaphoreType.DMA((n, n_chips))]
    return _Comm(views, [_sds((n_chips,) + g.shape[1:], g.dtype) for g in grads], sems,
                 lambda i, o, s: copies(i, o, s, False), lambda i, o, s: copies(i, o, s, True))


def _pair_add(name, grads, landed, after=()):
    n, n_chips = len(grads), N_DEV // 2

    def body(core_ref, *refs):
        g_refs, l_refs, p_refs = refs[:n], refs[n:2 * n], refs[2 * n + len(after):]
        for a in range(n):
            p_refs[a][...] = (g_refs[a][...].astype(F32) + l_refs[a][...].astype(F32)).astype(BF16)

    views = [g.reshape((n_chips, 2) + g.shape[1:]) for g in grads]
    blocks = [pl.BlockSpec((1,) + g.shape[1:], lambda q, core: (q, 0, 0)) for g in grads]
    mine = [pl.BlockSpec((1, None) + g.shape[1:], lambda q, core: (q, core[0], 0, 0)) for g in grads]
    return list(pl.pallas_call(
        body, name=name, out_shape=[_sds((n_chips,) + g.shape[1:], BF16) for g in grads],
        grid_spec=pltpu.PrefetchScalarGridSpec(num_scalar_prefetch=1, grid=(n_chips,),
                                               in_specs=mine + blocks + [ANY] * len(after), out_specs=blocks),
        compiler_params=pltpu.CompilerParams(dimension_semantics=("arbitrary",), vmem_limit_bytes=VMEM_LIMIT),
    )(lax.axis_index("c").reshape(1), *views, *landed, *after))


def _call(name, body, grid, in_specs, out_specs, out_shape, args, scratch=(), after=()):
    ni, na = len(in_specs), len(after)

    def ordered(*refs):
        body(*refs[:ni], *refs[ni + na:])

    return list(pl.pallas_call(
        ordered if after else body, name=name, grid=grid, in_specs=list(in_specs) + [ANY] * na,
        out_specs=list(out_specs), out_shape=list(out_shape), scratch_shapes=list(scratch),
        compiler_params=pltpu.CompilerParams(dimension_semantics=("arbitrary",) * len(grid),
                                             vmem_limit_bytes=VMEM_LIMIT))(*args, *after))


SIBLING_AND_NEIGHBOURS, OTHER_CHIPS, EVERYONE = (1, 4, 2), CHIPS, tuple(range(1, N_DEV))


def _by_sequencer(name, comm, peers, collective_id):
    src = [jax.new_ref(a, memory_space=pltpu.MemorySpace.HBM) for a in comm.args]
    dst = [jax.empty_ref(s, memory_space=pltpu.MemorySpace.HBM) for s in comm.out_shape]

    @pl.kernel(mesh=plsc.ScalarSubcoreMesh(axis_name="sequencer", num_cores=1), name=name,
               scratch_types=tuple(comm.scratch), compiler_params=pltpu.CompilerParams(collective_id=collective_id))
    def launch(*sems):
        x, y, c, _ = _my_place()
        barrier = pltpu.get_barrier_semaphore()
        for k in peers:
            pl.semaphore_signal(barrier, inc=1, device_id=_peer(x, y, c, k)[0], device_id_type=MESH)
        pl.semaphore_wait(barrier, len(peers))
        comm.start(src, dst, sems)
        if comm.relay is not None:
            comm.relay(src, dst, sems)
        comm.finish(src, dst, sems)

    launch()
    return [d[...] for d in dst]


def _gather_first(first, later, swapped):
    nf = len(first)
    layer_of = [(a, l) for a, s in enumerate(later) for l in range(s.shape[0])]
    nl = len(layer_of)
    dts = [BF16] * (nf - 2) + [F32, F32]
    shard = [s.shape[:0:-1] if sw else s.shape[1:] for s, sw in zip(later, swapped)]

    def body(*refs):
        ins, refs = refs[:nf + len(later)], refs[nf + len(later):]
        outs, refs = refs[:nf], refs[nf:]
        casts, refs = refs[:nl], refs[nl:]
        stage, sems = refs[:nf], refs[nf:]
        for a in range(nf):
            stage[a][...] = ins[a][...].astype(dts[a])
            _gather_start(stage, outs, sems, only=a)
        for k, (a, l) in enumerate(layer_of):
            block = ins[nf + a][l]
            casts[k][0] = (block.T if swapped[a] else block).astype(BF16)
        _gather_relay(stage, outs, sems)
        _gather_finish(stage, outs, sems)

    res = pl.pallas_call(
        body, name="gather_first",
        in_specs=[VMEM] * (nf + len(later)), out_specs=[ANY] * nf + [VMEM] * nl,
        out_shape=[_sds((N_DEV,) + s.shape[1:], dt) for s, dt in zip(first, dts)]
        + [_sds((1,) + shard[a], BF16) for a, _ in layer_of],
        scratch_shapes=[pltpu.VMEM(s.shape, dt) for s, dt in zip(first, dts)] + _relay_sems(nf),
        compiler_params=pltpu.CompilerParams(vmem_limit_bytes=VMEM_LIMIT),
    )(*first, *later)
    return list(res[:nf]), list(res[nf:])


def _a_mix_fwd(x, g, w_in, ln_g, ln_b, w_s, b_st, w_out):
    t = x.shape[0]
    nblk = TM // GMLP_BLOCK

    def body(x_ref, g_ref, win_ref, lng_ref, lnb_ref, ws_ref, bst_ref, wout_ref, h_ref, z_ref, gated_scr):
        xv = x_ref[...]
        hb = _rms_fwd(xv, g_ref[...])[0].astype(BF16)
        for d in range(N_DEV):
            z_ref[:, d * FF_SLOT:(d + 1) * FF_SLOT] = _dot(hb, win_ref[d])
        u = _gelu(z_ref[:, :GATE_DIM])
        vb = _ln_fwd(_gelu(z_ref[:, GATE_DIM:]), lng_ref[...], lnb_ref[...])[0].astype(BF16)
        mask = _gate_mask()
        for gi in range(A_GROUPS):
            wm = jnp.where(mask, ws_ref[gi], 0.0).astype(BF16)
            bias = bst_ref[:, gi:gi + 1]
            cs = slice(gi * A_GROUP_DIM, (gi + 1) * A_GROUP_DIM)
            for n in range(nblk):
                rs = slice(n * GMLP_BLOCK, (n + 1) * GMLP_BLOCK)
                sv = _dot(wm, vb[rs, cs]) + bias
                gated_scr[rs, cs] = (u[rs, cs] * sv).astype(BF16)
        h_ref[...] = xv + _dot(gated_scr[...], wout_ref[...])

    return _call(
        "a_mix_fwd", body, (t // TM,),
        [_row(D_MODEL), _res((1, D_MODEL)), _res((N_DEV, D_MODEL, FF_SLOT)), _res((1, GATE_DIM)),
         _res((1, GATE_DIM)), _res((A_GROUPS, GMLP_BLOCK, GMLP_BLOCK)), _res((GMLP_BLOCK, A_GROUPS)),
         _res((GATE_DIM, D_MODEL))],
        [_row(D_MODEL), _row(2 * GATE_DIM), _row(GATE_DIM)],
        [_sds((t, D_MODEL), F32), _sds((t, 2 * GATE_DIM), F32), _sds((t, GATE_DIM), BF16)],
        (x, g, w_in, ln_g, ln_b, w_s, b_st, w_out))


MLP_W_SPECS = (ANY, ANY)
MLP_W_SCRATCH = (pltpu.VMEM((N_DEV, D_MODEL, FF_SLOT), BF16), pltpu.VMEM((N_DEV, FF_SLOT, D_MODEL), BF16),
                 pltpu.SemaphoreType.DMA((2, N_DEV)))


def _with_streamed_weights(hbm, vmem, sems, order, compute):
    copies = {(j, d): pltpu.make_async_copy(hbm[j].at[d], vmem[j].at[d], sems.at[j, d])
              for d in range(N_DEV) for j in order}
    first = pl.program_id(0) == 0

    def assign(ref, val):
        ref[...] = val

    def add(ref, val):
        ref[...] += val

    @pl.when(first)
    def _():
        for c in copies.values():
            c.start()
        compute(lambda j, d: copies[j, d].wait(), assign)

    @pl.when(jnp.logical_not(first))
    def _():
        compute(lambda j, d: None, add)


def _mlp_fwd(h, g, w1, w2):
    t = h.shape[0]

    def body(h_ref, g_ref, w1_hbm, w2_hbm, o_ref, a_ref, w1_ref, w2_ref, sems):
        def compute(arrived, acc):
            hv = h_ref[...]
            hb = _rms_fwd(hv, g_ref[...])[0].astype(BF16)
            o_ref[...] = hv
            for d in range(N_DEV):
                arrived(0, d)
                a = _dot(hb, w1_ref[d])
                a_ref[:, d * FF_SLOT:(d + 1) * FF_SLOT] = a
                r = jnp.maximum(a, 0.0)
                arrived(1, d)
                o_ref[...] += _dot((r * r).astype(BF16), w2_ref[d])

        _with_streamed_weights((w1_hbm, w2_hbm), (w1_ref, w2_ref), sems, (0, 1), compute)

    return _call(
        "mlp_fwd", body, (t // TM_MLP_FWD,), [_row(D_MODEL, TM_MLP_FWD), _res((1, D_MODEL)), *MLP_W_SPECS],
        [_row(D_MODEL, TM_MLP_FWD), _row(D_FF, TM_MLP_FWD)], [_sds((t, D_MODEL), F32), _sds((t, D_FF), F32)],
        (h, g, w1, w2), scratch=MLP_W_SCRATCH)


def _mlp_fwd_loss(h, g, w1, w2, final_g, target):
    t = h.shape[0]

    def body(h_ref, g_ref, w1_hbm, w2_hbm, fg_ref, t_ref, a_ref, loss_ref, dh_ref, dg_ref, w1_ref, w2_ref, sems):
        def compute(arrived, acc):
            hv = h_ref[...]
            hb = _rms_fwd(hv, g_ref[...])[0].astype(BF16)
            out = hv
            for d in range(N_DEV):
                arrived(0, d)
                a = _dot(hb, w1_ref[d])
                a_ref[:, d * FF_SLOT:(d + 1) * FF_SLOT] = a
                r = jnp.maximum(a, 0.0)
                arrived(1, d)
                out = out + _dot((r * r).astype(BF16), w2_ref[d])
            y, xhat, rstd = _rms_fwd(out, fg_ref[...])
            err = y - t_ref[...]
            part = 0.5 * jnp.sum(jnp.mean(err * err, axis=-1, keepdims=True), axis=0, keepdims=True)
            dx, dg = _rms_bwd(err * (1.0 / D_MODEL), xhat, rstd, fg_ref[...])
            dh_ref[...] = dx
            acc(dg_ref, dg)
            acc(loss_ref, part)

        _with_streamed_weights((w1_hbm, w2_hbm), (w1_ref, w2_ref), sems, (0, 1), compute)

    return _call(
        "mlp_fwd_loss", body, (t // TM,),
        [_row(D_MODEL), _res((1, D_MODEL)), *MLP_W_SPECS, _res((1, D_MODEL)), _row(D_MODEL)],
        [_row(D_FF), _const((1, 1)), _row(D_MODEL), _const((1, D_MODEL))],
        [_sds((t, D_FF), F32), _sds((1, 1), F32), _sds((t, D_MODEL), F32), _sds((1, D_MODEL), F32)],
        (h, g, w1, w2, final_g, target), scratch=MLP_W_SCRATCH)


KVQ_W_SPECS = (_res((1, D_MODEL)), _res((D_MODEL, KV_LORA + QK_ROPE)), _res((1, KV_LORA)),
               _res((B_HEADS, KV_LORA, QK_NOPE + V_HEAD)), _res((1, D_MODEL)), _res((D_MODEL, Q_LORA)),
               _res((1, Q_LORA)), _res((B_HEADS, Q_LORA, QK_NOPE + QK_ROPE)))


def _kvq_fwd(h, pos, inv_freq, kvq_w):
    t = h.shape[0]
    half = QK_ROPE // 2

    def body(h_ref, pos_ref, invf_ref, srcg_ref, wkva_ref, kvag_ref, wkvb_ref, mixg_ref, wqa_ref, qg_ref, wqb_ref,
             ckv_ref, k_ref, v_ref, cqpre_ref, q_ref, cos_ref, sin_ref):
        hv = h_ref[...]
        xhat = hv * lax.rsqrt(jnp.mean(hv * hv, axis=-1, keepdims=True) + EPS)
        ang = pos_ref[...].astype(F32) * invf_ref[...]
        cos, sin = jnp.cos(ang), jnp.sin(ang)
        cos_ref[...] = cos
        sin_ref[...] = sin
        ckv = _dot((xhat * srcg_ref[...]).astype(BF16), wkva_ref[...])
        ckv_ref[...] = ckv
        cb = _rms_fwd(ckv[:, :KV_LORA], kvag_ref[...])[0].astype(BF16)
        kpe = _rope(ckv[:, KV_LORA:], cos, sin).astype(BF16)
        for hd in range(B_HEADS):
            kv = _dot(cb, wkvb_ref[hd])
            k_ref[hd, :, 0:QK_NOPE] = kv[:, :QK_NOPE].astype(BF16)
            k_ref[hd, :, QK_NOPE:] = kpe
            v_ref[hd] = kv[:, QK_NOPE:].astype(BF16)
        cqpre = _dot((xhat * mixg_ref[...]).astype(BF16), wqa_ref[...])
        cqpre_ref[...] = cqpre
        cqb = _rms_fwd(cqpre, qg_ref[...])[0].astype(BF16)
        for hd in range(B_HEADS):
            q = _dot(cqb, wqb_ref[hd])
            q_ref[hd, :, 0:QK_NOPE] = q[:, :QK_NOPE].astype(BF16)
            q_ref[hd, :, QK_NOPE:] = _rope(q[:, QK_NOPE:], cos, sin).astype(BF16)

    tm = TM_KVQ
    return _call(
        "kvq_fwd", body, (t // tm,), [_row(D_MODEL, tm), _row(1, tm), _res((1, half)), *KVQ_W_SPECS],
        [_row(KV_LORA + QK_ROPE, tm), _heads(QK_NOPE + QK_ROPE, tm), _heads(V_HEAD, tm), _row(Q_LORA, tm),
         _heads(QK_NOPE + QK_ROPE, tm), _row(half, tm), _row(half, tm)],
        [_sds((t, KV_LORA + QK_ROPE), F32), _sds((B_HEADS, t, QK_NOPE + QK_ROPE), BF16),
         _sds((B_HEADS, t, V_HEAD), BF16), _sds((t, Q_LORA), F32), _sds((B_HEADS, t, QK_NOPE + QK_ROPE), BF16),
         _sds((t, half), F32), _sds((t, half), F32)],
        (h, pos, inv_freq, *kvq_w))


def _softmax_rows(q, k_ref, k):
    past, upto = k * TM, (k + 1) * TM
    s = _dot_nt(q, k_ref[0:upto, :])
    own = jnp.where(_att_mask(0, TM, TM), s[:, past:], jnp.finfo(F32).min)
    s = own if k == 0 else jnp.concatenate([s[:, :past], own], axis=1)
    e = jnp.exp2((s - jnp.max(s, axis=-1, keepdims=True)) * (ATT_SCALE * LOG2_E))
    return e * (1.0 / jnp.sum(e, axis=-1, keepdims=True))


def _for_my_tile(i, nq, fn):
    for k in range(nq):
        @pl.when(i == k)
        def _(k=k):
            fn(k)


def _attn_fwd(h, q, k, v, w_o):
    t = h.shape[0]
    nq, hps = t // TM, HEADS_PER_STEP

    def body(h_ref, q_ref, k_ref, v_ref, wo_ref, o_ref, att_ref):
        i, pair = pl.program_id(0), pl.program_id(1)

        @pl.when(pair == 0)
        def _():
            o_ref[...] = h_ref[...]

        def tile(kt):
            proj = None
            for j in range(hps):
                hd = pair * hps + j
                p = _softmax_rows(q_ref[j], k_ref.at[hd], kt)
                ob = _dot(p.astype(BF16), v_ref[hd, 0:(kt + 1) * TM, :]).astype(BF16)
                att_ref[j] = ob
                proj = _dot(ob, wo_ref[hd]) if proj is None else proj + _dot(ob, wo_ref[hd])
            o_ref[...] += proj

        _for_my_tile(i, nq, tile)

    def per_head(d):
        return pl.BlockSpec((hps, TM, d), lambda i, pair: (pair, i, 0))

    def resident(shape):
        zeros = (0,) * len(shape)
        return pl.BlockSpec(shape, lambda i, pair: zeros, pipeline_mode=pl.Buffered(1))

    tile_spec = pl.BlockSpec((TM, D_MODEL), lambda i, pair: (i, 0))
    return _call(
        "attn_fwd", body, (nq, B_HEADS // hps),
        [tile_spec, per_head(QK_NOPE + QK_ROPE), resident((B_HEADS, t, QK_NOPE + QK_ROPE)),
         resident((B_HEADS, t, V_HEAD)), resident((B_HEADS, V_HEAD, D_MODEL))],
        [tile_spec, per_head(V_HEAD)], [_sds((t, D_MODEL), F32), _sds((B_HEADS, t, V_HEAD), BF16)],
        (h, q, k, v, w_o))


def _mlp_bwd(h, a, dho, g, w1, w2, layer, after=()):
    t = h.shape[0]

    def body(h_ref, a_ref, dho_ref, g_ref, w1_hbm, w2_hbm, dhi_ref, dg_ref, hn_ref, f_ref, da_ref, dhib_ref,
             w1_ref, w2_ref, sems):
        def compute(arrived, acc):
            gv = g_ref[...]
            y, xhat, rstd = _rms_fwd(h_ref[...], gv)
            hn_ref[...] = y.astype(BF16)
            dho_v = dho_ref[...]
            dhob = dho_v.astype(BF16)
            dhn = jnp.zeros((TM, D_MODEL), F32)
            for d in range(N_DEV):
                cs = slice(d * FF_SLOT, (d + 1) * FF_SLOT)
                r = jnp.maximum(a_ref[:, cs], 0.0)
                f_ref[:, cs] = (r * r).astype(BF16)
                arrived(1, d)
                da = (_dot_nt(dhob, w2_ref[d]) * (2.0 * r)).astype(BF16)
                da_ref[:, cs] = da
                arrived(0, d)
                dhn = dhn + _dot_nt(da, w1_ref[d])
            dx, dg = _rms_bwd(dhn, xhat, rstd, gv)
            dhi = dho_v + dx
            dhi_ref[...] = dhi
            dhib_ref[...] = dhi.astype(BF16)
            acc(dg_ref, dg)

        _with_streamed_weights((w1_hbm, w2_hbm), (w1_ref, w2_ref), sems, (1, 0), compute)

    return _call(
        f"mlp_bwd_{layer}", body, (t // TM,),
        [_row(D_MODEL), _row(D_FF), _row(D_MODEL), _res((1, D_MODEL)), *MLP_W_SPECS],
        [_row(D_MODEL), _const((1, D_MODEL)), _row(D_MODEL), _row(D_FF), _row(D_FF), _row(D_MODEL)],
        [_sds((t, D_MODEL), F32), _sds((1, D_MODEL), F32), _sds((t, D_MODEL), BF16), _sds((t, D_FF), BF16),
         _sds((t, D_FF), BF16), _sds((t, D_MODEL), BF16)],
        (h, a, dho, g, w1, w2), scratch=MLP_W_SCRATCH, after=after)


def _attn_bwd(dh, q, k, v, w_o, cos, sin, after=()):
    t = dh.shape[0]
    half, hps = QK_ROPE // 2, HEADS_PER_STEP

    def body(dh_ref, q_ref, k_ref, v_ref, wo_ref, cos_ref, sin_ref, dq_ref, dk_ref, dv_ref):
        i = pl.program_id(1)

        @pl.when(i == 0)
        def _():
            dk_ref[...] = jnp.zeros_like(dk_ref)
            dv_ref[...] = jnp.zeros_like(dv_ref)

        def tile(kt):
            keys = slice(0, (kt + 1) * TM)
            for j in range(hps):
                qj = q_ref[j]
                do = _dot_nt(dh_ref[kt * TM:(kt + 1) * TM, :], wo_ref[j]).astype(BF16)
                p = _softmax_rows(qj, k_ref.at[j], kt)
                dp = _dot_nt(do, v_ref[j, keys, :])
                ds = (p * (dp - jnp.sum(p * dp, axis=-1, keepdims=True)) * ATT_SCALE).astype(BF16)
                dq = _dot(ds, k_ref[j, keys, :])
                dq_ref[j, :, 0:QK_NOPE] = dq[:, :QK_NOPE].astype(BF16)
                dq_ref[j, :, QK_NOPE:] = _rope(dq[:, QK_NOPE:], cos_ref[...], -sin_ref[...]).astype(BF16)
                dk_ref[j, keys, :] += _dot_tn(ds, qj)
                dv_ref[j, keys, :] += _dot_tn(p.astype(BF16), do)

        _for_my_tile(i, t // TM, tile)

    def per_pair(rows, d, tiled):
        return pl.BlockSpec((hps, rows, d), (lambda pair, i: (pair, i, 0)) if tiled else (lambda pair, i: (pair, 0, 0)))

    def tile(d):
        return pl.BlockSpec((TM, d), lambda pair, i: (i, 0))

    return _call(
        "attn_bwd", body, (B_HEADS // hps, t // TM),
        [pl.BlockSpec((t, D_MODEL), lambda pair, i: (0, 0), pipeline_mode=pl.Buffered(1)),
         per_pair(TM, QK_NOPE + QK_ROPE, True), per_pair(t, QK_NOPE + QK_ROPE, False), per_pair(t, V_HEAD, False),
         per_pair(V_HEAD, D_MODEL, False), tile(half), tile(half)],
        [per_pair(TM, QK_NOPE + QK_ROPE, True), per_pair(t, QK_NOPE + QK_ROPE, False), per_pair(t, V_HEAD, False)],
        [_sds((B_HEADS, t, QK_NOPE + QK_ROPE), BF16), _sds((B_HEADS, t, QK_NOPE + QK_ROPE), F32),
         _sds((B_HEADS, t, V_HEAD), F32)],
        (dh, q, k, v, w_o, cos, sin), after=after)


def _kvq_bwd(h, dh, ckv, cqpre, dq, dk, dv, cos, sin, kvq_w, after=()):
    t = h.shape[0]
    tm = TM
    half, last = QK_ROPE // 2, t // tm - 1
    grad_shapes = [(D_MODEL, Q_LORA), (B_HEADS, Q_LORA, QK_NOPE + QK_ROPE), (D_MODEL, KV_LORA + QK_ROPE),
                   (B_HEADS, KV_LORA, QK_NOPE + V_HEAD)]

    def body(h_ref, dh_ref, ckv_ref, cqpre_ref, dq_ref, dk_ref, dv_ref, cos_ref, sin_ref,
             srcg_ref, wkva_ref, kvag_ref, wkvb_ref, mixg_ref, wqa_ref, qg_ref, wqb_ref,
             dhi_ref, dmixg_ref, dsrcg_ref, dqg_ref, dkvag_ref, gqa_ref, gqb_ref, gkva_ref, gkvb_ref,
             aqa, aqb, akva, akvb):
        @pl.when(pl.program_id(0) == 0)
        def _():
            for acc in (aqa, aqb, akva, akvb):
                acc[...] = jnp.zeros_like(acc)

        hv = h_ref[...]
        rstd = lax.rsqrt(jnp.mean(hv * hv, axis=-1, keepdims=True) + EPS)
        xhat = hv * rstd
        mixg, srcg, qg, kvag = mixg_ref[...], srcg_ref[...], qg_ref[...], kvag_ref[...]
        cq, cqhat, crstd = _rms_fwd(cqpre_ref[...], qg)
        cqb = cq.astype(BF16)
        dcq = jnp.zeros((tm, Q_LORA), F32)
        for hd in range(B_HEADS):
            dcq = dcq + _dot_nt(dq_ref[hd], wqb_ref[hd])
            aqb[hd] += _dot_tn(cqb, dq_ref[hd])
        dcqpre, dqg = _rms_bwd(dcq, cqhat, crstd, qg)
        dcqpre_b = dcqpre.astype(BF16)
        aqa[...] += _dot_tn((xhat * mixg).astype(BF16), dcqpre_b)
        dxq, dmixg = _rms_bwd(_dot_nt(dcqpre_b, wqa_ref[...]), xhat, rstd, mixg)
        ckv = ckv_ref[...]
        c, chat, krstd = _rms_fwd(ckv[:, :KV_LORA], kvag)
        cb = c.astype(BF16)
        dc = jnp.zeros((tm, KV_LORA), F32)
        dkpe = jnp.zeros((tm, QK_ROPE), F32)
        for hd in range(B_HEADS):
            dkv = jnp.concatenate([dk_ref[hd, :, 0:QK_NOPE], dv_ref[hd]], axis=-1).astype(BF16)
            akvb[hd] += _dot_tn(cb, dkv)
            dc = dc + _dot_nt(dkv, wkvb_ref[hd])
            dkpe = dkpe + dk_ref[hd, :, QK_NOPE:]
        dlat, dkvag = _rms_bwd(dc, chat, krstd, kvag)
        dpe = _rope(dkpe, cos_ref[...], -sin_ref[...])
        dckv_b = jnp.concatenate([dlat, dpe], axis=-1).astype(BF16)
        akva[...] += _dot_tn((xhat * srcg).astype(BF16), dckv_b)
        dxk, dsrcg = _rms_bwd(_dot_nt(dckv_b, wkva_ref[...]), xhat, rstd, srcg)
        dhi_ref[...] = dh_ref[...] + dxq + dxk
        _acc(dmixg_ref, dmixg)
        _acc(dsrcg_ref, dsrcg)
        _acc(dqg_ref, dqg)
        _acc(dkvag_ref, dkvag)

        @pl.when(pl.program_id(0) == last)
        def _():
            for out, acc in ((gqa_ref, aqa), (gqb_ref, aqb), (gkva_ref, akva), (gkvb_ref, akvb)):
                out[...] = acc[...].astype(BF16)

    return _call(
        "kvq_bwd", body, (t // tm,),
        [_row(D_MODEL, tm), _row(D_MODEL, tm), _row(KV_LORA + QK_ROPE, tm), _row(Q_LORA, tm),
         _heads(QK_NOPE + QK_ROPE, tm), _heads(QK_NOPE + QK_ROPE, tm), _heads(V_HEAD, tm), _row(half, tm),
         _row(half, tm), *KVQ_W_SPECS],
        [_row(D_MODEL, tm), _const((1, D_MODEL)), _const((1, D_MODEL)), _const((1, Q_LORA)), _const((1, KV_LORA))]
        + [_const(s) for s in grad_shapes],
        [_sds((t, D_MODEL), F32), _sds((1, D_MODEL), F32), _sds((1, D_MODEL), F32), _sds((1, Q_LORA), F32),
         _sds((1, KV_LORA), F32)] + [_sds(s, BF16) for s in grad_shapes],
        (h, dh, ckv, cqpre, dq, dk, dv, cos, sin, *kvq_w), scratch=[pltpu.VMEM(s, F32) for s in grad_shapes],
        after=after)


def _a_mix_bwd(x, z, dh, g, w_in, ln_g, ln_b, w_s, b_st, w_out, after=()):
    t = x.shape[0]
    tm = TM_GATE
    nblk = tm // GMLP_BLOCK

    def body(x_ref, z_ref, dh_ref, g_ref, win_ref, lng_ref, lnb_ref, ws_ref, bst_ref, wout_ref,
             dx_ref, hn_ref, dz_ref, dg_ref, dlng_ref, dlnb_ref, dws_ref, dbs_ref, dvn_scr, gelu_grad_v):
        @pl.when(pl.program_id(0) == 0)
        def _():
            dws_ref[...] = jnp.zeros_like(dws_ref)
            dbs_ref[...] = jnp.zeros_like(dbs_ref)

        gv, lng = g_ref[...], lng_ref[...]
        y, xhat, rstd = _rms_fwd(x_ref[...], gv)
        hn_ref[...] = y.astype(BF16)
        dhv = dh_ref[...]
        dgated = _dot_nt(dhv.astype(BF16), wout_ref[...])
        u, gelu_grad_u = _gelu_and_grad(z_ref[:, :GATE_DIM])
        v, gelu_grad_v[...] = _gelu_and_grad(z_ref[:, GATE_DIM:])
        vn, vhat, lrstd = _ln_fwd(v, lng, lnb_ref[...])
        vb = vn.astype(BF16)
        mask = _gate_mask()
        for gi in range(A_GROUPS):
            wm = jnp.where(mask, ws_ref[gi], 0.0).astype(BF16)
            bias = bst_ref[:, gi:gi + 1]
            cs = slice(gi * A_GROUP_DIM, (gi + 1) * A_GROUP_DIM)
            dws = jnp.zeros((GMLP_BLOCK, GMLP_BLOCK), F32)
            dbs = jnp.zeros((GMLP_BLOCK, 1), F32)
            for n in range(nblk):
                rs = slice(n * GMLP_BLOCK, (n + 1) * GMLP_BLOCK)
                sv = _dot(wm, vb[rs, cs]) + bias
                dz_ref[rs, cs] = (dgated[rs, cs] * sv * gelu_grad_u[rs, cs]).astype(BF16)
                dsv = dgated[rs, cs] * u[rs, cs]
                dsvb = dsv.astype(BF16)
                dws = dws + _dot_nt(dsvb, vb[rs, cs])
                dbs = dbs + jnp.sum(dsv, axis=-1, keepdims=True)
                dvn_scr[rs, cs] = _dot_tn(wm, dsvb)
            dws_ref[gi] += jnp.where(mask, dws, 0.0)
            dbs_ref[gi] += dbs
        dvn = dvn_scr[...]
        dvhat = dvn * lng
        dv = lrstd * (dvhat - jnp.mean(dvhat, axis=-1, keepdims=True)
                      - vhat * jnp.mean(dvhat * vhat, axis=-1, keepdims=True))
        dz_ref[:, GATE_DIM:] = (dv * gelu_grad_v[...]).astype(BF16)
        dhn = jnp.zeros((tm, D_MODEL), F32)
        for d in range(N_DEV):
            dhn = dhn + _dot_nt(dz_ref[:, d * FF_SLOT:(d + 1) * FF_SLOT], win_ref[d])
        dx, dg = _rms_bwd(dhn, xhat, rstd, gv)
        dx_ref[...] = dhv + dx
        _acc(dg_ref, dg)
        _acc(dlng_ref, jnp.sum(dvn * vhat, axis=0, keepdims=True))
        _acc(dlnb_ref, jnp.sum(dvn, axis=0, keepdims=True))

    return _call(
        "a_mix_bwd", body, (t // tm,),
        [_row(D_MODEL, tm), _row(2 * GATE_DIM, tm), _row(D_MODEL, tm), _res((1, D_MODEL)),
         _res((N_DEV, D_MODEL, FF_SLOT)), _res((1, GATE_DIM)), _res((1, GATE_DIM)),
         _res((A_GROUPS, GMLP_BLOCK, GMLP_BLOCK)), _res((GMLP_BLOCK, A_GROUPS)), _res((GATE_DIM, D_MODEL))],
        [_row(D_MODEL, tm), _row(D_MODEL, tm), _row(2 * GATE_DIM, tm),
         _const((1, D_MODEL)), _const((1, GATE_DIM)), _const((1, GATE_DIM)),
         _const((A_GROUPS, GMLP_BLOCK, GMLP_BLOCK)), _const((A_GROUPS, GMLP_BLOCK, 1))],
        [_sds((t, D_MODEL), F32), _sds((t, D_MODEL), BF16),
         _sds((t, 2 * GATE_DIM), BF16), _sds((1, D_MODEL), F32), _sds((1, GATE_DIM), F32),
         _sds((1, GATE_DIM), F32), _sds((A_GROUPS, GMLP_BLOCK, GMLP_BLOCK), F32),
         _sds((A_GROUPS, GMLP_BLOCK, 1), F32)],
        (x, z, dh, g, w_in, ln_g, ln_b, w_s, b_st, w_out),
        scratch=[pltpu.VMEM((tm, GATE_DIM), F32), pltpu.VMEM((tm, GATE_DIM), F32)], after=after)


def _wgrad(name, a, b, a_spec, b_spec, m, n, after=(), slots=1):
    def body(a_ref, b_ref, o_ref):
        res = _dot_tn(a_ref[...].astype(BF16), b_ref[...].astype(BF16)).astype(BF16)
        for s in range(slots):
            o_ref[s] = res[s * m:(s + 1) * m] if res.shape[0] == slots * m else res[:, s * n:(s + 1) * n]

    return _call(name, body, (N_DEV // slots,), [a_spec, b_spec],
                 [pl.BlockSpec((slots, m, n), lambda d: (d, 0, 0))], [_sds((N_DEV, m, n), BF16)], (a, b),
                 after=after)[0]


def _full(t, d):
    return pl.BlockSpec((t, d), lambda i: (0, 0), pipeline_mode=pl.Buffered(1))


def _cols(t, d):
    return pl.BlockSpec((t, d), lambda i: (0, i))


def _head(t, d):
    return pl.BlockSpec((None, t, d), lambda i: (i, 0, 0))


def _local_step(x, pos, target, inv_freq, wg, sm, shards=None):
    t = x.shape[0]
    wg = dict(wg)
    dist = shards is not None
    mix_g = [sm["norm_mix_g"][l:l + 1] for l in range(2)]
    mlp_g = [sm["norm_mlp_g"][l:l + 1] for l in range(2)]

    ids = iter(range(2, 2 + 9))

    def gather(names):
        if dist:
            got = _by_sequencer("gather_" + names[0], _gather_comm([shards[k] for k in names]),
                                SIBLING_AND_NEIGHBOURS, next(ids))
            wg.update(zip(names, got))

    def send(name, names):
        if dist:
            comm = _exchange_comm(grads=[g[k] for k in names])
            g.update(zip(names, _by_sequencer("exchange_" + name, comm, EVERYONE, next(ids))))

    def send_sums(name, names, meanwhile):
        if not dist:
            meanwhile()
            return ()
        grads = [g[k] for k in names]
        landed = _by_sequencer("pair_exchange_" + name, _pair_exchange_comm(grads), (1,), next(ids))
        sums = _pair_add("pair_add_" + name, grads, landed, after=meanwhile())
        g.update(zip(names, _by_sequencer("exchange_" + name, _chip_exchange_comm(sums), OTHER_CHIPS, next(ids))))
        return sums

    def a_args():
        return (wg["a_w_in"], wg["a_ln_v_g"], wg["a_ln_v_b"], sm["a_w_s"], sm["a_b_st"], wg["a_w_out"])

    def kvq_w():
        return (sm["kv_src_norm_g"], wg["kv_w_a"], sm["kv_a_norm_g"], wg["kv_w_b"], mix_g[1], wg["b_w_q_a"],
                sm["b_q_norm_g"], wg["b_w_q_b"])

    gather(("mlp_w1_0", "mlp_w2_0"))
    h1, z, gated = _a_mix_fwd(x, mix_g[0], *a_args())
    gather(("kv_w_a", "kv_w_b", "b_w_q_a", "b_w_q_b", "b_w_o"))
    h2, a0 = _mlp_fwd(h1, mlp_g[0], wg["mlp_w1_0"], wg["mlp_w2_0"])
    if dist:
        wg["b_w_q_a"] = wg["b_w_q_a"].reshape(D_MODEL, Q_LORA)
        wg["kv_w_a"] = wg["kv_w_a"].reshape(D_MODEL, KV_LORA + QK_ROPE)
    gather(("mlp_w1_1", "mlp_w2_1"))
    ckv, k, v, cqpre, q, cos, sin = _kvq_fwd(h2, pos, inv_freq, kvq_w())
    h3, att = _attn_fwd(h2, q, k, v, wg["b_w_o"])
    a1, loss, dh4, d_final_g = _mlp_fwd_loss(h3, mlp_g[1], wg["mlp_w1_1"], wg["mlp_w2_1"], sm["final_norm_g"], target)

    g = {}
    dh3, d_mlp_g1, hn, f, da, dh3_b = _mlp_bwd(h3, a1, dh4, mlp_g[1], wg["mlp_w1_1"], wg["mlp_w2_1"], 1)
    dq, dk, dv = _attn_bwd(dh3_b, q, k, v, wg["b_w_o"], cos, sin)
    g["mlp_w1_1"] = _wgrad("wgrad_w1_1", hn, da, _full(t, D_MODEL), _cols(t, 2 * FF_SLOT), D_MODEL, FF_SLOT,
                           after=[dq], slots=2)
    g["mlp_w2_1"] = _wgrad("wgrad_w2_1", f, dh4, _cols(t, FF_SLOT), _full(t, D_MODEL), FF_SLOT, D_MODEL)

    def wgrad_w_o():
        g["b_w_o"] = _wgrad("wgrad_w_o", att, dh3_b, _head(t, V_HEAD), _full(t, D_MODEL), V_HEAD, D_MODEL)
        return [g["b_w_o"]]

    sums = send_sums("mlp_1", ("mlp_w1_1", "mlp_w2_1"), wgrad_w_o)
    dh2, d_mix_g1, d_src_g, d_q_g, d_kv_a_g, g_q_a, g["b_w_q_b"], g_kv_a, g["kv_w_b"] = _kvq_bwd(
        h2, dh3, ckv, cqpre, dq, dk, dv, cos, sin, kvq_w(), after=sums)
    g["b_w_q_a"] = g_q_a.reshape(N_DEV, D_MODEL // N_DEV, Q_LORA)
    g["kv_w_a"] = g_kv_a.reshape(N_DEV, D_MODEL // N_DEV, KV_LORA + QK_ROPE)
    qkv = ("b_w_q_a", "b_w_q_b", "kv_w_a", "kv_w_b")
    landed = [g[k] for k in qkv]
    send("qkv", qkv)
    dh1, d_mlp_g0, hn, f, da, dh1_b = _mlp_bwd(h1, a0, dh2, mlp_g[0], wg["mlp_w1_0"], wg["mlp_w2_0"], 0,
                                               after=landed if dist else ())
    landed = [g["mlp_w1_1"], g["mlp_w2_1"]] if dist else ()
    g["mlp_w1_0"] = _wgrad("wgrad_w1_0", hn, da, _full(t, D_MODEL), _cols(t, 2 * FF_SLOT), D_MODEL, FF_SLOT,
                           after=landed, slots=2)
    g["mlp_w2_0"] = _wgrad("wgrad_w2_0", f, dh2, _cols(t, FF_SLOT), _full(t, D_MODEL), FF_SLOT, D_MODEL)

    def wgrad_a_w_out():
        g["a_w_out"] = _wgrad("wgrad_a_w_out", gated, dh1_b, _cols(t, 2 * GATE_DIM // N_DEV), _full(t, D_MODEL),
                              GATE_DIM // N_DEV, D_MODEL, slots=2)
        return [g["a_w_out"]] + [g[k] for k in qkv]

    sums = send_sums("mlp_0", ("mlp_w1_0", "mlp_w2_0", "b_w_o"), wgrad_a_w_out)
    if dist:
        sums = _pair_reduce("pair_reduce_a_w_out", [g["a_w_out"]], after=sums)
    dx, hn, dz, d_mix_g0, d_ln_g, d_ln_b, d_ws, d_bs = _a_mix_bwd(x, z, dh1, mix_g[0], *a_args(), after=sums)
    small = {
        "norm_mix_g": jnp.concatenate([d_mix_g0, d_mix_g1], axis=0),
        "norm_mlp_g": jnp.concatenate([d_mlp_g0, d_mlp_g1], axis=0),
        "a_ln_v_g": d_ln_g.reshape(N_DEV, GATE_DIM // N_DEV),
        "a_ln_v_b": d_ln_b.reshape(N_DEV, GATE_DIM // N_DEV),
        "a_w_s": d_ws.astype(BF16) if dist else d_ws,
        "a_b_s": d_bs.reshape(A_GROUPS, GMLP_BLOCK),
        "b_q_norm_g": d_q_g,
        "kv_src_norm_g": d_src_g,
        "kv_a_norm_g": d_kv_a_g,
        "final_norm_g": d_final_g,
    }
    if dist:
        parts = [small[k].reshape((1,) + small[k].shape) for k in SMALL] + [loss.reshape(1, 1, 1)]
        comm = _together(_chip_exchange_comm(sums), _gather_comm(parts))
        g["a_w_out"], *got = _by_sequencer("exchange_a_w_out", comm, EVERYONE, next(ids))
        small, loss = dict(zip(SMALL, got)), got[-1]
    g["a_w_in"] = _wgrad("wgrad_a_w_in", hn, dz, _full(t, D_MODEL), _cols(t, 2 * FF_SLOT), D_MODEL, FF_SLOT, slots=2)
    return loss, dx, g, small


def _adamw(w, g, m, v):
    m = ADAM_B1 * m + (1.0 - ADAM_B1) * g
    v = ADAM_B2 * v + (1.0 - ADAM_B2) * (g * g)
    m_hat = m / (1.0 - ADAM_B1 ** ADAM_STEP)
    v_hat = v / (1.0 - ADAM_B2 ** ADAM_STEP)
    return -ADAM_LR * (m_hat / (jnp.sqrt(v_hat) + ADAM_EPS) + ADAM_WD * w), m, v


def _sum_in_device_order(r_ref):
    g = r_ref[0].astype(F32)
    for j in range(1, r_ref.shape[0]):
        g = g + r_ref[j].astype(F32)
    return g


def _adamw_sharded(name, recvs, w, m, v, swapped=False):
    layers, r, c = w.shape[0], *recvs[0][0].shape[1:]
    tr = r if swapped else math.gcd(r, 512)
    flat = [a for per_layer in recvs for a in per_layer]

    def body(*refs):
        r_refs, (w_ref, m_ref, v_ref) = refs[:len(flat)], refs[len(flat):len(flat) + 3]
        g_ref, d_ref, nm_ref, nv_ref = refs[-4:]
        layer = pl.program_id(0)
        g, pos = None, 0
        for li, per_layer in enumerate(recvs):
            total = None
            for ref in r_refs[pos:pos + len(per_layer)]:
                part = _sum_in_device_order(ref)
                total = part if total is None else total + part
            pos += len(per_layer)
            g = total if g is None else jnp.where(layer == li, total, g)
        if swapped:
            g = g.T
        g_ref[...] = g
        d_ref[...], nm_ref[...], nv_ref[...] = _adamw(w_ref[...], g, m_ref[...], v_ref[...])

    blk = pl.BlockSpec((None, tr, c), lambda l, i: (l, i, 0))
    if swapped:
        blk = pl.BlockSpec((None, c, r), lambda l, i: (l, 0, 0))
    return _call(name, body, (layers, r // tr),
                 [pl.BlockSpec((a.shape[0], tr, c), lambda l, i: (0, i, 0)) for a in flat] + [blk] * 3,
                 [blk] * 4, [_sds(w.shape, F32)] * 4, (*flat, w, m, v))


def _adamw_small(recvs, ws, ms, vs, own_row, losses):
    n = len(recvs)

    def body(*refs):
        r_refs, w_refs, m_refs, v_refs = (refs[i * n:(i + 1) * n] for i in range(4))
        outs, scr = refs[4 * n + 1:8 * n + 2], refs[8 * n + 2:]
        outs[-1][...] = _sum_in_device_order(refs[4 * n])
        me = _my_place()[3]
        for a in range(n):
            g = _sum_in_device_order(r_refs[a])
            if own_row[a]:
                scr[0][...] = g
                g = scr[0][pl.ds(me, 1), :]
            g_ref, d_ref, nm_ref, nv_ref = outs[4 * a:4 * a + 4]
            g_ref[...] = g
            d_ref[...], nm_ref[...], nv_ref[...] = _adamw(w_refs[a][...], g, m_refs[a][...], v_refs[a][...])

    out_shape = []
    for w in ws:
        out_shape += [_sds(w.shape, F32)] * 4
    return pl.pallas_call(
        body, name="adamw_small", in_specs=[VMEM] * (4 * n + 1), out_specs=[VMEM] * (4 * n + 1),
        out_shape=out_shape + [_sds((1, 1), F32)], scratch_shapes=[pltpu.VMEM((N_DEV, GATE_DIM // N_DEV), F32)],
    )(*recvs, *ws, *ms, *vs, losses)


BIG = ("a_w_in", "a_w_out", "b_w_q_a", "b_w_q_b", "b_w_o", "kv_w_a", "kv_w_b", "mlp_w1", "mlp_w2")
SMALL = ("norm_mix_g", "norm_mlp_g", "a_ln_v_g", "a_ln_v_b", "a_w_s", "a_b_s", "b_q_norm_g", "kv_src_norm_g",
         "kv_a_norm_g", "final_norm_g")
WEIGHTS = ("norm_mix_g", "norm_mlp_g", "a_w_in", "a_ln_v_g", "a_ln_v_b", "a_w_s", "a_b_s", "a_w_out", "b_w_q_a",
           "b_q_norm_g", "b_w_q_b", "b_w_o", "kv_src_norm_g", "kv_w_a", "kv_a_norm_g", "kv_w_b", "mlp_w1", "mlp_w2",
           "final_norm_g")


def _two_d(name, a):
    if name in ("a_w_s", "a_b_s"):
        return a.reshape(a.shape[1:])
    return a.reshape(1, -1) if a.ndim == 1 else a


def _three_d(a):
    return a if a.ndim == 3 else a.reshape((1,) + a.shape)


SWAPPED = ("b_w_q_b", "kv_w_a")


def _swapped(a):
    return jnp.swapaxes(_three_d(a), 1, 2)


def kernel(x, positions, norm_mix_g, norm_mlp_g, a_w_in, a_ln_v_g, a_ln_v_b, a_w_s, a_b_s, a_w_out, b_w_q_a, b_q_norm_g, b_w_q_b, b_w_o, kv_src_norm_g, kv_w_a, kv_a_norm_g, kv_w_b, mlp_w1, mlp_w2, final_norm_g, loss_target, m_norm_mix_g, m_norm_mlp_g, m_a_w_in, m_a_ln_v_g, m_a_ln_v_b, m_a_w_s, m_a_b_s, m_a_w_out, m_b_w_q_a, m_b_q_norm_g, m_b_w_q_b, m_b_w_o, m_kv_src_norm_g, m_kv_w_a, m_kv_a_norm_g, m_kv_w_b, m_mlp_w1, m_mlp_w2, m_final_norm_g, v_norm_mix_g, v_norm_mlp_g, v_a_w_in, v_a_ln_v_g, v_a_ln_v_b, v_a_w_s, v_a_b_s, v_a_w_out, v_b_w_q_a, v_b_q_norm_g, v_b_w_q_b, v_b_w_o, v_kv_src_norm_g, v_kv_w_a, v_kv_a_norm_g, v_kv_w_b, v_mlp_w1, v_mlp_w2, v_final_norm_g):
    w = dict(norm_mix_g=norm_mix_g, norm_mlp_g=norm_mlp_g, a_w_in=a_w_in, a_ln_v_g=a_ln_v_g, a_ln_v_b=a_ln_v_b,
             a_w_s=a_w_s, a_b_s=a_b_s, a_w_out=a_w_out, b_w_q_a=b_w_q_a, b_q_norm_g=b_q_norm_g, b_w_q_b=b_w_q_b,
             b_w_o=b_w_o, kv_src_norm_g=kv_src_norm_g, kv_w_a=kv_w_a, kv_a_norm_g=kv_a_norm_g, kv_w_b=kv_w_b,
             mlp_w1=mlp_w1, mlp_w2=mlp_w2, final_norm_g=final_norm_g)
    m = dict(norm_mix_g=m_norm_mix_g, norm_mlp_g=m_norm_mlp_g, a_w_in=m_a_w_in, a_ln_v_g=m_a_ln_v_g,
             a_ln_v_b=m_a_ln_v_b, a_w_s=m_a_w_s, a_b_s=m_a_b_s, a_w_out=m_a_w_out, b_w_q_a=m_b_w_q_a,
             b_q_norm_g=m_b_q_norm_g, b_w_q_b=m_b_w_q_b, b_w_o=m_b_w_o, kv_src_norm_g=m_kv_src_norm_g,
             kv_w_a=m_kv_w_a, kv_a_norm_g=m_kv_a_norm_g, kv_w_b=m_kv_w_b, mlp_w1=m_mlp_w1, mlp_w2=m_mlp_w2,
             final_norm_g=m_final_norm_g)
    v = dict(norm_mix_g=v_norm_mix_g, norm_mlp_g=v_norm_mlp_g, a_w_in=v_a_w_in, a_ln_v_g=v_a_ln_v_g,
             a_ln_v_b=v_a_ln_v_b, a_w_s=v_a_w_s, a_b_s=v_a_b_s, a_w_out=v_a_w_out, b_w_q_a=v_b_w_q_a,
             b_q_norm_g=v_b_q_norm_g, b_w_q_b=v_b_w_q_b, b_w_o=v_b_w_o, kv_src_norm_g=v_kv_src_norm_g,
             kv_w_a=v_kv_w_a, kv_a_norm_g=v_kv_a_norm_g, kv_w_b=v_kv_w_b, mlp_w1=v_mlp_w1, mlp_w2=v_mlp_w2,
             final_norm_g=v_final_norm_g)
    t = x.shape[1]

    first = ("a_w_in", "a_w_out", "a_ln_v_g", "a_ln_v_b")
    later = ("mlp_w1", "mlp_w2", "kv_w_a", "kv_w_b", "b_w_q_a", "b_w_q_b", "b_w_o")
    later_blocks = ("mlp_w1_0", "mlp_w1_1", "mlp_w2_0", "mlp_w2_1") + later[2:]
    got, casts = _gather_first([_three_d(w[k]) if k in BIG else w[k] for k in first],
                               [_swapped(w[k]) if k in SWAPPED else _three_d(w[k]) for k in later],
                               [k in SWAPPED for k in later])
    wg = dict(zip(first, got))
    wg["a_w_out"] = wg["a_w_out"].reshape(GATE_DIM, D_MODEL)
    wg["a_ln_v_g"] = wg["a_ln_v_g"].reshape(1, GATE_DIM)
    wg["a_ln_v_b"] = wg["a_ln_v_b"].reshape(1, GATE_DIM)
    shards = dict(zip(later_blocks, casts))

    sm = {k: _two_d(k, w[k]) for k in SMALL if k not in ("a_ln_v_g", "a_ln_v_b")}
    sm["a_b_st"] = sm["a_b_s"].T
    inv_freq = (ROPE_THETA ** (-jnp.arange(0, QK_ROPE, 2, dtype=F32) / QK_ROPE)).reshape(1, QK_ROPE // 2)

    losses, dx, g, small = _local_step(x[0], positions.reshape(t, 1), loss_target[0], inv_freq, wg, sm, shards)

    sums = _pair_reduce("pair_reduce_a_w_in", [g["a_w_in"]], after=[g["mlp_w1_0"], g["mlp_w2_0"]])
    g["a_w_in"], = _by_sequencer("exchange_last", _chip_exchange_comm(sums), OTHER_CHIPS, collective_id=1)

    out = {}
    for k in BIG:
        recvs = [[g[k + "_0"]], [g[k + "_1"]]] if k.startswith("mlp") else [[g[k]]]
        view = _swapped if k in SWAPPED else _three_d
        res = _adamw_sharded("adamw_" + k, recvs, view(w[k]), view(m[k]), view(v[k]), swapped=k in SWAPPED)
        out[k] = [view(o).reshape(w[k].shape) for o in res]
    own_row = [k in ("a_ln_v_g", "a_ln_v_b") for k in SMALL]
    res = _adamw_small([small[k] for k in SMALL], [_two_d(k, w[k]) for k in SMALL], [_two_d(k, m[k]) for k in SMALL],
                       [_two_d(k, v[k]) for k in SMALL], own_row, losses)
    for i, k in enumerate(SMALL):
        out[k] = [o.reshape(w[k].shape) for o in res[4 * i:4 * i + 4]]

    return (res[-1].reshape(()), dx.reshape(x.shape), *[out[k][0] for k in WEIGHTS], *[out[k][1] for k in WEIGHTS],
            *[out[k][2] for k in WEIGHTS], *[out[k][3] for k in WEIGHTS])
```
